```python
import jax
import jax.numpy as jnp
from jax import lax
import numpy as np

D_MODEL = 1024
BATCH = 16
SEQ = 2048
DEPTH = 2

N_EVEN = (DEPTH + 1) // 2
N_ODD = DEPTH // 2
EPS = 1e-6
NEG_INF = -1e30

SG_HEADS = 4
SG_HEAD_DIM = D_MODEL // 8
SG_WIDTH = SG_HEADS * SG_HEAD_DIM
SG_CHUNK = 128
SC_HEADS = 4
SC_HEAD_DIM = D_MODEL // 8
SC_WIDTH = SC_HEADS * SC_HEAD_DIM
CONV_WIDTH = 3
EVEN_IN = 2 * SG_WIDTH + 3 * SC_WIDTH
EVEN_MIX = SG_WIDTH + SC_WIDTH

POOL_WINDOWS = (2, 4, 8, 16)
POOL_GROUPS = len(POOL_WINDOWS)
POOL_GROUP_DIM = D_MODEL // 16
POOL_WIDTH = POOL_GROUPS * POOL_GROUP_DIM
MLA_HEADS = 6
Q_LORA = 3 * D_MODEL // 8
KV_LORA = D_MODEL // 4
QK_NOPE = 128
QK_ROPE = 64
QK_DIM = QK_NOPE + QK_ROPE
V_DIM = 128
ROPE_THETA = 10000.0
Q_BLOCK = 128
ODD_IN = POOL_WIDTH + Q_LORA + KV_LORA + QK_ROPE
ODD_MIX = POOL_WIDTH + MLA_HEADS * V_DIM

D_FF = ((8 * D_MODEL + 3 * 256 - 1) // (3 * 256)) * 256

kernel_name = 'hybrid_sgu_conv_pool_mla_trunk'


def rms_norm(x, g):
    xf = x.astype(jnp.float32)
    y = xf * lax.rsqrt(jnp.mean(xf * xf, axis=-1, keepdims=True) + EPS)
    return (y * g.astype(jnp.float32)).astype(x.dtype)


def layer_norm(x, g):
    xf = x.astype(jnp.float32)
    mu = jnp.mean(xf, axis=-1, keepdims=True)
    xc = xf - mu
    y = xc * lax.rsqrt(jnp.mean(xc * xc, axis=-1, keepdims=True) + EPS)
    return (y * g.astype(jnp.float32)).astype(x.dtype)


def spatial_gating(u, v, ln_g, w_s, b_s):
    bsz, s, _ = v.shape
    n_chunks = s // SG_CHUNK
    v = layer_norm(v.reshape(bsz, s, SG_HEADS, SG_HEAD_DIM), ln_g.reshape(SG_HEADS, SG_HEAD_DIM))
    v = v.reshape(bsz, n_chunks, SG_CHUNK, SG_HEADS, SG_HEAD_DIM)
    causal = jnp.tril(jnp.ones((SG_CHUNK, SG_CHUNK), dtype=bool))
    w = jnp.where(causal[None], w_s, 0.0).astype(v.dtype)
    mixed = jnp.einsum('hts,bnshd->bnthd', w, v) + b_s.T.astype(v.dtype)[None, None, :, :, None]
    return u * mixed.reshape(bsz, s, SG_WIDTH)


def short_conv(b_gate, c_gate, h, conv_w):
    z = c_gate * h
    y = lax.conv_general_dilated(
        z, conv_w[:, None, :].astype(z.dtype), window_strides=(1,),
        padding=[(CONV_WIDTH - 1, 0)], dimension_numbers=('NWC', 'WIO', 'NWC'),
        feature_group_count=SC_WIDTH)
    return b_gate * y


def multiscale_pool(z, lin_w, scale):
    bsz, s, _ = z.shape
    zf = z.astype(jnp.float32)
    cs = jnp.pad(jnp.cumsum(zf, axis=1), ((0, 0), (1, 0), (0, 0)))
    t = jnp.arange(1, s + 1, dtype=jnp.float32)[None, :, None]
    groups = []
    for g, w in enumerate(POOL_WINDOWS):
        lo, hi = g * POOL_GROUP_DIM, (g + 1) * POOL_GROUP_DIM
        c = cs[..., lo:hi]
        lower = jnp.pad(c[:, :s + 1 - w], ((0, 0), (w - 1, 0), (0, 0)))
        mean = (c[:, 1:] - lower) / jnp.minimum(t, float(w))
        groups.append(mean - zf[..., lo:hi])
    pooled = jnp.stack(groups, axis=2).astype(z.dtype)
    out = jnp.einsum('bsgi,gio->bsgo', pooled, lin_w)
    return out.reshape(bsz, s, POOL_WIDTH) * scale


def rope_tables(positions):
    inv_freq = ROPE_THETA ** (-jnp.arange(0, QK_ROPE, 2, dtype=jnp.float32) / QK_ROPE)
    ang = positions.astype(jnp.float32)[..., None] * inv_freq
    return jnp.cos(ang), jnp.sin(ang)


def apply_rope(x, cos, sin):
    c = cos[:, :, None, :].astype(x.dtype)
    s = sin[:, :, None, :].astype(x.dtype)
    x1, x2 = jnp.split(x, 2, axis=-1)
    return jnp.concatenate([x1 * c - x2 * s, x2 * c + x1 * s], axis=-1)


def latent_attention(q_lat, kv_lat, k_rope, cos, sin, q_a_g, q_b, kv_a_g, kv_b, q_g, k_g):
    bsz, s, _ = q_lat.shape
    q = (rms_norm(q_lat, q_a_g) @ q_b).reshape(bsz, s, MLA_HEADS, QK_DIM)
    kv = (rms_norm(kv_lat, kv_a_g) @ kv_b).reshape(bsz, s, MLA_HEADS, QK_NOPE + V_DIM)
    k_nope, v = kv[..., :QK_NOPE], kv[..., QK_NOPE:]
    k = jnp.concatenate(
        [k_nope, jnp.broadcast_to(k_rope[:, :, None, :], (bsz, s, MLA_HEADS, QK_ROPE))], axis=-1)
    q = rms_norm(q, q_g)
    k = rms_norm(k, k_g)
    q = jnp.concatenate([q[..., :QK_NOPE], apply_rope(q[..., QK_NOPE:], cos, sin)], axis=-1)
    k = jnp.concatenate([k[..., :QK_NOPE], apply_rope(k[..., QK_NOPE:], cos, sin)], axis=-1)
    scale = QK_DIM ** -0.5
    outs = []
    for i in range(s // Q_BLOCK):
        q0, k_end = i * Q_BLOCK, (i + 1) * Q_BLOCK
        logits = jnp.einsum('bqhd,bkhd->bhqk', q[:, q0:k_end], k[:, :k_end]).astype(jnp.float32) * scale
        mask = (q0 + jnp.arange(Q_BLOCK))[:, None] >= jnp.arange(k_end)[None, :]
        p = jax.nn.softmax(jnp.where(mask, logits, NEG_INF), axis=-1).astype(v.dtype)
        outs.append(jnp.einsum('bhqk,bkhd->bqhd', p, v[:, :k_end]))
    return jnp.concatenate(outs, axis=1).reshape(bsz, s, MLA_HEADS * V_DIM)


def swiglu(h, w_gate, w_up, w_down):
    return (jax.nn.silu(h @ w_gate) * (h @ w_up)) @ w_down


def _fwd_setup_inputs(seed: int = 0) -> dict:
    key = jax.random.key(seed)
    k = jax.random.split(key, 23)
    f32 = jnp.float32

    def nrm(kk, shape, fan_in):
        return jax.random.normal(kk, shape, f32) * (fan_in ** -0.5)

    def gain(kk, shape, noise=0.02):
        return 1.0 + noise * jax.random.normal(kk, shape, f32)

    x = jax.random.normal(k[0], (BATCH, SEQ, D_MODEL), f32)
    positions = (jnp.arange(SEQ, dtype=jnp.int32)[None, :]
                 + jax.random.randint(k[1], (BATCH, 1), 0, SEQ, dtype=jnp.int32))
    return {
        'x': x,
        'positions': positions,
        'mix_norm': gain(k[2], (DEPTH, D_MODEL)),
        'ffn_norm': gain(k[3], (DEPTH, D_MODEL)),
        'even_w_in': nrm(k[4], (N_EVEN, D_MODEL, EVEN_IN), D_MODEL),
        'sg_ln_g': gain(k[5], (N_EVEN, SG_WIDTH)),
        'sg_w_s': nrm(k[6], (N_EVEN, SG_HEADS, SG_CHUNK, SG_CHUNK), SG_CHUNK),
        'sg_b_s': gain(k[7], (N_EVEN, SG_HEADS, SG_CHUNK), 0.1),
        'sc_conv_w': nrm(k[8], (N_EVEN, CONV_WIDTH, SC_WIDTH), CONV_WIDTH),
        'even_w_out': nrm(k[9], (N_EVEN, EVEN_MIX, D_MODEL), EVEN_MIX),
        'odd_w_in': nrm(k[10], (N_ODD, D_MODEL, ODD_IN), D_MODEL),
        'pool_w': nrm(k[11], (N_ODD, POOL_GROUPS, POOL_GROUP_DIM, POOL_GROUP_DIM), POOL_GROUP_DIM),
        'pool_scale': gain(k[12], (N_ODD, POOL_WIDTH), 0.1),
        'q_a_norm': gain(k[13], (N_ODD, Q_LORA)),
        'q_b': nrm(k[14], (N_ODD, Q_LORA, MLA_HEADS * QK_DIM), Q_LORA),
        'kv_a_norm': gain(k[15], (N_ODD, KV_LORA)),
        'kv_b': nrm(k[16], (N_ODD, KV_LORA, MLA_HEADS * (QK_NOPE + V_DIM)), KV_LORA),
        'q_norm': gain(k[17], (N_ODD, QK_DIM)),
        'k_norm': gain(k[18], (N_ODD, QK_DIM)),
        'odd_w_out': nrm(k[19], (N_ODD, ODD_MIX, D_MODEL), ODD_MIX),
        'ffn_w_gate': nrm(k[20], (DEPTH, D_MODEL, D_FF), D_MODEL),
        'ffn_w_up': nrm(k[21], (DEPTH, D_MODEL, D_FF), D_MODEL),
        'ffn_w_down': nrm(k[22], (DEPTH, D_FF, D_MODEL), D_FF),
    }


def _fwd_reference(x, positions, mix_norm, ffn_norm, even_w_in, sg_ln_g, sg_w_s, sg_b_s, sc_conv_w,
              even_w_out, odd_w_in, pool_w, pool_scale, q_a_norm, q_b, kv_a_norm, kv_b, q_norm,
              k_norm, odd_w_out, ffn_w_gate, ffn_w_up, ffn_w_down):
    cos, sin = rope_tables(positions)
    for layer in range(DEPTH):
        i = layer // 2
        h = rms_norm(x, mix_norm[layer])
        if layer % 2 == 0:
            proj = h @ even_w_in[i]
            u, v, b_gate, c_gate, hv = jnp.split(
                proj, [SG_WIDTH, 2 * SG_WIDTH, 2 * SG_WIDTH + SC_WIDTH, 2 * SG_WIDTH + 2 * SC_WIDTH], axis=-1)
            a_out = spatial_gating(jax.nn.gelu(u, approximate=False), jax.nn.gelu(v, approximate=False),
                                   sg_ln_g[i], sg_w_s[i], sg_b_s[i])
            b_out = short_conv(b_gate, c_gate, hv, sc_conv_w[i])
            x = x + jnp.concatenate([a_out, b_out], axis=-1) @ even_w_out[i]
        else:
            proj = h @ odd_w_in[i]
            z_pool, q_lat, kv_lat, k_rope = jnp.split(
                proj, [POOL_WIDTH, POOL_WIDTH + Q_LORA, POOL_WIDTH + Q_LORA + KV_LORA], axis=-1)
            c_out = multiscale_pool(z_pool, pool_w[i], pool_scale[i])
            d_out = latent_attention(q_lat, kv_lat, k_rope, cos, sin, q_a_norm[i], q_b[i],
                                     kv_a_norm[i], kv_b[i], q_norm[i], k_norm[i])
            x = x + jnp.concatenate([c_out, d_out], axis=-1) @ odd_w_out[i]
        h = rms_norm(x, ffn_norm[layer])
        x = x + swiglu(h, ffn_w_gate[layer], ffn_w_up[layer], ffn_w_down[layer])
    return x


import jax as _jax
import jax.numpy as _jnp

TWIN_FORMAT = 'train_step'
FWD_PARAMS = ['x', 'positions', 'mix_norm', 'ffn_norm', 'even_w_in', 'sg_ln_g', 'sg_w_s', 'sg_b_s', 'sc_conv_w', 'even_w_out', 'odd_w_in', 'pool_w', 'pool_scale', 'q_a_norm', 'q_b', 'kv_a_norm', 'kv_b', 'q_norm', 'k_norm', 'odd_w_out', 'ffn_w_gate', 'ffn_w_up', 'ffn_w_down']
TWIN_WEIGHTS = ['mix_norm', 'ffn_norm', 'even_w_in', 'sg_ln_g', 'sg_w_s', 'sg_b_s', 'sc_conv_w', 'even_w_out', 'odd_w_in', 'pool_w', 'pool_scale', 'q_a_norm', 'q_b', 'kv_a_norm', 'kv_b', 'q_norm', 'k_norm', 'odd_w_out', 'ffn_w_gate', 'ffn_w_up', 'ffn_w_down']
TWIN_DIFF_INPUT = 'x'
TWIN_INPUTS = ['x', 'positions', 'mix_norm', 'ffn_norm', 'even_w_in', 'sg_ln_g', 'sg_w_s', 'sg_b_s', 'sc_conv_w', 'even_w_out', 'odd_w_in', 'pool_w', 'pool_scale', 'q_a_norm', 'q_b', 'kv_a_norm', 'kv_b', 'q_norm', 'k_norm', 'odd_w_out', 'ffn_w_gate', 'ffn_w_up', 'ffn_w_down', 'loss_target', 'm_mix_norm', 'm_ffn_norm', 'm_even_w_in', 'm_sg_ln_g', 'm_sg_w_s', 'm_sg_b_s', 'm_sc_conv_w', 'm_even_w_out', 'm_odd_w_in', 'm_pool_w', 'm_pool_scale', 'm_q_a_norm', 'm_q_b', 'm_kv_a_norm', 'm_kv_b', 'm_q_norm', 'm_k_norm', 'm_odd_w_out', 'm_ffn_w_gate', 'm_ffn_w_up', 'm_ffn_w_down', 'v_mix_norm', 'v_ffn_norm', 'v_even_w_in', 'v_sg_ln_g', 'v_sg_w_s', 'v_sg_b_s', 'v_sc_conv_w', 'v_even_w_out', 'v_odd_w_in', 'v_pool_w', 'v_pool_scale', 'v_q_a_norm', 'v_q_b', 'v_kv_a_norm', 'v_kv_b', 'v_q_norm', 'v_k_norm', 'v_odd_w_out', 'v_ffn_w_gate', 'v_ffn_w_up', 'v_ffn_w_down']
TWIN_OUTPUTS = ['loss', 'grad_x', 'grad_mix_norm', 'grad_ffn_norm', 'grad_even_w_in', 'grad_sg_ln_g', 'grad_sg_w_s', 'grad_sg_b_s', 'grad_sc_conv_w', 'grad_even_w_out', 'grad_odd_w_in', 'grad_pool_w', 'grad_pool_scale', 'grad_q_a_norm', 'grad_q_b', 'grad_kv_a_norm', 'grad_kv_b', 'grad_q_norm', 'grad_k_norm', 'grad_odd_w_out', 'grad_ffn_w_gate', 'grad_ffn_w_up', 'grad_ffn_w_down', 'delta_mix_norm', 'delta_ffn_norm', 'delta_even_w_in', 'delta_sg_ln_g', 'delta_sg_w_s', 'delta_sg_b_s', 'delta_sc_conv_w', 'delta_even_w_out', 'delta_odd_w_in', 'delta_pool_w', 'delta_pool_scale', 'delta_q_a_norm', 'delta_q_b', 'delta_kv_a_norm', 'delta_kv_b', 'delta_q_norm', 'delta_k_norm', 'delta_odd_w_out', 'delta_ffn_w_gate', 'delta_ffn_w_up', 'delta_ffn_w_down', 'new_m_mix_norm', 'new_m_ffn_norm', 'new_m_even_w_in', 'new_m_sg_ln_g', 'new_m_sg_w_s', 'new_m_sg_b_s', 'new_m_sc_conv_w', 'new_m_even_w_out', 'new_m_odd_w_in', 'new_m_pool_w', 'new_m_pool_scale', 'new_m_q_a_norm', 'new_m_q_b', 'new_m_kv_a_norm', 'new_m_kv_b', 'new_m_q_norm', 'new_m_k_norm', 'new_m_odd_w_out', 'new_m_ffn_w_gate', 'new_m_ffn_w_up', 'new_m_ffn_w_down', 'new_v_mix_norm', 'new_v_ffn_norm', 'new_v_even_w_in', 'new_v_sg_ln_g', 'new_v_sg_w_s', 'new_v_sg_b_s', 'new_v_sc_conv_w', 'new_v_even_w_out', 'new_v_odd_w_in', 'new_v_pool_w', 'new_v_pool_scale', 'new_v_q_a_norm', 'new_v_q_b', 'new_v_kv_a_norm', 'new_v_kv_b', 'new_v_q_norm', 'new_v_k_norm', 'new_v_odd_w_out', 'new_v_ffn_w_gate', 'new_v_ffn_w_up', 'new_v_ffn_w_down']
TWIN_LEAF_KINDS = {'loss': 'loss', 'grad_x': 'grad_x', 'grad_mix_norm': 'grad_w', 'grad_ffn_norm': 'grad_w', 'grad_even_w_in': 'grad_w', 'grad_sg_ln_g': 'grad_w', 'grad_sg_w_s': 'grad_w', 'grad_sg_b_s': 'grad_w', 'grad_sc_conv_w': 'grad_w', 'grad_even_w_out': 'grad_w', 'grad_odd_w_in': 'grad_w', 'grad_pool_w': 'grad_w', 'grad_pool_scale': 'grad_w', 'grad_q_a_norm': 'grad_w', 'grad_q_b': 'grad_w', 'grad_kv_a_norm': 'grad_w', 'grad_kv_b': 'grad_w', 'grad_q_norm': 'grad_w', 'grad_k_norm': 'grad_w', 'grad_odd_w_out': 'grad_w', 'grad_ffn_w_gate': 'grad_w', 'grad_ffn_w_up': 'grad_w', 'grad_ffn_w_down': 'grad_w', 'delta_mix_norm': 'delta_w', 'delta_ffn_norm': 'delta_w', 'delta_even_w_in': 'delta_w', 'delta_sg_ln_g': 'delta_w', 'delta_sg_w_s': 'delta_w', 'delta_sg_b_s': 'delta_w', 'delta_sc_conv_w': 'delta_w', 'delta_even_w_out': 'delta_w', 'delta_odd_w_in': 'delta_w', 'delta_pool_w': 'delta_w', 'delta_pool_scale': 'delta_w', 'delta_q_a_norm': 'delta_w', 'delta_q_b': 'delta_w', 'delta_kv_a_norm': 'delta_w', 'delta_kv_b': 'delta_w', 'delta_q_norm': 'delta_w', 'delta_k_norm': 'delta_w', 'delta_odd_w_out': 'delta_w', 'delta_ffn_w_gate': 'delta_w', 'delta_ffn_w_up': 'delta_w', 'delta_ffn_w_down': 'delta_w', 'new_m_mix_norm': 'new_m', 'new_m_ffn_norm': 'new_m', 'new_m_even_w_in': 'new_m', 'new_m_sg_ln_g': 'new_m', 'new_m_sg_w_s': 'new_m', 'new_m_sg_b_s': 'new_m', 'new_m_sc_conv_w': 'new_m', 'new_m_even_w_out': 'new_m', 'new_m_odd_w_in': 'new_m', 'new_m_pool_w': 'new_m', 'new_m_pool_scale': 'new_m', 'new_m_q_a_norm': 'new_m', 'new_m_q_b': 'new_m', 'new_m_kv_a_norm': 'new_m', 'new_m_kv_b': 'new_m', 'new_m_q_norm': 'new_m', 'new_m_k_norm': 'new_m', 'new_m_odd_w_out': 'new_m', 'new_m_ffn_w_gate': 'new_m', 'new_m_ffn_w_up': 'new_m', 'new_m_ffn_w_down': 'new_m', 'new_v_mix_norm': 'new_v', 'new_v_ffn_norm': 'new_v', 'new_v_even_w_in': 'new_v', 'new_v_sg_ln_g': 'new_v', 'new_v_sg_w_s': 'new_v', 'new_v_sg_b_s': 'new_v', 'new_v_sc_conv_w': 'new_v', 'new_v_even_w_out': 'new_v', 'new_v_odd_w_in': 'new_v', 'new_v_pool_w': 'new_v', 'new_v_pool_scale': 'new_v', 'new_v_q_a_norm': 'new_v', 'new_v_q_b': 'new_v', 'new_v_kv_a_norm': 'new_v', 'new_v_kv_b': 'new_v', 'new_v_q_norm': 'new_v', 'new_v_k_norm': 'new_v', 'new_v_odd_w_out': 'new_v', 'new_v_ffn_w_gate': 'new_v', 'new_v_ffn_w_up': 'new_v', 'new_v_ffn_w_down': 'new_v'}


def _forward(args):
    return _fwd_reference(*[args[k] for k in FWD_PARAMS])


def _output_shape():
    out = _jax.eval_shape(lambda: _forward(_fwd_setup_inputs(0)))
    return out.shape, out.dtype

N_MICROBATCH = 1
ADAM_LR = 0.001
ADAM_B1 = 0.9
ADAM_B2 = 0.999
ADAM_EPS = 1e-08
ADAM_WD = 0.01
ADAM_STEP = 10
PER_EXAMPLE_BATCH_AXIS = {'x': 0, 'positions': 0, 'loss_target': 0}
SHARED_INPUTS = []
_WEIGHT_DTYPES = {'mix_norm': _jnp.float32, 'ffn_norm': _jnp.float32, 'even_w_in': _jnp.float32, 'sg_ln_g': _jnp.float32, 'sg_w_s': _jnp.float32, 'sg_b_s': _jnp.float32, 'sc_conv_w': _jnp.float32, 'even_w_out': _jnp.float32, 'odd_w_in': _jnp.float32, 'pool_w': _jnp.float32, 'pool_scale': _jnp.float32, 'q_a_norm': _jnp.float32, 'q_b': _jnp.float32, 'kv_a_norm': _jnp.float32, 'kv_b': _jnp.float32, 'q_norm': _jnp.float32, 'k_norm': _jnp.float32, 'odd_w_out': _jnp.float32, 'ffn_w_gate': _jnp.float32, 'ffn_w_up': _jnp.float32, 'ffn_w_down': _jnp.float32}
MOMENT_SCALE = {'mix_norm': 4.079986e+01, 'ffn_norm': 2.469790e+01, 'even_w_in': 8.876125e-01, 'sg_ln_g': 6.599816e+00, 'sg_w_s': 5.980205e-01, 'sg_b_s': 1.400342e+01, 'sc_conv_w': 1.767104e+01, 'even_w_out': 2.250163e+00, 'odd_w_in': 1.423187e+00, 'pool_w': 3.409945e+00, 'pool_scale': 2.568405e+01, 'q_a_norm': 1.269667e-01, 'q_b': 7.251627e-02, 'kv_a_norm': 2.720420e+00, 'kv_b': 7.957144e-01, 'q_norm': 6.508599e-01, 'k_norm': 6.492458e-01, 'odd_w_out': 1.142040e+00, 'ffn_w_gate': 4.039585e-01, 'ffn_w_up': 2.830093e-01, 'ffn_w_down': 4.389344e-01}


def _to_microbatches(a, axis):
    t = _jnp.moveaxis(a, axis, 0)
    t = t.reshape((N_MICROBATCH, t.shape[0] // N_MICROBATCH) + t.shape[1:])
    return _jnp.moveaxis(t, 1, axis + 1)


def setup_inputs(seed: int = 0) -> dict:
    inp = _fwd_setup_inputs(seed)
    key = _jax.random.fold_in(_jax.random.key(seed), 7919)
    shape, _ = _output_shape()
    out = dict(inp)
    out["loss_target"] = _jax.random.normal(_jax.random.fold_in(key, 0), shape, _jnp.float32)
    for i, name in enumerate(TWIN_WEIGHTS):
        w = inp[name].astype(_jnp.float32)
        if MOMENT_SCALE is None:
            s = _jnp.sqrt(_jnp.mean(_jnp.square(w)) + 1e-30)
        else:
            s = MOMENT_SCALE[name]
        km, kv = _jax.random.split(_jax.random.fold_in(key, i + 1))
        out[name] = w
        out["m_" + name] = s * _jax.random.normal(km, w.shape, _jnp.float32)
        out["v_" + name] = (s * s) * _jax.random.uniform(kv, w.shape, _jnp.float32, 0.5, 1.5)
    if N_MICROBATCH > 1:
        for name, axis in PER_EXAMPLE_BATCH_AXIS.items():
            out[name] = _to_microbatches(out[name], axis)
    return {'x': out['x'], 'positions': out['positions'], 'mix_norm': out['mix_norm'], 'ffn_norm': out['ffn_norm'], 'even_w_in': out['even_w_in'], 'sg_ln_g': out['sg_ln_g'], 'sg_w_s': out['sg_w_s'], 'sg_b_s': out['sg_b_s'], 'sc_conv_w': out['sc_conv_w'], 'even_w_out': out['even_w_out'], 'odd_w_in': out['odd_w_in'], 'pool_w': out['pool_w'], 'pool_scale': out['pool_scale'], 'q_a_norm': out['q_a_norm'], 'q_b': out['q_b'], 'kv_a_norm': out['kv_a_norm'], 'kv_b': out['kv_b'], 'q_norm': out['q_norm'], 'k_norm': out['k_norm'], 'odd_w_out': out['odd_w_out'], 'ffn_w_gate': out['ffn_w_gate'], 'ffn_w_up': out['ffn_w_up'], 'ffn_w_down': out['ffn_w_down'], 'loss_target': out['loss_target'], 'm_mix_norm': out['m_mix_norm'], 'm_ffn_norm': out['m_ffn_norm'], 'm_even_w_in': out['m_even_w_in'], 'm_sg_ln_g': out['m_sg_ln_g'], 'm_sg_w_s': out['m_sg_w_s'], 'm_sg_b_s': out['m_sg_b_s'], 'm_sc_conv_w': out['m_sc_conv_w'], 'm_even_w_out': out['m_even_w_out'], 'm_odd_w_in': out['m_odd_w_in'], 'm_pool_w': out['m_pool_w'], 'm_pool_scale': out['m_pool_scale'], 'm_q_a_norm': out['m_q_a_norm'], 'm_q_b': out['m_q_b'], 'm_kv_a_norm': out['m_kv_a_norm'], 'm_kv_b': out['m_kv_b'], 'm_q_norm': out['m_q_norm'], 'm_k_norm': out['m_k_norm'], 'm_odd_w_out': out['m_odd_w_out'], 'm_ffn_w_gate': out['m_ffn_w_gate'], 'm_ffn_w_up': out['m_ffn_w_up'], 'm_ffn_w_down': out['m_ffn_w_down'], 'v_mix_norm': out['v_mix_norm'], 'v_ffn_norm': out['v_ffn_norm'], 'v_even_w_in': out['v_even_w_in'], 'v_sg_ln_g': out['v_sg_ln_g'], 'v_sg_w_s': out['v_sg_w_s'], 'v_sg_b_s': out['v_sg_b_s'], 'v_sc_conv_w': out['v_sc_conv_w'], 'v_even_w_out': out['v_even_w_out'], 'v_odd_w_in': out['v_odd_w_in'], 'v_pool_w': out['v_pool_w'], 'v_pool_scale': out['v_pool_scale'], 'v_q_a_norm': out['v_q_a_norm'], 'v_q_b': out['v_q_b'], 'v_kv_a_norm': out['v_kv_a_norm'], 'v_kv_b': out['v_kv_b'], 'v_q_norm': out['v_q_norm'], 'v_k_norm': out['v_k_norm'], 'v_odd_w_out': out['v_odd_w_out'], 'v_ffn_w_gate': out['v_ffn_w_gate'], 'v_ffn_w_up': out['v_ffn_w_up'], 'v_ffn_w_down': out['v_ffn_w_down']}


def _loss(weights, diff, rest, loss_target):
    with _jax.named_scope("forward"):
        args = {**rest, TWIN_DIFF_INPUT: diff, **{k: w.astype(_WEIGHT_DTYPES[k]) for k, w in weights.items()}}
        y = _forward(args)
    with _jax.named_scope("loss_head"):
        err = _jnp.square(y.astype(_jnp.float32) - loss_target)
        return 0.5 * _jnp.sum(_jnp.mean(err, axis=-1)) if err.ndim else 0.5 * err


def _adamw(w, g, m, v):
    m = ADAM_B1 * m + (1.0 - ADAM_B1) * g
    v = ADAM_B2 * v + (1.0 - ADAM_B2) * _jnp.square(g)
    m_hat = m / (1.0 - ADAM_B1 ** ADAM_STEP)
    v_hat = v / (1.0 - ADAM_B2 ** ADAM_STEP)
    delta = -ADAM_LR * (m_hat / (_jnp.sqrt(v_hat) + ADAM_EPS) + ADAM_WD * w)
    return delta, m, v


def reference(x, positions, mix_norm, ffn_norm, even_w_in, sg_ln_g, sg_w_s, sg_b_s, sc_conv_w, even_w_out, odd_w_in, pool_w, pool_scale, q_a_norm, q_b, kv_a_norm, kv_b, q_norm, k_norm, odd_w_out, ffn_w_gate, ffn_w_up, ffn_w_down, loss_target, m_mix_norm, m_ffn_norm, m_even_w_in, m_sg_ln_g, m_sg_w_s, m_sg_b_s, m_sc_conv_w, m_even_w_out, m_odd_w_in, m_pool_w, m_pool_scale, m_q_a_norm, m_q_b, m_kv_a_norm, m_kv_b, m_q_norm, m_k_norm, m_odd_w_out, m_ffn_w_gate, m_ffn_w_up, m_ffn_w_down, v_mix_norm, v_ffn_norm, v_even_w_in, v_sg_ln_g, v_sg_w_s, v_sg_b_s, v_sc_conv_w, v_even_w_out, v_odd_w_in, v_pool_w, v_pool_scale, v_q_a_norm, v_q_b, v_kv_a_norm, v_kv_b, v_q_norm, v_k_norm, v_odd_w_out, v_ffn_w_gate, v_ffn_w_up, v_ffn_w_down):
    given = dict(x=x, positions=positions, mix_norm=mix_norm, ffn_norm=ffn_norm, even_w_in=even_w_in, sg_ln_g=sg_ln_g, sg_w_s=sg_w_s, sg_b_s=sg_b_s, sc_conv_w=sc_conv_w, even_w_out=even_w_out, odd_w_in=odd_w_in, pool_w=pool_w, pool_scale=pool_scale, q_a_norm=q_a_norm, q_b=q_b, kv_a_norm=kv_a_norm, kv_b=kv_b, q_norm=q_norm, k_norm=k_norm, odd_w_out=odd_w_out, ffn_w_gate=ffn_w_gate, ffn_w_up=ffn_w_up, ffn_w_down=ffn_w_down, loss_target=loss_target, m_mix_norm=m_mix_norm, m_ffn_norm=m_ffn_norm, m_even_w_in=m_even_w_in, m_sg_ln_g=m_sg_ln_g, m_sg_w_s=m_sg_w_s, m_sg_b_s=m_sg_b_s, m_sc_conv_w=m_sc_conv_w, m_even_w_out=m_even_w_out, m_odd_w_in=m_odd_w_in, m_pool_w=m_pool_w, m_pool_scale=m_pool_scale, m_q_a_norm=m_q_a_norm, m_q_b=m_q_b, m_kv_a_norm=m_kv_a_norm, m_kv_b=m_kv_b, m_q_norm=m_q_norm, m_k_norm=m_k_norm, m_odd_w_out=m_odd_w_out, m_ffn_w_gate=m_ffn_w_gate, m_ffn_w_up=m_ffn_w_up, m_ffn_w_down=m_ffn_w_down, v_mix_norm=v_mix_norm, v_ffn_norm=v_ffn_norm, v_even_w_in=v_even_w_in, v_sg_ln_g=v_sg_ln_g, v_sg_w_s=v_sg_w_s, v_sg_b_s=v_sg_b_s, v_sc_conv_w=v_sc_conv_w, v_even_w_out=v_even_w_out, v_odd_w_in=v_odd_w_in, v_pool_w=v_pool_w, v_pool_scale=v_pool_scale, v_q_a_norm=v_q_a_norm, v_q_b=v_q_b, v_kv_a_norm=v_kv_a_norm, v_kv_b=v_kv_b, v_q_norm=v_q_norm, v_k_norm=v_k_norm, v_odd_w_out=v_odd_w_out, v_ffn_w_gate=v_ffn_w_gate, v_ffn_w_up=v_ffn_w_up, v_ffn_w_down=v_ffn_w_down)
    weights = {n: given[n] for n in TWIN_WEIGHTS}
    shared = {n: given[n] for n in SHARED_INPUTS}
    per_example = {n: given[n] for n in ['x', 'positions']}
    grad_fn = _jax.value_and_grad(_loss, argnums=(0, 1))

    def one_microbatch(ex, loss_target):
        ex = dict(ex)
        diff = ex.pop(TWIN_DIFF_INPUT)
        return grad_fn(weights, diff, {**shared, **ex}, loss_target)

    if N_MICROBATCH == 1:
        loss, (grad_w, grad_x) = one_microbatch(per_example, given["loss_target"])
    else:
        def body(carry, xs):
            loss_sum, grad_sum = carry
            l_k, (gw_k, gx_k) = one_microbatch(xs[0], xs[1])
            with _jax.named_scope("update"):
                return (loss_sum + l_k, _jax.tree.map(_jnp.add, grad_sum, gw_k)), gx_k

        init = (_jnp.zeros((), _jnp.float32), _jax.tree.map(_jnp.zeros_like, weights))
        (loss, grad_w), grad_x = _jax.lax.scan(body, init, (per_example, given["loss_target"]))
    with _jax.named_scope("update"):
        delta_w, new_m, new_v = {}, {}, {}
        for n in TWIN_WEIGHTS:
            delta_w[n], new_m[n], new_v[n] = _adamw(weights[n], grad_w[n], given["m_" + n], given["v_" + n])
    return (loss, grad_x, *[grad_w[n] for n in TWIN_WEIGHTS], *[delta_w[n] for n in TWIN_WEIGHTS],
            *[new_m[n] for n in TWIN_WEIGHTS], *[new_v[n] for n in TWIN_WEIGHTS])
```

```python
import functools

import numpy as np
import jax
import jax.numpy as jnp
from jax import lax
from jax.experimental import pallas as pl
from jax.experimental.pallas import tpu as pltpu

F32 = jnp.float32
BF16 = jnp.bfloat16
MESH = pl.DeviceIdType.MESH

D = 1024
EPS = 1e-6
NEG_INF = -1e30
SG_HEADS, SG_HD, SG_W, SG_CHUNK = 4, 128, 512, 128
SC_W = 512
EVEN_IN = 2560
POOL_W = 256
POOL_GD = 64
Q_LORA, KV_LORA, QK_ROPE, QK_NOPE, V_DIM = 384, 256, 64, 128, 128
QK_DIM = QK_NOPE + QK_ROPE
HEADS = 6
HP = 256
ODD_IN = 960
D_FF = 2816
ROPE_THETA = 10000.0
ATT_SCALE = QK_DIM ** -0.5
LR, B1, B2, ADAM_EPS, WD, STEP = 0.001, 0.9, 0.999, 1e-08, 0.01, 10

N_DEV = 8
TB = 512
TB_FFN_BWD = 256
HALO = 16
VMEM_LIMIT = 56 * 1024 * 1024

N_EIN, N_FF, N_SQ = 320, 352, 128
OFF_EIN = 0
OFF_FF = 352
OFF_EOUT, OFF_OIN, OFF_OOUT = 2560, 2688, 2816
OFF_QB, N_QB, N_QB_USED = 2944, 64, 54
OFF_KVB, N_KVB = 3008, 48
R_PACK = 3072

INV_SQRT2 = 0.7071067811865476
INV_SQRT_2PI = 0.3989422804014327


def _dot(a, b, ca, cb):
    return lax.dot_general(a, b, (((ca,), (cb,)), ((), ())), preferred_element_type=F32)


def _cparams(n_axes=1):
    return pltpu.CompilerParams(dimension_semantics=("arbitrary",) * n_axes, vmem_limit_bytes=VMEM_LIMIT)


def _wspec(n, off, arity=1):
    assert off % n == 0
    idx = off // n
    if arity == 1:
        return pl.BlockSpec((N_DEV, n, D), lambda i: (0, idx, 0), pipeline_mode=pl.Buffered(1))
    return pl.BlockSpec((N_DEV, n, D), lambda i, j: (0, idx, 0), pipeline_mode=pl.Buffered(1))


def _const_spec(shape):
    zeros = (0,) * len(shape)
    return pl.BlockSpec(shape, lambda *_: zeros)


def _rms(x, g, n=None):
    n = x.shape[-1] if n is None else n
    r = lax.rsqrt(jnp.sum(x * x, axis=-1, keepdims=True) / n + EPS)
    return x * r * g, r


def _rms_bwd(x, r, g, dy, n=None):
    n = x.shape[-1] if n is None else n
    xh = x * r
    dxh = dy * g
    dx = r * (dxh - xh * (jnp.sum(dxh * xh, axis=-1, keepdims=True) / n))
    dg = jnp.sum(dy * xh, axis=0, keepdims=True)
    return dx, dg


def _gelu(x):
    return 0.5 * x * (1.0 + lax.erf(x * INV_SQRT2))


def _gelu_grad(x):
    return 0.5 * (1.0 + lax.erf(x * INV_SQRT2)) + x * jnp.exp(-0.5 * x * x) * INV_SQRT_2PI


def _shift_down(a, k):
    rows = lax.broadcasted_iota(jnp.int32, a.shape, 0)
    return jnp.where(rows >= k, pltpu.roll(a, k, 0), 0.0)


def _shift_up(a, k):
    n = a.shape[0]
    rows = lax.broadcasted_iota(jnp.int32, a.shape, 0)
    return jnp.where(rows < n - k, pltpu.roll(a, n - k, 0), 0.0)


def _tril_bf16(w):
    r = lax.broadcasted_iota(jnp.int32, w.shape, 0)
    c = lax.broadcasted_iota(jnp.int32, w.shape, 1)
    return jnp.where(r >= c, w, 0.0).astype(BF16)


def _ln_head(vh, g):
    mu = jnp.mean(vh, axis=-1, keepdims=True)
    xc = vh - mu
    rr = lax.rsqrt(jnp.mean(xc * xc, axis=-1, keepdims=True) + EPS)
    xh = xc * rr
    return xh * g, xh, rr


def _conv_fwd(z, tail, cw_ref):
    ext = jnp.concatenate([tail, z], axis=0)
    zs1 = _shift_down(ext, 1)[HALO:]
    zs2 = _shift_down(ext, 2)[HALO:]
    y = cw_ref[2:3, :] * z + cw_ref[1:2, :] * zs1 + cw_ref[0:1, :] * zs2
    return y, zs1, zs2


def _pool_cnt(shape, blk_in_seq):
    rows = lax.broadcasted_iota(jnp.int32, shape, 0)
    grp = lax.broadcasted_iota(jnp.int32, shape, 1) // POOL_GD
    win = jnp.where(grp == 0, 2, jnp.where(grp == 1, 4, jnp.where(grp == 2, 8, 16)))
    tpos = blk_in_seq * shape[0] + rows + 1
    return jnp.minimum(tpos, win).astype(F32), grp


def _pool_select(grp, s2, s4, s8, s16):
    return jnp.where(grp == 0, s2, jnp.where(grp == 1, s4, jnp.where(grp == 2, s8, s16)))


def _pool_fwd(z, tail, blk_in_seq):
    ext = jnp.concatenate([tail, z], axis=0)
    s2 = ext + _shift_down(ext, 1)
    s4 = s2 + _shift_down(s2, 2)
    s8 = s4 + _shift_down(s4, 4)
    s16 = s8 + _shift_down(s8, 8)
    cnt, grp = _pool_cnt(z.shape, blk_in_seq)
    sums = _pool_select(grp, s2[HALO:], s4[HALO:], s8[HALO:], s16[HALO:])
    return sums / cnt - z, cnt, grp


def _pool_bwd(dpooled, dpm, head, grp):
    n = dpm.shape[0]
    ext = jnp.concatenate([dpm, head], axis=0)
    u2 = ext + _shift_up(ext, 1)
    u4 = u2 + _shift_up(u2, 2)
    u8 = u4 + _shift_up(u4, 4)
    u16 = u8 + _shift_up(u8, 8)
    return _pool_select(grp, u2[:n], u4[:n], u8[:n], u16[:n]) - dpooled


def _rope(y1, c, s1, s2):
    return y1 * c + pltpu.roll(y1, 96, 1) * s1 + pltpu.roll(y1, 32, 1) * s2


def _rope_bwd(d1, c, s1, s2):
    return d1 * c + pltpu.roll(d1 * s1, 32, 1) + pltpu.roll(d1 * s2, 96, 1)


def _qk_prep(xh_in, g, c, s1, s2):
    y, r = _rms(xh_in, g, QK_DIM)
    out = jnp.concatenate([y[:, :128], _rope(y[:, 128:], c, s1, s2)], axis=1)
    return out, r


def _qk_prep_bwd(dout, x_in, r, g, c, s1, s2):
    dy = jnp.concatenate([dout[:, :128], _rope_bwd(dout[:, 128:], c, s1, s2)], axis=1)
    return _rms_bwd(x_in, r, g, dy, QK_DIM)


def _place():
    return lax.axis_index("x"), lax.axis_index("y"), lax.axis_index("c")


def _all_gather(arrs, name):
    n = len(arrs)

    def body(*refs):
        ins, outs = refs[:n], refs[n:2 * n]
        send_sems, recv_sems, local_sems = refs[2 * n:]
        x, y, c = _place()
        me, sibling = (x, y, c), (x, y, 1 - c)
        chips = [(1 - x, y), (x, 1 - y), (1 - x, 1 - y)]

        def slot(a, px, py, pc):
            return outs[a].at[4 * px + 2 * py + pc]

        def copy(a, k, block, to, src=None):
            return pltpu.make_async_remote_copy(
                src_ref=slot(a, *block) if src is None else src, dst_ref=slot(a, *block),
                send_sem=send_sems.at[a, k], recv_sem=recv_sems.at[a, k], device_id=to, device_id_type=MESH)

        mine = [pltpu.make_async_copy(ins[a], slot(a, *me), local_sems.at[a]) for a in range(n)]
        for cp in mine:
            cp.start()
        first = []
        for a in range(n):
            first.append(copy(a, 0, me, sibling, src=ins[a]))
            first += [copy(a, 1 + j, me, (*chip, c), src=ins[a]) for j, chip in enumerate(chips)]
        for cp in first:
            cp.start()
        passed = []
        for j, chip in enumerate(chips):
            for a in range(n):
                copy(a, 1 + j, (*chip, c), me).wait_recv()
                fwd = copy(a, 4 + j, (*chip, c), sibling)
                fwd.start()
                passed.append(fwd)
        for a in range(n):
            copy(a, 0, sibling, me).wait_recv()
            for j, chip in enumerate(chips):
                copy(a, 4 + j, (*chip, 1 - c), me).wait_recv()
        for cp in first + passed:
            cp.wait_send()
        for cp in mine:
            cp.wait()

    any_spec = pl.BlockSpec(memory_space=pl.ANY)
    return pl.pallas_call(
        body, name=name,
        out_shape=[jax.ShapeDtypeStruct((N_DEV,) + a.shape, a.dtype) for a in arrs],
        in_specs=[any_spec] * n, out_specs=[any_spec] * n,
        scratch_shapes=[pltpu.SemaphoreType.DMA((n, 7)), pltpu.SemaphoreType.DMA((n, 7)),
                        pltpu.SemaphoreType.DMA((n,))],
    )(*arrs)


def _all_reduce_small(v):
    rows = v.shape[0]

    def body(v_ref, sum_ref, g_ref, send_sems, recv_sems):
        x, y, c = _place()
        me, sibling = (x, y, c), (x, y, 1 - c)
        chips = [(1 - x, y), (x, 1 - y), (1 - x, 1 - y)]

        def slot(px, py, pc):
            return g_ref.at[4 * px + 2 * py + pc]

        def copy(k, block, to, src=None):
            return pltpu.make_async_remote_copy(
                src_ref=slot(*block) if src is None else src, dst_ref=slot(*block),
                send_sem=send_sems.at[k], recv_sem=recv_sems.at[k], device_id=to, device_id_type=MESH)

        g_ref[4 * x + 2 * y + c] = v_ref[...]
        first = [copy(0, me, sibling, src=v_ref)]
        first += [copy(1 + j, me, (*chip, c), src=v_ref) for j, chip in enumerate(chips)]
        for cp in first:
            cp.start()
        passed = [copy(4 + j, (*chip, c), sibling) for j, chip in enumerate(chips)]
        for j, chip in enumerate(chips):
            copy(1 + j, (*chip, c), me).wait_recv()
            passed[j].start()
        copy(0, sibling, me).wait_recv()
        for j, chip in enumerate(chips):
            copy(4 + j, (*chip, 1 - c), me).wait_recv()
        for cp in first + passed:
            cp.wait_send()
        total = g_ref[0]
        for d in range(1, N_DEV):
            total = total + g_ref[d]
        sum_ref[...] = total

    vm = pl.BlockSpec(memory_space=pltpu.VMEM)
    out = pl.pallas_call(
        body, name="allreduce_small",
        out_shape=[jax.ShapeDtypeStruct((rows, 128), F32), jax.ShapeDtypeStruct((N_DEV, rows, 128), F32)],
        in_specs=[vm], out_specs=[vm, vm],
        scratch_shapes=[pltpu.SemaphoreType.DMA((7,)), pltpu.SemaphoreType.DMA((7,))],
        compiler_params=pltpu.CompilerParams(vmem_limit_bytes=VMEM_LIMIT),
    )(v)
    return out[0]


def _rs_pair_exchange(gp):
    _, rows, cols = gp.shape

    def body(gp_ref, out_ref, send_sems, recv_sems):
        x, y, c = _place()
        cps = []
        for j in range(4):
            cp = pltpu.make_async_remote_copy(
                src_ref=gp_ref.at[2 * j + (1 - c)], dst_ref=out_ref.at[j],
                send_sem=send_sems.at[j], recv_sem=recv_sems.at[j], device_id=(x, y, 1 - c), device_id_type=MESH)
            cp.start()
            cps.append(cp)
        for cp in cps:
            cp.wait()

    any_spec = pl.BlockSpec(memory_space=pl.ANY)
    return pl.pallas_call(
        body, name="rs_pair_exchange", out_shape=jax.ShapeDtypeStruct((4, rows, cols), gp.dtype),
        in_specs=[any_spec], out_specs=any_spec,
        scratch_shapes=[pltpu.SemaphoreType.DMA((4,)), pltpu.SemaphoreType.DMA((4,))],
    )(gp)


def _rs_pair_sum(gp, got, c_arr):
    _, rows, cols = gp.shape
    rb = 512
    gp4 = gp.reshape(4, 2, rows, cols)

    def body(c_ref, a_ref, b_ref, o_ref):
        o_ref[0] = (a_ref[0, 0].astype(F32) + b_ref[0].astype(F32)).astype(o_ref.dtype)

    return pl.pallas_call(
        body, name="rs_pair_sum", out_shape=jax.ShapeDtypeStruct((4, rows, cols), gp.dtype),
        grid_spec=pltpu.PrefetchScalarGridSpec(
            num_scalar_prefetch=1, grid=(4, rows // rb),
            in_specs=[pl.BlockSpec((1, 1, rb, cols), lambda j, r, cr: (j, cr[0], r, 0)),
                      pl.BlockSpec((1, rb, cols), lambda j, r, cr: (j, r, 0))],
            out_specs=pl.BlockSpec((1, rb, cols), lambda j, r, cr: (j, r, 0))),
        compiler_params=_cparams(2),
    )(c_arr, gp4, got)


def _rs_chip_exchange(pb):
    _, rows, cols = pb.shape

    def body(pb_ref, out_ref, send_sems, recv_sems):
        x, y, c = _place()
        chips = [(1 - x, y), (x, 1 - y), (1 - x, 1 - y)]
        cps = []
        for k, (px, py) in enumerate(chips):
            cp = pltpu.make_async_remote_copy(
                src_ref=pb_ref.at[2 * px + py], dst_ref=out_ref.at[k],
                send_sem=send_sems.at[k], recv_sem=recv_sems.at[k], device_id=(px, py, c), device_id_type=MESH)
            cp.start()
            cps.append(cp)
        for cp in cps:
            cp.wait()

    any_spec = pl.BlockSpec(memory_space=pl.ANY)
    return pl.pallas_call(
        body, name="rs_chip_exchange", out_shape=jax.ShapeDtypeStruct((3, rows, cols), pb.dtype),
        in_specs=[any_spec], out_specs=any_spec,
        scratch_shapes=[pltpu.SemaphoreType.DMA((3,)), pltpu.SemaphoreType.DMA((3,))],
    )(pb)


def _rs_final_sum(pb, got, chip_arr):
    _, rows, cols = pb.shape
    rb = 512

    def body(j_ref, a_ref, b_ref, o_ref):
        o_ref[...] = ((a_ref[0].astype(F32) + b_ref[0].astype(F32)) + b_ref[1].astype(F32)) + b_ref[2].astype(F32)

    return pl.pallas_call(
        body, name="rs_final_sum", out_shape=jax.ShapeDtypeStruct((rows, cols), F32),
        grid_spec=pltpu.PrefetchScalarGridSpec(
            num_scalar_prefetch=1, grid=(rows // rb,),
            in_specs=[pl.BlockSpec((1, rb, cols), lambda r, jr: (jr[0], r, 0)),
                      pl.BlockSpec((3, rb, cols), lambda r, jr: (0, r, 0))],
            out_specs=pl.BlockSpec((rb, cols), lambda r, jr: (r, 0))),
        compiler_params=_cparams(1),
    )(chip_arr, pb, got)


def _rope_tables(pos_col, inv_freq):
    t = pos_col.shape[0]

    def body(p_ref, f_ref, c_ref, s1_ref, s2_ref):
        ang = p_ref[...].astype(F32) * f_ref[...]
        lane = lax.broadcasted_iota(jnp.int32, ang.shape, 1)
        c_ref[...] = jnp.where(lane < QK_ROPE, jnp.cos(ang), 0.0)
        s = jnp.sin(ang)
        s1_ref[...] = jnp.where(lane < 32, -s, 0.0)
        s2_ref[...] = jnp.where((lane >= 32) & (lane < QK_ROPE), s, 0.0)

    spec = pl.BlockSpec((TB, 128), lambda i: (i, 0))
    return pl.pallas_call(
        body, name="rope_tables", out_shape=[jax.ShapeDtypeStruct((t, 128), F32)] * 3, grid=(t // TB,),
        in_specs=[pl.BlockSpec((TB, 1), lambda i: (i, 0)), _const_spec((1, 128))], out_specs=[spec] * 3,
        compiler_params=_cparams(1),
    )(pos_col, inv_freq)


def _sgu_conv_fwd(proj, tail, lng_ref, ws_ref, bst_ref, cw_ref):
    gu = _gelu(proj[:, 0:SG_W])
    gv = _gelu(proj[:, SG_W:2 * SG_W])
    bg = proj[:, 1024:1536]
    z = proj[:, 1536:2048] * proj[:, 2048:2560]
    heads = []
    for h in range(SG_HEADS):
        sl = slice(h * SG_HD, (h + 1) * SG_HD)
        vn, _, _ = _ln_head(gv[:, sl], lng_ref[:, sl])
        vnb = vn.astype(BF16)
        wm = _tril_bf16(ws_ref[h])
        bcol = bst_ref[:, h:h + 1]
        mixed = jnp.concatenate(
            [_dot(wm, vnb[k * SG_CHUNK:(k + 1) * SG_CHUNK], 1, 0) + bcol for k in range(TB // SG_CHUNK)], axis=0)
        heads.append(gu[:, sl] * mixed)
    a_out = jnp.concatenate(heads, axis=1)
    y, _, _ = _conv_fwd(z, tail, cw_ref)
    return a_out, bg * y, z


def _even_fwd(x, wg, gamma, lng, ws, bst, cw, seq):
    t = x.shape[0]
    nbs = seq // TB

    def body(x_ref, gam_ref, win_ref, wout_ref, lng_ref, ws_ref, bst_ref, cw_ref, x1_ref, proj_ref, tail_ref):
        i = pl.program_id(0)
        xv = x_ref[...]
        h, _ = _rms(xv, gam_ref[...])
        proj = _dot(h.astype(BF16), win_ref[...].reshape(EVEN_IN, D), 1, 1)
        proj_ref[...] = proj.astype(BF16)
        tail = jnp.where(i % nbs == 0, 0.0, tail_ref[...])
        a_out, b_out, z = _sgu_conv_fwd(proj, tail, lng_ref, ws_ref, bst_ref, cw_ref)
        tail_ref[...] = z[TB - HALO:, :]
        x1_ref[...] = (xv + _dot(a_out.astype(BF16), wout_ref[0:4].reshape(512, D), 1, 0)
                       + _dot(b_out.astype(BF16), wout_ref[4:8].reshape(512, D), 1, 0))

    row = pl.BlockSpec((TB, D), lambda i: (i, 0))
    return pl.pallas_call(
        body, name="even_fwd",
        out_shape=[jax.ShapeDtypeStruct((t, D), F32), jax.ShapeDtypeStruct((t, EVEN_IN), BF16)],
        grid=(t // TB,),
        in_specs=[row, _const_spec((1, D)), _wspec(N_EIN, OFF_EIN), _wspec(N_SQ, OFF_EOUT), _const_spec((1, SG_W)),
                  _const_spec((SG_HEADS, 128, 128)), _const_spec((128, 128)), _const_spec((8, SC_W))],
        out_specs=[row, pl.BlockSpec((TB, EVEN_IN), lambda i: (i, 0))],
        scratch_shapes=[pltpu.VMEM((HALO, SC_W), F32)],
        compiler_params=_cparams(1),
    )(x, gamma, wg, wg, lng, ws, bst, cw)


def _even_bwd(x, proj, dx1, wg, gamma, lng, ws, bst, cw, seq):
    t = x.shape[0]
    nb, nbs = t // TB, seq // TB

    def body(x_ref, proj_ref, ptail_ref, dx1_ref, gam_ref, win_ref, wout_ref, lng_ref, ws_ref, bst_ref, cw_ref,
             dx0_ref, dproj_ref, mix_ref, h_ref, dgam_ref, dws_ref, dbc_ref, dlng_ref, dcw_ref, head_ref):
        i = pl.program_id(0)
        blk = nb - 1 - i

        @pl.when(i == 0)
        def _():
            dgam_ref[...] = jnp.zeros_like(dgam_ref)
            dws_ref[...] = jnp.zeros_like(dws_ref)
            dbc_ref[...] = jnp.zeros_like(dbc_ref)
            dlng_ref[...] = jnp.zeros_like(dlng_ref)
            dcw_ref[...] = jnp.zeros_like(dcw_ref)

        xv = x_ref[...]
        gam = gam_ref[...]
        h, r = _rms(xv, gam)
        h_ref[...] = h.astype(BF16)
        dx1 = dx1_ref[...]
        dmix = _dot(dx1.astype(BF16), wout_ref[...].reshape(D, D), 1, 1)
        da, db = dmix[:, :SG_W], dmix[:, SG_W:]
        proj = proj_ref[...].astype(F32)
        u, v = proj[:, 0:SG_W], proj[:, SG_W:2 * SG_W]
        bg, cg, hv = proj[:, 1024:1536], proj[:, 1536:2048], proj[:, 2048:2560]
        gu, gv = _gelu(u), _gelu(v)

        a_heads, dgv_heads = [], []
        for hd in range(SG_HEADS):
            sl = slice(hd * SG_HD, (hd + 1) * SG_HD)
            g_h = lng_ref[:, sl]
            vn, xh, rr = _ln_head(gv[:, sl], g_h)
            vnb = vn.astype(BF16)
            wm = _tril_bf16(ws_ref[hd])
            bcol = bst_ref[:, hd:hd + 1]
            mixed_c, dvn_c = [], []
            dw_acc = jnp.zeros((128, 128), F32)
            db_acc = jnp.zeros((128, 1), F32)
            for k in range(TB // SG_CHUNK):
                rs = slice(k * SG_CHUNK, (k + 1) * SG_CHUNK)
                mixed = _dot(wm, vnb[rs], 1, 0) + bcol
                dmixed = da[rs, sl] * gu[rs, sl]
                dmb = dmixed.astype(BF16)
                dvn_c.append(_dot(wm, dmb, 0, 0))
                dw_acc = dw_acc + _dot(dmb, vnb[rs], 1, 1)
                db_acc = db_acc + jnp.sum(dmixed, axis=1, keepdims=True)
                mixed_c.append(mixed)
            mixed_h = jnp.concatenate(mixed_c, axis=0)
            dvn = jnp.concatenate(dvn_c, axis=0)
            r_i = lax.broadcasted_iota(jnp.int32, (128, 128), 0)
            c_i = lax.broadcasted_iota(jnp.int32, (128, 128), 1)
            dws_ref[hd] += jnp.where(r_i >= c_i, dw_acc, 0.0)
            dbc_ref[:, hd:hd + 1] += db_acc
            dlng_ref[:, sl] += jnp.sum(dvn * xh, axis=0, keepdims=True)
            dxh = dvn * g_h
            dgv = rr * (dxh - jnp.mean(dxh, axis=-1, keepdims=True)
                        - xh * jnp.mean(dxh * xh, axis=-1, keepdims=True))
            a_heads.append(gu[:, sl] * mixed_h)
            dproj_ref[:, sl] = (da[:, sl] * mixed_h * _gelu_grad(u[:, sl])).astype(BF16)
            dgv_heads.append(dgv * _gelu_grad(v[:, sl]))
        dproj_ref[:, SG_W:2 * SG_W] = jnp.concatenate(dgv_heads, axis=1).astype(BF16)
        mix_ref[:, :SG_W] = jnp.concatenate(a_heads, axis=1).astype(BF16)

        z = cg * hv
        pt = ptail_ref[...].astype(F32)
        tail = jnp.where(blk % nbs == 0, 0.0, pt[:, 1536:2048] * pt[:, 2048:2560])
        y, zs1, zs2 = _conv_fwd(z, tail, cw_ref)
        mix_ref[:, SG_W:] = (bg * y).astype(BF16)
        dy = db * bg
        head = jnp.where(blk % nbs == nbs - 1, 0.0, head_ref[...])
        ext = jnp.concatenate([dy, head], axis=0)
        dz = (cw_ref[2:3, :] * dy + cw_ref[1:2, :] * _shift_up(ext, 1)[:TB]
              + cw_ref[0:1, :] * _shift_up(ext, 2)[:TB])
        head_ref[...] = dy[:HALO, :]
        dcw_ref[2:3, :] += jnp.sum(dy * z, axis=0, keepdims=True)
        dcw_ref[1:2, :] += jnp.sum(dy * zs1, axis=0, keepdims=True)
        dcw_ref[0:1, :] += jnp.sum(dy * zs2, axis=0, keepdims=True)
        dproj_ref[:, 1024:1536] = (db * y).astype(BF16)
        dproj_ref[:, 1536:2048] = (dz * hv).astype(BF16)
        dproj_ref[:, 2048:2560] = (dz * cg).astype(BF16)

        dh = _dot(dproj_ref[...], win_ref[...].reshape(EVEN_IN, D), 1, 0)
        dxn, dgam = _rms_bwd(xv, r, gam, dh)
        dgam_ref[...] += dgam
        dx0_ref[...] = dx1 + dxn

    def rev(w):
        return pl.BlockSpec((TB, w), lambda i: (nb - 1 - i, 0))

    ptail = pl.BlockSpec((HALO, EVEN_IN), lambda i: (jnp.maximum((nb - 1 - i) * (TB // HALO) - 1, 0), 0))
    return pl.pallas_call(
        body, name="even_bwd",
        out_shape=[jax.ShapeDtypeStruct((t, D), F32), jax.ShapeDtypeStruct((t, EVEN_IN), BF16),
                   jax.ShapeDtypeStruct((t, D), BF16), jax.ShapeDtypeStruct((t, D), BF16),
                   jax.ShapeDtypeStruct((1, D), F32), jax.ShapeDtypeStruct((SG_HEADS, 128, 128), F32),
                   jax.ShapeDtypeStruct((128, 128), F32), jax.ShapeDtypeStruct((1, SG_W), F32),
                   jax.ShapeDtypeStruct((8, SC_W), F32)],
        grid=(nb,),
        in_specs=[rev(D), rev(EVEN_IN), ptail, rev(D), _const_spec((1, D)), _wspec(N_EIN, OFF_EIN),
                  _wspec(N_SQ, OFF_EOUT), _const_spec((1, SG_W)), _const_spec((SG_HEADS, 128, 128)),
                  _const_spec((128, 128)), _const_spec((8, SC_W))],
        out_specs=[rev(D), rev(EVEN_IN), rev(D), rev(D), _const_spec((1, D)), _const_spec((SG_HEADS, 128, 128)),
                   _const_spec((128, 128)), _const_spec((1, SG_W)), _const_spec((8, SC_W))],
        scratch_shapes=[pltpu.VMEM((HALO, SC_W), F32)],
        compiler_params=_cparams(1),
    )(x, proj, proj, dx1, gamma, wg, wg, lng, ws, bst, cw)


def _ffn_fwd(x, wg, gamma, layer):
    t = x.shape[0]
    off = OFF_FF + 3 * N_FF * layer

    def body(x_ref, gam_ref, wg_ref, wu_ref, wd_ref, y_ref, g_ref, u_ref):
        xv = x_ref[...]
        h, _ = _rms(xv, gam_ref[...])
        hb = h.astype(BF16)
        g = _dot(hb, wg_ref[...].reshape(D_FF, D), 1, 1)
        u = _dot(hb, wu_ref[...].reshape(D_FF, D), 1, 1)
        g_ref[...] = g.astype(BF16)
        u_ref[...] = u.astype(BF16)
        act = g * jax.nn.sigmoid(g) * u
        y_ref[...] = xv + _dot(act.astype(BF16), wd_ref[...].reshape(D_FF, D), 1, 0)

    row = pl.BlockSpec((TB, D), lambda i: (i, 0))
    wide = pl.BlockSpec((TB, D_FF), lambda i: (i, 0))
    return pl.pallas_call(
        body, name=f"ffn_fwd{layer}",
        out_shape=[jax.ShapeDtypeStruct((t, D), F32), jax.ShapeDtypeStruct((t, D_FF), BF16),
                   jax.ShapeDtypeStruct((t, D_FF), BF16)],
        grid=(t // TB,),
        in_specs=[row, _const_spec((1, D)), _wspec(N_FF, off), _wspec(N_FF, off + N_FF), _wspec(N_FF, off + 2 * N_FF)],
        out_specs=[row, wide, wide],
        compiler_params=_cparams(1),
    )(x, gamma, wg, wg, wg)


def _ffn_bwd(x, g, u, dy, wg, gamma, layer):
    t = x.shape[0]
    off = OFF_FF + 3 * N_FF * layer

    def body(x_ref, g_ref, u_ref, dy_ref, gam_ref, wg_ref, wu_ref, wd_ref,
             dx_ref, act_ref, dg_ref, du_ref, h_ref, dgam_ref):
        @pl.when(pl.program_id(0) == 0)
        def _():
            dgam_ref[...] = jnp.zeros_like(dgam_ref)

        xv = x_ref[...]
        gam = gam_ref[...]
        h, r = _rms(xv, gam)
        h_ref[...] = h.astype(BF16)
        dyv = dy_ref[...]
        dact = _dot(dyv.astype(BF16), wd_ref[...].reshape(D_FF, D), 1, 1)
        gv = g_ref[...].astype(F32)
        uv = u_ref[...].astype(F32)
        sg = jax.nn.sigmoid(gv)
        silu = gv * sg
        act_ref[...] = (silu * uv).astype(BF16)
        dgb = (dact * uv * (sg * (1.0 + gv * (1.0 - sg)))).astype(BF16)
        dub = (dact * silu).astype(BF16)
        dg_ref[...] = dgb
        du_ref[...] = dub
        dh = _dot(dgb, wg_ref[...].reshape(D_FF, D), 1, 0) + _dot(dub, wu_ref[...].reshape(D_FF, D), 1, 0)
        dxn, dgam = _rms_bwd(xv, r, gam, dh)
        dgam_ref[...] += dgam
        dx_ref[...] = dyv + dxn

    row = pl.BlockSpec((TB_FFN_BWD, D), lambda i: (i, 0))
    wide = pl.BlockSpec((TB_FFN_BWD, D_FF), lambda i: (i, 0))
    return pl.pallas_call(
        body, name=f"ffn_bwd{layer}",
        out_shape=[jax.ShapeDtypeStruct((t, D), F32), jax.ShapeDtypeStruct((t, D_FF), BF16),
                   jax.ShapeDtypeStruct((t, D_FF), BF16), jax.ShapeDtypeStruct((t, D_FF), BF16),
                   jax.ShapeDtypeStruct((t, D), BF16), jax.ShapeDtypeStruct((1, D), F32)],
        grid=(t // TB_FFN_BWD,),
        in_specs=[row, wide, wide, row, _const_spec((1, D)), _wspec(N_FF, off), _wspec(N_FF, off + N_FF),
                  _wspec(N_FF, off + 2 * N_FF)],
        out_specs=[row, wide, wide, wide, row, _const_spec((1, D))],
        compiler_params=_cparams(1),
    )(x, g, u, dy, gamma, wg, wg, wg)


def _odd_pre_fwd(x, wg, gamma, qbt, kvbt, qa_g, kva_g, pw_bd, pscale, seq):
    t = x.shape[0]
    nbs = seq // TB

    def body(x_ref, gam_ref, win_ref, qb_ref, kvb_ref, qa_ref, kva_ref, pw_ref, ps_ref,
             proj_ref, q_ref, kv_ref, kr_ref, c_ref, tail_ref):
        i = pl.program_id(0)
        h, _ = _rms(x_ref[...], gam_ref[...])
        proj = _dot(h.astype(BF16), win_ref[...].reshape(D, D), 1, 0)
        proj_ref[...] = proj.astype(BF16)
        zp, ql, kvl = proj[:, :POOL_W], proj[:, 256:640], proj[:, 640:896]
        kr_ref[...] = proj[:, 896:1024]
        qn, _ = _rms(ql, qa_ref[...])
        q_ref[...] = _dot(qn.astype(BF16), qb_ref[...], 1, 1).astype(BF16)
        kvn, _ = _rms(kvl, kva_ref[...])
        kv_ref[...] = _dot(kvn.astype(BF16), kvb_ref[...], 1, 1).astype(BF16)
        tail = jnp.where(i % nbs == 0, 0.0, tail_ref[...])
        pooled, _, _ = _pool_fwd(zp, tail, i % nbs)
        tail_ref[...] = zp[TB - HALO:, :]
        c_ref[...] = (_dot(pooled.astype(BF16), pw_ref[...], 1, 0) * ps_ref[...]).astype(BF16)

    def row(w):
        return pl.BlockSpec((TB, w), lambda i: (i, 0))

    return pl.pallas_call(
        body, name="odd_pre_fwd",
        out_shape=[jax.ShapeDtypeStruct((t, D), BF16), jax.ShapeDtypeStruct((t, HEADS * HP), BF16),
                   jax.ShapeDtypeStruct((t, HEADS * HP), BF16), jax.ShapeDtypeStruct((t, 128), F32),
                   jax.ShapeDtypeStruct((t, POOL_W), BF16)],
        grid=(t // TB,),
        in_specs=[row(D), _const_spec((1, D)), _wspec(N_SQ, OFF_OIN), _const_spec((HEADS * HP, Q_LORA)),
                  _const_spec((HEADS * HP, KV_LORA)), _const_spec((1, Q_LORA)), _const_spec((1, KV_LORA)),
                  _const_spec((POOL_W, POOL_W)), _const_spec((1, POOL_W))],
        out_specs=[row(D), row(HEADS * HP), row(HEADS * HP), row(128), row(POOL_W)],
        scratch_shapes=[pltpu.VMEM((HALO, POOL_W), F32)],
        compiler_params=_cparams(1),
    )(x, gamma, wg, qbt, kvbt, qa_g, kva_g, pw_bd, pscale)


def _odd_pre_bwd(x, proj, dx3, dmix, dq, dkv, dkr, wg, gamma, qbt, kvbt, qa_g, kva_g, pw_bd, pscale, seq):
    t = x.shape[0]
    nb, nbs = t // TB, seq // TB

    def body(x_ref, proj_ref, ptail_ref, dx3_ref, dco_ref, dq_ref, dkv_ref, dkr_ref, gam_ref, win_ref, qb_ref,
             kvb_ref, qa_ref, kva_ref, pw_ref, ps_ref,
             dx2_ref, dproj_ref, h_ref, qn_ref, kvn_ref, dgam_ref, dqa_ref, dkva_ref, dpw_ref, dps_ref, head_ref):
        i = pl.program_id(0)
        blk = nb - 1 - i

        @pl.when(i == 0)
        def _():
            dgam_ref[...] = jnp.zeros_like(dgam_ref)
            dqa_ref[...] = jnp.zeros_like(dqa_ref)
            dkva_ref[...] = jnp.zeros_like(dkva_ref)
            dpw_ref[...] = jnp.zeros_like(dpw_ref)
            dps_ref[...] = jnp.zeros_like(dps_ref)

        xv = x_ref[...]
        gam = gam_ref[...]
        h, r = _rms(xv, gam)
        h_ref[...] = h.astype(BF16)
        proj = proj_ref[...].astype(F32)
        zp, ql, kvl = proj[:, :POOL_W], proj[:, 256:640], proj[:, 640:896]

        qa = qa_ref[...]
        qn, rq = _rms(ql, qa)
        qn_ref[...] = qn.astype(BF16)
        dql, dqa = _rms_bwd(ql, rq, qa, _dot(dq_ref[...], qb_ref[...], 1, 0))
        dqa_ref[...] += dqa
        kva = kva_ref[...]
        kvn, rkv = _rms(kvl, kva)
        kvn_ref[...] = kvn.astype(BF16)
        dkvl, dkva = _rms_bwd(kvl, rkv, kva, _dot(dkv_ref[...], kvb_ref[...], 1, 0))
        dkva_ref[...] += dkva

        pt = ptail_ref[...].astype(F32)
        tail = jnp.where(blk % nbs == 0, 0.0, pt[:, :POOL_W])
        pooled, cnt, grp = _pool_fwd(zp, tail, blk % nbs)
        pb = pooled.astype(BF16)
        pw = pw_ref[...]
        dco = dco_ref[...].astype(F32)
        dps_ref[...] += jnp.sum(dco * _dot(pb, pw, 1, 0), axis=0, keepdims=True)
        dpo = (dco * ps_ref[...]).astype(BF16)
        dpw_ref[...] += _dot(pb, dpo, 0, 0)
        dpooled = _dot(dpo, pw, 1, 1)
        dpm = dpooled / cnt
        head = jnp.where(blk % nbs == nbs - 1, 0.0, head_ref[...])
        dz = _pool_bwd(dpooled, dpm, head, grp)
        head_ref[...] = dpm[:HALO, :]

        dproj_ref[:, :POOL_W] = dz.astype(BF16)
        dproj_ref[:, 256:640] = dql.astype(BF16)
        dproj_ref[:, 640:896] = dkvl.astype(BF16)
        dproj_ref[:, 896:1024] = dkr_ref[...].astype(BF16)
        dh = _dot(dproj_ref[...], win_ref[...].reshape(D, D), 1, 1)
        dxn, dgam = _rms_bwd(xv, r, gam, dh)
        dgam_ref[...] += dgam
        dx2_ref[...] = dx3_ref[...] + dxn

    def rev(w):
        return pl.BlockSpec((TB, w), lambda i: (nb - 1 - i, 0))

    ptail = pl.BlockSpec((HALO, D), lambda i: (jnp.maximum((nb - 1 - i) * (TB // HALO) - 1, 0), 0))
    return pl.pallas_call(
        body, name="odd_pre_bwd",
        out_shape=[jax.ShapeDtypeStruct((t, D), F32), jax.ShapeDtypeStruct((t, D), BF16),
                   jax.ShapeDtypeStruct((t, D), BF16), jax.ShapeDtypeStruct((t, Q_LORA), BF16),
                   jax.ShapeDtypeStruct((t, KV_LORA), BF16), jax.ShapeDtypeStruct((1, D), F32),
                   jax.ShapeDtypeStruct((1, Q_LORA), F32), jax.ShapeDtypeStruct((1, KV_LORA), F32),
                   jax.ShapeDtypeStruct((POOL_W, POOL_W), F32), jax.ShapeDtypeStruct((1, POOL_W), F32)],
        grid=(nb,),
        in_specs=[rev(D), rev(D), ptail, rev(D), rev(POOL_W), rev(HEADS * HP), rev(HEADS * HP), rev(128),
                  _const_spec((1, D)), _wspec(N_SQ, OFF_OIN), _const_spec((HEADS * HP, Q_LORA)),
                  _const_spec((HEADS * HP, KV_LORA)), _const_spec((1, Q_LORA)), _const_spec((1, KV_LORA)),
                  _const_spec((POOL_W, POOL_W)), _const_spec((1, POOL_W))],
        out_specs=[rev(D), rev(D), rev(D), rev(Q_LORA), rev(KV_LORA), _const_spec((1, D)), _const_spec((1, Q_LORA)),
                   _const_spec((1, KV_LORA)), _const_spec((POOL_W, POOL_W)), _const_spec((1, POOL_W))],
        scratch_shapes=[pltpu.VMEM((HALO, POOL_W), F32)],
        compiler_params=_cparams(1),
    )(x, proj, proj, dx3, dmix, dq, dkv, dkr, gamma, wg, qbt, kvbt, qa_g, kva_g, pw_bd, pscale)


def _attn_specs(seq):
    head = pl.BlockSpec((seq, HP), lambda b, h: (b, h))
    shared = pl.BlockSpec((seq, 128), lambda b, h: (b, 0))
    gain = pl.BlockSpec((1, HP), lambda b, h: (0, 0))
    return head, shared, gain


def _causal_probs(qf, kf, q0, q1):
    s = _dot(qf[q0:q1], kf[:q1], 1, 1) * ATT_SCALE
    rows = lax.broadcasted_iota(jnp.int32, s.shape, 0) + q0
    cols = lax.broadcasted_iota(jnp.int32, s.shape, 1)
    s = jnp.where(cols <= rows, s, NEG_INF)
    p = jnp.exp(s - jnp.max(s, axis=-1, keepdims=True))
    return p, jnp.sum(p, axis=-1, keepdims=True)


def _attn_fwd(q, kv, kr, cos, s1, s2, gq, gk, seq):
    t = q.shape[0]
    qb = min(512, seq)

    def body(q_ref, kv_ref, kr_ref, c_ref, s1_ref, s2_ref, gq_ref, gk_ref, o_ref):
        c, sa, sb = c_ref[...], s1_ref[...], s2_ref[...]
        qf, _ = _qk_prep(q_ref[...].astype(F32), gq_ref[...], c, sa, sb)
        kin = jnp.concatenate([kv_ref[:, :128].astype(F32), kr_ref[...]], axis=1)
        kf, _ = _qk_prep(kin, gk_ref[...], c, sa, sb)
        qf, kf = qf.astype(BF16), kf.astype(BF16)
        vb = kv_ref[:, 128:]
        for q0 in range(0, seq, qb):
            q1 = q0 + qb
            p, l = _causal_probs(qf, kf, q0, q1)
            o_ref[q0:q1, :] = (_dot(p.astype(BF16), vb[:q1], 1, 0) / l).astype(BF16)

    head, shared, gain = _attn_specs(seq)
    return pl.pallas_call(
        body, name="attn_fwd", out_shape=jax.ShapeDtypeStruct((t, HEADS * V_DIM), BF16),
        grid=(t // seq, HEADS),
        in_specs=[head, head, shared, shared, shared, shared, gain, gain],
        out_specs=pl.BlockSpec((seq, V_DIM), lambda b, h: (b, h)),
        compiler_params=_cparams(2),
    )(q, kv, kr, cos, s1, s2, gq, gk)


def _attn_bwd(q, kv, kr, cos, s1, s2, gq, gk, dmix, seq):
    t = q.shape[0]
    qb = min(512, seq)

    def body(q_ref, kv_ref, kr_ref, c_ref, s1_ref, s2_ref, gq_ref, gk_ref, do_ref,
             dq_ref, dkv_ref, dkr_ref, dgq_ref, dgk_ref, dqf_ref, dkf_ref, dv_ref):
        b, hd = pl.program_id(0), pl.program_id(1)

        @pl.when((b == 0) & (hd == 0))
        def _():
            dgq_ref[...] = jnp.zeros_like(dgq_ref)
            dgk_ref[...] = jnp.zeros_like(dgk_ref)

        c, sa, sb = c_ref[...], s1_ref[...], s2_ref[...]
        gq_v, gk_v = gq_ref[...], gk_ref[...]
        qin = q_ref[...].astype(F32)
        kin = jnp.concatenate([kv_ref[:, :128].astype(F32), kr_ref[...]], axis=1)
        qf32, rq = _qk_prep(qin, gq_v, c, sa, sb)
        kf32, rk = _qk_prep(kin, gk_v, c, sa, sb)
        qf, kf = qf32.astype(BF16), kf32.astype(BF16)
        vb = kv_ref[:, 128:]
        dkf_ref[...] = jnp.zeros_like(dkf_ref)
        dv_ref[...] = jnp.zeros_like(dv_ref)
        for q0 in range(0, seq, qb):
            q1 = q0 + qb
            p, l = _causal_probs(qf, kf, q0, q1)
            p = p / l
            pbf = p.astype(BF16)
            do = do_ref[q0:q1, :]
            dv_ref[:q1, :] += _dot(pbf, do, 0, 0)
            dp = _dot(do, vb[:q1], 1, 1)
            ds = (p * (dp - jnp.sum(p * dp, axis=-1, keepdims=True)) * ATT_SCALE).astype(BF16)
            dqf_ref[q0:q1, :] = _dot(ds, kf[:q1], 1, 0)
            dkf_ref[:q1, :] += _dot(ds, qf[q0:q1], 0, 0)
        dqin, dgq = _qk_prep_bwd(dqf_ref[...], qin, rq, gq_v, c, sa, sb)
        dkin, dgk = _qk_prep_bwd(dkf_ref[...], kin, rk, gk_v, c, sa, sb)
        dgq_ref[...] += dgq
        dgk_ref[...] += dgk
        dq_ref[...] = dqin.astype(BF16)
        dkv_ref[:, :128] = dkin[:, :128].astype(BF16)
        dkv_ref[:, 128:] = dv_ref[...].astype(BF16)

        @pl.when(hd == 0)
        def _():
            dkr_ref[...] = dkin[:, 128:]

        @pl.when(hd != 0)
        def _():
            dkr_ref[...] += dkin[:, 128:]

    head, shared, gain = _attn_specs(seq)
    return pl.pallas_call(
        body, name="attn_bwd",
        out_shape=[jax.ShapeDtypeStruct((t, HEADS * HP), BF16), jax.ShapeDtypeStruct((t, HEADS * HP), BF16),
                   jax.ShapeDtypeStruct((t, 128), F32), jax.ShapeDtypeStruct((1, HP), F32),
                   jax.ShapeDtypeStruct((1, HP), F32)],
        grid=(t // seq, HEADS),
        in_specs=[head, head, shared, shared, shared, shared, gain, gain,
                  pl.BlockSpec((seq, V_DIM), lambda b, h: (b, 2 + h))],
        out_specs=[head, head, shared, gain, gain],
        scratch_shapes=[pltpu.VMEM((seq, HP), F32), pltpu.VMEM((seq, HP), F32), pltpu.VMEM((seq, V_DIM), F32)],
        compiler_params=_cparams(2),
    )(q, kv, kr, cos, s1, s2, gq, gk, dmix)


def _odd_post_fwd(x, c_out, d_out, wg):
    t = x.shape[0]

    def body(x_ref, c_ref, d_ref, w_ref, y_ref):
        y_ref[...] = (x_ref[...] + _dot(c_ref[...], w_ref[0:2].reshape(POOL_W, D), 1, 0)
                      + _dot(d_ref[...], w_ref[2:8].reshape(HEADS * V_DIM, D), 1, 0))

    def row(w):
        return pl.BlockSpec((TB, w), lambda i: (i, 0))

    return pl.pallas_call(
        body, name="odd_post_fwd", out_shape=jax.ShapeDtypeStruct((t, D), F32), grid=(t // TB,),
        in_specs=[row(D), row(POOL_W), row(HEADS * V_DIM), _wspec(N_SQ, OFF_OOUT)], out_specs=row(D),
        compiler_params=_cparams(1),
    )(x, c_out, d_out, wg)


def _odd_post_bwd(dx3, wg):
    t = dx3.shape[0]

    def body(d_ref, w_ref, o_ref):
        o_ref[...] = _dot(d_ref[...].astype(BF16), w_ref[...].reshape(D, D), 1, 1).astype(BF16)

    row = pl.BlockSpec((TB, D), lambda i: (i, 0))
    return pl.pallas_call(
        body, name="odd_post_bwd", out_shape=jax.ShapeDtypeStruct((t, D), BF16), grid=(t // TB,),
        in_specs=[row, _wspec(N_SQ, OFF_OOUT)], out_specs=row, compiler_params=_cparams(1),
    )(dx3, wg)


def _loss_grad(y, target):
    t = y.shape[0]

    def body(y_ref, t_ref, dy_ref, loss_ref):
        @pl.when(pl.program_id(0) == 0)
        def _():
            loss_ref[...] = jnp.zeros_like(loss_ref)

        err = y_ref[...] - t_ref[...]
        dy_ref[...] = err * (1.0 / D)
        sq = jnp.sum(jnp.sum(err * err, axis=-1, keepdims=True), axis=0, keepdims=True)
        loss_ref[...] += (0.5 / D) * sq

    row = pl.BlockSpec((TB, D), lambda i: (i, 0))
    return pl.pallas_call(
        body, name="loss_grad",
        out_shape=[jax.ShapeDtypeStruct((t, D), F32), jax.ShapeDtypeStruct((8, 128), F32)], grid=(t // TB,),
        in_specs=[row, row], out_specs=[row, _const_spec((8, 128))], compiler_params=_cparams(1),
    )(y, target)


def _tn(a, b, tm, name, out_dtype=BF16):
    t, m = a.shape
    n = b.shape[1]
    nk = t // TB

    def body(a_ref, b_ref, o_ref, acc_ref):
        k = pl.program_id(1)

        @pl.when(k == 0)
        def _():
            acc_ref[...] = jnp.zeros_like(acc_ref)

        acc_ref[...] += _dot(a_ref[...].astype(BF16), b_ref[...].astype(BF16), 0, 0)

        @pl.when(k == nk - 1)
        def _():
            o_ref[...] = acc_ref[...].astype(out_dtype)

    return pl.pallas_call(
        body, name=name, out_shape=jax.ShapeDtypeStruct((m, n), out_dtype), grid=(m // tm, nk),
        in_specs=[pl.BlockSpec((TB, tm), lambda i, k: (k, i)), pl.BlockSpec((TB, n), lambda i, k: (k, 0))],
        out_specs=pl.BlockSpec((tm, n), lambda i, k: (i, 0)),
        scratch_shapes=[pltpu.VMEM((tm, n), F32)],
        compiler_params=_cparams(2),
    )(a, b)


def _adamw(ws, gs, ms, vs, name, row_block=None):
    n = len(ws)
    c1 = 1.0 - B1 ** STEP
    c2 = 1.0 - B2 ** STEP

    def body(*refs):
        for a in range(n):
            w, g, m, v = (refs[k * n + a][...] for k in range(4))
            d_ref, m_ref, v_ref = (refs[(4 + k) * n + a] for k in range(3))
            m_new = B1 * m + (1.0 - B1) * g
            v_new = B2 * v + (1.0 - B2) * (g * g)
            d_ref[...] = -LR * ((m_new / c1) / (jnp.sqrt(v_new / c2) + ADAM_EPS) + WD * w)
            m_ref[...] = m_new
            v_ref[...] = v_new

    if row_block is None:
        grid = (1,)
        specs = [pl.BlockSpec(w.shape, lambda i: (0, 0)) for w in ws]
    else:
        grid = (ws[0].shape[0] // row_block,)
        specs = [pl.BlockSpec((row_block, w.shape[1]), lambda i: (i, 0)) for w in ws]
    outs = pl.pallas_call(
        body, name=name, out_shape=[jax.ShapeDtypeStruct(w.shape, F32) for w in ws] * 3, grid=grid,
        in_specs=specs * 4, out_specs=specs * 3, compiler_params=_cparams(1),
    )(*ws, *gs, *ms, *vs)
    return outs[:n], outs[n:2 * n], outs[2 * n:]


def _rows1024(a, rows):
    flat = a.reshape(-1, D)
    return jnp.pad(flat, ((0, rows - flat.shape[0]), (0, 0)))


def _pack_shards(even_w_in, even_w_out, odd_w_in, q_b, kv_b, odd_w_out, ffn_w_gate, ffn_w_up, ffn_w_down):
    parts = [even_w_in[0].T, jnp.zeros((OFF_FF - N_EIN, D), F32)]
    for layer in range(2):
        parts += [ffn_w_gate[layer].T, ffn_w_up[layer].T, ffn_w_down[layer]]
    parts += [jnp.zeros((OFF_EOUT - OFF_FF - 6 * N_FF, D), F32), even_w_out[0],
              jnp.pad(odd_w_in[0], ((0, 0), (0, D - ODD_IN))), odd_w_out[0],
              _rows1024(q_b[0].T, N_QB), _rows1024(kv_b[0].T, N_KVB),
              jnp.zeros((R_PACK - OFF_KVB - N_KVB, D), F32)]
    return jnp.concatenate(parts, axis=0)


def _pad_heads(a):
    k = a.shape[1]
    return jnp.pad(a.reshape(HEADS, QK_DIM, k), ((0, 0), (0, HP - QK_DIM), (0, 0))).reshape(HEADS * HP, k)


def _small_pack(parts):
    flat = []
    for p in parts:
        v = p.reshape(-1)
        flat.append(jnp.pad(v, (0, (-v.shape[0]) % 1024)))
    return jnp.concatenate(flat).reshape(-1, 128)


def _small_unpack(buf, shapes):
    flat = buf.reshape(-1)
    out, off = [], 0
    for s in shapes:
        size = int(np.prod(s))
        out.append(flat[off:off + size].reshape(s))
        off += size + (-size) % 1024
    return out


def _local_step(x3d, positions, target3d, wg, mix_norm, ffn_norm, sg_ln_g, sg_w_s, sg_b_s, conv_w, pool_w,
                pool_scale, q_a_norm, kv_a_norm, q_norm, k_norm):
    bsz, seq, _ = x3d.shape
    t = bsz * seq
    x0 = x3d.reshape(t, D)
    target = target3d.reshape(t, D)

    ws = sg_w_s[0]
    bst = jnp.pad(sg_b_s[0].T, ((0, 0), (0, 128 - SG_HEADS)))
    cw = jnp.pad(conv_w, ((0, 8 - 3), (0, 0)))
    pw_bd = jax.scipy.linalg.block_diag(*[pool_w[0, g] for g in range(4)]).astype(BF16)
    gq = jnp.pad(q_norm, ((0, 0), (0, HP - QK_DIM)))
    gk = jnp.pad(k_norm, ((0, 0), (0, HP - QK_DIM)))
    qbt = _pad_heads(wg[:, OFF_QB:OFF_QB + N_QB_USED, :].reshape(HEADS * QK_DIM, Q_LORA))
    kvbt = wg[:, OFF_KVB:OFF_KVB + N_KVB, :].reshape(HEADS * HP, KV_LORA)
    lane = np.arange(128)
    inv_freq = np.where(lane < QK_ROPE, ROPE_THETA ** (-(2.0 * (lane % 32)) / QK_ROPE), 0.0)
    inv_freq = jnp.asarray(inv_freq.reshape(1, 128), F32)
    cos, s1, s2 = _rope_tables(positions.reshape(t, 1), inv_freq)

    x1, proj_e = _even_fwd(x0, wg, mix_norm[0:1], sg_ln_g, ws, bst, cw, seq)
    x2, g0, u0 = _ffn_fwd(x1, wg, ffn_norm[0:1], 0)
    proj_o, q, kv, kr, c_out = _odd_pre_fwd(x2, wg, mix_norm[1:2], qbt, kvbt, q_a_norm, kv_a_norm, pw_bd,
                                            pool_scale, seq)
    d_out = _attn_fwd(q, kv, kr, cos, s1, s2, gq, gk, seq)
    x3 = _odd_post_fwd(x2, c_out, d_out, wg)
    x4, g1, u1 = _ffn_fwd(x3, wg, ffn_norm[1:2], 1)
    dy, loss_tile = _loss_grad(x4, target)

    dx3, act1, dg1, du1, h3, dgam_f1 = _ffn_bwd(x3, g1, u1, dy, wg, ffn_norm[1:2], 1)
    d_down1 = _tn(act1, dy, 1408, "dw_down1")
    d_gate1 = _tn(dg1, h3, 1408, "dw_gate1")
    d_up1 = _tn(du1, h3, 1408, "dw_up1")
    dmix_o = _odd_post_bwd(dx3, wg)
    d_oout = jnp.concatenate([_tn(c_out, dx3, POOL_W, "dw_oout_c"), _tn(d_out, dx3, HEADS * V_DIM, "dw_oout_d")], 0)
    dq, dkv, dkr, dgq, dgk = _attn_bwd(q, kv, kr, cos, s1, s2, gq, gk, dmix_o, seq)
    (dx2, dproj_o, h2, qn, kvn, dgam_m1, dqa, dkva, dpw_bd, dps) = _odd_pre_bwd(
        x2, proj_o, dx3, dmix_o, dq, dkv, dkr, wg, mix_norm[1:2], qbt, kvbt, q_a_norm, kv_a_norm, pw_bd,
        pool_scale, seq)
    d_oin = _tn(h2, dproj_o, D, "dw_oin")
    d_qbt = _tn(dq, qn, HEADS * HP, "dw_qb")
    d_kvbt = _tn(dkv, kvn, HEADS * HP, "dw_kvb")
    dx1, act0, dg0, du0, h1, dgam_f0 = _ffn_bwd(x1, g0, u0, dx2, wg, ffn_norm[0:1], 0)
    d_down0 = _tn(act0, dx2, 1408, "dw_down0")
    d_gate0 = _tn(dg0, h1, 1408, "dw_gate0")
    d_up0 = _tn(du0, h1, 1408, "dw_up0")
    (dx0, dproj_e, mix_e, h0, dgam_m0, dws, dbc, dlng, dcw) = _even_bwd(
        x0, proj_e, dx1, wg, mix_norm[0:1], sg_ln_g, ws, bst, cw, seq)
    d_eout = _tn(mix_e, dx1, D, "dw_eout")
    d_eint = _tn(dproj_e, h0, 1280, "dw_ein")

    def slab(a, n):
        return a.reshape(N_DEV, n, D)

    d_qb_rows = d_qbt.reshape(HEADS, HP, Q_LORA)[:, :QK_DIM].reshape(N_DEV, N_QB_USED, D)
    gp = jnp.concatenate([
        slab(d_eint, N_EIN), jnp.zeros((N_DEV, OFF_FF - N_EIN, D), BF16),
        slab(d_gate0, N_FF), slab(d_up0, N_FF), slab(d_down0, N_FF),
        slab(d_gate1, N_FF), slab(d_up1, N_FF), slab(d_down1, N_FF),
        jnp.zeros((N_DEV, OFF_EOUT - OFF_FF - 6 * N_FF, D), BF16),
        slab(d_eout, N_SQ), slab(d_oin, N_SQ), slab(d_oout, N_SQ),
        jnp.pad(d_qb_rows, ((0, 0), (0, N_QB - N_QB_USED), (0, 0))), slab(d_kvbt, N_KVB),
        jnp.zeros((N_DEV, R_PACK - OFF_KVB - N_KVB, D), BF16)], axis=1)

    small = [
        jnp.concatenate([dgam_m0, dgam_m1], 0), jnp.concatenate([dgam_f0, dgam_f1], 0), dlng,
        dws[None], dbc[:, :SG_HEADS].T[None], dcw[:3],
        jnp.stack([dpw_bd[g * POOL_GD:(g + 1) * POOL_GD, g * POOL_GD:(g + 1) * POOL_GD] for g in range(4)])[None],
        dps, dqa, dkva, dgq[:, :QK_DIM], dgk[:, :QK_DIM], loss_tile[0:1, 0:1]]
    return dx0.reshape(bsz, seq, D), gp, small


SMALL_SHAPES = [(2, D), (2, D), (1, SG_W), (1, SG_HEADS, 128, 128), (1, SG_HEADS, 128), (3, SC_W),
                (1, 4, POOL_GD, POOL_GD), (1, POOL_W), (1, Q_LORA), (1, KV_LORA), (1, QK_DIM), (1, QK_DIM), (1, 1)]


def kernel(x, positions, mix_norm, ffn_norm, even_w_in, sg_ln_g, sg_w_s, sg_b_s, sc_conv_w, even_w_out, odd_w_in, pool_w, pool_scale, q_a_norm, q_b, kv_a_norm, kv_b, q_norm, k_norm, odd_w_out, ffn_w_gate, ffn_w_up, ffn_w_down, loss_target, m_mix_norm, m_ffn_norm, m_even_w_in, m_sg_ln_g, m_sg_w_s, m_sg_b_s, m_sc_conv_w, m_even_w_out, m_odd_w_in, m_pool_w, m_pool_scale, m_q_a_norm, m_q_b, m_kv_a_norm, m_kv_b, m_q_norm, m_k_norm, m_odd_w_out, m_ffn_w_gate, m_ffn_w_up, m_ffn_w_down, v_mix_norm, v_ffn_norm, v_even_w_in, v_sg_ln_g, v_sg_w_s, v_sg_b_s, v_sc_conv_w, v_even_w_out, v_odd_w_in, v_pool_w, v_pool_scale, v_q_a_norm, v_q_b, v_kv_a_norm, v_kv_b, v_q_norm, v_k_norm, v_odd_w_out, v_ffn_w_gate, v_ffn_w_up, v_ffn_w_down):
    xi, yi, ci = _place()
    me = 4 * xi + 2 * yi + ci

    packed = _pack_shards(even_w_in, even_w_out, odd_w_in, q_b, kv_b, odd_w_out, ffn_w_gate, ffn_w_up,
                          ffn_w_down).astype(BF16)
    def lane_pad(a):
        return jnp.pad(a, ((0, 0), (0, 128 - a.shape[1])))

    tile = jnp.concatenate([lane_pad(sc_conv_w[0]), lane_pad(pool_scale), lane_pad(q_a_norm), lane_pad(kv_a_norm),
                            jnp.zeros((2, 128), F32)], axis=0)
    wg, tiles = _all_gather([packed, tile], "all_gather_weights")
    conv_full = tiles[:, 0:3, 0:64].transpose(1, 0, 2).reshape(3, SC_W)
    pscale_full = tiles[:, 3, 0:32].reshape(1, POOL_W)
    qa_full = tiles[:, 4, 0:48].reshape(1, Q_LORA)
    kva_full = tiles[:, 5, 0:32].reshape(1, KV_LORA)

    grad_x, gp, small = _local_step(x, positions, loss_target, wg, mix_norm, ffn_norm, sg_ln_g, sg_w_s, sg_b_s,
                                    conv_full, pool_w, pscale_full, qa_full, kva_full, q_norm, k_norm)

    c_arr = jnp.reshape(ci, (1,)).astype(jnp.int32)
    chip_arr = jnp.reshape(2 * xi + yi, (1,)).astype(jnp.int32)
    got_a = _rs_pair_exchange(gp)
    pb = _rs_pair_sum(gp, got_a, c_arr)
    got_b = _rs_chip_exchange(pb)
    gsh = _rs_final_sum(pb, got_b, chip_arr)

    tot = _small_unpack(_all_reduce_small(_small_pack(small)), SMALL_SHAPES)
    (g_mix, g_ffn, g_lng, g_ws, g_bs, g_cw_full, g_pw, g_ps_full, g_qa_full, g_kva_full, g_qn, g_kn, loss) = tot
    g_cw = lax.dynamic_slice_in_dim(g_cw_full, me * 64, 64, axis=1)[None]
    g_ps = lax.dynamic_slice_in_dim(g_ps_full, me * 32, 32, axis=1)
    g_qa = lax.dynamic_slice_in_dim(g_qa_full, me * 48, 48, axis=1)
    g_kva = lax.dynamic_slice_in_dim(g_kva_full, me * 32, 32, axis=1)

    g_ein = gsh[OFF_EIN:OFF_EIN + N_EIN].T[None]
    ff = [gsh[OFF_FF + k * N_FF:OFF_FF + (k + 1) * N_FF] for k in range(6)]
    g_gate = jnp.stack([ff[0].T, ff[3].T])
    g_up = jnp.stack([ff[1].T, ff[4].T])
    g_down = jnp.stack([ff[2], ff[5]])
    g_eout = gsh[OFF_EOUT:OFF_EOUT + N_SQ][None]
    g_oin = gsh[OFF_OIN:OFF_OIN + N_SQ, :ODD_IN][None]
    g_oout = gsh[OFF_OOUT:OFF_OOUT + N_SQ][None]
    g_qb = gsh[OFF_QB:OFF_QB + N_QB_USED].reshape(144, Q_LORA).T[None]
    g_kvb = gsh[OFF_KVB:OFF_KVB + N_KVB].reshape(192, KV_LORA).T[None]

    grads = dict(mix_norm=g_mix, ffn_norm=g_ffn, even_w_in=g_ein, sg_ln_g=g_lng, sg_w_s=g_ws, sg_b_s=g_bs,
                 sc_conv_w=g_cw, even_w_out=g_eout, odd_w_in=g_oin, pool_w=g_pw, pool_scale=g_ps, q_a_norm=g_qa,
                 q_b=g_qb, kv_a_norm=g_kva, kv_b=g_kvb, q_norm=g_qn, k_norm=g_kn, odd_w_out=g_oout,
                 ffn_w_gate=g_gate, ffn_w_up=g_up, ffn_w_down=g_down)
    weights = dict(mix_norm=mix_norm, ffn_norm=ffn_norm, even_w_in=even_w_in, sg_ln_g=sg_ln_g, sg_w_s=sg_w_s,
                   sg_b_s=sg_b_s, sc_conv_w=sc_conv_w, even_w_out=even_w_out, odd_w_in=odd_w_in, pool_w=pool_w,
                   pool_scale=pool_scale, q_a_norm=q_a_norm, q_b=q_b, kv_a_norm=kv_a_norm, kv_b=kv_b, q_norm=q_norm,
                   k_norm=k_norm, odd_w_out=odd_w_out, ffn_w_gate=ffn_w_gate, ffn_w_up=ffn_w_up,
                   ffn_w_down=ffn_w_down)
    m_in = dict(mix_norm=m_mix_norm, ffn_norm=m_ffn_norm, even_w_in=m_even_w_in, sg_ln_g=m_sg_ln_g, sg_w_s=m_sg_w_s,
                sg_b_s=m_sg_b_s, sc_conv_w=m_sc_conv_w, even_w_out=m_even_w_out, odd_w_in=m_odd_w_in,
                pool_w=m_pool_w, pool_scale=m_pool_scale, q_a_norm=m_q_a_norm, q_b=m_q_b, kv_a_norm=m_kv_a_norm,
                kv_b=m_kv_b, q_norm=m_q_norm, k_norm=m_k_norm, odd_w_out=m_odd_w_out, ffn_w_gate=m_ffn_w_gate,
                ffn_w_up=m_ffn_w_up, ffn_w_down=m_ffn_w_down)
    v_in = dict(mix_norm=v_mix_norm, ffn_norm=v_ffn_norm, even_w_in=v_even_w_in, sg_ln_g=v_sg_ln_g, sg_w_s=v_sg_w_s,
                sg_b_s=v_sg_b_s, sc_conv_w=v_sc_conv_w, even_w_out=v_even_w_out, odd_w_in=v_odd_w_in,
                pool_w=v_pool_w, pool_scale=v_pool_scale, q_a_norm=v_q_a_norm, q_b=v_q_b, kv_a_norm=v_kv_a_norm,
                kv_b=v_kv_b, q_norm=v_q_norm, k_norm=v_k_norm, odd_w_out=v_odd_w_out, ffn_w_gate=v_ffn_w_gate,
                ffn_w_up=v_ffn_w_up, ffn_w_down=v_ffn_w_down)
    names = list(grads)

    big_rows = {"even_w_in": 256, "ffn_w_gate": 256, "ffn_w_up": 256, "ffn_w_down": 352}
    delta, new_m, new_v = {}, {}, {}

    def as2d(a):
        return a.reshape(-1, a.shape[-1])

    small_names = [k for k in names if weights[k].size <= 70000]
    outs = _adamw([as2d(weights[k]) for k in small_names], [as2d(grads[k]) for k in small_names],
                  [as2d(m_in[k]) for k in small_names], [as2d(v_in[k]) for k in small_names], "adamw_small")
    for i, k in enumerate(small_names):
        delta[k], new_m[k], new_v[k] = (o[i].reshape(weights[k].shape) for o in outs)
    for k in names:
        if k in small_names:
            continue
        outs = _adamw([as2d(weights[k])], [as2d(grads[k])], [as2d(m_in[k])], [as2d(v_in[k])], "adamw_" + k,
                      row_block=big_rows.get(k))
        delta[k], new_m[k], new_v[k] = (o[0].reshape(weights[k].shape) for o in outs)

    return (loss.reshape(()), grad_x, *[grads[k] for k in names], *[delta[k] for k in names],
            *[new_m[k] for k in names], *[new_v[k] for k in names])
```

```python
import functools

import numpy as np
import jax
import jax.numpy as jnp
from jax import lax
from jax.experimental import pallas as pl
from jax.experimental.pallas import tpu as pltpu

F32 = jnp.float32
BF16 = jnp.bfloat16
MESH = pl.DeviceIdType.MESH

D = 1024
EPS = 1e-6
NEG_INF = -1e30
SG_HEADS, SG_HD, SG_W, SG_CHUNK = 4, 128, 512, 128
SC_W = 512
EVEN_IN = 2560
POOL_W = 256
POOL_GD = 64
Q_LORA, KV_LORA, QK_ROPE, QK_NOPE, V_DIM = 384, 256, 64, 128, 128
QK_DIM = QK_NOPE + QK_ROPE
HEADS = 6
HP = 256
ODD_IN = 960
D_FF = 2816
ROPE_THETA = 10000.0
ATT_SCALE = QK_DIM ** -0.5
LR, B1, B2, ADAM_EPS, WD, STEP = 0.001, 0.9, 0.999, 1e-08, 0.01, 10

N_DEV = 8
TB = 512
TB_FFN_BWD = 256
HALO = 16
VMEM_LIMIT = 56 * 1024 * 1024

N_EIN, N_FF, N_SQ = 320, 352, 128
OFF_EIN, OFF_EOUT, R_MIX0 = 0, 384, 512
OFF_GATE, OFF_UP, OFF_DOWN, R_FFN = 0, 352, 704, 1056
OFF_OIN, OFF_OOUT, OFF_QB, OFF_KVB, R_MIX1 = 0, 128, 256, 320, 384
N_QB, N_QB_USED, N_KVB = 64, 54, 48

INV_SQRT2 = 0.7071067811865476
INV_SQRT_2PI = 0.3989422804014327


def _dot(a, b, ca, cb):
    return lax.dot_general(a, b, (((ca,), (cb,)), ((), ())), preferred_element_type=F32)


def _cparams(n_axes=1):
    return pltpu.CompilerParams(dimension_semantics=("arbitrary",) * n_axes, vmem_limit_bytes=VMEM_LIMIT)


def _wspec(n, off, arity=1):
    assert off % n == 0
    idx = off // n
    if arity == 1:
        return pl.BlockSpec((N_DEV, n, D), lambda i: (0, idx, 0), pipeline_mode=pl.Buffered(1))
    return pl.BlockSpec((N_DEV, n, D), lambda i, j: (0, idx, 0), pipeline_mode=pl.Buffered(1))


def _const_spec(shape):
    zeros = (0,) * len(shape)
    return pl.BlockSpec(shape, lambda *_: zeros)


class _Comm:
    def __init__(self, ins, out_shapes, sems, start, wait):
        self.ins, self.out_shapes, self.sems, self.start, self.wait = ins, out_shapes, sems, start, wait


def _call(body, name, grid, in_specs, out_specs, out_shape, args, scratch_shapes=(), comm=None):
    n_axes = len(grid)
    if comm is None:
        res = pl.pallas_call(
            body, name=name, grid=grid, in_specs=list(in_specs), out_specs=list(out_specs),
            out_shape=list(out_shape), scratch_shapes=list(scratch_shapes), compiler_params=_cparams(n_axes))(*args)
        return list(res), []
    ni, no, ns = len(in_specs), len(out_specs), len(scratch_shapes)
    ci, co = len(comm.ins), len(comm.out_shapes)

    def carrier(*refs):
        ins, cin = refs[:ni], refs[ni:ni + ci]
        outs, cout = refs[ni + ci:ni + ci + no], refs[ni + ci + no:ni + ci + no + co]
        scr, sems = refs[ni + ci + no + co:ni + ci + no + co + ns], refs[ni + ci + no + co + ns:]
        ids = [pl.program_id(a) for a in range(n_axes)]
        first = functools.reduce(jnp.logical_and, [i == 0 for i in ids])
        last = functools.reduce(jnp.logical_and, [i == g - 1 for i, g in zip(ids, grid)])

        @pl.when(first)
        def _():
            comm.start(cin, cout, sems)

        body(*ins, *outs, *scr)

        @pl.when(last)
        def _():
            comm.wait(cin, cout, sems)

    any_spec = pl.BlockSpec(memory_space=pl.ANY)
    res = pl.pallas_call(
        carrier, name=name, grid=grid, in_specs=list(in_specs) + [any_spec] * ci,
        out_specs=list(out_specs) + [any_spec] * co, out_shape=list(out_shape) + list(comm.out_shapes),
        scratch_shapes=list(scratch_shapes) + list(comm.sems), compiler_params=_cparams(n_axes))(*args, *comm.ins)
    return list(res[:no]), list(res[no:])


def _comm_alone(comm, name):
    ci, co = len(comm.ins), len(comm.out_shapes)

    def body(*refs):
        cin, cout, sems = refs[:ci], refs[ci:ci + co], refs[ci + co:]
        comm.start(cin, cout, sems)
        comm.wait(cin, cout, sems)

    any_spec = pl.BlockSpec(memory_space=pl.ANY)
    res = pl.pallas_call(
        body, name=name, out_shape=list(comm.out_shapes), in_specs=[any_spec] * ci, out_specs=[any_spec] * co,
        scratch_shapes=list(comm.sems))(*comm.ins)
    return list(res)


def _rms(x, g, n=None):
    n = x.shape[-1] if n is None else n
    r = lax.rsqrt(jnp.sum(x * x, axis=-1, keepdims=True) / n + EPS)
    return x * r * g, r


def _rms_bwd(x, r, g, dy, n=None):
    n = x.shape[-1] if n is None else n
    xh = x * r
    dxh = dy * g
    dx = r * (dxh - xh * (jnp.sum(dxh * xh, axis=-1, keepdims=True) / n))
    dg = jnp.sum(dy * xh, axis=0, keepdims=True)
    return dx, dg


def _gelu(x):
    return 0.5 * x * (1.0 + lax.erf(x * INV_SQRT2))


def _gelu_grad(x):
    return 0.5 * (1.0 + lax.erf(x * INV_SQRT2)) + x * jnp.exp(-0.5 * x * x) * INV_SQRT_2PI


def _shift_down(a, k):
    rows = lax.broadcasted_iota(jnp.int32, a.shape, 0)
    return jnp.where(rows >= k, pltpu.roll(a, k, 0), 0.0)


def _shift_up(a, k):
    n = a.shape[0]
    rows = lax.broadcasted_iota(jnp.int32, a.shape, 0)
    return jnp.where(rows < n - k, pltpu.roll(a, n - k, 0), 0.0)


def _tril_bf16(w):
    r = lax.broadcasted_iota(jnp.int32, w.shape, 0)
    c = lax.broadcasted_iota(jnp.int32, w.shape, 1)
    return jnp.where(r >= c, w, 0.0).astype(BF16)


def _ln_head(vh, g):
    mu = jnp.mean(vh, axis=-1, keepdims=True)
    xc = vh - mu
    rr = lax.rsqrt(jnp.mean(xc * xc, axis=-1, keepdims=True) + EPS)
    xh = xc * rr
    return xh * g, xh, rr


def _conv_fwd(z, tail, cw_ref):
    ext = jnp.concatenate([tail, z], axis=0)
    zs1 = _shift_down(ext, 1)[HALO:]
    zs2 = _shift_down(ext, 2)[HALO:]
    y = cw_ref[2:3, :] * z + cw_ref[1:2, :] * zs1 + cw_ref[0:1, :] * zs2
    return y, zs1, zs2


def _pool_cnt(shape, blk_in_seq):
    rows = lax.broadcasted_iota(jnp.int32, shape, 0)
    grp = lax.broadcasted_iota(jnp.int32, shape, 1) // POOL_GD
    win = jnp.where(grp == 0, 2, jnp.where(grp == 1, 4, jnp.where(grp == 2, 8, 16)))
    tpos = blk_in_seq * shape[0] + rows + 1
    return jnp.minimum(tpos, win).astype(F32), grp


def _pool_select(grp, s2, s4, s8, s16):
    return jnp.where(grp == 0, s2, jnp.where(grp == 1, s4, jnp.where(grp == 2, s8, s16)))


def _pool_fwd(z, tail, blk_in_seq):
    ext = jnp.concatenate([tail, z], axis=0)
    s2 = ext + _shift_down(ext, 1)
    s4 = s2 + _shift_down(s2, 2)
    s8 = s4 + _shift_down(s4, 4)
    s16 = s8 + _shift_down(s8, 8)
    cnt, grp = _pool_cnt(z.shape, blk_in_seq)
    sums = _pool_select(grp, s2[HALO:], s4[HALO:], s8[HALO:], s16[HALO:])
    return sums / cnt - z, cnt, grp


def _pool_bwd(dpooled, dpm, head, grp):
    n = dpm.shape[0]
    ext = jnp.concatenate([dpm, head], axis=0)
    u2 = ext + _shift_up(ext, 1)
    u4 = u2 + _shift_up(u2, 2)
    u8 = u4 + _shift_up(u4, 4)
    u16 = u8 + _shift_up(u8, 8)
    return _pool_select(grp, u2[:n], u4[:n], u8[:n], u16[:n]) - dpooled


def _rope(y1, c, s1, s2):
    return y1 * c + pltpu.roll(y1, 96, 1) * s1 + pltpu.roll(y1, 32, 1) * s2


def _rope_bwd(d1, c, s1, s2):
    return d1 * c + pltpu.roll(d1 * s1, 32, 1) + pltpu.roll(d1 * s2, 96, 1)


def _qk_prep(xh_in, g, c, s1, s2):
    y, r = _rms(xh_in, g, QK_DIM)
    out = jnp.concatenate([y[:, :128], _rope(y[:, 128:], c, s1, s2)], axis=1)
    return out, r


def _qk_prep_bwd(dout, x_in, r, g, c, s1, s2):
    dy = jnp.concatenate([dout[:, :128], _rope_bwd(dout[:, 128:], c, s1, s2)], axis=1)
    return _rms_bwd(x_in, r, g, dy, QK_DIM)


def _place():
    return lax.axis_index("x"), lax.axis_index("y"), lax.axis_index("c")


def _gather_comm(arrs):
    n = len(arrs)

    def plan(ins, outs, sems):
        send_sems, recv_sems, local_sems = sems
        x, y, c = _place()
        me, sibling = (x, y, c), (x, y, 1 - c)
        chips = [(1 - x, y), (x, 1 - y), (1 - x, 1 - y)]

        def slot(a, px, py, pc):
            return outs[a].at[4 * px + 2 * py + pc]

        def copy(a, k, block, to, src=None):
            return pltpu.make_async_remote_copy(
                src_ref=slot(a, *block) if src is None else src, dst_ref=slot(a, *block),
                send_sem=send_sems.at[a, k], recv_sem=recv_sems.at[a, k], device_id=to, device_id_type=MESH)

        mine = [pltpu.make_async_copy(ins[a], slot(a, *me), local_sems.at[a]) for a in range(n)]
        first = []
        for a in range(n):
            first.append(copy(a, 0, me, sibling, src=ins[a]))
            first += [copy(a, 1 + j, me, (*chip, c), src=ins[a]) for j, chip in enumerate(chips)]
        return c, me, sibling, chips, copy, mine, first

    def start(ins, outs, sems):
        _, _, _, _, _, mine, first = plan(ins, outs, sems)
        for cp in mine + first:
            cp.start()

    def wait(ins, outs, sems):
        c, me, sibling, chips, copy, mine, first = plan(ins, outs, sems)
        passed = []
        for j, chip in enumerate(chips):
            for a in range(n):
                copy(a, 1 + j, (*chip, c), me).wait_recv()
                fwd = copy(a, 4 + j, (*chip, c), sibling)
                fwd.start()
                passed.append(fwd)
        for a in range(n):
            copy(a, 0, sibling, me).wait_recv()
            for j, chip in enumerate(chips):
                copy(a, 4 + j, (*chip, 1 - c), me).wait_recv()
        for cp in first + passed:
            cp.wait_send()
        for cp in mine:
            cp.wait()

    return _Comm(
        list(arrs), [jax.ShapeDtypeStruct((N_DEV,) + a.shape, a.dtype) for a in arrs],
        [pltpu.SemaphoreType.DMA((n, 7)), pltpu.SemaphoreType.DMA((n, 7)), pltpu.SemaphoreType.DMA((n,))],
        start, wait)


def _all_reduce_small(v):
    rows = v.shape[0]

    def body(v_ref, sum_ref, g_ref, send_sems, recv_sems):
        x, y, c = _place()
        me, sibling = (x, y, c), (x, y, 1 - c)
        chips = [(1 - x, y), (x, 1 - y), (1 - x, 1 - y)]

        def slot(px, py, pc):
            return g_ref.at[4 * px + 2 * py + pc]

        def copy(k, block, to, src=None):
            return pltpu.make_async_remote_copy(
                src_ref=slot(*block) if src is None else src, dst_ref=slot(*block),
                send_sem=send_sems.at[k], recv_sem=recv_sems.at[k], device_id=to, device_id_type=MESH)

        g_ref[4 * x + 2 * y + c] = v_ref[...]
        first = [copy(0, me, sibling, src=v_ref)]
        first += [copy(1 + j, me, (*chip, c), src=v_ref) for j, chip in enumerate(chips)]
        for cp in first:
            cp.start()
        passed = [copy(4 + j, (*chip, c), sibling) for j, chip in enumerate(chips)]
        for j, chip in enumerate(chips):
            copy(1 + j, (*chip, c), me).wait_recv()
            passed[j].start()
        copy(0, sibling, me).wait_recv()
        for j, chip in enumerate(chips):
            copy(4 + j, (*chip, 1 - c), me).wait_recv()
        for cp in first + passed:
            cp.wait_send()
        total = g_ref[0]
        for d in range(1, N_DEV):
            total = total + g_ref[d]
        sum_ref[...] = total

    vm = pl.BlockSpec(memory_space=pltpu.VMEM)
    out = pl.pallas_call(
        body, name="allreduce_small",
        out_shape=[jax.ShapeDtypeStruct((rows, 128), F32), jax.ShapeDtypeStruct((N_DEV, rows, 128), F32)],
        in_specs=[vm], out_specs=[vm, vm],
        scratch_shapes=[pltpu.SemaphoreType.DMA((7,)), pltpu.SemaphoreType.DMA((7,))],
        compiler_params=pltpu.CompilerParams(vmem_limit_bytes=VMEM_LIMIT),
    )(v)
    return out[0]


def _sum_rows(rows):
    return rows if rows <= 512 else rows // 2


def _rs_pair_exchange(gp, name):
    _, rows, cols = gp.shape

    def body(gp_ref, out_ref, send_sems, recv_sems):
        x, y, c = _place()
        cps = []
        for j in range(4):
            cp = pltpu.make_async_remote_copy(
                src_ref=gp_ref.at[2 * j + (1 - c)], dst_ref=out_ref.at[j],
                send_sem=send_sems.at[j], recv_sem=recv_sems.at[j], device_id=(x, y, 1 - c), device_id_type=MESH)
            cp.start()
            cps.append(cp)
        for cp in cps:
            cp.wait()

    any_spec = pl.BlockSpec(memory_space=pl.ANY)
    return pl.pallas_call(
        body, name=name, out_shape=jax.ShapeDtypeStruct((4, rows, cols), gp.dtype),
        in_specs=[any_spec], out_specs=any_spec,
        scratch_shapes=[pltpu.SemaphoreType.DMA((4,)), pltpu.SemaphoreType.DMA((4,))],
    )(gp)


def _rs_pair_sum(gp, got, c_arr, name):
    _, rows, cols = gp.shape
    rb = _sum_rows(rows)
    gp4 = gp.reshape(4, 2, rows, cols)

    def body(c_ref, a_ref, b_ref, o_ref):
        o_ref[0] = (a_ref[0, 0].astype(F32) + b_ref[0].astype(F32)).astype(o_ref.dtype)

    return pl.pallas_call(
        body, name=name, out_shape=jax.ShapeDtypeStruct((4, rows, cols), gp.dtype),
        grid_spec=pltpu.PrefetchScalarGridSpec(
            num_scalar_prefetch=1, grid=(4, rows // rb),
            in_specs=[pl.BlockSpec((1, 1, rb, cols), lambda j, r, cr: (j, cr[0], r, 0)),
                      pl.BlockSpec((1, rb, cols), lambda j, r, cr: (j, r, 0))],
            out_specs=pl.BlockSpec((1, rb, cols), lambda j, r, cr: (j, r, 0))),
        compiler_params=_cparams(2),
    )(c_arr, gp4, got)


def _chip_exchange_comm(pb):
    _, rows, cols = pb.shape

    def copies(ins, outs, sems):
        send_sems, recv_sems = sems
        x, y, c = _place()
        chips = [(1 - x, y), (x, 1 - y), (1 - x, 1 - y)]
        return [pltpu.make_async_remote_copy(
            src_ref=ins[0].at[2 * px + py], dst_ref=outs[0].at[k], send_sem=send_sems.at[k],
            recv_sem=recv_sems.at[k], device_id=(px, py, c), device_id_type=MESH)
            for k, (px, py) in enumerate(chips)]

    def start(ins, outs, sems):
        for cp in copies(ins, outs, sems):
            cp.start()

    def wait(ins, outs, sems):
        for cp in copies(ins, outs, sems):
            cp.wait()

    return _Comm([pb], [jax.ShapeDtypeStruct((3, rows, cols), pb.dtype)],
                 [pltpu.SemaphoreType.DMA((3,)), pltpu.SemaphoreType.DMA((3,))], start, wait)


def _rs_final_sum(pb, got, chip_arr, name):
    _, rows, cols = pb.shape
    rb = _sum_rows(rows)

    def body(j_ref, a_ref, b_ref, o_ref):
        o_ref[...] = ((a_ref[0].astype(F32) + b_ref[0].astype(F32)) + b_ref[1].astype(F32)) + b_ref[2].astype(F32)

    return pl.pallas_call(
        body, name=name, out_shape=jax.ShapeDtypeStruct((rows, cols), F32),
        grid_spec=pltpu.PrefetchScalarGridSpec(
            num_scalar_prefetch=1, grid=(rows // rb,),
            in_specs=[pl.BlockSpec((1, rb, cols), lambda r, jr: (jr[0], r, 0)),
                      pl.BlockSpec((3, rb, cols), lambda r, jr: (0, r, 0))],
            out_specs=pl.BlockSpec((rb, cols), lambda r, jr: (r, 0))),
        compiler_params=_cparams(1),
    )(chip_arr, pb, got)


def _rope_tables(pos_col, inv_freq):
    t = pos_col.shape[0]

    def body(p_ref, f_ref, c_ref, s1_ref, s2_ref):
        ang = p_ref[...].astype(F32) * f_ref[...]
        lane = lax.broadcasted_iota(jnp.int32, ang.shape, 1)
        c_ref[...] = jnp.where(lane < QK_ROPE, jnp.cos(ang), 0.0)
        s = jnp.sin(ang)
        s1_ref[...] = jnp.where(lane < 32, -s, 0.0)
        s2_ref[...] = jnp.where((lane >= 32) & (lane < QK_ROPE), s, 0.0)

    spec = pl.BlockSpec((TB, 128), lambda i: (i, 0))
    return pl.pallas_call(
        body, name="rope_tables", out_shape=[jax.ShapeDtypeStruct((t, 128), F32)] * 3, grid=(t // TB,),
        in_specs=[pl.BlockSpec((TB, 1), lambda i: (i, 0)), _const_spec((1, 128))], out_specs=[spec] * 3,
        compiler_params=_cparams(1),
    )(pos_col, inv_freq)


def _sgu_conv_fwd(proj, tail, lng_ref, ws_ref, bst_ref, cw_ref):
    gu = _gelu(proj[:, 0:SG_W])
    gv = _gelu(proj[:, SG_W:2 * SG_W])
    bg = proj[:, 1024:1536]
    z = proj[:, 1536:2048] * proj[:, 2048:2560]
    heads = []
    for h in range(SG_HEADS):
        sl = slice(h * SG_HD, (h + 1) * SG_HD)
        vn, _, _ = _ln_head(gv[:, sl], lng_ref[:, sl])
        vnb = vn.astype(BF16)
        wm = _tril_bf16(ws_ref[h])
        bcol = bst_ref[:, h:h + 1]
        mixed = jnp.concatenate(
            [_dot(wm, vnb[k * SG_CHUNK:(k + 1) * SG_CHUNK], 1, 0) + bcol for k in range(TB // SG_CHUNK)], axis=0)
        heads.append(gu[:, sl] * mixed)
    a_out = jnp.concatenate(heads, axis=1)
    y, _, _ = _conv_fwd(z, tail, cw_ref)
    return a_out, bg * y, z


def _even_fwd(x, wg, gamma, lng, ws, bst, cw, seq, comm=None):
    t = x.shape[0]
    nbs = seq // TB

    def body(x_ref, gam_ref, win_ref, wout_ref, lng_ref, ws_ref, bst_ref, cw_ref, x1_ref, proj_ref, tail_ref):
        i = pl.program_id(0)
        xv = x_ref[...]
        h, _ = _rms(xv, gam_ref[...])
        proj = _dot(h.astype(BF16), win_ref[...].reshape(EVEN_IN, D), 1, 1)
        proj_ref[...] = proj.astype(BF16)
        tail = jnp.where(i % nbs == 0, 0.0, tail_ref[...])
        a_out, b_out, z = _sgu_conv_fwd(proj, tail, lng_ref, ws_ref, bst_ref, cw_ref)
        tail_ref[...] = z[TB - HALO:, :]
        x1_ref[...] = (xv + _dot(a_out.astype(BF16), wout_ref[0:4].reshape(512, D), 1, 0)
                       + _dot(b_out.astype(BF16), wout_ref[4:8].reshape(512, D), 1, 0))

    row = pl.BlockSpec((TB, D), lambda i: (i, 0))
    return _call(
        body, "even_fwd", (t // TB,),
        [row, _const_spec((1, D)), _wspec(N_EIN, OFF_EIN), _wspec(N_SQ, OFF_EOUT), _const_spec((1, SG_W)),
         _const_spec((SG_HEADS, 128, 128)), _const_spec((128, 128)), _const_spec((8, SC_W))],
        [row, pl.BlockSpec((TB, EVEN_IN), lambda i: (i, 0))],
        [jax.ShapeDtypeStruct((t, D), F32), jax.ShapeDtypeStruct((t, EVEN_IN), BF16)],
        (x, gamma, wg, wg, lng, ws, bst, cw), [pltpu.VMEM((HALO, SC_W), F32)], comm)


def _even_bwd(x, proj, dx1, wg, gamma, lng, ws, bst, cw, seq, comm=None):
    t = x.shape[0]
    nb, nbs = t // TB, seq // TB

    def body(x_ref, proj_ref, ptail_ref, dx1_ref, gam_ref, win_ref, wout_ref, lng_ref, ws_ref, bst_ref, cw_ref,
             dx0_ref, dproj_ref, mix_ref, h_ref, dgam_ref, dws_ref, dbc_ref, dlng_ref, dcw_ref, head_ref):
        i = pl.program_id(0)
        blk = nb - 1 - i

        @pl.when(i == 0)
        def _():
            dgam_ref[...] = jnp.zeros_like(dgam_ref)
            dws_ref[...] = jnp.zeros_like(dws_ref)
            dbc_ref[...] = jnp.zeros_like(dbc_ref)
            dlng_ref[...] = jnp.zeros_like(dlng_ref)
            dcw_ref[...] = jnp.zeros_like(dcw_ref)

        xv = x_ref[...]
        gam = gam_ref[...]
        h, r = _rms(xv, gam)
        h_ref[...] = h.astype(BF16)
        dx1 = dx1_ref[...]
        dmix = _dot(dx1.astype(BF16), wout_ref[...].reshape(D, D), 1, 1)
        da, db = dmix[:, :SG_W], dmix[:, SG_W:]
        proj = proj_ref[...].astype(F32)
        u, v = proj[:, 0:SG_W], proj[:, SG_W:2 * SG_W]
        bg, cg, hv = proj[:, 1024:1536], proj[:, 1536:2048], proj[:, 2048:2560]
        gu, gv = _gelu(u), _gelu(v)

        a_heads, dgv_heads = [], []
        for hd in range(SG_HEADS):
            sl = slice(hd * SG_HD, (hd + 1) * SG_HD)
            g_h = lng_ref[:, sl]
            vn, xh, rr = _ln_head(gv[:, sl], g_h)
            vnb = vn.astype(BF16)
            wm = _tril_bf16(ws_ref[hd])
            bcol = bst_ref[:, hd:hd + 1]
            mixed_c, dvn_c = [], []
            dw_acc = jnp.zeros((128, 128), F32)
            db_acc = jnp.zeros((128, 1), F32)
            for k in range(TB // SG_CHUNK):
                rs = slice(k * SG_CHUNK, (k + 1) * SG_CHUNK)
                mixed = _dot(wm, vnb[rs], 1, 0) + bcol
                dmixed = da[rs, sl] * gu[rs, sl]
                dmb = dmixed.astype(BF16)
                dvn_c.append(_dot(wm, dmb, 0, 0))
                dw_acc = dw_acc + _dot(dmb, vnb[rs], 1, 1)
                db_acc = db_acc + jnp.sum(dmixed, axis=1, keepdims=True)
                mixed_c.append(mixed)
            mixed_h = jnp.concatenate(mixed_c, axis=0)
            dvn = jnp.concatenate(dvn_c, axis=0)
            r_i = lax.broadcasted_iota(jnp.int32, (128, 128), 0)
            c_i = lax.broadcasted_iota(jnp.int32, (128, 128), 1)
            dws_ref[hd] += jnp.where(r_i >= c_i, dw_acc, 0.0)
            dbc_ref[:, hd:hd + 1] += db_acc
            dlng_ref[:, sl] += jnp.sum(dvn * xh, axis=0, keepdims=True)
            dxh = dvn * g_h
            dgv = rr * (dxh - jnp.mean(dxh, axis=-1, keepdims=True)
                        - xh * jnp.mean(dxh * xh, axis=-1, keepdims=True))
            a_heads.append(gu[:, sl] * mixed_h)
            dproj_ref[:, sl] = (da[:, sl] * mixed_h * _gelu_grad(u[:, sl])).astype(BF16)
            dgv_heads.append(dgv * _gelu_grad(v[:, sl]))
        dproj_ref[:, SG_W:2 * SG_W] = jnp.concatenate(dgv_heads, axis=1).astype(BF16)
        mix_ref[:, :SG_W] = jnp.concatenate(a_heads, axis=1).astype(BF16)

        z = cg * hv
        pt = ptail_ref[...].astype(F32)
        tail = jnp.where(blk % nbs == 0, 0.0, pt[:, 1536:2048] * pt[:, 2048:2560])
        y, zs1, zs2 = _conv_fwd(z, tail, cw_ref)
        mix_ref[:, SG_W:] = (bg * y).astype(BF16)
        dy = db * bg
        head = jnp.where(blk % nbs == nbs - 1, 0.0, head_ref[...])
        ext = jnp.concatenate([dy, head], axis=0)
        dz = (cw_ref[2:3, :] * dy + cw_ref[1:2, :] * _shift_up(ext, 1)[:TB]
              + cw_ref[0:1, :] * _shift_up(ext, 2)[:TB])
        head_ref[...] = dy[:HALO, :]
        dcw_ref[2:3, :] += jnp.sum(dy * z, axis=0, keepdims=True)
        dcw_ref[1:2, :] += jnp.sum(dy * zs1, axis=0, keepdims=True)
        dcw_ref[0:1, :] += jnp.sum(dy * zs2, axis=0, keepdims=True)
        dproj_ref[:, 1024:1536] = (db * y).astype(BF16)
        dproj_ref[:, 1536:2048] = (dz * hv).astype(BF16)
        dproj_ref[:, 2048:2560] = (dz * cg).astype(BF16)

        dh = _dot(dproj_ref[...], win_ref[...].reshape(EVEN_IN, D), 1, 0)
        dxn, dgam = _rms_bwd(xv, r, gam, dh)
        dgam_ref[...] += dgam
        dx0_ref[...] = dx1 + dxn

    def rev(w):
        return pl.BlockSpec((TB, w), lambda i: (nb - 1 - i, 0))

    ptail = pl.BlockSpec((HALO, EVEN_IN), lambda i: (jnp.maximum((nb - 1 - i) * (TB // HALO) - 1, 0), 0))
    return _call(
        body, "even_bwd", (nb,),
        [rev(D), rev(EVEN_IN), ptail, rev(D), _const_spec((1, D)), _wspec(N_EIN, OFF_EIN),
         _wspec(N_SQ, OFF_EOUT), _const_spec((1, SG_W)), _const_spec((SG_HEADS, 128, 128)),
         _const_spec((128, 128)), _const_spec((8, SC_W))],
        [rev(D), rev(EVEN_IN), rev(D), rev(D), _const_spec((1, D)), _const_spec((SG_HEADS, 128, 128)),
         _const_spec((128, 128)), _const_spec((1, SG_W)), _const_spec((8, SC_W))],
        [jax.ShapeDtypeStruct((t, D), F32), jax.ShapeDtypeStruct((t, EVEN_IN), BF16),
         jax.ShapeDtypeStruct((t, D), BF16), jax.ShapeDtypeStruct((t, D), BF16),
         jax.ShapeDtypeStruct((1, D), F32), jax.ShapeDtypeStruct((SG_HEADS, 128, 128), F32),
         jax.ShapeDtypeStruct((128, 128), F32), jax.ShapeDtypeStruct((1, SG_W), F32),
         jax.ShapeDtypeStruct((8, SC_W), F32)],
        (x, proj, proj, dx1, gamma, wg, wg, lng, ws, bst, cw), [pltpu.VMEM((HALO, SC_W), F32)], comm)


def _ffn_fwd(x, wg, gamma, name, comm=None):
    t = x.shape[0]

    def body(x_ref, gam_ref, wg_ref, wu_ref, wd_ref, y_ref, g_ref, u_ref):
        xv = x_ref[...]
        h, _ = _rms(xv, gam_ref[...])
        hb = h.astype(BF16)
        g = _dot(hb, wg_ref[...].reshape(D_FF, D), 1, 1)
        u = _dot(hb, wu_ref[...].reshape(D_FF, D), 1, 1)
        g_ref[...] = g.astype(BF16)
        u_ref[...] = u.astype(BF16)
        act = g * jax.nn.sigmoid(g) * u
        y_ref[...] = xv + _dot(act.astype(BF16), wd_ref[...].reshape(D_FF, D), 1, 0)

    row = pl.BlockSpec((TB, D), lambda i: (i, 0))
    wide = pl.BlockSpec((TB, D_FF), lambda i: (i, 0))
    return _call(
        body, name, (t // TB,),
        [row, _const_spec((1, D)), _wspec(N_FF, OFF_GATE), _wspec(N_FF, OFF_UP), _wspec(N_FF, OFF_DOWN)],
        [row, wide, wide],
        [jax.ShapeDtypeStruct((t, D), F32), jax.ShapeDtypeStruct((t, D_FF), BF16),
         jax.ShapeDtypeStruct((t, D_FF), BF16)],
        (x, gamma, wg, wg, wg), (), comm)


def _ffn_bwd(x, g, u, dy, wg, gamma, name, comm=None):
    t = x.shape[0]

    def body(x_ref, g_ref, u_ref, dy_ref, gam_ref, wg_ref, wu_ref, wd_ref,
             dx_ref, act_ref, dg_ref, du_ref, h_ref, dgam_ref):
        @pl.when(pl.program_id(0) == 0)
        def _():
            dgam_ref[...] = jnp.zeros_like(dgam_ref)

        xv = x_ref[...]
        gam = gam_ref[...]
        h, r = _rms(xv, gam)
        h_ref[...] = h.astype(BF16)
        dyv = dy_ref[...]
        dact = _dot(dyv.astype(BF16), wd_ref[...].reshape(D_FF, D), 1, 1)
        gv = g_ref[...].astype(F32)
        uv = u_ref[...].astype(F32)
        sg = jax.nn.sigmoid(gv)
        silu = gv * sg
        act_ref[...] = (silu * uv).astype(BF16)
        dgb = (dact * uv * (sg * (1.0 + gv * (1.0 - sg)))).astype(BF16)
        dub = (dact * silu).astype(BF16)
        dg_ref[...] = dgb
        du_ref[...] = dub
        dh = _dot(dgb, wg_ref[...].reshape(D_FF, D), 1, 0) + _dot(dub, wu_ref[...].reshape(D_FF, D), 1, 0)
        dxn, dgam = _rms_bwd(xv, r, gam, dh)
        dgam_ref[...] += dgam
        dx_ref[...] = dyv + dxn

    row = pl.BlockSpec((TB_FFN_BWD, D), lambda i: (i, 0))
    wide = pl.BlockSpec((TB_FFN_BWD, D_FF), lambda i: (i, 0))
    return _call(
        body, name, (t // TB_FFN_BWD,),
        [row, wide, wide, row, _const_spec((1, D)), _wspec(N_FF, OFF_GATE), _wspec(N_FF, OFF_UP),
         _wspec(N_FF, OFF_DOWN)],
        [row, wide, wide, wide, row, _const_spec((1, D))],
        [jax.ShapeDtypeStruct((t, D), F32), jax.ShapeDtypeStruct((t, D_FF), BF16),
         jax.ShapeDtypeStruct((t, D_FF), BF16), jax.ShapeDtypeStruct((t, D_FF), BF16),
         jax.ShapeDtypeStruct((t, D), BF16), jax.ShapeDtypeStruct((1, D), F32)],
        (x, g, u, dy, gamma, wg, wg, wg), (), comm)


def _odd_pre_fwd(x, wg, gamma, qbt, kvbt, qa_g, kva_g, pw_bd, pscale, seq):
    t = x.shape[0]
    nbs = seq // TB

    def body(x_ref, gam_ref, win_ref, qb_ref, kvb_ref, qa_ref, kva_ref, pw_ref, ps_ref,
             proj_ref, q_ref, kv_ref, kr_ref, c_ref, tail_ref):
        i = pl.program_id(0)
        h, _ = _rms(x_ref[...], gam_ref[...])
        proj = _dot(h.astype(BF16), win_ref[...].reshape(D, D), 1, 0)
        proj_ref[...] = proj.astype(BF16)
        zp, ql, kvl = proj[:, :POOL_W], proj[:, 256:640], proj[:, 640:896]
        kr_ref[...] = proj[:, 896:1024]
        qn, _ = _rms(ql, qa_ref[...])
        q_ref[...] = _dot(qn.astype(BF16), qb_ref[...], 1, 1).astype(BF16)
        kvn, _ = _rms(kvl, kva_ref[...])
        kv_ref[...] = _dot(kvn.astype(BF16), kvb_ref[...], 1, 1).astype(BF16)
        tail = jnp.where(i % nbs == 0, 0.0, tail_ref[...])
        pooled, _, _ = _pool_fwd(zp, tail, i % nbs)
        tail_ref[...] = zp[TB - HALO:, :]
        c_ref[...] = (_dot(pooled.astype(BF16), pw_ref[...], 1, 0) * ps_ref[...]).astype(BF16)

    def row(w):
        return pl.BlockSpec((TB, w), lambda i: (i, 0))

    return pl.pallas_call(
        body, name="odd_pre_fwd",
        out_shape=[jax.ShapeDtypeStruct((t, D), BF16), jax.ShapeDtypeStruct((t, HEADS * HP), BF16),
                   jax.ShapeDtypeStruct((t, HEADS * HP), BF16), jax.ShapeDtypeStruct((t, 128), F32),
                   jax.ShapeDtypeStruct((t, POOL_W), BF16)],
        grid=(t // TB,),
        in_specs=[row(D), _const_spec((1, D)), _wspec(N_SQ, OFF_OIN), _const_spec((HEADS * HP, Q_LORA)),
                  _const_spec((HEADS * HP, KV_LORA)), _const_spec((1, Q_LORA)), _const_spec((1, KV_LORA)),
                  _const_spec((POOL_W, POOL_W)), _const_spec((1, POOL_W))],
        out_specs=[row(D), row(HEADS * HP), row(HEADS * HP), row(128), row(POOL_W)],
        scratch_shapes=[pltpu.VMEM((HALO, POOL_W), F32)],
        compiler_params=_cparams(1),
    )(x, gamma, wg, qbt, kvbt, qa_g, kva_g, pw_bd, pscale)


def _odd_pre_bwd(x, proj, dx3, dmix, dq, dkv, dkr, wg, gamma, qbt, kvbt, qa_g, kva_g, pw_bd, pscale, seq):
    t = x.shape[0]
    nb, nbs = t // TB, seq // TB

    def body(x_ref, proj_ref, ptail_ref, dx3_ref, dco_ref, dq_ref, dkv_ref, dkr_ref, gam_ref, win_ref, qb_ref,
             kvb_ref, qa_ref, kva_ref, pw_ref, ps_ref,
             dx2_ref, dproj_ref, h_ref, qn_ref, kvn_ref, dgam_ref, dqa_ref, dkva_ref, dpw_ref, dps_ref, head_ref):
        i = pl.program_id(0)
        blk = nb - 1 - i

        @pl.when(i == 0)
        def _():
            dgam_ref[...] = jnp.zeros_like(dgam_ref)
            dqa_ref[...] = jnp.zeros_like(dqa_ref)
            dkva_ref[...] = jnp.zeros_like(dkva_ref)
            dpw_ref[...] = jnp.zeros_like(dpw_ref)
            dps_ref[...] = jnp.zeros_like(dps_ref)

        xv = x_ref[...]
        gam = gam_ref[...]
        h, r = _rms(xv, gam)
        h_ref[...] = h.astype(BF16)
        proj = proj_ref[...].astype(F32)
        zp, ql, kvl = proj[:, :POOL_W], proj[:, 256:640], proj[:, 640:896]

        qa = qa_ref[...]
        qn, rq = _rms(ql, qa)
        qn_ref[...] = qn.astype(BF16)
        dql, dqa = _rms_bwd(ql, rq, qa, _dot(dq_ref[...], qb_ref[...], 1, 0))
        dqa_ref[...] += dqa
        kva = kva_ref[...]
        kvn, rkv = _rms(kvl, kva)
        kvn_ref[...] = kvn.astype(BF16)
        dkvl, dkva = _rms_bwd(kvl, rkv, kva, _dot(dkv_ref[...], kvb_ref[...], 1, 0))
        dkva_ref[...] += dkva

        pt = ptail_ref[...].astype(F32)
        tail = jnp.where(blk % nbs == 0, 0.0, pt[:, :POOL_W])
        pooled, cnt, grp = _pool_fwd(zp, tail, blk % nbs)
        pb = pooled.astype(BF16)
        pw = pw_ref[...]
        dco = dco_ref[...].astype(F32)
        dps_ref[...] += jnp.sum(dco * _dot(pb, pw, 1, 0), axis=0, keepdims=True)
        dpo = (dco * ps_ref[...]).astype(BF16)
        dpw_ref[...] += _dot(pb, dpo, 0, 0)
        dpooled = _dot(dpo, pw, 1, 1)
        dpm = dpooled / cnt
        head = jnp.where(blk % nbs == nbs - 1, 0.0, head_ref[...])
        dz = _pool_bwd(dpooled, dpm, head, grp)
        head_ref[...] = dpm[:HALO, :]

        dproj_ref[:, :POOL_W] = dz.astype(BF16)
        dproj_ref[:, 256:640] = dql.astype(BF16)
        dproj_ref[:, 640:896] = dkvl.astype(BF16)
        dproj_ref[:, 896:1024] = dkr_ref[...].astype(BF16)
        dh = _dot(dproj_ref[...], win_ref[...].reshape(D, D), 1, 1)
        dxn, dgam = _rms_bwd(xv, r, gam, dh)
        dgam_ref[...] += dgam
        dx2_ref[...] = dx3_ref[...] + dxn

    def rev(w):
        return pl.BlockSpec((TB, w), lambda i: (nb - 1 - i, 0))

    ptail = pl.BlockSpec((HALO, D), lambda i: (jnp.maximum((nb - 1 - i) * (TB // HALO) - 1, 0), 0))
    return pl.pallas_call(
        body, name="odd_pre_bwd",
        out_shape=[jax.ShapeDtypeStruct((t, D), F32), jax.ShapeDtypeStruct((t, D), BF16),
                   jax.ShapeDtypeStruct((t, D), BF16), jax.ShapeDtypeStruct((t, Q_LORA), BF16),
                   jax.ShapeDtypeStruct((t, KV_LORA), BF16), jax.ShapeDtypeStruct((1, D), F32),
                   jax.ShapeDtypeStruct((1, Q_LORA), F32), jax.ShapeDtypeStruct((1, KV_LORA), F32),
                   jax.ShapeDtypeStruct((POOL_W, POOL_W), F32), jax.ShapeDtypeStruct((1, POOL_W), F32)],
        grid=(nb,),
        in_specs=[rev(D), rev(D), ptail, rev(D), rev(POOL_W), rev(HEADS * HP), rev(HEADS * HP), rev(128),
                  _const_spec((1, D)), _wspec(N_SQ, OFF_OIN), _const_spec((HEADS * HP, Q_LORA)),
                  _const_spec((HEADS * HP, KV_LORA)), _const_spec((1, Q_LORA)), _const_spec((1, KV_LORA)),
                  _const_spec((POOL_W, POOL_W)), _const_spec((1, POOL_W))],
        out_specs=[rev(D), rev(D), rev(D), rev(Q_LORA), rev(KV_LORA), _const_spec((1, D)), _const_spec((1, Q_LORA)),
                   _const_spec((1, KV_LORA)), _const_spec((POOL_W, POOL_W)), _const_spec((1, POOL_W))],
        scratch_shapes=[pltpu.VMEM((HALO, POOL_W), F32)],
        compiler_params=_cparams(1),
    )(x, proj, proj, dx3, dmix, dq, dkv, dkr, gamma, wg, qbt, kvbt, qa_g, kva_g, pw_bd, pscale)


def _attn_specs(seq):
    head = pl.BlockSpec((seq, HP), lambda b, h: (b, h))
    shared = pl.BlockSpec((seq, 128), lambda b, h: (b, 0))
    gain = pl.BlockSpec((1, HP), lambda b, h: (0, 0))
    return head, shared, gain


def _causal_probs(qf, kf, q0, q1):
    s = _dot(qf[q0:q1], kf[:q1], 1, 1) * ATT_SCALE
    rows = lax.broadcasted_iota(jnp.int32, s.shape, 0) + q0
    cols = lax.broadcasted_iota(jnp.int32, s.shape, 1)
    s = jnp.where(cols <= rows, s, NEG_INF)
    p = jnp.exp(s - jnp.max(s, axis=-1, keepdims=True))
    return p, jnp.sum(p, axis=-1, keepdims=True)


def _attn_fwd(q, kv, kr, cos, s1, s2, gq, gk, seq, comm=None):
    t = q.shape[0]
    qb = min(512, seq)

    def body(q_ref, kv_ref, kr_ref, c_ref, s1_ref, s2_ref, gq_ref, gk_ref, o_ref):
        c, sa, sb = c_ref[...], s1_ref[...], s2_ref[...]
        qf, _ = _qk_prep(q_ref[...].astype(F32), gq_ref[...], c, sa, sb)
        kin = jnp.concatenate([kv_ref[:, :128].astype(F32), kr_ref[...]], axis=1)
        kf, _ = _qk_prep(kin, gk_ref[...], c, sa, sb)
        qf, kf = qf.astype(BF16), kf.astype(BF16)
        vb = kv_ref[:, 128:]
        for q0 in range(0, seq, qb):
            q1 = q0 + qb
            p, l = _causal_probs(qf, kf, q0, q1)
            o_ref[q0:q1, :] = (_dot(p.astype(BF16), vb[:q1], 1, 0) / l).astype(BF16)

    head, shared, gain = _attn_specs(seq)
    return _call(
        body, "attn_fwd", (t // seq, HEADS),
        [head, head, shared, shared, shared, shared, gain, gain],
        [pl.BlockSpec((seq, V_DIM), lambda b, h: (b, h))],
        [jax.ShapeDtypeStruct((t, HEADS * V_DIM), BF16)],
        (q, kv, kr, cos, s1, s2, gq, gk), (), comm)


def _attn_bwd(q, kv, kr, cos, s1, s2, gq, gk, dmix, seq, comm=None):
    t = q.shape[0]
    qb = min(512, seq)

    def body(q_ref, kv_ref, kr_ref, c_ref, s1_ref, s2_ref, gq_ref, gk_ref, do_ref,
             dq_ref, dkv_ref, dkr_ref, dgq_ref, dgk_ref, dqf_ref, dkf_ref, dv_ref):
        b, hd = pl.program_id(0), pl.program_id(1)

        @pl.when((b == 0) & (hd == 0))
        def _():
            dgq_ref[...] = jnp.zeros_like(dgq_ref)
            dgk_ref[...] = jnp.zeros_like(dgk_ref)

        c, sa, sb = c_ref[...], s1_ref[...], s2_ref[...]
        gq_v, gk_v = gq_ref[...], gk_ref[...]
        qin = q_ref[...].astype(F32)
        kin = jnp.concatenate([kv_ref[:, :128].astype(F32), kr_ref[...]], axis=1)
        qf32, rq = _qk_prep(qin, gq_v, c, sa, sb)
        kf32, rk = _qk_prep(kin, gk_v, c, sa, sb)
        qf, kf = qf32.astype(BF16), kf32.astype(BF16)
        vb = kv_ref[:, 128:]
        dkf_ref[...] = jnp.zeros_like(dkf_ref)
        dv_ref[...] = jnp.zeros_like(dv_ref)
        for q0 in range(0, seq, qb):
            q1 = q0 + qb
            p, l = _causal_probs(qf, kf, q0, q1)
            p = p / l
            pbf = p.astype(BF16)
            do = do_ref[q0:q1, :]
            dv_ref[:q1, :] += _dot(pbf, do, 0, 0)
            dp = _dot(do, vb[:q1], 1, 1)
            ds = (p * (dp - jnp.sum(p * dp, axis=-1, keepdims=True)) * ATT_SCALE).astype(BF16)
            dqf_ref[q0:q1, :] = _dot(ds, kf[:q1], 1, 0)
            dkf_ref[:q1, :] += _dot(ds, qf[q0:q1], 0, 0)
        dqin, dgq = _qk_prep_bwd(dqf_ref[...], qin, rq, gq_v, c, sa, sb)
        dkin, dgk = _qk_prep_bwd(dkf_ref[...], kin, rk, gk_v, c, sa, sb)
        dgq_ref[...] += dgq
        dgk_ref[...] += dgk
        dq_ref[...] = dqin.astype(BF16)
        dkv_ref[:, :128] = dkin[:, :128].astype(BF16)
        dkv_ref[:, 128:] = dv_ref[...].astype(BF16)

        @pl.when(hd == 0)
        def _():
            dkr_ref[...] = dkin[:, 128:]

        @pl.when(hd != 0)
        def _():
            dkr_ref[...] += dkin[:, 128:]

    head, shared, gain = _attn_specs(seq)
    return _call(
        body, "attn_bwd", (t // seq, HEADS),
        [head, head, shared, shared, shared, shared, gain, gain,
         pl.BlockSpec((seq, V_DIM), lambda b, h: (b, 2 + h))],
        [head, head, shared, gain, gain],
        [jax.ShapeDtypeStruct((t, HEADS * HP), BF16), jax.ShapeDtypeStruct((t, HEADS * HP), BF16),
         jax.ShapeDtypeStruct((t, 128), F32), jax.ShapeDtypeStruct((1, HP), F32),
         jax.ShapeDtypeStruct((1, HP), F32)],
        (q, kv, kr, cos, s1, s2, gq, gk, dmix),
        [pltpu.VMEM((seq, HP), F32), pltpu.VMEM((seq, HP), F32), pltpu.VMEM((seq, V_DIM), F32)], comm)


def _odd_post_fwd(x, c_out, d_out, wg):
    t = x.shape[0]

    def body(x_ref, c_ref, d_ref, w_ref, y_ref):
        y_ref[...] = (x_ref[...] + _dot(c_ref[...], w_ref[0:2].reshape(POOL_W, D), 1, 0)
                      + _dot(d_ref[...], w_ref[2:8].reshape(HEADS * V_DIM, D), 1, 0))

    def row(w):
        return pl.BlockSpec((TB, w), lambda i: (i, 0))

    return pl.pallas_call(
        body, name="odd_post_fwd", out_shape=jax.ShapeDtypeStruct((t, D), F32), grid=(t // TB,),
        in_specs=[row(D), row(POOL_W), row(HEADS * V_DIM), _wspec(N_SQ, OFF_OOUT)], out_specs=row(D),
        compiler_params=_cparams(1),
    )(x, c_out, d_out, wg)


def _odd_post_bwd(dx3, wg):
    t = dx3.shape[0]

    def body(d_ref, w_ref, o_ref):
        o_ref[...] = _dot(d_ref[...].astype(BF16), w_ref[...].reshape(D, D), 1, 1).astype(BF16)

    row = pl.BlockSpec((TB, D), lambda i: (i, 0))
    return pl.pallas_call(
        body, name="odd_post_bwd", out_shape=jax.ShapeDtypeStruct((t, D), BF16), grid=(t // TB,),
        in_specs=[row, _wspec(N_SQ, OFF_OOUT)], out_specs=row, compiler_params=_cparams(1),
    )(dx3, wg)


def _loss_grad(y, target):
    t = y.shape[0]

    def body(y_ref, t_ref, dy_ref, loss_ref):
        @pl.when(pl.program_id(0) == 0)
        def _():
            loss_ref[...] = jnp.zeros_like(loss_ref)

        err = y_ref[...] - t_ref[...]
        dy_ref[...] = err * (1.0 / D)
        sq = jnp.sum(jnp.sum(err * err, axis=-1, keepdims=True), axis=0, keepdims=True)
        loss_ref[...] += (0.5 / D) * sq

    row = pl.BlockSpec((TB, D), lambda i: (i, 0))
    return pl.pallas_call(
        body, name="loss_grad",
        out_shape=[jax.ShapeDtypeStruct((t, D), F32), jax.ShapeDtypeStruct((8, 128), F32)], grid=(t // TB,),
        in_specs=[row, row], out_specs=[row, _const_spec((8, 128))], compiler_params=_cparams(1),
    )(y, target)


def _tn(a, b, tm, name, out_dtype=BF16):
    t, m = a.shape
    n = b.shape[1]
    nk = t // TB

    def body(a_ref, b_ref, o_ref, acc_ref):
        k = pl.program_id(1)

        @pl.when(k == 0)
        def _():
            acc_ref[...] = jnp.zeros_like(acc_ref)

        acc_ref[...] += _dot(a_ref[...].astype(BF16), b_ref[...].astype(BF16), 0, 0)

        @pl.when(k == nk - 1)
        def _():
            o_ref[...] = acc_ref[...].astype(out_dtype)

    return pl.pallas_call(
        body, name=name, out_shape=jax.ShapeDtypeStruct((m, n), out_dtype), grid=(m // tm, nk),
        in_specs=[pl.BlockSpec((TB, tm), lambda i, k: (k, i)), pl.BlockSpec((TB, n), lambda i, k: (k, 0))],
        out_specs=pl.BlockSpec((tm, n), lambda i, k: (i, 0)),
        scratch_shapes=[pltpu.VMEM((tm, n), F32)],
        compiler_params=_cparams(2),
    )(a, b)


def _adamw(ws, gs, ms, vs, name, row_block=None):
    n = len(ws)
    c1 = 1.0 - B1 ** STEP
    c2 = 1.0 - B2 ** STEP

    def body(*refs):
        for a in range(n):
            w, g, m, v = (refs[k * n + a][...] for k in range(4))
            d_ref, m_ref, v_ref = (refs[(4 + k) * n + a] for k in range(3))
            m_new = B1 * m + (1.0 - B1) * g
            v_new = B2 * v + (1.0 - B2) * (g * g)
            d_ref[...] = -LR * ((m_new / c1) / (jnp.sqrt(v_new / c2) + ADAM_EPS) + WD * w)
            m_ref[...] = m_new
            v_ref[...] = v_new

    if row_block is None:
        grid = (1,)
        specs = [pl.BlockSpec(w.shape, lambda i: (0, 0)) for w in ws]
    else:
        grid = (ws[0].shape[0] // row_block,)
        specs = [pl.BlockSpec((row_block, w.shape[1]), lambda i: (i, 0)) for w in ws]
    outs = pl.pallas_call(
        body, name=name, out_shape=[jax.ShapeDtypeStruct(w.shape, F32) for w in ws] * 3, grid=grid,
        in_specs=specs * 4, out_specs=specs * 3, compiler_params=_cparams(1),
    )(*ws, *gs, *ms, *vs)
    return outs[:n], outs[n:2 * n], outs[2 * n:]


def _rows1024(a, rows):
    flat = a.reshape(-1, D)
    return jnp.pad(flat, ((0, rows - flat.shape[0]), (0, 0)))


def _pack_shards(even_w_in, even_w_out, odd_w_in, q_b, kv_b, odd_w_out, ffn_w_gate, ffn_w_up, ffn_w_down):
    mix0 = jnp.concatenate([even_w_in[0].T, jnp.zeros((OFF_EOUT - N_EIN, D), F32), even_w_out[0]], axis=0)
    ffn = [jnp.concatenate([ffn_w_gate[layer].T, ffn_w_up[layer].T, ffn_w_down[layer]], axis=0)
           for layer in range(2)]
    mix1 = jnp.concatenate([jnp.pad(odd_w_in[0], ((0, 0), (0, D - ODD_IN))), odd_w_out[0],
                            _rows1024(q_b[0].T, N_QB), _rows1024(kv_b[0].T, N_KVB),
                            jnp.zeros((R_MIX1 - OFF_KVB - N_KVB, D), F32)], axis=0)
    return [c.astype(BF16) for c in (mix0, ffn[0], mix1, ffn[1])]


def _pad_heads(a):
    k = a.shape[1]
    return jnp.pad(a.reshape(HEADS, QK_DIM, k), ((0, 0), (0, HP - QK_DIM), (0, 0))).reshape(HEADS * HP, k)


def _small_pack(parts):
    flat = []
    for p in parts:
        v = p.reshape(-1)
        flat.append(jnp.pad(v, (0, (-v.shape[0]) % 1024)))
    return jnp.concatenate(flat).reshape(-1, 128)


def _small_unpack(buf, shapes):
    flat = buf.reshape(-1)
    out, off = [], 0
    for s in shapes:
        size = int(np.prod(s))
        out.append(flat[off:off + size].reshape(s))
        off += size + (-size) % 1024
    return out


def _reduce_scatter_head(gp, c_arr, tag):
    pb = _rs_pair_sum(gp, _rs_pair_exchange(gp, "rs_pair_exchange_" + tag), c_arr, "rs_pair_sum_" + tag)
    return pb, _chip_exchange_comm(pb)


def _step(x3d, positions, target3d, w_mix0, chunks, c_arr, chip_arr, mix_norm, ffn_norm, sg_ln_g, sg_w_s, sg_b_s,
          conv_w, pool_w, pool_scale, q_a_norm, kv_a_norm, q_norm, k_norm):
    bsz, seq, _ = x3d.shape
    t = bsz * seq
    x0 = x3d.reshape(t, D)
    target = target3d.reshape(t, D)
    my_ffn0, my_mix1, my_ffn1 = chunks

    ws = sg_w_s[0]
    bst = jnp.pad(sg_b_s[0].T, ((0, 0), (0, 128 - SG_HEADS)))
    cw = jnp.pad(conv_w, ((0, 8 - 3), (0, 0)))
    pw_bd = jax.scipy.linalg.block_diag(*[pool_w[0, g] for g in range(4)]).astype(BF16)
    gq = jnp.pad(q_norm, ((0, 0), (0, HP - QK_DIM)))
    gk = jnp.pad(k_norm, ((0, 0), (0, HP - QK_DIM)))
    lane = np.arange(128)
    inv_freq = np.where(lane < QK_ROPE, ROPE_THETA ** (-(2.0 * (lane % 32)) / QK_ROPE), 0.0)
    inv_freq = jnp.asarray(inv_freq.reshape(1, 128), F32)
    cos, s1, s2 = _rope_tables(positions.reshape(t, 1), inv_freq)

    def slab(a, n):
        return a.reshape(N_DEV, n, D)

    (x1, proj_e), (w_ffn0,) = _even_fwd(x0, w_mix0, mix_norm[0:1], sg_ln_g, ws, bst, cw, seq,
                                        _gather_comm([my_ffn0]))
    (x2, g0, u0), (w_mix1,) = _ffn_fwd(x1, w_ffn0, ffn_norm[0:1], "ffn_fwd0", _gather_comm([my_mix1]))
    qbt = _pad_heads(w_mix1[:, OFF_QB:OFF_QB + N_QB_USED, :].reshape(HEADS * QK_DIM, Q_LORA))
    kvbt = w_mix1[:, OFF_KVB:OFF_KVB + N_KVB, :].reshape(HEADS * HP, KV_LORA)
    proj_o, q, kv, kr, c_out = _odd_pre_fwd(x2, w_mix1, mix_norm[1:2], qbt, kvbt, q_a_norm, kv_a_norm, pw_bd,
                                            pool_scale, seq)
    (d_out,), (w_ffn1,) = _attn_fwd(q, kv, kr, cos, s1, s2, gq, gk, seq, _gather_comm([my_ffn1]))
    x3 = _odd_post_fwd(x2, c_out, d_out, w_mix1)
    (x4, g1, u1), _ = _ffn_fwd(x3, w_ffn1, ffn_norm[1:2], "ffn_fwd1")
    dy, loss_tile = _loss_grad(x4, target)

    (dx3, act1, dg1, du1, h3, dgam_f1), _ = _ffn_bwd(x3, g1, u1, dy, w_ffn1, ffn_norm[1:2], "ffn_bwd1")
    gp_ffn1 = jnp.concatenate([slab(_tn(dg1, h3, 1408, "dw_gate1"), N_FF), slab(_tn(du1, h3, 1408, "dw_up1"), N_FF),
                               slab(_tn(act1, dy, 1408, "dw_down1"), N_FF)], axis=1)
    pb_ffn1, rs_ffn1 = _reduce_scatter_head(gp_ffn1, c_arr, "ffn1")

    dmix_o = _odd_post_bwd(dx3, w_mix1)
    d_oout = jnp.concatenate([_tn(c_out, dx3, POOL_W, "dw_oout_c"), _tn(d_out, dx3, HEADS * V_DIM, "dw_oout_d")], 0)
    (dq, dkv, dkr, dgq, dgk), (got_ffn1,) = _attn_bwd(q, kv, kr, cos, s1, s2, gq, gk, dmix_o, seq, rs_ffn1)
    gsh_ffn1 = _rs_final_sum(pb_ffn1, got_ffn1, chip_arr, "rs_final_sum_ffn1")
    (dx2, dproj_o, h2, qn, kvn, dgam_m1, dqa, dkva, dpw_bd, dps) = _odd_pre_bwd(
        x2, proj_o, dx3, dmix_o, dq, dkv, dkr, w_mix1, mix_norm[1:2], qbt, kvbt, q_a_norm, kv_a_norm, pw_bd,
        pool_scale, seq)
    d_qbt = _tn(dq, qn, HEADS * HP, "dw_qb")
    d_qb_rows = d_qbt.reshape(HEADS, HP, Q_LORA)[:, :QK_DIM].reshape(N_DEV, N_QB_USED, D)
    gp_mix1 = jnp.concatenate([
        slab(_tn(h2, dproj_o, D, "dw_oin"), N_SQ), slab(d_oout, N_SQ),
        jnp.pad(d_qb_rows, ((0, 0), (0, N_QB - N_QB_USED), (0, 0))),
        slab(_tn(dkv, kvn, HEADS * HP, "dw_kvb"), N_KVB),
        jnp.zeros((N_DEV, R_MIX1 - OFF_KVB - N_KVB, D), BF16)], axis=1)
    pb_mix1, rs_mix1 = _reduce_scatter_head(gp_mix1, c_arr, "mix1")

    (dx1, act0, dg0, du0, h1, dgam_f0), (got_mix1,) = _ffn_bwd(x1, g0, u0, dx2, w_ffn0, ffn_norm[0:1], "ffn_bwd0",
                                                              rs_mix1)
    gsh_mix1 = _rs_final_sum(pb_mix1, got_mix1, chip_arr, "rs_final_sum_mix1")
    gp_ffn0 = jnp.concatenate([slab(_tn(dg0, h1, 1408, "dw_gate0"), N_FF), slab(_tn(du0, h1, 1408, "dw_up0"), N_FF),
                               slab(_tn(act0, dx2, 1408, "dw_down0"), N_FF)], axis=1)
    pb_ffn0, rs_ffn0 = _reduce_scatter_head(gp_ffn0, c_arr, "ffn0")

    (dx0, dproj_e, mix_e, h0, dgam_m0, dws, dbc, dlng, dcw), (got_ffn0,) = _even_bwd(
        x0, proj_e, dx1, w_mix0, mix_norm[0:1], sg_ln_g, ws, bst, cw, seq, rs_ffn0)
    gsh_ffn0 = _rs_final_sum(pb_ffn0, got_ffn0, chip_arr, "rs_final_sum_ffn0")
    gp_mix0 = jnp.concatenate([slab(_tn(dproj_e, h0, 1280, "dw_ein"), N_EIN),
                               jnp.zeros((N_DEV, OFF_EOUT - N_EIN, D), BF16),
                               slab(_tn(mix_e, dx1, D, "dw_eout"), N_SQ)], axis=1)
    pb_mix0, rs_mix0 = _reduce_scatter_head(gp_mix0, c_arr, "mix0")
    (got_mix0,) = _comm_alone(rs_mix0, "rs_chip_exchange_mix0")
    gsh_mix0 = _rs_final_sum(pb_mix0, got_mix0, chip_arr, "rs_final_sum_mix0")

    small = [
        jnp.concatenate([dgam_m0, dgam_m1], 0), jnp.concatenate([dgam_f0, dgam_f1], 0), dlng,
        dws[None], dbc[:, :SG_HEADS].T[None], dcw[:3],
        jnp.stack([dpw_bd[g * POOL_GD:(g + 1) * POOL_GD, g * POOL_GD:(g + 1) * POOL_GD] for g in range(4)])[None],
        dps, dqa, dkva, dgq[:, :QK_DIM], dgk[:, :QK_DIM], loss_tile[0:1, 0:1]]
    return dx0.reshape(bsz, seq, D), (gsh_mix0, gsh_ffn0, gsh_mix1, gsh_ffn1), small


SMALL_SHAPES = [(2, D), (2, D), (1, SG_W), (1, SG_HEADS, 128, 128), (1, SG_HEADS, 128), (3, SC_W),
                (1, 4, POOL_GD, POOL_GD), (1, POOL_W), (1, Q_LORA), (1, KV_LORA), (1, QK_DIM), (1, QK_DIM), (1, 1)]


def kernel(x, positions, mix_norm, ffn_norm, even_w_in, sg_ln_g, sg_w_s, sg_b_s, sc_conv_w, even_w_out, odd_w_in, pool_w, pool_scale, q_a_norm, q_b, kv_a_norm, kv_b, q_norm, k_norm, odd_w_out, ffn_w_gate, ffn_w_up, ffn_w_down, loss_target, m_mix_norm, m_ffn_norm, m_even_w_in, m_sg_ln_g, m_sg_w_s, m_sg_b_s, m_sc_conv_w, m_even_w_out, m_odd_w_in, m_pool_w, m_pool_scale, m_q_a_norm, m_q_b, m_kv_a_norm, m_kv_b, m_q_norm, m_k_norm, m_odd_w_out, m_ffn_w_gate, m_ffn_w_up, m_ffn_w_down, v_mix_norm, v_ffn_norm, v_even_w_in, v_sg_ln_g, v_sg_w_s, v_sg_b_s, v_sc_conv_w, v_even_w_out, v_odd_w_in, v_pool_w, v_pool_scale, v_q_a_norm, v_q_b, v_kv_a_norm, v_kv_b, v_q_norm, v_k_norm, v_odd_w_out, v_ffn_w_gate, v_ffn_w_up, v_ffn_w_down):
    xi, yi, ci = _place()
    me = 4 * xi + 2 * yi + ci

    my_mix0, my_ffn0, my_mix1, my_ffn1 = _pack_shards(even_w_in, even_w_out, odd_w_in, q_b, kv_b, odd_w_out,
                                                      ffn_w_gate, ffn_w_up, ffn_w_down)

    def lane_pad(a):
        return jnp.pad(a, ((0, 0), (0, 128 - a.shape[1])))

    tile = jnp.concatenate([lane_pad(sc_conv_w[0]), lane_pad(pool_scale), lane_pad(q_a_norm), lane_pad(kv_a_norm),
                            jnp.zeros((2, 128), F32)], axis=0)
    w_mix0, tiles = _comm_alone(_gather_comm([my_mix0, tile]), "all_gather_mix0")
    conv_full = tiles[:, 0:3, 0:64].transpose(1, 0, 2).reshape(3, SC_W)
    pscale_full = tiles[:, 3, 0:32].reshape(1, POOL_W)
    qa_full = tiles[:, 4, 0:48].reshape(1, Q_LORA)
    kva_full = tiles[:, 5, 0:32].reshape(1, KV_LORA)

    c_arr = jnp.reshape(ci, (1,)).astype(jnp.int32)
    chip_arr = jnp.reshape(2 * xi + yi, (1,)).astype(jnp.int32)
    grad_x, (gsh_mix0, gsh_ffn0, gsh_mix1, gsh_ffn1), small = _step(
        x, positions, loss_target, w_mix0, (my_ffn0, my_mix1, my_ffn1), c_arr, chip_arr, mix_norm, ffn_norm,
        sg_ln_g, sg_w_s, sg_b_s, conv_full, pool_w, pscale_full, qa_full, kva_full, q_norm, k_norm)

    tot = _small_unpack(_all_reduce_small(_small_pack(small)), SMALL_SHAPES)
    (g_mix, g_ffn, g_lng, g_ws, g_bs, g_cw_full, g_pw, g_ps_full, g_qa_full, g_kva_full, g_qn, g_kn, loss) = tot
    g_cw = lax.dynamic_slice_in_dim(g_cw_full, me * 64, 64, axis=1)[None]
    g_ps = lax.dynamic_slice_in_dim(g_ps_full, me * 32, 32, axis=1)
    g_qa = lax.dynamic_slice_in_dim(g_qa_full, me * 48, 48, axis=1)
    g_kva = lax.dynamic_slice_in_dim(g_kva_full, me * 32, 32, axis=1)

    def tr(a):
        return jnp.swapaxes(a, -1, -2)

    g_ein = tr(gsh_mix0[OFF_EIN:OFF_EIN + N_EIN][None])
    g_gate = tr(jnp.stack([gsh_ffn0[OFF_GATE:OFF_GATE + N_FF], gsh_ffn1[OFF_GATE:OFF_GATE + N_FF]]))
    g_up = tr(jnp.stack([gsh_ffn0[OFF_UP:OFF_UP + N_FF], gsh_ffn1[OFF_UP:OFF_UP + N_FF]]))
    g_down = jnp.stack([gsh_ffn0[OFF_DOWN:OFF_DOWN + N_FF], gsh_ffn1[OFF_DOWN:OFF_DOWN + N_FF]])
    g_eout = gsh_mix0[OFF_EOUT:OFF_EOUT + N_SQ][None]
    g_oin = gsh_mix1[OFF_OIN:OFF_OIN + N_SQ, :ODD_IN][None]
    g_oout = gsh_mix1[OFF_OOUT:OFF_OOUT + N_SQ][None]
    g_qb = tr(gsh_mix1[OFF_QB:OFF_QB + N_QB_USED].reshape(1, 144, Q_LORA))
    g_kvb = tr(gsh_mix1[OFF_KVB:OFF_KVB + N_KVB].reshape(1, 192, KV_LORA))
    transposed = ("even_w_in", "odd_w_in", "q_b", "kv_b", "ffn_w_gate", "ffn_w_up")

    grads = dict(mix_norm=g_mix, ffn_norm=g_ffn, even_w_in=g_ein, sg_ln_g=g_lng, sg_w_s=g_ws, sg_b_s=g_bs,
                 sc_conv_w=g_cw, even_w_out=g_eout, odd_w_in=g_oin, pool_w=g_pw, pool_scale=g_ps, q_a_norm=g_qa,
                 q_b=g_qb, kv_a_norm=g_kva, kv_b=g_kvb, q_norm=g_qn, k_norm=g_kn, odd_w_out=g_oout,
                 ffn_w_gate=g_gate, ffn_w_up=g_up, ffn_w_down=g_down)
    weights = dict(mix_norm=mix_norm, ffn_norm=ffn_norm, even_w_in=even_w_in, sg_ln_g=sg_ln_g, sg_w_s=sg_w_s,
                   sg_b_s=sg_b_s, sc_conv_w=sc_conv_w, even_w_out=even_w_out, odd_w_in=odd_w_in, pool_w=pool_w,
                   pool_scale=pool_scale, q_a_norm=q_a_norm, q_b=q_b, kv_a_norm=kv_a_norm, kv_b=kv_b, q_norm=q_norm,
                   k_norm=k_norm, odd_w_out=odd_w_out, ffn_w_gate=ffn_w_gate, ffn_w_up=ffn_w_up,
                   ffn_w_down=ffn_w_down)
    m_in = dict(mix_norm=m_mix_norm, ffn_norm=m_ffn_norm, even_w_in=m_even_w_in, sg_ln_g=m_sg_ln_g, sg_w_s=m_sg_w_s,
                sg_b_s=m_sg_b_s, sc_conv_w=m_sc_conv_w, even_w_out=m_even_w_out, odd_w_in=m_odd_w_in,
                pool_w=m_pool_w, pool_scale=m_pool_scale, q_a_norm=m_q_a_norm, q_b=m_q_b, kv_a_norm=m_kv_a_norm,
                kv_b=m_kv_b, q_norm=m_q_norm, k_norm=m_k_norm, odd_w_out=m_odd_w_out, ffn_w_gate=m_ffn_w_gate,
                ffn_w_up=m_ffn_w_up, ffn_w_down=m_ffn_w_down)
    v_in = dict(mix_norm=v_mix_norm, ffn_norm=v_ffn_norm, even_w_in=v_even_w_in, sg_ln_g=v_sg_ln_g, sg_w_s=v_sg_w_s,
                sg_b_s=v_sg_b_s, sc_conv_w=v_sc_conv_w, even_w_out=v_even_w_out, odd_w_in=v_odd_w_in,
                pool_w=v_pool_w, pool_scale=v_pool_scale, q_a_norm=v_q_a_norm, q_b=v_q_b, kv_a_norm=v_kv_a_norm,
                kv_b=v_kv_b, q_norm=v_q_norm, k_norm=v_k_norm, odd_w_out=v_odd_w_out, ffn_w_gate=v_ffn_w_gate,
                ffn_w_up=v_ffn_w_up, ffn_w_down=v_ffn_w_down)
    names = list(grads)

    big_rows = {"ffn_w_gate": 352, "ffn_w_up": 352, "ffn_w_down": 352}
    delta, new_m, new_v = {}, {}, {}

    def as2d(k, a):
        a = tr(a) if k in transposed else a
        return a.reshape(-1, a.shape[-1])

    def back(k, a):
        shape = weights[k].shape
        return tr(a.reshape(shape[:-2] + (shape[-1], shape[-2]))) if k in transposed else a.reshape(shape)

    small_names = [k for k in names if weights[k].size <= 70000]
    outs = _adamw([as2d(k, weights[k]) for k in small_names], [as2d(k, grads[k]) for k in small_names],
                  [as2d(k, m_in[k]) for k in small_names], [as2d(k, v_in[k]) for k in small_names], "adamw_small")
    for i, k in enumerate(small_names):
        delta[k], new_m[k], new_v[k] = (back(k, o[i]) for o in outs)
    for k in names:
        if k in small_names:
            continue
        outs = _adamw([as2d(k, weights[k])], [as2d(k, grads[k])], [as2d(k, m_in[k])], [as2d(k, v_in[k])],
                      "adamw_" + k, row_block=big_rows.get(k))
        delta[k], new_m[k], new_v[k] = (back(k, o[0]) for o in outs)

    return (loss.reshape(()), grad_x, *[grads[k] for k in names], *[delta[k] for k in names],
            *[new_m[k] for k in names], *[new_v[k] for k in names])
```

```python
import functools

import numpy as np
import jax
import jax.numpy as jnp
from jax import lax
from jax.experimental import pallas as pl
from jax.experimental.pallas import tpu as pltpu

F32 = jnp.float32
BF16 = jnp.bfloat16
MESH = pl.DeviceIdType.MESH

D = 1024
EPS = 1e-6
NEG_INF = -1e30
SG_HEADS, SG_HD, SG_W, SG_CHUNK = 4, 128, 512, 128
SC_W = 512
EVEN_IN = 2560
POOL_W = 256
POOL_GD = 64
Q_LORA, KV_LORA, QK_ROPE, QK_NOPE, V_DIM = 384, 256, 64, 128, 128
QK_DIM = QK_NOPE + QK_ROPE
HEADS = 6
HP = 256
ODD_IN = 960
D_FF = 2816
ROPE_THETA = 10000.0
ATT_SCALE = QK_DIM ** -0.5
LR, B1, B2, ADAM_EPS, WD, STEP = 0.001, 0.9, 0.999, 1e-08, 0.01, 10

N_DEV = 8
TB = 512
TB_FFN_BWD = 256
HALO = 16
VMEM_LIMIT = 56 * 1024 * 1024

N_EIN, N_FF, N_SQ = 320, 352, 128
OFF_EIN, OFF_EOUT, R_MIX0 = 0, 384, 512
OFF_GATE, OFF_UP, OFF_DOWN, R_FFN = 0, 352, 704, 1056
OFF_OIN, OFF_OOUT, OFF_QB, OFF_KVB, R_MIX1 = 0, 128, 256, 320, 384
N_QB, N_QB_USED, N_KVB = 64, 54, 48

INV_SQRT2 = 0.7071067811865476
INV_SQRT_2PI = 0.3989422804014327


def _dot(a, b, ca, cb):
    return lax.dot_general(a, b, (((ca,), (cb,)), ((), ())), preferred_element_type=F32)


def _cparams(n_axes=1):
    return pltpu.CompilerParams(dimension_semantics=("arbitrary",) * n_axes, vmem_limit_bytes=VMEM_LIMIT)


def _wspec(n, off, arity=1):
    assert off % n == 0
    idx = off // n
    if arity == 1:
        return pl.BlockSpec((N_DEV, n, D), lambda i: (0, idx, 0), pipeline_mode=pl.Buffered(1))
    return pl.BlockSpec((N_DEV, n, D), lambda i, j: (0, idx, 0), pipeline_mode=pl.Buffered(1))


def _const_spec(shape):
    zeros = (0,) * len(shape)
    return pl.BlockSpec(shape, lambda *_: zeros)


class _Comm:
    def __init__(self, ins, out_shapes, sems, start, wait, mid=None):
        self.ins, self.out_shapes, self.sems, self.start, self.wait, self.mid = ins, out_shapes, sems, start, wait, mid


def _both(c1, c2):
    def split(f1, f2):
        def run(ins, outs, sems):
            f1(ins[:len(c1.ins)], outs[:len(c1.out_shapes)], sems[:len(c1.sems)])
            f2(ins[len(c1.ins):], outs[len(c1.out_shapes):], sems[len(c1.sems):])
        return run

    assert c1.mid is None and c2.mid is None
    return _Comm(c1.ins + c2.ins, c1.out_shapes + c2.out_shapes, c1.sems + c2.sems,
                 split(c1.start, c2.start), split(c1.wait, c2.wait))


def _call(body, name, grid, in_specs, out_specs, out_shape, args, scratch_shapes=(), comm=None, aliases=None):
    n_axes = len(grid)
    aliases = aliases or {}
    if comm is None:
        res = pl.pallas_call(
            body, name=name, grid=grid, in_specs=list(in_specs), out_specs=list(out_specs),
            out_shape=list(out_shape), scratch_shapes=list(scratch_shapes), input_output_aliases=aliases,
            compiler_params=_cparams(n_axes))(*args)
        return list(res), []
    ni, no, ns = len(in_specs), len(out_specs), len(scratch_shapes)
    ci, co = len(comm.ins), len(comm.out_shapes)
    n_steps = int(np.prod(grid))

    def carrier(*refs):
        ins, cin = refs[:ni], refs[ni:ni + ci]
        outs, cout = refs[ni + ci:ni + ci + no], refs[ni + ci + no:ni + ci + no + co]
        scr, sems = refs[ni + ci + no + co:ni + ci + no + co + ns], refs[ni + ci + no + co + ns:]
        step = 0
        for a in range(n_axes):
            step = step * grid[a] + pl.program_id(a)

        @pl.when(step == 0)
        def _():
            comm.start(cin, cout, sems)

        body(*ins, *outs, *scr)

        if comm.mid is not None and n_steps >= 4:
            @pl.when(step == (3 * n_steps) // 4)
            def _():
                comm.mid(cin, cout, sems)

        @pl.when(step == n_steps - 1)
        def _():
            if comm.mid is not None and n_steps < 4:
                comm.mid(cin, cout, sems)
            comm.wait(cin, cout, sems)

    any_spec = pl.BlockSpec(memory_space=pl.ANY)
    res = pl.pallas_call(
        carrier, name=name, grid=grid, in_specs=list(in_specs) + [any_spec] * ci,
        out_specs=list(out_specs) + [any_spec] * co, out_shape=list(out_shape) + list(comm.out_shapes),
        scratch_shapes=list(scratch_shapes) + list(comm.sems), input_output_aliases=aliases,
        compiler_params=_cparams(n_axes))(*args, *comm.ins)
    return list(res[:no]), list(res[no:])


def _comm_alone(comm, name):
    ci, co = len(comm.ins), len(comm.out_shapes)

    def body(*refs):
        cin, cout, sems = refs[:ci], refs[ci:ci + co], refs[ci + co:]
        comm.start(cin, cout, sems)
        if comm.mid is not None:
            comm.mid(cin, cout, sems)
        comm.wait(cin, cout, sems)

    any_spec = pl.BlockSpec(memory_space=pl.ANY)
    res = pl.pallas_call(
        body, name=name, out_shape=list(comm.out_shapes), in_specs=[any_spec] * ci, out_specs=[any_spec] * co,
        scratch_shapes=list(comm.sems))(*comm.ins)
    return list(res)


def _rms(x, g, n=None):
    n = x.shape[-1] if n is None else n
    r = lax.rsqrt(jnp.sum(x * x, axis=-1, keepdims=True) / n + EPS)
    return x * r * g, r


def _rms_bwd(x, r, g, dy, n=None):
    n = x.shape[-1] if n is None else n
    xh = x * r
    dxh = dy * g
    dx = r * (dxh - xh * (jnp.sum(dxh * xh, axis=-1, keepdims=True) / n))
    dg = jnp.sum(dy * xh, axis=0, keepdims=True)
    return dx, dg


def _gelu(x):
    return 0.5 * x * (1.0 + lax.erf(x * INV_SQRT2))


def _gelu_grad(x):
    return 0.5 * (1.0 + lax.erf(x * INV_SQRT2)) + x * jnp.exp(-0.5 * x * x) * INV_SQRT_2PI


def _shift_down(a, k):
    rows = lax.broadcasted_iota(jnp.int32, a.shape, 0)
    return jnp.where(rows >= k, pltpu.roll(a, k, 0), 0.0)


def _shift_up(a, k):
    n = a.shape[0]
    rows = lax.broadcasted_iota(jnp.int32, a.shape, 0)
    return jnp.where(rows < n - k, pltpu.roll(a, n - k, 0), 0.0)


def _tril_bf16(w):
    r = lax.broadcasted_iota(jnp.int32, w.shape, 0)
    c = lax.broadcasted_iota(jnp.int32, w.shape, 1)
    return jnp.where(r >= c, w, 0.0).astype(BF16)


def _ln_head(vh, g):
    mu = jnp.mean(vh, axis=-1, keepdims=True)
    xc = vh - mu
    rr = lax.rsqrt(jnp.mean(xc * xc, axis=-1, keepdims=True) + EPS)
    xh = xc * rr
    return xh * g, xh, rr


def _conv_fwd(z, tail, cw_ref):
    ext = jnp.concatenate([tail, z], axis=0)
    zs1 = _shift_down(ext, 1)[HALO:]
    zs2 = _shift_down(ext, 2)[HALO:]
    y = cw_ref[2:3, :] * z + cw_ref[1:2, :] * zs1 + cw_ref[0:1, :] * zs2
    return y, zs1, zs2


def _pool_cnt(shape, blk_in_seq):
    rows = lax.broadcasted_iota(jnp.int32, shape, 0)
    grp = lax.broadcasted_iota(jnp.int32, shape, 1) // POOL_GD
    win = jnp.where(grp == 0, 2, jnp.where(grp == 1, 4, jnp.where(grp == 2, 8, 16)))
    tpos = blk_in_seq * shape[0] + rows + 1
    return jnp.minimum(tpos, win).astype(F32), grp


def _pool_select(grp, s2, s4, s8, s16):
    return jnp.where(grp == 0, s2, jnp.where(grp == 1, s4, jnp.where(grp == 2, s8, s16)))


def _pool_fwd(z, tail, blk_in_seq):
    ext = jnp.concatenate([tail, z], axis=0)
    s2 = ext + _shift_down(ext, 1)
    s4 = s2 + _shift_down(s2, 2)
    s8 = s4 + _shift_down(s4, 4)
    s16 = s8 + _shift_down(s8, 8)
    cnt, grp = _pool_cnt(z.shape, blk_in_seq)
    sums = _pool_select(grp, s2[HALO:], s4[HALO:], s8[HALO:], s16[HALO:])
    return sums / cnt - z, cnt, grp


def _pool_bwd(dpooled, dpm, head, grp):
    n = dpm.shape[0]
    ext = jnp.concatenate([dpm, head], axis=0)
    u2 = ext + _shift_up(ext, 1)
    u4 = u2 + _shift_up(u2, 2)
    u8 = u4 + _shift_up(u4, 4)
    u16 = u8 + _shift_up(u8, 8)
    return _pool_select(grp, u2[:n], u4[:n], u8[:n], u16[:n]) - dpooled


def _rope(y1, c, s1, s2):
    return y1 * c + pltpu.roll(y1, 96, 1) * s1 + pltpu.roll(y1, 32, 1) * s2


def _rope_bwd(d1, c, s1, s2):
    return d1 * c + pltpu.roll(d1 * s1, 32, 1) + pltpu.roll(d1 * s2, 96, 1)


def _qk_prep(xh_in, g, c, s1, s2):
    y, r = _rms(xh_in, g, QK_DIM)
    out = jnp.concatenate([y[:, :128], _rope(y[:, 128:], c, s1, s2)], axis=1)
    return out, r


def _qk_prep_bwd(dout, x_in, r, g, c, s1, s2):
    dy = jnp.concatenate([dout[:, :128], _rope_bwd(dout[:, 128:], c, s1, s2)], axis=1)
    return _rms_bwd(x_in, r, g, dy, QK_DIM)


def _place():
    return lax.axis_index("x"), lax.axis_index("y"), lax.axis_index("c")


def _gather_comm(arrs):
    n = len(arrs)

    def plan(ins, outs, sems):
        send_sems, recv_sems, local_sems = sems
        x, y, c = _place()
        me, sibling = (x, y, c), (x, y, 1 - c)
        chips = [(1 - x, y), (x, 1 - y), (1 - x, 1 - y)]

        def slot(a, px, py, pc):
            return outs[a].at[4 * px + 2 * py + pc]

        def copy(a, k, block, to, src=None):
            return pltpu.make_async_remote_copy(
                src_ref=slot(a, *block) if src is None else src, dst_ref=slot(a, *block),
                send_sem=send_sems.at[a, k], recv_sem=recv_sems.at[a, k], device_id=to, device_id_type=MESH)

        mine = [pltpu.make_async_copy(ins[a], slot(a, *me), local_sems.at[a]) for a in range(n)]
        first = []
        for a in range(n):
            first.append(copy(a, 0, me, sibling, src=ins[a]))
            first += [copy(a, 1 + j, me, (*chip, c), src=ins[a]) for j, chip in enumerate(chips)]
        return c, me, sibling, chips, copy, mine, first

    def start(ins, outs, sems):
        _, _, _, _, _, mine, first = plan(ins, outs, sems)
        for cp in mine + first:
            cp.start()

    def mid(ins, outs, sems):
        c, me, sibling, chips, copy, _, _ = plan(ins, outs, sems)
        for j, chip in enumerate(chips):
            for a in range(n):
                copy(a, 1 + j, (*chip, c), me).wait_recv()
                copy(a, 4 + j, (*chip, c), sibling).start()

    def wait(ins, outs, sems):
        c, me, sibling, chips, copy, mine, first = plan(ins, outs, sems)
        passed = [copy(a, 4 + j, (*chip, c), sibling) for j, chip in enumerate(chips) for a in range(n)]
        for a in range(n):
            copy(a, 0, sibling, me).wait_recv()
            for j, chip in enumerate(chips):
                copy(a, 4 + j, (*chip, 1 - c), me).wait_recv()
        for cp in first + passed:
            cp.wait_send()
        for cp in mine:
            cp.wait()

    return _Comm(
        list(arrs), [jax.ShapeDtypeStruct((N_DEV,) + a.shape, a.dtype) for a in arrs],
        [pltpu.SemaphoreType.DMA((n, 7)), pltpu.SemaphoreType.DMA((n, 7)), pltpu.SemaphoreType.DMA((n,))],
        start, wait, mid)


def _sum_gathered(g):
    rows = g.shape[1]

    def body(g_ref, sum_ref):
        total = g_ref[0]
        for d in range(1, N_DEV):
            total = total + g_ref[d]
        sum_ref[...] = total

    return pl.pallas_call(
        body, name="sum_gathered_small", out_shape=jax.ShapeDtypeStruct((rows, 128), F32), grid=(1,),
        in_specs=[pl.BlockSpec((N_DEV, rows, 128), lambda i: (0, 0, 0))],
        out_specs=pl.BlockSpec((rows, 128), lambda i: (0, 0)), compiler_params=_cparams(1),
    )(g)


def _sum_rows(rows):
    return rows if rows <= 512 else rows // 2


def _pair_exchange_comm(gp):
    _, rows, cols = gp.shape

    def copies(ins, outs, sems):
        send_sems, recv_sems = sems
        x, y, c = _place()
        return [pltpu.make_async_remote_copy(
            src_ref=ins[0].at[2 * j + (1 - c)], dst_ref=outs[0].at[j], send_sem=send_sems.at[j],
            recv_sem=recv_sems.at[j], device_id=(x, y, 1 - c), device_id_type=MESH) for j in range(4)]

    def start(ins, outs, sems):
        for cp in copies(ins, outs, sems):
            cp.start()

    def wait(ins, outs, sems):
        for cp in copies(ins, outs, sems):
            cp.wait()

    return _Comm([gp], [jax.ShapeDtypeStruct((4, rows, cols), gp.dtype)],
                 [pltpu.SemaphoreType.DMA((4,)), pltpu.SemaphoreType.DMA((4,))], start, wait)


def _rs_pair_sum(gp, got, c_arr, name):
    _, rows, cols = got.shape
    rb = _sum_rows(rows)
    gp4 = gp.reshape(4, 2, rows, cols)

    def body(c_ref, a_ref, b_ref, o_ref):
        o_ref[0] = (a_ref[0, 0].astype(F32) + b_ref[0].astype(F32)).astype(o_ref.dtype)

    return pl.pallas_call(
        body, name=name, out_shape=jax.ShapeDtypeStruct((4, rows, cols), gp.dtype),
        grid_spec=pltpu.PrefetchScalarGridSpec(
            num_scalar_prefetch=1, grid=(4, rows // rb),
            in_specs=[pl.BlockSpec((1, 1, rb, cols), lambda j, r, cr: (j, cr[0], r, 0)),
                      pl.BlockSpec((1, rb, cols), lambda j, r, cr: (j, r, 0))],
            out_specs=pl.BlockSpec((1, rb, cols), lambda j, r, cr: (j, r, 0))),
        compiler_params=_cparams(2),
    )(c_arr, gp4, got)


def _chip_exchange_comm(pb):
    _, rows, cols = pb.shape

    def copies(ins, outs, sems):
        send_sems, recv_sems = sems
        x, y, c = _place()
        chips = [(1 - x, y), (x, 1 - y), (1 - x, 1 - y)]
        return [pltpu.make_async_remote_copy(
            src_ref=ins[0].at[2 * px + py], dst_ref=outs[0].at[k], send_sem=send_sems.at[k],
            recv_sem=recv_sems.at[k], device_id=(px, py, c), device_id_type=MESH)
            for k, (px, py) in enumerate(chips)]

    def start(ins, outs, sems):
        for cp in copies(ins, outs, sems):
            cp.start()

    def wait(ins, outs, sems):
        for cp in copies(ins, outs, sems):
            cp.wait()

    return _Comm([pb], [jax.ShapeDtypeStruct((3, rows, cols), pb.dtype)],
                 [pltpu.SemaphoreType.DMA((3,)), pltpu.SemaphoreType.DMA((3,))], start, wait)


def _rs_final_sum(pb, got, chip_arr, name):
    _, rows, cols = got.shape
    rb = _sum_rows(rows)

    def body(j_ref, a_ref, b_ref, o_ref):
        o_ref[...] = ((a_ref[0].astype(F32) + b_ref[0].astype(F32)) + b_ref[1].astype(F32)) + b_ref[2].astype(F32)

    return pl.pallas_call(
        body, name=name, out_shape=jax.ShapeDtypeStruct((rows, cols), F32),
        grid_spec=pltpu.PrefetchScalarGridSpec(
            num_scalar_prefetch=1, grid=(rows // rb,),
            in_specs=[pl.BlockSpec((1, rb, cols), lambda r, jr: (jr[0], r, 0)),
                      pl.BlockSpec((3, rb, cols), lambda r, jr: (0, r, 0))],
            out_specs=pl.BlockSpec((rb, cols), lambda r, jr: (r, 0))),
        compiler_params=_cparams(1),
    )(chip_arr, pb, got)


def _rope_tables(pos_col, inv_freq, comm=None):
    t = pos_col.shape[0]

    def body(p_ref, f_ref, c_ref, s1_ref, s2_ref):
        ang = p_ref[...].astype(F32) * f_ref[...]
        lane = lax.broadcasted_iota(jnp.int32, ang.shape, 1)
        c_ref[...] = jnp.where(lane < QK_ROPE, jnp.cos(ang), 0.0)
        s = jnp.sin(ang)
        s1_ref[...] = jnp.where(lane < 32, -s, 0.0)
        s2_ref[...] = jnp.where((lane >= 32) & (lane < QK_ROPE), s, 0.0)

    spec = pl.BlockSpec((TB, 128), lambda i: (i, 0))
    return _call(
        body, "rope_tables", (t // TB,), [pl.BlockSpec((TB, 1), lambda i: (i, 0)), _const_spec((1, 128))],
        [spec] * 3, [jax.ShapeDtypeStruct((t, 128), F32)] * 3, (pos_col, inv_freq), (), comm)


def _sgu_conv_fwd(proj, tail, lng_ref, ws_ref, bst_ref, cw_ref):
    gu = _gelu(proj[:, 0:SG_W])
    gv = _gelu(proj[:, SG_W:2 * SG_W])
    bg = proj[:, 1024:1536]
    z = proj[:, 1536:2048] * proj[:, 2048:2560]
    heads = []
    for h in range(SG_HEADS):
        sl = slice(h * SG_HD, (h + 1) * SG_HD)
        vn, _, _ = _ln_head(gv[:, sl], lng_ref[:, sl])
        vnb = vn.astype(BF16)
        wm = _tril_bf16(ws_ref[h])
        bcol = bst_ref[:, h:h + 1]
        mixed = jnp.concatenate(
            [_dot(wm, vnb[k * SG_CHUNK:(k + 1) * SG_CHUNK], 1, 0) + bcol for k in range(TB // SG_CHUNK)], axis=0)
        heads.append(gu[:, sl] * mixed)
    a_out = jnp.concatenate(heads, axis=1)
    y, _, _ = _conv_fwd(z, tail, cw_ref)
    return a_out, bg * y, z


def _even_fwd(x, wg, gamma, lng, ws, bst, cw, seq, comm=None):
    t = x.shape[0]
    nbs = seq // TB

    def body(x_ref, gam_ref, win_ref, wout_ref, lng_ref, ws_ref, bst_ref, cw_ref, x1_ref, proj_ref, tail_ref):
        i = pl.program_id(0)
        xv = x_ref[...]
        h, _ = _rms(xv, gam_ref[...])
        proj = _dot(h.astype(BF16), win_ref[...].reshape(EVEN_IN, D), 1, 1)
        proj_ref[...] = proj.astype(BF16)
        tail = jnp.where(i % nbs == 0, 0.0, tail_ref[...])
        a_out, b_out, z = _sgu_conv_fwd(proj, tail, lng_ref, ws_ref, bst_ref, cw_ref)
        tail_ref[...] = z[TB - HALO:, :]
        x1_ref[...] = (xv + _dot(a_out.astype(BF16), wout_ref[0:4].reshape(512, D), 1, 0)
                       + _dot(b_out.astype(BF16), wout_ref[4:8].reshape(512, D), 1, 0))

    row = pl.BlockSpec((TB, D), lambda i: (i, 0))
    return _call(
        body, "even_fwd", (t // TB,),
        [row, _const_spec((1, D)), _wspec(N_EIN, OFF_EIN), _wspec(N_SQ, OFF_EOUT), _const_spec((1, SG_W)),
         _const_spec((SG_HEADS, 128, 128)), _const_spec((128, 128)), _const_spec((8, SC_W))],
        [row, pl.BlockSpec((TB, EVEN_IN), lambda i: (i, 0))],
        [jax.ShapeDtypeStruct((t, D), F32), jax.ShapeDtypeStruct((t, EVEN_IN), BF16)],
        (x, gamma, wg, wg, lng, ws, bst, cw), [pltpu.VMEM((HALO, SC_W), F32)], comm)


def _even_bwd(x, proj, dx1, wg, gamma, lng, ws, bst, cw, seq, comm=None):
    t = x.shape[0]
    nb, nbs = t // TB, seq // TB

    def body(x_ref, proj_ref, ptail_ref, dx1_ref, gam_ref, win_ref, wout_ref, lng_ref, ws_ref, bst_ref, cw_ref,
             dx0_ref, dproj_ref, mix_ref, h_ref, dgam_ref, dws_ref, dbc_ref, dlng_ref, dcw_ref, head_ref):
        i = pl.program_id(0)
        blk = nb - 1 - i

        @pl.when(i == 0)
        def _():
            dgam_ref[...] = jnp.zeros_like(dgam_ref)
            dws_ref[...] = jnp.zeros_like(dws_ref)
            dbc_ref[...] = jnp.zeros_like(dbc_ref)
            dlng_ref[...] = jnp.zeros_like(dlng_ref)
            dcw_ref[...] = jnp.zeros_like(dcw_ref)

        xv = x_ref[...]
        gam = gam_ref[...]
        h, r = _rms(xv, gam)
        h_ref[...] = h.astype(BF16)
        dx1 = dx1_ref[...]
        dmix = _dot(dx1.astype(BF16), wout_ref[...].reshape(D, D), 1, 1)
        da, db = dmix[:, :SG_W], dmix[:, SG_W:]
        proj = proj_ref[...].astype(F32)
        u, v = proj[:, 0:SG_W], proj[:, SG_W:2 * SG_W]
        bg, cg, hv = proj[:, 1024:1536], proj[:, 1536:2048], proj[:, 2048:2560]
        gu, gv = _gelu(u), _gelu(v)

        a_heads, dgv_heads = [], []
        for hd in range(SG_HEADS):
            sl = slice(hd * SG_HD, (hd + 1) * SG_HD)
            g_h = lng_ref[:, sl]
            vn, xh, rr = _ln_head(gv[:, sl], g_h)
            vnb = vn.astype(BF16)
            wm = _tril_bf16(ws_ref[hd])
            bcol = bst_ref[:, hd:hd + 1]
            mixed_c, dvn_c = [], []
            dw_acc = jnp.zeros((128, 128), F32)
            db_acc = jnp.zeros((128, 1), F32)
            for k in range(TB // SG_CHUNK):
                rs = slice(k * SG_CHUNK, (k + 1) * SG_CHUNK)
                mixed = _dot(wm, vnb[rs], 1, 0) + bcol
                dmixed = da[rs, sl] * gu[rs, sl]
                dmb = dmixed.astype(BF16)
                dvn_c.append(_dot(wm, dmb, 0, 0))
                dw_acc = dw_acc + _dot(dmb, vnb[rs], 1, 1)
                db_acc = db_acc + jnp.sum(dmixed, axis=1, keepdims=True)
                mixed_c.append(mixed)
            mixed_h = jnp.concatenate(mixed_c, axis=0)
            dvn = jnp.concatenate(dvn_c, axis=0)
            r_i = lax.broadcasted_iota(jnp.int32, (128, 128), 0)
            c_i = lax.broadcasted_iota(jnp.int32, (128, 128), 1)
            dws_ref[hd] += jnp.where(r_i >= c_i, dw_acc, 0.0)
            dbc_ref[:, hd:hd + 1] += db_acc
            dlng_ref[:, sl] += jnp.sum(dvn * xh, axis=0, keepdims=True)
            dxh = dvn * g_h
            dgv = rr * (dxh - jnp.mean(dxh, axis=-1, keepdims=True)
                        - xh * jnp.mean(dxh * xh, axis=-1, keepdims=True))
            a_heads.append(gu[:, sl] * mixed_h)
            dproj_ref[:, sl] = (da[:, sl] * mixed_h * _gelu_grad(u[:, sl])).astype(BF16)
            dgv_heads.append(dgv * _gelu_grad(v[:, sl]))
        dproj_ref[:, SG_W:2 * SG_W] = jnp.concatenate(dgv_heads, axis=1).astype(BF16)
        mix_ref[:, :SG_W] = jnp.concatenate(a_heads, axis=1).astype(BF16)

        z = cg * hv
        pt = ptail_ref[...].astype(F32)
        tail = jnp.where(blk % nbs == 0, 0.0, pt[:, 1536:2048] * pt[:, 2048:2560])
        y, zs1, zs2 = _conv_fwd(z, tail, cw_ref)
        mix_ref[:, SG_W:] = (bg * y).astype(BF16)
        dy = db * bg
        head = jnp.where(blk % nbs == nbs - 1, 0.0, head_ref[...])
        ext = jnp.concatenate([dy, head], axis=0)
        dz = (cw_ref[2:3, :] * dy + cw_ref[1:2, :] * _shift_up(ext, 1)[:TB]
              + cw_ref[0:1, :] * _shift_up(ext, 2)[:TB])
        head_ref[...] = dy[:HALO, :]
        dcw_ref[2:3, :] += jnp.sum(dy * z, axis=0, keepdims=True)
        dcw_ref[1:2, :] += jnp.sum(dy * zs1, axis=0, keepdims=True)
        dcw_ref[0:1, :] += jnp.sum(dy * zs2, axis=0, keepdims=True)
        dproj_ref[:, 1024:1536] = (db * y).astype(BF16)
        dproj_ref[:, 1536:2048] = (dz * hv).astype(BF16)
        dproj_ref[:, 2048:2560] = (dz * cg).astype(BF16)

        dh = _dot(dproj_ref[...], win_ref[...].reshape(EVEN_IN, D), 1, 0)
        dxn, dgam = _rms_bwd(xv, r, gam, dh)
        dgam_ref[...] += dgam
        dx0_ref[...] = dx1 + dxn

    def rev(w):
        return pl.BlockSpec((TB, w), lambda i: (nb - 1 - i, 0))

    ptail = pl.BlockSpec((HALO, EVEN_IN), lambda i: (jnp.maximum((nb - 1 - i) * (TB // HALO) - 1, 0), 0))
    return _call(
        body, "even_bwd", (nb,),
        [rev(D), rev(EVEN_IN), ptail, rev(D), _const_spec((1, D)), _wspec(N_EIN, OFF_EIN),
         _wspec(N_SQ, OFF_EOUT), _const_spec((1, SG_W)), _const_spec((SG_HEADS, 128, 128)),
         _const_spec((128, 128)), _const_spec((8, SC_W))],
        [rev(D), rev(EVEN_IN), rev(D), rev(D), _const_spec((1, D)), _const_spec((SG_HEADS, 128, 128)),
         _const_spec((128, 128)), _const_spec((1, SG_W)), _const_spec((8, SC_W))],
        [jax.ShapeDtypeStruct((t, D), F32), jax.ShapeDtypeStruct((t, EVEN_IN), BF16),
         jax.ShapeDtypeStruct((t, D), BF16), jax.ShapeDtypeStruct((t, D), BF16),
         jax.ShapeDtypeStruct((1, D), F32), jax.ShapeDtypeStruct((SG_HEADS, 128, 128), F32),
         jax.ShapeDtypeStruct((128, 128), F32), jax.ShapeDtypeStruct((1, SG_W), F32),
         jax.ShapeDtypeStruct((8, SC_W), F32)],
        (x, proj, proj, dx1, gamma, wg, wg, lng, ws, bst, cw), [pltpu.VMEM((HALO, SC_W), F32)], comm)


def _ffn_fwd(x, wg, gamma, name, comm=None):
    t = x.shape[0]

    def body(x_ref, gam_ref, wg_ref, wu_ref, wd_ref, y_ref, g_ref, u_ref):
        xv = x_ref[...]
        h, _ = _rms(xv, gam_ref[...])
        hb = h.astype(BF16)
        g = _dot(hb, wg_ref[...].reshape(D_FF, D), 1, 1)
        u = _dot(hb, wu_ref[...].reshape(D_FF, D), 1, 1)
        g_ref[...] = g.astype(BF16)
        u_ref[...] = u.astype(BF16)
        act = g * jax.nn.sigmoid(g) * u
        y_ref[...] = xv + _dot(act.astype(BF16), wd_ref[...].reshape(D_FF, D), 1, 0)

    row = pl.BlockSpec((TB, D), lambda i: (i, 0))
    wide = pl.BlockSpec((TB, D_FF), lambda i: (i, 0))
    return _call(
        body, name, (t // TB,),
        [row, _const_spec((1, D)), _wspec(N_FF, OFF_GATE), _wspec(N_FF, OFF_UP), _wspec(N_FF, OFF_DOWN)],
        [row, wide, wide],
        [jax.ShapeDtypeStruct((t, D), F32), jax.ShapeDtypeStruct((t, D_FF), BF16),
         jax.ShapeDtypeStruct((t, D_FF), BF16)],
        (x, gamma, wg, wg, wg), (), comm)


def _ffn_bwd(x, g, u, dy, wg, gamma, name, comm=None):
    t = x.shape[0]

    def body(x_ref, g_ref, u_ref, dy_ref, gam_ref, wg_ref, wu_ref, wd_ref,
             dx_ref, act_ref, dg_ref, du_ref, h_ref, dgam_ref):
        @pl.when(pl.program_id(0) == 0)
        def _():
            dgam_ref[...] = jnp.zeros_like(dgam_ref)

        xv = x_ref[...]
        gam = gam_ref[...]
        h, r = _rms(xv, gam)
        h_ref[...] = h.astype(BF16)
        dyv = dy_ref[...]
        dact = _dot(dyv.astype(BF16), wd_ref[...].reshape(D_FF, D), 1, 1)
        gv = g_ref[...].astype(F32)
        uv = u_ref[...].astype(F32)
        sg = jax.nn.sigmoid(gv)
        silu = gv * sg
        act_ref[...] = (silu * uv).astype(BF16)
        dgb = (dact * uv * (sg * (1.0 + gv * (1.0 - sg)))).astype(BF16)
        dub = (dact * silu).astype(BF16)
        dg_ref[...] = dgb
        du_ref[...] = dub
        dh = _dot(dgb, wg_ref[...].reshape(D_FF, D), 1, 0) + _dot(dub, wu_ref[...].reshape(D_FF, D), 1, 0)
        dxn, dgam = _rms_bwd(xv, r, gam, dh)
        dgam_ref[...] += dgam
        dx_ref[...] = dyv + dxn

    row = pl.BlockSpec((TB_FFN_BWD, D), lambda i: (i, 0))
    wide = pl.BlockSpec((TB_FFN_BWD, D_FF), lambda i: (i, 0))
    return _call(
        body, name, (t // TB_FFN_BWD,),
        [row, wide, wide, row, _const_spec((1, D)), _wspec(N_FF, OFF_GATE), _wspec(N_FF, OFF_UP),
         _wspec(N_FF, OFF_DOWN)],
        [row, wide, wide, wide, row, _const_spec((1, D))],
        [jax.ShapeDtypeStruct((t, D), F32), jax.ShapeDtypeStruct((t, D_FF), BF16),
         jax.ShapeDtypeStruct((t, D_FF), BF16), jax.ShapeDtypeStruct((t, D_FF), BF16),
         jax.ShapeDtypeStruct((t, D), BF16), jax.ShapeDtypeStruct((1, D), F32)],
        (x, g, u, dy, gamma, wg, wg, wg), (), comm)


def _odd_pre_fwd(x, wg, gamma, qbt, kvbt, qa_g, kva_g, pw_bd, pscale, seq):
    t = x.shape[0]
    nbs = seq // TB

    def body(x_ref, gam_ref, win_ref, qb_ref, kvb_ref, qa_ref, kva_ref, pw_ref, ps_ref,
             proj_ref, q_ref, kv_ref, kr_ref, c_ref, tail_ref):
        i = pl.program_id(0)
        h, _ = _rms(x_ref[...], gam_ref[...])
        proj = _dot(h.astype(BF16), win_ref[...].reshape(D, D), 1, 0)
        proj_ref[...] = proj.astype(BF16)
        zp, ql, kvl = proj[:, :POOL_W], proj[:, 256:640], proj[:, 640:896]
        kr_ref[...] = proj[:, 896:1024]
        qn, _ = _rms(ql, qa_ref[...])
        q_ref[...] = _dot(qn.astype(BF16), qb_ref[...], 1, 1).astype(BF16)
        kvn, _ = _rms(kvl, kva_ref[...])
        kv_ref[...] = _dot(kvn.astype(BF16), kvb_ref[...], 1, 1).astype(BF16)
        tail = jnp.where(i % nbs == 0, 0.0, tail_ref[...])
        pooled, _, _ = _pool_fwd(zp, tail, i % nbs)
        tail_ref[...] = zp[TB - HALO:, :]
        c_ref[...] = (_dot(pooled.astype(BF16), pw_ref[...], 1, 0) * ps_ref[...]).astype(BF16)

    def row(w):
        return pl.BlockSpec((TB, w), lambda i: (i, 0))

    return pl.pallas_call(
        body, name="odd_pre_fwd",
        out_shape=[jax.ShapeDtypeStruct((t, D), BF16), jax.ShapeDtypeStruct((t, HEADS * HP), BF16),
                   jax.ShapeDtypeStruct((t, HEADS * HP), BF16), jax.ShapeDtypeStruct((t, 128), F32),
                   jax.ShapeDtypeStruct((t, POOL_W), BF16)],
        grid=(t // TB,),
        in_specs=[row(D), _const_spec((1, D)), _wspec(N_SQ, OFF_OIN), _const_spec((HEADS * HP, Q_LORA)),
                  _const_spec((HEADS * HP, KV_LORA)), _const_spec((1, Q_LORA)), _const_spec((1, KV_LORA)),
                  _const_spec((POOL_W, POOL_W)), _const_spec((1, POOL_W))],
        out_specs=[row(D), row(HEADS * HP), row(HEADS * HP), row(128), row(POOL_W)],
        scratch_shapes=[pltpu.VMEM((HALO, POOL_W), F32)],
        compiler_params=_cparams(1),
    )(x, gamma, wg, qbt, kvbt, qa_g, kva_g, pw_bd, pscale)


def _odd_pre_bwd(x, proj, dx3, dmix, dq, dkv, dkr, wg, gamma, qbt, kvbt, qa_g, kva_g, pw_bd, pscale, seq):
    t = x.shape[0]
    nb, nbs = t // TB, seq // TB

    def body(x_ref, proj_ref, ptail_ref, dx3_ref, dco_ref, dq_ref, dkv_ref, dkr_ref, gam_ref, win_ref, qb_ref,
             kvb_ref, qa_ref, kva_ref, pw_ref, ps_ref,
             dx2_ref, dproj_ref, h_ref, qn_ref, kvn_ref, dgam_ref, dqa_ref, dkva_ref, dpw_ref, dps_ref, head_ref):
        i = pl.program_id(0)
        blk = nb - 1 - i

        @pl.when(i == 0)
        def _():
            dgam_ref[...] = jnp.zeros_like(dgam_ref)
            dqa_ref[...] = jnp.zeros_like(dqa_ref)
            dkva_ref[...] = jnp.zeros_like(dkva_ref)
            dpw_ref[...] = jnp.zeros_like(dpw_ref)
            dps_ref[...] = jnp.zeros_like(dps_ref)

        xv = x_ref[...]
        gam = gam_ref[...]
        h, r = _rms(xv, gam)
        h_ref[...] = h.astype(BF16)
        proj = proj_ref[...].astype(F32)
        zp, ql, kvl = proj[:, :POOL_W], proj[:, 256:640], proj[:, 640:896]

        qa = qa_ref[...]
        qn, rq = _rms(ql, qa)
        qn_ref[...] = qn.astype(BF16)
        dql, dqa = _rms_bwd(ql, rq, qa, _dot(dq_ref[...], qb_ref[...], 1, 0))
        dqa_ref[...] += dqa
        kva = kva_ref[...]
        kvn, rkv = _rms(kvl, kva)
        kvn_ref[...] = kvn.astype(BF16)
        dkvl, dkva = _rms_bwd(kvl, rkv, kva, _dot(dkv_ref[...], kvb_ref[...], 1, 0))
        dkva_ref[...] += dkva

        pt = ptail_ref[...].astype(F32)
        tail = jnp.where(blk % nbs == 0, 0.0, pt[:, :POOL_W])
        pooled, cnt, grp = _pool_fwd(zp, tail, blk % nbs)
        pb = pooled.astype(BF16)
        pw = pw_ref[...]
        dco = dco_ref[...].astype(F32)
        dps_ref[...] += jnp.sum(dco * _dot(pb, pw, 1, 0), axis=0, keepdims=True)
        dpo = (dco * ps_ref[...]).astype(BF16)
        dpw_ref[...] += _dot(pb, dpo, 0, 0)
        dpooled = _dot(dpo, pw, 1, 1)
        dpm = dpooled / cnt
        head = jnp.where(blk % nbs == nbs - 1, 0.0, head_ref[...])
        dz = _pool_bwd(dpooled, dpm, head, grp)
        head_ref[...] = dpm[:HALO, :]

        dproj_ref[:, :POOL_W] = dz.astype(BF16)
        dproj_ref[:, 256:640] = dql.astype(BF16)
        dproj_ref[:, 640:896] = dkvl.astype(BF16)
        dproj_ref[:, 896:1024] = dkr_ref[...].astype(BF16)
        dh = _dot(dproj_ref[...], win_ref[...].reshape(D, D), 1, 1)
        dxn, dgam = _rms_bwd(xv, r, gam, dh)
        dgam_ref[...] += dgam
        dx2_ref[...] = dx3_ref[...] + dxn

    def rev(w):
        return pl.BlockSpec((TB, w), lambda i: (nb - 1 - i, 0))

    ptail = pl.BlockSpec((HALO, D), lambda i: (jnp.maximum((nb - 1 - i) * (TB // HALO) - 1, 0), 0))
    return pl.pallas_call(
        body, name="odd_pre_bwd",
        out_shape=[jax.ShapeDtypeStruct((t, D), F32), jax.ShapeDtypeStruct((t, D), BF16),
                   jax.ShapeDtypeStruct((t, D), BF16), jax.ShapeDtypeStruct((t, Q_LORA), BF16),
                   jax.ShapeDtypeStruct((t, KV_LORA), BF16), jax.ShapeDtypeStruct((1, D), F32),
                   jax.ShapeDtypeStruct((1, Q_LORA), F32), jax.ShapeDtypeStruct((1, KV_LORA), F32),
                   jax.ShapeDtypeStruct((POOL_W, POOL_W), F32), jax.ShapeDtypeStruct((1, POOL_W), F32)],
        grid=(nb,),
        in_specs=[rev(D), rev(D), ptail, rev(D), rev(POOL_W), rev(HEADS * HP), rev(HEADS * HP), rev(128),
                  _const_spec((1, D)), _wspec(N_SQ, OFF_OIN), _const_spec((HEADS * HP, Q_LORA)),
                  _const_spec((HEADS * HP, KV_LORA)), _const_spec((1, Q_LORA)), _const_spec((1, KV_LORA)),
                  _const_spec((POOL_W, POOL_W)), _const_spec((1, POOL_W))],
        out_specs=[rev(D), rev(D), rev(D), rev(Q_LORA), rev(KV_LORA), _const_spec((1, D)), _const_spec((1, Q_LORA)),
                   _const_spec((1, KV_LORA)), _const_spec((POOL_W, POOL_W)), _const_spec((1, POOL_W))],
        scratch_shapes=[pltpu.VMEM((HALO, POOL_W), F32)],
        compiler_params=_cparams(1),
    )(x, proj, proj, dx3, dmix, dq, dkv, dkr, gamma, wg, qbt, kvbt, qa_g, kva_g, pw_bd, pscale)


def _attn_specs(seq):
    head = pl.BlockSpec((seq, HP), lambda b, h: (b, h))
    shared = pl.BlockSpec((seq, 128), lambda b, h: (b, 0))
    gain = pl.BlockSpec((1, HP), lambda b, h: (0, 0))
    return head, shared, gain


def _causal_probs(qf, kf, q0, q1):
    s = _dot(qf[q0:q1], kf[:q1], 1, 1) * ATT_SCALE
    rows = lax.broadcasted_iota(jnp.int32, s.shape, 0) + q0
    cols = lax.broadcasted_iota(jnp.int32, s.shape, 1)
    s = jnp.where(cols <= rows, s, NEG_INF)
    p = jnp.exp(s - jnp.max(s, axis=-1, keepdims=True))
    return p, jnp.sum(p, axis=-1, keepdims=True)


def _attn_fwd(q, kv, kr, cos, s1, s2, gq, gk, seq, comm=None):
    t = q.shape[0]
    qb = min(512, seq)

    def body(q_ref, kv_ref, kr_ref, c_ref, s1_ref, s2_ref, gq_ref, gk_ref, o_ref):
        c, sa, sb = c_ref[...], s1_ref[...], s2_ref[...]
        qf, _ = _qk_prep(q_ref[...].astype(F32), gq_ref[...], c, sa, sb)
        kin = jnp.concatenate([kv_ref[:, :128].astype(F32), kr_ref[...]], axis=1)
        kf, _ = _qk_prep(kin, gk_ref[...], c, sa, sb)
        qf, kf = qf.astype(BF16), kf.astype(BF16)
        vb = kv_ref[:, 128:]
        for q0 in range(0, seq, qb):
            q1 = q0 + qb
            p, l = _causal_probs(qf, kf, q0, q1)
            o_ref[q0:q1, :] = (_dot(p.astype(BF16), vb[:q1], 1, 0) / l).astype(BF16)

    head, shared, gain = _attn_specs(seq)
    return _call(
        body, "attn_fwd", (t // seq, HEADS),
        [head, head, shared, shared, shared, shared, gain, gain],
        [pl.BlockSpec((seq, V_DIM), lambda b, h: (b, h))],
        [jax.ShapeDtypeStruct((t, HEADS * V_DIM), BF16)],
        (q, kv, kr, cos, s1, s2, gq, gk), (), comm)


def _attn_bwd(q, kv, kr, cos, s1, s2, gq, gk, dmix, seq, comm=None):
    t = q.shape[0]
    qb = min(512, seq)

    def body(q_ref, kv_ref, kr_ref, c_ref, s1_ref, s2_ref, gq_ref, gk_ref, do_ref,
             dq_ref, dkv_ref, dkr_ref, dgq_ref, dgk_ref, dqf_ref, dkf_ref, dv_ref):
        b, hd = pl.program_id(0), pl.program_id(1)

        @pl.when((b == 0) & (hd == 0))
        def _():
            dgq_ref[...] = jnp.zeros_like(dgq_ref)
            dgk_ref[...] = jnp.zeros_like(dgk_ref)

        c, sa, sb = c_ref[...], s1_ref[...], s2_ref[...]
        gq_v, gk_v = gq_ref[...], gk_ref[...]
        qin = q_ref[...].astype(F32)
        kin = jnp.concatenate([kv_ref[:, :128].astype(F32), kr_ref[...]], axis=1)
        qf32, rq = _qk_prep(qin, gq_v, c, sa, sb)
        kf32, rk = _qk_prep(kin, gk_v, c, sa, sb)
        qf, kf = qf32.astype(BF16), kf32.astype(BF16)
        vb = kv_ref[:, 128:]
        dkf_ref[...] = jnp.zeros_like(dkf_ref)
        dv_ref[...] = jnp.zeros_like(dv_ref)
        for q0 in range(0, seq, qb):
            q1 = q0 + qb
            p, l = _causal_probs(qf, kf, q0, q1)
            p = p / l
            pbf = p.astype(BF16)
            do = do_ref[q0:q1, :]
            dv_ref[:q1, :] += _dot(pbf, do, 0, 0)
            dp = _dot(do, vb[:q1], 1, 1)
            ds = (p * (dp - jnp.sum(p * dp, axis=-1, keepdims=True)) * ATT_SCALE).astype(BF16)
            dqf_ref[q0:q1, :] = _dot(ds, kf[:q1], 1, 0)
            dkf_ref[:q1, :] += _dot(ds, qf[q0:q1], 0, 0)
        dqin, dgq = _qk_prep_bwd(dqf_ref[...], qin, rq, gq_v, c, sa, sb)
        dkin, dgk = _qk_prep_bwd(dkf_ref[...], kin, rk, gk_v, c, sa, sb)
        dgq_ref[...] += dgq
        dgk_ref[...] += dgk
        dq_ref[...] = dqin.astype(BF16)
        dkv_ref[:, :128] = dkin[:, :128].astype(BF16)
        dkv_ref[:, 128:] = dv_ref[...].astype(BF16)

        @pl.when(hd == 0)
        def _():
            dkr_ref[...] = dkin[:, 128:]

        @pl.when(hd != 0)
        def _():
            dkr_ref[...] += dkin[:, 128:]

    head, shared, gain = _attn_specs(seq)
    return _call(
        body, "attn_bwd", (t // seq, HEADS),
        [head, head, shared, shared, shared, shared, gain, gain,
         pl.BlockSpec((seq, V_DIM), lambda b, h: (b, 2 + h))],
        [head, head, shared, gain, gain],
        [jax.ShapeDtypeStruct((t, HEADS * HP), BF16), jax.ShapeDtypeStruct((t, HEADS * HP), BF16),
         jax.ShapeDtypeStruct((t, 128), F32), jax.ShapeDtypeStruct((1, HP), F32),
         jax.ShapeDtypeStruct((1, HP), F32)],
        (q, kv, kr, cos, s1, s2, gq, gk, dmix),
        [pltpu.VMEM((seq, HP), F32), pltpu.VMEM((seq, HP), F32), pltpu.VMEM((seq, V_DIM), F32)], comm)


def _odd_post_fwd(x, c_out, d_out, wg):
    t = x.shape[0]

    def body(x_ref, c_ref, d_ref, w_ref, y_ref):
        y_ref[...] = (x_ref[...] + _dot(c_ref[...], w_ref[0:2].reshape(POOL_W, D), 1, 0)
                      + _dot(d_ref[...], w_ref[2:8].reshape(HEADS * V_DIM, D), 1, 0))

    def row(w):
        return pl.BlockSpec((TB, w), lambda i: (i, 0))

    return pl.pallas_call(
        body, name="odd_post_fwd", out_shape=jax.ShapeDtypeStruct((t, D), F32), grid=(t // TB,),
        in_specs=[row(D), row(POOL_W), row(HEADS * V_DIM), _wspec(N_SQ, OFF_OOUT)], out_specs=row(D),
        compiler_params=_cparams(1),
    )(x, c_out, d_out, wg)


def _odd_post_bwd(dx3, wg, comm=None):
    t = dx3.shape[0]

    def body(d_ref, w_ref, o_ref):
        o_ref[...] = _dot(d_ref[...].astype(BF16), w_ref[...].reshape(D, D), 1, 1).astype(BF16)

    row = pl.BlockSpec((TB, D), lambda i: (i, 0))
    (res,), extra = _call(body, "odd_post_bwd", (t // TB,), [row, _wspec(N_SQ, OFF_OOUT)], [row],
                          [jax.ShapeDtypeStruct((t, D), BF16)], (dx3, wg), (), comm)
    return res, extra


def _loss_grad(y, target):
    t = y.shape[0]

    def body(y_ref, t_ref, dy_ref, loss_ref):
        @pl.when(pl.program_id(0) == 0)
        def _():
            loss_ref[...] = jnp.zeros_like(loss_ref)

        err = y_ref[...] - t_ref[...]
        dy_ref[...] = err * (1.0 / D)
        sq = jnp.sum(jnp.sum(err * err, axis=-1, keepdims=True), axis=0, keepdims=True)
        loss_ref[...] += (0.5 / D) * sq

    row = pl.BlockSpec((TB, D), lambda i: (i, 0))
    return pl.pallas_call(
        body, name="loss_grad",
        out_shape=[jax.ShapeDtypeStruct((t, D), F32), jax.ShapeDtypeStruct((8, 128), F32)], grid=(t // TB,),
        in_specs=[row, row], out_specs=[row, _const_spec((8, 128))], compiler_params=_cparams(1),
    )(y, target)


def _tn(a_list, b, tm, name, into=None, comm=None):
    t, n_out = b.shape
    widths = [a.shape[1] for a in a_list]
    m, na, nk = sum(widths), len(a_list), t // TB
    assert na == 1 or tm == m

    def body(*refs):
        a_refs, b_ref, o_ref, acc_ref = refs[:na], refs[na], refs[-2], refs[-1]
        k = pl.program_id(1)

        @pl.when(k == 0)
        def _():
            acc_ref[...] = jnp.zeros_like(acc_ref)

        bb = b_ref[...].astype(BF16)
        m0 = 0
        for a_ref, w in zip(a_refs, widths):
            rows = slice(0, tm) if na == 1 else slice(m0, m0 + w)
            acc_ref[rows, :] += _dot(a_ref[...].astype(BF16), bb, 0, 0)
            m0 += w

        @pl.when(k == nk - 1)
        def _():
            o_ref[...] = acc_ref[...].astype(BF16).reshape(o_ref.shape)

    if na == 1:
        in_specs = [pl.BlockSpec((TB, tm), lambda i, k: (k, i))]
    else:
        in_specs = [pl.BlockSpec((TB, w), lambda i, k: (k, 0)) for w in widths]
    in_specs.append(pl.BlockSpec((TB, n_out), lambda i, k: (k, 0)))
    args = list(a_list) + [b]
    if into is None:
        out_spec = pl.BlockSpec((tm, n_out), lambda i, k: (i, 0))
        out_shape = jax.ShapeDtypeStruct((m, n_out), BF16)
        aliases = {}
    else:
        buf, n, off = into
        assert n_out == D and tm % n == 0 and off % n == 0 and (na == 1 or tm // n == N_DEV)
        idx = off // n
        out_spec = pl.BlockSpec((tm // n, n, D), lambda i, k: (i, idx, 0))
        out_shape = jax.ShapeDtypeStruct(buf.shape, BF16)
        in_specs.append(pl.BlockSpec(memory_space=pl.ANY))
        args.append(buf)
        aliases = {len(args) - 1: 0}
    (res,), extra = _call(body, name, (m // tm, nk), in_specs, [out_spec], [out_shape], args,
                          [pltpu.VMEM((tm, n_out), F32)], comm, aliases)
    return (res, extra) if comm is not None else res


def _adamw(ws, gs, ms, vs, name, nblk=1, comm=None):
    n = len(ws)
    c1 = 1.0 - B1 ** STEP
    c2 = 1.0 - B2 ** STEP

    def body(*refs):
        for a in range(n):
            w, g, m, v = (refs[k * n + a][...] for k in range(4))
            d_ref, m_ref, v_ref = (refs[(4 + k) * n + a] for k in range(3))
            m_new = B1 * m + (1.0 - B1) * g
            v_new = B2 * v + (1.0 - B2) * (g * g)
            d_ref[...] = -LR * ((m_new / c1) / (jnp.sqrt(v_new / c2) + ADAM_EPS) + WD * w)
            m_ref[...] = m_new
            v_ref[...] = v_new

    grid = (nblk,)
    assert all(w.shape[0] % nblk == 0 and (nblk == 1 or (w.shape[0] // nblk) % 8 == 0) for w in ws)
    specs = [pl.BlockSpec((w.shape[0] // nblk, w.shape[1]), lambda i: (i, 0)) for w in ws]
    outs, extra = _call(body, name, grid, specs * 4, specs * 3, [jax.ShapeDtypeStruct(w.shape, F32) for w in ws] * 3,
                        (*ws, *gs, *ms, *vs), (), comm)
    res = (outs[:n], outs[n:2 * n], outs[2 * n:])
    return (res, extra) if comm is not None else res


def _rows1024(a, rows):
    flat = a.reshape(-1, D)
    return jnp.pad(flat, ((0, rows - flat.shape[0]), (0, 0)))


def _pack_shards(even_w_in, even_w_out, odd_w_in, q_b, kv_b, odd_w_out, ffn_w_gate, ffn_w_up, ffn_w_down):
    mix0 = jnp.concatenate([even_w_in[0].T, jnp.zeros((OFF_EOUT - N_EIN, D), F32), even_w_out[0]], axis=0)
    ffn = [jnp.concatenate([ffn_w_gate[layer].T, ffn_w_up[layer].T, ffn_w_down[layer]], axis=0)
           for layer in range(2)]
    mix1 = jnp.concatenate([jnp.pad(odd_w_in[0], ((0, 0), (0, D - ODD_IN))), odd_w_out[0],
                            _rows1024(q_b[0].T, N_QB), _rows1024(kv_b[0].T, N_KVB),
                            jnp.zeros((R_MIX1 - OFF_KVB - N_KVB, D), F32)], axis=0)
    return [c.astype(BF16) for c in (mix0, ffn[0], mix1, ffn[1])]


def _pad_heads(a):
    k = a.shape[1]
    return jnp.pad(a.reshape(HEADS, QK_DIM, k), ((0, 0), (0, HP - QK_DIM), (0, 0))).reshape(HEADS * HP, k)


def _small_pack(parts):
    flat = []
    for p in parts:
        v = p.reshape(-1)
        flat.append(jnp.pad(v, (0, (-v.shape[0]) % 1024)))
    return jnp.concatenate(flat).reshape(-1, 128)


def _small_unpack(buf, shapes):
    flat = buf.reshape(-1)
    out, off = [], 0
    for s in shapes:
        size = int(np.prod(s))
        out.append(flat[off:off + size].reshape(s))
        off += size + (-size) % 1024
    return out


def _step(x3d, positions, target3d, chunks, tile, c_arr, chip_arr, mix_norm, ffn_norm, sg_ln_g, sg_w_s, sg_b_s,
          pool_w, q_norm, k_norm):
    bsz, seq, _ = x3d.shape
    t = bsz * seq
    x0 = x3d.reshape(t, D)
    target = target3d.reshape(t, D)
    my_mix0, my_ffn0, my_mix1, my_ffn1 = chunks

    lane = np.arange(128)
    inv_freq = np.where(lane < QK_ROPE, ROPE_THETA ** (-(2.0 * (lane % 32)) / QK_ROPE), 0.0)
    inv_freq = jnp.asarray(inv_freq.reshape(1, 128), F32)
    (cos, s1, s2), (w_mix0, tiles) = _rope_tables(positions.reshape(t, 1), inv_freq, _gather_comm([my_mix0, tile]))

    conv_w = tiles[:, 0:3, 0:64].transpose(1, 0, 2).reshape(3, SC_W)
    pool_scale = tiles[:, 3, 0:32].reshape(1, POOL_W)
    q_a_norm = tiles[:, 4, 0:48].reshape(1, Q_LORA)
    kv_a_norm = tiles[:, 5, 0:32].reshape(1, KV_LORA)
    ws = sg_w_s[0]
    bst = jnp.pad(sg_b_s[0].T, ((0, 0), (0, 128 - SG_HEADS)))
    cw = jnp.pad(conv_w, ((0, 8 - 3), (0, 0)))
    pw_bd = jax.scipy.linalg.block_diag(*[pool_w[0, g] for g in range(4)]).astype(BF16)
    gq = jnp.pad(q_norm, ((0, 0), (0, HP - QK_DIM)))
    gk = jnp.pad(k_norm, ((0, 0), (0, HP - QK_DIM)))

    (x1, proj_e), (w_ffn0,) = _even_fwd(x0, w_mix0, mix_norm[0:1], sg_ln_g, ws, bst, cw, seq,
                                        _gather_comm([my_ffn0]))
    (x2, g0, u0), (w_mix1,) = _ffn_fwd(x1, w_ffn0, ffn_norm[0:1], "ffn_fwd0", _gather_comm([my_mix1]))
    qbt = _pad_heads(w_mix1[:, OFF_QB:OFF_QB + N_QB_USED, :].reshape(HEADS * QK_DIM, Q_LORA))
    kvbt = w_mix1[:, OFF_KVB:OFF_KVB + N_KVB, :].reshape(HEADS * HP, KV_LORA)
    proj_o, q, kv, kr, c_out = _odd_pre_fwd(x2, w_mix1, mix_norm[1:2], qbt, kvbt, q_a_norm, kv_a_norm, pw_bd,
                                            pool_scale, seq)
    (d_out,), (w_ffn1,) = _attn_fwd(q, kv, kr, cos, s1, s2, gq, gk, seq, _gather_comm([my_ffn1]))
    x3 = _odd_post_fwd(x2, c_out, d_out, w_mix1)
    (x4, g1, u1), _ = _ffn_fwd(x3, w_ffn1, ffn_norm[1:2], "ffn_fwd1")
    dy, loss_tile = _loss_grad(x4, target)

    def chunk(rows):
        return jnp.zeros((N_DEV, rows, D), BF16)

    (dx3, act1, dg1, du1, h3, dgam_f1), _ = _ffn_bwd(x3, g1, u1, dy, w_ffn1, ffn_norm[1:2], "ffn_bwd1")
    gp_ffn1 = _tn([dg1], h3, 1408, "dw_gate1", (chunk(R_FFN), N_FF, OFF_GATE))
    gp_ffn1 = _tn([du1], h3, 1408, "dw_up1", (gp_ffn1, N_FF, OFF_UP))
    gp_ffn1 = _tn([act1], dy, 1408, "dw_down1", (gp_ffn1, N_FF, OFF_DOWN))

    dmix_o, (ga_ffn1,) = _odd_post_bwd(dx3, w_mix1, _pair_exchange_comm(gp_ffn1))
    pb_ffn1 = _rs_pair_sum(gp_ffn1, ga_ffn1, c_arr, "rs_pair_sum_ffn1")
    gp_mix1 = _tn([c_out, d_out], dx3, D, "dw_oout", (chunk(R_MIX1), N_SQ, OFF_OOUT))
    (dq, dkv, dkr, dgq, dgk), (gb_ffn1,) = _attn_bwd(q, kv, kr, cos, s1, s2, gq, gk, dmix_o, seq,
                                                    _chip_exchange_comm(pb_ffn1))
    gsh_ffn1 = _rs_final_sum(pb_ffn1, gb_ffn1, chip_arr, "rs_final_sum_ffn1")
    (dx2, dproj_o, h2, qn, kvn, dgam_m1, dqa, dkva, dpw_bd, dps) = _odd_pre_bwd(
        x2, proj_o, dx3, dmix_o, dq, dkv, dkr, w_mix1, mix_norm[1:2], qbt, kvbt, q_a_norm, kv_a_norm, pw_bd,
        pool_scale, seq)
    gp_mix1 = _tn([h2], dproj_o, D, "dw_oin", (gp_mix1, N_SQ, OFF_OIN))
    d_qbt = _tn([dq], qn, HEADS * HP, "dw_qb")
    d_qb_rows = d_qbt.reshape(HEADS, HP, Q_LORA)[:, :QK_DIM].reshape(N_DEV, N_QB_USED, D)
    d_kvb_rows = _tn([dkv], kvn, HEADS * HP, "dw_kvb").reshape(N_DEV, N_KVB, D)
    gp_mix1 = lax.dynamic_update_slice(gp_mix1, d_qb_rows, (0, OFF_QB, 0))
    gp_mix1 = lax.dynamic_update_slice(gp_mix1, d_kvb_rows, (0, OFF_KVB, 0))

    (dx1, act0, dg0, du0, h1, dgam_f0), (ga_mix1,) = _ffn_bwd(x1, g0, u0, dx2, w_ffn0, ffn_norm[0:1], "ffn_bwd0",
                                                             _pair_exchange_comm(gp_mix1))
    pb_mix1 = _rs_pair_sum(gp_mix1, ga_mix1, c_arr, "rs_pair_sum_mix1")
    gp_ffn0a, (gb_mix1,) = _tn([dg0], h1, 1408, "dw_gate0", (chunk(2 * N_FF), N_FF, OFF_GATE),
                               _chip_exchange_comm(pb_mix1))
    gsh_mix1 = _rs_final_sum(pb_mix1, gb_mix1, chip_arr, "rs_final_sum_mix1")
    gp_ffn0a = _tn([du0], h1, 1408, "dw_up0", (gp_ffn0a, N_FF, OFF_UP))
    gp_ffn0b, (ga_ffn0a,) = _tn([act0], dx2, 1408, "dw_down0", (chunk(N_FF), N_FF, 0),
                                _pair_exchange_comm(gp_ffn0a))
    pb_ffn0a = _rs_pair_sum(gp_ffn0a, ga_ffn0a, c_arr, "rs_pair_sum_ffn0a")

    (dx0, dproj_e, mix_e, h0, dgam_m0, dws, dbc, dlng, dcw), (gb_ffn0a, ga_ffn0b) = _even_bwd(
        x0, proj_e, dx1, w_mix0, mix_norm[0:1], sg_ln_g, ws, bst, cw, seq,
        _both(_chip_exchange_comm(pb_ffn0a), _pair_exchange_comm(gp_ffn0b)))
    pb_ffn0b = _rs_pair_sum(gp_ffn0b, ga_ffn0b, c_arr, "rs_pair_sum_ffn0b")
    gsh_ffn0a = _rs_final_sum(pb_ffn0a, gb_ffn0a, chip_arr, "rs_final_sum_ffn0a")
    gp_mix0, (gb_ffn0b,) = _tn([mix_e], dx1, D, "dw_eout", (chunk(R_MIX0), N_SQ, OFF_EOUT),
                               _chip_exchange_comm(pb_ffn0b))
    gsh_ffn0b = _rs_final_sum(pb_ffn0b, gb_ffn0b, chip_arr, "rs_final_sum_ffn0b")

    small = _small_pack([
        jnp.concatenate([dgam_m0, dgam_m1], 0), jnp.concatenate([dgam_f0, dgam_f1], 0), dlng,
        dws[None], dbc[:, :SG_HEADS].T[None], dcw[:3],
        jnp.stack([dpw_bd[g * POOL_GD:(g + 1) * POOL_GD, g * POOL_GD:(g + 1) * POOL_GD] for g in range(4)])[None],
        dps, dqa, dkva, dgq[:, :QK_DIM], dgk[:, :QK_DIM], loss_tile[0:1, 0:1]])
    gp_mix0, (small_all,) = _tn([dproj_e], h0, 1280, "dw_ein", (gp_mix0, N_EIN, OFF_EIN), _gather_comm([small]))
    small_sum = _small_unpack(_sum_gathered(small_all), SMALL_SHAPES)
    return dx0.reshape(bsz, seq, D), (gsh_ffn0a, gsh_ffn0b, gsh_mix1, gsh_ffn1), gp_mix0, small_sum


SMALL_SHAPES = [(2, D), (2, D), (1, SG_W), (1, SG_HEADS, 128, 128), (1, SG_HEADS, 128), (3, SC_W),
                (1, 4, POOL_GD, POOL_GD), (1, POOL_W), (1, Q_LORA), (1, KV_LORA), (1, QK_DIM), (1, QK_DIM), (1, 1)]


def kernel(x, positions, mix_norm, ffn_norm, even_w_in, sg_ln_g, sg_w_s, sg_b_s, sc_conv_w, even_w_out, odd_w_in, pool_w, pool_scale, q_a_norm, q_b, kv_a_norm, kv_b, q_norm, k_norm, odd_w_out, ffn_w_gate, ffn_w_up, ffn_w_down, loss_target, m_mix_norm, m_ffn_norm, m_even_w_in, m_sg_ln_g, m_sg_w_s, m_sg_b_s, m_sc_conv_w, m_even_w_out, m_odd_w_in, m_pool_w, m_pool_scale, m_q_a_norm, m_q_b, m_kv_a_norm, m_kv_b, m_q_norm, m_k_norm, m_odd_w_out, m_ffn_w_gate, m_ffn_w_up, m_ffn_w_down, v_mix_norm, v_ffn_norm, v_even_w_in, v_sg_ln_g, v_sg_w_s, v_sg_b_s, v_sc_conv_w, v_even_w_out, v_odd_w_in, v_pool_w, v_pool_scale, v_q_a_norm, v_q_b, v_kv_a_norm, v_kv_b, v_q_norm, v_k_norm, v_odd_w_out, v_ffn_w_gate, v_ffn_w_up, v_ffn_w_down):
    xi, yi, ci = _place()
    me = 4 * xi + 2 * yi + ci

    chunks = _pack_shards(even_w_in, even_w_out, odd_w_in, q_b, kv_b, odd_w_out, ffn_w_gate, ffn_w_up, ffn_w_down)

    def lane_pad(a):
        return jnp.pad(a, ((0, 0), (0, 128 - a.shape[1])))

    tile = jnp.concatenate([lane_pad(sc_conv_w[0]), lane_pad(pool_scale), lane_pad(q_a_norm), lane_pad(kv_a_norm),
                            jnp.zeros((2, 128), F32)], axis=0)
    c_arr = jnp.reshape(ci, (1,)).astype(jnp.int32)
    chip_arr = jnp.reshape(2 * xi + yi, (1,)).astype(jnp.int32)
    grad_x, (gsh_ffn0a, gsh_ffn0b, gsh_mix1, gsh_ffn1), gp_mix0, tot = _step(
        x, positions, loss_target, chunks, tile, c_arr, chip_arr, mix_norm, ffn_norm, sg_ln_g, sg_w_s, sg_b_s,
        pool_w, q_norm, k_norm)

    (g_mix, g_ffn, g_lng, g_ws, g_bs, g_cw_full, g_pw, g_ps_full, g_qa_full, g_kva_full, g_qn, g_kn, loss) = tot
    g_cw = lax.dynamic_slice_in_dim(g_cw_full, me * 64, 64, axis=1)[None]
    g_ps = lax.dynamic_slice_in_dim(g_ps_full, me * 32, 32, axis=1)
    g_qa = lax.dynamic_slice_in_dim(g_qa_full, me * 48, 48, axis=1)
    g_kva = lax.dynamic_slice_in_dim(g_kva_full, me * 32, 32, axis=1)

    def tr(a):
        return jnp.swapaxes(a, -1, -2)

    g_gate = tr(jnp.stack([gsh_ffn0a[OFF_GATE:OFF_GATE + N_FF], gsh_ffn1[OFF_GATE:OFF_GATE + N_FF]]))
    g_up = tr(jnp.stack([gsh_ffn0a[OFF_UP:OFF_UP + N_FF], gsh_ffn1[OFF_UP:OFF_UP + N_FF]]))
    g_down = jnp.stack([gsh_ffn0b, gsh_ffn1[OFF_DOWN:OFF_DOWN + N_FF]])
    g_oin = gsh_mix1[OFF_OIN:OFF_OIN + N_SQ, :ODD_IN][None]
    g_oout = gsh_mix1[OFF_OOUT:OFF_OOUT + N_SQ][None]
    g_qb = tr(gsh_mix1[OFF_QB:OFF_QB + N_QB_USED].reshape(1, 144, Q_LORA))
    g_kvb = tr(gsh_mix1[OFF_KVB:OFF_KVB + N_KVB].reshape(1, 192, KV_LORA))
    transposed = ("even_w_in", "odd_w_in", "q_b", "kv_b", "ffn_w_gate", "ffn_w_up")

    names = ("mix_norm", "ffn_norm", "even_w_in", "sg_ln_g", "sg_w_s", "sg_b_s", "sc_conv_w", "even_w_out",
             "odd_w_in", "pool_w", "pool_scale", "q_a_norm", "q_b", "kv_a_norm", "kv_b", "q_norm", "k_norm",
             "odd_w_out", "ffn_w_gate", "ffn_w_up", "ffn_w_down")
    grads = dict(mix_norm=g_mix, ffn_norm=g_ffn, sg_ln_g=g_lng, sg_w_s=g_ws, sg_b_s=g_bs,
                 sc_conv_w=g_cw, odd_w_in=g_oin, pool_w=g_pw, pool_scale=g_ps, q_a_norm=g_qa,
                 q_b=g_qb, kv_a_norm=g_kva, kv_b=g_kvb, q_norm=g_qn, k_norm=g_kn, odd_w_out=g_oout,
                 ffn_w_gate=g_gate, ffn_w_up=g_up, ffn_w_down=g_down)
    weights = dict(mix_norm=mix_norm, ffn_norm=ffn_norm, even_w_in=even_w_in, sg_ln_g=sg_ln_g, sg_w_s=sg_w_s,
                   sg_b_s=sg_b_s, sc_conv_w=sc_conv_w, even_w_out=even_w_out, odd_w_in=odd_w_in, pool_w=pool_w,
                   pool_scale=pool_scale, q_a_norm=q_a_norm, q_b=q_b, kv_a_norm=kv_a_norm, kv_b=kv_b, q_norm=q_norm,
                   k_norm=k_norm, odd_w_out=odd_w_out, ffn_w_gate=ffn_w_gate, ffn_w_up=ffn_w_up,
                   ffn_w_down=ffn_w_down)
    m_in = dict(mix_norm=m_mix_norm, ffn_norm=m_ffn_norm, even_w_in=m_even_w_in, sg_ln_g=m_sg_ln_g, sg_w_s=m_sg_w_s,
                sg_b_s=m_sg_b_s, sc_conv_w=m_sc_conv_w, even_w_out=m_even_w_out, odd_w_in=m_odd_w_in,
                pool_w=m_pool_w, pool_scale=m_pool_scale, q_a_norm=m_q_a_norm, q_b=m_q_b, kv_a_norm=m_kv_a_norm,
                kv_b=m_kv_b, q_norm=m_q_norm, k_norm=m_k_norm, odd_w_out=m_odd_w_out, ffn_w_gate=m_ffn_w_gate,
                ffn_w_up=m_ffn_w_up, ffn_w_down=m_ffn_w_down)
    v_in = dict(mix_norm=v_mix_norm, ffn_norm=v_ffn_norm, even_w_in=v_even_w_in, sg_ln_g=v_sg_ln_g, sg_w_s=v_sg_w_s,
                sg_b_s=v_sg_b_s, sc_conv_w=v_sc_conv_w, even_w_out=v_even_w_out, odd_w_in=v_odd_w_in,
                pool_w=v_pool_w, pool_scale=v_pool_scale, q_a_norm=v_q_a_norm, q_b=v_q_b, kv_a_norm=v_kv_a_norm,
                kv_b=v_kv_b, q_norm=v_q_norm, k_norm=v_k_norm, odd_w_out=v_odd_w_out, ffn_w_gate=v_ffn_w_gate,
                ffn_w_up=v_ffn_w_up, ffn_w_down=v_ffn_w_down)
    delta, new_m, new_v = {}, {}, {}

    def as2d(k, a):
        a = tr(a) if k in transposed else a
        return a.reshape(-1, a.shape[-1])

    def back(k, a):
        shape = weights[k].shape
        return tr(a.reshape(shape[:-2] + (shape[-1], shape[-2]))) if k in transposed else a.reshape(shape)

    def update(group, name, nblk=1, comm=None):
        res = _adamw([as2d(k, weights[k]) for k in group], [as2d(k, grads[k]) for k in group],
                     [as2d(k, m_in[k]) for k in group], [as2d(k, v_in[k]) for k in group], name, nblk, comm)
        outs, extra = res if comm is not None else (res, [])
        for i, k in enumerate(group):
            delta[k], new_m[k], new_v[k] = (back(k, o[i]) for o in outs)
        return extra

    (ga_mix0,) = update(["ffn_w_down"], "adamw_ffn_w_down", 2, _pair_exchange_comm(gp_mix0))
    pb_mix0 = _rs_pair_sum(gp_mix0, ga_mix0, c_arr, "rs_pair_sum_mix0")
    (gb_mix0,) = update(["ffn_w_gate", "ffn_w_up", "odd_w_in", "odd_w_out"], "adamw_ffn_odd", 4,
                        _chip_exchange_comm(pb_mix0))
    gsh_mix0 = _rs_final_sum(pb_mix0, gb_mix0, chip_arr, "rs_final_sum_mix0")
    grads["even_w_in"] = tr(gsh_mix0[OFF_EIN:OFF_EIN + N_EIN][None])
    grads["even_w_out"] = gsh_mix0[OFF_EOUT:OFF_EOUT + N_SQ][None]
    update(["even_w_in", "even_w_out"], "adamw_mix0", 2)
    update([k for k in names if k not in delta], "adamw_small")

    return (loss.reshape(()), grad_x, *[grads[k] for k in names], *[delta[k] for k in names],
            *[new_m[k] for k in names], *[new_v[k] for k in names])
```

```python
import functools

import numpy as np
import jax
import jax.numpy as jnp
from jax import lax
from jax.experimental import pallas as pl
from jax.experimental.pallas import tpu as pltpu

F32 = jnp.float32
BF16 = jnp.bfloat16
MESH = pl.DeviceIdType.MESH

D = 1024
EPS = 1e-6
NEG_INF = -1e30
SG_HEADS, SG_HD, SG_W, SG_CHUNK = 4, 128, 512, 128
SC_W = 512
EVEN_IN = 2560
POOL_W = 256
POOL_GD = 64
Q_LORA, KV_LORA, QK_ROPE, QK_NOPE, V_DIM = 384, 256, 64, 128, 128
QK_DIM = QK_NOPE + QK_ROPE
HEADS = 6
HP = 256
ODD_IN = 960
D_FF = 2816
ROPE_THETA = 10000.0
ATT_SCALE = QK_DIM ** -0.5
LR, B1, B2, ADAM_EPS, WD, STEP = 0.001, 0.9, 0.999, 1e-08, 0.01, 10

N_DEV = 8
TB = 512
TB_FFN_BWD = 256
HALO = 16
VMEM_LIMIT = 56 * 1024 * 1024

N_EIN, N_FF, N_SQ = 320, 352, 128
OFF_EIN, OFF_EOUT, R_MIX0 = 0, 384, 512
OFF_GATE, OFF_UP, OFF_DOWN, R_FFN = 0, 352, 704, 1056
OFF_OIN, OFF_OOUT, OFF_QB, OFF_KVB, R_MIX1 = 0, 128, 256, 320, 384
N_QB, N_QB_USED, N_KVB = 64, 54, 48

INV_SQRT2 = 0.7071067811865476
INV_SQRT_2PI = 0.3989422804014327


def _dot(a, b, ca, cb):
    return lax.dot_general(a, b, (((ca,), (cb,)), ((), ())), preferred_element_type=F32)


def _cparams(n_axes=1):
    return pltpu.CompilerParams(dimension_semantics=("arbitrary",) * n_axes, vmem_limit_bytes=VMEM_LIMIT)


def _wspec(n, off, arity=1):
    assert off % n == 0
    idx = off // n
    if arity == 1:
        return pl.BlockSpec((N_DEV, n, D), lambda i: (0, idx, 0), pipeline_mode=pl.Buffered(1))
    return pl.BlockSpec((N_DEV, n, D), lambda i, j: (0, idx, 0), pipeline_mode=pl.Buffered(1))


def _const_spec(shape):
    zeros = (0,) * len(shape)
    return pl.BlockSpec(shape, lambda *_: zeros)


class _Comm:
    def __init__(self, ins, out_shapes, sems, start, wait, mid=None):
        self.ins, self.out_shapes, self.sems, self.start, self.wait, self.mid = ins, out_shapes, sems, start, wait, mid


def _both(c1, c2):
    def split(f1, f2):
        def run(ins, outs, sems):
            f1(ins[:len(c1.ins)], outs[:len(c1.out_shapes)], sems[:len(c1.sems)])
            f2(ins[len(c1.ins):], outs[len(c1.out_shapes):], sems[len(c1.sems):])
        return run

    assert c1.mid is None and c2.mid is None
    return _Comm(c1.ins + c2.ins, c1.out_shapes + c2.out_shapes, c1.sems + c2.sems,
                 split(c1.start, c2.start), split(c1.wait, c2.wait))


def _call(body, name, grid, in_specs, out_specs, out_shape, args, scratch_shapes=(), comm=None, aliases=None):
    n_axes = len(grid)
    aliases = aliases or {}
    if comm is None:
        res = pl.pallas_call(
            body, name=name, grid=grid, in_specs=list(in_specs), out_specs=list(out_specs),
            out_shape=list(out_shape), scratch_shapes=list(scratch_shapes), input_output_aliases=aliases,
            compiler_params=_cparams(n_axes))(*args)
        return list(res), []
    ni, no, ns = len(in_specs), len(out_specs), len(scratch_shapes)
    ci, co = len(comm.ins), len(comm.out_shapes)
    n_steps = int(np.prod(grid))

    def carrier(*refs):
        ins, cin = refs[:ni], refs[ni:ni + ci]
        outs, cout = refs[ni + ci:ni + ci + no], refs[ni + ci + no:ni + ci + no + co]
        scr, sems = refs[ni + ci + no + co:ni + ci + no + co + ns], refs[ni + ci + no + co + ns:]
        step = 0
        for a in range(n_axes):
            step = step * grid[a] + pl.program_id(a)

        @pl.when(step == 0)
        def _():
            comm.start(cin, cout, sems)

        body(*ins, *outs, *scr)

        if comm.mid is not None and n_steps >= 4:
            @pl.when(step == (3 * n_steps) // 4)
            def _():
                comm.mid(cin, cout, sems)

        @pl.when(step == n_steps - 1)
        def _():
            if comm.mid is not None and n_steps < 4:
                comm.mid(cin, cout, sems)
            comm.wait(cin, cout, sems)

    any_spec = pl.BlockSpec(memory_space=pl.ANY)
    res = pl.pallas_call(
        carrier, name=name, grid=grid, in_specs=list(in_specs) + [any_spec] * ci,
        out_specs=list(out_specs) + [any_spec] * co, out_shape=list(out_shape) + list(comm.out_shapes),
        scratch_shapes=list(scratch_shapes) + list(comm.sems), input_output_aliases=aliases,
        compiler_params=_cparams(n_axes))(*args, *comm.ins)
    return list(res[:no]), list(res[no:])


def _comm_alone(comm, name):
    ci, co = len(comm.ins), len(comm.out_shapes)

    def body(*refs):
        cin, cout, sems = refs[:ci], refs[ci:ci + co], refs[ci + co:]
        comm.start(cin, cout, sems)
        if comm.mid is not None:
            comm.mid(cin, cout, sems)
        comm.wait(cin, cout, sems)

    any_spec = pl.BlockSpec(memory_space=pl.ANY)
    res = pl.pallas_call(
        body, name=name, out_shape=list(comm.out_shapes), in_specs=[any_spec] * ci, out_specs=[any_spec] * co,
        scratch_shapes=list(comm.sems))(*comm.ins)
    return list(res)


def _rms(x, g, n=None):
    n = x.shape[-1] if n is None else n
    r = lax.rsqrt(jnp.sum(x * x, axis=-1, keepdims=True) / n + EPS)
    return x * r * g, r


def _rms_bwd(x, r, g, dy, n=None):
    n = x.shape[-1] if n is None else n
    xh = x * r
    dxh = dy * g
    dx = r * (dxh - xh * (jnp.sum(dxh * xh, axis=-1, keepdims=True) / n))
    dg = jnp.sum(dy * xh, axis=0, keepdims=True)
    return dx, dg


def _gelu(x):
    return 0.5 * x * (1.0 + lax.erf(x * INV_SQRT2))


def _gelu_grad(x):
    return 0.5 * (1.0 + lax.erf(x * INV_SQRT2)) + x * jnp.exp(-0.5 * x * x) * INV_SQRT_2PI


def _shift_down(a, k):
    rows = lax.broadcasted_iota(jnp.int32, a.shape, 0)
    return jnp.where(rows >= k, pltpu.roll(a, k, 0), 0.0)


def _shift_up(a, k):
    n = a.shape[0]
    rows = lax.broadcasted_iota(jnp.int32, a.shape, 0)
    return jnp.where(rows < n - k, pltpu.roll(a, n - k, 0), 0.0)


def _tril_bf16(w):
    r = lax.broadcasted_iota(jnp.int32, w.shape, 0)
    c = lax.broadcasted_iota(jnp.int32, w.shape, 1)
    return jnp.where(r >= c, w, 0.0).astype(BF16)


def _ln_head(vh, g):
    mu = jnp.mean(vh, axis=-1, keepdims=True)
    xc = vh - mu
    rr = lax.rsqrt(jnp.mean(xc * xc, axis=-1, keepdims=True) + EPS)
    xh = xc * rr
    return xh * g, xh, rr


def _conv_fwd(z, tail, cw_ref):
    ext = jnp.concatenate([tail, z], axis=0)
    zs1 = _shift_down(ext, 1)[HALO:]
    zs2 = _shift_down(ext, 2)[HALO:]
    y = cw_ref[2:3, :] * z + cw_ref[1:2, :] * zs1 + cw_ref[0:1, :] * zs2
    return y, zs1, zs2


def _pool_cnt(shape, blk_in_seq):
    rows = lax.broadcasted_iota(jnp.int32, shape, 0)
    grp = lax.broadcasted_iota(jnp.int32, shape, 1) // POOL_GD
    win = jnp.where(grp == 0, 2, jnp.where(grp == 1, 4, jnp.where(grp == 2, 8, 16)))
    tpos = blk_in_seq * shape[0] + rows + 1
    return jnp.minimum(tpos, win).astype(F32), grp


def _pool_select(grp, s2, s4, s8, s16):
    return jnp.where(grp == 0, s2, jnp.where(grp == 1, s4, jnp.where(grp == 2, s8, s16)))


def _pool_fwd(z, tail, blk_in_seq):
    ext = jnp.concatenate([tail, z], axis=0)
    s2 = ext + _shift_down(ext, 1)
    s4 = s2 + _shift_down(s2, 2)
    s8 = s4 + _shift_down(s4, 4)
    s16 = s8 + _shift_down(s8, 8)
    cnt, grp = _pool_cnt(z.shape, blk_in_seq)
    sums = _pool_select(grp, s2[HALO:], s4[HALO:], s8[HALO:], s16[HALO:])
    return sums / cnt - z, cnt, grp


def _pool_bwd(dpooled, dpm, head, grp):
    n = dpm.shape[0]
    ext = jnp.concatenate([dpm, head], axis=0)
    u2 = ext + _shift_up(ext, 1)
    u4 = u2 + _shift_up(u2, 2)
    u8 = u4 + _shift_up(u4, 4)
    u16 = u8 + _shift_up(u8, 8)
    return _pool_select(grp, u2[:n], u4[:n], u8[:n], u16[:n]) - dpooled


def _rope(y1, c, s1, s2):
    return y1 * c + pltpu.roll(y1, 96, 1) * s1 + pltpu.roll(y1, 32, 1) * s2


def _rope_bwd(d1, c, s1, s2):
    return d1 * c + pltpu.roll(d1 * s1, 32, 1) + pltpu.roll(d1 * s2, 96, 1)


def _qk_prep(xh_in, g, c, s1, s2):
    y, r = _rms(xh_in, g, QK_DIM)
    out = jnp.concatenate([y[:, :128], _rope(y[:, 128:], c, s1, s2)], axis=1)
    return out, r


def _qk_prep_bwd(dout, x_in, r, g, c, s1, s2):
    dy = jnp.concatenate([dout[:, :128], _rope_bwd(dout[:, 128:], c, s1, s2)], axis=1)
    return _rms_bwd(x_in, r, g, dy, QK_DIM)


def _place():
    return lax.axis_index("x"), lax.axis_index("y"), lax.axis_index("c")


def _gather_comm(arrs):
    n = len(arrs)

    def plan(ins, outs, sems):
        send_sems, recv_sems, local_sems = sems
        x, y, c = _place()
        me, sibling = (x, y, c), (x, y, 1 - c)
        chips = [(1 - x, y), (x, 1 - y), (1 - x, 1 - y)]

        def slot(a, px, py, pc):
            return outs[a].at[4 * px + 2 * py + pc]

        def copy(a, k, block, to, src=None):
            return pltpu.make_async_remote_copy(
                src_ref=slot(a, *block) if src is None else src, dst_ref=slot(a, *block),
                send_sem=send_sems.at[a, k], recv_sem=recv_sems.at[a, k], device_id=to, device_id_type=MESH)

        def own():
            mine = [pltpu.make_async_copy(ins[a], slot(a, *me), local_sems.at[a]) for a in range(n)]
            first = []
            for a in range(n):
                first.append(copy(a, 0, me, sibling, src=ins[a]))
                first += [copy(a, 1 + j, me, (*chip, c), src=ins[a]) for j, chip in enumerate(chips)]
            return mine, first

        return c, me, sibling, chips, copy, own

    def start(ins, outs, sems):
        mine, first = plan(ins, outs, sems)[-1]()
        for cp in mine + first:
            cp.start()

    def mid(ins, outs, sems):
        c, me, sibling, chips, copy, _ = plan(ins, outs, sems)
        for j, chip in enumerate(chips):
            for a in range(n):
                copy(a, 1 + j, (*chip, c), me).wait_recv()
                copy(a, 4 + j, (*chip, c), sibling).start()

    def wait(ins, outs, sems):
        c, me, sibling, chips, copy, own = plan(ins, outs, sems)
        mine, first = own()
        passed = [copy(a, 4 + j, (*chip, c), sibling) for j, chip in enumerate(chips) for a in range(n)]
        for a in range(n):
            copy(a, 0, sibling, me).wait_recv()
            for j, chip in enumerate(chips):
                copy(a, 4 + j, (*chip, 1 - c), me).wait_recv()
        for cp in first + passed:
            cp.wait_send()
        for cp in mine:
            cp.wait()

    return _Comm(
        list(arrs), [jax.ShapeDtypeStruct((N_DEV,) + a.shape, a.dtype) for a in arrs],
        [pltpu.SemaphoreType.DMA((n, 7)), pltpu.SemaphoreType.DMA((n, 7)), pltpu.SemaphoreType.DMA((n,))],
        start, wait, mid)


def _sum_gathered(g):
    rows = g.shape[1]

    def body(g_ref, sum_ref):
        total = g_ref[0]
        for d in range(1, N_DEV):
            total = total + g_ref[d]
        sum_ref[...] = total

    return pl.pallas_call(
        body, name="sum_gathered_small", out_shape=jax.ShapeDtypeStruct((rows, 128), F32), grid=(1,),
        in_specs=[pl.BlockSpec((N_DEV, rows, 128), lambda i: (0, 0, 0))],
        out_specs=pl.BlockSpec((rows, 128), lambda i: (0, 0)), compiler_params=_cparams(1),
    )(g)


def _sum_rows(rows):
    return rows if rows <= 512 else rows // 2


def _pair_exchange_comm(gp):
    _, rows, cols = gp.shape

    def copies(ins, outs, sems):
        send_sems, recv_sems = sems
        x, y, c = _place()
        return [pltpu.make_async_remote_copy(
            src_ref=ins[0].at[2 * j + (1 - c)], dst_ref=outs[0].at[j], send_sem=send_sems.at[j],
            recv_sem=recv_sems.at[j], device_id=(x, y, 1 - c), device_id_type=MESH) for j in range(4)]

    def start(ins, outs, sems):
        for cp in copies(ins, outs, sems):
            cp.start()

    def wait(ins, outs, sems):
        for cp in copies(ins, outs, sems):
            cp.wait()

    return _Comm([gp], [jax.ShapeDtypeStruct((4, rows, cols), gp.dtype)],
                 [pltpu.SemaphoreType.DMA((4,)), pltpu.SemaphoreType.DMA((4,))], start, wait)


def _rs_pair_sum(gp, got, c_arr, name):
    _, rows, cols = got.shape
    rb = _sum_rows(rows)
    gp4 = gp.reshape(4, 2, rows, cols)

    def body(c_ref, a_ref, b_ref, o_ref):
        o_ref[0] = (a_ref[0, 0].astype(F32) + b_ref[0].astype(F32)).astype(o_ref.dtype)

    return pl.pallas_call(
        body, name=name, out_shape=jax.ShapeDtypeStruct((4, rows, cols), gp.dtype),
        grid_spec=pltpu.PrefetchScalarGridSpec(
            num_scalar_prefetch=1, grid=(4, rows // rb),
            in_specs=[pl.BlockSpec((1, 1, rb, cols), lambda j, r, cr: (j, cr[0], r, 0)),
                      pl.BlockSpec((1, rb, cols), lambda j, r, cr: (j, r, 0))],
            out_specs=pl.BlockSpec((1, rb, cols), lambda j, r, cr: (j, r, 0))),
        compiler_params=_cparams(2),
    )(c_arr, gp4, got)


def _chip_exchange_comm(pb):
    _, rows, cols = pb.shape

    def copies(ins, outs, sems):
        send_sems, recv_sems = sems
        x, y, c = _place()
        chips = [(1 - x, y), (x, 1 - y), (1 - x, 1 - y)]
        return [pltpu.make_async_remote_copy(
            src_ref=ins[0].at[2 * px + py], dst_ref=outs[0].at[k], send_sem=send_sems.at[k],
            recv_sem=recv_sems.at[k], device_id=(px, py, c), device_id_type=MESH)
            for k, (px, py) in enumerate(chips)]

    def start(ins, outs, sems):
        for cp in copies(ins, outs, sems):
            cp.start()

    def wait(ins, outs, sems):
        for cp in copies(ins, outs, sems):
            cp.wait()

    return _Comm([pb], [jax.ShapeDtypeStruct((3, rows, cols), pb.dtype)],
                 [pltpu.SemaphoreType.DMA((3,)), pltpu.SemaphoreType.DMA((3,))], start, wait)


def _rs_final_sum(pb, got, chip_arr, name):
    _, rows, cols = got.shape
    rb = _sum_rows(rows)

    def body(j_ref, a_ref, b_ref, o_ref):
        o_ref[...] = ((a_ref[0].astype(F32) + b_ref[0].astype(F32)) + b_ref[1].astype(F32)) + b_ref[2].astype(F32)

    return pl.pallas_call(
        body, name=name, out_shape=jax.ShapeDtypeStruct((rows, cols), F32),
        grid_spec=pltpu.PrefetchScalarGridSpec(
            num_scalar_prefetch=1, grid=(rows // rb,),
            in_specs=[pl.BlockSpec((1, rb, cols), lambda r, jr: (jr[0], r, 0)),
                      pl.BlockSpec((3, rb, cols), lambda r, jr: (0, r, 0))],
            out_specs=pl.BlockSpec((rb, cols), lambda r, jr: (r, 0))),
        compiler_params=_cparams(1),
    )(chip_arr, pb, got)


def _rope_tables(pos_col, inv_freq, comm=None):
    t = pos_col.shape[0]

    def body(p_ref, f_ref, c_ref, s1_ref, s2_ref):
        ang = p_ref[...].astype(F32) * f_ref[...]
        lane = lax.broadcasted_iota(jnp.int32, ang.shape, 1)
        c_ref[...] = jnp.where(lane < QK_ROPE, jnp.cos(ang), 0.0)
        s = jnp.sin(ang)
        s1_ref[...] = jnp.where(lane < 32, -s, 0.0)
        s2_ref[...] = jnp.where((lane >= 32) & (lane < QK_ROPE), s, 0.0)

    spec = pl.BlockSpec((TB, 128), lambda i: (i, 0))
    return _call(
        body, "rope_tables", (t // TB,), [pl.BlockSpec((TB, 1), lambda i: (i, 0)), _const_spec((1, 128))],
        [spec] * 3, [jax.ShapeDtypeStruct((t, 128), F32)] * 3, (pos_col, inv_freq), (), comm)


def _sgu_conv_fwd(proj, tail, lng_ref, ws_ref, bst_ref, cw_ref):
    gu = _gelu(proj[:, 0:SG_W])
    gv = _gelu(proj[:, SG_W:2 * SG_W])
    bg = proj[:, 1024:1536]
    z = proj[:, 1536:2048] * proj[:, 2048:2560]
    heads = []
    for h in range(SG_HEADS):
        sl = slice(h * SG_HD, (h + 1) * SG_HD)
        vn, _, _ = _ln_head(gv[:, sl], lng_ref[:, sl])
        vnb = vn.astype(BF16)
        wm = _tril_bf16(ws_ref[h])
        bcol = bst_ref[:, h:h + 1]
        mixed = jnp.concatenate(
            [_dot(wm, vnb[k * SG_CHUNK:(k + 1) * SG_CHUNK], 1, 0) + bcol for k in range(TB // SG_CHUNK)], axis=0)
        heads.append(gu[:, sl] * mixed)
    a_out = jnp.concatenate(heads, axis=1)
    y, _, _ = _conv_fwd(z, tail, cw_ref)
    return a_out, bg * y, z


def _even_fwd(x, wg, gamma, lng, ws, bst, cw, seq, comm=None):
    t = x.shape[0]
    nbs = seq // TB

    def body(x_ref, gam_ref, win_ref, wout_ref, lng_ref, ws_ref, bst_ref, cw_ref, x1_ref, proj_ref, tail_ref):
        i = pl.program_id(0)
        xv = x_ref[...]
        h, _ = _rms(xv, gam_ref[...])
        proj = _dot(h.astype(BF16), win_ref[...].reshape(EVEN_IN, D), 1, 1)
        proj_ref[...] = proj.astype(BF16)
        tail = jnp.where(i % nbs == 0, 0.0, tail_ref[...])
        a_out, b_out, z = _sgu_conv_fwd(proj, tail, lng_ref, ws_ref, bst_ref, cw_ref)
        tail_ref[...] = z[TB - HALO:, :]
        x1_ref[...] = (xv + _dot(a_out.astype(BF16), wout_ref[0:4].reshape(512, D), 1, 0)
                       + _dot(b_out.astype(BF16), wout_ref[4:8].reshape(512, D), 1, 0))

    row = pl.BlockSpec((TB, D), lambda i: (i, 0))
    return _call(
        body, "even_fwd", (t // TB,),
        [row, _const_spec((1, D)), _wspec(N_EIN, OFF_EIN), _wspec(N_SQ, OFF_EOUT), _const_spec((1, SG_W)),
         _const_spec((SG_HEADS, 128, 128)), _const_spec((128, 128)), _const_spec((8, SC_W))],
        [row, pl.BlockSpec((TB, EVEN_IN), lambda i: (i, 0))],
        [jax.ShapeDtypeStruct((t, D), F32), jax.ShapeDtypeStruct((t, EVEN_IN), BF16)],
        (x, gamma, wg, wg, lng, ws, bst, cw), [pltpu.VMEM((HALO, SC_W), F32)], comm)


def _even_bwd(x, proj, dx1, wg, gamma, lng, ws, bst, cw, seq, comm=None):
    t = x.shape[0]
    nb, nbs = t // TB, seq // TB

    def body(x_ref, proj_ref, ptail_ref, dx1_ref, gam_ref, win_ref, wout_ref, lng_ref, ws_ref, bst_ref, cw_ref,
             dx0_ref, dproj_ref, mix_ref, h_ref, dgam_ref, dws_ref, dbc_ref, dlng_ref, dcw_ref, head_ref):
        i = pl.program_id(0)
        blk = nb - 1 - i

        @pl.when(i == 0)
        def _():
            dgam_ref[...] = jnp.zeros_like(dgam_ref)
            dws_ref[...] = jnp.zeros_like(dws_ref)
            dbc_ref[...] = jnp.zeros_like(dbc_ref)
            dlng_ref[...] = jnp.zeros_like(dlng_ref)
            dcw_ref[...] = jnp.zeros_like(dcw_ref)

        xv = x_ref[...]
        gam = gam_ref[...]
        h, r = _rms(xv, gam)
        h_ref[...] = h.astype(BF16)
        dx1 = dx1_ref[...]
        dmix = _dot(dx1.astype(BF16), wout_ref[...].reshape(D, D), 1, 1)
        da, db = dmix[:, :SG_W], dmix[:, SG_W:]
        proj = proj_ref[...].astype(F32)
        u, v = proj[:, 0:SG_W], proj[:, SG_W:2 * SG_W]
        bg, cg, hv = proj[:, 1024:1536], proj[:, 1536:2048], proj[:, 2048:2560]
        gu, gv = _gelu(u), _gelu(v)

        a_heads, dgv_heads = [], []
        for hd in range(SG_HEADS):
            sl = slice(hd * SG_HD, (hd + 1) * SG_HD)
            g_h = lng_ref[:, sl]
            vn, xh, rr = _ln_head(gv[:, sl], g_h)
            vnb = vn.astype(BF16)
            wm = _tril_bf16(ws_ref[hd])
            bcol = bst_ref[:, hd:hd + 1]
            mixed_c, dvn_c = [], []
            dw_acc = jnp.zeros((128, 128), F32)
            db_acc = jnp.zeros((128, 1), F32)
            for k in range(TB // SG_CHUNK):
                rs = slice(k * SG_CHUNK, (k + 1) * SG_CHUNK)
                mixed = _dot(wm, vnb[rs], 1, 0) + bcol
                dmixed = da[rs, sl] * gu[rs, sl]
                dmb = dmixed.astype(BF16)
                dvn_c.append(_dot(wm, dmb, 0, 0))
                dw_acc = dw_acc + _dot(dmb, vnb[rs], 1, 1)
                db_acc = db_acc + jnp.sum(dmixed, axis=1, keepdims=True)
                mixed_c.append(mixed)
            mixed_h = jnp.concatenate(mixed_c, axis=0)
            dvn = jnp.concatenate(dvn_c, axis=0)
            r_i = lax.broadcasted_iota(jnp.int32, (128, 128), 0)
            c_i = lax.broadcasted_iota(jnp.int32, (128, 128), 1)
            dws_ref[hd] += jnp.where(r_i >= c_i, dw_acc, 0.0)
            dbc_ref[:, hd:hd + 1] += db_acc
            dlng_ref[:, sl] += jnp.sum(dvn * xh, axis=0, keepdims=True)
            dxh = dvn * g_h
            dgv = rr * (dxh - jnp.mean(dxh, axis=-1, keepdims=True)
                        - xh * jnp.mean(dxh * xh, axis=-1, keepdims=True))
            a_heads.append(gu[:, sl] * mixed_h)
            dproj_ref[:, sl] = (da[:, sl] * mixed_h * _gelu_grad(u[:, sl])).astype(BF16)
            dgv_heads.append(dgv * _gelu_grad(v[:, sl]))
        dproj_ref[:, SG_W:2 * SG_W] = jnp.concatenate(dgv_heads, axis=1).astype(BF16)
        mix_ref[:, :SG_W] = jnp.concatenate(a_heads, axis=1).astype(BF16)

        z = cg * hv
        pt = ptail_ref[...].astype(F32)
        tail = jnp.where(blk % nbs == 0, 0.0, pt[:, 1536:2048] * pt[:, 2048:2560])
        y, zs1, zs2 = _conv_fwd(z, tail, cw_ref)
        mix_ref[:, SG_W:] = (bg * y).astype(BF16)
        dy = db * bg
        head = jnp.where(blk % nbs == nbs - 1, 0.0, head_ref[...])
        ext = jnp.concatenate([dy, head], axis=0)
        dz = (cw_ref[2:3, :] * dy + cw_ref[1:2, :] * _shift_up(ext, 1)[:TB]
              + cw_ref[0:1, :] * _shift_up(ext, 2)[:TB])
        head_ref[...] = dy[:HALO, :]
        dcw_ref[2:3, :] += jnp.sum(dy * z, axis=0, keepdims=True)
        dcw_ref[1:2, :] += jnp.sum(dy * zs1, axis=0, keepdims=True)
        dcw_ref[0:1, :] += jnp.sum(dy * zs2, axis=0, keepdims=True)
        dproj_ref[:, 1024:1536] = (db * y).astype(BF16)
        dproj_ref[:, 1536:2048] = (dz * hv).astype(BF16)
        dproj_ref[:, 2048:2560] = (dz * cg).astype(BF16)

        dh = _dot(dproj_ref[...], win_ref[...].reshape(EVEN_IN, D), 1, 0)
        dxn, dgam = _rms_bwd(xv, r, gam, dh)
        dgam_ref[...] += dgam
        dx0_ref[...] = dx1 + dxn

    def rev(w):
        return pl.BlockSpec((TB, w), lambda i: (nb - 1 - i, 0))

    ptail = pl.BlockSpec((HALO, EVEN_IN), lambda i: (jnp.maximum((nb - 1 - i) * (TB // HALO) - 1, 0), 0))
    return _call(
        body, "even_bwd", (nb,),
        [rev(D), rev(EVEN_IN), ptail, rev(D), _const_spec((1, D)), _wspec(N_EIN, OFF_EIN),
         _wspec(N_SQ, OFF_EOUT), _const_spec((1, SG_W)), _const_spec((SG_HEADS, 128, 128)),
         _const_spec((128, 128)), _const_spec((8, SC_W))],
        [rev(D), rev(EVEN_IN), rev(D), rev(D), _const_spec((1, D)), _const_spec((SG_HEADS, 128, 128)),
         _const_spec((128, 128)), _const_spec((1, SG_W)), _const_spec((8, SC_W))],
        [jax.ShapeDtypeStruct((t, D), F32), jax.ShapeDtypeStruct((t, EVEN_IN), BF16),
         jax.ShapeDtypeStruct((t, D), BF16), jax.ShapeDtypeStruct((t, D), BF16),
         jax.ShapeDtypeStruct((1, D), F32), jax.ShapeDtypeStruct((SG_HEADS, 128, 128), F32),
         jax.ShapeDtypeStruct((128, 128), F32), jax.ShapeDtypeStruct((1, SG_W), F32),
         jax.ShapeDtypeStruct((8, SC_W), F32)],
        (x, proj, proj, dx1, gamma, wg, wg, lng, ws, bst, cw), [pltpu.VMEM((HALO, SC_W), F32)], comm)


def _ffn_fwd(x, wg, gamma, name, comm=None, target=None):
    t = x.shape[0]
    last = target is not None

    def body(*refs):
        x_ref, gam_ref, wg_ref, wu_ref, wd_ref = refs[:5]
        y_ref, g_ref, u_ref = refs[5 + last:8 + last]
        xv = x_ref[...]
        h, _ = _rms(xv, gam_ref[...])
        hb = h.astype(BF16)
        g = _dot(hb, wg_ref[...].reshape(D_FF, D), 1, 1)
        u = _dot(hb, wu_ref[...].reshape(D_FF, D), 1, 1)
        g_ref[...] = g.astype(BF16)
        u_ref[...] = u.astype(BF16)
        act = g * jax.nn.sigmoid(g) * u
        y = xv + _dot(act.astype(BF16), wd_ref[...].reshape(D_FF, D), 1, 0)
        if not last:
            y_ref[...] = y
            return
        loss_ref = refs[9]

        @pl.when(pl.program_id(0) == 0)
        def _():
            loss_ref[...] = jnp.zeros_like(loss_ref)

        err = y - refs[5][...]
        y_ref[...] = err * (1.0 / D)
        sq = jnp.sum(jnp.sum(err * err, axis=-1, keepdims=True), axis=0, keepdims=True)
        loss_ref[...] += (0.5 / D) * sq

    row = pl.BlockSpec((TB, D), lambda i: (i, 0))
    wide = pl.BlockSpec((TB, D_FF), lambda i: (i, 0))
    in_specs = [row, _const_spec((1, D)), _wspec(N_FF, OFF_GATE), _wspec(N_FF, OFF_UP), _wspec(N_FF, OFF_DOWN)]
    out_specs = [row, wide, wide]
    out_shape = [jax.ShapeDtypeStruct((t, D), F32), jax.ShapeDtypeStruct((t, D_FF), BF16),
                 jax.ShapeDtypeStruct((t, D_FF), BF16)]
    args = (x, gamma, wg, wg, wg)
    if last:
        in_specs, args = in_specs + [row], args + (target,)
        out_specs, out_shape = out_specs + [_const_spec((8, 128))], out_shape + [jax.ShapeDtypeStruct((8, 128), F32)]
    return _call(body, name, (t // TB,), in_specs, out_specs, out_shape, args, (), comm)


def _ffn_bwd(x, g, u, dy, wg, gamma, name, comm=None):
    t = x.shape[0]

    def body(x_ref, g_ref, u_ref, dy_ref, gam_ref, wg_ref, wu_ref, wd_ref,
             dx_ref, act_ref, dg_ref, du_ref, h_ref, dgam_ref):
        @pl.when(pl.program_id(0) == 0)
        def _():
            dgam_ref[...] = jnp.zeros_like(dgam_ref)

        xv = x_ref[...]
        gam = gam_ref[...]
        h, r = _rms(xv, gam)
        h_ref[...] = h.astype(BF16)
        dyv = dy_ref[...]
        dact = _dot(dyv.astype(BF16), wd_ref[...].reshape(D_FF, D), 1, 1)
        gv = g_ref[...].astype(F32)
        uv = u_ref[...].astype(F32)
        sg = jax.nn.sigmoid(gv)
        silu = gv * sg
        act_ref[...] = (silu * uv).astype(BF16)
        dgb = (dact * uv * (sg * (1.0 + gv * (1.0 - sg)))).astype(BF16)
        dub = (dact * silu).astype(BF16)
        dg_ref[...] = dgb
        du_ref[...] = dub
        dh = _dot(dgb, wg_ref[...].reshape(D_FF, D), 1, 0) + _dot(dub, wu_ref[...].reshape(D_FF, D), 1, 0)
        dxn, dgam = _rms_bwd(xv, r, gam, dh)
        dgam_ref[...] += dgam
        dx_ref[...] = dyv + dxn

    row = pl.BlockSpec((TB_FFN_BWD, D), lambda i: (i, 0))
    wide = pl.BlockSpec((TB_FFN_BWD, D_FF), lambda i: (i, 0))
    return _call(
        body, name, (t // TB_FFN_BWD,),
        [row, wide, wide, row, _const_spec((1, D)), _wspec(N_FF, OFF_GATE), _wspec(N_FF, OFF_UP),
         _wspec(N_FF, OFF_DOWN)],
        [row, wide, wide, wide, row, _const_spec((1, D))],
        [jax.ShapeDtypeStruct((t, D), F32), jax.ShapeDtypeStruct((t, D_FF), BF16),
         jax.ShapeDtypeStruct((t, D_FF), BF16), jax.ShapeDtypeStruct((t, D_FF), BF16),
         jax.ShapeDtypeStruct((t, D), BF16), jax.ShapeDtypeStruct((1, D), F32)],
        (x, g, u, dy, gamma, wg, wg, wg), (), comm)


def _odd_pre_fwd(x, wg, gamma, qbt, kvbt, qa_g, kva_g, pw_bd, pscale, seq):
    t = x.shape[0]
    nbs = seq // TB

    def body(x_ref, gam_ref, win_ref, qb_ref, kvb_ref, qa_ref, kva_ref, pw_ref, ps_ref,
             proj_ref, q_ref, kv_ref, kr_ref, c_ref, tail_ref):
        i = pl.program_id(0)
        h, _ = _rms(x_ref[...], gam_ref[...])
        proj = _dot(h.astype(BF16), win_ref[...].reshape(D, D), 1, 0)
        proj_ref[...] = proj.astype(BF16)
        zp, ql, kvl = proj[:, :POOL_W], proj[:, 256:640], proj[:, 640:896]
        kr_ref[...] = proj[:, 896:1024]
        qn, _ = _rms(ql, qa_ref[...])
        q_ref[...] = _dot(qn.astype(BF16), qb_ref[...], 1, 1).astype(BF16)
        kvn, _ = _rms(kvl, kva_ref[...])
        kv_ref[...] = _dot(kvn.astype(BF16), kvb_ref[...], 1, 1).astype(BF16)
        tail = jnp.where(i % nbs == 0, 0.0, tail_ref[...])
        pooled, _, _ = _pool_fwd(zp, tail, i % nbs)
        tail_ref[...] = zp[TB - HALO:, :]
        c_ref[...] = (_dot(pooled.astype(BF16), pw_ref[...], 1, 0) * ps_ref[...]).astype(BF16)

    def row(w):
        return pl.BlockSpec((TB, w), lambda i: (i, 0))

    return pl.pallas_call(
        body, name="odd_pre_fwd",
        out_shape=[jax.ShapeDtypeStruct((t, D), BF16), jax.ShapeDtypeStruct((t, HEADS * HP), BF16),
                   jax.ShapeDtypeStruct((t, HEADS * HP), BF16), jax.ShapeDtypeStruct((t, 128), F32),
                   jax.ShapeDtypeStruct((t, POOL_W), BF16)],
        grid=(t // TB,),
        in_specs=[row(D), _const_spec((1, D)), _wspec(N_SQ, OFF_OIN), _const_spec((HEADS * HP, Q_LORA)),
                  _const_spec((HEADS * HP, KV_LORA)), _const_spec((1, Q_LORA)), _const_spec((1, KV_LORA)),
                  _const_spec((POOL_W, POOL_W)), _const_spec((1, POOL_W))],
        out_specs=[row(D), row(HEADS * HP), row(HEADS * HP), row(128), row(POOL_W)],
        scratch_shapes=[pltpu.VMEM((HALO, POOL_W), F32)],
        compiler_params=_cparams(1),
    )(x, gamma, wg, qbt, kvbt, qa_g, kva_g, pw_bd, pscale)


def _odd_pre_bwd(x, proj, dx3, dmix, dq, dkv, dkr, wg, gamma, qbt, kvbt, qa_g, kva_g, pw_bd, pscale, seq):
    t = x.shape[0]
    nb, nbs = t // TB, seq // TB

    def body(x_ref, proj_ref, ptail_ref, dx3_ref, dco_ref, dq_ref, dkv_ref, dkr_ref, gam_ref, win_ref, qb_ref,
             kvb_ref, qa_ref, kva_ref, pw_ref, ps_ref,
             dx2_ref, dproj_ref, h_ref, qn_ref, kvn_ref, dgam_ref, dqa_ref, dkva_ref, dpw_ref, dps_ref, head_ref):
        i = pl.program_id(0)
        blk = nb - 1 - i

        @pl.when(i == 0)
        def _():
            dgam_ref[...] = jnp.zeros_like(dgam_ref)
            dqa_ref[...] = jnp.zeros_like(dqa_ref)
            dkva_ref[...] = jnp.zeros_like(dkva_ref)
            dpw_ref[...] = jnp.zeros_like(dpw_ref)
            dps_ref[...] = jnp.zeros_like(dps_ref)

        xv = x_ref[...]
        gam = gam_ref[...]
        h, r = _rms(xv, gam)
        h_ref[...] = h.astype(BF16)
        proj = proj_ref[...].astype(F32)
        zp, ql, kvl = proj[:, :POOL_W], proj[:, 256:640], proj[:, 640:896]

        qa = qa_ref[...]
        qn, rq = _rms(ql, qa)
        qn_ref[...] = qn.astype(BF16)
        dql, dqa = _rms_bwd(ql, rq, qa, _dot(dq_ref[...], qb_ref[...], 1, 0))
        dqa_ref[...] += dqa
        kva = kva_ref[...]
        kvn, rkv = _rms(kvl, kva)
        kvn_ref[...] = kvn.astype(BF16)
        dkvl, dkva = _rms_bwd(kvl, rkv, kva, _dot(dkv_ref[...], kvb_ref[...], 1, 0))
        dkva_ref[...] += dkva

        pt = ptail_ref[...].astype(F32)
        tail = jnp.where(blk % nbs == 0, 0.0, pt[:, :POOL_W])
        pooled, cnt, grp = _pool_fwd(zp, tail, blk % nbs)
        pb = pooled.astype(BF16)
        pw = pw_ref[...]
        dco = dco_ref[...].astype(F32)
        dps_ref[...] += jnp.sum(dco * _dot(pb, pw, 1, 0), axis=0, keepdims=True)
        dpo = (dco * ps_ref[...]).astype(BF16)
        dpw_ref[...] += _dot(pb, dpo, 0, 0)
        dpooled = _dot(dpo, pw, 1, 1)
        dpm = dpooled / cnt
        head = jnp.where(blk % nbs == nbs - 1, 0.0, head_ref[...])
        dz = _pool_bwd(dpooled, dpm, head, grp)
        head_ref[...] = dpm[:HALO, :]

        dproj_ref[:, :POOL_W] = dz.astype(BF16)
        dproj_ref[:, 256:640] = dql.astype(BF16)
        dproj_ref[:, 640:896] = dkvl.astype(BF16)
        dproj_ref[:, 896:1024] = dkr_ref[...].astype(BF16)
        dh = _dot(dproj_ref[...], win_ref[...].reshape(D, D), 1, 1)
        dxn, dgam = _rms_bwd(xv, r, gam, dh)
        dgam_ref[...] += dgam
        dx2_ref[...] = dx3_ref[...] + dxn

    def rev(w):
        return pl.BlockSpec((TB, w), lambda i: (nb - 1 - i, 0))

    ptail = pl.BlockSpec((HALO, D), lambda i: (jnp.maximum((nb - 1 - i) * (TB // HALO) - 1, 0), 0))
    return pl.pallas_call(
        body, name="odd_pre_bwd",
        out_shape=[jax.ShapeDtypeStruct((t, D), F32), jax.ShapeDtypeStruct((t, D), BF16),
                   jax.ShapeDtypeStruct((t, D), BF16), jax.ShapeDtypeStruct((t, Q_LORA), BF16),
                   jax.ShapeDtypeStruct((t, KV_LORA), BF16), jax.ShapeDtypeStruct((1, D), F32),
                   jax.ShapeDtypeStruct((1, Q_LORA), F32), jax.ShapeDtypeStruct((1, KV_LORA), F32),
                   jax.ShapeDtypeStruct((POOL_W, POOL_W), F32), jax.ShapeDtypeStruct((1, POOL_W), F32)],
        grid=(nb,),
        in_specs=[rev(D), rev(D), ptail, rev(D), rev(POOL_W), rev(HEADS * HP), rev(HEADS * HP), rev(128),
                  _const_spec((1, D)), _wspec(N_SQ, OFF_OIN), _const_spec((HEADS * HP, Q_LORA)),
                  _const_spec((HEADS * HP, KV_LORA)), _const_spec((1, Q_LORA)), _const_spec((1, KV_LORA)),
                  _const_spec((POOL_W, POOL_W)), _const_spec((1, POOL_W))],
        out_specs=[rev(D), rev(D), rev(D), rev(Q_LORA), rev(KV_LORA), _const_spec((1, D)), _const_spec((1, Q_LORA)),
                   _const_spec((1, KV_LORA)), _const_spec((POOL_W, POOL_W)), _const_spec((1, POOL_W))],
        scratch_shapes=[pltpu.VMEM((HALO, POOL_W), F32)],
        compiler_params=_cparams(1),
    )(x, proj, proj, dx3, dmix, dq, dkv, dkr, gamma, wg, qbt, kvbt, qa_g, kva_g, pw_bd, pscale)


def _attn_specs(seq):
    head = pl.BlockSpec((seq, HP), lambda b, h: (b, h))
    shared = pl.BlockSpec((seq, 128), lambda b, h: (b, 0))
    gain = pl.BlockSpec((1, HP), lambda b, h: (0, 0))
    return head, shared, gain


def _causal_bias(n):
    rows = lax.broadcasted_iota(jnp.int32, (n, n), 0)
    cols = lax.broadcasted_iota(jnp.int32, (n, n), 1)
    return jnp.where(cols <= rows, 0.0, NEG_INF)


def _attn_fwd(q, kv, kr, cos, s1, s2, gq, gk, seq, comm=None):
    t = q.shape[0]
    qb = min(512, seq)

    def body(q_ref, kv_ref, kr_ref, c_ref, s1_ref, s2_ref, gq_ref, gk_ref, o_ref, lse_ref):
        c, sa, sb = c_ref[...], s1_ref[...], s2_ref[...]
        qf, _ = _qk_prep(q_ref[...].astype(F32), gq_ref[...], c, sa, sb)
        kin = jnp.concatenate([kv_ref[:, :128].astype(F32), kr_ref[...]], axis=1)
        kf, _ = _qk_prep(kin, gk_ref[...], c, sa, sb)
        qf, kf = qf.astype(BF16), kf.astype(BF16)
        v1 = jnp.concatenate([kv_ref[:, 128:], jnp.ones((seq, V_DIM), BF16)], axis=1)
        bias = _causal_bias(qb)
        for q0 in range(0, seq, qb):
            q1 = q0 + qb
            qblk = qf[q0:q1]
            s_dg = _dot(qblk, kf[q0:q1], 1, 1) + bias
            m = jnp.max(s_dg, axis=-1, keepdims=True)
            if q0:
                s_off = _dot(qblk, kf[:q0], 1, 1)
                m = jnp.maximum(m, jnp.max(s_off, axis=-1, keepdims=True))
            acc = _dot(jnp.exp(s_dg - m).astype(BF16), v1[q0:q1], 1, 0)
            if q0:
                acc = acc + _dot(jnp.exp(s_off - m).astype(BF16), v1[:q0], 1, 0)
            l = acc[:, V_DIM:]
            o_ref[q0:q1, :] = (acc[:, :V_DIM] / l).astype(BF16)
            lse_ref[q0:q1, :] = m + jnp.log(l)

    head, shared, gain = _attn_specs(seq)
    per_head = pl.BlockSpec((seq, V_DIM), lambda b, h: (b, h))
    return _call(
        body, "attn_fwd", (t // seq, HEADS),
        [head, head, shared, shared, shared, shared, gain, gain], [per_head, per_head],
        [jax.ShapeDtypeStruct((t, HEADS * V_DIM), BF16), jax.ShapeDtypeStruct((t, HEADS * V_DIM), F32)],
        (q, kv, kr, cos, s1, s2, gq, gk), (), comm)


def _attn_bwd(q, kv, kr, cos, s1, s2, gq, gk, dmix, d_out, lse, seq, comm=None):
    t = q.shape[0]
    qb = min(512, seq)

    def body(q_ref, kv_ref, kr_ref, c_ref, s1_ref, s2_ref, gq_ref, gk_ref, do_ref, o_ref, lse_ref,
             dq_ref, dkv_ref, dkr_ref, dgq_ref, dgk_ref, dqf_ref, dkf_ref, dv_ref):
        b, hd = pl.program_id(0), pl.program_id(1)

        @pl.when((b == 0) & (hd == 0))
        def _():
            dgq_ref[...] = jnp.zeros_like(dgq_ref)
            dgk_ref[...] = jnp.zeros_like(dgk_ref)

        c, sa, sb = c_ref[...], s1_ref[...], s2_ref[...]
        gq_v, gk_v = gq_ref[...], gk_ref[...]
        qin = q_ref[...].astype(F32)
        kin = jnp.concatenate([kv_ref[:, :128].astype(F32), kr_ref[...]], axis=1)
        qf32, rq = _qk_prep(qin, gq_v, c, sa, sb)
        kf32, rk = _qk_prep(kin, gk_v, c, sa, sb)
        qf, kf = qf32.astype(BF16), kf32.astype(BF16)
        vb = kv_ref[:, 128:]
        dkf_ref[...] = jnp.zeros_like(dkf_ref)
        dv_ref[...] = jnp.zeros_like(dv_ref)
        bias = _causal_bias(qb)
        for q0 in range(0, seq, qb):
            q1 = q0 + qb
            qblk = qf[q0:q1]
            do = do_ref[q0:q1, :]
            lse_col = lse_ref[q0:q1, 0:1]
            d_col = jnp.sum(do.astype(F32) * o_ref[q0:q1, :].astype(F32), axis=-1, keepdims=True)
            dq_acc = None
            for k0, k1, diag in ((q0, q1, True), (0, q0, False)):
                if k1 == k0:
                    continue
                s = _dot(qblk, kf[k0:k1], 1, 1)
                p = jnp.exp((s + bias if diag else s) - lse_col)
                dv_ref[k0:k1, :] += _dot(p.astype(BF16), do, 0, 0)
                ds = (p * (_dot(do, vb[k0:k1], 1, 1) - d_col)).astype(BF16)
                part = _dot(ds, kf[k0:k1], 1, 0)
                dq_acc = part if dq_acc is None else dq_acc + part
                dkf_ref[k0:k1, :] += _dot(ds, qblk, 0, 0)
            dqf_ref[q0:q1, :] = dq_acc
        dqin, dgq = _qk_prep_bwd(dqf_ref[...], qin, rq, gq_v, c, sa, sb)
        dkin, dgk = _qk_prep_bwd(dkf_ref[...], kin, rk, gk_v, c, sa, sb)
        dgq_ref[...] += dgq
        dgk_ref[...] += dgk
        dq_ref[...] = dqin.astype(BF16)
        dkv_ref[:, :128] = dkin[:, :128].astype(BF16)
        dkv_ref[:, 128:] = dv_ref[...].astype(BF16)

        @pl.when(hd == 0)
        def _():
            dkr_ref[...] = dkin[:, 128:]

        @pl.when(hd != 0)
        def _():
            dkr_ref[...] += dkin[:, 128:]

    head, shared, gain = _attn_specs(seq)
    per_head = pl.BlockSpec((seq, V_DIM), lambda b, h: (b, h))
    return _call(
        body, "attn_bwd", (t // seq, HEADS),
        [head, head, shared, shared, shared, shared, gain, gain,
         pl.BlockSpec((seq, V_DIM), lambda b, h: (b, 2 + h)), per_head, per_head],
        [head, head, shared, gain, gain],
        [jax.ShapeDtypeStruct((t, HEADS * HP), BF16), jax.ShapeDtypeStruct((t, HEADS * HP), BF16),
         jax.ShapeDtypeStruct((t, 128), F32), jax.ShapeDtypeStruct((1, HP), F32),
         jax.ShapeDtypeStruct((1, HP), F32)],
        (q, kv, kr, cos, s1, s2, gq, gk, dmix, d_out, lse),
        [pltpu.VMEM((seq, HP), F32), pltpu.VMEM((seq, HP), F32), pltpu.VMEM((seq, V_DIM), F32)], comm)


def _odd_post_fwd(x, c_out, d_out, wg):
    t = x.shape[0]

    def body(x_ref, c_ref, d_ref, w_ref, y_ref):
        y_ref[...] = (x_ref[...] + _dot(c_ref[...], w_ref[0:2].reshape(POOL_W, D), 1, 0)
                      + _dot(d_ref[...], w_ref[2:8].reshape(HEADS * V_DIM, D), 1, 0))

    def row(w):
        return pl.BlockSpec((TB, w), lambda i: (i, 0))

    return pl.pallas_call(
        body, name="odd_post_fwd", out_shape=jax.ShapeDtypeStruct((t, D), F32), grid=(t // TB,),
        in_specs=[row(D), row(POOL_W), row(HEADS * V_DIM), _wspec(N_SQ, OFF_OOUT)], out_specs=row(D),
        compiler_params=_cparams(1),
    )(x, c_out, d_out, wg)


def _odd_post_bwd(dx3, wg, comm=None):
    t = dx3.shape[0]

    def body(d_ref, w_ref, o_ref):
        o_ref[...] = _dot(d_ref[...].astype(BF16), w_ref[...].reshape(D, D), 1, 1).astype(BF16)

    row = pl.BlockSpec((TB, D), lambda i: (i, 0))
    (res,), extra = _call(body, "odd_post_bwd", (t // TB,), [row, _wspec(N_SQ, OFF_OOUT)], [row],
                          [jax.ShapeDtypeStruct((t, D), BF16)], (dx3, wg), (), comm)
    return res, extra


def _tn(a_list, b, tm, name, into=None, comm=None):
    t, n_out = b.shape
    widths = [a.shape[1] for a in a_list]
    m, na, nk = sum(widths), len(a_list), t // TB
    assert na == 1 or tm == m

    def body(*refs):
        a_refs, b_ref, o_ref, acc_ref = refs[:na], refs[na], refs[-2], refs[-1]
        k = pl.program_id(1)

        @pl.when(k == 0)
        def _():
            acc_ref[...] = jnp.zeros_like(acc_ref)

        bb = b_ref[...].astype(BF16)
        m0 = 0
        for a_ref, w in zip(a_refs, widths):
            rows = slice(0, tm) if na == 1 else slice(m0, m0 + w)
            acc_ref[rows, :] += _dot(a_ref[...].astype(BF16), bb, 0, 0)
            m0 += w

        @pl.when(k == nk - 1)
        def _():
            o_ref[...] = acc_ref[...].astype(BF16).reshape(o_ref.shape)

    if na == 1:
        in_specs = [pl.BlockSpec((TB, tm), lambda i, k: (k, i))]
    else:
        in_specs = [pl.BlockSpec((TB, w), lambda i, k: (k, 0)) for w in widths]
    in_specs.append(pl.BlockSpec((TB, n_out), lambda i, k: (k, 0)))
    args = list(a_list) + [b]
    if into is None:
        out_spec = pl.BlockSpec((tm, n_out), lambda i, k: (i, 0))
        out_shape = jax.ShapeDtypeStruct((m, n_out), BF16)
        aliases = {}
    else:
        buf, n, off = into
        assert n_out == D and tm % n == 0 and off % n == 0 and (na == 1 or tm // n == N_DEV)
        idx = off // n
        out_spec = pl.BlockSpec((tm // n, n, D), lambda i, k: (i, idx, 0))
        out_shape = jax.ShapeDtypeStruct(buf.shape, BF16)
        in_specs.append(pl.BlockSpec(memory_space=pl.ANY))
        args.append(buf)
        aliases = {len(args) - 1: 0}
    (res,), extra = _call(body, name, (m // tm, nk), in_specs, [out_spec], [out_shape], args,
                          [pltpu.VMEM((tm, n_out), F32)], comm, aliases)
    return (res, extra) if comm is not None else res


def _adamw(ws, gs, ms, vs, name, nblk=1, comm=None):
    n = len(ws)
    c1 = 1.0 - B1 ** STEP
    c2 = 1.0 - B2 ** STEP

    def body(*refs):
        for a in range(n):
            w, g, m, v = (refs[k * n + a][...] for k in range(4))
            d_ref, m_ref, v_ref = (refs[(4 + k) * n + a] for k in range(3))
            m_new = B1 * m + (1.0 - B1) * g
            v_new = B2 * v + (1.0 - B2) * (g * g)
            d_ref[...] = -LR * ((m_new / c1) / (jnp.sqrt(v_new / c2) + ADAM_EPS) + WD * w)
            m_ref[...] = m_new
            v_ref[...] = v_new

    grid = (nblk,)
    assert all(w.shape[0] % nblk == 0 and (nblk == 1 or (w.shape[0] // nblk) % 8 == 0) for w in ws)
    specs = [pl.BlockSpec((w.shape[0] // nblk, w.shape[1]), lambda i: (i, 0)) for w in ws]
    outs, extra = _call(body, name, grid, specs * 4, specs * 3, [jax.ShapeDtypeStruct(w.shape, F32) for w in ws] * 3,
                        (*ws, *gs, *ms, *vs), (), comm)
    res = (outs[:n], outs[n:2 * n], outs[2 * n:])
    return (res, extra) if comm is not None else res


def _rows1024(a, rows):
    flat = a.reshape(-1, D)
    return jnp.pad(flat, ((0, rows - flat.shape[0]), (0, 0)))


def _pack_shards(even_w_in, even_w_out, odd_w_in, q_b, kv_b, odd_w_out, ffn_w_gate, ffn_w_up, ffn_w_down):
    mix0 = jnp.concatenate([even_w_in[0].T, jnp.zeros((OFF_EOUT - N_EIN, D), F32), even_w_out[0]], axis=0)
    ffn = [jnp.concatenate([ffn_w_gate[layer].T, ffn_w_up[layer].T, ffn_w_down[layer]], axis=0)
           for layer in range(2)]
    mix1 = jnp.concatenate([jnp.pad(odd_w_in[0], ((0, 0), (0, D - ODD_IN))), odd_w_out[0],
                            _rows1024(q_b[0].T, N_QB), _rows1024(kv_b[0].T, N_KVB),
                            jnp.zeros((R_MIX1 - OFF_KVB - N_KVB, D), F32)], axis=0)
    return [c.astype(BF16) for c in (mix0, ffn[0], mix1, ffn[1])]


def _pad_heads(a):
    k = a.shape[1]
    return jnp.pad(a.reshape(HEADS, QK_DIM, k), ((0, 0), (0, HP - QK_DIM), (0, 0))).reshape(HEADS * HP, k)


def _small_pack(parts):
    flat = []
    for p in parts:
        v = p.reshape(-1)
        flat.append(jnp.pad(v, (0, (-v.shape[0]) % 1024)))
    return jnp.concatenate(flat).reshape(-1, 128)


def _small_unpack(buf, shapes):
    flat = buf.reshape(-1)
    out, off = [], 0
    for s in shapes:
        size = int(np.prod(s))
        out.append(flat[off:off + size].reshape(s))
        off += size + (-size) % 1024
    return out


def _step(x3d, positions, target3d, chunks, tile, c_arr, chip_arr, mix_norm, ffn_norm, sg_ln_g, sg_w_s, sg_b_s,
          pool_w, q_norm, k_norm):
    bsz, seq, _ = x3d.shape
    t = bsz * seq
    x0 = x3d.reshape(t, D)
    target = target3d.reshape(t, D)
    my_mix0, my_ffn0, my_mix1, my_ffn1 = chunks

    lane = np.arange(128)
    inv_freq = np.where(lane < QK_ROPE, ROPE_THETA ** (-(2.0 * (lane % 32)) / QK_ROPE), 0.0)
    inv_freq = jnp.asarray(inv_freq.reshape(1, 128), F32)
    (cos, s1, s2), (w_mix0, tiles) = _rope_tables(positions.reshape(t, 1), inv_freq, _gather_comm([my_mix0, tile]))

    conv_w = tiles[:, 0:3, 0:64].transpose(1, 0, 2).reshape(3, SC_W)
    pool_scale = tiles[:, 3, 0:32].reshape(1, POOL_W)
    q_a_norm = tiles[:, 4, 0:48].reshape(1, Q_LORA)
    kv_a_norm = tiles[:, 5, 0:32].reshape(1, KV_LORA)
    ws = sg_w_s[0]
    bst = jnp.pad(sg_b_s[0].T, ((0, 0), (0, 128 - SG_HEADS)))
    cw = jnp.pad(conv_w, ((0, 8 - 3), (0, 0)))
    pw_bd = jax.scipy.linalg.block_diag(*[pool_w[0, g] for g in range(4)]).astype(BF16)
    gq = jnp.pad(q_norm * ATT_SCALE, ((0, 0), (0, HP - QK_DIM)))
    gk = jnp.pad(k_norm, ((0, 0), (0, HP - QK_DIM)))

    (x1, proj_e), (w_ffn0,) = _even_fwd(x0, w_mix0, mix_norm[0:1], sg_ln_g, ws, bst, cw, seq,
                                        _gather_comm([my_ffn0]))
    (x2, g0, u0), (w_mix1,) = _ffn_fwd(x1, w_ffn0, ffn_norm[0:1], "ffn_fwd0", _gather_comm([my_mix1]))
    qbt = _pad_heads(w_mix1[:, OFF_QB:OFF_QB + N_QB_USED, :].reshape(HEADS * QK_DIM, Q_LORA))
    kvbt = w_mix1[:, OFF_KVB:OFF_KVB + N_KVB, :].reshape(HEADS * HP, KV_LORA)
    proj_o, q, kv, kr, c_out = _odd_pre_fwd(x2, w_mix1, mix_norm[1:2], qbt, kvbt, q_a_norm, kv_a_norm, pw_bd,
                                            pool_scale, seq)
    (d_out, lse), (w_ffn1,) = _attn_fwd(q, kv, kr, cos, s1, s2, gq, gk, seq, _gather_comm([my_ffn1]))
    x3 = _odd_post_fwd(x2, c_out, d_out, w_mix1)
    (dy, g1, u1, loss_tile), _ = _ffn_fwd(x3, w_ffn1, ffn_norm[1:2], "ffn_fwd1", None, target)

    def chunk(rows, padded=False):
        return jnp.zeros((N_DEV, rows, D), BF16) if padded else lax.empty((N_DEV, rows, D), BF16)

    (dx3, act1, dg1, du1, h3, dgam_f1), _ = _ffn_bwd(x3, g1, u1, dy, w_ffn1, ffn_norm[1:2], "ffn_bwd1")
    gp_ffn1 = _tn([dg1], h3, 1408, "dw_gate1", (chunk(R_FFN), N_FF, OFF_GATE))
    gp_ffn1 = _tn([du1], h3, 1408, "dw_up1", (gp_ffn1, N_FF, OFF_UP))
    gp_ffn1 = _tn([act1], dy, 1408, "dw_down1", (gp_ffn1, N_FF, OFF_DOWN))

    dmix_o, (ga_ffn1,) = _odd_post_bwd(dx3, w_mix1, _pair_exchange_comm(gp_ffn1))
    pb_ffn1 = _rs_pair_sum(gp_ffn1, ga_ffn1, c_arr, "rs_pair_sum_ffn1")
    gp_mix1 = _tn([c_out, d_out], dx3, D, "dw_oout", (chunk(R_MIX1, True), N_SQ, OFF_OOUT))
    (dq, dkv, dkr, dgq, dgk), (gb_ffn1,) = _attn_bwd(q, kv, kr, cos, s1, s2, gq, gk, dmix_o, d_out, lse, seq,
                                                    _chip_exchange_comm(pb_ffn1))
    gsh_ffn1 = _rs_final_sum(pb_ffn1, gb_ffn1, chip_arr, "rs_final_sum_ffn1")
    (dx2, dproj_o, h2, qn, kvn, dgam_m1, dqa, dkva, dpw_bd, dps) = _odd_pre_bwd(
        x2, proj_o, dx3, dmix_o, dq, dkv, dkr, w_mix1, mix_norm[1:2], qbt, kvbt, q_a_norm, kv_a_norm, pw_bd,
        pool_scale, seq)
    gp_mix1 = _tn([h2], dproj_o, D, "dw_oin", (gp_mix1, N_SQ, OFF_OIN))
    d_qbt = _tn([dq], qn, HEADS * HP, "dw_qb")
    d_qb_rows = d_qbt.reshape(HEADS, HP, Q_LORA)[:, :QK_DIM].reshape(N_DEV, N_QB_USED, D)
    d_kvb_rows = _tn([dkv], kvn, HEADS * HP, "dw_kvb").reshape(N_DEV, N_KVB, D)
    gp_mix1 = lax.dynamic_update_slice(gp_mix1, d_qb_rows, (0, OFF_QB, 0))
    gp_mix1 = lax.dynamic_update_slice(gp_mix1, d_kvb_rows, (0, OFF_KVB, 0))

    (dx1, act0, dg0, du0, h1, dgam_f0), (ga_mix1,) = _ffn_bwd(x1, g0, u0, dx2, w_ffn0, ffn_norm[0:1], "ffn_bwd0",
                                                             _pair_exchange_comm(gp_mix1))
    pb_mix1 = _rs_pair_sum(gp_mix1, ga_mix1, c_arr, "rs_pair_sum_mix1")
    gp_ffn0a, (gb_mix1,) = _tn([dg0], h1, 1408, "dw_gate0", (chunk(2 * N_FF), N_FF, OFF_GATE),
                               _chip_exchange_comm(pb_mix1))
    gsh_mix1 = _rs_final_sum(pb_mix1, gb_mix1, chip_arr, "rs_final_sum_mix1")
    gp_ffn0a = _tn([du0], h1, 1408, "dw_up0", (gp_ffn0a, N_FF, OFF_UP))
    gp_ffn0b, (ga_ffn0a,) = _tn([act0], dx2, 1408, "dw_down0", (chunk(N_FF), N_FF, 0),
                                _pair_exchange_comm(gp_ffn0a))
    pb_ffn0a = _rs_pair_sum(gp_ffn0a, ga_ffn0a, c_arr, "rs_pair_sum_ffn0a")

    (dx0, dproj_e, mix_e, h0, dgam_m0, dws, dbc, dlng, dcw), (gb_ffn0a, ga_ffn0b) = _even_bwd(
        x0, proj_e, dx1, w_mix0, mix_norm[0:1], sg_ln_g, ws, bst, cw, seq,
        _both(_chip_exchange_comm(pb_ffn0a), _pair_exchange_comm(gp_ffn0b)))
    pb_ffn0b = _rs_pair_sum(gp_ffn0b, ga_ffn0b, c_arr, "rs_pair_sum_ffn0b")
    gsh_ffn0a = _rs_final_sum(pb_ffn0a, gb_ffn0a, chip_arr, "rs_final_sum_ffn0a")
    gp_mix0, (gb_ffn0b,) = _tn([mix_e], dx1, D, "dw_eout", (chunk(R_MIX0, True), N_SQ, OFF_EOUT),
                               _chip_exchange_comm(pb_ffn0b))
    gsh_ffn0b = _rs_final_sum(pb_ffn0b, gb_ffn0b, chip_arr, "rs_final_sum_ffn0b")

    small = _small_pack([
        jnp.concatenate([dgam_m0, dgam_m1], 0), jnp.concatenate([dgam_f0, dgam_f1], 0), dlng,
        dws[None], dbc[:, :SG_HEADS].T[None], dcw[:3],
        jnp.stack([dpw_bd[g * POOL_GD:(g + 1) * POOL_GD, g * POOL_GD:(g + 1) * POOL_GD] for g in range(4)])[None],
        dps, dqa, dkva, dgq[:, :QK_DIM] * ATT_SCALE, dgk[:, :QK_DIM], loss_tile[0:1, 0:1]])
    gp_mix0, (small_all,) = _tn([dproj_e], h0, 1280, "dw_ein", (gp_mix0, N_EIN, OFF_EIN), _gather_comm([small]))
    small_sum = _small_unpack(_sum_gathered(small_all), SMALL_SHAPES)
    return dx0.reshape(bsz, seq, D), (gsh_ffn0a, gsh_ffn0b, gsh_mix1, gsh_ffn1), gp_mix0, small_sum


SMALL_SHAPES = [(2, D), (2, D), (1, SG_W), (1, SG_HEADS, 128, 128), (1, SG_HEADS, 128), (3, SC_W),
                (1, 4, POOL_GD, POOL_GD), (1, POOL_W), (1, Q_LORA), (1, KV_LORA), (1, QK_DIM), (1, QK_DIM), (1, 1)]


def kernel(x, positions, mix_norm, ffn_norm, even_w_in, sg_ln_g, sg_w_s, sg_b_s, sc_conv_w, even_w_out, odd_w_in, pool_w, pool_scale, q_a_norm, q_b, kv_a_norm, kv_b, q_norm, k_norm, odd_w_out, ffn_w_gate, ffn_w_up, ffn_w_down, loss_target, m_mix_norm, m_ffn_norm, m_even_w_in, m_sg_ln_g, m_sg_w_s, m_sg_b_s, m_sc_conv_w, m_even_w_out, m_odd_w_in, m_pool_w, m_pool_scale, m_q_a_norm, m_q_b, m_kv_a_norm, m_kv_b, m_q_norm, m_k_norm, m_odd_w_out, m_ffn_w_gate, m_ffn_w_up, m_ffn_w_down, v_mix_norm, v_ffn_norm, v_even_w_in, v_sg_ln_g, v_sg_w_s, v_sg_b_s, v_sc_conv_w, v_even_w_out, v_odd_w_in, v_pool_w, v_pool_scale, v_q_a_norm, v_q_b, v_kv_a_norm, v_kv_b, v_q_norm, v_k_norm, v_odd_w_out, v_ffn_w_gate, v_ffn_w_up, v_ffn_w_down):
    xi, yi, ci = _place()
    me = 4 * xi + 2 * yi + ci

    chunks = _pack_shards(even_w_in, even_w_out, odd_w_in, q_b, kv_b, odd_w_out, ffn_w_gate, ffn_w_up, ffn_w_down)

    def lane_pad(a):
        return jnp.pad(a, ((0, 0), (0, 128 - a.shape[1])))

    tile = jnp.concatenate([lane_pad(sc_conv_w[0]), lane_pad(pool_scale), lane_pad(q_a_norm), lane_pad(kv_a_norm),
                            jnp.zeros((2, 128), F32)], axis=0)
    c_arr = jnp.reshape(ci, (1,)).astype(jnp.int32)
    chip_arr = jnp.reshape(2 * xi + yi, (1,)).astype(jnp.int32)
    grad_x, (gsh_ffn0a, gsh_ffn0b, gsh_mix1, gsh_ffn1), gp_mix0, tot = _step(
        x, positions, loss_target, chunks, tile, c_arr, chip_arr, mix_norm, ffn_norm, sg_ln_g, sg_w_s, sg_b_s,
        pool_w, q_norm, k_norm)

    (g_mix, g_ffn, g_lng, g_ws, g_bs, g_cw_full, g_pw, g_ps_full, g_qa_full, g_kva_full, g_qn, g_kn, loss) = tot
    g_cw = lax.dynamic_slice_in_dim(g_cw_full, me * 64, 64, axis=1)[None]
    g_ps = lax.dynamic_slice_in_dim(g_ps_full, me * 32, 32, axis=1)
    g_qa = lax.dynamic_slice_in_dim(g_qa_full, me * 48, 48, axis=1)
    g_kva = lax.dynamic_slice_in_dim(g_kva_full, me * 32, 32, axis=1)

    def tr(a):
        return jnp.swapaxes(a, -1, -2)

    g_gate = tr(jnp.stack([gsh_ffn0a[OFF_GATE:OFF_GATE + N_FF], gsh_ffn1[OFF_GATE:OFF_GATE + N_FF]]))
    g_up = tr(jnp.stack([gsh_ffn0a[OFF_UP:OFF_UP + N_FF], gsh_ffn1[OFF_UP:OFF_UP + N_FF]]))
    g_down = jnp.stack([gsh_ffn0b, gsh_ffn1[OFF_DOWN:OFF_DOWN + N_FF]])
    g_oin = gsh_mix1[OFF_OIN:OFF_OIN + N_SQ, :ODD_IN][None]
    g_oout = gsh_mix1[OFF_OOUT:OFF_OOUT + N_SQ][None]
    g_qb = tr(gsh_mix1[OFF_QB:OFF_QB + N_QB_USED].reshape(1, 144, Q_LORA))
    g_kvb = tr(gsh_mix1[OFF_KVB:OFF_KVB + N_KVB].reshape(1, 192, KV_LORA))
    transposed = ("even_w_in", "odd_w_in", "q_b", "kv_b", "ffn_w_gate", "ffn_w_up")

    names = ("mix_norm", "ffn_norm", "even_w_in", "sg_ln_g", "sg_w_s", "sg_b_s", "sc_conv_w", "even_w_out",
             "odd_w_in", "pool_w", "pool_scale", "q_a_norm", "q_b", "kv_a_norm", "kv_b", "q_norm", "k_norm",
             "odd_w_out", "ffn_w_gate", "ffn_w_up", "ffn_w_down")
    grads = dict(mix_norm=g_mix, ffn_norm=g_ffn, sg_ln_g=g_lng, sg_w_s=g_ws, sg_b_s=g_bs,
                 sc_conv_w=g_cw, odd_w_in=g_oin, pool_w=g_pw, pool_scale=g_ps, q_a_norm=g_qa,
                 q_b=g_qb, kv_a_norm=g_kva, kv_b=g_kvb, q_norm=g_qn, k_norm=g_kn, odd_w_out=g_oout,
                 ffn_w_gate=g_gate, ffn_w_up=g_up, ffn_w_down=g_down)
    weights = dict(mix_norm=mix_norm, ffn_norm=ffn_norm, even_w_in=even_w_in, sg_ln_g=sg_ln_g, sg_w_s=sg_w_s,
                   sg_b_s=sg_b_s, sc_conv_w=sc_conv_w, even_w_out=even_w_out, odd_w_in=odd_w_in, pool_w=pool_w,
                   pool_scale=pool_scale, q_a_norm=q_a_norm, q_b=q_b, kv_a_norm=kv_a_norm, kv_b=kv_b, q_norm=q_norm,
                   k_norm=k_norm, odd_w_out=odd_w_out, ffn_w_gate=ffn_w_gate, ffn_w_up=ffn_w_up,
                   ffn_w_down=ffn_w_down)
    m_in = dict(mix_norm=m_mix_norm, ffn_norm=m_ffn_norm, even_w_in=m_even_w_in, sg_ln_g=m_sg_ln_g, sg_w_s=m_sg_w_s,
                sg_b_s=m_sg_b_s, sc_conv_w=m_sc_conv_w, even_w_out=m_even_w_out, odd_w_in=m_odd_w_in,
                pool_w=m_pool_w, pool_scale=m_pool_scale, q_a_norm=m_q_a_norm, q_b=m_q_b, kv_a_norm=m_kv_a_norm,
                kv_b=m_kv_b, q_norm=m_q_norm, k_norm=m_k_norm, odd_w_out=m_odd_w_out, ffn_w_gate=m_ffn_w_gate,
                ffn_w_up=m_ffn_w_up, ffn_w_down=m_ffn_w_down)
    v_in = dict(mix_norm=v_mix_norm, ffn_norm=v_ffn_norm, even_w_in=v_even_w_in, sg_ln_g=v_sg_ln_g, sg_w_s=v_sg_w_s,
                sg_b_s=v_sg_b_s, sc_conv_w=v_sc_conv_w, even_w_out=v_even_w_out, odd_w_in=v_odd_w_in,
                pool_w=v_pool_w, pool_scale=v_pool_scale, q_a_norm=v_q_a_norm, q_b=v_q_b, kv_a_norm=v_kv_a_norm,
                kv_b=v_kv_b, q_norm=v_q_norm, k_norm=v_k_norm, odd_w_out=v_odd_w_out, ffn_w_gate=v_ffn_w_gate,
                ffn_w_up=v_ffn_w_up, ffn_w_down=v_ffn_w_down)
    delta, new_m, new_v = {}, {}, {}

    def as2d(k, a):
        a = tr(a) if k in transposed else a
        return a.reshape(-1, a.shape[-1])

    def back(k, a):
        shape = weights[k].shape
        return tr(a.reshape(shape[:-2] + (shape[-1], shape[-2]))) if k in transposed else a.reshape(shape)

    def update(group, name, nblk=1, comm=None):
        res = _adamw([as2d(k, weights[k]) for k in group], [as2d(k, grads[k]) for k in group],
                     [as2d(k, m_in[k]) for k in group], [as2d(k, v_in[k]) for k in group], name, nblk, comm)
        outs, extra = res if comm is not None else (res, [])
        for i, k in enumerate(group):
            delta[k], new_m[k], new_v[k] = (back(k, o[i]) for o in outs)
        return extra

    (ga_mix0,) = update(["ffn_w_down"], "adamw_ffn_w_down", 2, _pair_exchange_comm(gp_mix0))
    pb_mix0 = _rs_pair_sum(gp_mix0, ga_mix0, c_arr, "rs_pair_sum_mix0")
    (gb_mix0,) = update(["ffn_w_gate", "ffn_w_up", "odd_w_in", "odd_w_out"], "adamw_ffn_odd", 4,
                        _chip_exchange_comm(pb_mix0))
    gsh_mix0 = _rs_final_sum(pb_mix0, gb_mix0, chip_arr, "rs_final_sum_mix0")
    grads["even_w_in"] = tr(gsh_mix0[OFF_EIN:OFF_EIN + N_EIN][None])
    grads["even_w_out"] = gsh_mix0[OFF_EOUT:OFF_EOUT + N_SQ][None]
    update(["even_w_in", "even_w_out"], "adamw_mix0", 2)
    update([k for k in names if k not in delta], "adamw_small")

    return (loss.reshape(()), grad_x, *[grads[k] for k in names], *[delta[k] for k in names],
            *[new_m[k] for k in names], *[new_v[k] for k in names])
```

```python
import functools

import numpy as np
import jax
import jax.numpy as jnp
from jax import lax
from jax.experimental import pallas as pl
from jax.experimental.pallas import tpu as pltpu

F32 = jnp.float32
BF16 = jnp.bfloat16
MESH = pl.DeviceIdType.MESH

D = 1024
EPS = 1e-6
NEG_INF = -1e30
SG_HEADS, SG_HD, SG_W, SG_CHUNK = 4, 128, 512, 128
SC_W = 512
EVEN_IN = 2560
POOL_W = 256
POOL_GD = 64
Q_LORA, KV_LORA, QK_ROPE, QK_NOPE, V_DIM = 384, 256, 64, 128, 128
QK_DIM = QK_NOPE + QK_ROPE
HEADS = 6
HP = 256
ODD_IN = 960
D_FF = 2816
ROPE_THETA = 10000.0
ATT_SCALE = QK_DIM ** -0.5
LR, B1, B2, ADAM_EPS, WD, STEP = 0.001, 0.9, 0.999, 1e-08, 0.01, 10

N_DEV = 8
TB = 512
TB_FFN_BWD = 256
TK_DW = 1024
HALO = 16
VMEM_LIMIT = 56 * 1024 * 1024

N_EIN, N_FF, N_SQ = 320, 352, 128
OFF_EIN, OFF_EOUT, R_MIX0 = 0, 384, 512
OFF_GATE, OFF_UP, OFF_DOWN, R_FFN = 0, 352, 704, 1056
OFF_OIN, OFF_OOUT, OFF_QB, OFF_KVB, R_MIX1 = 0, 128, 256, 320, 384
N_QB, N_QB_USED, N_KVB = 64, 54, 48

INV_SQRT2 = 0.7071067811865476
INV_SQRT_2PI = 0.3989422804014327


def _dot(a, b, ca, cb):
    return lax.dot_general(a, b, (((ca,), (cb,)), ((), ())), preferred_element_type=F32)


def _cparams(n_axes=1):
    return pltpu.CompilerParams(dimension_semantics=("arbitrary",) * n_axes, vmem_limit_bytes=VMEM_LIMIT)


def _wspec(n, off, arity=1):
    assert off % n == 0
    idx = off // n
    if arity == 1:
        return pl.BlockSpec((N_DEV, n, D), lambda i: (0, idx, 0), pipeline_mode=pl.Buffered(1))
    return pl.BlockSpec((N_DEV, n, D), lambda i, j: (0, idx, 0), pipeline_mode=pl.Buffered(1))


def _const_spec(shape):
    zeros = (0,) * len(shape)
    return pl.BlockSpec(shape, lambda *_: zeros)


class _Comm:
    def __init__(self, ins, out_shapes, sems, start, wait, mid=None):
        self.ins, self.out_shapes, self.sems, self.start, self.wait, self.mid = ins, out_shapes, sems, start, wait, mid


def _both(c1, c2):
    def split(f1, f2):
        def run(ins, outs, sems):
            f1(ins[:len(c1.ins)], outs[:len(c1.out_shapes)], sems[:len(c1.sems)])
            f2(ins[len(c1.ins):], outs[len(c1.out_shapes):], sems[len(c1.sems):])
        return run

    assert c1.mid is None and c2.mid is None
    return _Comm(c1.ins + c2.ins, c1.out_shapes + c2.out_shapes, c1.sems + c2.sems,
                 split(c1.start, c2.start), split(c1.wait, c2.wait))


def _call(body, name, grid, in_specs, out_specs, out_shape, args, scratch_shapes=(), comm=None, aliases=None):
    n_axes = len(grid)
    aliases = aliases or {}
    if comm is None:
        res = pl.pallas_call(
            body, name=name, grid=grid, in_specs=list(in_specs), out_specs=list(out_specs),
            out_shape=list(out_shape), scratch_shapes=list(scratch_shapes), input_output_aliases=aliases,
            compiler_params=_cparams(n_axes))(*args)
        return list(res), []
    ni, no, ns = len(in_specs), len(out_specs), len(scratch_shapes)
    ci, co = len(comm.ins), len(comm.out_shapes)
    n_steps = int(np.prod(grid))

    def carrier(*refs):
        ins, cin = refs[:ni], refs[ni:ni + ci]
        outs, cout = refs[ni + ci:ni + ci + no], refs[ni + ci + no:ni + ci + no + co]
        scr, sems = refs[ni + ci + no + co:ni + ci + no + co + ns], refs[ni + ci + no + co + ns:]
        step = 0
        for a in range(n_axes):
            step = step * grid[a] + pl.program_id(a)

        @pl.when(step == 0)
        def _():
            comm.start(cin, cout, sems)

        body(*ins, *outs, *scr)

        if comm.mid is not None and n_steps >= 4:
            @pl.when(step == (3 * n_steps) // 4)
            def _():
                comm.mid(cin, cout, sems)

        @pl.when(step == n_steps - 1)
        def _():
            if comm.mid is not None and n_steps < 4:
                comm.mid(cin, cout, sems)
            comm.wait(cin, cout, sems)

    any_spec = pl.BlockSpec(memory_space=pl.ANY)
    res = pl.pallas_call(
        carrier, name=name, grid=grid, in_specs=list(in_specs) + [any_spec] * ci,
        out_specs=list(out_specs) + [any_spec] * co, out_shape=list(out_shape) + list(comm.out_shapes),
        scratch_shapes=list(scratch_shapes) + list(comm.sems), input_output_aliases=aliases,
        compiler_params=_cparams(n_axes))(*args, *comm.ins)
    return list(res[:no]), list(res[no:])


def _comm_alone(comm, name):
    ci, co = len(comm.ins), len(comm.out_shapes)

    def body(*refs):
        cin, cout, sems = refs[:ci], refs[ci:ci + co], refs[ci + co:]
        comm.start(cin, cout, sems)
        if comm.mid is not None:
            comm.mid(cin, cout, sems)
        comm.wait(cin, cout, sems)

    any_spec = pl.BlockSpec(memory_space=pl.ANY)
    res = pl.pallas_call(
        body, name=name, out_shape=list(comm.out_shapes), in_specs=[any_spec] * ci, out_specs=[any_spec] * co,
        scratch_shapes=list(comm.sems))(*comm.ins)
    return list(res)


def _rms(x, g, n=None):
    n = x.shape[-1] if n is None else n
    r = lax.rsqrt(jnp.sum(x * x, axis=-1, keepdims=True) / n + EPS)
    return x * r * g, r


def _rms_bwd(x, r, g, dy, n=None):
    n = x.shape[-1] if n is None else n
    xh = x * r
    dxh = dy * g
    dx = r * (dxh - xh * (jnp.sum(dxh * xh, axis=-1, keepdims=True) / n))
    dg = jnp.sum(dy * xh, axis=0, keepdims=True)
    return dx, dg


def _gelu(x):
    return 0.5 * x * (1.0 + lax.erf(x * INV_SQRT2))


def _gelu_grad(x):
    return 0.5 * (1.0 + lax.erf(x * INV_SQRT2)) + x * jnp.exp(-0.5 * x * x) * INV_SQRT_2PI


def _shift_down(a, k):
    rows = lax.broadcasted_iota(jnp.int32, a.shape, 0)
    return jnp.where(rows >= k, pltpu.roll(a, k, 0), 0.0)


def _shift_up(a, k):
    n = a.shape[0]
    rows = lax.broadcasted_iota(jnp.int32, a.shape, 0)
    return jnp.where(rows < n - k, pltpu.roll(a, n - k, 0), 0.0)


def _tril_bf16(w):
    r = lax.broadcasted_iota(jnp.int32, w.shape, 0)
    c = lax.broadcasted_iota(jnp.int32, w.shape, 1)
    return jnp.where(r >= c, w, 0.0).astype(BF16)


def _ln_head(vh, g):
    mu = jnp.mean(vh, axis=-1, keepdims=True)
    xc = vh - mu
    rr = lax.rsqrt(jnp.mean(xc * xc, axis=-1, keepdims=True) + EPS)
    xh = xc * rr
    return xh * g, xh, rr


def _conv_fwd(z, tail, cw_ref):
    ext = jnp.concatenate([tail, z], axis=0)
    zs1 = _shift_down(ext, 1)[HALO:]
    zs2 = _shift_down(ext, 2)[HALO:]
    y = cw_ref[2:3, :] * z + cw_ref[1:2, :] * zs1 + cw_ref[0:1, :] * zs2
    return y, zs1, zs2


def _pool_cnt(shape, blk_in_seq):
    rows = lax.broadcasted_iota(jnp.int32, shape, 0)
    grp = lax.broadcasted_iota(jnp.int32, shape, 1) // POOL_GD
    win = jnp.where(grp == 0, 2, jnp.where(grp == 1, 4, jnp.where(grp == 2, 8, 16)))
    tpos = blk_in_seq * shape[0] + rows + 1
    return jnp.minimum(tpos, win).astype(F32), grp


def _pool_select(grp, s2, s4, s8, s16):
    return jnp.where(grp == 0, s2, jnp.where(grp == 1, s4, jnp.where(grp == 2, s8, s16)))


def _pool_fwd(z, tail, blk_in_seq):
    ext = jnp.concatenate([tail, z], axis=0)
    s2 = ext + _shift_down(ext, 1)
    s4 = s2 + _shift_down(s2, 2)
    s8 = s4 + _shift_down(s4, 4)
    s16 = s8 + _shift_down(s8, 8)
    cnt, grp = _pool_cnt(z.shape, blk_in_seq)
    sums = _pool_select(grp, s2[HALO:], s4[HALO:], s8[HALO:], s16[HALO:])
    return sums / cnt - z, cnt, grp


def _pool_bwd(dpooled, dpm, head, grp):
    n = dpm.shape[0]
    ext = jnp.concatenate([dpm, head], axis=0)
    u2 = ext + _shift_up(ext, 1)
    u4 = u2 + _shift_up(u2, 2)
    u8 = u4 + _shift_up(u4, 4)
    u16 = u8 + _shift_up(u8, 8)
    return _pool_select(grp, u2[:n], u4[:n], u8[:n], u16[:n]) - dpooled


def _rope(y1, c, s1, s2):
    return y1 * c + pltpu.roll(y1, 96, 1) * s1 + pltpu.roll(y1, 32, 1) * s2


def _rope_bwd(d1, c, s1, s2):
    return d1 * c + pltpu.roll(d1 * s1, 32, 1) + pltpu.roll(d1 * s2, 96, 1)


def _qk_prep(xh_in, g, c, s1, s2):
    y, r = _rms(xh_in, g, QK_DIM)
    out = jnp.concatenate([y[:, :128], _rope(y[:, 128:], c, s1, s2)], axis=1)
    return out, r


def _qk_prep_bwd(dout, x_in, r, g, c, s1, s2):
    dy = jnp.concatenate([dout[:, :128], _rope_bwd(dout[:, 128:], c, s1, s2)], axis=1)
    return _rms_bwd(x_in, r, g, dy, QK_DIM)


def _place():
    return lax.axis_index("x"), lax.axis_index("y"), lax.axis_index("c")


def _gather_comm(arrs):
    n = len(arrs)

    def plan(ins, outs, sems):
        send_sems, recv_sems, local_sems = sems
        x, y, c = _place()
        me, sibling = (x, y, c), (x, y, 1 - c)
        chips = [(1 - x, y), (x, 1 - y), (1 - x, 1 - y)]

        def slot(a, px, py, pc):
            return outs[a].at[4 * px + 2 * py + pc]

        def copy(a, k, block, to, src=None):
            return pltpu.make_async_remote_copy(
                src_ref=slot(a, *block) if src is None else src, dst_ref=slot(a, *block),
                send_sem=send_sems.at[a, k], recv_sem=recv_sems.at[a, k], device_id=to, device_id_type=MESH)

        def own():
            mine = [pltpu.make_async_copy(ins[a], slot(a, *me), local_sems.at[a]) for a in range(n)]
            first = []
            for a in range(n):
                first.append(copy(a, 0, me, sibling, src=ins[a]))
                first += [copy(a, 1 + j, me, (*chip, c), src=ins[a]) for j, chip in enumerate(chips)]
            return mine, first

        return c, me, sibling, chips, copy, own

    def start(ins, outs, sems):
        mine, first = plan(ins, outs, sems)[-1]()
        for cp in mine + first:
            cp.start()

    def mid(ins, outs, sems):
        c, me, sibling, chips, copy, _ = plan(ins, outs, sems)
        for j, chip in enumerate(chips):
            for a in range(n):
                copy(a, 1 + j, (*chip, c), me).wait_recv()
                copy(a, 4 + j, (*chip, c), sibling).start()

    def wait(ins, outs, sems):
        c, me, sibling, chips, copy, own = plan(ins, outs, sems)
        mine, first = own()
        passed = [copy(a, 4 + j, (*chip, c), sibling) for j, chip in enumerate(chips) for a in range(n)]
        for a in range(n):
            copy(a, 0, sibling, me).wait_recv()
            for j, chip in enumerate(chips):
                copy(a, 4 + j, (*chip, 1 - c), me).wait_recv()
        for cp in first + passed:
            cp.wait_send()
        for cp in mine:
            cp.wait()

    return _Comm(
        list(arrs), [jax.ShapeDtypeStruct((N_DEV,) + a.shape, a.dtype) for a in arrs],
        [pltpu.SemaphoreType.DMA((n, 7)), pltpu.SemaphoreType.DMA((n, 7)), pltpu.SemaphoreType.DMA((n,))],
        start, wait, mid)


def _sum_gathered(g):
    rows = g.shape[1]

    def body(g_ref, sum_ref):
        total = g_ref[0]
        for d in range(1, N_DEV):
            total = total + g_ref[d]
        sum_ref[...] = total

    return pl.pallas_call(
        body, name="sum_gathered_small", out_shape=jax.ShapeDtypeStruct((rows, 128), F32), grid=(1,),
        in_specs=[pl.BlockSpec((N_DEV, rows, 128), lambda i: (0, 0, 0))],
        out_specs=pl.BlockSpec((rows, 128), lambda i: (0, 0)), compiler_params=_cparams(1),
    )(g)


def _sum_rows(rows):
    return rows if rows <= 512 else rows // 2


def _pair_exchange_comm(gp):
    _, rows, cols = gp.shape

    def copies(ins, outs, sems):
        send_sems, recv_sems = sems
        x, y, c = _place()
        return [pltpu.make_async_remote_copy(
            src_ref=ins[0].at[2 * j + (1 - c)], dst_ref=outs[0].at[j], send_sem=send_sems.at[j],
            recv_sem=recv_sems.at[j], device_id=(x, y, 1 - c), device_id_type=MESH) for j in range(4)]

    def start(ins, outs, sems):
        for cp in copies(ins, outs, sems):
            cp.start()

    def wait(ins, outs, sems):
        for cp in copies(ins, outs, sems):
            cp.wait()

    return _Comm([gp], [jax.ShapeDtypeStruct((4, rows, cols), gp.dtype)],
                 [pltpu.SemaphoreType.DMA((4,)), pltpu.SemaphoreType.DMA((4,))], start, wait)


def _rs_pair_sum(gp, got, c_arr, name):
    _, rows, cols = got.shape
    rb = _sum_rows(rows)
    gp4 = gp.reshape(4, 2, rows, cols)

    def body(c_ref, a_ref, b_ref, o_ref):
        o_ref[0] = (a_ref[0, 0].astype(F32) + b_ref[0].astype(F32)).astype(o_ref.dtype)

    return pl.pallas_call(
        body, name=name, out_shape=jax.ShapeDtypeStruct((4, rows, cols), gp.dtype),
        grid_spec=pltpu.PrefetchScalarGridSpec(
            num_scalar_prefetch=1, grid=(4, rows // rb),
            in_specs=[pl.BlockSpec((1, 1, rb, cols), lambda j, r, cr: (j, cr[0], r, 0)),
                      pl.BlockSpec((1, rb, cols), lambda j, r, cr: (j, r, 0))],
            out_specs=pl.BlockSpec((1, rb, cols), lambda j, r, cr: (j, r, 0))),
        compiler_params=_cparams(2),
    )(c_arr, gp4, got)


def _chip_exchange_comm(pb):
    _, rows, cols = pb.shape

    def copies(ins, outs, sems):
        send_sems, recv_sems = sems
        x, y, c = _place()
        chips = [(1 - x, y), (x, 1 - y), (1 - x, 1 - y)]
        return [pltpu.make_async_remote_copy(
            src_ref=ins[0].at[2 * px + py], dst_ref=outs[0].at[k], send_sem=send_sems.at[k],
            recv_sem=recv_sems.at[k], device_id=(px, py, c), device_id_type=MESH)
            for k, (px, py) in enumerate(chips)]

    def start(ins, outs, sems):
        for cp in copies(ins, outs, sems):
            cp.start()

    def wait(ins, outs, sems):
        for cp in copies(ins, outs, sems):
            cp.wait()

    return _Comm([pb], [jax.ShapeDtypeStruct((3, rows, cols), pb.dtype)],
                 [pltpu.SemaphoreType.DMA((3,)), pltpu.SemaphoreType.DMA((3,))], start, wait)


def _rs_final_sum(pb, got, chip_arr, name):
    _, rows, cols = got.shape
    rb = _sum_rows(rows)

    def body(j_ref, a_ref, b_ref, o_ref):
        o_ref[...] = ((a_ref[0].astype(F32) + b_ref[0].astype(F32)) + b_ref[1].astype(F32)) + b_ref[2].astype(F32)

    return pl.pallas_call(
        body, name=name, out_shape=jax.ShapeDtypeStruct((rows, cols), F32),
        grid_spec=pltpu.PrefetchScalarGridSpec(
            num_scalar_prefetch=1, grid=(rows // rb,),
            in_specs=[pl.BlockSpec((1, rb, cols), lambda r, jr: (jr[0], r, 0)),
                      pl.BlockSpec((3, rb, cols), lambda r, jr: (0, r, 0))],
            out_specs=pl.BlockSpec((rb, cols), lambda r, jr: (r, 0))),
        compiler_params=_cparams(1),
    )(chip_arr, pb, got)


def _rope_tables(pos_col, inv_freq, comm=None):
    t = pos_col.shape[0]

    def body(p_ref, f_ref, c_ref, s1_ref, s2_ref):
        ang = p_ref[...].astype(F32) * f_ref[...]
        lane = lax.broadcasted_iota(jnp.int32, ang.shape, 1)
        c_ref[...] = jnp.where(lane < QK_ROPE, jnp.cos(ang), 0.0)
        s = jnp.sin(ang)
        s1_ref[...] = jnp.where(lane < 32, -s, 0.0)
        s2_ref[...] = jnp.where((lane >= 32) & (lane < QK_ROPE), s, 0.0)

    spec = pl.BlockSpec((TB, 128), lambda i: (i, 0))
    return _call(
        body, "rope_tables", (t // TB,), [pl.BlockSpec((TB, 1), lambda i: (i, 0)), _const_spec((1, 128))],
        [spec] * 3, [jax.ShapeDtypeStruct((t, 128), F32)] * 3, (pos_col, inv_freq), (), comm)


def _sgu_conv_fwd(proj, tail, lng_ref, ws_ref, bst_ref, cw_ref):
    gu = _gelu(proj[:, 0:SG_W])
    gv = _gelu(proj[:, SG_W:2 * SG_W])
    bg = proj[:, 1024:1536]
    z = proj[:, 1536:2048] * proj[:, 2048:2560]
    heads = []
    for h in range(SG_HEADS):
        sl = slice(h * SG_HD, (h + 1) * SG_HD)
        vn, _, _ = _ln_head(gv[:, sl], lng_ref[:, sl])
        vnb = vn.astype(BF16)
        wm = _tril_bf16(ws_ref[h])
        bcol = bst_ref[:, h:h + 1]
        mixed = jnp.concatenate(
            [_dot(wm, vnb[k * SG_CHUNK:(k + 1) * SG_CHUNK], 1, 0) + bcol for k in range(TB // SG_CHUNK)], axis=0)
        heads.append(gu[:, sl] * mixed)
    a_out = jnp.concatenate(heads, axis=1)
    y, _, _ = _conv_fwd(z, tail, cw_ref)
    return a_out, bg * y, z


def _even_fwd(x, wg, gamma, lng, ws, bst, cw, seq, comm=None):
    t = x.shape[0]
    nbs = seq // TB

    def body(x_ref, gam_ref, win_ref, wout_ref, lng_ref, ws_ref, bst_ref, cw_ref, x1_ref, proj_ref, tail_ref):
        i = pl.program_id(0)
        xv = x_ref[...]
        h, _ = _rms(xv, gam_ref[...])
        proj = _dot(h.astype(BF16), win_ref[...].reshape(EVEN_IN, D), 1, 1)
        proj_ref[...] = proj.astype(BF16)
        tail = jnp.where(i % nbs == 0, 0.0, tail_ref[...])
        a_out, b_out, z = _sgu_conv_fwd(proj, tail, lng_ref, ws_ref, bst_ref, cw_ref)
        tail_ref[...] = z[TB - HALO:, :]
        x1_ref[...] = (xv + _dot(a_out.astype(BF16), wout_ref[0:4].reshape(512, D), 1, 0)
                       + _dot(b_out.astype(BF16), wout_ref[4:8].reshape(512, D), 1, 0))

    row = pl.BlockSpec((TB, D), lambda i: (i, 0))
    return _call(
        body, "even_fwd", (t // TB,),
        [row, _const_spec((1, D)), _wspec(N_EIN, OFF_EIN), _wspec(N_SQ, OFF_EOUT), _const_spec((1, SG_W)),
         _const_spec((SG_HEADS, 128, 128)), _const_spec((128, 128)), _const_spec((8, SC_W))],
        [row, pl.BlockSpec((TB, EVEN_IN), lambda i: (i, 0))],
        [jax.ShapeDtypeStruct((t, D), F32), jax.ShapeDtypeStruct((t, EVEN_IN), BF16)],
        (x, gamma, wg, wg, lng, ws, bst, cw), [pltpu.VMEM((HALO, SC_W), F32)], comm)


def _even_bwd(x, proj, dx1, wg, gamma, lng, ws, bst, cw, seq, comm=None):
    t = x.shape[0]
    nb, nbs = t // TB, seq // TB

    def body(x_ref, proj_ref, ptail_ref, dx1_ref, gam_ref, win_ref, wout_ref, lng_ref, ws_ref, bst_ref, cw_ref,
             dx0_ref, dproj_ref, mix_ref, h_ref, dgam_ref, dws_ref, dbc_ref, dlng_ref, dcw_ref, head_ref):
        i = pl.program_id(0)
        blk = nb - 1 - i

        @pl.when(i == 0)
        def _():
            dgam_ref[...] = jnp.zeros_like(dgam_ref)
            dws_ref[...] = jnp.zeros_like(dws_ref)
            dbc_ref[...] = jnp.zeros_like(dbc_ref)
            dlng_ref[...] = jnp.zeros_like(dlng_ref)
            dcw_ref[...] = jnp.zeros_like(dcw_ref)

        xv = x_ref[...]
        gam = gam_ref[...]
        h, r = _rms(xv, gam)
        h_ref[...] = h.astype(BF16)
        dx1 = dx1_ref[...]
        dmix = _dot(dx1.astype(BF16), wout_ref[...].reshape(D, D), 1, 1)
        da, db = dmix[:, :SG_W], dmix[:, SG_W:]
        proj = proj_ref[...].astype(F32)
        u, v = proj[:, 0:SG_W], proj[:, SG_W:2 * SG_W]
        bg, cg, hv = proj[:, 1024:1536], proj[:, 1536:2048], proj[:, 2048:2560]
        gu, gv = _gelu(u), _gelu(v)

        a_heads, dgv_heads = [], []
        for hd in range(SG_HEADS):
            sl = slice(hd * SG_HD, (hd + 1) * SG_HD)
            g_h = lng_ref[:, sl]
            vn, xh, rr = _ln_head(gv[:, sl], g_h)
            vnb = vn.astype(BF16)
            wm = _tril_bf16(ws_ref[hd])
            bcol = bst_ref[:, hd:hd + 1]
            mixed_c, dvn_c = [], []
            dw_acc = jnp.zeros((128, 128), F32)
            db_acc = jnp.zeros((128, 1), F32)
            for k in range(TB // SG_CHUNK):
                rs = slice(k * SG_CHUNK, (k + 1) * SG_CHUNK)
                mixed = _dot(wm, vnb[rs], 1, 0) + bcol
                dmixed = da[rs, sl] * gu[rs, sl]
                dmb = dmixed.astype(BF16)
                dvn_c.append(_dot(wm, dmb, 0, 0))
                dw_acc = dw_acc + _dot(dmb, vnb[rs], 1, 1)
                db_acc = db_acc + jnp.sum(dmixed, axis=1, keepdims=True)
                mixed_c.append(mixed)
            mixed_h = jnp.concatenate(mixed_c, axis=0)
            dvn = jnp.concatenate(dvn_c, axis=0)
            r_i = lax.broadcasted_iota(jnp.int32, (128, 128), 0)
            c_i = lax.broadcasted_iota(jnp.int32, (128, 128), 1)
            dws_ref[hd] += jnp.where(r_i >= c_i, dw_acc, 0.0)
            dbc_ref[:, hd:hd + 1] += db_acc
            dlng_ref[:, sl] += jnp.sum(dvn * xh, axis=0, keepdims=True)
            dxh = dvn * g_h
            dgv = rr * (dxh - jnp.mean(dxh, axis=-1, keepdims=True)
                        - xh * jnp.mean(dxh * xh, axis=-1, keepdims=True))
            a_heads.append(gu[:, sl] * mixed_h)
            dproj_ref[:, sl] = (da[:, sl] * mixed_h * _gelu_grad(u[:, sl])).astype(BF16)
            dgv_heads.append(dgv * _gelu_grad(v[:, sl]))
        dproj_ref[:, SG_W:2 * SG_W] = jnp.concatenate(dgv_heads, axis=1).astype(BF16)
        mix_ref[:, :SG_W] = jnp.concatenate(a_heads, axis=1).astype(BF16)

        z = cg * hv
        pt = ptail_ref[...].astype(F32)
        tail = jnp.where(blk % nbs == 0, 0.0, pt[:, 1536:2048] * pt[:, 2048:2560])
        y, zs1, zs2 = _conv_fwd(z, tail, cw_ref)
        mix_ref[:, SG_W:] = (bg * y).astype(BF16)
        dy = db * bg
        head = jnp.where(blk % nbs == nbs - 1, 0.0, head_ref[...])
        ext = jnp.concatenate([dy, head], axis=0)
        dz = (cw_ref[2:3, :] * dy + cw_ref[1:2, :] * _shift_up(ext, 1)[:TB]
              + cw_ref[0:1, :] * _shift_up(ext, 2)[:TB])
        head_ref[...] = dy[:HALO, :]
        dcw_ref[2:3, :] += jnp.sum(dy * z, axis=0, keepdims=True)
        dcw_ref[1:2, :] += jnp.sum(dy * zs1, axis=0, keepdims=True)
        dcw_ref[0:1, :] += jnp.sum(dy * zs2, axis=0, keepdims=True)
        dproj_ref[:, 1024:1536] = (db * y).astype(BF16)
        dproj_ref[:, 1536:2048] = (dz * hv).astype(BF16)
        dproj_ref[:, 2048:2560] = (dz * cg).astype(BF16)

        dh = _dot(dproj_ref[...], win_ref[...].reshape(EVEN_IN, D), 1, 0)
        dxn, dgam = _rms_bwd(xv, r, gam, dh)
        dgam_ref[...] += dgam
        dx0_ref[...] = dx1 + dxn

    def rev(w):
        return pl.BlockSpec((TB, w), lambda i: (nb - 1 - i, 0))

    ptail = pl.BlockSpec((HALO, EVEN_IN), lambda i: (jnp.maximum((nb - 1 - i) * (TB // HALO) - 1, 0), 0))
    return _call(
        body, "even_bwd", (nb,),
        [rev(D), rev(EVEN_IN), ptail, rev(D), _const_spec((1, D)), _wspec(N_EIN, OFF_EIN),
         _wspec(N_SQ, OFF_EOUT), _const_spec((1, SG_W)), _const_spec((SG_HEADS, 128, 128)),
         _const_spec((128, 128)), _const_spec((8, SC_W))],
        [rev(D), rev(EVEN_IN), rev(D), rev(D), _const_spec((1, D)), _const_spec((SG_HEADS, 128, 128)),
         _const_spec((128, 128)), _const_spec((1, SG_W)), _const_spec((8, SC_W))],
        [jax.ShapeDtypeStruct((t, D), F32), jax.ShapeDtypeStruct((t, EVEN_IN), BF16),
         jax.ShapeDtypeStruct((t, D), BF16), jax.ShapeDtypeStruct((t, D), BF16),
         jax.ShapeDtypeStruct((1, D), F32), jax.ShapeDtypeStruct((SG_HEADS, 128, 128), F32),
         jax.ShapeDtypeStruct((128, 128), F32), jax.ShapeDtypeStruct((1, SG_W), F32),
         jax.ShapeDtypeStruct((8, SC_W), F32)],
        (x, proj, proj, dx1, gamma, wg, wg, lng, ws, bst, cw), [pltpu.VMEM((HALO, SC_W), F32)], comm)


def _ffn_fwd(x, wg, gamma, name, comm=None, target=None):
    t = x.shape[0]
    last = target is not None

    def body(*refs):
        x_ref, gam_ref, wg_ref, wu_ref, wd_ref = refs[:5]
        y_ref, g_ref, u_ref = refs[5 + last:8 + last]
        xv = x_ref[...]
        h, _ = _rms(xv, gam_ref[...])
        hb = h.astype(BF16)
        g = _dot(hb, wg_ref[...].reshape(D_FF, D), 1, 1)
        u = _dot(hb, wu_ref[...].reshape(D_FF, D), 1, 1)
        g_ref[...] = g.astype(BF16)
        u_ref[...] = u.astype(BF16)
        act = g * jax.nn.sigmoid(g) * u
        y = xv + _dot(act.astype(BF16), wd_ref[...].reshape(D_FF, D), 1, 0)
        if not last:
            y_ref[...] = y
            return
        loss_ref = refs[9]

        @pl.when(pl.program_id(0) == 0)
        def _():
            loss_ref[...] = jnp.zeros_like(loss_ref)

        err = y - refs[5][...]
        y_ref[...] = err * (1.0 / D)
        sq = jnp.sum(jnp.sum(err * err, axis=-1, keepdims=True), axis=0, keepdims=True)
        loss_ref[...] += (0.5 / D) * sq

    row = pl.BlockSpec((TB, D), lambda i: (i, 0))
    wide = pl.BlockSpec((TB, D_FF), lambda i: (i, 0))
    in_specs = [row, _const_spec((1, D)), _wspec(N_FF, OFF_GATE), _wspec(N_FF, OFF_UP), _wspec(N_FF, OFF_DOWN)]
    out_specs = [row, wide, wide]
    out_shape = [jax.ShapeDtypeStruct((t, D), F32), jax.ShapeDtypeStruct((t, D_FF), BF16),
                 jax.ShapeDtypeStruct((t, D_FF), BF16)]
    args = (x, gamma, wg, wg, wg)
    if last:
        in_specs, args = in_specs + [row], args + (target,)
        out_specs, out_shape = out_specs + [_const_spec((8, 128))], out_shape + [jax.ShapeDtypeStruct((8, 128), F32)]
    return _call(body, name, (t // TB,), in_specs, out_specs, out_shape, args, (), comm)


def _ffn_bwd(x, g, u, dy, wg, gamma, name, comm=None):
    t = x.shape[0]

    def body(x_ref, g_ref, u_ref, dy_ref, gam_ref, wg_ref, wu_ref, wd_ref,
             dx_ref, act_ref, dg_ref, du_ref, h_ref, dgam_ref):
        @pl.when(pl.program_id(0) == 0)
        def _():
            dgam_ref[...] = jnp.zeros_like(dgam_ref)

        xv = x_ref[...]
        gam = gam_ref[...]
        h, r = _rms(xv, gam)
        h_ref[...] = h.astype(BF16)
        dyv = dy_ref[...]
        dact = _dot(dyv.astype(BF16), wd_ref[...].reshape(D_FF, D), 1, 1)
        gv = g_ref[...].astype(F32)
        uv = u_ref[...].astype(F32)
        sg = jax.nn.sigmoid(gv)
        silu = gv * sg
        act_ref[...] = (silu * uv).astype(BF16)
        dgb = (dact * uv * (sg * (1.0 + gv * (1.0 - sg)))).astype(BF16)
        dub = (dact * silu).astype(BF16)
        dg_ref[...] = dgb
        du_ref[...] = dub
        dh = _dot(dgb, wg_ref[...].reshape(D_FF, D), 1, 0) + _dot(dub, wu_ref[...].reshape(D_FF, D), 1, 0)
        dxn, dgam = _rms_bwd(xv, r, gam, dh)
        dgam_ref[...] += dgam
        dx_ref[...] = dyv + dxn

    row = pl.BlockSpec((TB_FFN_BWD, D), lambda i: (i, 0))
    wide = pl.BlockSpec((TB_FFN_BWD, D_FF), lambda i: (i, 0))
    return _call(
        body, name, (t // TB_FFN_BWD,),
        [row, wide, wide, row, _const_spec((1, D)), _wspec(N_FF, OFF_GATE), _wspec(N_FF, OFF_UP),
         _wspec(N_FF, OFF_DOWN)],
        [row, wide, wide, wide, row, _const_spec((1, D))],
        [jax.ShapeDtypeStruct((t, D), F32), jax.ShapeDtypeStruct((t, D_FF), BF16),
         jax.ShapeDtypeStruct((t, D_FF), BF16), jax.ShapeDtypeStruct((t, D_FF), BF16),
         jax.ShapeDtypeStruct((t, D), BF16), jax.ShapeDtypeStruct((1, D), F32)],
        (x, g, u, dy, gamma, wg, wg, wg), (), comm)


def _odd_pre_fwd(x, wg, gamma, qbt, kvbt, qa_g, kva_g, pw_bd, pscale, seq):
    t = x.shape[0]
    nbs = seq // TB

    def body(x_ref, gam_ref, win_ref, qb_ref, kvb_ref, qa_ref, kva_ref, pw_ref, ps_ref,
             proj_ref, q_ref, kv_ref, kr_ref, c_ref, tail_ref):
        i = pl.program_id(0)
        h, _ = _rms(x_ref[...], gam_ref[...])
        proj = _dot(h.astype(BF16), win_ref[...].reshape(D, D), 1, 0)
        proj_ref[...] = proj.astype(BF16)
        zp, ql, kvl = proj[:, :POOL_W], proj[:, 256:640], proj[:, 640:896]
        kr_ref[...] = proj[:, 896:1024]
        qn, _ = _rms(ql, qa_ref[...])
        q_ref[...] = _dot(qn.astype(BF16), qb_ref[...], 1, 1).astype(BF16)
        kvn, _ = _rms(kvl, kva_ref[...])
        kv_ref[...] = _dot(kvn.astype(BF16), kvb_ref[...], 1, 1).astype(BF16)
        tail = jnp.where(i % nbs == 0, 0.0, tail_ref[...])
        pooled, _, _ = _pool_fwd(zp, tail, i % nbs)
        tail_ref[...] = zp[TB - HALO:, :]
        c_ref[...] = (_dot(pooled.astype(BF16), pw_ref[...], 1, 0) * ps_ref[...]).astype(BF16)

    def row(w):
        return pl.BlockSpec((TB, w), lambda i: (i, 0))

    return pl.pallas_call(
        body, name="odd_pre_fwd",
        out_shape=[jax.ShapeDtypeStruct((t, D), BF16), jax.ShapeDtypeStruct((t, HEADS * HP), BF16),
                   jax.ShapeDtypeStruct((t, HEADS * HP), BF16), jax.ShapeDtypeStruct((t, 128), F32),
                   jax.ShapeDtypeStruct((t, POOL_W), BF16)],
        grid=(t // TB,),
        in_specs=[row(D), _const_spec((1, D)), _wspec(N_SQ, OFF_OIN), _const_spec((HEADS * HP, Q_LORA)),
                  _const_spec((HEADS * HP, KV_LORA)), _const_spec((1, Q_LORA)), _const_spec((1, KV_LORA)),
                  _const_spec((POOL_W, POOL_W)), _const_spec((1, POOL_W))],
        out_specs=[row(D), row(HEADS * HP), row(HEADS * HP), row(128), row(POOL_W)],
        scratch_shapes=[pltpu.VMEM((HALO, POOL_W), F32)],
        compiler_params=_cparams(1),
    )(x, gamma, wg, qbt, kvbt, qa_g, kva_g, pw_bd, pscale)


def _odd_pre_bwd(x, proj, dx3, dmix, dq, dkv, dkr, wg, gamma, qbt, kvbt, qa_g, kva_g, pw_bd, pscale, seq):
    t = x.shape[0]
    nb, nbs = t // TB, seq // TB

    def body(x_ref, proj_ref, ptail_ref, dx3_ref, dco_ref, dq_ref, dkv_ref, dkr_ref, gam_ref, win_ref, qb_ref,
             kvb_ref, qa_ref, kva_ref, pw_ref, ps_ref,
             dx2_ref, dproj_ref, h_ref, qn_ref, kvn_ref, dgam_ref, dqa_ref, dkva_ref, dpw_ref, dps_ref, head_ref):
        i = pl.program_id(0)
        blk = nb - 1 - i

        @pl.when(i == 0)
        def _():
            dgam_ref[...] = jnp.zeros_like(dgam_ref)
            dqa_ref[...] = jnp.zeros_like(dqa_ref)
            dkva_ref[...] = jnp.zeros_like(dkva_ref)
            dpw_ref[...] = jnp.zeros_like(dpw_ref)
            dps_ref[...] = jnp.zeros_like(dps_ref)

        xv = x_ref[...]
        gam = gam_ref[...]
        h, r = _rms(xv, gam)
        h_ref[...] = h.astype(BF16)
        proj = proj_ref[...].astype(F32)
        zp, ql, kvl = proj[:, :POOL_W], proj[:, 256:640], proj[:, 640:896]

        qa = qa_ref[...]
        qn, rq = _rms(ql, qa)
        qn_ref[...] = qn.astype(BF16)
        dql, dqa = _rms_bwd(ql, rq, qa, _dot(dq_ref[...], qb_ref[...], 1, 0))
        dqa_ref[...] += dqa
        kva = kva_ref[...]
        kvn, rkv = _rms(kvl, kva)
        kvn_ref[...] = kvn.astype(BF16)
        dkvl, dkva = _rms_bwd(kvl, rkv, kva, _dot(dkv_ref[...], kvb_ref[...], 1, 0))
        dkva_ref[...] += dkva

        pt = ptail_ref[...].astype(F32)
        tail = jnp.where(blk % nbs == 0, 0.0, pt[:, :POOL_W])
        pooled, cnt, grp = _pool_fwd(zp, tail, blk % nbs)
        pb = pooled.astype(BF16)
        pw = pw_ref[...]
        dco = dco_ref[...].astype(F32)
        dps_ref[...] += jnp.sum(dco * _dot(pb, pw, 1, 0), axis=0, keepdims=True)
        dpo = (dco * ps_ref[...]).astype(BF16)
        dpw_ref[...] += _dot(pb, dpo, 0, 0)
        dpooled = _dot(dpo, pw, 1, 1)
        dpm = dpooled / cnt
        head = jnp.where(blk % nbs == nbs - 1, 0.0, head_ref[...])
        dz = _pool_bwd(dpooled, dpm, head, grp)
        head_ref[...] = dpm[:HALO, :]

        dproj_ref[:, :POOL_W] = dz.astype(BF16)
        dproj_ref[:, 256:640] = dql.astype(BF16)
        dproj_ref[:, 640:896] = dkvl.astype(BF16)
        dproj_ref[:, 896:1024] = dkr_ref[...].astype(BF16)
        dh = _dot(dproj_ref[...], win_ref[...].reshape(D, D), 1, 1)
        dxn, dgam = _rms_bwd(xv, r, gam, dh)
        dgam_ref[...] += dgam
        dx2_ref[...] = dx3_ref[...] + dxn

    def rev(w):
        return pl.BlockSpec((TB, w), lambda i: (nb - 1 - i, 0))

    ptail = pl.BlockSpec((HALO, D), lambda i: (jnp.maximum((nb - 1 - i) * (TB // HALO) - 1, 0), 0))
    return pl.pallas_call(
        body, name="odd_pre_bwd",
        out_shape=[jax.ShapeDtypeStruct((t, D), F32), jax.ShapeDtypeStruct((t, D), BF16),
                   jax.ShapeDtypeStruct((t, D), BF16), jax.ShapeDtypeStruct((t, Q_LORA), BF16),
                   jax.ShapeDtypeStruct((t, KV_LORA), BF16), jax.ShapeDtypeStruct((1, D), F32),
                   jax.ShapeDtypeStruct((1, Q_LORA), F32), jax.ShapeDtypeStruct((1, KV_LORA), F32),
                   jax.ShapeDtypeStruct((POOL_W, POOL_W), F32), jax.ShapeDtypeStruct((1, POOL_W), F32)],
        grid=(nb,),
        in_specs=[rev(D), rev(D), ptail, rev(D), rev(POOL_W), rev(HEADS * HP), rev(HEADS * HP), rev(128),
                  _const_spec((1, D)), _wspec(N_SQ, OFF_OIN), _const_spec((HEADS * HP, Q_LORA)),
                  _const_spec((HEADS * HP, KV_LORA)), _const_spec((1, Q_LORA)), _const_spec((1, KV_LORA)),
                  _const_spec((POOL_W, POOL_W)), _const_spec((1, POOL_W))],
        out_specs=[rev(D), rev(D), rev(D), rev(Q_LORA), rev(KV_LORA), _const_spec((1, D)), _const_spec((1, Q_LORA)),
                   _const_spec((1, KV_LORA)), _const_spec((POOL_W, POOL_W)), _const_spec((1, POOL_W))],
        scratch_shapes=[pltpu.VMEM((HALO, POOL_W), F32)],
        compiler_params=_cparams(1),
    )(x, proj, proj, dx3, dmix, dq, dkv, dkr, gamma, wg, qbt, kvbt, qa_g, kva_g, pw_bd, pscale)


def _attn_specs(seq):
    head = pl.BlockSpec((seq, HP), lambda b, h: (b, h))
    shared = pl.BlockSpec((seq, 128), lambda b, h: (b, 0))
    gain = pl.BlockSpec((1, HP), lambda b, h: (0, 0))
    return head, shared, gain


def _causal_bias(n):
    rows = lax.broadcasted_iota(jnp.int32, (n, n), 0)
    cols = lax.broadcasted_iota(jnp.int32, (n, n), 1)
    return jnp.where(cols <= rows, 0.0, NEG_INF)


def _attn_fwd(q, kv, kr, cos, s1, s2, gq, gk, seq, comm=None):
    t = q.shape[0]
    qb = min(512, seq)

    def body(q_ref, kv_ref, kr_ref, c_ref, s1_ref, s2_ref, gq_ref, gk_ref, o_ref, lse_ref):
        c, sa, sb = c_ref[...], s1_ref[...], s2_ref[...]
        qf, _ = _qk_prep(q_ref[...].astype(F32), gq_ref[...], c, sa, sb)
        kin = jnp.concatenate([kv_ref[:, :128].astype(F32), kr_ref[...]], axis=1)
        kf, _ = _qk_prep(kin, gk_ref[...], c, sa, sb)
        qf, kf = qf.astype(BF16), kf.astype(BF16)
        v1 = jnp.concatenate([kv_ref[:, 128:], jnp.ones((seq, V_DIM), BF16)], axis=1)
        bias = _causal_bias(qb)
        for q0 in range(0, seq, qb):
            q1 = q0 + qb
            qblk = qf[q0:q1]
            s_dg = _dot(qblk, kf[q0:q1], 1, 1) + bias
            m = jnp.max(s_dg, axis=-1, keepdims=True)
            if q0:
                s_off = _dot(qblk, kf[:q0], 1, 1)
                m = jnp.maximum(m, jnp.max(s_off, axis=-1, keepdims=True))
            acc = _dot(jnp.exp(s_dg - m).astype(BF16), v1[q0:q1], 1, 0)
            if q0:
                acc = acc + _dot(jnp.exp(s_off - m).astype(BF16), v1[:q0], 1, 0)
            l = acc[:, V_DIM:]
            o_ref[q0:q1, :] = (acc[:, :V_DIM] / l).astype(BF16)
            lse_ref[q0:q1, :] = m + jnp.log(l)

    head, shared, gain = _attn_specs(seq)
    per_head = pl.BlockSpec((seq, V_DIM), lambda b, h: (b, h))
    return _call(
        body, "attn_fwd", (t // seq, HEADS),
        [head, head, shared, shared, shared, shared, gain, gain], [per_head, per_head],
        [jax.ShapeDtypeStruct((t, HEADS * V_DIM), BF16), jax.ShapeDtypeStruct((t, HEADS * V_DIM), F32)],
        (q, kv, kr, cos, s1, s2, gq, gk), (), comm)


def _attn_bwd(q, kv, kr, cos, s1, s2, gq, gk, dmix, d_out, lse, seq, comm=None):
    t = q.shape[0]
    qb = min(512, seq)

    def body(q_ref, kv_ref, kr_ref, c_ref, s1_ref, s2_ref, gq_ref, gk_ref, do_ref, o_ref, lse_ref,
             dq_ref, dkv_ref, dkr_ref, dgq_ref, dgk_ref, dqf_ref, dkf_ref, dv_ref):
        b, hd = pl.program_id(0), pl.program_id(1)

        @pl.when((b == 0) & (hd == 0))
        def _():
            dgq_ref[...] = jnp.zeros_like(dgq_ref)
            dgk_ref[...] = jnp.zeros_like(dgk_ref)

        c, sa, sb = c_ref[...], s1_ref[...], s2_ref[...]
        gq_v, gk_v = gq_ref[...], gk_ref[...]
        qin = q_ref[...].astype(F32)
        kin = jnp.concatenate([kv_ref[:, :128].astype(F32), kr_ref[...]], axis=1)
        qf32, rq = _qk_prep(qin, gq_v, c, sa, sb)
        kf32, rk = _qk_prep(kin, gk_v, c, sa, sb)
        qf, kf = qf32.astype(BF16), kf32.astype(BF16)
        vb = kv_ref[:, 128:]
        dkf_ref[...] = jnp.zeros_like(dkf_ref)
        dv_ref[...] = jnp.zeros_like(dv_ref)
        bias = _causal_bias(qb)
        for q0 in range(0, seq, qb):
            q1 = q0 + qb
            qblk = qf[q0:q1]
            do = do_ref[q0:q1, :]
            lse_col = lse_ref[q0:q1, 0:1]
            d_col = jnp.sum(do.astype(F32) * o_ref[q0:q1, :].astype(F32), axis=-1, keepdims=True)
            dq_acc = None
            for k0, k1, diag in ((q0, q1, True), (0, q0, False)):
                if k1 == k0:
                    continue
                s = _dot(qblk, kf[k0:k1], 1, 1)
                p = jnp.exp((s + bias if diag else s) - lse_col)
                dv_ref[k0:k1, :] += _dot(p.astype(BF16), do, 0, 0)
                ds = (p * (_dot(do, vb[k0:k1], 1, 1) - d_col)).astype(BF16)
                part = _dot(ds, kf[k0:k1], 1, 0)
                dq_acc = part if dq_acc is None else dq_acc + part
                dkf_ref[k0:k1, :] += _dot(ds, qblk, 0, 0)
            dqf_ref[q0:q1, :] = dq_acc
        dqin, dgq = _qk_prep_bwd(dqf_ref[...], qin, rq, gq_v, c, sa, sb)
        dkin, dgk = _qk_prep_bwd(dkf_ref[...], kin, rk, gk_v, c, sa, sb)
        dgq_ref[...] += dgq
        dgk_ref[...] += dgk
        dq_ref[...] = dqin.astype(BF16)
        dkv_ref[:, :128] = dkin[:, :128].astype(BF16)
        dkv_ref[:, 128:] = dv_ref[...].astype(BF16)

        @pl.when(hd == 0)
        def _():
            dkr_ref[...] = dkin[:, 128:]

        @pl.when(hd != 0)
        def _():
            dkr_ref[...] += dkin[:, 128:]

    head, shared, gain = _attn_specs(seq)
    per_head = pl.BlockSpec((seq, V_DIM), lambda b, h: (b, h))
    return _call(
        body, "attn_bwd", (t // seq, HEADS),
        [head, head, shared, shared, shared, shared, gain, gain,
         pl.BlockSpec((seq, V_DIM), lambda b, h: (b, 2 + h)), per_head, per_head],
        [head, head, shared, gain, gain],
        [jax.ShapeDtypeStruct((t, HEADS * HP), BF16), jax.ShapeDtypeStruct((t, HEADS * HP), BF16),
         jax.ShapeDtypeStruct((t, 128), F32), jax.ShapeDtypeStruct((1, HP), F32),
         jax.ShapeDtypeStruct((1, HP), F32)],
        (q, kv, kr, cos, s1, s2, gq, gk, dmix, d_out, lse),
        [pltpu.VMEM((seq, HP), F32), pltpu.VMEM((seq, HP), F32), pltpu.VMEM((seq, V_DIM), F32)], comm)


def _odd_post_fwd(x, c_out, d_out, wg):
    t = x.shape[0]

    def body(x_ref, c_ref, d_ref, w_ref, y_ref):
        y_ref[...] = (x_ref[...] + _dot(c_ref[...], w_ref[0:2].reshape(POOL_W, D), 1, 0)
                      + _dot(d_ref[...], w_ref[2:8].reshape(HEADS * V_DIM, D), 1, 0))

    def row(w):
        return pl.BlockSpec((TB, w), lambda i: (i, 0))

    return pl.pallas_call(
        body, name="odd_post_fwd", out_shape=jax.ShapeDtypeStruct((t, D), F32), grid=(t // TB,),
        in_specs=[row(D), row(POOL_W), row(HEADS * V_DIM), _wspec(N_SQ, OFF_OOUT)], out_specs=row(D),
        compiler_params=_cparams(1),
    )(x, c_out, d_out, wg)


def _odd_post_bwd(dx3, wg, comm=None):
    t = dx3.shape[0]

    def body(d_ref, w_ref, o_ref):
        o_ref[...] = _dot(d_ref[...].astype(BF16), w_ref[...].reshape(D, D), 1, 1).astype(BF16)

    row = pl.BlockSpec((TB, D), lambda i: (i, 0))
    (res,), extra = _call(body, "odd_post_bwd", (t // TB,), [row, _wspec(N_SQ, OFF_OOUT)], [row],
                          [jax.ShapeDtypeStruct((t, D), BF16)], (dx3, wg), (), comm)
    return res, extra


def _tn(a_list, b, tm, name, into=None, comm=None):
    t, n_out = b.shape
    widths = [a.shape[1] for a in a_list]
    tk = min(TK_DW, t)
    m, na, nk = sum(widths), len(a_list), t // tk
    assert na == 1 or tm == m

    def body(*refs):
        a_refs, b_ref, o_ref, acc_ref = refs[:na], refs[na], refs[-2], refs[-1]
        k = pl.program_id(1)

        @pl.when(k == 0)
        def _():
            acc_ref[...] = jnp.zeros_like(acc_ref)

        bb = b_ref[...].astype(BF16)
        m0 = 0
        for a_ref, w in zip(a_refs, widths):
            rows = slice(0, tm) if na == 1 else slice(m0, m0 + w)
            acc_ref[rows, :] += _dot(a_ref[...].astype(BF16), bb, 0, 0)
            m0 += w

        @pl.when(k == nk - 1)
        def _():
            o_ref[...] = acc_ref[...].astype(BF16).reshape(o_ref.shape)

    if na == 1:
        in_specs = [pl.BlockSpec((tk, tm), lambda i, k: (k, i))]
    else:
        in_specs = [pl.BlockSpec((tk, w), lambda i, k: (k, 0)) for w in widths]
    in_specs.append(pl.BlockSpec((tk, n_out), lambda i, k: (k, 0)))
    args = list(a_list) + [b]
    if into is None:
        out_spec = pl.BlockSpec((tm, n_out), lambda i, k: (i, 0))
        out_shape = jax.ShapeDtypeStruct((m, n_out), BF16)
        aliases = {}
    else:
        buf, n, off = into
        assert n_out == D and tm % n == 0 and off % n == 0 and (na == 1 or tm // n == N_DEV)
        idx = off // n
        out_spec = pl.BlockSpec((tm // n, n, D), lambda i, k: (i, idx, 0))
        out_shape = jax.ShapeDtypeStruct(buf.shape, BF16)
        in_specs.append(pl.BlockSpec(memory_space=pl.ANY))
        args.append(buf)
        aliases = {len(args) - 1: 0}
    (res,), extra = _call(body, name, (m // tm, nk), in_specs, [out_spec], [out_shape], args,
                          [pltpu.VMEM((tm, n_out), F32)], comm, aliases)
    return (res, extra) if comm is not None else res


def _adamw(ws, gs, ms, vs, name, nblk=1):
    n = len(ws)
    c1 = 1.0 - B1 ** STEP
    c2 = 1.0 - B2 ** STEP

    def body(*refs):
        for a in range(n):
            w, g, m, v = (refs[k * n + a][...] for k in range(4))
            d_ref, m_ref, v_ref = (refs[(4 + k) * n + a] for k in range(3))
            m_new = B1 * m + (1.0 - B1) * g
            v_new = B2 * v + (1.0 - B2) * (g * g)
            d_ref[...] = -LR * ((m_new / c1) / (jnp.sqrt(v_new / c2) + ADAM_EPS) + WD * w)
            m_ref[...] = m_new
            v_ref[...] = v_new

    grid = (nblk,)
    assert all(w.shape[0] % nblk == 0 and (nblk == 1 or (w.shape[0] // nblk) % 8 == 0) for w in ws)
    specs = [pl.BlockSpec((w.shape[0] // nblk, w.shape[1]), lambda i: (i, 0)) for w in ws]
    outs, _ = _call(body, name, grid, specs * 4, specs * 3, [jax.ShapeDtypeStruct(w.shape, F32) for w in ws] * 3,
                    (*ws, *gs, *ms, *vs))
    return outs[:n], outs[n:2 * n], outs[2 * n:]


def _rows1024(a, rows):
    flat = a.reshape(-1, D)
    return jnp.pad(flat, ((0, rows - flat.shape[0]), (0, 0)))


def _pack_shards(even_w_in, even_w_out, odd_w_in, q_b, kv_b, odd_w_out, ffn_w_gate, ffn_w_up, ffn_w_down):
    mix0 = jnp.concatenate([even_w_in[0].T, jnp.zeros((OFF_EOUT - N_EIN, D), F32), even_w_out[0]], axis=0)
    ffn = [jnp.concatenate([ffn_w_gate[layer].T, ffn_w_up[layer].T, ffn_w_down[layer]], axis=0)
           for layer in range(2)]
    mix1 = jnp.concatenate([jnp.pad(odd_w_in[0], ((0, 0), (0, D - ODD_IN))), odd_w_out[0],
                            _rows1024(q_b[0].T, N_QB), _rows1024(kv_b[0].T, N_KVB),
                            jnp.zeros((R_MIX1 - OFF_KVB - N_KVB, D), F32)], axis=0)
    return [c.astype(BF16) for c in (mix0, ffn[0], mix1, ffn[1])]


def _pad_heads(a):
    k = a.shape[1]
    return jnp.pad(a.reshape(HEADS, QK_DIM, k), ((0, 0), (0, HP - QK_DIM), (0, 0))).reshape(HEADS * HP, k)


def _small_pack(parts):
    flat = []
    for p in parts:
        v = p.reshape(-1)
        flat.append(jnp.pad(v, (0, (-v.shape[0]) % 1024)))
    return jnp.concatenate(flat).reshape(-1, 128)


def _small_unpack(buf, shapes):
    flat = buf.reshape(-1)
    out, off = [], 0
    for s in shapes:
        size = int(np.prod(s))
        out.append(flat[off:off + size].reshape(s))
        off += size + (-size) % 1024
    return out


def _step(x3d, positions, target3d, chunks, tile, c_arr, chip_arr, mix_norm, ffn_norm, sg_ln_g, sg_w_s, sg_b_s,
          pool_w, q_norm, k_norm):
    bsz, seq, _ = x3d.shape
    t = bsz * seq
    x0 = x3d.reshape(t, D)
    target = target3d.reshape(t, D)
    my_mix0, my_ffn0, my_mix1, my_ffn1 = chunks

    lane = np.arange(128)
    inv_freq = np.where(lane < QK_ROPE, ROPE_THETA ** (-(2.0 * (lane % 32)) / QK_ROPE), 0.0)
    inv_freq = jnp.asarray(inv_freq.reshape(1, 128), F32)
    (cos, s1, s2), (w_mix0, tiles) = _rope_tables(positions.reshape(t, 1), inv_freq, _gather_comm([my_mix0, tile]))

    conv_w = tiles[:, 0:3, 0:64].transpose(1, 0, 2).reshape(3, SC_W)
    pool_scale = tiles[:, 3, 0:32].reshape(1, POOL_W)
    q_a_norm = tiles[:, 4, 0:48].reshape(1, Q_LORA)
    kv_a_norm = tiles[:, 5, 0:32].reshape(1, KV_LORA)
    ws = sg_w_s[0]
    bst = jnp.pad(sg_b_s[0].T, ((0, 0), (0, 128 - SG_HEADS)))
    cw = jnp.pad(conv_w, ((0, 8 - 3), (0, 0)))
    pw_bd = jax.scipy.linalg.block_diag(*[pool_w[0, g] for g in range(4)]).astype(BF16)
    gq = jnp.pad(q_norm * ATT_SCALE, ((0, 0), (0, HP - QK_DIM)))
    gk = jnp.pad(k_norm, ((0, 0), (0, HP - QK_DIM)))

    (x1, proj_e), (w_ffn0,) = _even_fwd(x0, w_mix0, mix_norm[0:1], sg_ln_g, ws, bst, cw, seq,
                                        _gather_comm([my_ffn0]))
    (x2, g0, u0), (w_mix1,) = _ffn_fwd(x1, w_ffn0, ffn_norm[0:1], "ffn_fwd0", _gather_comm([my_mix1]))
    qbt = _pad_heads(w_mix1[:, OFF_QB:OFF_QB + N_QB_USED, :].reshape(HEADS * QK_DIM, Q_LORA))
    kvbt = w_mix1[:, OFF_KVB:OFF_KVB + N_KVB, :].reshape(HEADS * HP, KV_LORA)
    proj_o, q, kv, kr, c_out = _odd_pre_fwd(x2, w_mix1, mix_norm[1:2], qbt, kvbt, q_a_norm, kv_a_norm, pw_bd,
                                            pool_scale, seq)
    (d_out, lse), (w_ffn1,) = _attn_fwd(q, kv, kr, cos, s1, s2, gq, gk, seq, _gather_comm([my_ffn1]))
    x3 = _odd_post_fwd(x2, c_out, d_out, w_mix1)
    (dy, g1, u1, loss_tile), _ = _ffn_fwd(x3, w_ffn1, ffn_norm[1:2], "ffn_fwd1", None, target)

    def chunk(rows, padded=False):
        return jnp.zeros((N_DEV, rows, D), BF16) if padded else lax.empty((N_DEV, rows, D), BF16)

    (dx3, act1, dg1, du1, h3, dgam_f1), _ = _ffn_bwd(x3, g1, u1, dy, w_ffn1, ffn_norm[1:2], "ffn_bwd1")
    gp_ffn1 = _tn([dg1], h3, 1408, "dw_gate1", (chunk(R_FFN), N_FF, OFF_GATE))
    gp_ffn1 = _tn([du1], h3, 1408, "dw_up1", (gp_ffn1, N_FF, OFF_UP))
    gp_ffn1 = _tn([act1], dy, 1408, "dw_down1", (gp_ffn1, N_FF, OFF_DOWN))

    dmix_o, (ga_ffn1,) = _odd_post_bwd(dx3, w_mix1, _pair_exchange_comm(gp_ffn1))
    pb_ffn1 = _rs_pair_sum(gp_ffn1, ga_ffn1, c_arr, "rs_pair_sum_ffn1")
    gp_mix1 = _tn([c_out, d_out], dx3, D, "dw_oout", (chunk(R_MIX1, True), N_SQ, OFF_OOUT))
    (dq, dkv, dkr, dgq, dgk), (gb_ffn1,) = _attn_bwd(q, kv, kr, cos, s1, s2, gq, gk, dmix_o, d_out, lse, seq,
                                                    _chip_exchange_comm(pb_ffn1))
    gsh_ffn1 = _rs_final_sum(pb_ffn1, gb_ffn1, chip_arr, "rs_final_sum_ffn1")
    (dx2, dproj_o, h2, qn, kvn, dgam_m1, dqa, dkva, dpw_bd, dps) = _odd_pre_bwd(
        x2, proj_o, dx3, dmix_o, dq, dkv, dkr, w_mix1, mix_norm[1:2], qbt, kvbt, q_a_norm, kv_a_norm, pw_bd,
        pool_scale, seq)
    gp_mix1 = _tn([h2], dproj_o, D, "dw_oin", (gp_mix1, N_SQ, OFF_OIN))
    d_qbt = _tn([dq], qn, HEADS * HP, "dw_qb")
    d_qb_rows = d_qbt.reshape(HEADS, HP, Q_LORA)[:, :QK_DIM].reshape(N_DEV, N_QB_USED, D)
    d_kvb_rows = _tn([dkv], kvn, HEADS * HP, "dw_kvb").reshape(N_DEV, N_KVB, D)
    gp_mix1 = lax.dynamic_update_slice(gp_mix1, d_qb_rows, (0, OFF_QB, 0))
    gp_mix1 = lax.dynamic_update_slice(gp_mix1, d_kvb_rows, (0, OFF_KVB, 0))

    (dx1, act0, dg0, du0, h1, dgam_f0), (ga_mix1,) = _ffn_bwd(x1, g0, u0, dx2, w_ffn0, ffn_norm[0:1], "ffn_bwd0",
                                                             _pair_exchange_comm(gp_mix1))
    pb_mix1 = _rs_pair_sum(gp_mix1, ga_mix1, c_arr, "rs_pair_sum_mix1")
    gp_ffn0a, (gb_mix1,) = _tn([dg0], h1, 1408, "dw_gate0", (chunk(2 * N_FF), N_FF, OFF_GATE),
                               _chip_exchange_comm(pb_mix1))
    gsh_mix1 = _rs_final_sum(pb_mix1, gb_mix1, chip_arr, "rs_final_sum_mix1")
    gp_ffn0a = _tn([du0], h1, 1408, "dw_up0", (gp_ffn0a, N_FF, OFF_UP))
    gp_ffn0b, (ga_ffn0a,) = _tn([act0], dx2, 1408, "dw_down0", (chunk(N_FF), N_FF, 0),
                                _pair_exchange_comm(gp_ffn0a))
    pb_ffn0a = _rs_pair_sum(gp_ffn0a, ga_ffn0a, c_arr, "rs_pair_sum_ffn0a")

    (dx0, dproj_e, mix_e, h0, dgam_m0, dws, dbc, dlng, dcw), (gb_ffn0a, ga_ffn0b) = _even_bwd(
        x0, proj_e, dx1, w_mix0, mix_norm[0:1], sg_ln_g, ws, bst, cw, seq,
        _both(_chip_exchange_comm(pb_ffn0a), _pair_exchange_comm(gp_ffn0b)))
    pb_ffn0b = _rs_pair_sum(gp_ffn0b, ga_ffn0b, c_arr, "rs_pair_sum_ffn0b")
    gsh_ffn0a = _rs_final_sum(pb_ffn0a, gb_ffn0a, chip_arr, "rs_final_sum_ffn0a")
    gp_mix0, (gb_ffn0b,) = _tn([mix_e], dx1, D, "dw_eout", (chunk(R_MIX0, True), N_SQ, OFF_EOUT),
                               _chip_exchange_comm(pb_ffn0b))
    gsh_ffn0b = _rs_final_sum(pb_ffn0b, gb_ffn0b, chip_arr, "rs_final_sum_ffn0b")

    small = _small_pack([
        jnp.concatenate([dgam_m0, dgam_m1], 0), jnp.concatenate([dgam_f0, dgam_f1], 0), dlng,
        dws[None], dbc[:, :SG_HEADS].T[None], dcw[:3],
        jnp.stack([dpw_bd[g * POOL_GD:(g + 1) * POOL_GD, g * POOL_GD:(g + 1) * POOL_GD] for g in range(4)])[None],
        dps, dqa, dkva, dgq[:, :QK_DIM] * ATT_SCALE, dgk[:, :QK_DIM], loss_tile[0:1, 0:1]])
    gp_mix0, (small_all,) = _tn([dproj_e], h0, 1280, "dw_ein", (gp_mix0, N_EIN, OFF_EIN), _gather_comm([small]))
    small_sum = _small_unpack(_sum_gathered(small_all), SMALL_SHAPES)
    return dx0.reshape(bsz, seq, D), (gsh_ffn0a, gsh_ffn0b, gsh_mix1, gsh_ffn1), gp_mix0, small_sum


SMALL_SHAPES = [(2, D), (2, D), (1, SG_W), (1, SG_HEADS, 128, 128), (1, SG_HEADS, 128), (3, SC_W),
                (1, 4, POOL_GD, POOL_GD), (1, POOL_W), (1, Q_LORA), (1, KV_LORA), (1, QK_DIM), (1, QK_DIM), (1, 1)]


def kernel(x, positions, mix_norm, ffn_norm, even_w_in, sg_ln_g, sg_w_s, sg_b_s, sc_conv_w, even_w_out, odd_w_in, pool_w, pool_scale, q_a_norm, q_b, kv_a_norm, kv_b, q_norm, k_norm, odd_w_out, ffn_w_gate, ffn_w_up, ffn_w_down, loss_target, m_mix_norm, m_ffn_norm, m_even_w_in, m_sg_ln_g, m_sg_w_s, m_sg_b_s, m_sc_conv_w, m_even_w_out, m_odd_w_in, m_pool_w, m_pool_scale, m_q_a_norm, m_q_b, m_kv_a_norm, m_kv_b, m_q_norm, m_k_norm, m_odd_w_out, m_ffn_w_gate, m_ffn_w_up, m_ffn_w_down, v_mix_norm, v_ffn_norm, v_even_w_in, v_sg_ln_g, v_sg_w_s, v_sg_b_s, v_sc_conv_w, v_even_w_out, v_odd_w_in, v_pool_w, v_pool_scale, v_q_a_norm, v_q_b, v_kv_a_norm, v_kv_b, v_q_norm, v_k_norm, v_odd_w_out, v_ffn_w_gate, v_ffn_w_up, v_ffn_w_down):
    xi, yi, ci = _place()
    me = 4 * xi + 2 * yi + ci

    chunks = _pack_shards(even_w_in, even_w_out, odd_w_in, q_b, kv_b, odd_w_out, ffn_w_gate, ffn_w_up, ffn_w_down)

    def lane_pad(a):
        return jnp.pad(a, ((0, 0), (0, 128 - a.shape[1])))

    tile = jnp.concatenate([lane_pad(sc_conv_w[0]), lane_pad(pool_scale), lane_pad(q_a_norm), lane_pad(kv_a_norm),
                            jnp.zeros((2, 128), F32)], axis=0)
    c_arr = jnp.reshape(ci, (1,)).astype(jnp.int32)
    chip_arr = jnp.reshape(2 * xi + yi, (1,)).astype(jnp.int32)
    grad_x, (gsh_ffn0a, gsh_ffn0b, gsh_mix1, gsh_ffn1), gp_mix0, tot = _step(
        x, positions, loss_target, chunks, tile, c_arr, chip_arr, mix_norm, ffn_norm, sg_ln_g, sg_w_s, sg_b_s,
        pool_w, q_norm, k_norm)

    (g_mix, g_ffn, g_lng, g_ws, g_bs, g_cw_full, g_pw, g_ps_full, g_qa_full, g_kva_full, g_qn, g_kn, loss) = tot
    g_cw = lax.dynamic_slice_in_dim(g_cw_full, me * 64, 64, axis=1)[None]
    g_ps = lax.dynamic_slice_in_dim(g_ps_full, me * 32, 32, axis=1)
    g_qa = lax.dynamic_slice_in_dim(g_qa_full, me * 48, 48, axis=1)
    g_kva = lax.dynamic_slice_in_dim(g_kva_full, me * 32, 32, axis=1)

    def tr(a):
        return jnp.swapaxes(a, -1, -2)

    g_gate = tr(jnp.stack([gsh_ffn0a[OFF_GATE:OFF_GATE + N_FF], gsh_ffn1[OFF_GATE:OFF_GATE + N_FF]]))
    g_up = tr(jnp.stack([gsh_ffn0a[OFF_UP:OFF_UP + N_FF], gsh_ffn1[OFF_UP:OFF_UP + N_FF]]))
    g_down = jnp.stack([gsh_ffn0b, gsh_ffn1[OFF_DOWN:OFF_DOWN + N_FF]])
    g_oin = gsh_mix1[OFF_OIN:OFF_OIN + N_SQ, :ODD_IN][None]
    g_oout = gsh_mix1[OFF_OOUT:OFF_OOUT + N_SQ][None]
    g_qb = tr(gsh_mix1[OFF_QB:OFF_QB + N_QB_USED].reshape(1, 144, Q_LORA))
    g_kvb = tr(gsh_mix1[OFF_KVB:OFF_KVB + N_KVB].reshape(1, 192, KV_LORA))
    transposed = ("even_w_in", "odd_w_in", "q_b", "kv_b", "ffn_w_gate", "ffn_w_up")

    names = ("mix_norm", "ffn_norm", "even_w_in", "sg_ln_g", "sg_w_s", "sg_b_s", "sc_conv_w", "even_w_out",
             "odd_w_in", "pool_w", "pool_scale", "q_a_norm", "q_b", "kv_a_norm", "kv_b", "q_norm", "k_norm",
             "odd_w_out", "ffn_w_gate", "ffn_w_up", "ffn_w_down")
    grads = dict(mix_norm=g_mix, ffn_norm=g_ffn, sg_ln_g=g_lng, sg_w_s=g_ws, sg_b_s=g_bs,
                 sc_conv_w=g_cw, odd_w_in=g_oin, pool_w=g_pw, pool_scale=g_ps, q_a_norm=g_qa,
                 q_b=g_qb, kv_a_norm=g_kva, kv_b=g_kvb, q_norm=g_qn, k_norm=g_kn, odd_w_out=g_oout,
                 ffn_w_gate=g_gate, ffn_w_up=g_up, ffn_w_down=g_down)
    weights = dict(mix_norm=mix_norm, ffn_norm=ffn_norm, even_w_in=even_w_in, sg_ln_g=sg_ln_g, sg_w_s=sg_w_s,
                   sg_b_s=sg_b_s, sc_conv_w=sc_conv_w, even_w_out=even_w_out, odd_w_in=odd_w_in, pool_w=pool_w,
                   pool_scale=pool_scale, q_a_norm=q_a_norm, q_b=q_b, kv_a_norm=kv_a_norm, kv_b=kv_b, q_norm=q_norm,
                   k_norm=k_norm, odd_w_out=odd_w_out, ffn_w_gate=ffn_w_gate, ffn_w_up=ffn_w_up,
                   ffn_w_down=ffn_w_down)
    m_in = dict(mix_norm=m_mix_norm, ffn_norm=m_ffn_norm, even_w_in=m_even_w_in, sg_ln_g=m_sg_ln_g, sg_w_s=m_sg_w_s,
                sg_b_s=m_sg_b_s, sc_conv_w=m_sc_conv_w, even_w_out=m_even_w_out, odd_w_in=m_odd_w_in,
                pool_w=m_pool_w, pool_scale=m_pool_scale, q_a_norm=m_q_a_norm, q_b=m_q_b, kv_a_norm=m_kv_a_norm,
                kv_b=m_kv_b, q_norm=m_q_norm, k_norm=m_k_norm, odd_w_out=m_odd_w_out, ffn_w_gate=m_ffn_w_gate,
                ffn_w_up=m_ffn_w_up, ffn_w_down=m_ffn_w_down)
    v_in = dict(mix_norm=v_mix_norm, ffn_norm=v_ffn_norm, even_w_in=v_even_w_in, sg_ln_g=v_sg_ln_g, sg_w_s=v_sg_w_s,
                sg_b_s=v_sg_b_s, sc_conv_w=v_sc_conv_w, even_w_out=v_even_w_out, odd_w_in=v_odd_w_in,
                pool_w=v_pool_w, pool_scale=v_pool_scale, q_a_norm=v_q_a_norm, q_b=v_q_b, kv_a_norm=v_kv_a_norm,
                kv_b=v_kv_b, q_norm=v_q_norm, k_norm=v_k_norm, odd_w_out=v_odd_w_out, ffn_w_gate=v_ffn_w_gate,
                ffn_w_up=v_ffn_w_up, ffn_w_down=v_ffn_w_down)
    delta, new_m, new_v = {}, {}, {}

    def as2d(k, a):
        a = tr(a) if k in transposed else a
        return a.reshape(-1, a.shape[-1])

    def back(k, a):
        shape = weights[k].shape
        return tr(a.reshape(shape[:-2] + (shape[-1], shape[-2]))) if k in transposed else a.reshape(shape)

    def update(group, name, nblk=1):
        outs = _adamw([as2d(k, weights[k]) for k in group], [as2d(k, grads[k]) for k in group],
                      [as2d(k, m_in[k]) for k in group], [as2d(k, v_in[k]) for k in group], name, nblk)
        for i, k in enumerate(group):
            delta[k], new_m[k], new_v[k] = (back(k, o[i]) for o in outs)

    (ga_mix0,) = _comm_alone(_pair_exchange_comm(gp_mix0), "rs_pair_exchange_mix0")
    pb_mix0 = _rs_pair_sum(gp_mix0, ga_mix0, c_arr, "rs_pair_sum_mix0")
    (gb_mix0,) = _comm_alone(_chip_exchange_comm(pb_mix0), "rs_chip_exchange_mix0")
    gsh_mix0 = _rs_final_sum(pb_mix0, gb_mix0, chip_arr, "rs_final_sum_mix0")
    grads["even_w_in"] = tr(gsh_mix0[OFF_EIN:OFF_EIN + N_EIN][None])
    grads["even_w_out"] = gsh_mix0[OFF_EOUT:OFF_EOUT + N_SQ][None]

    update(["ffn_w_gate", "ffn_w_up", "ffn_w_down"], "adamw_ffn", 4)
    update(["even_w_in", "even_w_out", "odd_w_in", "odd_w_out"], "adamw_mix", 2)
    update([k for k in names if k not in delta], "adamw_small")

    return (loss.reshape(()), grad_x, *[grads[k] for k in names], *[delta[k] for k in names],
            *[new_m[k] for k in names], *[new_v[k] for k in names])
```

```python
import functools

import numpy as np
import jax
import jax.numpy as jnp
from jax import lax
from jax.experimental import pallas as pl
from jax.experimental.pallas import tpu as pltpu

F32 = jnp.float32
BF16 = jnp.bfloat16
MESH = pl.DeviceIdType.MESH

D = 1024
EPS = 1e-6
NEG_INF = -1e30
SG_HEADS, SG_HD, SG_W, SG_CHUNK = 4, 128, 512, 128
SC_W = 512
EVEN_IN = 2560
POOL_W = 256
POOL_GD = 64
Q_LORA, KV_LORA, QK_ROPE, QK_NOPE, V_DIM = 384, 256, 64, 128, 128
QK_DIM = QK_NOPE + QK_ROPE
HEADS = 6
HP = 256
ODD_IN = 960
D_FF = 2816
ROPE_THETA = 10000.0
ATT_SCALE = QK_DIM ** -0.5
LR, B1, B2, ADAM_EPS, WD, STEP = 0.001, 0.9, 0.999, 1e-08, 0.01, 10

N_DEV = 8
TB = 512
TB_FFN_BWD = 256
TK_DW = 1024
HALO = 16
VMEM_LIMIT = 56 * 1024 * 1024

N_EIN, N_FF, N_SQ = 320, 352, 128
OFF_EIN, OFF_EOUT, R_MIX0 = 0, 384, 512
OFF_GATE, OFF_UP, R_GU = 0, 352, 704
OFF_OIN, OFF_OOUT, OFF_QB, OFF_KVB, R_MIX1 = 0, 128, 256, 320, 384
N_QB, N_QB_USED, N_KVB = 64, 54, 48

INV_SQRT2 = 0.7071067811865476
INV_SQRT_2PI = 0.3989422804014327


def _dot(a, b, ca, cb):
    return lax.dot_general(a, b, (((ca,), (cb,)), ((), ())), preferred_element_type=F32)


def _cparams(n_axes=1):
    return pltpu.CompilerParams(dimension_semantics=("arbitrary",) * n_axes, vmem_limit_bytes=VMEM_LIMIT)


def _wspec(n, off, arity=1):
    assert off % n == 0
    idx = off // n
    if arity == 1:
        return pl.BlockSpec((N_DEV, n, D), lambda i: (0, idx, 0), pipeline_mode=pl.Buffered(1))
    return pl.BlockSpec((N_DEV, n, D), lambda i, j: (0, idx, 0), pipeline_mode=pl.Buffered(1))


def _const_spec(shape):
    zeros = (0,) * len(shape)
    return pl.BlockSpec(shape, lambda *_: zeros)


class _Comm:
    def __init__(self, ins, out_shapes, sems, start, wait, mid=None):
        self.ins, self.out_shapes, self.sems, self.start, self.wait, self.mid = ins, out_shapes, sems, start, wait, mid


def _both(c1, c2):
    def split(f1, f2):
        def run(ins, outs, sems):
            f1(ins[:len(c1.ins)], outs[:len(c1.out_shapes)], sems[:len(c1.sems)])
            f2(ins[len(c1.ins):], outs[len(c1.out_shapes):], sems[len(c1.sems):])
        return run

    assert c1.mid is None and c2.mid is None
    return _Comm(c1.ins + c2.ins, c1.out_shapes + c2.out_shapes, c1.sems + c2.sems,
                 split(c1.start, c2.start), split(c1.wait, c2.wait))


def _call(body, name, grid, in_specs, out_specs, out_shape, args, scratch_shapes=(), comm=None, aliases=None):
    n_axes = len(grid)
    aliases = aliases or {}
    if comm is None:
        res = pl.pallas_call(
            body, name=name, grid=grid, in_specs=list(in_specs), out_specs=list(out_specs),
            out_shape=list(out_shape), scratch_shapes=list(scratch_shapes), input_output_aliases=aliases,
            compiler_params=_cparams(n_axes))(*args)
        return list(res), []
    ni, no, ns = len(in_specs), len(out_specs), len(scratch_shapes)
    ci, co = len(comm.ins), len(comm.out_shapes)
    n_steps = int(np.prod(grid))

    def carrier(*refs):
        ins, cin = refs[:ni], refs[ni:ni + ci]
        outs, cout = refs[ni + ci:ni + ci + no], refs[ni + ci + no:ni + ci + no + co]
        scr, sems = refs[ni + ci + no + co:ni + ci + no + co + ns], refs[ni + ci + no + co + ns:]
        step = 0
        for a in range(n_axes):
            step = step * grid[a] + pl.program_id(a)

        @pl.when(step == 0)
        def _():
            comm.start(cin, cout, sems)

        body(*ins, *outs, *scr)

        if comm.mid is not None and n_steps >= 4:
            @pl.when(step == (3 * n_steps) // 4)
            def _():
                comm.mid(cin, cout, sems)

        @pl.when(step == n_steps - 1)
        def _():
            if comm.mid is not None and n_steps < 4:
                comm.mid(cin, cout, sems)
            comm.wait(cin, cout, sems)

    any_spec = pl.BlockSpec(memory_space=pl.ANY)
    res = pl.pallas_call(
        carrier, name=name, grid=grid, in_specs=list(in_specs) + [any_spec] * ci,
        out_specs=list(out_specs) + [any_spec] * co, out_shape=list(out_shape) + list(comm.out_shapes),
        scratch_shapes=list(scratch_shapes) + list(comm.sems), input_output_aliases=aliases,
        compiler_params=_cparams(n_axes))(*args, *comm.ins)
    return list(res[:no]), list(res[no:])


def _comm_alone(comm, name):
    ci, co = len(comm.ins), len(comm.out_shapes)

    def body(*refs):
        cin, cout, sems = refs[:ci], refs[ci:ci + co], refs[ci + co:]
        comm.start(cin, cout, sems)
        if comm.mid is not None:
            comm.mid(cin, cout, sems)
        comm.wait(cin, cout, sems)

    any_spec = pl.BlockSpec(memory_space=pl.ANY)
    res = pl.pallas_call(
        body, name=name, out_shape=list(comm.out_shapes), in_specs=[any_spec] * ci, out_specs=[any_spec] * co,
        scratch_shapes=list(comm.sems))(*comm.ins)
    return list(res)


def _rms(x, g):
    r = lax.rsqrt(jnp.mean(x * x, axis=-1, keepdims=True) + EPS)
    return x * r * g, r


def _rms_bwd(x, r, g, dy):
    xh = x * r
    dxh = dy * g
    dx = r * (dxh - xh * jnp.mean(dxh * xh, axis=-1, keepdims=True))
    dg = jnp.sum(dy * xh, axis=0, keepdims=True)
    return dx, dg


def _gelu(x):
    return 0.5 * x * (1.0 + lax.erf(x * INV_SQRT2))


def _gelu_grad(x):
    return 0.5 * (1.0 + lax.erf(x * INV_SQRT2)) + x * jnp.exp(-0.5 * x * x) * INV_SQRT_2PI


def _shift_down(a, k):
    rows = lax.broadcasted_iota(jnp.int32, a.shape, 0)
    return jnp.where(rows >= k, pltpu.roll(a, k, 0), 0.0)


def _shift_up(a, k):
    n = a.shape[0]
    rows = lax.broadcasted_iota(jnp.int32, a.shape, 0)
    return jnp.where(rows < n - k, pltpu.roll(a, n - k, 0), 0.0)


def _tril_bf16(w):
    r = lax.broadcasted_iota(jnp.int32, w.shape, 0)
    c = lax.broadcasted_iota(jnp.int32, w.shape, 1)
    return jnp.where(r >= c, w, 0.0).astype(BF16)


def _ln_head(vh, g):
    mu = jnp.mean(vh, axis=-1, keepdims=True)
    xc = vh - mu
    rr = lax.rsqrt(jnp.mean(xc * xc, axis=-1, keepdims=True) + EPS)
    xh = xc * rr
    return xh * g, xh, rr


def _conv_fwd(z, tail, cw_ref):
    ext = jnp.concatenate([tail, z], axis=0)
    zs1 = _shift_down(ext, 1)[HALO:]
    zs2 = _shift_down(ext, 2)[HALO:]
    y = cw_ref[2:3, :] * z + cw_ref[1:2, :] * zs1 + cw_ref[0:1, :] * zs2
    return y, zs1, zs2


def _pool_cnt(shape, blk_in_seq):
    rows = lax.broadcasted_iota(jnp.int32, shape, 0)
    grp = lax.broadcasted_iota(jnp.int32, shape, 1) // POOL_GD
    win = jnp.where(grp == 0, 2, jnp.where(grp == 1, 4, jnp.where(grp == 2, 8, 16)))
    tpos = blk_in_seq * shape[0] + rows + 1
    return jnp.minimum(tpos, win).astype(F32), grp


def _pool_select(grp, s2, s4, s8, s16):
    return jnp.where(grp == 0, s2, jnp.where(grp == 1, s4, jnp.where(grp == 2, s8, s16)))


def _pool_fwd(z, tail, blk_in_seq):
    ext = jnp.concatenate([tail, z], axis=0)
    s2 = ext + _shift_down(ext, 1)
    s4 = s2 + _shift_down(s2, 2)
    s8 = s4 + _shift_down(s4, 4)
    s16 = s8 + _shift_down(s8, 8)
    cnt, grp = _pool_cnt(z.shape, blk_in_seq)
    sums = _pool_select(grp, s2[HALO:], s4[HALO:], s8[HALO:], s16[HALO:])
    return sums / cnt - z, cnt, grp


def _pool_bwd(dpooled, dpm, head, grp):
    n = dpm.shape[0]
    ext = jnp.concatenate([dpm, head], axis=0)
    u2 = ext + _shift_up(ext, 1)
    u4 = u2 + _shift_up(u2, 2)
    u8 = u4 + _shift_up(u4, 4)
    u16 = u8 + _shift_up(u8, 8)
    return _pool_select(grp, u2[:n], u4[:n], u8[:n], u16[:n]) - dpooled


def _split_bf16(a):
    hi = a.astype(BF16)
    return hi, (a - hi.astype(F32)).astype(BF16)


def _lane_sums(a):
    hi, lo = _split_bf16(a)
    ones = jnp.ones((a.shape[1], a.shape[1]), BF16)
    return _dot(hi, ones, 1, 0) + _dot(lo, ones, 1, 0)


def _swap_halves(y1):
    src = lax.broadcasted_iota(jnp.int32, (128, 128), 0)
    dst = lax.broadcasted_iota(jnp.int32, (128, 128), 1)
    perm = jnp.where(((dst < 32) & (src == dst + 32)) | ((dst >= 32) & (dst < QK_ROPE) & (src == dst - 32)), 1.0, 0.0)
    perm = perm.astype(BF16)
    hi, lo = _split_bf16(y1)
    return _dot(hi, perm, 1, 0) + _dot(lo, perm, 1, 0)


def _rope(y1, c, s):
    return y1 * c + _swap_halves(y1) * s


def _rope_bwd(d1, c, s):
    return d1 * c + _swap_halves(d1 * s)


def _qk_prep(x, g, c, s):
    r = lax.rsqrt(_lane_sums(x * x) * (1.0 / QK_DIM) + EPS)
    y = x * r * g
    return jnp.concatenate([y[:, :128], _rope(y[:, 128:], c, s)], axis=1), r


def _qk_prep_bwd(dout, x, r, g, c, s):
    dy = jnp.concatenate([dout[:, :128], _rope_bwd(dout[:, 128:], c, s)], axis=1)
    xh = x * r
    dxh = dy * g
    dx = r * (dxh - xh * (_lane_sums(dxh * xh) * (1.0 / QK_DIM)))
    return dx, jnp.sum(dy * xh, axis=0, keepdims=True)


def _place():
    return lax.axis_index("x"), lax.axis_index("y"), lax.axis_index("c")


def _gather_comm(arrs):
    n = len(arrs)

    def plan(ins, outs, sems):
        send_sems, recv_sems, local_sems = sems
        x, y, c = _place()
        me, sibling = (x, y, c), (x, y, 1 - c)
        chips = [(1 - x, y), (x, 1 - y), (1 - x, 1 - y)]

        def slot(a, px, py, pc):
            return outs[a].at[4 * px + 2 * py + pc]

        def copy(a, k, block, to, src=None):
            return pltpu.make_async_remote_copy(
                src_ref=slot(a, *block) if src is None else src, dst_ref=slot(a, *block),
                send_sem=send_sems.at[a, k], recv_sem=recv_sems.at[a, k], device_id=to, device_id_type=MESH)

        def own():
            mine = [pltpu.make_async_copy(ins[a], slot(a, *me), local_sems.at[a]) for a in range(n)]
            first = []
            for a in range(n):
                first.append(copy(a, 0, me, sibling, src=ins[a]))
                first += [copy(a, 1 + j, me, (*chip, c), src=ins[a]) for j, chip in enumerate(chips)]
            return mine, first

        return c, me, sibling, chips, copy, own

    def start(ins, outs, sems):
        mine, first = plan(ins, outs, sems)[-1]()
        for cp in mine + first:
            cp.start()

    def mid(ins, outs, sems):
        c, me, sibling, chips, copy, _ = plan(ins, outs, sems)
        for j, chip in enumerate(chips):
            for a in range(n):
                copy(a, 1 + j, (*chip, c), me).wait_recv()
                copy(a, 4 + j, (*chip, c), sibling).start()

    def wait(ins, outs, sems):
        c, me, sibling, chips, copy, own = plan(ins, outs, sems)
        mine, first = own()
        passed = [copy(a, 4 + j, (*chip, c), sibling) for j, chip in enumerate(chips) for a in range(n)]
        for a in range(n):
            copy(a, 0, sibling, me).wait_recv()
            for j, chip in enumerate(chips):
                copy(a, 4 + j, (*chip, 1 - c), me).wait_recv()
        for cp in first + passed:
            cp.wait_send()
        for cp in mine:
            cp.wait()

    return _Comm(
        list(arrs), [jax.ShapeDtypeStruct((N_DEV,) + a.shape, a.dtype) for a in arrs],
        [pltpu.SemaphoreType.DMA((n, 7)), pltpu.SemaphoreType.DMA((n, 7)), pltpu.SemaphoreType.DMA((n,))],
        start, wait, mid)


def _sum_gathered(g):
    rows = g.shape[1]

    def body(g_ref, sum_ref):
        total = g_ref[0]
        for d in range(1, N_DEV):
            total = total + g_ref[d]
        sum_ref[...] = total

    return pl.pallas_call(
        body, name="sum_gathered_small", out_shape=jax.ShapeDtypeStruct((rows, 128), F32), grid=(1,),
        in_specs=[pl.BlockSpec((N_DEV, rows, 128), lambda i: (0, 0, 0))],
        out_specs=pl.BlockSpec((rows, 128), lambda i: (0, 0)), compiler_params=_cparams(1),
    )(g)


def _sum_rows(rows):
    return rows if rows <= 512 else rows // 2


def _pair_exchange_comm(gp):
    _, rows, cols = gp.shape

    def copies(ins, outs, sems):
        send_sems, recv_sems = sems
        x, y, c = _place()
        return [pltpu.make_async_remote_copy(
            src_ref=ins[0].at[2 * j + (1 - c)], dst_ref=outs[0].at[j], send_sem=send_sems.at[j],
            recv_sem=recv_sems.at[j], device_id=(x, y, 1 - c), device_id_type=MESH) for j in range(4)]

    def start(ins, outs, sems):
        for cp in copies(ins, outs, sems):
            cp.start()

    def wait(ins, outs, sems):
        for cp in copies(ins, outs, sems):
            cp.wait()

    return _Comm([gp], [jax.ShapeDtypeStruct((4, rows, cols), gp.dtype)],
                 [pltpu.SemaphoreType.DMA((4,)), pltpu.SemaphoreType.DMA((4,))], start, wait)


def _rs_pair_sum(gp, got, c_arr, name):
    _, rows, cols = got.shape
    rb = _sum_rows(rows)
    gp4 = gp.reshape(4, 2, rows, cols)

    def body(c_ref, a_ref, b_ref, o_ref):
        o_ref[0] = (a_ref[0, 0].astype(F32) + b_ref[0].astype(F32)).astype(o_ref.dtype)

    return pl.pallas_call(
        body, name=name, out_shape=jax.ShapeDtypeStruct((4, rows, cols), gp.dtype),
        grid_spec=pltpu.PrefetchScalarGridSpec(
            num_scalar_prefetch=1, grid=(4, rows // rb),
            in_specs=[pl.BlockSpec((1, 1, rb, cols), lambda j, r, cr: (j, cr[0], r, 0)),
                      pl.BlockSpec((1, rb, cols), lambda j, r, cr: (j, r, 0))],
            out_specs=pl.BlockSpec((1, rb, cols), lambda j, r, cr: (j, r, 0))),
        compiler_params=_cparams(2),
    )(c_arr, gp4, got)


def _chip_exchange_comm(pb):
    _, rows, cols = pb.shape

    def copies(ins, outs, sems):
        send_sems, recv_sems = sems
        x, y, c = _place()
        chips = [(1 - x, y), (x, 1 - y), (1 - x, 1 - y)]
        return [pltpu.make_async_remote_copy(
            src_ref=ins[0].at[2 * px + py], dst_ref=outs[0].at[k], send_sem=send_sems.at[k],
            recv_sem=recv_sems.at[k], device_id=(px, py, c), device_id_type=MESH)
            for k, (px, py) in enumerate(chips)]

    def start(ins, outs, sems):
        for cp in copies(ins, outs, sems):
            cp.start()

    def wait(ins, outs, sems):
        for cp in copies(ins, outs, sems):
            cp.wait()

    return _Comm([pb], [jax.ShapeDtypeStruct((3, rows, cols), pb.dtype)],
                 [pltpu.SemaphoreType.DMA((3,)), pltpu.SemaphoreType.DMA((3,))], start, wait)


def _rs_final_sum(pb, got, chip_arr, name):
    _, rows, cols = got.shape
    rb = _sum_rows(rows)

    def body(j_ref, a_ref, b_ref, o_ref):
        o_ref[...] = ((a_ref[0].astype(F32) + b_ref[0].astype(F32)) + b_ref[1].astype(F32)) + b_ref[2].astype(F32)

    return pl.pallas_call(
        body, name=name, out_shape=jax.ShapeDtypeStruct((rows, cols), F32),
        grid_spec=pltpu.PrefetchScalarGridSpec(
            num_scalar_prefetch=1, grid=(rows // rb,),
            in_specs=[pl.BlockSpec((1, rb, cols), lambda r, jr: (jr[0], r, 0)),
                      pl.BlockSpec((3, rb, cols), lambda r, jr: (0, r, 0))],
            out_specs=pl.BlockSpec((rb, cols), lambda r, jr: (r, 0))),
        compiler_params=_cparams(1),
    )(chip_arr, pb, got)


def _rope_tables(pos_col, inv_freq, comm=None):
    t = pos_col.shape[0]

    def body(p_ref, f_ref, c_ref, s_ref):
        ang = p_ref[...].astype(F32) * f_ref[...]
        lane = lax.broadcasted_iota(jnp.int32, ang.shape, 1)
        c_ref[...] = jnp.where(lane < QK_ROPE, jnp.cos(ang), 0.0)
        s = jnp.sin(ang)
        s_ref[...] = jnp.where(lane < 32, -s, jnp.where(lane < QK_ROPE, s, 0.0))

    spec = pl.BlockSpec((TB, 128), lambda i: (i, 0))
    return _call(
        body, "rope_tables", (t // TB,), [pl.BlockSpec((TB, 1), lambda i: (i, 0)), _const_spec((1, 128))],
        [spec] * 2, [jax.ShapeDtypeStruct((t, 128), F32)] * 2, (pos_col, inv_freq), (), comm)


def _sgu_conv_fwd(proj, tail, lng_ref, ws_ref, bst_ref, cw_ref):
    gu = _gelu(proj[:, 0:SG_W])
    gv = _gelu(proj[:, SG_W:2 * SG_W])
    bg = proj[:, 1024:1536]
    z = proj[:, 1536:2048] * proj[:, 2048:2560]
    heads = []
    for h in range(SG_HEADS):
        sl = slice(h * SG_HD, (h + 1) * SG_HD)
        vn, _, _ = _ln_head(gv[:, sl], lng_ref[:, sl])
        vnb = vn.astype(BF16)
        wm = _tril_bf16(ws_ref[h])
        bcol = bst_ref[:, h:h + 1]
        mixed = jnp.concatenate(
            [_dot(wm, vnb[k * SG_CHUNK:(k + 1) * SG_CHUNK], 1, 0) + bcol for k in range(TB // SG_CHUNK)], axis=0)
        heads.append(gu[:, sl] * mixed)
    a_out = jnp.concatenate(heads, axis=1)
    y, _, _ = _conv_fwd(z, tail, cw_ref)
    return a_out, bg * y, z


def _even_fwd(x, wg, gamma, lng, ws, bst, cw, seq, comm=None):
    t = x.shape[0]
    nbs = seq // TB

    def body(x_ref, gam_ref, win_ref, wout_ref, lng_ref, ws_ref, bst_ref, cw_ref, x1_ref, proj_ref, tail_ref):
        i = pl.program_id(0)
        xv = x_ref[...]
        h, _ = _rms(xv, gam_ref[...])
        proj = _dot(h.astype(BF16), win_ref[...].reshape(EVEN_IN, D), 1, 1)
        proj_ref[...] = proj.astype(BF16)
        tail = jnp.where(i % nbs == 0, 0.0, tail_ref[...])
        a_out, b_out, z = _sgu_conv_fwd(proj, tail, lng_ref, ws_ref, bst_ref, cw_ref)
        tail_ref[...] = z[TB - HALO:, :]
        x1_ref[...] = (xv + _dot(a_out.astype(BF16), wout_ref[0:4].reshape(512, D), 1, 0)
                       + _dot(b_out.astype(BF16), wout_ref[4:8].reshape(512, D), 1, 0))

    row = pl.BlockSpec((TB, D), lambda i: (i, 0))
    return _call(
        body, "even_fwd", (t // TB,),
        [row, _const_spec((1, D)), _wspec(N_EIN, OFF_EIN), _wspec(N_SQ, OFF_EOUT), _const_spec((1, SG_W)),
         _const_spec((SG_HEADS, 128, 128)), _const_spec((128, 128)), _const_spec((8, SC_W))],
        [row, pl.BlockSpec((TB, EVEN_IN), lambda i: (i, 0))],
        [jax.ShapeDtypeStruct((t, D), F32), jax.ShapeDtypeStruct((t, EVEN_IN), BF16)],
        (x, gamma, wg, wg, lng, ws, bst, cw), [pltpu.VMEM((HALO, SC_W), F32)], comm)


def _even_bwd(x, proj, dx1, wg, gamma, lng, ws, bst, cw, seq, comm=None):
    t = x.shape[0]
    nb, nbs = t // TB, seq // TB

    def body(x_ref, proj_ref, ptail_ref, dx1_ref, gam_ref, win_ref, wout_ref, lng_ref, ws_ref, bst_ref, cw_ref,
             dx0_ref, dproj_ref, mix_ref, h_ref, dgam_ref, dws_ref, dbc_ref, dlng_ref, dcw_ref, head_ref):
        i = pl.program_id(0)
        blk = nb - 1 - i

        @pl.when(i == 0)
        def _():
            dgam_ref[...] = jnp.zeros_like(dgam_ref)
            dws_ref[...] = jnp.zeros_like(dws_ref)
            dbc_ref[...] = jnp.zeros_like(dbc_ref)
            dlng_ref[...] = jnp.zeros_like(dlng_ref)
            dcw_ref[...] = jnp.zeros_like(dcw_ref)

        xv = x_ref[...]
        gam = gam_ref[...]
        h, r = _rms(xv, gam)
        h_ref[...] = h.astype(BF16)
        dx1 = dx1_ref[...]
        dmix = _dot(dx1.astype(BF16), wout_ref[...].reshape(D, D), 1, 1)
        da, db = dmix[:, :SG_W], dmix[:, SG_W:]
        proj = proj_ref[...].astype(F32)
        u, v = proj[:, 0:SG_W], proj[:, SG_W:2 * SG_W]
        bg, cg, hv = proj[:, 1024:1536], proj[:, 1536:2048], proj[:, 2048:2560]
        gu, gv = _gelu(u), _gelu(v)

        a_heads, dgv_heads = [], []
        for hd in range(SG_HEADS):
            sl = slice(hd * SG_HD, (hd + 1) * SG_HD)
            g_h = lng_ref[:, sl]
            vn, xh, rr = _ln_head(gv[:, sl], g_h)
            vnb = vn.astype(BF16)
            wm = _tril_bf16(ws_ref[hd])
            bcol = bst_ref[:, hd:hd + 1]
            mixed_c, dvn_c = [], []
            dw_acc = jnp.zeros((128, 128), F32)
            db_acc = jnp.zeros((128, 1), F32)
            for k in range(TB // SG_CHUNK):
                rs = slice(k * SG_CHUNK, (k + 1) * SG_CHUNK)
                mixed = _dot(wm, vnb[rs], 1, 0) + bcol
                dmixed = da[rs, sl] * gu[rs, sl]
                dmb = dmixed.astype(BF16)
                dvn_c.append(_dot(wm, dmb, 0, 0))
                dw_acc = dw_acc + _dot(dmb, vnb[rs], 1, 1)
                db_acc = db_acc + jnp.sum(dmixed, axis=1, keepdims=True)
                mixed_c.append(mixed)
            mixed_h = jnp.concatenate(mixed_c, axis=0)
            dvn = jnp.concatenate(dvn_c, axis=0)
            r_i = lax.broadcasted_iota(jnp.int32, (128, 128), 0)
            c_i = lax.broadcasted_iota(jnp.int32, (128, 128), 1)
            dws_ref[hd] += jnp.where(r_i >= c_i, dw_acc, 0.0)
            dbc_ref[:, hd:hd + 1] += db_acc
            dlng_ref[:, sl] += jnp.sum(dvn * xh, axis=0, keepdims=True)
            dxh = dvn * g_h
            dgv = rr * (dxh - jnp.mean(dxh, axis=-1, keepdims=True)
                        - xh * jnp.mean(dxh * xh, axis=-1, keepdims=True))
            a_heads.append(gu[:, sl] * mixed_h)
            dproj_ref[:, sl] = (da[:, sl] * mixed_h * _gelu_grad(u[:, sl])).astype(BF16)
            dgv_heads.append(dgv * _gelu_grad(v[:, sl]))
        dproj_ref[:, SG_W:2 * SG_W] = jnp.concatenate(dgv_heads, axis=1).astype(BF16)
        mix_ref[:, :SG_W] = jnp.concatenate(a_heads, axis=1).astype(BF16)

        z = cg * hv
        pt = ptail_ref[...].astype(F32)
        tail = jnp.where(blk % nbs == 0, 0.0, pt[:, 1536:2048] * pt[:, 2048:2560])
        y, zs1, zs2 = _conv_fwd(z, tail, cw_ref)
        mix_ref[:, SG_W:] = (bg * y).astype(BF16)
        dy = db * bg
        head = jnp.where(blk % nbs == nbs - 1, 0.0, head_ref[...])
        ext = jnp.concatenate([dy, head], axis=0)
        dz = (cw_ref[2:3, :] * dy + cw_ref[1:2, :] * _shift_up(ext, 1)[:TB]
              + cw_ref[0:1, :] * _shift_up(ext, 2)[:TB])
        head_ref[...] = dy[:HALO, :]
        dcw_ref[2:3, :] += jnp.sum(dy * z, axis=0, keepdims=True)
        dcw_ref[1:2, :] += jnp.sum(dy * zs1, axis=0, keepdims=True)
        dcw_ref[0:1, :] += jnp.sum(dy * zs2, axis=0, keepdims=True)
        dproj_ref[:, 1024:1536] = (db * y).astype(BF16)
        dproj_ref[:, 1536:2048] = (dz * hv).astype(BF16)
        dproj_ref[:, 2048:2560] = (dz * cg).astype(BF16)

        dh = _dot(dproj_ref[...], win_ref[...].reshape(EVEN_IN, D), 1, 0)
        dxn, dgam = _rms_bwd(xv, r, gam, dh)
        dgam_ref[...] += dgam
        dx0_ref[...] = dx1 + dxn

    def rev(w):
        return pl.BlockSpec((TB, w), lambda i: (nb - 1 - i, 0))

    ptail = pl.BlockSpec((HALO, EVEN_IN), lambda i: (jnp.maximum((nb - 1 - i) * (TB // HALO) - 1, 0), 0))
    return _call(
        body, "even_bwd", (nb,),
        [rev(D), rev(EVEN_IN), ptail, rev(D), _const_spec((1, D)), _wspec(N_EIN, OFF_EIN),
         _wspec(N_SQ, OFF_EOUT), _const_spec((1, SG_W)), _const_spec((SG_HEADS, 128, 128)),
         _const_spec((128, 128)), _const_spec((8, SC_W))],
        [rev(D), rev(EVEN_IN), rev(D), rev(D), _const_spec((1, D)), _const_spec((SG_HEADS, 128, 128)),
         _const_spec((128, 128)), _const_spec((1, SG_W)), _const_spec((8, SC_W))],
        [jax.ShapeDtypeStruct((t, D), F32), jax.ShapeDtypeStruct((t, EVEN_IN), BF16),
         jax.ShapeDtypeStruct((t, D), BF16), jax.ShapeDtypeStruct((t, D), BF16),
         jax.ShapeDtypeStruct((1, D), F32), jax.ShapeDtypeStruct((SG_HEADS, 128, 128), F32),
         jax.ShapeDtypeStruct((128, 128), F32), jax.ShapeDtypeStruct((1, SG_W), F32),
         jax.ShapeDtypeStruct((8, SC_W), F32)],
        (x, proj, proj, dx1, gamma, wg, wg, lng, ws, bst, cw), [pltpu.VMEM((HALO, SC_W), F32)], comm)


def _ffn_fwd(x, w_gu, w_d, gamma, name, comm=None, target=None):
    t = x.shape[0]
    last = target is not None

    def body(*refs):
        x_ref, gam_ref, wg_ref, wu_ref, wd_ref = refs[:5]
        y_ref, g_ref, u_ref = refs[5 + last:8 + last]
        xv = x_ref[...]
        h, _ = _rms(xv, gam_ref[...])
        hb = h.astype(BF16)
        g = _dot(hb, wg_ref[...].reshape(D_FF, D), 1, 1)
        u = _dot(hb, wu_ref[...].reshape(D_FF, D), 1, 1)
        g_ref[...] = g.astype(BF16)
        u_ref[...] = u.astype(BF16)
        act = g * jax.nn.sigmoid(g) * u
        y = xv + _dot(act.astype(BF16), wd_ref[...].reshape(D_FF, D), 1, 0)
        if not last:
            y_ref[...] = y
            return
        loss_ref = refs[9]

        @pl.when(pl.program_id(0) == 0)
        def _():
            loss_ref[...] = jnp.zeros_like(loss_ref)

        err = y - refs[5][...]
        y_ref[...] = err * (1.0 / D)
        sq = jnp.sum(jnp.sum(err * err, axis=-1, keepdims=True), axis=0, keepdims=True)
        loss_ref[...] += (0.5 / D) * sq

    row = pl.BlockSpec((TB, D), lambda i: (i, 0))
    wide = pl.BlockSpec((TB, D_FF), lambda i: (i, 0))
    in_specs = [row, _const_spec((1, D)), _wspec(N_FF, OFF_GATE), _wspec(N_FF, OFF_UP), _wspec(N_FF, 0)]
    out_specs = [row, wide, wide]
    out_shape = [jax.ShapeDtypeStruct((t, D), F32), jax.ShapeDtypeStruct((t, D_FF), BF16),
                 jax.ShapeDtypeStruct((t, D_FF), BF16)]
    args = (x, gamma, w_gu, w_gu, w_d)
    if last:
        in_specs, args = in_specs + [row], args + (target,)
        out_specs, out_shape = out_specs + [_const_spec((8, 128))], out_shape + [jax.ShapeDtypeStruct((8, 128), F32)]
    return _call(body, name, (t // TB,), in_specs, out_specs, out_shape, args, (), comm)


def _ffn_bwd(x, g, u, dy, w_gu, w_d, gamma, name, comm=None):
    t = x.shape[0]

    def body(x_ref, g_ref, u_ref, dy_ref, gam_ref, wg_ref, wu_ref, wd_ref,
             dx_ref, act_ref, dg_ref, du_ref, h_ref, dgam_ref):
        @pl.when(pl.program_id(0) == 0)
        def _():
            dgam_ref[...] = jnp.zeros_like(dgam_ref)

        xv = x_ref[...]
        gam = gam_ref[...]
        h, r = _rms(xv, gam)
        h_ref[...] = h.astype(BF16)
        dyv = dy_ref[...]
        dact = _dot(dyv.astype(BF16), wd_ref[...].reshape(D_FF, D), 1, 1)
        gv = g_ref[...].astype(F32)
        uv = u_ref[...].astype(F32)
        sg = jax.nn.sigmoid(gv)
        silu = gv * sg
        act_ref[...] = (silu * uv).astype(BF16)
        dgb = (dact * uv * (sg * (1.0 + gv * (1.0 - sg)))).astype(BF16)
        dub = (dact * silu).astype(BF16)
        dg_ref[...] = dgb
        du_ref[...] = dub
        dh = _dot(dgb, wg_ref[...].reshape(D_FF, D), 1, 0) + _dot(dub, wu_ref[...].reshape(D_FF, D), 1, 0)
        dxn, dgam = _rms_bwd(xv, r, gam, dh)
        dgam_ref[...] += dgam
        dx_ref[...] = dyv + dxn

    row = pl.BlockSpec((TB_FFN_BWD, D), lambda i: (i, 0))
    wide = pl.BlockSpec((TB_FFN_BWD, D_FF), lambda i: (i, 0))
    return _call(
        body, name, (t // TB_FFN_BWD,),
        [row, wide, wide, row, _const_spec((1, D)), _wspec(N_FF, OFF_GATE), _wspec(N_FF, OFF_UP),
         _wspec(N_FF, 0)],
        [row, wide, wide, wide, row, _const_spec((1, D))],
        [jax.ShapeDtypeStruct((t, D), F32), jax.ShapeDtypeStruct((t, D_FF), BF16),
         jax.ShapeDtypeStruct((t, D_FF), BF16), jax.ShapeDtypeStruct((t, D_FF), BF16),
         jax.ShapeDtypeStruct((t, D), BF16), jax.ShapeDtypeStruct((1, D), F32)],
        (x, g, u, dy, gamma, w_gu, w_gu, w_d), (), comm)


def _odd_pre_fwd(x, wg, gamma, qbt, kvbt, qa_g, kva_g, pw_bd, pscale, seq):
    t = x.shape[0]
    nbs = seq // TB

    def body(x_ref, gam_ref, win_ref, qb_ref, kvb_ref, qa_ref, kva_ref, pw_ref, ps_ref,
             proj_ref, q_ref, kv_ref, kr_ref, c_ref, tail_ref):
        i = pl.program_id(0)
        h, _ = _rms(x_ref[...], gam_ref[...])
        proj = _dot(h.astype(BF16), win_ref[...].reshape(D, D), 1, 0)
        proj_ref[...] = proj.astype(BF16)
        zp, ql, kvl = proj[:, :POOL_W], proj[:, 256:640], proj[:, 640:896]
        kr_ref[...] = proj[:, 896:1024]
        qn, _ = _rms(ql, qa_ref[...])
        q_ref[...] = _dot(qn.astype(BF16), qb_ref[...], 1, 1).astype(BF16)
        kvn, _ = _rms(kvl, kva_ref[...])
        kv_ref[...] = _dot(kvn.astype(BF16), kvb_ref[...], 1, 1).astype(BF16)
        tail = jnp.where(i % nbs == 0, 0.0, tail_ref[...])
        pooled, _, _ = _pool_fwd(zp, tail, i % nbs)
        tail_ref[...] = zp[TB - HALO:, :]
        c_ref[...] = (_dot(pooled.astype(BF16), pw_ref[...], 1, 0) * ps_ref[...]).astype(BF16)

    def row(w):
        return pl.BlockSpec((TB, w), lambda i: (i, 0))

    return pl.pallas_call(
        body, name="odd_pre_fwd",
        out_shape=[jax.ShapeDtypeStruct((t, D), BF16), jax.ShapeDtypeStruct((t, HEADS * HP), BF16),
                   jax.ShapeDtypeStruct((t, HEADS * HP), BF16), jax.ShapeDtypeStruct((t, 128), F32),
                   jax.ShapeDtypeStruct((t, POOL_W), BF16)],
        grid=(t // TB,),
        in_specs=[row(D), _const_spec((1, D)), _wspec(N_SQ, OFF_OIN), _const_spec((HEADS * HP, Q_LORA)),
                  _const_spec((HEADS * HP, KV_LORA)), _const_spec((1, Q_LORA)), _const_spec((1, KV_LORA)),
                  _const_spec((POOL_W, POOL_W)), _const_spec((1, POOL_W))],
        out_specs=[row(D), row(HEADS * HP), row(HEADS * HP), row(128), row(POOL_W)],
        scratch_shapes=[pltpu.VMEM((HALO, POOL_W), F32)],
        compiler_params=_cparams(1),
    )(x, gamma, wg, qbt, kvbt, qa_g, kva_g, pw_bd, pscale)


def _odd_pre_bwd(x, proj, dx3, dmix, dq, dkv, dkr, wg, gamma, qbt, kvbt, qa_g, kva_g, pw_bd, pscale, seq):
    t = x.shape[0]
    nb, nbs = t // TB, seq // TB

    def body(x_ref, proj_ref, ptail_ref, dx3_ref, dco_ref, dq_ref, dkv_ref, dkr_ref, gam_ref, win_ref, qb_ref,
             kvb_ref, qa_ref, kva_ref, pw_ref, ps_ref,
             dx2_ref, dproj_ref, h_ref, qn_ref, kvn_ref, dgam_ref, dqa_ref, dkva_ref, dpw_ref, dps_ref, head_ref):
        i = pl.program_id(0)
        blk = nb - 1 - i

        @pl.when(i == 0)
        def _():
            dgam_ref[...] = jnp.zeros_like(dgam_ref)
            dqa_ref[...] = jnp.zeros_like(dqa_ref)
            dkva_ref[...] = jnp.zeros_like(dkva_ref)
            dpw_ref[...] = jnp.zeros_like(dpw_ref)
            dps_ref[...] = jnp.zeros_like(dps_ref)

        xv = x_ref[...]
        gam = gam_ref[...]
        h, r = _rms(xv, gam)
        h_ref[...] = h.astype(BF16)
        proj = proj_ref[...].astype(F32)
        zp, ql, kvl = proj[:, :POOL_W], proj[:, 256:640], proj[:, 640:896]

        qa = qa_ref[...]
        qn, rq = _rms(ql, qa)
        qn_ref[...] = qn.astype(BF16)
        dql, dqa = _rms_bwd(ql, rq, qa, _dot(dq_ref[...], qb_ref[...], 1, 0))
        dqa_ref[...] += dqa
        kva = kva_ref[...]
        kvn, rkv = _rms(kvl, kva)
        kvn_ref[...] = kvn.astype(BF16)
        dkvl, dkva = _rms_bwd(kvl, rkv, kva, _dot(dkv_ref[...], kvb_ref[...], 1, 0))
        dkva_ref[...] += dkva

        pt = ptail_ref[...].astype(F32)
        tail = jnp.where(blk % nbs == 0, 0.0, pt[:, :POOL_W])
        pooled, cnt, grp = _pool_fwd(zp, tail, blk % nbs)
        pb = pooled.astype(BF16)
        pw = pw_ref[...]
        dco = dco_ref[...].astype(F32)
        dps_ref[...] += jnp.sum(dco * _dot(pb, pw, 1, 0), axis=0, keepdims=True)
        dpo = (dco * ps_ref[...]).astype(BF16)
        dpw_ref[...] += _dot(pb, dpo, 0, 0)
        dpooled = _dot(dpo, pw, 1, 1)
        dpm = dpooled / cnt
        head = jnp.where(blk % nbs == nbs - 1, 0.0, head_ref[...])
        dz = _pool_bwd(dpooled, dpm, head, grp)
        head_ref[...] = dpm[:HALO, :]

        dproj_ref[:, :POOL_W] = dz.astype(BF16)
        dproj_ref[:, 256:640] = dql.astype(BF16)
        dproj_ref[:, 640:896] = dkvl.astype(BF16)
        dproj_ref[:, 896:1024] = dkr_ref[...].astype(BF16)
        dh = _dot(dproj_ref[...], win_ref[...].reshape(D, D), 1, 1)
        dxn, dgam = _rms_bwd(xv, r, gam, dh)
        dgam_ref[...] += dgam
        dx2_ref[...] = dx3_ref[...] + dxn

    def rev(w):
        return pl.BlockSpec((TB, w), lambda i: (nb - 1 - i, 0))

    ptail = pl.BlockSpec((HALO, D), lambda i: (jnp.maximum((nb - 1 - i) * (TB // HALO) - 1, 0), 0))
    return pl.pallas_call(
        body, name="odd_pre_bwd",
        out_shape=[jax.ShapeDtypeStruct((t, D), F32), jax.ShapeDtypeStruct((t, D), BF16),
                   jax.ShapeDtypeStruct((t, D), BF16), jax.ShapeDtypeStruct((t, Q_LORA), BF16),
                   jax.ShapeDtypeStruct((t, KV_LORA), BF16), jax.ShapeDtypeStruct((1, D), F32),
                   jax.ShapeDtypeStruct((1, Q_LORA), F32), jax.ShapeDtypeStruct((1, KV_LORA), F32),
                   jax.ShapeDtypeStruct((POOL_W, POOL_W), F32), jax.ShapeDtypeStruct((1, POOL_W), F32)],
        grid=(nb,),
        in_specs=[rev(D), rev(D), ptail, rev(D), rev(POOL_W), rev(HEADS * HP), rev(HEADS * HP), rev(128),
                  _const_spec((1, D)), _wspec(N_SQ, OFF_OIN), _const_spec((HEADS * HP, Q_LORA)),
                  _const_spec((HEADS * HP, KV_LORA)), _const_spec((1, Q_LORA)), _const_spec((1, KV_LORA)),
                  _const_spec((POOL_W, POOL_W)), _const_spec((1, POOL_W))],
        out_specs=[rev(D), rev(D), rev(D), rev(Q_LORA), rev(KV_LORA), _const_spec((1, D)), _const_spec((1, Q_LORA)),
                   _const_spec((1, KV_LORA)), _const_spec((POOL_W, POOL_W)), _const_spec((1, POOL_W))],
        scratch_shapes=[pltpu.VMEM((HALO, POOL_W), F32)],
        compiler_params=_cparams(1),
    )(x, proj, proj, dx3, dmix, dq, dkv, dkr, gamma, wg, qbt, kvbt, qa_g, kva_g, pw_bd, pscale)


def _attn_specs(seq):
    head = pl.BlockSpec((seq, HP), lambda b, h: (b, h))
    shared = pl.BlockSpec((seq, 128), lambda b, h: (b, 0))
    gain = pl.BlockSpec((1, HP), lambda b, h: (0, 0))
    return head, shared, gain


def _causal_bias(n):
    rows = lax.broadcasted_iota(jnp.int32, (n, n), 0)
    cols = lax.broadcasted_iota(jnp.int32, (n, n), 1)
    return jnp.where(cols <= rows, 0.0, NEG_INF)


def _attn_fwd(q, kv, kr, cos, sin, gq, gk, seq, comm=None):
    t = q.shape[0]
    qb = min(512, seq)

    def body(q_ref, kv_ref, kr_ref, c_ref, s_ref, gq_ref, gk_ref, o_ref, lse_ref):
        c, s = c_ref[...], s_ref[...]
        qf, _ = _qk_prep(q_ref[...].astype(F32), gq_ref[...], c, s)
        kin = jnp.concatenate([kv_ref[:, :128].astype(F32), kr_ref[...]], axis=1)
        kf, _ = _qk_prep(kin, gk_ref[...], c, s)
        qf, kf = qf.astype(BF16), kf.astype(BF16)
        v1 = jnp.concatenate([kv_ref[:, 128:], jnp.ones((seq, V_DIM), BF16)], axis=1)
        bias = _causal_bias(qb)
        for q0 in range(0, seq, qb):
            q1 = q0 + qb
            qblk = qf[q0:q1]
            s_dg = _dot(qblk, kf[q0:q1], 1, 1) + bias
            m = jnp.max(s_dg, axis=-1, keepdims=True)
            if q0:
                s_off = _dot(qblk, kf[:q0], 1, 1)
                m = jnp.maximum(m, jnp.max(s_off, axis=-1, keepdims=True))
            acc = _dot(jnp.exp(s_dg - m).astype(BF16), v1[q0:q1], 1, 0)
            if q0:
                acc = acc + _dot(jnp.exp(s_off - m).astype(BF16), v1[:q0], 1, 0)
            l = acc[:, V_DIM:]
            o_ref[q0:q1, :] = (acc[:, :V_DIM] / l).astype(BF16)
            lse_ref[q0:q1, :] = m + jnp.log(l)

    head, shared, gain = _attn_specs(seq)
    per_head = pl.BlockSpec((seq, V_DIM), lambda b, h: (b, h))
    return _call(
        body, "attn_fwd", (t // seq, HEADS),
        [head, head, shared, shared, shared, gain, gain], [per_head, per_head],
        [jax.ShapeDtypeStruct((t, HEADS * V_DIM), BF16), jax.ShapeDtypeStruct((t, HEADS * V_DIM), F32)],
        (q, kv, kr, cos, sin, gq, gk), (), comm)


def _attn_bwd(q, kv, kr, cos, sin, gq, gk, dmix, d_out, lse, seq, comm=None):
    t = q.shape[0]
    qb = min(512, seq)

    def body(q_ref, kv_ref, kr_ref, c_ref, s_ref, gq_ref, gk_ref, do_ref, o_ref, lse_ref,
             dq_ref, dkv_ref, dkr_ref, dgq_ref, dgk_ref, dqf_ref, dkf_ref, dv_ref):
        b, hd = pl.program_id(0), pl.program_id(1)

        @pl.when((b == 0) & (hd == 0))
        def _():
            dgq_ref[...] = jnp.zeros_like(dgq_ref)
            dgk_ref[...] = jnp.zeros_like(dgk_ref)

        c, sn = c_ref[...], s_ref[...]
        gq_v, gk_v = gq_ref[...], gk_ref[...]
        qin = q_ref[...].astype(F32)
        kin = jnp.concatenate([kv_ref[:, :128].astype(F32), kr_ref[...]], axis=1)
        qf32, rq = _qk_prep(qin, gq_v, c, sn)
        kf32, rk = _qk_prep(kin, gk_v, c, sn)
        qf, kf = qf32.astype(BF16), kf32.astype(BF16)
        vb = kv_ref[:, 128:]
        dkf_ref[...] = jnp.zeros_like(dkf_ref)
        dv_ref[...] = jnp.zeros_like(dv_ref)
        bias = _causal_bias(qb)
        for q0 in range(0, seq, qb):
            q1 = q0 + qb
            qblk = qf[q0:q1]
            do = do_ref[q0:q1, :]
            lse_col = lse_ref[q0:q1, 0:1]
            d_col = jnp.sum(do.astype(F32) * o_ref[q0:q1, :].astype(F32), axis=-1, keepdims=True)
            dq_acc = None
            for k0, k1, diag in ((q0, q1, True), (0, q0, False)):
                if k1 == k0:
                    continue
                s = _dot(qblk, kf[k0:k1], 1, 1)
                p = jnp.exp((s + bias if diag else s) - lse_col)
                dv_ref[k0:k1, :] += _dot(p.astype(BF16), do, 0, 0)
                ds = (p * (_dot(do, vb[k0:k1], 1, 1) - d_col)).astype(BF16)
                part = _dot(ds, kf[k0:k1], 1, 0)
                dq_acc = part if dq_acc is None else dq_acc + part
                dkf_ref[k0:k1, :] += _dot(ds, qblk, 0, 0)
            dqf_ref[q0:q1, :] = dq_acc
        dqin, dgq = _qk_prep_bwd(dqf_ref[...], qin, rq, gq_v, c, sn)
        dkin, dgk = _qk_prep_bwd(dkf_ref[...], kin, rk, gk_v, c, sn)
        dgq_ref[...] += dgq
        dgk_ref[...] += dgk
        dq_ref[...] = dqin.astype(BF16)
        dkv_ref[:, :128] = dkin[:, :128].astype(BF16)
        dkv_ref[:, 128:] = dv_ref[...].astype(BF16)

        @pl.when(hd == 0)
        def _():
            dkr_ref[...] = dkin[:, 128:]

        @pl.when(hd != 0)
        def _():
            dkr_ref[...] += dkin[:, 128:]

    head, shared, gain = _attn_specs(seq)
    per_head = pl.BlockSpec((seq, V_DIM), lambda b, h: (b, h))
    return _call(
        body, "attn_bwd", (t // seq, HEADS),
        [head, head, shared, shared, shared, gain, gain,
         pl.BlockSpec((seq, V_DIM), lambda b, h: (b, 2 + h)), per_head, per_head],
        [head, head, shared, gain, gain],
        [jax.ShapeDtypeStruct((t, HEADS * HP), BF16), jax.ShapeDtypeStruct((t, HEADS * HP), BF16),
         jax.ShapeDtypeStruct((t, 128), F32), jax.ShapeDtypeStruct((1, HP), F32),
         jax.ShapeDtypeStruct((1, HP), F32)],
        (q, kv, kr, cos, sin, gq, gk, dmix, d_out, lse),
        [pltpu.VMEM((seq, HP), F32), pltpu.VMEM((seq, HP), F32), pltpu.VMEM((seq, V_DIM), F32)], comm)


def _odd_post_fwd(x, c_out, d_out, wg):
    t = x.shape[0]

    def body(x_ref, c_ref, d_ref, w_ref, y_ref):
        y_ref[...] = (x_ref[...] + _dot(c_ref[...], w_ref[0:2].reshape(POOL_W, D), 1, 0)
                      + _dot(d_ref[...], w_ref[2:8].reshape(HEADS * V_DIM, D), 1, 0))

    def row(w):
        return pl.BlockSpec((TB, w), lambda i: (i, 0))

    return pl.pallas_call(
        body, name="odd_post_fwd", out_shape=jax.ShapeDtypeStruct((t, D), F32), grid=(t // TB,),
        in_specs=[row(D), row(POOL_W), row(HEADS * V_DIM), _wspec(N_SQ, OFF_OOUT)], out_specs=row(D),
        compiler_params=_cparams(1),
    )(x, c_out, d_out, wg)


def _odd_post_bwd(dx3, wg, comm=None):
    t = dx3.shape[0]

    def body(d_ref, w_ref, o_ref):
        o_ref[...] = _dot(d_ref[...].astype(BF16), w_ref[...].reshape(D, D), 1, 1).astype(BF16)

    row = pl.BlockSpec((TB, D), lambda i: (i, 0))
    (res,), extra = _call(body, "odd_post_bwd", (t // TB,), [row, _wspec(N_SQ, OFF_OOUT)], [row],
                          [jax.ShapeDtypeStruct((t, D), BF16)], (dx3, wg), (), comm)
    return res, extra


def _tn(a_list, b, tm, name, into=None, comm=None):
    t, n_out = b.shape
    widths = [a.shape[1] for a in a_list]
    tk = min(TK_DW, t)
    m, na, nk = sum(widths), len(a_list), t // tk
    assert na == 1 or tm == m

    def body(*refs):
        a_refs, b_ref, o_ref, acc_ref = refs[:na], refs[na], refs[-2], refs[-1]
        k = pl.program_id(1)

        @pl.when(k == 0)
        def _():
            acc_ref[...] = jnp.zeros_like(acc_ref)

        bb = b_ref[...].astype(BF16)
        m0 = 0
        for a_ref, w in zip(a_refs, widths):
            rows = slice(0, tm) if na == 1 else slice(m0, m0 + w)
            acc_ref[rows, :] += _dot(a_ref[...].astype(BF16), bb, 0, 0)
            m0 += w

        @pl.when(k == nk - 1)
        def _():
            o_ref[...] = acc_ref[...].astype(BF16).reshape(o_ref.shape)

    if na == 1:
        in_specs = [pl.BlockSpec((tk, tm), lambda i, k: (k, i))]
    else:
        in_specs = [pl.BlockSpec((tk, w), lambda i, k: (k, 0)) for w in widths]
    in_specs.append(pl.BlockSpec((tk, n_out), lambda i, k: (k, 0)))
    args = list(a_list) + [b]
    if into is None:
        out_spec = pl.BlockSpec((tm, n_out), lambda i, k: (i, 0))
        out_shape = jax.ShapeDtypeStruct((m, n_out), BF16)
        aliases = {}
    else:
        buf, n, off = into
        assert n_out == D and tm % n == 0 and off % n == 0 and (na == 1 or tm // n == N_DEV)
        idx = off // n
        out_spec = pl.BlockSpec((tm // n, n, D), lambda i, k: (i, idx, 0))
        out_shape = jax.ShapeDtypeStruct(buf.shape, BF16)
        in_specs.append(pl.BlockSpec(memory_space=pl.ANY))
        args.append(buf)
        aliases = {len(args) - 1: 0}
    (res,), extra = _call(body, name, (m // tm, nk), in_specs, [out_spec], [out_shape], args,
                          [pltpu.VMEM((tm, n_out), F32)], comm, aliases)
    return (res, extra) if comm is not None else res


def _adamw(ws, gs, ms, vs, name, nblk=1):
    n = len(ws)
    c1 = 1.0 - B1 ** STEP
    c2 = 1.0 - B2 ** STEP

    def body(*refs):
        for a in range(n):
            w, g, m, v = (refs[k * n + a][...] for k in range(4))
            d_ref, m_ref, v_ref = (refs[(4 + k) * n + a] for k in range(3))
            m_new = B1 * m + (1.0 - B1) * g
            v_new = B2 * v + (1.0 - B2) * (g * g)
            d_ref[...] = -LR * ((m_new / c1) / (jnp.sqrt(v_new / c2) + ADAM_EPS) + WD * w)
            m_ref[...] = m_new
            v_ref[...] = v_new

    grid = (nblk,)
    assert all(w.shape[0] % nblk == 0 and (nblk == 1 or (w.shape[0] // nblk) % 8 == 0) for w in ws)
    specs = [pl.BlockSpec((w.shape[0] // nblk, w.shape[1]), lambda i: (i, 0)) for w in ws]
    outs, _ = _call(body, name, grid, specs * 4, specs * 3, [jax.ShapeDtypeStruct(w.shape, F32) for w in ws] * 3,
                    (*ws, *gs, *ms, *vs))
    return outs[:n], outs[n:2 * n], outs[2 * n:]


def _rows1024(a, rows):
    flat = a.reshape(-1, D)
    return jnp.pad(flat, ((0, rows - flat.shape[0]), (0, 0)))


def _pack_shards(even_w_in, even_w_out, odd_w_in, q_b, kv_b, odd_w_out, ffn_w_gate, ffn_w_up, ffn_w_down):
    mix0 = jnp.concatenate([even_w_in[0].T, jnp.zeros((OFF_EOUT - N_EIN, D), F32), even_w_out[0]], axis=0)
    gu = [jnp.concatenate([ffn_w_gate[layer].T, ffn_w_up[layer].T], axis=0) for layer in range(2)]
    mix1 = jnp.concatenate([jnp.pad(odd_w_in[0], ((0, 0), (0, D - ODD_IN))), odd_w_out[0],
                            _rows1024(q_b[0].T, N_QB), _rows1024(kv_b[0].T, N_KVB),
                            jnp.zeros((R_MIX1 - OFF_KVB - N_KVB, D), F32)], axis=0)
    return [c.astype(BF16) for c in (mix0, gu[0], ffn_w_down[0], mix1, gu[1], ffn_w_down[1])]


def _pad_heads(a):
    k = a.shape[1]
    return jnp.pad(a.reshape(HEADS, QK_DIM, k), ((0, 0), (0, HP - QK_DIM), (0, 0))).reshape(HEADS * HP, k)


def _small_pack(parts):
    flat = []
    for p in parts:
        v = p.reshape(-1)
        flat.append(jnp.pad(v, (0, (-v.shape[0]) % 1024)))
    return jnp.concatenate(flat).reshape(-1, 128)


def _small_unpack(buf, shapes):
    flat = buf.reshape(-1)
    out, off = [], 0
    for s in shapes:
        size = int(np.prod(s))
        out.append(flat[off:off + size].reshape(s))
        off += size + (-size) % 1024
    return out


def _step(x3d, positions, target3d, chunks, tile, c_arr, chip_arr, mix_norm, ffn_norm, sg_ln_g, sg_w_s, sg_b_s,
          pool_w, q_norm, k_norm):
    bsz, seq, _ = x3d.shape
    t = bsz * seq
    x0 = x3d.reshape(t, D)
    target = target3d.reshape(t, D)
    my_mix0, my_gu0, my_d0, my_mix1, my_gu1, my_d1 = chunks

    lane = np.arange(128)
    inv_freq = np.where(lane < QK_ROPE, ROPE_THETA ** (-(2.0 * (lane % 32)) / QK_ROPE), 0.0)
    inv_freq = jnp.asarray(inv_freq.reshape(1, 128), F32)
    (cos, sin), (w_mix0, tiles) = _rope_tables(positions.reshape(t, 1), inv_freq, _gather_comm([my_mix0, tile]))

    conv_w = tiles[:, 0:3, 0:64].transpose(1, 0, 2).reshape(3, SC_W)
    pool_scale = tiles[:, 3, 0:32].reshape(1, POOL_W)
    q_a_norm = tiles[:, 4, 0:48].reshape(1, Q_LORA)
    kv_a_norm = tiles[:, 5, 0:32].reshape(1, KV_LORA)
    ws = sg_w_s[0]
    bst = jnp.pad(sg_b_s[0].T, ((0, 0), (0, 128 - SG_HEADS)))
    cw = jnp.pad(conv_w, ((0, 8 - 3), (0, 0)))
    pw_bd = jax.scipy.linalg.block_diag(*[pool_w[0, g] for g in range(4)]).astype(BF16)
    gq = jnp.pad(q_norm * ATT_SCALE, ((0, 0), (0, HP - QK_DIM)))
    gk = jnp.pad(k_norm, ((0, 0), (0, HP - QK_DIM)))

    (x1, proj_e), (w_gu0, w_d0) = _even_fwd(x0, w_mix0, mix_norm[0:1], sg_ln_g, ws, bst, cw, seq,
                                            _gather_comm([my_gu0, my_d0]))
    (x2, g0, u0), (w_mix1, w_d1) = _ffn_fwd(x1, w_gu0, w_d0, ffn_norm[0:1], "ffn_fwd0",
                                            _gather_comm([my_mix1, my_d1]))
    qbt = _pad_heads(w_mix1[:, OFF_QB:OFF_QB + N_QB_USED, :].reshape(HEADS * QK_DIM, Q_LORA))
    kvbt = w_mix1[:, OFF_KVB:OFF_KVB + N_KVB, :].reshape(HEADS * HP, KV_LORA)
    proj_o, q, kv, kr, c_out = _odd_pre_fwd(x2, w_mix1, mix_norm[1:2], qbt, kvbt, q_a_norm, kv_a_norm, pw_bd,
                                            pool_scale, seq)
    (d_out, lse), (w_gu1,) = _attn_fwd(q, kv, kr, cos, sin, gq, gk, seq, _gather_comm([my_gu1]))
    x3 = _odd_post_fwd(x2, c_out, d_out, w_mix1)
    (dy, g1, u1, loss_tile), _ = _ffn_fwd(x3, w_gu1, w_d1, ffn_norm[1:2], "ffn_fwd1", None, target)

    def chunk(rows, padded=False):
        return jnp.zeros((N_DEV, rows, D), BF16) if padded else lax.empty((N_DEV, rows, D), BF16)

    (dx3, act1, dg1, du1, h3, dgam_f1), _ = _ffn_bwd(x3, g1, u1, dy, w_gu1, w_d1, ffn_norm[1:2], "ffn_bwd1")
    gp_ffn1 = _tn([dg1], h3, 1408, "dw_gate1", (chunk(R_GU + N_FF), N_FF, OFF_GATE))
    gp_ffn1 = _tn([du1], h3, 1408, "dw_up1", (gp_ffn1, N_FF, OFF_UP))
    gp_ffn1 = _tn([act1], dy, 1408, "dw_down1", (gp_ffn1, N_FF, R_GU))

    dmix_o, (ga_ffn1,) = _odd_post_bwd(dx3, w_mix1, _pair_exchange_comm(gp_ffn1))
    pb_ffn1 = _rs_pair_sum(gp_ffn1, ga_ffn1, c_arr, "rs_pair_sum_ffn1")
    gp_mix1 = _tn([c_out, d_out], dx3, D, "dw_oout", (chunk(R_MIX1, True), N_SQ, OFF_OOUT))
    (dq, dkv, dkr, dgq, dgk), (gb_ffn1,) = _attn_bwd(q, kv, kr, cos, sin, gq, gk, dmix_o, d_out, lse, seq,
                                                    _chip_exchange_comm(pb_ffn1))
    gsh_ffn1 = _rs_final_sum(pb_ffn1, gb_ffn1, chip_arr, "rs_final_sum_ffn1")
    (dx2, dproj_o, h2, qn, kvn, dgam_m1, dqa, dkva, dpw_bd, dps) = _odd_pre_bwd(
        x2, proj_o, dx3, dmix_o, dq, dkv, dkr, w_mix1, mix_norm[1:2], qbt, kvbt, q_a_norm, kv_a_norm, pw_bd,
        pool_scale, seq)
    gp_mix1 = _tn([h2], dproj_o, D, "dw_oin", (gp_mix1, N_SQ, OFF_OIN))
    d_qbt = _tn([dq], qn, HEADS * HP, "dw_qb")
    d_qb_rows = d_qbt.reshape(HEADS, HP, Q_LORA)[:, :QK_DIM].reshape(N_DEV, N_QB_USED, D)
    d_kvb_rows = _tn([dkv], kvn, HEADS * HP, "dw_kvb").reshape(N_DEV, N_KVB, D)
    gp_mix1 = lax.dynamic_update_slice(gp_mix1, d_qb_rows, (0, OFF_QB, 0))
    gp_mix1 = lax.dynamic_update_slice(gp_mix1, d_kvb_rows, (0, OFF_KVB, 0))

    (dx1, act0, dg0, du0, h1, dgam_f0), (ga_mix1,) = _ffn_bwd(x1, g0, u0, dx2, w_gu0, w_d0, ffn_norm[0:1], "ffn_bwd0",
                                                             _pair_exchange_comm(gp_mix1))
    pb_mix1 = _rs_pair_sum(gp_mix1, ga_mix1, c_arr, "rs_pair_sum_mix1")
    gp_ffn0a, (gb_mix1,) = _tn([dg0], h1, 1408, "dw_gate0", (chunk(R_GU), N_FF, OFF_GATE),
                               _chip_exchange_comm(pb_mix1))
    gsh_mix1 = _rs_final_sum(pb_mix1, gb_mix1, chip_arr, "rs_final_sum_mix1")
    gp_ffn0a = _tn([du0], h1, 1408, "dw_up0", (gp_ffn0a, N_FF, OFF_UP))
    gp_ffn0b, (ga_ffn0a,) = _tn([act0], dx2, 1408, "dw_down0", (chunk(N_FF), N_FF, 0),
                                _pair_exchange_comm(gp_ffn0a))
    pb_ffn0a = _rs_pair_sum(gp_ffn0a, ga_ffn0a, c_arr, "rs_pair_sum_ffn0a")

    (dx0, dproj_e, mix_e, h0, dgam_m0, dws, dbc, dlng, dcw), (gb_ffn0a, ga_ffn0b) = _even_bwd(
        x0, proj_e, dx1, w_mix0, mix_norm[0:1], sg_ln_g, ws, bst, cw, seq,
        _both(_chip_exchange_comm(pb_ffn0a), _pair_exchange_comm(gp_ffn0b)))
    pb_ffn0b = _rs_pair_sum(gp_ffn0b, ga_ffn0b, c_arr, "rs_pair_sum_ffn0b")
    gsh_ffn0a = _rs_final_sum(pb_ffn0a, gb_ffn0a, chip_arr, "rs_final_sum_ffn0a")
    gp_mix0, (gb_ffn0b,) = _tn([mix_e], dx1, D, "dw_eout", (chunk(R_MIX0, True), N_SQ, OFF_EOUT),
                               _chip_exchange_comm(pb_ffn0b))
    gsh_ffn0b = _rs_final_sum(pb_ffn0b, gb_ffn0b, chip_arr, "rs_final_sum_ffn0b")

    small = _small_pack([
        jnp.concatenate([dgam_m0, dgam_m1], 0), jnp.concatenate([dgam_f0, dgam_f1], 0), dlng,
        dws[None], dbc[:, :SG_HEADS].T[None], dcw[:3],
        jnp.stack([dpw_bd[g * POOL_GD:(g + 1) * POOL_GD, g * POOL_GD:(g + 1) * POOL_GD] for g in range(4)])[None],
        dps, dqa, dkva, dgq[:, :QK_DIM] * ATT_SCALE, dgk[:, :QK_DIM], loss_tile[0:1, 0:1]])
    gp_mix0, (small_all,) = _tn([dproj_e], h0, 1280, "dw_ein", (gp_mix0, N_EIN, OFF_EIN), _gather_comm([small]))
    small_sum = _small_unpack(_sum_gathered(small_all), SMALL_SHAPES)
    return dx0.reshape(bsz, seq, D), (gsh_ffn0a, gsh_ffn0b, gsh_mix1, gsh_ffn1), gp_mix0, small_sum


SMALL_SHAPES = [(2, D), (2, D), (1, SG_W), (1, SG_HEADS, 128, 128), (1, SG_HEADS, 128), (3, SC_W),
                (1, 4, POOL_GD, POOL_GD), (1, POOL_W), (1, Q_LORA), (1, KV_LORA), (1, QK_DIM), (1, QK_DIM), (1, 1)]


def kernel(x, positions, mix_norm, ffn_norm, even_w_in, sg_ln_g, sg_w_s, sg_b_s, sc_conv_w, even_w_out, odd_w_in, pool_w, pool_scale, q_a_norm, q_b, kv_a_norm, kv_b, q_norm, k_norm, odd_w_out, ffn_w_gate, ffn_w_up, ffn_w_down, loss_target, m_mix_norm, m_ffn_norm, m_even_w_in, m_sg_ln_g, m_sg_w_s, m_sg_b_s, m_sc_conv_w, m_even_w_out, m_odd_w_in, m_pool_w, m_pool_scale, m_q_a_norm, m_q_b, m_kv_a_norm, m_kv_b, m_q_norm, m_k_norm, m_odd_w_out, m_ffn_w_gate, m_ffn_w_up, m_ffn_w_down, v_mix_norm, v_ffn_norm, v_even_w_in, v_sg_ln_g, v_sg_w_s, v_sg_b_s, v_sc_conv_w, v_even_w_out, v_odd_w_in, v_pool_w, v_pool_scale, v_q_a_norm, v_q_b, v_kv_a_norm, v_kv_b, v_q_norm, v_k_norm, v_odd_w_out, v_ffn_w_gate, v_ffn_w_up, v_ffn_w_down):
    xi, yi, ci = _place()
    me = 4 * xi + 2 * yi + ci

    chunks = _pack_shards(even_w_in, even_w_out, odd_w_in, q_b, kv_b, odd_w_out, ffn_w_gate, ffn_w_up, ffn_w_down)

    def lane_pad(a):
        return jnp.pad(a, ((0, 0), (0, 128 - a.shape[1])))

    tile = jnp.concatenate([lane_pad(sc_conv_w[0]), lane_pad(pool_scale), lane_pad(q_a_norm), lane_pad(kv_a_norm),
                            jnp.zeros((2, 128), F32)], axis=0)
    c_arr = jnp.reshape(ci, (1,)).astype(jnp.int32)
    chip_arr = jnp.reshape(2 * xi + yi, (1,)).astype(jnp.int32)
    grad_x, (gsh_ffn0a, gsh_ffn0b, gsh_mix1, gsh_ffn1), gp_mix0, tot = _step(
        x, positions, loss_target, chunks, tile, c_arr, chip_arr, mix_norm, ffn_norm, sg_ln_g, sg_w_s, sg_b_s,
        pool_w, q_norm, k_norm)

    (g_mix, g_ffn, g_lng, g_ws, g_bs, g_cw_full, g_pw, g_ps_full, g_qa_full, g_kva_full, g_qn, g_kn, loss) = tot
    g_cw = lax.dynamic_slice_in_dim(g_cw_full, me * 64, 64, axis=1)[None]
    g_ps = lax.dynamic_slice_in_dim(g_ps_full, me * 32, 32, axis=1)
    g_qa = lax.dynamic_slice_in_dim(g_qa_full, me * 48, 48, axis=1)
    g_kva = lax.dynamic_slice_in_dim(g_kva_full, me * 32, 32, axis=1)

    def tr(a):
        return jnp.swapaxes(a, -1, -2)

    g_gate = tr(jnp.stack([gsh_ffn0a[OFF_GATE:OFF_GATE + N_FF], gsh_ffn1[OFF_GATE:OFF_GATE + N_FF]]))
    g_up = tr(jnp.stack([gsh_ffn0a[OFF_UP:OFF_UP + N_FF], gsh_ffn1[OFF_UP:OFF_UP + N_FF]]))
    g_down = jnp.stack([gsh_ffn0b, gsh_ffn1[R_GU:R_GU + N_FF]])
    g_oin = gsh_mix1[OFF_OIN:OFF_OIN + N_SQ, :ODD_IN][None]
    g_oout = gsh_mix1[OFF_OOUT:OFF_OOUT + N_SQ][None]
    g_qb = tr(gsh_mix1[OFF_QB:OFF_QB + N_QB_USED].reshape(1, 144, Q_LORA))
    g_kvb = tr(gsh_mix1[OFF_KVB:OFF_KVB + N_KVB].reshape(1, 192, KV_LORA))
    transposed = ("even_w_in", "odd_w_in", "q_b", "kv_b", "ffn_w_gate", "ffn_w_up")

    names = ("mix_norm", "ffn_norm", "even_w_in", "sg_ln_g", "sg_w_s", "sg_b_s", "sc_conv_w", "even_w_out",
             "odd_w_in", "pool_w", "pool_scale", "q_a_norm", "q_b", "kv_a_norm", "kv_b", "q_norm", "k_norm",
             "odd_w_out", "ffn_w_gate", "ffn_w_up", "ffn_w_down")
    grads = dict(mix_norm=g_mix, ffn_norm=g_ffn, sg_ln_g=g_lng, sg_w_s=g_ws, sg_b_s=g_bs,
                 sc_conv_w=g_cw, odd_w_in=g_oin, pool_w=g_pw, pool_scale=g_ps, q_a_norm=g_qa,
                 q_b=g_qb, kv_a_norm=g_kva, kv_b=g_kvb, q_norm=g_qn, k_norm=g_kn, odd_w_out=g_oout,
                 ffn_w_gate=g_gate, ffn_w_up=g_up, ffn_w_down=g_down)
    weights = dict(mix_norm=mix_norm, ffn_norm=ffn_norm, even_w_in=even_w_in, sg_ln_g=sg_ln_g, sg_w_s=sg_w_s,
                   sg_b_s=sg_b_s, sc_conv_w=sc_conv_w, even_w_out=even_w_out, odd_w_in=odd_w_in, pool_w=pool_w,
                   pool_scale=pool_scale, q_a_norm=q_a_norm, q_b=q_b, kv_a_norm=kv_a_norm, kv_b=kv_b, q_norm=q_norm,
                   k_norm=k_norm, odd_w_out=odd_w_out, ffn_w_gate=ffn_w_gate, ffn_w_up=ffn_w_up,
                   ffn_w_down=ffn_w_down)
    m_in = dict(mix_norm=m_mix_norm, ffn_norm=m_ffn_norm, even_w_in=m_even_w_in, sg_ln_g=m_sg_ln_g, sg_w_s=m_sg_w_s,
                sg_b_s=m_sg_b_s, sc_conv_w=m_sc_conv_w, even_w_out=m_even_w_out, odd_w_in=m_odd_w_in,
                pool_w=m_pool_w, pool_scale=m_pool_scale, q_a_norm=m_q_a_norm, q_b=m_q_b, kv_a_norm=m_kv_a_norm,
                kv_b=m_kv_b, q_norm=m_q_norm, k_norm=m_k_norm, odd_w_out=m_odd_w_out, ffn_w_gate=m_ffn_w_gate,
                ffn_w_up=m_ffn_w_up, ffn_w_down=m_ffn_w_down)
    v_in = dict(mix_norm=v_mix_norm, ffn_norm=v_ffn_norm, even_w_in=v_even_w_in, sg_ln_g=v_sg_ln_g, sg_w_s=v_sg_w_s,
                sg_b_s=v_sg_b_s, sc_conv_w=v_sc_conv_w, even_w_out=v_even_w_out, odd_w_in=v_odd_w_in,
                pool_w=v_pool_w, pool_scale=v_pool_scale, q_a_norm=v_q_a_norm, q_b=v_q_b, kv_a_norm=v_kv_a_norm,
                kv_b=v_kv_b, q_norm=v_q_norm, k_norm=v_k_norm, odd_w_out=v_odd_w_out, ffn_w_gate=v_ffn_w_gate,
                ffn_w_up=v_ffn_w_up, ffn_w_down=v_ffn_w_down)
    delta, new_m, new_v = {}, {}, {}

    def as2d(k, a):
        a = tr(a) if k in transposed else a
        return a.reshape(-1, a.shape[-1])

    def back(k, a):
        shape = weights[k].shape
        return tr(a.reshape(shape[:-2] + (shape[-1], shape[-2]))) if k in transposed else a.reshape(shape)

    def update(group, name, nblk=1):
        outs = _adamw([as2d(k, weights[k]) for k in group], [as2d(k, grads[k]) for k in group],
                      [as2d(k, m_in[k]) for k in group], [as2d(k, v_in[k]) for k in group], name, nblk)
        for i, k in enumerate(group):
            delta[k], new_m[k], new_v[k] = (back(k, o[i]) for o in outs)

    (ga_mix0,) = _comm_alone(_pair_exchange_comm(gp_mix0), "rs_pair_exchange_mix0")
    pb_mix0 = _rs_pair_sum(gp_mix0, ga_mix0, c_arr, "rs_pair_sum_mix0")
    (gb_mix0,) = _comm_alone(_chip_exchange_comm(pb_mix0), "rs_chip_exchange_mix0")
    gsh_mix0 = _rs_final_sum(pb_mix0, gb_mix0, chip_arr, "rs_final_sum_mix0")
    grads["even_w_in"] = tr(gsh_mix0[OFF_EIN:OFF_EIN + N_EIN][None])
    grads["even_w_out"] = gsh_mix0[OFF_EOUT:OFF_EOUT + N_SQ][None]

    update(["ffn_w_gate", "ffn_w_up", "ffn_w_down"], "adamw_ffn", 4)
    update(["even_w_in", "even_w_out", "odd_w_in", "odd_w_out"], "adamw_mix", 2)
    update([k for k in names if k not in delta], "adamw_small")

    return (loss.reshape(()), grad_x, *[grads[k] for k in names], *[delta[k] for k in names],
            *[new_m[k] for k in names], *[new_v[k] for k in names])
```

```python
import functools

import numpy as np
import jax
import jax.numpy as jnp
from jax import lax
from jax.experimental import pallas as pl
from jax.experimental.pallas import tpu as pltpu

F32 = jnp.float32
BF16 = jnp.bfloat16
MESH = pl.DeviceIdType.MESH

D = 1024
EPS = 1e-6
NEG_INF = -1e30
SG_HEADS, SG_HD, SG_W, SG_CHUNK = 4, 128, 512, 128
SC_W = 512
EVEN_IN = 2560
POOL_W = 256
POOL_GD = 64
Q_LORA, KV_LORA, QK_ROPE, QK_NOPE, V_DIM = 384, 256, 64, 128, 128
QK_DIM = QK_NOPE + QK_ROPE
HEADS = 6
HP = 256
ODD_IN = 960
D_FF = 2816
ROPE_THETA = 10000.0
ATT_SCALE = QK_DIM ** -0.5
LR, B1, B2, ADAM_EPS, WD, STEP = 0.001, 0.9, 0.999, 1e-08, 0.01, 10

N_DEV = 8
TB = 512
TK_DW = 1024
HALO = 16
VMEM_LIMIT = 56 * 1024 * 1024

N_EIN, N_FF, N_SQ = 320, 352, 128
OFF_EIN, OFF_EOUT, R_MIX0 = 0, 384, 512
OFF_GATE, OFF_UP, R_GU = 0, 352, 704
OFF_OIN, OFF_OOUT, OFF_QB, OFF_KVB, R_MIX1 = 0, 128, 256, 320, 384
N_QB, N_QB_USED, N_KVB = 64, 54, 48

INV_SQRT2 = 0.7071067811865476
INV_SQRT_2PI = 0.3989422804014327


def _dot(a, b, ca, cb):
    return lax.dot_general(a, b, (((ca,), (cb,)), ((), ())), preferred_element_type=F32)


def _cparams(n_axes=1):
    return pltpu.CompilerParams(dimension_semantics=("arbitrary",) * n_axes, vmem_limit_bytes=VMEM_LIMIT)


def _wspec(n, off, arity=1):
    assert off % n == 0
    idx = off // n
    if arity == 1:
        return pl.BlockSpec((N_DEV, n, D), lambda i: (0, idx, 0), pipeline_mode=pl.Buffered(1))
    return pl.BlockSpec((N_DEV, n, D), lambda i, j: (0, idx, 0), pipeline_mode=pl.Buffered(1))


def _const_spec(shape):
    zeros = (0,) * len(shape)
    return pl.BlockSpec(shape, lambda *_: zeros)


class _Comm:
    def __init__(self, ins, out_shapes, sems, start, wait, mid=None):
        self.ins, self.out_shapes, self.sems, self.start, self.wait, self.mid = ins, out_shapes, sems, start, wait, mid


def _both(c1, c2):
    def split(f1, f2):
        def run(ins, outs, sems):
            f1(ins[:len(c1.ins)], outs[:len(c1.out_shapes)], sems[:len(c1.sems)])
            f2(ins[len(c1.ins):], outs[len(c1.out_shapes):], sems[len(c1.sems):])
        return run

    assert c1.mid is None and c2.mid is None
    return _Comm(c1.ins + c2.ins, c1.out_shapes + c2.out_shapes, c1.sems + c2.sems,
                 split(c1.start, c2.start), split(c1.wait, c2.wait))


def _call(body, name, grid, in_specs, out_specs, out_shape, args, scratch_shapes=(), comm=None, aliases=None):
    n_axes = len(grid)
    aliases = aliases or {}
    if comm is None:
        res = pl.pallas_call(
            body, name=name, grid=grid, in_specs=list(in_specs), out_specs=list(out_specs),
            out_shape=list(out_shape), scratch_shapes=list(scratch_shapes), input_output_aliases=aliases,
            compiler_params=_cparams(n_axes))(*args)
        return list(res), []
    ni, no, ns = len(in_specs), len(out_specs), len(scratch_shapes)
    ci, co = len(comm.ins), len(comm.out_shapes)
    n_steps = int(np.prod(grid))

    def carrier(*refs):
        ins, cin = refs[:ni], refs[ni:ni + ci]
        outs, cout = refs[ni + ci:ni + ci + no], refs[ni + ci + no:ni + ci + no + co]
        scr, sems = refs[ni + ci + no + co:ni + ci + no + co + ns], refs[ni + ci + no + co + ns:]
        step = 0
        for a in range(n_axes):
            step = step * grid[a] + pl.program_id(a)

        @pl.when(step == 0)
        def _():
            comm.start(cin, cout, sems)

        body(*ins, *outs, *scr)

        if comm.mid is not None and n_steps >= 4:
            @pl.when(step == (3 * n_steps) // 4)
            def _():
                comm.mid(cin, cout, sems)

        @pl.when(step == n_steps - 1)
        def _():
            if comm.mid is not None and n_steps < 4:
                comm.mid(cin, cout, sems)
            comm.wait(cin, cout, sems)

    any_spec = pl.BlockSpec(memory_space=pl.ANY)
    res = pl.pallas_call(
        carrier, name=name, grid=grid, in_specs=list(in_specs) + [any_spec] * ci,
        out_specs=list(out_specs) + [any_spec] * co, out_shape=list(out_shape) + list(comm.out_shapes),
        scratch_shapes=list(scratch_shapes) + list(comm.sems), input_output_aliases=aliases,
        compiler_params=_cparams(n_axes))(*args, *comm.ins)
    return list(res[:no]), list(res[no:])


def _comm_alone(comm, name):
    ci, co = len(comm.ins), len(comm.out_shapes)

    def body(*refs):
        cin, cout, sems = refs[:ci], refs[ci:ci + co], refs[ci + co:]
        comm.start(cin, cout, sems)
        if comm.mid is not None:
            comm.mid(cin, cout, sems)
        comm.wait(cin, cout, sems)

    any_spec = pl.BlockSpec(memory_space=pl.ANY)
    res = pl.pallas_call(
        body, name=name, out_shape=list(comm.out_shapes), in_specs=[any_spec] * ci, out_specs=[any_spec] * co,
        scratch_shapes=list(comm.sems))(*comm.ins)
    return list(res)


def _rms(x, g):
    r = lax.rsqrt(jnp.mean(x * x, axis=-1, keepdims=True) + EPS)
    return x * r * g, r


def _rms_bwd(x, r, g, dy):
    xh = x * r
    dxh = dy * g
    dx = r * (dxh - xh * jnp.mean(dxh * xh, axis=-1, keepdims=True))
    dg = jnp.sum(dy * xh, axis=0, keepdims=True)
    return dx, dg


def _gelu(x):
    return 0.5 * x * (1.0 + lax.erf(x * INV_SQRT2))


def _gelu_grad(x):
    return 0.5 * (1.0 + lax.erf(x * INV_SQRT2)) + x * jnp.exp(-0.5 * x * x) * INV_SQRT_2PI


def _shift_down(a, k):
    rows = lax.broadcasted_iota(jnp.int32, a.shape, 0)
    return jnp.where(rows >= k, pltpu.roll(a, k, 0), 0.0)


def _shift_up(a, k):
    n = a.shape[0]
    rows = lax.broadcasted_iota(jnp.int32, a.shape, 0)
    return jnp.where(rows < n - k, pltpu.roll(a, n - k, 0), 0.0)


def _tril_bf16(w):
    r = lax.broadcasted_iota(jnp.int32, w.shape, 0)
    c = lax.broadcasted_iota(jnp.int32, w.shape, 1)
    return jnp.where(r >= c, w, 0.0).astype(BF16)


def _ln_head(vh, g):
    mu = jnp.mean(vh, axis=-1, keepdims=True)
    xc = vh - mu
    rr = lax.rsqrt(jnp.mean(xc * xc, axis=-1, keepdims=True) + EPS)
    xh = xc * rr
    return xh * g, xh, rr


def _conv_fwd(z, tail, cw_ref):
    ext = jnp.concatenate([tail, z], axis=0)
    zs1 = _shift_down(ext, 1)[HALO:]
    zs2 = _shift_down(ext, 2)[HALO:]
    y = cw_ref[2:3, :] * z + cw_ref[1:2, :] * zs1 + cw_ref[0:1, :] * zs2
    return y, zs1, zs2


def _pool_cnt(shape, blk_in_seq):
    rows = lax.broadcasted_iota(jnp.int32, shape, 0)
    grp = lax.broadcasted_iota(jnp.int32, shape, 1) // POOL_GD
    win = jnp.where(grp == 0, 2, jnp.where(grp == 1, 4, jnp.where(grp == 2, 8, 16)))
    tpos = blk_in_seq * shape[0] + rows + 1
    return jnp.minimum(tpos, win).astype(F32), grp


def _pool_select(grp, s2, s4, s8, s16):
    return jnp.where(grp == 0, s2, jnp.where(grp == 1, s4, jnp.where(grp == 2, s8, s16)))


def _pool_fwd(z, tail, blk_in_seq):
    ext = jnp.concatenate([tail, z], axis=0)
    s2 = ext + _shift_down(ext, 1)
    s4 = s2 + _shift_down(s2, 2)
    s8 = s4 + _shift_down(s4, 4)
    s16 = s8 + _shift_down(s8, 8)
    cnt, grp = _pool_cnt(z.shape, blk_in_seq)
    sums = _pool_select(grp, s2[HALO:], s4[HALO:], s8[HALO:], s16[HALO:])
    return sums / cnt - z, cnt, grp


def _pool_bwd(dpooled, dpm, head, grp):
    n = dpm.shape[0]
    ext = jnp.concatenate([dpm, head], axis=0)
    u2 = ext + _shift_up(ext, 1)
    u4 = u2 + _shift_up(u2, 2)
    u8 = u4 + _shift_up(u4, 4)
    u16 = u8 + _shift_up(u8, 8)
    return _pool_select(grp, u2[:n], u4[:n], u8[:n], u16[:n]) - dpooled


def _split_bf16(a):
    hi = a.astype(BF16)
    return hi, (a - hi.astype(F32)).astype(BF16)


def _lane_sums(a):
    hi, lo = _split_bf16(a)
    ones = jnp.ones((a.shape[1], a.shape[1]), BF16)
    return _dot(hi, ones, 1, 0) + _dot(lo, ones, 1, 0)


def _swap_halves(y1):
    src = lax.broadcasted_iota(jnp.int32, (128, 128), 0)
    dst = lax.broadcasted_iota(jnp.int32, (128, 128), 1)
    perm = jnp.where(((dst < 32) & (src == dst + 32)) | ((dst >= 32) & (dst < QK_ROPE) & (src == dst - 32)), 1.0, 0.0)
    perm = perm.astype(BF16)
    hi, lo = _split_bf16(y1)
    return _dot(hi, perm, 1, 0) + _dot(lo, perm, 1, 0)


def _rope(y1, c, s):
    return y1 * c + _swap_halves(y1) * s


def _rope_bwd(d1, c, s):
    return d1 * c + _swap_halves(d1 * s)


def _qk_prep(x, g, c, s):
    r = lax.rsqrt(_lane_sums(x * x) * (1.0 / QK_DIM) + EPS)
    y = x * r * g
    return jnp.concatenate([y[:, :128], _rope(y[:, 128:], c, s)], axis=1), r


def _qk_prep_bwd(dout, x, r, g, c, s):
    dy = jnp.concatenate([dout[:, :128], _rope_bwd(dout[:, 128:], c, s)], axis=1)
    xh = x * r
    dxh = dy * g
    dx = r * (dxh - xh * (_lane_sums(dxh * xh) * (1.0 / QK_DIM)))
    return dx, jnp.sum(dy * xh, axis=0, keepdims=True)


def _place():
    return lax.axis_index("x"), lax.axis_index("y"), lax.axis_index("c")


def _gather_comm(arrs):
    n = len(arrs)

    def plan(ins, outs, sems):
        send_sems, recv_sems, local_sems = sems
        x, y, c = _place()
        me, sibling = (x, y, c), (x, y, 1 - c)
        chips = [(1 - x, y), (x, 1 - y), (1 - x, 1 - y)]

        def slot(a, px, py, pc):
            return outs[a].at[4 * px + 2 * py + pc]

        def copy(a, k, block, to, src=None):
            return pltpu.make_async_remote_copy(
                src_ref=slot(a, *block) if src is None else src, dst_ref=slot(a, *block),
                send_sem=send_sems.at[a, k], recv_sem=recv_sems.at[a, k], device_id=to, device_id_type=MESH)

        def own():
            mine = [pltpu.make_async_copy(ins[a], slot(a, *me), local_sems.at[a]) for a in range(n)]
            first = []
            for a in range(n):
                first.append(copy(a, 0, me, sibling, src=ins[a]))
                first += [copy(a, 1 + j, me, (*chip, c), src=ins[a]) for j, chip in enumerate(chips)]
            return mine, first

        return c, me, sibling, chips, copy, own

    def start(ins, outs, sems):
        mine, first = plan(ins, outs, sems)[-1]()
        for cp in mine + first:
            cp.start()

    def mid(ins, outs, sems):
        c, me, sibling, chips, copy, _ = plan(ins, outs, sems)
        for j, chip in enumerate(chips):
            for a in range(n):
                copy(a, 1 + j, (*chip, c), me).wait_recv()
                copy(a, 4 + j, (*chip, c), sibling).start()

    def wait(ins, outs, sems):
        c, me, sibling, chips, copy, own = plan(ins, outs, sems)
        mine, first = own()
        passed = [copy(a, 4 + j, (*chip, c), sibling) for j, chip in enumerate(chips) for a in range(n)]
        for a in range(n):
            copy(a, 0, sibling, me).wait_recv()
            for j, chip in enumerate(chips):
                copy(a, 4 + j, (*chip, 1 - c), me).wait_recv()
        for cp in first + passed:
            cp.wait_send()
        for cp in mine:
            cp.wait()

    return _Comm(
        list(arrs), [jax.ShapeDtypeStruct((N_DEV,) + a.shape, a.dtype) for a in arrs],
        [pltpu.SemaphoreType.DMA((n, 7)), pltpu.SemaphoreType.DMA((n, 7)), pltpu.SemaphoreType.DMA((n,))],
        start, wait, mid)


def _sum_gathered(g):
    rows = g.shape[1]

    def body(g_ref, sum_ref):
        total = g_ref[0]
        for d in range(1, N_DEV):
            total = total + g_ref[d]
        sum_ref[...] = total

    return pl.pallas_call(
        body, name="sum_gathered_small", out_shape=jax.ShapeDtypeStruct((rows, 128), F32), grid=(1,),
        in_specs=[pl.BlockSpec((N_DEV, rows, 128), lambda i: (0, 0, 0))],
        out_specs=pl.BlockSpec((rows, 128), lambda i: (0, 0)), compiler_params=_cparams(1),
    )(g)


def _sum_rows(rows):
    return rows if rows <= 512 else rows // 2


def _pair_exchange_comm(gp):
    _, rows, cols = gp.shape

    def copies(ins, outs, sems):
        send_sems, recv_sems = sems
        x, y, c = _place()
        return [pltpu.make_async_remote_copy(
            src_ref=ins[0].at[2 * j + (1 - c)], dst_ref=outs[0].at[j], send_sem=send_sems.at[j],
            recv_sem=recv_sems.at[j], device_id=(x, y, 1 - c), device_id_type=MESH) for j in range(4)]

    def start(ins, outs, sems):
        for cp in copies(ins, outs, sems):
            cp.start()

    def wait(ins, outs, sems):
        for cp in copies(ins, outs, sems):
            cp.wait()

    return _Comm([gp], [jax.ShapeDtypeStruct((4, rows, cols), gp.dtype)],
                 [pltpu.SemaphoreType.DMA((4,)), pltpu.SemaphoreType.DMA((4,))], start, wait)


def _rs_pair_sum(gp, got, c_arr, name):
    _, rows, cols = got.shape
    rb = _sum_rows(rows)
    gp4 = gp.reshape(4, 2, rows, cols)

    def body(c_ref, a_ref, b_ref, o_ref):
        o_ref[0] = (a_ref[0, 0].astype(F32) + b_ref[0].astype(F32)).astype(o_ref.dtype)

    return pl.pallas_call(
        body, name=name, out_shape=jax.ShapeDtypeStruct((4, rows, cols), gp.dtype),
        grid_spec=pltpu.PrefetchScalarGridSpec(
            num_scalar_prefetch=1, grid=(4, rows // rb),
            in_specs=[pl.BlockSpec((1, 1, rb, cols), lambda j, r, cr: (j, cr[0], r, 0)),
                      pl.BlockSpec((1, rb, cols), lambda j, r, cr: (j, r, 0))],
            out_specs=pl.BlockSpec((1, rb, cols), lambda j, r, cr: (j, r, 0))),
        compiler_params=_cparams(2),
    )(c_arr, gp4, got)


def _chip_exchange_comm(pb):
    _, rows, cols = pb.shape

    def copies(ins, outs, sems):
        send_sems, recv_sems = sems
        x, y, c = _place()
        chips = [(1 - x, y), (x, 1 - y), (1 - x, 1 - y)]
        return [pltpu.make_async_remote_copy(
            src_ref=ins[0].at[2 * px + py], dst_ref=outs[0].at[k], send_sem=send_sems.at[k],
            recv_sem=recv_sems.at[k], device_id=(px, py, c), device_id_type=MESH)
            for k, (px, py) in enumerate(chips)]

    def start(ins, outs, sems):
        for cp in copies(ins, outs, sems):
            cp.start()

    def wait(ins, outs, sems):
        for cp in copies(ins, outs, sems):
            cp.wait()

    return _Comm([pb], [jax.ShapeDtypeStruct((3, rows, cols), pb.dtype)],
                 [pltpu.SemaphoreType.DMA((3,)), pltpu.SemaphoreType.DMA((3,))], start, wait)


def _rs_final_sum(pb, got, chip_arr, name):
    _, rows, cols = got.shape
    rb = _sum_rows(rows)

    def body(j_ref, a_ref, b_ref, o_ref):
        o_ref[...] = ((a_ref[0].astype(F32) + b_ref[0].astype(F32)) + b_ref[1].astype(F32)) + b_ref[2].astype(F32)

    return pl.pallas_call(
        body, name=name, out_shape=jax.ShapeDtypeStruct((rows, cols), F32),
        grid_spec=pltpu.PrefetchScalarGridSpec(
            num_scalar_prefetch=1, grid=(rows // rb,),
            in_specs=[pl.BlockSpec((1, rb, cols), lambda r, jr: (jr[0], r, 0)),
                      pl.BlockSpec((3, rb, cols), lambda r, jr: (0, r, 0))],
            out_specs=pl.BlockSpec((rb, cols), lambda r, jr: (r, 0))),
        compiler_params=_cparams(1),
    )(chip_arr, pb, got)


def _rope_tables(pos_col, inv_freq, comm=None):
    t = pos_col.shape[0]

    def body(p_ref, f_ref, c_ref, s_ref):
        ang = p_ref[...].astype(F32) * f_ref[...]
        lane = lax.broadcasted_iota(jnp.int32, ang.shape, 1)
        c_ref[...] = jnp.where(lane < QK_ROPE, jnp.cos(ang), 0.0)
        s = jnp.sin(ang)
        s_ref[...] = jnp.where(lane < 32, -s, jnp.where(lane < QK_ROPE, s, 0.0))

    spec = pl.BlockSpec((TB, 128), lambda i: (i, 0))
    return _call(
        body, "rope_tables", (t // TB,), [pl.BlockSpec((TB, 1), lambda i: (i, 0)), _const_spec((1, 128))],
        [spec] * 2, [jax.ShapeDtypeStruct((t, 128), F32)] * 2, (pos_col, inv_freq), (), comm)


def _sgu_conv_fwd(proj, tail, lng_ref, ws_ref, bst_ref, cw_ref):
    gu = _gelu(proj[:, 0:SG_W])
    gv = _gelu(proj[:, SG_W:2 * SG_W])
    bg = proj[:, 1024:1536]
    z = proj[:, 1536:2048] * proj[:, 2048:2560]
    heads = []
    for h in range(SG_HEADS):
        sl = slice(h * SG_HD, (h + 1) * SG_HD)
        vn, _, _ = _ln_head(gv[:, sl], lng_ref[:, sl])
        vnb = vn.astype(BF16)
        wm = _tril_bf16(ws_ref[h])
        bcol = bst_ref[:, h:h + 1]
        mixed = jnp.concatenate(
            [_dot(wm, vnb[k * SG_CHUNK:(k + 1) * SG_CHUNK], 1, 0) + bcol for k in range(TB // SG_CHUNK)], axis=0)
        heads.append(gu[:, sl] * mixed)
    a_out = jnp.concatenate(heads, axis=1)
    y, _, _ = _conv_fwd(z, tail, cw_ref)
    return a_out, bg * y, z


def _even_fwd(x, wg, gamma, lng, ws, bst, cw, seq, comm=None):
    t = x.shape[0]
    nbs = seq // TB

    def body(x_ref, gam_ref, win_ref, wout_ref, lng_ref, ws_ref, bst_ref, cw_ref, x1_ref, proj_ref, tail_ref):
        i = pl.program_id(0)
        xv = x_ref[...]
        h, _ = _rms(xv, gam_ref[...])
        proj = _dot(h.astype(BF16), win_ref[...].reshape(EVEN_IN, D), 1, 1)
        proj_ref[...] = proj.astype(BF16)
        tail = jnp.where(i % nbs == 0, 0.0, tail_ref[...])
        a_out, b_out, z = _sgu_conv_fwd(proj, tail, lng_ref, ws_ref, bst_ref, cw_ref)
        tail_ref[...] = z[TB - HALO:, :]
        x1_ref[...] = (xv + _dot(a_out.astype(BF16), wout_ref[0:4].reshape(512, D), 1, 0)
                       + _dot(b_out.astype(BF16), wout_ref[4:8].reshape(512, D), 1, 0))

    row = pl.BlockSpec((TB, D), lambda i: (i, 0))
    return _call(
        body, "even_fwd", (t // TB,),
        [row, _const_spec((1, D)), _wspec(N_EIN, OFF_EIN), _wspec(N_SQ, OFF_EOUT), _const_spec((1, SG_W)),
         _const_spec((SG_HEADS, 128, 128)), _const_spec((128, 128)), _const_spec((8, SC_W))],
        [row, pl.BlockSpec((TB, EVEN_IN), lambda i: (i, 0))],
        [jax.ShapeDtypeStruct((t, D), F32), jax.ShapeDtypeStruct((t, EVEN_IN), BF16)],
        (x, gamma, wg, wg, lng, ws, bst, cw), [pltpu.VMEM((HALO, SC_W), F32)], comm)


def _even_bwd(x, proj, dx1, wg, gamma, lng, ws, bst, cw, seq, comm=None):
    t = x.shape[0]
    nb, nbs = t // TB, seq // TB

    def body(x_ref, proj_ref, ptail_ref, dx1_ref, gam_ref, win_ref, wout_ref, lng_ref, ws_ref, bst_ref, cw_ref,
             dx0_ref, dproj_ref, mix_ref, h_ref, dgam_ref, dws_ref, dbc_ref, dlng_ref, dcw_ref, head_ref):
        i = pl.program_id(0)
        blk = nb - 1 - i

        @pl.when(i == 0)
        def _():
            dgam_ref[...] = jnp.zeros_like(dgam_ref)
            dws_ref[...] = jnp.zeros_like(dws_ref)
            dbc_ref[...] = jnp.zeros_like(dbc_ref)
            dlng_ref[...] = jnp.zeros_like(dlng_ref)
            dcw_ref[...] = jnp.zeros_like(dcw_ref)

        xv = x_ref[...]
        gam = gam_ref[...]
        h, r = _rms(xv, gam)
        h_ref[...] = h.astype(BF16)
        dx1 = dx1_ref[...]
        dmix = _dot(dx1.astype(BF16), wout_ref[...].reshape(D, D), 1, 1)
        da, db = dmix[:, :SG_W], dmix[:, SG_W:]
        proj = proj_ref[...].astype(F32)
        u, v = proj[:, 0:SG_W], proj[:, SG_W:2 * SG_W]
        bg, cg, hv = proj[:, 1024:1536], proj[:, 1536:2048], proj[:, 2048:2560]
        gu, gv = _gelu(u), _gelu(v)

        a_heads, dgv_heads = [], []
        for hd in range(SG_HEADS):
            sl = slice(hd * SG_HD, (hd + 1) * SG_HD)
            g_h = lng_ref[:, sl]
            vn, xh, rr = _ln_head(gv[:, sl], g_h)
            vnb = vn.astype(BF16)
            wm = _tril_bf16(ws_ref[hd])
            bcol = bst_ref[:, hd:hd + 1]
            mixed_c, dvn_c = [], []
            dw_acc = jnp.zeros((128, 128), F32)
            db_acc = jnp.zeros((128, 1), F32)
            for k in range(TB // SG_CHUNK):
                rs = slice(k * SG_CHUNK, (k + 1) * SG_CHUNK)
                mixed = _dot(wm, vnb[rs], 1, 0) + bcol
                dmixed = da[rs, sl] * gu[rs, sl]
                dmb = dmixed.astype(BF16)
                dvn_c.append(_dot(wm, dmb, 0, 0))
                dw_acc = dw_acc + _dot(dmb, vnb[rs], 1, 1)
                db_acc = db_acc + jnp.sum(dmixed, axis=1, keepdims=True)
                mixed_c.append(mixed)
            mixed_h = jnp.concatenate(mixed_c, axis=0)
            dvn = jnp.concatenate(dvn_c, axis=0)
            r_i = lax.broadcasted_iota(jnp.int32, (128, 128), 0)
            c_i = lax.broadcasted_iota(jnp.int32, (128, 128), 1)
            dws_ref[hd] += jnp.where(r_i >= c_i, dw_acc, 0.0)
            dbc_ref[:, hd:hd + 1] += db_acc
            dlng_ref[:, sl] += jnp.sum(dvn * xh, axis=0, keepdims=True)
            dxh = dvn * g_h
            dgv = rr * (dxh - jnp.mean(dxh, axis=-1, keepdims=True)
                        - xh * jnp.mean(dxh * xh, axis=-1, keepdims=True))
            a_heads.append(gu[:, sl] * mixed_h)
            dproj_ref[:, sl] = (da[:, sl] * mixed_h * _gelu_grad(u[:, sl])).astype(BF16)
            dgv_heads.append(dgv * _gelu_grad(v[:, sl]))
        dproj_ref[:, SG_W:2 * SG_W] = jnp.concatenate(dgv_heads, axis=1).astype(BF16)
        mix_ref[:, :SG_W] = jnp.concatenate(a_heads, axis=1).astype(BF16)

        z = cg * hv
        pt = ptail_ref[...].astype(F32)
        tail = jnp.where(blk % nbs == 0, 0.0, pt[:, 1536:2048] * pt[:, 2048:2560])
        y, zs1, zs2 = _conv_fwd(z, tail, cw_ref)
        mix_ref[:, SG_W:] = (bg * y).astype(BF16)
        dy = db * bg
        head = jnp.where(blk % nbs == nbs - 1, 0.0, head_ref[...])
        ext = jnp.concatenate([dy, head], axis=0)
        dz = (cw_ref[2:3, :] * dy + cw_ref[1:2, :] * _shift_up(ext, 1)[:TB]
              + cw_ref[0:1, :] * _shift_up(ext, 2)[:TB])
        head_ref[...] = dy[:HALO, :]
        dcw_ref[2:3, :] += jnp.sum(dy * z, axis=0, keepdims=True)
        dcw_ref[1:2, :] += jnp.sum(dy * zs1, axis=0, keepdims=True)
        dcw_ref[0:1, :] += jnp.sum(dy * zs2, axis=0, keepdims=True)
        dproj_ref[:, 1024:1536] = (db * y).astype(BF16)
        dproj_ref[:, 1536:2048] = (dz * hv).astype(BF16)
        dproj_ref[:, 2048:2560] = (dz * cg).astype(BF16)

        dh = _dot(dproj_ref[...], win_ref[...].reshape(EVEN_IN, D), 1, 0)
        dxn, dgam = _rms_bwd(xv, r, gam, dh)
        dgam_ref[...] += dgam
        dx0_ref[...] = dx1 + dxn

    def rev(w):
        return pl.BlockSpec((TB, w), lambda i: (nb - 1 - i, 0))

    ptail = pl.BlockSpec((HALO, EVEN_IN), lambda i: (jnp.maximum((nb - 1 - i) * (TB // HALO) - 1, 0), 0))
    return _call(
        body, "even_bwd", (nb,),
        [rev(D), rev(EVEN_IN), ptail, rev(D), _const_spec((1, D)), _wspec(N_EIN, OFF_EIN),
         _wspec(N_SQ, OFF_EOUT), _const_spec((1, SG_W)), _const_spec((SG_HEADS, 128, 128)),
         _const_spec((128, 128)), _const_spec((8, SC_W))],
        [rev(D), rev(EVEN_IN), rev(D), rev(D), _const_spec((1, D)), _const_spec((SG_HEADS, 128, 128)),
         _const_spec((128, 128)), _const_spec((1, SG_W)), _const_spec((8, SC_W))],
        [jax.ShapeDtypeStruct((t, D), F32), jax.ShapeDtypeStruct((t, EVEN_IN), BF16),
         jax.ShapeDtypeStruct((t, D), BF16), jax.ShapeDtypeStruct((t, D), BF16),
         jax.ShapeDtypeStruct((1, D), F32), jax.ShapeDtypeStruct((SG_HEADS, 128, 128), F32),
         jax.ShapeDtypeStruct((128, 128), F32), jax.ShapeDtypeStruct((1, SG_W), F32),
         jax.ShapeDtypeStruct((8, SC_W), F32)],
        (x, proj, proj, dx1, gamma, wg, wg, lng, ws, bst, cw), [pltpu.VMEM((HALO, SC_W), F32)], comm)


def _ffn_fwd(x, w_gu, w_d, gamma, name, comm=None, target=None):
    t = x.shape[0]
    last = target is not None

    def body(*refs):
        x_ref, gam_ref, wg_ref, wu_ref, wd_ref = refs[:5]
        y_ref, g_ref, u_ref = refs[5 + last:8 + last]
        xv = x_ref[...]
        h, _ = _rms(xv, gam_ref[...])
        hb = h.astype(BF16)
        g = _dot(hb, wg_ref[...].reshape(D_FF, D), 1, 1)
        u = _dot(hb, wu_ref[...].reshape(D_FF, D), 1, 1)
        g_ref[...] = g.astype(BF16)
        u_ref[...] = u.astype(BF16)
        act = g * jax.nn.sigmoid(g) * u
        y = xv + _dot(act.astype(BF16), wd_ref[...].reshape(D_FF, D), 1, 0)
        if not last:
            y_ref[...] = y
            return
        loss_ref = refs[9]

        @pl.when(pl.program_id(0) == 0)
        def _():
            loss_ref[...] = jnp.zeros_like(loss_ref)

        err = y - refs[5][...]
        y_ref[...] = err * (1.0 / D)
        sq = jnp.sum(jnp.sum(err * err, axis=-1, keepdims=True), axis=0, keepdims=True)
        loss_ref[...] += (0.5 / D) * sq

    row = pl.BlockSpec((TB, D), lambda i: (i, 0))
    wide = pl.BlockSpec((TB, D_FF), lambda i: (i, 0))
    in_specs = [row, _const_spec((1, D)), _wspec(N_FF, OFF_GATE), _wspec(N_FF, OFF_UP), _wspec(N_FF, 0)]
    out_specs = [row, wide, wide]
    out_shape = [jax.ShapeDtypeStruct((t, D), F32), jax.ShapeDtypeStruct((t, D_FF), BF16),
                 jax.ShapeDtypeStruct((t, D_FF), BF16)]
    args = (x, gamma, w_gu, w_gu, w_d)
    if last:
        in_specs, args = in_specs + [row], args + (target,)
        out_specs, out_shape = out_specs + [_const_spec((8, 128))], out_shape + [jax.ShapeDtypeStruct((8, 128), F32)]
    return _call(body, name, (t // TB,), in_specs, out_specs, out_shape, args, (), comm)


def _ffn_bwd_act(g, u, dy, w_d, name, comm=None):
    t = dy.shape[0]

    def body(g_ref, u_ref, dy_ref, wd_ref, act_ref, dg_ref, du_ref):
        dact = _dot(dy_ref[...].astype(BF16), wd_ref[...].reshape(D_FF, D), 1, 1)
        gv = g_ref[...].astype(F32)
        uv = u_ref[...].astype(F32)
        sg = jax.nn.sigmoid(gv)
        silu = gv * sg
        act_ref[...] = (silu * uv).astype(BF16)
        dg_ref[...] = (dact * uv * (sg * (1.0 + gv * (1.0 - sg)))).astype(BF16)
        du_ref[...] = (dact * silu).astype(BF16)

    row = pl.BlockSpec((TB, D), lambda i: (i, 0))
    wide = pl.BlockSpec((TB, D_FF), lambda i: (i, 0))
    return _call(body, name, (t // TB,), [wide, wide, row, _wspec(N_FF, 0)], [wide, wide, wide],
                 [jax.ShapeDtypeStruct((t, D_FF), BF16)] * 3, (g, u, dy, w_d), (), comm)


def _ffn_bwd_in(x, dg, du, dy, w_gu, gamma, name, comm=None):
    t = x.shape[0]

    def body(x_ref, dg_ref, du_ref, dy_ref, gam_ref, wg_ref, wu_ref, dx_ref, h_ref, dgam_ref):
        @pl.when(pl.program_id(0) == 0)
        def _():
            dgam_ref[...] = jnp.zeros_like(dgam_ref)

        xv = x_ref[...]
        gam = gam_ref[...]
        h, r = _rms(xv, gam)
        h_ref[...] = h.astype(BF16)
        dh = (_dot(dg_ref[...], wg_ref[...].reshape(D_FF, D), 1, 0)
              + _dot(du_ref[...], wu_ref[...].reshape(D_FF, D), 1, 0))
        dxn, dgam = _rms_bwd(xv, r, gam, dh)
        dgam_ref[...] += dgam
        dx_ref[...] = dy_ref[...] + dxn

    row = pl.BlockSpec((TB, D), lambda i: (i, 0))
    wide = pl.BlockSpec((TB, D_FF), lambda i: (i, 0))
    return _call(
        body, name, (t // TB,),
        [row, wide, wide, row, _const_spec((1, D)), _wspec(N_FF, OFF_GATE), _wspec(N_FF, OFF_UP)],
        [row, row, _const_spec((1, D))],
        [jax.ShapeDtypeStruct((t, D), F32), jax.ShapeDtypeStruct((t, D), BF16), jax.ShapeDtypeStruct((1, D), F32)],
        (x, dg, du, dy, gamma, w_gu, w_gu), (), comm)


def _odd_pre_fwd(x, wg, gamma, qbt, kvbt, qa_g, kva_g, pw_bd, pscale, seq):
    t = x.shape[0]
    nbs = seq // TB

    def body(x_ref, gam_ref, win_ref, qb_ref, kvb_ref, qa_ref, kva_ref, pw_ref, ps_ref,
             proj_ref, q_ref, kv_ref, kr_ref, c_ref, tail_ref):
        i = pl.program_id(0)
        h, _ = _rms(x_ref[...], gam_ref[...])
        proj = _dot(h.astype(BF16), win_ref[...].reshape(D, D), 1, 0)
        proj_ref[...] = proj.astype(BF16)
        zp, ql, kvl = proj[:, :POOL_W], proj[:, 256:640], proj[:, 640:896]
        kr_ref[...] = proj[:, 896:1024]
        qn, _ = _rms(ql, qa_ref[...])
        q_ref[...] = _dot(qn.astype(BF16), qb_ref[...], 1, 1).astype(BF16)
        kvn, _ = _rms(kvl, kva_ref[...])
        kv_ref[...] = _dot(kvn.astype(BF16), kvb_ref[...], 1, 1).astype(BF16)
        tail = jnp.where(i % nbs == 0, 0.0, tail_ref[...])
        pooled, _, _ = _pool_fwd(zp, tail, i % nbs)
        tail_ref[...] = zp[TB - HALO:, :]
        c_ref[...] = (_dot(pooled.astype(BF16), pw_ref[...], 1, 0) * ps_ref[...]).astype(BF16)

    def row(w):
        return pl.BlockSpec((TB, w), lambda i: (i, 0))

    return pl.pallas_call(
        body, name="odd_pre_fwd",
        out_shape=[jax.ShapeDtypeStruct((t, D), BF16), jax.ShapeDtypeStruct((t, HEADS * HP), BF16),
                   jax.ShapeDtypeStruct((t, HEADS * HP), BF16), jax.ShapeDtypeStruct((t, 128), F32),
                   jax.ShapeDtypeStruct((t, POOL_W), BF16)],
        grid=(t // TB,),
        in_specs=[row(D), _const_spec((1, D)), _wspec(N_SQ, OFF_OIN), _const_spec((HEADS * HP, Q_LORA)),
                  _const_spec((HEADS * HP, KV_LORA)), _const_spec((1, Q_LORA)), _const_spec((1, KV_LORA)),
                  _const_spec((POOL_W, POOL_W)), _const_spec((1, POOL_W))],
        out_specs=[row(D), row(HEADS * HP), row(HEADS * HP), row(128), row(POOL_W)],
        scratch_shapes=[pltpu.VMEM((HALO, POOL_W), F32)],
        compiler_params=_cparams(1),
    )(x, gamma, wg, qbt, kvbt, qa_g, kva_g, pw_bd, pscale)


def _odd_pre_bwd(x, proj, dx3, dmix, dq, dkv, dkr, wg, gamma, qbt, kvbt, qa_g, kva_g, pw_bd, pscale, seq):
    t = x.shape[0]
    nb, nbs = t // TB, seq // TB

    def body(x_ref, proj_ref, ptail_ref, dx3_ref, dco_ref, dq_ref, dkv_ref, dkr_ref, gam_ref, win_ref, qb_ref,
             kvb_ref, qa_ref, kva_ref, pw_ref, ps_ref,
             dx2_ref, dproj_ref, h_ref, qn_ref, kvn_ref, dgam_ref, dqa_ref, dkva_ref, dpw_ref, dps_ref, head_ref):
        i = pl.program_id(0)
        blk = nb - 1 - i

        @pl.when(i == 0)
        def _():
            dgam_ref[...] = jnp.zeros_like(dgam_ref)
            dqa_ref[...] = jnp.zeros_like(dqa_ref)
            dkva_ref[...] = jnp.zeros_like(dkva_ref)
            dpw_ref[...] = jnp.zeros_like(dpw_ref)
            dps_ref[...] = jnp.zeros_like(dps_ref)

        xv = x_ref[...]
        gam = gam_ref[...]
        h, r = _rms(xv, gam)
        h_ref[...] = h.astype(BF16)
        proj = proj_ref[...].astype(F32)
        zp, ql, kvl = proj[:, :POOL_W], proj[:, 256:640], proj[:, 640:896]

        qa = qa_ref[...]
        qn, rq = _rms(ql, qa)
        qn_ref[...] = qn.astype(BF16)
        dql, dqa = _rms_bwd(ql, rq, qa, _dot(dq_ref[...], qb_ref[...], 1, 0))
        dqa_ref[...] += dqa
        kva = kva_ref[...]
        kvn, rkv = _rms(kvl, kva)
        kvn_ref[...] = kvn.astype(BF16)
        dkvl, dkva = _rms_bwd(kvl, rkv, kva, _dot(dkv_ref[...], kvb_ref[...], 1, 0))
        dkva_ref[...] += dkva

        pt = ptail_ref[...].astype(F32)
        tail = jnp.where(blk % nbs == 0, 0.0, pt[:, :POOL_W])
        pooled, cnt, grp = _pool_fwd(zp, tail, blk % nbs)
        pb = pooled.astype(BF16)
        pw = pw_ref[...]
        dco = dco_ref[...].astype(F32)
        dps_ref[...] += jnp.sum(dco * _dot(pb, pw, 1, 0), axis=0, keepdims=True)
        dpo = (dco * ps_ref[...]).astype(BF16)
        dpw_ref[...] += _dot(pb, dpo, 0, 0)
        dpooled = _dot(dpo, pw, 1, 1)
        dpm = dpooled / cnt
        head = jnp.where(blk % nbs == nbs - 1, 0.0, head_ref[...])
        dz = _pool_bwd(dpooled, dpm, head, grp)
        head_ref[...] = dpm[:HALO, :]

        dproj_ref[:, :POOL_W] = dz.astype(BF16)
        dproj_ref[:, 256:640] = dql.astype(BF16)
        dproj_ref[:, 640:896] = dkvl.astype(BF16)
        dproj_ref[:, 896:1024] = dkr_ref[...].astype(BF16)
        dh = _dot(dproj_ref[...], win_ref[...].reshape(D, D), 1, 1)
        dxn, dgam = _rms_bwd(xv, r, gam, dh)
        dgam_ref[...] += dgam
        dx2_ref[...] = dx3_ref[...] + dxn

    def rev(w):
        return pl.BlockSpec((TB, w), lambda i: (nb - 1 - i, 0))

    ptail = pl.BlockSpec((HALO, D), lambda i: (jnp.maximum((nb - 1 - i) * (TB // HALO) - 1, 0), 0))
    return pl.pallas_call(
        body, name="odd_pre_bwd",
        out_shape=[jax.ShapeDtypeStruct((t, D), F32), jax.ShapeDtypeStruct((t, D), BF16),
                   jax.ShapeDtypeStruct((t, D), BF16), jax.ShapeDtypeStruct((t, Q_LORA), BF16),
                   jax.ShapeDtypeStruct((t, KV_LORA), BF16), jax.ShapeDtypeStruct((1, D), F32),
                   jax.ShapeDtypeStruct((1, Q_LORA), F32), jax.ShapeDtypeStruct((1, KV_LORA), F32),
                   jax.ShapeDtypeStruct((POOL_W, POOL_W), F32), jax.ShapeDtypeStruct((1, POOL_W), F32)],
        grid=(nb,),
        in_specs=[rev(D), rev(D), ptail, rev(D), rev(POOL_W), rev(HEADS * HP), rev(HEADS * HP), rev(128),
                  _const_spec((1, D)), _wspec(N_SQ, OFF_OIN), _const_spec((HEADS * HP, Q_LORA)),
                  _const_spec((HEADS * HP, KV_LORA)), _const_spec((1, Q_LORA)), _const_spec((1, KV_LORA)),
                  _const_spec((POOL_W, POOL_W)), _const_spec((1, POOL_W))],
        out_specs=[rev(D), rev(D), rev(D), rev(Q_LORA), rev(KV_LORA), _const_spec((1, D)), _const_spec((1, Q_LORA)),
                   _const_spec((1, KV_LORA)), _const_spec((POOL_W, POOL_W)), _const_spec((1, POOL_W))],
        scratch_shapes=[pltpu.VMEM((HALO, POOL_W), F32)],
        compiler_params=_cparams(1),
    )(x, proj, proj, dx3, dmix, dq, dkv, dkr, gamma, wg, qbt, kvbt, qa_g, kva_g, pw_bd, pscale)


def _attn_specs(seq):
    head = pl.BlockSpec((seq, HP), lambda b, h: (b, h))
    shared = pl.BlockSpec((seq, 128), lambda b, h: (b, 0))
    gain = pl.BlockSpec((1, HP), lambda b, h: (0, 0))
    return head, shared, gain


def _causal_bias(n):
    rows = lax.broadcasted_iota(jnp.int32, (n, n), 0)
    cols = lax.broadcasted_iota(jnp.int32, (n, n), 1)
    return jnp.where(cols <= rows, 0.0, NEG_INF)


def _attn_fwd(q, kv, kr, cos, sin, gq, gk, seq, comm=None):
    t = q.shape[0]
    qb = min(512, seq)

    def body(q_ref, kv_ref, kr_ref, c_ref, s_ref, gq_ref, gk_ref, o_ref, lse_ref):
        c, s = c_ref[...], s_ref[...]
        qf, _ = _qk_prep(q_ref[...].astype(F32), gq_ref[...], c, s)
        kin = jnp.concatenate([kv_ref[:, :128].astype(F32), kr_ref[...]], axis=1)
        kf, _ = _qk_prep(kin, gk_ref[...], c, s)
        qf, kf = qf.astype(BF16), kf.astype(BF16)
        v1 = jnp.concatenate([kv_ref[:, 128:], jnp.ones((seq, V_DIM), BF16)], axis=1)
        bias = _causal_bias(qb)
        for q0 in range(0, seq, qb):
            q1 = q0 + qb
            qblk = qf[q0:q1]
            s_dg = _dot(qblk, kf[q0:q1], 1, 1) + bias
            m = jnp.max(s_dg, axis=-1, keepdims=True)
            if q0:
                s_off = _dot(qblk, kf[:q0], 1, 1)
                m = jnp.maximum(m, jnp.max(s_off, axis=-1, keepdims=True))
            acc = _dot(jnp.exp(s_dg - m).astype(BF16), v1[q0:q1], 1, 0)
            if q0:
                acc = acc + _dot(jnp.exp(s_off - m).astype(BF16), v1[:q0], 1, 0)
            l = acc[:, V_DIM:]
            o_ref[q0:q1, :] = (acc[:, :V_DIM] / l).astype(BF16)
            lse_ref[q0:q1, :] = m + jnp.log(l)

    head, shared, gain = _attn_specs(seq)
    per_head = pl.BlockSpec((seq, V_DIM), lambda b, h: (b, h))
    return _call(
        body, "attn_fwd", (t // seq, HEADS),
        [head, head, shared, shared, shared, gain, gain], [per_head, per_head],
        [jax.ShapeDtypeStruct((t, HEADS * V_DIM), BF16), jax.ShapeDtypeStruct((t, HEADS * V_DIM), F32)],
        (q, kv, kr, cos, sin, gq, gk), (), comm)


def _attn_bwd(q, kv, kr, cos, sin, gq, gk, dmix, d_out, lse, seq, comm=None):
    t = q.shape[0]
    qb = min(512, seq)

    def body(q_ref, kv_ref, kr_ref, c_ref, s_ref, gq_ref, gk_ref, do_ref, o_ref, lse_ref,
             dq_ref, dkv_ref, dkr_ref, dgq_ref, dgk_ref, dqf_ref, dkf_ref, dv_ref):
        b, hd = pl.program_id(0), pl.program_id(1)

        @pl.when((b == 0) & (hd == 0))
        def _():
            dgq_ref[...] = jnp.zeros_like(dgq_ref)
            dgk_ref[...] = jnp.zeros_like(dgk_ref)

        c, sn = c_ref[...], s_ref[...]
        gq_v, gk_v = gq_ref[...], gk_ref[...]
        qin = q_ref[...].astype(F32)
        kin = jnp.concatenate([kv_ref[:, :128].astype(F32), kr_ref[...]], axis=1)
        qf32, rq = _qk_prep(qin, gq_v, c, sn)
        kf32, rk = _qk_prep(kin, gk_v, c, sn)
        qf, kf = qf32.astype(BF16), kf32.astype(BF16)
        vb = kv_ref[:, 128:]
        dkf_ref[...] = jnp.zeros_like(dkf_ref)
        dv_ref[...] = jnp.zeros_like(dv_ref)
        bias = _causal_bias(qb)
        for q0 in range(0, seq, qb):
            q1 = q0 + qb
            qblk = qf[q0:q1]
            do = do_ref[q0:q1, :]
            lse_col = lse_ref[q0:q1, 0:1]
            d_col = jnp.sum(do.astype(F32) * o_ref[q0:q1, :].astype(F32), axis=-1, keepdims=True)
            dq_acc = None
            for k0, k1, diag in ((q0, q1, True), (0, q0, False)):
                if k1 == k0:
                    continue
                s = _dot(qblk, kf[k0:k1], 1, 1)
                p = jnp.exp((s + bias if diag else s) - lse_col)
                dv_ref[k0:k1, :] += _dot(p.astype(BF16), do, 0, 0)
                ds = (p * (_dot(do, vb[k0:k1], 1, 1) - d_col)).astype(BF16)
                part = _dot(ds, kf[k0:k1], 1, 0)
                dq_acc = part if dq_acc is None else dq_acc + part
                dkf_ref[k0:k1, :] += _dot(ds, qblk, 0, 0)
            dqf_ref[q0:q1, :] = dq_acc
        dqin, dgq = _qk_prep_bwd(dqf_ref[...], qin, rq, gq_v, c, sn)
        dkin, dgk = _qk_prep_bwd(dkf_ref[...], kin, rk, gk_v, c, sn)
        dgq_ref[...] += dgq
        dgk_ref[...] += dgk
        dq_ref[...] = dqin.astype(BF16)
        dkv_ref[:, :128] = dkin[:, :128].astype(BF16)
        dkv_ref[:, 128:] = dv_ref[...].astype(BF16)

        @pl.when(hd == 0)
        def _():
            dkr_ref[...] = dkin[:, 128:]

        @pl.when(hd != 0)
        def _():
            dkr_ref[...] += dkin[:, 128:]

    head, shared, gain = _attn_specs(seq)
    per_head = pl.BlockSpec((seq, V_DIM), lambda b, h: (b, h))
    return _call(
        body, "attn_bwd", (t // seq, HEADS),
        [head, head, shared, shared, shared, gain, gain,
         pl.BlockSpec((seq, V_DIM), lambda b, h: (b, 2 + h)), per_head, per_head],
        [head, head, shared, gain, gain],
        [jax.ShapeDtypeStruct((t, HEADS * HP), BF16), jax.ShapeDtypeStruct((t, HEADS * HP), BF16),
         jax.ShapeDtypeStruct((t, 128), F32), jax.ShapeDtypeStruct((1, HP), F32),
         jax.ShapeDtypeStruct((1, HP), F32)],
        (q, kv, kr, cos, sin, gq, gk, dmix, d_out, lse),
        [pltpu.VMEM((seq, HP), F32), pltpu.VMEM((seq, HP), F32), pltpu.VMEM((seq, V_DIM), F32)], comm)


def _odd_post_fwd(x, c_out, d_out, wg):
    t = x.shape[0]

    def body(x_ref, c_ref, d_ref, w_ref, y_ref):
        y_ref[...] = (x_ref[...] + _dot(c_ref[...], w_ref[0:2].reshape(POOL_W, D), 1, 0)
                      + _dot(d_ref[...], w_ref[2:8].reshape(HEADS * V_DIM, D), 1, 0))

    def row(w):
        return pl.BlockSpec((TB, w), lambda i: (i, 0))

    return pl.pallas_call(
        body, name="odd_post_fwd", out_shape=jax.ShapeDtypeStruct((t, D), F32), grid=(t // TB,),
        in_specs=[row(D), row(POOL_W), row(HEADS * V_DIM), _wspec(N_SQ, OFF_OOUT)], out_specs=row(D),
        compiler_params=_cparams(1),
    )(x, c_out, d_out, wg)


def _odd_post_bwd(dx3, wg, comm=None):
    t = dx3.shape[0]

    def body(d_ref, w_ref, o_ref):
        o_ref[...] = _dot(d_ref[...].astype(BF16), w_ref[...].reshape(D, D), 1, 1).astype(BF16)

    row = pl.BlockSpec((TB, D), lambda i: (i, 0))
    (res,), extra = _call(body, "odd_post_bwd", (t // TB,), [row, _wspec(N_SQ, OFF_OOUT)], [row],
                          [jax.ShapeDtypeStruct((t, D), BF16)], (dx3, wg), (), comm)
    return res, extra


def _tn(a_list, b, tm, name, into=None, comm=None):
    t, n_out = b.shape
    widths = [a.shape[1] for a in a_list]
    tk = min(TK_DW, t)
    m, na, nk = sum(widths), len(a_list), t // tk
    assert na == 1 or tm == m

    def body(*refs):
        a_refs, b_ref, o_ref, acc_ref = refs[:na], refs[na], refs[-2], refs[-1]
        k = pl.program_id(1)

        @pl.when(k == 0)
        def _():
            acc_ref[...] = jnp.zeros_like(acc_ref)

        bb = b_ref[...].astype(BF16)
        m0 = 0
        for a_ref, w in zip(a_refs, widths):
            rows = slice(0, tm) if na == 1 else slice(m0, m0 + w)
            acc_ref[rows, :] += _dot(a_ref[...].astype(BF16), bb, 0, 0)
            m0 += w

        @pl.when(k == nk - 1)
        def _():
            o_ref[...] = acc_ref[...].astype(BF16).reshape(o_ref.shape)

    if na == 1:
        in_specs = [pl.BlockSpec((tk, tm), lambda i, k: (k, i))]
    else:
        in_specs = [pl.BlockSpec((tk, w), lambda i, k: (k, 0)) for w in widths]
    in_specs.append(pl.BlockSpec((tk, n_out), lambda i, k: (k, 0)))
    args = list(a_list) + [b]
    if into is None:
        out_spec = pl.BlockSpec((tm, n_out), lambda i, k: (i, 0))
        out_shape = jax.ShapeDtypeStruct((m, n_out), BF16)
        aliases = {}
    else:
        buf, n, off = into
        assert n_out == D and tm % n == 0 and off % n == 0 and (na == 1 or tm // n == N_DEV)
        idx = off // n
        out_spec = pl.BlockSpec((tm // n, n, D), lambda i, k: (i, idx, 0))
        out_shape = jax.ShapeDtypeStruct(buf.shape, BF16)
        in_specs.append(pl.BlockSpec(memory_space=pl.ANY))
        args.append(buf)
        aliases = {len(args) - 1: 0}
    (res,), extra = _call(body, name, (m // tm, nk), in_specs, [out_spec], [out_shape], args,
                          [pltpu.VMEM((tm, n_out), F32)], comm, aliases)
    return (res, extra) if comm is not None else res


def _adamw(ws, gs, ms, vs, name, nblk=1):
    n = len(ws)
    c1 = 1.0 - B1 ** STEP
    c2 = 1.0 - B2 ** STEP

    def body(*refs):
        for a in range(n):
            w, g, m, v = (refs[k * n + a][...] for k in range(4))
            d_ref, m_ref, v_ref = (refs[(4 + k) * n + a] for k in range(3))
            m_new = B1 * m + (1.0 - B1) * g
            v_new = B2 * v + (1.0 - B2) * (g * g)
            d_ref[...] = -LR * ((m_new / c1) / (jnp.sqrt(v_new / c2) + ADAM_EPS) + WD * w)
            m_ref[...] = m_new
            v_ref[...] = v_new

    grid = (nblk,)
    assert all(w.shape[0] % nblk == 0 and (nblk == 1 or (w.shape[0] // nblk) % 8 == 0) for w in ws)
    specs = [pl.BlockSpec((w.shape[0] // nblk, w.shape[1]), lambda i: (i, 0)) for w in ws]
    outs, _ = _call(body, name, grid, specs * 4, specs * 3, [jax.ShapeDtypeStruct(w.shape, F32) for w in ws] * 3,
                    (*ws, *gs, *ms, *vs))
    return outs[:n], outs[n:2 * n], outs[2 * n:]


def _rows1024(a, rows):
    flat = a.reshape(-1, D)
    return jnp.pad(flat, ((0, rows - flat.shape[0]), (0, 0)))


def _pack_shards(even_w_in, even_w_out, odd_w_in, q_b, kv_b, odd_w_out, ffn_w_gate, ffn_w_up, ffn_w_down):
    mix0 = jnp.concatenate([even_w_in[0].T, jnp.zeros((OFF_EOUT - N_EIN, D), F32), even_w_out[0]], axis=0)
    gu = [jnp.concatenate([ffn_w_gate[layer].T, ffn_w_up[layer].T], axis=0) for layer in range(2)]
    mix1 = jnp.concatenate([jnp.pad(odd_w_in[0], ((0, 0), (0, D - ODD_IN))), odd_w_out[0],
                            _rows1024(q_b[0].T, N_QB), _rows1024(kv_b[0].T, N_KVB),
                            jnp.zeros((R_MIX1 - OFF_KVB - N_KVB, D), F32)], axis=0)
    return [c.astype(BF16) for c in (mix0, gu[0], ffn_w_down[0], mix1, gu[1], ffn_w_down[1])]


def _pad_heads(a):
    k = a.shape[1]
    return jnp.pad(a.reshape(HEADS, QK_DIM, k), ((0, 0), (0, HP - QK_DIM), (0, 0))).reshape(HEADS * HP, k)


def _small_pack(parts):
    flat = []
    for p in parts:
        v = p.reshape(-1)
        flat.append(jnp.pad(v, (0, (-v.shape[0]) % 1024)))
    return jnp.concatenate(flat).reshape(-1, 128)


def _small_unpack(buf, shapes):
    flat = buf.reshape(-1)
    out, off = [], 0
    for s in shapes:
        size = int(np.prod(s))
        out.append(flat[off:off + size].reshape(s))
        off += size + (-size) % 1024
    return out


def _step(x3d, positions, target3d, chunks, tile, c_arr, chip_arr, mix_norm, ffn_norm, sg_ln_g, sg_w_s, sg_b_s,
          pool_w, q_norm, k_norm):
    bsz, seq, _ = x3d.shape
    t = bsz * seq
    x0 = x3d.reshape(t, D)
    target = target3d.reshape(t, D)
    my_mix0, my_gu0, my_d0, my_mix1, my_gu1, my_d1 = chunks

    lane = np.arange(128)
    inv_freq = np.where(lane < QK_ROPE, ROPE_THETA ** (-(2.0 * (lane % 32)) / QK_ROPE), 0.0)
    inv_freq = jnp.asarray(inv_freq.reshape(1, 128), F32)
    (cos, sin), (w_mix0, tiles) = _rope_tables(positions.reshape(t, 1), inv_freq, _gather_comm([my_mix0, tile]))

    conv_w = tiles[:, 0:3, 0:64].transpose(1, 0, 2).reshape(3, SC_W)
    pool_scale = tiles[:, 3, 0:32].reshape(1, POOL_W)
    q_a_norm = tiles[:, 4, 0:48].reshape(1, Q_LORA)
    kv_a_norm = tiles[:, 5, 0:32].reshape(1, KV_LORA)
    ws = sg_w_s[0]
    bst = jnp.pad(sg_b_s[0].T, ((0, 0), (0, 128 - SG_HEADS)))
    cw = jnp.pad(conv_w, ((0, 8 - 3), (0, 0)))
    pw_bd = jax.scipy.linalg.block_diag(*[pool_w[0, g] for g in range(4)]).astype(BF16)
    gq = jnp.pad(q_norm * ATT_SCALE, ((0, 0), (0, HP - QK_DIM)))
    gk = jnp.pad(k_norm, ((0, 0), (0, HP - QK_DIM)))

    (x1, proj_e), (w_gu0, w_d0) = _even_fwd(x0, w_mix0, mix_norm[0:1], sg_ln_g, ws, bst, cw, seq,
                                            _gather_comm([my_gu0, my_d0]))
    (x2, g0, u0), (w_mix1, w_d1) = _ffn_fwd(x1, w_gu0, w_d0, ffn_norm[0:1], "ffn_fwd0",
                                            _gather_comm([my_mix1, my_d1]))
    qbt = _pad_heads(w_mix1[:, OFF_QB:OFF_QB + N_QB_USED, :].reshape(HEADS * QK_DIM, Q_LORA))
    kvbt = w_mix1[:, OFF_KVB:OFF_KVB + N_KVB, :].reshape(HEADS * HP, KV_LORA)
    proj_o, q, kv, kr, c_out = _odd_pre_fwd(x2, w_mix1, mix_norm[1:2], qbt, kvbt, q_a_norm, kv_a_norm, pw_bd,
                                            pool_scale, seq)
    (d_out, lse), (w_gu1,) = _attn_fwd(q, kv, kr, cos, sin, gq, gk, seq, _gather_comm([my_gu1]))
    x3 = _odd_post_fwd(x2, c_out, d_out, w_mix1)
    (dy, g1, u1, loss_tile), _ = _ffn_fwd(x3, w_gu1, w_d1, ffn_norm[1:2], "ffn_fwd1", None, target)

    def chunk(rows, padded=False):
        return jnp.zeros((N_DEV, rows, D), BF16) if padded else lax.empty((N_DEV, rows, D), BF16)

    (act1, dg1, du1), _ = _ffn_bwd_act(g1, u1, dy, w_d1, "ffn_bwd_act1")
    (dx3, h3, dgam_f1), _ = _ffn_bwd_in(x3, dg1, du1, dy, w_gu1, ffn_norm[1:2], "ffn_bwd_in1")
    gp_ffn1 = _tn([dg1], h3, 1408, "dw_gate1", (chunk(R_GU + N_FF), N_FF, OFF_GATE))
    gp_ffn1 = _tn([du1], h3, 1408, "dw_up1", (gp_ffn1, N_FF, OFF_UP))
    gp_ffn1 = _tn([act1], dy, 1408, "dw_down1", (gp_ffn1, N_FF, R_GU))

    dmix_o, (ga_ffn1,) = _odd_post_bwd(dx3, w_mix1, _pair_exchange_comm(gp_ffn1))
    pb_ffn1 = _rs_pair_sum(gp_ffn1, ga_ffn1, c_arr, "rs_pair_sum_ffn1")
    gp_mix1 = _tn([c_out, d_out], dx3, D, "dw_oout", (chunk(R_MIX1, True), N_SQ, OFF_OOUT))
    (dq, dkv, dkr, dgq, dgk), (gb_ffn1,) = _attn_bwd(q, kv, kr, cos, sin, gq, gk, dmix_o, d_out, lse, seq,
                                                    _chip_exchange_comm(pb_ffn1))
    gsh_ffn1 = _rs_final_sum(pb_ffn1, gb_ffn1, chip_arr, "rs_final_sum_ffn1")
    (dx2, dproj_o, h2, qn, kvn, dgam_m1, dqa, dkva, dpw_bd, dps) = _odd_pre_bwd(
        x2, proj_o, dx3, dmix_o, dq, dkv, dkr, w_mix1, mix_norm[1:2], qbt, kvbt, q_a_norm, kv_a_norm, pw_bd,
        pool_scale, seq)
    gp_mix1 = _tn([h2], dproj_o, D, "dw_oin", (gp_mix1, N_SQ, OFF_OIN))
    d_qbt = _tn([dq], qn, HEADS * HP, "dw_qb")
    d_qb_rows = d_qbt.reshape(HEADS, HP, Q_LORA)[:, :QK_DIM].reshape(N_DEV, N_QB_USED, D)
    d_kvb_rows = _tn([dkv], kvn, HEADS * HP, "dw_kvb").reshape(N_DEV, N_KVB, D)
    gp_mix1 = lax.dynamic_update_slice(gp_mix1, d_qb_rows, (0, OFF_QB, 0))
    gp_mix1 = lax.dynamic_update_slice(gp_mix1, d_kvb_rows, (0, OFF_KVB, 0))

    (act0, dg0, du0), (ga_mix1,) = _ffn_bwd_act(g0, u0, dx2, w_d0, "ffn_bwd_act0", _pair_exchange_comm(gp_mix1))
    pb_mix1 = _rs_pair_sum(gp_mix1, ga_mix1, c_arr, "rs_pair_sum_mix1")
    (dx1, h1, dgam_f0), (gb_mix1,) = _ffn_bwd_in(x1, dg0, du0, dx2, w_gu0, ffn_norm[0:1], "ffn_bwd_in0",
                                                 _chip_exchange_comm(pb_mix1))
    gsh_mix1 = _rs_final_sum(pb_mix1, gb_mix1, chip_arr, "rs_final_sum_mix1")
    gp_ffn0a = _tn([dg0], h1, 1408, "dw_gate0", (chunk(R_GU), N_FF, OFF_GATE))
    gp_ffn0a = _tn([du0], h1, 1408, "dw_up0", (gp_ffn0a, N_FF, OFF_UP))
    gp_ffn0b, (ga_ffn0a,) = _tn([act0], dx2, 1408, "dw_down0", (chunk(N_FF), N_FF, 0),
                                _pair_exchange_comm(gp_ffn0a))
    pb_ffn0a = _rs_pair_sum(gp_ffn0a, ga_ffn0a, c_arr, "rs_pair_sum_ffn0a")

    (dx0, dproj_e, mix_e, h0, dgam_m0, dws, dbc, dlng, dcw), (gb_ffn0a, ga_ffn0b) = _even_bwd(
        x0, proj_e, dx1, w_mix0, mix_norm[0:1], sg_ln_g, ws, bst, cw, seq,
        _both(_chip_exchange_comm(pb_ffn0a), _pair_exchange_comm(gp_ffn0b)))
    pb_ffn0b = _rs_pair_sum(gp_ffn0b, ga_ffn0b, c_arr, "rs_pair_sum_ffn0b")
    gsh_ffn0a = _rs_final_sum(pb_ffn0a, gb_ffn0a, chip_arr, "rs_final_sum_ffn0a")

    small = _small_pack([
        jnp.concatenate([dgam_m0, dgam_m1], 0), jnp.concatenate([dgam_f0, dgam_f1], 0), dlng,
        dws[None], dbc[:, :SG_HEADS].T[None], dcw[:3],
        jnp.stack([dpw_bd[g * POOL_GD:(g + 1) * POOL_GD, g * POOL_GD:(g + 1) * POOL_GD] for g in range(4)])[None],
        dps, dqa, dkva, dgq[:, :QK_DIM] * ATT_SCALE, dgk[:, :QK_DIM], loss_tile[0:1, 0:1]])
    gp_mix0, (small_all,) = _tn([mix_e], dx1, D, "dw_eout", (chunk(R_MIX0, True), N_SQ, OFF_EOUT),
                                _gather_comm([small]))
    gp_mix0, (gb_ffn0b,) = _tn([dproj_e], h0, 1280, "dw_ein", (gp_mix0, N_EIN, OFF_EIN),
                               _chip_exchange_comm(pb_ffn0b))
    gsh_ffn0b = _rs_final_sum(pb_ffn0b, gb_ffn0b, chip_arr, "rs_final_sum_ffn0b")
    small_sum = _small_unpack(_sum_gathered(small_all), SMALL_SHAPES)
    return dx0.reshape(bsz, seq, D), (gsh_ffn0a, gsh_ffn0b, gsh_mix1, gsh_ffn1), gp_mix0, small_sum


SMALL_SHAPES = [(2, D), (2, D), (1, SG_W), (1, SG_HEADS, 128, 128), (1, SG_HEADS, 128), (3, SC_W),
                (1, 4, POOL_GD, POOL_GD), (1, POOL_W), (1, Q_LORA), (1, KV_LORA), (1, QK_DIM), (1, QK_DIM), (1, 1)]


def kernel(x, positions, mix_norm, ffn_norm, even_w_in, sg_ln_g, sg_w_s, sg_b_s, sc_conv_w, even_w_out, odd_w_in, pool_w, pool_scale, q_a_norm, q_b, kv_a_norm, kv_b, q_norm, k_norm, odd_w_out, ffn_w_gate, ffn_w_up, ffn_w_down, loss_target, m_mix_norm, m_ffn_norm, m_even_w_in, m_sg_ln_g, m_sg_w_s, m_sg_b_s, m_sc_conv_w, m_even_w_out, m_odd_w_in, m_pool_w, m_pool_scale, m_q_a_norm, m_q_b, m_kv_a_norm, m_kv_b, m_q_norm, m_k_norm, m_odd_w_out, m_ffn_w_gate, m_ffn_w_up, m_ffn_w_down, v_mix_norm, v_ffn_norm, v_even_w_in, v_sg_ln_g, v_sg_w_s, v_sg_b_s, v_sc_conv_w, v_even_w_out, v_odd_w_in, v_pool_w, v_pool_scale, v_q_a_norm, v_q_b, v_kv_a_norm, v_kv_b, v_q_norm, v_k_norm, v_odd_w_out, v_ffn_w_gate, v_ffn_w_up, v_ffn_w_down):
    xi, yi, ci = _place()
    me = 4 * xi + 2 * yi + ci

    chunks = _pack_shards(even_w_in, even_w_out, odd_w_in, q_b, kv_b, odd_w_out, ffn_w_gate, ffn_w_up, ffn_w_down)

    def lane_pad(a):
        return jnp.pad(a, ((0, 0), (0, 128 - a.shape[1])))

    tile = jnp.concatenate([lane_pad(sc_conv_w[0]), lane_pad(pool_scale), lane_pad(q_a_norm), lane_pad(kv_a_norm),
                            jnp.zeros((2, 128), F32)], axis=0)
    c_arr = jnp.reshape(ci, (1,)).astype(jnp.int32)
    chip_arr = jnp.reshape(2 * xi + yi, (1,)).astype(jnp.int32)
    grad_x, (gsh_ffn0a, gsh_ffn0b, gsh_mix1, gsh_ffn1), gp_mix0, tot = _step(
        x, positions, loss_target, chunks, tile, c_arr, chip_arr, mix_norm, ffn_norm, sg_ln_g, sg_w_s, sg_b_s,
        pool_w, q_norm, k_norm)

    (g_mix, g_ffn, g_lng, g_ws, g_bs, g_cw_full, g_pw, g_ps_full, g_qa_full, g_kva_full, g_qn, g_kn, loss) = tot
    g_cw = lax.dynamic_slice_in_dim(g_cw_full, me * 64, 64, axis=1)[None]
    g_ps = lax.dynamic_slice_in_dim(g_ps_full, me * 32, 32, axis=1)
    g_qa = lax.dynamic_slice_in_dim(g_qa_full, me * 48, 48, axis=1)
    g_kva = lax.dynamic_slice_in_dim(g_kva_full, me * 32, 32, axis=1)

    def tr(a):
        return jnp.swapaxes(a, -1, -2)

    g_gate = tr(jnp.stack([gsh_ffn0a[OFF_GATE:OFF_GATE + N_FF], gsh_ffn1[OFF_GATE:OFF_GATE + N_FF]]))
    g_up = tr(jnp.stack([gsh_ffn0a[OFF_UP:OFF_UP + N_FF], gsh_ffn1[OFF_UP:OFF_UP + N_FF]]))
    g_down = jnp.stack([gsh_ffn0b, gsh_ffn1[R_GU:R_GU + N_FF]])
    g_oin = gsh_mix1[OFF_OIN:OFF_OIN + N_SQ, :ODD_IN][None]
    g_oout = gsh_mix1[OFF_OOUT:OFF_OOUT + N_SQ][None]
    g_qb = tr(gsh_mix1[OFF_QB:OFF_QB + N_QB_USED].reshape(1, 144, Q_LORA))
    g_kvb = tr(gsh_mix1[OFF_KVB:OFF_KVB + N_KVB].reshape(1, 192, KV_LORA))
    transposed = ("even_w_in", "odd_w_in", "q_b", "kv_b", "ffn_w_gate", "ffn_w_up")

    names = ("mix_norm", "ffn_norm", "even_w_in", "sg_ln_g", "sg_w_s", "sg_b_s", "sc_conv_w", "even_w_out",
             "odd_w_in", "pool_w", "pool_scale", "q_a_norm", "q_b", "kv_a_norm", "kv_b", "q_norm", "k_norm",
             "odd_w_out", "ffn_w_gate", "ffn_w_up", "ffn_w_down")
    grads = dict(mix_norm=g_mix, ffn_norm=g_ffn, sg_ln_g=g_lng, sg_w_s=g_ws, sg_b_s=g_bs,
                 sc_conv_w=g_cw, odd_w_in=g_oin, pool_w=g_pw, pool_scale=g_ps, q_a_norm=g_qa,
                 q_b=g_qb, kv_a_norm=g_kva, kv_b=g_kvb, q_norm=g_qn, k_norm=g_kn, odd_w_out=g_oout,
                 ffn_w_gate=g_gate, ffn_w_up=g_up, ffn_w_down=g_down)
    weights = dict(mix_norm=mix_norm, ffn_norm=ffn_norm, even_w_in=even_w_in, sg_ln_g=sg_ln_g, sg_w_s=sg_w_s,
                   sg_b_s=sg_b_s, sc_conv_w=sc_conv_w, even_w_out=even_w_out, odd_w_in=odd_w_in, pool_w=pool_w,
                   pool_scale=pool_scale, q_a_norm=q_a_norm, q_b=q_b, kv_a_norm=kv_a_norm, kv_b=kv_b, q_norm=q_norm,
                   k_norm=k_norm, odd_w_out=odd_w_out, ffn_w_gate=ffn_w_gate, ffn_w_up=ffn_w_up,
                   ffn_w_down=ffn_w_down)
    m_in = dict(mix_norm=m_mix_norm, ffn_norm=m_ffn_norm, even_w_in=m_even_w_in, sg_ln_g=m_sg_ln_g, sg_w_s=m_sg_w_s,
                sg_b_s=m_sg_b_s, sc_conv_w=m_sc_conv_w, even_w_out=m_even_w_out, odd_w_in=m_odd_w_in,
                pool_w=m_pool_w, pool_scale=m_pool_scale, q_a_norm=m_q_a_norm, q_b=m_q_b, kv_a_norm=m_kv_a_norm,
                kv_b=m_kv_b, q_norm=m_q_norm, k_norm=m_k_norm, odd_w_out=m_odd_w_out, ffn_w_gate=m_ffn_w_gate,
                ffn_w_up=m_ffn_w_up, ffn_w_down=m_ffn_w_down)
    v_in = dict(mix_norm=v_mix_norm, ffn_norm=v_ffn_norm, even_w_in=v_even_w_in, sg_ln_g=v_sg_ln_g, sg_w_s=v_sg_w_s,
                sg_b_s=v_sg_b_s, sc_conv_w=v_sc_conv_w, even_w_out=v_even_w_out, odd_w_in=v_odd_w_in,
                pool_w=v_pool_w, pool_scale=v_pool_scale, q_a_norm=v_q_a_norm, q_b=v_q_b, kv_a_norm=v_kv_a_norm,
                kv_b=v_kv_b, q_norm=v_q_norm, k_norm=v_k_norm, odd_w_out=v_odd_w_out, ffn_w_gate=v_ffn_w_gate,
                ffn_w_up=v_ffn_w_up, ffn_w_down=v_ffn_w_down)
    delta, new_m, new_v = {}, {}, {}

    def as2d(k, a):
        a = tr(a) if k in transposed else a
        return a.reshape(-1, a.shape[-1])

    def back(k, a):
        shape = weights[k].shape
        return tr(a.reshape(shape[:-2] + (shape[-1], shape[-2]))) if k in transposed else a.reshape(shape)

    def update(group, name, nblk=1):
        outs = _adamw([as2d(k, weights[k]) for k in group], [as2d(k, grads[k]) for k in group],
                      [as2d(k, m_in[k]) for k in group], [as2d(k, v_in[k]) for k in group], name, nblk)
        for i, k in enumerate(group):
            delta[k], new_m[k], new_v[k] = (back(k, o[i]) for o in outs)

    (ga_mix0,) = _comm_alone(_pair_exchange_comm(gp_mix0), "rs_pair_exchange_mix0")
    pb_mix0 = _rs_pair_sum(gp_mix0, ga_mix0, c_arr, "rs_pair_sum_mix0")
    (gb_mix0,) = _comm_alone(_chip_exchange_comm(pb_mix0), "rs_chip_exchange_mix0")
    gsh_mix0 = _rs_final_sum(pb_mix0, gb_mix0, chip_arr, "rs_final_sum_mix0")
    grads["even_w_in"] = tr(gsh_mix0[OFF_EIN:OFF_EIN + N_EIN][None])
    grads["even_w_out"] = gsh_mix0[OFF_EOUT:OFF_EOUT + N_SQ][None]

    update(["ffn_w_gate", "ffn_w_up", "ffn_w_down"], "adamw_ffn", 4)
    update(["even_w_in", "even_w_out", "odd_w_in", "odd_w_out"], "adamw_mix", 2)
    update([k for k in names if k not in delta], "adamw_small")

    return (loss.reshape(()), grad_x, *[grads[k] for k in names], *[delta[k] for k in names],
            *[new_m[k] for k in names], *[new_v[k] for k in names])
```

```python
import functools

import numpy as np
import jax
import jax.numpy as jnp
from jax import lax
from jax.experimental import pallas as pl
from jax.experimental.pallas import tpu as pltpu

F32 = jnp.float32
BF16 = jnp.bfloat16
MESH = pl.DeviceIdType.MESH

D = 1024
EPS = 1e-6
NEG_INF = -1e30
SG_HEADS, SG_HD, SG_W, SG_CHUNK = 4, 128, 512, 128
SC_W = 512
EVEN_IN = 2560
POOL_W = 256
POOL_GD = 64
Q_LORA, KV_LORA, QK_ROPE, QK_NOPE, V_DIM = 384, 256, 64, 128, 128
QK_DIM = QK_NOPE + QK_ROPE
HEADS = 6
HP = 256
ODD_IN = 960
D_FF = 2816
ROPE_THETA = 10000.0
ATT_SCALE = QK_DIM ** -0.5
LR, B1, B2, ADAM_EPS, WD, STEP = 0.001, 0.9, 0.999, 1e-08, 0.01, 10

N_DEV = 8
TB = 512
TB_FFN_BWD = 256
TK_DW = 1024
HALO = 16
VMEM_LIMIT = 56 * 1024 * 1024

N_EIN, N_FF, N_SQ = 320, 352, 128
OFF_EIN, OFF_EOUT, R_MIX0 = 0, 384, 512
OFF_GATE, OFF_UP, R_GU = 0, 352, 704
OFF_OIN, OFF_OOUT, OFF_QB, OFF_KVB, R_MIX1 = 0, 128, 256, 320, 384
N_QB, N_QB_USED, N_KVB = 64, 54, 48

INV_SQRT2 = 0.7071067811865476
INV_SQRT_2PI = 0.3989422804014327


def _dot(a, b, ca, cb):
    return lax.dot_general(a, b, (((ca,), (cb,)), ((), ())), preferred_element_type=F32)


def _cparams(n_axes=1):
    return pltpu.CompilerParams(dimension_semantics=("arbitrary",) * n_axes, vmem_limit_bytes=VMEM_LIMIT)


def _wspec(n, off, arity=1):
    assert off % n == 0
    idx = off // n
    if arity == 1:
        return pl.BlockSpec((N_DEV, n, D), lambda i: (0, idx, 0), pipeline_mode=pl.Buffered(1))
    return pl.BlockSpec((N_DEV, n, D), lambda i, j: (0, idx, 0), pipeline_mode=pl.Buffered(1))


def _const_spec(shape):
    zeros = (0,) * len(shape)
    return pl.BlockSpec(shape, lambda *_: zeros)


class _Comm:
    def __init__(self, ins, out_shapes, sems, start, wait, mid=None):
        self.ins, self.out_shapes, self.sems, self.start, self.wait, self.mid = ins, out_shapes, sems, start, wait, mid


def _both(c1, c2):
    def split(f1, f2):
        def run(ins, outs, sems):
            f1(ins[:len(c1.ins)], outs[:len(c1.out_shapes)], sems[:len(c1.sems)])
            f2(ins[len(c1.ins):], outs[len(c1.out_shapes):], sems[len(c1.sems):])
        return run

    assert c1.mid is None and c2.mid is None
    return _Comm(c1.ins + c2.ins, c1.out_shapes + c2.out_shapes, c1.sems + c2.sems,
                 split(c1.start, c2.start), split(c1.wait, c2.wait))


def _call(body, name, grid, in_specs, out_specs, out_shape, args, scratch_shapes=(), comm=None, aliases=None):
    n_axes = len(grid)
    aliases = aliases or {}
    if comm is None:
        res = pl.pallas_call(
            body, name=name, grid=grid, in_specs=list(in_specs), out_specs=list(out_specs),
            out_shape=list(out_shape), scratch_shapes=list(scratch_shapes), input_output_aliases=aliases,
            compiler_params=_cparams(n_axes))(*args)
        return list(res), []
    ni, no, ns = len(in_specs), len(out_specs), len(scratch_shapes)
    ci, co = len(comm.ins), len(comm.out_shapes)
    n_steps = int(np.prod(grid))

    def carrier(*refs):
        ins, cin = refs[:ni], refs[ni:ni + ci]
        outs, cout = refs[ni + ci:ni + ci + no], refs[ni + ci + no:ni + ci + no + co]
        scr, sems = refs[ni + ci + no + co:ni + ci + no + co + ns], refs[ni + ci + no + co + ns:]
        step = 0
        for a in range(n_axes):
            step = step * grid[a] + pl.program_id(a)

        @pl.when(step == 0)
        def _():
            comm.start(cin, cout, sems)

        body(*ins, *outs, *scr)

        if comm.mid is not None and n_steps >= 4:
            @pl.when(step == (3 * n_steps) // 4)
            def _():
                comm.mid(cin, cout, sems)

        @pl.when(step == n_steps - 1)
        def _():
            if comm.mid is not None and n_steps < 4:
                comm.mid(cin, cout, sems)
            comm.wait(cin, cout, sems)

    any_spec = pl.BlockSpec(memory_space=pl.ANY)
    res = pl.pallas_call(
        carrier, name=name, grid=grid, in_specs=list(in_specs) + [any_spec] * ci,
        out_specs=list(out_specs) + [any_spec] * co, out_shape=list(out_shape) + list(comm.out_shapes),
        scratch_shapes=list(scratch_shapes) + list(comm.sems), input_output_aliases=aliases,
        compiler_params=_cparams(n_axes))(*args, *comm.ins)
    return list(res[:no]), list(res[no:])


def _comm_alone(comm, name):
    ci, co = len(comm.ins), len(comm.out_shapes)

    def body(*refs):
        cin, cout, sems = refs[:ci], refs[ci:ci + co], refs[ci + co:]
        comm.start(cin, cout, sems)
        if comm.mid is not None:
            comm.mid(cin, cout, sems)
        comm.wait(cin, cout, sems)

    any_spec = pl.BlockSpec(memory_space=pl.ANY)
    res = pl.pallas_call(
        body, name=name, out_shape=list(comm.out_shapes), in_specs=[any_spec] * ci, out_specs=[any_spec] * co,
        scratch_shapes=list(comm.sems))(*comm.ins)
    return list(res)


def _rms(x, g):
    r = lax.rsqrt(jnp.mean(x * x, axis=-1, keepdims=True) + EPS)
    return x * r * g, r


def _rms_bwd(x, r, g, dy):
    xh = x * r
    dxh = dy * g
    dx = r * (dxh - xh * jnp.mean(dxh * xh, axis=-1, keepdims=True))
    dg = jnp.sum(dy * xh, axis=0, keepdims=True)
    return dx, dg


def _gelu(x):
    return 0.5 * x * (1.0 + lax.erf(x * INV_SQRT2))


def _gelu_grad(x):
    return 0.5 * (1.0 + lax.erf(x * INV_SQRT2)) + x * jnp.exp(-0.5 * x * x) * INV_SQRT_2PI


def _shift_down(a, k):
    rows = lax.broadcasted_iota(jnp.int32, a.shape, 0)
    return jnp.where(rows >= k, pltpu.roll(a, k, 0), 0.0)


def _shift_up(a, k):
    n = a.shape[0]
    rows = lax.broadcasted_iota(jnp.int32, a.shape, 0)
    return jnp.where(rows < n - k, pltpu.roll(a, n - k, 0), 0.0)


def _tril_bf16(w):
    r = lax.broadcasted_iota(jnp.int32, w.shape, 0)
    c = lax.broadcasted_iota(jnp.int32, w.shape, 1)
    return jnp.where(r >= c, w, 0.0).astype(BF16)


def _ln_head(vh, g):
    mu = jnp.mean(vh, axis=-1, keepdims=True)
    xc = vh - mu
    rr = lax.rsqrt(jnp.mean(xc * xc, axis=-1, keepdims=True) + EPS)
    xh = xc * rr
    return xh * g, xh, rr


def _conv_fwd(z, tail, cw_ref):
    ext = jnp.concatenate([tail, z], axis=0)
    zs1 = _shift_down(ext, 1)[HALO:]
    zs2 = _shift_down(ext, 2)[HALO:]
    y = cw_ref[2:3, :] * z + cw_ref[1:2, :] * zs1 + cw_ref[0:1, :] * zs2
    return y, zs1, zs2


def _pool_cnt(shape, blk_in_seq):
    rows = lax.broadcasted_iota(jnp.int32, shape, 0)
    grp = lax.broadcasted_iota(jnp.int32, shape, 1) // POOL_GD
    win = jnp.where(grp == 0, 2, jnp.where(grp == 1, 4, jnp.where(grp == 2, 8, 16)))
    tpos = blk_in_seq * shape[0] + rows + 1
    return jnp.minimum(tpos, win).astype(F32), grp


def _pool_select(grp, s2, s4, s8, s16):
    return jnp.where(grp == 0, s2, jnp.where(grp == 1, s4, jnp.where(grp == 2, s8, s16)))


def _pool_fwd(z, tail, blk_in_seq):
    ext = jnp.concatenate([tail, z], axis=0)
    s2 = ext + _shift_down(ext, 1)
    s4 = s2 + _shift_down(s2, 2)
    s8 = s4 + _shift_down(s4, 4)
    s16 = s8 + _shift_down(s8, 8)
    cnt, grp = _pool_cnt(z.shape, blk_in_seq)
    sums = _pool_select(grp, s2[HALO:], s4[HALO:], s8[HALO:], s16[HALO:])
    return sums / cnt - z, cnt, grp


def _pool_bwd(dpooled, dpm, head, grp):
    n = dpm.shape[0]
    ext = jnp.concatenate([dpm, head], axis=0)
    u2 = ext + _shift_up(ext, 1)
    u4 = u2 + _shift_up(u2, 2)
    u8 = u4 + _shift_up(u4, 4)
    u16 = u8 + _shift_up(u8, 8)
    return _pool_select(grp, u2[:n], u4[:n], u8[:n], u16[:n]) - dpooled


def _split_bf16(a):
    hi = a.astype(BF16)
    return hi, (a - hi.astype(F32)).astype(BF16)


def _lane_sums(a):
    return _dot(a.astype(BF16), jnp.ones((a.shape[1], a.shape[1]), BF16), 1, 0)


def _swap_halves(y1):
    src = lax.broadcasted_iota(jnp.int32, (128, 128), 0)
    dst = lax.broadcasted_iota(jnp.int32, (128, 128), 1)
    perm = jnp.where(((dst < 32) & (src == dst + 32)) | ((dst >= 32) & (dst < QK_ROPE) & (src == dst - 32)), 1.0, 0.0)
    perm = perm.astype(BF16)
    hi, lo = _split_bf16(y1)
    return _dot(hi, perm, 1, 0) + _dot(lo, perm, 1, 0)


def _rope(y1, c, s):
    return y1 * c + _swap_halves(y1) * s


def _rope_bwd(d1, c, s):
    return d1 * c + _swap_halves(d1 * s)


def _qk_prep(x, g, c, s):
    r = lax.rsqrt(_lane_sums(x * x) * (1.0 / QK_DIM) + EPS)
    y = x * r * g
    return jnp.concatenate([y[:, :128], _rope(y[:, 128:], c, s)], axis=1), r


def _qk_prep_bwd(dout, x, r, g, c, s):
    dy = jnp.concatenate([dout[:, :128], _rope_bwd(dout[:, 128:], c, s)], axis=1)
    xh = x * r
    dxh = dy * g
    dx = r * (dxh - xh * (_lane_sums(dxh * xh) * (1.0 / QK_DIM)))
    return dx, jnp.sum(dy * xh, axis=0, keepdims=True)


def _place():
    return lax.axis_index("x"), lax.axis_index("y"), lax.axis_index("c")


def _gather_comm(arrs):
    n = len(arrs)

    def plan(ins, outs, sems):
        send_sems, recv_sems, local_sems = sems
        x, y, c = _place()
        me, sibling = (x, y, c), (x, y, 1 - c)
        chips = [(1 - x, y), (x, 1 - y), (1 - x, 1 - y)]

        def slot(a, px, py, pc):
            return outs[a].at[4 * px + 2 * py + pc]

        def copy(a, k, block, to, src=None):
            return pltpu.make_async_remote_copy(
                src_ref=slot(a, *block) if src is None else src, dst_ref=slot(a, *block),
                send_sem=send_sems.at[a, k], recv_sem=recv_sems.at[a, k], device_id=to, device_id_type=MESH)

        def own():
            mine = [pltpu.make_async_copy(ins[a], slot(a, *me), local_sems.at[a]) for a in range(n)]
            first = []
            for a in range(n):
                first.append(copy(a, 0, me, sibling, src=ins[a]))
                first += [copy(a, 1 + j, me, (*chip, c), src=ins[a]) for j, chip in enumerate(chips)]
            return mine, first

        return c, me, sibling, chips, copy, own

    def start(ins, outs, sems):
        mine, first = plan(ins, outs, sems)[-1]()
        for cp in mine + first:
            cp.start()

    def mid(ins, outs, sems):
        c, me, sibling, chips, copy, _ = plan(ins, outs, sems)
        for j, chip in enumerate(chips):
            for a in range(n):
                copy(a, 1 + j, (*chip, c), me).wait_recv()
                copy(a, 4 + j, (*chip, c), sibling).start()

    def wait(ins, outs, sems):
        c, me, sibling, chips, copy, own = plan(ins, outs, sems)
        mine, first = own()
        passed = [copy(a, 4 + j, (*chip, c), sibling) for j, chip in enumerate(chips) for a in range(n)]
        for a in range(n):
            copy(a, 0, sibling, me).wait_recv()
            for j, chip in enumerate(chips):
                copy(a, 4 + j, (*chip, 1 - c), me).wait_recv()
        for cp in first + passed:
            cp.wait_send()
        for cp in mine:
            cp.wait()

    return _Comm(
        list(arrs), [jax.ShapeDtypeStruct((N_DEV,) + a.shape, a.dtype) for a in arrs],
        [pltpu.SemaphoreType.DMA((n, 7)), pltpu.SemaphoreType.DMA((n, 7)), pltpu.SemaphoreType.DMA((n,))],
        start, wait, mid)


def _sum_gathered(g):
    rows = g.shape[1]

    def body(g_ref, sum_ref):
        total = g_ref[0]
        for d in range(1, N_DEV):
            total = total + g_ref[d]
        sum_ref[...] = total

    return pl.pallas_call(
        body, name="sum_gathered_small", out_shape=jax.ShapeDtypeStruct((rows, 128), F32), grid=(1,),
        in_specs=[pl.BlockSpec((N_DEV, rows, 128), lambda i: (0, 0, 0))],
        out_specs=pl.BlockSpec((rows, 128), lambda i: (0, 0)), compiler_params=_cparams(1),
    )(g)


def _sum_rows(rows):
    return rows if rows <= 512 else rows // 2


def _pair_exchange_comm(gp):
    _, rows, cols = gp.shape

    def copies(ins, outs, sems):
        send_sems, recv_sems = sems
        x, y, c = _place()
        return [pltpu.make_async_remote_copy(
            src_ref=ins[0].at[2 * j + (1 - c)], dst_ref=outs[0].at[j], send_sem=send_sems.at[j],
            recv_sem=recv_sems.at[j], device_id=(x, y, 1 - c), device_id_type=MESH) for j in range(4)]

    def start(ins, outs, sems):
        for cp in copies(ins, outs, sems):
            cp.start()

    def wait(ins, outs, sems):
        for cp in copies(ins, outs, sems):
            cp.wait()

    return _Comm([gp], [jax.ShapeDtypeStruct((4, rows, cols), gp.dtype)],
                 [pltpu.SemaphoreType.DMA((4,)), pltpu.SemaphoreType.DMA((4,))], start, wait)


def _rs_pair_sum(gp, got, c_arr, name):
    _, rows, cols = got.shape
    rb = _sum_rows(rows)
    gp4 = gp.reshape(4, 2, rows, cols)

    def body(c_ref, a_ref, b_ref, o_ref):
        o_ref[0] = (a_ref[0, 0].astype(F32) + b_ref[0].astype(F32)).astype(o_ref.dtype)

    return pl.pallas_call(
        body, name=name, out_shape=jax.ShapeDtypeStruct((4, rows, cols), gp.dtype),
        grid_spec=pltpu.PrefetchScalarGridSpec(
            num_scalar_prefetch=1, grid=(4, rows // rb),
            in_specs=[pl.BlockSpec((1, 1, rb, cols), lambda j, r, cr: (j, cr[0], r, 0)),
                      pl.BlockSpec((1, rb, cols), lambda j, r, cr: (j, r, 0))],
            out_specs=pl.BlockSpec((1, rb, cols), lambda j, r, cr: (j, r, 0))),
        compiler_params=_cparams(2),
    )(c_arr, gp4, got)


def _chip_exchange_comm(pb):
    _, rows, cols = pb.shape

    def copies(ins, outs, sems):
        send_sems, recv_sems = sems
        x, y, c = _place()
        chips = [(1 - x, y), (x, 1 - y), (1 - x, 1 - y)]
        return [pltpu.make_async_remote_copy(
            src_ref=ins[0].at[2 * px + py], dst_ref=outs[0].at[k], send_sem=send_sems.at[k],
            recv_sem=recv_sems.at[k], device_id=(px, py, c), device_id_type=MESH)
            for k, (px, py) in enumerate(chips)]

    def start(ins, outs, sems):
        for cp in copies(ins, outs, sems):
            cp.start()

    def wait(ins, outs, sems):
        for cp in copies(ins, outs, sems):
            cp.wait()

    return _Comm([pb], [jax.ShapeDtypeStruct((3, rows, cols), pb.dtype)],
                 [pltpu.SemaphoreType.DMA((3,)), pltpu.SemaphoreType.DMA((3,))], start, wait)


def _rs_final_sum(pb, got, chip_arr, name):
    _, rows, cols = got.shape
    rb = _sum_rows(rows)

    def body(j_ref, a_ref, b_ref, o_ref):
        o_ref[...] = ((a_ref[0].astype(F32) + b_ref[0].astype(F32)) + b_ref[1].astype(F32)) + b_ref[2].astype(F32)

    return pl.pallas_call(
        body, name=name, out_shape=jax.ShapeDtypeStruct((rows, cols), F32),
        grid_spec=pltpu.PrefetchScalarGridSpec(
            num_scalar_prefetch=1, grid=(rows // rb,),
            in_specs=[pl.BlockSpec((1, rb, cols), lambda r, jr: (jr[0], r, 0)),
                      pl.BlockSpec((3, rb, cols), lambda r, jr: (0, r, 0))],
            out_specs=pl.BlockSpec((rb, cols), lambda r, jr: (r, 0))),
        compiler_params=_cparams(1),
    )(chip_arr, pb, got)


def _rope_tables(pos_col, inv_freq, comm=None):
    t = pos_col.shape[0]

    def body(p_ref, f_ref, c_ref, s_ref):
        ang = p_ref[...].astype(F32) * f_ref[...]
        lane = lax.broadcasted_iota(jnp.int32, ang.shape, 1)
        c_ref[...] = jnp.where(lane < QK_ROPE, jnp.cos(ang), 0.0)
        s = jnp.sin(ang)
        s_ref[...] = jnp.where(lane < 32, -s, jnp.where(lane < QK_ROPE, s, 0.0))

    spec = pl.BlockSpec((TB, 128), lambda i: (i, 0))
    return _call(
        body, "rope_tables", (t // TB,), [pl.BlockSpec((TB, 1), lambda i: (i, 0)), _const_spec((1, 128))],
        [spec] * 2, [jax.ShapeDtypeStruct((t, 128), F32)] * 2, (pos_col, inv_freq), (), comm)


def _sgu_conv_fwd(proj, tail, lng_ref, ws_ref, bst_ref, cw_ref):
    gu = _gelu(proj[:, 0:SG_W])
    gv = _gelu(proj[:, SG_W:2 * SG_W])
    bg = proj[:, 1024:1536]
    z = proj[:, 1536:2048] * proj[:, 2048:2560]
    heads = []
    for h in range(SG_HEADS):
        sl = slice(h * SG_HD, (h + 1) * SG_HD)
        vn, _, _ = _ln_head(gv[:, sl], lng_ref[:, sl])
        vnb = vn.astype(BF16)
        wm = _tril_bf16(ws_ref[h])
        bcol = bst_ref[:, h:h + 1]
        mixed = jnp.concatenate(
            [_dot(wm, vnb[k * SG_CHUNK:(k + 1) * SG_CHUNK], 1, 0) + bcol for k in range(TB // SG_CHUNK)], axis=0)
        heads.append(gu[:, sl] * mixed)
    a_out = jnp.concatenate(heads, axis=1)
    y, _, _ = _conv_fwd(z, tail, cw_ref)
    return a_out, bg * y, z


def _even_fwd(x, wg, gamma, lng, ws, bst, cw, seq, comm=None):
    t = x.shape[0]
    nbs = seq // TB

    def body(x_ref, gam_ref, win_ref, wout_ref, lng_ref, ws_ref, bst_ref, cw_ref, x1_ref, proj_ref, tail_ref):
        i = pl.program_id(0)
        xv = x_ref[...]
        h, _ = _rms(xv, gam_ref[...])
        proj = _dot(h.astype(BF16), win_ref[...].reshape(EVEN_IN, D), 1, 1)
        proj_ref[...] = proj.astype(BF16)
        tail = jnp.where(i % nbs == 0, 0.0, tail_ref[...])
        a_out, b_out, z = _sgu_conv_fwd(proj, tail, lng_ref, ws_ref, bst_ref, cw_ref)
        tail_ref[...] = z[TB - HALO:, :]
        x1_ref[...] = (xv + _dot(a_out.astype(BF16), wout_ref[0:4].reshape(512, D), 1, 0)
                       + _dot(b_out.astype(BF16), wout_ref[4:8].reshape(512, D), 1, 0))

    row = pl.BlockSpec((TB, D), lambda i: (i, 0))
    return _call(
        body, "even_fwd", (t // TB,),
        [row, _const_spec((1, D)), _wspec(N_EIN, OFF_EIN), _wspec(N_SQ, OFF_EOUT), _const_spec((1, SG_W)),
         _const_spec((SG_HEADS, 128, 128)), _const_spec((128, 128)), _const_spec((8, SC_W))],
        [row, pl.BlockSpec((TB, EVEN_IN), lambda i: (i, 0))],
        [jax.ShapeDtypeStruct((t, D), F32), jax.ShapeDtypeStruct((t, EVEN_IN), BF16)],
        (x, gamma, wg, wg, lng, ws, bst, cw), [pltpu.VMEM((HALO, SC_W), F32)], comm)


def _even_bwd(x, proj, dx1, wg, gamma, lng, ws, bst, cw, seq, comm=None):
    t = x.shape[0]
    nb, nbs = t // TB, seq // TB

    def body(x_ref, proj_ref, ptail_ref, dx1_ref, gam_ref, win_ref, wout_ref, lng_ref, ws_ref, bst_ref, cw_ref,
             dx0_ref, dproj_ref, mix_ref, h_ref, dgam_ref, dws_ref, dbc_ref, dlng_ref, dcw_ref, head_ref):
        i = pl.program_id(0)
        blk = nb - 1 - i

        @pl.when(i == 0)
        def _():
            dgam_ref[...] = jnp.zeros_like(dgam_ref)
            dws_ref[...] = jnp.zeros_like(dws_ref)
            dbc_ref[...] = jnp.zeros_like(dbc_ref)
            dlng_ref[...] = jnp.zeros_like(dlng_ref)
            dcw_ref[...] = jnp.zeros_like(dcw_ref)

        xv = x_ref[...]
        gam = gam_ref[...]
        h, r = _rms(xv, gam)
        h_ref[...] = h.astype(BF16)
        dx1 = dx1_ref[...]
        dmix = _dot(dx1.astype(BF16), wout_ref[...].reshape(D, D), 1, 1)
        da, db = dmix[:, :SG_W], dmix[:, SG_W:]
        proj = proj_ref[...].astype(F32)
        u, v = proj[:, 0:SG_W], proj[:, SG_W:2 * SG_W]
        bg, cg, hv = proj[:, 1024:1536], proj[:, 1536:2048], proj[:, 2048:2560]
        gu, gv = _gelu(u), _gelu(v)

        a_heads, dgv_heads = [], []
        for hd in range(SG_HEADS):
            sl = slice(hd * SG_HD, (hd + 1) * SG_HD)
            g_h = lng_ref[:, sl]
            vn, xh, rr = _ln_head(gv[:, sl], g_h)
            vnb = vn.astype(BF16)
            wm = _tril_bf16(ws_ref[hd])
            bcol = bst_ref[:, hd:hd + 1]
            mixed_c, dvn_c = [], []
            dw_acc = jnp.zeros((128, 128), F32)
            db_acc = jnp.zeros((128, 1), F32)
            for k in range(TB // SG_CHUNK):
                rs = slice(k * SG_CHUNK, (k + 1) * SG_CHUNK)
                mixed = _dot(wm, vnb[rs], 1, 0) + bcol
                dmixed = da[rs, sl] * gu[rs, sl]
                dmb = dmixed.astype(BF16)
                dvn_c.append(_dot(wm, dmb, 0, 0))
                dw_acc = dw_acc + _dot(dmb, vnb[rs], 1, 1)
                db_acc = db_acc + jnp.sum(dmixed, axis=1, keepdims=True)
                mixed_c.append(mixed)
            mixed_h = jnp.concatenate(mixed_c, axis=0)
            dvn = jnp.concatenate(dvn_c, axis=0)
            r_i = lax.broadcasted_iota(jnp.int32, (128, 128), 0)
            c_i = lax.broadcasted_iota(jnp.int32, (128, 128), 1)
            dws_ref[hd] += jnp.where(r_i >= c_i, dw_acc, 0.0)
            dbc_ref[:, hd:hd + 1] += db_acc
            dlng_ref[:, sl] += jnp.sum(dvn * xh, axis=0, keepdims=True)
            dxh = dvn * g_h
            dgv = rr * (dxh - jnp.mean(dxh, axis=-1, keepdims=True)
                        - xh * jnp.mean(dxh * xh, axis=-1, keepdims=True))
            a_heads.append(gu[:, sl] * mixed_h)
            dproj_ref[:, sl] = (da[:, sl] * mixed_h * _gelu_grad(u[:, sl])).astype(BF16)
            dgv_heads.append(dgv * _gelu_grad(v[:, sl]))
        dproj_ref[:, SG_W:2 * SG_W] = jnp.concatenate(dgv_heads, axis=1).astype(BF16)
        mix_ref[:, :SG_W] = jnp.concatenate(a_heads, axis=1).astype(BF16)

        z = cg * hv
        pt = ptail_ref[...].astype(F32)
        tail = jnp.where(blk % nbs == 0, 0.0, pt[:, 1536:2048] * pt[:, 2048:2560])
        y, zs1, zs2 = _conv_fwd(z, tail, cw_ref)
        mix_ref[:, SG_W:] = (bg * y).astype(BF16)
        dy = db * bg
        head = jnp.where(blk % nbs == nbs - 1, 0.0, head_ref[...])
        ext = jnp.concatenate([dy, head], axis=0)
        dz = (cw_ref[2:3, :] * dy + cw_ref[1:2, :] * _shift_up(ext, 1)[:TB]
              + cw_ref[0:1, :] * _shift_up(ext, 2)[:TB])
        head_ref[...] = dy[:HALO, :]
        dcw_ref[2:3, :] += jnp.sum(dy * z, axis=0, keepdims=True)
        dcw_ref[1:2, :] += jnp.sum(dy * zs1, axis=0, keepdims=True)
        dcw_ref[0:1, :] += jnp.sum(dy * zs2, axis=0, keepdims=True)
        dproj_ref[:, 1024:1536] = (db * y).astype(BF16)
        dproj_ref[:, 1536:2048] = (dz * hv).astype(BF16)
        dproj_ref[:, 2048:2560] = (dz * cg).astype(BF16)

        dh = _dot(dproj_ref[...], win_ref[...].reshape(EVEN_IN, D), 1, 0)
        dxn, dgam = _rms_bwd(xv, r, gam, dh)
        dgam_ref[...] += dgam
        dx0_ref[...] = dx1 + dxn

    def rev(w):
        return pl.BlockSpec((TB, w), lambda i: (nb - 1 - i, 0))

    ptail = pl.BlockSpec((HALO, EVEN_IN), lambda i: (jnp.maximum((nb - 1 - i) * (TB // HALO) - 1, 0), 0))
    return _call(
        body, "even_bwd", (nb,),
        [rev(D), rev(EVEN_IN), ptail, rev(D), _const_spec((1, D)), _wspec(N_EIN, OFF_EIN),
         _wspec(N_SQ, OFF_EOUT), _const_spec((1, SG_W)), _const_spec((SG_HEADS, 128, 128)),
         _const_spec((128, 128)), _const_spec((8, SC_W))],
        [rev(D), rev(EVEN_IN), rev(D), rev(D), _const_spec((1, D)), _const_spec((SG_HEADS, 128, 128)),
         _const_spec((128, 128)), _const_spec((1, SG_W)), _const_spec((8, SC_W))],
        [jax.ShapeDtypeStruct((t, D), F32), jax.ShapeDtypeStruct((t, EVEN_IN), BF16),
         jax.ShapeDtypeStruct((t, D), BF16), jax.ShapeDtypeStruct((t, D), BF16),
         jax.ShapeDtypeStruct((1, D), F32), jax.ShapeDtypeStruct((SG_HEADS, 128, 128), F32),
         jax.ShapeDtypeStruct((128, 128), F32), jax.ShapeDtypeStruct((1, SG_W), F32),
         jax.ShapeDtypeStruct((8, SC_W), F32)],
        (x, proj, proj, dx1, gamma, wg, wg, lng, ws, bst, cw), [pltpu.VMEM((HALO, SC_W), F32)], comm)


def _ffn_fwd(x, w_gu, w_d, gamma, name, comm=None, target=None):
    t = x.shape[0]
    last = target is not None

    def body(*refs):
        x_ref, gam_ref, wg_ref, wu_ref, wd_ref = refs[:5]
        y_ref, g_ref, u_ref = refs[5 + last:8 + last]
        xv = x_ref[...]
        h, _ = _rms(xv, gam_ref[...])
        hb = h.astype(BF16)
        g = _dot(hb, wg_ref[...].reshape(D_FF, D), 1, 1)
        u = _dot(hb, wu_ref[...].reshape(D_FF, D), 1, 1)
        g_ref[...] = g.astype(BF16)
        u_ref[...] = u.astype(BF16)
        act = g * jax.nn.sigmoid(g) * u
        y = xv + _dot(act.astype(BF16), wd_ref[...].reshape(D_FF, D), 1, 0)
        if not last:
            y_ref[...] = y
            return
        loss_ref = refs[9]

        @pl.when(pl.program_id(0) == 0)
        def _():
            loss_ref[...] = jnp.zeros_like(loss_ref)

        err = y - refs[5][...]
        y_ref[...] = err * (1.0 / D)
        sq = jnp.sum(jnp.sum(err * err, axis=-1, keepdims=True), axis=0, keepdims=True)
        loss_ref[...] += (0.5 / D) * sq

    row = pl.BlockSpec((TB, D), lambda i: (i, 0))
    wide = pl.BlockSpec((TB, D_FF), lambda i: (i, 0))
    in_specs = [row, _const_spec((1, D)), _wspec(N_FF, OFF_GATE), _wspec(N_FF, OFF_UP), _wspec(N_FF, 0)]
    out_specs = [row, wide, wide]
    out_shape = [jax.ShapeDtypeStruct((t, D), F32), jax.ShapeDtypeStruct((t, D_FF), BF16),
                 jax.ShapeDtypeStruct((t, D_FF), BF16)]
    args = (x, gamma, w_gu, w_gu, w_d)
    if last:
        in_specs, args = in_specs + [row], args + (target,)
        out_specs, out_shape = out_specs + [_const_spec((8, 128))], out_shape + [jax.ShapeDtypeStruct((8, 128), F32)]
    return _call(body, name, (t // TB,), in_specs, out_specs, out_shape, args, (), comm)


def _ffn_up(x, w_gu, gamma, name, comm=None):
    t = x.shape[0]

    def body(x_ref, gam_ref, wg_ref, wu_ref, g_ref, u_ref, act_ref):
        h, _ = _rms(x_ref[...], gam_ref[...])
        hb = h.astype(BF16)
        g = _dot(hb, wg_ref[...].reshape(D_FF, D), 1, 1)
        u = _dot(hb, wu_ref[...].reshape(D_FF, D), 1, 1)
        g_ref[...] = g.astype(BF16)
        u_ref[...] = u.astype(BF16)
        act_ref[...] = (g * jax.nn.sigmoid(g) * u).astype(BF16)

    row = pl.BlockSpec((TB, D), lambda i: (i, 0))
    wide = pl.BlockSpec((TB, D_FF), lambda i: (i, 0))
    return _call(body, name, (t // TB,), [row, _const_spec((1, D)), _wspec(N_FF, OFF_GATE), _wspec(N_FF, OFF_UP)],
                 [wide, wide, wide], [jax.ShapeDtypeStruct((t, D_FF), BF16)] * 3, (x, gamma, w_gu, w_gu), (), comm)


def _ffn_down(x, act, w_d, name, comm=None):
    t = x.shape[0]

    def body(x_ref, a_ref, wd_ref, y_ref):
        y_ref[...] = x_ref[...] + _dot(a_ref[...], wd_ref[...].reshape(D_FF, D), 1, 0)

    row = pl.BlockSpec((TB, D), lambda i: (i, 0))
    wide = pl.BlockSpec((TB, D_FF), lambda i: (i, 0))
    return _call(body, name, (t // TB,), [row, wide, _wspec(N_FF, 0)], [row], [jax.ShapeDtypeStruct((t, D), F32)],
                 (x, act, w_d), (), comm)


def _ffn_bwd(x, g, u, dy, w_gu, w_d, gamma, name, comm=None):
    t = x.shape[0]

    def body(x_ref, g_ref, u_ref, dy_ref, gam_ref, wg_ref, wu_ref, wd_ref,
             dx_ref, act_ref, dg_ref, du_ref, h_ref, dgam_ref):
        @pl.when(pl.program_id(0) == 0)
        def _():
            dgam_ref[...] = jnp.zeros_like(dgam_ref)

        xv = x_ref[...]
        gam = gam_ref[...]
        h, r = _rms(xv, gam)
        h_ref[...] = h.astype(BF16)
        dyv = dy_ref[...]
        dact = _dot(dyv.astype(BF16), wd_ref[...].reshape(D_FF, D), 1, 1)
        gv = g_ref[...].astype(F32)
        uv = u_ref[...].astype(F32)
        sg = jax.nn.sigmoid(gv)
        silu = gv * sg
        act_ref[...] = (silu * uv).astype(BF16)
        dgb = (dact * uv * (sg * (1.0 + gv * (1.0 - sg)))).astype(BF16)
        dub = (dact * silu).astype(BF16)
        dg_ref[...] = dgb
        du_ref[...] = dub
        dh = _dot(dgb, wg_ref[...].reshape(D_FF, D), 1, 0) + _dot(dub, wu_ref[...].reshape(D_FF, D), 1, 0)
        dxn, dgam = _rms_bwd(xv, r, gam, dh)
        dgam_ref[...] += dgam
        dx_ref[...] = dyv + dxn

    row = pl.BlockSpec((TB_FFN_BWD, D), lambda i: (i, 0))
    wide = pl.BlockSpec((TB_FFN_BWD, D_FF), lambda i: (i, 0))
    return _call(
        body, name, (t // TB_FFN_BWD,),
        [row, wide, wide, row, _const_spec((1, D)), _wspec(N_FF, OFF_GATE), _wspec(N_FF, OFF_UP),
         _wspec(N_FF, 0)],
        [row, wide, wide, wide, row, _const_spec((1, D))],
        [jax.ShapeDtypeStruct((t, D), F32), jax.ShapeDtypeStruct((t, D_FF), BF16),
         jax.ShapeDtypeStruct((t, D_FF), BF16), jax.ShapeDtypeStruct((t, D_FF), BF16),
         jax.ShapeDtypeStruct((t, D), BF16), jax.ShapeDtypeStruct((1, D), F32)],
        (x, g, u, dy, gamma, w_gu, w_gu, w_d), (), comm)


def _odd_pre_fwd(x, wg, gamma, qbt, kvbt, qa_g, kva_g, pw_bd, pscale, seq):
    t = x.shape[0]
    nbs = seq // TB

    def body(x_ref, gam_ref, win_ref, qb_ref, kvb_ref, qa_ref, kva_ref, pw_ref, ps_ref,
             proj_ref, q_ref, kv_ref, kr_ref, c_ref, tail_ref):
        i = pl.program_id(0)
        h, _ = _rms(x_ref[...], gam_ref[...])
        proj = _dot(h.astype(BF16), win_ref[...].reshape(D, D), 1, 0)
        proj_ref[...] = proj.astype(BF16)
        zp, ql, kvl = proj[:, :POOL_W], proj[:, 256:640], proj[:, 640:896]
        kr_ref[...] = proj[:, 896:1024]
        qn, _ = _rms(ql, qa_ref[...])
        q_ref[...] = _dot(qn.astype(BF16), qb_ref[...], 1, 1).astype(BF16)
        kvn, _ = _rms(kvl, kva_ref[...])
        kv_ref[...] = _dot(kvn.astype(BF16), kvb_ref[...], 1, 1).astype(BF16)
        tail = jnp.where(i % nbs == 0, 0.0, tail_ref[...])
        pooled, _, _ = _pool_fwd(zp, tail, i % nbs)
        tail_ref[...] = zp[TB - HALO:, :]
        c_ref[...] = (_dot(pooled.astype(BF16), pw_ref[...], 1, 0) * ps_ref[...]).astype(BF16)

    def row(w):
        return pl.BlockSpec((TB, w), lambda i: (i, 0))

    return pl.pallas_call(
        body, name="odd_pre_fwd",
        out_shape=[jax.ShapeDtypeStruct((t, D), BF16), jax.ShapeDtypeStruct((t, HEADS * HP), BF16),
                   jax.ShapeDtypeStruct((t, HEADS * HP), BF16), jax.ShapeDtypeStruct((t, 128), F32),
                   jax.ShapeDtypeStruct((t, POOL_W), BF16)],
        grid=(t // TB,),
        in_specs=[row(D), _const_spec((1, D)), _wspec(N_SQ, OFF_OIN), _const_spec((HEADS * HP, Q_LORA)),
                  _const_spec((HEADS * HP, KV_LORA)), _const_spec((1, Q_LORA)), _const_spec((1, KV_LORA)),
                  _const_spec((POOL_W, POOL_W)), _const_spec((1, POOL_W))],
        out_specs=[row(D), row(HEADS * HP), row(HEADS * HP), row(128), row(POOL_W)],
        scratch_shapes=[pltpu.VMEM((HALO, POOL_W), F32)],
        compiler_params=_cparams(1),
    )(x, gamma, wg, qbt, kvbt, qa_g, kva_g, pw_bd, pscale)


def _odd_pre_bwd(x, proj, dx3, dmix, dq, dkv, dkr, wg, gamma, qbt, kvbt, qa_g, kva_g, pw_bd, pscale, seq):
    t = x.shape[0]
    nb, nbs = t // TB, seq // TB

    def body(x_ref, proj_ref, ptail_ref, dx3_ref, dco_ref, dq_ref, dkv_ref, dkr_ref, gam_ref, win_ref, qb_ref,
             kvb_ref, qa_ref, kva_ref, pw_ref, ps_ref,
             dx2_ref, dproj_ref, h_ref, qn_ref, kvn_ref, dgam_ref, dqa_ref, dkva_ref, dpw_ref, dps_ref, head_ref):
        i = pl.program_id(0)
        blk = nb - 1 - i

        @pl.when(i == 0)
        def _():
            dgam_ref[...] = jnp.zeros_like(dgam_ref)
            dqa_ref[...] = jnp.zeros_like(dqa_ref)
            dkva_ref[...] = jnp.zeros_like(dkva_ref)
            dpw_ref[...] = jnp.zeros_like(dpw_ref)
            dps_ref[...] = jnp.zeros_like(dps_ref)

        xv = x_ref[...]
        gam = gam_ref[...]
        h, r = _rms(xv, gam)
        h_ref[...] = h.astype(BF16)
        proj = proj_ref[...].astype(F32)
        zp, ql, kvl = proj[:, :POOL_W], proj[:, 256:640], proj[:, 640:896]

        qa = qa_ref[...]
        qn, rq = _rms(ql, qa)
        qn_ref[...] = qn.astype(BF16)
        dql, dqa = _rms_bwd(ql, rq, qa, _dot(dq_ref[...], qb_ref[...], 1, 0))
        dqa_ref[...] += dqa
        kva = kva_ref[...]
        kvn, rkv = _rms(kvl, kva)
        kvn_ref[...] = kvn.astype(BF16)
        dkvl, dkva = _rms_bwd(kvl, rkv, kva, _dot(dkv_ref[...], kvb_ref[...], 1, 0))
        dkva_ref[...] += dkva

        pt = ptail_ref[...].astype(F32)
        tail = jnp.where(blk % nbs == 0, 0.0, pt[:, :POOL_W])
        pooled, cnt, grp = _pool_fwd(zp, tail, blk % nbs)
        pb = pooled.astype(BF16)
        pw = pw_ref[...]
        dco = dco_ref[...].astype(F32)
        dps_ref[...] += jnp.sum(dco * _dot(pb, pw, 1, 0), axis=0, keepdims=True)
        dpo = (dco * ps_ref[...]).astype(BF16)
        dpw_ref[...] += _dot(pb, dpo, 0, 0)
        dpooled = _dot(dpo, pw, 1, 1)
        dpm = dpooled / cnt
        head = jnp.where(blk % nbs == nbs - 1, 0.0, head_ref[...])
        dz = _pool_bwd(dpooled, dpm, head, grp)
        head_ref[...] = dpm[:HALO, :]

        dproj_ref[:, :POOL_W] = dz.astype(BF16)
        dproj_ref[:, 256:640] = dql.astype(BF16)
        dproj_ref[:, 640:896] = dkvl.astype(BF16)
        dproj_ref[:, 896:1024] = dkr_ref[...].astype(BF16)
        dh = _dot(dproj_ref[...], win_ref[...].reshape(D, D), 1, 1)
        dxn, dgam = _rms_bwd(xv, r, gam, dh)
        dgam_ref[...] += dgam
        dx2_ref[...] = dx3_ref[...] + dxn

    def rev(w):
        return pl.BlockSpec((TB, w), lambda i: (nb - 1 - i, 0))

    ptail = pl.BlockSpec((HALO, D), lambda i: (jnp.maximum((nb - 1 - i) * (TB // HALO) - 1, 0), 0))
    return pl.pallas_call(
        body, name="odd_pre_bwd",
        out_shape=[jax.ShapeDtypeStruct((t, D), F32), jax.ShapeDtypeStruct((t, D), BF16),
                   jax.ShapeDtypeStruct((t, D), BF16), jax.ShapeDtypeStruct((t, Q_LORA), BF16),
                   jax.ShapeDtypeStruct((t, KV_LORA), BF16), jax.ShapeDtypeStruct((1, D), F32),
                   jax.ShapeDtypeStruct((1, Q_LORA), F32), jax.ShapeDtypeStruct((1, KV_LORA), F32),
                   jax.ShapeDtypeStruct((POOL_W, POOL_W), F32), jax.ShapeDtypeStruct((1, POOL_W), F32)],
        grid=(nb,),
        in_specs=[rev(D), rev(D), ptail, rev(D), rev(POOL_W), rev(HEADS * HP), rev(HEADS * HP), rev(128),
                  _const_spec((1, D)), _wspec(N_SQ, OFF_OIN), _const_spec((HEADS * HP, Q_LORA)),
                  _const_spec((HEADS * HP, KV_LORA)), _const_spec((1, Q_LORA)), _const_spec((1, KV_LORA)),
                  _const_spec((POOL_W, POOL_W)), _const_spec((1, POOL_W))],
        out_specs=[rev(D), rev(D), rev(D), rev(Q_LORA), rev(KV_LORA), _const_spec((1, D)), _const_spec((1, Q_LORA)),
                   _const_spec((1, KV_LORA)), _const_spec((POOL_W, POOL_W)), _const_spec((1, POOL_W))],
        scratch_shapes=[pltpu.VMEM((HALO, POOL_W), F32)],
        compiler_params=_cparams(1),
    )(x, proj, proj, dx3, dmix, dq, dkv, dkr, gamma, wg, qbt, kvbt, qa_g, kva_g, pw_bd, pscale)


def _attn_specs(seq):
    head = pl.BlockSpec((seq, HP), lambda b, h: (b, h))
    shared = pl.BlockSpec((seq, 128), lambda b, h: (b, 0))
    gain = pl.BlockSpec((1, HP), lambda b, h: (0, 0))
    return head, shared, gain


def _causal_bias(n):
    rows = lax.broadcasted_iota(jnp.int32, (n, n), 0)
    cols = lax.broadcasted_iota(jnp.int32, (n, n), 1)
    return jnp.where(cols <= rows, 0.0, NEG_INF)


def _attn_fwd(q, kv, kr, cos, sin, gq, gk, seq, comm=None):
    t = q.shape[0]
    qb = min(512, seq)

    def body(q_ref, kv_ref, kr_ref, c_ref, s_ref, gq_ref, gk_ref, o_ref, lse_ref):
        c, s = c_ref[...], s_ref[...]
        qf, _ = _qk_prep(q_ref[...].astype(F32), gq_ref[...], c, s)
        kin = jnp.concatenate([kv_ref[:, :128].astype(F32), kr_ref[...]], axis=1)
        kf, _ = _qk_prep(kin, gk_ref[...], c, s)
        qf, kf = qf.astype(BF16), kf.astype(BF16)
        v1 = jnp.concatenate([kv_ref[:, 128:], jnp.ones((seq, V_DIM), BF16)], axis=1)
        bias = _causal_bias(qb)
        for q0 in range(0, seq, qb):
            q1 = q0 + qb
            qblk = qf[q0:q1]
            s_dg = _dot(qblk, kf[q0:q1], 1, 1) + bias
            m = jnp.max(s_dg, axis=-1, keepdims=True)
            if q0:
                s_off = _dot(qblk, kf[:q0], 1, 1)
                m = jnp.maximum(m, jnp.max(s_off, axis=-1, keepdims=True))
            acc = _dot(jnp.exp(s_dg - m).astype(BF16), v1[q0:q1], 1, 0)
            if q0:
                acc = acc + _dot(jnp.exp(s_off - m).astype(BF16), v1[:q0], 1, 0)
            l = acc[:, V_DIM:]
            o_ref[q0:q1, :] = (acc[:, :V_DIM] / l).astype(BF16)
            lse_ref[q0:q1, :] = m + jnp.log(l)

    head, shared, gain = _attn_specs(seq)
    per_head = pl.BlockSpec((seq, V_DIM), lambda b, h: (b, h))
    return _call(
        body, "attn_fwd", (t // seq, HEADS),
        [head, head, shared, shared, shared, gain, gain], [per_head, per_head],
        [jax.ShapeDtypeStruct((t, HEADS * V_DIM), BF16), jax.ShapeDtypeStruct((t, HEADS * V_DIM), F32)],
        (q, kv, kr, cos, sin, gq, gk), (), comm)


def _attn_bwd(q, kv, kr, cos, sin, gq, gk, dmix, d_out, lse, seq, comm=None):
    t = q.shape[0]
    qb = min(512, seq)

    def body(q_ref, kv_ref, kr_ref, c_ref, s_ref, gq_ref, gk_ref, do_ref, o_ref, lse_ref,
             dq_ref, dkv_ref, dkr_ref, dgq_ref, dgk_ref, dqf_ref, dkf_ref, dv_ref):
        b, hd = pl.program_id(0), pl.program_id(1)

        @pl.when((b == 0) & (hd == 0))
        def _():
            dgq_ref[...] = jnp.zeros_like(dgq_ref)
            dgk_ref[...] = jnp.zeros_like(dgk_ref)

        c, sn = c_ref[...], s_ref[...]
        gq_v, gk_v = gq_ref[...], gk_ref[...]
        qin = q_ref[...].astype(F32)
        kin = jnp.concatenate([kv_ref[:, :128].astype(F32), kr_ref[...]], axis=1)
        qf32, rq = _qk_prep(qin, gq_v, c, sn)
        kf32, rk = _qk_prep(kin, gk_v, c, sn)
        qf, kf = qf32.astype(BF16), kf32.astype(BF16)
        vb = kv_ref[:, 128:]
        dkf_ref[...] = jnp.zeros_like(dkf_ref)
        dv_ref[...] = jnp.zeros_like(dv_ref)
        bias = _causal_bias(qb)
        for q0 in range(0, seq, qb):
            q1 = q0 + qb
            qblk = qf[q0:q1]
            do = do_ref[q0:q1, :]
            lse_col = lse_ref[q0:q1, 0:1]
            d_col = jnp.sum(do.astype(F32) * o_ref[q0:q1, :].astype(F32), axis=-1, keepdims=True)
            dq_acc = None
            for k0, k1, diag in ((q0, q1, True), (0, q0, False)):
                if k1 == k0:
                    continue
                s = _dot(qblk, kf[k0:k1], 1, 1)
                p = jnp.exp((s + bias if diag else s) - lse_col)
                dv_ref[k0:k1, :] += _dot(p.astype(BF16), do, 0, 0)
                ds = (p * (_dot(do, vb[k0:k1], 1, 1) - d_col)).astype(BF16)
                part = _dot(ds, kf[k0:k1], 1, 0)
                dq_acc = part if dq_acc is None else dq_acc + part
                dkf_ref[k0:k1, :] += _dot(ds, qblk, 0, 0)
            dqf_ref[q0:q1, :] = dq_acc
        dqin, dgq = _qk_prep_bwd(dqf_ref[...], qin, rq, gq_v, c, sn)
        dkin, dgk = _qk_prep_bwd(dkf_ref[...], kin, rk, gk_v, c, sn)
        dgq_ref[...] += dgq
        dgk_ref[...] += dgk
        dq_ref[...] = dqin.astype(BF16)
        dkv_ref[:, :128] = dkin[:, :128].astype(BF16)
        dkv_ref[:, 128:] = dv_ref[...].astype(BF16)

        @pl.when(hd == 0)
        def _():
            dkr_ref[...] = dkin[:, 128:]

        @pl.when(hd != 0)
        def _():
            dkr_ref[...] += dkin[:, 128:]

    head, shared, gain = _attn_specs(seq)
    per_head = pl.BlockSpec((seq, V_DIM), lambda b, h: (b, h))
    return _call(
        body, "attn_bwd", (t // seq, HEADS),
        [head, head, shared, shared, shared, gain, gain,
         pl.BlockSpec((seq, V_DIM), lambda b, h: (b, 2 + h)), per_head, per_head],
        [head, head, shared, gain, gain],
        [jax.ShapeDtypeStruct((t, HEADS * HP), BF16), jax.ShapeDtypeStruct((t, HEADS * HP), BF16),
         jax.ShapeDtypeStruct((t, 128), F32), jax.ShapeDtypeStruct((1, HP), F32),
         jax.ShapeDtypeStruct((1, HP), F32)],
        (q, kv, kr, cos, sin, gq, gk, dmix, d_out, lse),
        [pltpu.VMEM((seq, HP), F32), pltpu.VMEM((seq, HP), F32), pltpu.VMEM((seq, V_DIM), F32)], comm)


def _odd_post_fwd(x, c_out, d_out, wg):
    t = x.shape[0]

    def body(x_ref, c_ref, d_ref, w_ref, y_ref):
        y_ref[...] = (x_ref[...] + _dot(c_ref[...], w_ref[0:2].reshape(POOL_W, D), 1, 0)
                      + _dot(d_ref[...], w_ref[2:8].reshape(HEADS * V_DIM, D), 1, 0))

    def row(w):
        return pl.BlockSpec((TB, w), lambda i: (i, 0))

    return pl.pallas_call(
        body, name="odd_post_fwd", out_shape=jax.ShapeDtypeStruct((t, D), F32), grid=(t // TB,),
        in_specs=[row(D), row(POOL_W), row(HEADS * V_DIM), _wspec(N_SQ, OFF_OOUT)], out_specs=row(D),
        compiler_params=_cparams(1),
    )(x, c_out, d_out, wg)


def _odd_post_bwd(dx3, wg, comm=None):
    t = dx3.shape[0]

    def body(d_ref, w_ref, o_ref):
        o_ref[...] = _dot(d_ref[...].astype(BF16), w_ref[...].reshape(D, D), 1, 1).astype(BF16)

    row = pl.BlockSpec((TB, D), lambda i: (i, 0))
    (res,), extra = _call(body, "odd_post_bwd", (t // TB,), [row, _wspec(N_SQ, OFF_OOUT)], [row],
                          [jax.ShapeDtypeStruct((t, D), BF16)], (dx3, wg), (), comm)
    return res, extra


def _tn(a_list, b, tm, name, into=None, comm=None):
    t, n_out = b.shape
    widths = [a.shape[1] for a in a_list]
    tk = min(TK_DW, t)
    m, na, nk = sum(widths), len(a_list), t // tk
    assert na == 1 or tm == m

    def body(*refs):
        a_refs, b_ref, o_ref, acc_ref = refs[:na], refs[na], refs[-2], refs[-1]
        k = pl.program_id(1)

        @pl.when(k == 0)
        def _():
            acc_ref[...] = jnp.zeros_like(acc_ref)

        bb = b_ref[...].astype(BF16)
        m0 = 0
        for a_ref, w in zip(a_refs, widths):
            rows = slice(0, tm) if na == 1 else slice(m0, m0 + w)
            acc_ref[rows, :] += _dot(a_ref[...].astype(BF16), bb, 0, 0)
            m0 += w

        @pl.when(k == nk - 1)
        def _():
            o_ref[...] = acc_ref[...].astype(BF16).reshape(o_ref.shape)

    if na == 1:
        in_specs = [pl.BlockSpec((tk, tm), lambda i, k: (k, i))]
    else:
        in_specs = [pl.BlockSpec((tk, w), lambda i, k: (k, 0)) for w in widths]
    in_specs.append(pl.BlockSpec((tk, n_out), lambda i, k: (k, 0)))
    args = list(a_list) + [b]
    if into is None:
        out_spec = pl.BlockSpec((tm, n_out), lambda i, k: (i, 0))
        out_shape = jax.ShapeDtypeStruct((m, n_out), BF16)
        aliases = {}
    else:
        buf, n, off = into
        assert n_out == D and tm % n == 0 and off % n == 0 and (na == 1 or tm // n == N_DEV)
        idx = off // n
        out_spec = pl.BlockSpec((tm // n, n, D), lambda i, k: (i, idx, 0))
        out_shape = jax.ShapeDtypeStruct(buf.shape, BF16)
        in_specs.append(pl.BlockSpec(memory_space=pl.ANY))
        args.append(buf)
        aliases = {len(args) - 1: 0}
    (res,), extra = _call(body, name, (m // tm, nk), in_specs, [out_spec], [out_shape], args,
                          [pltpu.VMEM((tm, n_out), F32)], comm, aliases)
    return (res, extra) if comm is not None else res


def _adamw(ws, gs, ms, vs, name, nblk=1):
    n = len(ws)
    c1 = 1.0 - B1 ** STEP
    c2 = 1.0 - B2 ** STEP

    def body(*refs):
        for a in range(n):
            w, g, m, v = (refs[k * n + a][...] for k in range(4))
            d_ref, m_ref, v_ref = (refs[(4 + k) * n + a] for k in range(3))
            m_new = B1 * m + (1.0 - B1) * g
            v_new = B2 * v + (1.0 - B2) * (g * g)
            d_ref[...] = -LR * ((m_new / c1) / (jnp.sqrt(v_new / c2) + ADAM_EPS) + WD * w)
            m_ref[...] = m_new
            v_ref[...] = v_new

    grid = (nblk,)
    assert all(w.shape[0] % nblk == 0 and (nblk == 1 or (w.shape[0] // nblk) % 8 == 0) for w in ws)
    specs = [pl.BlockSpec((w.shape[0] // nblk, w.shape[1]), lambda i: (i, 0)) for w in ws]
    outs, _ = _call(body, name, grid, specs * 4, specs * 3, [jax.ShapeDtypeStruct(w.shape, F32) for w in ws] * 3,
                    (*ws, *gs, *ms, *vs))
    return outs[:n], outs[n:2 * n], outs[2 * n:]


def _rows1024(a, rows):
    flat = a.reshape(-1, D)
    return jnp.pad(flat, ((0, rows - flat.shape[0]), (0, 0)))


def _pack_shards(even_w_in, even_w_out, odd_w_in, q_b, kv_b, odd_w_out, ffn_w_gate, ffn_w_up, ffn_w_down):
    mix0 = jnp.concatenate([even_w_in[0].T, jnp.zeros((OFF_EOUT - N_EIN, D), F32), even_w_out[0]], axis=0)
    gu = [jnp.concatenate([ffn_w_gate[layer].T, ffn_w_up[layer].T], axis=0) for layer in range(2)]
    mix1 = jnp.concatenate([jnp.pad(odd_w_in[0], ((0, 0), (0, D - ODD_IN))), odd_w_out[0],
                            _rows1024(q_b[0].T, N_QB), _rows1024(kv_b[0].T, N_KVB),
                            jnp.zeros((R_MIX1 - OFF_KVB - N_KVB, D), F32)], axis=0)
    return [c.astype(BF16) for c in (mix0, gu[0], ffn_w_down[0], mix1, gu[1], ffn_w_down[1])]


def _pad_heads(a):
    k = a.shape[1]
    return jnp.pad(a.reshape(HEADS, QK_DIM, k), ((0, 0), (0, HP - QK_DIM), (0, 0))).reshape(HEADS * HP, k)


def _small_pack(parts):
    flat = []
    for p in parts:
        v = p.reshape(-1)
        flat.append(jnp.pad(v, (0, (-v.shape[0]) % 1024)))
    return jnp.concatenate(flat).reshape(-1, 128)


def _small_unpack(buf, shapes):
    flat = buf.reshape(-1)
    out, off = [], 0
    for s in shapes:
        size = int(np.prod(s))
        out.append(flat[off:off + size].reshape(s))
        off += size + (-size) % 1024
    return out


def _step(x3d, positions, target3d, chunks, tile, c_arr, chip_arr, mix_norm, ffn_norm, sg_ln_g, sg_w_s, sg_b_s,
          pool_w, q_norm, k_norm):
    bsz, seq, _ = x3d.shape
    t = bsz * seq
    x0 = x3d.reshape(t, D)
    target = target3d.reshape(t, D)
    my_mix0, my_gu0, my_d0, my_mix1, my_gu1, my_d1 = chunks

    lane = np.arange(128)
    inv_freq = np.where(lane < QK_ROPE, ROPE_THETA ** (-(2.0 * (lane % 32)) / QK_ROPE), 0.0)
    inv_freq = jnp.asarray(inv_freq.reshape(1, 128), F32)
    (cos, sin), (w_mix0, tiles) = _rope_tables(positions.reshape(t, 1), inv_freq, _gather_comm([my_mix0, tile]))

    conv_w = tiles[:, 0:3, 0:64].transpose(1, 0, 2).reshape(3, SC_W)
    pool_scale = tiles[:, 3, 0:32].reshape(1, POOL_W)
    q_a_norm = tiles[:, 4, 0:48].reshape(1, Q_LORA)
    kv_a_norm = tiles[:, 5, 0:32].reshape(1, KV_LORA)
    ws = sg_w_s[0]
    bst = jnp.pad(sg_b_s[0].T, ((0, 0), (0, 128 - SG_HEADS)))
    cw = jnp.pad(conv_w, ((0, 8 - 3), (0, 0)))
    pw_bd = jax.scipy.linalg.block_diag(*[pool_w[0, g] for g in range(4)]).astype(BF16)
    gq = jnp.pad(q_norm * ATT_SCALE, ((0, 0), (0, HP - QK_DIM)))
    gk = jnp.pad(k_norm, ((0, 0), (0, HP - QK_DIM)))

    (x1, proj_e), (w_gu0,) = _even_fwd(x0, w_mix0, mix_norm[0:1], sg_ln_g, ws, bst, cw, seq, _gather_comm([my_gu0]))
    (g0, u0, act0), (w_d0, w_d1) = _ffn_up(x1, w_gu0, ffn_norm[0:1], "ffn_up0", _gather_comm([my_d0, my_d1]))
    (x2,), (w_mix1,) = _ffn_down(x1, act0, w_d0, "ffn_down0", _gather_comm([my_mix1]))
    qbt = _pad_heads(w_mix1[:, OFF_QB:OFF_QB + N_QB_USED, :].reshape(HEADS * QK_DIM, Q_LORA))
    kvbt = w_mix1[:, OFF_KVB:OFF_KVB + N_KVB, :].reshape(HEADS * HP, KV_LORA)
    proj_o, q, kv, kr, c_out = _odd_pre_fwd(x2, w_mix1, mix_norm[1:2], qbt, kvbt, q_a_norm, kv_a_norm, pw_bd,
                                            pool_scale, seq)
    (d_out, lse), (w_gu1,) = _attn_fwd(q, kv, kr, cos, sin, gq, gk, seq, _gather_comm([my_gu1]))
    x3 = _odd_post_fwd(x2, c_out, d_out, w_mix1)
    (dy, g1, u1, loss_tile), _ = _ffn_fwd(x3, w_gu1, w_d1, ffn_norm[1:2], "ffn_fwd1", None, target)

    def chunk(rows, padded=False):
        return jnp.zeros((N_DEV, rows, D), BF16) if padded else lax.empty((N_DEV, rows, D), BF16)

    (dx3, act1, dg1, du1, h3, dgam_f1), _ = _ffn_bwd(x3, g1, u1, dy, w_gu1, w_d1, ffn_norm[1:2], "ffn_bwd1")
    gp_ffn1 = _tn([dg1], h3, 1408, "dw_gate1", (chunk(R_GU + N_FF), N_FF, OFF_GATE))
    gp_ffn1 = _tn([du1], h3, 1408, "dw_up1", (gp_ffn1, N_FF, OFF_UP))
    gp_ffn1 = _tn([act1], dy, 1408, "dw_down1", (gp_ffn1, N_FF, R_GU))

    dmix_o, (ga_ffn1,) = _odd_post_bwd(dx3, w_mix1, _pair_exchange_comm(gp_ffn1))
    pb_ffn1 = _rs_pair_sum(gp_ffn1, ga_ffn1, c_arr, "rs_pair_sum_ffn1")
    gp_mix1 = _tn([c_out, d_out], dx3, D, "dw_oout", (chunk(R_MIX1, True), N_SQ, OFF_OOUT))
    (dq, dkv, dkr, dgq, dgk), (gb_ffn1,) = _attn_bwd(q, kv, kr, cos, sin, gq, gk, dmix_o, d_out, lse, seq,
                                                    _chip_exchange_comm(pb_ffn1))
    gsh_ffn1 = _rs_final_sum(pb_ffn1, gb_ffn1, chip_arr, "rs_final_sum_ffn1")
    (dx2, dproj_o, h2, qn, kvn, dgam_m1, dqa, dkva, dpw_bd, dps) = _odd_pre_bwd(
        x2, proj_o, dx3, dmix_o, dq, dkv, dkr, w_mix1, mix_norm[1:2], qbt, kvbt, q_a_norm, kv_a_norm, pw_bd,
        pool_scale, seq)
    gp_mix1 = _tn([h2], dproj_o, D, "dw_oin", (gp_mix1, N_SQ, OFF_OIN))
    d_qbt = _tn([dq], qn, HEADS * HP, "dw_qb")
    d_qb_rows = d_qbt.reshape(HEADS, HP, Q_LORA)[:, :QK_DIM].reshape(N_DEV, N_QB_USED, D)
    d_kvb_rows = _tn([dkv], kvn, HEADS * HP, "dw_kvb").reshape(N_DEV, N_KVB, D)
    gp_mix1 = lax.dynamic_update_slice(gp_mix1, d_qb_rows, (0, OFF_QB, 0))
    gp_mix1 = lax.dynamic_update_slice(gp_mix1, d_kvb_rows, (0, OFF_KVB, 0))

    (dx1, act0, dg0, du0, h1, dgam_f0), (ga_mix1,) = _ffn_bwd(x1, g0, u0, dx2, w_gu0, w_d0, ffn_norm[0:1], "ffn_bwd0",
                                                             _pair_exchange_comm(gp_mix1))
    pb_mix1 = _rs_pair_sum(gp_mix1, ga_mix1, c_arr, "rs_pair_sum_mix1")
    gp_ffn0a, (gb_mix1,) = _tn([dg0], h1, 1408, "dw_gate0", (chunk(R_GU), N_FF, OFF_GATE),
                               _chip_exchange_comm(pb_mix1))
    gsh_mix1 = _rs_final_sum(pb_mix1, gb_mix1, chip_arr, "rs_final_sum_mix1")
    gp_ffn0a = _tn([du0], h1, 1408, "dw_up0", (gp_ffn0a, N_FF, OFF_UP))
    gp_ffn0b, (ga_ffn0a,) = _tn([act0], dx2, 1408, "dw_down0", (chunk(N_FF), N_FF, 0),
                                _pair_exchange_comm(gp_ffn0a))
    pb_ffn0a = _rs_pair_sum(gp_ffn0a, ga_ffn0a, c_arr, "rs_pair_sum_ffn0a")

    (dx0, dproj_e, mix_e, h0, dgam_m0, dws, dbc, dlng, dcw), (gb_ffn0a, ga_ffn0b) = _even_bwd(
        x0, proj_e, dx1, w_mix0, mix_norm[0:1], sg_ln_g, ws, bst, cw, seq,
        _both(_chip_exchange_comm(pb_ffn0a), _pair_exchange_comm(gp_ffn0b)))
    pb_ffn0b = _rs_pair_sum(gp_ffn0b, ga_ffn0b, c_arr, "rs_pair_sum_ffn0b")
    gsh_ffn0a = _rs_final_sum(pb_ffn0a, gb_ffn0a, chip_arr, "rs_final_sum_ffn0a")

    small = _small_pack([
        jnp.concatenate([dgam_m0, dgam_m1], 0), jnp.concatenate([dgam_f0, dgam_f1], 0), dlng,
        dws[None], dbc[:, :SG_HEADS].T[None], dcw[:3],
        jnp.stack([dpw_bd[g * POOL_GD:(g + 1) * POOL_GD, g * POOL_GD:(g + 1) * POOL_GD] for g in range(4)])[None],
        dps, dqa, dkva, dgq[:, :QK_DIM] * ATT_SCALE, dgk[:, :QK_DIM], loss_tile[0:1, 0:1]])
    gp_mix0, (small_all,) = _tn([mix_e], dx1, D, "dw_eout", (chunk(R_MIX0, True), N_SQ, OFF_EOUT),
                                _gather_comm([small]))
    gp_mix0, (gb_ffn0b,) = _tn([dproj_e], h0, 1280, "dw_ein", (gp_mix0, N_EIN, OFF_EIN),
                               _chip_exchange_comm(pb_ffn0b))
    gsh_ffn0b = _rs_final_sum(pb_ffn0b, gb_ffn0b, chip_arr, "rs_final_sum_ffn0b")
    small_sum = _small_unpack(_sum_gathered(small_all), SMALL_SHAPES)
    return dx0.reshape(bsz, seq, D), (gsh_ffn0a, gsh_ffn0b, gsh_mix1, gsh_ffn1), gp_mix0, small_sum


SMALL_SHAPES = [(2, D), (2, D), (1, SG_W), (1, SG_HEADS, 128, 128), (1, SG_HEADS, 128), (3, SC_W),
                (1, 4, POOL_GD, POOL_GD), (1, POOL_W), (1, Q_LORA), (1, KV_LORA), (1, QK_DIM), (1, QK_DIM), (1, 1)]


def kernel(x, positions, mix_norm, ffn_norm, even_w_in, sg_ln_g, sg_w_s, sg_b_s, sc_conv_w, even_w_out, odd_w_in, pool_w, pool_scale, q_a_norm, q_b, kv_a_norm, kv_b, q_norm, k_norm, odd_w_out, ffn_w_gate, ffn_w_up, ffn_w_down, loss_target, m_mix_norm, m_ffn_norm, m_even_w_in, m_sg_ln_g, m_sg_w_s, m_sg_b_s, m_sc_conv_w, m_even_w_out, m_odd_w_in, m_pool_w, m_pool_scale, m_q_a_norm, m_q_b, m_kv_a_norm, m_kv_b, m_q_norm, m_k_norm, m_odd_w_out, m_ffn_w_gate, m_ffn_w_up, m_ffn_w_down, v_mix_norm, v_ffn_norm, v_even_w_in, v_sg_ln_g, v_sg_w_s, v_sg_b_s, v_sc_conv_w, v_even_w_out, v_odd_w_in, v_pool_w, v_pool_scale, v_q_a_norm, v_q_b, v_kv_a_norm, v_kv_b, v_q_norm, v_k_norm, v_odd_w_out, v_ffn_w_gate, v_ffn_w_up, v_ffn_w_down):
    xi, yi, ci = _place()
    me = 4 * xi + 2 * yi + ci

    chunks = _pack_shards(even_w_in, even_w_out, odd_w_in, q_b, kv_b, odd_w_out, ffn_w_gate, ffn_w_up, ffn_w_down)

    def lane_pad(a):
        return jnp.pad(a, ((0, 0), (0, 128 - a.shape[1])))

    tile = jnp.concatenate([lane_pad(sc_conv_w[0]), lane_pad(pool_scale), lane_pad(q_a_norm), lane_pad(kv_a_norm),
                            jnp.zeros((2, 128), F32)], axis=0)
    c_arr = jnp.reshape(ci, (1,)).astype(jnp.int32)
    chip_arr = jnp.reshape(2 * xi + yi, (1,)).astype(jnp.int32)
    grad_x, (gsh_ffn0a, gsh_ffn0b, gsh_mix1, gsh_ffn1), gp_mix0, tot = _step(
        x, positions, loss_target, chunks, tile, c_arr, chip_arr, mix_norm, ffn_norm, sg_ln_g, sg_w_s, sg_b_s,
        pool_w, q_norm, k_norm)

    (g_mix, g_ffn, g_lng, g_ws, g_bs, g_cw_full, g_pw, g_ps_full, g_qa_full, g_kva_full, g_qn, g_kn, loss) = tot
    g_cw = lax.dynamic_slice_in_dim(g_cw_full, me * 64, 64, axis=1)[None]
    g_ps = lax.dynamic_slice_in_dim(g_ps_full, me * 32, 32, axis=1)
    g_qa = lax.dynamic_slice_in_dim(g_qa_full, me * 48, 48, axis=1)
    g_kva = lax.dynamic_slice_in_dim(g_kva_full, me * 32, 32, axis=1)

    def tr(a):
        return jnp.swapaxes(a, -1, -2)

    g_gate = tr(jnp.stack([gsh_ffn0a[OFF_GATE:OFF_GATE + N_FF], gsh_ffn1[OFF_GATE:OFF_GATE + N_FF]]))
    g_up = tr(jnp.stack([gsh_ffn0a[OFF_UP:OFF_UP + N_FF], gsh_ffn1[OFF_UP:OFF_UP + N_FF]]))
    g_down = jnp.stack([gsh_ffn0b, gsh_ffn1[R_GU:R_GU + N_FF]])
    g_oin = gsh_mix1[OFF_OIN:OFF_OIN + N_SQ, :ODD_IN][None]
    g_oout = gsh_mix1[OFF_OOUT:OFF_OOUT + N_SQ][None]
    g_qb = tr(gsh_mix1[OFF_QB:OFF_QB + N_QB_USED].reshape(1, 144, Q_LORA))
    g_kvb = tr(gsh_mix1[OFF_KVB:OFF_KVB + N_KVB].reshape(1, 192, KV_LORA))
    transposed = ("even_w_in", "odd_w_in", "q_b", "kv_b", "ffn_w_gate", "ffn_w_up")

    names = ("mix_norm", "ffn_norm", "even_w_in", "sg_ln_g", "sg_w_s", "sg_b_s", "sc_conv_w", "even_w_out",
             "odd_w_in", "pool_w", "pool_scale", "q_a_norm", "q_b", "kv_a_norm", "kv_b", "q_norm", "k_norm",
             "odd_w_out", "ffn_w_gate", "ffn_w_up", "ffn_w_down")
    grads = dict(mix_norm=g_mix, ffn_norm=g_ffn, sg_ln_g=g_lng, sg_w_s=g_ws, sg_b_s=g_bs,
                 sc_conv_w=g_cw, odd_w_in=g_oin, pool_w=g_pw, pool_scale=g_ps, q_a_norm=g_qa,
                 q_b=g_qb, kv_a_norm=g_kva, kv_b=g_kvb, q_norm=g_qn, k_norm=g_kn, odd_w_out=g_oout,
                 ffn_w_gate=g_gate, ffn_w_up=g_up, ffn_w_down=g_down)
    weights = dict(mix_norm=mix_norm, ffn_norm=ffn_norm, even_w_in=even_w_in, sg_ln_g=sg_ln_g, sg_w_s=sg_w_s,
                   sg_b_s=sg_b_s, sc_conv_w=sc_conv_w, even_w_out=even_w_out, odd_w_in=odd_w_in, pool_w=pool_w,
                   pool_scale=pool_scale, q_a_norm=q_a_norm, q_b=q_b, kv_a_norm=kv_a_norm, kv_b=kv_b, q_norm=q_norm,
                   k_norm=k_norm, odd_w_out=odd_w_out, ffn_w_gate=ffn_w_gate, ffn_w_up=ffn_w_up,
                   ffn_w_down=ffn_w_down)
    m_in = dict(mix_norm=m_mix_norm, ffn_norm=m_ffn_norm, even_w_in=m_even_w_in, sg_ln_g=m_sg_ln_g, sg_w_s=m_sg_w_s,
                sg_b_s=m_sg_b_s, sc_conv_w=m_sc_conv_w, even_w_out=m_even_w_out, odd_w_in=m_odd_w_in,
                pool_w=m_pool_w, pool_scale=m_pool_scale, q_a_norm=m_q_a_norm, q_b=m_q_b, kv_a_norm=m_kv_a_norm,
                kv_b=m_kv_b, q_norm=m_q_norm, k_norm=m_k_norm, odd_w_out=m_odd_w_out, ffn_w_gate=m_ffn_w_gate,
                ffn_w_up=m_ffn_w_up, ffn_w_down=m_ffn_w_down)
    v_in = dict(mix_norm=v_mix_norm, ffn_norm=v_ffn_norm, even_w_in=v_even_w_in, sg_ln_g=v_sg_ln_g, sg_w_s=v_sg_w_s,
                sg_b_s=v_sg_b_s, sc_conv_w=v_sc_conv_w, even_w_out=v_even_w_out, odd_w_in=v_odd_w_in,
                pool_w=v_pool_w, pool_scale=v_pool_scale, q_a_norm=v_q_a_norm, q_b=v_q_b, kv_a_norm=v_kv_a_norm,
                kv_b=v_kv_b, q_norm=v_q_norm, k_norm=v_k_norm, odd_w_out=v_odd_w_out, ffn_w_gate=v_ffn_w_gate,
                ffn_w_up=v_ffn_w_up, ffn_w_down=v_ffn_w_down)
    delta, new_m, new_v = {}, {}, {}

    def as2d(k, a):
        a = tr(a) if k in transposed else a
        return a.reshape(-1, a.shape[-1])

    def back(k, a):
        shape = weights[k].shape
        return tr(a.reshape(shape[:-2] + (shape[-1], shape[-2]))) if k in transposed else a.reshape(shape)

    def update(group, name, nblk=1):
        outs = _adamw([as2d(k, weights[k]) for k in group], [as2d(k, grads[k]) for k in group],
                      [as2d(k, m_in[k]) for k in group], [as2d(k, v_in[k]) for k in group], name, nblk)
        for i, k in enumerate(group):
            delta[k], new_m[k], new_v[k] = (back(k, o[i]) for o in outs)

    (ga_mix0,) = _comm_alone(_pair_exchange_comm(gp_mix0), "rs_pair_exchange_mix0")
    pb_mix0 = _rs_pair_sum(gp_mix0, ga_mix0, c_arr, "rs_pair_sum_mix0")
    (gb_mix0,) = _comm_alone(_chip_exchange_comm(pb_mix0), "rs_chip_exchange_mix0")
    gsh_mix0 = _rs_final_sum(pb_mix0, gb_mix0, chip_arr, "rs_final_sum_mix0")
    grads["even_w_in"] = tr(gsh_mix0[OFF_EIN:OFF_EIN + N_EIN][None])
    grads["even_w_out"] = gsh_mix0[OFF_EOUT:OFF_EOUT + N_SQ][None]

    update(["ffn_w_gate", "ffn_w_up", "ffn_w_down"], "adamw_ffn", 4)
    update(["even_w_in", "even_w_out", "odd_w_in", "odd_w_out"], "adamw_mix", 2)
    update([k for k in names if k not in delta], "adamw_small")

    return (loss.reshape(()), grad_x, *[grads[k] for k in names], *[delta[k] for k in names],
            *[new_m[k] for k in names], *[new_v[k] for k in names])
```

```python
import functools

import numpy as np
import jax
import jax.numpy as jnp
from jax import lax
from jax.experimental import pallas as pl
from jax.experimental.pallas import tpu as pltpu

F32 = jnp.float32
BF16 = jnp.bfloat16
MESH = pl.DeviceIdType.MESH

D = 1024
EPS = 1e-6
NEG_INF = -1e30
SG_HEADS, SG_HD, SG_W, SG_CHUNK = 4, 128, 512, 128
SC_W = 512
EVEN_IN = 2560
POOL_W = 256
POOL_GD = 64
Q_LORA, KV_LORA, QK_ROPE, QK_NOPE, V_DIM = 384, 256, 64, 128, 128
QK_DIM = QK_NOPE + QK_ROPE
HEADS = 6
HP = 256
ODD_IN = 960
D_FF = 2816
ROPE_THETA = 10000.0
ATT_SCALE = QK_DIM ** -0.5
LR, B1, B2, ADAM_EPS, WD, STEP = 0.001, 0.9, 0.999, 1e-08, 0.01, 10

N_DEV = 8
TB = 512
TB_FFN_BWD = 256
TK_DW = 1024
HALO = 16
VMEM_LIMIT = 56 * 1024 * 1024

N_EIN, N_FF, N_SQ = 320, 352, 128
OFF_EIN, OFF_EOUT, R_MIX0 = 0, 384, 512
OFF_GATE, OFF_UP, R_GU = 0, 352, 704
OFF_OIN, OFF_OOUT, OFF_QB, OFF_KVB, R_MIX1 = 0, 128, 256, 320, 384
N_QB, N_QB_USED, N_KVB = 64, 54, 48

INV_SQRT2 = 0.7071067811865476
INV_SQRT_2PI = 0.3989422804014327


def _dot(a, b, ca, cb):
    return lax.dot_general(a, b, (((ca,), (cb,)), ((), ())), preferred_element_type=F32)


def _cparams(n_axes=1):
    return pltpu.CompilerParams(dimension_semantics=("arbitrary",) * n_axes, vmem_limit_bytes=VMEM_LIMIT)


def _wspec(n, off, arity=1):
    assert off % n == 0
    idx = off // n
    if arity == 1:
        return pl.BlockSpec((N_DEV, n, D), lambda i: (0, idx, 0), pipeline_mode=pl.Buffered(1))
    return pl.BlockSpec((N_DEV, n, D), lambda i, j: (0, idx, 0), pipeline_mode=pl.Buffered(1))


def _const_spec(shape):
    zeros = (0,) * len(shape)
    return pl.BlockSpec(shape, lambda *_: zeros)


class _Comm:
    def __init__(self, ins, out_shapes, sems, start, wait, mid=None):
        self.ins, self.out_shapes, self.sems, self.start, self.wait, self.mid = ins, out_shapes, sems, start, wait, mid


def _both(c1, c2):
    def split(f1, f2):
        def run(ins, outs, sems):
            f1(ins[:len(c1.ins)], outs[:len(c1.out_shapes)], sems[:len(c1.sems)])
            f2(ins[len(c1.ins):], outs[len(c1.out_shapes):], sems[len(c1.sems):])
        return run

    assert c1.mid is None and c2.mid is None
    return _Comm(c1.ins + c2.ins, c1.out_shapes + c2.out_shapes, c1.sems + c2.sems,
                 split(c1.start, c2.start), split(c1.wait, c2.wait))


def _call(body, name, grid, in_specs, out_specs, out_shape, args, scratch_shapes=(), comm=None, aliases=None):
    n_axes = len(grid)
    aliases = aliases or {}
    if comm is None:
        res = pl.pallas_call(
            body, name=name, grid=grid, in_specs=list(in_specs), out_specs=list(out_specs),
            out_shape=list(out_shape), scratch_shapes=list(scratch_shapes), input_output_aliases=aliases,
            compiler_params=_cparams(n_axes))(*args)
        return list(res), []
    ni, no, ns = len(in_specs), len(out_specs), len(scratch_shapes)
    ci, co = len(comm.ins), len(comm.out_shapes)
    n_steps = int(np.prod(grid))

    def carrier(*refs):
        ins, cin = refs[:ni], refs[ni:ni + ci]
        outs, cout = refs[ni + ci:ni + ci + no], refs[ni + ci + no:ni + ci + no + co]
        scr, sems = refs[ni + ci + no + co:ni + ci + no + co + ns], refs[ni + ci + no + co + ns:]
        step = 0
        for a in range(n_axes):
            step = step * grid[a] + pl.program_id(a)

        @pl.when(step == 0)
        def _():
            comm.start(cin, cout, sems)

        body(*ins, *outs, *scr)

        if comm.mid is not None and n_steps >= 4:
            @pl.when(step == (3 * n_steps) // 4)
            def _():
                comm.mid(cin, cout, sems)

        @pl.when(step == n_steps - 1)
        def _():
            if comm.mid is not None and n_steps < 4:
                comm.mid(cin, cout, sems)
            comm.wait(cin, cout, sems)

    any_spec = pl.BlockSpec(memory_space=pl.ANY)
    res = pl.pallas_call(
        carrier, name=name, grid=grid, in_specs=list(in_specs) + [any_spec] * ci,
        out_specs=list(out_specs) + [any_spec] * co, out_shape=list(out_shape) + list(comm.out_shapes),
        scratch_shapes=list(scratch_shapes) + list(comm.sems), input_output_aliases=aliases,
        compiler_params=_cparams(n_axes))(*args, *comm.ins)
    return list(res[:no]), list(res[no:])


def _comm_alone(comm, name):
    ci, co = len(comm.ins), len(comm.out_shapes)

    def body(*refs):
        cin, cout, sems = refs[:ci], refs[ci:ci + co], refs[ci + co:]
        comm.start(cin, cout, sems)
        if comm.mid is not None:
            comm.mid(cin, cout, sems)
        comm.wait(cin, cout, sems)

    any_spec = pl.BlockSpec(memory_space=pl.ANY)
    res = pl.pallas_call(
        body, name=name, out_shape=list(comm.out_shapes), in_specs=[any_spec] * ci, out_specs=[any_spec] * co,
        scratch_shapes=list(comm.sems))(*comm.ins)
    return list(res)


def _rms(x, g):
    r = lax.rsqrt(jnp.mean(x * x, axis=-1, keepdims=True) + EPS)
    return x * r * g, r


def _rms_bwd(x, r, g, dy):
    xh = x * r
    dxh = dy * g
    dx = r * (dxh - xh * jnp.mean(dxh * xh, axis=-1, keepdims=True))
    dg = jnp.sum(dy * xh, axis=0, keepdims=True)
    return dx, dg


def _gelu(x):
    return 0.5 * x * (1.0 + lax.erf(x * INV_SQRT2))


def _gelu_grad(x):
    return 0.5 * (1.0 + lax.erf(x * INV_SQRT2)) + x * jnp.exp(-0.5 * x * x) * INV_SQRT_2PI


def _shift_down(a, k):
    rows = lax.broadcasted_iota(jnp.int32, a.shape, 0)
    return jnp.where(rows >= k, pltpu.roll(a, k, 0), 0.0)


def _shift_up(a, k):
    n = a.shape[0]
    rows = lax.broadcasted_iota(jnp.int32, a.shape, 0)
    return jnp.where(rows < n - k, pltpu.roll(a, n - k, 0), 0.0)


def _tril_bf16(w):
    r = lax.broadcasted_iota(jnp.int32, w.shape, 0)
    c = lax.broadcasted_iota(jnp.int32, w.shape, 1)
    return jnp.where(r >= c, w, 0.0).astype(BF16)


def _ln_head(vh, g):
    mu = jnp.mean(vh, axis=-1, keepdims=True)
    xc = vh - mu
    rr = lax.rsqrt(jnp.mean(xc * xc, axis=-1, keepdims=True) + EPS)
    xh = xc * rr
    return xh * g, xh, rr


def _conv_fwd(z, tail, cw_ref):
    ext = jnp.concatenate([tail, z], axis=0)
    zs1 = _shift_down(ext, 1)[HALO:]
    zs2 = _shift_down(ext, 2)[HALO:]
    y = cw_ref[2:3, :] * z + cw_ref[1:2, :] * zs1 + cw_ref[0:1, :] * zs2
    return y, zs1, zs2


def _pool_cnt(shape, blk_in_seq):
    rows = lax.broadcasted_iota(jnp.int32, shape, 0)
    grp = lax.broadcasted_iota(jnp.int32, shape, 1) // POOL_GD
    win = jnp.where(grp == 0, 2, jnp.where(grp == 1, 4, jnp.where(grp == 2, 8, 16)))
    tpos = blk_in_seq * shape[0] + rows + 1
    return jnp.minimum(tpos, win).astype(F32), grp


def _pool_select(grp, s2, s4, s8, s16):
    return jnp.where(grp == 0, s2, jnp.where(grp == 1, s4, jnp.where(grp == 2, s8, s16)))


def _pool_fwd(z, tail, blk_in_seq):
    ext = jnp.concatenate([tail, z], axis=0)
    s2 = ext + _shift_down(ext, 1)
    s4 = s2 + _shift_down(s2, 2)
    s8 = s4 + _shift_down(s4, 4)
    s16 = s8 + _shift_down(s8, 8)
    cnt, grp = _pool_cnt(z.shape, blk_in_seq)
    sums = _pool_select(grp, s2[HALO:], s4[HALO:], s8[HALO:], s16[HALO:])
    return sums / cnt - z, cnt, grp


def _pool_bwd(dpooled, dpm, head, grp):
    n = dpm.shape[0]
    ext = jnp.concatenate([dpm, head], axis=0)
    u2 = ext + _shift_up(ext, 1)
    u4 = u2 + _shift_up(u2, 2)
    u8 = u4 + _shift_up(u4, 4)
    u16 = u8 + _shift_up(u8, 8)
    return _pool_select(grp, u2[:n], u4[:n], u8[:n], u16[:n]) - dpooled


def _lane_sums(a):
    return _dot(a.astype(BF16), jnp.ones((a.shape[1], a.shape[1]), BF16), 1, 0)


def _swap_halves(y1):
    src = lax.broadcasted_iota(jnp.int32, (128, 128), 0)
    dst = lax.broadcasted_iota(jnp.int32, (128, 128), 1)
    perm = jnp.where(((dst < 32) & (src == dst + 32)) | ((dst >= 32) & (dst < QK_ROPE) & (src == dst - 32)), 1.0, 0.0)
    return _dot(y1.astype(BF16), perm.astype(BF16), 1, 0)


def _rope(y1, c, s):
    return y1 * c + _swap_halves(y1) * s


def _rope_bwd(d1, c, s):
    return d1 * c + _swap_halves(d1 * s)


def _qk_prep(x, g, c, s):
    r = lax.rsqrt(_lane_sums(x * x) * (1.0 / QK_DIM) + EPS)
    y = x * r * g
    return jnp.concatenate([y[:, :128], _rope(y[:, 128:], c, s)], axis=1), r


def _qk_prep_bwd(dout, x, r, g, c, s):
    dy = jnp.concatenate([dout[:, :128], _rope_bwd(dout[:, 128:], c, s)], axis=1)
    xh = x * r
    dxh = dy * g
    dx = r * (dxh - xh * (_lane_sums(dxh * xh) * (1.0 / QK_DIM)))
    return dx, jnp.sum(dy * xh, axis=0, keepdims=True)


def _place():
    return lax.axis_index("x"), lax.axis_index("y"), lax.axis_index("c")


def _gather_comm(arrs):
    n = len(arrs)

    def plan(ins, outs, sems):
        send_sems, recv_sems, local_sems = sems
        x, y, c = _place()
        me, sibling = (x, y, c), (x, y, 1 - c)
        chips = [(1 - x, y), (x, 1 - y), (1 - x, 1 - y)]

        def slot(a, px, py, pc):
            return outs[a].at[4 * px + 2 * py + pc]

        def copy(a, k, block, to, src=None):
            return pltpu.make_async_remote_copy(
                src_ref=slot(a, *block) if src is None else src, dst_ref=slot(a, *block),
                send_sem=send_sems.at[a, k], recv_sem=recv_sems.at[a, k], device_id=to, device_id_type=MESH)

        def own():
            mine = [pltpu.make_async_copy(ins[a], slot(a, *me), local_sems.at[a]) for a in range(n)]
            first = []
            for a in range(n):
                first.append(copy(a, 0, me, sibling, src=ins[a]))
                first += [copy(a, 1 + j, me, (*chip, c), src=ins[a]) for j, chip in enumerate(chips)]
            return mine, first

        return c, me, sibling, chips, copy, own

    def start(ins, outs, sems):
        mine, first = plan(ins, outs, sems)[-1]()
        for cp in mine + first:
            cp.start()

    def mid(ins, outs, sems):
        c, me, sibling, chips, copy, _ = plan(ins, outs, sems)
        for j, chip in enumerate(chips):
            for a in range(n):
                copy(a, 1 + j, (*chip, c), me).wait_recv()
                copy(a, 4 + j, (*chip, c), sibling).start()

    def wait(ins, outs, sems):
        c, me, sibling, chips, copy, own = plan(ins, outs, sems)
        mine, first = own()
        passed = [copy(a, 4 + j, (*chip, c), sibling) for j, chip in enumerate(chips) for a in range(n)]
        for a in range(n):
            copy(a, 0, sibling, me).wait_recv()
            for j, chip in enumerate(chips):
                copy(a, 4 + j, (*chip, 1 - c), me).wait_recv()
        for cp in first + passed:
            cp.wait_send()
        for cp in mine:
            cp.wait()

    return _Comm(
        list(arrs), [jax.ShapeDtypeStruct((N_DEV,) + a.shape, a.dtype) for a in arrs],
        [pltpu.SemaphoreType.DMA((n, 7)), pltpu.SemaphoreType.DMA((n, 7)), pltpu.SemaphoreType.DMA((n,))],
        start, wait, mid)


def _sum_gathered(g):
    rows = g.shape[1]

    def body(g_ref, sum_ref):
        total = g_ref[0]
        for d in range(1, N_DEV):
            total = total + g_ref[d]
        sum_ref[...] = total

    return pl.pallas_call(
        body, name="sum_gathered_small", out_shape=jax.ShapeDtypeStruct((rows, 128), F32), grid=(1,),
        in_specs=[pl.BlockSpec((N_DEV, rows, 128), lambda i: (0, 0, 0))],
        out_specs=pl.BlockSpec((rows, 128), lambda i: (0, 0)), compiler_params=_cparams(1),
    )(g)


def _sum_rows(rows):
    return rows if rows <= 512 else rows // 2


def _pair_exchange_comm(gp):
    _, rows, cols = gp.shape

    def copies(ins, outs, sems):
        send_sems, recv_sems = sems
        x, y, c = _place()
        return [pltpu.make_async_remote_copy(
            src_ref=ins[0].at[2 * j + (1 - c)], dst_ref=outs[0].at[j], send_sem=send_sems.at[j],
            recv_sem=recv_sems.at[j], device_id=(x, y, 1 - c), device_id_type=MESH) for j in range(4)]

    def start(ins, outs, sems):
        for cp in copies(ins, outs, sems):
            cp.start()

    def wait(ins, outs, sems):
        for cp in copies(ins, outs, sems):
            cp.wait()

    return _Comm([gp], [jax.ShapeDtypeStruct((4, rows, cols), gp.dtype)],
                 [pltpu.SemaphoreType.DMA((4,)), pltpu.SemaphoreType.DMA((4,))], start, wait)


def _rs_pair_sum(gp, got, c_arr, name):
    _, rows, cols = got.shape
    rb = _sum_rows(rows)
    gp4 = gp.reshape(4, 2, rows, cols)

    def body(c_ref, a_ref, b_ref, o_ref):
        o_ref[0] = (a_ref[0, 0].astype(F32) + b_ref[0].astype(F32)).astype(o_ref.dtype)

    return pl.pallas_call(
        body, name=name, out_shape=jax.ShapeDtypeStruct((4, rows, cols), gp.dtype),
        grid_spec=pltpu.PrefetchScalarGridSpec(
            num_scalar_prefetch=1, grid=(4, rows // rb),
            in_specs=[pl.BlockSpec((1, 1, rb, cols), lambda j, r, cr: (j, cr[0], r, 0)),
                      pl.BlockSpec((1, rb, cols), lambda j, r, cr: (j, r, 0))],
            out_specs=pl.BlockSpec((1, rb, cols), lambda j, r, cr: (j, r, 0))),
        compiler_params=_cparams(2),
    )(c_arr, gp4, got)


def _chip_exchange_comm(pb):
    _, rows, cols = pb.shape

    def copies(ins, outs, sems):
        send_sems, recv_sems = sems
        x, y, c = _place()
        chips = [(1 - x, y), (x, 1 - y), (1 - x, 1 - y)]
        return [pltpu.make_async_remote_copy(
            src_ref=ins[0].at[2 * px + py], dst_ref=outs[0].at[k], send_sem=send_sems.at[k],
            recv_sem=recv_sems.at[k], device_id=(px, py, c), device_id_type=MESH)
            for k, (px, py) in enumerate(chips)]

    def start(ins, outs, sems):
        for cp in copies(ins, outs, sems):
            cp.start()

    def wait(ins, outs, sems):
        for cp in copies(ins, outs, sems):
            cp.wait()

    return _Comm([pb], [jax.ShapeDtypeStruct((3, rows, cols), pb.dtype)],
                 [pltpu.SemaphoreType.DMA((3,)), pltpu.SemaphoreType.DMA((3,))], start, wait)


def _rs_final_sum(pb, got, chip_arr, name):
    _, rows, cols = got.shape
    rb = _sum_rows(rows)

    def body(j_ref, a_ref, b_ref, o_ref):
        o_ref[...] = ((a_ref[0].astype(F32) + b_ref[0].astype(F32)) + b_ref[1].astype(F32)) + b_ref[2].astype(F32)

    return pl.pallas_call(
        body, name=name, out_shape=jax.ShapeDtypeStruct((rows, cols), F32),
        grid_spec=pltpu.PrefetchScalarGridSpec(
            num_scalar_prefetch=1, grid=(rows // rb,),
            in_specs=[pl.BlockSpec((1, rb, cols), lambda r, jr: (jr[0], r, 0)),
                      pl.BlockSpec((3, rb, cols), lambda r, jr: (0, r, 0))],
            out_specs=pl.BlockSpec((rb, cols), lambda r, jr: (r, 0))),
        compiler_params=_cparams(1),
    )(chip_arr, pb, got)


def _rope_tables(pos_col, inv_freq, comm=None):
    t = pos_col.shape[0]

    def body(p_ref, f_ref, c_ref, s_ref):
        ang = p_ref[...].astype(F32) * f_ref[...]
        lane = lax.broadcasted_iota(jnp.int32, ang.shape, 1)
        c_ref[...] = jnp.where(lane < QK_ROPE, jnp.cos(ang), 0.0)
        s = jnp.sin(ang)
        s_ref[...] = jnp.where(lane < 32, -s, jnp.where(lane < QK_ROPE, s, 0.0))

    spec = pl.BlockSpec((TB, 128), lambda i: (i, 0))
    return _call(
        body, "rope_tables", (t // TB,), [pl.BlockSpec((TB, 1), lambda i: (i, 0)), _const_spec((1, 128))],
        [spec] * 2, [jax.ShapeDtypeStruct((t, 128), F32)] * 2, (pos_col, inv_freq), (), comm)


def _sgu_conv_fwd(proj, tail, lng_ref, ws_ref, bst_ref, cw_ref):
    gu = _gelu(proj[:, 0:SG_W])
    gv = _gelu(proj[:, SG_W:2 * SG_W])
    bg = proj[:, 1024:1536]
    z = proj[:, 1536:2048] * proj[:, 2048:2560]
    heads = []
    for h in range(SG_HEADS):
        sl = slice(h * SG_HD, (h + 1) * SG_HD)
        vn, _, _ = _ln_head(gv[:, sl], lng_ref[:, sl])
        vnb = vn.astype(BF16)
        wm = _tril_bf16(ws_ref[h])
        bcol = bst_ref[:, h:h + 1]
        mixed = jnp.concatenate(
            [_dot(wm, vnb[k * SG_CHUNK:(k + 1) * SG_CHUNK], 1, 0) + bcol for k in range(TB // SG_CHUNK)], axis=0)
        heads.append(gu[:, sl] * mixed)
    a_out = jnp.concatenate(heads, axis=1)
    y, _, _ = _conv_fwd(z, tail, cw_ref)
    return a_out, bg * y, z


def _even_fwd(x, wg, gamma, lng, ws, bst, cw, seq, comm=None):
    t = x.shape[0]
    nbs = seq // TB

    def body(x_ref, gam_ref, win_ref, wout_ref, lng_ref, ws_ref, bst_ref, cw_ref, x1_ref, proj_ref, tail_ref):
        i = pl.program_id(0)
        xv = x_ref[...]
        h, _ = _rms(xv, gam_ref[...])
        proj = _dot(h.astype(BF16), win_ref[...].reshape(EVEN_IN, D), 1, 1)
        proj_ref[...] = proj.astype(BF16)
        tail = jnp.where(i % nbs == 0, 0.0, tail_ref[...])
        a_out, b_out, z = _sgu_conv_fwd(proj, tail, lng_ref, ws_ref, bst_ref, cw_ref)
        tail_ref[...] = z[TB - HALO:, :]
        x1_ref[...] = (xv + _dot(a_out.astype(BF16), wout_ref[0:4].reshape(512, D), 1, 0)
                       + _dot(b_out.astype(BF16), wout_ref[4:8].reshape(512, D), 1, 0))

    row = pl.BlockSpec((TB, D), lambda i: (i, 0))
    return _call(
        body, "even_fwd", (t // TB,),
        [row, _const_spec((1, D)), _wspec(N_EIN, OFF_EIN), _wspec(N_SQ, OFF_EOUT), _const_spec((1, SG_W)),
         _const_spec((SG_HEADS, 128, 128)), _const_spec((128, 128)), _const_spec((8, SC_W))],
        [row, pl.BlockSpec((TB, EVEN_IN), lambda i: (i, 0))],
        [jax.ShapeDtypeStruct((t, D), F32), jax.ShapeDtypeStruct((t, EVEN_IN), BF16)],
        (x, gamma, wg, wg, lng, ws, bst, cw), [pltpu.VMEM((HALO, SC_W), F32)], comm)


def _even_bwd(x, proj, dx1, wg, gamma, lng, ws, bst, cw, seq, comm=None):
    t = x.shape[0]
    nb, nbs = t // TB, seq // TB

    def body(x_ref, proj_ref, ptail_ref, dx1_ref, gam_ref, win_ref, wout_ref, lng_ref, ws_ref, bst_ref, cw_ref,
             dx0_ref, dproj_ref, mix_ref, h_ref, dgam_ref, dws_ref, dbc_ref, dlng_ref, dcw_ref, head_ref):
        i = pl.program_id(0)
        blk = nb - 1 - i

        @pl.when(i == 0)
        def _():
            dgam_ref[...] = jnp.zeros_like(dgam_ref)
            dws_ref[...] = jnp.zeros_like(dws_ref)
            dbc_ref[...] = jnp.zeros_like(dbc_ref)
            dlng_ref[...] = jnp.zeros_like(dlng_ref)
            dcw_ref[...] = jnp.zeros_like(dcw_ref)

        xv = x_ref[...]
        gam = gam_ref[...]
        h, r = _rms(xv, gam)
        h_ref[...] = h.astype(BF16)
        dx1 = dx1_ref[...]
        dmix = _dot(dx1.astype(BF16), wout_ref[...].reshape(D, D), 1, 1)
        da, db = dmix[:, :SG_W], dmix[:, SG_W:]
        proj = proj_ref[...].astype(F32)
        u, v = proj[:, 0:SG_W], proj[:, SG_W:2 * SG_W]
        bg, cg, hv = proj[:, 1024:1536], proj[:, 1536:2048], proj[:, 2048:2560]
        gu, gv = _gelu(u), _gelu(v)

        a_heads, dgv_heads = [], []
        for hd in range(SG_HEADS):
            sl = slice(hd * SG_HD, (hd + 1) * SG_HD)
            g_h = lng_ref[:, sl]
            vn, xh, rr = _ln_head(gv[:, sl], g_h)
            vnb = vn.astype(BF16)
            wm = _tril_bf16(ws_ref[hd])
            bcol = bst_ref[:, hd:hd + 1]
            mixed_c, dvn_c = [], []
            dw_acc = jnp.zeros((128, 128), F32)
            db_acc = jnp.zeros((128, 1), F32)
            for k in range(TB // SG_CHUNK):
                rs = slice(k * SG_CHUNK, (k + 1) * SG_CHUNK)
                mixed = _dot(wm, vnb[rs], 1, 0) + bcol
                dmixed = da[rs, sl] * gu[rs, sl]
                dmb = dmixed.astype(BF16)
                dvn_c.append(_dot(wm, dmb, 0, 0))
                dw_acc = dw_acc + _dot(dmb, vnb[rs], 1, 1)
                db_acc = db_acc + jnp.sum(dmixed, axis=1, keepdims=True)
                mixed_c.append(mixed)
            mixed_h = jnp.concatenate(mixed_c, axis=0)
            dvn = jnp.concatenate(dvn_c, axis=0)
            r_i = lax.broadcasted_iota(jnp.int32, (128, 128), 0)
            c_i = lax.broadcasted_iota(jnp.int32, (128, 128), 1)
            dws_ref[hd] += jnp.where(r_i >= c_i, dw_acc, 0.0)
            dbc_ref[:, hd:hd + 1] += db_acc
            dlng_ref[:, sl] += jnp.sum(dvn * xh, axis=0, keepdims=True)
            dxh = dvn * g_h
            dgv = rr * (dxh - jnp.mean(dxh, axis=-1, keepdims=True)
                        - xh * jnp.mean(dxh * xh, axis=-1, keepdims=True))
            a_heads.append(gu[:, sl] * mixed_h)
            dproj_ref[:, sl] = (da[:, sl] * mixed_h * _gelu_grad(u[:, sl])).astype(BF16)
            dgv_heads.append(dgv * _gelu_grad(v[:, sl]))
        dproj_ref[:, SG_W:2 * SG_W] = jnp.concatenate(dgv_heads, axis=1).astype(BF16)
        mix_ref[:, :SG_W] = jnp.concatenate(a_heads, axis=1).astype(BF16)

        z = cg * hv
        pt = ptail_ref[...].astype(F32)
        tail = jnp.where(blk % nbs == 0, 0.0, pt[:, 1536:2048] * pt[:, 2048:2560])
        y, zs1, zs2 = _conv_fwd(z, tail, cw_ref)
        mix_ref[:, SG_W:] = (bg * y).astype(BF16)
        dy = db * bg
        head = jnp.where(blk % nbs == nbs - 1, 0.0, head_ref[...])
        ext = jnp.concatenate([dy, head], axis=0)
        dz = (cw_ref[2:3, :] * dy + cw_ref[1:2, :] * _shift_up(ext, 1)[:TB]
              + cw_ref[0:1, :] * _shift_up(ext, 2)[:TB])
        head_ref[...] = dy[:HALO, :]
        dcw_ref[2:3, :] += jnp.sum(dy * z, axis=0, keepdims=True)
        dcw_ref[1:2, :] += jnp.sum(dy * zs1, axis=0, keepdims=True)
        dcw_ref[0:1, :] += jnp.sum(dy * zs2, axis=0, keepdims=True)
        dproj_ref[:, 1024:1536] = (db * y).astype(BF16)
        dproj_ref[:, 1536:2048] = (dz * hv).astype(BF16)
        dproj_ref[:, 2048:2560] = (dz * cg).astype(BF16)

        dh = _dot(dproj_ref[...], win_ref[...].reshape(EVEN_IN, D), 1, 0)
        dxn, dgam = _rms_bwd(xv, r, gam, dh)
        dgam_ref[...] += dgam
        dx0_ref[...] = dx1 + dxn

    def rev(w):
        return pl.BlockSpec((TB, w), lambda i: (nb - 1 - i, 0))

    ptail = pl.BlockSpec((HALO, EVEN_IN), lambda i: (jnp.maximum((nb - 1 - i) * (TB // HALO) - 1, 0), 0))
    return _call(
        body, "even_bwd", (nb,),
        [rev(D), rev(EVEN_IN), ptail, rev(D), _const_spec((1, D)), _wspec(N_EIN, OFF_EIN),
         _wspec(N_SQ, OFF_EOUT), _const_spec((1, SG_W)), _const_spec((SG_HEADS, 128, 128)),
         _const_spec((128, 128)), _const_spec((8, SC_W))],
        [rev(D), rev(EVEN_IN), rev(D), rev(D), _const_spec((1, D)), _const_spec((SG_HEADS, 128, 128)),
         _const_spec((128, 128)), _const_spec((1, SG_W)), _const_spec((8, SC_W))],
        [jax.ShapeDtypeStruct((t, D), F32), jax.ShapeDtypeStruct((t, EVEN_IN), BF16),
         jax.ShapeDtypeStruct((t, D), BF16), jax.ShapeDtypeStruct((t, D), BF16),
         jax.ShapeDtypeStruct((1, D), F32), jax.ShapeDtypeStruct((SG_HEADS, 128, 128), F32),
         jax.ShapeDtypeStruct((128, 128), F32), jax.ShapeDtypeStruct((1, SG_W), F32),
         jax.ShapeDtypeStruct((8, SC_W), F32)],
        (x, proj, proj, dx1, gamma, wg, wg, lng, ws, bst, cw), [pltpu.VMEM((HALO, SC_W), F32)], comm)


def _ffn_fwd(x, w_gu, w_d, gamma, name, comm=None, target=None):
    t = x.shape[0]
    last = target is not None

    def body(*refs):
        x_ref, gam_ref, wg_ref, wu_ref, wd_ref = refs[:5]
        y_ref, g_ref, u_ref = refs[5 + last:8 + last]
        xv = x_ref[...]
        h, _ = _rms(xv, gam_ref[...])
        hb = h.astype(BF16)
        g = _dot(hb, wg_ref[...].reshape(D_FF, D), 1, 1)
        u = _dot(hb, wu_ref[...].reshape(D_FF, D), 1, 1)
        g_ref[...] = g.astype(BF16)
        u_ref[...] = u.astype(BF16)
        act = g * jax.nn.sigmoid(g) * u
        y = xv + _dot(act.astype(BF16), wd_ref[...].reshape(D_FF, D), 1, 0)
        if not last:
            y_ref[...] = y
            return
        loss_ref = refs[9]

        @pl.when(pl.program_id(0) == 0)
        def _():
            loss_ref[...] = jnp.zeros_like(loss_ref)

        err = y - refs[5][...]
        y_ref[...] = err * (1.0 / D)
        sq = jnp.sum(jnp.sum(err * err, axis=-1, keepdims=True), axis=0, keepdims=True)
        loss_ref[...] += (0.5 / D) * sq

    row = pl.BlockSpec((TB, D), lambda i: (i, 0))
    wide = pl.BlockSpec((TB, D_FF), lambda i: (i, 0))
    in_specs = [row, _const_spec((1, D)), _wspec(N_FF, OFF_GATE), _wspec(N_FF, OFF_UP), _wspec(N_FF, 0)]
    out_specs = [row, wide, wide]
    out_shape = [jax.ShapeDtypeStruct((t, D), F32), jax.ShapeDtypeStruct((t, D_FF), BF16),
                 jax.ShapeDtypeStruct((t, D_FF), BF16)]
    args = (x, gamma, w_gu, w_gu, w_d)
    if last:
        in_specs, args = in_specs + [row], args + (target,)
        out_specs, out_shape = out_specs + [_const_spec((8, 128))], out_shape + [jax.ShapeDtypeStruct((8, 128), F32)]
    return _call(body, name, (t // TB,), in_specs, out_specs, out_shape, args, (), comm)


def _ffn_up(x, w_gu, gamma, name, comm=None):
    t = x.shape[0]

    def body(x_ref, gam_ref, wg_ref, wu_ref, g_ref, u_ref, act_ref):
        h, _ = _rms(x_ref[...], gam_ref[...])
        hb = h.astype(BF16)
        g = _dot(hb, wg_ref[...].reshape(D_FF, D), 1, 1)
        u = _dot(hb, wu_ref[...].reshape(D_FF, D), 1, 1)
        g_ref[...] = g.astype(BF16)
        u_ref[...] = u.astype(BF16)
        act_ref[...] = (g * jax.nn.sigmoid(g) * u).astype(BF16)

    row = pl.BlockSpec((TB, D), lambda i: (i, 0))
    wide = pl.BlockSpec((TB, D_FF), lambda i: (i, 0))
    return _call(body, name, (t // TB,), [row, _const_spec((1, D)), _wspec(N_FF, OFF_GATE), _wspec(N_FF, OFF_UP)],
                 [wide, wide, wide], [jax.ShapeDtypeStruct((t, D_FF), BF16)] * 3, (x, gamma, w_gu, w_gu), (), comm)


def _ffn_down(x, act, w_d, name, comm=None):
    t = x.shape[0]

    def body(x_ref, a_ref, wd_ref, y_ref):
        y_ref[...] = x_ref[...] + _dot(a_ref[...], wd_ref[...].reshape(D_FF, D), 1, 0)

    row = pl.BlockSpec((TB, D), lambda i: (i, 0))
    wide = pl.BlockSpec((TB, D_FF), lambda i: (i, 0))
    return _call(body, name, (t // TB,), [row, wide, _wspec(N_FF, 0)], [row], [jax.ShapeDtypeStruct((t, D), F32)],
                 (x, act, w_d), (), comm)


def _ffn_bwd(x, g, u, dy, w_gu, w_d, gamma, name, comm=None):
    t = x.shape[0]

    def body(x_ref, g_ref, u_ref, dy_ref, gam_ref, wg_ref, wu_ref, wd_ref,
             dx_ref, act_ref, dg_ref, du_ref, h_ref, dgam_ref):
        @pl.when(pl.program_id(0) == 0)
        def _():
            dgam_ref[...] = jnp.zeros_like(dgam_ref)

        xv = x_ref[...]
        gam = gam_ref[...]
        h, r = _rms(xv, gam)
        h_ref[...] = h.astype(BF16)
        dyv = dy_ref[...]
        dact = _dot(dyv.astype(BF16), wd_ref[...].reshape(D_FF, D), 1, 1)
        gv = g_ref[...].astype(F32)
        uv = u_ref[...].astype(F32)
        sg = jax.nn.sigmoid(gv)
        silu = gv * sg
        act_ref[...] = (silu * uv).astype(BF16)
        dgb = (dact * uv * (sg * (1.0 + gv * (1.0 - sg)))).astype(BF16)
        dub = (dact * silu).astype(BF16)
        dg_ref[...] = dgb
        du_ref[...] = dub
        dh = _dot(dgb, wg_ref[...].reshape(D_FF, D), 1, 0) + _dot(dub, wu_ref[...].reshape(D_FF, D), 1, 0)
        dxn, dgam = _rms_bwd(xv, r, gam, dh)
        dgam_ref[...] += dgam
        dx_ref[...] = dyv + dxn

    row = pl.BlockSpec((TB_FFN_BWD, D), lambda i: (i, 0))
    wide = pl.BlockSpec((TB_FFN_BWD, D_FF), lambda i: (i, 0))
    return _call(
        body, name, (t // TB_FFN_BWD,),
        [row, wide, wide, row, _const_spec((1, D)), _wspec(N_FF, OFF_GATE), _wspec(N_FF, OFF_UP),
         _wspec(N_FF, 0)],
        [row, wide, wide, wide, row, _const_spec((1, D))],
        [jax.ShapeDtypeStruct((t, D), F32), jax.ShapeDtypeStruct((t, D_FF), BF16),
         jax.ShapeDtypeStruct((t, D_FF), BF16), jax.ShapeDtypeStruct((t, D_FF), BF16),
         jax.ShapeDtypeStruct((t, D), BF16), jax.ShapeDtypeStruct((1, D), F32)],
        (x, g, u, dy, gamma, w_gu, w_gu, w_d), (), comm)


def _odd_pre_fwd(x, wg, gamma, qbt, kvbt, qa_g, kva_g, pw_bd, pscale, seq, comm=None):
    t = x.shape[0]
    nbs = seq // TB

    def body(x_ref, gam_ref, win_ref, qb_ref, kvb_ref, qa_ref, kva_ref, pw_ref, ps_ref,
             proj_ref, q_ref, kv_ref, kr_ref, c_ref, tail_ref):
        i = pl.program_id(0)
        h, _ = _rms(x_ref[...], gam_ref[...])
        proj = _dot(h.astype(BF16), win_ref[...].reshape(D, D), 1, 0)
        proj_ref[...] = proj.astype(BF16)
        zp, ql, kvl = proj[:, :POOL_W], proj[:, 256:640], proj[:, 640:896]
        kr_ref[...] = proj[:, 896:1024]
        qn, _ = _rms(ql, qa_ref[...])
        q_ref[...] = _dot(qn.astype(BF16), qb_ref[...], 1, 1).astype(BF16)
        kvn, _ = _rms(kvl, kva_ref[...])
        kv_ref[...] = _dot(kvn.astype(BF16), kvb_ref[...], 1, 1).astype(BF16)
        tail = jnp.where(i % nbs == 0, 0.0, tail_ref[...])
        pooled, _, _ = _pool_fwd(zp, tail, i % nbs)
        tail_ref[...] = zp[TB - HALO:, :]
        c_ref[...] = (_dot(pooled.astype(BF16), pw_ref[...], 1, 0) * ps_ref[...]).astype(BF16)

    def row(w):
        return pl.BlockSpec((TB, w), lambda i: (i, 0))

    return _call(
        body, "odd_pre_fwd", (t // TB,),
        [row(D), _const_spec((1, D)), _wspec(N_SQ, OFF_OIN), _const_spec((HEADS * HP, Q_LORA)),
         _const_spec((HEADS * HP, KV_LORA)), _const_spec((1, Q_LORA)), _const_spec((1, KV_LORA)),
         _const_spec((POOL_W, POOL_W)), _const_spec((1, POOL_W))],
        [row(D), row(HEADS * HP), row(HEADS * HP), row(128), row(POOL_W)],
        [jax.ShapeDtypeStruct((t, D), BF16), jax.ShapeDtypeStruct((t, HEADS * HP), BF16),
         jax.ShapeDtypeStruct((t, HEADS * HP), BF16), jax.ShapeDtypeStruct((t, 128), F32),
         jax.ShapeDtypeStruct((t, POOL_W), BF16)],
        (x, gamma, wg, qbt, kvbt, qa_g, kva_g, pw_bd, pscale), [pltpu.VMEM((HALO, POOL_W), F32)], comm)


def _odd_pre_bwd(x, proj, dx3, dmix, dq, dkv, dkr, wg, gamma, qbt, kvbt, qa_g, kva_g, pw_bd, pscale, seq):
    t = x.shape[0]
    nb, nbs = t // TB, seq // TB

    def body(x_ref, proj_ref, ptail_ref, dx3_ref, dco_ref, dq_ref, dkv_ref, dkr_ref, gam_ref, win_ref, qb_ref,
             kvb_ref, qa_ref, kva_ref, pw_ref, ps_ref,
             dx2_ref, dproj_ref, h_ref, qn_ref, kvn_ref, dgam_ref, dqa_ref, dkva_ref, dpw_ref, dps_ref, head_ref):
        i = pl.program_id(0)
        blk = nb - 1 - i

        @pl.when(i == 0)
        def _():
            dgam_ref[...] = jnp.zeros_like(dgam_ref)
            dqa_ref[...] = jnp.zeros_like(dqa_ref)
            dkva_ref[...] = jnp.zeros_like(dkva_ref)
            dpw_ref[...] = jnp.zeros_like(dpw_ref)
            dps_ref[...] = jnp.zeros_like(dps_ref)

        xv = x_ref[...]
        gam = gam_ref[...]
        h, r = _rms(xv, gam)
        h_ref[...] = h.astype(BF16)
        proj = proj_ref[...].astype(F32)
        zp, ql, kvl = proj[:, :POOL_W], proj[:, 256:640], proj[:, 640:896]

        qa = qa_ref[...]
        qn, rq = _rms(ql, qa)
        qn_ref[...] = qn.astype(BF16)
        dql, dqa = _rms_bwd(ql, rq, qa, _dot(dq_ref[...], qb_ref[...], 1, 0))
        dqa_ref[...] += dqa
        kva = kva_ref[...]
        kvn, rkv = _rms(kvl, kva)
        kvn_ref[...] = kvn.astype(BF16)
        dkvl, dkva = _rms_bwd(kvl, rkv, kva, _dot(dkv_ref[...], kvb_ref[...], 1, 0))
        dkva_ref[...] += dkva

        pt = ptail_ref[...].astype(F32)
        tail = jnp.where(blk % nbs == 0, 0.0, pt[:, :POOL_W])
        pooled, cnt, grp = _pool_fwd(zp, tail, blk % nbs)
        pb = pooled.astype(BF16)
        pw = pw_ref[...]
        dco = dco_ref[...].astype(F32)
        dps_ref[...] += jnp.sum(dco * _dot(pb, pw, 1, 0), axis=0, keepdims=True)
        dpo = (dco * ps_ref[...]).astype(BF16)
        dpw_ref[...] += _dot(pb, dpo, 0, 0)
        dpooled = _dot(dpo, pw, 1, 1)
        dpm = dpooled / cnt
        head = jnp.where(blk % nbs == nbs - 1, 0.0, head_ref[...])
        dz = _pool_bwd(dpooled, dpm, head, grp)
        head_ref[...] = dpm[:HALO, :]

        dproj_ref[:, :POOL_W] = dz.astype(BF16)
        dproj_ref[:, 256:640] = dql.astype(BF16)
        dproj_ref[:, 640:896] = dkvl.astype(BF16)
        dproj_ref[:, 896:1024] = dkr_ref[...].astype(BF16)
        dh = _dot(dproj_ref[...], win_ref[...].reshape(D, D), 1, 1)
        dxn, dgam = _rms_bwd(xv, r, gam, dh)
        dgam_ref[...] += dgam
        dx2_ref[...] = dx3_ref[...] + dxn

    def rev(w):
        return pl.BlockSpec((TB, w), lambda i: (nb - 1 - i, 0))

    ptail = pl.BlockSpec((HALO, D), lambda i: (jnp.maximum((nb - 1 - i) * (TB // HALO) - 1, 0), 0))
    return pl.pallas_call(
        body, name="odd_pre_bwd",
        out_shape=[jax.ShapeDtypeStruct((t, D), F32), jax.ShapeDtypeStruct((t, D), BF16),
                   jax.ShapeDtypeStruct((t, D), BF16), jax.ShapeDtypeStruct((t, Q_LORA), BF16),
                   jax.ShapeDtypeStruct((t, KV_LORA), BF16), jax.ShapeDtypeStruct((1, D), F32),
                   jax.ShapeDtypeStruct((1, Q_LORA), F32), jax.ShapeDtypeStruct((1, KV_LORA), F32),
                   jax.ShapeDtypeStruct((POOL_W, POOL_W), F32), jax.ShapeDtypeStruct((1, POOL_W), F32)],
        grid=(nb,),
        in_specs=[rev(D), rev(D), ptail, rev(D), rev(POOL_W), rev(HEADS * HP), rev(HEADS * HP), rev(128),
                  _const_spec((1, D)), _wspec(N_SQ, OFF_OIN), _const_spec((HEADS * HP, Q_LORA)),
                  _const_spec((HEADS * HP, KV_LORA)), _const_spec((1, Q_LORA)), _const_spec((1, KV_LORA)),
                  _const_spec((POOL_W, POOL_W)), _const_spec((1, POOL_W))],
        out_specs=[rev(D), rev(D), rev(D), rev(Q_LORA), rev(KV_LORA), _const_spec((1, D)), _const_spec((1, Q_LORA)),
                   _const_spec((1, KV_LORA)), _const_spec((POOL_W, POOL_W)), _const_spec((1, POOL_W))],
        scratch_shapes=[pltpu.VMEM((HALO, POOL_W), F32)],
        compiler_params=_cparams(1),
    )(x, proj, proj, dx3, dmix, dq, dkv, dkr, gamma, wg, qbt, kvbt, qa_g, kva_g, pw_bd, pscale)


def _attn_specs(seq):
    head = pl.BlockSpec((seq, HP), lambda b, h: (b, h))
    shared = pl.BlockSpec((seq, 128), lambda b, h: (b, 0))
    gain = pl.BlockSpec((1, HP), lambda b, h: (0, 0))
    return head, shared, gain


def _causal_bias(n):
    rows = lax.broadcasted_iota(jnp.int32, (n, n), 0)
    cols = lax.broadcasted_iota(jnp.int32, (n, n), 1)
    return jnp.where(cols <= rows, 0.0, NEG_INF)


def _attn_fwd(q, kv, kr, cos, sin, gq, gk, seq, comm=None):
    t = q.shape[0]
    qb = min(512, seq)

    def body(q_ref, kv_ref, kr_ref, c_ref, s_ref, gq_ref, gk_ref, o_ref, lse_ref):
        c, s = c_ref[...], s_ref[...]
        qf, _ = _qk_prep(q_ref[...].astype(F32), gq_ref[...], c, s)
        kin = jnp.concatenate([kv_ref[:, :128].astype(F32), kr_ref[...]], axis=1)
        kf, _ = _qk_prep(kin, gk_ref[...], c, s)
        qf, kf = qf.astype(BF16), kf.astype(BF16)
        v1 = jnp.concatenate([kv_ref[:, 128:], jnp.ones((seq, V_DIM), BF16)], axis=1)
        bias = _causal_bias(qb)
        for q0 in range(0, seq, qb):
            q1 = q0 + qb
            qblk = qf[q0:q1]
            s_dg = _dot(qblk, kf[q0:q1], 1, 1) + bias
            m = jnp.max(s_dg, axis=-1, keepdims=True)
            if q0:
                s_off = _dot(qblk, kf[:q0], 1, 1)
                m = jnp.maximum(m, jnp.max(s_off, axis=-1, keepdims=True))
            acc = _dot(jnp.exp(s_dg - m).astype(BF16), v1[q0:q1], 1, 0)
            if q0:
                acc = acc + _dot(jnp.exp(s_off - m).astype(BF16), v1[:q0], 1, 0)
            l = acc[:, V_DIM:]
            o_ref[q0:q1, :] = (acc[:, :V_DIM] / l).astype(BF16)
            lse_ref[q0:q1, :] = m + jnp.log(l)

    head, shared, gain = _attn_specs(seq)
    per_head = pl.BlockSpec((seq, V_DIM), lambda b, h: (b, h))
    return _call(
        body, "attn_fwd", (t // seq, HEADS),
        [head, head, shared, shared, shared, gain, gain], [per_head, per_head],
        [jax.ShapeDtypeStruct((t, HEADS * V_DIM), BF16), jax.ShapeDtypeStruct((t, HEADS * V_DIM), F32)],
        (q, kv, kr, cos, sin, gq, gk), (), comm)


def _attn_bwd(q, kv, kr, cos, sin, gq, gk, dmix, d_out, lse, seq, comm=None):
    t = q.shape[0]
    qb = min(512, seq)

    def body(q_ref, kv_ref, kr_ref, c_ref, s_ref, gq_ref, gk_ref, do_ref, o_ref, lse_ref,
             dq_ref, dkv_ref, dkr_ref, dgq_ref, dgk_ref, dqf_ref, dkf_ref, dv_ref):
        b, hd = pl.program_id(0), pl.program_id(1)

        @pl.when((b == 0) & (hd == 0))
        def _():
            dgq_ref[...] = jnp.zeros_like(dgq_ref)
            dgk_ref[...] = jnp.zeros_like(dgk_ref)

        c, sn = c_ref[...], s_ref[...]
        gq_v, gk_v = gq_ref[...], gk_ref[...]
        qin = q_ref[...].astype(F32)
        kin = jnp.concatenate([kv_ref[:, :128].astype(F32), kr_ref[...]], axis=1)
        qf32, rq = _qk_prep(qin, gq_v, c, sn)
        kf32, rk = _qk_prep(kin, gk_v, c, sn)
        qf, kf = qf32.astype(BF16), kf32.astype(BF16)
        vb = kv_ref[:, 128:]
        dkf_ref[...] = jnp.zeros_like(dkf_ref)
        dv_ref[...] = jnp.zeros_like(dv_ref)
        bias = _causal_bias(qb)
        for q0 in range(0, seq, qb):
            q1 = q0 + qb
            qblk = qf[q0:q1]
            do = do_ref[q0:q1, :]
            lse_col = lse_ref[q0:q1, 0:1]
            d_col = jnp.sum(do.astype(F32) * o_ref[q0:q1, :].astype(F32), axis=-1, keepdims=True)
            dq_acc = None
            for k0, k1, diag in ((q0, q1, True), (0, q0, False)):
                if k1 == k0:
                    continue
                s = _dot(qblk, kf[k0:k1], 1, 1)
                p = jnp.exp((s + bias if diag else s) - lse_col)
                dv_ref[k0:k1, :] += _dot(p.astype(BF16), do, 0, 0)
                ds = (p * (_dot(do, vb[k0:k1], 1, 1) - d_col)).astype(BF16)
                part = _dot(ds, kf[k0:k1], 1, 0)
                dq_acc = part if dq_acc is None else dq_acc + part
                dkf_ref[k0:k1, :] += _dot(ds, qblk, 0, 0)
            dqf_ref[q0:q1, :] = dq_acc
        dqin, dgq = _qk_prep_bwd(dqf_ref[...], qin, rq, gq_v, c, sn)
        dkin, dgk = _qk_prep_bwd(dkf_ref[...], kin, rk, gk_v, c, sn)
        dgq_ref[...] += dgq
        dgk_ref[...] += dgk
        dq_ref[...] = dqin.astype(BF16)
        dkv_ref[:, :128] = dkin[:, :128].astype(BF16)
        dkv_ref[:, 128:] = dv_ref[...].astype(BF16)

        @pl.when(hd == 0)
        def _():
            dkr_ref[...] = dkin[:, 128:]

        @pl.when(hd != 0)
        def _():
            dkr_ref[...] += dkin[:, 128:]

    head, shared, gain = _attn_specs(seq)
    per_head = pl.BlockSpec((seq, V_DIM), lambda b, h: (b, h))
    return _call(
        body, "attn_bwd", (t // seq, HEADS),
        [head, head, shared, shared, shared, gain, gain,
         pl.BlockSpec((seq, V_DIM), lambda b, h: (b, 2 + h)), per_head, per_head],
        [head, head, shared, gain, gain],
        [jax.ShapeDtypeStruct((t, HEADS * HP), BF16), jax.ShapeDtypeStruct((t, HEADS * HP), BF16),
         jax.ShapeDtypeStruct((t, 128), F32), jax.ShapeDtypeStruct((1, HP), F32),
         jax.ShapeDtypeStruct((1, HP), F32)],
        (q, kv, kr, cos, sin, gq, gk, dmix, d_out, lse),
        [pltpu.VMEM((seq, HP), F32), pltpu.VMEM((seq, HP), F32), pltpu.VMEM((seq, V_DIM), F32)], comm)


def _odd_post_fwd(x, c_out, d_out, wg):
    t = x.shape[0]

    def body(x_ref, c_ref, d_ref, w_ref, y_ref):
        y_ref[...] = (x_ref[...] + _dot(c_ref[...], w_ref[0:2].reshape(POOL_W, D), 1, 0)
                      + _dot(d_ref[...], w_ref[2:8].reshape(HEADS * V_DIM, D), 1, 0))

    def row(w):
        return pl.BlockSpec((TB, w), lambda i: (i, 0))

    return pl.pallas_call(
        body, name="odd_post_fwd", out_shape=jax.ShapeDtypeStruct((t, D), F32), grid=(t // TB,),
        in_specs=[row(D), row(POOL_W), row(HEADS * V_DIM), _wspec(N_SQ, OFF_OOUT)], out_specs=row(D),
        compiler_params=_cparams(1),
    )(x, c_out, d_out, wg)


def _odd_post_bwd(dx3, wg, comm=None):
    t = dx3.shape[0]

    def body(d_ref, w_ref, o_ref):
        o_ref[...] = _dot(d_ref[...].astype(BF16), w_ref[...].reshape(D, D), 1, 1).astype(BF16)

    row = pl.BlockSpec((TB, D), lambda i: (i, 0))
    (res,), extra = _call(body, "odd_post_bwd", (t // TB,), [row, _wspec(N_SQ, OFF_OOUT)], [row],
                          [jax.ShapeDtypeStruct((t, D), BF16)], (dx3, wg), (), comm)
    return res, extra


def _tn(a_list, b, tm, name, into=None, comm=None):
    t, n_out = b.shape
    widths = [a.shape[1] for a in a_list]
    tk = min(TK_DW, t)
    m, na, nk = sum(widths), len(a_list), t // tk
    assert na == 1 or tm == m

    def body(*refs):
        a_refs, b_ref, o_ref, acc_ref = refs[:na], refs[na], refs[-2], refs[-1]
        k = pl.program_id(1)

        @pl.when(k == 0)
        def _():
            acc_ref[...] = jnp.zeros_like(acc_ref)

        bb = b_ref[...].astype(BF16)
        m0 = 0
        for a_ref, w in zip(a_refs, widths):
            rows = slice(0, tm) if na == 1 else slice(m0, m0 + w)
            acc_ref[rows, :] += _dot(a_ref[...].astype(BF16), bb, 0, 0)
            m0 += w

        @pl.when(k == nk - 1)
        def _():
            o_ref[...] = acc_ref[...].astype(BF16).reshape(o_ref.shape)

    if na == 1:
        in_specs = [pl.BlockSpec((tk, tm), lambda i, k: (k, i))]
    else:
        in_specs = [pl.BlockSpec((tk, w), lambda i, k: (k, 0)) for w in widths]
    in_specs.append(pl.BlockSpec((tk, n_out), lambda i, k: (k, 0)))
    args = list(a_list) + [b]
    if into is None:
        out_spec = pl.BlockSpec((tm, n_out), lambda i, k: (i, 0))
        out_shape = jax.ShapeDtypeStruct((m, n_out), BF16)
        aliases = {}
    else:
        buf, n, off = into
        assert n_out == D and tm % n == 0 and off % n == 0 and (na == 1 or tm // n == N_DEV)
        idx = off // n
        out_spec = pl.BlockSpec((tm // n, n, D), lambda i, k: (i, idx, 0))
        out_shape = jax.ShapeDtypeStruct(buf.shape, BF16)
        in_specs.append(pl.BlockSpec(memory_space=pl.ANY))
        args.append(buf)
        aliases = {len(args) - 1: 0}
    (res,), extra = _call(body, name, (m // tm, nk), in_specs, [out_spec], [out_shape], args,
                          [pltpu.VMEM((tm, n_out), F32)], comm, aliases)
    return (res, extra) if comm is not None else res


def _adamw(ws, gs, ms, vs, name, nblk=1):
    n = len(ws)
    c1 = 1.0 - B1 ** STEP
    c2 = 1.0 - B2 ** STEP

    def body(*refs):
        for a in range(n):
            w, g, m, v = (refs[k * n + a][...] for k in range(4))
            d_ref, m_ref, v_ref = (refs[(4 + k) * n + a] for k in range(3))
            m_new = B1 * m + (1.0 - B1) * g
            v_new = B2 * v + (1.0 - B2) * (g * g)
            d_ref[...] = -LR * ((m_new / c1) / (jnp.sqrt(v_new / c2) + ADAM_EPS) + WD * w)
            m_ref[...] = m_new
            v_ref[...] = v_new

    grid = (nblk,)
    assert all(w.shape[0] % nblk == 0 and (nblk == 1 or (w.shape[0] // nblk) % 8 == 0) for w in ws)
    specs = [pl.BlockSpec((w.shape[0] // nblk, w.shape[1]), lambda i: (i, 0)) for w in ws]
    outs, _ = _call(body, name, grid, specs * 4, specs * 3, [jax.ShapeDtypeStruct(w.shape, F32) for w in ws] * 3,
                    (*ws, *gs, *ms, *vs))
    return outs[:n], outs[n:2 * n], outs[2 * n:]


def _rows1024(a, rows):
    flat = a.reshape(-1, D)
    return jnp.pad(flat, ((0, rows - flat.shape[0]), (0, 0)))


def _pack_shards(even_w_in, even_w_out, odd_w_in, q_b, kv_b, odd_w_out, ffn_w_gate, ffn_w_up, ffn_w_down):
    mix0 = jnp.concatenate([even_w_in[0].T, jnp.zeros((OFF_EOUT - N_EIN, D), F32), even_w_out[0]], axis=0)
    gu = [jnp.concatenate([ffn_w_gate[layer].T, ffn_w_up[layer].T], axis=0) for layer in range(2)]
    mix1 = jnp.concatenate([jnp.pad(odd_w_in[0], ((0, 0), (0, D - ODD_IN))), odd_w_out[0],
                            _rows1024(q_b[0].T, N_QB), _rows1024(kv_b[0].T, N_KVB),
                            jnp.zeros((R_MIX1 - OFF_KVB - N_KVB, D), F32)], axis=0)
    return [c.astype(BF16) for c in (mix0, gu[0], ffn_w_down[0], mix1, gu[1], ffn_w_down[1])]


def _pad_heads(a):
    k = a.shape[1]
    return jnp.pad(a.reshape(HEADS, QK_DIM, k), ((0, 0), (0, HP - QK_DIM), (0, 0))).reshape(HEADS * HP, k)


def _small_pack(parts):
    flat = []
    for p in parts:
        v = p.reshape(-1)
        flat.append(jnp.pad(v, (0, (-v.shape[0]) % 1024)))
    return jnp.concatenate(flat).reshape(-1, 128)


def _small_unpack(buf, shapes):
    flat = buf.reshape(-1)
    out, off = [], 0
    for s in shapes:
        size = int(np.prod(s))
        out.append(flat[off:off + size].reshape(s))
        off += size + (-size) % 1024
    return out


def _step(x3d, positions, target3d, chunks, tile, c_arr, chip_arr, mix_norm, ffn_norm, sg_ln_g, sg_w_s, sg_b_s,
          pool_w, q_norm, k_norm):
    bsz, seq, _ = x3d.shape
    t = bsz * seq
    x0 = x3d.reshape(t, D)
    target = target3d.reshape(t, D)
    my_mix0, my_gu0, my_d0, my_mix1, my_gu1, my_d1 = chunks

    lane = np.arange(128)
    inv_freq = np.where(lane < QK_ROPE, ROPE_THETA ** (-(2.0 * (lane % 32)) / QK_ROPE), 0.0)
    inv_freq = jnp.asarray(inv_freq.reshape(1, 128), F32)
    (cos, sin), (w_mix0, tiles) = _rope_tables(positions.reshape(t, 1), inv_freq, _gather_comm([my_mix0, tile]))

    conv_w = tiles[:, 0:3, 0:64].transpose(1, 0, 2).reshape(3, SC_W)
    pool_scale = tiles[:, 3, 0:32].reshape(1, POOL_W)
    q_a_norm = tiles[:, 4, 0:48].reshape(1, Q_LORA)
    kv_a_norm = tiles[:, 5, 0:32].reshape(1, KV_LORA)
    ws = sg_w_s[0]
    bst = jnp.pad(sg_b_s[0].T, ((0, 0), (0, 128 - SG_HEADS)))
    cw = jnp.pad(conv_w, ((0, 8 - 3), (0, 0)))
    pw_bd = jax.scipy.linalg.block_diag(*[pool_w[0, g] for g in range(4)]).astype(BF16)
    gq = jnp.pad(q_norm * ATT_SCALE, ((0, 0), (0, HP - QK_DIM)))
    gk = jnp.pad(k_norm, ((0, 0), (0, HP - QK_DIM)))

    (x1, proj_e), (w_gu0,) = _even_fwd(x0, w_mix0, mix_norm[0:1], sg_ln_g, ws, bst, cw, seq, _gather_comm([my_gu0]))
    (g0, u0, act0), (w_d0,) = _ffn_up(x1, w_gu0, ffn_norm[0:1], "ffn_up0", _gather_comm([my_d0]))
    (x2,), (w_mix1,) = _ffn_down(x1, act0, w_d0, "ffn_down0", _gather_comm([my_mix1]))
    qbt = _pad_heads(w_mix1[:, OFF_QB:OFF_QB + N_QB_USED, :].reshape(HEADS * QK_DIM, Q_LORA))
    kvbt = w_mix1[:, OFF_KVB:OFF_KVB + N_KVB, :].reshape(HEADS * HP, KV_LORA)
    (proj_o, q, kv, kr, c_out), (w_d1,) = _odd_pre_fwd(x2, w_mix1, mix_norm[1:2], qbt, kvbt, q_a_norm, kv_a_norm,
                                                     pw_bd, pool_scale, seq, _gather_comm([my_d1]))
    (d_out, lse), (w_gu1,) = _attn_fwd(q, kv, kr, cos, sin, gq, gk, seq, _gather_comm([my_gu1]))
    x3 = _odd_post_fwd(x2, c_out, d_out, w_mix1)
    (dy, g1, u1, loss_tile), _ = _ffn_fwd(x3, w_gu1, w_d1, ffn_norm[1:2], "ffn_fwd1", None, target)

    def chunk(rows, padded=False):
        return jnp.zeros((N_DEV, rows, D), BF16) if padded else lax.empty((N_DEV, rows, D), BF16)

    (dx3, act1, dg1, du1, h3, dgam_f1), _ = _ffn_bwd(x3, g1, u1, dy, w_gu1, w_d1, ffn_norm[1:2], "ffn_bwd1")
    gp_ffn1 = _tn([dg1], h3, 1408, "dw_gate1", (chunk(R_GU + N_FF), N_FF, OFF_GATE))
    gp_ffn1 = _tn([du1], h3, 1408, "dw_up1", (gp_ffn1, N_FF, OFF_UP))
    gp_ffn1 = _tn([act1], dy, 1408, "dw_down1", (gp_ffn1, N_FF, R_GU))

    dmix_o, (ga_ffn1,) = _odd_post_bwd(dx3, w_mix1, _pair_exchange_comm(gp_ffn1))
    pb_ffn1 = _rs_pair_sum(gp_ffn1, ga_ffn1, c_arr, "rs_pair_sum_ffn1")
    gp_mix1 = _tn([c_out, d_out], dx3, D, "dw_oout", (chunk(R_MIX1, True), N_SQ, OFF_OOUT))
    (dq, dkv, dkr, dgq, dgk), (gb_ffn1,) = _attn_bwd(q, kv, kr, cos, sin, gq, gk, dmix_o, d_out, lse, seq,
                                                    _chip_exchange_comm(pb_ffn1))
    gsh_ffn1 = _rs_final_sum(pb_ffn1, gb_ffn1, chip_arr, "rs_final_sum_ffn1")
    (dx2, dproj_o, h2, qn, kvn, dgam_m1, dqa, dkva, dpw_bd, dps) = _odd_pre_bwd(
        x2, proj_o, dx3, dmix_o, dq, dkv, dkr, w_mix1, mix_norm[1:2], qbt, kvbt, q_a_norm, kv_a_norm, pw_bd,
        pool_scale, seq)
    gp_mix1 = _tn([h2], dproj_o, D, "dw_oin", (gp_mix1, N_SQ, OFF_OIN))
    d_qbt = _tn([dq], qn, HEADS * HP, "dw_qb")
    d_qb_rows = d_qbt.reshape(HEADS, HP, Q_LORA)[:, :QK_DIM].reshape(N_DEV, N_QB_USED, D)
    d_kvb_rows = _tn([dkv], kvn, HEADS * HP, "dw_kvb").reshape(N_DEV, N_KVB, D)
    gp_mix1 = lax.dynamic_update_slice(gp_mix1, d_qb_rows, (0, OFF_QB, 0))
    gp_mix1 = lax.dynamic_update_slice(gp_mix1, d_kvb_rows, (0, OFF_KVB, 0))

    (dx1, act0, dg0, du0, h1, dgam_f0), (ga_mix1,) = _ffn_bwd(x1, g0, u0, dx2, w_gu0, w_d0, ffn_norm[0:1], "ffn_bwd0",
                                                             _pair_exchange_comm(gp_mix1))
    pb_mix1 = _rs_pair_sum(gp_mix1, ga_mix1, c_arr, "rs_pair_sum_mix1")
    gp_ffn0a, (gb_mix1,) = _tn([dg0], h1, 1408, "dw_gate0", (chunk(R_GU), N_FF, OFF_GATE),
                               _chip_exchange_comm(pb_mix1))
    gsh_mix1 = _rs_final_sum(pb_mix1, gb_mix1, chip_arr, "rs_final_sum_mix1")
    gp_ffn0a = _tn([du0], h1, 1408, "dw_up0", (gp_ffn0a, N_FF, OFF_UP))
    gp_ffn0b, (ga_ffn0a,) = _tn([act0], dx2, 1408, "dw_down0", (chunk(N_FF), N_FF, 0),
                                _pair_exchange_comm(gp_ffn0a))
    pb_ffn0a = _rs_pair_sum(gp_ffn0a, ga_ffn0a, c_arr, "rs_pair_sum_ffn0a")

    (dx0, dproj_e, mix_e, h0, dgam_m0, dws, dbc, dlng, dcw), (gb_ffn0a, ga_ffn0b) = _even_bwd(
        x0, proj_e, dx1, w_mix0, mix_norm[0:1], sg_ln_g, ws, bst, cw, seq,
        _both(_chip_exchange_comm(pb_ffn0a), _pair_exchange_comm(gp_ffn0b)))
    pb_ffn0b = _rs_pair_sum(gp_ffn0b, ga_ffn0b, c_arr, "rs_pair_sum_ffn0b")
    gsh_ffn0a = _rs_final_sum(pb_ffn0a, gb_ffn0a, chip_arr, "rs_final_sum_ffn0a")

    small = _small_pack([
        jnp.concatenate([dgam_m0, dgam_m1], 0), jnp.concatenate([dgam_f0, dgam_f1], 0), dlng,
        dws[None], dbc[:, :SG_HEADS].T[None], dcw[:3],
        jnp.stack([dpw_bd[g * POOL_GD:(g + 1) * POOL_GD, g * POOL_GD:(g + 1) * POOL_GD] for g in range(4)])[None],
        dps, dqa, dkva, dgq[:, :QK_DIM] * ATT_SCALE, dgk[:, :QK_DIM], loss_tile[0:1, 0:1]])
    gp_mix0, (small_all,) = _tn([mix_e], dx1, D, "dw_eout", (chunk(R_MIX0, True), N_SQ, OFF_EOUT),
                                _gather_comm([small]))
    gp_mix0, (gb_ffn0b,) = _tn([dproj_e], h0, 1280, "dw_ein", (gp_mix0, N_EIN, OFF_EIN),
                               _chip_exchange_comm(pb_ffn0b))
    gsh_ffn0b = _rs_final_sum(pb_ffn0b, gb_ffn0b, chip_arr, "rs_final_sum_ffn0b")
    small_sum = _small_unpack(_sum_gathered(small_all), SMALL_SHAPES)
    return dx0.reshape(bsz, seq, D), (gsh_ffn0a, gsh_ffn0b, gsh_mix1, gsh_ffn1), gp_mix0, small_sum


SMALL_SHAPES = [(2, D), (2, D), (1, SG_W), (1, SG_HEADS, 128, 128), (1, SG_HEADS, 128), (3, SC_W),
                (1, 4, POOL_GD, POOL_GD), (1, POOL_W), (1, Q_LORA), (1, KV_LORA), (1, QK_DIM), (1, QK_DIM), (1, 1)]


def kernel(x, positions, mix_norm, ffn_norm, even_w_in, sg_ln_g, sg_w_s, sg_b_s, sc_conv_w, even_w_out, odd_w_in, pool_w, pool_scale, q_a_norm, q_b, kv_a_norm, kv_b, q_norm, k_norm, odd_w_out, ffn_w_gate, ffn_w_up, ffn_w_down, loss_target, m_mix_norm, m_ffn_norm, m_even_w_in, m_sg_ln_g, m_sg_w_s, m_sg_b_s, m_sc_conv_w, m_even_w_out, m_odd_w_in, m_pool_w, m_pool_scale, m_q_a_norm, m_q_b, m_kv_a_norm, m_kv_b, m_q_norm, m_k_norm, m_odd_w_out, m_ffn_w_gate, m_ffn_w_up, m_ffn_w_down, v_mix_norm, v_ffn_norm, v_even_w_in, v_sg_ln_g, v_sg_w_s, v_sg_b_s, v_sc_conv_w, v_even_w_out, v_odd_w_in, v_pool_w, v_pool_scale, v_q_a_norm, v_q_b, v_kv_a_norm, v_kv_b, v_q_norm, v_k_norm, v_odd_w_out, v_ffn_w_gate, v_ffn_w_up, v_ffn_w_down):
    xi, yi, ci = _place()
    me = 4 * xi + 2 * yi + ci

    chunks = _pack_shards(even_w_in, even_w_out, odd_w_in, q_b, kv_b, odd_w_out, ffn_w_gate, ffn_w_up, ffn_w_down)

    def lane_pad(a):
        return jnp.pad(a, ((0, 0), (0, 128 - a.shape[1])))

    tile = jnp.concatenate([lane_pad(sc_conv_w[0]), lane_pad(pool_scale), lane_pad(q_a_norm), lane_pad(kv_a_norm),
                            jnp.zeros((2, 128), F32)], axis=0)
    c_arr = jnp.reshape(ci, (1,)).astype(jnp.int32)
    chip_arr = jnp.reshape(2 * xi + yi, (1,)).astype(jnp.int32)
    grad_x, (gsh_ffn0a, gsh_ffn0b, gsh_mix1, gsh_ffn1), gp_mix0, tot = _step(
        x, positions, loss_target, chunks, tile, c_arr, chip_arr, mix_norm, ffn_norm, sg_ln_g, sg_w_s, sg_b_s,
        pool_w, q_norm, k_norm)

    (g_mix, g_ffn, g_lng, g_ws, g_bs, g_cw_full, g_pw, g_ps_full, g_qa_full, g_kva_full, g_qn, g_kn, loss) = tot
    g_cw = lax.dynamic_slice_in_dim(g_cw_full, me * 64, 64, axis=1)[None]
    g_ps = lax.dynamic_slice_in_dim(g_ps_full, me * 32, 32, axis=1)
    g_qa = lax.dynamic_slice_in_dim(g_qa_full, me * 48, 48, axis=1)
    g_kva = lax.dynamic_slice_in_dim(g_kva_full, me * 32, 32, axis=1)

    def tr(a):
        return jnp.swapaxes(a, -1, -2)

    g_gate = tr(jnp.stack([gsh_ffn0a[OFF_GATE:OFF_GATE + N_FF], gsh_ffn1[OFF_GATE:OFF_GATE + N_FF]]))
    g_up = tr(jnp.stack([gsh_ffn0a[OFF_UP:OFF_UP + N_FF], gsh_ffn1[OFF_UP:OFF_UP + N_FF]]))
    g_down = jnp.stack([gsh_ffn0b, gsh_ffn1[R_GU:R_GU + N_FF]])
    g_oin = gsh_mix1[OFF_OIN:OFF_OIN + N_SQ, :ODD_IN][None]
    g_oout = gsh_mix1[OFF_OOUT:OFF_OOUT + N_SQ][None]
    g_qb = tr(gsh_mix1[OFF_QB:OFF_QB + N_QB_USED].reshape(1, 144, Q_LORA))
    g_kvb = tr(gsh_mix1[OFF_KVB:OFF_KVB + N_KVB].reshape(1, 192, KV_LORA))
    transposed = ("even_w_in", "odd_w_in", "q_b", "kv_b", "ffn_w_gate", "ffn_w_up")

    names = ("mix_norm", "ffn_norm", "even_w_in", "sg_ln_g", "sg_w_s", "sg_b_s", "sc_conv_w", "even_w_out",
             "odd_w_in", "pool_w", "pool_scale", "q_a_norm", "q_b", "kv_a_norm", "kv_b", "q_norm", "k_norm",
             "odd_w_out", "ffn_w_gate", "ffn_w_up", "ffn_w_down")
    grads = dict(mix_norm=g_mix, ffn_norm=g_ffn, sg_ln_g=g_lng, sg_w_s=g_ws, sg_b_s=g_bs,
                 sc_conv_w=g_cw, odd_w_in=g_oin, pool_w=g_pw, pool_scale=g_ps, q_a_norm=g_qa,
                 q_b=g_qb, kv_a_norm=g_kva, kv_b=g_kvb, q_norm=g_qn, k_norm=g_kn, odd_w_out=g_oout,
                 ffn_w_gate=g_gate, ffn_w_up=g_up, ffn_w_down=g_down)
    weights = dict(mix_norm=mix_norm, ffn_norm=ffn_norm, even_w_in=even_w_in, sg_ln_g=sg_ln_g, sg_w_s=sg_w_s,
                   sg_b_s=sg_b_s, sc_conv_w=sc_conv_w, even_w_out=even_w_out, odd_w_in=odd_w_in, pool_w=pool_w,
                   pool_scale=pool_scale, q_a_norm=q_a_norm, q_b=q_b, kv_a_norm=kv_a_norm, kv_b=kv_b, q_norm=q_norm,
                   k_norm=k_norm, odd_w_out=odd_w_out, ffn_w_gate=ffn_w_gate, ffn_w_up=ffn_w_up,
                   ffn_w_down=ffn_w_down)
    m_in = dict(mix_norm=m_mix_norm, ffn_norm=m_ffn_norm, even_w_in=m_even_w_in, sg_ln_g=m_sg_ln_g, sg_w_s=m_sg_w_s,
                sg_b_s=m_sg_b_s, sc_conv_w=m_sc_conv_w, even_w_out=m_even_w_out, odd_w_in=m_odd_w_in,
                pool_w=m_pool_w, pool_scale=m_pool_scale, q_a_norm=m_q_a_norm, q_b=m_q_b, kv_a_norm=m_kv_a_norm,
                kv_b=m_kv_b, q_norm=m_q_norm, k_norm=m_k_norm, odd_w_out=m_odd_w_out, ffn_w_gate=m_ffn_w_gate,
                ffn_w_up=m_ffn_w_up, ffn_w_down=m_ffn_w_down)
    v_in = dict(mix_norm=v_mix_norm, ffn_norm=v_ffn_norm, even_w_in=v_even_w_in, sg_ln_g=v_sg_ln_g, sg_w_s=v_sg_w_s,
                sg_b_s=v_sg_b_s, sc_conv_w=v_sc_conv_w, even_w_out=v_even_w_out, odd_w_in=v_odd_w_in,
                pool_w=v_pool_w, pool_scale=v_pool_scale, q_a_norm=v_q_a_norm, q_b=v_q_b, kv_a_norm=v_kv_a_norm,
                kv_b=v_kv_b, q_norm=v_q_norm, k_norm=v_k_norm, odd_w_out=v_odd_w_out, ffn_w_gate=v_ffn_w_gate,
                ffn_w_up=v_ffn_w_up, ffn_w_down=v_ffn_w_down)
    delta, new_m, new_v = {}, {}, {}

    def as2d(k, a):
        a = tr(a) if k in transposed else a
        return a.reshape(-1, a.shape[-1])

    def back(k, a):
        shape = weights[k].shape
        return tr(a.reshape(shape[:-2] + (shape[-1], shape[-2]))) if k in transposed else a.reshape(shape)

    def update(group, name, nblk=1):
        outs = _adamw([as2d(k, weights[k]) for k in group], [as2d(k, grads[k]) for k in group],
                      [as2d(k, m_in[k]) for k in group], [as2d(k, v_in[k]) for k in group], name, nblk)
        for i, k in enumerate(group):
            delta[k], new_m[k], new_v[k] = (back(k, o[i]) for o in outs)

    (ga_mix0,) = _comm_alone(_pair_exchange_comm(gp_mix0), "rs_pair_exchange_mix0")
    pb_mix0 = _rs_pair_sum(gp_mix0, ga_mix0, c_arr, "rs_pair_sum_mix0")
    (gb_mix0,) = _comm_alone(_chip_exchange_comm(pb_mix0), "rs_chip_exchange_mix0")
    gsh_mix0 = _rs_final_sum(pb_mix0, gb_mix0, chip_arr, "rs_final_sum_mix0")
    grads["even_w_in"] = tr(gsh_mix0[OFF_EIN:OFF_EIN + N_EIN][None])
    grads["even_w_out"] = gsh_mix0[OFF_EOUT:OFF_EOUT + N_SQ][None]

    update(["ffn_w_gate", "ffn_w_up", "ffn_w_down"], "adamw_ffn", 4)
    update(["even_w_in", "even_w_out", "odd_w_in", "odd_w_out"], "adamw_mix", 2)
    update([k for k in names if k not in delta], "adamw_small")

    return (loss.reshape(()), grad_x, *[grads[k] for k in names], *[delta[k] for k in names],
            *[new_m[k] for k in names], *[new_v[k] for k in names])
```

```python
import functools

import numpy as np
import jax
import jax.numpy as jnp
from jax import lax
from jax.experimental import pallas as pl
from jax.experimental.pallas import tpu as pltpu

F32 = jnp.float32
BF16 = jnp.bfloat16
MESH = pl.DeviceIdType.MESH

D = 1024
EPS = 1e-6
NEG_INF = -1e30
SG_HEADS, SG_HD, SG_W, SG_CHUNK = 4, 128, 512, 128
SC_W = 512
EVEN_IN = 2560
POOL_W = 256
POOL_GD = 64
Q_LORA, KV_LORA, QK_ROPE, QK_NOPE, V_DIM = 384, 256, 64, 128, 128
QK_DIM = QK_NOPE + QK_ROPE
HEADS = 6
HP = 256
ODD_IN = 960
D_FF = 2816
ROPE_THETA = 10000.0
ATT_SCALE = QK_DIM ** -0.5
LR, B1, B2, ADAM_EPS, WD, STEP = 0.001, 0.9, 0.999, 1e-08, 0.01, 10

N_DEV = 8
TB = 512
TB_FFN_BWD = 256
TK_DW = 1024
HALO = 16
VMEM_LIMIT = 56 * 1024 * 1024

N_EIN, N_FF, N_SQ = 320, 352, 128
OFF_EIN, OFF_EOUT, R_MIX0 = 0, 384, 512
OFF_GATE, OFF_UP, R_GU = 0, 352, 704
OFF_OIN, OFF_OOUT, OFF_QB, OFF_KVB, R_MIX1 = 0, 128, 256, 320, 384
N_QB, N_QB_USED, N_KVB = 64, 54, 48

INV_SQRT2 = 0.7071067811865476
INV_SQRT_2PI = 0.3989422804014327


def _dot(a, b, ca, cb):
    return lax.dot_general(a, b, (((ca,), (cb,)), ((), ())), preferred_element_type=F32)


def _cparams(n_axes=1):
    return pltpu.CompilerParams(dimension_semantics=("arbitrary",) * n_axes, vmem_limit_bytes=VMEM_LIMIT)


def _wspec(n, off, arity=1):
    assert off % n == 0
    idx = off // n
    if arity == 1:
        return pl.BlockSpec((N_DEV, n, D), lambda i: (0, idx, 0), pipeline_mode=pl.Buffered(1))
    return pl.BlockSpec((N_DEV, n, D), lambda i, j: (0, idx, 0), pipeline_mode=pl.Buffered(1))


def _const_spec(shape):
    zeros = (0,) * len(shape)
    return pl.BlockSpec(shape, lambda *_: zeros)


class _Comm:
    def __init__(self, ins, out_shapes, sems, start, wait, mid=None):
        self.ins, self.out_shapes, self.sems, self.start, self.wait, self.mid = ins, out_shapes, sems, start, wait, mid


def _both(c1, c2):
    def split(f1, f2):
        def run(ins, outs, sems):
            f1(ins[:len(c1.ins)], outs[:len(c1.out_shapes)], sems[:len(c1.sems)])
            f2(ins[len(c1.ins):], outs[len(c1.out_shapes):], sems[len(c1.sems):])
        return run

    assert c1.mid is None and c2.mid is None
    return _Comm(c1.ins + c2.ins, c1.out_shapes + c2.out_shapes, c1.sems + c2.sems,
                 split(c1.start, c2.start), split(c1.wait, c2.wait))


def _call(body, name, grid, in_specs, out_specs, out_shape, args, scratch_shapes=(), comm=None, aliases=None):
    n_axes = len(grid)
    aliases = aliases or {}
    if comm is None:
        res = pl.pallas_call(
            body, name=name, grid=grid, in_specs=list(in_specs), out_specs=list(out_specs),
            out_shape=list(out_shape), scratch_shapes=list(scratch_shapes), input_output_aliases=aliases,
            compiler_params=_cparams(n_axes))(*args)
        return list(res), []
    ni, no, ns = len(in_specs), len(out_specs), len(scratch_shapes)
    ci, co = len(comm.ins), len(comm.out_shapes)
    n_steps = int(np.prod(grid))

    def carrier(*refs):
        ins, cin = refs[:ni], refs[ni:ni + ci]
        outs, cout = refs[ni + ci:ni + ci + no], refs[ni + ci + no:ni + ci + no + co]
        scr, sems = refs[ni + ci + no + co:ni + ci + no + co + ns], refs[ni + ci + no + co + ns:]
        step = 0
        for a in range(n_axes):
            step = step * grid[a] + pl.program_id(a)

        @pl.when(step == 0)
        def _():
            comm.start(cin, cout, sems)

        body(*ins, *outs, *scr)

        if comm.mid is not None and n_steps >= 4:
            @pl.when(step == (3 * n_steps) // 4)
            def _():
                comm.mid(cin, cout, sems)

        @pl.when(step == n_steps - 1)
        def _():
            if comm.mid is not None and n_steps < 4:
                comm.mid(cin, cout, sems)
            comm.wait(cin, cout, sems)

    any_spec = pl.BlockSpec(memory_space=pl.ANY)
    res = pl.pallas_call(
        carrier, name=name, grid=grid, in_specs=list(in_specs) + [any_spec] * ci,
        out_specs=list(out_specs) + [any_spec] * co, out_shape=list(out_shape) + list(comm.out_shapes),
        scratch_shapes=list(scratch_shapes) + list(comm.sems), input_output_aliases=aliases,
        compiler_params=_cparams(n_axes))(*args, *comm.ins)
    return list(res[:no]), list(res[no:])


def _comm_alone(comm, name):
    ci, co = len(comm.ins), len(comm.out_shapes)

    def body(*refs):
        cin, cout, sems = refs[:ci], refs[ci:ci + co], refs[ci + co:]
        comm.start(cin, cout, sems)
        if comm.mid is not None:
            comm.mid(cin, cout, sems)
        comm.wait(cin, cout, sems)

    any_spec = pl.BlockSpec(memory_space=pl.ANY)
    res = pl.pallas_call(
        body, name=name, out_shape=list(comm.out_shapes), in_specs=[any_spec] * ci, out_specs=[any_spec] * co,
        scratch_shapes=list(comm.sems))(*comm.ins)
    return list(res)


def _rms(x, g):
    r = lax.rsqrt(jnp.mean(x * x, axis=-1, keepdims=True) + EPS)
    return x * r * g, r


def _rms_bwd(x, r, g, dy):
    xh = x * r
    dxh = dy * g
    dx = r * (dxh - xh * jnp.mean(dxh * xh, axis=-1, keepdims=True))
    dg = jnp.sum(dy * xh, axis=0, keepdims=True)
    return dx, dg


def _gelu(x):
    return 0.5 * x * (1.0 + lax.erf(x * INV_SQRT2))


def _gelu_grad(x):
    return 0.5 * (1.0 + lax.erf(x * INV_SQRT2)) + x * jnp.exp(-0.5 * x * x) * INV_SQRT_2PI


def _shift_down(a, k):
    rows = lax.broadcasted_iota(jnp.int32, a.shape, 0)
    return jnp.where(rows >= k, pltpu.roll(a, k, 0), 0.0)


def _shift_up(a, k):
    n = a.shape[0]
    rows = lax.broadcasted_iota(jnp.int32, a.shape, 0)
    return jnp.where(rows < n - k, pltpu.roll(a, n - k, 0), 0.0)


def _tril_bf16(w):
    r = lax.broadcasted_iota(jnp.int32, w.shape, 0)
    c = lax.broadcasted_iota(jnp.int32, w.shape, 1)
    return jnp.where(r >= c, w, 0.0).astype(BF16)


def _ln_head(vh, g):
    mu = jnp.mean(vh, axis=-1, keepdims=True)
    xc = vh - mu
    rr = lax.rsqrt(jnp.mean(xc * xc, axis=-1, keepdims=True) + EPS)
    xh = xc * rr
    return xh * g, xh, rr


def _conv_fwd(z, tail, cw_ref):
    ext = jnp.concatenate([tail, z], axis=0)
    zs1 = _shift_down(ext, 1)[HALO:]
    zs2 = _shift_down(ext, 2)[HALO:]
    y = cw_ref[2:3, :] * z + cw_ref[1:2, :] * zs1 + cw_ref[0:1, :] * zs2
    return y, zs1, zs2


def _pool_cnt(shape, blk_in_seq):
    rows = lax.broadcasted_iota(jnp.int32, shape, 0)
    grp = lax.broadcasted_iota(jnp.int32, shape, 1) // POOL_GD
    win = jnp.where(grp == 0, 2, jnp.where(grp == 1, 4, jnp.where(grp == 2, 8, 16)))
    tpos = blk_in_seq * shape[0] + rows + 1
    return jnp.minimum(tpos, win).astype(F32), grp


def _pool_select(grp, s2, s4, s8, s16):
    return jnp.where(grp == 0, s2, jnp.where(grp == 1, s4, jnp.where(grp == 2, s8, s16)))


def _pool_fwd(z, tail, blk_in_seq):
    ext = jnp.concatenate([tail, z], axis=0)
    s2 = ext + _shift_down(ext, 1)
    s4 = s2 + _shift_down(s2, 2)
    s8 = s4 + _shift_down(s4, 4)
    s16 = s8 + _shift_down(s8, 8)
    cnt, grp = _pool_cnt(z.shape, blk_in_seq)
    sums = _pool_select(grp, s2[HALO:], s4[HALO:], s8[HALO:], s16[HALO:])
    return sums / cnt - z, cnt, grp


def _pool_bwd(dpooled, dpm, head, grp):
    n = dpm.shape[0]
    ext = jnp.concatenate([dpm, head], axis=0)
    u2 = ext + _shift_up(ext, 1)
    u4 = u2 + _shift_up(u2, 2)
    u8 = u4 + _shift_up(u4, 4)
    u16 = u8 + _shift_up(u8, 8)
    return _pool_select(grp, u2[:n], u4[:n], u8[:n], u16[:n]) - dpooled


def _lane_sums(a):
    return _dot(a.astype(BF16), jnp.ones((a.shape[1], a.shape[1]), BF16), 1, 0)


def _swap_halves(y1):
    src = lax.broadcasted_iota(jnp.int32, (128, 128), 0)
    dst = lax.broadcasted_iota(jnp.int32, (128, 128), 1)
    perm = jnp.where(((dst < 32) & (src == dst + 32)) | ((dst >= 32) & (dst < QK_ROPE) & (src == dst - 32)), 1.0, 0.0)
    return _dot(y1.astype(BF16), perm.astype(BF16), 1, 0)


def _rope(y1, c, s):
    return y1 * c + _swap_halves(y1) * s


def _rope_bwd(d1, c, s):
    return d1 * c + _swap_halves(d1 * s)


def _qk_prep(x, g, c, s):
    r = lax.rsqrt(_lane_sums(x * x) * (1.0 / QK_DIM) + EPS)
    y = x * r * g
    return jnp.concatenate([y[:, :128], _rope(y[:, 128:], c, s)], axis=1), r


def _qk_prep_bwd(dout, x, r, g, c, s):
    dy = jnp.concatenate([dout[:, :128], _rope_bwd(dout[:, 128:], c, s)], axis=1)
    xh = x * r
    dxh = dy * g
    dx = r * (dxh - xh * (_lane_sums(dxh * xh) * (1.0 / QK_DIM)))
    return dx, jnp.sum(dy * xh, axis=0, keepdims=True)


def _place():
    return lax.axis_index("x"), lax.axis_index("y"), lax.axis_index("c")


def _gather_comm(arrs):
    n = len(arrs)

    def plan(ins, outs, sems):
        send_sems, recv_sems, local_sems = sems
        x, y, c = _place()
        me, sibling = (x, y, c), (x, y, 1 - c)
        chips = [(1 - x, y), (x, 1 - y), (1 - x, 1 - y)]

        def slot(a, px, py, pc):
            return outs[a].at[4 * px + 2 * py + pc]

        def copy(a, k, block, to, src=None):
            return pltpu.make_async_remote_copy(
                src_ref=slot(a, *block) if src is None else src, dst_ref=slot(a, *block),
                send_sem=send_sems.at[a, k], recv_sem=recv_sems.at[a, k], device_id=to, device_id_type=MESH)

        def own():
            mine = [pltpu.make_async_copy(ins[a], slot(a, *me), local_sems.at[a]) for a in range(n)]
            first = []
            for a in range(n):
                first.append(copy(a, 0, me, sibling, src=ins[a]))
                first += [copy(a, 1 + j, me, (*chip, c), src=ins[a]) for j, chip in enumerate(chips)]
            return mine, first

        return c, me, sibling, chips, copy, own

    def start(ins, outs, sems):
        mine, first = plan(ins, outs, sems)[-1]()
        for cp in mine + first:
            cp.start()

    def mid(ins, outs, sems):
        c, me, sibling, chips, copy, _ = plan(ins, outs, sems)
        for j, chip in enumerate(chips):
            for a in range(n):
                copy(a, 1 + j, (*chip, c), me).wait_recv()
                copy(a, 4 + j, (*chip, c), sibling).start()

    def wait(ins, outs, sems):
        c, me, sibling, chips, copy, own = plan(ins, outs, sems)
        mine, first = own()
        passed = [copy(a, 4 + j, (*chip, c), sibling) for j, chip in enumerate(chips) for a in range(n)]
        for a in range(n):
            copy(a, 0, sibling, me).wait_recv()
            for j, chip in enumerate(chips):
                copy(a, 4 + j, (*chip, 1 - c), me).wait_recv()
        for cp in first + passed:
            cp.wait_send()
        for cp in mine:
            cp.wait()

    return _Comm(
        list(arrs), [jax.ShapeDtypeStruct((N_DEV,) + a.shape, a.dtype) for a in arrs],
        [pltpu.SemaphoreType.DMA((n, 7)), pltpu.SemaphoreType.DMA((n, 7)), pltpu.SemaphoreType.DMA((n,))],
        start, wait, mid)


def _sum_gathered(g):
    rows = g.shape[1]

    def body(g_ref, sum_ref):
        total = g_ref[0]
        for d in range(1, N_DEV):
            total = total + g_ref[d]
        sum_ref[...] = total

    return pl.pallas_call(
        body, name="sum_gathered_small", out_shape=jax.ShapeDtypeStruct((rows, 128), F32), grid=(1,),
        in_specs=[pl.BlockSpec((N_DEV, rows, 128), lambda i: (0, 0, 0))],
        out_specs=pl.BlockSpec((rows, 128), lambda i: (0, 0)), compiler_params=_cparams(1),
    )(g)


def _sum_rows(rows):
    return rows if rows <= 512 else rows // 2


def _pair_exchange_comm(gp):
    _, rows, cols = gp.shape

    def copies(ins, outs, sems):
        send_sems, recv_sems = sems
        x, y, c = _place()
        return [pltpu.make_async_remote_copy(
            src_ref=ins[0].at[2 * j + (1 - c)], dst_ref=outs[0].at[j], send_sem=send_sems.at[j],
            recv_sem=recv_sems.at[j], device_id=(x, y, 1 - c), device_id_type=MESH) for j in range(4)]

    def start(ins, outs, sems):
        for cp in copies(ins, outs, sems):
            cp.start()

    def wait(ins, outs, sems):
        for cp in copies(ins, outs, sems):
            cp.wait()

    return _Comm([gp], [jax.ShapeDtypeStruct((4, rows, cols), gp.dtype)],
                 [pltpu.SemaphoreType.DMA((4,)), pltpu.SemaphoreType.DMA((4,))], start, wait)


def _rs_pair_sum(gp, got, c_arr, name):
    _, rows, cols = got.shape
    rb = _sum_rows(rows)
    gp4 = gp.reshape(4, 2, rows, cols)

    def body(c_ref, a_ref, b_ref, o_ref):
        o_ref[0] = (a_ref[0, 0].astype(F32) + b_ref[0].astype(F32)).astype(o_ref.dtype)

    return pl.pallas_call(
        body, name=name, out_shape=jax.ShapeDtypeStruct((4, rows, cols), gp.dtype),
        grid_spec=pltpu.PrefetchScalarGridSpec(
            num_scalar_prefetch=1, grid=(4, rows // rb),
            in_specs=[pl.BlockSpec((1, 1, rb, cols), lambda j, r, cr: (j, cr[0], r, 0)),
                      pl.BlockSpec((1, rb, cols), lambda j, r, cr: (j, r, 0))],
            out_specs=pl.BlockSpec((1, rb, cols), lambda j, r, cr: (j, r, 0))),
        compiler_params=_cparams(2),
    )(c_arr, gp4, got)


def _chip_exchange_comm(pb):
    _, rows, cols = pb.shape

    def copies(ins, outs, sems):
        send_sems, recv_sems = sems
        x, y, c = _place()
        chips = [(1 - x, y), (x, 1 - y), (1 - x, 1 - y)]
        return [pltpu.make_async_remote_copy(
            src_ref=ins[0].at[2 * px + py], dst_ref=outs[0].at[k], send_sem=send_sems.at[k],
            recv_sem=recv_sems.at[k], device_id=(px, py, c), device_id_type=MESH)
            for k, (px, py) in enumerate(chips)]

    def start(ins, outs, sems):
        for cp in copies(ins, outs, sems):
            cp.start()

    def wait(ins, outs, sems):
        for cp in copies(ins, outs, sems):
            cp.wait()

    return _Comm([pb], [jax.ShapeDtypeStruct((3, rows, cols), pb.dtype)],
                 [pltpu.SemaphoreType.DMA((3,)), pltpu.SemaphoreType.DMA((3,))], start, wait)


def _rs_final_sums(pbs, gots, chip, name, comm=None):
    n = len(pbs)
    mine = [lax.dynamic_index_in_dim(pb, chip, 0, keepdims=False) for pb in pbs]

    def body(*refs):
        for a in range(n):
            m_ref, g_ref, o_ref = refs[a], refs[n + a], refs[2 * n + a]
            o_ref[...] = ((m_ref[...].astype(F32) + g_ref[0].astype(F32)) + g_ref[1].astype(F32)) + g_ref[2].astype(F32)

    half = [m.shape[0] // 2 for m in mine]
    res, extra = _call(
        body, name, (2,),
        [pl.BlockSpec((h, D), lambda i: (i, 0)) for h in half] + [pl.BlockSpec((3, h, D), lambda i: (0, i, 0)) for h in half],
        [pl.BlockSpec((h, D), lambda i: (i, 0)) for h in half], [jax.ShapeDtypeStruct(m.shape, F32) for m in mine],
        (*mine, *gots), (), comm)
    return (res, extra) if comm is not None else res


def _rope_tables(pos_col, inv_freq, comm=None):
    t = pos_col.shape[0]

    def body(p_ref, f_ref, c_ref, s_ref):
        ang = p_ref[...].astype(F32) * f_ref[...]
        lane = lax.broadcasted_iota(jnp.int32, ang.shape, 1)
        c_ref[...] = jnp.where(lane < QK_ROPE, jnp.cos(ang), 0.0)
        s = jnp.sin(ang)
        s_ref[...] = jnp.where(lane < 32, -s, jnp.where(lane < QK_ROPE, s, 0.0))

    spec = pl.BlockSpec((TB, 128), lambda i: (i, 0))
    return _call(
        body, "rope_tables", (t // TB,), [pl.BlockSpec((TB, 1), lambda i: (i, 0)), _const_spec((1, 128))],
        [spec] * 2, [jax.ShapeDtypeStruct((t, 128), F32)] * 2, (pos_col, inv_freq), (), comm)


def _sgu_conv_fwd(proj, tail, lng_ref, ws_ref, bst_ref, cw_ref):
    gu = _gelu(proj[:, 0:SG_W])
    gv = _gelu(proj[:, SG_W:2 * SG_W])
    bg = proj[:, 1024:1536]
    z = proj[:, 1536:2048] * proj[:, 2048:2560]
    heads = []
    for h in range(SG_HEADS):
        sl = slice(h * SG_HD, (h + 1) * SG_HD)
        vn, _, _ = _ln_head(gv[:, sl], lng_ref[:, sl])
        vnb = vn.astype(BF16)
        wm = _tril_bf16(ws_ref[h])
        bcol = bst_ref[:, h:h + 1]
        mixed = jnp.concatenate(
            [_dot(wm, vnb[k * SG_CHUNK:(k + 1) * SG_CHUNK], 1, 0) + bcol for k in range(TB // SG_CHUNK)], axis=0)
        heads.append(gu[:, sl] * mixed)
    a_out = jnp.concatenate(heads, axis=1)
    y, _, _ = _conv_fwd(z, tail, cw_ref)
    return a_out, bg * y, z


def _even_fwd(x, wg, gamma, lng, ws, bst, cw, seq, comm=None):
    t = x.shape[0]
    nbs = seq // TB

    def body(x_ref, gam_ref, win_ref, wout_ref, lng_ref, ws_ref, bst_ref, cw_ref, x1_ref, proj_ref, tail_ref):
        i = pl.program_id(0)
        xv = x_ref[...]
        h, _ = _rms(xv, gam_ref[...])
        proj = _dot(h.astype(BF16), win_ref[...].reshape(EVEN_IN, D), 1, 1)
        proj_ref[...] = proj.astype(BF16)
        tail = jnp.where(i % nbs == 0, 0.0, tail_ref[...])
        a_out, b_out, z = _sgu_conv_fwd(proj, tail, lng_ref, ws_ref, bst_ref, cw_ref)
        tail_ref[...] = z[TB - HALO:, :]
        x1_ref[...] = (xv + _dot(a_out.astype(BF16), wout_ref[0:4].reshape(512, D), 1, 0)
                       + _dot(b_out.astype(BF16), wout_ref[4:8].reshape(512, D), 1, 0))

    row = pl.BlockSpec((TB, D), lambda i: (i, 0))
    return _call(
        body, "even_fwd", (t // TB,),
        [row, _const_spec((1, D)), _wspec(N_EIN, OFF_EIN), _wspec(N_SQ, OFF_EOUT), _const_spec((1, SG_W)),
         _const_spec((SG_HEADS, 128, 128)), _const_spec((128, 128)), _const_spec((8, SC_W))],
        [row, pl.BlockSpec((TB, EVEN_IN), lambda i: (i, 0))],
        [jax.ShapeDtypeStruct((t, D), F32), jax.ShapeDtypeStruct((t, EVEN_IN), BF16)],
        (x, gamma, wg, wg, lng, ws, bst, cw), [pltpu.VMEM((HALO, SC_W), F32)], comm)


def _even_bwd(x, proj, dx1, wg, gamma, lng, ws, bst, cw, seq, comm=None):
    t = x.shape[0]
    nb, nbs = t // TB, seq // TB

    def body(x_ref, proj_ref, ptail_ref, dx1_ref, gam_ref, win_ref, wout_ref, lng_ref, ws_ref, bst_ref, cw_ref,
             dx0_ref, dproj_ref, mix_ref, h_ref, dgam_ref, dws_ref, dbc_ref, dlng_ref, dcw_ref, head_ref):
        i = pl.program_id(0)
        blk = nb - 1 - i

        @pl.when(i == 0)
        def _():
            dgam_ref[...] = jnp.zeros_like(dgam_ref)
            dws_ref[...] = jnp.zeros_like(dws_ref)
            dbc_ref[...] = jnp.zeros_like(dbc_ref)
            dlng_ref[...] = jnp.zeros_like(dlng_ref)
            dcw_ref[...] = jnp.zeros_like(dcw_ref)

        xv = x_ref[...]
        gam = gam_ref[...]
        h, r = _rms(xv, gam)
        h_ref[...] = h.astype(BF16)
        dx1 = dx1_ref[...]
        dmix = _dot(dx1.astype(BF16), wout_ref[...].reshape(D, D), 1, 1)
        da, db = dmix[:, :SG_W], dmix[:, SG_W:]
        proj = proj_ref[...].astype(F32)
        u, v = proj[:, 0:SG_W], proj[:, SG_W:2 * SG_W]
        bg, cg, hv = proj[:, 1024:1536], proj[:, 1536:2048], proj[:, 2048:2560]
        gu, gv = _gelu(u), _gelu(v)

        a_heads, dgv_heads = [], []
        for hd in range(SG_HEADS):
            sl = slice(hd * SG_HD, (hd + 1) * SG_HD)
            g_h = lng_ref[:, sl]
            vn, xh, rr = _ln_head(gv[:, sl], g_h)
            vnb = vn.astype(BF16)
            wm = _tril_bf16(ws_ref[hd])
            bcol = bst_ref[:, hd:hd + 1]
            mixed_c, dvn_c = [], []
            dw_acc = jnp.zeros((128, 128), F32)
            db_acc = jnp.zeros((128, 1), F32)
            for k in range(TB // SG_CHUNK):
                rs = slice(k * SG_CHUNK, (k + 1) * SG_CHUNK)
                mixed = _dot(wm, vnb[rs], 1, 0) + bcol
                dmixed = da[rs, sl] * gu[rs, sl]
                dmb = dmixed.astype(BF16)
                dvn_c.append(_dot(wm, dmb, 0, 0))
                dw_acc = dw_acc + _dot(dmb, vnb[rs], 1, 1)
                db_acc = db_acc + jnp.sum(dmixed, axis=1, keepdims=True)
                mixed_c.append(mixed)
            mixed_h = jnp.concatenate(mixed_c, axis=0)
            dvn = jnp.concatenate(dvn_c, axis=0)
            r_i = lax.broadcasted_iota(jnp.int32, (128, 128), 0)
            c_i = lax.broadcasted_iota(jnp.int32, (128, 128), 1)
            dws_ref[hd] += jnp.where(r_i >= c_i, dw_acc, 0.0)
            dbc_ref[:, hd:hd + 1] += db_acc
            dlng_ref[:, sl] += jnp.sum(dvn * xh, axis=0, keepdims=True)
            dxh = dvn * g_h
            dgv = rr * (dxh - jnp.mean(dxh, axis=-1, keepdims=True)
                        - xh * jnp.mean(dxh * xh, axis=-1, keepdims=True))
            a_heads.append(gu[:, sl] * mixed_h)
            dproj_ref[:, sl] = (da[:, sl] * mixed_h * _gelu_grad(u[:, sl])).astype(BF16)
            dgv_heads.append(dgv * _gelu_grad(v[:, sl]))
        dproj_ref[:, SG_W:2 * SG_W] = jnp.concatenate(dgv_heads, axis=1).astype(BF16)
        mix_ref[:, :SG_W] = jnp.concatenate(a_heads, axis=1).astype(BF16)

        z = cg * hv
        pt = ptail_ref[...].astype(F32)
        tail = jnp.where(blk % nbs == 0, 0.0, pt[:, 1536:2048] * pt[:, 2048:2560])
        y, zs1, zs2 = _conv_fwd(z, tail, cw_ref)
        mix_ref[:, SG_W:] = (bg * y).astype(BF16)
        dy = db * bg
        head = jnp.where(blk % nbs == nbs - 1, 0.0, head_ref[...])
        ext = jnp.concatenate([dy, head], axis=0)
        dz = (cw_ref[2:3, :] * dy + cw_ref[1:2, :] * _shift_up(ext, 1)[:TB]
              + cw_ref[0:1, :] * _shift_up(ext, 2)[:TB])
        head_ref[...] = dy[:HALO, :]
        dcw_ref[2:3, :] += jnp.sum(dy * z, axis=0, keepdims=True)
        dcw_ref[1:2, :] += jnp.sum(dy * zs1, axis=0, keepdims=True)
        dcw_ref[0:1, :] += jnp.sum(dy * zs2, axis=0, keepdims=True)
        dproj_ref[:, 1024:1536] = (db * y).astype(BF16)
        dproj_ref[:, 1536:2048] = (dz * hv).astype(BF16)
        dproj_ref[:, 2048:2560] = (dz * cg).astype(BF16)

        dh = _dot(dproj_ref[...], win_ref[...].reshape(EVEN_IN, D), 1, 0)
        dxn, dgam = _rms_bwd(xv, r, gam, dh)
        dgam_ref[...] += dgam
        dx0_ref[...] = dx1 + dxn

    def rev(w):
        return pl.BlockSpec((TB, w), lambda i: (nb - 1 - i, 0))

    ptail = pl.BlockSpec((HALO, EVEN_IN), lambda i: (jnp.maximum((nb - 1 - i) * (TB // HALO) - 1, 0), 0))
    return _call(
        body, "even_bwd", (nb,),
        [rev(D), rev(EVEN_IN), ptail, rev(D), _const_spec((1, D)), _wspec(N_EIN, OFF_EIN),
         _wspec(N_SQ, OFF_EOUT), _const_spec((1, SG_W)), _const_spec((SG_HEADS, 128, 128)),
         _const_spec((128, 128)), _const_spec((8, SC_W))],
        [rev(D), rev(EVEN_IN), rev(D), rev(D), _const_spec((1, D)), _const_spec((SG_HEADS, 128, 128)),
         _const_spec((128, 128)), _const_spec((1, SG_W)), _const_spec((8, SC_W))],
        [jax.ShapeDtypeStruct((t, D), F32), jax.ShapeDtypeStruct((t, EVEN_IN), BF16),
         jax.ShapeDtypeStruct((t, D), BF16), jax.ShapeDtypeStruct((t, D), BF16),
         jax.ShapeDtypeStruct((1, D), F32), jax.ShapeDtypeStruct((SG_HEADS, 128, 128), F32),
         jax.ShapeDtypeStruct((128, 128), F32), jax.ShapeDtypeStruct((1, SG_W), F32),
         jax.ShapeDtypeStruct((8, SC_W), F32)],
        (x, proj, proj, dx1, gamma, wg, wg, lng, ws, bst, cw), [pltpu.VMEM((HALO, SC_W), F32)], comm)


def _ffn_fwd(x, w_gu, w_d, gamma, name, comm=None, target=None):
    t = x.shape[0]
    last = target is not None

    def body(*refs):
        x_ref, gam_ref, wg_ref, wu_ref, wd_ref = refs[:5]
        y_ref, g_ref, u_ref = refs[5 + last:8 + last]
        xv = x_ref[...]
        h, _ = _rms(xv, gam_ref[...])
        hb = h.astype(BF16)
        g = _dot(hb, wg_ref[...].reshape(D_FF, D), 1, 1)
        u = _dot(hb, wu_ref[...].reshape(D_FF, D), 1, 1)
        g_ref[...] = g.astype(BF16)
        u_ref[...] = u.astype(BF16)
        act = g * jax.nn.sigmoid(g) * u
        y = xv + _dot(act.astype(BF16), wd_ref[...].reshape(D_FF, D), 1, 0)
        if not last:
            y_ref[...] = y
            return
        loss_ref = refs[9]

        @pl.when(pl.program_id(0) == 0)
        def _():
            loss_ref[...] = jnp.zeros_like(loss_ref)

        err = y - refs[5][...]
        y_ref[...] = err * (1.0 / D)
        sq = jnp.sum(jnp.sum(err * err, axis=-1, keepdims=True), axis=0, keepdims=True)
        loss_ref[...] += (0.5 / D) * sq

    row = pl.BlockSpec((TB, D), lambda i: (i, 0))
    wide = pl.BlockSpec((TB, D_FF), lambda i: (i, 0))
    in_specs = [row, _const_spec((1, D)), _wspec(N_FF, OFF_GATE), _wspec(N_FF, OFF_UP), _wspec(N_FF, 0)]
    out_specs = [row, wide, wide]
    out_shape = [jax.ShapeDtypeStruct((t, D), F32), jax.ShapeDtypeStruct((t, D_FF), BF16),
                 jax.ShapeDtypeStruct((t, D_FF), BF16)]
    args = (x, gamma, w_gu, w_gu, w_d)
    if last:
        in_specs, args = in_specs + [row], args + (target,)
        out_specs, out_shape = out_specs + [_const_spec((8, 128))], out_shape + [jax.ShapeDtypeStruct((8, 128), F32)]
    return _call(body, name, (t // TB,), in_specs, out_specs, out_shape, args, (), comm)


def _ffn_up(x, w_gu, gamma, name, comm=None):
    t = x.shape[0]

    def body(x_ref, gam_ref, wg_ref, wu_ref, g_ref, u_ref, act_ref):
        h, _ = _rms(x_ref[...], gam_ref[...])
        hb = h.astype(BF16)
        g = _dot(hb, wg_ref[...].reshape(D_FF, D), 1, 1)
        u = _dot(hb, wu_ref[...].reshape(D_FF, D), 1, 1)
        g_ref[...] = g.astype(BF16)
        u_ref[...] = u.astype(BF16)
        act_ref[...] = (g * jax.nn.sigmoid(g) * u).astype(BF16)

    row = pl.BlockSpec((TB, D), lambda i: (i, 0))
    wide = pl.BlockSpec((TB, D_FF), lambda i: (i, 0))
    return _call(body, name, (t // TB,), [row, _const_spec((1, D)), _wspec(N_FF, OFF_GATE), _wspec(N_FF, OFF_UP)],
                 [wide, wide, wide], [jax.ShapeDtypeStruct((t, D_FF), BF16)] * 3, (x, gamma, w_gu, w_gu), (), comm)


def _ffn_down(x, act, w_d, name, comm=None):
    t = x.shape[0]

    def body(x_ref, a_ref, wd_ref, y_ref):
        y_ref[...] = x_ref[...] + _dot(a_ref[...], wd_ref[...].reshape(D_FF, D), 1, 0)

    row = pl.BlockSpec((TB, D), lambda i: (i, 0))
    wide = pl.BlockSpec((TB, D_FF), lambda i: (i, 0))
    return _call(body, name, (t // TB,), [row, wide, _wspec(N_FF, 0)], [row], [jax.ShapeDtypeStruct((t, D), F32)],
                 (x, act, w_d), (), comm)


def _ffn_bwd(x, g, u, dy, w_gu, w_d, gamma, name, comm=None):
    t = x.shape[0]

    def body(x_ref, g_ref, u_ref, dy_ref, gam_ref, wg_ref, wu_ref, wd_ref,
             dx_ref, act_ref, dg_ref, du_ref, h_ref, dgam_ref):
        @pl.when(pl.program_id(0) == 0)
        def _():
            dgam_ref[...] = jnp.zeros_like(dgam_ref)

        xv = x_ref[...]
        gam = gam_ref[...]
        h, r = _rms(xv, gam)
        h_ref[...] = h.astype(BF16)
        dyv = dy_ref[...]
        dact = _dot(dyv.astype(BF16), wd_ref[...].reshape(D_FF, D), 1, 1)
        gv = g_ref[...].astype(F32)
        uv = u_ref[...].astype(F32)
        sg = jax.nn.sigmoid(gv)
        silu = gv * sg
        act_ref[...] = (silu * uv).astype(BF16)
        dgb = (dact * uv * (sg * (1.0 + gv * (1.0 - sg)))).astype(BF16)
        dub = (dact * silu).astype(BF16)
        dg_ref[...] = dgb
        du_ref[...] = dub
        dh = _dot(dgb, wg_ref[...].reshape(D_FF, D), 1, 0) + _dot(dub, wu_ref[...].reshape(D_FF, D), 1, 0)
        dxn, dgam = _rms_bwd(xv, r, gam, dh)
        dgam_ref[...] += dgam
        dx_ref[...] = dyv + dxn

    row = pl.BlockSpec((TB_FFN_BWD, D), lambda i: (i, 0))
    wide = pl.BlockSpec((TB_FFN_BWD, D_FF), lambda i: (i, 0))
    return _call(
        body, name, (t // TB_FFN_BWD,),
        [row, wide, wide, row, _const_spec((1, D)), _wspec(N_FF, OFF_GATE), _wspec(N_FF, OFF_UP),
         _wspec(N_FF, 0)],
        [row, wide, wide, wide, row, _const_spec((1, D))],
        [jax.ShapeDtypeStruct((t, D), F32), jax.ShapeDtypeStruct((t, D_FF), BF16),
         jax.ShapeDtypeStruct((t, D_FF), BF16), jax.ShapeDtypeStruct((t, D_FF), BF16),
         jax.ShapeDtypeStruct((t, D), BF16), jax.ShapeDtypeStruct((1, D), F32)],
        (x, g, u, dy, gamma, w_gu, w_gu, w_d), (), comm)


def _odd_pre_fwd(x, wg, gamma, qbt, kvbt, qa_g, kva_g, pw_bd, pscale, seq, comm=None):
    t = x.shape[0]
    nbs = seq // TB

    def body(x_ref, gam_ref, win_ref, qb_ref, kvb_ref, qa_ref, kva_ref, pw_ref, ps_ref,
             proj_ref, q_ref, kv_ref, kr_ref, c_ref, tail_ref):
        i = pl.program_id(0)
        h, _ = _rms(x_ref[...], gam_ref[...])
        proj = _dot(h.astype(BF16), win_ref[...].reshape(D, D), 1, 0)
        proj_ref[...] = proj.astype(BF16)
        zp, ql, kvl = proj[:, :POOL_W], proj[:, 256:640], proj[:, 640:896]
        kr_ref[...] = proj[:, 896:1024]
        qn, _ = _rms(ql, qa_ref[...])
        q_ref[...] = _dot(qn.astype(BF16), qb_ref[...], 1, 1).astype(BF16)
        kvn, _ = _rms(kvl, kva_ref[...])
        kv_ref[...] = _dot(kvn.astype(BF16), kvb_ref[...], 1, 1).astype(BF16)
        tail = jnp.where(i % nbs == 0, 0.0, tail_ref[...])
        pooled, _, _ = _pool_fwd(zp, tail, i % nbs)
        tail_ref[...] = zp[TB - HALO:, :]
        c_ref[...] = (_dot(pooled.astype(BF16), pw_ref[...], 1, 0) * ps_ref[...]).astype(BF16)

    def row(w):
        return pl.BlockSpec((TB, w), lambda i: (i, 0))

    return _call(
        body, "odd_pre_fwd", (t // TB,),
        [row(D), _const_spec((1, D)), _wspec(N_SQ, OFF_OIN), _const_spec((HEADS * HP, Q_LORA)),
         _const_spec((HEADS * HP, KV_LORA)), _const_spec((1, Q_LORA)), _const_spec((1, KV_LORA)),
         _const_spec((POOL_W, POOL_W)), _const_spec((1, POOL_W))],
        [row(D), row(HEADS * HP), row(HEADS * HP), row(128), row(POOL_W)],
        [jax.ShapeDtypeStruct((t, D), BF16), jax.ShapeDtypeStruct((t, HEADS * HP), BF16),
         jax.ShapeDtypeStruct((t, HEADS * HP), BF16), jax.ShapeDtypeStruct((t, 128), F32),
         jax.ShapeDtypeStruct((t, POOL_W), BF16)],
        (x, gamma, wg, qbt, kvbt, qa_g, kva_g, pw_bd, pscale), [pltpu.VMEM((HALO, POOL_W), F32)], comm)


def _odd_pre_bwd(x, proj, dx3, dmix, dq, dkv, dkr, wg, gamma, qbt, kvbt, qa_g, kva_g, pw_bd, pscale, seq):
    t = x.shape[0]
    nb, nbs = t // TB, seq // TB

    def body(x_ref, proj_ref, ptail_ref, dx3_ref, dco_ref, dq_ref, dkv_ref, dkr_ref, gam_ref, win_ref, qb_ref,
             kvb_ref, qa_ref, kva_ref, pw_ref, ps_ref,
             dx2_ref, dproj_ref, h_ref, qn_ref, kvn_ref, dgam_ref, dqa_ref, dkva_ref, dpw_ref, dps_ref, head_ref):
        i = pl.program_id(0)
        blk = nb - 1 - i

        @pl.when(i == 0)
        def _():
            dgam_ref[...] = jnp.zeros_like(dgam_ref)
            dqa_ref[...] = jnp.zeros_like(dqa_ref)
            dkva_ref[...] = jnp.zeros_like(dkva_ref)
            dpw_ref[...] = jnp.zeros_like(dpw_ref)
            dps_ref[...] = jnp.zeros_like(dps_ref)

        xv = x_ref[...]
        gam = gam_ref[...]
        h, r = _rms(xv, gam)
        h_ref[...] = h.astype(BF16)
        proj = proj_ref[...].astype(F32)
        zp, ql, kvl = proj[:, :POOL_W], proj[:, 256:640], proj[:, 640:896]

        qa = qa_ref[...]
        qn, rq = _rms(ql, qa)
        qn_ref[...] = qn.astype(BF16)
        dql, dqa = _rms_bwd(ql, rq, qa, _dot(dq_ref[...], qb_ref[...], 1, 0))
        dqa_ref[...] += dqa
        kva = kva_ref[...]
        kvn, rkv = _rms(kvl, kva)
        kvn_ref[...] = kvn.astype(BF16)
        dkvl, dkva = _rms_bwd(kvl, rkv, kva, _dot(dkv_ref[...], kvb_ref[...], 1, 0))
        dkva_ref[...] += dkva

        pt = ptail_ref[...].astype(F32)
        tail = jnp.where(blk % nbs == 0, 0.0, pt[:, :POOL_W])
        pooled, cnt, grp = _pool_fwd(zp, tail, blk % nbs)
        pb = pooled.astype(BF16)
        pw = pw_ref[...]
        dco = dco_ref[...].astype(F32)
        dps_ref[...] += jnp.sum(dco * _dot(pb, pw, 1, 0), axis=0, keepdims=True)
        dpo = (dco * ps_ref[...]).astype(BF16)
        dpw_ref[...] += _dot(pb, dpo, 0, 0)
        dpooled = _dot(dpo, pw, 1, 1)
        dpm = dpooled / cnt
        head = jnp.where(blk % nbs == nbs - 1, 0.0, head_ref[...])
        dz = _pool_bwd(dpooled, dpm, head, grp)
        head_ref[...] = dpm[:HALO, :]

        dproj_ref[:, :POOL_W] = dz.astype(BF16)
        dproj_ref[:, 256:640] = dql.astype(BF16)
        dproj_ref[:, 640:896] = dkvl.astype(BF16)
        dproj_ref[:, 896:1024] = dkr_ref[...].astype(BF16)
        dh = _dot(dproj_ref[...], win_ref[...].reshape(D, D), 1, 1)
        dxn, dgam = _rms_bwd(xv, r, gam, dh)
        dgam_ref[...] += dgam
        dx2_ref[...] = dx3_ref[...] + dxn

    def rev(w):
        return pl.BlockSpec((TB, w), lambda i: (nb - 1 - i, 0))

    ptail = pl.BlockSpec((HALO, D), lambda i: (jnp.maximum((nb - 1 - i) * (TB // HALO) - 1, 0), 0))
    return pl.pallas_call(
        body, name="odd_pre_bwd",
        out_shape=[jax.ShapeDtypeStruct((t, D), F32), jax.ShapeDtypeStruct((t, D), BF16),
                   jax.ShapeDtypeStruct((t, D), BF16), jax.ShapeDtypeStruct((t, Q_LORA), BF16),
                   jax.ShapeDtypeStruct((t, KV_LORA), BF16), jax.ShapeDtypeStruct((1, D), F32),
                   jax.ShapeDtypeStruct((1, Q_LORA), F32), jax.ShapeDtypeStruct((1, KV_LORA), F32),
                   jax.ShapeDtypeStruct((POOL_W, POOL_W), F32), jax.ShapeDtypeStruct((1, POOL_W), F32)],
        grid=(nb,),
        in_specs=[rev(D), rev(D), ptail, rev(D), rev(POOL_W), rev(HEADS * HP), rev(HEADS * HP), rev(128),
                  _const_spec((1, D)), _wspec(N_SQ, OFF_OIN), _const_spec((HEADS * HP, Q_LORA)),
                  _const_spec((HEADS * HP, KV_LORA)), _const_spec((1, Q_LORA)), _const_spec((1, KV_LORA)),
                  _const_spec((POOL_W, POOL_W)), _const_spec((1, POOL_W))],
        out_specs=[rev(D), rev(D), rev(D), rev(Q_LORA), rev(KV_LORA), _const_spec((1, D)), _const_spec((1, Q_LORA)),
                   _const_spec((1, KV_LORA)), _const_spec((POOL_W, POOL_W)), _const_spec((1, POOL_W))],
        scratch_shapes=[pltpu.VMEM((HALO, POOL_W), F32)],
        compiler_params=_cparams(1),
    )(x, proj, proj, dx3, dmix, dq, dkv, dkr, gamma, wg, qbt, kvbt, qa_g, kva_g, pw_bd, pscale)


def _attn_specs(seq):
    head = pl.BlockSpec((seq, HP), lambda b, h: (b, h))
    shared = pl.BlockSpec((seq, 128), lambda b, h: (b, 0))
    gain = pl.BlockSpec((1, HP), lambda b, h: (0, 0))
    return head, shared, gain


def _causal_bias(n):
    rows = lax.broadcasted_iota(jnp.int32, (n, n), 0)
    cols = lax.broadcasted_iota(jnp.int32, (n, n), 1)
    return jnp.where(cols <= rows, 0.0, NEG_INF)


def _attn_fwd(q, kv, kr, cos, sin, gq, gk, seq, comm=None):
    t = q.shape[0]
    qb = min(512, seq)

    def body(q_ref, kv_ref, kr_ref, c_ref, s_ref, gq_ref, gk_ref, o_ref, lse_ref):
        c, s = c_ref[...], s_ref[...]
        qf, _ = _qk_prep(q_ref[...].astype(F32), gq_ref[...], c, s)
        kin = jnp.concatenate([kv_ref[:, :128].astype(F32), kr_ref[...]], axis=1)
        kf, _ = _qk_prep(kin, gk_ref[...], c, s)
        qf, kf = qf.astype(BF16), kf.astype(BF16)
        v1 = jnp.concatenate([kv_ref[:, 128:], jnp.ones((seq, V_DIM), BF16)], axis=1)
        bias = _causal_bias(qb)
        for q0 in range(0, seq, qb):
            q1 = q0 + qb
            qblk = qf[q0:q1]
            s_dg = _dot(qblk, kf[q0:q1], 1, 1) + bias
            m = jnp.max(s_dg, axis=-1, keepdims=True)
            if q0:
                s_off = _dot(qblk, kf[:q0], 1, 1)
                m = jnp.maximum(m, jnp.max(s_off, axis=-1, keepdims=True))
            acc = _dot(jnp.exp(s_dg - m).astype(BF16), v1[q0:q1], 1, 0)
            if q0:
                acc = acc + _dot(jnp.exp(s_off - m).astype(BF16), v1[:q0], 1, 0)
            l = acc[:, V_DIM:]
            o_ref[q0:q1, :] = (acc[:, :V_DIM] / l).astype(BF16)
            lse_ref[q0:q1, :] = m + jnp.log(l)

    head, shared, gain = _attn_specs(seq)
    per_head = pl.BlockSpec((seq, V_DIM), lambda b, h: (b, h))
    return _call(
        body, "attn_fwd", (t // seq, HEADS),
        [head, head, shared, shared, shared, gain, gain], [per_head, per_head],
        [jax.ShapeDtypeStruct((t, HEADS * V_DIM), BF16), jax.ShapeDtypeStruct((t, HEADS * V_DIM), F32)],
        (q, kv, kr, cos, sin, gq, gk), (), comm)


def _attn_bwd(q, kv, kr, cos, sin, gq, gk, dmix, d_out, lse, seq, comm=None):
    t = q.shape[0]
    qb = min(512, seq)

    def body(q_ref, kv_ref, kr_ref, c_ref, s_ref, gq_ref, gk_ref, do_ref, o_ref, lse_ref,
             dq_ref, dkv_ref, dkr_ref, dgq_ref, dgk_ref, dqf_ref, dkf_ref, dv_ref):
        b, hd = pl.program_id(0), pl.program_id(1)

        @pl.when((b == 0) & (hd == 0))
        def _():
            dgq_ref[...] = jnp.zeros_like(dgq_ref)
            dgk_ref[...] = jnp.zeros_like(dgk_ref)

        c, sn = c_ref[...], s_ref[...]
        gq_v, gk_v = gq_ref[...], gk_ref[...]
        qin = q_ref[...].astype(F32)
        kin = jnp.concatenate([kv_ref[:, :128].astype(F32), kr_ref[...]], axis=1)
        qf32, rq = _qk_prep(qin, gq_v, c, sn)
        kf32, rk = _qk_prep(kin, gk_v, c, sn)
        qf, kf = qf32.astype(BF16), kf32.astype(BF16)
        vb = kv_ref[:, 128:]
        dkf_ref[...] = jnp.zeros_like(dkf_ref)
        dv_ref[...] = jnp.zeros_like(dv_ref)
        bias = _causal_bias(qb)
        for q0 in range(0, seq, qb):
            q1 = q0 + qb
            qblk = qf[q0:q1]
            do = do_ref[q0:q1, :]
            lse_col = lse_ref[q0:q1, 0:1]
            d_col = jnp.sum(do.astype(F32) * o_ref[q0:q1, :].astype(F32), axis=-1, keepdims=True)
            dq_acc = None
            for k0, k1, diag in ((q0, q1, True), (0, q0, False)):
                if k1 == k0:
                    continue
                s = _dot(qblk, kf[k0:k1], 1, 1)
                p = jnp.exp((s + bias if diag else s) - lse_col)
                dv_ref[k0:k1, :] += _dot(p.astype(BF16), do, 0, 0)
                ds = (p * (_dot(do, vb[k0:k1], 1, 1) - d_col)).astype(BF16)
                part = _dot(ds, kf[k0:k1], 1, 0)
                dq_acc = part if dq_acc is None else dq_acc + part
                dkf_ref[k0:k1, :] += _dot(ds, qblk, 0, 0)
            dqf_ref[q0:q1, :] = dq_acc
        dqin, dgq = _qk_prep_bwd(dqf_ref[...], qin, rq, gq_v, c, sn)
        dkin, dgk = _qk_prep_bwd(dkf_ref[...], kin, rk, gk_v, c, sn)
        dgq_ref[...] += dgq
        dgk_ref[...] += dgk
        dq_ref[...] = dqin.astype(BF16)
        dkv_ref[:, :128] = dkin[:, :128].astype(BF16)
        dkv_ref[:, 128:] = dv_ref[...].astype(BF16)

        @pl.when(hd == 0)
        def _():
            dkr_ref[...] = dkin[:, 128:]

        @pl.when(hd != 0)
        def _():
            dkr_ref[...] += dkin[:, 128:]

    head, shared, gain = _attn_specs(seq)
    per_head = pl.BlockSpec((seq, V_DIM), lambda b, h: (b, h))
    return _call(
        body, "attn_bwd", (t // seq, HEADS),
        [head, head, shared, shared, shared, gain, gain,
         pl.BlockSpec((seq, V_DIM), lambda b, h: (b, 2 + h)), per_head, per_head],
        [head, head, shared, gain, gain],
        [jax.ShapeDtypeStruct((t, HEADS * HP), BF16), jax.ShapeDtypeStruct((t, HEADS * HP), BF16),
         jax.ShapeDtypeStruct((t, 128), F32), jax.ShapeDtypeStruct((1, HP), F32),
         jax.ShapeDtypeStruct((1, HP), F32)],
        (q, kv, kr, cos, sin, gq, gk, dmix, d_out, lse),
        [pltpu.VMEM((seq, HP), F32), pltpu.VMEM((seq, HP), F32), pltpu.VMEM((seq, V_DIM), F32)], comm)


def _odd_post_fwd(x, c_out, d_out, wg):
    t = x.shape[0]

    def body(x_ref, c_ref, d_ref, w_ref, y_ref):
        y_ref[...] = (x_ref[...] + _dot(c_ref[...], w_ref[0:2].reshape(POOL_W, D), 1, 0)
                      + _dot(d_ref[...], w_ref[2:8].reshape(HEADS * V_DIM, D), 1, 0))

    def row(w):
        return pl.BlockSpec((TB, w), lambda i: (i, 0))

    return pl.pallas_call(
        body, name="odd_post_fwd", out_shape=jax.ShapeDtypeStruct((t, D), F32), grid=(t // TB,),
        in_specs=[row(D), row(POOL_W), row(HEADS * V_DIM), _wspec(N_SQ, OFF_OOUT)], out_specs=row(D),
        compiler_params=_cparams(1),
    )(x, c_out, d_out, wg)


def _odd_post_bwd(dx3, wg, comm=None):
    t = dx3.shape[0]

    def body(d_ref, w_ref, o_ref):
        o_ref[...] = _dot(d_ref[...].astype(BF16), w_ref[...].reshape(D, D), 1, 1).astype(BF16)

    row = pl.BlockSpec((TB, D), lambda i: (i, 0))
    (res,), extra = _call(body, "odd_post_bwd", (t // TB,), [row, _wspec(N_SQ, OFF_OOUT)], [row],
                          [jax.ShapeDtypeStruct((t, D), BF16)], (dx3, wg), (), comm)
    return res, extra


def _tn(a_list, b, tm, name, into=None, comm=None):
    t, n_out = b.shape
    widths = [a.shape[1] for a in a_list]
    tk = min(TK_DW, t)
    m, na, nk = sum(widths), len(a_list), t // tk
    assert na == 1 or tm == m

    def body(*refs):
        a_refs, b_ref, o_ref, acc_ref = refs[:na], refs[na], refs[-2], refs[-1]
        k = pl.program_id(1)

        @pl.when(k == 0)
        def _():
            acc_ref[...] = jnp.zeros_like(acc_ref)

        bb = b_ref[...].astype(BF16)
        m0 = 0
        for a_ref, w in zip(a_refs, widths):
            rows = slice(0, tm) if na == 1 else slice(m0, m0 + w)
            acc_ref[rows, :] += _dot(a_ref[...].astype(BF16), bb, 0, 0)
            m0 += w

        @pl.when(k == nk - 1)
        def _():
            o_ref[...] = acc_ref[...].astype(BF16).reshape(o_ref.shape)

    if na == 1:
        in_specs = [pl.BlockSpec((tk, tm), lambda i, k: (k, i))]
    else:
        in_specs = [pl.BlockSpec((tk, w), lambda i, k: (k, 0)) for w in widths]
    in_specs.append(pl.BlockSpec((tk, n_out), lambda i, k: (k, 0)))
    args = list(a_list) + [b]
    if into is None:
        out_spec = pl.BlockSpec((tm, n_out), lambda i, k: (i, 0))
        out_shape = jax.ShapeDtypeStruct((m, n_out), BF16)
        aliases = {}
    else:
        buf, n, off = into
        assert n_out == D and tm % n == 0 and off % n == 0 and (na == 1 or tm // n == N_DEV)
        idx = off // n
        out_spec = pl.BlockSpec((tm // n, n, D), lambda i, k: (i, idx, 0))
        out_shape = jax.ShapeDtypeStruct(buf.shape, BF16)
        in_specs.append(pl.BlockSpec(memory_space=pl.ANY))
        args.append(buf)
        aliases = {len(args) - 1: 0}
    (res,), extra = _call(body, name, (m // tm, nk), in_specs, [out_spec], [out_shape], args,
                          [pltpu.VMEM((tm, n_out), F32)], comm, aliases)
    return (res, extra) if comm is not None else res


def _adamw(ws, gs, ms, vs, name, nblk=1):
    n = len(ws)
    c1 = 1.0 - B1 ** STEP
    c2 = 1.0 - B2 ** STEP

    def body(*refs):
        for a in range(n):
            w, g, m, v = (refs[k * n + a][...] for k in range(4))
            d_ref, m_ref, v_ref = (refs[(4 + k) * n + a] for k in range(3))
            m_new = B1 * m + (1.0 - B1) * g
            v_new = B2 * v + (1.0 - B2) * (g * g)
            d_ref[...] = -LR * ((m_new / c1) / (jnp.sqrt(v_new / c2) + ADAM_EPS) + WD * w)
            m_ref[...] = m_new
            v_ref[...] = v_new

    grid = (nblk,)
    assert all(w.shape[0] % nblk == 0 and (nblk == 1 or (w.shape[0] // nblk) % 8 == 0) for w in ws)
    specs = [pl.BlockSpec((w.shape[0] // nblk, w.shape[1]), lambda i: (i, 0)) for w in ws]
    outs, _ = _call(body, name, grid, specs * 4, specs * 3, [jax.ShapeDtypeStruct(w.shape, F32) for w in ws] * 3,
                    (*ws, *gs, *ms, *vs))
    return outs[:n], outs[n:2 * n], outs[2 * n:]


def _rows1024(a, rows):
    flat = a.reshape(-1, D)
    return jnp.pad(flat, ((0, rows - flat.shape[0]), (0, 0)))


def _pack_shards(even_w_in, even_w_out, odd_w_in, q_b, kv_b, odd_w_out, ffn_w_gate, ffn_w_up, ffn_w_down):
    mix0 = jnp.concatenate([even_w_in[0].T, jnp.zeros((OFF_EOUT - N_EIN, D), F32), even_w_out[0]], axis=0)
    gu = [jnp.concatenate([ffn_w_gate[layer].T, ffn_w_up[layer].T], axis=0) for layer in range(2)]
    mix1 = jnp.concatenate([jnp.pad(odd_w_in[0], ((0, 0), (0, D - ODD_IN))), odd_w_out[0],
                            _rows1024(q_b[0].T, N_QB), _rows1024(kv_b[0].T, N_KVB),
                            jnp.zeros((R_MIX1 - OFF_KVB - N_KVB, D), F32)], axis=0)
    return [c.astype(BF16) for c in (mix0, gu[0], ffn_w_down[0], mix1, gu[1], ffn_w_down[1])]


def _pad_heads(a):
    k = a.shape[1]
    return jnp.pad(a.reshape(HEADS, QK_DIM, k), ((0, 0), (0, HP - QK_DIM), (0, 0))).reshape(HEADS * HP, k)


def _small_pack(parts):
    flat = []
    for p in parts:
        v = p.reshape(-1)
        flat.append(jnp.pad(v, (0, (-v.shape[0]) % 1024)))
    return jnp.concatenate(flat).reshape(-1, 128)


def _small_unpack(buf, shapes):
    flat = buf.reshape(-1)
    out, off = [], 0
    for s in shapes:
        size = int(np.prod(s))
        out.append(flat[off:off + size].reshape(s))
        off += size + (-size) % 1024
    return out


def _step(x3d, positions, target3d, chunks, tile, c_arr, mix_norm, ffn_norm, sg_ln_g, sg_w_s, sg_b_s,
          pool_w, q_norm, k_norm):
    bsz, seq, _ = x3d.shape
    t = bsz * seq
    x0 = x3d.reshape(t, D)
    target = target3d.reshape(t, D)
    my_mix0, my_gu0, my_d0, my_mix1, my_gu1, my_d1 = chunks

    lane = np.arange(128)
    inv_freq = np.where(lane < QK_ROPE, ROPE_THETA ** (-(2.0 * (lane % 32)) / QK_ROPE), 0.0)
    inv_freq = jnp.asarray(inv_freq.reshape(1, 128), F32)
    (cos, sin), (w_mix0, tiles) = _rope_tables(positions.reshape(t, 1), inv_freq, _gather_comm([my_mix0, tile]))

    conv_w = tiles[:, 0:3, 0:64].transpose(1, 0, 2).reshape(3, SC_W)
    pool_scale = tiles[:, 3, 0:32].reshape(1, POOL_W)
    q_a_norm = tiles[:, 4, 0:48].reshape(1, Q_LORA)
    kv_a_norm = tiles[:, 5, 0:32].reshape(1, KV_LORA)
    ws = sg_w_s[0]
    bst = jnp.pad(sg_b_s[0].T, ((0, 0), (0, 128 - SG_HEADS)))
    cw = jnp.pad(conv_w, ((0, 8 - 3), (0, 0)))
    pw_bd = jax.scipy.linalg.block_diag(*[pool_w[0, g] for g in range(4)]).astype(BF16)
    gq = jnp.pad(q_norm * ATT_SCALE, ((0, 0), (0, HP - QK_DIM)))
    gk = jnp.pad(k_norm, ((0, 0), (0, HP - QK_DIM)))

    (x1, proj_e), (w_gu0,) = _even_fwd(x0, w_mix0, mix_norm[0:1], sg_ln_g, ws, bst, cw, seq, _gather_comm([my_gu0]))
    (g0, u0, act0), (w_d0, w_mix1) = _ffn_up(x1, w_gu0, ffn_norm[0:1], "ffn_up0", _gather_comm([my_d0, my_mix1]))
    (x2,), (w_d1,) = _ffn_down(x1, act0, w_d0, "ffn_down0", _gather_comm([my_d1]))
    qbt = _pad_heads(w_mix1[:, OFF_QB:OFF_QB + N_QB_USED, :].reshape(HEADS * QK_DIM, Q_LORA))
    kvbt = w_mix1[:, OFF_KVB:OFF_KVB + N_KVB, :].reshape(HEADS * HP, KV_LORA)
    (proj_o, q, kv, kr, c_out), _ = _odd_pre_fwd(x2, w_mix1, mix_norm[1:2], qbt, kvbt, q_a_norm, kv_a_norm,
                                                pw_bd, pool_scale, seq)
    (d_out, lse), (w_gu1,) = _attn_fwd(q, kv, kr, cos, sin, gq, gk, seq, _gather_comm([my_gu1]))
    x3 = _odd_post_fwd(x2, c_out, d_out, w_mix1)
    (dy, g1, u1, loss_tile), _ = _ffn_fwd(x3, w_gu1, w_d1, ffn_norm[1:2], "ffn_fwd1", None, target)

    def chunk(rows, padded=False):
        return jnp.zeros((N_DEV, rows, D), BF16) if padded else lax.empty((N_DEV, rows, D), BF16)

    (dx3, act1, dg1, du1, h3, dgam_f1), _ = _ffn_bwd(x3, g1, u1, dy, w_gu1, w_d1, ffn_norm[1:2], "ffn_bwd1")
    gp_ffn1 = _tn([dg1], h3, 1408, "dw_gate1", (chunk(R_GU + N_FF), N_FF, OFF_GATE))
    gp_ffn1 = _tn([du1], h3, 1408, "dw_up1", (gp_ffn1, N_FF, OFF_UP))
    gp_ffn1 = _tn([act1], dy, 1408, "dw_down1", (gp_ffn1, N_FF, R_GU))

    dmix_o, (ga_ffn1,) = _odd_post_bwd(dx3, w_mix1, _pair_exchange_comm(gp_ffn1))
    pb_ffn1 = _rs_pair_sum(gp_ffn1, ga_ffn1, c_arr, "rs_pair_sum_ffn1")
    gp_mix1 = _tn([c_out, d_out], dx3, D, "dw_oout", (chunk(R_MIX1, True), N_SQ, OFF_OOUT))
    (dq, dkv, dkr, dgq, dgk), (gb_ffn1,) = _attn_bwd(q, kv, kr, cos, sin, gq, gk, dmix_o, d_out, lse, seq,
                                                    _chip_exchange_comm(pb_ffn1))
    (dx2, dproj_o, h2, qn, kvn, dgam_m1, dqa, dkva, dpw_bd, dps) = _odd_pre_bwd(
        x2, proj_o, dx3, dmix_o, dq, dkv, dkr, w_mix1, mix_norm[1:2], qbt, kvbt, q_a_norm, kv_a_norm, pw_bd,
        pool_scale, seq)
    gp_mix1 = _tn([h2], dproj_o, D, "dw_oin", (gp_mix1, N_SQ, OFF_OIN))
    d_qbt = _tn([dq], qn, HEADS * HP, "dw_qb")
    d_qb_rows = d_qbt.reshape(HEADS, HP, Q_LORA)[:, :QK_DIM].reshape(N_DEV, N_QB_USED, D)
    d_kvb_rows = _tn([dkv], kvn, HEADS * HP, "dw_kvb").reshape(N_DEV, N_KVB, D)
    gp_mix1 = lax.dynamic_update_slice(gp_mix1, d_qb_rows, (0, OFF_QB, 0))
    gp_mix1 = lax.dynamic_update_slice(gp_mix1, d_kvb_rows, (0, OFF_KVB, 0))

    (dx1, act0, dg0, du0, h1, dgam_f0), (ga_mix1,) = _ffn_bwd(x1, g0, u0, dx2, w_gu0, w_d0, ffn_norm[0:1], "ffn_bwd0",
                                                             _pair_exchange_comm(gp_mix1))
    pb_mix1 = _rs_pair_sum(gp_mix1, ga_mix1, c_arr, "rs_pair_sum_mix1")
    gp_ffn0a, (gb_mix1,) = _tn([dg0], h1, 1408, "dw_gate0", (chunk(R_GU), N_FF, OFF_GATE),
                               _chip_exchange_comm(pb_mix1))
    gp_ffn0a = _tn([du0], h1, 1408, "dw_up0", (gp_ffn0a, N_FF, OFF_UP))
    gp_ffn0b, (ga_ffn0a,) = _tn([act0], dx2, 1408, "dw_down0", (chunk(N_FF), N_FF, 0),
                                _pair_exchange_comm(gp_ffn0a))
    pb_ffn0a = _rs_pair_sum(gp_ffn0a, ga_ffn0a, c_arr, "rs_pair_sum_ffn0a")

    (dx0, dproj_e, mix_e, h0, dgam_m0, dws, dbc, dlng, dcw), (gb_ffn0a, ga_ffn0b) = _even_bwd(
        x0, proj_e, dx1, w_mix0, mix_norm[0:1], sg_ln_g, ws, bst, cw, seq,
        _both(_chip_exchange_comm(pb_ffn0a), _pair_exchange_comm(gp_ffn0b)))
    pb_ffn0b = _rs_pair_sum(gp_ffn0b, ga_ffn0b, c_arr, "rs_pair_sum_ffn0b")

    small = _small_pack([
        jnp.concatenate([dgam_m0, dgam_m1], 0), jnp.concatenate([dgam_f0, dgam_f1], 0), dlng,
        dws[None], dbc[:, :SG_HEADS].T[None], dcw[:3],
        jnp.stack([dpw_bd[g * POOL_GD:(g + 1) * POOL_GD, g * POOL_GD:(g + 1) * POOL_GD] for g in range(4)])[None],
        dps, dqa, dkva, dgq[:, :QK_DIM] * ATT_SCALE, dgk[:, :QK_DIM], loss_tile[0:1, 0:1]])
    gp_mix0, (small_all,) = _tn([mix_e], dx1, D, "dw_eout", (chunk(R_MIX0, True), N_SQ, OFF_EOUT),
                                _gather_comm([small]))
    gp_mix0, (gb_ffn0b,) = _tn([dproj_e], h0, 1280, "dw_ein", (gp_mix0, N_EIN, OFF_EIN),
                               _chip_exchange_comm(pb_ffn0b))
    small_sum = _small_unpack(_sum_gathered(small_all), SMALL_SHAPES)
    partials = ([pb_ffn0a, pb_ffn0b, pb_mix1, pb_ffn1], [gb_ffn0a, gb_ffn0b, gb_mix1, gb_ffn1])
    return dx0.reshape(bsz, seq, D), partials, gp_mix0, small_sum


SMALL_SHAPES = [(2, D), (2, D), (1, SG_W), (1, SG_HEADS, 128, 128), (1, SG_HEADS, 128), (3, SC_W),
                (1, 4, POOL_GD, POOL_GD), (1, POOL_W), (1, Q_LORA), (1, KV_LORA), (1, QK_DIM), (1, QK_DIM), (1, 1)]


def kernel(x, positions, mix_norm, ffn_norm, even_w_in, sg_ln_g, sg_w_s, sg_b_s, sc_conv_w, even_w_out, odd_w_in, pool_w, pool_scale, q_a_norm, q_b, kv_a_norm, kv_b, q_norm, k_norm, odd_w_out, ffn_w_gate, ffn_w_up, ffn_w_down, loss_target, m_mix_norm, m_ffn_norm, m_even_w_in, m_sg_ln_g, m_sg_w_s, m_sg_b_s, m_sc_conv_w, m_even_w_out, m_odd_w_in, m_pool_w, m_pool_scale, m_q_a_norm, m_q_b, m_kv_a_norm, m_kv_b, m_q_norm, m_k_norm, m_odd_w_out, m_ffn_w_gate, m_ffn_w_up, m_ffn_w_down, v_mix_norm, v_ffn_norm, v_even_w_in, v_sg_ln_g, v_sg_w_s, v_sg_b_s, v_sc_conv_w, v_even_w_out, v_odd_w_in, v_pool_w, v_pool_scale, v_q_a_norm, v_q_b, v_kv_a_norm, v_kv_b, v_q_norm, v_k_norm, v_odd_w_out, v_ffn_w_gate, v_ffn_w_up, v_ffn_w_down):
    xi, yi, ci = _place()
    me = 4 * xi + 2 * yi + ci

    chunks = _pack_shards(even_w_in, even_w_out, odd_w_in, q_b, kv_b, odd_w_out, ffn_w_gate, ffn_w_up, ffn_w_down)

    def lane_pad(a):
        return jnp.pad(a, ((0, 0), (0, 128 - a.shape[1])))

    tile = jnp.concatenate([lane_pad(sc_conv_w[0]), lane_pad(pool_scale), lane_pad(q_a_norm), lane_pad(kv_a_norm),
                            jnp.zeros((2, 128), F32)], axis=0)
    c_arr = jnp.reshape(ci, (1,)).astype(jnp.int32)
    grad_x, (pbs, gbs), gp_mix0, tot = _step(
        x, positions, loss_target, chunks, tile, c_arr, mix_norm, ffn_norm, sg_ln_g, sg_w_s, sg_b_s,
        pool_w, q_norm, k_norm)

    chip = 2 * xi + yi
    (ga_mix0,) = _comm_alone(_pair_exchange_comm(gp_mix0), "rs_pair_exchange_mix0")
    pb_mix0 = _rs_pair_sum(gp_mix0, ga_mix0, c_arr, "rs_pair_sum_mix0")
    (gsh_ffn0a, gsh_ffn0b, gsh_mix1, gsh_ffn1), (gb_mix0,) = _rs_final_sums(
        pbs, gbs, chip, "rs_final_sums", _chip_exchange_comm(pb_mix0))
    (gsh_mix0,) = _rs_final_sums([pb_mix0], [gb_mix0], chip, "rs_final_sum_mix0")

    (g_mix, g_ffn, g_lng, g_ws, g_bs, g_cw_full, g_pw, g_ps_full, g_qa_full, g_kva_full, g_qn, g_kn, loss) = tot
    g_cw = lax.dynamic_slice_in_dim(g_cw_full, me * 64, 64, axis=1)[None]
    g_ps = lax.dynamic_slice_in_dim(g_ps_full, me * 32, 32, axis=1)
    g_qa = lax.dynamic_slice_in_dim(g_qa_full, me * 48, 48, axis=1)
    g_kva = lax.dynamic_slice_in_dim(g_kva_full, me * 32, 32, axis=1)

    def tr(a):
        return jnp.swapaxes(a, -1, -2)

    g_gate = tr(jnp.stack([gsh_ffn0a[OFF_GATE:OFF_GATE + N_FF], gsh_ffn1[OFF_GATE:OFF_GATE + N_FF]]))
    g_up = tr(jnp.stack([gsh_ffn0a[OFF_UP:OFF_UP + N_FF], gsh_ffn1[OFF_UP:OFF_UP + N_FF]]))
    g_down = jnp.stack([gsh_ffn0b, gsh_ffn1[R_GU:R_GU + N_FF]])
    g_oin = gsh_mix1[OFF_OIN:OFF_OIN + N_SQ, :ODD_IN][None]
    g_oout = gsh_mix1[OFF_OOUT:OFF_OOUT + N_SQ][None]
    g_qb = tr(gsh_mix1[OFF_QB:OFF_QB + N_QB_USED].reshape(1, 144, Q_LORA))
    g_kvb = tr(gsh_mix1[OFF_KVB:OFF_KVB + N_KVB].reshape(1, 192, KV_LORA))
    transposed = ("even_w_in", "odd_w_in", "q_b", "kv_b", "ffn_w_gate", "ffn_w_up")

    names = ("mix_norm", "ffn_norm", "even_w_in", "sg_ln_g", "sg_w_s", "sg_b_s", "sc_conv_w", "even_w_out",
             "odd_w_in", "pool_w", "pool_scale", "q_a_norm", "q_b", "kv_a_norm", "kv_b", "q_norm", "k_norm",
             "odd_w_out", "ffn_w_gate", "ffn_w_up", "ffn_w_down")
    grads = dict(mix_norm=g_mix, ffn_norm=g_ffn, sg_ln_g=g_lng, sg_w_s=g_ws, sg_b_s=g_bs,
                 sc_conv_w=g_cw, odd_w_in=g_oin, pool_w=g_pw, pool_scale=g_ps, q_a_norm=g_qa,
                 q_b=g_qb, kv_a_norm=g_kva, kv_b=g_kvb, q_norm=g_qn, k_norm=g_kn, odd_w_out=g_oout,
                 ffn_w_gate=g_gate, ffn_w_up=g_up, ffn_w_down=g_down)
    weights = dict(mix_norm=mix_norm, ffn_norm=ffn_norm, even_w_in=even_w_in, sg_ln_g=sg_ln_g, sg_w_s=sg_w_s,
                   sg_b_s=sg_b_s, sc_conv_w=sc_conv_w, even_w_out=even_w_out, odd_w_in=odd_w_in, pool_w=pool_w,
                   pool_scale=pool_scale, q_a_norm=q_a_norm, q_b=q_b, kv_a_norm=kv_a_norm, kv_b=kv_b, q_norm=q_norm,
                   k_norm=k_norm, odd_w_out=odd_w_out, ffn_w_gate=ffn_w_gate, ffn_w_up=ffn_w_up,
                   ffn_w_down=ffn_w_down)
    m_in = dict(mix_norm=m_mix_norm, ffn_norm=m_ffn_norm, even_w_in=m_even_w_in, sg_ln_g=m_sg_ln_g, sg_w_s=m_sg_w_s,
                sg_b_s=m_sg_b_s, sc_conv_w=m_sc_conv_w, even_w_out=m_even_w_out, odd_w_in=m_odd_w_in,
                pool_w=m_pool_w, pool_scale=m_pool_scale, q_a_norm=m_q_a_norm, q_b=m_q_b, kv_a_norm=m_kv_a_norm,
                kv_b=m_kv_b, q_norm=m_q_norm, k_norm=m_k_norm, odd_w_out=m_odd_w_out, ffn_w_gate=m_ffn_w_gate,
                ffn_w_up=m_ffn_w_up, ffn_w_down=m_ffn_w_down)
    v_in = dict(mix_norm=v_mix_norm, ffn_norm=v_ffn_norm, even_w_in=v_even_w_in, sg_ln_g=v_sg_ln_g, sg_w_s=v_sg_w_s,
                sg_b_s=v_sg_b_s, sc_conv_w=v_sc_conv_w, even_w_out=v_even_w_out, odd_w_in=v_odd_w_in,
                pool_w=v_pool_w, pool_scale=v_pool_scale, q_a_norm=v_q_a_norm, q_b=v_q_b, kv_a_norm=v_kv_a_norm,
                kv_b=v_kv_b, q_norm=v_q_norm, k_norm=v_k_norm, odd_w_out=v_odd_w_out, ffn_w_gate=v_ffn_w_gate,
                ffn_w_up=v_ffn_w_up, ffn_w_down=v_ffn_w_down)
    delta, new_m, new_v = {}, {}, {}

    def as2d(k, a):
        a = tr(a) if k in transposed else a
        return a.reshape(-1, a.shape[-1])

    def back(k, a):
        shape = weights[k].shape
        return tr(a.reshape(shape[:-2] + (shape[-1], shape[-2]))) if k in transposed else a.reshape(shape)

    def update(group, name, nblk=1):
        outs = _adamw([as2d(k, weights[k]) for k in group], [as2d(k, grads[k]) for k in group],
                      [as2d(k, m_in[k]) for k in group], [as2d(k, v_in[k]) for k in group], name, nblk)
        for i, k in enumerate(group):
            delta[k], new_m[k], new_v[k] = (back(k, o[i]) for o in outs)

    grads["even_w_in"] = tr(gsh_mix0[OFF_EIN:OFF_EIN + N_EIN][None])
    grads["even_w_out"] = gsh_mix0[OFF_EOUT:OFF_EOUT + N_SQ][None]
    update(["ffn_w_gate", "ffn_w_up", "ffn_w_down"], "adamw_ffn", 4)
    update(["even_w_in", "even_w_out", "odd_w_in", "odd_w_out"], "adamw_mix", 2)
    update([k for k in names if k not in delta], "adamw_small")

    return (loss.reshape(()), grad_x, *[grads[k] for k in names], *[delta[k] for k in names],
            *[new_m[k] for k in names], *[new_v[k] for k in names])
```

```python
import functools

import numpy as np
import jax
import jax.numpy as jnp
from jax import lax
from jax.experimental import pallas as pl
from jax.experimental.pallas import tpu as pltpu

F32 = jnp.float32
BF16 = jnp.bfloat16
MESH = pl.DeviceIdType.MESH

D = 1024
EPS = 1e-6
NEG_INF = -1e30
SG_HEADS, SG_HD, SG_W, SG_CHUNK = 4, 128, 512, 128
SC_W = 512
EVEN_IN = 2560
POOL_W = 256
POOL_GD = 64
Q_LORA, KV_LORA, QK_ROPE, QK_NOPE, V_DIM = 384, 256, 64, 128, 128
QK_DIM = QK_NOPE + QK_ROPE
HEADS = 6
HP = 256
ODD_IN = 960
D_FF = 2816
ROPE_THETA = 10000.0
ATT_SCALE = QK_DIM ** -0.5
LR, B1, B2, ADAM_EPS, WD, STEP = 0.001, 0.9, 0.999, 1e-08, 0.01, 10

N_DEV = 8
TB = 512
TB_FFN_BWD = 256
TK_DW = 1024
HALO = 16
VMEM_LIMIT = 56 * 1024 * 1024

N_EIN, N_FF, N_SQ = 320, 352, 128
OFF_EIN, OFF_EOUT, R_MIX0 = 0, 384, 512
OFF_GATE, OFF_UP, R_GU = 0, 352, 704
OFF_OIN, OFF_OOUT, OFF_QB, OFF_KVB, R_MIX1 = 0, 128, 256, 320, 384
N_QB, N_QB_USED, N_KVB = 64, 54, 48

INV_SQRT2 = 0.7071067811865476
INV_SQRT_2PI = 0.3989422804014327


def _dot(a, b, ca, cb):
    return lax.dot_general(a, b, (((ca,), (cb,)), ((), ())), preferred_element_type=F32)


def _cparams(n_axes=1):
    return pltpu.CompilerParams(dimension_semantics=("arbitrary",) * n_axes, vmem_limit_bytes=VMEM_LIMIT)


def _wspec(n, off, arity=1):
    assert off % n == 0
    idx = off // n
    if arity == 1:
        return pl.BlockSpec((N_DEV, n, D), lambda i: (0, idx, 0), pipeline_mode=pl.Buffered(1))
    return pl.BlockSpec((N_DEV, n, D), lambda i, j: (0, idx, 0), pipeline_mode=pl.Buffered(1))


def _const_spec(shape):
    zeros = (0,) * len(shape)
    return pl.BlockSpec(shape, lambda *_: zeros)


class _Comm:
    def __init__(self, ins, out_shapes, sems, start, wait, mid=None):
        self.ins, self.out_shapes, self.sems, self.start, self.wait, self.mid = ins, out_shapes, sems, start, wait, mid


def _both(c1, c2):
    def split(f1, f2):
        def run(ins, outs, sems):
            f1(ins[:len(c1.ins)], outs[:len(c1.out_shapes)], sems[:len(c1.sems)])
            f2(ins[len(c1.ins):], outs[len(c1.out_shapes):], sems[len(c1.sems):])
        return run

    assert c1.mid is None and c2.mid is None
    return _Comm(c1.ins + c2.ins, c1.out_shapes + c2.out_shapes, c1.sems + c2.sems,
                 split(c1.start, c2.start), split(c1.wait, c2.wait))


def _call(body, name, grid, in_specs, out_specs, out_shape, args, scratch_shapes=(), comm=None, aliases=None):
    n_axes = len(grid)
    aliases = aliases or {}
    if comm is None:
        res = pl.pallas_call(
            body, name=name, grid=grid, in_specs=list(in_specs), out_specs=list(out_specs),
            out_shape=list(out_shape), scratch_shapes=list(scratch_shapes), input_output_aliases=aliases,
            compiler_params=_cparams(n_axes))(*args)
        return list(res), []
    ni, no, ns = len(in_specs), len(out_specs), len(scratch_shapes)
    ci, co = len(comm.ins), len(comm.out_shapes)
    n_steps = int(np.prod(grid))

    def carrier(*refs):
        ins, cin = refs[:ni], refs[ni:ni + ci]
        outs, cout = refs[ni + ci:ni + ci + no], refs[ni + ci + no:ni + ci + no + co]
        scr, sems = refs[ni + ci + no + co:ni + ci + no + co + ns], refs[ni + ci + no + co + ns:]
        step = 0
        for a in range(n_axes):
            step = step * grid[a] + pl.program_id(a)

        @pl.when(step == 0)
        def _():
            comm.start(cin, cout, sems)

        body(*ins, *outs, *scr)

        if comm.mid is not None and n_steps >= 4:
            @pl.when(step == (3 * n_steps) // 4)
            def _():
                comm.mid(cin, cout, sems)

        @pl.when(step == n_steps - 1)
        def _():
            if comm.mid is not None and n_steps < 4:
                comm.mid(cin, cout, sems)
            comm.wait(cin, cout, sems)

    any_spec = pl.BlockSpec(memory_space=pl.ANY)
    res = pl.pallas_call(
        carrier, name=name, grid=grid, in_specs=list(in_specs) + [any_spec] * ci,
        out_specs=list(out_specs) + [any_spec] * co, out_shape=list(out_shape) + list(comm.out_shapes),
        scratch_shapes=list(scratch_shapes) + list(comm.sems), input_output_aliases=aliases,
        compiler_params=_cparams(n_axes))(*args, *comm.ins)
    return list(res[:no]), list(res[no:])


def _comm_alone(comm, name):
    ci, co = len(comm.ins), len(comm.out_shapes)

    def body(*refs):
        cin, cout, sems = refs[:ci], refs[ci:ci + co], refs[ci + co:]
        comm.start(cin, cout, sems)
        if comm.mid is not None:
            comm.mid(cin, cout, sems)
        comm.wait(cin, cout, sems)

    any_spec = pl.BlockSpec(memory_space=pl.ANY)
    res = pl.pallas_call(
        body, name=name, out_shape=list(comm.out_shapes), in_specs=[any_spec] * ci, out_specs=[any_spec] * co,
        scratch_shapes=list(comm.sems))(*comm.ins)
    return list(res)


def _rms(x, g):
    r = lax.rsqrt(jnp.mean(x * x, axis=-1, keepdims=True) + EPS)
    return x * r * g, r


def _rms_bwd(x, r, g, dy):
    xh = x * r
    dxh = dy * g
    dx = r * (dxh - xh * jnp.mean(dxh * xh, axis=-1, keepdims=True))
    dg = jnp.sum(dy * xh, axis=0, keepdims=True)
    return dx, dg


def _gelu(x):
    return 0.5 * x * (1.0 + lax.erf(x * INV_SQRT2))


def _gelu_grad(x):
    return 0.5 * (1.0 + lax.erf(x * INV_SQRT2)) + x * jnp.exp(-0.5 * x * x) * INV_SQRT_2PI


def _shift_down(a, k):
    rows = lax.broadcasted_iota(jnp.int32, a.shape, 0)
    return jnp.where(rows >= k, pltpu.roll(a, k, 0), 0.0)


def _shift_up(a, k):
    n = a.shape[0]
    rows = lax.broadcasted_iota(jnp.int32, a.shape, 0)
    return jnp.where(rows < n - k, pltpu.roll(a, n - k, 0), 0.0)


def _tril_bf16(w):
    r = lax.broadcasted_iota(jnp.int32, w.shape, 0)
    c = lax.broadcasted_iota(jnp.int32, w.shape, 1)
    return jnp.where(r >= c, w, 0.0).astype(BF16)


def _ln_head(vh, g):
    mu = jnp.mean(vh, axis=-1, keepdims=True)
    xc = vh - mu
    rr = lax.rsqrt(jnp.mean(xc * xc, axis=-1, keepdims=True) + EPS)
    xh = xc * rr
    return xh * g, xh, rr


def _conv_fwd(z, tail, cw_ref):
    ext = jnp.concatenate([tail, z], axis=0)
    zs1 = _shift_down(ext, 1)[HALO:]
    zs2 = _shift_down(ext, 2)[HALO:]
    y = cw_ref[2:3, :] * z + cw_ref[1:2, :] * zs1 + cw_ref[0:1, :] * zs2
    return y, zs1, zs2


def _pool_cnt(shape, blk_in_seq):
    rows = lax.broadcasted_iota(jnp.int32, shape, 0)
    grp = lax.broadcasted_iota(jnp.int32, shape, 1) // POOL_GD
    win = jnp.where(grp == 0, 2, jnp.where(grp == 1, 4, jnp.where(grp == 2, 8, 16)))
    tpos = blk_in_seq * shape[0] + rows + 1
    return jnp.minimum(tpos, win).astype(F32), grp


def _pool_select(grp, s2, s4, s8, s16):
    return jnp.where(grp == 0, s2, jnp.where(grp == 1, s4, jnp.where(grp == 2, s8, s16)))


def _pool_fwd(z, tail, blk_in_seq):
    ext = jnp.concatenate([tail, z], axis=0)
    s2 = ext + _shift_down(ext, 1)
    s4 = s2 + _shift_down(s2, 2)
    s8 = s4 + _shift_down(s4, 4)
    s16 = s8 + _shift_down(s8, 8)
    cnt, grp = _pool_cnt(z.shape, blk_in_seq)
    sums = _pool_select(grp, s2[HALO:], s4[HALO:], s8[HALO:], s16[HALO:])
    return sums / cnt - z, cnt, grp


def _pool_bwd(dpooled, dpm, head, grp):
    n = dpm.shape[0]
    ext = jnp.concatenate([dpm, head], axis=0)
    u2 = ext + _shift_up(ext, 1)
    u4 = u2 + _shift_up(u2, 2)
    u8 = u4 + _shift_up(u4, 4)
    u16 = u8 + _shift_up(u8, 8)
    return _pool_select(grp, u2[:n], u4[:n], u8[:n], u16[:n]) - dpooled


def _lane_sums(a):
    return _dot(a.astype(BF16), jnp.ones((a.shape[1], a.shape[1]), BF16), 1, 0)


def _swap_halves(y1):
    src = lax.broadcasted_iota(jnp.int32, (128, 128), 0)
    dst = lax.broadcasted_iota(jnp.int32, (128, 128), 1)
    perm = jnp.where(((dst < 32) & (src == dst + 32)) | ((dst >= 32) & (dst < QK_ROPE) & (src == dst - 32)), 1.0, 0.0)
    return _dot(y1.astype(BF16), perm.astype(BF16), 1, 0)


def _rope(y1, c, s):
    return y1 * c + _swap_halves(y1) * s


def _rope_bwd(d1, c, s):
    return d1 * c + _swap_halves(d1 * s)


def _qk_prep(x, g, c, s):
    r = lax.rsqrt(_lane_sums(x * x) * (1.0 / QK_DIM) + EPS)
    y = x * r * g
    return jnp.concatenate([y[:, :128], _rope(y[:, 128:], c, s)], axis=1), r


def _qk_prep_bwd(dout, x, r, g, c, s):
    dy = jnp.concatenate([dout[:, :128], _rope_bwd(dout[:, 128:], c, s)], axis=1)
    xh = x * r
    dxh = dy * g
    dx = r * (dxh - xh * (_lane_sums(dxh * xh) * (1.0 / QK_DIM)))
    return dx, jnp.sum(dy * xh, axis=0, keepdims=True)


def _place():
    return lax.axis_index("x"), lax.axis_index("y"), lax.axis_index("c")


def _gather_comm(arrs):
    n = len(arrs)

    def plan(ins, outs, sems):
        send_sems, recv_sems, local_sems = sems
        x, y, c = _place()
        me, sibling = (x, y, c), (x, y, 1 - c)
        chips = [(1 - x, y), (x, 1 - y), (1 - x, 1 - y)]

        def slot(a, px, py, pc):
            return outs[a].at[4 * px + 2 * py + pc]

        def copy(a, k, block, to, src=None):
            return pltpu.make_async_remote_copy(
                src_ref=slot(a, *block) if src is None else src, dst_ref=slot(a, *block),
                send_sem=send_sems.at[a, k], recv_sem=recv_sems.at[a, k], device_id=to, device_id_type=MESH)

        def own():
            mine = [pltpu.make_async_copy(ins[a], slot(a, *me), local_sems.at[a]) for a in range(n)]
            first = []
            for a in range(n):
                first.append(copy(a, 0, me, sibling, src=ins[a]))
                first += [copy(a, 1 + j, me, (*chip, c), src=ins[a]) for j, chip in enumerate(chips)]
            return mine, first

        return c, me, sibling, chips, copy, own

    def start(ins, outs, sems):
        mine, first = plan(ins, outs, sems)[-1]()
        for cp in mine + first:
            cp.start()

    def mid(ins, outs, sems):
        c, me, sibling, chips, copy, _ = plan(ins, outs, sems)
        for j, chip in enumerate(chips):
            for a in range(n):
                copy(a, 1 + j, (*chip, c), me).wait_recv()
                copy(a, 4 + j, (*chip, c), sibling).start()

    def wait(ins, outs, sems):
        c, me, sibling, chips, copy, own = plan(ins, outs, sems)
        mine, first = own()
        passed = [copy(a, 4 + j, (*chip, c), sibling) for j, chip in enumerate(chips) for a in range(n)]
        for a in range(n):
            copy(a, 0, sibling, me).wait_recv()
            for j, chip in enumerate(chips):
                copy(a, 4 + j, (*chip, 1 - c), me).wait_recv()
        for cp in first + passed:
            cp.wait_send()
        for cp in mine:
            cp.wait()

    return _Comm(
        list(arrs), [jax.ShapeDtypeStruct((N_DEV,) + a.shape, a.dtype) for a in arrs],
        [pltpu.SemaphoreType.DMA((n, 7)), pltpu.SemaphoreType.DMA((n, 7)), pltpu.SemaphoreType.DMA((n,))],
        start, wait, mid)


def _sum_gathered(g):
    rows = g.shape[1]

    def body(g_ref, sum_ref):
        total = g_ref[0]
        for d in range(1, N_DEV):
            total = total + g_ref[d]
        sum_ref[...] = total

    return pl.pallas_call(
        body, name="sum_gathered_small", out_shape=jax.ShapeDtypeStruct((rows, 128), F32), grid=(1,),
        in_specs=[pl.BlockSpec((N_DEV, rows, 128), lambda i: (0, 0, 0))],
        out_specs=pl.BlockSpec((rows, 128), lambda i: (0, 0)), compiler_params=_cparams(1),
    )(g)


def _sum_rows(rows):
    return rows if rows <= 512 else rows // 2


def _pair_exchange_comm(gp):
    _, rows, cols = gp.shape

    def copies(ins, outs, sems):
        send_sems, recv_sems = sems
        x, y, c = _place()
        return [pltpu.make_async_remote_copy(
            src_ref=ins[0].at[2 * j + (1 - c)], dst_ref=outs[0].at[j], send_sem=send_sems.at[j],
            recv_sem=recv_sems.at[j], device_id=(x, y, 1 - c), device_id_type=MESH) for j in range(4)]

    def start(ins, outs, sems):
        for cp in copies(ins, outs, sems):
            cp.start()

    def wait(ins, outs, sems):
        for cp in copies(ins, outs, sems):
            cp.wait()

    return _Comm([gp], [jax.ShapeDtypeStruct((4, rows, cols), gp.dtype)],
                 [pltpu.SemaphoreType.DMA((4,)), pltpu.SemaphoreType.DMA((4,))], start, wait)


def _rs_pair_sum(gp, got, where, name):
    _, rows, cols = got.shape
    rb = _sum_rows(rows)
    gp4 = gp.reshape(4, 2, rows, cols)

    def body(w_ref, a_ref, b_ref, o_ref):
        o_ref[0] = (a_ref[0, 0].astype(F32) + b_ref[0].astype(F32)).astype(o_ref.dtype)

    return pl.pallas_call(
        body, name=name, out_shape=jax.ShapeDtypeStruct((4, rows, cols), gp.dtype),
        grid_spec=pltpu.PrefetchScalarGridSpec(
            num_scalar_prefetch=1, grid=(4, rows // rb),
            in_specs=[pl.BlockSpec((1, 1, rb, cols), lambda k, r, w: (w[1 + k], w[0], r, 0)),
                      pl.BlockSpec((1, rb, cols), lambda k, r, w: (w[1 + k], r, 0))],
            out_specs=pl.BlockSpec((1, rb, cols), lambda k, r, w: (k, r, 0))),
        compiler_params=_cparams(2),
    )(where, gp4, got)


def _chip_exchange_comm(pb):
    _, rows, cols = pb.shape

    def copies(ins, outs, sems):
        send_sems, recv_sems = sems
        x, y, c = _place()
        chips = [(1 - x, y), (x, 1 - y), (1 - x, 1 - y)]
        return [pltpu.make_async_remote_copy(
            src_ref=ins[0].at[1 + k], dst_ref=outs[0].at[k], send_sem=send_sems.at[k],
            recv_sem=recv_sems.at[k], device_id=(px, py, c), device_id_type=MESH)
            for k, (px, py) in enumerate(chips)]

    def start(ins, outs, sems):
        for cp in copies(ins, outs, sems):
            cp.start()

    def wait(ins, outs, sems):
        for cp in copies(ins, outs, sems):
            cp.wait()

    return _Comm([pb], [jax.ShapeDtypeStruct((3, rows, cols), pb.dtype)],
                 [pltpu.SemaphoreType.DMA((3,)), pltpu.SemaphoreType.DMA((3,))], start, wait)


def _rs_final_sums(pbs, gots, name, comm=None):
    n = len(pbs)

    def body(*refs):
        for a in range(n):
            m_ref, g_ref, o_ref = refs[a], refs[n + a], refs[2 * n + a]
            o_ref[...] = ((m_ref[0].astype(F32) + g_ref[0].astype(F32)) + g_ref[1].astype(F32)) + g_ref[2].astype(F32)

    half = [pb.shape[1] // 2 for pb in pbs]
    res, extra = _call(
        body, name, (2,),
        [pl.BlockSpec((1, h, D), lambda i: (0, i, 0)) for h in half] + [pl.BlockSpec((3, h, D), lambda i: (0, i, 0)) for h in half],
        [pl.BlockSpec((h, D), lambda i: (i, 0)) for h in half],
        [jax.ShapeDtypeStruct(pb.shape[1:], F32) for pb in pbs], (*pbs, *gots), (), comm)
    return (res, extra) if comm is not None else res


def _rope_tables(pos_col, inv_freq, comm=None):
    t = pos_col.shape[0]

    def body(p_ref, f_ref, c_ref, s_ref):
        ang = p_ref[...].astype(F32) * f_ref[...]
        lane = lax.broadcasted_iota(jnp.int32, ang.shape, 1)
        c_ref[...] = jnp.where(lane < QK_ROPE, jnp.cos(ang), 0.0)
        s = jnp.sin(ang)
        s_ref[...] = jnp.where(lane < 32, -s, jnp.where(lane < QK_ROPE, s, 0.0))

    spec = pl.BlockSpec((TB, 128), lambda i: (i, 0))
    return _call(
        body, "rope_tables", (t // TB,), [pl.BlockSpec((TB, 1), lambda i: (i, 0)), _const_spec((1, 128))],
        [spec] * 2, [jax.ShapeDtypeStruct((t, 128), F32)] * 2, (pos_col, inv_freq), (), comm)


def _sgu_conv_fwd(proj, tail, lng_ref, ws_ref, bst_ref, cw_ref):
    gu = _gelu(proj[:, 0:SG_W])
    gv = _gelu(proj[:, SG_W:2 * SG_W])
    bg = proj[:, 1024:1536]
    z = proj[:, 1536:2048] * proj[:, 2048:2560]
    heads = []
    for h in range(SG_HEADS):
        sl = slice(h * SG_HD, (h + 1) * SG_HD)
        vn, _, _ = _ln_head(gv[:, sl], lng_ref[:, sl])
        vnb = vn.astype(BF16)
        wm = _tril_bf16(ws_ref[h])
        bcol = bst_ref[:, h:h + 1]
        mixed = jnp.concatenate(
            [_dot(wm, vnb[k * SG_CHUNK:(k + 1) * SG_CHUNK], 1, 0) + bcol for k in range(TB // SG_CHUNK)], axis=0)
        heads.append(gu[:, sl] * mixed)
    a_out = jnp.concatenate(heads, axis=1)
    y, _, _ = _conv_fwd(z, tail, cw_ref)
    return a_out, bg * y, z


def _even_fwd(x, wg, gamma, lng, ws, bst, cw, seq, comm=None):
    t = x.shape[0]
    nbs = seq // TB

    def body(x_ref, gam_ref, win_ref, wout_ref, lng_ref, ws_ref, bst_ref, cw_ref, x1_ref, proj_ref, tail_ref):
        i = pl.program_id(0)
        xv = x_ref[...]
        h, _ = _rms(xv, gam_ref[...])
        proj = _dot(h.astype(BF16), win_ref[...].reshape(EVEN_IN, D), 1, 1)
        proj_ref[...] = proj.astype(BF16)
        tail = jnp.where(i % nbs == 0, 0.0, tail_ref[...])
        a_out, b_out, z = _sgu_conv_fwd(proj, tail, lng_ref, ws_ref, bst_ref, cw_ref)
        tail_ref[...] = z[TB - HALO:, :]
        x1_ref[...] = (xv + _dot(a_out.astype(BF16), wout_ref[0:4].reshape(512, D), 1, 0)
                       + _dot(b_out.astype(BF16), wout_ref[4:8].reshape(512, D), 1, 0))

    row = pl.BlockSpec((TB, D), lambda i: (i, 0))
    return _call(
        body, "even_fwd", (t // TB,),
        [row, _const_spec((1, D)), _wspec(N_EIN, OFF_EIN), _wspec(N_SQ, OFF_EOUT), _const_spec((1, SG_W)),
         _const_spec((SG_HEADS, 128, 128)), _const_spec((128, 128)), _const_spec((8, SC_W))],
        [row, pl.BlockSpec((TB, EVEN_IN), lambda i: (i, 0))],
        [jax.ShapeDtypeStruct((t, D), F32), jax.ShapeDtypeStruct((t, EVEN_IN), BF16)],
        (x, gamma, wg, wg, lng, ws, bst, cw), [pltpu.VMEM((HALO, SC_W), F32)], comm)


def _even_bwd(x, proj, dx1, wg, gamma, lng, ws, bst, cw, seq, comm=None):
    t = x.shape[0]
    nb, nbs = t // TB, seq // TB

    def body(x_ref, proj_ref, ptail_ref, dx1_ref, gam_ref, win_ref, wout_ref, lng_ref, ws_ref, bst_ref, cw_ref,
             dx0_ref, dproj_ref, mix_ref, h_ref, dgam_ref, dws_ref, dbc_ref, dlng_ref, dcw_ref, head_ref):
        i = pl.program_id(0)
        blk = nb - 1 - i

        @pl.when(i == 0)
        def _():
            dgam_ref[...] = jnp.zeros_like(dgam_ref)
            dws_ref[...] = jnp.zeros_like(dws_ref)
            dbc_ref[...] = jnp.zeros_like(dbc_ref)
            dlng_ref[...] = jnp.zeros_like(dlng_ref)
            dcw_ref[...] = jnp.zeros_like(dcw_ref)

        xv = x_ref[...]
        gam = gam_ref[...]
        h, r = _rms(xv, gam)
        h_ref[...] = h.astype(BF16)
        dx1 = dx1_ref[...]
        dmix = _dot(dx1.astype(BF16), wout_ref[...].reshape(D, D), 1, 1)
        da, db = dmix[:, :SG_W], dmix[:, SG_W:]
        proj = proj_ref[...].astype(F32)
        u, v = proj[:, 0:SG_W], proj[:, SG_W:2 * SG_W]
        bg, cg, hv = proj[:, 1024:1536], proj[:, 1536:2048], proj[:, 2048:2560]
        gu, gv = _gelu(u), _gelu(v)

        a_heads, dgv_heads = [], []
        for hd in range(SG_HEADS):
            sl = slice(hd * SG_HD, (hd + 1) * SG_HD)
            g_h = lng_ref[:, sl]
            vn, xh, rr = _ln_head(gv[:, sl], g_h)
            vnb = vn.astype(BF16)
            wm = _tril_bf16(ws_ref[hd])
            bcol = bst_ref[:, hd:hd + 1]
            mixed_c, dvn_c = [], []
            dw_acc = jnp.zeros((128, 128), F32)
            db_acc = jnp.zeros((128, 1), F32)
            for k in range(TB // SG_CHUNK):
                rs = slice(k * SG_CHUNK, (k + 1) * SG_CHUNK)
                mixed = _dot(wm, vnb[rs], 1, 0) + bcol
                dmixed = da[rs, sl] * gu[rs, sl]
                dmb = dmixed.astype(BF16)
                dvn_c.append(_dot(wm, dmb, 0, 0))
                dw_acc = dw_acc + _dot(dmb, vnb[rs], 1, 1)
                db_acc = db_acc + jnp.sum(dmixed, axis=1, keepdims=True)
                mixed_c.append(mixed)
            mixed_h = jnp.concatenate(mixed_c, axis=0)
            dvn = jnp.concatenate(dvn_c, axis=0)
            r_i = lax.broadcasted_iota(jnp.int32, (128, 128), 0)
            c_i = lax.broadcasted_iota(jnp.int32, (128, 128), 1)
            dws_ref[hd] += jnp.where(r_i >= c_i, dw_acc, 0.0)
            dbc_ref[:, hd:hd + 1] += db_acc
            dlng_ref[:, sl] += jnp.sum(dvn * xh, axis=0, keepdims=True)
            dxh = dvn * g_h
            dgv = rr * (dxh - jnp.mean(dxh, axis=-1, keepdims=True)
                        - xh * jnp.mean(dxh * xh, axis=-1, keepdims=True))
            a_heads.append(gu[:, sl] * mixed_h)
            dproj_ref[:, sl] = (da[:, sl] * mixed_h * _gelu_grad(u[:, sl])).astype(BF16)
            dgv_heads.append(dgv * _gelu_grad(v[:, sl]))
        dproj_ref[:, SG_W:2 * SG_W] = jnp.concatenate(dgv_heads, axis=1).astype(BF16)
        mix_ref[:, :SG_W] = jnp.concatenate(a_heads, axis=1).astype(BF16)

        z = cg * hv
        pt = ptail_ref[...].astype(F32)
        tail = jnp.where(blk % nbs == 0, 0.0, pt[:, 1536:2048] * pt[:, 2048:2560])
        y, zs1, zs2 = _conv_fwd(z, tail, cw_ref)
        mix_ref[:, SG_W:] = (bg * y).astype(BF16)
        dy = db * bg
        head = jnp.where(blk % nbs == nbs - 1, 0.0, head_ref[...])
        ext = jnp.concatenate([dy, head], axis=0)
        dz = (cw_ref[2:3, :] * dy + cw_ref[1:2, :] * _shift_up(ext, 1)[:TB]
              + cw_ref[0:1, :] * _shift_up(ext, 2)[:TB])
        head_ref[...] = dy[:HALO, :]
        dcw_ref[2:3, :] += jnp.sum(dy * z, axis=0, keepdims=True)
        dcw_ref[1:2, :] += jnp.sum(dy * zs1, axis=0, keepdims=True)
        dcw_ref[0:1, :] += jnp.sum(dy * zs2, axis=0, keepdims=True)
        dproj_ref[:, 1024:1536] = (db * y).astype(BF16)
        dproj_ref[:, 1536:2048] = (dz * hv).astype(BF16)
        dproj_ref[:, 2048:2560] = (dz * cg).astype(BF16)

        dh = _dot(dproj_ref[...], win_ref[...].reshape(EVEN_IN, D), 1, 0)
        dxn, dgam = _rms_bwd(xv, r, gam, dh)
        dgam_ref[...] += dgam
        dx0_ref[...] = dx1 + dxn

    def rev(w):
        return pl.BlockSpec((TB, w), lambda i: (nb - 1 - i, 0))

    ptail = pl.BlockSpec((HALO, EVEN_IN), lambda i: (jnp.maximum((nb - 1 - i) * (TB // HALO) - 1, 0), 0))
    return _call(
        body, "even_bwd", (nb,),
        [rev(D), rev(EVEN_IN), ptail, rev(D), _const_spec((1, D)), _wspec(N_EIN, OFF_EIN),
         _wspec(N_SQ, OFF_EOUT), _const_spec((1, SG_W)), _const_spec((SG_HEADS, 128, 128)),
         _const_spec((128, 128)), _const_spec((8, SC_W))],
        [rev(D), rev(EVEN_IN), rev(D), rev(D), _const_spec((1, D)), _const_spec((SG_HEADS, 128, 128)),
         _const_spec((128, 128)), _const_spec((1, SG_W)), _const_spec((8, SC_W))],
        [jax.ShapeDtypeStruct((t, D), F32), jax.ShapeDtypeStruct((t, EVEN_IN), BF16),
         jax.ShapeDtypeStruct((t, D), BF16), jax.ShapeDtypeStruct((t, D), BF16),
         jax.ShapeDtypeStruct((1, D), F32), jax.ShapeDtypeStruct((SG_HEADS, 128, 128), F32),
         jax.ShapeDtypeStruct((128, 128), F32), jax.ShapeDtypeStruct((1, SG_W), F32),
         jax.ShapeDtypeStruct((8, SC_W), F32)],
        (x, proj, proj, dx1, gamma, wg, wg, lng, ws, bst, cw), [pltpu.VMEM((HALO, SC_W), F32)], comm)


def _ffn_fwd(x, w_gu, w_d, gamma, name, comm=None, target=None):
    t = x.shape[0]
    last = target is not None

    def body(*refs):
        x_ref, gam_ref, wg_ref, wu_ref, wd_ref = refs[:5]
        y_ref, g_ref, u_ref = refs[5 + last:8 + last]
        xv = x_ref[...]
        h, _ = _rms(xv, gam_ref[...])
        hb = h.astype(BF16)
        g = _dot(hb, wg_ref[...].reshape(D_FF, D), 1, 1)
        u = _dot(hb, wu_ref[...].reshape(D_FF, D), 1, 1)
        g_ref[...] = g.astype(BF16)
        u_ref[...] = u.astype(BF16)
        act = g * jax.nn.sigmoid(g) * u
        y = xv + _dot(act.astype(BF16), wd_ref[...].reshape(D_FF, D), 1, 0)
        if not last:
            y_ref[...] = y
            return
        loss_ref = refs[9]

        @pl.when(pl.program_id(0) == 0)
        def _():
            loss_ref[...] = jnp.zeros_like(loss_ref)

        err = y - refs[5][...]
        y_ref[...] = err * (1.0 / D)
        sq = jnp.sum(jnp.sum(err * err, axis=-1, keepdims=True), axis=0, keepdims=True)
        loss_ref[...] += (0.5 / D) * sq

    row = pl.BlockSpec((TB, D), lambda i: (i, 0))
    wide = pl.BlockSpec((TB, D_FF), lambda i: (i, 0))
    in_specs = [row, _const_spec((1, D)), _wspec(N_FF, OFF_GATE), _wspec(N_FF, OFF_UP), _wspec(N_FF, 0)]
    out_specs = [row, wide, wide]
    out_shape = [jax.ShapeDtypeStruct((t, D), F32), jax.ShapeDtypeStruct((t, D_FF), BF16),
                 jax.ShapeDtypeStruct((t, D_FF), BF16)]
    args = (x, gamma, w_gu, w_gu, w_d)
    if last:
        in_specs, args = in_specs + [row], args + (target,)
        out_specs, out_shape = out_specs + [_const_spec((8, 128))], out_shape + [jax.ShapeDtypeStruct((8, 128), F32)]
    return _call(body, name, (t // TB,), in_specs, out_specs, out_shape, args, (), comm)


def _ffn_up(x, w_gu, gamma, name, comm=None):
    t = x.shape[0]

    def body(x_ref, gam_ref, wg_ref, wu_ref, g_ref, u_ref, act_ref):
        h, _ = _rms(x_ref[...], gam_ref[...])
        hb = h.astype(BF16)
        g = _dot(hb, wg_ref[...].reshape(D_FF, D), 1, 1)
        u = _dot(hb, wu_ref[...].reshape(D_FF, D), 1, 1)
        g_ref[...] = g.astype(BF16)
        u_ref[...] = u.astype(BF16)
        act_ref[...] = (g * jax.nn.sigmoid(g) * u).astype(BF16)

    row = pl.BlockSpec((TB, D), lambda i: (i, 0))
    wide = pl.BlockSpec((TB, D_FF), lambda i: (i, 0))
    return _call(body, name, (t // TB,), [row, _const_spec((1, D)), _wspec(N_FF, OFF_GATE), _wspec(N_FF, OFF_UP)],
                 [wide, wide, wide], [jax.ShapeDtypeStruct((t, D_FF), BF16)] * 3, (x, gamma, w_gu, w_gu), (), comm)


def _ffn_down(x, act, w_d, name, comm=None):
    t = x.shape[0]

    def body(x_ref, a_ref, wd_ref, y_ref):
        y_ref[...] = x_ref[...] + _dot(a_ref[...], wd_ref[...].reshape(D_FF, D), 1, 0)

    row = pl.BlockSpec((TB, D), lambda i: (i, 0))
    wide = pl.BlockSpec((TB, D_FF), lambda i: (i, 0))
    return _call(body, name, (t // TB,), [row, wide, _wspec(N_FF, 0)], [row], [jax.ShapeDtypeStruct((t, D), F32)],
                 (x, act, w_d), (), comm)


def _ffn_bwd(x, g, u, dy, w_gu, w_d, gamma, name, comm=None):
    t = x.shape[0]

    def body(x_ref, g_ref, u_ref, dy_ref, gam_ref, wg_ref, wu_ref, wd_ref,
             dx_ref, act_ref, dg_ref, du_ref, h_ref, dgam_ref):
        @pl.when(pl.program_id(0) == 0)
        def _():
            dgam_ref[...] = jnp.zeros_like(dgam_ref)

        xv = x_ref[...]
        gam = gam_ref[...]
        h, r = _rms(xv, gam)
        h_ref[...] = h.astype(BF16)
        dyv = dy_ref[...]
        dact = _dot(dyv.astype(BF16), wd_ref[...].reshape(D_FF, D), 1, 1)
        gv = g_ref[...].astype(F32)
        uv = u_ref[...].astype(F32)
        sg = jax.nn.sigmoid(gv)
        silu = gv * sg
        act_ref[...] = (silu * uv).astype(BF16)
        dgb = (dact * uv * (sg * (1.0 + gv * (1.0 - sg)))).astype(BF16)
        dub = (dact * silu).astype(BF16)
        dg_ref[...] = dgb
        du_ref[...] = dub
        dh = _dot(dgb, wg_ref[...].reshape(D_FF, D), 1, 0) + _dot(dub, wu_ref[...].reshape(D_FF, D), 1, 0)
        dxn, dgam = _rms_bwd(xv, r, gam, dh)
        dgam_ref[...] += dgam
        dx_ref[...] = dyv + dxn

    row = pl.BlockSpec((TB_FFN_BWD, D), lambda i: (i, 0))
    wide = pl.BlockSpec((TB_FFN_BWD, D_FF), lambda i: (i, 0))
    return _call(
        body, name, (t // TB_FFN_BWD,),
        [row, wide, wide, row, _const_spec((1, D)), _wspec(N_FF, OFF_GATE), _wspec(N_FF, OFF_UP),
         _wspec(N_FF, 0)],
        [row, wide, wide, wide, row, _const_spec((1, D))],
        [jax.ShapeDtypeStruct((t, D), F32), jax.ShapeDtypeStruct((t, D_FF), BF16),
         jax.ShapeDtypeStruct((t, D_FF), BF16), jax.ShapeDtypeStruct((t, D_FF), BF16),
         jax.ShapeDtypeStruct((t, D), BF16), jax.ShapeDtypeStruct((1, D), F32)],
        (x, g, u, dy, gamma, w_gu, w_gu, w_d), (), comm)


def _odd_pre_fwd(x, wg, gamma, qbt, kvbt, qa_g, kva_g, pw_bd, pscale, seq, comm=None):
    t = x.shape[0]
    nbs = seq // TB

    def body(x_ref, gam_ref, win_ref, qb_ref, kvb_ref, qa_ref, kva_ref, pw_ref, ps_ref,
             proj_ref, q_ref, kv_ref, kr_ref, c_ref, tail_ref):
        i = pl.program_id(0)
        h, _ = _rms(x_ref[...], gam_ref[...])
        proj = _dot(h.astype(BF16), win_ref[...].reshape(D, D), 1, 0)
        proj_ref[...] = proj.astype(BF16)
        zp, ql, kvl = proj[:, :POOL_W], proj[:, 256:640], proj[:, 640:896]
        kr_ref[...] = proj[:, 896:1024]
        qn, _ = _rms(ql, qa_ref[...])
        q_ref[...] = _dot(qn.astype(BF16), qb_ref[...], 1, 1).astype(BF16)
        kvn, _ = _rms(kvl, kva_ref[...])
        kv_ref[...] = _dot(kvn.astype(BF16), kvb_ref[...], 1, 1).astype(BF16)
        tail = jnp.where(i % nbs == 0, 0.0, tail_ref[...])
        pooled, _, _ = _pool_fwd(zp, tail, i % nbs)
        tail_ref[...] = zp[TB - HALO:, :]
        c_ref[...] = (_dot(pooled.astype(BF16), pw_ref[...], 1, 0) * ps_ref[...]).astype(BF16)

    def row(w):
        return pl.BlockSpec((TB, w), lambda i: (i, 0))

    return _call(
        body, "odd_pre_fwd", (t // TB,),
        [row(D), _const_spec((1, D)), _wspec(N_SQ, OFF_OIN), _const_spec((HEADS * HP, Q_LORA)),
         _const_spec((HEADS * HP, KV_LORA)), _const_spec((1, Q_LORA)), _const_spec((1, KV_LORA)),
         _const_spec((POOL_W, POOL_W)), _const_spec((1, POOL_W))],
        [row(D), row(HEADS * HP), row(HEADS * HP), row(128), row(POOL_W)],
        [jax.ShapeDtypeStruct((t, D), BF16), jax.ShapeDtypeStruct((t, HEADS * HP), BF16),
         jax.ShapeDtypeStruct((t, HEADS * HP), BF16), jax.ShapeDtypeStruct((t, 128), F32),
         jax.ShapeDtypeStruct((t, POOL_W), BF16)],
        (x, gamma, wg, qbt, kvbt, qa_g, kva_g, pw_bd, pscale), [pltpu.VMEM((HALO, POOL_W), F32)], comm)


def _odd_pre_bwd(x, proj, dx3, dmix, dq, dkv, dkr, wg, gamma, qbt, kvbt, qa_g, kva_g, pw_bd, pscale, seq):
    t = x.shape[0]
    nb, nbs = t // TB, seq // TB

    def body(x_ref, proj_ref, ptail_ref, dx3_ref, dco_ref, dq_ref, dkv_ref, dkr_ref, gam_ref, win_ref, qb_ref,
             kvb_ref, qa_ref, kva_ref, pw_ref, ps_ref,
             dx2_ref, dproj_ref, h_ref, qn_ref, kvn_ref, dgam_ref, dqa_ref, dkva_ref, dpw_ref, dps_ref, head_ref):
        i = pl.program_id(0)
        blk = nb - 1 - i

        @pl.when(i == 0)
        def _():
            dgam_ref[...] = jnp.zeros_like(dgam_ref)
            dqa_ref[...] = jnp.zeros_like(dqa_ref)
            dkva_ref[...] = jnp.zeros_like(dkva_ref)
            dpw_ref[...] = jnp.zeros_like(dpw_ref)
            dps_ref[...] = jnp.zeros_like(dps_ref)

        xv = x_ref[...]
        gam = gam_ref[...]
        h, r = _rms(xv, gam)
        h_ref[...] = h.astype(BF16)
        proj = proj_ref[...].astype(F32)
        zp, ql, kvl = proj[:, :POOL_W], proj[:, 256:640], proj[:, 640:896]

        qa = qa_ref[...]
        qn, rq = _rms(ql, qa)
        qn_ref[...] = qn.astype(BF16)
        dql, dqa = _rms_bwd(ql, rq, qa, _dot(dq_ref[...], qb_ref[...], 1, 0))
        dqa_ref[...] += dqa
        kva = kva_ref[...]
        kvn, rkv = _rms(kvl, kva)
        kvn_ref[...] = kvn.astype(BF16)
        dkvl, dkva = _rms_bwd(kvl, rkv, kva, _dot(dkv_ref[...], kvb_ref[...], 1, 0))
        dkva_ref[...] += dkva

        pt = ptail_ref[...].astype(F32)
        tail = jnp.where(blk % nbs == 0, 0.0, pt[:, :POOL_W])
        pooled, cnt, grp = _pool_fwd(zp, tail, blk % nbs)
        pb = pooled.astype(BF16)
        pw = pw_ref[...]
        dco = dco_ref[...].astype(F32)
        dps_ref[...] += jnp.sum(dco * _dot(pb, pw, 1, 0), axis=0, keepdims=True)
        dpo = (dco * ps_ref[...]).astype(BF16)
        dpw_ref[...] += _dot(pb, dpo, 0, 0)
        dpooled = _dot(dpo, pw, 1, 1)
        dpm = dpooled / cnt
        head = jnp.where(blk % nbs == nbs - 1, 0.0, head_ref[...])
        dz = _pool_bwd(dpooled, dpm, head, grp)
        head_ref[...] = dpm[:HALO, :]

        dproj_ref[:, :POOL_W] = dz.astype(BF16)
        dproj_ref[:, 256:640] = dql.astype(BF16)
        dproj_ref[:, 640:896] = dkvl.astype(BF16)
        dproj_ref[:, 896:1024] = dkr_ref[...].astype(BF16)
        dh = _dot(dproj_ref[...], win_ref[...].reshape(D, D), 1, 1)
        dxn, dgam = _rms_bwd(xv, r, gam, dh)
        dgam_ref[...] += dgam
        dx2_ref[...] = dx3_ref[...] + dxn

    def rev(w):
        return pl.BlockSpec((TB, w), lambda i: (nb - 1 - i, 0))

    ptail = pl.BlockSpec((HALO, D), lambda i: (jnp.maximum((nb - 1 - i) * (TB // HALO) - 1, 0), 0))
    return pl.pallas_call(
        body, name="odd_pre_bwd",
        out_shape=[jax.ShapeDtypeStruct((t, D), F32), jax.ShapeDtypeStruct((t, D), BF16),
                   jax.ShapeDtypeStruct((t, D), BF16), jax.ShapeDtypeStruct((t, Q_LORA), BF16),
                   jax.ShapeDtypeStruct((t, KV_LORA), BF16), jax.ShapeDtypeStruct((1, D), F32),
                   jax.ShapeDtypeStruct((1, Q_LORA), F32), jax.ShapeDtypeStruct((1, KV_LORA), F32),
                   jax.ShapeDtypeStruct((POOL_W, POOL_W), F32), jax.ShapeDtypeStruct((1, POOL_W), F32)],
        grid=(nb,),
        in_specs=[rev(D), rev(D), ptail, rev(D), rev(POOL_W), rev(HEADS * HP), rev(HEADS * HP), rev(128),
                  _const_spec((1, D)), _wspec(N_SQ, OFF_OIN), _const_spec((HEADS * HP, Q_LORA)),
                  _const_spec((HEADS * HP, KV_LORA)), _const_spec((1, Q_LORA)), _const_spec((1, KV_LORA)),
                  _const_spec((POOL_W, POOL_W)), _const_spec((1, POOL_W))],
        out_specs=[rev(D), rev(D), rev(D), rev(Q_LORA), rev(KV_LORA), _const_spec((1, D)), _const_spec((1, Q_LORA)),
                   _const_spec((1, KV_LORA)), _const_spec((POOL_W, POOL_W)), _const_spec((1, POOL_W))],
        scratch_shapes=[pltpu.VMEM((HALO, POOL_W), F32)],
        compiler_params=_cparams(1),
    )(x, proj, proj, dx3, dmix, dq, dkv, dkr, gamma, wg, qbt, kvbt, qa_g, kva_g, pw_bd, pscale)


def _attn_specs(seq):
    head = pl.BlockSpec((seq, HP), lambda b, h: (b, h))
    shared = pl.BlockSpec((seq, 128), lambda b, h: (b, 0))
    gain = pl.BlockSpec((1, HP), lambda b, h: (0, 0))
    return head, shared, gain


def _causal_bias(n):
    rows = lax.broadcasted_iota(jnp.int32, (n, n), 0)
    cols = lax.broadcasted_iota(jnp.int32, (n, n), 1)
    return jnp.where(cols <= rows, 0.0, NEG_INF)


def _attn_fwd(q, kv, kr, cos, sin, gq, gk, seq, comm=None):
    t = q.shape[0]
    qb = min(512, seq)

    def body(q_ref, kv_ref, kr_ref, c_ref, s_ref, gq_ref, gk_ref, o_ref, lse_ref):
        c, s = c_ref[...], s_ref[...]
        qf, _ = _qk_prep(q_ref[...].astype(F32), gq_ref[...], c, s)
        kin = jnp.concatenate([kv_ref[:, :128].astype(F32), kr_ref[...]], axis=1)
        kf, _ = _qk_prep(kin, gk_ref[...], c, s)
        qf, kf = qf.astype(BF16), kf.astype(BF16)
        v1 = jnp.concatenate([kv_ref[:, 128:], jnp.ones((seq, V_DIM), BF16)], axis=1)
        bias = _causal_bias(qb)
        for q0 in range(0, seq, qb):
            q1 = q0 + qb
            qblk = qf[q0:q1]
            s_dg = _dot(qblk, kf[q0:q1], 1, 1) + bias
            m = jnp.max(s_dg, axis=-1, keepdims=True)
            if q0:
                s_off = _dot(qblk, kf[:q0], 1, 1)
                m = jnp.maximum(m, jnp.max(s_off, axis=-1, keepdims=True))
            acc = _dot(jnp.exp(s_dg - m).astype(BF16), v1[q0:q1], 1, 0)
            if q0:
                acc = acc + _dot(jnp.exp(s_off - m).astype(BF16), v1[:q0], 1, 0)
            l = acc[:, V_DIM:]
            o_ref[q0:q1, :] = (acc[:, :V_DIM] / l).astype(BF16)
            lse_ref[q0:q1, :] = m + jnp.log(l)

    head, shared, gain = _attn_specs(seq)
    per_head = pl.BlockSpec((seq, V_DIM), lambda b, h: (b, h))
    return _call(
        body, "attn_fwd", (t // seq, HEADS),
        [head, head, shared, shared, shared, gain, gain], [per_head, per_head],
        [jax.ShapeDtypeStruct((t, HEADS * V_DIM), BF16), jax.ShapeDtypeStruct((t, HEADS * V_DIM), F32)],
        (q, kv, kr, cos, sin, gq, gk), (), comm)


def _attn_bwd(q, kv, kr, cos, sin, gq, gk, dmix, d_out, lse, seq, comm=None):
    t = q.shape[0]
    qb = min(512, seq)

    def body(q_ref, kv_ref, kr_ref, c_ref, s_ref, gq_ref, gk_ref, do_ref, o_ref, lse_ref,
             dq_ref, dkv_ref, dkr_ref, dgq_ref, dgk_ref, dqf_ref, dkf_ref, dv_ref):
        b, hd = pl.program_id(0), pl.program_id(1)

        @pl.when((b == 0) & (hd == 0))
        def _():
            dgq_ref[...] = jnp.zeros_like(dgq_ref)
            dgk_ref[...] = jnp.zeros_like(dgk_ref)

        c, sn = c_ref[...], s_ref[...]
        gq_v, gk_v = gq_ref[...], gk_ref[...]
        qin = q_ref[...].astype(F32)
        kin = jnp.concatenate([kv_ref[:, :128].astype(F32), kr_ref[...]], axis=1)
        qf32, rq = _qk_prep(qin, gq_v, c, sn)
        kf32, rk = _qk_prep(kin, gk_v, c, sn)
        qf, kf = qf32.astype(BF16), kf32.astype(BF16)
        vb = kv_ref[:, 128:]
        dkf_ref[...] = jnp.zeros_like(dkf_ref)
        dv_ref[...] = jnp.zeros_like(dv_ref)
        bias = _causal_bias(qb)
        for q0 in range(0, seq, qb):
            q1 = q0 + qb
            qblk = qf[q0:q1]
            do = do_ref[q0:q1, :]
            lse_col = lse_ref[q0:q1, 0:1]
            d_col = jnp.sum(do.astype(F32) * o_ref[q0:q1, :].astype(F32), axis=-1, keepdims=True)
            dq_acc = None
            for k0, k1, diag in ((q0, q1, True), (0, q0, False)):
                if k1 == k0:
                    continue
                s = _dot(qblk, kf[k0:k1], 1, 1)
                p = jnp.exp((s + bias if diag else s) - lse_col)
                dv_ref[k0:k1, :] += _dot(p.astype(BF16), do, 0, 0)
                ds = (p * (_dot(do, vb[k0:k1], 1, 1) - d_col)).astype(BF16)
                part = _dot(ds, kf[k0:k1], 1, 0)
                dq_acc = part if dq_acc is None else dq_acc + part
                dkf_ref[k0:k1, :] += _dot(ds, qblk, 0, 0)
            dqf_ref[q0:q1, :] = dq_acc
        dqin, dgq = _qk_prep_bwd(dqf_ref[...], qin, rq, gq_v, c, sn)
        dkin, dgk = _qk_prep_bwd(dkf_ref[...], kin, rk, gk_v, c, sn)
        dgq_ref[...] += dgq
        dgk_ref[...] += dgk
        dq_ref[...] = dqin.astype(BF16)
        dkv_ref[:, :128] = dkin[:, :128].astype(BF16)
        dkv_ref[:, 128:] = dv_ref[...].astype(BF16)

        @pl.when(hd == 0)
        def _():
            dkr_ref[...] = dkin[:, 128:]

        @pl.when(hd != 0)
        def _():
            dkr_ref[...] += dkin[:, 128:]

    head, shared, gain = _attn_specs(seq)
    per_head = pl.BlockSpec((seq, V_DIM), lambda b, h: (b, h))
    return _call(
        body, "attn_bwd", (t // seq, HEADS),
        [head, head, shared, shared, shared, gain, gain,
         pl.BlockSpec((seq, V_DIM), lambda b, h: (b, 2 + h)), per_head, per_head],
        [head, head, shared, gain, gain],
        [jax.ShapeDtypeStruct((t, HEADS * HP), BF16), jax.ShapeDtypeStruct((t, HEADS * HP), BF16),
         jax.ShapeDtypeStruct((t, 128), F32), jax.ShapeDtypeStruct((1, HP), F32),
         jax.ShapeDtypeStruct((1, HP), F32)],
        (q, kv, kr, cos, sin, gq, gk, dmix, d_out, lse),
        [pltpu.VMEM((seq, HP), F32), pltpu.VMEM((seq, HP), F32), pltpu.VMEM((seq, V_DIM), F32)], comm)


def _odd_post_fwd(x, c_out, d_out, wg):
    t = x.shape[0]

    def body(x_ref, c_ref, d_ref, w_ref, y_ref):
        y_ref[...] = (x_ref[...] + _dot(c_ref[...], w_ref[0:2].reshape(POOL_W, D), 1, 0)
                      + _dot(d_ref[...], w_ref[2:8].reshape(HEADS * V_DIM, D), 1, 0))

    def row(w):
        return pl.BlockSpec((TB, w), lambda i: (i, 0))

    return pl.pallas_call(
        body, name="odd_post_fwd", out_shape=jax.ShapeDtypeStruct((t, D), F32), grid=(t // TB,),
        in_specs=[row(D), row(POOL_W), row(HEADS * V_DIM), _wspec(N_SQ, OFF_OOUT)], out_specs=row(D),
        compiler_params=_cparams(1),
    )(x, c_out, d_out, wg)


def _odd_post_bwd(dx3, wg, comm=None):
    t = dx3.shape[0]

    def body(d_ref, w_ref, o_ref):
        o_ref[...] = _dot(d_ref[...].astype(BF16), w_ref[...].reshape(D, D), 1, 1).astype(BF16)

    row = pl.BlockSpec((TB, D), lambda i: (i, 0))
    (res,), extra = _call(body, "odd_post_bwd", (t // TB,), [row, _wspec(N_SQ, OFF_OOUT)], [row],
                          [jax.ShapeDtypeStruct((t, D), BF16)], (dx3, wg), (), comm)
    return res, extra


def _tn(a_list, b, tm, name, into=None, comm=None):
    t, n_out = b.shape
    widths = [a.shape[1] for a in a_list]
    tk = min(TK_DW, t)
    m, na, nk = sum(widths), len(a_list), t // tk
    assert na == 1 or tm == m

    def body(*refs):
        a_refs, b_ref, o_ref, acc_ref = refs[:na], refs[na], refs[-2], refs[-1]
        k = pl.program_id(1)

        @pl.when(k == 0)
        def _():
            acc_ref[...] = jnp.zeros_like(acc_ref)

        bb = b_ref[...].astype(BF16)
        m0 = 0
        for a_ref, w in zip(a_refs, widths):
            rows = slice(0, tm) if na == 1 else slice(m0, m0 + w)
            acc_ref[rows, :] += _dot(a_ref[...].astype(BF16), bb, 0, 0)
            m0 += w

        @pl.when(k == nk - 1)
        def _():
            o_ref[...] = acc_ref[...].astype(BF16).reshape(o_ref.shape)

    if na == 1:
        in_specs = [pl.BlockSpec((tk, tm), lambda i, k: (k, i))]
    else:
        in_specs = [pl.BlockSpec((tk, w), lambda i, k: (k, 0)) for w in widths]
    in_specs.append(pl.BlockSpec((tk, n_out), lambda i, k: (k, 0)))
    args = list(a_list) + [b]
    if into is None:
        out_spec = pl.BlockSpec((tm, n_out), lambda i, k: (i, 0))
        out_shape = jax.ShapeDtypeStruct((m, n_out), BF16)
        aliases = {}
    else:
        buf, n, off = into
        assert n_out == D and tm % n == 0 and off % n == 0 and (na == 1 or tm // n == N_DEV)
        idx = off // n
        out_spec = pl.BlockSpec((tm // n, n, D), lambda i, k: (i, idx, 0))
        out_shape = jax.ShapeDtypeStruct(buf.shape, BF16)
        in_specs.append(pl.BlockSpec(memory_space=pl.ANY))
        args.append(buf)
        aliases = {len(args) - 1: 0}
    (res,), extra = _call(body, name, (m // tm, nk), in_specs, [out_spec], [out_shape], args,
                          [pltpu.VMEM((tm, n_out), F32)], comm, aliases)
    return (res, extra) if comm is not None else res


def _adamw(ws, gs, ms, vs, name, nblk=1):
    n = len(ws)
    c1 = 1.0 - B1 ** STEP
    c2 = 1.0 - B2 ** STEP

    def body(*refs):
        for a in range(n):
            w, g, m, v = (refs[k * n + a][...] for k in range(4))
            d_ref, m_ref, v_ref = (refs[(4 + k) * n + a] for k in range(3))
            m_new = B1 * m + (1.0 - B1) * g
            v_new = B2 * v + (1.0 - B2) * (g * g)
            d_ref[...] = -LR * ((m_new / c1) / (jnp.sqrt(v_new / c2) + ADAM_EPS) + WD * w)
            m_ref[...] = m_new
            v_ref[...] = v_new

    grid = (nblk,)
    assert all(w.shape[0] % nblk == 0 and (nblk == 1 or (w.shape[0] // nblk) % 8 == 0) for w in ws)
    specs = [pl.BlockSpec((w.shape[0] // nblk, w.shape[1]), lambda i: (i, 0)) for w in ws]
    outs, _ = _call(body, name, grid, specs * 4, specs * 3, [jax.ShapeDtypeStruct(w.shape, F32) for w in ws] * 3,
                    (*ws, *gs, *ms, *vs))
    return outs[:n], outs[n:2 * n], outs[2 * n:]


def _rows1024(a, rows):
    flat = a.reshape(-1, D)
    return jnp.pad(flat, ((0, rows - flat.shape[0]), (0, 0)))


def _pack_shards(even_w_in, even_w_out, odd_w_in, q_b, kv_b, odd_w_out, ffn_w_gate, ffn_w_up, ffn_w_down):
    mix0 = jnp.concatenate([even_w_in[0].T, jnp.zeros((OFF_EOUT - N_EIN, D), F32), even_w_out[0]], axis=0)
    gu = [jnp.concatenate([ffn_w_gate[layer].T, ffn_w_up[layer].T], axis=0) for layer in range(2)]
    mix1 = jnp.concatenate([jnp.pad(odd_w_in[0], ((0, 0), (0, D - ODD_IN))), odd_w_out[0],
                            _rows1024(q_b[0].T, N_QB), _rows1024(kv_b[0].T, N_KVB),
                            jnp.zeros((R_MIX1 - OFF_KVB - N_KVB, D), F32)], axis=0)
    return [c.astype(BF16) for c in (mix0, gu[0], ffn_w_down[0], mix1, gu[1], ffn_w_down[1])]


def _pad_heads(a):
    k = a.shape[1]
    return jnp.pad(a.reshape(HEADS, QK_DIM, k), ((0, 0), (0, HP - QK_DIM), (0, 0))).reshape(HEADS * HP, k)


def _small_pack(parts):
    flat = []
    for p in parts:
        v = p.reshape(-1)
        flat.append(jnp.pad(v, (0, (-v.shape[0]) % 1024)))
    return jnp.concatenate(flat).reshape(-1, 128)


def _small_unpack(buf, shapes):
    flat = buf.reshape(-1)
    out, off = [], 0
    for s in shapes:
        size = int(np.prod(s))
        out.append(flat[off:off + size].reshape(s))
        off += size + (-size) % 1024
    return out


def _step(x3d, positions, target3d, chunks, tile, where, mix_norm, ffn_norm, sg_ln_g, sg_w_s, sg_b_s,
          pool_w, q_norm, k_norm):
    bsz, seq, _ = x3d.shape
    t = bsz * seq
    x0 = x3d.reshape(t, D)
    target = target3d.reshape(t, D)
    my_mix0, my_gu0, my_d0, my_mix1, my_gu1, my_d1 = chunks

    lane = np.arange(128)
    inv_freq = np.where(lane < QK_ROPE, ROPE_THETA ** (-(2.0 * (lane % 32)) / QK_ROPE), 0.0)
    inv_freq = jnp.asarray(inv_freq.reshape(1, 128), F32)
    (cos, sin), (w_mix0, tiles) = _rope_tables(positions.reshape(t, 1), inv_freq, _gather_comm([my_mix0, tile]))

    conv_w = tiles[:, 0:3, 0:64].transpose(1, 0, 2).reshape(3, SC_W)
    pool_scale = tiles[:, 3, 0:32].reshape(1, POOL_W)
    q_a_norm = tiles[:, 4, 0:48].reshape(1, Q_LORA)
    kv_a_norm = tiles[:, 5, 0:32].reshape(1, KV_LORA)
    ws = sg_w_s[0]
    bst = jnp.pad(sg_b_s[0].T, ((0, 0), (0, 128 - SG_HEADS)))
    cw = jnp.pad(conv_w, ((0, 8 - 3), (0, 0)))
    pw_bd = jax.scipy.linalg.block_diag(*[pool_w[0, g] for g in range(4)]).astype(BF16)
    gq = jnp.pad(q_norm * ATT_SCALE, ((0, 0), (0, HP - QK_DIM)))
    gk = jnp.pad(k_norm, ((0, 0), (0, HP - QK_DIM)))

    (x1, proj_e), (w_gu0,) = _even_fwd(x0, w_mix0, mix_norm[0:1], sg_ln_g, ws, bst, cw, seq, _gather_comm([my_gu0]))
    (g0, u0, act0), (w_d0, w_mix1) = _ffn_up(x1, w_gu0, ffn_norm[0:1], "ffn_up0", _gather_comm([my_d0, my_mix1]))
    (x2,), (w_d1,) = _ffn_down(x1, act0, w_d0, "ffn_down0", _gather_comm([my_d1]))
    qbt = _pad_heads(w_mix1[:, OFF_QB:OFF_QB + N_QB_USED, :].reshape(HEADS * QK_DIM, Q_LORA))
    kvbt = w_mix1[:, OFF_KVB:OFF_KVB + N_KVB, :].reshape(HEADS * HP, KV_LORA)
    (proj_o, q, kv, kr, c_out), _ = _odd_pre_fwd(x2, w_mix1, mix_norm[1:2], qbt, kvbt, q_a_norm, kv_a_norm,
                                                pw_bd, pool_scale, seq)
    (d_out, lse), (w_gu1,) = _attn_fwd(q, kv, kr, cos, sin, gq, gk, seq, _gather_comm([my_gu1]))
    x3 = _odd_post_fwd(x2, c_out, d_out, w_mix1)
    (dy, g1, u1, loss_tile), _ = _ffn_fwd(x3, w_gu1, w_d1, ffn_norm[1:2], "ffn_fwd1", None, target)

    def chunk(rows, padded=False):
        return jnp.zeros((N_DEV, rows, D), BF16) if padded else lax.empty((N_DEV, rows, D), BF16)

    (dx3, act1, dg1, du1, h3, dgam_f1), _ = _ffn_bwd(x3, g1, u1, dy, w_gu1, w_d1, ffn_norm[1:2], "ffn_bwd1")
    gp_ffn1 = _tn([dg1], h3, 1408, "dw_gate1", (chunk(R_GU + N_FF), N_FF, OFF_GATE))
    gp_ffn1 = _tn([du1], h3, 1408, "dw_up1", (gp_ffn1, N_FF, OFF_UP))
    gp_ffn1 = _tn([act1], dy, 1408, "dw_down1", (gp_ffn1, N_FF, R_GU))

    dmix_o, (ga_ffn1,) = _odd_post_bwd(dx3, w_mix1, _pair_exchange_comm(gp_ffn1))
    pb_ffn1 = _rs_pair_sum(gp_ffn1, ga_ffn1, where, "rs_pair_sum_ffn1")
    gp_mix1 = _tn([c_out, d_out], dx3, D, "dw_oout", (chunk(R_MIX1, True), N_SQ, OFF_OOUT))
    (dq, dkv, dkr, dgq, dgk), (gb_ffn1,) = _attn_bwd(q, kv, kr, cos, sin, gq, gk, dmix_o, d_out, lse, seq,
                                                    _chip_exchange_comm(pb_ffn1))
    (dx2, dproj_o, h2, qn, kvn, dgam_m1, dqa, dkva, dpw_bd, dps) = _odd_pre_bwd(
        x2, proj_o, dx3, dmix_o, dq, dkv, dkr, w_mix1, mix_norm[1:2], qbt, kvbt, q_a_norm, kv_a_norm, pw_bd,
        pool_scale, seq)
    gp_mix1 = _tn([h2], dproj_o, D, "dw_oin", (gp_mix1, N_SQ, OFF_OIN))
    d_qbt = _tn([dq], qn, HEADS * HP, "dw_qb")
    d_qb_rows = d_qbt.reshape(HEADS, HP, Q_LORA)[:, :QK_DIM].reshape(N_DEV, N_QB_USED, D)
    d_kvb_rows = _tn([dkv], kvn, HEADS * HP, "dw_kvb").reshape(N_DEV, N_KVB, D)
    gp_mix1 = lax.dynamic_update_slice(gp_mix1, d_qb_rows, (0, OFF_QB, 0))
    gp_mix1 = lax.dynamic_update_slice(gp_mix1, d_kvb_rows, (0, OFF_KVB, 0))

    (dx1, act0, dg0, du0, h1, dgam_f0), (ga_mix1,) = _ffn_bwd(x1, g0, u0, dx2, w_gu0, w_d0, ffn_norm[0:1], "ffn_bwd0",
                                                             _pair_exchange_comm(gp_mix1))
    pb_mix1 = _rs_pair_sum(gp_mix1, ga_mix1, where, "rs_pair_sum_mix1")
    gp_ffn0a, (gb_mix1,) = _tn([dg0], h1, 1408, "dw_gate0", (chunk(R_GU), N_FF, OFF_GATE),
                               _chip_exchange_comm(pb_mix1))
    gp_ffn0a = _tn([du0], h1, 1408, "dw_up0", (gp_ffn0a, N_FF, OFF_UP))
    gp_ffn0b, (ga_ffn0a,) = _tn([act0], dx2, 1408, "dw_down0", (chunk(N_FF), N_FF, 0),
                                _pair_exchange_comm(gp_ffn0a))
    pb_ffn0a = _rs_pair_sum(gp_ffn0a, ga_ffn0a, where, "rs_pair_sum_ffn0a")

    (dx0, dproj_e, mix_e, h0, dgam_m0, dws, dbc, dlng, dcw), (gb_ffn0a, ga_ffn0b) = _even_bwd(
        x0, proj_e, dx1, w_mix0, mix_norm[0:1], sg_ln_g, ws, bst, cw, seq,
        _both(_chip_exchange_comm(pb_ffn0a), _pair_exchange_comm(gp_ffn0b)))
    pb_ffn0b = _rs_pair_sum(gp_ffn0b, ga_ffn0b, where, "rs_pair_sum_ffn0b")

    small = _small_pack([
        jnp.concatenate([dgam_m0, dgam_m1], 0), jnp.concatenate([dgam_f0, dgam_f1], 0), dlng,
        dws[None], dbc[:, :SG_HEADS].T[None], dcw[:3],
        jnp.stack([dpw_bd[g * POOL_GD:(g + 1) * POOL_GD, g * POOL_GD:(g + 1) * POOL_GD] for g in range(4)])[None],
        dps, dqa, dkva, dgq[:, :QK_DIM] * ATT_SCALE, dgk[:, :QK_DIM], loss_tile[0:1, 0:1]])
    gp_mix0, (small_all,) = _tn([mix_e], dx1, D, "dw_eout", (chunk(R_MIX0, True), N_SQ, OFF_EOUT),
                                _gather_comm([small]))
    gp_mix0, (gb_ffn0b,) = _tn([dproj_e], h0, 1280, "dw_ein", (gp_mix0, N_EIN, OFF_EIN),
                               _chip_exchange_comm(pb_ffn0b))
    small_sum = _small_unpack(_sum_gathered(small_all), SMALL_SHAPES)
    partials = ([pb_ffn0a, pb_ffn0b, pb_mix1, pb_ffn1], [gb_ffn0a, gb_ffn0b, gb_mix1, gb_ffn1])
    return dx0.reshape(bsz, seq, D), partials, gp_mix0, small_sum


SMALL_SHAPES = [(2, D), (2, D), (1, SG_W), (1, SG_HEADS, 128, 128), (1, SG_HEADS, 128), (3, SC_W),
                (1, 4, POOL_GD, POOL_GD), (1, POOL_W), (1, Q_LORA), (1, KV_LORA), (1, QK_DIM), (1, QK_DIM), (1, 1)]


def kernel(x, positions, mix_norm, ffn_norm, even_w_in, sg_ln_g, sg_w_s, sg_b_s, sc_conv_w, even_w_out, odd_w_in, pool_w, pool_scale, q_a_norm, q_b, kv_a_norm, kv_b, q_norm, k_norm, odd_w_out, ffn_w_gate, ffn_w_up, ffn_w_down, loss_target, m_mix_norm, m_ffn_norm, m_even_w_in, m_sg_ln_g, m_sg_w_s, m_sg_b_s, m_sc_conv_w, m_even_w_out, m_odd_w_in, m_pool_w, m_pool_scale, m_q_a_norm, m_q_b, m_kv_a_norm, m_kv_b, m_q_norm, m_k_norm, m_odd_w_out, m_ffn_w_gate, m_ffn_w_up, m_ffn_w_down, v_mix_norm, v_ffn_norm, v_even_w_in, v_sg_ln_g, v_sg_w_s, v_sg_b_s, v_sc_conv_w, v_even_w_out, v_odd_w_in, v_pool_w, v_pool_scale, v_q_a_norm, v_q_b, v_kv_a_norm, v_kv_b, v_q_norm, v_k_norm, v_odd_w_out, v_ffn_w_gate, v_ffn_w_up, v_ffn_w_down):
    xi, yi, ci = _place()
    me = 4 * xi + 2 * yi + ci

    chunks = _pack_shards(even_w_in, even_w_out, odd_w_in, q_b, kv_b, odd_w_out, ffn_w_gate, ffn_w_up, ffn_w_down)

    def lane_pad(a):
        return jnp.pad(a, ((0, 0), (0, 128 - a.shape[1])))

    tile = jnp.concatenate([lane_pad(sc_conv_w[0]), lane_pad(pool_scale), lane_pad(q_a_norm), lane_pad(kv_a_norm),
                            jnp.zeros((2, 128), F32)], axis=0)
    chip = 2 * xi + yi
    where = jnp.stack([ci, chip, chip ^ 2, chip ^ 1, chip ^ 3]).astype(jnp.int32)
    grad_x, (pbs, gbs), gp_mix0, tot = _step(
        x, positions, loss_target, chunks, tile, where, mix_norm, ffn_norm, sg_ln_g, sg_w_s, sg_b_s,
        pool_w, q_norm, k_norm)

    (ga_mix0,) = _comm_alone(_pair_exchange_comm(gp_mix0), "rs_pair_exchange_mix0")
    pb_mix0 = _rs_pair_sum(gp_mix0, ga_mix0, where, "rs_pair_sum_mix0")
    (gsh_ffn0a, gsh_ffn0b, gsh_mix1, gsh_ffn1), (gb_mix0,) = _rs_final_sums(
        pbs, gbs, "rs_final_sums", _chip_exchange_comm(pb_mix0))
    (gsh_mix0,) = _rs_final_sums([pb_mix0], [gb_mix0], "rs_final_sum_mix0")

    (g_mix, g_ffn, g_lng, g_ws, g_bs, g_cw_full, g_pw, g_ps_full, g_qa_full, g_kva_full, g_qn, g_kn, loss) = tot
    g_cw = lax.dynamic_slice_in_dim(g_cw_full, me * 64, 64, axis=1)[None]
    g_ps = lax.dynamic_slice_in_dim(g_ps_full, me * 32, 32, axis=1)
    g_qa = lax.dynamic_slice_in_dim(g_qa_full, me * 48, 48, axis=1)
    g_kva = lax.dynamic_slice_in_dim(g_kva_full, me * 32, 32, axis=1)

    def tr(a):
        return jnp.swapaxes(a, -1, -2)

    g_gate = tr(jnp.stack([gsh_ffn0a[OFF_GATE:OFF_GATE + N_FF], gsh_ffn1[OFF_GATE:OFF_GATE + N_FF]]))
    g_up = tr(jnp.stack([gsh_ffn0a[OFF_UP:OFF_UP + N_FF], gsh_ffn1[OFF_UP:OFF_UP + N_FF]]))
    g_down = jnp.stack([gsh_ffn0b, gsh_ffn1[R_GU:R_GU + N_FF]])
    g_oin = gsh_mix1[OFF_OIN:OFF_OIN + N_SQ, :ODD_IN][None]
    g_oout = gsh_mix1[OFF_OOUT:OFF_OOUT + N_SQ][None]
    g_qb = tr(gsh_mix1[OFF_QB:OFF_QB + N_QB_USED].reshape(1, 144, Q_LORA))
    g_kvb = tr(gsh_mix1[OFF_KVB:OFF_KVB + N_KVB].reshape(1, 192, KV_LORA))
    transposed = ("even_w_in", "odd_w_in", "q_b", "kv_b", "ffn_w_gate", "ffn_w_up")

    names = ("mix_norm", "ffn_norm", "even_w_in", "sg_ln_g", "sg_w_s", "sg_b_s", "sc_conv_w", "even_w_out",
             "odd_w_in", "pool_w", "pool_scale", "q_a_norm", "q_b", "kv_a_norm", "kv_b", "q_norm", "k_norm",
             "odd_w_out", "ffn_w_gate", "ffn_w_up", "ffn_w_down")
    grads = dict(mix_norm=g_mix, ffn_norm=g_ffn, sg_ln_g=g_lng, sg_w_s=g_ws, sg_b_s=g_bs,
                 sc_conv_w=g_cw, odd_w_in=g_oin, pool_w=g_pw, pool_scale=g_ps, q_a_norm=g_qa,
                 q_b=g_qb, kv_a_norm=g_kva, kv_b=g_kvb, q_norm=g_qn, k_norm=g_kn, odd_w_out=g_oout,
                 ffn_w_gate=g_gate, ffn_w_up=g_up, ffn_w_down=g_down)
    weights = dict(mix_norm=mix_norm, ffn_norm=ffn_norm, even_w_in=even_w_in, sg_ln_g=sg_ln_g, sg_w_s=sg_w_s,
                   sg_b_s=sg_b_s, sc_conv_w=sc_conv_w, even_w_out=even_w_out, odd_w_in=odd_w_in, pool_w=pool_w,
                   pool_scale=pool_scale, q_a_norm=q_a_norm, q_b=q_b, kv_a_norm=kv_a_norm, kv_b=kv_b, q_norm=q_norm,
                   k_norm=k_norm, odd_w_out=odd_w_out, ffn_w_gate=ffn_w_gate, ffn_w_up=ffn_w_up,
                   ffn_w_down=ffn_w_down)
    m_in = dict(mix_norm=m_mix_norm, ffn_norm=m_ffn_norm, even_w_in=m_even_w_in, sg_ln_g=m_sg_ln_g, sg_w_s=m_sg_w_s,
                sg_b_s=m_sg_b_s, sc_conv_w=m_sc_conv_w, even_w_out=m_even_w_out, odd_w_in=m_odd_w_in,
                pool_w=m_pool_w, pool_scale=m_pool_scale, q_a_norm=m_q_a_norm, q_b=m_q_b, kv_a_norm=m_kv_a_norm,
                kv_b=m_kv_b, q_norm=m_q_norm, k_norm=m_k_norm, odd_w_out=m_odd_w_out, ffn_w_gate=m_ffn_w_gate,
                ffn_w_up=m_ffn_w_up, ffn_w_down=m_ffn_w_down)
    v_in = dict(mix_norm=v_mix_norm, ffn_norm=v_ffn_norm, even_w_in=v_even_w_in, sg_ln_g=v_sg_ln_g, sg_w_s=v_sg_w_s,
                sg_b_s=v_sg_b_s, sc_conv_w=v_sc_conv_w, even_w_out=v_even_w_out, odd_w_in=v_odd_w_in,
                pool_w=v_pool_w, pool_scale=v_pool_scale, q_a_norm=v_q_a_norm, q_b=v_q_b, kv_a_norm=v_kv_a_norm,
                kv_b=v_kv_b, q_norm=v_q_norm, k_norm=v_k_norm, odd_w_out=v_odd_w_out, ffn_w_gate=v_ffn_w_gate,
                ffn_w_up=v_ffn_w_up, ffn_w_down=v_ffn_w_down)
    delta, new_m, new_v = {}, {}, {}

    def as2d(k, a):
        a = tr(a) if k in transposed else a
        return a.reshape(-1, a.shape[-1])

    def back(k, a):
        shape = weights[k].shape
        return tr(a.reshape(shape[:-2] + (shape[-1], shape[-2]))) if k in transposed else a.reshape(shape)

    def update(group, name, nblk=1):
        outs = _adamw([as2d(k, weights[k]) for k in group], [as2d(k, grads[k]) for k in group],
                      [as2d(k, m_in[k]) for k in group], [as2d(k, v_in[k]) for k in group], name, nblk)
        for i, k in enumerate(group):
            delta[k], new_m[k], new_v[k] = (back(k, o[i]) for o in outs)

    grads["even_w_in"] = tr(gsh_mix0[OFF_EIN:OFF_EIN + N_EIN][None])
    grads["even_w_out"] = gsh_mix0[OFF_EOUT:OFF_EOUT + N_SQ][None]
    update(["ffn_w_gate", "ffn_w_up", "ffn_w_down"], "adamw_ffn", 4)
    update(["even_w_in", "even_w_out", "odd_w_in", "odd_w_out"], "adamw_mix", 2)
    update([k for k in names if k not in delta], "adamw_small")

    return (loss.reshape(()), grad_x, *[grads[k] for k in names], *[delta[k] for k in names],
            *[new_m[k] for k in names], *[new_v[k] for k in names])
```

```python
import functools

import numpy as np
import jax
import jax.numpy as jnp
from jax import lax
from jax.experimental import pallas as pl
from jax.experimental.pallas import tpu as pltpu

F32 = jnp.float32
BF16 = jnp.bfloat16
MESH = pl.DeviceIdType.MESH

D = 1024
EPS = 1e-6
NEG_INF = -1e30
SG_HEADS, SG_HD, SG_W, SG_CHUNK = 4, 128, 512, 128
SC_W = 512
EVEN_IN = 2560
POOL_W = 256
POOL_GD = 64
Q_LORA, KV_LORA, QK_ROPE, QK_NOPE, V_DIM = 384, 256, 64, 128, 128
QK_DIM = QK_NOPE + QK_ROPE
HEADS = 6
HP = 256
ODD_IN = 960
D_FF = 2816
ROPE_THETA = 10000.0
ATT_SCALE = QK_DIM ** -0.5
LR, B1, B2, ADAM_EPS, WD, STEP = 0.001, 0.9, 0.999, 1e-08, 0.01, 10

N_DEV = 8
TB = 512
TB_FFN_BWD = 256
TK_DW = 1024
HALO = 16
VMEM_LIMIT = 56 * 1024 * 1024

N_EIN, N_FF, N_SQ = 320, 352, 128
OFF_EIN, OFF_EOUT, R_MIX0 = 0, 384, 512
OFF_GATE, OFF_UP, R_GU = 0, 352, 704
OFF_OIN, OFF_OOUT, OFF_QB, OFF_KVB, R_MIX1 = 0, 128, 256, 320, 384
N_QB, N_QB_USED, N_KVB = 64, 54, 48

INV_SQRT2 = 0.7071067811865476
INV_SQRT_2PI = 0.3989422804014327


def _dot(a, b, ca, cb):
    return lax.dot_general(a, b, (((ca,), (cb,)), ((), ())), preferred_element_type=F32)


def _cparams(n_axes=1):
    return pltpu.CompilerParams(dimension_semantics=("arbitrary",) * n_axes, vmem_limit_bytes=VMEM_LIMIT)


def _wspec(n, off, arity=1):
    assert off % n == 0
    idx = off // n
    if arity == 1:
        return pl.BlockSpec((N_DEV, n, D), lambda i: (0, idx, 0), pipeline_mode=pl.Buffered(1))
    return pl.BlockSpec((N_DEV, n, D), lambda i, j: (0, idx, 0), pipeline_mode=pl.Buffered(1))


def _const_spec(shape):
    zeros = (0,) * len(shape)
    return pl.BlockSpec(shape, lambda *_: zeros)


class _Comm:
    def __init__(self, ins, out_shapes, sems, start, wait, mid=None):
        self.ins, self.out_shapes, self.sems, self.start, self.wait, self.mid = ins, out_shapes, sems, start, wait, mid


def _both(c1, c2):
    def split(f1, f2):
        def run(ins, outs, sems):
            f1(ins[:len(c1.ins)], outs[:len(c1.out_shapes)], sems[:len(c1.sems)])
            f2(ins[len(c1.ins):], outs[len(c1.out_shapes):], sems[len(c1.sems):])
        return run

    assert c1.mid is None and c2.mid is None
    return _Comm(c1.ins + c2.ins, c1.out_shapes + c2.out_shapes, c1.sems + c2.sems,
                 split(c1.start, c2.start), split(c1.wait, c2.wait))


def _call(body, name, grid, in_specs, out_specs, out_shape, args, scratch_shapes=(), comm=None, aliases=None):
    n_axes = len(grid)
    aliases = aliases or {}
    if comm is None:
        res = pl.pallas_call(
            body, name=name, grid=grid, in_specs=list(in_specs), out_specs=list(out_specs),
            out_shape=list(out_shape), scratch_shapes=list(scratch_shapes), input_output_aliases=aliases,
            compiler_params=_cparams(n_axes))(*args)
        return list(res), []
    ni, no, ns = len(in_specs), len(out_specs), len(scratch_shapes)
    ci, co = len(comm.ins), len(comm.out_shapes)
    n_steps = int(np.prod(grid))

    def carrier(*refs):
        ins, cin = refs[:ni], refs[ni:ni + ci]
        outs, cout = refs[ni + ci:ni + ci + no], refs[ni + ci + no:ni + ci + no + co]
        scr, sems = refs[ni + ci + no + co:ni + ci + no + co + ns], refs[ni + ci + no + co + ns:]
        step = 0
        for a in range(n_axes):
            step = step * grid[a] + pl.program_id(a)

        @pl.when(step == 0)
        def _():
            comm.start(cin, cout, sems)

        body(*ins, *outs, *scr)

        if comm.mid is not None and n_steps >= 4:
            @pl.when(step == (3 * n_steps) // 4)
            def _():
                comm.mid(cin, cout, sems)

        @pl.when(step == n_steps - 1)
        def _():
            if comm.mid is not None and n_steps < 4:
                comm.mid(cin, cout, sems)
            comm.wait(cin, cout, sems)

    any_spec = pl.BlockSpec(memory_space=pl.ANY)
    res = pl.pallas_call(
        carrier, name=name, grid=grid, in_specs=list(in_specs) + [any_spec] * ci,
        out_specs=list(out_specs) + [any_spec] * co, out_shape=list(out_shape) + list(comm.out_shapes),
        scratch_shapes=list(scratch_shapes) + list(comm.sems), input_output_aliases=aliases,
        compiler_params=_cparams(n_axes))(*args, *comm.ins)
    return list(res[:no]), list(res[no:])


def _comm_alone(comm, name):
    ci, co = len(comm.ins), len(comm.out_shapes)

    def body(*refs):
        cin, cout, sems = refs[:ci], refs[ci:ci + co], refs[ci + co:]
        comm.start(cin, cout, sems)
        if comm.mid is not None:
            comm.mid(cin, cout, sems)
        comm.wait(cin, cout, sems)

    any_spec = pl.BlockSpec(memory_space=pl.ANY)
    res = pl.pallas_call(
        body, name=name, out_shape=list(comm.out_shapes), in_specs=[any_spec] * ci, out_specs=[any_spec] * co,
        scratch_shapes=list(comm.sems))(*comm.ins)
    return list(res)


def _rms(x, g):
    r = lax.rsqrt(jnp.mean(x * x, axis=-1, keepdims=True) + EPS)
    return x * r * g, r


def _rms_bwd(x, r, g, dy):
    xh = x * r
    dxh = dy * g
    dx = r * (dxh - xh * jnp.mean(dxh * xh, axis=-1, keepdims=True))
    dg = jnp.sum(dy * xh, axis=0, keepdims=True)
    return dx, dg


def _gelu(x):
    return 0.5 * x * (1.0 + lax.erf(x * INV_SQRT2))


def _gelu_grad(x):
    return 0.5 * (1.0 + lax.erf(x * INV_SQRT2)) + x * jnp.exp(-0.5 * x * x) * INV_SQRT_2PI


def _shift_down(a, k):
    rows = lax.broadcasted_iota(jnp.int32, a.shape, 0)
    return jnp.where(rows >= k, pltpu.roll(a, k, 0), 0.0)


def _shift_up(a, k):
    n = a.shape[0]
    rows = lax.broadcasted_iota(jnp.int32, a.shape, 0)
    return jnp.where(rows < n - k, pltpu.roll(a, n - k, 0), 0.0)


def _tril_bf16(w):
    r = lax.broadcasted_iota(jnp.int32, w.shape, 0)
    c = lax.broadcasted_iota(jnp.int32, w.shape, 1)
    return jnp.where(r >= c, w, 0.0).astype(BF16)


def _ln_head(vh, g):
    mu = jnp.mean(vh, axis=-1, keepdims=True)
    xc = vh - mu
    rr = lax.rsqrt(jnp.mean(xc * xc, axis=-1, keepdims=True) + EPS)
    xh = xc * rr
    return xh * g, xh, rr


def _conv_fwd(z, tail, cw_ref):
    ext = jnp.concatenate([tail, z], axis=0)
    zs1 = _shift_down(ext, 1)[HALO:]
    zs2 = _shift_down(ext, 2)[HALO:]
    y = cw_ref[2:3, :] * z + cw_ref[1:2, :] * zs1 + cw_ref[0:1, :] * zs2
    return y, zs1, zs2


def _pool_cnt(shape, blk_in_seq):
    rows = lax.broadcasted_iota(jnp.int32, shape, 0)
    grp = lax.broadcasted_iota(jnp.int32, shape, 1) // POOL_GD
    win = jnp.where(grp == 0, 2, jnp.where(grp == 1, 4, jnp.where(grp == 2, 8, 16)))
    tpos = blk_in_seq * shape[0] + rows + 1
    return jnp.minimum(tpos, win).astype(F32), grp


def _pool_select(grp, s2, s4, s8, s16):
    return jnp.where(grp == 0, s2, jnp.where(grp == 1, s4, jnp.where(grp == 2, s8, s16)))


def _pool_fwd(z, tail, blk_in_seq):
    ext = jnp.concatenate([tail, z], axis=0)
    s2 = ext + _shift_down(ext, 1)
    s4 = s2 + _shift_down(s2, 2)
    s8 = s4 + _shift_down(s4, 4)
    s16 = s8 + _shift_down(s8, 8)
    cnt, grp = _pool_cnt(z.shape, blk_in_seq)
    sums = _pool_select(grp, s2[HALO:], s4[HALO:], s8[HALO:], s16[HALO:])
    return sums / cnt - z, cnt, grp


def _pool_bwd(dpooled, dpm, head, grp):
    n = dpm.shape[0]
    ext = jnp.concatenate([dpm, head], axis=0)
    u2 = ext + _shift_up(ext, 1)
    u4 = u2 + _shift_up(u2, 2)
    u8 = u4 + _shift_up(u4, 4)
    u16 = u8 + _shift_up(u8, 8)
    return _pool_select(grp, u2[:n], u4[:n], u8[:n], u16[:n]) - dpooled


def _lane_sums(a):
    return _dot(a.astype(BF16), jnp.ones((a.shape[1], a.shape[1]), BF16), 1, 0)


def _swap_halves(y1):
    src = lax.broadcasted_iota(jnp.int32, (128, 128), 0)
    dst = lax.broadcasted_iota(jnp.int32, (128, 128), 1)
    perm = jnp.where(((dst < 32) & (src == dst + 32)) | ((dst >= 32) & (dst < QK_ROPE) & (src == dst - 32)), 1.0, 0.0)
    return _dot(y1.astype(BF16), perm.astype(BF16), 1, 0)


def _rope(y1, c, s):
    return y1 * c + _swap_halves(y1) * s


def _rope_bwd(d1, c, s):
    return d1 * c + _swap_halves(d1 * s)


def _qk_prep(x, g, c, s):
    r = lax.rsqrt(_lane_sums(x * x) * (1.0 / QK_DIM) + EPS)
    y = x * r * g
    return jnp.concatenate([y[:, :128], _rope(y[:, 128:], c, s)], axis=1), r


def _qk_prep_bwd(dout, x, r, g, c, s):
    dy = jnp.concatenate([dout[:, :128], _rope_bwd(dout[:, 128:], c, s)], axis=1)
    xh = x * r
    dxh = dy * g
    dx = r * (dxh - xh * (_lane_sums(dxh * xh) * (1.0 / QK_DIM)))
    return dx, jnp.sum(dy * xh, axis=0, keepdims=True)


def _place():
    return lax.axis_index("x"), lax.axis_index("y"), lax.axis_index("c")


def _gather_comm(arrs):
    n = len(arrs)

    def plan(ins, outs, sems):
        send_sems, recv_sems, local_sems = sems
        x, y, c = _place()
        me, sibling = (x, y, c), (x, y, 1 - c)
        chips = [(1 - x, y), (x, 1 - y), (1 - x, 1 - y)]

        def slot(a, px, py, pc):
            return outs[a].at[4 * px + 2 * py + pc]

        def copy(a, k, block, to, src=None):
            return pltpu.make_async_remote_copy(
                src_ref=slot(a, *block) if src is None else src, dst_ref=slot(a, *block),
                send_sem=send_sems.at[a, k], recv_sem=recv_sems.at[a, k], device_id=to, device_id_type=MESH)

        def own():
            mine = [pltpu.make_async_copy(ins[a], slot(a, *me), local_sems.at[a]) for a in range(n)]
            first = []
            for a in range(n):
                first.append(copy(a, 0, me, sibling, src=ins[a]))
                first += [copy(a, 1 + j, me, (*chip, c), src=ins[a]) for j, chip in enumerate(chips)]
            return mine, first

        return c, me, sibling, chips, copy, own

    def start(ins, outs, sems):
        mine, first = plan(ins, outs, sems)[-1]()
        for cp in mine + first:
            cp.start()

    def mid(ins, outs, sems):
        c, me, sibling, chips, copy, _ = plan(ins, outs, sems)
        for j, chip in enumerate(chips):
            for a in range(n):
                copy(a, 1 + j, (*chip, c), me).wait_recv()
                copy(a, 4 + j, (*chip, c), sibling).start()

    def wait(ins, outs, sems):
        c, me, sibling, chips, copy, own = plan(ins, outs, sems)
        mine, first = own()
        passed = [copy(a, 4 + j, (*chip, c), sibling) for j, chip in enumerate(chips) for a in range(n)]
        for a in range(n):
            copy(a, 0, sibling, me).wait_recv()
            for j, chip in enumerate(chips):
                copy(a, 4 + j, (*chip, 1 - c), me).wait_recv()
        for cp in first + passed:
            cp.wait_send()
        for cp in mine:
            cp.wait()

    return _Comm(
        list(arrs), [jax.ShapeDtypeStruct((N_DEV,) + a.shape, a.dtype) for a in arrs],
        [pltpu.SemaphoreType.DMA((n, 7)), pltpu.SemaphoreType.DMA((n, 7)), pltpu.SemaphoreType.DMA((n,))],
        start, wait, mid)


def _sum_gathered(g):
    rows = g.shape[1]

    def body(g_ref, sum_ref):
        total = g_ref[0]
        for d in range(1, N_DEV):
            total = total + g_ref[d]
        sum_ref[...] = total

    return pl.pallas_call(
        body, name="sum_gathered_small", out_shape=jax.ShapeDtypeStruct((rows, 128), F32), grid=(1,),
        in_specs=[pl.BlockSpec((N_DEV, rows, 128), lambda i: (0, 0, 0))],
        out_specs=pl.BlockSpec((rows, 128), lambda i: (0, 0)), compiler_params=_cparams(1),
    )(g)


def _sum_rows(rows):
    return rows if rows <= 512 else rows // 2


def _pair_exchange_comm(gp):
    _, rows, cols = gp.shape

    def copies(ins, outs, sems):
        send_sems, recv_sems = sems
        x, y, c = _place()
        return [pltpu.make_async_remote_copy(
            src_ref=ins[0].at[2 * j + (1 - c)], dst_ref=outs[0].at[j], send_sem=send_sems.at[j],
            recv_sem=recv_sems.at[j], device_id=(x, y, 1 - c), device_id_type=MESH) for j in range(4)]

    def start(ins, outs, sems):
        for cp in copies(ins, outs, sems):
            cp.start()

    def wait(ins, outs, sems):
        for cp in copies(ins, outs, sems):
            cp.wait()

    return _Comm([gp], [jax.ShapeDtypeStruct((4, rows, cols), gp.dtype)],
                 [pltpu.SemaphoreType.DMA((4,)), pltpu.SemaphoreType.DMA((4,))], start, wait)


def _rs_pair_sum(gp, got, where, name):
    _, rows, cols = got.shape
    rb = _sum_rows(rows)
    gp4 = gp.reshape(4, 2, rows, cols)

    def body(w_ref, a_ref, b_ref, o_ref):
        o_ref[0] = (a_ref[0, 0].astype(F32) + b_ref[0].astype(F32)).astype(o_ref.dtype)

    return pl.pallas_call(
        body, name=name, out_shape=jax.ShapeDtypeStruct((4, rows, cols), gp.dtype),
        grid_spec=pltpu.PrefetchScalarGridSpec(
            num_scalar_prefetch=1, grid=(4, rows // rb),
            in_specs=[pl.BlockSpec((1, 1, rb, cols), lambda k, r, w: (w[1 + k], w[0], r, 0)),
                      pl.BlockSpec((1, rb, cols), lambda k, r, w: (w[1 + k], r, 0))],
            out_specs=pl.BlockSpec((1, rb, cols), lambda k, r, w: (k, r, 0))),
        compiler_params=_cparams(2),
    )(where, gp4, got)


def _chip_exchange_comm(pb):
    _, rows, cols = pb.shape

    def copies(ins, outs, sems):
        send_sems, recv_sems = sems
        x, y, c = _place()
        chips = [(1 - x, y), (x, 1 - y), (1 - x, 1 - y)]
        return [pltpu.make_async_remote_copy(
            src_ref=ins[0].at[1 + k], dst_ref=outs[0].at[k], send_sem=send_sems.at[k],
            recv_sem=recv_sems.at[k], device_id=(px, py, c), device_id_type=MESH)
            for k, (px, py) in enumerate(chips)]

    def start(ins, outs, sems):
        for cp in copies(ins, outs, sems):
            cp.start()

    def wait(ins, outs, sems):
        for cp in copies(ins, outs, sems):
            cp.wait()

    return _Comm([pb], [jax.ShapeDtypeStruct((3, rows, cols), pb.dtype)],
                 [pltpu.SemaphoreType.DMA((3,)), pltpu.SemaphoreType.DMA((3,))], start, wait)


def _chip_exchange_start(pb):
    _, rows, cols = pb.shape

    def body(pb_ref, land_ref, *rest):
        sems, token = rest[:6], rest[8]
        x, y, c = _place()
        chips = [(1 - x, y), (x, 1 - y), (1 - x, 1 - y)]
        for k, (px, py) in enumerate(chips):
            pltpu.make_async_remote_copy(
                src_ref=pb_ref.at[1 + k], dst_ref=land_ref.at[k], send_sem=sems[k], recv_sem=sems[3 + k],
                device_id=(px, py, c), device_id_type=MESH).start()
        token[...] = jnp.zeros_like(token)

    hbm = pl.BlockSpec(memory_space=pltpu.HBM)
    sem = pl.BlockSpec(memory_space=pltpu.SEMAPHORE)
    land = lax.empty((3, rows, cols), pb.dtype)
    res = pl.pallas_call(
        body, name="rs_chip_exchange_start_mix0",
        out_shape=(*[pltpu.SemaphoreType.DMA(())] * 6, pltpu.HBM(pb.shape, pb.dtype), pltpu.HBM(land.shape, land.dtype),
                   jax.ShapeDtypeStruct((8, 128), F32)),
        in_specs=(hbm, hbm), out_specs=(*[sem] * 6, hbm, hbm, pl.BlockSpec(memory_space=pltpu.VMEM)),
        input_output_aliases={0: 6, 1: 7},
        compiler_params=pltpu.CompilerParams(has_side_effects=pltpu.SideEffectType.DATAFLOW_SIDE_EFFECTING),
    )(pltpu.with_memory_space_constraint(pb, pltpu.HBM), pltpu.with_memory_space_constraint(land, pltpu.HBM))
    return list(res[:6]), res[6], res[7], res[8]


def _chip_exchange_wait(sems, pb_thru, land_thru, after):
    def body(pb_ref, land_ref, *rest):
        sems_in = rest[:6]
        x, y, c = _place()
        chips = [(1 - x, y), (x, 1 - y), (1 - x, 1 - y)]
        for k, (px, py) in enumerate(chips):
            cp = pltpu.make_async_remote_copy(
                src_ref=pb_ref.at[1 + k], dst_ref=land_ref.at[k], send_sem=sems_in[k], recv_sem=sems_in[3 + k],
                device_id=(px, py, c), device_id_type=MESH)
            cp.wait_send()
            cp.wait_recv()

    hbm = pl.BlockSpec(memory_space=pltpu.HBM)
    sem = pl.BlockSpec(memory_space=pltpu.SEMAPHORE)
    res = pl.pallas_call(
        body, name="rs_chip_exchange_wait_mix0",
        out_shape=(pltpu.HBM(pb_thru.shape, pb_thru.dtype), pltpu.HBM(land_thru.shape, land_thru.dtype)),
        in_specs=(hbm, hbm, *[sem] * 6, pl.BlockSpec(memory_space=pl.ANY)), out_specs=(hbm, hbm),
        input_output_aliases={0: 0, 1: 1},
        compiler_params=pltpu.CompilerParams(has_side_effects=pltpu.SideEffectType.DATAFLOW_SIDE_EFFECTING),
    )(pb_thru, land_thru, *sems, after)
    return res[0], res[1]


def _rs_final_sums(pbs, gots, name, after=None):
    n = len(pbs)

    def body(*refs):
        outs = refs[len(refs) - n:]
        for a in range(n):
            m_ref, g_ref, o_ref = refs[a], refs[n + a], outs[a]
            o_ref[...] = ((m_ref[0].astype(F32) + g_ref[0].astype(F32)) + g_ref[1].astype(F32)) + g_ref[2].astype(F32)

    half = [pb.shape[1] // 2 for pb in pbs]
    in_specs = ([pl.BlockSpec((1, h, D), lambda i: (0, i, 0)) for h in half]
                + [pl.BlockSpec((3, h, D), lambda i: (0, i, 0)) for h in half])
    args = (*pbs, *gots)
    if after is not None:
        in_specs, args = in_specs + [pl.BlockSpec(memory_space=pl.ANY)], args + (after,)
    res, _ = _call(body, name, (2,), in_specs, [pl.BlockSpec((h, D), lambda i: (i, 0)) for h in half],
                   [jax.ShapeDtypeStruct(pb.shape[1:], F32) for pb in pbs], args)
    return res


def _rope_tables(pos_col, inv_freq, comm=None):
    t = pos_col.shape[0]

    def body(p_ref, f_ref, c_ref, s_ref):
        ang = p_ref[...].astype(F32) * f_ref[...]
        lane = lax.broadcasted_iota(jnp.int32, ang.shape, 1)
        c_ref[...] = jnp.where(lane < QK_ROPE, jnp.cos(ang), 0.0)
        s = jnp.sin(ang)
        s_ref[...] = jnp.where(lane < 32, -s, jnp.where(lane < QK_ROPE, s, 0.0))

    spec = pl.BlockSpec((TB, 128), lambda i: (i, 0))
    return _call(
        body, "rope_tables", (t // TB,), [pl.BlockSpec((TB, 1), lambda i: (i, 0)), _const_spec((1, 128))],
        [spec] * 2, [jax.ShapeDtypeStruct((t, 128), F32)] * 2, (pos_col, inv_freq), (), comm)


def _sgu_conv_fwd(proj, tail, lng_ref, ws_ref, bst_ref, cw_ref):
    gu = _gelu(proj[:, 0:SG_W])
    gv = _gelu(proj[:, SG_W:2 * SG_W])
    bg = proj[:, 1024:1536]
    z = proj[:, 1536:2048] * proj[:, 2048:2560]
    heads = []
    for h in range(SG_HEADS):
        sl = slice(h * SG_HD, (h + 1) * SG_HD)
        vn, _, _ = _ln_head(gv[:, sl], lng_ref[:, sl])
        vnb = vn.astype(BF16)
        wm = _tril_bf16(ws_ref[h])
        bcol = bst_ref[:, h:h + 1]
        mixed = jnp.concatenate(
            [_dot(wm, vnb[k * SG_CHUNK:(k + 1) * SG_CHUNK], 1, 0) + bcol for k in range(TB // SG_CHUNK)], axis=0)
        heads.append(gu[:, sl] * mixed)
    a_out = jnp.concatenate(heads, axis=1)
    y, _, _ = _conv_fwd(z, tail, cw_ref)
    return a_out, bg * y, z


def _even_fwd(x, wg, gamma, lng, ws, bst, cw, seq, comm=None):
    t = x.shape[0]
    nbs = seq // TB

    def body(x_ref, gam_ref, win_ref, wout_ref, lng_ref, ws_ref, bst_ref, cw_ref, x1_ref, proj_ref, tail_ref):
        i = pl.program_id(0)
        xv = x_ref[...]
        h, _ = _rms(xv, gam_ref[...])
        proj = _dot(h.astype(BF16), win_ref[...].reshape(EVEN_IN, D), 1, 1)
        proj_ref[...] = proj.astype(BF16)
        tail = jnp.where(i % nbs == 0, 0.0, tail_ref[...])
        a_out, b_out, z = _sgu_conv_fwd(proj, tail, lng_ref, ws_ref, bst_ref, cw_ref)
        tail_ref[...] = z[TB - HALO:, :]
        x1_ref[...] = (xv + _dot(a_out.astype(BF16), wout_ref[0:4].reshape(512, D), 1, 0)
                       + _dot(b_out.astype(BF16), wout_ref[4:8].reshape(512, D), 1, 0))

    row = pl.BlockSpec((TB, D), lambda i: (i, 0))
    return _call(
        body, "even_fwd", (t // TB,),
        [row, _const_spec((1, D)), _wspec(N_EIN, OFF_EIN), _wspec(N_SQ, OFF_EOUT), _const_spec((1, SG_W)),
         _const_spec((SG_HEADS, 128, 128)), _const_spec((128, 128)), _const_spec((8, SC_W))],
        [row, pl.BlockSpec((TB, EVEN_IN), lambda i: (i, 0))],
        [jax.ShapeDtypeStruct((t, D), F32), jax.ShapeDtypeStruct((t, EVEN_IN), BF16)],
        (x, gamma, wg, wg, lng, ws, bst, cw), [pltpu.VMEM((HALO, SC_W), F32)], comm)


def _even_bwd(x, proj, dx1, wg, gamma, lng, ws, bst, cw, seq, comm=None):
    t = x.shape[0]
    nb, nbs = t // TB, seq // TB

    def body(x_ref, proj_ref, ptail_ref, dx1_ref, gam_ref, win_ref, wout_ref, lng_ref, ws_ref, bst_ref, cw_ref,
             dx0_ref, dproj_ref, mix_ref, h_ref, dgam_ref, dws_ref, dbc_ref, dlng_ref, dcw_ref, head_ref):
        i = pl.program_id(0)
        blk = nb - 1 - i

        @pl.when(i == 0)
        def _():
            dgam_ref[...] = jnp.zeros_like(dgam_ref)
            dws_ref[...] = jnp.zeros_like(dws_ref)
            dbc_ref[...] = jnp.zeros_like(dbc_ref)
            dlng_ref[...] = jnp.zeros_like(dlng_ref)
            dcw_ref[...] = jnp.zeros_like(dcw_ref)

        xv = x_ref[...]
        gam = gam_ref[...]
        h, r = _rms(xv, gam)
        h_ref[...] = h.astype(BF16)
        dx1 = dx1_ref[...]
        dmix = _dot(dx1.astype(BF16), wout_ref[...].reshape(D, D), 1, 1)
        da, db = dmix[:, :SG_W], dmix[:, SG_W:]
        proj = proj_ref[...].astype(F32)
        u, v = proj[:, 0:SG_W], proj[:, SG_W:2 * SG_W]
        bg, cg, hv = proj[:, 1024:1536], proj[:, 1536:2048], proj[:, 2048:2560]
        gu, gv = _gelu(u), _gelu(v)

        a_heads, dgv_heads = [], []
        for hd in range(SG_HEADS):
            sl = slice(hd * SG_HD, (hd + 1) * SG_HD)
            g_h = lng_ref[:, sl]
            vn, xh, rr = _ln_head(gv[:, sl], g_h)
            vnb = vn.astype(BF16)
            wm = _tril_bf16(ws_ref[hd])
            bcol = bst_ref[:, hd:hd + 1]
            mixed_c, dvn_c = [], []
            dw_acc = jnp.zeros((128, 128), F32)
            db_acc = jnp.zeros((128, 1), F32)
            for k in range(TB // SG_CHUNK):
                rs = slice(k * SG_CHUNK, (k + 1) * SG_CHUNK)
                mixed = _dot(wm, vnb[rs], 1, 0) + bcol
                dmixed = da[rs, sl] * gu[rs, sl]
                dmb = dmixed.astype(BF16)
                dvn_c.append(_dot(wm, dmb, 0, 0))
                dw_acc = dw_acc + _dot(dmb, vnb[rs], 1, 1)
                db_acc = db_acc + jnp.sum(dmixed, axis=1, keepdims=True)
                mixed_c.append(mixed)
            mixed_h = jnp.concatenate(mixed_c, axis=0)
            dvn = jnp.concatenate(dvn_c, axis=0)
            r_i = lax.broadcasted_iota(jnp.int32, (128, 128), 0)
            c_i = lax.broadcasted_iota(jnp.int32, (128, 128), 1)
            dws_ref[hd] += jnp.where(r_i >= c_i, dw_acc, 0.0)
            dbc_ref[:, hd:hd + 1] += db_acc
            dlng_ref[:, sl] += jnp.sum(dvn * xh, axis=0, keepdims=True)
            dxh = dvn * g_h
            dgv = rr * (dxh - jnp.mean(dxh, axis=-1, keepdims=True)
                        - xh * jnp.mean(dxh * xh, axis=-1, keepdims=True))
            a_heads.append(gu[:, sl] * mixed_h)
            dproj_ref[:, sl] = (da[:, sl] * mixed_h * _gelu_grad(u[:, sl])).astype(BF16)
            dgv_heads.append(dgv * _gelu_grad(v[:, sl]))
        dproj_ref[:, SG_W:2 * SG_W] = jnp.concatenate(dgv_heads, axis=1).astype(BF16)
        mix_ref[:, :SG_W] = jnp.concatenate(a_heads, axis=1).astype(BF16)

        z = cg * hv
        pt = ptail_ref[...].astype(F32)
        tail = jnp.where(blk % nbs == 0, 0.0, pt[:, 1536:2048] * pt[:, 2048:2560])
        y, zs1, zs2 = _conv_fwd(z, tail, cw_ref)
        mix_ref[:, SG_W:] = (bg * y).astype(BF16)
        dy = db * bg
        head = jnp.where(blk % nbs == nbs - 1, 0.0, head_ref[...])
        ext = jnp.concatenate([dy, head], axis=0)
        dz = (cw_ref[2:3, :] * dy + cw_ref[1:2, :] * _shift_up(ext, 1)[:TB]
              + cw_ref[0:1, :] * _shift_up(ext, 2)[:TB])
        head_ref[...] = dy[:HALO, :]
        dcw_ref[2:3, :] += jnp.sum(dy * z, axis=0, keepdims=True)
        dcw_ref[1:2, :] += jnp.sum(dy * zs1, axis=0, keepdims=True)
        dcw_ref[0:1, :] += jnp.sum(dy * zs2, axis=0, keepdims=True)
        dproj_ref[:, 1024:1536] = (db * y).astype(BF16)
        dproj_ref[:, 1536:2048] = (dz * hv).astype(BF16)
        dproj_ref[:, 2048:2560] = (dz * cg).astype(BF16)

        dh = _dot(dproj_ref[...], win_ref[...].reshape(EVEN_IN, D), 1, 0)
        dxn, dgam = _rms_bwd(xv, r, gam, dh)
        dgam_ref[...] += dgam
        dx0_ref[...] = dx1 + dxn

    def rev(w):
        return pl.BlockSpec((TB, w), lambda i: (nb - 1 - i, 0))

    ptail = pl.BlockSpec((HALO, EVEN_IN), lambda i: (jnp.maximum((nb - 1 - i) * (TB // HALO) - 1, 0), 0))
    return _call(
        body, "even_bwd", (nb,),
        [rev(D), rev(EVEN_IN), ptail, rev(D), _const_spec((1, D)), _wspec(N_EIN, OFF_EIN),
         _wspec(N_SQ, OFF_EOUT), _const_spec((1, SG_W)), _const_spec((SG_HEADS, 128, 128)),
         _const_spec((128, 128)), _const_spec((8, SC_W))],
        [rev(D), rev(EVEN_IN), rev(D), rev(D), _const_spec((1, D)), _const_spec((SG_HEADS, 128, 128)),
         _const_spec((128, 128)), _const_spec((1, SG_W)), _const_spec((8, SC_W))],
        [jax.ShapeDtypeStruct((t, D), F32), jax.ShapeDtypeStruct((t, EVEN_IN), BF16),
         jax.ShapeDtypeStruct((t, D), BF16), jax.ShapeDtypeStruct((t, D), BF16),
         jax.ShapeDtypeStruct((1, D), F32), jax.ShapeDtypeStruct((SG_HEADS, 128, 128), F32),
         jax.ShapeDtypeStruct((128, 128), F32), jax.ShapeDtypeStruct((1, SG_W), F32),
         jax.ShapeDtypeStruct((8, SC_W), F32)],
        (x, proj, proj, dx1, gamma, wg, wg, lng, ws, bst, cw), [pltpu.VMEM((HALO, SC_W), F32)], comm)


def _ffn_fwd(x, w_gu, w_d, gamma, name, comm=None, target=None):
    t = x.shape[0]
    last = target is not None

    def body(*refs):
        x_ref, gam_ref, wg_ref, wu_ref, wd_ref = refs[:5]
        y_ref, g_ref, u_ref = refs[5 + last:8 + last]
        xv = x_ref[...]
        h, _ = _rms(xv, gam_ref[...])
        hb = h.astype(BF16)
        g = _dot(hb, wg_ref[...].reshape(D_FF, D), 1, 1)
        u = _dot(hb, wu_ref[...].reshape(D_FF, D), 1, 1)
        g_ref[...] = g.astype(BF16)
        u_ref[...] = u.astype(BF16)
        act = g * jax.nn.sigmoid(g) * u
        y = xv + _dot(act.astype(BF16), wd_ref[...].reshape(D_FF, D), 1, 0)
        if not last:
            y_ref[...] = y
            return
        loss_ref = refs[9]

        @pl.when(pl.program_id(0) == 0)
        def _():
            loss_ref[...] = jnp.zeros_like(loss_ref)

        err = y - refs[5][...]
        y_ref[...] = err * (1.0 / D)
        sq = jnp.sum(jnp.sum(err * err, axis=-1, keepdims=True), axis=0, keepdims=True)
        loss_ref[...] += (0.5 / D) * sq

    row = pl.BlockSpec((TB, D), lambda i: (i, 0))
    wide = pl.BlockSpec((TB, D_FF), lambda i: (i, 0))
    in_specs = [row, _const_spec((1, D)), _wspec(N_FF, OFF_GATE), _wspec(N_FF, OFF_UP), _wspec(N_FF, 0)]
    out_specs = [row, wide, wide]
    out_shape = [jax.ShapeDtypeStruct((t, D), F32), jax.ShapeDtypeStruct((t, D_FF), BF16),
                 jax.ShapeDtypeStruct((t, D_FF), BF16)]
    args = (x, gamma, w_gu, w_gu, w_d)
    if last:
        in_specs, args = in_specs + [row], args + (target,)
        out_specs, out_shape = out_specs + [_const_spec((8, 128))], out_shape + [jax.ShapeDtypeStruct((8, 128), F32)]
    return _call(body, name, (t // TB,), in_specs, out_specs, out_shape, args, (), comm)


def _ffn_up(x, w_gu, gamma, name, comm=None):
    t = x.shape[0]

    def body(x_ref, gam_ref, wg_ref, wu_ref, g_ref, u_ref, act_ref):
        h, _ = _rms(x_ref[...], gam_ref[...])
        hb = h.astype(BF16)
        g = _dot(hb, wg_ref[...].reshape(D_FF, D), 1, 1)
        u = _dot(hb, wu_ref[...].reshape(D_FF, D), 1, 1)
        g_ref[...] = g.astype(BF16)
        u_ref[...] = u.astype(BF16)
        act_ref[...] = (g * jax.nn.sigmoid(g) * u).astype(BF16)

    row = pl.BlockSpec((TB, D), lambda i: (i, 0))
    wide = pl.BlockSpec((TB, D_FF), lambda i: (i, 0))
    return _call(body, name, (t // TB,), [row, _const_spec((1, D)), _wspec(N_FF, OFF_GATE), _wspec(N_FF, OFF_UP)],
                 [wide, wide, wide], [jax.ShapeDtypeStruct((t, D_FF), BF16)] * 3, (x, gamma, w_gu, w_gu), (), comm)


def _ffn_down(x, act, w_d, name, comm=None):
    t = x.shape[0]

    def body(x_ref, a_ref, wd_ref, y_ref):
        y_ref[...] = x_ref[...] + _dot(a_ref[...], wd_ref[...].reshape(D_FF, D), 1, 0)

    row = pl.BlockSpec((TB, D), lambda i: (i, 0))
    wide = pl.BlockSpec((TB, D_FF), lambda i: (i, 0))
    return _call(body, name, (t // TB,), [row, wide, _wspec(N_FF, 0)], [row], [jax.ShapeDtypeStruct((t, D), F32)],
                 (x, act, w_d), (), comm)


def _ffn_bwd(x, g, u, dy, w_gu, w_d, gamma, name, comm=None):
    t = x.shape[0]

    def body(x_ref, g_ref, u_ref, dy_ref, gam_ref, wg_ref, wu_ref, wd_ref,
             dx_ref, act_ref, dg_ref, du_ref, h_ref, dgam_ref):
        @pl.when(pl.program_id(0) == 0)
        def _():
            dgam_ref[...] = jnp.zeros_like(dgam_ref)

        xv = x_ref[...]
        gam = gam_ref[...]
        h, r = _rms(xv, gam)
        h_ref[...] = h.astype(BF16)
        dyv = dy_ref[...]
        dact = _dot(dyv.astype(BF16), wd_ref[...].reshape(D_FF, D), 1, 1)
        gv = g_ref[...].astype(F32)
        uv = u_ref[...].astype(F32)
        sg = jax.nn.sigmoid(gv)
        silu = gv * sg
        act_ref[...] = (silu * uv).astype(BF16)
        dgb = (dact * uv * (sg * (1.0 + gv * (1.0 - sg)))).astype(BF16)
        dub = (dact * silu).astype(BF16)
        dg_ref[...] = dgb
        du_ref[...] = dub
        dh = _dot(dgb, wg_ref[...].reshape(D_FF, D), 1, 0) + _dot(dub, wu_ref[...].reshape(D_FF, D), 1, 0)
        dxn, dgam = _rms_bwd(xv, r, gam, dh)
        dgam_ref[...] += dgam
        dx_ref[...] = dyv + dxn

    row = pl.BlockSpec((TB_FFN_BWD, D), lambda i: (i, 0))
    wide = pl.BlockSpec((TB_FFN_BWD, D_FF), lambda i: (i, 0))
    return _call(
        body, name, (t // TB_FFN_BWD,),
        [row, wide, wide, row, _const_spec((1, D)), _wspec(N_FF, OFF_GATE), _wspec(N_FF, OFF_UP),
         _wspec(N_FF, 0)],
        [row, wide, wide, wide, row, _const_spec((1, D))],
        [jax.ShapeDtypeStruct((t, D), F32), jax.ShapeDtypeStruct((t, D_FF), BF16),
         jax.ShapeDtypeStruct((t, D_FF), BF16), jax.ShapeDtypeStruct((t, D_FF), BF16),
         jax.ShapeDtypeStruct((t, D), BF16), jax.ShapeDtypeStruct((1, D), F32)],
        (x, g, u, dy, gamma, w_gu, w_gu, w_d), (), comm)


def _odd_pre_fwd(x, wg, gamma, qbt, kvbt, qa_g, kva_g, pw_bd, pscale, seq, comm=None):
    t = x.shape[0]
    nbs = seq // TB

    def body(x_ref, gam_ref, win_ref, qb_ref, kvb_ref, qa_ref, kva_ref, pw_ref, ps_ref,
             proj_ref, q_ref, kv_ref, kr_ref, c_ref, tail_ref):
        i = pl.program_id(0)
        h, _ = _rms(x_ref[...], gam_ref[...])
        proj = _dot(h.astype(BF16), win_ref[...].reshape(D, D), 1, 0)
        proj_ref[...] = proj.astype(BF16)
        zp, ql, kvl = proj[:, :POOL_W], proj[:, 256:640], proj[:, 640:896]
        kr_ref[...] = proj[:, 896:1024]
        qn, _ = _rms(ql, qa_ref[...])
        q_ref[...] = _dot(qn.astype(BF16), qb_ref[...], 1, 1).astype(BF16)
        kvn, _ = _rms(kvl, kva_ref[...])
        kv_ref[...] = _dot(kvn.astype(BF16), kvb_ref[...], 1, 1).astype(BF16)
        tail = jnp.where(i % nbs == 0, 0.0, tail_ref[...])
        pooled, _, _ = _pool_fwd(zp, tail, i % nbs)
        tail_ref[...] = zp[TB - HALO:, :]
        c_ref[...] = (_dot(pooled.astype(BF16), pw_ref[...], 1, 0) * ps_ref[...]).astype(BF16)

    def row(w):
        return pl.BlockSpec((TB, w), lambda i: (i, 0))

    return _call(
        body, "odd_pre_fwd", (t // TB,),
        [row(D), _const_spec((1, D)), _wspec(N_SQ, OFF_OIN), _const_spec((HEADS * HP, Q_LORA)),
         _const_spec((HEADS * HP, KV_LORA)), _const_spec((1, Q_LORA)), _const_spec((1, KV_LORA)),
         _const_spec((POOL_W, POOL_W)), _const_spec((1, POOL_W))],
        [row(D), row(HEADS * HP), row(HEADS * HP), row(128), row(POOL_W)],
        [jax.ShapeDtypeStruct((t, D), BF16), jax.ShapeDtypeStruct((t, HEADS * HP), BF16),
         jax.ShapeDtypeStruct((t, HEADS * HP), BF16), jax.ShapeDtypeStruct((t, 128), F32),
         jax.ShapeDtypeStruct((t, POOL_W), BF16)],
        (x, gamma, wg, qbt, kvbt, qa_g, kva_g, pw_bd, pscale), [pltpu.VMEM((HALO, POOL_W), F32)], comm)


def _odd_pre_bwd(x, proj, dx3, dmix, dq, dkv, dkr, wg, gamma, qbt, kvbt, qa_g, kva_g, pw_bd, pscale, seq):
    t = x.shape[0]
    nb, nbs = t // TB, seq // TB

    def body(x_ref, proj_ref, ptail_ref, dx3_ref, dco_ref, dq_ref, dkv_ref, dkr_ref, gam_ref, win_ref, qb_ref,
             kvb_ref, qa_ref, kva_ref, pw_ref, ps_ref,
             dx2_ref, dproj_ref, h_ref, qn_ref, kvn_ref, dgam_ref, dqa_ref, dkva_ref, dpw_ref, dps_ref, head_ref):
        i = pl.program_id(0)
        blk = nb - 1 - i

        @pl.when(i == 0)
        def _():
            dgam_ref[...] = jnp.zeros_like(dgam_ref)
            dqa_ref[...] = jnp.zeros_like(dqa_ref)
            dkva_ref[...] = jnp.zeros_like(dkva_ref)
            dpw_ref[...] = jnp.zeros_like(dpw_ref)
            dps_ref[...] = jnp.zeros_like(dps_ref)

        xv = x_ref[...]
        gam = gam_ref[...]
        h, r = _rms(xv, gam)
        h_ref[...] = h.astype(BF16)
        proj = proj_ref[...].astype(F32)
        zp, ql, kvl = proj[:, :POOL_W], proj[:, 256:640], proj[:, 640:896]

        qa = qa_ref[...]
        qn, rq = _rms(ql, qa)
        qn_ref[...] = qn.astype(BF16)
        dql, dqa = _rms_bwd(ql, rq, qa, _dot(dq_ref[...], qb_ref[...], 1, 0))
        dqa_ref[...] += dqa
        kva = kva_ref[...]
        kvn, rkv = _rms(kvl, kva)
        kvn_ref[...] = kvn.astype(BF16)
        dkvl, dkva = _rms_bwd(kvl, rkv, kva, _dot(dkv_ref[...], kvb_ref[...], 1, 0))
        dkva_ref[...] += dkva

        pt = ptail_ref[...].astype(F32)
        tail = jnp.where(blk % nbs == 0, 0.0, pt[:, :POOL_W])
        pooled, cnt, grp = _pool_fwd(zp, tail, blk % nbs)
        pb = pooled.astype(BF16)
        pw = pw_ref[...]
        dco = dco_ref[...].astype(F32)
        dps_ref[...] += jnp.sum(dco * _dot(pb, pw, 1, 0), axis=0, keepdims=True)
        dpo = (dco * ps_ref[...]).astype(BF16)
        dpw_ref[...] += _dot(pb, dpo, 0, 0)
        dpooled = _dot(dpo, pw, 1, 1)
        dpm = dpooled / cnt
        head = jnp.where(blk % nbs == nbs - 1, 0.0, head_ref[...])
        dz = _pool_bwd(dpooled, dpm, head, grp)
        head_ref[...] = dpm[:HALO, :]

        dproj_ref[:, :POOL_W] = dz.astype(BF16)
        dproj_ref[:, 256:640] = dql.astype(BF16)
        dproj_ref[:, 640:896] = dkvl.astype(BF16)
        dproj_ref[:, 896:1024] = dkr_ref[...].astype(BF16)
        dh = _dot(dproj_ref[...], win_ref[...].reshape(D, D), 1, 1)
        dxn, dgam = _rms_bwd(xv, r, gam, dh)
        dgam_ref[...] += dgam
        dx2_ref[...] = dx3_ref[...] + dxn

    def rev(w):
        return pl.BlockSpec((TB, w), lambda i: (nb - 1 - i, 0))

    ptail = pl.BlockSpec((HALO, D), lambda i: (jnp.maximum((nb - 1 - i) * (TB // HALO) - 1, 0), 0))
    return pl.pallas_call(
        body, name="odd_pre_bwd",
        out_shape=[jax.ShapeDtypeStruct((t, D), F32), jax.ShapeDtypeStruct((t, D), BF16),
                   jax.ShapeDtypeStruct((t, D), BF16), jax.ShapeDtypeStruct((t, Q_LORA), BF16),
                   jax.ShapeDtypeStruct((t, KV_LORA), BF16), jax.ShapeDtypeStruct((1, D), F32),
                   jax.ShapeDtypeStruct((1, Q_LORA), F32), jax.ShapeDtypeStruct((1, KV_LORA), F32),
                   jax.ShapeDtypeStruct((POOL_W, POOL_W), F32), jax.ShapeDtypeStruct((1, POOL_W), F32)],
        grid=(nb,),
        in_specs=[rev(D), rev(D), ptail, rev(D), rev(POOL_W), rev(HEADS * HP), rev(HEADS * HP), rev(128),
                  _const_spec((1, D)), _wspec(N_SQ, OFF_OIN), _const_spec((HEADS * HP, Q_LORA)),
                  _const_spec((HEADS * HP, KV_LORA)), _const_spec((1, Q_LORA)), _const_spec((1, KV_LORA)),
                  _const_spec((POOL_W, POOL_W)), _const_spec((1, POOL_W))],
        out_specs=[rev(D), rev(D), rev(D), rev(Q_LORA), rev(KV_LORA), _const_spec((1, D)), _const_spec((1, Q_LORA)),
                   _const_spec((1, KV_LORA)), _const_spec((POOL_W, POOL_W)), _const_spec((1, POOL_W))],
        scratch_shapes=[pltpu.VMEM((HALO, POOL_W), F32)],
        compiler_params=_cparams(1),
    )(x, proj, proj, dx3, dmix, dq, dkv, dkr, gamma, wg, qbt, kvbt, qa_g, kva_g, pw_bd, pscale)


def _attn_specs(seq):
    head = pl.BlockSpec((seq, HP), lambda b, h: (b, h))
    shared = pl.BlockSpec((seq, 128), lambda b, h: (b, 0))
    gain = pl.BlockSpec((1, HP), lambda b, h: (0, 0))
    return head, shared, gain


def _causal_bias(n):
    rows = lax.broadcasted_iota(jnp.int32, (n, n), 0)
    cols = lax.broadcasted_iota(jnp.int32, (n, n), 1)
    return jnp.where(cols <= rows, 0.0, NEG_INF)


def _attn_fwd(q, kv, kr, cos, sin, gq, gk, seq, comm=None):
    t = q.shape[0]
    qb = min(512, seq)

    def body(q_ref, kv_ref, kr_ref, c_ref, s_ref, gq_ref, gk_ref, o_ref, lse_ref):
        c, s = c_ref[...], s_ref[...]
        qf, _ = _qk_prep(q_ref[...].astype(F32), gq_ref[...], c, s)
        kin = jnp.concatenate([kv_ref[:, :128].astype(F32), kr_ref[...]], axis=1)
        kf, _ = _qk_prep(kin, gk_ref[...], c, s)
        qf, kf = qf.astype(BF16), kf.astype(BF16)
        v1 = jnp.concatenate([kv_ref[:, 128:], jnp.ones((seq, V_DIM), BF16)], axis=1)
        bias = _causal_bias(qb)
        for q0 in range(0, seq, qb):
            q1 = q0 + qb
            qblk = qf[q0:q1]
            s_dg = _dot(qblk, kf[q0:q1], 1, 1) + bias
            m = jnp.max(s_dg, axis=-1, keepdims=True)
            if q0:
                s_off = _dot(qblk, kf[:q0], 1, 1)
                m = jnp.maximum(m, jnp.max(s_off, axis=-1, keepdims=True))
            acc = _dot(jnp.exp(s_dg - m).astype(BF16), v1[q0:q1], 1, 0)
            if q0:
                acc = acc + _dot(jnp.exp(s_off - m).astype(BF16), v1[:q0], 1, 0)
            l = acc[:, V_DIM:]
            o_ref[q0:q1, :] = (acc[:, :V_DIM] / l).astype(BF16)
            lse_ref[q0:q1, :] = m + jnp.log(l)

    head, shared, gain = _attn_specs(seq)
    per_head = pl.BlockSpec((seq, V_DIM), lambda b, h: (b, h))
    return _call(
        body, "attn_fwd", (t // seq, HEADS),
        [head, head, shared, shared, shared, gain, gain], [per_head, per_head],
        [jax.ShapeDtypeStruct((t, HEADS * V_DIM), BF16), jax.ShapeDtypeStruct((t, HEADS * V_DIM), F32)],
        (q, kv, kr, cos, sin, gq, gk), (), comm)


def _attn_bwd(q, kv, kr, cos, sin, gq, gk, dmix, d_out, lse, seq, comm=None):
    t = q.shape[0]
    qb = min(512, seq)

    def body(q_ref, kv_ref, kr_ref, c_ref, s_ref, gq_ref, gk_ref, do_ref, o_ref, lse_ref,
             dq_ref, dkv_ref, dkr_ref, dgq_ref, dgk_ref, dqf_ref, dkf_ref, dv_ref):
        b, hd = pl.program_id(0), pl.program_id(1)

        @pl.when((b == 0) & (hd == 0))
        def _():
            dgq_ref[...] = jnp.zeros_like(dgq_ref)
            dgk_ref[...] = jnp.zeros_like(dgk_ref)

        c, sn = c_ref[...], s_ref[...]
        gq_v, gk_v = gq_ref[...], gk_ref[...]
        qin = q_ref[...].astype(F32)
        kin = jnp.concatenate([kv_ref[:, :128].astype(F32), kr_ref[...]], axis=1)
        qf32, rq = _qk_prep(qin, gq_v, c, sn)
        kf32, rk = _qk_prep(kin, gk_v, c, sn)
        qf, kf = qf32.astype(BF16), kf32.astype(BF16)
        vb = kv_ref[:, 128:]
        dkf_ref[...] = jnp.zeros_like(dkf_ref)
        dv_ref[...] = jnp.zeros_like(dv_ref)
        bias = _causal_bias(qb)
        for q0 in range(0, seq, qb):
            q1 = q0 + qb
            qblk = qf[q0:q1]
            do = do_ref[q0:q1, :]
            lse_col = lse_ref[q0:q1, 0:1]
            d_col = jnp.sum(do.astype(F32) * o_ref[q0:q1, :].astype(F32), axis=-1, keepdims=True)
            dq_acc = None
            for k0, k1, diag in ((q0, q1, True), (0, q0, False)):
                if k1 == k0:
                    continue
                s = _dot(qblk, kf[k0:k1], 1, 1)
                p = jnp.exp((s + bias if diag else s) - lse_col)
                dv_ref[k0:k1, :] += _dot(p.astype(BF16), do, 0, 0)
                ds = (p * (_dot(do, vb[k0:k1], 1, 1) - d_col)).astype(BF16)
                part = _dot(ds, kf[k0:k1], 1, 0)
                dq_acc = part if dq_acc is None else dq_acc + part
                dkf_ref[k0:k1, :] += _dot(ds, qblk, 0, 0)
            dqf_ref[q0:q1, :] = dq_acc
        dqin, dgq = _qk_prep_bwd(dqf_ref[...], qin, rq, gq_v, c, sn)
        dkin, dgk = _qk_prep_bwd(dkf_ref[...], kin, rk, gk_v, c, sn)
        dgq_ref[...] += dgq
        dgk_ref[...] += dgk
        dq_ref[...] = dqin.astype(BF16)
        dkv_ref[:, :128] = dkin[:, :128].astype(BF16)
        dkv_ref[:, 128:] = dv_ref[...].astype(BF16)

        @pl.when(hd == 0)
        def _():
            dkr_ref[...] = dkin[:, 128:]

        @pl.when(hd != 0)
        def _():
            dkr_ref[...] += dkin[:, 128:]

    head, shared, gain = _attn_specs(seq)
    per_head = pl.BlockSpec((seq, V_DIM), lambda b, h: (b, h))
    return _call(
        body, "attn_bwd", (t // seq, HEADS),
        [head, head, shared, shared, shared, gain, gain,
         pl.BlockSpec((seq, V_DIM), lambda b, h: (b, 2 + h)), per_head, per_head],
        [head, head, shared, gain, gain],
        [jax.ShapeDtypeStruct((t, HEADS * HP), BF16), jax.ShapeDtypeStruct((t, HEADS * HP), BF16),
         jax.ShapeDtypeStruct((t, 128), F32), jax.ShapeDtypeStruct((1, HP), F32),
         jax.ShapeDtypeStruct((1, HP), F32)],
        (q, kv, kr, cos, sin, gq, gk, dmix, d_out, lse),
        [pltpu.VMEM((seq, HP), F32), pltpu.VMEM((seq, HP), F32), pltpu.VMEM((seq, V_DIM), F32)], comm)


def _odd_post_fwd(x, c_out, d_out, wg):
    t = x.shape[0]

    def body(x_ref, c_ref, d_ref, w_ref, y_ref):
        y_ref[...] = (x_ref[...] + _dot(c_ref[...], w_ref[0:2].reshape(POOL_W, D), 1, 0)
                      + _dot(d_ref[...], w_ref[2:8].reshape(HEADS * V_DIM, D), 1, 0))

    def row(w):
        return pl.BlockSpec((TB, w), lambda i: (i, 0))

    return pl.pallas_call(
        body, name="odd_post_fwd", out_shape=jax.ShapeDtypeStruct((t, D), F32), grid=(t // TB,),
        in_specs=[row(D), row(POOL_W), row(HEADS * V_DIM), _wspec(N_SQ, OFF_OOUT)], out_specs=row(D),
        compiler_params=_cparams(1),
    )(x, c_out, d_out, wg)


def _odd_post_bwd(dx3, wg, comm=None):
    t = dx3.shape[0]

    def body(d_ref, w_ref, o_ref):
        o_ref[...] = _dot(d_ref[...].astype(BF16), w_ref[...].reshape(D, D), 1, 1).astype(BF16)

    row = pl.BlockSpec((TB, D), lambda i: (i, 0))
    (res,), extra = _call(body, "odd_post_bwd", (t // TB,), [row, _wspec(N_SQ, OFF_OOUT)], [row],
                          [jax.ShapeDtypeStruct((t, D), BF16)], (dx3, wg), (), comm)
    return res, extra


def _tn(a_list, b, tm, name, into=None, comm=None):
    t, n_out = b.shape
    widths = [a.shape[1] for a in a_list]
    tk = min(TK_DW, t)
    m, na, nk = sum(widths), len(a_list), t // tk
    assert na == 1 or tm == m

    def body(*refs):
        a_refs, b_ref, o_ref, acc_ref = refs[:na], refs[na], refs[-2], refs[-1]
        k = pl.program_id(1)

        @pl.when(k == 0)
        def _():
            acc_ref[...] = jnp.zeros_like(acc_ref)

        bb = b_ref[...].astype(BF16)
        m0 = 0
        for a_ref, w in zip(a_refs, widths):
            rows = slice(0, tm) if na == 1 else slice(m0, m0 + w)
            acc_ref[rows, :] += _dot(a_ref[...].astype(BF16), bb, 0, 0)
            m0 += w

        @pl.when(k == nk - 1)
        def _():
            o_ref[...] = acc_ref[...].astype(BF16).reshape(o_ref.shape)

    if na == 1:
        in_specs = [pl.BlockSpec((tk, tm), lambda i, k: (k, i))]
    else:
        in_specs = [pl.BlockSpec((tk, w), lambda i, k: (k, 0)) for w in widths]
    in_specs.append(pl.BlockSpec((tk, n_out), lambda i, k: (k, 0)))
    args = list(a_list) + [b]
    if into is None:
        out_spec = pl.BlockSpec((tm, n_out), lambda i, k: (i, 0))
        out_shape = jax.ShapeDtypeStruct((m, n_out), BF16)
        aliases = {}
    else:
        buf, n, off = into
        assert n_out == D and tm % n == 0 and off % n == 0 and (na == 1 or tm // n == N_DEV)
        idx = off // n
        out_spec = pl.BlockSpec((tm // n, n, D), lambda i, k: (i, idx, 0))
        out_shape = jax.ShapeDtypeStruct(buf.shape, BF16)
        in_specs.append(pl.BlockSpec(memory_space=pl.ANY))
        args.append(buf)
        aliases = {len(args) - 1: 0}
    (res,), extra = _call(body, name, (m // tm, nk), in_specs, [out_spec], [out_shape], args,
                          [pltpu.VMEM((tm, n_out), F32)], comm, aliases)
    return (res, extra) if comm is not None else res


def _adamw(ws, gs, ms, vs, name, nblk=1):
    n = len(ws)
    c1 = 1.0 - B1 ** STEP
    c2 = 1.0 - B2 ** STEP

    def body(*refs):
        for a in range(n):
            w, g, m, v = (refs[k * n + a][...] for k in range(4))
            d_ref, m_ref, v_ref = (refs[(4 + k) * n + a] for k in range(3))
            m_new = B1 * m + (1.0 - B1) * g
            v_new = B2 * v + (1.0 - B2) * (g * g)
            d_ref[...] = -LR * ((m_new / c1) / (jnp.sqrt(v_new / c2) + ADAM_EPS) + WD * w)
            m_ref[...] = m_new
            v_ref[...] = v_new

    grid = (nblk,)
    assert all(w.shape[0] % nblk == 0 and (nblk == 1 or (w.shape[0] // nblk) % 8 == 0) for w in ws)
    specs = [pl.BlockSpec((w.shape[0] // nblk, w.shape[1]), lambda i: (i, 0)) for w in ws]
    outs, _ = _call(body, name, grid, specs * 4, specs * 3, [jax.ShapeDtypeStruct(w.shape, F32) for w in ws] * 3,
                    (*ws, *gs, *ms, *vs))
    return outs[:n], outs[n:2 * n], outs[2 * n:]


def _rows1024(a, rows):
    flat = a.reshape(-1, D)
    return jnp.pad(flat, ((0, rows - flat.shape[0]), (0, 0)))


def _pack_shards(even_w_in, even_w_out, odd_w_in, q_b, kv_b, odd_w_out, ffn_w_gate, ffn_w_up, ffn_w_down):
    mix0 = jnp.concatenate([even_w_in[0].T, jnp.zeros((OFF_EOUT - N_EIN, D), F32), even_w_out[0]], axis=0)
    gu = [jnp.concatenate([ffn_w_gate[layer].T, ffn_w_up[layer].T], axis=0) for layer in range(2)]
    mix1 = jnp.concatenate([jnp.pad(odd_w_in[0], ((0, 0), (0, D - ODD_IN))), odd_w_out[0],
                            _rows1024(q_b[0].T, N_QB), _rows1024(kv_b[0].T, N_KVB),
                            jnp.zeros((R_MIX1 - OFF_KVB - N_KVB, D), F32)], axis=0)
    return [c.astype(BF16) for c in (mix0, gu[0], ffn_w_down[0], mix1, gu[1], ffn_w_down[1])]


def _pad_heads(a):
    k = a.shape[1]
    return jnp.pad(a.reshape(HEADS, QK_DIM, k), ((0, 0), (0, HP - QK_DIM), (0, 0))).reshape(HEADS * HP, k)


def _small_pack(parts):
    flat = []
    for p in parts:
        v = p.reshape(-1)
        flat.append(jnp.pad(v, (0, (-v.shape[0]) % 1024)))
    return jnp.concatenate(flat).reshape(-1, 128)


def _small_unpack(buf, shapes):
    flat = buf.reshape(-1)
    out, off = [], 0
    for s in shapes:
        size = int(np.prod(s))
        out.append(flat[off:off + size].reshape(s))
        off += size + (-size) % 1024
    return out


def _step(x3d, positions, target3d, chunks, tile, where, mix_norm, ffn_norm, sg_ln_g, sg_w_s, sg_b_s,
          pool_w, q_norm, k_norm):
    bsz, seq, _ = x3d.shape
    t = bsz * seq
    x0 = x3d.reshape(t, D)
    target = target3d.reshape(t, D)
    my_mix0, my_gu0, my_d0, my_mix1, my_gu1, my_d1 = chunks

    lane = np.arange(128)
    inv_freq = np.where(lane < QK_ROPE, ROPE_THETA ** (-(2.0 * (lane % 32)) / QK_ROPE), 0.0)
    inv_freq = jnp.asarray(inv_freq.reshape(1, 128), F32)
    (cos, sin), (w_mix0, tiles) = _rope_tables(positions.reshape(t, 1), inv_freq, _gather_comm([my_mix0, tile]))

    conv_w = tiles[:, 0:3, 0:64].transpose(1, 0, 2).reshape(3, SC_W)
    pool_scale = tiles[:, 3, 0:32].reshape(1, POOL_W)
    q_a_norm = tiles[:, 4, 0:48].reshape(1, Q_LORA)
    kv_a_norm = tiles[:, 5, 0:32].reshape(1, KV_LORA)
    ws = sg_w_s[0]
    bst = jnp.pad(sg_b_s[0].T, ((0, 0), (0, 128 - SG_HEADS)))
    cw = jnp.pad(conv_w, ((0, 8 - 3), (0, 0)))
    pw_bd = jax.scipy.linalg.block_diag(*[pool_w[0, g] for g in range(4)]).astype(BF16)
    gq = jnp.pad(q_norm * ATT_SCALE, ((0, 0), (0, HP - QK_DIM)))
    gk = jnp.pad(k_norm, ((0, 0), (0, HP - QK_DIM)))

    (x1, proj_e), (w_gu0,) = _even_fwd(x0, w_mix0, mix_norm[0:1], sg_ln_g, ws, bst, cw, seq, _gather_comm([my_gu0]))
    (g0, u0, act0), (w_d0, w_mix1) = _ffn_up(x1, w_gu0, ffn_norm[0:1], "ffn_up0", _gather_comm([my_d0, my_mix1]))
    (x2,), (w_d1,) = _ffn_down(x1, act0, w_d0, "ffn_down0", _gather_comm([my_d1]))
    qbt = _pad_heads(w_mix1[:, OFF_QB:OFF_QB + N_QB_USED, :].reshape(HEADS * QK_DIM, Q_LORA))
    kvbt = w_mix1[:, OFF_KVB:OFF_KVB + N_KVB, :].reshape(HEADS * HP, KV_LORA)
    (proj_o, q, kv, kr, c_out), _ = _odd_pre_fwd(x2, w_mix1, mix_norm[1:2], qbt, kvbt, q_a_norm, kv_a_norm,
                                                pw_bd, pool_scale, seq)
    (d_out, lse), (w_gu1,) = _attn_fwd(q, kv, kr, cos, sin, gq, gk, seq, _gather_comm([my_gu1]))
    x3 = _odd_post_fwd(x2, c_out, d_out, w_mix1)
    (dy, g1, u1, loss_tile), _ = _ffn_fwd(x3, w_gu1, w_d1, ffn_norm[1:2], "ffn_fwd1", None, target)

    def chunk(rows, padded=False):
        return jnp.zeros((N_DEV, rows, D), BF16) if padded else lax.empty((N_DEV, rows, D), BF16)

    (dx3, act1, dg1, du1, h3, dgam_f1), _ = _ffn_bwd(x3, g1, u1, dy, w_gu1, w_d1, ffn_norm[1:2], "ffn_bwd1")
    gp_ffn1 = _tn([dg1], h3, 1408, "dw_gate1", (chunk(R_GU + N_FF), N_FF, OFF_GATE))
    gp_ffn1 = _tn([du1], h3, 1408, "dw_up1", (gp_ffn1, N_FF, OFF_UP))
    gp_ffn1 = _tn([act1], dy, 1408, "dw_down1", (gp_ffn1, N_FF, R_GU))

    dmix_o, (ga_ffn1,) = _odd_post_bwd(dx3, w_mix1, _pair_exchange_comm(gp_ffn1))
    pb_ffn1 = _rs_pair_sum(gp_ffn1, ga_ffn1, where, "rs_pair_sum_ffn1")
    gp_mix1 = _tn([c_out, d_out], dx3, D, "dw_oout", (chunk(R_MIX1, True), N_SQ, OFF_OOUT))
    (dq, dkv, dkr, dgq, dgk), (gb_ffn1,) = _attn_bwd(q, kv, kr, cos, sin, gq, gk, dmix_o, d_out, lse, seq,
                                                    _chip_exchange_comm(pb_ffn1))
    (dx2, dproj_o, h2, qn, kvn, dgam_m1, dqa, dkva, dpw_bd, dps) = _odd_pre_bwd(
        x2, proj_o, dx3, dmix_o, dq, dkv, dkr, w_mix1, mix_norm[1:2], qbt, kvbt, q_a_norm, kv_a_norm, pw_bd,
        pool_scale, seq)
    gp_mix1 = _tn([h2], dproj_o, D, "dw_oin", (gp_mix1, N_SQ, OFF_OIN))
    d_qbt = _tn([dq], qn, HEADS * HP, "dw_qb")
    d_qb_rows = d_qbt.reshape(HEADS, HP, Q_LORA)[:, :QK_DIM].reshape(N_DEV, N_QB_USED, D)
    d_kvb_rows = _tn([dkv], kvn, HEADS * HP, "dw_kvb").reshape(N_DEV, N_KVB, D)
    gp_mix1 = lax.dynamic_update_slice(gp_mix1, d_qb_rows, (0, OFF_QB, 0))
    gp_mix1 = lax.dynamic_update_slice(gp_mix1, d_kvb_rows, (0, OFF_KVB, 0))

    (dx1, act0, dg0, du0, h1, dgam_f0), (ga_mix1,) = _ffn_bwd(x1, g0, u0, dx2, w_gu0, w_d0, ffn_norm[0:1], "ffn_bwd0",
                                                             _pair_exchange_comm(gp_mix1))
    pb_mix1 = _rs_pair_sum(gp_mix1, ga_mix1, where, "rs_pair_sum_mix1")
    gp_ffn0a, (gb_mix1,) = _tn([dg0], h1, 1408, "dw_gate0", (chunk(R_GU), N_FF, OFF_GATE),
                               _chip_exchange_comm(pb_mix1))
    gp_ffn0a = _tn([du0], h1, 1408, "dw_up0", (gp_ffn0a, N_FF, OFF_UP))
    gp_ffn0b, (ga_ffn0a,) = _tn([act0], dx2, 1408, "dw_down0", (chunk(N_FF), N_FF, 0),
                                _pair_exchange_comm(gp_ffn0a))
    pb_ffn0a = _rs_pair_sum(gp_ffn0a, ga_ffn0a, where, "rs_pair_sum_ffn0a")

    (dx0, dproj_e, mix_e, h0, dgam_m0, dws, dbc, dlng, dcw), (gb_ffn0a, ga_ffn0b) = _even_bwd(
        x0, proj_e, dx1, w_mix0, mix_norm[0:1], sg_ln_g, ws, bst, cw, seq,
        _both(_chip_exchange_comm(pb_ffn0a), _pair_exchange_comm(gp_ffn0b)))
    pb_ffn0b = _rs_pair_sum(gp_ffn0b, ga_ffn0b, where, "rs_pair_sum_ffn0b")

    small = _small_pack([
        jnp.concatenate([dgam_m0, dgam_m1], 0), jnp.concatenate([dgam_f0, dgam_f1], 0), dlng,
        dws[None], dbc[:, :SG_HEADS].T[None], dcw[:3],
        jnp.stack([dpw_bd[g * POOL_GD:(g + 1) * POOL_GD, g * POOL_GD:(g + 1) * POOL_GD] for g in range(4)])[None],
        dps, dqa, dkva, dgq[:, :QK_DIM] * ATT_SCALE, dgk[:, :QK_DIM], loss_tile[0:1, 0:1]])
    gp_mix0, (small_all,) = _tn([mix_e], dx1, D, "dw_eout", (chunk(R_MIX0, True), N_SQ, OFF_EOUT),
                                _gather_comm([small]))
    gp_mix0, (gb_ffn0b,) = _tn([dproj_e], h0, 1280, "dw_ein", (gp_mix0, N_EIN, OFF_EIN),
                               _chip_exchange_comm(pb_ffn0b))
    small_sum = _small_unpack(_sum_gathered(small_all), SMALL_SHAPES)
    partials = ([pb_ffn0a, pb_ffn0b, pb_mix1, pb_ffn1], [gb_ffn0a, gb_ffn0b, gb_mix1, gb_ffn1])
    return dx0.reshape(bsz, seq, D), partials, gp_mix0, small_sum


SMALL_SHAPES = [(2, D), (2, D), (1, SG_W), (1, SG_HEADS, 128, 128), (1, SG_HEADS, 128), (3, SC_W),
                (1, 4, POOL_GD, POOL_GD), (1, POOL_W), (1, Q_LORA), (1, KV_LORA), (1, QK_DIM), (1, QK_DIM), (1, 1)]


def kernel(x, positions, mix_norm, ffn_norm, even_w_in, sg_ln_g, sg_w_s, sg_b_s, sc_conv_w, even_w_out, odd_w_in, pool_w, pool_scale, q_a_norm, q_b, kv_a_norm, kv_b, q_norm, k_norm, odd_w_out, ffn_w_gate, ffn_w_up, ffn_w_down, loss_target, m_mix_norm, m_ffn_norm, m_even_w_in, m_sg_ln_g, m_sg_w_s, m_sg_b_s, m_sc_conv_w, m_even_w_out, m_odd_w_in, m_pool_w, m_pool_scale, m_q_a_norm, m_q_b, m_kv_a_norm, m_kv_b, m_q_norm, m_k_norm, m_odd_w_out, m_ffn_w_gate, m_ffn_w_up, m_ffn_w_down, v_mix_norm, v_ffn_norm, v_even_w_in, v_sg_ln_g, v_sg_w_s, v_sg_b_s, v_sc_conv_w, v_even_w_out, v_odd_w_in, v_pool_w, v_pool_scale, v_q_a_norm, v_q_b, v_kv_a_norm, v_kv_b, v_q_norm, v_k_norm, v_odd_w_out, v_ffn_w_gate, v_ffn_w_up, v_ffn_w_down):
    xi, yi, ci = _place()
    me = 4 * xi + 2 * yi + ci

    chunks = _pack_shards(even_w_in, even_w_out, odd_w_in, q_b, kv_b, odd_w_out, ffn_w_gate, ffn_w_up, ffn_w_down)

    def lane_pad(a):
        return jnp.pad(a, ((0, 0), (0, 128 - a.shape[1])))

    tile = jnp.concatenate([lane_pad(sc_conv_w[0]), lane_pad(pool_scale), lane_pad(q_a_norm), lane_pad(kv_a_norm),
                            jnp.zeros((2, 128), F32)], axis=0)
    chip = 2 * xi + yi
    where = jnp.stack([ci, chip, chip ^ 2, chip ^ 1, chip ^ 3]).astype(jnp.int32)
    grad_x, (pbs, gbs), gp_mix0, tot = _step(
        x, positions, loss_target, chunks, tile, where, mix_norm, ffn_norm, sg_ln_g, sg_w_s, sg_b_s,
        pool_w, q_norm, k_norm)

    (ga_mix0,) = _comm_alone(_pair_exchange_comm(gp_mix0), "rs_pair_exchange_mix0")
    pb_mix0 = _rs_pair_sum(gp_mix0, ga_mix0, where, "rs_pair_sum_mix0")
    mix0_sems, pb_mix0, land_mix0, started = _chip_exchange_start(pb_mix0)
    gsh_ffn0a, gsh_ffn0b, gsh_mix1, gsh_ffn1 = _rs_final_sums(pbs, gbs, "rs_final_sums", started)

    (g_mix, g_ffn, g_lng, g_ws, g_bs, g_cw_full, g_pw, g_ps_full, g_qa_full, g_kva_full, g_qn, g_kn, loss) = tot
    g_cw = lax.dynamic_slice_in_dim(g_cw_full, me * 64, 64, axis=1)[None]
    g_ps = lax.dynamic_slice_in_dim(g_ps_full, me * 32, 32, axis=1)
    g_qa = lax.dynamic_slice_in_dim(g_qa_full, me * 48, 48, axis=1)
    g_kva = lax.dynamic_slice_in_dim(g_kva_full, me * 32, 32, axis=1)

    def tr(a):
        return jnp.swapaxes(a, -1, -2)

    g_gate = tr(jnp.stack([gsh_ffn0a[OFF_GATE:OFF_GATE + N_FF], gsh_ffn1[OFF_GATE:OFF_GATE + N_FF]]))
    g_up = tr(jnp.stack([gsh_ffn0a[OFF_UP:OFF_UP + N_FF], gsh_ffn1[OFF_UP:OFF_UP + N_FF]]))
    g_down = jnp.stack([gsh_ffn0b, gsh_ffn1[R_GU:R_GU + N_FF]])
    g_oin = gsh_mix1[OFF_OIN:OFF_OIN + N_SQ, :ODD_IN][None]
    g_oout = gsh_mix1[OFF_OOUT:OFF_OOUT + N_SQ][None]
    g_qb = tr(gsh_mix1[OFF_QB:OFF_QB + N_QB_USED].reshape(1, 144, Q_LORA))
    g_kvb = tr(gsh_mix1[OFF_KVB:OFF_KVB + N_KVB].reshape(1, 192, KV_LORA))
    transposed = ("even_w_in", "odd_w_in", "q_b", "kv_b", "ffn_w_gate", "ffn_w_up")

    names = ("mix_norm", "ffn_norm", "even_w_in", "sg_ln_g", "sg_w_s", "sg_b_s", "sc_conv_w", "even_w_out",
             "odd_w_in", "pool_w", "pool_scale", "q_a_norm", "q_b", "kv_a_norm", "kv_b", "q_norm", "k_norm",
             "odd_w_out", "ffn_w_gate", "ffn_w_up", "ffn_w_down")
    grads = dict(mix_norm=g_mix, ffn_norm=g_ffn, sg_ln_g=g_lng, sg_w_s=g_ws, sg_b_s=g_bs,
                 sc_conv_w=g_cw, odd_w_in=g_oin, pool_w=g_pw, pool_scale=g_ps, q_a_norm=g_qa,
                 q_b=g_qb, kv_a_norm=g_kva, kv_b=g_kvb, q_norm=g_qn, k_norm=g_kn, odd_w_out=g_oout,
                 ffn_w_gate=g_gate, ffn_w_up=g_up, ffn_w_down=g_down)
    weights = dict(mix_norm=mix_norm, ffn_norm=ffn_norm, even_w_in=even_w_in, sg_ln_g=sg_ln_g, sg_w_s=sg_w_s,
                   sg_b_s=sg_b_s, sc_conv_w=sc_conv_w, even_w_out=even_w_out, odd_w_in=odd_w_in, pool_w=pool_w,
                   pool_scale=pool_scale, q_a_norm=q_a_norm, q_b=q_b, kv_a_norm=kv_a_norm, kv_b=kv_b, q_norm=q_norm,
                   k_norm=k_norm, odd_w_out=odd_w_out, ffn_w_gate=ffn_w_gate, ffn_w_up=ffn_w_up,
                   ffn_w_down=ffn_w_down)
    m_in = dict(mix_norm=m_mix_norm, ffn_norm=m_ffn_norm, even_w_in=m_even_w_in, sg_ln_g=m_sg_ln_g, sg_w_s=m_sg_w_s,
                sg_b_s=m_sg_b_s, sc_conv_w=m_sc_conv_w, even_w_out=m_even_w_out, odd_w_in=m_odd_w_in,
                pool_w=m_pool_w, pool_scale=m_pool_scale, q_a_norm=m_q_a_norm, q_b=m_q_b, kv_a_norm=m_kv_a_norm,
                kv_b=m_kv_b, q_norm=m_q_norm, k_norm=m_k_norm, odd_w_out=m_odd_w_out, ffn_w_gate=m_ffn_w_gate,
                ffn_w_up=m_ffn_w_up, ffn_w_down=m_ffn_w_down)
    v_in = dict(mix_norm=v_mix_norm, ffn_norm=v_ffn_norm, even_w_in=v_even_w_in, sg_ln_g=v_sg_ln_g, sg_w_s=v_sg_w_s,
                sg_b_s=v_sg_b_s, sc_conv_w=v_sc_conv_w, even_w_out=v_even_w_out, odd_w_in=v_odd_w_in,
                pool_w=v_pool_w, pool_scale=v_pool_scale, q_a_norm=v_q_a_norm, q_b=v_q_b, kv_a_norm=v_kv_a_norm,
                kv_b=v_kv_b, q_norm=v_q_norm, k_norm=v_k_norm, odd_w_out=v_odd_w_out, ffn_w_gate=v_ffn_w_gate,
                ffn_w_up=v_ffn_w_up, ffn_w_down=v_ffn_w_down)
    delta, new_m, new_v = {}, {}, {}

    def as2d(k, a):
        a = tr(a) if k in transposed else a
        return a.reshape(-1, a.shape[-1])

    def back(k, a):
        shape = weights[k].shape
        return tr(a.reshape(shape[:-2] + (shape[-1], shape[-2]))) if k in transposed else a.reshape(shape)

    def update(group, name, nblk=1):
        outs = _adamw([as2d(k, weights[k]) for k in group], [as2d(k, grads[k]) for k in group],
                      [as2d(k, m_in[k]) for k in group], [as2d(k, v_in[k]) for k in group], name, nblk)
        for i, k in enumerate(group):
            delta[k], new_m[k], new_v[k] = (back(k, o[i]) for o in outs)

    update(["ffn_w_gate", "ffn_w_up", "ffn_w_down"], "adamw_ffn", 4)
    update(["odd_w_in", "odd_w_out"], "adamw_mix1", 2)
    update([k for k in names if k not in delta and k not in ("even_w_in", "even_w_out")], "adamw_small")

    pb_mix0, gb_mix0 = _chip_exchange_wait(mix0_sems, pb_mix0, land_mix0, new_v["k_norm"])
    (gsh_mix0,) = _rs_final_sums([pb_mix0], [gb_mix0], "rs_final_sum_mix0")
    grads["even_w_in"] = tr(gsh_mix0[OFF_EIN:OFF_EIN + N_EIN][None])
    grads["even_w_out"] = gsh_mix0[OFF_EOUT:OFF_EOUT + N_SQ][None]
    update(["even_w_in", "even_w_out"], "adamw_mix0", 2)

    return (loss.reshape(()), grad_x, *[grads[k] for k in names], *[delta[k] for k in names],
            *[new_m[k] for k in names], *[new_v[k] for k in names])
```

```python
import functools

import numpy as np
import jax
import jax.numpy as jnp
from jax import lax
from jax.experimental import pallas as pl
from jax.experimental.pallas import tpu as pltpu

F32 = jnp.float32
BF16 = jnp.bfloat16
MESH = pl.DeviceIdType.MESH

D = 1024
EPS = 1e-6
NEG_INF = -1e30
SG_HEADS, SG_HD, SG_W, SG_CHUNK = 4, 128, 512, 128
SC_W = 512
EVEN_IN = 2560
POOL_W = 256
POOL_GD = 64
Q_LORA, KV_LORA, QK_ROPE, QK_NOPE, V_DIM = 384, 256, 64, 128, 128
QK_DIM = QK_NOPE + QK_ROPE
HEADS = 6
HP = 256
ODD_IN = 960
D_FF = 2816
ROPE_THETA = 10000.0
ATT_SCALE = QK_DIM ** -0.5
LR, B1, B2, ADAM_EPS, WD, STEP = 0.001, 0.9, 0.999, 1e-08, 0.01, 10

N_DEV = 8
TB = 512
TB_FFN_BWD = 256
TK_DW = 1024
HALO = 16
VMEM_LIMIT = 56 * 1024 * 1024

N_EIN, N_FF, N_SQ = 320, 352, 128
OFF_EIN, OFF_EOUT, R_MIX0 = 0, 384, 512
OFF_GATE, OFF_UP, R_GU = 0, 352, 704
OFF_OIN, OFF_OOUT, OFF_QB, OFF_KVB, R_MIX1 = 0, 128, 256, 320, 384
N_QB, N_QB_USED, N_KVB = 64, 54, 48

INV_SQRT2 = 0.7071067811865476
INV_SQRT_2PI = 0.3989422804014327


def _dot(a, b, ca, cb):
    return lax.dot_general(a, b, (((ca,), (cb,)), ((), ())), preferred_element_type=F32)


def _cparams(n_axes=1):
    return pltpu.CompilerParams(dimension_semantics=("arbitrary",) * n_axes, vmem_limit_bytes=VMEM_LIMIT)


def _wspec(n, off, arity=1):
    assert off % n == 0
    idx = off // n
    if arity == 1:
        return pl.BlockSpec((N_DEV, n, D), lambda i: (0, idx, 0), pipeline_mode=pl.Buffered(1))
    return pl.BlockSpec((N_DEV, n, D), lambda i, j: (0, idx, 0), pipeline_mode=pl.Buffered(1))


def _const_spec(shape):
    zeros = (0,) * len(shape)
    return pl.BlockSpec(shape, lambda *_: zeros)


class _Comm:
    def __init__(self, ins, out_shapes, sems, start, wait, mid=None):
        self.ins, self.out_shapes, self.sems, self.start, self.wait, self.mid = ins, out_shapes, sems, start, wait, mid


def _both(c1, c2):
    def split(f1, f2):
        def run(ins, outs, sems):
            f1(ins[:len(c1.ins)], outs[:len(c1.out_shapes)], sems[:len(c1.sems)])
            f2(ins[len(c1.ins):], outs[len(c1.out_shapes):], sems[len(c1.sems):])
        return run

    assert c1.mid is None and c2.mid is None
    return _Comm(c1.ins + c2.ins, c1.out_shapes + c2.out_shapes, c1.sems + c2.sems,
                 split(c1.start, c2.start), split(c1.wait, c2.wait))


def _call(body, name, grid, in_specs, out_specs, out_shape, args, scratch_shapes=(), comm=None, aliases=None):
    n_axes = len(grid)
    aliases = aliases or {}
    if comm is None:
        res = pl.pallas_call(
            body, name=name, grid=grid, in_specs=list(in_specs), out_specs=list(out_specs),
            out_shape=list(out_shape), scratch_shapes=list(scratch_shapes), input_output_aliases=aliases,
            compiler_params=_cparams(n_axes))(*args)
        return list(res), []
    ni, no, ns = len(in_specs), len(out_specs), len(scratch_shapes)
    ci, co = len(comm.ins), len(comm.out_shapes)
    n_steps = int(np.prod(grid))

    def carrier(*refs):
        ins, cin = refs[:ni], refs[ni:ni + ci]
        outs, cout = refs[ni + ci:ni + ci + no], refs[ni + ci + no:ni + ci + no + co]
        scr, sems = refs[ni + ci + no + co:ni + ci + no + co + ns], refs[ni + ci + no + co + ns:]
        step = 0
        for a in range(n_axes):
            step = step * grid[a] + pl.program_id(a)

        @pl.when(step == 0)
        def _():
            comm.start(cin, cout, sems)

        body(*ins, *outs, *scr)

        if comm.mid is not None and n_steps >= 4:
            @pl.when(step == (3 * n_steps) // 4)
            def _():
                comm.mid(cin, cout, sems)

        @pl.when(step == n_steps - 1)
        def _():
            if comm.mid is not None and n_steps < 4:
                comm.mid(cin, cout, sems)
            comm.wait(cin, cout, sems)

    any_spec = pl.BlockSpec(memory_space=pl.ANY)
    res = pl.pallas_call(
        carrier, name=name, grid=grid, in_specs=list(in_specs) + [any_spec] * ci,
        out_specs=list(out_specs) + [any_spec] * co, out_shape=list(out_shape) + list(comm.out_shapes),
        scratch_shapes=list(scratch_shapes) + list(comm.sems), input_output_aliases=aliases,
        compiler_params=_cparams(n_axes))(*args, *comm.ins)
    return list(res[:no]), list(res[no:])


def _comm_alone(comm, name):
    ci, co = len(comm.ins), len(comm.out_shapes)

    def body(*refs):
        cin, cout, sems = refs[:ci], refs[ci:ci + co], refs[ci + co:]
        comm.start(cin, cout, sems)
        if comm.mid is not None:
            comm.mid(cin, cout, sems)
        comm.wait(cin, cout, sems)

    any_spec = pl.BlockSpec(memory_space=pl.ANY)
    res = pl.pallas_call(
        body, name=name, out_shape=list(comm.out_shapes), in_specs=[any_spec] * ci, out_specs=[any_spec] * co,
        scratch_shapes=list(comm.sems))(*comm.ins)
    return list(res)


def _rms(x, g):
    r = lax.rsqrt(jnp.mean(x * x, axis=-1, keepdims=True) + EPS)
    return x * r * g, r


def _rms_bwd(x, r, g, dy):
    xh = x * r
    dxh = dy * g
    dx = r * (dxh - xh * jnp.mean(dxh * xh, axis=-1, keepdims=True))
    dg = jnp.sum(dy * xh, axis=0, keepdims=True)
    return dx, dg


def _gelu(x):
    return 0.5 * x * (1.0 + lax.erf(x * INV_SQRT2))


def _gelu_grad(x):
    return 0.5 * (1.0 + lax.erf(x * INV_SQRT2)) + x * jnp.exp(-0.5 * x * x) * INV_SQRT_2PI


def _shift_down(a, k):
    rows = lax.broadcasted_iota(jnp.int32, a.shape, 0)
    return jnp.where(rows >= k, pltpu.roll(a, k, 0), 0.0)


def _shift_up(a, k):
    n = a.shape[0]
    rows = lax.broadcasted_iota(jnp.int32, a.shape, 0)
    return jnp.where(rows < n - k, pltpu.roll(a, n - k, 0), 0.0)


def _tril_bf16(w):
    r = lax.broadcasted_iota(jnp.int32, w.shape, 0)
    c = lax.broadcasted_iota(jnp.int32, w.shape, 1)
    return jnp.where(r >= c, w, 0.0).astype(BF16)


def _ln_head(vh, g):
    mu = jnp.mean(vh, axis=-1, keepdims=True)
    xc = vh - mu
    rr = lax.rsqrt(jnp.mean(xc * xc, axis=-1, keepdims=True) + EPS)
    xh = xc * rr
    return xh * g, xh, rr


def _conv_fwd(z, tail, cw_ref):
    ext = jnp.concatenate([tail, z], axis=0)
    zs1 = _shift_down(ext, 1)[HALO:]
    zs2 = _shift_down(ext, 2)[HALO:]
    y = cw_ref[2:3, :] * z + cw_ref[1:2, :] * zs1 + cw_ref[0:1, :] * zs2
    return y, zs1, zs2


def _pool_cnt(shape, blk_in_seq):
    rows = lax.broadcasted_iota(jnp.int32, shape, 0)
    grp = lax.broadcasted_iota(jnp.int32, shape, 1) // POOL_GD
    win = jnp.where(grp == 0, 2, jnp.where(grp == 1, 4, jnp.where(grp == 2, 8, 16)))
    tpos = blk_in_seq * shape[0] + rows + 1
    return jnp.minimum(tpos, win).astype(F32), grp


def _pool_select(grp, s2, s4, s8, s16):
    return jnp.where(grp == 0, s2, jnp.where(grp == 1, s4, jnp.where(grp == 2, s8, s16)))


def _pool_fwd(z, tail, blk_in_seq):
    ext = jnp.concatenate([tail, z], axis=0)
    s2 = ext + _shift_down(ext, 1)
    s4 = s2 + _shift_down(s2, 2)
    s8 = s4 + _shift_down(s4, 4)
    s16 = s8 + _shift_down(s8, 8)
    cnt, grp = _pool_cnt(z.shape, blk_in_seq)
    sums = _pool_select(grp, s2[HALO:], s4[HALO:], s8[HALO:], s16[HALO:])
    return sums / cnt - z, cnt, grp


def _pool_bwd(dpooled, dpm, head, grp):
    n = dpm.shape[0]
    ext = jnp.concatenate([dpm, head], axis=0)
    u2 = ext + _shift_up(ext, 1)
    u4 = u2 + _shift_up(u2, 2)
    u8 = u4 + _shift_up(u4, 4)
    u16 = u8 + _shift_up(u8, 8)
    return _pool_select(grp, u2[:n], u4[:n], u8[:n], u16[:n]) - dpooled


def _lane_sums(a):
    return _dot(a.astype(BF16), jnp.ones((a.shape[1], a.shape[1]), BF16), 1, 0)


def _swap_halves(y1):
    src = lax.broadcasted_iota(jnp.int32, (128, 128), 0)
    dst = lax.broadcasted_iota(jnp.int32, (128, 128), 1)
    perm = jnp.where(((dst < 32) & (src == dst + 32)) | ((dst >= 32) & (dst < QK_ROPE) & (src == dst - 32)), 1.0, 0.0)
    return _dot(y1.astype(BF16), perm.astype(BF16), 1, 0)


def _rope(y1, c, s):
    return y1 * c + _swap_halves(y1) * s


def _rope_bwd(d1, c, s):
    return d1 * c + _swap_halves(d1 * s)


def _qk_prep(x, g, c, s):
    r = lax.rsqrt(_lane_sums(x * x) * (1.0 / QK_DIM) + EPS)
    y = x * r * g
    return jnp.concatenate([y[:, :128], _rope(y[:, 128:], c, s)], axis=1), r


def _qk_prep_bwd(dout, x, r, g, c, s):
    dy = jnp.concatenate([dout[:, :128], _rope_bwd(dout[:, 128:], c, s)], axis=1)
    xh = x * r
    dxh = dy * g
    dx = r * (dxh - xh * (_lane_sums(dxh * xh) * (1.0 / QK_DIM)))
    return dx, jnp.sum(dy * xh, axis=0, keepdims=True)


def _place():
    return lax.axis_index("x"), lax.axis_index("y"), lax.axis_index("c")


def _gather_comm(arrs):
    n = len(arrs)

    def plan(ins, outs, sems):
        send_sems, recv_sems, local_sems = sems
        x, y, c = _place()
        me, sibling = (x, y, c), (x, y, 1 - c)
        chips = [(1 - x, y), (x, 1 - y), (1 - x, 1 - y)]

        def slot(a, px, py, pc):
            return outs[a].at[4 * px + 2 * py + pc]

        def copy(a, k, block, to, src=None):
            return pltpu.make_async_remote_copy(
                src_ref=slot(a, *block) if src is None else src, dst_ref=slot(a, *block),
                send_sem=send_sems.at[a, k], recv_sem=recv_sems.at[a, k], device_id=to, device_id_type=MESH)

        def own():
            mine = [pltpu.make_async_copy(ins[a], slot(a, *me), local_sems.at[a]) for a in range(n)]
            first = []
            for a in range(n):
                first.append(copy(a, 0, me, sibling, src=ins[a]))
                first += [copy(a, 1 + j, me, (*chip, c), src=ins[a]) for j, chip in enumerate(chips)]
            return mine, first

        return c, me, sibling, chips, copy, own

    def start(ins, outs, sems):
        mine, first = plan(ins, outs, sems)[-1]()
        for cp in mine + first:
            cp.start()

    def mid(ins, outs, sems):
        c, me, sibling, chips, copy, _ = plan(ins, outs, sems)
        for j, chip in enumerate(chips):
            for a in range(n):
                copy(a, 1 + j, (*chip, c), me).wait_recv()
                copy(a, 4 + j, (*chip, c), sibling).start()

    def wait(ins, outs, sems):
        c, me, sibling, chips, copy, own = plan(ins, outs, sems)
        mine, first = own()
        passed = [copy(a, 4 + j, (*chip, c), sibling) for j, chip in enumerate(chips) for a in range(n)]
        for a in range(n):
            copy(a, 0, sibling, me).wait_recv()
            for j, chip in enumerate(chips):
                copy(a, 4 + j, (*chip, 1 - c), me).wait_recv()
        for cp in first + passed:
            cp.wait_send()
        for cp in mine:
            cp.wait()

    return _Comm(
        list(arrs), [jax.ShapeDtypeStruct((N_DEV,) + a.shape, a.dtype) for a in arrs],
        [pltpu.SemaphoreType.DMA((n, 7)), pltpu.SemaphoreType.DMA((n, 7)), pltpu.SemaphoreType.DMA((n,))],
        start, wait, mid)


def _sum_gathered(g):
    rows = g.shape[1]

    def body(g_ref, sum_ref):
        total = g_ref[0]
        for d in range(1, N_DEV):
            total = total + g_ref[d]
        sum_ref[...] = total

    return pl.pallas_call(
        body, name="sum_gathered_small", out_shape=jax.ShapeDtypeStruct((rows, 128), F32), grid=(1,),
        in_specs=[pl.BlockSpec((N_DEV, rows, 128), lambda i: (0, 0, 0))],
        out_specs=pl.BlockSpec((rows, 128), lambda i: (0, 0)), compiler_params=_cparams(1),
    )(g)


def _sum_rows(rows):
    return rows if rows <= 512 else rows // 2


def _pair_exchange_comm(gp):
    _, rows, cols = gp.shape

    def copies(ins, outs, sems):
        send_sems, recv_sems = sems
        x, y, c = _place()
        return [pltpu.make_async_remote_copy(
            src_ref=ins[0].at[2 * j + (1 - c)], dst_ref=outs[0].at[j], send_sem=send_sems.at[j],
            recv_sem=recv_sems.at[j], device_id=(x, y, 1 - c), device_id_type=MESH) for j in range(4)]

    def start(ins, outs, sems):
        for cp in copies(ins, outs, sems):
            cp.start()

    def wait(ins, outs, sems):
        for cp in copies(ins, outs, sems):
            cp.wait()

    return _Comm([gp], [jax.ShapeDtypeStruct((4, rows, cols), gp.dtype)],
                 [pltpu.SemaphoreType.DMA((4,)), pltpu.SemaphoreType.DMA((4,))], start, wait)


def _rs_pair_sum(gp, got, where, name):
    _, rows, cols = got.shape
    rb = _sum_rows(rows)
    gp4 = gp.reshape(4, 2, rows, cols)

    def body(w_ref, a_ref, b_ref, o_ref):
        o_ref[0] = (a_ref[0, 0].astype(F32) + b_ref[0].astype(F32)).astype(o_ref.dtype)

    return pl.pallas_call(
        body, name=name, out_shape=jax.ShapeDtypeStruct((4, rows, cols), gp.dtype),
        grid_spec=pltpu.PrefetchScalarGridSpec(
            num_scalar_prefetch=1, grid=(4, rows // rb),
            in_specs=[pl.BlockSpec((1, 1, rb, cols), lambda k, r, w: (w[1 + k], w[0], r, 0)),
                      pl.BlockSpec((1, rb, cols), lambda k, r, w: (w[1 + k], r, 0))],
            out_specs=pl.BlockSpec((1, rb, cols), lambda k, r, w: (k, r, 0))),
        compiler_params=_cparams(2),
    )(where, gp4, got)


def _chip_exchange_comm(pb):
    _, rows, cols = pb.shape

    def copies(ins, outs, sems):
        send_sems, recv_sems = sems
        x, y, c = _place()
        chips = [(1 - x, y), (x, 1 - y), (1 - x, 1 - y)]
        return [pltpu.make_async_remote_copy(
            src_ref=ins[0].at[1 + k], dst_ref=outs[0].at[k], send_sem=send_sems.at[k],
            recv_sem=recv_sems.at[k], device_id=(px, py, c), device_id_type=MESH)
            for k, (px, py) in enumerate(chips)]

    def start(ins, outs, sems):
        for cp in copies(ins, outs, sems):
            cp.start()

    def wait(ins, outs, sems):
        for cp in copies(ins, outs, sems):
            cp.wait()

    return _Comm([pb], [jax.ShapeDtypeStruct((3, rows, cols), pb.dtype)],
                 [pltpu.SemaphoreType.DMA((3,)), pltpu.SemaphoreType.DMA((3,))], start, wait)


def _chip_exchange_start(pb):
    _, rows, cols = pb.shape

    def body(pb_ref, land_ref, *rest):
        sems, token = rest[:6], rest[8]
        x, y, c = _place()
        chips = [(1 - x, y), (x, 1 - y), (1 - x, 1 - y)]
        for k, (px, py) in enumerate(chips):
            pltpu.make_async_remote_copy(
                src_ref=pb_ref.at[1 + k], dst_ref=land_ref.at[k], send_sem=sems[k], recv_sem=sems[3 + k],
                device_id=(px, py, c), device_id_type=MESH).start()
        token[...] = jnp.zeros_like(token)

    hbm = pl.BlockSpec(memory_space=pltpu.HBM)
    sem = pl.BlockSpec(memory_space=pltpu.SEMAPHORE)
    land = lax.empty((3, rows, cols), pb.dtype)
    res = pl.pallas_call(
        body, name="rs_chip_exchange_start_mix0",
        out_shape=(*[pltpu.SemaphoreType.DMA(())] * 6, pltpu.HBM(pb.shape, pb.dtype), pltpu.HBM(land.shape, land.dtype),
                   jax.ShapeDtypeStruct((8, 128), F32)),
        in_specs=(hbm, hbm), out_specs=(*[sem] * 6, hbm, hbm, pl.BlockSpec(memory_space=pltpu.VMEM)),
        input_output_aliases={0: 6, 1: 7},
        compiler_params=pltpu.CompilerParams(has_side_effects=pltpu.SideEffectType.DATAFLOW_SIDE_EFFECTING),
    )(pltpu.with_memory_space_constraint(pb, pltpu.HBM), pltpu.with_memory_space_constraint(land, pltpu.HBM))
    return list(res[:6]), res[6], res[7], res[8]


def _chip_exchange_wait(sems, pb_thru, land_thru, after):
    def body(pb_ref, land_ref, *rest):
        sems_in = rest[:6]
        x, y, c = _place()
        chips = [(1 - x, y), (x, 1 - y), (1 - x, 1 - y)]
        for k, (px, py) in enumerate(chips):
            cp = pltpu.make_async_remote_copy(
                src_ref=pb_ref.at[1 + k], dst_ref=land_ref.at[k], send_sem=sems_in[k], recv_sem=sems_in[3 + k],
                device_id=(px, py, c), device_id_type=MESH)
            cp.wait_send()
            cp.wait_recv()

    hbm = pl.BlockSpec(memory_space=pltpu.HBM)
    sem = pl.BlockSpec(memory_space=pltpu.SEMAPHORE)
    res = pl.pallas_call(
        body, name="rs_chip_exchange_wait_mix0",
        out_shape=(pltpu.HBM(pb_thru.shape, pb_thru.dtype), pltpu.HBM(land_thru.shape, land_thru.dtype)),
        in_specs=(hbm, hbm, *[sem] * 6, pl.BlockSpec(memory_space=pl.ANY)), out_specs=(hbm, hbm),
        input_output_aliases={0: 0, 1: 1},
        compiler_params=pltpu.CompilerParams(has_side_effects=pltpu.SideEffectType.DATAFLOW_SIDE_EFFECTING),
    )(pb_thru, land_thru, *sems, after)
    return res[0], res[1]


def _rs_final_sums(pbs, gots, name, after=None):
    n = len(pbs)

    def body(*refs):
        outs = refs[len(refs) - n:]
        for a in range(n):
            m_ref, g_ref, o_ref = refs[a], refs[n + a], outs[a]
            o_ref[...] = ((m_ref[0].astype(F32) + g_ref[0].astype(F32)) + g_ref[1].astype(F32)) + g_ref[2].astype(F32)

    half = [pb.shape[1] // 2 for pb in pbs]
    in_specs = ([pl.BlockSpec((1, h, D), lambda i: (0, i, 0)) for h in half]
                + [pl.BlockSpec((3, h, D), lambda i: (0, i, 0)) for h in half])
    args = (*pbs, *gots)
    if after is not None:
        in_specs, args = in_specs + [pl.BlockSpec(memory_space=pl.ANY)], args + (after,)
    res, _ = _call(body, name, (2,), in_specs, [pl.BlockSpec((h, D), lambda i: (i, 0)) for h in half],
                   [jax.ShapeDtypeStruct(pb.shape[1:], F32) for pb in pbs], args)
    return res


def _rope_tables(pos_col, inv_freq, comm=None):
    t = pos_col.shape[0]

    def body(p_ref, f_ref, c_ref, s_ref):
        ang = p_ref[...].astype(F32) * f_ref[...]
        lane = lax.broadcasted_iota(jnp.int32, ang.shape, 1)
        c_ref[...] = jnp.where(lane < QK_ROPE, jnp.cos(ang), 0.0)
        s = jnp.sin(ang)
        s_ref[...] = jnp.where(lane < 32, -s, jnp.where(lane < QK_ROPE, s, 0.0))

    spec = pl.BlockSpec((TB, 128), lambda i: (i, 0))
    return _call(
        body, "rope_tables", (t // TB,), [pl.BlockSpec((TB, 1), lambda i: (i, 0)), _const_spec((1, 128))],
        [spec] * 2, [jax.ShapeDtypeStruct((t, 128), F32)] * 2, (pos_col, inv_freq), (), comm)


def _sgu_conv_fwd(proj, tail, lng_ref, ws_ref, bst_ref, cw_ref):
    gu = _gelu(proj[:, 0:SG_W])
    gv = _gelu(proj[:, SG_W:2 * SG_W])
    bg = proj[:, 1024:1536]
    z = proj[:, 1536:2048] * proj[:, 2048:2560]
    heads = []
    for h in range(SG_HEADS):
        sl = slice(h * SG_HD, (h + 1) * SG_HD)
        vn, _, _ = _ln_head(gv[:, sl], lng_ref[:, sl])
        vnb = vn.astype(BF16)
        wm = _tril_bf16(ws_ref[h])
        bcol = bst_ref[:, h:h + 1]
        mixed = jnp.concatenate(
            [_dot(wm, vnb[k * SG_CHUNK:(k + 1) * SG_CHUNK], 1, 0) + bcol for k in range(TB // SG_CHUNK)], axis=0)
        heads.append(gu[:, sl] * mixed)
    a_out = jnp.concatenate(heads, axis=1)
    y, _, _ = _conv_fwd(z, tail, cw_ref)
    return a_out, bg * y, z


def _even_fwd(x, wg, gamma, lng, ws, bst, cw, seq, comm=None):
    t = x.shape[0]
    nbs = seq // TB

    def body(x_ref, gam_ref, win_ref, wout_ref, lng_ref, ws_ref, bst_ref, cw_ref, x1_ref, proj_ref, tail_ref):
        i = pl.program_id(0)
        xv = x_ref[...]
        h, _ = _rms(xv, gam_ref[...])
        proj = _dot(h.astype(BF16), win_ref[...].reshape(EVEN_IN, D), 1, 1)
        proj_ref[...] = proj.astype(BF16)
        tail = jnp.where(i % nbs == 0, 0.0, tail_ref[...])
        a_out, b_out, z = _sgu_conv_fwd(proj, tail, lng_ref, ws_ref, bst_ref, cw_ref)
        tail_ref[...] = z[TB - HALO:, :]
        x1_ref[...] = (xv + _dot(a_out.astype(BF16), wout_ref[0:4].reshape(512, D), 1, 0)
                       + _dot(b_out.astype(BF16), wout_ref[4:8].reshape(512, D), 1, 0))

    row = pl.BlockSpec((TB, D), lambda i: (i, 0))
    return _call(
        body, "even_fwd", (t // TB,),
        [row, _const_spec((1, D)), _wspec(N_EIN, OFF_EIN), _wspec(N_SQ, OFF_EOUT), _const_spec((1, SG_W)),
         _const_spec((SG_HEADS, 128, 128)), _const_spec((128, 128)), _const_spec((8, SC_W))],
        [row, pl.BlockSpec((TB, EVEN_IN), lambda i: (i, 0))],
        [jax.ShapeDtypeStruct((t, D), F32), jax.ShapeDtypeStruct((t, EVEN_IN), BF16)],
        (x, gamma, wg, wg, lng, ws, bst, cw), [pltpu.VMEM((HALO, SC_W), F32)], comm)


def _even_bwd(x, proj, dx1, wg, gamma, lng, ws, bst, cw, seq, comm=None):
    t = x.shape[0]
    nb, nbs = t // TB, seq // TB

    def body(x_ref, proj_ref, ptail_ref, dx1_ref, gam_ref, win_ref, wout_ref, lng_ref, ws_ref, bst_ref, cw_ref,
             dx0_ref, dproj_ref, mix_ref, h_ref, dgam_ref, dws_ref, dbc_ref, dlng_ref, dcw_ref, head_ref):
        i = pl.program_id(0)
        blk = nb - 1 - i

        @pl.when(i == 0)
        def _():
            dgam_ref[...] = jnp.zeros_like(dgam_ref)
            dws_ref[...] = jnp.zeros_like(dws_ref)
            dbc_ref[...] = jnp.zeros_like(dbc_ref)
            dlng_ref[...] = jnp.zeros_like(dlng_ref)
            dcw_ref[...] = jnp.zeros_like(dcw_ref)

        xv = x_ref[...]
        gam = gam_ref[...]
        h, r = _rms(xv, gam)
        h_ref[...] = h.astype(BF16)
        dx1 = dx1_ref[...]
        dmix = _dot(dx1.astype(BF16), wout_ref[...].reshape(D, D), 1, 1)
        da, db = dmix[:, :SG_W], dmix[:, SG_W:]
        proj = proj_ref[...].astype(F32)
        u, v = proj[:, 0:SG_W], proj[:, SG_W:2 * SG_W]
        bg, cg, hv = proj[:, 1024:1536], proj[:, 1536:2048], proj[:, 2048:2560]
        gu, gv = _gelu(u), _gelu(v)

        a_heads, dgv_heads = [], []
        for hd in range(SG_HEADS):
            sl = slice(hd * SG_HD, (hd + 1) * SG_HD)
            g_h = lng_ref[:, sl]
            vn, xh, rr = _ln_head(gv[:, sl], g_h)
            vnb = vn.astype(BF16)
            wm = _tril_bf16(ws_ref[hd])
            bcol = bst_ref[:, hd:hd + 1]
            mixed_c, dvn_c = [], []
            dw_acc = jnp.zeros((128, 128), F32)
            db_acc = jnp.zeros((128, 1), F32)
            for k in range(TB // SG_CHUNK):
                rs = slice(k * SG_CHUNK, (k + 1) * SG_CHUNK)
                mixed = _dot(wm, vnb[rs], 1, 0) + bcol
                dmixed = da[rs, sl] * gu[rs, sl]
                dmb = dmixed.astype(BF16)
                dvn_c.append(_dot(wm, dmb, 0, 0))
                dw_acc = dw_acc + _dot(dmb, vnb[rs], 1, 1)
                db_acc = db_acc + jnp.sum(dmixed, axis=1, keepdims=True)
                mixed_c.append(mixed)
            mixed_h = jnp.concatenate(mixed_c, axis=0)
            dvn = jnp.concatenate(dvn_c, axis=0)
            r_i = lax.broadcasted_iota(jnp.int32, (128, 128), 0)
            c_i = lax.broadcasted_iota(jnp.int32, (128, 128), 1)
            dws_ref[hd] += jnp.where(r_i >= c_i, dw_acc, 0.0)
            dbc_ref[:, hd:hd + 1] += db_acc
            dlng_ref[:, sl] += jnp.sum(dvn * xh, axis=0, keepdims=True)
            dxh = dvn * g_h
            dgv = rr * (dxh - jnp.mean(dxh, axis=-1, keepdims=True)
                        - xh * jnp.mean(dxh * xh, axis=-1, keepdims=True))
            a_heads.append(gu[:, sl] * mixed_h)
            dproj_ref[:, sl] = (da[:, sl] * mixed_h * _gelu_grad(u[:, sl])).astype(BF16)
            dgv_heads.append(dgv * _gelu_grad(v[:, sl]))
        dproj_ref[:, SG_W:2 * SG_W] = jnp.concatenate(dgv_heads, axis=1).astype(BF16)
        mix_ref[:, :SG_W] = jnp.concatenate(a_heads, axis=1).astype(BF16)

        z = cg * hv
        pt = ptail_ref[...].astype(F32)
        tail = jnp.where(blk % nbs == 0, 0.0, pt[:, 1536:2048] * pt[:, 2048:2560])
        y, zs1, zs2 = _conv_fwd(z, tail, cw_ref)
        mix_ref[:, SG_W:] = (bg * y).astype(BF16)
        dy = db * bg
        head = jnp.where(blk % nbs == nbs - 1, 0.0, head_ref[...])
        ext = jnp.concatenate([dy, head], axis=0)
        dz = (cw_ref[2:3, :] * dy + cw_ref[1:2, :] * _shift_up(ext, 1)[:TB]
              + cw_ref[0:1, :] * _shift_up(ext, 2)[:TB])
        head_ref[...] = dy[:HALO, :]
        dcw_ref[2:3, :] += jnp.sum(dy * z, axis=0, keepdims=True)
        dcw_ref[1:2, :] += jnp.sum(dy * zs1, axis=0, keepdims=True)
        dcw_ref[0:1, :] += jnp.sum(dy * zs2, axis=0, keepdims=True)
        dproj_ref[:, 1024:1536] = (db * y).astype(BF16)
        dproj_ref[:, 1536:2048] = (dz * hv).astype(BF16)
        dproj_ref[:, 2048:2560] = (dz * cg).astype(BF16)

        dh = _dot(dproj_ref[...], win_ref[...].reshape(EVEN_IN, D), 1, 0)
        dxn, dgam = _rms_bwd(xv, r, gam, dh)
        dgam_ref[...] += dgam
        dx0_ref[...] = dx1 + dxn

    def rev(w):
        return pl.BlockSpec((TB, w), lambda i: (nb - 1 - i, 0))

    ptail = pl.BlockSpec((HALO, EVEN_IN), lambda i: (jnp.maximum((nb - 1 - i) * (TB // HALO) - 1, 0), 0))
    return _call(
        body, "even_bwd", (nb,),
        [rev(D), rev(EVEN_IN), ptail, rev(D), _const_spec((1, D)), _wspec(N_EIN, OFF_EIN),
         _wspec(N_SQ, OFF_EOUT), _const_spec((1, SG_W)), _const_spec((SG_HEADS, 128, 128)),
         _const_spec((128, 128)), _const_spec((8, SC_W))],
        [rev(D), rev(EVEN_IN), rev(D), rev(D), _const_spec((1, D)), _const_spec((SG_HEADS, 128, 128)),
         _const_spec((128, 128)), _const_spec((1, SG_W)), _const_spec((8, SC_W))],
        [jax.ShapeDtypeStruct((t, D), F32), jax.ShapeDtypeStruct((t, EVEN_IN), BF16),
         jax.ShapeDtypeStruct((t, D), BF16), jax.ShapeDtypeStruct((t, D), BF16),
         jax.ShapeDtypeStruct((1, D), F32), jax.ShapeDtypeStruct((SG_HEADS, 128, 128), F32),
         jax.ShapeDtypeStruct((128, 128), F32), jax.ShapeDtypeStruct((1, SG_W), F32),
         jax.ShapeDtypeStruct((8, SC_W), F32)],
        (x, proj, proj, dx1, gamma, wg, wg, lng, ws, bst, cw), [pltpu.VMEM((HALO, SC_W), F32)], comm)


def _last_block_fwd(x, c_out, d_out, w_mix1, w_gu, w_d, gamma, target):
    t = x.shape[0]

    def body(x_ref, c_ref, d_ref, wo_ref, gam_ref, wg_ref, wu_ref, wd_ref, t_ref,
             x3_ref, dy_ref, g_ref, u_ref, loss_ref):
        @pl.when(pl.program_id(0) == 0)
        def _():
            loss_ref[...] = jnp.zeros_like(loss_ref)

        xv = (x_ref[...] + _dot(c_ref[...], wo_ref[0:2].reshape(POOL_W, D), 1, 0)
              + _dot(d_ref[...], wo_ref[2:8].reshape(HEADS * V_DIM, D), 1, 0))
        x3_ref[...] = xv
        h, _ = _rms(xv, gam_ref[...])
        hb = h.astype(BF16)
        g = _dot(hb, wg_ref[...].reshape(D_FF, D), 1, 1)
        u = _dot(hb, wu_ref[...].reshape(D_FF, D), 1, 1)
        g_ref[...] = g.astype(BF16)
        u_ref[...] = u.astype(BF16)
        act = g * jax.nn.sigmoid(g) * u
        err = xv + _dot(act.astype(BF16), wd_ref[...].reshape(D_FF, D), 1, 0) - t_ref[...]
        dy_ref[...] = err * (1.0 / D)
        sq = jnp.sum(jnp.sum(err * err, axis=-1, keepdims=True), axis=0, keepdims=True)
        loss_ref[...] += (0.5 / D) * sq

    def row(w):
        return pl.BlockSpec((TB, w), lambda i: (i, 0))

    res, _ = _call(
        body, "last_block_fwd", (t // TB,),
        [row(D), row(POOL_W), row(HEADS * V_DIM), _wspec(N_SQ, OFF_OOUT), _const_spec((1, D)),
         _wspec(N_FF, OFF_GATE), _wspec(N_FF, OFF_UP), _wspec(N_FF, 0), row(D)],
        [row(D), row(D), row(D_FF), row(D_FF), _const_spec((8, 128))],
        [jax.ShapeDtypeStruct((t, D), F32), jax.ShapeDtypeStruct((t, D), F32), jax.ShapeDtypeStruct((t, D_FF), BF16),
         jax.ShapeDtypeStruct((t, D_FF), BF16), jax.ShapeDtypeStruct((8, 128), F32)],
        (x, c_out, d_out, w_mix1, gamma, w_gu, w_gu, w_d, target))
    return res


def _ffn_up(x, w_gu, gamma, name, comm=None):
    t = x.shape[0]

    def body(x_ref, gam_ref, wg_ref, wu_ref, g_ref, u_ref, act_ref):
        h, _ = _rms(x_ref[...], gam_ref[...])
        hb = h.astype(BF16)
        g = _dot(hb, wg_ref[...].reshape(D_FF, D), 1, 1)
        u = _dot(hb, wu_ref[...].reshape(D_FF, D), 1, 1)
        g_ref[...] = g.astype(BF16)
        u_ref[...] = u.astype(BF16)
        act_ref[...] = (g * jax.nn.sigmoid(g) * u).astype(BF16)

    row = pl.BlockSpec((TB, D), lambda i: (i, 0))
    wide = pl.BlockSpec((TB, D_FF), lambda i: (i, 0))
    return _call(body, name, (t // TB,), [row, _const_spec((1, D)), _wspec(N_FF, OFF_GATE), _wspec(N_FF, OFF_UP)],
                 [wide, wide, wide], [jax.ShapeDtypeStruct((t, D_FF), BF16)] * 3, (x, gamma, w_gu, w_gu), (), comm)


def _ffn_down(x, act, w_d, name, comm=None):
    t = x.shape[0]

    def body(x_ref, a_ref, wd_ref, y_ref):
        y_ref[...] = x_ref[...] + _dot(a_ref[...], wd_ref[...].reshape(D_FF, D), 1, 0)

    row = pl.BlockSpec((TB, D), lambda i: (i, 0))
    wide = pl.BlockSpec((TB, D_FF), lambda i: (i, 0))
    return _call(body, name, (t // TB,), [row, wide, _wspec(N_FF, 0)], [row], [jax.ShapeDtypeStruct((t, D), F32)],
                 (x, act, w_d), (), comm)


def _ffn_bwd(x, g, u, dy, w_gu, w_d, gamma, name, comm=None, w_mix1=None):
    t = x.shape[0]
    with_dmix = w_mix1 is not None

    def body(*refs):
        x_ref, g_ref, u_ref, dy_ref, gam_ref, wg_ref, wu_ref, wd_ref = refs[:8]
        dx_ref, act_ref, dg_ref, du_ref, h_ref, dgam_ref = refs[8 + with_dmix:14 + with_dmix]

        @pl.when(pl.program_id(0) == 0)
        def _():
            dgam_ref[...] = jnp.zeros_like(dgam_ref)

        xv = x_ref[...]
        gam = gam_ref[...]
        h, r = _rms(xv, gam)
        h_ref[...] = h.astype(BF16)
        dyv = dy_ref[...]
        dact = _dot(dyv.astype(BF16), wd_ref[...].reshape(D_FF, D), 1, 1)
        gv = g_ref[...].astype(F32)
        uv = u_ref[...].astype(F32)
        sg = jax.nn.sigmoid(gv)
        silu = gv * sg
        act_ref[...] = (silu * uv).astype(BF16)
        dgb = (dact * uv * (sg * (1.0 + gv * (1.0 - sg)))).astype(BF16)
        dub = (dact * silu).astype(BF16)
        dg_ref[...] = dgb
        du_ref[...] = dub
        dh = _dot(dgb, wg_ref[...].reshape(D_FF, D), 1, 0) + _dot(dub, wu_ref[...].reshape(D_FF, D), 1, 0)
        dxn, dgam = _rms_bwd(xv, r, gam, dh)
        dgam_ref[...] += dgam
        dx = dyv + dxn
        dx_ref[...] = dx
        if with_dmix:
            refs[15][...] = _dot(dx.astype(BF16), refs[8][...].reshape(D, D), 1, 1).astype(BF16)

    row = pl.BlockSpec((TB_FFN_BWD, D), lambda i: (i, 0))
    wide = pl.BlockSpec((TB_FFN_BWD, D_FF), lambda i: (i, 0))
    in_specs = [row, wide, wide, row, _const_spec((1, D)), _wspec(N_FF, OFF_GATE), _wspec(N_FF, OFF_UP),
                _wspec(N_FF, 0)]
    out_specs = [row, wide, wide, wide, row, _const_spec((1, D))]
    out_shape = [jax.ShapeDtypeStruct((t, D), F32), jax.ShapeDtypeStruct((t, D_FF), BF16),
                 jax.ShapeDtypeStruct((t, D_FF), BF16), jax.ShapeDtypeStruct((t, D_FF), BF16),
                 jax.ShapeDtypeStruct((t, D), BF16), jax.ShapeDtypeStruct((1, D), F32)]
    args = (x, g, u, dy, gamma, w_gu, w_gu, w_d)
    if with_dmix:
        in_specs, args = in_specs + [_wspec(N_SQ, OFF_OOUT)], args + (w_mix1,)
        out_specs, out_shape = out_specs + [row], out_shape + [jax.ShapeDtypeStruct((t, D), BF16)]
    return _call(body, name, (t // TB_FFN_BWD,), in_specs, out_specs, out_shape, args, (), comm)


def _odd_pre_fwd(x, wg, gamma, qbt, kvbt, qa_g, kva_g, pw_bd, pscale, seq, comm=None):
    t = x.shape[0]
    nbs = seq // TB

    def body(x_ref, gam_ref, win_ref, qb_ref, kvb_ref, qa_ref, kva_ref, pw_ref, ps_ref,
             proj_ref, q_ref, kv_ref, kr_ref, c_ref, tail_ref):
        i = pl.program_id(0)
        h, _ = _rms(x_ref[...], gam_ref[...])
        proj = _dot(h.astype(BF16), win_ref[...].reshape(D, D), 1, 0)
        proj_ref[...] = proj.astype(BF16)
        zp, ql, kvl = proj[:, :POOL_W], proj[:, 256:640], proj[:, 640:896]
        kr_ref[...] = proj[:, 896:1024]
        qn, _ = _rms(ql, qa_ref[...])
        q_ref[...] = _dot(qn.astype(BF16), qb_ref[...], 1, 1).astype(BF16)
        kvn, _ = _rms(kvl, kva_ref[...])
        kv_ref[...] = _dot(kvn.astype(BF16), kvb_ref[...], 1, 1).astype(BF16)
        tail = jnp.where(i % nbs == 0, 0.0, tail_ref[...])
        pooled, _, _ = _pool_fwd(zp, tail, i % nbs)
        tail_ref[...] = zp[TB - HALO:, :]
        c_ref[...] = (_dot(pooled.astype(BF16), pw_ref[...], 1, 0) * ps_ref[...]).astype(BF16)

    def row(w):
        return pl.BlockSpec((TB, w), lambda i: (i, 0))

    return _call(
        body, "odd_pre_fwd", (t // TB,),
        [row(D), _const_spec((1, D)), _wspec(N_SQ, OFF_OIN), _const_spec((HEADS * HP, Q_LORA)),
         _const_spec((HEADS * HP, KV_LORA)), _const_spec((1, Q_LORA)), _const_spec((1, KV_LORA)),
         _const_spec((POOL_W, POOL_W)), _const_spec((1, POOL_W))],
        [row(D), row(HEADS * HP), row(HEADS * HP), row(128), row(POOL_W)],
        [jax.ShapeDtypeStruct((t, D), BF16), jax.ShapeDtypeStruct((t, HEADS * HP), BF16),
         jax.ShapeDtypeStruct((t, HEADS * HP), BF16), jax.ShapeDtypeStruct((t, 128), F32),
         jax.ShapeDtypeStruct((t, POOL_W), BF16)],
        (x, gamma, wg, qbt, kvbt, qa_g, kva_g, pw_bd, pscale), [pltpu.VMEM((HALO, POOL_W), F32)], comm)


def _odd_pre_bwd(x, proj, dx3, dmix, dq, dkv, dkr, wg, gamma, qbt, kvbt, qa_g, kva_g, pw_bd, pscale, seq):
    t = x.shape[0]
    nb, nbs = t // TB, seq // TB

    def body(x_ref, proj_ref, ptail_ref, dx3_ref, dco_ref, dq_ref, dkv_ref, dkr_ref, gam_ref, win_ref, qb_ref,
             kvb_ref, qa_ref, kva_ref, pw_ref, ps_ref,
             dx2_ref, dproj_ref, h_ref, qn_ref, kvn_ref, dgam_ref, dqa_ref, dkva_ref, dpw_ref, dps_ref, head_ref):
        i = pl.program_id(0)
        blk = nb - 1 - i

        @pl.when(i == 0)
        def _():
            dgam_ref[...] = jnp.zeros_like(dgam_ref)
            dqa_ref[...] = jnp.zeros_like(dqa_ref)
            dkva_ref[...] = jnp.zeros_like(dkva_ref)
            dpw_ref[...] = jnp.zeros_like(dpw_ref)
            dps_ref[...] = jnp.zeros_like(dps_ref)

        xv = x_ref[...]
        gam = gam_ref[...]
        h, r = _rms(xv, gam)
        h_ref[...] = h.astype(BF16)
        proj = proj_ref[...].astype(F32)
        zp, ql, kvl = proj[:, :POOL_W], proj[:, 256:640], proj[:, 640:896]

        qa = qa_ref[...]
        qn, rq = _rms(ql, qa)
        qn_ref[...] = qn.astype(BF16)
        dql, dqa = _rms_bwd(ql, rq, qa, _dot(dq_ref[...], qb_ref[...], 1, 0))
        dqa_ref[...] += dqa
        kva = kva_ref[...]
        kvn, rkv = _rms(kvl, kva)
        kvn_ref[...] = kvn.astype(BF16)
        dkvl, dkva = _rms_bwd(kvl, rkv, kva, _dot(dkv_ref[...], kvb_ref[...], 1, 0))
        dkva_ref[...] += dkva

        pt = ptail_ref[...].astype(F32)
        tail = jnp.where(blk % nbs == 0, 0.0, pt[:, :POOL_W])
        pooled, cnt, grp = _pool_fwd(zp, tail, blk % nbs)
        pb = pooled.astype(BF16)
        pw = pw_ref[...]
        dco = dco_ref[...].astype(F32)
        dps_ref[...] += jnp.sum(dco * _dot(pb, pw, 1, 0), axis=0, keepdims=True)
        dpo = (dco * ps_ref[...]).astype(BF16)
        dpw_ref[...] += _dot(pb, dpo, 0, 0)
        dpooled = _dot(dpo, pw, 1, 1)
        dpm = dpooled / cnt
        head = jnp.where(blk % nbs == nbs - 1, 0.0, head_ref[...])
        dz = _pool_bwd(dpooled, dpm, head, grp)
        head_ref[...] = dpm[:HALO, :]

        dproj_ref[:, :POOL_W] = dz.astype(BF16)
        dproj_ref[:, 256:640] = dql.astype(BF16)
        dproj_ref[:, 640:896] = dkvl.astype(BF16)
        dproj_ref[:, 896:1024] = dkr_ref[...].astype(BF16)
        dh = _dot(dproj_ref[...], win_ref[...].reshape(D, D), 1, 1)
        dxn, dgam = _rms_bwd(xv, r, gam, dh)
        dgam_ref[...] += dgam
        dx2_ref[...] = dx3_ref[...] + dxn

    def rev(w):
        return pl.BlockSpec((TB, w), lambda i: (nb - 1 - i, 0))

    ptail = pl.BlockSpec((HALO, D), lambda i: (jnp.maximum((nb - 1 - i) * (TB // HALO) - 1, 0), 0))
    return pl.pallas_call(
        body, name="odd_pre_bwd",
        out_shape=[jax.ShapeDtypeStruct((t, D), F32), jax.ShapeDtypeStruct((t, D), BF16),
                   jax.ShapeDtypeStruct((t, D), BF16), jax.ShapeDtypeStruct((t, Q_LORA), BF16),
                   jax.ShapeDtypeStruct((t, KV_LORA), BF16), jax.ShapeDtypeStruct((1, D), F32),
                   jax.ShapeDtypeStruct((1, Q_LORA), F32), jax.ShapeDtypeStruct((1, KV_LORA), F32),
                   jax.ShapeDtypeStruct((POOL_W, POOL_W), F32), jax.ShapeDtypeStruct((1, POOL_W), F32)],
        grid=(nb,),
        in_specs=[rev(D), rev(D), ptail, rev(D), rev(POOL_W), rev(HEADS * HP), rev(HEADS * HP), rev(128),
                  _const_spec((1, D)), _wspec(N_SQ, OFF_OIN), _const_spec((HEADS * HP, Q_LORA)),
                  _const_spec((HEADS * HP, KV_LORA)), _const_spec((1, Q_LORA)), _const_spec((1, KV_LORA)),
                  _const_spec((POOL_W, POOL_W)), _const_spec((1, POOL_W))],
        out_specs=[rev(D), rev(D), rev(D), rev(Q_LORA), rev(KV_LORA), _const_spec((1, D)), _const_spec((1, Q_LORA)),
                   _const_spec((1, KV_LORA)), _const_spec((POOL_W, POOL_W)), _const_spec((1, POOL_W))],
        scratch_shapes=[pltpu.VMEM((HALO, POOL_W), F32)],
        compiler_params=_cparams(1),
    )(x, proj, proj, dx3, dmix, dq, dkv, dkr, gamma, wg, qbt, kvbt, qa_g, kva_g, pw_bd, pscale)


def _attn_specs(seq):
    head = pl.BlockSpec((seq, HP), lambda b, h: (b, h))
    shared = pl.BlockSpec((seq, 128), lambda b, h: (b, 0))
    gain = pl.BlockSpec((1, HP), lambda b, h: (0, 0))
    return head, shared, gain


def _causal_bias(n):
    rows = lax.broadcasted_iota(jnp.int32, (n, n), 0)
    cols = lax.broadcasted_iota(jnp.int32, (n, n), 1)
    return jnp.where(cols <= rows, 0.0, NEG_INF)


def _attn_fwd(q, kv, kr, cos, sin, gq, gk, seq, comm=None):
    t = q.shape[0]
    qb = min(512, seq)

    def body(q_ref, kv_ref, kr_ref, c_ref, s_ref, gq_ref, gk_ref, o_ref, lse_ref):
        c, s = c_ref[...], s_ref[...]
        qf, _ = _qk_prep(q_ref[...].astype(F32), gq_ref[...], c, s)
        kin = jnp.concatenate([kv_ref[:, :128].astype(F32), kr_ref[...]], axis=1)
        kf, _ = _qk_prep(kin, gk_ref[...], c, s)
        qf, kf = qf.astype(BF16), kf.astype(BF16)
        v1 = jnp.concatenate([kv_ref[:, 128:], jnp.ones((seq, V_DIM), BF16)], axis=1)
        bias = _causal_bias(qb)
        for q0 in range(0, seq, qb):
            q1 = q0 + qb
            qblk = qf[q0:q1]
            s_dg = _dot(qblk, kf[q0:q1], 1, 1) + bias
            m = jnp.max(s_dg, axis=-1, keepdims=True)
            if q0:
                s_off = _dot(qblk, kf[:q0], 1, 1)
                m = jnp.maximum(m, jnp.max(s_off, axis=-1, keepdims=True))
            acc = _dot(jnp.exp(s_dg - m).astype(BF16), v1[q0:q1], 1, 0)
            if q0:
                acc = acc + _dot(jnp.exp(s_off - m).astype(BF16), v1[:q0], 1, 0)
            l = acc[:, V_DIM:]
            o_ref[q0:q1, :] = (acc[:, :V_DIM] / l).astype(BF16)
            lse_ref[q0:q1, :] = m + jnp.log(l)

    head, shared, gain = _attn_specs(seq)
    per_head = pl.BlockSpec((seq, V_DIM), lambda b, h: (b, h))
    return _call(
        body, "attn_fwd", (t // seq, HEADS),
        [head, head, shared, shared, shared, gain, gain], [per_head, per_head],
        [jax.ShapeDtypeStruct((t, HEADS * V_DIM), BF16), jax.ShapeDtypeStruct((t, HEADS * V_DIM), F32)],
        (q, kv, kr, cos, sin, gq, gk), (), comm)


def _attn_bwd(q, kv, kr, cos, sin, gq, gk, dmix, d_out, lse, seq, comm=None):
    t = q.shape[0]
    qb = min(512, seq)

    def body(q_ref, kv_ref, kr_ref, c_ref, s_ref, gq_ref, gk_ref, do_ref, o_ref, lse_ref,
             dq_ref, dkv_ref, dkr_ref, dgq_ref, dgk_ref, dqf_ref, dkf_ref, dv_ref):
        b, hd = pl.program_id(0), pl.program_id(1)

        @pl.when((b == 0) & (hd == 0))
        def _():
            dgq_ref[...] = jnp.zeros_like(dgq_ref)
            dgk_ref[...] = jnp.zeros_like(dgk_ref)

        c, sn = c_ref[...], s_ref[...]
        gq_v, gk_v = gq_ref[...], gk_ref[...]
        qin = q_ref[...].astype(F32)
        kin = jnp.concatenate([kv_ref[:, :128].astype(F32), kr_ref[...]], axis=1)
        qf32, rq = _qk_prep(qin, gq_v, c, sn)
        kf32, rk = _qk_prep(kin, gk_v, c, sn)
        qf, kf = qf32.astype(BF16), kf32.astype(BF16)
        vb = kv_ref[:, 128:]
        dkf_ref[...] = jnp.zeros_like(dkf_ref)
        dv_ref[...] = jnp.zeros_like(dv_ref)
        bias = _causal_bias(qb)
        for q0 in range(0, seq, qb):
            q1 = q0 + qb
            qblk = qf[q0:q1]
            do = do_ref[q0:q1, :]
            lse_col = lse_ref[q0:q1, 0:1]
            d_col = jnp.sum(do.astype(F32) * o_ref[q0:q1, :].astype(F32), axis=-1, keepdims=True)
            dq_acc = None
            for k0, k1, diag in ((q0, q1, True), (0, q0, False)):
                if k1 == k0:
                    continue
                s = _dot(qblk, kf[k0:k1], 1, 1)
                p = jnp.exp((s + bias if diag else s) - lse_col)
                dv_ref[k0:k1, :] += _dot(p.astype(BF16), do, 0, 0)
                ds = (p * (_dot(do, vb[k0:k1], 1, 1) - d_col)).astype(BF16)
                part = _dot(ds, kf[k0:k1], 1, 0)
                dq_acc = part if dq_acc is None else dq_acc + part
                dkf_ref[k0:k1, :] += _dot(ds, qblk, 0, 0)
            dqf_ref[q0:q1, :] = dq_acc
        dqin, dgq = _qk_prep_bwd(dqf_ref[...], qin, rq, gq_v, c, sn)
        dkin, dgk = _qk_prep_bwd(dkf_ref[...], kin, rk, gk_v, c, sn)
        dgq_ref[...] += dgq
        dgk_ref[...] += dgk
        dq_ref[...] = dqin.astype(BF16)
        dkv_ref[:, :128] = dkin[:, :128].astype(BF16)
        dkv_ref[:, 128:] = dv_ref[...].astype(BF16)

        @pl.when(hd == 0)
        def _():
            dkr_ref[...] = dkin[:, 128:]

        @pl.when(hd != 0)
        def _():
            dkr_ref[...] += dkin[:, 128:]

    head, shared, gain = _attn_specs(seq)
    per_head = pl.BlockSpec((seq, V_DIM), lambda b, h: (b, h))
    return _call(
        body, "attn_bwd", (t // seq, HEADS),
        [head, head, shared, shared, shared, gain, gain,
         pl.BlockSpec((seq, V_DIM), lambda b, h: (b, 2 + h)), per_head, per_head],
        [head, head, shared, gain, gain],
        [jax.ShapeDtypeStruct((t, HEADS * HP), BF16), jax.ShapeDtypeStruct((t, HEADS * HP), BF16),
         jax.ShapeDtypeStruct((t, 128), F32), jax.ShapeDtypeStruct((1, HP), F32),
         jax.ShapeDtypeStruct((1, HP), F32)],
        (q, kv, kr, cos, sin, gq, gk, dmix, d_out, lse),
        [pltpu.VMEM((seq, HP), F32), pltpu.VMEM((seq, HP), F32), pltpu.VMEM((seq, V_DIM), F32)], comm)


def _tn(a_list, b, tm, name, into=None, comm=None):
    t, n_out = b.shape
    widths = [a.shape[1] for a in a_list]
    tk = min(TK_DW, t)
    m, na, nk = sum(widths), len(a_list), t // tk
    assert na == 1 or tm == m

    def body(*refs):
        a_refs, b_ref, o_ref, acc_ref = refs[:na], refs[na], refs[-2], refs[-1]
        k = pl.program_id(1)

        @pl.when(k == 0)
        def _():
            acc_ref[...] = jnp.zeros_like(acc_ref)

        bb = b_ref[...].astype(BF16)
        m0 = 0
        for a_ref, w in zip(a_refs, widths):
            rows = slice(0, tm) if na == 1 else slice(m0, m0 + w)
            acc_ref[rows, :] += _dot(a_ref[...].astype(BF16), bb, 0, 0)
            m0 += w

        @pl.when(k == nk - 1)
        def _():
            o_ref[...] = acc_ref[...].astype(BF16).reshape(o_ref.shape)

    if na == 1:
        in_specs = [pl.BlockSpec((tk, tm), lambda i, k: (k, i))]
    else:
        in_specs = [pl.BlockSpec((tk, w), lambda i, k: (k, 0)) for w in widths]
    in_specs.append(pl.BlockSpec((tk, n_out), lambda i, k: (k, 0)))
    args = list(a_list) + [b]
    if into is None:
        out_spec = pl.BlockSpec((tm, n_out), lambda i, k: (i, 0))
        out_shape = jax.ShapeDtypeStruct((m, n_out), BF16)
        aliases = {}
    else:
        buf, n, off = into
        assert n_out == D and tm % n == 0 and off % n == 0 and (na == 1 or tm // n == N_DEV)
        idx = off // n
        out_spec = pl.BlockSpec((tm // n, n, D), lambda i, k: (i, idx, 0))
        out_shape = jax.ShapeDtypeStruct(buf.shape, BF16)
        in_specs.append(pl.BlockSpec(memory_space=pl.ANY))
        args.append(buf)
        aliases = {len(args) - 1: 0}
    (res,), extra = _call(body, name, (m // tm, nk), in_specs, [out_spec], [out_shape], args,
                          [pltpu.VMEM((tm, n_out), F32)], comm, aliases)
    return (res, extra) if comm is not None else res


def _adamw(ws, gs, ms, vs, name, nblk=1):
    n = len(ws)
    c1 = 1.0 - B1 ** STEP
    c2 = 1.0 - B2 ** STEP

    def body(*refs):
        for a in range(n):
            w, g, m, v = (refs[k * n + a][...] for k in range(4))
            d_ref, m_ref, v_ref = (refs[(4 + k) * n + a] for k in range(3))
            m_new = B1 * m + (1.0 - B1) * g
            v_new = B2 * v + (1.0 - B2) * (g * g)
            d_ref[...] = -LR * ((m_new / c1) / (jnp.sqrt(v_new / c2) + ADAM_EPS) + WD * w)
            m_ref[...] = m_new
            v_ref[...] = v_new

    grid = (nblk,)
    assert all(w.shape[0] % nblk == 0 and (nblk == 1 or (w.shape[0] // nblk) % 8 == 0) for w in ws)
    specs = [pl.BlockSpec((w.shape[0] // nblk, w.shape[1]), lambda i: (i, 0)) for w in ws]
    outs, _ = _call(body, name, grid, specs * 4, specs * 3, [jax.ShapeDtypeStruct(w.shape, F32) for w in ws] * 3,
                    (*ws, *gs, *ms, *vs))
    return outs[:n], outs[n:2 * n], outs[2 * n:]


def _rows1024(a, rows):
    flat = a.reshape(-1, D)
    return jnp.pad(flat, ((0, rows - flat.shape[0]), (0, 0)))


def _pack_shards(even_w_in, even_w_out, odd_w_in, q_b, kv_b, odd_w_out, ffn_w_gate, ffn_w_up, ffn_w_down):
    mix0 = jnp.concatenate([even_w_in[0].T, jnp.zeros((OFF_EOUT - N_EIN, D), F32), even_w_out[0]], axis=0)
    gu = [jnp.concatenate([ffn_w_gate[layer].T, ffn_w_up[layer].T], axis=0) for layer in range(2)]
    mix1 = jnp.concatenate([jnp.pad(odd_w_in[0], ((0, 0), (0, D - ODD_IN))), odd_w_out[0],
                            _rows1024(q_b[0].T, N_QB), _rows1024(kv_b[0].T, N_KVB),
                            jnp.zeros((R_MIX1 - OFF_KVB - N_KVB, D), F32)], axis=0)
    return [c.astype(BF16) for c in (mix0, gu[0], ffn_w_down[0], mix1, gu[1], ffn_w_down[1])]


def _pad_heads(a):
    k = a.shape[1]
    return jnp.pad(a.reshape(HEADS, QK_DIM, k), ((0, 0), (0, HP - QK_DIM), (0, 0))).reshape(HEADS * HP, k)


def _small_pack(parts):
    flat = []
    for p in parts:
        v = p.reshape(-1)
        flat.append(jnp.pad(v, (0, (-v.shape[0]) % 1024)))
    return jnp.concatenate(flat).reshape(-1, 128)


def _small_unpack(buf, shapes):
    flat = buf.reshape(-1)
    out, off = [], 0
    for s in shapes:
        size = int(np.prod(s))
        out.append(flat[off:off + size].reshape(s))
        off += size + (-size) % 1024
    return out


def _step(x3d, positions, target3d, chunks, tile, where, mix_norm, ffn_norm, sg_ln_g, sg_w_s, sg_b_s,
          pool_w, q_norm, k_norm):
    bsz, seq, _ = x3d.shape
    t = bsz * seq
    x0 = x3d.reshape(t, D)
    target = target3d.reshape(t, D)
    my_mix0, my_gu0, my_d0, my_mix1, my_gu1, my_d1 = chunks

    lane = np.arange(128)
    inv_freq = np.where(lane < QK_ROPE, ROPE_THETA ** (-(2.0 * (lane % 32)) / QK_ROPE), 0.0)
    inv_freq = jnp.asarray(inv_freq.reshape(1, 128), F32)
    (cos, sin), (w_mix0, tiles) = _rope_tables(positions.reshape(t, 1), inv_freq, _gather_comm([my_mix0, tile]))

    conv_w = tiles[:, 0:3, 0:64].transpose(1, 0, 2).reshape(3, SC_W)
    pool_scale = tiles[:, 3, 0:32].reshape(1, POOL_W)
    q_a_norm = tiles[:, 4, 0:48].reshape(1, Q_LORA)
    kv_a_norm = tiles[:, 5, 0:32].reshape(1, KV_LORA)
    ws = sg_w_s[0]
    bst = jnp.pad(sg_b_s[0].T, ((0, 0), (0, 128 - SG_HEADS)))
    cw = jnp.pad(conv_w, ((0, 8 - 3), (0, 0)))
    pw_bd = jax.scipy.linalg.block_diag(*[pool_w[0, g] for g in range(4)]).astype(BF16)
    gq = jnp.pad(q_norm * ATT_SCALE, ((0, 0), (0, HP - QK_DIM)))
    gk = jnp.pad(k_norm, ((0, 0), (0, HP - QK_DIM)))

    (x1, proj_e), (w_gu0,) = _even_fwd(x0, w_mix0, mix_norm[0:1], sg_ln_g, ws, bst, cw, seq, _gather_comm([my_gu0]))
    (g0, u0, act0), (w_d0, w_mix1) = _ffn_up(x1, w_gu0, ffn_norm[0:1], "ffn_up0", _gather_comm([my_d0, my_mix1]))
    (x2,), (w_d1,) = _ffn_down(x1, act0, w_d0, "ffn_down0", _gather_comm([my_d1]))
    qbt = _pad_heads(w_mix1[:, OFF_QB:OFF_QB + N_QB_USED, :].reshape(HEADS * QK_DIM, Q_LORA))
    kvbt = w_mix1[:, OFF_KVB:OFF_KVB + N_KVB, :].reshape(HEADS * HP, KV_LORA)
    (proj_o, q, kv, kr, c_out), _ = _odd_pre_fwd(x2, w_mix1, mix_norm[1:2], qbt, kvbt, q_a_norm, kv_a_norm,
                                                pw_bd, pool_scale, seq)
    (d_out, lse), (w_gu1,) = _attn_fwd(q, kv, kr, cos, sin, gq, gk, seq, _gather_comm([my_gu1]))
    x3, dy, g1, u1, loss_tile = _last_block_fwd(x2, c_out, d_out, w_mix1, w_gu1, w_d1, ffn_norm[1:2], target)

    def chunk(rows, padded=False):
        return jnp.zeros((N_DEV, rows, D), BF16) if padded else lax.empty((N_DEV, rows, D), BF16)

    (dx3, act1, dg1, du1, h3, dgam_f1, dmix_o), _ = _ffn_bwd(x3, g1, u1, dy, w_gu1, w_d1, ffn_norm[1:2], "ffn_bwd1",
                                                           None, w_mix1)
    gp_ffn1 = _tn([dg1], h3, 1408, "dw_gate1", (chunk(R_GU + N_FF), N_FF, OFF_GATE))
    gp_ffn1 = _tn([du1], h3, 1408, "dw_up1", (gp_ffn1, N_FF, OFF_UP))
    gp_ffn1 = _tn([act1], dy, 1408, "dw_down1", (gp_ffn1, N_FF, R_GU))

    gp_mix1, (ga_ffn1,) = _tn([c_out, d_out], dx3, D, "dw_oout", (chunk(R_MIX1, True), N_SQ, OFF_OOUT),
                              _pair_exchange_comm(gp_ffn1))
    pb_ffn1 = _rs_pair_sum(gp_ffn1, ga_ffn1, where, "rs_pair_sum_ffn1")
    (dq, dkv, dkr, dgq, dgk), (gb_ffn1,) = _attn_bwd(q, kv, kr, cos, sin, gq, gk, dmix_o, d_out, lse, seq,
                                                    _chip_exchange_comm(pb_ffn1))
    (dx2, dproj_o, h2, qn, kvn, dgam_m1, dqa, dkva, dpw_bd, dps) = _odd_pre_bwd(
        x2, proj_o, dx3, dmix_o, dq, dkv, dkr, w_mix1, mix_norm[1:2], qbt, kvbt, q_a_norm, kv_a_norm, pw_bd,
        pool_scale, seq)
    gp_mix1 = _tn([h2], dproj_o, D, "dw_oin", (gp_mix1, N_SQ, OFF_OIN))
    d_qbt = _tn([dq], qn, HEADS * HP, "dw_qb")
    d_qb_rows = d_qbt.reshape(HEADS, HP, Q_LORA)[:, :QK_DIM].reshape(N_DEV, N_QB_USED, D)
    d_kvb_rows = _tn([dkv], kvn, HEADS * HP, "dw_kvb").reshape(N_DEV, N_KVB, D)
    gp_mix1 = lax.dynamic_update_slice(gp_mix1, d_qb_rows, (0, OFF_QB, 0))
    gp_mix1 = lax.dynamic_update_slice(gp_mix1, d_kvb_rows, (0, OFF_KVB, 0))

    (dx1, act0, dg0, du0, h1, dgam_f0), (ga_mix1,) = _ffn_bwd(x1, g0, u0, dx2, w_gu0, w_d0, ffn_norm[0:1], "ffn_bwd0",
                                                             _pair_exchange_comm(gp_mix1))
    pb_mix1 = _rs_pair_sum(gp_mix1, ga_mix1, where, "rs_pair_sum_mix1")
    gp_ffn0a, (gb_mix1,) = _tn([dg0], h1, 1408, "dw_gate0", (chunk(R_GU), N_FF, OFF_GATE),
                               _chip_exchange_comm(pb_mix1))
    gp_ffn0a = _tn([du0], h1, 1408, "dw_up0", (gp_ffn0a, N_FF, OFF_UP))
    gp_ffn0b, (ga_ffn0a,) = _tn([act0], dx2, 1408, "dw_down0", (chunk(N_FF), N_FF, 0),
                                _pair_exchange_comm(gp_ffn0a))
    pb_ffn0a = _rs_pair_sum(gp_ffn0a, ga_ffn0a, where, "rs_pair_sum_ffn0a")

    (dx0, dproj_e, mix_e, h0, dgam_m0, dws, dbc, dlng, dcw), (gb_ffn0a, ga_ffn0b) = _even_bwd(
        x0, proj_e, dx1, w_mix0, mix_norm[0:1], sg_ln_g, ws, bst, cw, seq,
        _both(_chip_exchange_comm(pb_ffn0a), _pair_exchange_comm(gp_ffn0b)))
    pb_ffn0b = _rs_pair_sum(gp_ffn0b, ga_ffn0b, where, "rs_pair_sum_ffn0b")

    small = _small_pack([
        jnp.concatenate([dgam_m0, dgam_m1], 0), jnp.concatenate([dgam_f0, dgam_f1], 0), dlng,
        dws[None], dbc[:, :SG_HEADS].T[None], dcw[:3],
        jnp.stack([dpw_bd[g * POOL_GD:(g + 1) * POOL_GD, g * POOL_GD:(g + 1) * POOL_GD] for g in range(4)])[None],
        dps, dqa, dkva, dgq[:, :QK_DIM] * ATT_SCALE, dgk[:, :QK_DIM], loss_tile[0:1, 0:1]])
    gp_mix0, (small_all,) = _tn([mix_e], dx1, D, "dw_eout", (chunk(R_MIX0, True), N_SQ, OFF_EOUT),
                                _gather_comm([small]))
    gp_mix0, (gb_ffn0b,) = _tn([dproj_e], h0, 1280, "dw_ein", (gp_mix0, N_EIN, OFF_EIN),
                               _chip_exchange_comm(pb_ffn0b))
    small_sum = _small_unpack(_sum_gathered(small_all), SMALL_SHAPES)
    partials = ([pb_ffn0a, pb_ffn0b, pb_mix1, pb_ffn1], [gb_ffn0a, gb_ffn0b, gb_mix1, gb_ffn1])
    return dx0.reshape(bsz, seq, D), partials, gp_mix0, small_sum


SMALL_SHAPES = [(2, D), (2, D), (1, SG_W), (1, SG_HEADS, 128, 128), (1, SG_HEADS, 128), (3, SC_W),
                (1, 4, POOL_GD, POOL_GD), (1, POOL_W), (1, Q_LORA), (1, KV_LORA), (1, QK_DIM), (1, QK_DIM), (1, 1)]


def kernel(x, positions, mix_norm, ffn_norm, even_w_in, sg_ln_g, sg_w_s, sg_b_s, sc_conv_w, even_w_out, odd_w_in, pool_w, pool_scale, q_a_norm, q_b, kv_a_norm, kv_b, q_norm, k_norm, odd_w_out, ffn_w_gate, ffn_w_up, ffn_w_down, loss_target, m_mix_norm, m_ffn_norm, m_even_w_in, m_sg_ln_g, m_sg_w_s, m_sg_b_s, m_sc_conv_w, m_even_w_out, m_odd_w_in, m_pool_w, m_pool_scale, m_q_a_norm, m_q_b, m_kv_a_norm, m_kv_b, m_q_norm, m_k_norm, m_odd_w_out, m_ffn_w_gate, m_ffn_w_up, m_ffn_w_down, v_mix_norm, v_ffn_norm, v_even_w_in, v_sg_ln_g, v_sg_w_s, v_sg_b_s, v_sc_conv_w, v_even_w_out, v_odd_w_in, v_pool_w, v_pool_scale, v_q_a_norm, v_q_b, v_kv_a_norm, v_kv_b, v_q_norm, v_k_norm, v_odd_w_out, v_ffn_w_gate, v_ffn_w_up, v_ffn_w_down):
    xi, yi, ci = _place()
    me = 4 * xi + 2 * yi + ci

    chunks = _pack_shards(even_w_in, even_w_out, odd_w_in, q_b, kv_b, odd_w_out, ffn_w_gate, ffn_w_up, ffn_w_down)

    def lane_pad(a):
        return jnp.pad(a, ((0, 0), (0, 128 - a.shape[1])))

    tile = jnp.concatenate([lane_pad(sc_conv_w[0]), lane_pad(pool_scale), lane_pad(q_a_norm), lane_pad(kv_a_norm),
                            jnp.zeros((2, 128), F32)], axis=0)
    chip = 2 * xi + yi
    where = jnp.stack([ci, chip, chip ^ 2, chip ^ 1, chip ^ 3]).astype(jnp.int32)
    grad_x, (pbs, gbs), gp_mix0, tot = _step(
        x, positions, loss_target, chunks, tile, where, mix_norm, ffn_norm, sg_ln_g, sg_w_s, sg_b_s,
        pool_w, q_norm, k_norm)

    (ga_mix0,) = _comm_alone(_pair_exchange_comm(gp_mix0), "rs_pair_exchange_mix0")
    pb_mix0 = _rs_pair_sum(gp_mix0, ga_mix0, where, "rs_pair_sum_mix0")
    mix0_sems, pb_mix0, land_mix0, started = _chip_exchange_start(pb_mix0)
    gsh_ffn0a, gsh_ffn0b, gsh_mix1, gsh_ffn1 = _rs_final_sums(pbs, gbs, "rs_final_sums", started)

    (g_mix, g_ffn, g_lng, g_ws, g_bs, g_cw_full, g_pw, g_ps_full, g_qa_full, g_kva_full, g_qn, g_kn, loss) = tot
    g_cw = lax.dynamic_slice_in_dim(g_cw_full, me * 64, 64, axis=1)[None]
    g_ps = lax.dynamic_slice_in_dim(g_ps_full, me * 32, 32, axis=1)
    g_qa = lax.dynamic_slice_in_dim(g_qa_full, me * 48, 48, axis=1)
    g_kva = lax.dynamic_slice_in_dim(g_kva_full, me * 32, 32, axis=1)

    def tr(a):
        return jnp.swapaxes(a, -1, -2)

    g_gate = tr(jnp.stack([gsh_ffn0a[OFF_GATE:OFF_GATE + N_FF], gsh_ffn1[OFF_GATE:OFF_GATE + N_FF]]))
    g_up = tr(jnp.stack([gsh_ffn0a[OFF_UP:OFF_UP + N_FF], gsh_ffn1[OFF_UP:OFF_UP + N_FF]]))
    g_down = jnp.stack([gsh_ffn0b, gsh_ffn1[R_GU:R_GU + N_FF]])
    g_oin = gsh_mix1[OFF_OIN:OFF_OIN + N_SQ, :ODD_IN][None]
    g_oout = gsh_mix1[OFF_OOUT:OFF_OOUT + N_SQ][None]
    g_qb = tr(gsh_mix1[OFF_QB:OFF_QB + N_QB_USED].reshape(1, 144, Q_LORA))
    g_kvb = tr(gsh_mix1[OFF_KVB:OFF_KVB + N_KVB].reshape(1, 192, KV_LORA))
    transposed = ("even_w_in", "odd_w_in", "q_b", "kv_b", "ffn_w_gate", "ffn_w_up")

    names = ("mix_norm", "ffn_norm", "even_w_in", "sg_ln_g", "sg_w_s", "sg_b_s", "sc_conv_w", "even_w_out",
             "odd_w_in", "pool_w", "pool_scale", "q_a_norm", "q_b", "kv_a_norm", "kv_b", "q_norm", "k_norm",
             "odd_w_out", "ffn_w_gate", "ffn_w_up", "ffn_w_down")
    grads = dict(mix_norm=g_mix, ffn_norm=g_ffn, sg_ln_g=g_lng, sg_w_s=g_ws, sg_b_s=g_bs,
                 sc_conv_w=g_cw, odd_w_in=g_oin, pool_w=g_pw, pool_scale=g_ps, q_a_norm=g_qa,
                 q_b=g_qb, kv_a_norm=g_kva, kv_b=g_kvb, q_norm=g_qn, k_norm=g_kn, odd_w_out=g_oout,
                 ffn_w_gate=g_gate, ffn_w_up=g_up, ffn_w_down=g_down)
    weights = dict(mix_norm=mix_norm, ffn_norm=ffn_norm, even_w_in=even_w_in, sg_ln_g=sg_ln_g, sg_w_s=sg_w_s,
                   sg_b_s=sg_b_s, sc_conv_w=sc_conv_w, even_w_out=even_w_out, odd_w_in=odd_w_in, pool_w=pool_w,
                   pool_scale=pool_scale, q_a_norm=q_a_norm, q_b=q_b, kv_a_norm=kv_a_norm, kv_b=kv_b, q_norm=q_norm,
                   k_norm=k_norm, odd_w_out=odd_w_out, ffn_w_gate=ffn_w_gate, ffn_w_up=ffn_w_up,
                   ffn_w_down=ffn_w_down)
    m_in = dict(mix_norm=m_mix_norm, ffn_norm=m_ffn_norm, even_w_in=m_even_w_in, sg_ln_g=m_sg_ln_g, sg_w_s=m_sg_w_s,
                sg_b_s=m_sg_b_s, sc_conv_w=m_sc_conv_w, even_w_out=m_even_w_out, odd_w_in=m_odd_w_in,
                pool_w=m_pool_w, pool_scale=m_pool_scale, q_a_norm=m_q_a_norm, q_b=m_q_b, kv_a_norm=m_kv_a_norm,
                kv_b=m_kv_b, q_norm=m_q_norm, k_norm=m_k_norm, odd_w_out=m_odd_w_out, ffn_w_gate=m_ffn_w_gate,
                ffn_w_up=m_ffn_w_up, ffn_w_down=m_ffn_w_down)
    v_in = dict(mix_norm=v_mix_norm, ffn_norm=v_ffn_norm, even_w_in=v_even_w_in, sg_ln_g=v_sg_ln_g, sg_w_s=v_sg_w_s,
                sg_b_s=v_sg_b_s, sc_conv_w=v_sc_conv_w, even_w_out=v_even_w_out, odd_w_in=v_odd_w_in,
                pool_w=v_pool_w, pool_scale=v_pool_scale, q_a_norm=v_q_a_norm, q_b=v_q_b, kv_a_norm=v_kv_a_norm,
                kv_b=v_kv_b, q_norm=v_q_norm, k_norm=v_k_norm, odd_w_out=v_odd_w_out, ffn_w_gate=v_ffn_w_gate,
                ffn_w_up=v_ffn_w_up, ffn_w_down=v_ffn_w_down)
    delta, new_m, new_v = {}, {}, {}

    def as2d(k, a):
        a = tr(a) if k in transposed else a
        return a.reshape(-1, a.shape[-1])

    def back(k, a):
        shape = weights[k].shape
        return tr(a.reshape(shape[:-2] + (shape[-1], shape[-2]))) if k in transposed else a.reshape(shape)

    def update(group, name, nblk=1):
        outs = _adamw([as2d(k, weights[k]) for k in group], [as2d(k, grads[k]) for k in group],
                      [as2d(k, m_in[k]) for k in group], [as2d(k, v_in[k]) for k in group], name, nblk)
        for i, k in enumerate(group):
            delta[k], new_m[k], new_v[k] = (back(k, o[i]) for o in outs)

    update(["ffn_w_gate", "ffn_w_up", "ffn_w_down"], "adamw_ffn", 4)
    update(["odd_w_in", "odd_w_out"], "adamw_mix1", 2)
    update([k for k in names if k not in delta and k not in ("even_w_in", "even_w_out")], "adamw_small")

    pb_mix0, gb_mix0 = _chip_exchange_wait(mix0_sems, pb_mix0, land_mix0, new_v["k_norm"])
    (gsh_mix0,) = _rs_final_sums([pb_mix0], [gb_mix0], "rs_final_sum_mix0")
    grads["even_w_in"] = tr(gsh_mix0[OFF_EIN:OFF_EIN + N_EIN][None])
    grads["even_w_out"] = gsh_mix0[OFF_EOUT:OFF_EOUT + N_SQ][None]
    update(["even_w_in", "even_w_out"], "adamw_mix0", 2)

    return (loss.reshape(()), grad_x, *[grads[k] for k in names], *[delta[k] for k in names],
            *[new_m[k] for k in names], *[new_v[k] for k in names])
```

```python
import functools

import numpy as np
import jax
import jax.numpy as jnp
from jax import lax
from jax.experimental import pallas as pl
from jax.experimental.pallas import tpu as pltpu

F32 = jnp.float32
BF16 = jnp.bfloat16
MESH = pl.DeviceIdType.MESH

D = 1024
EPS = 1e-6
NEG_INF = -1e30
SG_HEADS, SG_HD, SG_W, SG_CHUNK = 4, 128, 512, 128
SC_W = 512
EVEN_IN = 2560
POOL_W = 256
POOL_GD = 64
Q_LORA, KV_LORA, QK_ROPE, QK_NOPE, V_DIM = 384, 256, 64, 128, 128
QK_DIM = QK_NOPE + QK_ROPE
HEADS = 6
HP = 256
ODD_IN = 960
D_FF = 2816
ROPE_THETA = 10000.0
ATT_SCALE = QK_DIM ** -0.5
LR, B1, B2, ADAM_EPS, WD, STEP = 0.001, 0.9, 0.999, 1e-08, 0.01, 10

N_DEV = 8
TB = 512
TB_FFN_BWD = 256
TK_DW = 1024
HALO = 16
VMEM_LIMIT = 56 * 1024 * 1024

N_EIN, N_FF, N_SQ = 320, 352, 128
OFF_EIN, OFF_EOUT, R_MIX0 = 0, 384, 512
OFF_GATE, OFF_UP, R_GU = 0, 352, 704
OFF_OIN, OFF_OOUT, OFF_QB, OFF_KVB, R_MIX1 = 0, 128, 256, 320, 384
N_QB, N_QB_USED, N_KVB = 64, 54, 48

INV_SQRT2 = 0.7071067811865476
INV_SQRT_2PI = 0.3989422804014327


def _dot(a, b, ca, cb):
    return lax.dot_general(a, b, (((ca,), (cb,)), ((), ())), preferred_element_type=F32)


def _cparams(n_axes=1):
    return pltpu.CompilerParams(dimension_semantics=("arbitrary",) * n_axes, vmem_limit_bytes=VMEM_LIMIT)


def _wspec(n, off):
    assert off % n == 0
    idx = off // n
    return pl.BlockSpec((N_DEV, n, D), lambda i: (0, idx, 0), pipeline_mode=pl.Buffered(1))


def _const_spec(shape):
    zeros = (0,) * len(shape)
    return pl.BlockSpec(shape, lambda *_: zeros)


class _Comm:
    def __init__(self, ins, out_shapes, sems, start, wait, mid=None):
        self.ins, self.out_shapes, self.sems, self.start, self.wait, self.mid = ins, out_shapes, sems, start, wait, mid


def _both(c1, c2):
    def split(f1, f2):
        def run(ins, outs, sems):
            f1(ins[:len(c1.ins)], outs[:len(c1.out_shapes)], sems[:len(c1.sems)])
            f2(ins[len(c1.ins):], outs[len(c1.out_shapes):], sems[len(c1.sems):])
        return run

    assert c1.mid is None and c2.mid is None
    return _Comm(c1.ins + c2.ins, c1.out_shapes + c2.out_shapes, c1.sems + c2.sems,
                 split(c1.start, c2.start), split(c1.wait, c2.wait))


def _call(body, name, grid, in_specs, out_specs, out_shape, args, scratch_shapes=(), comm=None, aliases=None):
    n_axes = len(grid)
    aliases = aliases or {}
    if comm is None:
        res = pl.pallas_call(
            body, name=name, grid=grid, in_specs=list(in_specs), out_specs=list(out_specs),
            out_shape=list(out_shape), scratch_shapes=list(scratch_shapes), input_output_aliases=aliases,
            compiler_params=_cparams(n_axes))(*args)
        return list(res), []
    ni, no, ns = len(in_specs), len(out_specs), len(scratch_shapes)
    ci, co = len(comm.ins), len(comm.out_shapes)
    n_steps = int(np.prod(grid))

    def carrier(*refs):
        ins, cin = refs[:ni], refs[ni:ni + ci]
        outs, cout = refs[ni + ci:ni + ci + no], refs[ni + ci + no:ni + ci + no + co]
        scr, sems = refs[ni + ci + no + co:ni + ci + no + co + ns], refs[ni + ci + no + co + ns:]
        step = 0
        for a in range(n_axes):
            step = step * grid[a] + pl.program_id(a)

        @pl.when(step == 0)
        def _():
            comm.start(cin, cout, sems)

        body(*ins, *outs, *scr)

        if comm.mid is not None and n_steps >= 4:
            @pl.when(step == (3 * n_steps) // 4)
            def _():
                comm.mid(cin, cout, sems)

        @pl.when(step == n_steps - 1)
        def _():
            if comm.mid is not None and n_steps < 4:
                comm.mid(cin, cout, sems)
            comm.wait(cin, cout, sems)

    any_spec = pl.BlockSpec(memory_space=pl.ANY)
    res = pl.pallas_call(
        carrier, name=name, grid=grid, in_specs=list(in_specs) + [any_spec] * ci,
        out_specs=list(out_specs) + [any_spec] * co, out_shape=list(out_shape) + list(comm.out_shapes),
        scratch_shapes=list(scratch_shapes) + list(comm.sems), input_output_aliases=aliases,
        compiler_params=_cparams(n_axes))(*args, *comm.ins)
    return list(res[:no]), list(res[no:])


def _comm_alone(comm, name):
    ci, co = len(comm.ins), len(comm.out_shapes)

    def body(*refs):
        cin, cout, sems = refs[:ci], refs[ci:ci + co], refs[ci + co:]
        comm.start(cin, cout, sems)
        if comm.mid is not None:
            comm.mid(cin, cout, sems)
        comm.wait(cin, cout, sems)

    any_spec = pl.BlockSpec(memory_space=pl.ANY)
    res = pl.pallas_call(
        body, name=name, out_shape=list(comm.out_shapes), in_specs=[any_spec] * ci, out_specs=[any_spec] * co,
        scratch_shapes=list(comm.sems))(*comm.ins)
    return list(res)


def _rms(x, g):
    r = lax.rsqrt(jnp.mean(x * x, axis=-1, keepdims=True) + EPS)
    return x * r * g, r


def _rms_bwd(x, r, g, dy):
    xh = x * r
    dxh = dy * g
    dx = r * (dxh - xh * jnp.mean(dxh * xh, axis=-1, keepdims=True))
    dg = jnp.sum(dy * xh, axis=0, keepdims=True)
    return dx, dg


def _gelu(x):
    return 0.5 * x * (1.0 + lax.erf(x * INV_SQRT2))


def _gelu_grad(x):
    return 0.5 * (1.0 + lax.erf(x * INV_SQRT2)) + x * jnp.exp(-0.5 * x * x) * INV_SQRT_2PI


def _shift_down(a, k):
    rows = lax.broadcasted_iota(jnp.int32, a.shape, 0)
    return jnp.where(rows >= k, pltpu.roll(a, k, 0), 0.0)


def _shift_up(a, k):
    n = a.shape[0]
    rows = lax.broadcasted_iota(jnp.int32, a.shape, 0)
    return jnp.where(rows < n - k, pltpu.roll(a, n - k, 0), 0.0)


def _tril_bf16(w):
    r = lax.broadcasted_iota(jnp.int32, w.shape, 0)
    c = lax.broadcasted_iota(jnp.int32, w.shape, 1)
    return jnp.where(r >= c, w, 0.0).astype(BF16)


def _ln_head(vh, g):
    mu = jnp.mean(vh, axis=-1, keepdims=True)
    xc = vh - mu
    rr = lax.rsqrt(jnp.mean(xc * xc, axis=-1, keepdims=True) + EPS)
    xh = xc * rr
    return xh * g, xh, rr


def _conv_fwd(z, tail, cw_ref):
    ext = jnp.concatenate([tail, z], axis=0)
    zs1 = _shift_down(ext, 1)[HALO:]
    zs2 = _shift_down(ext, 2)[HALO:]
    y = cw_ref[2:3, :] * z + cw_ref[1:2, :] * zs1 + cw_ref[0:1, :] * zs2
    return y, zs1, zs2


def _pool_cnt(shape, blk_in_seq):
    rows = lax.broadcasted_iota(jnp.int32, shape, 0)
    grp = lax.broadcasted_iota(jnp.int32, shape, 1) // POOL_GD
    win = jnp.where(grp == 0, 2, jnp.where(grp == 1, 4, jnp.where(grp == 2, 8, 16)))
    tpos = blk_in_seq * shape[0] + rows + 1
    return jnp.minimum(tpos, win).astype(F32), grp


def _pool_select(grp, s2, s4, s8, s16):
    return jnp.where(grp == 0, s2, jnp.where(grp == 1, s4, jnp.where(grp == 2, s8, s16)))


def _pool_fwd(z, tail, blk_in_seq):
    ext = jnp.concatenate([tail, z], axis=0)
    s2 = ext + _shift_down(ext, 1)
    s4 = s2 + _shift_down(s2, 2)
    s8 = s4 + _shift_down(s4, 4)
    s16 = s8 + _shift_down(s8, 8)
    cnt, grp = _pool_cnt(z.shape, blk_in_seq)
    sums = _pool_select(grp, s2[HALO:], s4[HALO:], s8[HALO:], s16[HALO:])
    return sums / cnt - z, cnt, grp


def _pool_bwd(dpooled, dpm, head, grp):
    n = dpm.shape[0]
    ext = jnp.concatenate([dpm, head], axis=0)
    u2 = ext + _shift_up(ext, 1)
    u4 = u2 + _shift_up(u2, 2)
    u8 = u4 + _shift_up(u4, 4)
    u16 = u8 + _shift_up(u8, 8)
    return _pool_select(grp, u2[:n], u4[:n], u8[:n], u16[:n]) - dpooled


def _lane_sums(a):
    return _dot(a.astype(BF16), jnp.ones((a.shape[1], a.shape[1]), BF16), 1, 0)


def _swap_halves(y1):
    src = lax.broadcasted_iota(jnp.int32, (128, 128), 0)
    dst = lax.broadcasted_iota(jnp.int32, (128, 128), 1)
    perm = jnp.where(((dst < 32) & (src == dst + 32)) | ((dst >= 32) & (dst < QK_ROPE) & (src == dst - 32)), 1.0, 0.0)
    return _dot(y1.astype(BF16), perm.astype(BF16), 1, 0)


def _rope(y1, c, s):
    return y1 * c + _swap_halves(y1) * s


def _rope_bwd(d1, c, s):
    return d1 * c + _swap_halves(d1 * s)


def _qk_prep(x, g, c, s):
    r = lax.rsqrt(_lane_sums(x * x) * (1.0 / QK_DIM) + EPS)
    y = x * r * g
    return jnp.concatenate([y[:, :128], _rope(y[:, 128:], c, s)], axis=1), r


def _qk_prep_bwd(dout, x, r, g, c, s):
    dy = jnp.concatenate([dout[:, :128], _rope_bwd(dout[:, 128:], c, s)], axis=1)
    xh = x * r
    dxh = dy * g
    dx = r * (dxh - xh * (_lane_sums(dxh * xh) * (1.0 / QK_DIM)))
    return dx, jnp.sum(dy * xh, axis=0, keepdims=True)


def _place():
    return lax.axis_index("x"), lax.axis_index("y"), lax.axis_index("c")


def _gather_comm(arrs):
    n = len(arrs)

    def halves(a):
        rows = arrs[a].shape[0]
        tile = 16 if arrs[a].dtype == BF16 else 8
        top = rows // 2 if rows % (2 * tile) == 0 else rows
        return (0, top), (top, rows - top)

    def plan(ins, outs, sems):
        send_sems, recv_sems, local_sems = sems
        x, y, c = _place()
        me, sib, xn, yn, dg = (x, y, c), (x, y, 1 - c), (1 - x, y, c), (x, 1 - y, c), (1 - x, 1 - y, c)

        def slot(a, dev, part=None):
            ref = outs[a].at[4 * dev[0] + 2 * dev[1] + dev[2]]
            return ref if part is None else ref.at[pl.ds(part[0], part[1])]

        def copy(a, k, block, to, src=None, part=None):
            return pltpu.make_async_remote_copy(
                src_ref=slot(a, block, part) if src is None else src, dst_ref=slot(a, block, part),
                send_sem=send_sems.at[a, k], recv_sem=recv_sems.at[a, k], device_id=to, device_id_type=MESH)

        local = [pltpu.make_async_copy(ins[a], slot(a, me), local_sems.at[a]) for a in range(n)]
        return me, sib, xn, yn, dg, copy, local

    def start(ins, outs, sems):
        me, sib, xn, yn, _, copy, local = plan(ins, outs, sems)
        for a in range(n):
            local[a].start()
            for k, to in enumerate((sib, xn, yn)):
                copy(a, k, me, to, src=ins[a]).start()

    def mid(ins, outs, sems):
        me, sib, xn, yn, _, copy, _ = plan(ins, outs, sems)
        for a in range(n):
            top, bottom = halves(a)
            copy(a, 1, xn, me).wait_recv()
            copy(a, 3, xn, yn, part=top).start()
            copy(a, 5, xn, sib).start()
            copy(a, 2, yn, me).wait_recv()
            if bottom[1]:
                copy(a, 4, yn, xn, part=bottom).start()
            copy(a, 6, yn, sib).start()

    def wait(ins, outs, sems):
        me, sib, xn, yn, dg, copy, local = plan(ins, outs, sems)
        other = lambda dev: (dev[0], dev[1], 1 - dev[2])
        for a in range(n):
            top, bottom = halves(a)
            copy(a, 3, dg, me, part=top).wait_recv()
            if bottom[1]:
                copy(a, 4, dg, me, part=bottom).wait_recv()
            copy(a, 7, dg, sib).start()
        for a in range(n):
            top, bottom = halves(a)
            for k, block in ((0, sib), (5, other(xn)), (6, other(yn)), (7, other(dg))):
                copy(a, k, block, me).wait_recv()
            for k, block in ((0, me), (1, me), (2, me), (5, xn), (6, yn), (7, dg)):
                copy(a, k, block, me, src=ins[a] if k < 3 else None).wait_send()
            copy(a, 3, xn, me, part=top).wait_send()
            if bottom[1]:
                copy(a, 4, yn, me, part=bottom).wait_send()
            local[a].wait()

    return _Comm(
        list(arrs), [jax.ShapeDtypeStruct((N_DEV,) + a.shape, a.dtype) for a in arrs],
        [pltpu.SemaphoreType.DMA((n, 8)), pltpu.SemaphoreType.DMA((n, 8)), pltpu.SemaphoreType.DMA((n,))],
        start, wait, mid)


def _sum_gathered(g):
    rows = g.shape[1]

    def body(g_ref, sum_ref):
        total = g_ref[0]
        for d in range(1, N_DEV):
            total = total + g_ref[d]
        sum_ref[...] = total

    return pl.pallas_call(
        body, name="sum_gathered_small", out_shape=jax.ShapeDtypeStruct((rows, 128), F32), grid=(1,),
        in_specs=[pl.BlockSpec((N_DEV, rows, 128), lambda i: (0, 0, 0))],
        out_specs=pl.BlockSpec((rows, 128), lambda i: (0, 0)), compiler_params=_cparams(1),
    )(g)


def _sum_rows(rows):
    return rows if rows <= 512 else rows // 2


def _pair_exchange_comm(gp):
    _, rows, cols = gp.shape

    def copies(ins, outs, sems):
        send_sems, recv_sems = sems
        x, y, c = _place()
        return [pltpu.make_async_remote_copy(
            src_ref=ins[0].at[2 * j + (1 - c)], dst_ref=outs[0].at[j], send_sem=send_sems.at[j],
            recv_sem=recv_sems.at[j], device_id=(x, y, 1 - c), device_id_type=MESH) for j in range(4)]

    def start(ins, outs, sems):
        for cp in copies(ins, outs, sems):
            cp.start()

    def wait(ins, outs, sems):
        for cp in copies(ins, outs, sems):
            cp.wait()

    return _Comm([gp], [jax.ShapeDtypeStruct((4, rows, cols), gp.dtype)],
                 [pltpu.SemaphoreType.DMA((4,)), pltpu.SemaphoreType.DMA((4,))], start, wait)


def _rs_pair_sum(gp, got, where, name):
    _, rows, cols = got.shape
    rb = _sum_rows(rows)
    gp4 = gp.reshape(4, 2, rows, cols)

    def body(w_ref, a_ref, b_ref, o_ref):
        o_ref[0] = (a_ref[0, 0].astype(F32) + b_ref[0].astype(F32)).astype(o_ref.dtype)

    return pl.pallas_call(
        body, name=name, out_shape=jax.ShapeDtypeStruct((4, rows, cols), gp.dtype),
        grid_spec=pltpu.PrefetchScalarGridSpec(
            num_scalar_prefetch=1, grid=(4, rows // rb),
            in_specs=[pl.BlockSpec((1, 1, rb, cols), lambda k, r, w: (w[1 + k], w[0], r, 0)),
                      pl.BlockSpec((1, rb, cols), lambda k, r, w: (w[1 + k], r, 0))],
            out_specs=pl.BlockSpec((1, rb, cols), lambda k, r, w: (k, r, 0))),
        compiler_params=_cparams(2),
    )(where, gp4, got)


def _chip_exchange_comm(pb):
    _, rows, cols = pb.shape

    def copies(ins, outs, sems):
        send_sems, recv_sems = sems
        x, y, c = _place()
        chips = [(1 - x, y), (x, 1 - y), (1 - x, 1 - y)]
        return [pltpu.make_async_remote_copy(
            src_ref=ins[0].at[1 + k], dst_ref=outs[0].at[k], send_sem=send_sems.at[k],
            recv_sem=recv_sems.at[k], device_id=(px, py, c), device_id_type=MESH)
            for k, (px, py) in enumerate(chips)]

    def start(ins, outs, sems):
        for cp in copies(ins, outs, sems):
            cp.start()

    def wait(ins, outs, sems):
        for cp in copies(ins, outs, sems):
            cp.wait()

    return _Comm([pb], [jax.ShapeDtypeStruct((3, rows, cols), pb.dtype)],
                 [pltpu.SemaphoreType.DMA((3,)), pltpu.SemaphoreType.DMA((3,))], start, wait)


def _chip_exchange_start(pb):
    _, rows, cols = pb.shape

    def body(pb_ref, land_ref, *rest):
        sems, token = rest[:6], rest[8]
        x, y, c = _place()
        chips = [(1 - x, y), (x, 1 - y), (1 - x, 1 - y)]
        for k, (px, py) in enumerate(chips):
            pltpu.make_async_remote_copy(
                src_ref=pb_ref.at[1 + k], dst_ref=land_ref.at[k], send_sem=sems[k], recv_sem=sems[3 + k],
                device_id=(px, py, c), device_id_type=MESH).start()
        token[...] = jnp.zeros_like(token)

    hbm = pl.BlockSpec(memory_space=pltpu.HBM)
    sem = pl.BlockSpec(memory_space=pltpu.SEMAPHORE)
    land = lax.empty((3, rows, cols), pb.dtype)
    res = pl.pallas_call(
        body, name="rs_chip_exchange_start_mix0",
        out_shape=(*[pltpu.SemaphoreType.DMA(())] * 6, pltpu.HBM(pb.shape, pb.dtype), pltpu.HBM(land.shape, land.dtype),
                   jax.ShapeDtypeStruct((8, 128), F32)),
        in_specs=(hbm, hbm), out_specs=(*[sem] * 6, hbm, hbm, pl.BlockSpec(memory_space=pltpu.VMEM)),
        input_output_aliases={0: 6, 1: 7},
        compiler_params=pltpu.CompilerParams(has_side_effects=pltpu.SideEffectType.DATAFLOW_SIDE_EFFECTING),
    )(pltpu.with_memory_space_constraint(pb, pltpu.HBM), pltpu.with_memory_space_constraint(land, pltpu.HBM))
    return list(res[:6]), res[6], res[7], res[8]


def _chip_exchange_wait(sems, pb_thru, land_thru, after):
    def body(pb_ref, land_ref, *rest):
        sems_in = rest[:6]
        x, y, c = _place()
        chips = [(1 - x, y), (x, 1 - y), (1 - x, 1 - y)]
        for k, (px, py) in enumerate(chips):
            cp = pltpu.make_async_remote_copy(
                src_ref=pb_ref.at[1 + k], dst_ref=land_ref.at[k], send_sem=sems_in[k], recv_sem=sems_in[3 + k],
                device_id=(px, py, c), device_id_type=MESH)
            cp.wait_send()
            cp.wait_recv()

    hbm = pl.BlockSpec(memory_space=pltpu.HBM)
    sem = pl.BlockSpec(memory_space=pltpu.SEMAPHORE)
    res = pl.pallas_call(
        body, name="rs_chip_exchange_wait_mix0",
        out_shape=(pltpu.HBM(pb_thru.shape, pb_thru.dtype), pltpu.HBM(land_thru.shape, land_thru.dtype)),
        in_specs=(hbm, hbm, *[sem] * 6, pl.BlockSpec(memory_space=pl.ANY)), out_specs=(hbm, hbm),
        input_output_aliases={0: 0, 1: 1},
        compiler_params=pltpu.CompilerParams(has_side_effects=pltpu.SideEffectType.DATAFLOW_SIDE_EFFECTING),
    )(pb_thru, land_thru, *sems, after)
    return res[0], res[1]


def _rs_final_sums(pbs, gots, name, after=None):
    n = len(pbs)

    def body(*refs):
        outs = refs[len(refs) - n:]
        for a in range(n):
            m_ref, g_ref, o_ref = refs[a], refs[n + a], outs[a]
            o_ref[...] = ((m_ref[0].astype(F32) + g_ref[0].astype(F32)) + g_ref[1].astype(F32)) + g_ref[2].astype(F32)

    half = [pb.shape[1] // 2 for pb in pbs]
    in_specs = ([pl.BlockSpec((1, h, D), lambda i: (0, i, 0)) for h in half]
                + [pl.BlockSpec((3, h, D), lambda i: (0, i, 0)) for h in half])
    args = (*pbs, *gots)
    if after is not None:
        in_specs, args = in_specs + [pl.BlockSpec(memory_space=pl.ANY)], args + (after,)
    res, _ = _call(body, name, (2,), in_specs, [pl.BlockSpec((h, D), lambda i: (i, 0)) for h in half],
                   [jax.ShapeDtypeStruct(pb.shape[1:], F32) for pb in pbs], args)
    return res


def _rope_tables(pos_col, inv_freq, comm=None):
    t = pos_col.shape[0]

    def body(p_ref, f_ref, c_ref, s_ref):
        ang = p_ref[...].astype(F32) * f_ref[...]
        lane = lax.broadcasted_iota(jnp.int32, ang.shape, 1)
        c_ref[...] = jnp.where(lane < QK_ROPE, jnp.cos(ang), 0.0)
        s = jnp.sin(ang)
        s_ref[...] = jnp.where(lane < 32, -s, jnp.where(lane < QK_ROPE, s, 0.0))

    spec = pl.BlockSpec((TB, 128), lambda i: (i, 0))
    return _call(
        body, "rope_tables", (t // TB,), [pl.BlockSpec((TB, 1), lambda i: (i, 0)), _const_spec((1, 128))],
        [spec] * 2, [jax.ShapeDtypeStruct((t, 128), F32)] * 2, (pos_col, inv_freq), (), comm)


def _sgu_conv_fwd(proj, tail, lng_ref, ws_ref, bst_ref, cw_ref):
    gu = _gelu(proj[:, 0:SG_W])
    gv = _gelu(proj[:, SG_W:2 * SG_W])
    bg = proj[:, 1024:1536]
    z = proj[:, 1536:2048] * proj[:, 2048:2560]
    heads = []
    for h in range(SG_HEADS):
        sl = slice(h * SG_HD, (h + 1) * SG_HD)
        vn, _, _ = _ln_head(gv[:, sl], lng_ref[:, sl])
        vnb = vn.astype(BF16)
        wm = _tril_bf16(ws_ref[h])
        bcol = bst_ref[:, h:h + 1]
        mixed = jnp.concatenate(
            [_dot(wm, vnb[k * SG_CHUNK:(k + 1) * SG_CHUNK], 1, 0) + bcol for k in range(TB // SG_CHUNK)], axis=0)
        heads.append(gu[:, sl] * mixed)
    a_out = jnp.concatenate(heads, axis=1)
    y, _, _ = _conv_fwd(z, tail, cw_ref)
    return a_out, bg * y, z


def _even_fwd(x, wg, gamma, lng, ws, bst, cw, seq, comm=None):
    t = x.shape[0]
    nbs = seq // TB

    def body(x_ref, gam_ref, win_ref, wout_ref, lng_ref, ws_ref, bst_ref, cw_ref, x1_ref, proj_ref, tail_ref):
        i = pl.program_id(0)
        xv = x_ref[...]
        h, _ = _rms(xv, gam_ref[...])
        proj = _dot(h.astype(BF16), win_ref[...].reshape(EVEN_IN, D), 1, 1)
        proj_ref[...] = proj.astype(BF16)
        tail = jnp.where(i % nbs == 0, 0.0, tail_ref[...])
        a_out, b_out, z = _sgu_conv_fwd(proj, tail, lng_ref, ws_ref, bst_ref, cw_ref)
        tail_ref[...] = z[TB - HALO:, :]
        x1_ref[...] = (xv + _dot(a_out.astype(BF16), wout_ref[0:4].reshape(512, D), 1, 0)
                       + _dot(b_out.astype(BF16), wout_ref[4:8].reshape(512, D), 1, 0))

    row = pl.BlockSpec((TB, D), lambda i: (i, 0))
    return _call(
        body, "even_fwd", (t // TB,),
        [row, _const_spec((1, D)), _wspec(N_EIN, OFF_EIN), _wspec(N_SQ, OFF_EOUT), _const_spec((1, SG_W)),
         _const_spec((SG_HEADS, 128, 128)), _const_spec((128, 128)), _const_spec((8, SC_W))],
        [row, pl.BlockSpec((TB, EVEN_IN), lambda i: (i, 0))],
        [jax.ShapeDtypeStruct((t, D), F32), jax.ShapeDtypeStruct((t, EVEN_IN), BF16)],
        (x, gamma, wg, wg, lng, ws, bst, cw), [pltpu.VMEM((HALO, SC_W), F32)], comm)


def _even_bwd(x, proj, dx1, wg, gamma, lng, ws, bst, cw, seq, comm=None):
    t = x.shape[0]
    nb, nbs = t // TB, seq // TB

    def body(x_ref, proj_ref, ptail_ref, dx1_ref, gam_ref, win_ref, wout_ref, lng_ref, ws_ref, bst_ref, cw_ref,
             dx0_ref, dproj_ref, mix_ref, h_ref, dgam_ref, dws_ref, dbc_ref, dlng_ref, dcw_ref, head_ref):
        i = pl.program_id(0)
        blk = nb - 1 - i

        @pl.when(i == 0)
        def _():
            dgam_ref[...] = jnp.zeros_like(dgam_ref)
            dws_ref[...] = jnp.zeros_like(dws_ref)
            dbc_ref[...] = jnp.zeros_like(dbc_ref)
            dlng_ref[...] = jnp.zeros_like(dlng_ref)
            dcw_ref[...] = jnp.zeros_like(dcw_ref)

        xv = x_ref[...]
        gam = gam_ref[...]
        h, r = _rms(xv, gam)
        h_ref[...] = h.astype(BF16)
        dx1 = dx1_ref[...]
        dmix = _dot(dx1.astype(BF16), wout_ref[...].reshape(D, D), 1, 1)
        da, db = dmix[:, :SG_W], dmix[:, SG_W:]
        proj = proj_ref[...].astype(F32)
        u, v = proj[:, 0:SG_W], proj[:, SG_W:2 * SG_W]
        bg, cg, hv = proj[:, 1024:1536], proj[:, 1536:2048], proj[:, 2048:2560]
        gu, gv = _gelu(u), _gelu(v)

        a_heads, dgv_heads = [], []
        for hd in range(SG_HEADS):
            sl = slice(hd * SG_HD, (hd + 1) * SG_HD)
            g_h = lng_ref[:, sl]
            vn, xh, rr = _ln_head(gv[:, sl], g_h)
            vnb = vn.astype(BF16)
            wm = _tril_bf16(ws_ref[hd])
            bcol = bst_ref[:, hd:hd + 1]
            mixed_c, dvn_c = [], []
            dw_acc = jnp.zeros((128, 128), F32)
            db_acc = jnp.zeros((128, 1), F32)
            for k in range(TB // SG_CHUNK):
                rs = slice(k * SG_CHUNK, (k + 1) * SG_CHUNK)
                mixed = _dot(wm, vnb[rs], 1, 0) + bcol
                dmixed = da[rs, sl] * gu[rs, sl]
                dmb = dmixed.astype(BF16)
                dvn_c.append(_dot(wm, dmb, 0, 0))
                dw_acc = dw_acc + _dot(dmb, vnb[rs], 1, 1)
                db_acc = db_acc + jnp.sum(dmixed, axis=1, keepdims=True)
                mixed_c.append(mixed)
            mixed_h = jnp.concatenate(mixed_c, axis=0)
            dvn = jnp.concatenate(dvn_c, axis=0)
            r_i = lax.broadcasted_iota(jnp.int32, (128, 128), 0)
            c_i = lax.broadcasted_iota(jnp.int32, (128, 128), 1)
            dws_ref[hd] += jnp.where(r_i >= c_i, dw_acc, 0.0)
            dbc_ref[:, hd:hd + 1] += db_acc
            dlng_ref[:, sl] += jnp.sum(dvn * xh, axis=0, keepdims=True)
            dxh = dvn * g_h
            dgv = rr * (dxh - jnp.mean(dxh, axis=-1, keepdims=True)
                        - xh * jnp.mean(dxh * xh, axis=-1, keepdims=True))
            a_heads.append(gu[:, sl] * mixed_h)
            dproj_ref[:, sl] = (da[:, sl] * mixed_h * _gelu_grad(u[:, sl])).astype(BF16)
            dgv_heads.append(dgv * _gelu_grad(v[:, sl]))
        dproj_ref[:, SG_W:2 * SG_W] = jnp.concatenate(dgv_heads, axis=1).astype(BF16)
        mix_ref[:, :SG_W] = jnp.concatenate(a_heads, axis=1).astype(BF16)

        z = cg * hv
        pt = ptail_ref[...].astype(F32)
        tail = jnp.where(blk % nbs == 0, 0.0, pt[:, 1536:2048] * pt[:, 2048:2560])
        y, zs1, zs2 = _conv_fwd(z, tail, cw_ref)
        mix_ref[:, SG_W:] = (bg * y).astype(BF16)
        dy = db * bg
        head = jnp.where(blk % nbs == nbs - 1, 0.0, head_ref[...])
        ext = jnp.concatenate([dy, head], axis=0)
        dz = (cw_ref[2:3, :] * dy + cw_ref[1:2, :] * _shift_up(ext, 1)[:TB]
              + cw_ref[0:1, :] * _shift_up(ext, 2)[:TB])
        head_ref[...] = dy[:HALO, :]
        dcw_ref[2:3, :] += jnp.sum(dy * z, axis=0, keepdims=True)
        dcw_ref[1:2, :] += jnp.sum(dy * zs1, axis=0, keepdims=True)
        dcw_ref[0:1, :] += jnp.sum(dy * zs2, axis=0, keepdims=True)
        dproj_ref[:, 1024:1536] = (db * y).astype(BF16)
        dproj_ref[:, 1536:2048] = (dz * hv).astype(BF16)
        dproj_ref[:, 2048:2560] = (dz * cg).astype(BF16)

        dh = _dot(dproj_ref[...], win_ref[...].reshape(EVEN_IN, D), 1, 0)
        dxn, dgam = _rms_bwd(xv, r, gam, dh)
        dgam_ref[...] += dgam
        dx0_ref[...] = dx1 + dxn

    def rev(w):
        return pl.BlockSpec((TB, w), lambda i: (nb - 1 - i, 0))

    ptail = pl.BlockSpec((HALO, EVEN_IN), lambda i: (jnp.maximum((nb - 1 - i) * (TB // HALO) - 1, 0), 0))
    return _call(
        body, "even_bwd", (nb,),
        [rev(D), rev(EVEN_IN), ptail, rev(D), _const_spec((1, D)), _wspec(N_EIN, OFF_EIN),
         _wspec(N_SQ, OFF_EOUT), _const_spec((1, SG_W)), _const_spec((SG_HEADS, 128, 128)),
         _const_spec((128, 128)), _const_spec((8, SC_W))],
        [rev(D), rev(EVEN_IN), rev(D), rev(D), _const_spec((1, D)), _const_spec((SG_HEADS, 128, 128)),
         _const_spec((128, 128)), _const_spec((1, SG_W)), _const_spec((8, SC_W))],
        [jax.ShapeDtypeStruct((t, D), F32), jax.ShapeDtypeStruct((t, EVEN_IN), BF16),
         jax.ShapeDtypeStruct((t, D), BF16), jax.ShapeDtypeStruct((t, D), BF16),
         jax.ShapeDtypeStruct((1, D), F32), jax.ShapeDtypeStruct((SG_HEADS, 128, 128), F32),
         jax.ShapeDtypeStruct((128, 128), F32), jax.ShapeDtypeStruct((1, SG_W), F32),
         jax.ShapeDtypeStruct((8, SC_W), F32)],
        (x, proj, proj, dx1, gamma, wg, wg, lng, ws, bst, cw), [pltpu.VMEM((HALO, SC_W), F32)], comm)


def _last_block_fwd(x, c_out, d_out, w_mix1, w_gu, w_d, gamma, target):
    t = x.shape[0]

    def body(x_ref, c_ref, d_ref, wo_ref, gam_ref, wg_ref, wu_ref, wd_ref, t_ref,
             x3_ref, dy_ref, g_ref, u_ref, loss_ref):
        @pl.when(pl.program_id(0) == 0)
        def _():
            loss_ref[...] = jnp.zeros_like(loss_ref)

        xv = (x_ref[...] + _dot(c_ref[...], wo_ref[0:2].reshape(POOL_W, D), 1, 0)
              + _dot(d_ref[...], wo_ref[2:8].reshape(HEADS * V_DIM, D), 1, 0))
        x3_ref[...] = xv
        h, _ = _rms(xv, gam_ref[...])
        hb = h.astype(BF16)
        g = _dot(hb, wg_ref[...].reshape(D_FF, D), 1, 1)
        u = _dot(hb, wu_ref[...].reshape(D_FF, D), 1, 1)
        g_ref[...] = g.astype(BF16)
        u_ref[...] = u.astype(BF16)
        act = g * jax.nn.sigmoid(g) * u
        err = xv + _dot(act.astype(BF16), wd_ref[...].reshape(D_FF, D), 1, 0) - t_ref[...]
        dy_ref[...] = err * (1.0 / D)
        sq = jnp.sum(jnp.sum(err * err, axis=-1, keepdims=True), axis=0, keepdims=True)
        loss_ref[...] += (0.5 / D) * sq

    def row(w):
        return pl.BlockSpec((TB, w), lambda i: (i, 0))

    res, _ = _call(
        body, "last_block_fwd", (t // TB,),
        [row(D), row(POOL_W), row(HEADS * V_DIM), _wspec(N_SQ, OFF_OOUT), _const_spec((1, D)),
         _wspec(N_FF, OFF_GATE), _wspec(N_FF, OFF_UP), _wspec(N_FF, 0), row(D)],
        [row(D), row(D), row(D_FF), row(D_FF), _const_spec((8, 128))],
        [jax.ShapeDtypeStruct((t, D), F32), jax.ShapeDtypeStruct((t, D), F32), jax.ShapeDtypeStruct((t, D_FF), BF16),
         jax.ShapeDtypeStruct((t, D_FF), BF16), jax.ShapeDtypeStruct((8, 128), F32)],
        (x, c_out, d_out, w_mix1, gamma, w_gu, w_gu, w_d, target))
    return res


def _ffn_up(x, w_gu, gamma, name, comm=None):
    t = x.shape[0]

    def body(x_ref, gam_ref, wg_ref, wu_ref, g_ref, u_ref, act_ref):
        h, _ = _rms(x_ref[...], gam_ref[...])
        hb = h.astype(BF16)
        g = _dot(hb, wg_ref[...].reshape(D_FF, D), 1, 1)
        u = _dot(hb, wu_ref[...].reshape(D_FF, D), 1, 1)
        g_ref[...] = g.astype(BF16)
        u_ref[...] = u.astype(BF16)
        act_ref[...] = (g * jax.nn.sigmoid(g) * u).astype(BF16)

    row = pl.BlockSpec((TB, D), lambda i: (i, 0))
    wide = pl.BlockSpec((TB, D_FF), lambda i: (i, 0))
    return _call(body, name, (t // TB,), [row, _const_spec((1, D)), _wspec(N_FF, OFF_GATE), _wspec(N_FF, OFF_UP)],
                 [wide, wide, wide], [jax.ShapeDtypeStruct((t, D_FF), BF16)] * 3, (x, gamma, w_gu, w_gu), (), comm)


def _ffn_down(x, act, w_d, name, comm=None):
    t = x.shape[0]

    def body(x_ref, a_ref, wd_ref, y_ref):
        y_ref[...] = x_ref[...] + _dot(a_ref[...], wd_ref[...].reshape(D_FF, D), 1, 0)

    row = pl.BlockSpec((TB, D), lambda i: (i, 0))
    wide = pl.BlockSpec((TB, D_FF), lambda i: (i, 0))
    return _call(body, name, (t // TB,), [row, wide, _wspec(N_FF, 0)], [row], [jax.ShapeDtypeStruct((t, D), F32)],
                 (x, act, w_d), (), comm)


def _ffn_bwd(x, g, u, dy, w_gu, w_d, gamma, name, comm=None, w_mix1=None):
    t = x.shape[0]
    with_dmix = w_mix1 is not None

    def body(*refs):
        x_ref, g_ref, u_ref, dy_ref, gam_ref, wg_ref, wu_ref, wd_ref = refs[:8]
        dx_ref, act_ref, dg_ref, du_ref, h_ref, dgam_ref = refs[8 + with_dmix:14 + with_dmix]

        @pl.when(pl.program_id(0) == 0)
        def _():
            dgam_ref[...] = jnp.zeros_like(dgam_ref)

        xv = x_ref[...]
        gam = gam_ref[...]
        h, r = _rms(xv, gam)
        h_ref[...] = h.astype(BF16)
        dyv = dy_ref[...]
        dact = _dot(dyv.astype(BF16), wd_ref[...].reshape(D_FF, D), 1, 1)
        gv = g_ref[...].astype(F32)
        uv = u_ref[...].astype(F32)
        sg = jax.nn.sigmoid(gv)
        silu = gv * sg
        act_ref[...] = (silu * uv).astype(BF16)
        dgb = (dact * uv * (sg * (1.0 + gv * (1.0 - sg)))).astype(BF16)
        dub = (dact * silu).astype(BF16)
        dg_ref[...] = dgb
        du_ref[...] = dub
        dh = _dot(dgb, wg_ref[...].reshape(D_FF, D), 1, 0) + _dot(dub, wu_ref[...].reshape(D_FF, D), 1, 0)
        dxn, dgam = _rms_bwd(xv, r, gam, dh)
        dgam_ref[...] += dgam
        dx = dyv + dxn
        dx_ref[...] = dx
        if with_dmix:
            refs[15][...] = _dot(dx.astype(BF16), refs[8][...].reshape(D, D), 1, 1).astype(BF16)

    row = pl.BlockSpec((TB_FFN_BWD, D), lambda i: (i, 0))
    wide = pl.BlockSpec((TB_FFN_BWD, D_FF), lambda i: (i, 0))
    in_specs = [row, wide, wide, row, _const_spec((1, D)), _wspec(N_FF, OFF_GATE), _wspec(N_FF, OFF_UP),
                _wspec(N_FF, 0)]
    out_specs = [row, wide, wide, wide, row, _const_spec((1, D))]
    out_shape = [jax.ShapeDtypeStruct((t, D), F32), jax.ShapeDtypeStruct((t, D_FF), BF16),
                 jax.ShapeDtypeStruct((t, D_FF), BF16), jax.ShapeDtypeStruct((t, D_FF), BF16),
                 jax.ShapeDtypeStruct((t, D), BF16), jax.ShapeDtypeStruct((1, D), F32)]
    args = (x, g, u, dy, gamma, w_gu, w_gu, w_d)
    if with_dmix:
        in_specs, args = in_specs + [_wspec(N_SQ, OFF_OOUT)], args + (w_mix1,)
        out_specs, out_shape = out_specs + [row], out_shape + [jax.ShapeDtypeStruct((t, D), BF16)]
    return _call(body, name, (t // TB_FFN_BWD,), in_specs, out_specs, out_shape, args, (), comm)


def _odd_pre_fwd(x, wg, gamma, qbt, kvbt, qa_g, kva_g, pw_bd, pscale, seq, comm=None):
    t = x.shape[0]
    nbs = seq // TB

    def body(x_ref, gam_ref, win_ref, qb_ref, kvb_ref, qa_ref, kva_ref, pw_ref, ps_ref,
             proj_ref, q_ref, kv_ref, kr_ref, c_ref, tail_ref):
        i = pl.program_id(0)
        h, _ = _rms(x_ref[...], gam_ref[...])
        proj = _dot(h.astype(BF16), win_ref[...].reshape(D, D), 1, 0)
        proj_ref[...] = proj.astype(BF16)
        zp, ql, kvl = proj[:, :POOL_W], proj[:, 256:640], proj[:, 640:896]
        kr_ref[...] = proj[:, 896:1024]
        qn, _ = _rms(ql, qa_ref[...])
        q_ref[...] = _dot(qn.astype(BF16), qb_ref[...], 1, 1).astype(BF16)
        kvn, _ = _rms(kvl, kva_ref[...])
        kv_ref[...] = _dot(kvn.astype(BF16), kvb_ref[...], 1, 1).astype(BF16)
        tail = jnp.where(i % nbs == 0, 0.0, tail_ref[...])
        pooled, _, _ = _pool_fwd(zp, tail, i % nbs)
        tail_ref[...] = zp[TB - HALO:, :]
        c_ref[...] = (_dot(pooled.astype(BF16), pw_ref[...], 1, 0) * ps_ref[...]).astype(BF16)

    def row(w):
        return pl.BlockSpec((TB, w), lambda i: (i, 0))

    return _call(
        body, "odd_pre_fwd", (t // TB,),
        [row(D), _const_spec((1, D)), _wspec(N_SQ, OFF_OIN), _const_spec((HEADS * HP, Q_LORA)),
         _const_spec((HEADS * HP, KV_LORA)), _const_spec((1, Q_LORA)), _const_spec((1, KV_LORA)),
         _const_spec((POOL_W, POOL_W)), _const_spec((1, POOL_W))],
        [row(D), row(HEADS * HP), row(HEADS * HP), row(128), row(POOL_W)],
        [jax.ShapeDtypeStruct((t, D), BF16), jax.ShapeDtypeStruct((t, HEADS * HP), BF16),
         jax.ShapeDtypeStruct((t, HEADS * HP), BF16), jax.ShapeDtypeStruct((t, 128), F32),
         jax.ShapeDtypeStruct((t, POOL_W), BF16)],
        (x, gamma, wg, qbt, kvbt, qa_g, kva_g, pw_bd, pscale), [pltpu.VMEM((HALO, POOL_W), F32)], comm)


def _odd_pre_bwd(x, proj, dx3, dmix, dq, dkv, dkr, wg, gamma, qbt, kvbt, qa_g, kva_g, pw_bd, pscale, seq):
    t = x.shape[0]
    nb, nbs = t // TB, seq // TB

    def body(x_ref, proj_ref, ptail_ref, dx3_ref, dco_ref, dq_ref, dkv_ref, dkr_ref, gam_ref, win_ref, qb_ref,
             kvb_ref, qa_ref, kva_ref, pw_ref, ps_ref,
             dx2_ref, dproj_ref, h_ref, qn_ref, kvn_ref, dgam_ref, dqa_ref, dkva_ref, dpw_ref, dps_ref, head_ref):
        i = pl.program_id(0)
        blk = nb - 1 - i

        @pl.when(i == 0)
        def _():
            dgam_ref[...] = jnp.zeros_like(dgam_ref)
            dqa_ref[...] = jnp.zeros_like(dqa_ref)
            dkva_ref[...] = jnp.zeros_like(dkva_ref)
            dpw_ref[...] = jnp.zeros_like(dpw_ref)
            dps_ref[...] = jnp.zeros_like(dps_ref)

        xv = x_ref[...]
        gam = gam_ref[...]
        h, r = _rms(xv, gam)
        h_ref[...] = h.astype(BF16)
        proj = proj_ref[...].astype(F32)
        zp, ql, kvl = proj[:, :POOL_W], proj[:, 256:640], proj[:, 640:896]

        qa = qa_ref[...]
        qn, rq = _rms(ql, qa)
        qn_ref[...] = qn.astype(BF16)
        dql, dqa = _rms_bwd(ql, rq, qa, _dot(dq_ref[...], qb_ref[...], 1, 0))
        dqa_ref[...] += dqa
        kva = kva_ref[...]
        kvn, rkv = _rms(kvl, kva)
        kvn_ref[...] = kvn.astype(BF16)
        dkvl, dkva = _rms_bwd(kvl, rkv, kva, _dot(dkv_ref[...], kvb_ref[...], 1, 0))
        dkva_ref[...] += dkva

        pt = ptail_ref[...].astype(F32)
        tail = jnp.where(blk % nbs == 0, 0.0, pt[:, :POOL_W])
        pooled, cnt, grp = _pool_fwd(zp, tail, blk % nbs)
        pb = pooled.astype(BF16)
        pw = pw_ref[...]
        dco = dco_ref[...].astype(F32)
        dps_ref[...] += jnp.sum(dco * _dot(pb, pw, 1, 0), axis=0, keepdims=True)
        dpo = (dco * ps_ref[...]).astype(BF16)
        dpw_ref[...] += _dot(pb, dpo, 0, 0)
        dpooled = _dot(dpo, pw, 1, 1)
        dpm = dpooled / cnt
        head = jnp.where(blk % nbs == nbs - 1, 0.0, head_ref[...])
        dz = _pool_bwd(dpooled, dpm, head, grp)
        head_ref[...] = dpm[:HALO, :]

        dproj_ref[:, :POOL_W] = dz.astype(BF16)
        dproj_ref[:, 256:640] = dql.astype(BF16)
        dproj_ref[:, 640:896] = dkvl.astype(BF16)
        dproj_ref[:, 896:1024] = dkr_ref[...].astype(BF16)
        dh = _dot(dproj_ref[...], win_ref[...].reshape(D, D), 1, 1)
        dxn, dgam = _rms_bwd(xv, r, gam, dh)
        dgam_ref[...] += dgam
        dx2_ref[...] = dx3_ref[...] + dxn

    def rev(w):
        return pl.BlockSpec((TB, w), lambda i: (nb - 1 - i, 0))

    ptail = pl.BlockSpec((HALO, D), lambda i: (jnp.maximum((nb - 1 - i) * (TB // HALO) - 1, 0), 0))
    return pl.pallas_call(
        body, name="odd_pre_bwd",
        out_shape=[jax.ShapeDtypeStruct((t, D), F32), jax.ShapeDtypeStruct((t, D), BF16),
                   jax.ShapeDtypeStruct((t, D), BF16), jax.ShapeDtypeStruct((t, Q_LORA), BF16),
                   jax.ShapeDtypeStruct((t, KV_LORA), BF16), jax.ShapeDtypeStruct((1, D), F32),
                   jax.ShapeDtypeStruct((1, Q_LORA), F32), jax.ShapeDtypeStruct((1, KV_LORA), F32),
                   jax.ShapeDtypeStruct((POOL_W, POOL_W), F32), jax.ShapeDtypeStruct((1, POOL_W), F32)],
        grid=(nb,),
        in_specs=[rev(D), rev(D), ptail, rev(D), rev(POOL_W), rev(HEADS * HP), rev(HEADS * HP), rev(128),
                  _const_spec((1, D)), _wspec(N_SQ, OFF_OIN), _const_spec((HEADS * HP, Q_LORA)),
                  _const_spec((HEADS * HP, KV_LORA)), _const_spec((1, Q_LORA)), _const_spec((1, KV_LORA)),
                  _const_spec((POOL_W, POOL_W)), _const_spec((1, POOL_W))],
        out_specs=[rev(D), rev(D), rev(D), rev(Q_LORA), rev(KV_LORA), _const_spec((1, D)), _const_spec((1, Q_LORA)),
                   _const_spec((1, KV_LORA)), _const_spec((POOL_W, POOL_W)), _const_spec((1, POOL_W))],
        scratch_shapes=[pltpu.VMEM((HALO, POOL_W), F32)],
        compiler_params=_cparams(1),
    )(x, proj, proj, dx3, dmix, dq, dkv, dkr, gamma, wg, qbt, kvbt, qa_g, kva_g, pw_bd, pscale)


def _attn_specs(seq):
    head = pl.BlockSpec((seq, HP), lambda b, h: (b, h))
    shared = pl.BlockSpec((seq, 128), lambda b, h: (b, 0))
    gain = pl.BlockSpec((1, HP), lambda b, h: (0, 0))
    return head, shared, gain


def _causal_bias(n):
    rows = lax.broadcasted_iota(jnp.int32, (n, n), 0)
    cols = lax.broadcasted_iota(jnp.int32, (n, n), 1)
    return jnp.where(cols <= rows, 0.0, NEG_INF)


def _attn_fwd(q, kv, kr, cos, sin, gq, gk, seq, comm=None):
    t = q.shape[0]
    qb = min(512, seq)

    def body(q_ref, kv_ref, kr_ref, c_ref, s_ref, gq_ref, gk_ref, o_ref, lse_ref):
        c, s = c_ref[...], s_ref[...]
        qf, _ = _qk_prep(q_ref[...].astype(F32), gq_ref[...], c, s)
        kin = jnp.concatenate([kv_ref[:, :128].astype(F32), kr_ref[...]], axis=1)
        kf, _ = _qk_prep(kin, gk_ref[...], c, s)
        qf, kf = qf.astype(BF16), kf.astype(BF16)
        v1 = jnp.concatenate([kv_ref[:, 128:], jnp.ones((seq, V_DIM), BF16)], axis=1)
        bias = _causal_bias(qb)
        for q0 in range(0, seq, qb):
            q1 = q0 + qb
            qblk = qf[q0:q1]
            s_dg = _dot(qblk, kf[q0:q1], 1, 1) + bias
            m = jnp.max(s_dg, axis=-1, keepdims=True)
            if q0:
                s_off = _dot(qblk, kf[:q0], 1, 1)
                m = jnp.maximum(m, jnp.max(s_off, axis=-1, keepdims=True))
            acc = _dot(jnp.exp(s_dg - m).astype(BF16), v1[q0:q1], 1, 0)
            if q0:
                acc = acc + _dot(jnp.exp(s_off - m).astype(BF16), v1[:q0], 1, 0)
            l = acc[:, V_DIM:]
            o_ref[q0:q1, :] = (acc[:, :V_DIM] / l).astype(BF16)
            lse_ref[q0:q1, :] = m + jnp.log(l)

    head, shared, gain = _attn_specs(seq)
    per_head = pl.BlockSpec((seq, V_DIM), lambda b, h: (b, h))
    return _call(
        body, "attn_fwd", (t // seq, HEADS),
        [head, head, shared, shared, shared, gain, gain], [per_head, per_head],
        [jax.ShapeDtypeStruct((t, HEADS * V_DIM), BF16), jax.ShapeDtypeStruct((t, HEADS * V_DIM), F32)],
        (q, kv, kr, cos, sin, gq, gk), (), comm)


def _attn_bwd(q, kv, kr, cos, sin, gq, gk, dmix, d_out, lse, seq, comm=None):
    t = q.shape[0]
    qb = min(512, seq)

    def body(q_ref, kv_ref, kr_ref, c_ref, s_ref, gq_ref, gk_ref, do_ref, o_ref, lse_ref,
             dq_ref, dkv_ref, dkr_ref, dgq_ref, dgk_ref, dqf_ref, dkf_ref, dv_ref):
        b, hd = pl.program_id(0), pl.program_id(1)

        @pl.when((b == 0) & (hd == 0))
        def _():
            dgq_ref[...] = jnp.zeros_like(dgq_ref)
            dgk_ref[...] = jnp.zeros_like(dgk_ref)

        c, sn = c_ref[...], s_ref[...]
        gq_v, gk_v = gq_ref[...], gk_ref[...]
        qin = q_ref[...].astype(F32)
        kin = jnp.concatenate([kv_ref[:, :128].astype(F32), kr_ref[...]], axis=1)
        qf32, rq = _qk_prep(qin, gq_v, c, sn)
        kf32, rk = _qk_prep(kin, gk_v, c, sn)
        qf, kf = qf32.astype(BF16), kf32.astype(BF16)
        vb = kv_ref[:, 128:]
        dkf_ref[...] = jnp.zeros_like(dkf_ref)
        dv_ref[...] = jnp.zeros_like(dv_ref)
        bias = _causal_bias(qb)
        for q0 in range(0, seq, qb):
            q1 = q0 + qb
            qblk = qf[q0:q1]
            do = do_ref[q0:q1, :]
            lse_col = lse_ref[q0:q1, 0:1]
            d_col = jnp.sum(do.astype(F32) * o_ref[q0:q1, :].astype(F32), axis=-1, keepdims=True)
            dq_acc = None
            for k0, k1, diag in ((q0, q1, True), (0, q0, False)):
                if k1 == k0:
                    continue
                s = _dot(qblk, kf[k0:k1], 1, 1)
                p = jnp.exp((s + bias if diag else s) - lse_col)
                dv_ref[k0:k1, :] += _dot(p.astype(BF16), do, 0, 0)
                ds = (p * (_dot(do, vb[k0:k1], 1, 1) - d_col)).astype(BF16)
                part = _dot(ds, kf[k0:k1], 1, 0)
                dq_acc = part if dq_acc is None else dq_acc + part
                dkf_ref[k0:k1, :] += _dot(ds, qblk, 0, 0)
            dqf_ref[q0:q1, :] = dq_acc
        dqin, dgq = _qk_prep_bwd(dqf_ref[...], qin, rq, gq_v, c, sn)
        dkin, dgk = _qk_prep_bwd(dkf_ref[...], kin, rk, gk_v, c, sn)
        dgq_ref[...] += dgq
        dgk_ref[...] += dgk
        dq_ref[...] = dqin.astype(BF16)
        dkv_ref[:, :128] = dkin[:, :128].astype(BF16)
        dkv_ref[:, 128:] = dv_ref[...].astype(BF16)

        @pl.when(hd == 0)
        def _():
            dkr_ref[...] = dkin[:, 128:]

        @pl.when(hd != 0)
        def _():
            dkr_ref[...] += dkin[:, 128:]

    head, shared, gain = _attn_specs(seq)
    per_head = pl.BlockSpec((seq, V_DIM), lambda b, h: (b, h))
    return _call(
        body, "attn_bwd", (t // seq, HEADS),
        [head, head, shared, shared, shared, gain, gain,
         pl.BlockSpec((seq, V_DIM), lambda b, h: (b, 2 + h)), per_head, per_head],
        [head, head, shared, gain, gain],
        [jax.ShapeDtypeStruct((t, HEADS * HP), BF16), jax.ShapeDtypeStruct((t, HEADS * HP), BF16),
         jax.ShapeDtypeStruct((t, 128), F32), jax.ShapeDtypeStruct((1, HP), F32),
         jax.ShapeDtypeStruct((1, HP), F32)],
        (q, kv, kr, cos, sin, gq, gk, dmix, d_out, lse),
        [pltpu.VMEM((seq, HP), F32), pltpu.VMEM((seq, HP), F32), pltpu.VMEM((seq, V_DIM), F32)], comm)


def _tn(a_list, b, tm, name, into=None, comm=None):
    t, n_out = b.shape
    widths = [a.shape[1] for a in a_list]
    tk = min(TK_DW, t)
    m, na, nk = sum(widths), len(a_list), t // tk
    assert na == 1 or tm == m

    def body(*refs):
        a_refs, b_ref, o_ref, acc_ref = refs[:na], refs[na], refs[-2], refs[-1]
        k = pl.program_id(1)

        @pl.when(k == 0)
        def _():
            acc_ref[...] = jnp.zeros_like(acc_ref)

        bb = b_ref[...].astype(BF16)
        m0 = 0
        for a_ref, w in zip(a_refs, widths):
            rows = slice(0, tm) if na == 1 else slice(m0, m0 + w)
            acc_ref[rows, :] += _dot(a_ref[...].astype(BF16), bb, 0, 0)
            m0 += w

        @pl.when(k == nk - 1)
        def _():
            o_ref[...] = acc_ref[...].astype(BF16).reshape(o_ref.shape)

    if na == 1:
        in_specs = [pl.BlockSpec((tk, tm), lambda i, k: (k, i))]
    else:
        in_specs = [pl.BlockSpec((tk, w), lambda i, k: (k, 0)) for w in widths]
    in_specs.append(pl.BlockSpec((tk, n_out), lambda i, k: (k, 0)))
    args = list(a_list) + [b]
    if into is None:
        out_spec = pl.BlockSpec((tm, n_out), lambda i, k: (i, 0))
        out_shape = jax.ShapeDtypeStruct((m, n_out), BF16)
        aliases = {}
    else:
        buf, n, off = into
        assert n_out == D and tm % n == 0 and off % n == 0 and (na == 1 or tm // n == N_DEV)
        idx = off // n
        out_spec = pl.BlockSpec((tm // n, n, D), lambda i, k: (i, idx, 0))
        out_shape = jax.ShapeDtypeStruct(buf.shape, BF16)
        in_specs.append(pl.BlockSpec(memory_space=pl.ANY))
        args.append(buf)
        aliases = {len(args) - 1: 0}
    (res,), extra = _call(body, name, (m // tm, nk), in_specs, [out_spec], [out_shape], args,
                          [pltpu.VMEM((tm, n_out), F32)], comm, aliases)
    return (res, extra) if comm is not None else res


def _adamw(ws, gs, ms, vs, name, nblk=1):
    n = len(ws)
    c1 = 1.0 - B1 ** STEP
    c2 = 1.0 - B2 ** STEP

    def body(*refs):
        for a in range(n):
            w, g, m, v = (refs[k * n + a][...] for k in range(4))
            d_ref, m_ref, v_ref = (refs[(4 + k) * n + a] for k in range(3))
            m_new = B1 * m + (1.0 - B1) * g
            v_new = B2 * v + (1.0 - B2) * (g * g)
            d_ref[...] = -LR * ((m_new / c1) / (jnp.sqrt(v_new / c2) + ADAM_EPS) + WD * w)
            m_ref[...] = m_new
            v_ref[...] = v_new

    grid = (nblk,)
    assert all(w.shape[0] % nblk == 0 and (nblk == 1 or (w.shape[0] // nblk) % 8 == 0) for w in ws)
    specs = [pl.BlockSpec((w.shape[0] // nblk, w.shape[1]), lambda i: (i, 0)) for w in ws]
    outs, _ = _call(body, name, grid, specs * 4, specs * 3, [jax.ShapeDtypeStruct(w.shape, F32) for w in ws] * 3,
                    (*ws, *gs, *ms, *vs))
    return outs[:n], outs[n:2 * n], outs[2 * n:]


def _rows1024(a, rows):
    flat = a.reshape(-1, D)
    return jnp.pad(flat, ((0, rows - flat.shape[0]), (0, 0)))


def _pack_shards(even_w_in, even_w_out, odd_w_in, q_b, kv_b, odd_w_out, ffn_w_gate, ffn_w_up, ffn_w_down):
    mix0 = jnp.concatenate([even_w_in[0].T, jnp.zeros((OFF_EOUT - N_EIN, D), F32), even_w_out[0]], axis=0)
    gu = [jnp.concatenate([ffn_w_gate[layer].T, ffn_w_up[layer].T], axis=0) for layer in range(2)]
    mix1 = jnp.concatenate([jnp.pad(odd_w_in[0], ((0, 0), (0, D - ODD_IN))), odd_w_out[0],
                            _rows1024(q_b[0].T, N_QB), _rows1024(kv_b[0].T, N_KVB),
                            jnp.zeros((R_MIX1 - OFF_KVB - N_KVB, D), F32)], axis=0)
    return [c.astype(BF16) for c in (mix0, gu[0], ffn_w_down[0], mix1, gu[1], ffn_w_down[1])]


def _pad_heads(a):
    k = a.shape[1]
    return jnp.pad(a.reshape(HEADS, QK_DIM, k), ((0, 0), (0, HP - QK_DIM), (0, 0))).reshape(HEADS * HP, k)


def _small_pack(parts):
    flat = []
    for p in parts:
        v = p.reshape(-1)
        flat.append(jnp.pad(v, (0, (-v.shape[0]) % 1024)))
    return jnp.concatenate(flat).reshape(-1, 128)


def _small_unpack(buf, shapes):
    flat = buf.reshape(-1)
    out, off = [], 0
    for s in shapes:
        size = int(np.prod(s))
        out.append(flat[off:off + size].reshape(s))
        off += size + (-size) % 1024
    return out


def _step(x3d, positions, target3d, chunks, tile, where, mix_norm, ffn_norm, sg_ln_g, sg_w_s, sg_b_s,
          pool_w, q_norm, k_norm):
    bsz, seq, _ = x3d.shape
    t = bsz * seq
    x0 = x3d.reshape(t, D)
    target = target3d.reshape(t, D)
    my_mix0, my_gu0, my_d0, my_mix1, my_gu1, my_d1 = chunks

    lane = np.arange(128)
    inv_freq = np.where(lane < QK_ROPE, ROPE_THETA ** (-(2.0 * (lane % 32)) / QK_ROPE), 0.0)
    inv_freq = jnp.asarray(inv_freq.reshape(1, 128), F32)
    (cos, sin), (w_mix0, tiles) = _rope_tables(positions.reshape(t, 1), inv_freq, _gather_comm([my_mix0, tile]))

    conv_w = tiles[:, 0:3, 0:64].transpose(1, 0, 2).reshape(3, SC_W)
    pool_scale = tiles[:, 3, 0:32].reshape(1, POOL_W)
    q_a_norm = tiles[:, 4, 0:48].reshape(1, Q_LORA)
    kv_a_norm = tiles[:, 5, 0:32].reshape(1, KV_LORA)
    ws = sg_w_s[0]
    bst = jnp.pad(sg_b_s[0].T, ((0, 0), (0, 128 - SG_HEADS)))
    cw = jnp.pad(conv_w, ((0, 8 - 3), (0, 0)))
    pw_bd = jax.scipy.linalg.block_diag(*[pool_w[0, g] for g in range(4)]).astype(BF16)
    gq = jnp.pad(q_norm * ATT_SCALE, ((0, 0), (0, HP - QK_DIM)))
    gk = jnp.pad(k_norm, ((0, 0), (0, HP - QK_DIM)))

    (x1, proj_e), (w_gu0,) = _even_fwd(x0, w_mix0, mix_norm[0:1], sg_ln_g, ws, bst, cw, seq, _gather_comm([my_gu0]))
    (g0, u0, act0), (w_d0, w_mix1) = _ffn_up(x1, w_gu0, ffn_norm[0:1], "ffn_up0", _gather_comm([my_d0, my_mix1]))
    (x2,), (w_d1,) = _ffn_down(x1, act0, w_d0, "ffn_down0", _gather_comm([my_d1]))
    qbt = _pad_heads(w_mix1[:, OFF_QB:OFF_QB + N_QB_USED, :].reshape(HEADS * QK_DIM, Q_LORA))
    kvbt = w_mix1[:, OFF_KVB:OFF_KVB + N_KVB, :].reshape(HEADS * HP, KV_LORA)
    (proj_o, q, kv, kr, c_out), _ = _odd_pre_fwd(x2, w_mix1, mix_norm[1:2], qbt, kvbt, q_a_norm, kv_a_norm,
                                                pw_bd, pool_scale, seq)
    (d_out, lse), (w_gu1,) = _attn_fwd(q, kv, kr, cos, sin, gq, gk, seq, _gather_comm([my_gu1]))
    x3, dy, g1, u1, loss_tile = _last_block_fwd(x2, c_out, d_out, w_mix1, w_gu1, w_d1, ffn_norm[1:2], target)

    def chunk(rows, padded=False):
        return jnp.zeros((N_DEV, rows, D), BF16) if padded else lax.empty((N_DEV, rows, D), BF16)

    (dx3, act1, dg1, du1, h3, dgam_f1, dmix_o), _ = _ffn_bwd(x3, g1, u1, dy, w_gu1, w_d1, ffn_norm[1:2], "ffn_bwd1",
                                                           None, w_mix1)
    gp_ffn1 = _tn([dg1], h3, 1408, "dw_gate1", (chunk(R_GU + N_FF), N_FF, OFF_GATE))
    gp_ffn1 = _tn([du1], h3, 1408, "dw_up1", (gp_ffn1, N_FF, OFF_UP))
    gp_ffn1 = _tn([act1], dy, 1408, "dw_down1", (gp_ffn1, N_FF, R_GU))

    gp_mix1, (ga_ffn1,) = _tn([c_out, d_out], dx3, D, "dw_oout", (chunk(R_MIX1, True), N_SQ, OFF_OOUT),
                              _pair_exchange_comm(gp_ffn1))
    pb_ffn1 = _rs_pair_sum(gp_ffn1, ga_ffn1, where, "rs_pair_sum_ffn1")
    (dq, dkv, dkr, dgq, dgk), (gb_ffn1,) = _attn_bwd(q, kv, kr, cos, sin, gq, gk, dmix_o, d_out, lse, seq,
                                                    _chip_exchange_comm(pb_ffn1))
    (dx2, dproj_o, h2, qn, kvn, dgam_m1, dqa, dkva, dpw_bd, dps) = _odd_pre_bwd(
        x2, proj_o, dx3, dmix_o, dq, dkv, dkr, w_mix1, mix_norm[1:2], qbt, kvbt, q_a_norm, kv_a_norm, pw_bd,
        pool_scale, seq)
    gp_mix1 = _tn([h2], dproj_o, D, "dw_oin", (gp_mix1, N_SQ, OFF_OIN))
    d_qbt = _tn([dq], qn, HEADS * HP, "dw_qb")
    d_qb_rows = d_qbt.reshape(HEADS, HP, Q_LORA)[:, :QK_DIM].reshape(N_DEV, N_QB_USED, D)
    d_kvb_rows = _tn([dkv], kvn, HEADS * HP, "dw_kvb").reshape(N_DEV, N_KVB, D)
    gp_mix1 = lax.dynamic_update_slice(gp_mix1, d_qb_rows, (0, OFF_QB, 0))
    gp_mix1 = lax.dynamic_update_slice(gp_mix1, d_kvb_rows, (0, OFF_KVB, 0))

    (dx1, act0, dg0, du0, h1, dgam_f0), (ga_mix1,) = _ffn_bwd(x1, g0, u0, dx2, w_gu0, w_d0, ffn_norm[0:1], "ffn_bwd0",
                                                             _pair_exchange_comm(gp_mix1))
    pb_mix1 = _rs_pair_sum(gp_mix1, ga_mix1, where, "rs_pair_sum_mix1")
    gp_ffn0a, (gb_mix1,) = _tn([dg0], h1, 1408, "dw_gate0", (chunk(R_GU), N_FF, OFF_GATE),
                               _chip_exchange_comm(pb_mix1))
    gp_ffn0a = _tn([du0], h1, 1408, "dw_up0", (gp_ffn0a, N_FF, OFF_UP))
    gp_ffn0b, (ga_ffn0a,) = _tn([act0], dx2, 1408, "dw_down0", (chunk(N_FF), N_FF, 0),
                                _pair_exchange_comm(gp_ffn0a))
    pb_ffn0a = _rs_pair_sum(gp_ffn0a, ga_ffn0a, where, "rs_pair_sum_ffn0a")

    (dx0, dproj_e, mix_e, h0, dgam_m0, dws, dbc, dlng, dcw), (gb_ffn0a, ga_ffn0b) = _even_bwd(
        x0, proj_e, dx1, w_mix0, mix_norm[0:1], sg_ln_g, ws, bst, cw, seq,
        _both(_chip_exchange_comm(pb_ffn0a), _pair_exchange_comm(gp_ffn0b)))
    pb_ffn0b = _rs_pair_sum(gp_ffn0b, ga_ffn0b, where, "rs_pair_sum_ffn0b")

    small = _small_pack([
        jnp.concatenate([dgam_m0, dgam_m1], 0), jnp.concatenate([dgam_f0, dgam_f1], 0), dlng,
        dws[None], dbc[:, :SG_HEADS].T[None], dcw[:3],
        jnp.stack([dpw_bd[g * POOL_GD:(g + 1) * POOL_GD, g * POOL_GD:(g + 1) * POOL_GD] for g in range(4)])[None],
        dps, dqa, dkva, dgq[:, :QK_DIM] * ATT_SCALE, dgk[:, :QK_DIM], loss_tile[0:1, 0:1]])
    gp_mix0, (small_all,) = _tn([mix_e], dx1, D, "dw_eout", (chunk(R_MIX0, True), N_SQ, OFF_EOUT),
                                _gather_comm([small]))
    gp_mix0, (gb_ffn0b,) = _tn([dproj_e], h0, 1280, "dw_ein", (gp_mix0, N_EIN, OFF_EIN),
                               _chip_exchange_comm(pb_ffn0b))
    small_sum = _small_unpack(_sum_gathered(small_all), SMALL_SHAPES)
    partials = ([pb_ffn0a, pb_ffn0b, pb_mix1, pb_ffn1], [gb_ffn0a, gb_ffn0b, gb_mix1, gb_ffn1])
    return dx0.reshape(bsz, seq, D), partials, gp_mix0, small_sum


SMALL_SHAPES = [(2, D), (2, D), (1, SG_W), (1, SG_HEADS, 128, 128), (1, SG_HEADS, 128), (3, SC_W),
                (1, 4, POOL_GD, POOL_GD), (1, POOL_W), (1, Q_LORA), (1, KV_LORA), (1, QK_DIM), (1, QK_DIM), (1, 1)]


def kernel(x, positions, mix_norm, ffn_norm, even_w_in, sg_ln_g, sg_w_s, sg_b_s, sc_conv_w, even_w_out, odd_w_in, pool_w, pool_scale, q_a_norm, q_b, kv_a_norm, kv_b, q_norm, k_norm, odd_w_out, ffn_w_gate, ffn_w_up, ffn_w_down, loss_target, m_mix_norm, m_ffn_norm, m_even_w_in, m_sg_ln_g, m_sg_w_s, m_sg_b_s, m_sc_conv_w, m_even_w_out, m_odd_w_in, m_pool_w, m_pool_scale, m_q_a_norm, m_q_b, m_kv_a_norm, m_kv_b, m_q_norm, m_k_norm, m_odd_w_out, m_ffn_w_gate, m_ffn_w_up, m_ffn_w_down, v_mix_norm, v_ffn_norm, v_even_w_in, v_sg_ln_g, v_sg_w_s, v_sg_b_s, v_sc_conv_w, v_even_w_out, v_odd_w_in, v_pool_w, v_pool_scale, v_q_a_norm, v_q_b, v_kv_a_norm, v_kv_b, v_q_norm, v_k_norm, v_odd_w_out, v_ffn_w_gate, v_ffn_w_up, v_ffn_w_down):
    xi, yi, ci = _place()
    me = 4 * xi + 2 * yi + ci

    chunks = _pack_shards(even_w_in, even_w_out, odd_w_in, q_b, kv_b, odd_w_out, ffn_w_gate, ffn_w_up, ffn_w_down)

    def lane_pad(a):
        return jnp.pad(a, ((0, 0), (0, 128 - a.shape[1])))

    tile = jnp.concatenate([lane_pad(sc_conv_w[0]), lane_pad(pool_scale), lane_pad(q_a_norm), lane_pad(kv_a_norm),
                            jnp.zeros((2, 128), F32)], axis=0)
    chip = 2 * xi + yi
    where = jnp.stack([ci, chip, chip ^ 2, chip ^ 1, chip ^ 3]).astype(jnp.int32)
    grad_x, (pbs, gbs), gp_mix0, tot = _step(
        x, positions, loss_target, chunks, tile, where, mix_norm, ffn_norm, sg_ln_g, sg_w_s, sg_b_s,
        pool_w, q_norm, k_norm)

    (ga_mix0,) = _comm_alone(_pair_exchange_comm(gp_mix0), "rs_pair_exchange_mix0")
    pb_mix0 = _rs_pair_sum(gp_mix0, ga_mix0, where, "rs_pair_sum_mix0")
    mix0_sems, pb_mix0, land_mix0, started = _chip_exchange_start(pb_mix0)
    gsh_ffn0a, gsh_ffn0b, gsh_mix1, gsh_ffn1 = _rs_final_sums(pbs, gbs, "rs_final_sums", started)

    (g_mix, g_ffn, g_lng, g_ws, g_bs, g_cw_full, g_pw, g_ps_full, g_qa_full, g_kva_full, g_qn, g_kn, loss) = tot
    g_cw = lax.dynamic_slice_in_dim(g_cw_full, me * 64, 64, axis=1)[None]
    g_ps = lax.dynamic_slice_in_dim(g_ps_full, me * 32, 32, axis=1)
    g_qa = lax.dynamic_slice_in_dim(g_qa_full, me * 48, 48, axis=1)
    g_kva = lax.dynamic_slice_in_dim(g_kva_full, me * 32, 32, axis=1)

    def tr(a):
        return jnp.swapaxes(a, -1, -2)

    g_gate = tr(jnp.stack([gsh_ffn0a[OFF_GATE:OFF_GATE + N_FF], gsh_ffn1[OFF_GATE:OFF_GATE + N_FF]]))
    g_up = tr(jnp.stack([gsh_ffn0a[OFF_UP:OFF_UP + N_FF], gsh_ffn1[OFF_UP:OFF_UP + N_FF]]))
    g_down = jnp.stack([gsh_ffn0b, gsh_ffn1[R_GU:R_GU + N_FF]])
    g_oin = gsh_mix1[OFF_OIN:OFF_OIN + N_SQ, :ODD_IN][None]
    g_oout = gsh_mix1[OFF_OOUT:OFF_OOUT + N_SQ][None]
    g_qb = tr(gsh_mix1[OFF_QB:OFF_QB + N_QB_USED].reshape(1, 144, Q_LORA))
    g_kvb = tr(gsh_mix1[OFF_KVB:OFF_KVB + N_KVB].reshape(1, 192, KV_LORA))
    transposed = ("even_w_in", "odd_w_in", "q_b", "kv_b", "ffn_w_gate", "ffn_w_up")

    names = ("mix_norm", "ffn_norm", "even_w_in", "sg_ln_g", "sg_w_s", "sg_b_s", "sc_conv_w", "even_w_out",
             "odd_w_in", "pool_w", "pool_scale", "q_a_norm", "q_b", "kv_a_norm", "kv_b", "q_norm", "k_norm",
             "odd_w_out", "ffn_w_gate", "ffn_w_up", "ffn_w_down")
    grads = dict(mix_norm=g_mix, ffn_norm=g_ffn, sg_ln_g=g_lng, sg_w_s=g_ws, sg_b_s=g_bs,
                 sc_conv_w=g_cw, odd_w_in=g_oin, pool_w=g_pw, pool_scale=g_ps, q_a_norm=g_qa,
                 q_b=g_qb, kv_a_norm=g_kva, kv_b=g_kvb, q_norm=g_qn, k_norm=g_kn, odd_w_out=g_oout,
                 ffn_w_gate=g_gate, ffn_w_up=g_up, ffn_w_down=g_down)
    weights = dict(mix_norm=mix_norm, ffn_norm=ffn_norm, even_w_in=even_w_in, sg_ln_g=sg_ln_g, sg_w_s=sg_w_s,
                   sg_b_s=sg_b_s, sc_conv_w=sc_conv_w, even_w_out=even_w_out, odd_w_in=odd_w_in, pool_w=pool_w,
                   pool_scale=pool_scale, q_a_norm=q_a_norm, q_b=q_b, kv_a_norm=kv_a_norm, kv_b=kv_b, q_norm=q_norm,
                   k_norm=k_norm, odd_w_out=odd_w_out, ffn_w_gate=ffn_w_gate, ffn_w_up=ffn_w_up,
                   ffn_w_down=ffn_w_down)
    m_in = dict(mix_norm=m_mix_norm, ffn_norm=m_ffn_norm, even_w_in=m_even_w_in, sg_ln_g=m_sg_ln_g, sg_w_s=m_sg_w_s,
                sg_b_s=m_sg_b_s, sc_conv_w=m_sc_conv_w, even_w_out=m_even_w_out, odd_w_in=m_odd_w_in,
                pool_w=m_pool_w, pool_scale=m_pool_scale, q_a_norm=m_q_a_norm, q_b=m_q_b, kv_a_norm=m_kv_a_norm,
                kv_b=m_kv_b, q_norm=m_q_norm, k_norm=m_k_norm, odd_w_out=m_odd_w_out, ffn_w_gate=m_ffn_w_gate,
                ffn_w_up=m_ffn_w_up, ffn_w_down=m_ffn_w_down)
    v_in = dict(mix_norm=v_mix_norm, ffn_norm=v_ffn_norm, even_w_in=v_even_w_in, sg_ln_g=v_sg_ln_g, sg_w_s=v_sg_w_s,
                sg_b_s=v_sg_b_s, sc_conv_w=v_sc_conv_w, even_w_out=v_even_w_out, odd_w_in=v_odd_w_in,
                pool_w=v_pool_w, pool_scale=v_pool_scale, q_a_norm=v_q_a_norm, q_b=v_q_b, kv_a_norm=v_kv_a_norm,
                kv_b=v_kv_b, q_norm=v_q_norm, k_norm=v_k_norm, odd_w_out=v_odd_w_out, ffn_w_gate=v_ffn_w_gate,
                ffn_w_up=v_ffn_w_up, ffn_w_down=v_ffn_w_down)
    delta, new_m, new_v = {}, {}, {}

    def as2d(k, a):
        a = tr(a) if k in transposed else a
        return a.reshape(-1, a.shape[-1])

    def back(k, a):
        shape = weights[k].shape
        return tr(a.reshape(shape[:-2] + (shape[-1], shape[-2]))) if k in transposed else a.reshape(shape)

    def update(group, name, nblk=1):
        outs = _adamw([as2d(k, weights[k]) for k in group], [as2d(k, grads[k]) for k in group],
                      [as2d(k, m_in[k]) for k in group], [as2d(k, v_in[k]) for k in group], name, nblk)
        for i, k in enumerate(group):
            delta[k], new_m[k], new_v[k] = (back(k, o[i]) for o in outs)

    update(["ffn_w_gate", "ffn_w_up", "ffn_w_down"], "adamw_ffn", 4)
    update(["odd_w_in", "odd_w_out"], "adamw_mix1", 2)
    update([k for k in names if k not in delta and k not in ("even_w_in", "even_w_out")], "adamw_small")

    pb_mix0, gb_mix0 = _chip_exchange_wait(mix0_sems, pb_mix0, land_mix0, new_v["k_norm"])
    (gsh_mix0,) = _rs_final_sums([pb_mix0], [gb_mix0], "rs_final_sum_mix0")
    grads["even_w_in"] = tr(gsh_mix0[OFF_EIN:OFF_EIN + N_EIN][None])
    grads["even_w_out"] = gsh_mix0[OFF_EOUT:OFF_EOUT + N_SQ][None]
    update(["even_w_in", "even_w_out"], "adamw_mix0", 2)

    return (loss.reshape(()), grad_x, *[grads[k] for k in names], *[delta[k] for k in names],
            *[new_m[k] for k in names], *[new_v[k] for k in names])
```

```python
import functools

import numpy as np
import jax
import jax.numpy as jnp
from jax import lax
from jax.experimental import pallas as pl
from jax.experimental.pallas import tpu as pltpu

F32 = jnp.float32
BF16 = jnp.bfloat16
MESH = pl.DeviceIdType.MESH

D = 1024
EPS = 1e-6
NEG_INF = -1e30
SG_HEADS, SG_HD, SG_W, SG_CHUNK = 4, 128, 512, 128
SC_W = 512
EVEN_IN = 2560
POOL_W = 256
POOL_GD = 64
Q_LORA, KV_LORA, QK_ROPE, QK_NOPE, V_DIM = 384, 256, 64, 128, 128
QK_DIM = QK_NOPE + QK_ROPE
HEADS = 6
HP = 256
ODD_IN = 960
D_FF = 2816
ROPE_THETA = 10000.0
ATT_SCALE = QK_DIM ** -0.5
LR, B1, B2, ADAM_EPS, WD, STEP = 0.001, 0.9, 0.999, 1e-08, 0.01, 10

N_DEV = 8
TB = 512
TB_FFN_BWD = 256
TK_DW = 1024
HALO = 16
VMEM_LIMIT = 56 * 1024 * 1024

N_EIN, N_FF, N_SQ = 320, 352, 128
OFF_EIN, OFF_EOUT, R_MIX0 = 0, 384, 512
OFF_GATE, OFF_UP, R_GU = 0, 352, 704
OFF_OIN, OFF_OOUT, OFF_QB, OFF_KVB, R_MIX1 = 0, 128, 256, 320, 384
N_QB, N_QB_USED, N_KVB = 64, 54, 48

INV_SQRT2 = 0.7071067811865476
INV_SQRT_2PI = 0.3989422804014327


def _dot(a, b, ca, cb):
    return lax.dot_general(a, b, (((ca,), (cb,)), ((), ())), preferred_element_type=F32)


def _cparams(n_axes=1):
    return pltpu.CompilerParams(dimension_semantics=("arbitrary",) * n_axes, vmem_limit_bytes=VMEM_LIMIT)


def _wspec(n, off):
    assert off % n == 0
    idx = off // n
    return pl.BlockSpec((N_DEV, n, D), lambda i: (0, idx, 0), pipeline_mode=pl.Buffered(1))


def _const_spec(shape):
    zeros = (0,) * len(shape)
    return pl.BlockSpec(shape, lambda *_: zeros)


class _Comm:
    def __init__(self, ins, out_shapes, sems, start, wait, mid=None):
        self.ins, self.out_shapes, self.sems, self.start, self.wait, self.mid = ins, out_shapes, sems, start, wait, mid


def _both(c1, c2):
    def split(f1, f2):
        def run(ins, outs, sems):
            f1(ins[:len(c1.ins)], outs[:len(c1.out_shapes)], sems[:len(c1.sems)])
            f2(ins[len(c1.ins):], outs[len(c1.out_shapes):], sems[len(c1.sems):])
        return run

    assert c1.mid is None and c2.mid is None
    return _Comm(c1.ins + c2.ins, c1.out_shapes + c2.out_shapes, c1.sems + c2.sems,
                 split(c1.start, c2.start), split(c1.wait, c2.wait))


def _call(body, name, grid, in_specs, out_specs, out_shape, args, scratch_shapes=(), comm=None, aliases=None):
    n_axes = len(grid)
    aliases = aliases or {}
    if comm is None:
        res = pl.pallas_call(
            body, name=name, grid=grid, in_specs=list(in_specs), out_specs=list(out_specs),
            out_shape=list(out_shape), scratch_shapes=list(scratch_shapes), input_output_aliases=aliases,
            compiler_params=_cparams(n_axes))(*args)
        return list(res), []
    ni, no, ns = len(in_specs), len(out_specs), len(scratch_shapes)
    ci, co = len(comm.ins), len(comm.out_shapes)
    n_steps = int(np.prod(grid))

    def carrier(*refs):
        ins, cin = refs[:ni], refs[ni:ni + ci]
        outs, cout = refs[ni + ci:ni + ci + no], refs[ni + ci + no:ni + ci + no + co]
        scr, sems = refs[ni + ci + no + co:ni + ci + no + co + ns], refs[ni + ci + no + co + ns:]
        step = 0
        for a in range(n_axes):
            step = step * grid[a] + pl.program_id(a)

        @pl.when(step == 0)
        def _():
            comm.start(cin, cout, sems)

        body(*ins, *outs, *scr)

        if comm.mid is not None and n_steps >= 4:
            @pl.when(step == n_steps // 2)
            def _():
                comm.mid(cin, cout, sems)

        @pl.when(step == n_steps - 1)
        def _():
            if comm.mid is not None and n_steps < 4:
                comm.mid(cin, cout, sems)
            comm.wait(cin, cout, sems)

    any_spec = pl.BlockSpec(memory_space=pl.ANY)
    res = pl.pallas_call(
        carrier, name=name, grid=grid, in_specs=list(in_specs) + [any_spec] * ci,
        out_specs=list(out_specs) + [any_spec] * co, out_shape=list(out_shape) + list(comm.out_shapes),
        scratch_shapes=list(scratch_shapes) + list(comm.sems), input_output_aliases=aliases,
        compiler_params=_cparams(n_axes))(*args, *comm.ins)
    return list(res[:no]), list(res[no:])


def _comm_alone(comm, name):
    ci, co = len(comm.ins), len(comm.out_shapes)

    def body(*refs):
        cin, cout, sems = refs[:ci], refs[ci:ci + co], refs[ci + co:]
        comm.start(cin, cout, sems)
        if comm.mid is not None:
            comm.mid(cin, cout, sems)
        comm.wait(cin, cout, sems)

    any_spec = pl.BlockSpec(memory_space=pl.ANY)
    res = pl.pallas_call(
        body, name=name, out_shape=list(comm.out_shapes), in_specs=[any_spec] * ci, out_specs=[any_spec] * co,
        scratch_shapes=list(comm.sems))(*comm.ins)
    return list(res)


def _rms(x, g):
    r = lax.rsqrt(jnp.mean(x * x, axis=-1, keepdims=True) + EPS)
    return x * r * g, r


def _rms_bwd(x, r, g, dy):
    xh = x * r
    dxh = dy * g
    dx = r * (dxh - xh * jnp.mean(dxh * xh, axis=-1, keepdims=True))
    dg = jnp.sum(dy * xh, axis=0, keepdims=True)
    return dx, dg


def _gelu(x):
    return 0.5 * x * (1.0 + lax.erf(x * INV_SQRT2))


def _gelu_grad(x):
    return 0.5 * (1.0 + lax.erf(x * INV_SQRT2)) + x * jnp.exp(-0.5 * x * x) * INV_SQRT_2PI


def _shift_down(a, k):
    rows = lax.broadcasted_iota(jnp.int32, a.shape, 0)
    return jnp.where(rows >= k, pltpu.roll(a, k, 0), 0.0)


def _shift_up(a, k):
    n = a.shape[0]
    rows = lax.broadcasted_iota(jnp.int32, a.shape, 0)
    return jnp.where(rows < n - k, pltpu.roll(a, n - k, 0), 0.0)


def _tril_bf16(w):
    r = lax.broadcasted_iota(jnp.int32, w.shape, 0)
    c = lax.broadcasted_iota(jnp.int32, w.shape, 1)
    return jnp.where(r >= c, w, 0.0).astype(BF16)


def _ln_head(vh, g):
    mu = jnp.mean(vh, axis=-1, keepdims=True)
    xc = vh - mu
    rr = lax.rsqrt(jnp.mean(xc * xc, axis=-1, keepdims=True) + EPS)
    xh = xc * rr
    return xh * g, xh, rr


def _conv_fwd(z, tail, cw_ref):
    ext = jnp.concatenate([tail, z], axis=0)
    zs1 = _shift_down(ext, 1)[HALO:]
    zs2 = _shift_down(ext, 2)[HALO:]
    y = cw_ref[2:3, :] * z + cw_ref[1:2, :] * zs1 + cw_ref[0:1, :] * zs2
    return y, zs1, zs2


def _pool_cnt(shape, blk_in_seq):
    rows = lax.broadcasted_iota(jnp.int32, shape, 0)
    grp = lax.broadcasted_iota(jnp.int32, shape, 1) // POOL_GD
    win = jnp.where(grp == 0, 2, jnp.where(grp == 1, 4, jnp.where(grp == 2, 8, 16)))
    tpos = blk_in_seq * shape[0] + rows + 1
    return jnp.minimum(tpos, win).astype(F32), grp


def _pool_select(grp, s2, s4, s8, s16):
    return jnp.where(grp == 0, s2, jnp.where(grp == 1, s4, jnp.where(grp == 2, s8, s16)))


def _pool_fwd(z, tail, blk_in_seq):
    ext = jnp.concatenate([tail, z], axis=0)
    s2 = ext + _shift_down(ext, 1)
    s4 = s2 + _shift_down(s2, 2)
    s8 = s4 + _shift_down(s4, 4)
    s16 = s8 + _shift_down(s8, 8)
    cnt, grp = _pool_cnt(z.shape, blk_in_seq)
    sums = _pool_select(grp, s2[HALO:], s4[HALO:], s8[HALO:], s16[HALO:])
    return sums / cnt - z, cnt, grp


def _pool_bwd(dpooled, dpm, head, grp):
    n = dpm.shape[0]
    ext = jnp.concatenate([dpm, head], axis=0)
    u2 = ext + _shift_up(ext, 1)
    u4 = u2 + _shift_up(u2, 2)
    u8 = u4 + _shift_up(u4, 4)
    u16 = u8 + _shift_up(u8, 8)
    return _pool_select(grp, u2[:n], u4[:n], u8[:n], u16[:n]) - dpooled


def _lane_sums(a):
    return _dot(a.astype(BF16), jnp.ones((a.shape[1], a.shape[1]), BF16), 1, 0)


def _swap_halves(y1):
    src = lax.broadcasted_iota(jnp.int32, (128, 128), 0)
    dst = lax.broadcasted_iota(jnp.int32, (128, 128), 1)
    perm = jnp.where(((dst < 32) & (src == dst + 32)) | ((dst >= 32) & (dst < QK_ROPE) & (src == dst - 32)), 1.0, 0.0)
    return _dot(y1.astype(BF16), perm.astype(BF16), 1, 0)


def _rope(y1, c, s):
    return y1 * c + _swap_halves(y1) * s


def _rope_bwd(d1, c, s):
    return d1 * c + _swap_halves(d1 * s)


def _qk_prep(x, g, c, s):
    r = lax.rsqrt(_lane_sums(x * x) * (1.0 / QK_DIM) + EPS)
    y = x * r * g
    return jnp.concatenate([y[:, :128], _rope(y[:, 128:], c, s)], axis=1), r


def _qk_prep_bwd(dout, x, r, g, c, s):
    dy = jnp.concatenate([dout[:, :128], _rope_bwd(dout[:, 128:], c, s)], axis=1)
    xh = x * r
    dxh = dy * g
    dx = r * (dxh - xh * (_lane_sums(dxh * xh) * (1.0 / QK_DIM)))
    return dx, jnp.sum(dy * xh, axis=0, keepdims=True)


def _place():
    return lax.axis_index("x"), lax.axis_index("y"), lax.axis_index("c")


def _gather_comm(arrs):
    n = len(arrs)

    def halves(a):
        rows = arrs[a].shape[0]
        tile = 16 if arrs[a].dtype == BF16 else 8
        top = rows // 2 if rows % (2 * tile) == 0 else rows
        return (0, top), (top, rows - top)

    def plan(ins, outs, sems):
        send_sems, recv_sems, local_sems = sems
        x, y, c = _place()
        me, sib, xn, yn, dg = (x, y, c), (x, y, 1 - c), (1 - x, y, c), (x, 1 - y, c), (1 - x, 1 - y, c)

        def slot(a, dev, part=None):
            ref = outs[a].at[4 * dev[0] + 2 * dev[1] + dev[2]]
            return ref if part is None else ref.at[pl.ds(part[0], part[1])]

        def copy(a, k, block, to, src=None, part=None):
            return pltpu.make_async_remote_copy(
                src_ref=slot(a, block, part) if src is None else src, dst_ref=slot(a, block, part),
                send_sem=send_sems.at[a, k], recv_sem=recv_sems.at[a, k], device_id=to, device_id_type=MESH)

        local = [pltpu.make_async_copy(ins[a], slot(a, me), local_sems.at[a]) for a in range(n)]
        return me, sib, xn, yn, dg, copy, local

    def start(ins, outs, sems):
        me, sib, xn, yn, _, copy, local = plan(ins, outs, sems)
        for a in range(n):
            local[a].start()
            for k, to in enumerate((sib, xn, yn)):
                copy(a, k, me, to, src=ins[a]).start()

    def mid(ins, outs, sems):
        me, sib, xn, yn, _, copy, _ = plan(ins, outs, sems)
        for a in range(n):
            top, bottom = halves(a)
            copy(a, 1, xn, me).wait_recv()
            copy(a, 3, xn, yn, part=top).start()
            copy(a, 5, xn, sib).start()
            copy(a, 2, yn, me).wait_recv()
            if bottom[1]:
                copy(a, 4, yn, xn, part=bottom).start()
            copy(a, 6, yn, sib).start()

    def wait(ins, outs, sems):
        me, sib, xn, yn, dg, copy, local = plan(ins, outs, sems)
        other = lambda dev: (dev[0], dev[1], 1 - dev[2])
        for a in range(n):
            top, bottom = halves(a)
            copy(a, 3, dg, me, part=top).wait_recv()
            if bottom[1]:
                copy(a, 4, dg, me, part=bottom).wait_recv()
            copy(a, 7, dg, sib).start()
        for a in range(n):
            top, bottom = halves(a)
            for k, block in ((0, sib), (5, other(xn)), (6, other(yn)), (7, other(dg))):
                copy(a, k, block, me).wait_recv()
            for k, block in ((0, me), (1, me), (2, me), (5, xn), (6, yn), (7, dg)):
                copy(a, k, block, me, src=ins[a] if k < 3 else None).wait_send()
            copy(a, 3, xn, me, part=top).wait_send()
            if bottom[1]:
                copy(a, 4, yn, me, part=bottom).wait_send()
            local[a].wait()

    return _Comm(
        list(arrs), [jax.ShapeDtypeStruct((N_DEV,) + a.shape, a.dtype) for a in arrs],
        [pltpu.SemaphoreType.DMA((n, 8)), pltpu.SemaphoreType.DMA((n, 8)), pltpu.SemaphoreType.DMA((n,))],
        start, wait, mid)


def _sum_gathered(g):
    rows = g.shape[1]

    def body(g_ref, sum_ref):
        total = g_ref[0]
        for d in range(1, N_DEV):
            total = total + g_ref[d]
        sum_ref[...] = total

    return pl.pallas_call(
        body, name="sum_gathered_small", out_shape=jax.ShapeDtypeStruct((rows, 128), F32), grid=(1,),
        in_specs=[pl.BlockSpec((N_DEV, rows, 128), lambda i: (0, 0, 0))],
        out_specs=pl.BlockSpec((rows, 128), lambda i: (0, 0)), compiler_params=_cparams(1),
    )(g)


def _sum_rows(rows):
    return rows if rows <= 512 else rows // 2


def _pair_exchange_comm(gp):
    _, rows, cols = gp.shape

    def copies(ins, outs, sems):
        send_sems, recv_sems = sems
        x, y, c = _place()
        return [pltpu.make_async_remote_copy(
            src_ref=ins[0].at[2 * j + (1 - c)], dst_ref=outs[0].at[j], send_sem=send_sems.at[j],
            recv_sem=recv_sems.at[j], device_id=(x, y, 1 - c), device_id_type=MESH) for j in range(4)]

    def start(ins, outs, sems):
        for cp in copies(ins, outs, sems):
            cp.start()

    def wait(ins, outs, sems):
        for cp in copies(ins, outs, sems):
            cp.wait()

    return _Comm([gp], [jax.ShapeDtypeStruct((4, rows, cols), gp.dtype)],
                 [pltpu.SemaphoreType.DMA((4,)), pltpu.SemaphoreType.DMA((4,))], start, wait)


def _rs_pair_sum(gp, got, where, name):
    _, rows, cols = got.shape
    rb = _sum_rows(rows)
    gp4 = gp.reshape(4, 2, rows, cols)

    def body(w_ref, a_ref, b_ref, o_ref):
        o_ref[0] = (a_ref[0, 0].astype(F32) + b_ref[0].astype(F32)).astype(o_ref.dtype)

    return pl.pallas_call(
        body, name=name, out_shape=jax.ShapeDtypeStruct((4, rows, cols), gp.dtype),
        grid_spec=pltpu.PrefetchScalarGridSpec(
            num_scalar_prefetch=1, grid=(4, rows // rb),
            in_specs=[pl.BlockSpec((1, 1, rb, cols), lambda k, r, w: (w[1 + k], w[0], r, 0)),
                      pl.BlockSpec((1, rb, cols), lambda k, r, w: (w[1 + k], r, 0))],
            out_specs=pl.BlockSpec((1, rb, cols), lambda k, r, w: (k, r, 0))),
        compiler_params=_cparams(2),
    )(where, gp4, got)


def _chip_exchange_comm(pb):
    _, rows, cols = pb.shape

    def copies(ins, outs, sems):
        send_sems, recv_sems = sems
        x, y, c = _place()
        chips = [(1 - x, y), (x, 1 - y), (1 - x, 1 - y)]
        return [pltpu.make_async_remote_copy(
            src_ref=ins[0].at[1 + k], dst_ref=outs[0].at[k], send_sem=send_sems.at[k],
            recv_sem=recv_sems.at[k], device_id=(px, py, c), device_id_type=MESH)
            for k, (px, py) in enumerate(chips)]

    def start(ins, outs, sems):
        for cp in copies(ins, outs, sems):
            cp.start()

    def wait(ins, outs, sems):
        for cp in copies(ins, outs, sems):
            cp.wait()

    return _Comm([pb], [jax.ShapeDtypeStruct((3, rows, cols), pb.dtype)],
                 [pltpu.SemaphoreType.DMA((3,)), pltpu.SemaphoreType.DMA((3,))], start, wait)


def _chip_exchange_start(pb):
    _, rows, cols = pb.shape

    def body(pb_ref, land_ref, *rest):
        sems, token = rest[:6], rest[8]
        x, y, c = _place()
        chips = [(1 - x, y), (x, 1 - y), (1 - x, 1 - y)]
        for k, (px, py) in enumerate(chips):
            pltpu.make_async_remote_copy(
                src_ref=pb_ref.at[1 + k], dst_ref=land_ref.at[k], send_sem=sems[k], recv_sem=sems[3 + k],
                device_id=(px, py, c), device_id_type=MESH).start()
        token[...] = jnp.zeros_like(token)

    hbm = pl.BlockSpec(memory_space=pltpu.HBM)
    sem = pl.BlockSpec(memory_space=pltpu.SEMAPHORE)
    land = lax.empty((3, rows, cols), pb.dtype)
    res = pl.pallas_call(
        body, name="rs_chip_exchange_start_mix0",
        out_shape=(*[pltpu.SemaphoreType.DMA(())] * 6, pltpu.HBM(pb.shape, pb.dtype), pltpu.HBM(land.shape, land.dtype),
                   jax.ShapeDtypeStruct((8, 128), F32)),
        in_specs=(hbm, hbm), out_specs=(*[sem] * 6, hbm, hbm, pl.BlockSpec(memory_space=pltpu.VMEM)),
        input_output_aliases={0: 6, 1: 7},
        compiler_params=pltpu.CompilerParams(has_side_effects=pltpu.SideEffectType.DATAFLOW_SIDE_EFFECTING),
    )(pltpu.with_memory_space_constraint(pb, pltpu.HBM), pltpu.with_memory_space_constraint(land, pltpu.HBM))
    return list(res[:6]), res[6], res[7], res[8]


def _chip_exchange_wait(sems, pb_thru, land_thru, after):
    def body(pb_ref, land_ref, *rest):
        sems_in = rest[:6]
        x, y, c = _place()
        chips = [(1 - x, y), (x, 1 - y), (1 - x, 1 - y)]
        for k, (px, py) in enumerate(chips):
            cp = pltpu.make_async_remote_copy(
                src_ref=pb_ref.at[1 + k], dst_ref=land_ref.at[k], send_sem=sems_in[k], recv_sem=sems_in[3 + k],
                device_id=(px, py, c), device_id_type=MESH)
            cp.wait_send()
            cp.wait_recv()

    hbm = pl.BlockSpec(memory_space=pltpu.HBM)
    sem = pl.BlockSpec(memory_space=pltpu.SEMAPHORE)
    res = pl.pallas_call(
        body, name="rs_chip_exchange_wait_mix0",
        out_shape=(pltpu.HBM(pb_thru.shape, pb_thru.dtype), pltpu.HBM(land_thru.shape, land_thru.dtype)),
        in_specs=(hbm, hbm, *[sem] * 6, pl.BlockSpec(memory_space=pl.ANY)), out_specs=(hbm, hbm),
        input_output_aliases={0: 0, 1: 1},
        compiler_params=pltpu.CompilerParams(has_side_effects=pltpu.SideEffectType.DATAFLOW_SIDE_EFFECTING),
    )(pb_thru, land_thru, *sems, after)
    return res[0], res[1]


def _rs_final_sums(pbs, gots, name, after=None):
    n = len(pbs)

    def body(*refs):
        outs = refs[len(refs) - n:]
        for a in range(n):
            m_ref, g_ref, o_ref = refs[a], refs[n + a], outs[a]
            o_ref[...] = ((m_ref[0].astype(F32) + g_ref[0].astype(F32)) + g_ref[1].astype(F32)) + g_ref[2].astype(F32)

    half = [pb.shape[1] // 2 for pb in pbs]
    in_specs = ([pl.BlockSpec((1, h, D), lambda i: (0, i, 0)) for h in half]
                + [pl.BlockSpec((3, h, D), lambda i: (0, i, 0)) for h in half])
    args = (*pbs, *gots)
    if after is not None:
        in_specs, args = in_specs + [pl.BlockSpec(memory_space=pl.ANY)], args + (after,)
    res, _ = _call(body, name, (2,), in_specs, [pl.BlockSpec((h, D), lambda i: (i, 0)) for h in half],
                   [jax.ShapeDtypeStruct(pb.shape[1:], F32) for pb in pbs], args)
    return res


def _rope_tables(pos_col, inv_freq, comm=None):
    t = pos_col.shape[0]

    def body(p_ref, f_ref, c_ref, s_ref):
        ang = p_ref[...].astype(F32) * f_ref[...]
        lane = lax.broadcasted_iota(jnp.int32, ang.shape, 1)
        c_ref[...] = jnp.where(lane < QK_ROPE, jnp.cos(ang), 0.0)
        s = jnp.sin(ang)
        s_ref[...] = jnp.where(lane < 32, -s, jnp.where(lane < QK_ROPE, s, 0.0))

    spec = pl.BlockSpec((TB, 128), lambda i: (i, 0))
    return _call(
        body, "rope_tables", (t // TB,), [pl.BlockSpec((TB, 1), lambda i: (i, 0)), _const_spec((1, 128))],
        [spec] * 2, [jax.ShapeDtypeStruct((t, 128), F32)] * 2, (pos_col, inv_freq), (), comm)


def _sgu_conv_fwd(proj, tail, lng_ref, ws_ref, bst_ref, cw_ref):
    gu = _gelu(proj[:, 0:SG_W])
    gv = _gelu(proj[:, SG_W:2 * SG_W])
    bg = proj[:, 1024:1536]
    z = proj[:, 1536:2048] * proj[:, 2048:2560]
    heads = []
    for h in range(SG_HEADS):
        sl = slice(h * SG_HD, (h + 1) * SG_HD)
        vn, _, _ = _ln_head(gv[:, sl], lng_ref[:, sl])
        vnb = vn.astype(BF16)
        wm = _tril_bf16(ws_ref[h])
        bcol = bst_ref[:, h:h + 1]
        mixed = jnp.concatenate(
            [_dot(wm, vnb[k * SG_CHUNK:(k + 1) * SG_CHUNK], 1, 0) + bcol for k in range(TB // SG_CHUNK)], axis=0)
        heads.append(gu[:, sl] * mixed)
    a_out = jnp.concatenate(heads, axis=1)
    y, _, _ = _conv_fwd(z, tail, cw_ref)
    return a_out, bg * y, z


def _even_fwd(x, wg, gamma, lng, ws, bst, cw, seq, comm=None):
    t = x.shape[0]
    nbs = seq // TB

    def body(x_ref, gam_ref, win_ref, wout_ref, lng_ref, ws_ref, bst_ref, cw_ref, x1_ref, proj_ref, tail_ref):
        i = pl.program_id(0)
        xv = x_ref[...]
        h, _ = _rms(xv, gam_ref[...])
        proj = _dot(h.astype(BF16), win_ref[...].reshape(EVEN_IN, D), 1, 1)
        proj_ref[...] = proj.astype(BF16)
        tail = jnp.where(i % nbs == 0, 0.0, tail_ref[...])
        a_out, b_out, z = _sgu_conv_fwd(proj, tail, lng_ref, ws_ref, bst_ref, cw_ref)
        tail_ref[...] = z[TB - HALO:, :]
        x1_ref[...] = (xv + _dot(a_out.astype(BF16), wout_ref[0:4].reshape(512, D), 1, 0)
                       + _dot(b_out.astype(BF16), wout_ref[4:8].reshape(512, D), 1, 0))

    row = pl.BlockSpec((TB, D), lambda i: (i, 0))
    return _call(
        body, "even_fwd", (t // TB,),
        [row, _const_spec((1, D)), _wspec(N_EIN, OFF_EIN), _wspec(N_SQ, OFF_EOUT), _const_spec((1, SG_W)),
         _const_spec((SG_HEADS, 128, 128)), _const_spec((128, 128)), _const_spec((8, SC_W))],
        [row, pl.BlockSpec((TB, EVEN_IN), lambda i: (i, 0))],
        [jax.ShapeDtypeStruct((t, D), F32), jax.ShapeDtypeStruct((t, EVEN_IN), BF16)],
        (x, gamma, wg, wg, lng, ws, bst, cw), [pltpu.VMEM((HALO, SC_W), F32)], comm)


def _even_bwd(x, proj, dx1, wg, gamma, lng, ws, bst, cw, seq, comm=None):
    t = x.shape[0]
    nb, nbs = t // TB, seq // TB

    def body(x_ref, proj_ref, ptail_ref, dx1_ref, gam_ref, win_ref, wout_ref, lng_ref, ws_ref, bst_ref, cw_ref,
             dx0_ref, dproj_ref, mix_ref, h_ref, dgam_ref, dws_ref, dbc_ref, dlng_ref, dcw_ref, head_ref):
        i = pl.program_id(0)
        blk = nb - 1 - i

        @pl.when(i == 0)
        def _():
            dgam_ref[...] = jnp.zeros_like(dgam_ref)
            dws_ref[...] = jnp.zeros_like(dws_ref)
            dbc_ref[...] = jnp.zeros_like(dbc_ref)
            dlng_ref[...] = jnp.zeros_like(dlng_ref)
            dcw_ref[...] = jnp.zeros_like(dcw_ref)

        xv = x_ref[...]
        gam = gam_ref[...]
        h, r = _rms(xv, gam)
        h_ref[...] = h.astype(BF16)
        dx1 = dx1_ref[...]
        dmix = _dot(dx1.astype(BF16), wout_ref[...].reshape(D, D), 1, 1)
        da, db = dmix[:, :SG_W], dmix[:, SG_W:]
        proj = proj_ref[...].astype(F32)
        u, v = proj[:, 0:SG_W], proj[:, SG_W:2 * SG_W]
        bg, cg, hv = proj[:, 1024:1536], proj[:, 1536:2048], proj[:, 2048:2560]
        gu, gv = _gelu(u), _gelu(v)

        a_heads, dgv_heads = [], []
        for hd in range(SG_HEADS):
            sl = slice(hd * SG_HD, (hd + 1) * SG_HD)
            g_h = lng_ref[:, sl]
            vn, xh, rr = _ln_head(gv[:, sl], g_h)
            vnb = vn.astype(BF16)
            wm = _tril_bf16(ws_ref[hd])
            bcol = bst_ref[:, hd:hd + 1]
            mixed_c, dvn_c = [], []
            dw_acc = jnp.zeros((128, 128), F32)
            db_acc = jnp.zeros((128, 1), F32)
            for k in range(TB // SG_CHUNK):
                rs = slice(k * SG_CHUNK, (k + 1) * SG_CHUNK)
                mixed = _dot(wm, vnb[rs], 1, 0) + bcol
                dmixed = da[rs, sl] * gu[rs, sl]
                dmb = dmixed.astype(BF16)
                dvn_c.append(_dot(wm, dmb, 0, 0))
                dw_acc = dw_acc + _dot(dmb, vnb[rs], 1, 1)
                db_acc = db_acc + jnp.sum(dmixed, axis=1, keepdims=True)
                mixed_c.append(mixed)
            mixed_h = jnp.concatenate(mixed_c, axis=0)
            dvn = jnp.concatenate(dvn_c, axis=0)
            r_i = lax.broadcasted_iota(jnp.int32, (128, 128), 0)
            c_i = lax.broadcasted_iota(jnp.int32, (128, 128), 1)
            dws_ref[hd] += jnp.where(r_i >= c_i, dw_acc, 0.0)
            dbc_ref[:, hd:hd + 1] += db_acc
            dlng_ref[:, sl] += jnp.sum(dvn * xh, axis=0, keepdims=True)
            dxh = dvn * g_h
            dgv = rr * (dxh - jnp.mean(dxh, axis=-1, keepdims=True)
                        - xh * jnp.mean(dxh * xh, axis=-1, keepdims=True))
            a_heads.append(gu[:, sl] * mixed_h)
            dproj_ref[:, sl] = (da[:, sl] * mixed_h * _gelu_grad(u[:, sl])).astype(BF16)
            dgv_heads.append(dgv * _gelu_grad(v[:, sl]))
        dproj_ref[:, SG_W:2 * SG_W] = jnp.concatenate(dgv_heads, axis=1).astype(BF16)
        mix_ref[:, :SG_W] = jnp.concatenate(a_heads, axis=1).astype(BF16)

        z = cg * hv
        pt = ptail_ref[...].astype(F32)
        tail = jnp.where(blk % nbs == 0, 0.0, pt[:, 1536:2048] * pt[:, 2048:2560])
        y, zs1, zs2 = _conv_fwd(z, tail, cw_ref)
        mix_ref[:, SG_W:] = (bg * y).astype(BF16)
        dy = db * bg
        head = jnp.where(blk % nbs == nbs - 1, 0.0, head_ref[...])
        ext = jnp.concatenate([dy, head], axis=0)
        dz = (cw_ref[2:3, :] * dy + cw_ref[1:2, :] * _shift_up(ext, 1)[:TB]
              + cw_ref[0:1, :] * _shift_up(ext, 2)[:TB])
        head_ref[...] = dy[:HALO, :]
        dcw_ref[2:3, :] += jnp.sum(dy * z, axis=0, keepdims=True)
        dcw_ref[1:2, :] += jnp.sum(dy * zs1, axis=0, keepdims=True)
        dcw_ref[0:1, :] += jnp.sum(dy * zs2, axis=0, keepdims=True)
        dproj_ref[:, 1024:1536] = (db * y).astype(BF16)
        dproj_ref[:, 1536:2048] = (dz * hv).astype(BF16)
        dproj_ref[:, 2048:2560] = (dz * cg).astype(BF16)

        dh = _dot(dproj_ref[...], win_ref[...].reshape(EVEN_IN, D), 1, 0)
        dxn, dgam = _rms_bwd(xv, r, gam, dh)
        dgam_ref[...] += dgam
        dx0_ref[...] = dx1 + dxn

    def rev(w):
        return pl.BlockSpec((TB, w), lambda i: (nb - 1 - i, 0))

    ptail = pl.BlockSpec((HALO, EVEN_IN), lambda i: (jnp.maximum((nb - 1 - i) * (TB // HALO) - 1, 0), 0))
    return _call(
        body, "even_bwd", (nb,),
        [rev(D), rev(EVEN_IN), ptail, rev(D), _const_spec((1, D)), _wspec(N_EIN, OFF_EIN),
         _wspec(N_SQ, OFF_EOUT), _const_spec((1, SG_W)), _const_spec((SG_HEADS, 128, 128)),
         _const_spec((128, 128)), _const_spec((8, SC_W))],
        [rev(D), rev(EVEN_IN), rev(D), rev(D), _const_spec((1, D)), _const_spec((SG_HEADS, 128, 128)),
         _const_spec((128, 128)), _const_spec((1, SG_W)), _const_spec((8, SC_W))],
        [jax.ShapeDtypeStruct((t, D), F32), jax.ShapeDtypeStruct((t, EVEN_IN), BF16),
         jax.ShapeDtypeStruct((t, D), BF16), jax.ShapeDtypeStruct((t, D), BF16),
         jax.ShapeDtypeStruct((1, D), F32), jax.ShapeDtypeStruct((SG_HEADS, 128, 128), F32),
         jax.ShapeDtypeStruct((128, 128), F32), jax.ShapeDtypeStruct((1, SG_W), F32),
         jax.ShapeDtypeStruct((8, SC_W), F32)],
        (x, proj, proj, dx1, gamma, wg, wg, lng, ws, bst, cw), [pltpu.VMEM((HALO, SC_W), F32)], comm)


def _last_block_fwd(x, c_out, d_out, w_mix1, w_gu, w_d, gamma, target):
    t = x.shape[0]

    def body(x_ref, c_ref, d_ref, wo_ref, gam_ref, wg_ref, wu_ref, wd_ref, t_ref,
             x3_ref, dy_ref, g_ref, u_ref, loss_ref):
        @pl.when(pl.program_id(0) == 0)
        def _():
            loss_ref[...] = jnp.zeros_like(loss_ref)

        xv = (x_ref[...] + _dot(c_ref[...], wo_ref[0:2].reshape(POOL_W, D), 1, 0)
              + _dot(d_ref[...], wo_ref[2:8].reshape(HEADS * V_DIM, D), 1, 0))
        x3_ref[...] = xv
        h, _ = _rms(xv, gam_ref[...])
        hb = h.astype(BF16)
        g = _dot(hb, wg_ref[...].reshape(D_FF, D), 1, 1)
        u = _dot(hb, wu_ref[...].reshape(D_FF, D), 1, 1)
        g_ref[...] = g.astype(BF16)
        u_ref[...] = u.astype(BF16)
        act = g * jax.nn.sigmoid(g) * u
        err = xv + _dot(act.astype(BF16), wd_ref[...].reshape(D_FF, D), 1, 0) - t_ref[...]
        dy_ref[...] = err * (1.0 / D)
        sq = jnp.sum(jnp.sum(err * err, axis=-1, keepdims=True), axis=0, keepdims=True)
        loss_ref[...] += (0.5 / D) * sq

    def row(w):
        return pl.BlockSpec((TB, w), lambda i: (i, 0))

    res, _ = _call(
        body, "last_block_fwd", (t // TB,),
        [row(D), row(POOL_W), row(HEADS * V_DIM), _wspec(N_SQ, OFF_OOUT), _const_spec((1, D)),
         _wspec(N_FF, OFF_GATE), _wspec(N_FF, OFF_UP), _wspec(N_FF, 0), row(D)],
        [row(D), row(D), row(D_FF), row(D_FF), _const_spec((8, 128))],
        [jax.ShapeDtypeStruct((t, D), F32), jax.ShapeDtypeStruct((t, D), F32), jax.ShapeDtypeStruct((t, D_FF), BF16),
         jax.ShapeDtypeStruct((t, D_FF), BF16), jax.ShapeDtypeStruct((8, 128), F32)],
        (x, c_out, d_out, w_mix1, gamma, w_gu, w_gu, w_d, target))
    return res


def _ffn_up(x, w_gu, gamma, name, comm=None):
    t = x.shape[0]

    def body(x_ref, gam_ref, wg_ref, wu_ref, g_ref, u_ref, act_ref):
        h, _ = _rms(x_ref[...], gam_ref[...])
        hb = h.astype(BF16)
        g = _dot(hb, wg_ref[...].reshape(D_FF, D), 1, 1)
        u = _dot(hb, wu_ref[...].reshape(D_FF, D), 1, 1)
        g_ref[...] = g.astype(BF16)
        u_ref[...] = u.astype(BF16)
        act_ref[...] = (g * jax.nn.sigmoid(g) * u).astype(BF16)

    row = pl.BlockSpec((TB, D), lambda i: (i, 0))
    wide = pl.BlockSpec((TB, D_FF), lambda i: (i, 0))
    return _call(body, name, (t // TB,), [row, _const_spec((1, D)), _wspec(N_FF, OFF_GATE), _wspec(N_FF, OFF_UP)],
                 [wide, wide, wide], [jax.ShapeDtypeStruct((t, D_FF), BF16)] * 3, (x, gamma, w_gu, w_gu), (), comm)


def _ffn_down(x, act, w_d, name, comm=None):
    t = x.shape[0]

    def body(x_ref, a_ref, wd_ref, y_ref):
        y_ref[...] = x_ref[...] + _dot(a_ref[...], wd_ref[...].reshape(D_FF, D), 1, 0)

    row = pl.BlockSpec((TB, D), lambda i: (i, 0))
    wide = pl.BlockSpec((TB, D_FF), lambda i: (i, 0))
    return _call(body, name, (t // TB,), [row, wide, _wspec(N_FF, 0)], [row], [jax.ShapeDtypeStruct((t, D), F32)],
                 (x, act, w_d), (), comm)


def _ffn_bwd(x, g, u, dy, w_gu, w_d, gamma, name, comm=None, w_mix1=None):
    t = x.shape[0]
    with_dmix = w_mix1 is not None

    def body(*refs):
        x_ref, g_ref, u_ref, dy_ref, gam_ref, wg_ref, wu_ref, wd_ref = refs[:8]
        dx_ref, act_ref, dg_ref, du_ref, h_ref, dgam_ref = refs[8 + with_dmix:14 + with_dmix]

        @pl.when(pl.program_id(0) == 0)
        def _():
            dgam_ref[...] = jnp.zeros_like(dgam_ref)

        xv = x_ref[...]
        gam = gam_ref[...]
        h, r = _rms(xv, gam)
        h_ref[...] = h.astype(BF16)
        dyv = dy_ref[...]
        dact = _dot(dyv.astype(BF16), wd_ref[...].reshape(D_FF, D), 1, 1)
        gv = g_ref[...].astype(F32)
        uv = u_ref[...].astype(F32)
        sg = jax.nn.sigmoid(gv)
        silu = gv * sg
        act_ref[...] = (silu * uv).astype(BF16)
        dgb = (dact * uv * (sg * (1.0 + gv * (1.0 - sg)))).astype(BF16)
        dub = (dact * silu).astype(BF16)
        dg_ref[...] = dgb
        du_ref[...] = dub
        dh = _dot(dgb, wg_ref[...].reshape(D_FF, D), 1, 0) + _dot(dub, wu_ref[...].reshape(D_FF, D), 1, 0)
        dxn, dgam = _rms_bwd(xv, r, gam, dh)
        dgam_ref[...] += dgam
        dx = dyv + dxn
        dx_ref[...] = dx
        if with_dmix:
            refs[15][...] = _dot(dx.astype(BF16), refs[8][...].reshape(D, D), 1, 1).astype(BF16)

    row = pl.BlockSpec((TB_FFN_BWD, D), lambda i: (i, 0))
    wide = pl.BlockSpec((TB_FFN_BWD, D_FF), lambda i: (i, 0))
    in_specs = [row, wide, wide, row, _const_spec((1, D)), _wspec(N_FF, OFF_GATE), _wspec(N_FF, OFF_UP),
                _wspec(N_FF, 0)]
    out_specs = [row, wide, wide, wide, row, _const_spec((1, D))]
    out_shape = [jax.ShapeDtypeStruct((t, D), F32), jax.ShapeDtypeStruct((t, D_FF), BF16),
                 jax.ShapeDtypeStruct((t, D_FF), BF16), jax.ShapeDtypeStruct((t, D_FF), BF16),
                 jax.ShapeDtypeStruct((t, D), BF16), jax.ShapeDtypeStruct((1, D), F32)]
    args = (x, g, u, dy, gamma, w_gu, w_gu, w_d)
    if with_dmix:
        in_specs, args = in_specs + [_wspec(N_SQ, OFF_OOUT)], args + (w_mix1,)
        out_specs, out_shape = out_specs + [row], out_shape + [jax.ShapeDtypeStruct((t, D), BF16)]
    return _call(body, name, (t // TB_FFN_BWD,), in_specs, out_specs, out_shape, args, (), comm)


def _odd_pre_fwd(x, wg, gamma, qbt, kvbt, qa_g, kva_g, pw_bd, pscale, seq, comm=None):
    t = x.shape[0]
    nbs = seq // TB

    def body(x_ref, gam_ref, win_ref, qb_ref, kvb_ref, qa_ref, kva_ref, pw_ref, ps_ref,
             proj_ref, q_ref, kv_ref, kr_ref, c_ref, tail_ref):
        i = pl.program_id(0)
        h, _ = _rms(x_ref[...], gam_ref[...])
        proj = _dot(h.astype(BF16), win_ref[...].reshape(D, D), 1, 0)
        proj_ref[...] = proj.astype(BF16)
        zp, ql, kvl = proj[:, :POOL_W], proj[:, 256:640], proj[:, 640:896]
        kr_ref[...] = proj[:, 896:1024]
        qn, _ = _rms(ql, qa_ref[...])
        q_ref[...] = _dot(qn.astype(BF16), qb_ref[...], 1, 1).astype(BF16)
        kvn, _ = _rms(kvl, kva_ref[...])
        kv_ref[...] = _dot(kvn.astype(BF16), kvb_ref[...], 1, 1).astype(BF16)
        tail = jnp.where(i % nbs == 0, 0.0, tail_ref[...])
        pooled, _, _ = _pool_fwd(zp, tail, i % nbs)
        tail_ref[...] = zp[TB - HALO:, :]
        c_ref[...] = (_dot(pooled.astype(BF16), pw_ref[...], 1, 0) * ps_ref[...]).astype(BF16)

    def row(w):
        return pl.BlockSpec((TB, w), lambda i: (i, 0))

    return _call(
        body, "odd_pre_fwd", (t // TB,),
        [row(D), _const_spec((1, D)), _wspec(N_SQ, OFF_OIN), _const_spec((HEADS * HP, Q_LORA)),
         _const_spec((HEADS * HP, KV_LORA)), _const_spec((1, Q_LORA)), _const_spec((1, KV_LORA)),
         _const_spec((POOL_W, POOL_W)), _const_spec((1, POOL_W))],
        [row(D), row(HEADS * HP), row(HEADS * HP), row(128), row(POOL_W)],
        [jax.ShapeDtypeStruct((t, D), BF16), jax.ShapeDtypeStruct((t, HEADS * HP), BF16),
         jax.ShapeDtypeStruct((t, HEADS * HP), BF16), jax.ShapeDtypeStruct((t, 128), F32),
         jax.ShapeDtypeStruct((t, POOL_W), BF16)],
        (x, gamma, wg, qbt, kvbt, qa_g, kva_g, pw_bd, pscale), [pltpu.VMEM((HALO, POOL_W), F32)], comm)


def _odd_pre_bwd(x, proj, dx3, dmix, dq, dkv, dkr, wg, gamma, qbt, kvbt, qa_g, kva_g, pw_bd, pscale, seq):
    t = x.shape[0]
    nb, nbs = t // TB, seq // TB

    def body(x_ref, proj_ref, ptail_ref, dx3_ref, dco_ref, dq_ref, dkv_ref, dkr_ref, gam_ref, win_ref, qb_ref,
             kvb_ref, qa_ref, kva_ref, pw_ref, ps_ref,
             dx2_ref, dproj_ref, h_ref, qn_ref, kvn_ref, dgam_ref, dqa_ref, dkva_ref, dpw_ref, dps_ref, head_ref):
        i = pl.program_id(0)
        blk = nb - 1 - i

        @pl.when(i == 0)
        def _():
            dgam_ref[...] = jnp.zeros_like(dgam_ref)
            dqa_ref[...] = jnp.zeros_like(dqa_ref)
            dkva_ref[...] = jnp.zeros_like(dkva_ref)
            dpw_ref[...] = jnp.zeros_like(dpw_ref)
            dps_ref[...] = jnp.zeros_like(dps_ref)

        xv = x_ref[...]
        gam = gam_ref[...]
        h, r = _rms(xv, gam)
        h_ref[...] = h.astype(BF16)
        proj = proj_ref[...].astype(F32)
        zp, ql, kvl = proj[:, :POOL_W], proj[:, 256:640], proj[:, 640:896]

        qa = qa_ref[...]
        qn, rq = _rms(ql, qa)
        qn_ref[...] = qn.astype(BF16)
        dql, dqa = _rms_bwd(ql, rq, qa, _dot(dq_ref[...], qb_ref[...], 1, 0))
        dqa_ref[...] += dqa
        kva = kva_ref[...]
        kvn, rkv = _rms(kvl, kva)
        kvn_ref[...] = kvn.astype(BF16)
        dkvl, dkva = _rms_bwd(kvl, rkv, kva, _dot(dkv_ref[...], kvb_ref[...], 1, 0))
        dkva_ref[...] += dkva

        pt = ptail_ref[...].astype(F32)
        tail = jnp.where(blk % nbs == 0, 0.0, pt[:, :POOL_W])
        pooled, cnt, grp = _pool_fwd(zp, tail, blk % nbs)
        pb = pooled.astype(BF16)
        pw = pw_ref[...]
        dco = dco_ref[...].astype(F32)
        dps_ref[...] += jnp.sum(dco * _dot(pb, pw, 1, 0), axis=0, keepdims=True)
        dpo = (dco * ps_ref[...]).astype(BF16)
        dpw_ref[...] += _dot(pb, dpo, 0, 0)
        dpooled = _dot(dpo, pw, 1, 1)
        dpm = dpooled / cnt
        head = jnp.where(blk % nbs == nbs - 1, 0.0, head_ref[...])
        dz = _pool_bwd(dpooled, dpm, head, grp)
        head_ref[...] = dpm[:HALO, :]

        dproj_ref[:, :POOL_W] = dz.astype(BF16)
        dproj_ref[:, 256:640] = dql.astype(BF16)
        dproj_ref[:, 640:896] = dkvl.astype(BF16)
        dproj_ref[:, 896:1024] = dkr_ref[...].astype(BF16)
        dh = _dot(dproj_ref[...], win_ref[...].reshape(D, D), 1, 1)
        dxn, dgam = _rms_bwd(xv, r, gam, dh)
        dgam_ref[...] += dgam
        dx2_ref[...] = dx3_ref[...] + dxn

    def rev(w):
        return pl.BlockSpec((TB, w), lambda i: (nb - 1 - i, 0))

    ptail = pl.BlockSpec((HALO, D), lambda i: (jnp.maximum((nb - 1 - i) * (TB // HALO) - 1, 0), 0))
    return pl.pallas_call(
        body, name="odd_pre_bwd",
        out_shape=[jax.ShapeDtypeStruct((t, D), F32), jax.ShapeDtypeStruct((t, D), BF16),
                   jax.ShapeDtypeStruct((t, D), BF16), jax.ShapeDtypeStruct((t, Q_LORA), BF16),
                   jax.ShapeDtypeStruct((t, KV_LORA), BF16), jax.ShapeDtypeStruct((1, D), F32),
                   jax.ShapeDtypeStruct((1, Q_LORA), F32), jax.ShapeDtypeStruct((1, KV_LORA), F32),
                   jax.ShapeDtypeStruct((POOL_W, POOL_W), F32), jax.ShapeDtypeStruct((1, POOL_W), F32)],
        grid=(nb,),
        in_specs=[rev(D), rev(D), ptail, rev(D), rev(POOL_W), rev(HEADS * HP), rev(HEADS * HP), rev(128),
                  _const_spec((1, D)), _wspec(N_SQ, OFF_OIN), _const_spec((HEADS * HP, Q_LORA)),
                  _const_spec((HEADS * HP, KV_LORA)), _const_spec((1, Q_LORA)), _const_spec((1, KV_LORA)),
                  _const_spec((POOL_W, POOL_W)), _const_spec((1, POOL_W))],
        out_specs=[rev(D), rev(D), rev(D), rev(Q_LORA), rev(KV_LORA), _const_spec((1, D)), _const_spec((1, Q_LORA)),
                   _const_spec((1, KV_LORA)), _const_spec((POOL_W, POOL_W)), _const_spec((1, POOL_W))],
        scratch_shapes=[pltpu.VMEM((HALO, POOL_W), F32)],
        compiler_params=_cparams(1),
    )(x, proj, proj, dx3, dmix, dq, dkv, dkr, gamma, wg, qbt, kvbt, qa_g, kva_g, pw_bd, pscale)


def _attn_specs(seq):
    head = pl.BlockSpec((seq, HP), lambda b, h: (b, h))
    shared = pl.BlockSpec((seq, 128), lambda b, h: (b, 0))
    gain = pl.BlockSpec((1, HP), lambda b, h: (0, 0))
    return head, shared, gain


def _causal_bias(n):
    rows = lax.broadcasted_iota(jnp.int32, (n, n), 0)
    cols = lax.broadcasted_iota(jnp.int32, (n, n), 1)
    return jnp.where(cols <= rows, 0.0, NEG_INF)


def _attn_fwd(q, kv, kr, cos, sin, gq, gk, seq, comm=None):
    t = q.shape[0]
    qb = min(512, seq)

    def body(q_ref, kv_ref, kr_ref, c_ref, s_ref, gq_ref, gk_ref, o_ref, lse_ref):
        c, s = c_ref[...], s_ref[...]
        qf, _ = _qk_prep(q_ref[...].astype(F32), gq_ref[...], c, s)
        kin = jnp.concatenate([kv_ref[:, :128].astype(F32), kr_ref[...]], axis=1)
        kf, _ = _qk_prep(kin, gk_ref[...], c, s)
        qf, kf = qf.astype(BF16), kf.astype(BF16)
        v1 = jnp.concatenate([kv_ref[:, 128:], jnp.ones((seq, V_DIM), BF16)], axis=1)
        bias = _causal_bias(qb)
        for q0 in range(0, seq, qb):
            q1 = q0 + qb
            qblk = qf[q0:q1]
            s_dg = _dot(qblk, kf[q0:q1], 1, 1) + bias
            m = jnp.max(s_dg, axis=-1, keepdims=True)
            if q0:
                s_off = _dot(qblk, kf[:q0], 1, 1)
                m = jnp.maximum(m, jnp.max(s_off, axis=-1, keepdims=True))
            acc = _dot(jnp.exp(s_dg - m).astype(BF16), v1[q0:q1], 1, 0)
            if q0:
                acc = acc + _dot(jnp.exp(s_off - m).astype(BF16), v1[:q0], 1, 0)
            l = acc[:, V_DIM:]
            o_ref[q0:q1, :] = (acc[:, :V_DIM] / l).astype(BF16)
            lse_ref[q0:q1, :] = m + jnp.log(l)

    head, shared, gain = _attn_specs(seq)
    per_head = pl.BlockSpec((seq, V_DIM), lambda b, h: (b, h))
    return _call(
        body, "attn_fwd", (t // seq, HEADS),
        [head, head, shared, shared, shared, gain, gain], [per_head, per_head],
        [jax.ShapeDtypeStruct((t, HEADS * V_DIM), BF16), jax.ShapeDtypeStruct((t, HEADS * V_DIM), F32)],
        (q, kv, kr, cos, sin, gq, gk), (), comm)


def _attn_bwd(q, kv, kr, cos, sin, gq, gk, dmix, d_out, lse, seq, comm=None):
    t = q.shape[0]
    qb = min(512, seq)

    def body(q_ref, kv_ref, kr_ref, c_ref, s_ref, gq_ref, gk_ref, do_ref, o_ref, lse_ref,
             dq_ref, dkv_ref, dkr_ref, dgq_ref, dgk_ref, dqf_ref, dkf_ref, dv_ref):
        b, hd = pl.program_id(0), pl.program_id(1)

        @pl.when((b == 0) & (hd == 0))
        def _():
            dgq_ref[...] = jnp.zeros_like(dgq_ref)
            dgk_ref[...] = jnp.zeros_like(dgk_ref)

        c, sn = c_ref[...], s_ref[...]
        gq_v, gk_v = gq_ref[...], gk_ref[...]
        qin = q_ref[...].astype(F32)
        kin = jnp.concatenate([kv_ref[:, :128].astype(F32), kr_ref[...]], axis=1)
        qf32, rq = _qk_prep(qin, gq_v, c, sn)
        kf32, rk = _qk_prep(kin, gk_v, c, sn)
        qf, kf = qf32.astype(BF16), kf32.astype(BF16)
        vb = kv_ref[:, 128:]
        dkf_ref[...] = jnp.zeros_like(dkf_ref)
        dv_ref[...] = jnp.zeros_like(dv_ref)
        bias = _causal_bias(qb)
        for q0 in range(0, seq, qb):
            q1 = q0 + qb
            qblk = qf[q0:q1]
            do = do_ref[q0:q1, :]
            lse_col = lse_ref[q0:q1, 0:1]
            d_col = jnp.sum(do.astype(F32) * o_ref[q0:q1, :].astype(F32), axis=-1, keepdims=True)
            dq_acc = None
            for k0, k1, diag in ((q0, q1, True), (0, q0, False)):
                if k1 == k0:
                    continue
                s = _dot(qblk, kf[k0:k1], 1, 1)
                p = jnp.exp((s + bias if diag else s) - lse_col)
                dv_ref[k0:k1, :] += _dot(p.astype(BF16), do, 0, 0)
                ds = (p * (_dot(do, vb[k0:k1], 1, 1) - d_col)).astype(BF16)
                part = _dot(ds, kf[k0:k1], 1, 0)
                dq_acc = part if dq_acc is None else dq_acc + part
                dkf_ref[k0:k1, :] += _dot(ds, qblk, 0, 0)
            dqf_ref[q0:q1, :] = dq_acc
        dqin, dgq = _qk_prep_bwd(dqf_ref[...], qin, rq, gq_v, c, sn)
        dkin, dgk = _qk_prep_bwd(dkf_ref[...], kin, rk, gk_v, c, sn)
        dgq_ref[...] += dgq
        dgk_ref[...] += dgk
        dq_ref[...] = dqin.astype(BF16)
        dkv_ref[:, :128] = dkin[:, :128].astype(BF16)
        dkv_ref[:, 128:] = dv_ref[...].astype(BF16)

        @pl.when(hd == 0)
        def _():
            dkr_ref[...] = dkin[:, 128:]

        @pl.when(hd != 0)
        def _():
            dkr_ref[...] += dkin[:, 128:]

    head, shared, gain = _attn_specs(seq)
    per_head = pl.BlockSpec((seq, V_DIM), lambda b, h: (b, h))
    return _call(
        body, "attn_bwd", (t // seq, HEADS),
        [head, head, shared, shared, shared, gain, gain,
         pl.BlockSpec((seq, V_DIM), lambda b, h: (b, 2 + h)), per_head, per_head],
        [head, head, shared, gain, gain],
        [jax.ShapeDtypeStruct((t, HEADS * HP), BF16), jax.ShapeDtypeStruct((t, HEADS * HP), BF16),
         jax.ShapeDtypeStruct((t, 128), F32), jax.ShapeDtypeStruct((1, HP), F32),
         jax.ShapeDtypeStruct((1, HP), F32)],
        (q, kv, kr, cos, sin, gq, gk, dmix, d_out, lse),
        [pltpu.VMEM((seq, HP), F32), pltpu.VMEM((seq, HP), F32), pltpu.VMEM((seq, V_DIM), F32)], comm)


def _tn(a_list, b, tm, name, into=None, comm=None):
    t, n_out = b.shape
    widths = [a.shape[1] for a in a_list]
    tk = min(TK_DW, t)
    m, na, nk = sum(widths), len(a_list), t // tk
    assert na == 1 or tm == m

    def body(*refs):
        a_refs, b_ref, o_ref, acc_ref = refs[:na], refs[na], refs[-2], refs[-1]
        k = pl.program_id(1)

        @pl.when(k == 0)
        def _():
            acc_ref[...] = jnp.zeros_like(acc_ref)

        bb = b_ref[...].astype(BF16)
        m0 = 0
        for a_ref, w in zip(a_refs, widths):
            rows = slice(0, tm) if na == 1 else slice(m0, m0 + w)
            acc_ref[rows, :] += _dot(a_ref[...].astype(BF16), bb, 0, 0)
            m0 += w

        @pl.when(k == nk - 1)
        def _():
            o_ref[...] = acc_ref[...].astype(BF16).reshape(o_ref.shape)

    if na == 1:
        in_specs = [pl.BlockSpec((tk, tm), lambda i, k: (k, i))]
    else:
        in_specs = [pl.BlockSpec((tk, w), lambda i, k: (k, 0)) for w in widths]
    in_specs.append(pl.BlockSpec((tk, n_out), lambda i, k: (k, 0)))
    args = list(a_list) + [b]
    if into is None:
        out_spec = pl.BlockSpec((tm, n_out), lambda i, k: (i, 0))
        out_shape = jax.ShapeDtypeStruct((m, n_out), BF16)
        aliases = {}
    else:
        buf, n, off = into
        assert n_out == D and tm % n == 0 and off % n == 0 and (na == 1 or tm // n == N_DEV)
        idx = off // n
        out_spec = pl.BlockSpec((tm // n, n, D), lambda i, k: (i, idx, 0))
        out_shape = jax.ShapeDtypeStruct(buf.shape, BF16)
        in_specs.append(pl.BlockSpec(memory_space=pl.ANY))
        args.append(buf)
        aliases = {len(args) - 1: 0}
    (res,), extra = _call(body, name, (m // tm, nk), in_specs, [out_spec], [out_shape], args,
                          [pltpu.VMEM((tm, n_out), F32)], comm, aliases)
    return (res, extra) if comm is not None else res


def _adamw(ws, gs, ms, vs, name, nblk=1):
    n = len(ws)
    c1 = 1.0 - B1 ** STEP
    c2 = 1.0 - B2 ** STEP

    def body(*refs):
        for a in range(n):
            w, g, m, v = (refs[k * n + a][...] for k in range(4))
            d_ref, m_ref, v_ref = (refs[(4 + k) * n + a] for k in range(3))
            m_new = B1 * m + (1.0 - B1) * g
            v_new = B2 * v + (1.0 - B2) * (g * g)
            d_ref[...] = -LR * ((m_new / c1) / (jnp.sqrt(v_new / c2) + ADAM_EPS) + WD * w)
            m_ref[...] = m_new
            v_ref[...] = v_new

    grid = (nblk,)
    assert all(w.shape[0] % nblk == 0 and (nblk == 1 or (w.shape[0] // nblk) % 8 == 0) for w in ws)
    specs = [pl.BlockSpec((w.shape[0] // nblk, w.shape[1]), lambda i: (i, 0)) for w in ws]
    outs, _ = _call(body, name, grid, specs * 4, specs * 3, [jax.ShapeDtypeStruct(w.shape, F32) for w in ws] * 3,
                    (*ws, *gs, *ms, *vs))
    return outs[:n], outs[n:2 * n], outs[2 * n:]


def _rows1024(a, rows):
    flat = a.reshape(-1, D)
    return jnp.pad(flat, ((0, rows - flat.shape[0]), (0, 0)))


def _pack_shards(even_w_in, even_w_out, odd_w_in, q_b, kv_b, odd_w_out, ffn_w_gate, ffn_w_up, ffn_w_down):
    mix0 = jnp.concatenate([even_w_in[0].T, jnp.zeros((OFF_EOUT - N_EIN, D), F32), even_w_out[0]], axis=0)
    gu = [jnp.concatenate([ffn_w_gate[layer].T, ffn_w_up[layer].T], axis=0) for layer in range(2)]
    mix1 = jnp.concatenate([jnp.pad(odd_w_in[0], ((0, 0), (0, D - ODD_IN))), odd_w_out[0],
                            _rows1024(q_b[0].T, N_QB), _rows1024(kv_b[0].T, N_KVB),
                            jnp.zeros((R_MIX1 - OFF_KVB - N_KVB, D), F32)], axis=0)
    return [c.astype(BF16) for c in (mix0, gu[0], ffn_w_down[0], mix1, gu[1], ffn_w_down[1])]


def _pad_heads(a):
    k = a.shape[1]
    return jnp.pad(a.reshape(HEADS, QK_DIM, k), ((0, 0), (0, HP - QK_DIM), (0, 0))).reshape(HEADS * HP, k)


def _small_pack(parts):
    flat = []
    for p in parts:
        v = p.reshape(-1)
        flat.append(jnp.pad(v, (0, (-v.shape[0]) % 1024)))
    return jnp.concatenate(flat).reshape(-1, 128)


def _small_unpack(buf, shapes):
    flat = buf.reshape(-1)
    out, off = [], 0
    for s in shapes:
        size = int(np.prod(s))
        out.append(flat[off:off + size].reshape(s))
        off += size + (-size) % 1024
    return out


def _step(x3d, positions, target3d, chunks, tile, where, mix_norm, ffn_norm, sg_ln_g, sg_w_s, sg_b_s,
          pool_w, q_norm, k_norm):
    bsz, seq, _ = x3d.shape
    t = bsz * seq
    x0 = x3d.reshape(t, D)
    target = target3d.reshape(t, D)
    my_mix0, my_gu0, my_d0, my_mix1, my_gu1, my_d1 = chunks

    lane = np.arange(128)
    inv_freq = np.where(lane < QK_ROPE, ROPE_THETA ** (-(2.0 * (lane % 32)) / QK_ROPE), 0.0)
    inv_freq = jnp.asarray(inv_freq.reshape(1, 128), F32)
    (cos, sin), (w_mix0, tiles) = _rope_tables(positions.reshape(t, 1), inv_freq, _gather_comm([my_mix0, tile]))

    conv_w = tiles[:, 0:3, 0:64].transpose(1, 0, 2).reshape(3, SC_W)
    pool_scale = tiles[:, 3, 0:32].reshape(1, POOL_W)
    q_a_norm = tiles[:, 4, 0:48].reshape(1, Q_LORA)
    kv_a_norm = tiles[:, 5, 0:32].reshape(1, KV_LORA)
    ws = sg_w_s[0]
    bst = jnp.pad(sg_b_s[0].T, ((0, 0), (0, 128 - SG_HEADS)))
    cw = jnp.pad(conv_w, ((0, 8 - 3), (0, 0)))
    pw_bd = jax.scipy.linalg.block_diag(*[pool_w[0, g] for g in range(4)]).astype(BF16)
    gq = jnp.pad(q_norm * ATT_SCALE, ((0, 0), (0, HP - QK_DIM)))
    gk = jnp.pad(k_norm, ((0, 0), (0, HP - QK_DIM)))

    (x1, proj_e), (w_gu0,) = _even_fwd(x0, w_mix0, mix_norm[0:1], sg_ln_g, ws, bst, cw, seq, _gather_comm([my_gu0]))
    (g0, u0, act0), (w_d0, w_mix1) = _ffn_up(x1, w_gu0, ffn_norm[0:1], "ffn_up0", _gather_comm([my_d0, my_mix1]))
    (x2,), (w_d1,) = _ffn_down(x1, act0, w_d0, "ffn_down0", _gather_comm([my_d1]))
    qbt = _pad_heads(w_mix1[:, OFF_QB:OFF_QB + N_QB_USED, :].reshape(HEADS * QK_DIM, Q_LORA))
    kvbt = w_mix1[:, OFF_KVB:OFF_KVB + N_KVB, :].reshape(HEADS * HP, KV_LORA)
    (proj_o, q, kv, kr, c_out), _ = _odd_pre_fwd(x2, w_mix1, mix_norm[1:2], qbt, kvbt, q_a_norm, kv_a_norm,
                                                pw_bd, pool_scale, seq)
    (d_out, lse), (w_gu1,) = _attn_fwd(q, kv, kr, cos, sin, gq, gk, seq, _gather_comm([my_gu1]))
    x3, dy, g1, u1, loss_tile = _last_block_fwd(x2, c_out, d_out, w_mix1, w_gu1, w_d1, ffn_norm[1:2], target)

    def chunk(rows, padded=False):
        return jnp.zeros((N_DEV, rows, D), BF16) if padded else lax.empty((N_DEV, rows, D), BF16)

    (dx3, act1, dg1, du1, h3, dgam_f1, dmix_o), _ = _ffn_bwd(x3, g1, u1, dy, w_gu1, w_d1, ffn_norm[1:2], "ffn_bwd1",
                                                           None, w_mix1)
    gp_ffn1 = _tn([dg1], h3, 1408, "dw_gate1", (chunk(R_GU + N_FF), N_FF, OFF_GATE))
    gp_ffn1 = _tn([du1], h3, 1408, "dw_up1", (gp_ffn1, N_FF, OFF_UP))
    gp_ffn1 = _tn([act1], dy, 1408, "dw_down1", (gp_ffn1, N_FF, R_GU))

    gp_mix1, (ga_ffn1,) = _tn([c_out, d_out], dx3, D, "dw_oout", (chunk(R_MIX1, True), N_SQ, OFF_OOUT),
                              _pair_exchange_comm(gp_ffn1))
    pb_ffn1 = _rs_pair_sum(gp_ffn1, ga_ffn1, where, "rs_pair_sum_ffn1")
    (dq, dkv, dkr, dgq, dgk), (gb_ffn1,) = _attn_bwd(q, kv, kr, cos, sin, gq, gk, dmix_o, d_out, lse, seq,
                                                    _chip_exchange_comm(pb_ffn1))
    (dx2, dproj_o, h2, qn, kvn, dgam_m1, dqa, dkva, dpw_bd, dps) = _odd_pre_bwd(
        x2, proj_o, dx3, dmix_o, dq, dkv, dkr, w_mix1, mix_norm[1:2], qbt, kvbt, q_a_norm, kv_a_norm, pw_bd,
        pool_scale, seq)
    gp_mix1 = _tn([h2], dproj_o, D, "dw_oin", (gp_mix1, N_SQ, OFF_OIN))
    d_qbt = _tn([dq], qn, HEADS * HP, "dw_qb")
    d_qb_rows = d_qbt.reshape(HEADS, HP, Q_LORA)[:, :QK_DIM].reshape(N_DEV, N_QB_USED, D)
    d_kvb_rows = _tn([dkv], kvn, HEADS * HP, "dw_kvb").reshape(N_DEV, N_KVB, D)
    gp_mix1 = lax.dynamic_update_slice(gp_mix1, d_qb_rows, (0, OFF_QB, 0))
    gp_mix1 = lax.dynamic_update_slice(gp_mix1, d_kvb_rows, (0, OFF_KVB, 0))

    (dx1, act0, dg0, du0, h1, dgam_f0), (ga_mix1,) = _ffn_bwd(x1, g0, u0, dx2, w_gu0, w_d0, ffn_norm[0:1], "ffn_bwd0",
                                                             _pair_exchange_comm(gp_mix1))
    pb_mix1 = _rs_pair_sum(gp_mix1, ga_mix1, where, "rs_pair_sum_mix1")
    gp_ffn0a, (gb_mix1,) = _tn([dg0], h1, 1408, "dw_gate0", (chunk(R_GU), N_FF, OFF_GATE),
                               _chip_exchange_comm(pb_mix1))
    gp_ffn0a = _tn([du0], h1, 1408, "dw_up0", (gp_ffn0a, N_FF, OFF_UP))
    gp_ffn0b, (ga_ffn0a,) = _tn([act0], dx2, 1408, "dw_down0", (chunk(N_FF), N_FF, 0),
                                _pair_exchange_comm(gp_ffn0a))
    pb_ffn0a = _rs_pair_sum(gp_ffn0a, ga_ffn0a, where, "rs_pair_sum_ffn0a")

    (dx0, dproj_e, mix_e, h0, dgam_m0, dws, dbc, dlng, dcw), (gb_ffn0a, ga_ffn0b) = _even_bwd(
        x0, proj_e, dx1, w_mix0, mix_norm[0:1], sg_ln_g, ws, bst, cw, seq,
        _both(_chip_exchange_comm(pb_ffn0a), _pair_exchange_comm(gp_ffn0b)))
    pb_ffn0b = _rs_pair_sum(gp_ffn0b, ga_ffn0b, where, "rs_pair_sum_ffn0b")

    small = _small_pack([
        jnp.concatenate([dgam_m0, dgam_m1], 0), jnp.concatenate([dgam_f0, dgam_f1], 0), dlng,
        dws[None], dbc[:, :SG_HEADS].T[None], dcw[:3],
        jnp.stack([dpw_bd[g * POOL_GD:(g + 1) * POOL_GD, g * POOL_GD:(g + 1) * POOL_GD] for g in range(4)])[None],
        dps, dqa, dkva, dgq[:, :QK_DIM] * ATT_SCALE, dgk[:, :QK_DIM], loss_tile[0:1, 0:1]])
    gp_mix0, (small_all,) = _tn([mix_e], dx1, D, "dw_eout", (chunk(R_MIX0, True), N_SQ, OFF_EOUT),
                                _gather_comm([small]))
    gp_mix0, (gb_ffn0b,) = _tn([dproj_e], h0, 1280, "dw_ein", (gp_mix0, N_EIN, OFF_EIN),
                               _chip_exchange_comm(pb_ffn0b))
    small_sum = _small_unpack(_sum_gathered(small_all), SMALL_SHAPES)
    partials = ([pb_ffn0a, pb_ffn0b, pb_mix1, pb_ffn1], [gb_ffn0a, gb_ffn0b, gb_mix1, gb_ffn1])
    return dx0.reshape(bsz, seq, D), partials, gp_mix0, small_sum


SMALL_SHAPES = [(2, D), (2, D), (1, SG_W), (1, SG_HEADS, 128, 128), (1, SG_HEADS, 128), (3, SC_W),
                (1, 4, POOL_GD, POOL_GD), (1, POOL_W), (1, Q_LORA), (1, KV_LORA), (1, QK_DIM), (1, QK_DIM), (1, 1)]


def kernel(x, positions, mix_norm, ffn_norm, even_w_in, sg_ln_g, sg_w_s, sg_b_s, sc_conv_w, even_w_out, odd_w_in, pool_w, pool_scale, q_a_norm, q_b, kv_a_norm, kv_b, q_norm, k_norm, odd_w_out, ffn_w_gate, ffn_w_up, ffn_w_down, loss_target, m_mix_norm, m_ffn_norm, m_even_w_in, m_sg_ln_g, m_sg_w_s, m_sg_b_s, m_sc_conv_w, m_even_w_out, m_odd_w_in, m_pool_w, m_pool_scale, m_q_a_norm, m_q_b, m_kv_a_norm, m_kv_b, m_q_norm, m_k_norm, m_odd_w_out, m_ffn_w_gate, m_ffn_w_up, m_ffn_w_down, v_mix_norm, v_ffn_norm, v_even_w_in, v_sg_ln_g, v_sg_w_s, v_sg_b_s, v_sc_conv_w, v_even_w_out, v_odd_w_in, v_pool_w, v_pool_scale, v_q_a_norm, v_q_b, v_kv_a_norm, v_kv_b, v_q_norm, v_k_norm, v_odd_w_out, v_ffn_w_gate, v_ffn_w_up, v_ffn_w_down):
    xi, yi, ci = _place()
    me = 4 * xi + 2 * yi + ci

    chunks = _pack_shards(even_w_in, even_w_out, odd_w_in, q_b, kv_b, odd_w_out, ffn_w_gate, ffn_w_up, ffn_w_down)

    def lane_pad(a):
        return jnp.pad(a, ((0, 0), (0, 128 - a.shape[1])))

    tile = jnp.concatenate([lane_pad(sc_conv_w[0]), lane_pad(pool_scale), lane_pad(q_a_norm), lane_pad(kv_a_norm),
                            jnp.zeros((2, 128), F32)], axis=0)
    chip = 2 * xi + yi
    where = jnp.stack([ci, chip, chip ^ 2, chip ^ 1, chip ^ 3]).astype(jnp.int32)
    grad_x, (pbs, gbs), gp_mix0, tot = _step(
        x, positions, loss_target, chunks, tile, where, mix_norm, ffn_norm, sg_ln_g, sg_w_s, sg_b_s,
        pool_w, q_norm, k_norm)

    (ga_mix0,) = _comm_alone(_pair_exchange_comm(gp_mix0), "rs_pair_exchange_mix0")
    pb_mix0 = _rs_pair_sum(gp_mix0, ga_mix0, where, "rs_pair_sum_mix0")
    mix0_sems, pb_mix0, land_mix0, started = _chip_exchange_start(pb_mix0)
    gsh_ffn0a, gsh_ffn0b, gsh_mix1, gsh_ffn1 = _rs_final_sums(pbs, gbs, "rs_final_sums", started)

    (g_mix, g_ffn, g_lng, g_ws, g_bs, g_cw_full, g_pw, g_ps_full, g_qa_full, g_kva_full, g_qn, g_kn, loss) = tot
    g_cw = lax.dynamic_slice_in_dim(g_cw_full, me * 64, 64, axis=1)[None]
    g_ps = lax.dynamic_slice_in_dim(g_ps_full, me * 32, 32, axis=1)
    g_qa = lax.dynamic_slice_in_dim(g_qa_full, me * 48, 48, axis=1)
    g_kva = lax.dynamic_slice_in_dim(g_kva_full, me * 32, 32, axis=1)

    def tr(a):
        return jnp.swapaxes(a, -1, -2)

    g_gate = tr(jnp.stack([gsh_ffn0a[OFF_GATE:OFF_GATE + N_FF], gsh_ffn1[OFF_GATE:OFF_GATE + N_FF]]))
    g_up = tr(jnp.stack([gsh_ffn0a[OFF_UP:OFF_UP + N_FF], gsh_ffn1[OFF_UP:OFF_UP + N_FF]]))
    g_down = jnp.stack([gsh_ffn0b, gsh_ffn1[R_GU:R_GU + N_FF]])
    g_oin = gsh_mix1[OFF_OIN:OFF_OIN + N_SQ, :ODD_IN][None]
    g_oout = gsh_mix1[OFF_OOUT:OFF_OOUT + N_SQ][None]
    g_qb = tr(gsh_mix1[OFF_QB:OFF_QB + N_QB_USED].reshape(1, 144, Q_LORA))
    g_kvb = tr(gsh_mix1[OFF_KVB:OFF_KVB + N_KVB].reshape(1, 192, KV_LORA))
    transposed = ("even_w_in", "odd_w_in", "q_b", "kv_b", "ffn_w_gate", "ffn_w_up")

    names = ("mix_norm", "ffn_norm", "even_w_in", "sg_ln_g", "sg_w_s", "sg_b_s", "sc_conv_w", "even_w_out",
             "odd_w_in", "pool_w", "pool_scale", "q_a_norm", "q_b", "kv_a_norm", "kv_b", "q_norm", "k_norm",
             "odd_w_out", "ffn_w_gate", "ffn_w_up", "ffn_w_down")
    grads = dict(mix_norm=g_mix, ffn_norm=g_ffn, sg_ln_g=g_lng, sg_w_s=g_ws, sg_b_s=g_bs,
                 sc_conv_w=g_cw, odd_w_in=g_oin, pool_w=g_pw, pool_scale=g_ps, q_a_norm=g_qa,
                 q_b=g_qb, kv_a_norm=g_kva, kv_b=g_kvb, q_norm=g_qn, k_norm=g_kn, odd_w_out=g_oout,
                 ffn_w_gate=g_gate, ffn_w_up=g_up, ffn_w_down=g_down)
    weights = dict(mix_norm=mix_norm, ffn_norm=ffn_norm, even_w_in=even_w_in, sg_ln_g=sg_ln_g, sg_w_s=sg_w_s,
                   sg_b_s=sg_b_s, sc_conv_w=sc_conv_w, even_w_out=even_w_out, odd_w_in=odd_w_in, pool_w=pool_w,
                   pool_scale=pool_scale, q_a_norm=q_a_norm, q_b=q_b, kv_a_norm=kv_a_norm, kv_b=kv_b, q_norm=q_norm,
                   k_norm=k_norm, odd_w_out=odd_w_out, ffn_w_gate=ffn_w_gate, ffn_w_up=ffn_w_up,
                   ffn_w_down=ffn_w_down)
    m_in = dict(mix_norm=m_mix_norm, ffn_norm=m_ffn_norm, even_w_in=m_even_w_in, sg_ln_g=m_sg_ln_g, sg_w_s=m_sg_w_s,
                sg_b_s=m_sg_b_s, sc_conv_w=m_sc_conv_w, even_w_out=m_even_w_out, odd_w_in=m_odd_w_in,
                pool_w=m_pool_w, pool_scale=m_pool_scale, q_a_norm=m_q_a_norm, q_b=m_q_b, kv_a_norm=m_kv_a_norm,
                kv_b=m_kv_b, q_norm=m_q_norm, k_norm=m_k_norm, odd_w_out=m_odd_w_out, ffn_w_gate=m_ffn_w_gate,
                ffn_w_up=m_ffn_w_up, ffn_w_down=m_ffn_w_down)
    v_in = dict(mix_norm=v_mix_norm, ffn_norm=v_ffn_norm, even_w_in=v_even_w_in, sg_ln_g=v_sg_ln_g, sg_w_s=v_sg_w_s,
                sg_b_s=v_sg_b_s, sc_conv_w=v_sc_conv_w, even_w_out=v_even_w_out, odd_w_in=v_odd_w_in,
                pool_w=v_pool_w, pool_scale=v_pool_scale, q_a_norm=v_q_a_norm, q_b=v_q_b, kv_a_norm=v_kv_a_norm,
                kv_b=v_kv_b, q_norm=v_q_norm, k_norm=v_k_norm, odd_w_out=v_odd_w_out, ffn_w_gate=v_ffn_w_gate,
                ffn_w_up=v_ffn_w_up, ffn_w_down=v_ffn_w_down)
    delta, new_m, new_v = {}, {}, {}

    def as2d(k, a):
        a = tr(a) if k in transposed else a
        return a.reshape(-1, a.shape[-1])

    def back(k, a):
        shape = weights[k].shape
        return tr(a.reshape(shape[:-2] + (shape[-1], shape[-2]))) if k in transposed else a.reshape(shape)

    def update(group, name, nblk=1):
        outs = _adamw([as2d(k, weights[k]) for k in group], [as2d(k, grads[k]) for k in group],
                      [as2d(k, m_in[k]) for k in group], [as2d(k, v_in[k]) for k in group], name, nblk)
        for i, k in enumerate(group):
            delta[k], new_m[k], new_v[k] = (back(k, o[i]) for o in outs)

    update(["ffn_w_gate", "ffn_w_up", "ffn_w_down"], "adamw_ffn", 4)
    update(["odd_w_in", "odd_w_out"], "adamw_mix1", 2)
    update([k for k in names if k not in delta and k not in ("even_w_in", "even_w_out")], "adamw_small")

    pb_mix0, gb_mix0 = _chip_exchange_wait(mix0_sems, pb_mix0, land_mix0, new_v["k_norm"])
    (gsh_mix0,) = _rs_final_sums([pb_mix0], [gb_mix0], "rs_final_sum_mix0")
    grads["even_w_in"] = tr(gsh_mix0[OFF_EIN:OFF_EIN + N_EIN][None])
    grads["even_w_out"] = gsh_mix0[OFF_EOUT:OFF_EOUT + N_SQ][None]
    update(["even_w_in", "even_w_out"], "adamw_mix0", 2)

    return (loss.reshape(()), grad_x, *[grads[k] for k in names], *[delta[k] for k in names],
            *[new_m[k] for k in names], *[new_v[k] for k in names])
```

```python
import functools

import numpy as np
import jax
import jax.numpy as jnp
from jax import lax
from jax.experimental import pallas as pl
from jax.experimental.pallas import tpu as pltpu

F32 = jnp.float32
BF16 = jnp.bfloat16
MESH = pl.DeviceIdType.MESH

D = 1024
EPS = 1e-6
NEG_INF = -1e30
SG_HEADS, SG_HD, SG_W, SG_CHUNK = 4, 128, 512, 128
SC_W = 512
EVEN_IN = 2560
POOL_W = 256
POOL_GD = 64
Q_LORA, KV_LORA, QK_ROPE, QK_NOPE, V_DIM = 384, 256, 64, 128, 128
QK_DIM = QK_NOPE + QK_ROPE
HEADS = 6
HP = 256
ODD_IN = 960
D_FF = 2816
ROPE_THETA = 10000.0
ATT_SCALE = QK_DIM ** -0.5
LR, B1, B2, ADAM_EPS, WD, STEP = 0.001, 0.9, 0.999, 1e-08, 0.01, 10

N_DEV = 8
TB = 512
TB_FFN_BWD = 256
TK_DW = 1024
HALO = 16
VMEM_LIMIT = 56 * 1024 * 1024

N_EIN, N_FF, N_SQ = 320, 352, 128
OFF_EIN, OFF_EOUT, R_MIX0 = 0, 384, 512
OFF_GATE, OFF_UP, R_GU = 0, 352, 704
OFF_OIN, OFF_OOUT, OFF_QB, OFF_KVB, R_MIX1 = 0, 128, 256, 320, 384
N_QB, N_QB_USED, N_KVB = 64, 54, 48

INV_SQRT2 = 0.7071067811865476
INV_SQRT_2PI = 0.3989422804014327


def _dot(a, b, ca, cb):
    return lax.dot_general(a, b, (((ca,), (cb,)), ((), ())), preferred_element_type=F32)


def _cparams(n_axes=1):
    return pltpu.CompilerParams(dimension_semantics=("arbitrary",) * n_axes, vmem_limit_bytes=VMEM_LIMIT)


def _wspec(n, off):
    assert off % n == 0
    idx = off // n
    return pl.BlockSpec((N_DEV, n, D), lambda i: (0, idx, 0), pipeline_mode=pl.Buffered(1))


def _const_spec(shape):
    zeros = (0,) * len(shape)
    return pl.BlockSpec(shape, lambda *_: zeros)


class _Comm:
    def __init__(self, ins, out_shapes, sems, start, wait, mid=None):
        self.ins, self.out_shapes, self.sems, self.start, self.wait, self.mid = ins, out_shapes, sems, start, wait, mid


def _call(body, name, grid, in_specs, out_specs, out_shape, args, scratch_shapes=(), comm=None, aliases=None):
    n_axes = len(grid)
    aliases = aliases or {}
    if comm is None:
        res = pl.pallas_call(
            body, name=name, grid=grid, in_specs=list(in_specs), out_specs=list(out_specs),
            out_shape=list(out_shape), scratch_shapes=list(scratch_shapes), input_output_aliases=aliases,
            compiler_params=_cparams(n_axes))(*args)
        return list(res), []
    ni, no, ns = len(in_specs), len(out_specs), len(scratch_shapes)
    ci, co = len(comm.ins), len(comm.out_shapes)
    n_steps = int(np.prod(grid))

    def carrier(*refs):
        ins, cin = refs[:ni], refs[ni:ni + ci]
        outs, cout = refs[ni + ci:ni + ci + no], refs[ni + ci + no:ni + ci + no + co]
        scr, sems = refs[ni + ci + no + co:ni + ci + no + co + ns], refs[ni + ci + no + co + ns:]
        step = 0
        for a in range(n_axes):
            step = step * grid[a] + pl.program_id(a)

        @pl.when(step == 0)
        def _():
            comm.start(cin, cout, sems)

        body(*ins, *outs, *scr)

        if comm.mid is not None and n_steps >= 4:
            @pl.when(step == n_steps // 2)
            def _():
                comm.mid(cin, cout, sems)

        @pl.when(step == n_steps - 1)
        def _():
            if comm.mid is not None and n_steps < 4:
                comm.mid(cin, cout, sems)
            comm.wait(cin, cout, sems)

    any_spec = pl.BlockSpec(memory_space=pl.ANY)
    res = pl.pallas_call(
        carrier, name=name, grid=grid, in_specs=list(in_specs) + [any_spec] * ci,
        out_specs=list(out_specs) + [any_spec] * co, out_shape=list(out_shape) + list(comm.out_shapes),
        scratch_shapes=list(scratch_shapes) + list(comm.sems), input_output_aliases=aliases,
        compiler_params=_cparams(n_axes))(*args, *comm.ins)
    return list(res[:no]), list(res[no:])


def _comm_alone(comm, name):
    ci, co = len(comm.ins), len(comm.out_shapes)

    def body(*refs):
        cin, cout, sems = refs[:ci], refs[ci:ci + co], refs[ci + co:]
        comm.start(cin, cout, sems)
        if comm.mid is not None:
            comm.mid(cin, cout, sems)
        comm.wait(cin, cout, sems)

    any_spec = pl.BlockSpec(memory_space=pl.ANY)
    res = pl.pallas_call(
        body, name=name, out_shape=list(comm.out_shapes), in_specs=[any_spec] * ci, out_specs=[any_spec] * co,
        scratch_shapes=list(comm.sems))(*comm.ins)
    return list(res)


def _rms(x, g):
    r = lax.rsqrt(jnp.mean(x * x, axis=-1, keepdims=True) + EPS)
    return x * r * g, r


def _rms_bwd(x, r, g, dy):
    xh = x * r
    dxh = dy * g
    dx = r * (dxh - xh * jnp.mean(dxh * xh, axis=-1, keepdims=True))
    dg = jnp.sum(dy * xh, axis=0, keepdims=True)
    return dx, dg


def _gelu(x):
    return 0.5 * x * (1.0 + lax.erf(x * INV_SQRT2))


def _gelu_grad(x):
    return 0.5 * (1.0 + lax.erf(x * INV_SQRT2)) + x * jnp.exp(-0.5 * x * x) * INV_SQRT_2PI


def _shift_down(a, k):
    rows = lax.broadcasted_iota(jnp.int32, a.shape, 0)
    return jnp.where(rows >= k, pltpu.roll(a, k, 0), 0.0)


def _shift_up(a, k):
    n = a.shape[0]
    rows = lax.broadcasted_iota(jnp.int32, a.shape, 0)
    return jnp.where(rows < n - k, pltpu.roll(a, n - k, 0), 0.0)


def _tril_bf16(w):
    r = lax.broadcasted_iota(jnp.int32, w.shape, 0)
    c = lax.broadcasted_iota(jnp.int32, w.shape, 1)
    return jnp.where(r >= c, w, 0.0).astype(BF16)


def _ln_head(vh, g):
    mu = jnp.mean(vh, axis=-1, keepdims=True)
    xc = vh - mu
    rr = lax.rsqrt(jnp.mean(xc * xc, axis=-1, keepdims=True) + EPS)
    xh = xc * rr
    return xh * g, xh, rr


def _conv_fwd(z, tail, cw_ref):
    ext = jnp.concatenate([tail, z], axis=0)
    zs1 = _shift_down(ext, 1)[HALO:]
    zs2 = _shift_down(ext, 2)[HALO:]
    y = cw_ref[2:3, :] * z + cw_ref[1:2, :] * zs1 + cw_ref[0:1, :] * zs2
    return y, zs1, zs2


def _pool_cnt(shape, blk_in_seq):
    rows = lax.broadcasted_iota(jnp.int32, shape, 0)
    grp = lax.broadcasted_iota(jnp.int32, shape, 1) // POOL_GD
    win = jnp.where(grp == 0, 2, jnp.where(grp == 1, 4, jnp.where(grp == 2, 8, 16)))
    tpos = blk_in_seq * shape[0] + rows + 1
    return jnp.minimum(tpos, win).astype(F32), grp


def _pool_select(grp, s2, s4, s8, s16):
    return jnp.where(grp == 0, s2, jnp.where(grp == 1, s4, jnp.where(grp == 2, s8, s16)))


def _pool_fwd(z, tail, blk_in_seq):
    ext = jnp.concatenate([tail, z], axis=0)
    s2 = ext + _shift_down(ext, 1)
    s4 = s2 + _shift_down(s2, 2)
    s8 = s4 + _shift_down(s4, 4)
    s16 = s8 + _shift_down(s8, 8)
    cnt, grp = _pool_cnt(z.shape, blk_in_seq)
    sums = _pool_select(grp, s2[HALO:], s4[HALO:], s8[HALO:], s16[HALO:])
    return sums / cnt - z, cnt, grp


def _pool_bwd(dpooled, dpm, head, grp):
    n = dpm.shape[0]
    ext = jnp.concatenate([dpm, head], axis=0)
    u2 = ext + _shift_up(ext, 1)
    u4 = u2 + _shift_up(u2, 2)
    u8 = u4 + _shift_up(u4, 4)
    u16 = u8 + _shift_up(u8, 8)
    return _pool_select(grp, u2[:n], u4[:n], u8[:n], u16[:n]) - dpooled


def _lane_sums(a):
    return _dot(a.astype(BF16), jnp.ones((a.shape[1], a.shape[1]), BF16), 1, 0)


def _swap_halves(y1):
    src = lax.broadcasted_iota(jnp.int32, (128, 128), 0)
    dst = lax.broadcasted_iota(jnp.int32, (128, 128), 1)
    perm = jnp.where(((dst < 32) & (src == dst + 32)) | ((dst >= 32) & (dst < QK_ROPE) & (src == dst - 32)), 1.0, 0.0)
    return _dot(y1.astype(BF16), perm.astype(BF16), 1, 0)


def _rope(y1, c, s):
    return y1 * c + _swap_halves(y1) * s


def _rope_bwd(d1, c, s):
    return d1 * c + _swap_halves(d1 * s)


def _qk_prep(x, g, c, s):
    r = lax.rsqrt(_lane_sums(x * x) * (1.0 / QK_DIM) + EPS)
    y = x * r * g
    return jnp.concatenate([y[:, :128], _rope(y[:, 128:], c, s)], axis=1), r


def _qk_prep_bwd(dout, x, r, g, c, s):
    dy = jnp.concatenate([dout[:, :128], _rope_bwd(dout[:, 128:], c, s)], axis=1)
    xh = x * r
    dxh = dy * g
    dx = r * (dxh - xh * (_lane_sums(dxh * xh) * (1.0 / QK_DIM)))
    return dx, jnp.sum(dy * xh, axis=0, keepdims=True)


def _place():
    return lax.axis_index("x"), lax.axis_index("y"), lax.axis_index("c")


def _gather_comm(arrs):
    n = len(arrs)

    def halves(a):
        rows = arrs[a].shape[0]
        tile = 16 if arrs[a].dtype == BF16 else 8
        top = rows // 2 if rows % (2 * tile) == 0 else rows
        return (0, top), (top, rows - top)

    def plan(ins, outs, sems):
        send_sems, recv_sems, local_sems = sems
        x, y, c = _place()
        me, sib, xn, yn, dg = (x, y, c), (x, y, 1 - c), (1 - x, y, c), (x, 1 - y, c), (1 - x, 1 - y, c)

        def slot(a, dev, part=None):
            ref = outs[a].at[4 * dev[0] + 2 * dev[1] + dev[2]]
            return ref if part is None else ref.at[pl.ds(part[0], part[1])]

        def copy(a, k, block, to, src=None, part=None):
            return pltpu.make_async_remote_copy(
                src_ref=slot(a, block, part) if src is None else src, dst_ref=slot(a, block, part),
                send_sem=send_sems.at[a, k], recv_sem=recv_sems.at[a, k], device_id=to, device_id_type=MESH)

        local = [pltpu.make_async_copy(ins[a], slot(a, me), local_sems.at[a]) for a in range(n)]
        return me, sib, xn, yn, dg, copy, local

    def start(ins, outs, sems):
        me, sib, xn, yn, _, copy, local = plan(ins, outs, sems)
        for a in range(n):
            local[a].start()
            for k, to in enumerate((sib, xn, yn)):
                copy(a, k, me, to, src=ins[a]).start()

    def mid(ins, outs, sems):
        me, sib, xn, yn, _, copy, _ = plan(ins, outs, sems)
        for a in range(n):
            top, bottom = halves(a)
            copy(a, 1, xn, me).wait_recv()
            copy(a, 3, xn, yn, part=top).start()
            copy(a, 5, xn, sib).start()
            copy(a, 2, yn, me).wait_recv()
            if bottom[1]:
                copy(a, 4, yn, xn, part=bottom).start()
            copy(a, 6, yn, sib).start()

    def wait(ins, outs, sems):
        me, sib, xn, yn, dg, copy, local = plan(ins, outs, sems)
        other = lambda dev: (dev[0], dev[1], 1 - dev[2])
        for a in range(n):
            top, bottom = halves(a)
            copy(a, 3, dg, me, part=top).wait_recv()
            if bottom[1]:
                copy(a, 4, dg, me, part=bottom).wait_recv()
            copy(a, 7, dg, sib).start()
        for a in range(n):
            top, bottom = halves(a)
            for k, block in ((0, sib), (5, other(xn)), (6, other(yn)), (7, other(dg))):
                copy(a, k, block, me).wait_recv()
            for k, block in ((0, me), (1, me), (2, me), (5, xn), (6, yn), (7, dg)):
                copy(a, k, block, me, src=ins[a] if k < 3 else None).wait_send()
            copy(a, 3, xn, me, part=top).wait_send()
            if bottom[1]:
                copy(a, 4, yn, me, part=bottom).wait_send()
            local[a].wait()

    return _Comm(
        list(arrs), [jax.ShapeDtypeStruct((N_DEV,) + a.shape, a.dtype) for a in arrs],
        [pltpu.SemaphoreType.DMA((n, 8)), pltpu.SemaphoreType.DMA((n, 8)), pltpu.SemaphoreType.DMA((n,))],
        start, wait, mid)


def _sum_gathered(g):
    rows = g.shape[1]

    def body(g_ref, sum_ref):
        total = g_ref[0]
        for d in range(1, N_DEV):
            total = total + g_ref[d]
        sum_ref[...] = total

    return pl.pallas_call(
        body, name="sum_gathered_small", out_shape=jax.ShapeDtypeStruct((rows, 128), F32), grid=(1,),
        in_specs=[pl.BlockSpec((N_DEV, rows, 128), lambda i: (0, 0, 0))],
        out_specs=pl.BlockSpec((rows, 128), lambda i: (0, 0)), compiler_params=_cparams(1),
    )(g)


def _sum_rows(rows):
    return rows if rows <= 512 else rows // 2


def _pair_exchange_comm(gp):
    _, rows, cols = gp.shape

    def copies(ins, outs, sems):
        send_sems, recv_sems = sems
        x, y, c = _place()
        return [pltpu.make_async_remote_copy(
            src_ref=ins[0].at[2 * j + (1 - c)], dst_ref=outs[0].at[j], send_sem=send_sems.at[j],
            recv_sem=recv_sems.at[j], device_id=(x, y, 1 - c), device_id_type=MESH) for j in range(4)]

    def start(ins, outs, sems):
        for cp in copies(ins, outs, sems):
            cp.start()

    def wait(ins, outs, sems):
        for cp in copies(ins, outs, sems):
            cp.wait()

    return _Comm([gp], [jax.ShapeDtypeStruct((4, rows, cols), gp.dtype)],
                 [pltpu.SemaphoreType.DMA((4,)), pltpu.SemaphoreType.DMA((4,))], start, wait)


def _rs_pair_sum(gp, got, where, name):
    _, rows, cols = got.shape
    rb = _sum_rows(rows)
    gp4 = gp.reshape(4, 2, rows, cols)

    def body(w_ref, a_ref, b_ref, o_ref):
        o_ref[0] = (a_ref[0, 0].astype(F32) + b_ref[0].astype(F32)).astype(o_ref.dtype)

    return pl.pallas_call(
        body, name=name, out_shape=jax.ShapeDtypeStruct((4, rows, cols), gp.dtype),
        grid_spec=pltpu.PrefetchScalarGridSpec(
            num_scalar_prefetch=1, grid=(4, rows // rb),
            in_specs=[pl.BlockSpec((1, 1, rb, cols), lambda k, r, w: (w[1 + k], w[0], r, 0)),
                      pl.BlockSpec((1, rb, cols), lambda k, r, w: (w[1 + k], r, 0))],
            out_specs=pl.BlockSpec((1, rb, cols), lambda k, r, w: (k, r, 0))),
        compiler_params=_cparams(2),
    )(where, gp4, got)


def _chip_exchange_comm(pb):
    _, rows, cols = pb.shape

    def copies(ins, outs, sems):
        send_sems, recv_sems = sems
        x, y, c = _place()
        chips = [(1 - x, y), (x, 1 - y), (1 - x, 1 - y)]
        return [pltpu.make_async_remote_copy(
            src_ref=ins[0].at[1 + k], dst_ref=outs[0].at[k], send_sem=send_sems.at[k],
            recv_sem=recv_sems.at[k], device_id=(px, py, c), device_id_type=MESH)
            for k, (px, py) in enumerate(chips)]

    def start(ins, outs, sems):
        for cp in copies(ins, outs, sems):
            cp.start()

    def wait(ins, outs, sems):
        for cp in copies(ins, outs, sems):
            cp.wait()

    return _Comm([pb], [jax.ShapeDtypeStruct((3, rows, cols), pb.dtype)],
                 [pltpu.SemaphoreType.DMA((3,)), pltpu.SemaphoreType.DMA((3,))], start, wait)


def _chip_exchange_start(pb, tag):
    _, rows, cols = pb.shape

    def body(pb_ref, land_ref, *rest):
        sems, token = rest[:6], rest[8]
        x, y, c = _place()
        chips = [(1 - x, y), (x, 1 - y), (1 - x, 1 - y)]
        for k, (px, py) in enumerate(chips):
            pltpu.make_async_remote_copy(
                src_ref=pb_ref.at[1 + k], dst_ref=land_ref.at[k], send_sem=sems[k], recv_sem=sems[3 + k],
                device_id=(px, py, c), device_id_type=MESH).start()
        token[...] = jnp.zeros_like(token)

    hbm = pl.BlockSpec(memory_space=pltpu.HBM)
    sem = pl.BlockSpec(memory_space=pltpu.SEMAPHORE)
    land = lax.empty((3, rows, cols), pb.dtype)
    res = pl.pallas_call(
        body, name="rs_chip_exchange_start_" + tag,
        out_shape=(*[pltpu.SemaphoreType.DMA(())] * 6, pltpu.HBM(pb.shape, pb.dtype), pltpu.HBM(land.shape, land.dtype),
                   jax.ShapeDtypeStruct((8, 128), F32)),
        in_specs=(hbm, hbm), out_specs=(*[sem] * 6, hbm, hbm, pl.BlockSpec(memory_space=pltpu.VMEM)),
        input_output_aliases={0: 6, 1: 7},
        compiler_params=pltpu.CompilerParams(has_side_effects=pltpu.SideEffectType.DATAFLOW_SIDE_EFFECTING),
    )(pltpu.with_memory_space_constraint(pb, pltpu.HBM), pltpu.with_memory_space_constraint(land, pltpu.HBM))
    return list(res[:6]), res[6], res[7], res[8]


def _chip_exchange_wait(sems, pb_thru, land_thru, after, tag):
    def body(pb_ref, land_ref, *rest):
        sems_in = rest[:6]
        x, y, c = _place()
        chips = [(1 - x, y), (x, 1 - y), (1 - x, 1 - y)]
        for k, (px, py) in enumerate(chips):
            cp = pltpu.make_async_remote_copy(
                src_ref=pb_ref.at[1 + k], dst_ref=land_ref.at[k], send_sem=sems_in[k], recv_sem=sems_in[3 + k],
                device_id=(px, py, c), device_id_type=MESH)
            cp.wait_send()
            cp.wait_recv()

    hbm = pl.BlockSpec(memory_space=pltpu.HBM)
    sem = pl.BlockSpec(memory_space=pltpu.SEMAPHORE)
    res = pl.pallas_call(
        body, name="rs_chip_exchange_wait_" + tag,
        out_shape=(pltpu.HBM(pb_thru.shape, pb_thru.dtype), pltpu.HBM(land_thru.shape, land_thru.dtype)),
        in_specs=(hbm, hbm, *[sem] * 6, pl.BlockSpec(memory_space=pl.ANY)), out_specs=(hbm, hbm),
        input_output_aliases={0: 0, 1: 1},
        compiler_params=pltpu.CompilerParams(has_side_effects=pltpu.SideEffectType.DATAFLOW_SIDE_EFFECTING),
    )(pb_thru, land_thru, *sems, after)
    return res[0], res[1]


def _rs_final_sums(pbs, gots, name, after=None):
    n = len(pbs)

    def body(*refs):
        outs = refs[len(refs) - n:]
        for a in range(n):
            m_ref, g_ref, o_ref = refs[a], refs[n + a], outs[a]
            o_ref[...] = ((m_ref[0].astype(F32) + g_ref[0].astype(F32)) + g_ref[1].astype(F32)) + g_ref[2].astype(F32)

    half = [pb.shape[1] // 2 for pb in pbs]
    in_specs = ([pl.BlockSpec((1, h, D), lambda i: (0, i, 0)) for h in half]
                + [pl.BlockSpec((3, h, D), lambda i: (0, i, 0)) for h in half])
    args = (*pbs, *gots)
    if after is not None:
        in_specs, args = in_specs + [pl.BlockSpec(memory_space=pl.ANY)], args + (after,)
    res, _ = _call(body, name, (2,), in_specs, [pl.BlockSpec((h, D), lambda i: (i, 0)) for h in half],
                   [jax.ShapeDtypeStruct(pb.shape[1:], F32) for pb in pbs], args)
    return res


def _rope_tables(pos_col, inv_freq, comm=None):
    t = pos_col.shape[0]

    def body(p_ref, f_ref, c_ref, s_ref):
        ang = p_ref[...].astype(F32) * f_ref[...]
        lane = lax.broadcasted_iota(jnp.int32, ang.shape, 1)
        c_ref[...] = jnp.where(lane < QK_ROPE, jnp.cos(ang), 0.0)
        s = jnp.sin(ang)
        s_ref[...] = jnp.where(lane < 32, -s, jnp.where(lane < QK_ROPE, s, 0.0))

    spec = pl.BlockSpec((TB, 128), lambda i: (i, 0))
    return _call(
        body, "rope_tables", (t // TB,), [pl.BlockSpec((TB, 1), lambda i: (i, 0)), _const_spec((1, 128))],
        [spec] * 2, [jax.ShapeDtypeStruct((t, 128), F32)] * 2, (pos_col, inv_freq), (), comm)


def _sgu_conv_fwd(proj, tail, lng_ref, ws_ref, bst_ref, cw_ref):
    gu = _gelu(proj[:, 0:SG_W])
    gv = _gelu(proj[:, SG_W:2 * SG_W])
    bg = proj[:, 1024:1536]
    z = proj[:, 1536:2048] * proj[:, 2048:2560]
    heads = []
    for h in range(SG_HEADS):
        sl = slice(h * SG_HD, (h + 1) * SG_HD)
        vn, _, _ = _ln_head(gv[:, sl], lng_ref[:, sl])
        vnb = vn.astype(BF16)
        wm = _tril_bf16(ws_ref[h])
        bcol = bst_ref[:, h:h + 1]
        mixed = jnp.concatenate(
            [_dot(wm, vnb[k * SG_CHUNK:(k + 1) * SG_CHUNK], 1, 0) + bcol for k in range(TB // SG_CHUNK)], axis=0)
        heads.append(gu[:, sl] * mixed)
    a_out = jnp.concatenate(heads, axis=1)
    y, _, _ = _conv_fwd(z, tail, cw_ref)
    return a_out, bg * y, z


def _even_fwd(x, wg, gamma, lng, ws, bst, cw, seq, comm=None):
    t = x.shape[0]
    nbs = seq // TB

    def body(x_ref, gam_ref, win_ref, wout_ref, lng_ref, ws_ref, bst_ref, cw_ref, x1_ref, proj_ref, tail_ref):
        i = pl.program_id(0)
        xv = x_ref[...]
        h, _ = _rms(xv, gam_ref[...])
        proj = _dot(h.astype(BF16), win_ref[...].reshape(EVEN_IN, D), 1, 1)
        proj_ref[...] = proj.astype(BF16)
        tail = jnp.where(i % nbs == 0, 0.0, tail_ref[...])
        a_out, b_out, z = _sgu_conv_fwd(proj, tail, lng_ref, ws_ref, bst_ref, cw_ref)
        tail_ref[...] = z[TB - HALO:, :]
        x1_ref[...] = (xv + _dot(a_out.astype(BF16), wout_ref[0:4].reshape(512, D), 1, 0)
                       + _dot(b_out.astype(BF16), wout_ref[4:8].reshape(512, D), 1, 0))

    row = pl.BlockSpec((TB, D), lambda i: (i, 0))
    return _call(
        body, "even_fwd", (t // TB,),
        [row, _const_spec((1, D)), _wspec(N_EIN, OFF_EIN), _wspec(N_SQ, OFF_EOUT), _const_spec((1, SG_W)),
         _const_spec((SG_HEADS, 128, 128)), _const_spec((128, 128)), _const_spec((8, SC_W))],
        [row, pl.BlockSpec((TB, EVEN_IN), lambda i: (i, 0))],
        [jax.ShapeDtypeStruct((t, D), F32), jax.ShapeDtypeStruct((t, EVEN_IN), BF16)],
        (x, gamma, wg, wg, lng, ws, bst, cw), [pltpu.VMEM((HALO, SC_W), F32)], comm)


def _even_bwd(x, proj, dx1, wg, gamma, lng, ws, bst, cw, seq, comm=None):
    t = x.shape[0]
    nb, nbs = t // TB, seq // TB

    def body(x_ref, proj_ref, ptail_ref, dx1_ref, gam_ref, win_ref, wout_ref, lng_ref, ws_ref, bst_ref, cw_ref,
             dx0_ref, dproj_ref, mix_ref, h_ref, dgam_ref, dws_ref, dbc_ref, dlng_ref, dcw_ref, head_ref):
        i = pl.program_id(0)
        blk = nb - 1 - i

        @pl.when(i == 0)
        def _():
            dgam_ref[...] = jnp.zeros_like(dgam_ref)
            dws_ref[...] = jnp.zeros_like(dws_ref)
            dbc_ref[...] = jnp.zeros_like(dbc_ref)
            dlng_ref[...] = jnp.zeros_like(dlng_ref)
            dcw_ref[...] = jnp.zeros_like(dcw_ref)

        xv = x_ref[...]
        gam = gam_ref[...]
        h, r = _rms(xv, gam)
        h_ref[...] = h.astype(BF16)
        dx1 = dx1_ref[...]
        dmix = _dot(dx1.astype(BF16), wout_ref[...].reshape(D, D), 1, 1)
        da, db = dmix[:, :SG_W], dmix[:, SG_W:]
        proj = proj_ref[...].astype(F32)
        u, v = proj[:, 0:SG_W], proj[:, SG_W:2 * SG_W]
        bg, cg, hv = proj[:, 1024:1536], proj[:, 1536:2048], proj[:, 2048:2560]
        gu, gv = _gelu(u), _gelu(v)

        a_heads, dgv_heads = [], []
        for hd in range(SG_HEADS):
            sl = slice(hd * SG_HD, (hd + 1) * SG_HD)
            g_h = lng_ref[:, sl]
            vn, xh, rr = _ln_head(gv[:, sl], g_h)
            vnb = vn.astype(BF16)
            wm = _tril_bf16(ws_ref[hd])
            bcol = bst_ref[:, hd:hd + 1]
            mixed_c, dvn_c = [], []
            dw_acc = jnp.zeros((128, 128), F32)
            db_acc = jnp.zeros((128, 1), F32)
            for k in range(TB // SG_CHUNK):
                rs = slice(k * SG_CHUNK, (k + 1) * SG_CHUNK)
                mixed = _dot(wm, vnb[rs], 1, 0) + bcol
                dmixed = da[rs, sl] * gu[rs, sl]
                dmb = dmixed.astype(BF16)
                dvn_c.append(_dot(wm, dmb, 0, 0))
                dw_acc = dw_acc + _dot(dmb, vnb[rs], 1, 1)
                db_acc = db_acc + jnp.sum(dmixed, axis=1, keepdims=True)
                mixed_c.append(mixed)
            mixed_h = jnp.concatenate(mixed_c, axis=0)
            dvn = jnp.concatenate(dvn_c, axis=0)
            r_i = lax.broadcasted_iota(jnp.int32, (128, 128), 0)
            c_i = lax.broadcasted_iota(jnp.int32, (128, 128), 1)
            dws_ref[hd] += jnp.where(r_i >= c_i, dw_acc, 0.0)
            dbc_ref[:, hd:hd + 1] += db_acc
            dlng_ref[:, sl] += jnp.sum(dvn * xh, axis=0, keepdims=True)
            dxh = dvn * g_h
            dgv = rr * (dxh - jnp.mean(dxh, axis=-1, keepdims=True)
                        - xh * jnp.mean(dxh * xh, axis=-1, keepdims=True))
            a_heads.append(gu[:, sl] * mixed_h)
            dproj_ref[:, sl] = (da[:, sl] * mixed_h * _gelu_grad(u[:, sl])).astype(BF16)
            dgv_heads.append(dgv * _gelu_grad(v[:, sl]))
        dproj_ref[:, SG_W:2 * SG_W] = jnp.concatenate(dgv_heads, axis=1).astype(BF16)
        mix_ref[:, :SG_W] = jnp.concatenate(a_heads, axis=1).astype(BF16)

        z = cg * hv
        pt = ptail_ref[...].astype(F32)
        tail = jnp.where(blk % nbs == 0, 0.0, pt[:, 1536:2048] * pt[:, 2048:2560])
        y, zs1, zs2 = _conv_fwd(z, tail, cw_ref)
        mix_ref[:, SG_W:] = (bg * y).astype(BF16)
        dy = db * bg
        head = jnp.where(blk % nbs == nbs - 1, 0.0, head_ref[...])
        ext = jnp.concatenate([dy, head], axis=0)
        dz = (cw_ref[2:3, :] * dy + cw_ref[1:2, :] * _shift_up(ext, 1)[:TB]
              + cw_ref[0:1, :] * _shift_up(ext, 2)[:TB])
        head_ref[...] = dy[:HALO, :]
        dcw_ref[2:3, :] += jnp.sum(dy * z, axis=0, keepdims=True)
        dcw_ref[1:2, :] += jnp.sum(dy * zs1, axis=0, keepdims=True)
        dcw_ref[0:1, :] += jnp.sum(dy * zs2, axis=0, keepdims=True)
        dproj_ref[:, 1024:1536] = (db * y).astype(BF16)
        dproj_ref[:, 1536:2048] = (dz * hv).astype(BF16)
        dproj_ref[:, 2048:2560] = (dz * cg).astype(BF16)

        dh = _dot(dproj_ref[...], win_ref[...].reshape(EVEN_IN, D), 1, 0)
        dxn, dgam = _rms_bwd(xv, r, gam, dh)
        dgam_ref[...] += dgam
        dx0_ref[...] = dx1 + dxn

    def rev(w):
        return pl.BlockSpec((TB, w), lambda i: (nb - 1 - i, 0))

    ptail = pl.BlockSpec((HALO, EVEN_IN), lambda i: (jnp.maximum((nb - 1 - i) * (TB // HALO) - 1, 0), 0))
    return _call(
        body, "even_bwd", (nb,),
        [rev(D), rev(EVEN_IN), ptail, rev(D), _const_spec((1, D)), _wspec(N_EIN, OFF_EIN),
         _wspec(N_SQ, OFF_EOUT), _const_spec((1, SG_W)), _const_spec((SG_HEADS, 128, 128)),
         _const_spec((128, 128)), _const_spec((8, SC_W))],
        [rev(D), rev(EVEN_IN), rev(D), rev(D), _const_spec((1, D)), _const_spec((SG_HEADS, 128, 128)),
         _const_spec((128, 128)), _const_spec((1, SG_W)), _const_spec((8, SC_W))],
        [jax.ShapeDtypeStruct((t, D), F32), jax.ShapeDtypeStruct((t, EVEN_IN), BF16),
         jax.ShapeDtypeStruct((t, D), BF16), jax.ShapeDtypeStruct((t, D), BF16),
         jax.ShapeDtypeStruct((1, D), F32), jax.ShapeDtypeStruct((SG_HEADS, 128, 128), F32),
         jax.ShapeDtypeStruct((128, 128), F32), jax.ShapeDtypeStruct((1, SG_W), F32),
         jax.ShapeDtypeStruct((8, SC_W), F32)],
        (x, proj, proj, dx1, gamma, wg, wg, lng, ws, bst, cw), [pltpu.VMEM((HALO, SC_W), F32)], comm)


def _last_block_fwd(x, c_out, d_out, w_mix1, w_gu, w_d, gamma, target):
    t = x.shape[0]

    def body(x_ref, c_ref, d_ref, wo_ref, gam_ref, wg_ref, wu_ref, wd_ref, t_ref,
             x3_ref, dy_ref, g_ref, u_ref, loss_ref):
        @pl.when(pl.program_id(0) == 0)
        def _():
            loss_ref[...] = jnp.zeros_like(loss_ref)

        xv = (x_ref[...] + _dot(c_ref[...], wo_ref[0:2].reshape(POOL_W, D), 1, 0)
              + _dot(d_ref[...], wo_ref[2:8].reshape(HEADS * V_DIM, D), 1, 0))
        x3_ref[...] = xv
        h, _ = _rms(xv, gam_ref[...])
        hb = h.astype(BF16)
        g = _dot(hb, wg_ref[...].reshape(D_FF, D), 1, 1)
        u = _dot(hb, wu_ref[...].reshape(D_FF, D), 1, 1)
        g_ref[...] = g.astype(BF16)
        u_ref[...] = u.astype(BF16)
        act = g * jax.nn.sigmoid(g) * u
        err = xv + _dot(act.astype(BF16), wd_ref[...].reshape(D_FF, D), 1, 0) - t_ref[...]
        dy_ref[...] = err * (1.0 / D)
        sq = jnp.sum(jnp.sum(err * err, axis=-1, keepdims=True), axis=0, keepdims=True)
        loss_ref[...] += (0.5 / D) * sq

    def row(w):
        return pl.BlockSpec((TB, w), lambda i: (i, 0))

    res, _ = _call(
        body, "last_block_fwd", (t // TB,),
        [row(D), row(POOL_W), row(HEADS * V_DIM), _wspec(N_SQ, OFF_OOUT), _const_spec((1, D)),
         _wspec(N_FF, OFF_GATE), _wspec(N_FF, OFF_UP), _wspec(N_FF, 0), row(D)],
        [row(D), row(D), row(D_FF), row(D_FF), _const_spec((8, 128))],
        [jax.ShapeDtypeStruct((t, D), F32), jax.ShapeDtypeStruct((t, D), F32), jax.ShapeDtypeStruct((t, D_FF), BF16),
         jax.ShapeDtypeStruct((t, D_FF), BF16), jax.ShapeDtypeStruct((8, 128), F32)],
        (x, c_out, d_out, w_mix1, gamma, w_gu, w_gu, w_d, target))
    return res


def _ffn_up(x, w_gu, gamma, name, comm=None):
    t = x.shape[0]

    def body(x_ref, gam_ref, wg_ref, wu_ref, g_ref, u_ref, act_ref):
        h, _ = _rms(x_ref[...], gam_ref[...])
        hb = h.astype(BF16)
        g = _dot(hb, wg_ref[...].reshape(D_FF, D), 1, 1)
        u = _dot(hb, wu_ref[...].reshape(D_FF, D), 1, 1)
        g_ref[...] = g.astype(BF16)
        u_ref[...] = u.astype(BF16)
        act_ref[...] = (g * jax.nn.sigmoid(g) * u).astype(BF16)

    row = pl.BlockSpec((TB, D), lambda i: (i, 0))
    wide = pl.BlockSpec((TB, D_FF), lambda i: (i, 0))
    return _call(body, name, (t // TB,), [row, _const_spec((1, D)), _wspec(N_FF, OFF_GATE), _wspec(N_FF, OFF_UP)],
                 [wide, wide, wide], [jax.ShapeDtypeStruct((t, D_FF), BF16)] * 3, (x, gamma, w_gu, w_gu), (), comm)


def _ffn_down(x, act, w_d, name, comm=None):
    t = x.shape[0]

    def body(x_ref, a_ref, wd_ref, y_ref):
        y_ref[...] = x_ref[...] + _dot(a_ref[...], wd_ref[...].reshape(D_FF, D), 1, 0)

    row = pl.BlockSpec((TB, D), lambda i: (i, 0))
    wide = pl.BlockSpec((TB, D_FF), lambda i: (i, 0))
    return _call(body, name, (t // TB,), [row, wide, _wspec(N_FF, 0)], [row], [jax.ShapeDtypeStruct((t, D), F32)],
                 (x, act, w_d), (), comm)


def _ffn_bwd(x, g, u, dy, w_gu, w_d, gamma, name, comm=None, w_mix1=None):
    t = x.shape[0]
    with_dmix = w_mix1 is not None

    def body(*refs):
        x_ref, g_ref, u_ref, dy_ref, gam_ref, wg_ref, wu_ref, wd_ref = refs[:8]
        dx_ref, act_ref, dg_ref, du_ref, h_ref, dgam_ref = refs[8 + with_dmix:14 + with_dmix]

        @pl.when(pl.program_id(0) == 0)
        def _():
            dgam_ref[...] = jnp.zeros_like(dgam_ref)

        xv = x_ref[...]
        gam = gam_ref[...]
        h, r = _rms(xv, gam)
        h_ref[...] = h.astype(BF16)
        dyv = dy_ref[...]
        dact = _dot(dyv.astype(BF16), wd_ref[...].reshape(D_FF, D), 1, 1)
        gv = g_ref[...].astype(F32)
        uv = u_ref[...].astype(F32)
        sg = jax.nn.sigmoid(gv)
        silu = gv * sg
        act_ref[...] = (silu * uv).astype(BF16)
        dgb = (dact * uv * (sg * (1.0 + gv * (1.0 - sg)))).astype(BF16)
        dub = (dact * silu).astype(BF16)
        dg_ref[...] = dgb
        du_ref[...] = dub
        dh = _dot(dgb, wg_ref[...].reshape(D_FF, D), 1, 0) + _dot(dub, wu_ref[...].reshape(D_FF, D), 1, 0)
        dxn, dgam = _rms_bwd(xv, r, gam, dh)
        dgam_ref[...] += dgam
        dx = dyv + dxn
        dx_ref[...] = dx
        if with_dmix:
            refs[15][...] = _dot(dx.astype(BF16), refs[8][...].reshape(D, D), 1, 1).astype(BF16)

    row = pl.BlockSpec((TB_FFN_BWD, D), lambda i: (i, 0))
    wide = pl.BlockSpec((TB_FFN_BWD, D_FF), lambda i: (i, 0))
    in_specs = [row, wide, wide, row, _const_spec((1, D)), _wspec(N_FF, OFF_GATE), _wspec(N_FF, OFF_UP),
                _wspec(N_FF, 0)]
    out_specs = [row, wide, wide, wide, row, _const_spec((1, D))]
    out_shape = [jax.ShapeDtypeStruct((t, D), F32), jax.ShapeDtypeStruct((t, D_FF), BF16),
                 jax.ShapeDtypeStruct((t, D_FF), BF16), jax.ShapeDtypeStruct((t, D_FF), BF16),
                 jax.ShapeDtypeStruct((t, D), BF16), jax.ShapeDtypeStruct((1, D), F32)]
    args = (x, g, u, dy, gamma, w_gu, w_gu, w_d)
    if with_dmix:
        in_specs, args = in_specs + [_wspec(N_SQ, OFF_OOUT)], args + (w_mix1,)
        out_specs, out_shape = out_specs + [row], out_shape + [jax.ShapeDtypeStruct((t, D), BF16)]
    return _call(body, name, (t // TB_FFN_BWD,), in_specs, out_specs, out_shape, args, (), comm)


def _odd_pre_fwd(x, wg, gamma, qbt, kvbt, qa_g, kva_g, pw_bd, pscale, seq, comm=None):
    t = x.shape[0]
    nbs = seq // TB

    def body(x_ref, gam_ref, win_ref, qb_ref, kvb_ref, qa_ref, kva_ref, pw_ref, ps_ref,
             proj_ref, q_ref, kv_ref, kr_ref, c_ref, tail_ref):
        i = pl.program_id(0)
        h, _ = _rms(x_ref[...], gam_ref[...])
        proj = _dot(h.astype(BF16), win_ref[...].reshape(D, D), 1, 0)
        proj_ref[...] = proj.astype(BF16)
        zp, ql, kvl = proj[:, :POOL_W], proj[:, 256:640], proj[:, 640:896]
        kr_ref[...] = proj[:, 896:1024]
        qn, _ = _rms(ql, qa_ref[...])
        q_ref[...] = _dot(qn.astype(BF16), qb_ref[...], 1, 1).astype(BF16)
        kvn, _ = _rms(kvl, kva_ref[...])
        kv_ref[...] = _dot(kvn.astype(BF16), kvb_ref[...], 1, 1).astype(BF16)
        tail = jnp.where(i % nbs == 0, 0.0, tail_ref[...])
        pooled, _, _ = _pool_fwd(zp, tail, i % nbs)
        tail_ref[...] = zp[TB - HALO:, :]
        c_ref[...] = (_dot(pooled.astype(BF16), pw_ref[...], 1, 0) * ps_ref[...]).astype(BF16)

    def row(w):
        return pl.BlockSpec((TB, w), lambda i: (i, 0))

    return _call(
        body, "odd_pre_fwd", (t // TB,),
        [row(D), _const_spec((1, D)), _wspec(N_SQ, OFF_OIN), _const_spec((HEADS * HP, Q_LORA)),
         _const_spec((HEADS * HP, KV_LORA)), _const_spec((1, Q_LORA)), _const_spec((1, KV_LORA)),
         _const_spec((POOL_W, POOL_W)), _const_spec((1, POOL_W))],
        [row(D), row(HEADS * HP), row(HEADS * HP), row(128), row(POOL_W)],
        [jax.ShapeDtypeStruct((t, D), BF16), jax.ShapeDtypeStruct((t, HEADS * HP), BF16),
         jax.ShapeDtypeStruct((t, HEADS * HP), BF16), jax.ShapeDtypeStruct((t, 128), F32),
         jax.ShapeDtypeStruct((t, POOL_W), BF16)],
        (x, gamma, wg, qbt, kvbt, qa_g, kva_g, pw_bd, pscale), [pltpu.VMEM((HALO, POOL_W), F32)], comm)


def _odd_pre_bwd(x, proj, dx3, dmix, dq, dkv, dkr, wg, gamma, qbt, kvbt, qa_g, kva_g, pw_bd, pscale, seq):
    t = x.shape[0]
    nb, nbs = t // TB, seq // TB

    def body(x_ref, proj_ref, ptail_ref, dx3_ref, dco_ref, dq_ref, dkv_ref, dkr_ref, gam_ref, win_ref, qb_ref,
             kvb_ref, qa_ref, kva_ref, pw_ref, ps_ref,
             dx2_ref, dproj_ref, h_ref, qn_ref, kvn_ref, dgam_ref, dqa_ref, dkva_ref, dpw_ref, dps_ref, head_ref):
        i = pl.program_id(0)
        blk = nb - 1 - i

        @pl.when(i == 0)
        def _():
            dgam_ref[...] = jnp.zeros_like(dgam_ref)
            dqa_ref[...] = jnp.zeros_like(dqa_ref)
            dkva_ref[...] = jnp.zeros_like(dkva_ref)
            dpw_ref[...] = jnp.zeros_like(dpw_ref)
            dps_ref[...] = jnp.zeros_like(dps_ref)

        xv = x_ref[...]
        gam = gam_ref[...]
        h, r = _rms(xv, gam)
        h_ref[...] = h.astype(BF16)
        proj = proj_ref[...].astype(F32)
        zp, ql, kvl = proj[:, :POOL_W], proj[:, 256:640], proj[:, 640:896]

        qa = qa_ref[...]
        qn, rq = _rms(ql, qa)
        qn_ref[...] = qn.astype(BF16)
        dql, dqa = _rms_bwd(ql, rq, qa, _dot(dq_ref[...], qb_ref[...], 1, 0))
        dqa_ref[...] += dqa
        kva = kva_ref[...]
        kvn, rkv = _rms(kvl, kva)
        kvn_ref[...] = kvn.astype(BF16)
        dkvl, dkva = _rms_bwd(kvl, rkv, kva, _dot(dkv_ref[...], kvb_ref[...], 1, 0))
        dkva_ref[...] += dkva

        pt = ptail_ref[...].astype(F32)
        tail = jnp.where(blk % nbs == 0, 0.0, pt[:, :POOL_W])
        pooled, cnt, grp = _pool_fwd(zp, tail, blk % nbs)
        pb = pooled.astype(BF16)
        pw = pw_ref[...]
        dco = dco_ref[...].astype(F32)
        dps_ref[...] += jnp.sum(dco * _dot(pb, pw, 1, 0), axis=0, keepdims=True)
        dpo = (dco * ps_ref[...]).astype(BF16)
        dpw_ref[...] += _dot(pb, dpo, 0, 0)
        dpooled = _dot(dpo, pw, 1, 1)
        dpm = dpooled / cnt
        head = jnp.where(blk % nbs == nbs - 1, 0.0, head_ref[...])
        dz = _pool_bwd(dpooled, dpm, head, grp)
        head_ref[...] = dpm[:HALO, :]

        dproj_ref[:, :POOL_W] = dz.astype(BF16)
        dproj_ref[:, 256:640] = dql.astype(BF16)
        dproj_ref[:, 640:896] = dkvl.astype(BF16)
        dproj_ref[:, 896:1024] = dkr_ref[...].astype(BF16)
        dh = _dot(dproj_ref[...], win_ref[...].reshape(D, D), 1, 1)
        dxn, dgam = _rms_bwd(xv, r, gam, dh)
        dgam_ref[...] += dgam
        dx2_ref[...] = dx3_ref[...] + dxn

    def rev(w):
        return pl.BlockSpec((TB, w), lambda i: (nb - 1 - i, 0))

    ptail = pl.BlockSpec((HALO, D), lambda i: (jnp.maximum((nb - 1 - i) * (TB // HALO) - 1, 0), 0))
    return pl.pallas_call(
        body, name="odd_pre_bwd",
        out_shape=[jax.ShapeDtypeStruct((t, D), F32), jax.ShapeDtypeStruct((t, D), BF16),
                   jax.ShapeDtypeStruct((t, D), BF16), jax.ShapeDtypeStruct((t, Q_LORA), BF16),
                   jax.ShapeDtypeStruct((t, KV_LORA), BF16), jax.ShapeDtypeStruct((1, D), F32),
                   jax.ShapeDtypeStruct((1, Q_LORA), F32), jax.ShapeDtypeStruct((1, KV_LORA), F32),
                   jax.ShapeDtypeStruct((POOL_W, POOL_W), F32), jax.ShapeDtypeStruct((1, POOL_W), F32)],
        grid=(nb,),
        in_specs=[rev(D), rev(D), ptail, rev(D), rev(POOL_W), rev(HEADS * HP), rev(HEADS * HP), rev(128),
                  _const_spec((1, D)), _wspec(N_SQ, OFF_OIN), _const_spec((HEADS * HP, Q_LORA)),
                  _const_spec((HEADS * HP, KV_LORA)), _const_spec((1, Q_LORA)), _const_spec((1, KV_LORA)),
                  _const_spec((POOL_W, POOL_W)), _const_spec((1, POOL_W))],
        out_specs=[rev(D), rev(D), rev(D), rev(Q_LORA), rev(KV_LORA), _const_spec((1, D)), _const_spec((1, Q_LORA)),
                   _const_spec((1, KV_LORA)), _const_spec((POOL_W, POOL_W)), _const_spec((1, POOL_W))],
        scratch_shapes=[pltpu.VMEM((HALO, POOL_W), F32)],
        compiler_params=_cparams(1),
    )(x, proj, proj, dx3, dmix, dq, dkv, dkr, gamma, wg, qbt, kvbt, qa_g, kva_g, pw_bd, pscale)


def _attn_specs(seq):
    head = pl.BlockSpec((seq, HP), lambda b, h: (b, h))
    shared = pl.BlockSpec((seq, 128), lambda b, h: (b, 0))
    gain = pl.BlockSpec((1, HP), lambda b, h: (0, 0))
    return head, shared, gain


def _causal_bias(n):
    rows = lax.broadcasted_iota(jnp.int32, (n, n), 0)
    cols = lax.broadcasted_iota(jnp.int32, (n, n), 1)
    return jnp.where(cols <= rows, 0.0, NEG_INF)


def _attn_fwd(q, kv, kr, cos, sin, gq, gk, seq, comm=None):
    t = q.shape[0]
    qb = min(512, seq)

    def body(q_ref, kv_ref, kr_ref, c_ref, s_ref, gq_ref, gk_ref, o_ref, lse_ref):
        c, s = c_ref[...], s_ref[...]
        qf, _ = _qk_prep(q_ref[...].astype(F32), gq_ref[...], c, s)
        kin = jnp.concatenate([kv_ref[:, :128].astype(F32), kr_ref[...]], axis=1)
        kf, _ = _qk_prep(kin, gk_ref[...], c, s)
        qf, kf = qf.astype(BF16), kf.astype(BF16)
        v1 = jnp.concatenate([kv_ref[:, 128:], jnp.ones((seq, V_DIM), BF16)], axis=1)
        bias = _causal_bias(qb)
        for q0 in range(0, seq, qb):
            q1 = q0 + qb
            qblk = qf[q0:q1]
            s_dg = _dot(qblk, kf[q0:q1], 1, 1) + bias
            m = jnp.max(s_dg, axis=-1, keepdims=True)
            if q0:
                s_off = _dot(qblk, kf[:q0], 1, 1)
                m = jnp.maximum(m, jnp.max(s_off, axis=-1, keepdims=True))
            acc = _dot(jnp.exp(s_dg - m).astype(BF16), v1[q0:q1], 1, 0)
            if q0:
                acc = acc + _dot(jnp.exp(s_off - m).astype(BF16), v1[:q0], 1, 0)
            l = acc[:, V_DIM:]
            o_ref[q0:q1, :] = (acc[:, :V_DIM] / l).astype(BF16)
            lse_ref[q0:q1, :] = m + jnp.log(l)

    head, shared, gain = _attn_specs(seq)
    per_head = pl.BlockSpec((seq, V_DIM), lambda b, h: (b, h))
    return _call(
        body, "attn_fwd", (t // seq, HEADS),
        [head, head, shared, shared, shared, gain, gain], [per_head, per_head],
        [jax.ShapeDtypeStruct((t, HEADS * V_DIM), BF16), jax.ShapeDtypeStruct((t, HEADS * V_DIM), F32)],
        (q, kv, kr, cos, sin, gq, gk), (), comm)


def _attn_bwd(q, kv, kr, cos, sin, gq, gk, dmix, d_out, lse, seq, comm=None):
    t = q.shape[0]
    qb = min(512, seq)

    def body(q_ref, kv_ref, kr_ref, c_ref, s_ref, gq_ref, gk_ref, do_ref, o_ref, lse_ref,
             dq_ref, dkv_ref, dkr_ref, dgq_ref, dgk_ref, dqf_ref, dkf_ref, dv_ref):
        b, hd = pl.program_id(0), pl.program_id(1)

        @pl.when((b == 0) & (hd == 0))
        def _():
            dgq_ref[...] = jnp.zeros_like(dgq_ref)
            dgk_ref[...] = jnp.zeros_like(dgk_ref)

        c, sn = c_ref[...], s_ref[...]
        gq_v, gk_v = gq_ref[...], gk_ref[...]
        qin = q_ref[...].astype(F32)
        kin = jnp.concatenate([kv_ref[:, :128].astype(F32), kr_ref[...]], axis=1)
        qf32, rq = _qk_prep(qin, gq_v, c, sn)
        kf32, rk = _qk_prep(kin, gk_v, c, sn)
        qf, kf = qf32.astype(BF16), kf32.astype(BF16)
        vb = kv_ref[:, 128:]
        dkf_ref[...] = jnp.zeros_like(dkf_ref)
        dv_ref[...] = jnp.zeros_like(dv_ref)
        bias = _causal_bias(qb)
        for q0 in range(0, seq, qb):
            q1 = q0 + qb
            qblk = qf[q0:q1]
            do = do_ref[q0:q1, :]
            lse_col = lse_ref[q0:q1, 0:1]
            d_col = jnp.sum(do.astype(F32) * o_ref[q0:q1, :].astype(F32), axis=-1, keepdims=True)
            dq_acc = None
            for k0, k1, diag in ((q0, q1, True), (0, q0, False)):
                if k1 == k0:
                    continue
                s = _dot(qblk, kf[k0:k1], 1, 1)
                p = jnp.exp((s + bias if diag else s) - lse_col)
                dv_ref[k0:k1, :] += _dot(p.astype(BF16), do, 0, 0)
                ds = (p * (_dot(do, vb[k0:k1], 1, 1) - d_col)).astype(BF16)
                part = _dot(ds, kf[k0:k1], 1, 0)
                dq_acc = part if dq_acc is None else dq_acc + part
                dkf_ref[k0:k1, :] += _dot(ds, qblk, 0, 0)
            dqf_ref[q0:q1, :] = dq_acc
        dqin, dgq = _qk_prep_bwd(dqf_ref[...], qin, rq, gq_v, c, sn)
        dkin, dgk = _qk_prep_bwd(dkf_ref[...], kin, rk, gk_v, c, sn)
        dgq_ref[...] += dgq
        dgk_ref[...] += dgk
        dq_ref[...] = dqin.astype(BF16)
        dkv_ref[:, :128] = dkin[:, :128].astype(BF16)
        dkv_ref[:, 128:] = dv_ref[...].astype(BF16)

        @pl.when(hd == 0)
        def _():
            dkr_ref[...] = dkin[:, 128:]

        @pl.when(hd != 0)
        def _():
            dkr_ref[...] += dkin[:, 128:]

    head, shared, gain = _attn_specs(seq)
    per_head = pl.BlockSpec((seq, V_DIM), lambda b, h: (b, h))
    return _call(
        body, "attn_bwd", (t // seq, HEADS),
        [head, head, shared, shared, shared, gain, gain,
         pl.BlockSpec((seq, V_DIM), lambda b, h: (b, 2 + h)), per_head, per_head],
        [head, head, shared, gain, gain],
        [jax.ShapeDtypeStruct((t, HEADS * HP), BF16), jax.ShapeDtypeStruct((t, HEADS * HP), BF16),
         jax.ShapeDtypeStruct((t, 128), F32), jax.ShapeDtypeStruct((1, HP), F32),
         jax.ShapeDtypeStruct((1, HP), F32)],
        (q, kv, kr, cos, sin, gq, gk, dmix, d_out, lse),
        [pltpu.VMEM((seq, HP), F32), pltpu.VMEM((seq, HP), F32), pltpu.VMEM((seq, V_DIM), F32)], comm)


def _tn(a_list, b, tm, name, into=None, comm=None, after=None):
    t, n_out = b.shape
    widths = [a.shape[1] for a in a_list]
    tk = min(TK_DW, t)
    m, na, nk = sum(widths), len(a_list), t // tk
    assert na == 1 or tm == m

    def body(*refs):
        a_refs, b_ref, o_ref, acc_ref = refs[:na], refs[na], refs[-2], refs[-1]
        k = pl.program_id(1)

        @pl.when(k == 0)
        def _():
            acc_ref[...] = jnp.zeros_like(acc_ref)

        bb = b_ref[...].astype(BF16)
        m0 = 0
        for a_ref, w in zip(a_refs, widths):
            rows = slice(0, tm) if na == 1 else slice(m0, m0 + w)
            acc_ref[rows, :] += _dot(a_ref[...].astype(BF16), bb, 0, 0)
            m0 += w

        @pl.when(k == nk - 1)
        def _():
            o_ref[...] = acc_ref[...].astype(BF16).reshape(o_ref.shape)

    if na == 1:
        in_specs = [pl.BlockSpec((tk, tm), lambda i, k: (k, i))]
    else:
        in_specs = [pl.BlockSpec((tk, w), lambda i, k: (k, 0)) for w in widths]
    in_specs.append(pl.BlockSpec((tk, n_out), lambda i, k: (k, 0)))
    args = list(a_list) + [b]
    if into is None:
        out_spec = pl.BlockSpec((tm, n_out), lambda i, k: (i, 0))
        out_shape = jax.ShapeDtypeStruct((m, n_out), BF16)
        aliases = {}
    else:
        buf, n, off = into
        assert n_out == D and tm % n == 0 and off % n == 0 and (na == 1 or tm // n == N_DEV)
        idx = off // n
        out_spec = pl.BlockSpec((tm // n, n, D), lambda i, k: (i, idx, 0))
        out_shape = jax.ShapeDtypeStruct(buf.shape, BF16)
        in_specs.append(pl.BlockSpec(memory_space=pl.ANY))
        args.append(buf)
        aliases = {len(args) - 1: 0}
    if after is not None:
        in_specs.append(pl.BlockSpec(memory_space=pl.ANY))
        args.append(after)
    (res,), extra = _call(body, name, (m // tm, nk), in_specs, [out_spec], [out_shape], args,
                          [pltpu.VMEM((tm, n_out), F32)], comm, aliases)
    return (res, extra) if comm is not None else res


def _adamw(ws, gs, ms, vs, name, nblk=1):
    n = len(ws)
    c1 = 1.0 - B1 ** STEP
    c2 = 1.0 - B2 ** STEP

    def body(*refs):
        for a in range(n):
            w, g, m, v = (refs[k * n + a][...] for k in range(4))
            d_ref, m_ref, v_ref = (refs[(4 + k) * n + a] for k in range(3))
            m_new = B1 * m + (1.0 - B1) * g
            v_new = B2 * v + (1.0 - B2) * (g * g)
            d_ref[...] = -LR * ((m_new / c1) / (jnp.sqrt(v_new / c2) + ADAM_EPS) + WD * w)
            m_ref[...] = m_new
            v_ref[...] = v_new

    grid = (nblk,)
    assert all(w.shape[0] % nblk == 0 and (nblk == 1 or (w.shape[0] // nblk) % 8 == 0) for w in ws)
    specs = [pl.BlockSpec((w.shape[0] // nblk, w.shape[1]), lambda i: (i, 0)) for w in ws]
    outs, _ = _call(body, name, grid, specs * 4, specs * 3, [jax.ShapeDtypeStruct(w.shape, F32) for w in ws] * 3,
                    (*ws, *gs, *ms, *vs))
    return outs[:n], outs[n:2 * n], outs[2 * n:]


def _rows1024(a, rows):
    flat = a.reshape(-1, D)
    return jnp.pad(flat, ((0, rows - flat.shape[0]), (0, 0)))


def _pack_shards(even_w_in, even_w_out, odd_w_in, q_b, kv_b, odd_w_out, ffn_w_gate, ffn_w_up, ffn_w_down):
    mix0 = jnp.concatenate([even_w_in[0].T, jnp.zeros((OFF_EOUT - N_EIN, D), F32), even_w_out[0]], axis=0)
    gu = [jnp.concatenate([ffn_w_gate[layer].T, ffn_w_up[layer].T], axis=0) for layer in range(2)]
    mix1 = jnp.concatenate([jnp.pad(odd_w_in[0], ((0, 0), (0, D - ODD_IN))), odd_w_out[0],
                            _rows1024(q_b[0].T, N_QB), _rows1024(kv_b[0].T, N_KVB),
                            jnp.zeros((R_MIX1 - OFF_KVB - N_KVB, D), F32)], axis=0)
    return [c.astype(BF16) for c in (mix0, gu[0], ffn_w_down[0], mix1, gu[1], ffn_w_down[1])]


def _pad_heads(a):
    k = a.shape[1]
    return jnp.pad(a.reshape(HEADS, QK_DIM, k), ((0, 0), (0, HP - QK_DIM), (0, 0))).reshape(HEADS * HP, k)


def _small_pack(parts):
    flat = []
    for p in parts:
        v = p.reshape(-1)
        flat.append(jnp.pad(v, (0, (-v.shape[0]) % 1024)))
    return jnp.concatenate(flat).reshape(-1, 128)


def _small_unpack(buf, shapes):
    flat = buf.reshape(-1)
    out, off = [], 0
    for s in shapes:
        size = int(np.prod(s))
        out.append(flat[off:off + size].reshape(s))
        off += size + (-size) % 1024
    return out


def _step(x3d, positions, target3d, chunks, tile, where, mix_norm, ffn_norm, sg_ln_g, sg_w_s, sg_b_s,
          pool_w, q_norm, k_norm):
    bsz, seq, _ = x3d.shape
    t = bsz * seq
    x0 = x3d.reshape(t, D)
    target = target3d.reshape(t, D)
    my_mix0, my_gu0, my_d0, my_mix1, my_gu1, my_d1 = chunks

    lane = np.arange(128)
    inv_freq = np.where(lane < QK_ROPE, ROPE_THETA ** (-(2.0 * (lane % 32)) / QK_ROPE), 0.0)
    inv_freq = jnp.asarray(inv_freq.reshape(1, 128), F32)
    (cos, sin), (w_mix0, tiles) = _rope_tables(positions.reshape(t, 1), inv_freq, _gather_comm([my_mix0, tile]))

    conv_w = tiles[:, 0:3, 0:64].transpose(1, 0, 2).reshape(3, SC_W)
    pool_scale = tiles[:, 3, 0:32].reshape(1, POOL_W)
    q_a_norm = tiles[:, 4, 0:48].reshape(1, Q_LORA)
    kv_a_norm = tiles[:, 5, 0:32].reshape(1, KV_LORA)
    ws = sg_w_s[0]
    bst = jnp.pad(sg_b_s[0].T, ((0, 0), (0, 128 - SG_HEADS)))
    cw = jnp.pad(conv_w, ((0, 8 - 3), (0, 0)))
    pw_bd = jax.scipy.linalg.block_diag(*[pool_w[0, g] for g in range(4)]).astype(BF16)
    gq = jnp.pad(q_norm * ATT_SCALE, ((0, 0), (0, HP - QK_DIM)))
    gk = jnp.pad(k_norm, ((0, 0), (0, HP - QK_DIM)))

    (x1, proj_e), (w_gu0,) = _even_fwd(x0, w_mix0, mix_norm[0:1], sg_ln_g, ws, bst, cw, seq, _gather_comm([my_gu0]))
    (g0, u0, act0), (w_d0, w_mix1) = _ffn_up(x1, w_gu0, ffn_norm[0:1], "ffn_up0", _gather_comm([my_d0, my_mix1]))
    (x2,), (w_d1,) = _ffn_down(x1, act0, w_d0, "ffn_down0", _gather_comm([my_d1]))
    qbt = _pad_heads(w_mix1[:, OFF_QB:OFF_QB + N_QB_USED, :].reshape(HEADS * QK_DIM, Q_LORA))
    kvbt = w_mix1[:, OFF_KVB:OFF_KVB + N_KVB, :].reshape(HEADS * HP, KV_LORA)
    (proj_o, q, kv, kr, c_out), _ = _odd_pre_fwd(x2, w_mix1, mix_norm[1:2], qbt, kvbt, q_a_norm, kv_a_norm,
                                                pw_bd, pool_scale, seq)
    (d_out, lse), (w_gu1,) = _attn_fwd(q, kv, kr, cos, sin, gq, gk, seq, _gather_comm([my_gu1]))
    x3, dy, g1, u1, loss_tile = _last_block_fwd(x2, c_out, d_out, w_mix1, w_gu1, w_d1, ffn_norm[1:2], target)

    def chunk(rows, padded=False):
        return jnp.zeros((N_DEV, rows, D), BF16) if padded else lax.empty((N_DEV, rows, D), BF16)

    (dx3, act1, dg1, du1, h3, dgam_f1, dmix_o), _ = _ffn_bwd(x3, g1, u1, dy, w_gu1, w_d1, ffn_norm[1:2], "ffn_bwd1",
                                                           None, w_mix1)
    gp_ffn1 = _tn([dg1], h3, 1408, "dw_gate1", (chunk(R_GU + N_FF), N_FF, OFF_GATE))
    gp_ffn1 = _tn([du1], h3, 1408, "dw_up1", (gp_ffn1, N_FF, OFF_UP))
    gp_ffn1 = _tn([act1], dy, 1408, "dw_down1", (gp_ffn1, N_FF, R_GU))

    gp_mix1, (ga_ffn1,) = _tn([c_out, d_out], dx3, D, "dw_oout", (chunk(R_MIX1, True), N_SQ, OFF_OOUT),
                              _pair_exchange_comm(gp_ffn1))
    pb_ffn1 = _rs_pair_sum(gp_ffn1, ga_ffn1, where, "rs_pair_sum_ffn1")
    (dq, dkv, dkr, dgq, dgk), (gb_ffn1,) = _attn_bwd(q, kv, kr, cos, sin, gq, gk, dmix_o, d_out, lse, seq,
                                                    _chip_exchange_comm(pb_ffn1))
    (dx2, dproj_o, h2, qn, kvn, dgam_m1, dqa, dkva, dpw_bd, dps) = _odd_pre_bwd(
        x2, proj_o, dx3, dmix_o, dq, dkv, dkr, w_mix1, mix_norm[1:2], qbt, kvbt, q_a_norm, kv_a_norm, pw_bd,
        pool_scale, seq)
    gp_mix1 = _tn([h2], dproj_o, D, "dw_oin", (gp_mix1, N_SQ, OFF_OIN))
    d_qbt = _tn([dq], qn, HEADS * HP, "dw_qb")
    d_qb_rows = d_qbt.reshape(HEADS, HP, Q_LORA)[:, :QK_DIM].reshape(N_DEV, N_QB_USED, D)
    d_kvb_rows = _tn([dkv], kvn, HEADS * HP, "dw_kvb").reshape(N_DEV, N_KVB, D)
    gp_mix1 = lax.dynamic_update_slice(gp_mix1, d_qb_rows, (0, OFF_QB, 0))
    gp_mix1 = lax.dynamic_update_slice(gp_mix1, d_kvb_rows, (0, OFF_KVB, 0))

    (dx1, act0, dg0, du0, h1, dgam_f0), (ga_mix1,) = _ffn_bwd(x1, g0, u0, dx2, w_gu0, w_d0, ffn_norm[0:1], "ffn_bwd0",
                                                             _pair_exchange_comm(gp_mix1))
    pb_mix1 = _rs_pair_sum(gp_mix1, ga_mix1, where, "rs_pair_sum_mix1")
    *open_mix1, started = _chip_exchange_start(pb_mix1, "mix1")
    gp_ffn0a = _tn([dg0], h1, 1408, "dw_gate0", (chunk(R_GU), N_FF, OFF_GATE), None, started)
    gp_ffn0a = _tn([du0], h1, 1408, "dw_up0", (gp_ffn0a, N_FF, OFF_UP))
    gp_ffn0b, (ga_ffn0a,) = _tn([act0], dx2, 1408, "dw_down0", (chunk(N_FF), N_FF, 0),
                                _pair_exchange_comm(gp_ffn0a))
    pb_ffn0a = _rs_pair_sum(gp_ffn0a, ga_ffn0a, where, "rs_pair_sum_ffn0a")
    *open_ffn0a, started = _chip_exchange_start(pb_ffn0a, "ffn0a")

    (dx0, dproj_e, mix_e, h0, dgam_m0, dws, dbc, dlng, dcw), (ga_ffn0b,) = _even_bwd(
        x0, proj_e, dx1, w_mix0, mix_norm[0:1], sg_ln_g, ws, bst, cw + 0.0 * started[:, :1], seq,
        _pair_exchange_comm(gp_ffn0b))
    pb_ffn0b = _rs_pair_sum(gp_ffn0b, ga_ffn0b, where, "rs_pair_sum_ffn0b")
    *open_ffn0b, started = _chip_exchange_start(pb_ffn0b, "ffn0b")

    small = _small_pack([
        jnp.concatenate([dgam_m0, dgam_m1], 0), jnp.concatenate([dgam_f0, dgam_f1], 0), dlng,
        dws[None], dbc[:, :SG_HEADS].T[None], dcw[:3],
        jnp.stack([dpw_bd[g * POOL_GD:(g + 1) * POOL_GD, g * POOL_GD:(g + 1) * POOL_GD] for g in range(4)])[None],
        dps, dqa, dkva, dgq[:, :QK_DIM] * ATT_SCALE, dgk[:, :QK_DIM], loss_tile[0:1, 0:1]])
    gp_mix0, (small_all,) = _tn([mix_e], dx1, D, "dw_eout", (chunk(R_MIX0, True), N_SQ, OFF_EOUT),
                                _gather_comm([small]), started)
    gp_mix0 = _tn([dproj_e], h0, 1280, "dw_ein", (gp_mix0, N_EIN, OFF_EIN))
    small_sum = _small_unpack(_sum_gathered(small_all), SMALL_SHAPES)
    return dx0.reshape(bsz, seq, D), (pb_ffn1, gb_ffn1), (open_ffn0a, open_ffn0b, open_mix1), gp_mix0, small_sum


SMALL_SHAPES = [(2, D), (2, D), (1, SG_W), (1, SG_HEADS, 128, 128), (1, SG_HEADS, 128), (3, SC_W),
                (1, 4, POOL_GD, POOL_GD), (1, POOL_W), (1, Q_LORA), (1, KV_LORA), (1, QK_DIM), (1, QK_DIM), (1, 1)]


def kernel(x, positions, mix_norm, ffn_norm, even_w_in, sg_ln_g, sg_w_s, sg_b_s, sc_conv_w, even_w_out, odd_w_in, pool_w, pool_scale, q_a_norm, q_b, kv_a_norm, kv_b, q_norm, k_norm, odd_w_out, ffn_w_gate, ffn_w_up, ffn_w_down, loss_target, m_mix_norm, m_ffn_norm, m_even_w_in, m_sg_ln_g, m_sg_w_s, m_sg_b_s, m_sc_conv_w, m_even_w_out, m_odd_w_in, m_pool_w, m_pool_scale, m_q_a_norm, m_q_b, m_kv_a_norm, m_kv_b, m_q_norm, m_k_norm, m_odd_w_out, m_ffn_w_gate, m_ffn_w_up, m_ffn_w_down, v_mix_norm, v_ffn_norm, v_even_w_in, v_sg_ln_g, v_sg_w_s, v_sg_b_s, v_sc_conv_w, v_even_w_out, v_odd_w_in, v_pool_w, v_pool_scale, v_q_a_norm, v_q_b, v_kv_a_norm, v_kv_b, v_q_norm, v_k_norm, v_odd_w_out, v_ffn_w_gate, v_ffn_w_up, v_ffn_w_down):
    xi, yi, ci = _place()
    me = 4 * xi + 2 * yi + ci

    chunks = _pack_shards(even_w_in, even_w_out, odd_w_in, q_b, kv_b, odd_w_out, ffn_w_gate, ffn_w_up, ffn_w_down)

    def lane_pad(a):
        return jnp.pad(a, ((0, 0), (0, 128 - a.shape[1])))

    tile = jnp.concatenate([lane_pad(sc_conv_w[0]), lane_pad(pool_scale), lane_pad(q_a_norm), lane_pad(kv_a_norm),
                            jnp.zeros((2, 128), F32)], axis=0)
    chip = 2 * xi + yi
    where = jnp.stack([ci, chip, chip ^ 2, chip ^ 1, chip ^ 3]).astype(jnp.int32)
    grad_x, (pb_ffn1, gb_ffn1), in_flight, gp_mix0, tot = _step(
        x, positions, loss_target, chunks, tile, where, mix_norm, ffn_norm, sg_ln_g, sg_w_s, sg_b_s,
        pool_w, q_norm, k_norm)

    (ga_mix0,) = _comm_alone(_pair_exchange_comm(gp_mix0), "rs_pair_exchange_mix0")
    pb_mix0 = _rs_pair_sum(gp_mix0, ga_mix0, where, "rs_pair_sum_mix0")
    mix0_sems, pb_mix0, land_mix0, started = _chip_exchange_start(pb_mix0, "mix0")
    landed = [_chip_exchange_wait(*parts, started, tag) for parts, tag in zip(in_flight, ("ffn0a", "ffn0b", "mix1"))]
    gsh_ffn0a, gsh_ffn0b, gsh_mix1, gsh_ffn1 = _rs_final_sums(
        [pb for pb, _ in landed] + [pb_ffn1], [gb for _, gb in landed] + [gb_ffn1], "rs_final_sums", started)

    (g_mix, g_ffn, g_lng, g_ws, g_bs, g_cw_full, g_pw, g_ps_full, g_qa_full, g_kva_full, g_qn, g_kn, loss) = tot
    g_cw = lax.dynamic_slice_in_dim(g_cw_full, me * 64, 64, axis=1)[None]
    g_ps = lax.dynamic_slice_in_dim(g_ps_full, me * 32, 32, axis=1)
    g_qa = lax.dynamic_slice_in_dim(g_qa_full, me * 48, 48, axis=1)
    g_kva = lax.dynamic_slice_in_dim(g_kva_full, me * 32, 32, axis=1)

    def tr(a):
        return jnp.swapaxes(a, -1, -2)

    g_gate = tr(jnp.stack([gsh_ffn0a[OFF_GATE:OFF_GATE + N_FF], gsh_ffn1[OFF_GATE:OFF_GATE + N_FF]]))
    g_up = tr(jnp.stack([gsh_ffn0a[OFF_UP:OFF_UP + N_FF], gsh_ffn1[OFF_UP:OFF_UP + N_FF]]))
    g_down = jnp.stack([gsh_ffn0b, gsh_ffn1[R_GU:R_GU + N_FF]])
    g_oin = gsh_mix1[OFF_OIN:OFF_OIN + N_SQ, :ODD_IN][None]
    g_oout = gsh_mix1[OFF_OOUT:OFF_OOUT + N_SQ][None]
    g_qb = tr(gsh_mix1[OFF_QB:OFF_QB + N_QB_USED].reshape(1, 144, Q_LORA))
    g_kvb = tr(gsh_mix1[OFF_KVB:OFF_KVB + N_KVB].reshape(1, 192, KV_LORA))
    transposed = ("even_w_in", "odd_w_in", "q_b", "kv_b", "ffn_w_gate", "ffn_w_up")

    names = ("mix_norm", "ffn_norm", "even_w_in", "sg_ln_g", "sg_w_s", "sg_b_s", "sc_conv_w", "even_w_out",
             "odd_w_in", "pool_w", "pool_scale", "q_a_norm", "q_b", "kv_a_norm", "kv_b", "q_norm", "k_norm",
             "odd_w_out", "ffn_w_gate", "ffn_w_up", "ffn_w_down")
    grads = dict(mix_norm=g_mix, ffn_norm=g_ffn, sg_ln_g=g_lng, sg_w_s=g_ws, sg_b_s=g_bs,
                 sc_conv_w=g_cw, odd_w_in=g_oin, pool_w=g_pw, pool_scale=g_ps, q_a_norm=g_qa,
                 q_b=g_qb, kv_a_norm=g_kva, kv_b=g_kvb, q_norm=g_qn, k_norm=g_kn, odd_w_out=g_oout,
                 ffn_w_gate=g_gate, ffn_w_up=g_up, ffn_w_down=g_down)
    weights = dict(mix_norm=mix_norm, ffn_norm=ffn_norm, even_w_in=even_w_in, sg_ln_g=sg_ln_g, sg_w_s=sg_w_s,
                   sg_b_s=sg_b_s, sc_conv_w=sc_conv_w, even_w_out=even_w_out, odd_w_in=odd_w_in, pool_w=pool_w,
                   pool_scale=pool_scale, q_a_norm=q_a_norm, q_b=q_b, kv_a_norm=kv_a_norm, kv_b=kv_b, q_norm=q_norm,
                   k_norm=k_norm, odd_w_out=odd_w_out, ffn_w_gate=ffn_w_gate, ffn_w_up=ffn_w_up,
                   ffn_w_down=ffn_w_down)
    m_in = dict(mix_norm=m_mix_norm, ffn_norm=m_ffn_norm, even_w_in=m_even_w_in, sg_ln_g=m_sg_ln_g, sg_w_s=m_sg_w_s,
                sg_b_s=m_sg_b_s, sc_conv_w=m_sc_conv_w, even_w_out=m_even_w_out, odd_w_in=m_odd_w_in,
                pool_w=m_pool_w, pool_scale=m_pool_scale, q_a_norm=m_q_a_norm, q_b=m_q_b, kv_a_norm=m_kv_a_norm,
                kv_b=m_kv_b, q_norm=m_q_norm, k_norm=m_k_norm, odd_w_out=m_odd_w_out, ffn_w_gate=m_ffn_w_gate,
                ffn_w_up=m_ffn_w_up, ffn_w_down=m_ffn_w_down)
    v_in = dict(mix_norm=v_mix_norm, ffn_norm=v_ffn_norm, even_w_in=v_even_w_in, sg_ln_g=v_sg_ln_g, sg_w_s=v_sg_w_s,
                sg_b_s=v_sg_b_s, sc_conv_w=v_sc_conv_w, even_w_out=v_even_w_out, odd_w_in=v_odd_w_in,
                pool_w=v_pool_w, pool_scale=v_pool_scale, q_a_norm=v_q_a_norm, q_b=v_q_b, kv_a_norm=v_kv_a_norm,
                kv_b=v_kv_b, q_norm=v_q_norm, k_norm=v_k_norm, odd_w_out=v_odd_w_out, ffn_w_gate=v_ffn_w_gate,
                ffn_w_up=v_ffn_w_up, ffn_w_down=v_ffn_w_down)
    delta, new_m, new_v = {}, {}, {}

    def as2d(k, a):
        a = tr(a) if k in transposed else a
        return a.reshape(-1, a.shape[-1])

    def back(k, a):
        shape = weights[k].shape
        return tr(a.reshape(shape[:-2] + (shape[-1], shape[-2]))) if k in transposed else a.reshape(shape)

    def update(group, name, nblk=1):
        outs = _adamw([as2d(k, weights[k]) for k in group], [as2d(k, grads[k]) for k in group],
                      [as2d(k, m_in[k]) for k in group], [as2d(k, v_in[k]) for k in group], name, nblk)
        for i, k in enumerate(group):
            delta[k], new_m[k], new_v[k] = (back(k, o[i]) for o in outs)

    update(["ffn_w_gate", "ffn_w_up", "ffn_w_down"], "adamw_ffn", 4)
    update(["odd_w_in", "odd_w_out"], "adamw_mix1", 2)
    update([k for k in names if k not in delta and k not in ("even_w_in", "even_w_out")], "adamw_small")

    pb_mix0, gb_mix0 = _chip_exchange_wait(mix0_sems, pb_mix0, land_mix0, new_v["k_norm"], "mix0")
    (gsh_mix0,) = _rs_final_sums([pb_mix0], [gb_mix0], "rs_final_sum_mix0")
    grads["even_w_in"] = tr(gsh_mix0[OFF_EIN:OFF_EIN + N_EIN][None])
    grads["even_w_out"] = gsh_mix0[OFF_EOUT:OFF_EOUT + N_SQ][None]
    update(["even_w_in", "even_w_out"], "adamw_mix0", 2)

    return (loss.reshape(()), grad_x, *[grads[k] for k in names], *[delta[k] for k in names],
            *[new_m[k] for k in names], *[new_v[k] for k in names])
```

```python
import functools

import numpy as np
import jax
import jax.numpy as jnp
from jax import lax
from jax.experimental import pallas as pl
from jax.experimental.pallas import tpu as pltpu

F32 = jnp.float32
BF16 = jnp.bfloat16
MESH = pl.DeviceIdType.MESH

D = 1024
EPS = 1e-6
NEG_INF = -1e30
SG_HEADS, SG_HD, SG_W, SG_CHUNK = 4, 128, 512, 128
SC_W = 512
EVEN_IN = 2560
POOL_W = 256
POOL_GD = 64
Q_LORA, KV_LORA, QK_ROPE, QK_NOPE, V_DIM = 384, 256, 64, 128, 128
QK_DIM = QK_NOPE + QK_ROPE
HEADS = 6
HP = 256
ODD_IN = 960
D_FF = 2816
ROPE_THETA = 10000.0
ATT_SCALE = QK_DIM ** -0.5
LR, B1, B2, ADAM_EPS, WD, STEP = 0.001, 0.9, 0.999, 1e-08, 0.01, 10

N_DEV = 8
TB = 512
TB_FFN_BWD = 256
TK_DW = 2048
HALO = 16
VMEM_LIMIT = 56 * 1024 * 1024

N_EIN, N_FF, N_SQ = 320, 352, 128
OFF_EIN, OFF_EOUT, R_MIX0 = 0, 384, 512
OFF_GATE, OFF_UP, R_GU = 0, 352, 704
OFF_OIN, OFF_OOUT, OFF_QB, OFF_KVB, R_MIX1 = 0, 128, 256, 320, 384
N_QB, N_QB_USED, N_KVB = 64, 54, 48

INV_SQRT2 = 0.7071067811865476
INV_SQRT_2PI = 0.3989422804014327


def _dot(a, b, ca, cb):
    return lax.dot_general(a, b, (((ca,), (cb,)), ((), ())), preferred_element_type=F32)


def _cparams(n_axes=1):
    return pltpu.CompilerParams(dimension_semantics=("arbitrary",) * n_axes, vmem_limit_bytes=VMEM_LIMIT)


def _wspec(n, off):
    assert off % n == 0
    idx = off // n
    return pl.BlockSpec((N_DEV, n, D), lambda i: (0, idx, 0), pipeline_mode=pl.Buffered(1))


def _const_spec(shape):
    zeros = (0,) * len(shape)
    return pl.BlockSpec(shape, lambda *_: zeros)


class _Comm:
    def __init__(self, ins, out_shapes, sems, start, wait, mid=None):
        self.ins, self.out_shapes, self.sems, self.start, self.wait, self.mid = ins, out_shapes, sems, start, wait, mid


def _both(c1, c2):
    def split(f1, f2):
        def run(ins, outs, sems):
            f1(ins[:len(c1.ins)], outs[:len(c1.out_shapes)], sems[:len(c1.sems)])
            f2(ins[len(c1.ins):], outs[len(c1.out_shapes):], sems[len(c1.sems):])
        return run

    assert c1.mid is None and c2.mid is None
    return _Comm(c1.ins + c2.ins, c1.out_shapes + c2.out_shapes, c1.sems + c2.sems,
                 split(c1.start, c2.start), split(c1.wait, c2.wait))


def _call(body, name, grid, in_specs, out_specs, out_shape, args, scratch_shapes=(), comm=None, aliases=None):
    n_axes = len(grid)
    aliases = aliases or {}
    if comm is None:
        res = pl.pallas_call(
            body, name=name, grid=grid, in_specs=list(in_specs), out_specs=list(out_specs),
            out_shape=list(out_shape), scratch_shapes=list(scratch_shapes), input_output_aliases=aliases,
            compiler_params=_cparams(n_axes))(*args)
        return list(res), []
    ni, no, ns = len(in_specs), len(out_specs), len(scratch_shapes)
    ci, co = len(comm.ins), len(comm.out_shapes)
    n_steps = int(np.prod(grid))

    def carrier(*refs):
        ins, cin = refs[:ni], refs[ni:ni + ci]
        outs, cout = refs[ni + ci:ni + ci + no], refs[ni + ci + no:ni + ci + no + co]
        scr, sems = refs[ni + ci + no + co:ni + ci + no + co + ns], refs[ni + ci + no + co + ns:]
        step = 0
        for a in range(n_axes):
            step = step * grid[a] + pl.program_id(a)

        @pl.when(step == 0)
        def _():
            comm.start(cin, cout, sems)

        body(*ins, *outs, *scr)

        if comm.mid is not None and n_steps >= 4:
            @pl.when(step == n_steps // 2)
            def _():
                comm.mid(cin, cout, sems)

        @pl.when(step == n_steps - 1)
        def _():
            if comm.mid is not None and n_steps < 4:
                comm.mid(cin, cout, sems)
            comm.wait(cin, cout, sems)

    any_spec = pl.BlockSpec(memory_space=pl.ANY)
    res = pl.pallas_call(
        carrier, name=name, grid=grid, in_specs=list(in_specs) + [any_spec] * ci,
        out_specs=list(out_specs) + [any_spec] * co, out_shape=list(out_shape) + list(comm.out_shapes),
        scratch_shapes=list(scratch_shapes) + list(comm.sems), input_output_aliases=aliases,
        compiler_params=_cparams(n_axes))(*args, *comm.ins)
    return list(res[:no]), list(res[no:])


def _comm_alone(comm, name):
    ci, co = len(comm.ins), len(comm.out_shapes)

    def body(*refs):
        cin, cout, sems = refs[:ci], refs[ci:ci + co], refs[ci + co:]
        comm.start(cin, cout, sems)
        if comm.mid is not None:
            comm.mid(cin, cout, sems)
        comm.wait(cin, cout, sems)

    any_spec = pl.BlockSpec(memory_space=pl.ANY)
    res = pl.pallas_call(
        body, name=name, out_shape=list(comm.out_shapes), in_specs=[any_spec] * ci, out_specs=[any_spec] * co,
        scratch_shapes=list(comm.sems))(*comm.ins)
    return list(res)


def _rms(x, g):
    r = lax.rsqrt(jnp.mean(x * x, axis=-1, keepdims=True) + EPS)
    return x * r * g, r


def _rms_bwd(x, r, g, dy):
    xh = x * r
    dxh = dy * g
    dx = r * (dxh - xh * jnp.mean(dxh * xh, axis=-1, keepdims=True))
    dg = jnp.sum(dy * xh, axis=0, keepdims=True)
    return dx, dg


def _gelu(x):
    return 0.5 * x * (1.0 + lax.erf(x * INV_SQRT2))


def _gelu_grad(x):
    return 0.5 * (1.0 + lax.erf(x * INV_SQRT2)) + x * jnp.exp(-0.5 * x * x) * INV_SQRT_2PI


def _shift_down(a, k):
    rows = lax.broadcasted_iota(jnp.int32, a.shape, 0)
    return jnp.where(rows >= k, pltpu.roll(a, k, 0), 0.0)


def _shift_up(a, k):
    n = a.shape[0]
    rows = lax.broadcasted_iota(jnp.int32, a.shape, 0)
    return jnp.where(rows < n - k, pltpu.roll(a, n - k, 0), 0.0)


def _tril_bf16(w):
    r = lax.broadcasted_iota(jnp.int32, w.shape, 0)
    c = lax.broadcasted_iota(jnp.int32, w.shape, 1)
    return jnp.where(r >= c, w, 0.0).astype(BF16)


def _ln_head(vh, g):
    mu = jnp.mean(vh, axis=-1, keepdims=True)
    xc = vh - mu
    rr = lax.rsqrt(jnp.mean(xc * xc, axis=-1, keepdims=True) + EPS)
    xh = xc * rr
    return xh * g, xh, rr


def _conv_fwd(z, tail, cw_ref):
    ext = jnp.concatenate([tail, z], axis=0)
    zs1 = _shift_down(ext, 1)[HALO:]
    zs2 = _shift_down(ext, 2)[HALO:]
    y = cw_ref[2:3, :] * z + cw_ref[1:2, :] * zs1 + cw_ref[0:1, :] * zs2
    return y, zs1, zs2


def _pool_cnt(shape, blk_in_seq):
    rows = lax.broadcasted_iota(jnp.int32, shape, 0)
    grp = lax.broadcasted_iota(jnp.int32, shape, 1) // POOL_GD
    win = jnp.where(grp == 0, 2, jnp.where(grp == 1, 4, jnp.where(grp == 2, 8, 16)))
    tpos = blk_in_seq * shape[0] + rows + 1
    return jnp.minimum(tpos, win).astype(F32), grp


def _pool_select(grp, s2, s4, s8, s16):
    return jnp.where(grp == 0, s2, jnp.where(grp == 1, s4, jnp.where(grp == 2, s8, s16)))


def _pool_fwd(z, tail, blk_in_seq):
    ext = jnp.concatenate([tail, z], axis=0)
    s2 = ext + _shift_down(ext, 1)
    s4 = s2 + _shift_down(s2, 2)
    s8 = s4 + _shift_down(s4, 4)
    s16 = s8 + _shift_down(s8, 8)
    cnt, grp = _pool_cnt(z.shape, blk_in_seq)
    sums = _pool_select(grp, s2[HALO:], s4[HALO:], s8[HALO:], s16[HALO:])
    return sums / cnt - z, cnt, grp


def _pool_bwd(dpooled, dpm, head, grp):
    n = dpm.shape[0]
    ext = jnp.concatenate([dpm, head], axis=0)
    u2 = ext + _shift_up(ext, 1)
    u4 = u2 + _shift_up(u2, 2)
    u8 = u4 + _shift_up(u4, 4)
    u16 = u8 + _shift_up(u8, 8)
    return _pool_select(grp, u2[:n], u4[:n], u8[:n], u16[:n]) - dpooled


def _lane_sums(a):
    return _dot(a.astype(BF16), jnp.ones((a.shape[1], a.shape[1]), BF16), 1, 0)


def _swap_halves(y1):
    src = lax.broadcasted_iota(jnp.int32, (128, 128), 0)
    dst = lax.broadcasted_iota(jnp.int32, (128, 128), 1)
    perm = jnp.where(((dst < 32) & (src == dst + 32)) | ((dst >= 32) & (dst < QK_ROPE) & (src == dst - 32)), 1.0, 0.0)
    return _dot(y1.astype(BF16), perm.astype(BF16), 1, 0)


def _rope(y1, c, s):
    return y1 * c + _swap_halves(y1) * s


def _rope_bwd(d1, c, s):
    return d1 * c + _swap_halves(d1 * s)


def _qk_prep(x, g, c, s):
    r = lax.rsqrt(_lane_sums(x * x) * (1.0 / QK_DIM) + EPS)
    y = x * r * g
    return jnp.concatenate([y[:, :128], _rope(y[:, 128:], c, s)], axis=1), r


def _qk_prep_bwd(dout, x, r, g, c, s):
    dy = jnp.concatenate([dout[:, :128], _rope_bwd(dout[:, 128:], c, s)], axis=1)
    xh = x * r
    dxh = dy * g
    dx = r * (dxh - xh * (_lane_sums(dxh * xh) * (1.0 / QK_DIM)))
    return dx, jnp.sum(dy * xh, axis=0, keepdims=True)


def _place():
    return lax.axis_index("x"), lax.axis_index("y"), lax.axis_index("c")


def _gather_comm(arrs):
    n = len(arrs)

    def halves(a):
        rows = arrs[a].shape[0]
        tile = 16 if arrs[a].dtype == BF16 else 8
        top = rows // 2 if rows % (2 * tile) == 0 else rows
        return (0, top), (top, rows - top)

    def plan(ins, outs, sems):
        send_sems, recv_sems, local_sems = sems
        x, y, c = _place()
        me, sib, xn, yn, dg = (x, y, c), (x, y, 1 - c), (1 - x, y, c), (x, 1 - y, c), (1 - x, 1 - y, c)

        def slot(a, dev, part=None):
            ref = outs[a].at[4 * dev[0] + 2 * dev[1] + dev[2]]
            return ref if part is None else ref.at[pl.ds(part[0], part[1])]

        def copy(a, k, block, to, src=None, part=None):
            return pltpu.make_async_remote_copy(
                src_ref=slot(a, block, part) if src is None else src, dst_ref=slot(a, block, part),
                send_sem=send_sems.at[a, k], recv_sem=recv_sems.at[a, k], device_id=to, device_id_type=MESH)

        local = [pltpu.make_async_copy(ins[a], slot(a, me), local_sems.at[a]) for a in range(n)]
        return me, sib, xn, yn, dg, copy, local

    def start(ins, outs, sems):
        me, sib, xn, yn, _, copy, local = plan(ins, outs, sems)
        for a in range(n):
            local[a].start()
            for k, to in enumerate((sib, xn, yn)):
                copy(a, k, me, to, src=ins[a]).start()

    def mid(ins, outs, sems):
        me, sib, xn, yn, _, copy, _ = plan(ins, outs, sems)
        for a in range(n):
            top, bottom = halves(a)
            copy(a, 1, xn, me).wait_recv()
            copy(a, 3, xn, yn, part=top).start()
            copy(a, 5, xn, sib).start()
            copy(a, 2, yn, me).wait_recv()
            if bottom[1]:
                copy(a, 4, yn, xn, part=bottom).start()
            copy(a, 6, yn, sib).start()

    def wait(ins, outs, sems):
        me, sib, xn, yn, dg, copy, local = plan(ins, outs, sems)
        other = lambda dev: (dev[0], dev[1], 1 - dev[2])
        for a in range(n):
            top, bottom = halves(a)
            copy(a, 3, dg, me, part=top).wait_recv()
            if bottom[1]:
                copy(a, 4, dg, me, part=bottom).wait_recv()
            copy(a, 7, dg, sib).start()
        for a in range(n):
            top, bottom = halves(a)
            for k, block in ((0, sib), (5, other(xn)), (6, other(yn)), (7, other(dg))):
                copy(a, k, block, me).wait_recv()
            for k, block in ((0, me), (1, me), (2, me), (5, xn), (6, yn), (7, dg)):
                copy(a, k, block, me, src=ins[a] if k < 3 else None).wait_send()
            copy(a, 3, xn, me, part=top).wait_send()
            if bottom[1]:
                copy(a, 4, yn, me, part=bottom).wait_send()
            local[a].wait()

    return _Comm(
        list(arrs), [jax.ShapeDtypeStruct((N_DEV,) + a.shape, a.dtype) for a in arrs],
        [pltpu.SemaphoreType.DMA((n, 8)), pltpu.SemaphoreType.DMA((n, 8)), pltpu.SemaphoreType.DMA((n,))],
        start, wait, mid)


def _sum_gathered(g):
    rows = g.shape[1]

    def body(g_ref, sum_ref):
        total = g_ref[0]
        for d in range(1, N_DEV):
            total = total + g_ref[d]
        sum_ref[...] = total

    return pl.pallas_call(
        body, name="sum_gathered_small", out_shape=jax.ShapeDtypeStruct((rows, 128), F32), grid=(1,),
        in_specs=[pl.BlockSpec((N_DEV, rows, 128), lambda i: (0, 0, 0))],
        out_specs=pl.BlockSpec((rows, 128), lambda i: (0, 0)), compiler_params=_cparams(1),
    )(g)


def _sum_rows(rows):
    return rows if rows <= 512 else rows // 2


def _pair_exchange_comm(gp):
    _, rows, cols = gp.shape

    def copies(ins, outs, sems):
        send_sems, recv_sems = sems
        x, y, c = _place()
        return [pltpu.make_async_remote_copy(
            src_ref=ins[0].at[2 * j + (1 - c)], dst_ref=outs[0].at[j], send_sem=send_sems.at[j],
            recv_sem=recv_sems.at[j], device_id=(x, y, 1 - c), device_id_type=MESH) for j in range(4)]

    def start(ins, outs, sems):
        for cp in copies(ins, outs, sems):
            cp.start()

    def wait(ins, outs, sems):
        for cp in copies(ins, outs, sems):
            cp.wait()

    return _Comm([gp], [jax.ShapeDtypeStruct((4, rows, cols), gp.dtype)],
                 [pltpu.SemaphoreType.DMA((4,)), pltpu.SemaphoreType.DMA((4,))], start, wait)


def _rs_pair_sum(gp, got, where, name):
    _, rows, cols = got.shape
    rb = _sum_rows(rows)
    gp4 = gp.reshape(4, 2, rows, cols)

    def body(w_ref, a_ref, b_ref, o_ref):
        o_ref[0] = (a_ref[0, 0].astype(F32) + b_ref[0].astype(F32)).astype(o_ref.dtype)

    return pl.pallas_call(
        body, name=name, out_shape=jax.ShapeDtypeStruct((4, rows, cols), gp.dtype),
        grid_spec=pltpu.PrefetchScalarGridSpec(
            num_scalar_prefetch=1, grid=(4, rows // rb),
            in_specs=[pl.BlockSpec((1, 1, rb, cols), lambda k, r, w: (w[1 + k], w[0], r, 0)),
                      pl.BlockSpec((1, rb, cols), lambda k, r, w: (w[1 + k], r, 0))],
            out_specs=pl.BlockSpec((1, rb, cols), lambda k, r, w: (k, r, 0))),
        compiler_params=_cparams(2),
    )(where, gp4, got)


def _chip_exchange_comm(pb):
    _, rows, cols = pb.shape

    def copies(ins, outs, sems):
        send_sems, recv_sems = sems
        x, y, c = _place()
        chips = [(1 - x, y), (x, 1 - y), (1 - x, 1 - y)]
        return [pltpu.make_async_remote_copy(
            src_ref=ins[0].at[1 + k], dst_ref=outs[0].at[k], send_sem=send_sems.at[k],
            recv_sem=recv_sems.at[k], device_id=(px, py, c), device_id_type=MESH)
            for k, (px, py) in enumerate(chips)]

    def start(ins, outs, sems):
        for cp in copies(ins, outs, sems):
            cp.start()

    def wait(ins, outs, sems):
        for cp in copies(ins, outs, sems):
            cp.wait()

    return _Comm([pb], [jax.ShapeDtypeStruct((3, rows, cols), pb.dtype)],
                 [pltpu.SemaphoreType.DMA((3,)), pltpu.SemaphoreType.DMA((3,))], start, wait)


def _chip_exchange_start(pb):
    _, rows, cols = pb.shape

    def body(pb_ref, land_ref, *rest):
        sems, token = rest[:6], rest[8]
        x, y, c = _place()
        chips = [(1 - x, y), (x, 1 - y), (1 - x, 1 - y)]
        for k, (px, py) in enumerate(chips):
            pltpu.make_async_remote_copy(
                src_ref=pb_ref.at[1 + k], dst_ref=land_ref.at[k], send_sem=sems[k], recv_sem=sems[3 + k],
                device_id=(px, py, c), device_id_type=MESH).start()
        token[...] = jnp.zeros_like(token)

    hbm = pl.BlockSpec(memory_space=pltpu.HBM)
    sem = pl.BlockSpec(memory_space=pltpu.SEMAPHORE)
    land = lax.empty((3, rows, cols), pb.dtype)
    res = pl.pallas_call(
        body, name="rs_chip_exchange_start_mix0",
        out_shape=(*[pltpu.SemaphoreType.DMA(())] * 6, pltpu.HBM(pb.shape, pb.dtype), pltpu.HBM(land.shape, land.dtype),
                   jax.ShapeDtypeStruct((8, 128), F32)),
        in_specs=(hbm, hbm), out_specs=(*[sem] * 6, hbm, hbm, pl.BlockSpec(memory_space=pltpu.VMEM)),
        input_output_aliases={0: 6, 1: 7},
        compiler_params=pltpu.CompilerParams(has_side_effects=pltpu.SideEffectType.DATAFLOW_SIDE_EFFECTING),
    )(pltpu.with_memory_space_constraint(pb, pltpu.HBM), pltpu.with_memory_space_constraint(land, pltpu.HBM))
    return list(res[:6]), res[6], res[7], res[8]


def _chip_exchange_wait(sems, pb_thru, land_thru, after):
    def body(pb_ref, land_ref, *rest):
        sems_in = rest[:6]
        x, y, c = _place()
        chips = [(1 - x, y), (x, 1 - y), (1 - x, 1 - y)]
        for k, (px, py) in enumerate(chips):
            cp = pltpu.make_async_remote_copy(
                src_ref=pb_ref.at[1 + k], dst_ref=land_ref.at[k], send_sem=sems_in[k], recv_sem=sems_in[3 + k],
                device_id=(px, py, c), device_id_type=MESH)
            cp.wait_send()
            cp.wait_recv()

    hbm = pl.BlockSpec(memory_space=pltpu.HBM)
    sem = pl.BlockSpec(memory_space=pltpu.SEMAPHORE)
    res = pl.pallas_call(
        body, name="rs_chip_exchange_wait_mix0",
        out_shape=(pltpu.HBM(pb_thru.shape, pb_thru.dtype), pltpu.HBM(land_thru.shape, land_thru.dtype)),
        in_specs=(hbm, hbm, *[sem] * 6, pl.BlockSpec(memory_space=pl.ANY)), out_specs=(hbm, hbm),
        input_output_aliases={0: 0, 1: 1},
        compiler_params=pltpu.CompilerParams(has_side_effects=pltpu.SideEffectType.DATAFLOW_SIDE_EFFECTING),
    )(pb_thru, land_thru, *sems, after)
    return res[0], res[1]


def _rs_final_sums(pbs, gots, name, after=None):
    n = len(pbs)

    def body(*refs):
        outs = refs[len(refs) - n:]
        for a in range(n):
            m_ref, g_ref, o_ref = refs[a], refs[n + a], outs[a]
            o_ref[...] = ((m_ref[0].astype(F32) + g_ref[0].astype(F32)) + g_ref[1].astype(F32)) + g_ref[2].astype(F32)

    half = [pb.shape[1] // 2 for pb in pbs]
    in_specs = ([pl.BlockSpec((1, h, D), lambda i: (0, i, 0)) for h in half]
                + [pl.BlockSpec((3, h, D), lambda i: (0, i, 0)) for h in half])
    args = (*pbs, *gots)
    if after is not None:
        in_specs, args = in_specs + [pl.BlockSpec(memory_space=pl.ANY)], args + (after,)
    res, _ = _call(body, name, (2,), in_specs, [pl.BlockSpec((h, D), lambda i: (i, 0)) for h in half],
                   [jax.ShapeDtypeStruct(pb.shape[1:], F32) for pb in pbs], args)
    return res


def _rope_tables(pos_col, inv_freq, comm=None):
    t = pos_col.shape[0]

    def body(p_ref, f_ref, c_ref, s_ref):
        ang = p_ref[...].astype(F32) * f_ref[...]
        lane = lax.broadcasted_iota(jnp.int32, ang.shape, 1)
        c_ref[...] = jnp.where(lane < QK_ROPE, jnp.cos(ang), 0.0)
        s = jnp.sin(ang)
        s_ref[...] = jnp.where(lane < 32, -s, jnp.where(lane < QK_ROPE, s, 0.0))

    spec = pl.BlockSpec((TB, 128), lambda i: (i, 0))
    return _call(
        body, "rope_tables", (t // TB,), [pl.BlockSpec((TB, 1), lambda i: (i, 0)), _const_spec((1, 128))],
        [spec] * 2, [jax.ShapeDtypeStruct((t, 128), F32)] * 2, (pos_col, inv_freq), (), comm)


def _sgu_conv_fwd(proj, tail, lng_ref, ws_ref, bst_ref, cw_ref):
    gu = _gelu(proj[:, 0:SG_W])
    gv = _gelu(proj[:, SG_W:2 * SG_W])
    bg = proj[:, 1024:1536]
    z = proj[:, 1536:2048] * proj[:, 2048:2560]
    heads = []
    for h in range(SG_HEADS):
        sl = slice(h * SG_HD, (h + 1) * SG_HD)
        vn, _, _ = _ln_head(gv[:, sl], lng_ref[:, sl])
        vnb = vn.astype(BF16)
        wm = _tril_bf16(ws_ref[h])
        bcol = bst_ref[:, h:h + 1]
        mixed = jnp.concatenate(
            [_dot(wm, vnb[k * SG_CHUNK:(k + 1) * SG_CHUNK], 1, 0) + bcol for k in range(TB // SG_CHUNK)], axis=0)
        heads.append(gu[:, sl] * mixed)
    a_out = jnp.concatenate(heads, axis=1)
    y, _, _ = _conv_fwd(z, tail, cw_ref)
    return a_out, bg * y, z


def _even_fwd(x, wg, gamma, lng, ws, bst, cw, seq, comm=None):
    t = x.shape[0]
    nbs = seq // TB

    def body(x_ref, gam_ref, win_ref, wout_ref, lng_ref, ws_ref, bst_ref, cw_ref, x1_ref, proj_ref, tail_ref):
        i = pl.program_id(0)
        xv = x_ref[...]
        h, _ = _rms(xv, gam_ref[...])
        proj = _dot(h.astype(BF16), win_ref[...].reshape(EVEN_IN, D), 1, 1)
        proj_ref[...] = proj.astype(BF16)
        tail = jnp.where(i % nbs == 0, 0.0, tail_ref[...])
        a_out, b_out, z = _sgu_conv_fwd(proj, tail, lng_ref, ws_ref, bst_ref, cw_ref)
        tail_ref[...] = z[TB - HALO:, :]
        x1_ref[...] = (xv + _dot(a_out.astype(BF16), wout_ref[0:4].reshape(512, D), 1, 0)
                       + _dot(b_out.astype(BF16), wout_ref[4:8].reshape(512, D), 1, 0))

    row = pl.BlockSpec((TB, D), lambda i: (i, 0))
    return _call(
        body, "even_fwd", (t // TB,),
        [row, _const_spec((1, D)), _wspec(N_EIN, OFF_EIN), _wspec(N_SQ, OFF_EOUT), _const_spec((1, SG_W)),
         _const_spec((SG_HEADS, 128, 128)), _const_spec((128, 128)), _const_spec((8, SC_W))],
        [row, pl.BlockSpec((TB, EVEN_IN), lambda i: (i, 0))],
        [jax.ShapeDtypeStruct((t, D), F32), jax.ShapeDtypeStruct((t, EVEN_IN), BF16)],
        (x, gamma, wg, wg, lng, ws, bst, cw), [pltpu.VMEM((HALO, SC_W), F32)], comm)


def _even_bwd(x, proj, dx1, wg, gamma, lng, ws, bst, cw, seq, comm=None):
    t = x.shape[0]
    nb, nbs = t // TB, seq // TB

    def body(x_ref, proj_ref, ptail_ref, dx1_ref, gam_ref, win_ref, wout_ref, lng_ref, ws_ref, bst_ref, cw_ref,
             dx0_ref, dproj_ref, mix_ref, h_ref, dgam_ref, dws_ref, dbc_ref, dlng_ref, dcw_ref, head_ref):
        i = pl.program_id(0)
        blk = nb - 1 - i

        @pl.when(i == 0)
        def _():
            dgam_ref[...] = jnp.zeros_like(dgam_ref)
            dws_ref[...] = jnp.zeros_like(dws_ref)
            dbc_ref[...] = jnp.zeros_like(dbc_ref)
            dlng_ref[...] = jnp.zeros_like(dlng_ref)
            dcw_ref[...] = jnp.zeros_like(dcw_ref)

        xv = x_ref[...]
        gam = gam_ref[...]
        h, r = _rms(xv, gam)
        h_ref[...] = h.astype(BF16)
        dx1 = dx1_ref[...]
        dmix = _dot(dx1.astype(BF16), wout_ref[...].reshape(D, D), 1, 1)
        da, db = dmix[:, :SG_W], dmix[:, SG_W:]
        proj = proj_ref[...].astype(F32)
        u, v = proj[:, 0:SG_W], proj[:, SG_W:2 * SG_W]
        bg, cg, hv = proj[:, 1024:1536], proj[:, 1536:2048], proj[:, 2048:2560]
        gu, gv = _gelu(u), _gelu(v)

        a_heads, dgv_heads = [], []
        for hd in range(SG_HEADS):
            sl = slice(hd * SG_HD, (hd + 1) * SG_HD)
            g_h = lng_ref[:, sl]
            vn, xh, rr = _ln_head(gv[:, sl], g_h)
            vnb = vn.astype(BF16)
            wm = _tril_bf16(ws_ref[hd])
            bcol = bst_ref[:, hd:hd + 1]
            mixed_c, dvn_c = [], []
            dw_acc = jnp.zeros((128, 128), F32)
            db_acc = jnp.zeros((128, 1), F32)
            for k in range(TB // SG_CHUNK):
                rs = slice(k * SG_CHUNK, (k + 1) * SG_CHUNK)
                mixed = _dot(wm, vnb[rs], 1, 0) + bcol
                dmixed = da[rs, sl] * gu[rs, sl]
                dmb = dmixed.astype(BF16)
                dvn_c.append(_dot(wm, dmb, 0, 0))
                dw_acc = dw_acc + _dot(dmb, vnb[rs], 1, 1)
                db_acc = db_acc + jnp.sum(dmixed, axis=1, keepdims=True)
                mixed_c.append(mixed)
            mixed_h = jnp.concatenate(mixed_c, axis=0)
            dvn = jnp.concatenate(dvn_c, axis=0)
            r_i = lax.broadcasted_iota(jnp.int32, (128, 128), 0)
            c_i = lax.broadcasted_iota(jnp.int32, (128, 128), 1)
            dws_ref[hd] += jnp.where(r_i >= c_i, dw_acc, 0.0)
            dbc_ref[:, hd:hd + 1] += db_acc
            dlng_ref[:, sl] += jnp.sum(dvn * xh, axis=0, keepdims=True)
            dxh = dvn * g_h
            dgv = rr * (dxh - jnp.mean(dxh, axis=-1, keepdims=True)
                        - xh * jnp.mean(dxh * xh, axis=-1, keepdims=True))
            a_heads.append(gu[:, sl] * mixed_h)
            dproj_ref[:, sl] = (da[:, sl] * mixed_h * _gelu_grad(u[:, sl])).astype(BF16)
            dgv_heads.append(dgv * _gelu_grad(v[:, sl]))
        dproj_ref[:, SG_W:2 * SG_W] = jnp.concatenate(dgv_heads, axis=1).astype(BF16)
        mix_ref[:, :SG_W] = jnp.concatenate(a_heads, axis=1).astype(BF16)

        z = cg * hv
        pt = ptail_ref[...].astype(F32)
        tail = jnp.where(blk % nbs == 0, 0.0, pt[:, 1536:2048] * pt[:, 2048:2560])
        y, zs1, zs2 = _conv_fwd(z, tail, cw_ref)
        mix_ref[:, SG_W:] = (bg * y).astype(BF16)
        dy = db * bg
        head = jnp.where(blk % nbs == nbs - 1, 0.0, head_ref[...])
        ext = jnp.concatenate([dy, head], axis=0)
        dz = (cw_ref[2:3, :] * dy + cw_ref[1:2, :] * _shift_up(ext, 1)[:TB]
              + cw_ref[0:1, :] * _shift_up(ext, 2)[:TB])
        head_ref[...] = dy[:HALO, :]
        dcw_ref[2:3, :] += jnp.sum(dy * z, axis=0, keepdims=True)
        dcw_ref[1:2, :] += jnp.sum(dy * zs1, axis=0, keepdims=True)
        dcw_ref[0:1, :] += jnp.sum(dy * zs2, axis=0, keepdims=True)
        dproj_ref[:, 1024:1536] = (db * y).astype(BF16)
        dproj_ref[:, 1536:2048] = (dz * hv).astype(BF16)
        dproj_ref[:, 2048:2560] = (dz * cg).astype(BF16)

        dh = _dot(dproj_ref[...], win_ref[...].reshape(EVEN_IN, D), 1, 0)
        dxn, dgam = _rms_bwd(xv, r, gam, dh)
        dgam_ref[...] += dgam
        dx0_ref[...] = dx1 + dxn

    def rev(w):
        return pl.BlockSpec((TB, w), lambda i: (nb - 1 - i, 0))

    ptail = pl.BlockSpec((HALO, EVEN_IN), lambda i: (jnp.maximum((nb - 1 - i) * (TB // HALO) - 1, 0), 0))
    return _call(
        body, "even_bwd", (nb,),
        [rev(D), rev(EVEN_IN), ptail, rev(D), _const_spec((1, D)), _wspec(N_EIN, OFF_EIN),
         _wspec(N_SQ, OFF_EOUT), _const_spec((1, SG_W)), _const_spec((SG_HEADS, 128, 128)),
         _const_spec((128, 128)), _const_spec((8, SC_W))],
        [rev(D), rev(EVEN_IN), rev(D), rev(D), _const_spec((1, D)), _const_spec((SG_HEADS, 128, 128)),
         _const_spec((128, 128)), _const_spec((1, SG_W)), _const_spec((8, SC_W))],
        [jax.ShapeDtypeStruct((t, D), F32), jax.ShapeDtypeStruct((t, EVEN_IN), BF16),
         jax.ShapeDtypeStruct((t, D), BF16), jax.ShapeDtypeStruct((t, D), BF16),
         jax.ShapeDtypeStruct((1, D), F32), jax.ShapeDtypeStruct((SG_HEADS, 128, 128), F32),
         jax.ShapeDtypeStruct((128, 128), F32), jax.ShapeDtypeStruct((1, SG_W), F32),
         jax.ShapeDtypeStruct((8, SC_W), F32)],
        (x, proj, proj, dx1, gamma, wg, wg, lng, ws, bst, cw), [pltpu.VMEM((HALO, SC_W), F32)], comm)


def _last_block_fwd(x, c_out, d_out, w_mix1, w_gu, w_d, gamma, target):
    t = x.shape[0]

    def body(x_ref, c_ref, d_ref, wo_ref, gam_ref, wg_ref, wu_ref, wd_ref, t_ref,
             x3_ref, dy_ref, g_ref, u_ref, loss_ref):
        @pl.when(pl.program_id(0) == 0)
        def _():
            loss_ref[...] = jnp.zeros_like(loss_ref)

        xv = (x_ref[...] + _dot(c_ref[...], wo_ref[0:2].reshape(POOL_W, D), 1, 0)
              + _dot(d_ref[...], wo_ref[2:8].reshape(HEADS * V_DIM, D), 1, 0))
        x3_ref[...] = xv
        h, _ = _rms(xv, gam_ref[...])
        hb = h.astype(BF16)
        g = _dot(hb, wg_ref[...].reshape(D_FF, D), 1, 1)
        u = _dot(hb, wu_ref[...].reshape(D_FF, D), 1, 1)
        g_ref[...] = g.astype(BF16)
        u_ref[...] = u.astype(BF16)
        act = g * jax.nn.sigmoid(g) * u
        err = xv + _dot(act.astype(BF16), wd_ref[...].reshape(D_FF, D), 1, 0) - t_ref[...]
        dy_ref[...] = err * (1.0 / D)
        sq = jnp.sum(jnp.sum(err * err, axis=-1, keepdims=True), axis=0, keepdims=True)
        loss_ref[...] += (0.5 / D) * sq

    def row(w):
        return pl.BlockSpec((TB, w), lambda i: (i, 0))

    res, _ = _call(
        body, "last_block_fwd", (t // TB,),
        [row(D), row(POOL_W), row(HEADS * V_DIM), _wspec(N_SQ, OFF_OOUT), _const_spec((1, D)),
         _wspec(N_FF, OFF_GATE), _wspec(N_FF, OFF_UP), _wspec(N_FF, 0), row(D)],
        [row(D), row(D), row(D_FF), row(D_FF), _const_spec((8, 128))],
        [jax.ShapeDtypeStruct((t, D), F32), jax.ShapeDtypeStruct((t, D), F32), jax.ShapeDtypeStruct((t, D_FF), BF16),
         jax.ShapeDtypeStruct((t, D_FF), BF16), jax.ShapeDtypeStruct((8, 128), F32)],
        (x, c_out, d_out, w_mix1, gamma, w_gu, w_gu, w_d, target))
    return res


def _ffn_up(x, w_gu, gamma, name, comm=None):
    t = x.shape[0]

    def body(x_ref, gam_ref, wg_ref, wu_ref, g_ref, u_ref, act_ref):
        h, _ = _rms(x_ref[...], gam_ref[...])
        hb = h.astype(BF16)
        g = _dot(hb, wg_ref[...].reshape(D_FF, D), 1, 1)
        u = _dot(hb, wu_ref[...].reshape(D_FF, D), 1, 1)
        g_ref[...] = g.astype(BF16)
        u_ref[...] = u.astype(BF16)
        act_ref[...] = (g * jax.nn.sigmoid(g) * u).astype(BF16)

    row = pl.BlockSpec((TB, D), lambda i: (i, 0))
    wide = pl.BlockSpec((TB, D_FF), lambda i: (i, 0))
    return _call(body, name, (t // TB,), [row, _const_spec((1, D)), _wspec(N_FF, OFF_GATE), _wspec(N_FF, OFF_UP)],
                 [wide, wide, wide], [jax.ShapeDtypeStruct((t, D_FF), BF16)] * 3, (x, gamma, w_gu, w_gu), (), comm)


def _ffn_down(x, act, w_d, name, comm=None):
    t = x.shape[0]

    def body(x_ref, a_ref, wd_ref, y_ref):
        y_ref[...] = x_ref[...] + _dot(a_ref[...], wd_ref[...].reshape(D_FF, D), 1, 0)

    row = pl.BlockSpec((TB, D), lambda i: (i, 0))
    wide = pl.BlockSpec((TB, D_FF), lambda i: (i, 0))
    return _call(body, name, (t // TB,), [row, wide, _wspec(N_FF, 0)], [row], [jax.ShapeDtypeStruct((t, D), F32)],
                 (x, act, w_d), (), comm)


def _ffn_bwd(x, g, u, dy, w_gu, w_d, gamma, name, comm=None, w_mix1=None):
    t = x.shape[0]
    with_dmix = w_mix1 is not None

    def body(*refs):
        x_ref, g_ref, u_ref, dy_ref, gam_ref, wg_ref, wu_ref, wd_ref = refs[:8]
        dx_ref, act_ref, dg_ref, du_ref, h_ref, dgam_ref = refs[8 + with_dmix:14 + with_dmix]

        @pl.when(pl.program_id(0) == 0)
        def _():
            dgam_ref[...] = jnp.zeros_like(dgam_ref)

        xv = x_ref[...]
        gam = gam_ref[...]
        h, r = _rms(xv, gam)
        h_ref[...] = h.astype(BF16)
        dyv = dy_ref[...]
        dact = _dot(dyv.astype(BF16), wd_ref[...].reshape(D_FF, D), 1, 1)
        gv = g_ref[...].astype(F32)
        uv = u_ref[...].astype(F32)
        sg = jax.nn.sigmoid(gv)
        silu = gv * sg
        act_ref[...] = (silu * uv).astype(BF16)
        dgb = (dact * uv * (sg * (1.0 + gv * (1.0 - sg)))).astype(BF16)
        dub = (dact * silu).astype(BF16)
        dg_ref[...] = dgb
        du_ref[...] = dub
        dh = _dot(dgb, wg_ref[...].reshape(D_FF, D), 1, 0) + _dot(dub, wu_ref[...].reshape(D_FF, D), 1, 0)
        dxn, dgam = _rms_bwd(xv, r, gam, dh)
        dgam_ref[...] += dgam
        dx = dyv + dxn
        dx_ref[...] = dx
        if with_dmix:
            refs[15][...] = _dot(dx.astype(BF16), refs[8][...].reshape(D, D), 1, 1).astype(BF16)

    row = pl.BlockSpec((TB_FFN_BWD, D), lambda i: (i, 0))
    wide = pl.BlockSpec((TB_FFN_BWD, D_FF), lambda i: (i, 0))
    in_specs = [row, wide, wide, row, _const_spec((1, D)), _wspec(N_FF, OFF_GATE), _wspec(N_FF, OFF_UP),
                _wspec(N_FF, 0)]
    out_specs = [row, wide, wide, wide, row, _const_spec((1, D))]
    out_shape = [jax.ShapeDtypeStruct((t, D), F32), jax.ShapeDtypeStruct((t, D_FF), BF16),
                 jax.ShapeDtypeStruct((t, D_FF), BF16), jax.ShapeDtypeStruct((t, D_FF), BF16),
                 jax.ShapeDtypeStruct((t, D), BF16), jax.ShapeDtypeStruct((1, D), F32)]
    args = (x, g, u, dy, gamma, w_gu, w_gu, w_d)
    if with_dmix:
        in_specs, args = in_specs + [_wspec(N_SQ, OFF_OOUT)], args + (w_mix1,)
        out_specs, out_shape = out_specs + [row], out_shape + [jax.ShapeDtypeStruct((t, D), BF16)]
    return _call(body, name, (t // TB_FFN_BWD,), in_specs, out_specs, out_shape, args, (), comm)


def _odd_pre_fwd(x, wg, gamma, qbt, kvbt, qa_g, kva_g, pw_bd, pscale, seq, comm=None):
    t = x.shape[0]
    nbs = seq // TB

    def body(x_ref, gam_ref, win_ref, qb_ref, kvb_ref, qa_ref, kva_ref, pw_ref, ps_ref,
             proj_ref, q_ref, kv_ref, kr_ref, c_ref, tail_ref):
        i = pl.program_id(0)
        h, _ = _rms(x_ref[...], gam_ref[...])
        proj = _dot(h.astype(BF16), win_ref[...].reshape(D, D), 1, 0)
        proj_ref[...] = proj.astype(BF16)
        zp, ql, kvl = proj[:, :POOL_W], proj[:, 256:640], proj[:, 640:896]
        kr_ref[...] = proj[:, 896:1024]
        qn, _ = _rms(ql, qa_ref[...])
        q_ref[...] = _dot(qn.astype(BF16), qb_ref[...], 1, 1).astype(BF16)
        kvn, _ = _rms(kvl, kva_ref[...])
        kv_ref[...] = _dot(kvn.astype(BF16), kvb_ref[...], 1, 1).astype(BF16)
        tail = jnp.where(i % nbs == 0, 0.0, tail_ref[...])
        pooled, _, _ = _pool_fwd(zp, tail, i % nbs)
        tail_ref[...] = zp[TB - HALO:, :]
        c_ref[...] = (_dot(pooled.astype(BF16), pw_ref[...], 1, 0) * ps_ref[...]).astype(BF16)

    def row(w):
        return pl.BlockSpec((TB, w), lambda i: (i, 0))

    return _call(
        body, "odd_pre_fwd", (t // TB,),
        [row(D), _const_spec((1, D)), _wspec(N_SQ, OFF_OIN), _const_spec((HEADS * HP, Q_LORA)),
         _const_spec((HEADS * HP, KV_LORA)), _const_spec((1, Q_LORA)), _const_spec((1, KV_LORA)),
         _const_spec((POOL_W, POOL_W)), _const_spec((1, POOL_W))],
        [row(D), row(HEADS * HP), row(HEADS * HP), row(128), row(POOL_W)],
        [jax.ShapeDtypeStruct((t, D), BF16), jax.ShapeDtypeStruct((t, HEADS * HP), BF16),
         jax.ShapeDtypeStruct((t, HEADS * HP), BF16), jax.ShapeDtypeStruct((t, 128), F32),
         jax.ShapeDtypeStruct((t, POOL_W), BF16)],
        (x, gamma, wg, qbt, kvbt, qa_g, kva_g, pw_bd, pscale), [pltpu.VMEM((HALO, POOL_W), F32)], comm)


def _odd_pre_bwd(x, proj, dx3, dmix, dq, dkv, dkr, wg, gamma, qbt, kvbt, qa_g, kva_g, pw_bd, pscale, seq):
    t = x.shape[0]
    nb, nbs = t // TB, seq // TB

    def body(x_ref, proj_ref, ptail_ref, dx3_ref, dco_ref, dq_ref, dkv_ref, dkr_ref, gam_ref, win_ref, qb_ref,
             kvb_ref, qa_ref, kva_ref, pw_ref, ps_ref,
             dx2_ref, dproj_ref, h_ref, qn_ref, kvn_ref, dgam_ref, dqa_ref, dkva_ref, dpw_ref, dps_ref, head_ref):
        i = pl.program_id(0)
        blk = nb - 1 - i

        @pl.when(i == 0)
        def _():
            dgam_ref[...] = jnp.zeros_like(dgam_ref)
            dqa_ref[...] = jnp.zeros_like(dqa_ref)
            dkva_ref[...] = jnp.zeros_like(dkva_ref)
            dpw_ref[...] = jnp.zeros_like(dpw_ref)
            dps_ref[...] = jnp.zeros_like(dps_ref)

        xv = x_ref[...]
        gam = gam_ref[...]
        h, r = _rms(xv, gam)
        h_ref[...] = h.astype(BF16)
        proj = proj_ref[...].astype(F32)
        zp, ql, kvl = proj[:, :POOL_W], proj[:, 256:640], proj[:, 640:896]

        qa = qa_ref[...]
        qn, rq = _rms(ql, qa)
        qn_ref[...] = qn.astype(BF16)
        dql, dqa = _rms_bwd(ql, rq, qa, _dot(dq_ref[...], qb_ref[...], 1, 0))
        dqa_ref[...] += dqa
        kva = kva_ref[...]
        kvn, rkv = _rms(kvl, kva)
        kvn_ref[...] = kvn.astype(BF16)
        dkvl, dkva = _rms_bwd(kvl, rkv, kva, _dot(dkv_ref[...], kvb_ref[...], 1, 0))
        dkva_ref[...] += dkva

        pt = ptail_ref[...].astype(F32)
        tail = jnp.where(blk % nbs == 0, 0.0, pt[:, :POOL_W])
        pooled, cnt, grp = _pool_fwd(zp, tail, blk % nbs)
        pb = pooled.astype(BF16)
        pw = pw_ref[...]
        dco = dco_ref[...].astype(F32)
        dps_ref[...] += jnp.sum(dco * _dot(pb, pw, 1, 0), axis=0, keepdims=True)
        dpo = (dco * ps_ref[...]).astype(BF16)
        dpw_ref[...] += _dot(pb, dpo, 0, 0)
        dpooled = _dot(dpo, pw, 1, 1)
        dpm = dpooled / cnt
        head = jnp.where(blk % nbs == nbs - 1, 0.0, head_ref[...])
        dz = _pool_bwd(dpooled, dpm, head, grp)
        head_ref[...] = dpm[:HALO, :]

        dproj_ref[:, :POOL_W] = dz.astype(BF16)
        dproj_ref[:, 256:640] = dql.astype(BF16)
        dproj_ref[:, 640:896] = dkvl.astype(BF16)
        dproj_ref[:, 896:1024] = dkr_ref[...].astype(BF16)
        dh = _dot(dproj_ref[...], win_ref[...].reshape(D, D), 1, 1)
        dxn, dgam = _rms_bwd(xv, r, gam, dh)
        dgam_ref[...] += dgam
        dx2_ref[...] = dx3_ref[...] + dxn

    def rev(w):
        return pl.BlockSpec((TB, w), lambda i: (nb - 1 - i, 0))

    ptail = pl.BlockSpec((HALO, D), lambda i: (jnp.maximum((nb - 1 - i) * (TB // HALO) - 1, 0), 0))
    return pl.pallas_call(
        body, name="odd_pre_bwd",
        out_shape=[jax.ShapeDtypeStruct((t, D), F32), jax.ShapeDtypeStruct((t, D), BF16),
                   jax.ShapeDtypeStruct((t, D), BF16), jax.ShapeDtypeStruct((t, Q_LORA), BF16),
                   jax.ShapeDtypeStruct((t, KV_LORA), BF16), jax.ShapeDtypeStruct((1, D), F32),
                   jax.ShapeDtypeStruct((1, Q_LORA), F32), jax.ShapeDtypeStruct((1, KV_LORA), F32),
                   jax.ShapeDtypeStruct((POOL_W, POOL_W), F32), jax.ShapeDtypeStruct((1, POOL_W), F32)],
        grid=(nb,),
        in_specs=[rev(D), rev(D), ptail, rev(D), rev(POOL_W), rev(HEADS * HP), rev(HEADS * HP), rev(128),
                  _const_spec((1, D)), _wspec(N_SQ, OFF_OIN), _const_spec((HEADS * HP, Q_LORA)),
                  _const_spec((HEADS * HP, KV_LORA)), _const_spec((1, Q_LORA)), _const_spec((1, KV_LORA)),
                  _const_spec((POOL_W, POOL_W)), _const_spec((1, POOL_W))],
        out_specs=[rev(D), rev(D), rev(D), rev(Q_LORA), rev(KV_LORA), _const_spec((1, D)), _const_spec((1, Q_LORA)),
                   _const_spec((1, KV_LORA)), _const_spec((POOL_W, POOL_W)), _const_spec((1, POOL_W))],
        scratch_shapes=[pltpu.VMEM((HALO, POOL_W), F32)],
        compiler_params=_cparams(1),
    )(x, proj, proj, dx3, dmix, dq, dkv, dkr, gamma, wg, qbt, kvbt, qa_g, kva_g, pw_bd, pscale)


def _attn_specs(seq):
    head = pl.BlockSpec((seq, HP), lambda b, h: (b, h))
    shared = pl.BlockSpec((seq, 128), lambda b, h: (b, 0))
    gain = pl.BlockSpec((1, HP), lambda b, h: (0, 0))
    return head, shared, gain


def _causal_bias(n):
    rows = lax.broadcasted_iota(jnp.int32, (n, n), 0)
    cols = lax.broadcasted_iota(jnp.int32, (n, n), 1)
    return jnp.where(cols <= rows, 0.0, NEG_INF)


def _attn_fwd(q, kv, kr, cos, sin, gq, gk, seq, comm=None):
    t = q.shape[0]
    qb = min(512, seq)

    def body(q_ref, kv_ref, kr_ref, c_ref, s_ref, gq_ref, gk_ref, o_ref, lse_ref):
        c, s = c_ref[...], s_ref[...]
        qf, _ = _qk_prep(q_ref[...].astype(F32), gq_ref[...], c, s)
        kin = jnp.concatenate([kv_ref[:, :128].astype(F32), kr_ref[...]], axis=1)
        kf, _ = _qk_prep(kin, gk_ref[...], c, s)
        qf, kf = qf.astype(BF16), kf.astype(BF16)
        v1 = jnp.concatenate([kv_ref[:, 128:], jnp.ones((seq, V_DIM), BF16)], axis=1)
        bias = _causal_bias(qb)
        for q0 in range(0, seq, qb):
            q1 = q0 + qb
            qblk = qf[q0:q1]
            s_dg = _dot(qblk, kf[q0:q1], 1, 1) + bias
            m = jnp.max(s_dg, axis=-1, keepdims=True)
            if q0:
                s_off = _dot(qblk, kf[:q0], 1, 1)
                m = jnp.maximum(m, jnp.max(s_off, axis=-1, keepdims=True))
            acc = _dot(jnp.exp(s_dg - m).astype(BF16), v1[q0:q1], 1, 0)
            if q0:
                acc = acc + _dot(jnp.exp(s_off - m).astype(BF16), v1[:q0], 1, 0)
            l = acc[:, V_DIM:]
            o_ref[q0:q1, :] = (acc[:, :V_DIM] / l).astype(BF16)
            lse_ref[q0:q1, :] = m + jnp.log(l)

    head, shared, gain = _attn_specs(seq)
    per_head = pl.BlockSpec((seq, V_DIM), lambda b, h: (b, h))
    return _call(
        body, "attn_fwd", (t // seq, HEADS),
        [head, head, shared, shared, shared, gain, gain], [per_head, per_head],
        [jax.ShapeDtypeStruct((t, HEADS * V_DIM), BF16), jax.ShapeDtypeStruct((t, HEADS * V_DIM), F32)],
        (q, kv, kr, cos, sin, gq, gk), (), comm)


def _attn_bwd(q, kv, kr, cos, sin, gq, gk, dmix, d_out, lse, seq, comm=None):
    t = q.shape[0]
    qb = min(512, seq)

    def body(q_ref, kv_ref, kr_ref, c_ref, s_ref, gq_ref, gk_ref, do_ref, o_ref, lse_ref,
             dq_ref, dkv_ref, dkr_ref, dgq_ref, dgk_ref, dqf_ref, dkf_ref, dv_ref):
        b, hd = pl.program_id(0), pl.program_id(1)

        @pl.when((b == 0) & (hd == 0))
        def _():
            dgq_ref[...] = jnp.zeros_like(dgq_ref)
            dgk_ref[...] = jnp.zeros_like(dgk_ref)

        c, sn = c_ref[...], s_ref[...]
        gq_v, gk_v = gq_ref[...], gk_ref[...]
        qin = q_ref[...].astype(F32)
        kin = jnp.concatenate([kv_ref[:, :128].astype(F32), kr_ref[...]], axis=1)
        qf32, rq = _qk_prep(qin, gq_v, c, sn)
        kf32, rk = _qk_prep(kin, gk_v, c, sn)
        qf, kf = qf32.astype(BF16), kf32.astype(BF16)
        vb = kv_ref[:, 128:]
        dkf_ref[...] = jnp.zeros_like(dkf_ref)
        dv_ref[...] = jnp.zeros_like(dv_ref)
        bias = _causal_bias(qb)
        for q0 in range(0, seq, qb):
            q1 = q0 + qb
            qblk = qf[q0:q1]
            do = do_ref[q0:q1, :]
            lse_col = lse_ref[q0:q1, 0:1]
            d_col = jnp.sum(do.astype(F32) * o_ref[q0:q1, :].astype(F32), axis=-1, keepdims=True)
            dq_acc = None
            for k0, k1, diag in ((q0, q1, True), (0, q0, False)):
                if k1 == k0:
                    continue
                s = _dot(qblk, kf[k0:k1], 1, 1)
                p = jnp.exp((s + bias if diag else s) - lse_col)
                dv_ref[k0:k1, :] += _dot(p.astype(BF16), do, 0, 0)
                ds = (p * (_dot(do, vb[k0:k1], 1, 1) - d_col)).astype(BF16)
                part = _dot(ds, kf[k0:k1], 1, 0)
                dq_acc = part if dq_acc is None else dq_acc + part
                dkf_ref[k0:k1, :] += _dot(ds, qblk, 0, 0)
            dqf_ref[q0:q1, :] = dq_acc
        dqin, dgq = _qk_prep_bwd(dqf_ref[...], qin, rq, gq_v, c, sn)
        dkin, dgk = _qk_prep_bwd(dkf_ref[...], kin, rk, gk_v, c, sn)
        dgq_ref[...] += dgq
        dgk_ref[...] += dgk
        dq_ref[...] = dqin.astype(BF16)
        dkv_ref[:, :128] = dkin[:, :128].astype(BF16)
        dkv_ref[:, 128:] = dv_ref[...].astype(BF16)

        @pl.when(hd == 0)
        def _():
            dkr_ref[...] = dkin[:, 128:]

        @pl.when(hd != 0)
        def _():
            dkr_ref[...] += dkin[:, 128:]

    head, shared, gain = _attn_specs(seq)
    per_head = pl.BlockSpec((seq, V_DIM), lambda b, h: (b, h))
    return _call(
        body, "attn_bwd", (t // seq, HEADS),
        [head, head, shared, shared, shared, gain, gain,
         pl.BlockSpec((seq, V_DIM), lambda b, h: (b, 2 + h)), per_head, per_head],
        [head, head, shared, gain, gain],
        [jax.ShapeDtypeStruct((t, HEADS * HP), BF16), jax.ShapeDtypeStruct((t, HEADS * HP), BF16),
         jax.ShapeDtypeStruct((t, 128), F32), jax.ShapeDtypeStruct((1, HP), F32),
         jax.ShapeDtypeStruct((1, HP), F32)],
        (q, kv, kr, cos, sin, gq, gk, dmix, d_out, lse),
        [pltpu.VMEM((seq, HP), F32), pltpu.VMEM((seq, HP), F32), pltpu.VMEM((seq, V_DIM), F32)], comm)


def _tn(a_list, b, tm, name, into=None, comm=None):
    t, n_out = b.shape
    widths = [a.shape[1] for a in a_list]
    tk = min(TK_DW, t)
    m, na, nk = sum(widths), len(a_list), t // tk
    assert na == 1 or tm == m

    def body(*refs):
        a_refs, b_ref, o_ref, acc_ref = refs[:na], refs[na], refs[-2], refs[-1]
        k = pl.program_id(1)

        @pl.when(k == 0)
        def _():
            acc_ref[...] = jnp.zeros_like(acc_ref)

        bb = b_ref[...].astype(BF16)
        m0 = 0
        for a_ref, w in zip(a_refs, widths):
            rows = slice(0, tm) if na == 1 else slice(m0, m0 + w)
            acc_ref[rows, :] += _dot(a_ref[...].astype(BF16), bb, 0, 0)
            m0 += w

        @pl.when(k == nk - 1)
        def _():
            o_ref[...] = acc_ref[...].astype(BF16).reshape(o_ref.shape)

    if na == 1:
        in_specs = [pl.BlockSpec((tk, tm), lambda i, k: (k, i))]
    else:
        in_specs = [pl.BlockSpec((tk, w), lambda i, k: (k, 0)) for w in widths]
    in_specs.append(pl.BlockSpec((tk, n_out), lambda i, k: (k, 0)))
    args = list(a_list) + [b]
    if into is None:
        out_spec = pl.BlockSpec((tm, n_out), lambda i, k: (i, 0))
        out_shape = jax.ShapeDtypeStruct((m, n_out), BF16)
        aliases = {}
    else:
        buf, n, off = into
        assert n_out == D and tm % n == 0 and off % n == 0 and (na == 1 or tm // n == N_DEV)
        idx = off // n
        out_spec = pl.BlockSpec((tm // n, n, D), lambda i, k: (i, idx, 0))
        out_shape = jax.ShapeDtypeStruct(buf.shape, BF16)
        in_specs.append(pl.BlockSpec(memory_space=pl.ANY))
        args.append(buf)
        aliases = {len(args) - 1: 0}
    (res,), extra = _call(body, name, (m // tm, nk), in_specs, [out_spec], [out_shape], args,
                          [pltpu.VMEM((tm, n_out), F32)], comm, aliases)
    return (res, extra) if comm is not None else res


def _adamw(ws, gs, ms, vs, name, nblk=1):
    n = len(ws)
    c1 = 1.0 - B1 ** STEP
    c2 = 1.0 - B2 ** STEP

    def body(*refs):
        for a in range(n):
            w, g, m, v = (refs[k * n + a][...] for k in range(4))
            d_ref, m_ref, v_ref = (refs[(4 + k) * n + a] for k in range(3))
            m_new = B1 * m + (1.0 - B1) * g
            v_new = B2 * v + (1.0 - B2) * (g * g)
            d_ref[...] = -LR * ((m_new / c1) / (jnp.sqrt(v_new / c2) + ADAM_EPS) + WD * w)
            m_ref[...] = m_new
            v_ref[...] = v_new

    grid = (nblk,)
    assert all(w.shape[0] % nblk == 0 and (nblk == 1 or (w.shape[0] // nblk) % 8 == 0) for w in ws)
    specs = [pl.BlockSpec((w.shape[0] // nblk, w.shape[1]), lambda i: (i, 0)) for w in ws]
    outs, _ = _call(body, name, grid, specs * 4, specs * 3, [jax.ShapeDtypeStruct(w.shape, F32) for w in ws] * 3,
                    (*ws, *gs, *ms, *vs))
    return outs[:n], outs[n:2 * n], outs[2 * n:]


def _rows1024(a, rows):
    flat = a.reshape(-1, D)
    return jnp.pad(flat, ((0, rows - flat.shape[0]), (0, 0)))


def _pack_shards(even_w_in, even_w_out, odd_w_in, q_b, kv_b, odd_w_out, ffn_w_gate, ffn_w_up, ffn_w_down):
    mix0 = jnp.concatenate([even_w_in[0].T, jnp.zeros((OFF_EOUT - N_EIN, D), F32), even_w_out[0]], axis=0)
    gu = [jnp.concatenate([ffn_w_gate[layer].T, ffn_w_up[layer].T], axis=0) for layer in range(2)]
    mix1 = jnp.concatenate([jnp.pad(odd_w_in[0], ((0, 0), (0, D - ODD_IN))), odd_w_out[0],
                            _rows1024(q_b[0].T, N_QB), _rows1024(kv_b[0].T, N_KVB),
                            jnp.zeros((R_MIX1 - OFF_KVB - N_KVB, D), F32)], axis=0)
    return [c.astype(BF16) for c in (mix0, gu[0], ffn_w_down[0], mix1, gu[1], ffn_w_down[1])]


def _pad_heads(a):
    k = a.shape[1]
    return jnp.pad(a.reshape(HEADS, QK_DIM, k), ((0, 0), (0, HP - QK_DIM), (0, 0))).reshape(HEADS * HP, k)


def _small_pack(parts):
    flat = []
    for p in parts:
        v = p.reshape(-1)
        flat.append(jnp.pad(v, (0, (-v.shape[0]) % 1024)))
    return jnp.concatenate(flat).reshape(-1, 128)


def _small_unpack(buf, shapes):
    flat = buf.reshape(-1)
    out, off = [], 0
    for s in shapes:
        size = int(np.prod(s))
        out.append(flat[off:off + size].reshape(s))
        off += size + (-size) % 1024
    return out


def _step(x3d, positions, target3d, chunks, tile, where, mix_norm, ffn_norm, sg_ln_g, sg_w_s, sg_b_s,
          pool_w, q_norm, k_norm):
    bsz, seq, _ = x3d.shape
    t = bsz * seq
    x0 = x3d.reshape(t, D)
    target = target3d.reshape(t, D)
    my_mix0, my_gu0, my_d0, my_mix1, my_gu1, my_d1 = chunks

    lane = np.arange(128)
    inv_freq = np.where(lane < QK_ROPE, ROPE_THETA ** (-(2.0 * (lane % 32)) / QK_ROPE), 0.0)
    inv_freq = jnp.asarray(inv_freq.reshape(1, 128), F32)
    (cos, sin), (w_mix0, tiles) = _rope_tables(positions.reshape(t, 1), inv_freq, _gather_comm([my_mix0, tile]))

    conv_w = tiles[:, 0:3, 0:64].transpose(1, 0, 2).reshape(3, SC_W)
    pool_scale = tiles[:, 3, 0:32].reshape(1, POOL_W)
    q_a_norm = tiles[:, 4, 0:48].reshape(1, Q_LORA)
    kv_a_norm = tiles[:, 5, 0:32].reshape(1, KV_LORA)
    ws = sg_w_s[0]
    bst = jnp.pad(sg_b_s[0].T, ((0, 0), (0, 128 - SG_HEADS)))
    cw = jnp.pad(conv_w, ((0, 8 - 3), (0, 0)))
    pw_bd = jax.scipy.linalg.block_diag(*[pool_w[0, g] for g in range(4)]).astype(BF16)
    gq = jnp.pad(q_norm * ATT_SCALE, ((0, 0), (0, HP - QK_DIM)))
    gk = jnp.pad(k_norm, ((0, 0), (0, HP - QK_DIM)))

    (x1, proj_e), (w_gu0,) = _even_fwd(x0, w_mix0, mix_norm[0:1], sg_ln_g, ws, bst, cw, seq, _gather_comm([my_gu0]))
    (g0, u0, act0), (w_d0, w_mix1) = _ffn_up(x1, w_gu0, ffn_norm[0:1], "ffn_up0", _gather_comm([my_d0, my_mix1]))
    (x2,), (w_d1,) = _ffn_down(x1, act0, w_d0, "ffn_down0", _gather_comm([my_d1]))
    qbt = _pad_heads(w_mix1[:, OFF_QB:OFF_QB + N_QB_USED, :].reshape(HEADS * QK_DIM, Q_LORA))
    kvbt = w_mix1[:, OFF_KVB:OFF_KVB + N_KVB, :].reshape(HEADS * HP, KV_LORA)
    (proj_o, q, kv, kr, c_out), _ = _odd_pre_fwd(x2, w_mix1, mix_norm[1:2], qbt, kvbt, q_a_norm, kv_a_norm,
                                                pw_bd, pool_scale, seq)
    (d_out, lse), (w_gu1,) = _attn_fwd(q, kv, kr, cos, sin, gq, gk, seq, _gather_comm([my_gu1]))
    x3, dy, g1, u1, loss_tile = _last_block_fwd(x2, c_out, d_out, w_mix1, w_gu1, w_d1, ffn_norm[1:2], target)

    def chunk(rows, padded=False):
        return jnp.zeros((N_DEV, rows, D), BF16) if padded else lax.empty((N_DEV, rows, D), BF16)

    (dx3, act1, dg1, du1, h3, dgam_f1, dmix_o), _ = _ffn_bwd(x3, g1, u1, dy, w_gu1, w_d1, ffn_norm[1:2], "ffn_bwd1",
                                                           None, w_mix1)
    gp_ffn1 = _tn([dg1], h3, 1408, "dw_gate1", (chunk(R_GU + N_FF), N_FF, OFF_GATE))
    gp_ffn1 = _tn([du1], h3, 1408, "dw_up1", (gp_ffn1, N_FF, OFF_UP))
    gp_ffn1 = _tn([act1], dy, 1408, "dw_down1", (gp_ffn1, N_FF, R_GU))

    gp_mix1, (ga_ffn1,) = _tn([c_out, d_out], dx3, D, "dw_oout", (chunk(R_MIX1, True), N_SQ, OFF_OOUT),
                              _pair_exchange_comm(gp_ffn1))
    pb_ffn1 = _rs_pair_sum(gp_ffn1, ga_ffn1, where, "rs_pair_sum_ffn1")
    (dq, dkv, dkr, dgq, dgk), (gb_ffn1,) = _attn_bwd(q, kv, kr, cos, sin, gq, gk, dmix_o, d_out, lse, seq,
                                                    _chip_exchange_comm(pb_ffn1))
    (dx2, dproj_o, h2, qn, kvn, dgam_m1, dqa, dkva, dpw_bd, dps) = _odd_pre_bwd(
        x2, proj_o, dx3, dmix_o, dq, dkv, dkr, w_mix1, mix_norm[1:2], qbt, kvbt, q_a_norm, kv_a_norm, pw_bd,
        pool_scale, seq)
    gp_mix1 = _tn([h2], dproj_o, D, "dw_oin", (gp_mix1, N_SQ, OFF_OIN))
    d_qbt = _tn([dq], qn, HEADS * HP, "dw_qb")
    d_qb_rows = d_qbt.reshape(HEADS, HP, Q_LORA)[:, :QK_DIM].reshape(N_DEV, N_QB_USED, D)
    d_kvb_rows = _tn([dkv], kvn, HEADS * HP, "dw_kvb").reshape(N_DEV, N_KVB, D)
    gp_mix1 = lax.dynamic_update_slice(gp_mix1, d_qb_rows, (0, OFF_QB, 0))
    gp_mix1 = lax.dynamic_update_slice(gp_mix1, d_kvb_rows, (0, OFF_KVB, 0))

    (dx1, act0, dg0, du0, h1, dgam_f0), (ga_mix1,) = _ffn_bwd(x1, g0, u0, dx2, w_gu0, w_d0, ffn_norm[0:1], "ffn_bwd0",
                                                             _pair_exchange_comm(gp_mix1))
    pb_mix1 = _rs_pair_sum(gp_mix1, ga_mix1, where, "rs_pair_sum_mix1")
    gp_ffn0a, (gb_mix1,) = _tn([dg0], h1, 1408, "dw_gate0", (chunk(R_GU), N_FF, OFF_GATE),
                               _chip_exchange_comm(pb_mix1))
    gp_ffn0a = _tn([du0], h1, 1408, "dw_up0", (gp_ffn0a, N_FF, OFF_UP))
    gp_ffn0b, (ga_ffn0a,) = _tn([act0], dx2, 1408, "dw_down0", (chunk(N_FF), N_FF, 0),
                                _pair_exchange_comm(gp_ffn0a))
    pb_ffn0a = _rs_pair_sum(gp_ffn0a, ga_ffn0a, where, "rs_pair_sum_ffn0a")

    (dx0, dproj_e, mix_e, h0, dgam_m0, dws, dbc, dlng, dcw), (gb_ffn0a, ga_ffn0b) = _even_bwd(
        x0, proj_e, dx1, w_mix0, mix_norm[0:1], sg_ln_g, ws, bst, cw, seq,
        _both(_chip_exchange_comm(pb_ffn0a), _pair_exchange_comm(gp_ffn0b)))
    pb_ffn0b = _rs_pair_sum(gp_ffn0b, ga_ffn0b, where, "rs_pair_sum_ffn0b")

    small = _small_pack([
        jnp.concatenate([dgam_m0, dgam_m1], 0), jnp.concatenate([dgam_f0, dgam_f1], 0), dlng,
        dws[None], dbc[:, :SG_HEADS].T[None], dcw[:3],
        jnp.stack([dpw_bd[g * POOL_GD:(g + 1) * POOL_GD, g * POOL_GD:(g + 1) * POOL_GD] for g in range(4)])[None],
        dps, dqa, dkva, dgq[:, :QK_DIM] * ATT_SCALE, dgk[:, :QK_DIM], loss_tile[0:1, 0:1]])
    gp_mix0, (small_all,) = _tn([mix_e], dx1, D, "dw_eout", (chunk(R_MIX0, True), N_SQ, OFF_EOUT),
                                _gather_comm([small]))
    gp_mix0, (gb_ffn0b,) = _tn([dproj_e], h0, 1280, "dw_ein", (gp_mix0, N_EIN, OFF_EIN),
                               _chip_exchange_comm(pb_ffn0b))
    small_sum = _small_unpack(_sum_gathered(small_all), SMALL_SHAPES)
    partials = ([pb_ffn0a, pb_ffn0b, pb_mix1, pb_ffn1], [gb_ffn0a, gb_ffn0b, gb_mix1, gb_ffn1])
    return dx0.reshape(bsz, seq, D), partials, gp_mix0, small_sum


SMALL_SHAPES = [(2, D), (2, D), (1, SG_W), (1, SG_HEADS, 128, 128), (1, SG_HEADS, 128), (3, SC_W),
                (1, 4, POOL_GD, POOL_GD), (1, POOL_W), (1, Q_LORA), (1, KV_LORA), (1, QK_DIM), (1, QK_DIM), (1, 1)]


def kernel(x, positions, mix_norm, ffn_norm, even_w_in, sg_ln_g, sg_w_s, sg_b_s, sc_conv_w, even_w_out, odd_w_in, pool_w, pool_scale, q_a_norm, q_b, kv_a_norm, kv_b, q_norm, k_norm, odd_w_out, ffn_w_gate, ffn_w_up, ffn_w_down, loss_target, m_mix_norm, m_ffn_norm, m_even_w_in, m_sg_ln_g, m_sg_w_s, m_sg_b_s, m_sc_conv_w, m_even_w_out, m_odd_w_in, m_pool_w, m_pool_scale, m_q_a_norm, m_q_b, m_kv_a_norm, m_kv_b, m_q_norm, m_k_norm, m_odd_w_out, m_ffn_w_gate, m_ffn_w_up, m_ffn_w_down, v_mix_norm, v_ffn_norm, v_even_w_in, v_sg_ln_g, v_sg_w_s, v_sg_b_s, v_sc_conv_w, v_even_w_out, v_odd_w_in, v_pool_w, v_pool_scale, v_q_a_norm, v_q_b, v_kv_a_norm, v_kv_b, v_q_norm, v_k_norm, v_odd_w_out, v_ffn_w_gate, v_ffn_w_up, v_ffn_w_down):
    xi, yi, ci = _place()
    me = 4 * xi + 2 * yi + ci

    chunks = _pack_shards(even_w_in, even_w_out, odd_w_in, q_b, kv_b, odd_w_out, ffn_w_gate, ffn_w_up, ffn_w_down)

    def lane_pad(a):
        return jnp.pad(a, ((0, 0), (0, 128 - a.shape[1])))

    tile = jnp.concatenate([lane_pad(sc_conv_w[0]), lane_pad(pool_scale), lane_pad(q_a_norm), lane_pad(kv_a_norm),
                            jnp.zeros((2, 128), F32)], axis=0)
    chip = 2 * xi + yi
    where = jnp.stack([ci, chip, chip ^ 2, chip ^ 1, chip ^ 3]).astype(jnp.int32)
    grad_x, (pbs, gbs), gp_mix0, tot = _step(
        x, positions, loss_target, chunks, tile, where, mix_norm, ffn_norm, sg_ln_g, sg_w_s, sg_b_s,
        pool_w, q_norm, k_norm)

    (ga_mix0,) = _comm_alone(_pair_exchange_comm(gp_mix0), "rs_pair_exchange_mix0")
    pb_mix0 = _rs_pair_sum(gp_mix0, ga_mix0, where, "rs_pair_sum_mix0")
    mix0_sems, pb_mix0, land_mix0, started = _chip_exchange_start(pb_mix0)
    gsh_ffn0a, gsh_ffn0b, gsh_mix1, gsh_ffn1 = _rs_final_sums(pbs, gbs, "rs_final_sums", started)

    (g_mix, g_ffn, g_lng, g_ws, g_bs, g_cw_full, g_pw, g_ps_full, g_qa_full, g_kva_full, g_qn, g_kn, loss) = tot
    g_cw = lax.dynamic_slice_in_dim(g_cw_full, me * 64, 64, axis=1)[None]
    g_ps = lax.dynamic_slice_in_dim(g_ps_full, me * 32, 32, axis=1)
    g_qa = lax.dynamic_slice_in_dim(g_qa_full, me * 48, 48, axis=1)
    g_kva = lax.dynamic_slice_in_dim(g_kva_full, me * 32, 32, axis=1)

    def tr(a):
        return jnp.swapaxes(a, -1, -2)

    g_gate = tr(jnp.stack([gsh_ffn0a[OFF_GATE:OFF_GATE + N_FF], gsh_ffn1[OFF_GATE:OFF_GATE + N_FF]]))
    g_up = tr(jnp.stack([gsh_ffn0a[OFF_UP:OFF_UP + N_FF], gsh_ffn1[OFF_UP:OFF_UP + N_FF]]))
    g_down = jnp.stack([gsh_ffn0b, gsh_ffn1[R_GU:R_GU + N_FF]])
    g_oin = gsh_mix1[OFF_OIN:OFF_OIN + N_SQ, :ODD_IN][None]
    g_oout = gsh_mix1[OFF_OOUT:OFF_OOUT + N_SQ][None]
    g_qb = tr(gsh_mix1[OFF_QB:OFF_QB + N_QB_USED].reshape(1, 144, Q_LORA))
    g_kvb = tr(gsh_mix1[OFF_KVB:OFF_KVB + N_KVB].reshape(1, 192, KV_LORA))
    transposed = ("even_w_in", "odd_w_in", "q_b", "kv_b", "ffn_w_gate", "ffn_w_up")

    names = ("mix_norm", "ffn_norm", "even_w_in", "sg_ln_g", "sg_w_s", "sg_b_s", "sc_conv_w", "even_w_out",
             "odd_w_in", "pool_w", "pool_scale", "q_a_norm", "q_b", "kv_a_norm", "kv_b", "q_norm", "k_norm",
             "odd_w_out", "ffn_w_gate", "ffn_w_up", "ffn_w_down")
    grads = dict(mix_norm=g_mix, ffn_norm=g_ffn, sg_ln_g=g_lng, sg_w_s=g_ws, sg_b_s=g_bs,
                 sc_conv_w=g_cw, odd_w_in=g_oin, pool_w=g_pw, pool_scale=g_ps, q_a_norm=g_qa,
                 q_b=g_qb, kv_a_norm=g_kva, kv_b=g_kvb, q_norm=g_qn, k_norm=g_kn, odd_w_out=g_oout,
                 ffn_w_gate=g_gate, ffn_w_up=g_up, ffn_w_down=g_down)
    weights = dict(mix_norm=mix_norm, ffn_norm=ffn_norm, even_w_in=even_w_in, sg_ln_g=sg_ln_g, sg_w_s=sg_w_s,
                   sg_b_s=sg_b_s, sc_conv_w=sc_conv_w, even_w_out=even_w_out, odd_w_in=odd_w_in, pool_w=pool_w,
                   pool_scale=pool_scale, q_a_norm=q_a_norm, q_b=q_b, kv_a_norm=kv_a_norm, kv_b=kv_b, q_norm=q_norm,
                   k_norm=k_norm, odd_w_out=odd_w_out, ffn_w_gate=ffn_w_gate, ffn_w_up=ffn_w_up,
                   ffn_w_down=ffn_w_down)
    m_in = dict(mix_norm=m_mix_norm, ffn_norm=m_ffn_norm, even_w_in=m_even_w_in, sg_ln_g=m_sg_ln_g, sg_w_s=m_sg_w_s,
                sg_b_s=m_sg_b_s, sc_conv_w=m_sc_conv_w, even_w_out=m_even_w_out, odd_w_in=m_odd_w_in,
                pool_w=m_pool_w, pool_scale=m_pool_scale, q_a_norm=m_q_a_norm, q_b=m_q_b, kv_a_norm=m_kv_a_norm,
                kv_b=m_kv_b, q_norm=m_q_norm, k_norm=m_k_norm, odd_w_out=m_odd_w_out, ffn_w_gate=m_ffn_w_gate,
                ffn_w_up=m_ffn_w_up, ffn_w_down=m_ffn_w_down)
    v_in = dict(mix_norm=v_mix_norm, ffn_norm=v_ffn_norm, even_w_in=v_even_w_in, sg_ln_g=v_sg_ln_g, sg_w_s=v_sg_w_s,
                sg_b_s=v_sg_b_s, sc_conv_w=v_sc_conv_w, even_w_out=v_even_w_out, odd_w_in=v_odd_w_in,
                pool_w=v_pool_w, pool_scale=v_pool_scale, q_a_norm=v_q_a_norm, q_b=v_q_b, kv_a_norm=v_kv_a_norm,
                kv_b=v_kv_b, q_norm=v_q_norm, k_norm=v_k_norm, odd_w_out=v_odd_w_out, ffn_w_gate=v_ffn_w_gate,
                ffn_w_up=v_ffn_w_up, ffn_w_down=v_ffn_w_down)
    delta, new_m, new_v = {}, {}, {}

    def as2d(k, a):
        a = tr(a) if k in transposed else a
        return a.reshape(-1, a.shape[-1])

    def back(k, a):
        shape = weights[k].shape
        return tr(a.reshape(shape[:-2] + (shape[-1], shape[-2]))) if k in transposed else a.reshape(shape)

    def update(group, name, nblk=1):
        outs = _adamw([as2d(k, weights[k]) for k in group], [as2d(k, grads[k]) for k in group],
                      [as2d(k, m_in[k]) for k in group], [as2d(k, v_in[k]) for k in group], name, nblk)
        for i, k in enumerate(group):
            delta[k], new_m[k], new_v[k] = (back(k, o[i]) for o in outs)

    update(["ffn_w_gate", "ffn_w_up", "ffn_w_down"], "adamw_ffn", 4)
    update(["odd_w_in", "odd_w_out"], "adamw_mix1", 2)
    update([k for k in names if k not in delta and k not in ("even_w_in", "even_w_out")], "adamw_small")

    pb_mix0, gb_mix0 = _chip_exchange_wait(mix0_sems, pb_mix0, land_mix0, new_v["k_norm"])
    (gsh_mix0,) = _rs_final_sums([pb_mix0], [gb_mix0], "rs_final_sum_mix0")
    grads["even_w_in"] = tr(gsh_mix0[OFF_EIN:OFF_EIN + N_EIN][None])
    grads["even_w_out"] = gsh_mix0[OFF_EOUT:OFF_EOUT + N_SQ][None]
    update(["even_w_in", "even_w_out"], "adamw_mix0", 2)

    return (loss.reshape(()), grad_x, *[grads[k] for k in names], *[delta[k] for k in names],
            *[new_m[k] for k in names], *[new_v[k] for k in names])
```

```python
import functools

import numpy as np
import jax
import jax.numpy as jnp
from jax import lax
from jax.experimental import pallas as pl
from jax.experimental.pallas import tpu as pltpu

F32 = jnp.float32
BF16 = jnp.bfloat16
MESH = pl.DeviceIdType.MESH

D = 1024
EPS = 1e-6
NEG_INF = -1e30
SG_HEADS, SG_HD, SG_W, SG_CHUNK = 4, 128, 512, 128
SC_W = 512
EVEN_IN = 2560
POOL_W = 256
POOL_GD = 64
Q_LORA, KV_LORA, QK_ROPE, QK_NOPE, V_DIM = 384, 256, 64, 128, 128
QK_DIM = QK_NOPE + QK_ROPE
HEADS = 6
HP = 256
ODD_IN = 960
D_FF = 2816
ROPE_THETA = 10000.0
ATT_SCALE = QK_DIM ** -0.5
LR, B1, B2, ADAM_EPS, WD, STEP = 0.001, 0.9, 0.999, 1e-08, 0.01, 10

N_DEV = 8
TB = 512
TB_FFN_BWD = 256
TK_DW = 1024
HALO = 16
VMEM_LIMIT = 56 * 1024 * 1024

N_EIN, N_FF, N_SQ = 320, 352, 128
OFF_EIN, OFF_EOUT, R_MIX0 = 0, 384, 512
OFF_GATE, OFF_UP, R_GU = 0, 352, 704
OFF_OIN, OFF_OOUT, OFF_QB, OFF_KVB, R_MIX1 = 0, 128, 256, 320, 384
N_QB, N_QB_USED, N_KVB = 64, 54, 48

INV_SQRT2 = 0.7071067811865476
INV_SQRT_2PI = 0.3989422804014327


def _dot(a, b, ca, cb):
    return lax.dot_general(a, b, (((ca,), (cb,)), ((), ())), preferred_element_type=F32)


def _cparams(n_axes=1):
    return pltpu.CompilerParams(dimension_semantics=("arbitrary",) * n_axes, vmem_limit_bytes=VMEM_LIMIT)


def _wspec(n, off):
    assert off % n == 0
    idx = off // n
    return pl.BlockSpec((N_DEV, n, D), lambda i: (0, idx, 0), pipeline_mode=pl.Buffered(1))


def _const_spec(shape):
    zeros = (0,) * len(shape)
    return pl.BlockSpec(shape, lambda *_: zeros)


class _Comm:
    def __init__(self, ins, out_shapes, sems, start, wait, mid=None):
        self.ins, self.out_shapes, self.sems, self.start, self.wait, self.mid = ins, out_shapes, sems, start, wait, mid


def _both(c1, c2):
    def split(f1, f2):
        def run(ins, outs, sems):
            f1(ins[:len(c1.ins)], outs[:len(c1.out_shapes)], sems[:len(c1.sems)])
            f2(ins[len(c1.ins):], outs[len(c1.out_shapes):], sems[len(c1.sems):])
        return run

    def nothing(ins, outs, sems):
        pass

    mid = None if c1.mid is None and c2.mid is None else split(c1.mid or nothing, c2.mid or nothing)
    return _Comm(c1.ins + c2.ins, c1.out_shapes + c2.out_shapes, c1.sems + c2.sems,
                 split(c1.start, c2.start), split(c1.wait, c2.wait), mid)


def _call(body, name, grid, in_specs, out_specs, out_shape, args, scratch_shapes=(), comm=None, aliases=None):
    n_axes = len(grid)
    aliases = aliases or {}
    if comm is None:
        res = pl.pallas_call(
            body, name=name, grid=grid, in_specs=list(in_specs), out_specs=list(out_specs),
            out_shape=list(out_shape), scratch_shapes=list(scratch_shapes), input_output_aliases=aliases,
            compiler_params=_cparams(n_axes))(*args)
        return list(res), []
    ni, no, ns = len(in_specs), len(out_specs), len(scratch_shapes)
    ci, co = len(comm.ins), len(comm.out_shapes)
    n_steps = int(np.prod(grid))

    def carrier(*refs):
        ins, cin = refs[:ni], refs[ni:ni + ci]
        outs, cout = refs[ni + ci:ni + ci + no], refs[ni + ci + no:ni + ci + no + co]
        scr, sems = refs[ni + ci + no + co:ni + ci + no + co + ns], refs[ni + ci + no + co + ns:]
        step = 0
        for a in range(n_axes):
            step = step * grid[a] + pl.program_id(a)

        @pl.when(step == 0)
        def _():
            comm.start(cin, cout, sems)

        body(*ins, *outs, *scr)

        if comm.mid is not None and n_steps >= 4:
            @pl.when(step == n_steps // 2)
            def _():
                comm.mid(cin, cout, sems)

        @pl.when(step == n_steps - 1)
        def _():
            if comm.mid is not None and n_steps < 4:
                comm.mid(cin, cout, sems)
            comm.wait(cin, cout, sems)

    any_spec = pl.BlockSpec(memory_space=pl.ANY)
    res = pl.pallas_call(
        carrier, name=name, grid=grid, in_specs=list(in_specs) + [any_spec] * ci,
        out_specs=list(out_specs) + [any_spec] * co, out_shape=list(out_shape) + list(comm.out_shapes),
        scratch_shapes=list(scratch_shapes) + list(comm.sems), input_output_aliases=aliases,
        compiler_params=_cparams(n_axes))(*args, *comm.ins)
    return list(res[:no]), list(res[no:])


def _comm_alone(comm, name):
    ci, co = len(comm.ins), len(comm.out_shapes)

    def body(*refs):
        cin, cout, sems = refs[:ci], refs[ci:ci + co], refs[ci + co:]
        comm.start(cin, cout, sems)
        if comm.mid is not None:
            comm.mid(cin, cout, sems)
        comm.wait(cin, cout, sems)

    any_spec = pl.BlockSpec(memory_space=pl.ANY)
    res = pl.pallas_call(
        body, name=name, out_shape=list(comm.out_shapes), in_specs=[any_spec] * ci, out_specs=[any_spec] * co,
        scratch_shapes=list(comm.sems))(*comm.ins)
    return list(res)


def _rms(x, g):
    r = lax.rsqrt(jnp.mean(x * x, axis=-1, keepdims=True) + EPS)
    return x * r * g, r


def _rms_bwd(x, r, g, dy):
    xh = x * r
    dxh = dy * g
    dx = r * (dxh - xh * jnp.mean(dxh * xh, axis=-1, keepdims=True))
    dg = jnp.sum(dy * xh, axis=0, keepdims=True)
    return dx, dg


def _gelu(x):
    return 0.5 * x * (1.0 + lax.erf(x * INV_SQRT2))


def _gelu_grad(x):
    return 0.5 * (1.0 + lax.erf(x * INV_SQRT2)) + x * jnp.exp(-0.5 * x * x) * INV_SQRT_2PI


def _shift_down(a, k):
    rows = lax.broadcasted_iota(jnp.int32, a.shape, 0)
    return jnp.where(rows >= k, pltpu.roll(a, k, 0), 0.0)


def _shift_up(a, k):
    n = a.shape[0]
    rows = lax.broadcasted_iota(jnp.int32, a.shape, 0)
    return jnp.where(rows < n - k, pltpu.roll(a, n - k, 0), 0.0)


def _tril_bf16(w):
    r = lax.broadcasted_iota(jnp.int32, w.shape, 0)
    c = lax.broadcasted_iota(jnp.int32, w.shape, 1)
    return jnp.where(r >= c, w, 0.0).astype(BF16)


def _ln_head(vh, g):
    mu = jnp.mean(vh, axis=-1, keepdims=True)
    xc = vh - mu
    rr = lax.rsqrt(jnp.mean(xc * xc, axis=-1, keepdims=True) + EPS)
    xh = xc * rr
    return xh * g, xh, rr


def _conv_fwd(z, tail, cw_ref):
    ext = jnp.concatenate([tail, z], axis=0)
    zs1 = _shift_down(ext, 1)[HALO:]
    zs2 = _shift_down(ext, 2)[HALO:]
    y = cw_ref[2:3, :] * z + cw_ref[1:2, :] * zs1 + cw_ref[0:1, :] * zs2
    return y, zs1, zs2


def _pool_cnt(shape, blk_in_seq):
    rows = lax.broadcasted_iota(jnp.int32, shape, 0)
    grp = lax.broadcasted_iota(jnp.int32, shape, 1) // POOL_GD
    win = jnp.where(grp == 0, 2, jnp.where(grp == 1, 4, jnp.where(grp == 2, 8, 16)))
    tpos = blk_in_seq * shape[0] + rows + 1
    return jnp.minimum(tpos, win).astype(F32), grp


def _pool_select(grp, s2, s4, s8, s16):
    return jnp.where(grp == 0, s2, jnp.where(grp == 1, s4, jnp.where(grp == 2, s8, s16)))


def _pool_fwd(z, tail, blk_in_seq):
    ext = jnp.concatenate([tail, z], axis=0)
    s2 = ext + _shift_down(ext, 1)
    s4 = s2 + _shift_down(s2, 2)
    s8 = s4 + _shift_down(s4, 4)
    s16 = s8 + _shift_down(s8, 8)
    cnt, grp = _pool_cnt(z.shape, blk_in_seq)
    sums = _pool_select(grp, s2[HALO:], s4[HALO:], s8[HALO:], s16[HALO:])
    return sums / cnt - z, cnt, grp


def _pool_bwd(dpooled, dpm, head, grp):
    n = dpm.shape[0]
    ext = jnp.concatenate([dpm, head], axis=0)
    u2 = ext + _shift_up(ext, 1)
    u4 = u2 + _shift_up(u2, 2)
    u8 = u4 + _shift_up(u4, 4)
    u16 = u8 + _shift_up(u8, 8)
    return _pool_select(grp, u2[:n], u4[:n], u8[:n], u16[:n]) - dpooled


def _lane_sums(a):
    return _dot(a.astype(BF16), jnp.ones((a.shape[1], a.shape[1]), BF16), 1, 0)


def _swap_halves(y1):
    src = lax.broadcasted_iota(jnp.int32, (128, 128), 0)
    dst = lax.broadcasted_iota(jnp.int32, (128, 128), 1)
    perm = jnp.where(((dst < 32) & (src == dst + 32)) | ((dst >= 32) & (dst < QK_ROPE) & (src == dst - 32)), 1.0, 0.0)
    return _dot(y1.astype(BF16), perm.astype(BF16), 1, 0)


def _rope(y1, c, s):
    return y1 * c + _swap_halves(y1) * s


def _rope_bwd(d1, c, s):
    return d1 * c + _swap_halves(d1 * s)


def _qk_prep(x, g, c, s):
    r = lax.rsqrt(_lane_sums(x * x) * (1.0 / QK_DIM) + EPS)
    y = x * r * g
    return jnp.concatenate([y[:, :128], _rope(y[:, 128:], c, s)], axis=1), r


def _qk_prep_bwd(dout, x, r, g, c, s):
    dy = jnp.concatenate([dout[:, :128], _rope_bwd(dout[:, 128:], c, s)], axis=1)
    xh = x * r
    dxh = dy * g
    dx = r * (dxh - xh * (_lane_sums(dxh * xh) * (1.0 / QK_DIM)))
    return dx, jnp.sum(dy * xh, axis=0, keepdims=True)


def _place():
    return lax.axis_index("x"), lax.axis_index("y"), lax.axis_index("c")


def _gather_comm(arrs):
    n = len(arrs)

    def halves(a):
        rows = arrs[a].shape[0]
        tile = 16 if arrs[a].dtype == BF16 else 8
        top = rows // 2 if rows % (2 * tile) == 0 else rows
        return (0, top), (top, rows - top)

    def plan(ins, outs, sems):
        send_sems, recv_sems, local_sems = sems
        x, y, c = _place()
        me, sib, xn, yn, dg = (x, y, c), (x, y, 1 - c), (1 - x, y, c), (x, 1 - y, c), (1 - x, 1 - y, c)

        def slot(a, dev, part=None):
            ref = outs[a].at[4 * dev[0] + 2 * dev[1] + dev[2]]
            return ref if part is None else ref.at[pl.ds(part[0], part[1])]

        def copy(a, k, block, to, src=None, part=None):
            return pltpu.make_async_remote_copy(
                src_ref=slot(a, block, part) if src is None else src, dst_ref=slot(a, block, part),
                send_sem=send_sems.at[a, k], recv_sem=recv_sems.at[a, k], device_id=to, device_id_type=MESH)

        local = [pltpu.make_async_copy(ins[a], slot(a, me), local_sems.at[a]) for a in range(n)]
        return me, sib, xn, yn, dg, copy, local

    def start(ins, outs, sems):
        me, sib, xn, yn, _, copy, local = plan(ins, outs, sems)
        for a in range(n):
            local[a].start()
            for k, to in enumerate((sib, xn, yn)):
                copy(a, k, me, to, src=ins[a]).start()

    def mid(ins, outs, sems):
        me, sib, xn, yn, _, copy, _ = plan(ins, outs, sems)
        for a in range(n):
            top, bottom = halves(a)
            copy(a, 1, xn, me).wait_recv()
            copy(a, 3, xn, yn, part=top).start()
            copy(a, 5, xn, sib).start()
            copy(a, 2, yn, me).wait_recv()
            if bottom[1]:
                copy(a, 4, yn, xn, part=bottom).start()
            copy(a, 6, yn, sib).start()

    def wait(ins, outs, sems):
        me, sib, xn, yn, dg, copy, local = plan(ins, outs, sems)
        other = lambda dev: (dev[0], dev[1], 1 - dev[2])
        for a in range(n):
            top, bottom = halves(a)
            copy(a, 3, dg, me, part=top).wait_recv()
            if bottom[1]:
                copy(a, 4, dg, me, part=bottom).wait_recv()
            copy(a, 7, dg, sib).start()
        for a in range(n):
            top, bottom = halves(a)
            for k, block in ((0, sib), (5, other(xn)), (6, other(yn)), (7, other(dg))):
                copy(a, k, block, me).wait_recv()
            for k, block in ((0, me), (1, me), (2, me), (5, xn), (6, yn), (7, dg)):
                copy(a, k, block, me, src=ins[a] if k < 3 else None).wait_send()
            copy(a, 3, xn, me, part=top).wait_send()
            if bottom[1]:
                copy(a, 4, yn, me, part=bottom).wait_send()
            local[a].wait()

    return _Comm(
        list(arrs), [jax.ShapeDtypeStruct((N_DEV,) + a.shape, a.dtype) for a in arrs],
        [pltpu.SemaphoreType.DMA((n, 8)), pltpu.SemaphoreType.DMA((n, 8)), pltpu.SemaphoreType.DMA((n,))],
        start, wait, mid)


def _sum_gathered(g):
    rows = g.shape[1]

    def body(g_ref, sum_ref):
        total = g_ref[0]
        for d in range(1, N_DEV):
            total = total + g_ref[d]
        sum_ref[...] = total

    return pl.pallas_call(
        body, name="sum_gathered_small", out_shape=jax.ShapeDtypeStruct((rows, 128), F32), grid=(1,),
        in_specs=[pl.BlockSpec((N_DEV, rows, 128), lambda i: (0, 0, 0))],
        out_specs=pl.BlockSpec((rows, 128), lambda i: (0, 0)), compiler_params=_cparams(1),
    )(g)


def _sum_rows(rows):
    return rows if rows <= 512 else rows // 2


def _pair_exchange_comm(gp):
    _, rows, cols = gp.shape

    def copies(ins, outs, sems):
        send_sems, recv_sems = sems
        x, y, c = _place()
        return [pltpu.make_async_remote_copy(
            src_ref=ins[0].at[2 * j + (1 - c)], dst_ref=outs[0].at[j], send_sem=send_sems.at[j],
            recv_sem=recv_sems.at[j], device_id=(x, y, 1 - c), device_id_type=MESH) for j in range(4)]

    def start(ins, outs, sems):
        for cp in copies(ins, outs, sems):
            cp.start()

    def wait(ins, outs, sems):
        for cp in copies(ins, outs, sems):
            cp.wait()

    return _Comm([gp], [jax.ShapeDtypeStruct((4, rows, cols), gp.dtype)],
                 [pltpu.SemaphoreType.DMA((4,)), pltpu.SemaphoreType.DMA((4,))], start, wait)


def _rs_pair_sum(gp, got, where, name):
    _, rows, cols = got.shape
    rb = _sum_rows(rows)
    gp4 = gp.reshape(4, 2, rows, cols)

    def body(w_ref, a_ref, b_ref, o_ref):
        o_ref[0] = (a_ref[0, 0].astype(F32) + b_ref[0].astype(F32)).astype(o_ref.dtype)

    return pl.pallas_call(
        body, name=name, out_shape=jax.ShapeDtypeStruct((4, rows, cols), gp.dtype),
        grid_spec=pltpu.PrefetchScalarGridSpec(
            num_scalar_prefetch=1, grid=(4, rows // rb),
            in_specs=[pl.BlockSpec((1, 1, rb, cols), lambda k, r, w: (w[1 + k], w[0], r, 0)),
                      pl.BlockSpec((1, rb, cols), lambda k, r, w: (w[1 + k], r, 0))],
            out_specs=pl.BlockSpec((1, rb, cols), lambda k, r, w: (k, r, 0))),
        compiler_params=_cparams(2),
    )(where, gp4, got)


def _chip_exchange_comm(pb):
    _, rows, cols = pb.shape

    def copies(ins, outs, sems):
        send_sems, recv_sems = sems
        x, y, c = _place()
        chips = [(1 - x, y), (x, 1 - y), (1 - x, 1 - y)]
        return [pltpu.make_async_remote_copy(
            src_ref=ins[0].at[1 + k], dst_ref=outs[0].at[k], send_sem=send_sems.at[k],
            recv_sem=recv_sems.at[k], device_id=(px, py, c), device_id_type=MESH)
            for k, (px, py) in enumerate(chips)]

    def start(ins, outs, sems):
        for cp in copies(ins, outs, sems):
            cp.start()

    def wait(ins, outs, sems):
        for cp in copies(ins, outs, sems):
            cp.wait()

    return _Comm([pb], [jax.ShapeDtypeStruct((3, rows, cols), pb.dtype)],
                 [pltpu.SemaphoreType.DMA((3,)), pltpu.SemaphoreType.DMA((3,))], start, wait)


def _chip_exchange_start(pb, tag):
    _, rows, cols = pb.shape

    def body(pb_ref, land_ref, *rest):
        sems, token = rest[:6], rest[8]
        x, y, c = _place()
        chips = [(1 - x, y), (x, 1 - y), (1 - x, 1 - y)]
        for k, (px, py) in enumerate(chips):
            pltpu.make_async_remote_copy(
                src_ref=pb_ref.at[1 + k], dst_ref=land_ref.at[k], send_sem=sems[k], recv_sem=sems[3 + k],
                device_id=(px, py, c), device_id_type=MESH).start()
        token[...] = jnp.zeros_like(token)

    hbm = pl.BlockSpec(memory_space=pltpu.HBM)
    sem = pl.BlockSpec(memory_space=pltpu.SEMAPHORE)
    land = lax.empty((3, rows, cols), pb.dtype)
    res = pl.pallas_call(
        body, name="rs_chip_exchange_start_" + tag,
        out_shape=(*[pltpu.SemaphoreType.DMA(())] * 6, pltpu.HBM(pb.shape, pb.dtype), pltpu.HBM(land.shape, land.dtype),
                   jax.ShapeDtypeStruct((8, 128), F32)),
        in_specs=(hbm, hbm), out_specs=(*[sem] * 6, hbm, hbm, pl.BlockSpec(memory_space=pltpu.VMEM)),
        input_output_aliases={0: 6, 1: 7},
        compiler_params=pltpu.CompilerParams(has_side_effects=pltpu.SideEffectType.DATAFLOW_SIDE_EFFECTING),
    )(pltpu.with_memory_space_constraint(pb, pltpu.HBM), pltpu.with_memory_space_constraint(land, pltpu.HBM))
    return list(res[:6]), res[6], res[7], res[8]


def _chip_exchange_wait(sems, pb_thru, land_thru, after, tag):
    def body(pb_ref, land_ref, *rest):
        sems_in = rest[:6]
        x, y, c = _place()
        chips = [(1 - x, y), (x, 1 - y), (1 - x, 1 - y)]
        for k, (px, py) in enumerate(chips):
            cp = pltpu.make_async_remote_copy(
                src_ref=pb_ref.at[1 + k], dst_ref=land_ref.at[k], send_sem=sems_in[k], recv_sem=sems_in[3 + k],
                device_id=(px, py, c), device_id_type=MESH)
            cp.wait_send()
            cp.wait_recv()

    hbm = pl.BlockSpec(memory_space=pltpu.HBM)
    sem = pl.BlockSpec(memory_space=pltpu.SEMAPHORE)
    res = pl.pallas_call(
        body, name="rs_chip_exchange_wait_" + tag,
        out_shape=(pltpu.HBM(pb_thru.shape, pb_thru.dtype), pltpu.HBM(land_thru.shape, land_thru.dtype)),
        in_specs=(hbm, hbm, *[sem] * 6, pl.BlockSpec(memory_space=pl.ANY)), out_specs=(hbm, hbm),
        input_output_aliases={0: 0, 1: 1},
        compiler_params=pltpu.CompilerParams(has_side_effects=pltpu.SideEffectType.DATAFLOW_SIDE_EFFECTING),
    )(pb_thru, land_thru, *sems, after)
    return res[0], res[1]


def _rs_final_sums(pbs, gots, name, after=None):
    n = len(pbs)

    def body(*refs):
        outs = refs[len(refs) - n:]
        for a in range(n):
            m_ref, g_ref, o_ref = refs[a], refs[n + a], outs[a]
            o_ref[...] = ((m_ref[0].astype(F32) + g_ref[0].astype(F32)) + g_ref[1].astype(F32)) + g_ref[2].astype(F32)

    half = [pb.shape[1] // 2 for pb in pbs]
    in_specs = ([pl.BlockSpec((1, h, D), lambda i: (0, i, 0)) for h in half]
                + [pl.BlockSpec((3, h, D), lambda i: (0, i, 0)) for h in half])
    args = (*pbs, *gots)
    if after is not None:
        in_specs, args = in_specs + [pl.BlockSpec(memory_space=pl.ANY)], args + (after,)
    res, _ = _call(body, name, (2,), in_specs, [pl.BlockSpec((h, D), lambda i: (i, 0)) for h in half],
                   [jax.ShapeDtypeStruct(pb.shape[1:], F32) for pb in pbs], args)
    return res


def _rope_tables(pos_col, inv_freq, comm=None):
    t = pos_col.shape[0]

    def body(p_ref, f_ref, c_ref, s_ref):
        ang = p_ref[...].astype(F32) * f_ref[...]
        lane = lax.broadcasted_iota(jnp.int32, ang.shape, 1)
        c_ref[...] = jnp.where(lane < QK_ROPE, jnp.cos(ang), 0.0)
        s = jnp.sin(ang)
        s_ref[...] = jnp.where(lane < 32, -s, jnp.where(lane < QK_ROPE, s, 0.0))

    spec = pl.BlockSpec((TB, 128), lambda i: (i, 0))
    return _call(
        body, "rope_tables", (t // TB,), [pl.BlockSpec((TB, 1), lambda i: (i, 0)), _const_spec((1, 128))],
        [spec] * 2, [jax.ShapeDtypeStruct((t, 128), F32)] * 2, (pos_col, inv_freq), (), comm)


def _sgu_conv_fwd(proj, tail, lng_ref, ws_ref, bst_ref, cw_ref):
    gu = _gelu(proj[:, 0:SG_W])
    gv = _gelu(proj[:, SG_W:2 * SG_W])
    bg = proj[:, 1024:1536]
    z = proj[:, 1536:2048] * proj[:, 2048:2560]
    heads = []
    for h in range(SG_HEADS):
        sl = slice(h * SG_HD, (h + 1) * SG_HD)
        vn, _, _ = _ln_head(gv[:, sl], lng_ref[:, sl])
        vnb = vn.astype(BF16)
        wm = _tril_bf16(ws_ref[h])
        bcol = bst_ref[:, h:h + 1]
        mixed = jnp.concatenate(
            [_dot(wm, vnb[k * SG_CHUNK:(k + 1) * SG_CHUNK], 1, 0) + bcol for k in range(TB // SG_CHUNK)], axis=0)
        heads.append(gu[:, sl] * mixed)
    a_out = jnp.concatenate(heads, axis=1)
    y, _, _ = _conv_fwd(z, tail, cw_ref)
    return a_out, bg * y, z


def _even_fwd(x, wg, gamma, lng, ws, bst, cw, seq, comm=None):
    t = x.shape[0]
    nbs = seq // TB

    def body(x_ref, gam_ref, win_ref, wout_ref, lng_ref, ws_ref, bst_ref, cw_ref, x1_ref, proj_ref, tail_ref):
        i = pl.program_id(0)
        xv = x_ref[...]
        h, _ = _rms(xv, gam_ref[...])
        proj = _dot(h.astype(BF16), win_ref[...].reshape(EVEN_IN, D), 1, 1)
        proj_ref[...] = proj.astype(BF16)
        tail = jnp.where(i % nbs == 0, 0.0, tail_ref[...])
        a_out, b_out, z = _sgu_conv_fwd(proj, tail, lng_ref, ws_ref, bst_ref, cw_ref)
        tail_ref[...] = z[TB - HALO:, :]
        x1_ref[...] = (xv + _dot(a_out.astype(BF16), wout_ref[0:4].reshape(512, D), 1, 0)
                       + _dot(b_out.astype(BF16), wout_ref[4:8].reshape(512, D), 1, 0))

    row = pl.BlockSpec((TB, D), lambda i: (i, 0))
    return _call(
        body, "even_fwd", (t // TB,),
        [row, _const_spec((1, D)), _wspec(N_EIN, OFF_EIN), _wspec(N_SQ, OFF_EOUT), _const_spec((1, SG_W)),
         _const_spec((SG_HEADS, 128, 128)), _const_spec((128, 128)), _const_spec((8, SC_W))],
        [row, pl.BlockSpec((TB, EVEN_IN), lambda i: (i, 0))],
        [jax.ShapeDtypeStruct((t, D), F32), jax.ShapeDtypeStruct((t, EVEN_IN), BF16)],
        (x, gamma, wg, wg, lng, ws, bst, cw), [pltpu.VMEM((HALO, SC_W), F32)], comm)


def _even_bwd(x, proj, dx1, wg, gamma, lng, ws, bst, cw, seq, comm=None):
    t = x.shape[0]
    nb, nbs = t // TB, seq // TB

    def body(x_ref, proj_ref, ptail_ref, dx1_ref, gam_ref, win_ref, wout_ref, lng_ref, ws_ref, bst_ref, cw_ref,
             dx0_ref, dproj_ref, mix_ref, h_ref, dgam_ref, dws_ref, dbc_ref, dlng_ref, dcw_ref, head_ref):
        i = pl.program_id(0)
        blk = nb - 1 - i

        @pl.when(i == 0)
        def _():
            dgam_ref[...] = jnp.zeros_like(dgam_ref)
            dws_ref[...] = jnp.zeros_like(dws_ref)
            dbc_ref[...] = jnp.zeros_like(dbc_ref)
            dlng_ref[...] = jnp.zeros_like(dlng_ref)
            dcw_ref[...] = jnp.zeros_like(dcw_ref)

        xv = x_ref[...]
        gam = gam_ref[...]
        h, r = _rms(xv, gam)
        h_ref[...] = h.astype(BF16)
        dx1 = dx1_ref[...]
        dmix = _dot(dx1.astype(BF16), wout_ref[...].reshape(D, D), 1, 1)
        da, db = dmix[:, :SG_W], dmix[:, SG_W:]
        proj = proj_ref[...].astype(F32)
        u, v = proj[:, 0:SG_W], proj[:, SG_W:2 * SG_W]
        bg, cg, hv = proj[:, 1024:1536], proj[:, 1536:2048], proj[:, 2048:2560]
        gu, gv = _gelu(u), _gelu(v)

        a_heads, dgv_heads = [], []
        for hd in range(SG_HEADS):
            sl = slice(hd * SG_HD, (hd + 1) * SG_HD)
            g_h = lng_ref[:, sl]
            vn, xh, rr = _ln_head(gv[:, sl], g_h)
            vnb = vn.astype(BF16)
            wm = _tril_bf16(ws_ref[hd])
            bcol = bst_ref[:, hd:hd + 1]
            mixed_c, dvn_c = [], []
            dw_acc = jnp.zeros((128, 128), F32)
            db_acc = jnp.zeros((128, 1), F32)
            for k in range(TB // SG_CHUNK):
                rs = slice(k * SG_CHUNK, (k + 1) * SG_CHUNK)
                mixed = _dot(wm, vnb[rs], 1, 0) + bcol
                dmixed = da[rs, sl] * gu[rs, sl]
                dmb = dmixed.astype(BF16)
                dvn_c.append(_dot(wm, dmb, 0, 0))
                dw_acc = dw_acc + _dot(dmb, vnb[rs], 1, 1)
                db_acc = db_acc + jnp.sum(dmixed, axis=1, keepdims=True)
                mixed_c.append(mixed)
            mixed_h = jnp.concatenate(mixed_c, axis=0)
            dvn = jnp.concatenate(dvn_c, axis=0)
            r_i = lax.broadcasted_iota(jnp.int32, (128, 128), 0)
            c_i = lax.broadcasted_iota(jnp.int32, (128, 128), 1)
            dws_ref[hd] += jnp.where(r_i >= c_i, dw_acc, 0.0)
            dbc_ref[:, hd:hd + 1] += db_acc
            dlng_ref[:, sl] += jnp.sum(dvn * xh, axis=0, keepdims=True)
            dxh = dvn * g_h
            dgv = rr * (dxh - jnp.mean(dxh, axis=-1, keepdims=True)
                        - xh * jnp.mean(dxh * xh, axis=-1, keepdims=True))
            a_heads.append(gu[:, sl] * mixed_h)
            dproj_ref[:, sl] = (da[:, sl] * mixed_h * _gelu_grad(u[:, sl])).astype(BF16)
            dgv_heads.append(dgv * _gelu_grad(v[:, sl]))
        dproj_ref[:, SG_W:2 * SG_W] = jnp.concatenate(dgv_heads, axis=1).astype(BF16)
        mix_ref[:, :SG_W] = jnp.concatenate(a_heads, axis=1).astype(BF16)

        z = cg * hv
        pt = ptail_ref[...].astype(F32)
        tail = jnp.where(blk % nbs == 0, 0.0, pt[:, 1536:2048] * pt[:, 2048:2560])
        y, zs1, zs2 = _conv_fwd(z, tail, cw_ref)
        mix_ref[:, SG_W:] = (bg * y).astype(BF16)
        dy = db * bg
        head = jnp.where(blk % nbs == nbs - 1, 0.0, head_ref[...])
        ext = jnp.concatenate([dy, head], axis=0)
        dz = (cw_ref[2:3, :] * dy + cw_ref[1:2, :] * _shift_up(ext, 1)[:TB]
              + cw_ref[0:1, :] * _shift_up(ext, 2)[:TB])
        head_ref[...] = dy[:HALO, :]
        dcw_ref[2:3, :] += jnp.sum(dy * z, axis=0, keepdims=True)
        dcw_ref[1:2, :] += jnp.sum(dy * zs1, axis=0, keepdims=True)
        dcw_ref[0:1, :] += jnp.sum(dy * zs2, axis=0, keepdims=True)
        dproj_ref[:, 1024:1536] = (db * y).astype(BF16)
        dproj_ref[:, 1536:2048] = (dz * hv).astype(BF16)
        dproj_ref[:, 2048:2560] = (dz * cg).astype(BF16)

        dh = _dot(dproj_ref[...], win_ref[...].reshape(EVEN_IN, D), 1, 0)
        dxn, dgam = _rms_bwd(xv, r, gam, dh)
        dgam_ref[...] += dgam
        dx0_ref[...] = dx1 + dxn

    def rev(w):
        return pl.BlockSpec((TB, w), lambda i: (nb - 1 - i, 0))

    ptail = pl.BlockSpec((HALO, EVEN_IN), lambda i: (jnp.maximum((nb - 1 - i) * (TB // HALO) - 1, 0), 0))
    return _call(
        body, "even_bwd", (nb,),
        [rev(D), rev(EVEN_IN), ptail, rev(D), _const_spec((1, D)), _wspec(N_EIN, OFF_EIN),
         _wspec(N_SQ, OFF_EOUT), _const_spec((1, SG_W)), _const_spec((SG_HEADS, 128, 128)),
         _const_spec((128, 128)), _const_spec((8, SC_W))],
        [rev(D), rev(EVEN_IN), rev(D), rev(D), _const_spec((1, D)), _const_spec((SG_HEADS, 128, 128)),
         _const_spec((128, 128)), _const_spec((1, SG_W)), _const_spec((8, SC_W))],
        [jax.ShapeDtypeStruct((t, D), F32), jax.ShapeDtypeStruct((t, EVEN_IN), BF16),
         jax.ShapeDtypeStruct((t, D), BF16), jax.ShapeDtypeStruct((t, D), BF16),
         jax.ShapeDtypeStruct((1, D), F32), jax.ShapeDtypeStruct((SG_HEADS, 128, 128), F32),
         jax.ShapeDtypeStruct((128, 128), F32), jax.ShapeDtypeStruct((1, SG_W), F32),
         jax.ShapeDtypeStruct((8, SC_W), F32)],
        (x, proj, proj, dx1, gamma, wg, wg, lng, ws, bst, cw), [pltpu.VMEM((HALO, SC_W), F32)], comm)


def _last_block_fwd(x, c_out, d_out, w_mix1, w_gu, w_d, gamma, target):
    t = x.shape[0]

    def body(x_ref, c_ref, d_ref, wo_ref, gam_ref, wg_ref, wu_ref, wd_ref, t_ref,
             x3_ref, dy_ref, g_ref, u_ref, loss_ref):
        @pl.when(pl.program_id(0) == 0)
        def _():
            loss_ref[...] = jnp.zeros_like(loss_ref)

        xv = (x_ref[...] + _dot(c_ref[...], wo_ref[0:2].reshape(POOL_W, D), 1, 0)
              + _dot(d_ref[...], wo_ref[2:8].reshape(HEADS * V_DIM, D), 1, 0))
        x3_ref[...] = xv
        h, _ = _rms(xv, gam_ref[...])
        hb = h.astype(BF16)
        g = _dot(hb, wg_ref[...].reshape(D_FF, D), 1, 1)
        u = _dot(hb, wu_ref[...].reshape(D_FF, D), 1, 1)
        g_ref[...] = g.astype(BF16)
        u_ref[...] = u.astype(BF16)
        act = g * jax.nn.sigmoid(g) * u
        err = xv + _dot(act.astype(BF16), wd_ref[...].reshape(D_FF, D), 1, 0) - t_ref[...]
        dy_ref[...] = err * (1.0 / D)
        sq = jnp.sum(jnp.sum(err * err, axis=-1, keepdims=True), axis=0, keepdims=True)
        loss_ref[...] += (0.5 / D) * sq

    def row(w):
        return pl.BlockSpec((TB, w), lambda i: (i, 0))

    res, _ = _call(
        body, "last_block_fwd", (t // TB,),
        [row(D), row(POOL_W), row(HEADS * V_DIM), _wspec(N_SQ, OFF_OOUT), _const_spec((1, D)),
         _wspec(N_FF, OFF_GATE), _wspec(N_FF, OFF_UP), _wspec(N_FF, 0), row(D)],
        [row(D), row(D), row(D_FF), row(D_FF), _const_spec((8, 128))],
        [jax.ShapeDtypeStruct((t, D), F32), jax.ShapeDtypeStruct((t, D), F32), jax.ShapeDtypeStruct((t, D_FF), BF16),
         jax.ShapeDtypeStruct((t, D_FF), BF16), jax.ShapeDtypeStruct((8, 128), F32)],
        (x, c_out, d_out, w_mix1, gamma, w_gu, w_gu, w_d, target))
    return res


def _ffn_up(x, w_gu, gamma, name, comm=None):
    t = x.shape[0]

    def body(x_ref, gam_ref, wg_ref, wu_ref, g_ref, u_ref, act_ref):
        h, _ = _rms(x_ref[...], gam_ref[...])
        hb = h.astype(BF16)
        g = _dot(hb, wg_ref[...].reshape(D_FF, D), 1, 1)
        u = _dot(hb, wu_ref[...].reshape(D_FF, D), 1, 1)
        g_ref[...] = g.astype(BF16)
        u_ref[...] = u.astype(BF16)
        act_ref[...] = (g * jax.nn.sigmoid(g) * u).astype(BF16)

    row = pl.BlockSpec((TB, D), lambda i: (i, 0))
    wide = pl.BlockSpec((TB, D_FF), lambda i: (i, 0))
    return _call(body, name, (t // TB,), [row, _const_spec((1, D)), _wspec(N_FF, OFF_GATE), _wspec(N_FF, OFF_UP)],
                 [wide, wide, wide], [jax.ShapeDtypeStruct((t, D_FF), BF16)] * 3, (x, gamma, w_gu, w_gu), (), comm)


def _ffn_down(x, act, w_d, name, comm=None):
    t = x.shape[0]

    def body(x_ref, a_ref, wd_ref, y_ref):
        y_ref[...] = x_ref[...] + _dot(a_ref[...], wd_ref[...].reshape(D_FF, D), 1, 0)

    row = pl.BlockSpec((TB, D), lambda i: (i, 0))
    wide = pl.BlockSpec((TB, D_FF), lambda i: (i, 0))
    return _call(body, name, (t // TB,), [row, wide, _wspec(N_FF, 0)], [row], [jax.ShapeDtypeStruct((t, D), F32)],
                 (x, act, w_d), (), comm)


def _ffn_bwd(x, g, u, dy, w_gu, w_d, gamma, name, comm=None, w_mix1=None):
    t = x.shape[0]
    with_dmix = w_mix1 is not None

    def body(*refs):
        x_ref, g_ref, u_ref, dy_ref, gam_ref, wg_ref, wu_ref, wd_ref = refs[:8]
        dx_ref, act_ref, dg_ref, du_ref, h_ref, dgam_ref = refs[8 + with_dmix:14 + with_dmix]

        @pl.when(pl.program_id(0) == 0)
        def _():
            dgam_ref[...] = jnp.zeros_like(dgam_ref)

        xv = x_ref[...]
        gam = gam_ref[...]
        h, r = _rms(xv, gam)
        h_ref[...] = h.astype(BF16)
        dyv = dy_ref[...]
        dact = _dot(dyv.astype(BF16), wd_ref[...].reshape(D_FF, D), 1, 1)
        gv = g_ref[...].astype(F32)
        uv = u_ref[...].astype(F32)
        sg = jax.nn.sigmoid(gv)
        silu = gv * sg
        act_ref[...] = (silu * uv).astype(BF16)
        dgb = (dact * uv * (sg * (1.0 + gv * (1.0 - sg)))).astype(BF16)
        dub = (dact * silu).astype(BF16)
        dg_ref[...] = dgb
        du_ref[...] = dub
        dh = _dot(dgb, wg_ref[...].reshape(D_FF, D), 1, 0) + _dot(dub, wu_ref[...].reshape(D_FF, D), 1, 0)
        dxn, dgam = _rms_bwd(xv, r, gam, dh)
        dgam_ref[...] += dgam
        dx = dyv + dxn
        dx_ref[...] = dx
        if with_dmix:
            refs[15][...] = _dot(dx.astype(BF16), refs[8][...].reshape(D, D), 1, 1).astype(BF16)

    row = pl.BlockSpec((TB_FFN_BWD, D), lambda i: (i, 0))
    wide = pl.BlockSpec((TB_FFN_BWD, D_FF), lambda i: (i, 0))
    in_specs = [row, wide, wide, row, _const_spec((1, D)), _wspec(N_FF, OFF_GATE), _wspec(N_FF, OFF_UP),
                _wspec(N_FF, 0)]
    out_specs = [row, wide, wide, wide, row, _const_spec((1, D))]
    out_shape = [jax.ShapeDtypeStruct((t, D), F32), jax.ShapeDtypeStruct((t, D_FF), BF16),
                 jax.ShapeDtypeStruct((t, D_FF), BF16), jax.ShapeDtypeStruct((t, D_FF), BF16),
                 jax.ShapeDtypeStruct((t, D), BF16), jax.ShapeDtypeStruct((1, D), F32)]
    args = (x, g, u, dy, gamma, w_gu, w_gu, w_d)
    if with_dmix:
        in_specs, args = in_specs + [_wspec(N_SQ, OFF_OOUT)], args + (w_mix1,)
        out_specs, out_shape = out_specs + [row], out_shape + [jax.ShapeDtypeStruct((t, D), BF16)]
    return _call(body, name, (t // TB_FFN_BWD,), in_specs, out_specs, out_shape, args, (), comm)


def _odd_pre_fwd(x, wg, gamma, qbt, kvbt, qa_g, kva_g, pw_bd, pscale, seq, comm=None):
    t = x.shape[0]
    nbs = seq // TB

    def body(x_ref, gam_ref, win_ref, qb_ref, kvb_ref, qa_ref, kva_ref, pw_ref, ps_ref,
             proj_ref, q_ref, kv_ref, kr_ref, c_ref, tail_ref):
        i = pl.program_id(0)
        h, _ = _rms(x_ref[...], gam_ref[...])
        proj = _dot(h.astype(BF16), win_ref[...].reshape(D, D), 1, 0)
        proj_ref[...] = proj.astype(BF16)
        zp, ql, kvl = proj[:, :POOL_W], proj[:, 256:640], proj[:, 640:896]
        kr_ref[...] = proj[:, 896:1024]
        qn, _ = _rms(ql, qa_ref[...])
        q_ref[...] = _dot(qn.astype(BF16), qb_ref[...], 1, 1).astype(BF16)
        kvn, _ = _rms(kvl, kva_ref[...])
        kv_ref[...] = _dot(kvn.astype(BF16), kvb_ref[...], 1, 1).astype(BF16)
        tail = jnp.where(i % nbs == 0, 0.0, tail_ref[...])
        pooled, _, _ = _pool_fwd(zp, tail, i % nbs)
        tail_ref[...] = zp[TB - HALO:, :]
        c_ref[...] = (_dot(pooled.astype(BF16), pw_ref[...], 1, 0) * ps_ref[...]).astype(BF16)

    def row(w):
        return pl.BlockSpec((TB, w), lambda i: (i, 0))

    return _call(
        body, "odd_pre_fwd", (t // TB,),
        [row(D), _const_spec((1, D)), _wspec(N_SQ, OFF_OIN), _const_spec((HEADS * HP, Q_LORA)),
         _const_spec((HEADS * HP, KV_LORA)), _const_spec((1, Q_LORA)), _const_spec((1, KV_LORA)),
         _const_spec((POOL_W, POOL_W)), _const_spec((1, POOL_W))],
        [row(D), row(HEADS * HP), row(HEADS * HP), row(128), row(POOL_W)],
        [jax.ShapeDtypeStruct((t, D), BF16), jax.ShapeDtypeStruct((t, HEADS * HP), BF16),
         jax.ShapeDtypeStruct((t, HEADS * HP), BF16), jax.ShapeDtypeStruct((t, 128), F32),
         jax.ShapeDtypeStruct((t, POOL_W), BF16)],
        (x, gamma, wg, qbt, kvbt, qa_g, kva_g, pw_bd, pscale), [pltpu.VMEM((HALO, POOL_W), F32)], comm)


def _odd_pre_bwd(x, proj, dx3, dmix, dq, dkv, dkr, wg, gamma, qbt, kvbt, qa_g, kva_g, pw_bd, pscale, seq):
    t = x.shape[0]
    nb, nbs = t // TB, seq // TB

    def body(x_ref, proj_ref, ptail_ref, dx3_ref, dco_ref, dq_ref, dkv_ref, dkr_ref, gam_ref, win_ref, qb_ref,
             kvb_ref, qa_ref, kva_ref, pw_ref, ps_ref,
             dx2_ref, dproj_ref, h_ref, qn_ref, kvn_ref, dgam_ref, dqa_ref, dkva_ref, dpw_ref, dps_ref, head_ref):
        i = pl.program_id(0)
        blk = nb - 1 - i

        @pl.when(i == 0)
        def _():
            dgam_ref[...] = jnp.zeros_like(dgam_ref)
            dqa_ref[...] = jnp.zeros_like(dqa_ref)
            dkva_ref[...] = jnp.zeros_like(dkva_ref)
            dpw_ref[...] = jnp.zeros_like(dpw_ref)
            dps_ref[...] = jnp.zeros_like(dps_ref)

        xv = x_ref[...]
        gam = gam_ref[...]
        h, r = _rms(xv, gam)
        h_ref[...] = h.astype(BF16)
        proj = proj_ref[...].astype(F32)
        zp, ql, kvl = proj[:, :POOL_W], proj[:, 256:640], proj[:, 640:896]

        qa = qa_ref[...]
        qn, rq = _rms(ql, qa)
        qn_ref[...] = qn.astype(BF16)
        dql, dqa = _rms_bwd(ql, rq, qa, _dot(dq_ref[...], qb_ref[...], 1, 0))
        dqa_ref[...] += dqa
        kva = kva_ref[...]
        kvn, rkv = _rms(kvl, kva)
        kvn_ref[...] = kvn.astype(BF16)
        dkvl, dkva = _rms_bwd(kvl, rkv, kva, _dot(dkv_ref[...], kvb_ref[...], 1, 0))
        dkva_ref[...] += dkva

        pt = ptail_ref[...].astype(F32)
        tail = jnp.where(blk % nbs == 0, 0.0, pt[:, :POOL_W])
        pooled, cnt, grp = _pool_fwd(zp, tail, blk % nbs)
        pb = pooled.astype(BF16)
        pw = pw_ref[...]
        dco = dco_ref[...].astype(F32)
        dps_ref[...] += jnp.sum(dco * _dot(pb, pw, 1, 0), axis=0, keepdims=True)
        dpo = (dco * ps_ref[...]).astype(BF16)
        dpw_ref[...] += _dot(pb, dpo, 0, 0)
        dpooled = _dot(dpo, pw, 1, 1)
        dpm = dpooled / cnt
        head = jnp.where(blk % nbs == nbs - 1, 0.0, head_ref[...])
        dz = _pool_bwd(dpooled, dpm, head, grp)
        head_ref[...] = dpm[:HALO, :]

        dproj_ref[:, :POOL_W] = dz.astype(BF16)
        dproj_ref[:, 256:640] = dql.astype(BF16)
        dproj_ref[:, 640:896] = dkvl.astype(BF16)
        dproj_ref[:, 896:1024] = dkr_ref[...].astype(BF16)
        dh = _dot(dproj_ref[...], win_ref[...].reshape(D, D), 1, 1)
        dxn, dgam = _rms_bwd(xv, r, gam, dh)
        dgam_ref[...] += dgam
        dx2_ref[...] = dx3_ref[...] + dxn

    def rev(w):
        return pl.BlockSpec((TB, w), lambda i: (nb - 1 - i, 0))

    ptail = pl.BlockSpec((HALO, D), lambda i: (jnp.maximum((nb - 1 - i) * (TB // HALO) - 1, 0), 0))
    return pl.pallas_call(
        body, name="odd_pre_bwd",
        out_shape=[jax.ShapeDtypeStruct((t, D), F32), jax.ShapeDtypeStruct((t, D), BF16),
                   jax.ShapeDtypeStruct((t, D), BF16), jax.ShapeDtypeStruct((t, Q_LORA), BF16),
                   jax.ShapeDtypeStruct((t, KV_LORA), BF16), jax.ShapeDtypeStruct((1, D), F32),
                   jax.ShapeDtypeStruct((1, Q_LORA), F32), jax.ShapeDtypeStruct((1, KV_LORA), F32),
                   jax.ShapeDtypeStruct((POOL_W, POOL_W), F32), jax.ShapeDtypeStruct((1, POOL_W), F32)],
        grid=(nb,),
        in_specs=[rev(D), rev(D), ptail, rev(D), rev(POOL_W), rev(HEADS * HP), rev(HEADS * HP), rev(128),
                  _const_spec((1, D)), _wspec(N_SQ, OFF_OIN), _const_spec((HEADS * HP, Q_LORA)),
                  _const_spec((HEADS * HP, KV_LORA)), _const_spec((1, Q_LORA)), _const_spec((1, KV_LORA)),
                  _const_spec((POOL_W, POOL_W)), _const_spec((1, POOL_W))],
        out_specs=[rev(D), rev(D), rev(D), rev(Q_LORA), rev(KV_LORA), _const_spec((1, D)), _const_spec((1, Q_LORA)),
                   _const_spec((1, KV_LORA)), _const_spec((POOL_W, POOL_W)), _const_spec((1, POOL_W))],
        scratch_shapes=[pltpu.VMEM((HALO, POOL_W), F32)],
        compiler_params=_cparams(1),
    )(x, proj, proj, dx3, dmix, dq, dkv, dkr, gamma, wg, qbt, kvbt, qa_g, kva_g, pw_bd, pscale)


def _attn_specs(seq):
    head = pl.BlockSpec((seq, HP), lambda b, h: (b, h))
    shared = pl.BlockSpec((seq, 128), lambda b, h: (b, 0))
    gain = pl.BlockSpec((1, HP), lambda b, h: (0, 0))
    return head, shared, gain


def _causal_bias(n):
    rows = lax.broadcasted_iota(jnp.int32, (n, n), 0)
    cols = lax.broadcasted_iota(jnp.int32, (n, n), 1)
    return jnp.where(cols <= rows, 0.0, NEG_INF)


def _attn_fwd(q, kv, kr, cos, sin, gq, gk, seq, comm=None):
    t = q.shape[0]
    qb = min(512, seq)

    def body(q_ref, kv_ref, kr_ref, c_ref, s_ref, gq_ref, gk_ref, o_ref, lse_ref):
        c, s = c_ref[...], s_ref[...]
        qf, _ = _qk_prep(q_ref[...].astype(F32), gq_ref[...], c, s)
        kin = jnp.concatenate([kv_ref[:, :128].astype(F32), kr_ref[...]], axis=1)
        kf, _ = _qk_prep(kin, gk_ref[...], c, s)
        qf, kf = qf.astype(BF16), kf.astype(BF16)
        v1 = jnp.concatenate([kv_ref[:, 128:], jnp.ones((seq, V_DIM), BF16)], axis=1)
        bias = _causal_bias(qb)
        for q0 in range(0, seq, qb):
            q1 = q0 + qb
            qblk = qf[q0:q1]
            s_dg = _dot(qblk, kf[q0:q1], 1, 1) + bias
            m = jnp.max(s_dg, axis=-1, keepdims=True)
            if q0:
                s_off = _dot(qblk, kf[:q0], 1, 1)
                m = jnp.maximum(m, jnp.max(s_off, axis=-1, keepdims=True))
            acc = _dot(jnp.exp(s_dg - m).astype(BF16), v1[q0:q1], 1, 0)
            if q0:
                acc = acc + _dot(jnp.exp(s_off - m).astype(BF16), v1[:q0], 1, 0)
            l = acc[:, V_DIM:]
            o_ref[q0:q1, :] = (acc[:, :V_DIM] / l).astype(BF16)
            lse_ref[q0:q1, :] = m + jnp.log(l)

    head, shared, gain = _attn_specs(seq)
    per_head = pl.BlockSpec((seq, V_DIM), lambda b, h: (b, h))
    return _call(
        body, "attn_fwd", (t // seq, HEADS),
        [head, head, shared, shared, shared, gain, gain], [per_head, per_head],
        [jax.ShapeDtypeStruct((t, HEADS * V_DIM), BF16), jax.ShapeDtypeStruct((t, HEADS * V_DIM), F32)],
        (q, kv, kr, cos, sin, gq, gk), (), comm)


def _attn_bwd(q, kv, kr, cos, sin, gq, gk, dmix, d_out, lse, seq, comm=None):
    t = q.shape[0]
    qb = min(512, seq)

    def body(q_ref, kv_ref, kr_ref, c_ref, s_ref, gq_ref, gk_ref, do_ref, o_ref, lse_ref,
             dq_ref, dkv_ref, dkr_ref, dgq_ref, dgk_ref, dqf_ref, dkf_ref, dv_ref):
        b, hd = pl.program_id(0), pl.program_id(1)

        @pl.when((b == 0) & (hd == 0))
        def _():
            dgq_ref[...] = jnp.zeros_like(dgq_ref)
            dgk_ref[...] = jnp.zeros_like(dgk_ref)

        c, sn = c_ref[...], s_ref[...]
        gq_v, gk_v = gq_ref[...], gk_ref[...]
        qin = q_ref[...].astype(F32)
        kin = jnp.concatenate([kv_ref[:, :128].astype(F32), kr_ref[...]], axis=1)
        qf32, rq = _qk_prep(qin, gq_v, c, sn)
        kf32, rk = _qk_prep(kin, gk_v, c, sn)
        qf, kf = qf32.astype(BF16), kf32.astype(BF16)
        vb = kv_ref[:, 128:]
        dkf_ref[...] = jnp.zeros_like(dkf_ref)
        dv_ref[...] = jnp.zeros_like(dv_ref)
        bias = _causal_bias(qb)
        for q0 in range(0, seq, qb):
            q1 = q0 + qb
            qblk = qf[q0:q1]
            do = do_ref[q0:q1, :]
            lse_col = lse_ref[q0:q1, 0:1]
            d_col = jnp.sum(do.astype(F32) * o_ref[q0:q1, :].astype(F32), axis=-1, keepdims=True)
            dq_acc = None
            for k0, k1, diag in ((q0, q1, True), (0, q0, False)):
                if k1 == k0:
                    continue
                s = _dot(qblk, kf[k0:k1], 1, 1)
                p = jnp.exp((s + bias if diag else s) - lse_col)
                dv_ref[k0:k1, :] += _dot(p.astype(BF16), do, 0, 0)
                ds = (p * (_dot(do, vb[k0:k1], 1, 1) - d_col)).astype(BF16)
                part = _dot(ds, kf[k0:k1], 1, 0)
                dq_acc = part if dq_acc is None else dq_acc + part
                dkf_ref[k0:k1, :] += _dot(ds, qblk, 0, 0)
            dqf_ref[q0:q1, :] = dq_acc
        dqin, dgq = _qk_prep_bwd(dqf_ref[...], qin, rq, gq_v, c, sn)
        dkin, dgk = _qk_prep_bwd(dkf_ref[...], kin, rk, gk_v, c, sn)
        dgq_ref[...] += dgq
        dgk_ref[...] += dgk
        dq_ref[...] = dqin.astype(BF16)
        dkv_ref[:, :128] = dkin[:, :128].astype(BF16)
        dkv_ref[:, 128:] = dv_ref[...].astype(BF16)

        @pl.when(hd == 0)
        def _():
            dkr_ref[...] = dkin[:, 128:]

        @pl.when(hd != 0)
        def _():
            dkr_ref[...] += dkin[:, 128:]

    head, shared, gain = _attn_specs(seq)
    per_head = pl.BlockSpec((seq, V_DIM), lambda b, h: (b, h))
    return _call(
        body, "attn_bwd", (t // seq, HEADS),
        [head, head, shared, shared, shared, gain, gain,
         pl.BlockSpec((seq, V_DIM), lambda b, h: (b, 2 + h)), per_head, per_head],
        [head, head, shared, gain, gain],
        [jax.ShapeDtypeStruct((t, HEADS * HP), BF16), jax.ShapeDtypeStruct((t, HEADS * HP), BF16),
         jax.ShapeDtypeStruct((t, 128), F32), jax.ShapeDtypeStruct((1, HP), F32),
         jax.ShapeDtypeStruct((1, HP), F32)],
        (q, kv, kr, cos, sin, gq, gk, dmix, d_out, lse),
        [pltpu.VMEM((seq, HP), F32), pltpu.VMEM((seq, HP), F32), pltpu.VMEM((seq, V_DIM), F32)], comm)


def _tn(a_list, b, tm, name, into=None, comm=None):
    t, n_out = b.shape
    widths = [a.shape[1] for a in a_list]
    tk = min(TK_DW, t)
    m, na, nk = sum(widths), len(a_list), t // tk
    assert na == 1 or tm == m

    def body(*refs):
        a_refs, b_ref, o_ref, acc_ref = refs[:na], refs[na], refs[-2], refs[-1]
        k = pl.program_id(1)

        @pl.when(k == 0)
        def _():
            acc_ref[...] = jnp.zeros_like(acc_ref)

        bb = b_ref[...].astype(BF16)
        m0 = 0
        for a_ref, w in zip(a_refs, widths):
            rows = slice(0, tm) if na == 1 else slice(m0, m0 + w)
            acc_ref[rows, :] += _dot(a_ref[...].astype(BF16), bb, 0, 0)
            m0 += w

        @pl.when(k == nk - 1)
        def _():
            o_ref[...] = acc_ref[...].astype(BF16).reshape(o_ref.shape)

    if na == 1:
        in_specs = [pl.BlockSpec((tk, tm), lambda i, k: (k, i))]
    else:
        in_specs = [pl.BlockSpec((tk, w), lambda i, k: (k, 0)) for w in widths]
    in_specs.append(pl.BlockSpec((tk, n_out), lambda i, k: (k, 0)))
    args = list(a_list) + [b]
    if into is None:
        out_spec = pl.BlockSpec((tm, n_out), lambda i, k: (i, 0))
        out_shape = jax.ShapeDtypeStruct((m, n_out), BF16)
        aliases = {}
    else:
        buf, n, off = into
        assert n_out == D and tm % n == 0 and off % n == 0 and (na == 1 or tm // n == N_DEV)
        idx = off // n
        out_spec = pl.BlockSpec((tm // n, n, D), lambda i, k: (i, idx, 0))
        out_shape = jax.ShapeDtypeStruct(buf.shape, BF16)
        in_specs.append(pl.BlockSpec(memory_space=pl.ANY))
        args.append(buf)
        aliases = {len(args) - 1: 0}
    (res,), extra = _call(body, name, (m // tm, nk), in_specs, [out_spec], [out_shape], args,
                          [pltpu.VMEM((tm, n_out), F32)], comm, aliases)
    return (res, extra) if comm is not None else res


def _adamw(ws, gs, ms, vs, name, nblk=1):
    n = len(ws)
    c1 = 1.0 - B1 ** STEP
    c2 = 1.0 - B2 ** STEP

    def body(*refs):
        for a in range(n):
            w, g, m, v = (refs[k * n + a][...] for k in range(4))
            d_ref, m_ref, v_ref = (refs[(4 + k) * n + a] for k in range(3))
            m_new = B1 * m + (1.0 - B1) * g
            v_new = B2 * v + (1.0 - B2) * (g * g)
            d_ref[...] = -LR * ((m_new / c1) / (jnp.sqrt(v_new / c2) + ADAM_EPS) + WD * w)
            m_ref[...] = m_new
            v_ref[...] = v_new

    grid = (nblk,)
    assert all(w.shape[0] % nblk == 0 and (nblk == 1 or (w.shape[0] // nblk) % 8 == 0) for w in ws)
    specs = [pl.BlockSpec((w.shape[0] // nblk, w.shape[1]), lambda i: (i, 0)) for w in ws]
    outs, _ = _call(body, name, grid, specs * 4, specs * 3, [jax.ShapeDtypeStruct(w.shape, F32) for w in ws] * 3,
                    (*ws, *gs, *ms, *vs))
    return outs[:n], outs[n:2 * n], outs[2 * n:]


def _rows1024(a, rows):
    flat = a.reshape(-1, D)
    return jnp.pad(flat, ((0, rows - flat.shape[0]), (0, 0)))


def _pack_shards(even_w_in, even_w_out, odd_w_in, q_b, kv_b, odd_w_out, ffn_w_gate, ffn_w_up, ffn_w_down):
    mix0 = jnp.concatenate([even_w_in[0].T, jnp.zeros((OFF_EOUT - N_EIN, D), F32), even_w_out[0]], axis=0)
    gu = [jnp.concatenate([ffn_w_gate[layer].T, ffn_w_up[layer].T], axis=0) for layer in range(2)]
    mix1 = jnp.concatenate([jnp.pad(odd_w_in[0], ((0, 0), (0, D - ODD_IN))), odd_w_out[0],
                            _rows1024(q_b[0].T, N_QB), _rows1024(kv_b[0].T, N_KVB),
                            jnp.zeros((R_MIX1 - OFF_KVB - N_KVB, D), F32)], axis=0)
    return [c.astype(BF16) for c in (mix0, gu[0], ffn_w_down[0], mix1, gu[1], ffn_w_down[1])]


def _pad_heads(a):
    k = a.shape[1]
    return jnp.pad(a.reshape(HEADS, QK_DIM, k), ((0, 0), (0, HP - QK_DIM), (0, 0))).reshape(HEADS * HP, k)


def _small_pack(parts):
    flat = []
    for p in parts:
        v = p.reshape(-1)
        flat.append(jnp.pad(v, (0, (-v.shape[0]) % 1024)))
    return jnp.concatenate(flat).reshape(-1, 128)


def _small_unpack(buf, shapes):
    flat = buf.reshape(-1)
    out, off = [], 0
    for s in shapes:
        size = int(np.prod(s))
        out.append(flat[off:off + size].reshape(s))
        off += size + (-size) % 1024
    return out


def _step(x3d, positions, target3d, chunks, tile, where, mix_norm, ffn_norm, sg_ln_g, sg_w_s, sg_b_s,
          pool_w, q_norm, k_norm):
    bsz, seq, _ = x3d.shape
    t = bsz * seq
    x0 = x3d.reshape(t, D)
    target = target3d.reshape(t, D)
    my_mix0, my_gu0, my_d0, my_mix1, my_gu1, my_d1 = chunks

    lane = np.arange(128)
    inv_freq = np.where(lane < QK_ROPE, ROPE_THETA ** (-(2.0 * (lane % 32)) / QK_ROPE), 0.0)
    inv_freq = jnp.asarray(inv_freq.reshape(1, 128), F32)
    (cos, sin), (w_mix0, tiles) = _rope_tables(positions.reshape(t, 1), inv_freq, _gather_comm([my_mix0, tile]))

    conv_w = tiles[:, 0:3, 0:64].transpose(1, 0, 2).reshape(3, SC_W)
    pool_scale = tiles[:, 3, 0:32].reshape(1, POOL_W)
    q_a_norm = tiles[:, 4, 0:48].reshape(1, Q_LORA)
    kv_a_norm = tiles[:, 5, 0:32].reshape(1, KV_LORA)
    ws = sg_w_s[0]
    bst = jnp.pad(sg_b_s[0].T, ((0, 0), (0, 128 - SG_HEADS)))
    cw = jnp.pad(conv_w, ((0, 8 - 3), (0, 0)))
    pw_bd = jax.scipy.linalg.block_diag(*[pool_w[0, g] for g in range(4)]).astype(BF16)
    gq = jnp.pad(q_norm * ATT_SCALE, ((0, 0), (0, HP - QK_DIM)))
    gk = jnp.pad(k_norm, ((0, 0), (0, HP - QK_DIM)))

    (x1, proj_e), (w_gu0,) = _even_fwd(x0, w_mix0, mix_norm[0:1], sg_ln_g, ws, bst, cw, seq, _gather_comm([my_gu0]))
    (g0, u0, act0), (w_d0, w_mix1) = _ffn_up(x1, w_gu0, ffn_norm[0:1], "ffn_up0", _gather_comm([my_d0, my_mix1]))
    (x2,), (w_d1,) = _ffn_down(x1, act0, w_d0, "ffn_down0", _gather_comm([my_d1]))
    qbt = _pad_heads(w_mix1[:, OFF_QB:OFF_QB + N_QB_USED, :].reshape(HEADS * QK_DIM, Q_LORA))
    kvbt = w_mix1[:, OFF_KVB:OFF_KVB + N_KVB, :].reshape(HEADS * HP, KV_LORA)
    (proj_o, q, kv, kr, c_out), _ = _odd_pre_fwd(x2, w_mix1, mix_norm[1:2], qbt, kvbt, q_a_norm, kv_a_norm,
                                                pw_bd, pool_scale, seq)
    (d_out, lse), (w_gu1,) = _attn_fwd(q, kv, kr, cos, sin, gq, gk, seq, _gather_comm([my_gu1]))
    x3, dy, g1, u1, loss_tile = _last_block_fwd(x2, c_out, d_out, w_mix1, w_gu1, w_d1, ffn_norm[1:2], target)

    def chunk(rows, padded=False):
        return jnp.zeros((N_DEV, rows, D), BF16) if padded else lax.empty((N_DEV, rows, D), BF16)

    (dx3, act1, dg1, du1, h3, dgam_f1, dmix_o), _ = _ffn_bwd(x3, g1, u1, dy, w_gu1, w_d1, ffn_norm[1:2], "ffn_bwd1",
                                                           None, w_mix1)
    gp_ffn1 = _tn([dg1], h3, 1408, "dw_gate1", (chunk(R_GU + N_FF), N_FF, OFF_GATE))
    gp_ffn1 = _tn([du1], h3, 1408, "dw_up1", (gp_ffn1, N_FF, OFF_UP))
    gp_ffn1 = _tn([act1], dy, 1408, "dw_down1", (gp_ffn1, N_FF, R_GU))

    gp_mix1, (ga_ffn1,) = _tn([c_out, d_out], dx3, D, "dw_oout", (chunk(R_MIX1, True), N_SQ, OFF_OOUT),
                              _pair_exchange_comm(gp_ffn1))
    pb_ffn1 = _rs_pair_sum(gp_ffn1, ga_ffn1, where, "rs_pair_sum_ffn1")
    (dq, dkv, dkr, dgq, dgk), (gb_ffn1,) = _attn_bwd(q, kv, kr, cos, sin, gq, gk, dmix_o, d_out, lse, seq,
                                                    _chip_exchange_comm(pb_ffn1))
    (dx2, dproj_o, h2, qn, kvn, dgam_m1, dqa, dkva, dpw_bd, dps) = _odd_pre_bwd(
        x2, proj_o, dx3, dmix_o, dq, dkv, dkr, w_mix1, mix_norm[1:2], qbt, kvbt, q_a_norm, kv_a_norm, pw_bd,
        pool_scale, seq)
    gp_mix1 = _tn([h2], dproj_o, D, "dw_oin", (gp_mix1, N_SQ, OFF_OIN))
    d_qbt = _tn([dq], qn, HEADS * HP, "dw_qb")
    d_qb_rows = d_qbt.reshape(HEADS, HP, Q_LORA)[:, :QK_DIM].reshape(N_DEV, N_QB_USED, D)
    d_kvb_rows = _tn([dkv], kvn, HEADS * HP, "dw_kvb").reshape(N_DEV, N_KVB, D)
    gp_mix1 = lax.dynamic_update_slice(gp_mix1, d_qb_rows, (0, OFF_QB, 0))
    gp_mix1 = lax.dynamic_update_slice(gp_mix1, d_kvb_rows, (0, OFF_KVB, 0))

    (dx1, act0, dg0, du0, h1, dgam_f0), (ga_mix1,) = _ffn_bwd(x1, g0, u0, dx2, w_gu0, w_d0, ffn_norm[0:1], "ffn_bwd0",
                                                             _pair_exchange_comm(gp_mix1))
    pb_mix1 = _rs_pair_sum(gp_mix1, ga_mix1, where, "rs_pair_sum_mix1")
    gp_ffn0a, (gb_mix1,) = _tn([dg0], h1, 1408, "dw_gate0", (chunk(R_GU), N_FF, OFF_GATE),
                               _chip_exchange_comm(pb_mix1))
    gp_ffn0a = _tn([du0], h1, 1408, "dw_up0", (gp_ffn0a, N_FF, OFF_UP))
    gp_ffn0b, (ga_ffn0a,) = _tn([act0], dx2, 1408, "dw_down0", (chunk(N_FF), N_FF, 0),
                                _pair_exchange_comm(gp_ffn0a))
    pb_ffn0a = _rs_pair_sum(gp_ffn0a, ga_ffn0a, where, "rs_pair_sum_ffn0a")
    *open_ffn0a, started = _chip_exchange_start(pb_ffn0a, "ffn0a")

    (dx0, dproj_e, mix_e, h0, dgam_m0, dws, dbc, dlng, dcw), _ = _even_bwd(
        x0, proj_e, dx1, w_mix0, mix_norm[0:1], sg_ln_g, ws, bst, cw + 0.0 * started[:, :1], seq)

    small = _small_pack([
        jnp.concatenate([dgam_m0, dgam_m1], 0), jnp.concatenate([dgam_f0, dgam_f1], 0), dlng,
        dws[None], dbc[:, :SG_HEADS].T[None], dcw[:3],
        jnp.stack([dpw_bd[g * POOL_GD:(g + 1) * POOL_GD, g * POOL_GD:(g + 1) * POOL_GD] for g in range(4)])[None],
        dps, dqa, dkva, dgq[:, :QK_DIM] * ATT_SCALE, dgk[:, :QK_DIM], loss_tile[0:1, 0:1]])
    gp_mix0, (small_all, ga_ffn0b) = _tn([mix_e], dx1, D, "dw_eout", (chunk(R_MIX0, True), N_SQ, OFF_EOUT),
                                         _both(_gather_comm([small]), _pair_exchange_comm(gp_ffn0b)))
    pb_ffn0b = _rs_pair_sum(gp_ffn0b, ga_ffn0b, where, "rs_pair_sum_ffn0b")
    gp_mix0, (gb_ffn0b,) = _tn([dproj_e], h0, 1280, "dw_ein", (gp_mix0, N_EIN, OFF_EIN),
                               _chip_exchange_comm(pb_ffn0b))
    small_sum = _small_unpack(_sum_gathered(small_all), SMALL_SHAPES)
    partials = ([pb_ffn0b, pb_mix1, pb_ffn1], [gb_ffn0b, gb_mix1, gb_ffn1])
    return dx0.reshape(bsz, seq, D), partials, open_ffn0a, gp_mix0, small_sum


SMALL_SHAPES = [(2, D), (2, D), (1, SG_W), (1, SG_HEADS, 128, 128), (1, SG_HEADS, 128), (3, SC_W),
                (1, 4, POOL_GD, POOL_GD), (1, POOL_W), (1, Q_LORA), (1, KV_LORA), (1, QK_DIM), (1, QK_DIM), (1, 1)]


def kernel(x, positions, mix_norm, ffn_norm, even_w_in, sg_ln_g, sg_w_s, sg_b_s, sc_conv_w, even_w_out, odd_w_in, pool_w, pool_scale, q_a_norm, q_b, kv_a_norm, kv_b, q_norm, k_norm, odd_w_out, ffn_w_gate, ffn_w_up, ffn_w_down, loss_target, m_mix_norm, m_ffn_norm, m_even_w_in, m_sg_ln_g, m_sg_w_s, m_sg_b_s, m_sc_conv_w, m_even_w_out, m_odd_w_in, m_pool_w, m_pool_scale, m_q_a_norm, m_q_b, m_kv_a_norm, m_kv_b, m_q_norm, m_k_norm, m_odd_w_out, m_ffn_w_gate, m_ffn_w_up, m_ffn_w_down, v_mix_norm, v_ffn_norm, v_even_w_in, v_sg_ln_g, v_sg_w_s, v_sg_b_s, v_sc_conv_w, v_even_w_out, v_odd_w_in, v_pool_w, v_pool_scale, v_q_a_norm, v_q_b, v_kv_a_norm, v_kv_b, v_q_norm, v_k_norm, v_odd_w_out, v_ffn_w_gate, v_ffn_w_up, v_ffn_w_down):
    xi, yi, ci = _place()
    me = 4 * xi + 2 * yi + ci

    chunks = _pack_shards(even_w_in, even_w_out, odd_w_in, q_b, kv_b, odd_w_out, ffn_w_gate, ffn_w_up, ffn_w_down)

    def lane_pad(a):
        return jnp.pad(a, ((0, 0), (0, 128 - a.shape[1])))

    tile = jnp.concatenate([lane_pad(sc_conv_w[0]), lane_pad(pool_scale), lane_pad(q_a_norm), lane_pad(kv_a_norm),
                            jnp.zeros((2, 128), F32)], axis=0)
    chip = 2 * xi + yi
    where = jnp.stack([ci, chip, chip ^ 2, chip ^ 1, chip ^ 3]).astype(jnp.int32)
    grad_x, (pbs, gbs), ffn0a_in_flight, gp_mix0, tot = _step(
        x, positions, loss_target, chunks, tile, where, mix_norm, ffn_norm, sg_ln_g, sg_w_s, sg_b_s,
        pool_w, q_norm, k_norm)

    (ga_mix0,) = _comm_alone(_pair_exchange_comm(gp_mix0), "rs_pair_exchange_mix0")
    pb_mix0 = _rs_pair_sum(gp_mix0, ga_mix0, where, "rs_pair_sum_mix0")
    mix0_sems, pb_mix0, land_mix0, started = _chip_exchange_start(pb_mix0, "mix0")
    pb_ffn0a, gb_ffn0a = _chip_exchange_wait(*ffn0a_in_flight, started, "ffn0a")
    gsh_ffn0a, gsh_ffn0b, gsh_mix1, gsh_ffn1 = _rs_final_sums([pb_ffn0a] + pbs, [gb_ffn0a] + gbs, "rs_final_sums",
                                                              started)

    (g_mix, g_ffn, g_lng, g_ws, g_bs, g_cw_full, g_pw, g_ps_full, g_qa_full, g_kva_full, g_qn, g_kn, loss) = tot
    g_cw = lax.dynamic_slice_in_dim(g_cw_full, me * 64, 64, axis=1)[None]
    g_ps = lax.dynamic_slice_in_dim(g_ps_full, me * 32, 32, axis=1)
    g_qa = lax.dynamic_slice_in_dim(g_qa_full, me * 48, 48, axis=1)
    g_kva = lax.dynamic_slice_in_dim(g_kva_full, me * 32, 32, axis=1)

    def tr(a):
        return jnp.swapaxes(a, -1, -2)

    g_gate = tr(jnp.stack([gsh_ffn0a[OFF_GATE:OFF_GATE + N_FF], gsh_ffn1[OFF_GATE:OFF_GATE + N_FF]]))
    g_up = tr(jnp.stack([gsh_ffn0a[OFF_UP:OFF_UP + N_FF], gsh_ffn1[OFF_UP:OFF_UP + N_FF]]))
    g_down = jnp.stack([gsh_ffn0b, gsh_ffn1[R_GU:R_GU + N_FF]])
    g_oin = gsh_mix1[OFF_OIN:OFF_OIN + N_SQ, :ODD_IN][None]
    g_oout = gsh_mix1[OFF_OOUT:OFF_OOUT + N_SQ][None]
    g_qb = tr(gsh_mix1[OFF_QB:OFF_QB + N_QB_USED].reshape(1, 144, Q_LORA))
    g_kvb = tr(gsh_mix1[OFF_KVB:OFF_KVB + N_KVB].reshape(1, 192, KV_LORA))
    transposed = ("even_w_in", "odd_w_in", "q_b", "kv_b", "ffn_w_gate", "ffn_w_up")

    names = ("mix_norm", "ffn_norm", "even_w_in", "sg_ln_g", "sg_w_s", "sg_b_s", "sc_conv_w", "even_w_out",
             "odd_w_in", "pool_w", "pool_scale", "q_a_norm", "q_b", "kv_a_norm", "kv_b", "q_norm", "k_norm",
             "odd_w_out", "ffn_w_gate", "ffn_w_up", "ffn_w_down")
    grads = dict(mix_norm=g_mix, ffn_norm=g_ffn, sg_ln_g=g_lng, sg_w_s=g_ws, sg_b_s=g_bs,
                 sc_conv_w=g_cw, odd_w_in=g_oin, pool_w=g_pw, pool_scale=g_ps, q_a_norm=g_qa,
                 q_b=g_qb, kv_a_norm=g_kva, kv_b=g_kvb, q_norm=g_qn, k_norm=g_kn, odd_w_out=g_oout,
                 ffn_w_gate=g_gate, ffn_w_up=g_up, ffn_w_down=g_down)
    weights = dict(mix_norm=mix_norm, ffn_norm=ffn_norm, even_w_in=even_w_in, sg_ln_g=sg_ln_g, sg_w_s=sg_w_s,
                   sg_b_s=sg_b_s, sc_conv_w=sc_conv_w, even_w_out=even_w_out, odd_w_in=odd_w_in, pool_w=pool_w,
                   pool_scale=pool_scale, q_a_norm=q_a_norm, q_b=q_b, kv_a_norm=kv_a_norm, kv_b=kv_b, q_norm=q_norm,
                   k_norm=k_norm, odd_w_out=odd_w_out, ffn_w_gate=ffn_w_gate, ffn_w_up=ffn_w_up,
                   ffn_w_down=ffn_w_down)
    m_in = dict(mix_norm=m_mix_norm, ffn_norm=m_ffn_norm, even_w_in=m_even_w_in, sg_ln_g=m_sg_ln_g, sg_w_s=m_sg_w_s,
                sg_b_s=m_sg_b_s, sc_conv_w=m_sc_conv_w, even_w_out=m_even_w_out, odd_w_in=m_odd_w_in,
                pool_w=m_pool_w, pool_scale=m_pool_scale, q_a_norm=m_q_a_norm, q_b=m_q_b, kv_a_norm=m_kv_a_norm,
                kv_b=m_kv_b, q_norm=m_q_norm, k_norm=m_k_norm, odd_w_out=m_odd_w_out, ffn_w_gate=m_ffn_w_gate,
                ffn_w_up=m_ffn_w_up, ffn_w_down=m_ffn_w_down)
    v_in = dict(mix_norm=v_mix_norm, ffn_norm=v_ffn_norm, even_w_in=v_even_w_in, sg_ln_g=v_sg_ln_g, sg_w_s=v_sg_w_s,
                sg_b_s=v_sg_b_s, sc_conv_w=v_sc_conv_w, even_w_out=v_even_w_out, odd_w_in=v_odd_w_in,
                pool_w=v_pool_w, pool_scale=v_pool_scale, q_a_norm=v_q_a_norm, q_b=v_q_b, kv_a_norm=v_kv_a_norm,
                kv_b=v_kv_b, q_norm=v_q_norm, k_norm=v_k_norm, odd_w_out=v_odd_w_out, ffn_w_gate=v_ffn_w_gate,
                ffn_w_up=v_ffn_w_up, ffn_w_down=v_ffn_w_down)
    delta, new_m, new_v = {}, {}, {}

    def as2d(k, a):
        a = tr(a) if k in transposed else a
        return a.reshape(-1, a.shape[-1])

    def back(k, a):
        shape = weights[k].shape
        return tr(a.reshape(shape[:-2] + (shape[-1], shape[-2]))) if k in transposed else a.reshape(shape)

    def update(group, name, nblk=1):
        outs = _adamw([as2d(k, weights[k]) for k in group], [as2d(k, grads[k]) for k in group],
                      [as2d(k, m_in[k]) for k in group], [as2d(k, v_in[k]) for k in group], name, nblk)
        for i, k in enumerate(group):
            delta[k], new_m[k], new_v[k] = (back(k, o[i]) for o in outs)

    update(["ffn_w_gate", "ffn_w_up", "ffn_w_down"], "adamw_ffn", 4)
    update(["odd_w_in", "odd_w_out"], "adamw_mix1", 2)
    update([k for k in names if k not in delta and k not in ("even_w_in", "even_w_out")], "adamw_small")

    pb_mix0, gb_mix0 = _chip_exchange_wait(mix0_sems, pb_mix0, land_mix0, new_v["k_norm"], "mix0")
    (gsh_mix0,) = _rs_final_sums([pb_mix0], [gb_mix0], "rs_final_sum_mix0")
    grads["even_w_in"] = tr(gsh_mix0[OFF_EIN:OFF_EIN + N_EIN][None])
    grads["even_w_out"] = gsh_mix0[OFF_EOUT:OFF_EOUT + N_SQ][None]
    update(["even_w_in", "even_w_out"], "adamw_mix0", 2)

    return (loss.reshape(()), grad_x, *[grads[k] for k in names], *[delta[k] for k in names],
            *[new_m[k] for k in names], *[new_v[k] for k in names])
```

```python
import functools

import numpy as np
import jax
import jax.numpy as jnp
from jax import lax
from jax.experimental import pallas as pl
from jax.experimental.pallas import tpu as pltpu

F32 = jnp.float32
BF16 = jnp.bfloat16
MESH = pl.DeviceIdType.MESH

D = 1024
EPS = 1e-6
NEG_INF = -1e30
SG_HEADS, SG_HD, SG_W, SG_CHUNK = 4, 128, 512, 128
SC_W = 512
EVEN_IN = 2560
POOL_W = 256
POOL_GD = 64
Q_LORA, KV_LORA, QK_ROPE, QK_NOPE, V_DIM = 384, 256, 64, 128, 128
QK_DIM = QK_NOPE + QK_ROPE
HEADS = 6
HP = 256
ODD_IN = 960
D_FF = 2816
ROPE_THETA = 10000.0
ATT_SCALE = QK_DIM ** -0.5
LR, B1, B2, ADAM_EPS, WD, STEP = 0.001, 0.9, 0.999, 1e-08, 0.01, 10

N_DEV = 8
TB = 512
TB_FFN_BWD = 256
TK_DW = 1024
HALO = 16
VMEM_LIMIT = 56 * 1024 * 1024

N_EIN, N_FF, N_SQ = 320, 352, 128
OFF_EIN, OFF_EOUT, R_MIX0 = 0, 384, 512
OFF_GATE, OFF_UP, R_GU = 0, 352, 704
OFF_OIN, OFF_OOUT, OFF_QB, OFF_KVB, R_MIX1 = 0, 128, 256, 320, 384
N_QB, N_QB_USED, N_KVB = 64, 54, 48

INV_SQRT2 = 0.7071067811865476
INV_SQRT_2PI = 0.3989422804014327


def _dot(a, b, ca, cb):
    return lax.dot_general(a, b, (((ca,), (cb,)), ((), ())), preferred_element_type=F32)


def _cparams(n_axes=1):
    return pltpu.CompilerParams(dimension_semantics=("arbitrary",) * n_axes, vmem_limit_bytes=VMEM_LIMIT)


def _wspec(n, off):
    assert off % n == 0
    idx = off // n
    return pl.BlockSpec((N_DEV, n, D), lambda i: (0, idx, 0), pipeline_mode=pl.Buffered(1))


def _const_spec(shape):
    zeros = (0,) * len(shape)
    return pl.BlockSpec(shape, lambda *_: zeros)


class _Comm:
    def __init__(self, ins, out_shapes, sems, start, wait, mid=None):
        self.ins, self.out_shapes, self.sems, self.start, self.wait, self.mid = ins, out_shapes, sems, start, wait, mid


def _both(c1, c2):
    def split(f1, f2):
        def run(ins, outs, sems):
            f1(ins[:len(c1.ins)], outs[:len(c1.out_shapes)], sems[:len(c1.sems)])
            f2(ins[len(c1.ins):], outs[len(c1.out_shapes):], sems[len(c1.sems):])
        return run

    def nothing(ins, outs, sems):
        pass

    mid = None if c1.mid is None and c2.mid is None else split(c1.mid or nothing, c2.mid or nothing)
    return _Comm(c1.ins + c2.ins, c1.out_shapes + c2.out_shapes, c1.sems + c2.sems,
                 split(c1.start, c2.start), split(c1.wait, c2.wait), mid)


def _call(body, name, grid, in_specs, out_specs, out_shape, args, scratch_shapes=(), comm=None, aliases=None):
    n_axes = len(grid)
    aliases = aliases or {}
    if comm is None:
        res = pl.pallas_call(
            body, name=name, grid=grid, in_specs=list(in_specs), out_specs=list(out_specs),
            out_shape=list(out_shape), scratch_shapes=list(scratch_shapes), input_output_aliases=aliases,
            compiler_params=_cparams(n_axes))(*args)
        return list(res), []
    ni, no, ns = len(in_specs), len(out_specs), len(scratch_shapes)
    ci, co = len(comm.ins), len(comm.out_shapes)
    n_steps = int(np.prod(grid))

    def carrier(*refs):
        ins, cin = refs[:ni], refs[ni:ni + ci]
        outs, cout = refs[ni + ci:ni + ci + no], refs[ni + ci + no:ni + ci + no + co]
        scr, sems = refs[ni + ci + no + co:ni + ci + no + co + ns], refs[ni + ci + no + co + ns:]
        step = 0
        for a in range(n_axes):
            step = step * grid[a] + pl.program_id(a)

        @pl.when(step == 0)
        def _():
            comm.start(cin, cout, sems)

        body(*ins, *outs, *scr)

        if comm.mid is not None and n_steps >= 4:
            @pl.when(step == n_steps // 2)
            def _():
                comm.mid(cin, cout, sems)

        @pl.when(step == n_steps - 1)
        def _():
            if comm.mid is not None and n_steps < 4:
                comm.mid(cin, cout, sems)
            comm.wait(cin, cout, sems)

    any_spec = pl.BlockSpec(memory_space=pl.ANY)
    res = pl.pallas_call(
        carrier, name=name, grid=grid, in_specs=list(in_specs) + [any_spec] * ci,
        out_specs=list(out_specs) + [any_spec] * co, out_shape=list(out_shape) + list(comm.out_shapes),
        scratch_shapes=list(scratch_shapes) + list(comm.sems), input_output_aliases=aliases,
        compiler_params=_cparams(n_axes))(*args, *comm.ins)
    return list(res[:no]), list(res[no:])


def _comm_alone(comm, name):
    ci, co = len(comm.ins), len(comm.out_shapes)

    def body(*refs):
        cin, cout, sems = refs[:ci], refs[ci:ci + co], refs[ci + co:]
        comm.start(cin, cout, sems)
        if comm.mid is not None:
            comm.mid(cin, cout, sems)
        comm.wait(cin, cout, sems)

    any_spec = pl.BlockSpec(memory_space=pl.ANY)
    res = pl.pallas_call(
        body, name=name, out_shape=list(comm.out_shapes), in_specs=[any_spec] * ci, out_specs=[any_spec] * co,
        scratch_shapes=list(comm.sems))(*comm.ins)
    return list(res)


def _rms(x, g):
    r = lax.rsqrt(jnp.mean(x * x, axis=-1, keepdims=True) + EPS)
    return x * r * g, r


def _rms_bwd(x, r, g, dy):
    xh = x * r
    dxh = dy * g
    dx = r * (dxh - xh * jnp.mean(dxh * xh, axis=-1, keepdims=True))
    dg = jnp.sum(dy * xh, axis=0, keepdims=True)
    return dx, dg


def _gelu(x):
    return 0.5 * x * (1.0 + lax.erf(x * INV_SQRT2))


def _gelu_grad(x):
    return 0.5 * (1.0 + lax.erf(x * INV_SQRT2)) + x * jnp.exp(-0.5 * x * x) * INV_SQRT_2PI


def _shift_down(a, k):
    rows = lax.broadcasted_iota(jnp.int32, a.shape, 0)
    return jnp.where(rows >= k, pltpu.roll(a, k, 0), 0.0)


def _shift_up(a, k):
    n = a.shape[0]
    rows = lax.broadcasted_iota(jnp.int32, a.shape, 0)
    return jnp.where(rows < n - k, pltpu.roll(a, n - k, 0), 0.0)


def _tril_bf16(w):
    r = lax.broadcasted_iota(jnp.int32, w.shape, 0)
    c = lax.broadcasted_iota(jnp.int32, w.shape, 1)
    return jnp.where(r >= c, w, 0.0).astype(BF16)


def _ln_head(vh, g):
    mu = jnp.mean(vh, axis=-1, keepdims=True)
    xc = vh - mu
    rr = lax.rsqrt(jnp.mean(xc * xc, axis=-1, keepdims=True) + EPS)
    xh = xc * rr
    return xh * g, xh, rr


def _conv_fwd(z, tail, cw_ref):
    ext = jnp.concatenate([tail, z], axis=0)
    zs1 = _shift_down(ext, 1)[HALO:]
    zs2 = _shift_down(ext, 2)[HALO:]
    y = cw_ref[2:3, :] * z + cw_ref[1:2, :] * zs1 + cw_ref[0:1, :] * zs2
    return y, zs1, zs2


def _pool_cnt(shape, blk_in_seq):
    rows = lax.broadcasted_iota(jnp.int32, shape, 0)
    grp = lax.broadcasted_iota(jnp.int32, shape, 1) // POOL_GD
    win = jnp.where(grp == 0, 2, jnp.where(grp == 1, 4, jnp.where(grp == 2, 8, 16)))
    tpos = blk_in_seq * shape[0] + rows + 1
    return jnp.minimum(tpos, win).astype(F32), grp


def _pool_select(grp, s2, s4, s8, s16):
    return jnp.where(grp == 0, s2, jnp.where(grp == 1, s4, jnp.where(grp == 2, s8, s16)))


def _pool_fwd(z, tail, blk_in_seq):
    ext = jnp.concatenate([tail, z], axis=0)
    s2 = ext + _shift_down(ext, 1)
    s4 = s2 + _shift_down(s2, 2)
    s8 = s4 + _shift_down(s4, 4)
    s16 = s8 + _shift_down(s8, 8)
    cnt, grp = _pool_cnt(z.shape, blk_in_seq)
    sums = _pool_select(grp, s2[HALO:], s4[HALO:], s8[HALO:], s16[HALO:])
    return sums / cnt - z, cnt, grp


def _pool_bwd(dpooled, dpm, head, grp):
    n = dpm.shape[0]
    ext = jnp.concatenate([dpm, head], axis=0)
    u2 = ext + _shift_up(ext, 1)
    u4 = u2 + _shift_up(u2, 2)
    u8 = u4 + _shift_up(u4, 4)
    u16 = u8 + _shift_up(u8, 8)
    return _pool_select(grp, u2[:n], u4[:n], u8[:n], u16[:n]) - dpooled


def _lane_sums(a):
    return _dot(a.astype(BF16), jnp.ones((a.shape[1], a.shape[1]), BF16), 1, 0)


def _swap_halves(y1):
    src = lax.broadcasted_iota(jnp.int32, (128, 128), 0)
    dst = lax.broadcasted_iota(jnp.int32, (128, 128), 1)
    perm = jnp.where(((dst < 32) & (src == dst + 32)) | ((dst >= 32) & (dst < QK_ROPE) & (src == dst - 32)), 1.0, 0.0)
    return _dot(y1.astype(BF16), perm.astype(BF16), 1, 0)


def _rope(y1, c, s):
    return y1 * c + _swap_halves(y1) * s


def _rope_bwd(d1, c, s):
    return d1 * c + _swap_halves(d1 * s)


def _qk_prep(x, g, c, s):
    r = lax.rsqrt(_lane_sums(x * x) * (1.0 / QK_DIM) + EPS)
    y = x * r * g
    return jnp.concatenate([y[:, :128], _rope(y[:, 128:], c, s)], axis=1), r


def _qk_prep_bwd(dout, x, r, g, c, s):
    dy = jnp.concatenate([dout[:, :128], _rope_bwd(dout[:, 128:], c, s)], axis=1)
    xh = x * r
    dxh = dy * g
    dx = r * (dxh - xh * (_lane_sums(dxh * xh) * (1.0 / QK_DIM)))
    return dx, jnp.sum(dy * xh, axis=0, keepdims=True)


def _place():
    return lax.axis_index("x"), lax.axis_index("y"), lax.axis_index("c")


def _gather_comm(arrs):
    n = len(arrs)

    def halves(a):
        rows = arrs[a].shape[0]
        tile = 16 if arrs[a].dtype == BF16 else 8
        top = rows // 2 if rows % (2 * tile) == 0 else rows
        return (0, top), (top, rows - top)

    def plan(ins, outs, sems):
        send_sems, recv_sems, local_sems = sems
        x, y, c = _place()
        me, sib, xn, yn, dg = (x, y, c), (x, y, 1 - c), (1 - x, y, c), (x, 1 - y, c), (1 - x, 1 - y, c)

        def slot(a, dev, part=None):
            ref = outs[a].at[4 * dev[0] + 2 * dev[1] + dev[2]]
            return ref if part is None else ref.at[pl.ds(part[0], part[1])]

        def copy(a, k, block, to, src=None, part=None):
            return pltpu.make_async_remote_copy(
                src_ref=slot(a, block, part) if src is None else src, dst_ref=slot(a, block, part),
                send_sem=send_sems.at[a, k], recv_sem=recv_sems.at[a, k], device_id=to, device_id_type=MESH)

        local = [pltpu.make_async_copy(ins[a], slot(a, me), local_sems.at[a]) for a in range(n)]
        return me, sib, xn, yn, dg, copy, local

    def start(ins, outs, sems):
        me, sib, xn, yn, _, copy, local = plan(ins, outs, sems)
        for a in range(n):
            local[a].start()
            for k, to in enumerate((sib, xn, yn)):
                copy(a, k, me, to, src=ins[a]).start()

    def mid(ins, outs, sems):
        me, sib, xn, yn, _, copy, _ = plan(ins, outs, sems)
        for a in range(n):
            top, bottom = halves(a)
            copy(a, 1, xn, me).wait_recv()
            copy(a, 3, xn, yn, part=top).start()
            copy(a, 5, xn, sib).start()
            copy(a, 2, yn, me).wait_recv()
            if bottom[1]:
                copy(a, 4, yn, xn, part=bottom).start()
            copy(a, 6, yn, sib).start()

    def wait(ins, outs, sems):
        me, sib, xn, yn, dg, copy, local = plan(ins, outs, sems)
        other = lambda dev: (dev[0], dev[1], 1 - dev[2])
        for a in range(n):
            top, bottom = halves(a)
            copy(a, 3, dg, me, part=top).wait_recv()
            if bottom[1]:
                copy(a, 4, dg, me, part=bottom).wait_recv()
            copy(a, 7, dg, sib).start()
        for a in range(n):
            top, bottom = halves(a)
            for k, block in ((0, sib), (5, other(xn)), (6, other(yn)), (7, other(dg))):
                copy(a, k, block, me).wait_recv()
            for k, block in ((0, me), (1, me), (2, me), (5, xn), (6, yn), (7, dg)):
                copy(a, k, block, me, src=ins[a] if k < 3 else None).wait_send()
            copy(a, 3, xn, me, part=top).wait_send()
            if bottom[1]:
                copy(a, 4, yn, me, part=bottom).wait_send()
            local[a].wait()

    return _Comm(
        list(arrs), [jax.ShapeDtypeStruct((N_DEV,) + a.shape, a.dtype) for a in arrs],
        [pltpu.SemaphoreType.DMA((n, 8)), pltpu.SemaphoreType.DMA((n, 8)), pltpu.SemaphoreType.DMA((n,))],
        start, wait, mid)


def _sum_gathered(g):
    rows = g.shape[1]

    def body(g_ref, sum_ref):
        total = g_ref[0]
        for d in range(1, N_DEV):
            total = total + g_ref[d]
        sum_ref[...] = total

    return pl.pallas_call(
        body, name="sum_gathered_small", out_shape=jax.ShapeDtypeStruct((rows, 128), F32), grid=(1,),
        in_specs=[pl.BlockSpec((N_DEV, rows, 128), lambda i: (0, 0, 0))],
        out_specs=pl.BlockSpec((rows, 128), lambda i: (0, 0)), compiler_params=_cparams(1),
    )(g)


def _sum_rows(rows):
    return rows if rows <= 512 else rows // 2


def _pair_exchange_comm(gp):
    _, rows, cols = gp.shape

    def copies(ins, outs, sems):
        send_sems, recv_sems = sems
        x, y, c = _place()
        return [pltpu.make_async_remote_copy(
            src_ref=ins[0].at[2 * j + (1 - c)], dst_ref=outs[0].at[j], send_sem=send_sems.at[j],
            recv_sem=recv_sems.at[j], device_id=(x, y, 1 - c), device_id_type=MESH) for j in range(4)]

    def start(ins, outs, sems):
        for cp in copies(ins, outs, sems):
            cp.start()

    def wait(ins, outs, sems):
        for cp in copies(ins, outs, sems):
            cp.wait()

    return _Comm([gp], [jax.ShapeDtypeStruct((4, rows, cols), gp.dtype)],
                 [pltpu.SemaphoreType.DMA((4,)), pltpu.SemaphoreType.DMA((4,))], start, wait)


def _rs_pair_sum(gp, got, where, name):
    _, rows, cols = got.shape
    rb = _sum_rows(rows)
    gp4 = gp.reshape(4, 2, rows, cols)

    def body(w_ref, a_ref, b_ref, o_ref):
        o_ref[0] = (a_ref[0, 0].astype(F32) + b_ref[0].astype(F32)).astype(o_ref.dtype)

    return pl.pallas_call(
        body, name=name, out_shape=jax.ShapeDtypeStruct((4, rows, cols), gp.dtype),
        grid_spec=pltpu.PrefetchScalarGridSpec(
            num_scalar_prefetch=1, grid=(4, rows // rb),
            in_specs=[pl.BlockSpec((1, 1, rb, cols), lambda k, r, w: (w[1 + k], w[0], r, 0)),
                      pl.BlockSpec((1, rb, cols), lambda k, r, w: (w[1 + k], r, 0))],
            out_specs=pl.BlockSpec((1, rb, cols), lambda k, r, w: (k, r, 0))),
        compiler_params=_cparams(2),
    )(where, gp4, got)


def _chip_exchange_comm(pb):
    _, rows, cols = pb.shape

    def copies(ins, outs, sems):
        send_sems, recv_sems = sems
        x, y, c = _place()
        chips = [(1 - x, y), (x, 1 - y), (1 - x, 1 - y)]
        return [pltpu.make_async_remote_copy(
            src_ref=ins[0].at[1 + k], dst_ref=outs[0].at[k], send_sem=send_sems.at[k],
            recv_sem=recv_sems.at[k], device_id=(px, py, c), device_id_type=MESH)
            for k, (px, py) in enumerate(chips)]

    def start(ins, outs, sems):
        for cp in copies(ins, outs, sems):
            cp.start()

    def wait(ins, outs, sems):
        for cp in copies(ins, outs, sems):
            cp.wait()

    return _Comm([pb], [jax.ShapeDtypeStruct((3, rows, cols), pb.dtype)],
                 [pltpu.SemaphoreType.DMA((3,)), pltpu.SemaphoreType.DMA((3,))], start, wait)


def _chip_exchange_start(pb, tag):
    _, rows, cols = pb.shape

    def body(pb_ref, land_ref, *rest):
        sems, token = rest[:6], rest[8]
        x, y, c = _place()
        chips = [(1 - x, y), (x, 1 - y), (1 - x, 1 - y)]
        for k, (px, py) in enumerate(chips):
            pltpu.make_async_remote_copy(
                src_ref=pb_ref.at[1 + k], dst_ref=land_ref.at[k], send_sem=sems[k], recv_sem=sems[3 + k],
                device_id=(px, py, c), device_id_type=MESH).start()
        token[...] = jnp.zeros_like(token)

    hbm = pl.BlockSpec(memory_space=pltpu.HBM)
    sem = pl.BlockSpec(memory_space=pltpu.SEMAPHORE)
    land = lax.empty((3, rows, cols), pb.dtype)
    res = pl.pallas_call(
        body, name="rs_chip_exchange_start_" + tag,
        out_shape=(*[pltpu.SemaphoreType.DMA(())] * 6, pltpu.HBM(pb.shape, pb.dtype), pltpu.HBM(land.shape, land.dtype),
                   jax.ShapeDtypeStruct((8, 128), F32)),
        in_specs=(hbm, hbm), out_specs=(*[sem] * 6, hbm, hbm, pl.BlockSpec(memory_space=pltpu.VMEM)),
        input_output_aliases={0: 6, 1: 7},
        compiler_params=pltpu.CompilerParams(has_side_effects=pltpu.SideEffectType.DATAFLOW_SIDE_EFFECTING),
    )(pltpu.with_memory_space_constraint(pb, pltpu.HBM), pltpu.with_memory_space_constraint(land, pltpu.HBM))
    return list(res[:6]), res[6], res[7], res[8]


def _chip_exchange_wait(sems, pb_thru, land_thru, after, tag):
    def body(pb_ref, land_ref, *rest):
        sems_in = rest[:6]
        x, y, c = _place()
        chips = [(1 - x, y), (x, 1 - y), (1 - x, 1 - y)]
        for k, (px, py) in enumerate(chips):
            cp = pltpu.make_async_remote_copy(
                src_ref=pb_ref.at[1 + k], dst_ref=land_ref.at[k], send_sem=sems_in[k], recv_sem=sems_in[3 + k],
                device_id=(px, py, c), device_id_type=MESH)
            cp.wait_send()
            cp.wait_recv()

    hbm = pl.BlockSpec(memory_space=pltpu.HBM)
    sem = pl.BlockSpec(memory_space=pltpu.SEMAPHORE)
    res = pl.pallas_call(
        body, name="rs_chip_exchange_wait_" + tag,
        out_shape=(pltpu.HBM(pb_thru.shape, pb_thru.dtype), pltpu.HBM(land_thru.shape, land_thru.dtype)),
        in_specs=(hbm, hbm, *[sem] * 6, pl.BlockSpec(memory_space=pl.ANY)), out_specs=(hbm, hbm),
        input_output_aliases={0: 0, 1: 1},
        compiler_params=pltpu.CompilerParams(has_side_effects=pltpu.SideEffectType.DATAFLOW_SIDE_EFFECTING),
    )(pb_thru, land_thru, *sems, after)
    return res[0], res[1]


def _rs_final_sums(pbs, gots, name, after=None):
    n = len(pbs)

    def body(*refs):
        outs = refs[len(refs) - n:]
        for a in range(n):
            m_ref, g_ref, o_ref = refs[a], refs[n + a], outs[a]
            o_ref[...] = ((m_ref[0].astype(F32) + g_ref[0].astype(F32)) + g_ref[1].astype(F32)) + g_ref[2].astype(F32)

    half = [pb.shape[1] // 2 for pb in pbs]
    in_specs = ([pl.BlockSpec((1, h, D), lambda i: (0, i, 0)) for h in half]
                + [pl.BlockSpec((3, h, D), lambda i: (0, i, 0)) for h in half])
    args = (*pbs, *gots)
    if after is not None:
        in_specs, args = in_specs + [pl.BlockSpec(memory_space=pl.ANY)], args + (after,)
    res, _ = _call(body, name, (2,), in_specs, [pl.BlockSpec((h, D), lambda i: (i, 0)) for h in half],
                   [jax.ShapeDtypeStruct(pb.shape[1:], F32) for pb in pbs], args)
    return res


def _rope_tables(pos_col, inv_freq, comm=None):
    t = pos_col.shape[0]

    def body(p_ref, f_ref, c_ref, s_ref):
        ang = p_ref[...].astype(F32) * f_ref[...]
        lane = lax.broadcasted_iota(jnp.int32, ang.shape, 1)
        c_ref[...] = jnp.where(lane < QK_ROPE, jnp.cos(ang), 0.0)
        s = jnp.sin(ang)
        s_ref[...] = jnp.where(lane < 32, -s, jnp.where(lane < QK_ROPE, s, 0.0))

    spec = pl.BlockSpec((TB, 128), lambda i: (i, 0))
    return _call(
        body, "rope_tables", (t // TB,), [pl.BlockSpec((TB, 1), lambda i: (i, 0)), _const_spec((1, 128))],
        [spec] * 2, [jax.ShapeDtypeStruct((t, 128), F32)] * 2, (pos_col, inv_freq), (), comm)


def _sgu_conv_fwd(proj, tail, lng_ref, ws_ref, bst_ref, cw_ref):
    gu = _gelu(proj[:, 0:SG_W])
    gv = _gelu(proj[:, SG_W:2 * SG_W])
    bg = proj[:, 1024:1536]
    z = proj[:, 1536:2048] * proj[:, 2048:2560]
    heads = []
    for h in range(SG_HEADS):
        sl = slice(h * SG_HD, (h + 1) * SG_HD)
        vn, _, _ = _ln_head(gv[:, sl], lng_ref[:, sl])
        vnb = vn.astype(BF16)
        wm = _tril_bf16(ws_ref[h])
        bcol = bst_ref[:, h:h + 1]
        mixed = jnp.concatenate(
            [_dot(wm, vnb[k * SG_CHUNK:(k + 1) * SG_CHUNK], 1, 0) + bcol for k in range(TB // SG_CHUNK)], axis=0)
        heads.append(gu[:, sl] * mixed)
    a_out = jnp.concatenate(heads, axis=1)
    y, _, _ = _conv_fwd(z, tail, cw_ref)
    return a_out, bg * y, z


def _even_fwd(x, wg, gamma, lng, ws, bst, cw, seq, comm=None):
    t = x.shape[0]
    nbs = seq // TB

    def body(x_ref, gam_ref, win_ref, wout_ref, lng_ref, ws_ref, bst_ref, cw_ref, x1_ref, proj_ref, tail_ref):
        i = pl.program_id(0)
        xv = x_ref[...]
        h, _ = _rms(xv, gam_ref[...])
        proj = _dot(h.astype(BF16), win_ref[...].reshape(EVEN_IN, D), 1, 1)
        proj_ref[...] = proj.astype(BF16)
        tail = jnp.where(i % nbs == 0, 0.0, tail_ref[...])
        a_out, b_out, z = _sgu_conv_fwd(proj, tail, lng_ref, ws_ref, bst_ref, cw_ref)
        tail_ref[...] = z[TB - HALO:, :]
        x1_ref[...] = (xv + _dot(a_out.astype(BF16), wout_ref[0:4].reshape(512, D), 1, 0)
                       + _dot(b_out.astype(BF16), wout_ref[4:8].reshape(512, D), 1, 0))

    row = pl.BlockSpec((TB, D), lambda i: (i, 0))
    return _call(
        body, "even_fwd", (t // TB,),
        [row, _const_spec((1, D)), _wspec(N_EIN, OFF_EIN), _wspec(N_SQ, OFF_EOUT), _const_spec((1, SG_W)),
         _const_spec((SG_HEADS, 128, 128)), _const_spec((128, 128)), _const_spec((8, SC_W))],
        [row, pl.BlockSpec((TB, EVEN_IN), lambda i: (i, 0))],
        [jax.ShapeDtypeStruct((t, D), F32), jax.ShapeDtypeStruct((t, EVEN_IN), BF16)],
        (x, gamma, wg, wg, lng, ws, bst, cw), [pltpu.VMEM((HALO, SC_W), F32)], comm)


def _even_bwd(x, proj, dx1, wg, gamma, lng, ws, bst, cw, seq, comm=None):
    t = x.shape[0]
    nb, nbs = t // TB, seq // TB

    def body(x_ref, proj_ref, ptail_ref, dx1_ref, gam_ref, win_ref, wout_ref, lng_ref, ws_ref, bst_ref, cw_ref,
             dx0_ref, dproj_ref, mix_ref, h_ref, dgam_ref, dws_ref, dbc_ref, dlng_ref, dcw_ref, head_ref):
        i = pl.program_id(0)
        blk = nb - 1 - i

        @pl.when(i == 0)
        def _():
            dgam_ref[...] = jnp.zeros_like(dgam_ref)
            dws_ref[...] = jnp.zeros_like(dws_ref)
            dbc_ref[...] = jnp.zeros_like(dbc_ref)
            dlng_ref[...] = jnp.zeros_like(dlng_ref)
            dcw_ref[...] = jnp.zeros_like(dcw_ref)

        xv = x_ref[...]
        gam = gam_ref[...]
        h, r = _rms(xv, gam)
        h_ref[...] = h.astype(BF16)
        dx1 = dx1_ref[...]
        dmix = _dot(dx1.astype(BF16), wout_ref[...].reshape(D, D), 1, 1)
        da, db = dmix[:, :SG_W], dmix[:, SG_W:]
        proj = proj_ref[...].astype(F32)
        u, v = proj[:, 0:SG_W], proj[:, SG_W:2 * SG_W]
        bg, cg, hv = proj[:, 1024:1536], proj[:, 1536:2048], proj[:, 2048:2560]
        gu, gv = _gelu(u), _gelu(v)

        a_heads, dgv_heads = [], []
        for hd in range(SG_HEADS):
            sl = slice(hd * SG_HD, (hd + 1) * SG_HD)
            g_h = lng_ref[:, sl]
            vn, xh, rr = _ln_head(gv[:, sl], g_h)
            vnb = vn.astype(BF16)
            wm = _tril_bf16(ws_ref[hd])
            bcol = bst_ref[:, hd:hd + 1]
            mixed_c, dvn_c = [], []
            dw_acc = jnp.zeros((128, 128), F32)
            db_acc = jnp.zeros((128, 1), F32)
            for k in range(TB // SG_CHUNK):
                rs = slice(k * SG_CHUNK, (k + 1) * SG_CHUNK)
                mixed = _dot(wm, vnb[rs], 1, 0) + bcol
                dmixed = da[rs, sl] * gu[rs, sl]
                dmb = dmixed.astype(BF16)
                dvn_c.append(_dot(wm, dmb, 0, 0))
                dw_acc = dw_acc + _dot(dmb, vnb[rs], 1, 1)
                db_acc = db_acc + jnp.sum(dmixed, axis=1, keepdims=True)
                mixed_c.append(mixed)
            mixed_h = jnp.concatenate(mixed_c, axis=0)
            dvn = jnp.concatenate(dvn_c, axis=0)
            r_i = lax.broadcasted_iota(jnp.int32, (128, 128), 0)
            c_i = lax.broadcasted_iota(jnp.int32, (128, 128), 1)
            dws_ref[hd] += jnp.where(r_i >= c_i, dw_acc, 0.0)
            dbc_ref[:, hd:hd + 1] += db_acc
            dlng_ref[:, sl] += jnp.sum(dvn * xh, axis=0, keepdims=True)
            dxh = dvn * g_h
            dgv = rr * (dxh - jnp.mean(dxh, axis=-1, keepdims=True)
                        - xh * jnp.mean(dxh * xh, axis=-1, keepdims=True))
            a_heads.append(gu[:, sl] * mixed_h)
            dproj_ref[:, sl] = (da[:, sl] * mixed_h * _gelu_grad(u[:, sl])).astype(BF16)
            dgv_heads.append(dgv * _gelu_grad(v[:, sl]))
        dproj_ref[:, SG_W:2 * SG_W] = jnp.concatenate(dgv_heads, axis=1).astype(BF16)
        mix_ref[:, :SG_W] = jnp.concatenate(a_heads, axis=1).astype(BF16)

        z = cg * hv
        pt = ptail_ref[...].astype(F32)
        tail = jnp.where(blk % nbs == 0, 0.0, pt[:, 1536:2048] * pt[:, 2048:2560])
        y, zs1, zs2 = _conv_fwd(z, tail, cw_ref)
        mix_ref[:, SG_W:] = (bg * y).astype(BF16)
        dy = db * bg
        head = jnp.where(blk % nbs == nbs - 1, 0.0, head_ref[...])
        ext = jnp.concatenate([dy, head], axis=0)
        dz = (cw_ref[2:3, :] * dy + cw_ref[1:2, :] * _shift_up(ext, 1)[:TB]
              + cw_ref[0:1, :] * _shift_up(ext, 2)[:TB])
        head_ref[...] = dy[:HALO, :]
        dcw_ref[2:3, :] += jnp.sum(dy * z, axis=0, keepdims=True)
        dcw_ref[1:2, :] += jnp.sum(dy * zs1, axis=0, keepdims=True)
        dcw_ref[0:1, :] += jnp.sum(dy * zs2, axis=0, keepdims=True)
        dproj_ref[:, 1024:1536] = (db * y).astype(BF16)
        dproj_ref[:, 1536:2048] = (dz * hv).astype(BF16)
        dproj_ref[:, 2048:2560] = (dz * cg).astype(BF16)

        dh = _dot(dproj_ref[...], win_ref[...].reshape(EVEN_IN, D), 1, 0)
        dxn, dgam = _rms_bwd(xv, r, gam, dh)
        dgam_ref[...] += dgam
        dx0_ref[...] = dx1 + dxn

    def rev(w):
        return pl.BlockSpec((TB, w), lambda i: (nb - 1 - i, 0))

    ptail = pl.BlockSpec((HALO, EVEN_IN), lambda i: (jnp.maximum((nb - 1 - i) * (TB // HALO) - 1, 0), 0))
    return _call(
        body, "even_bwd", (nb,),
        [rev(D), rev(EVEN_IN), ptail, rev(D), _const_spec((1, D)), _wspec(N_EIN, OFF_EIN),
         _wspec(N_SQ, OFF_EOUT), _const_spec((1, SG_W)), _const_spec((SG_HEADS, 128, 128)),
         _const_spec((128, 128)), _const_spec((8, SC_W))],
        [rev(D), rev(EVEN_IN), rev(D), rev(D), _const_spec((1, D)), _const_spec((SG_HEADS, 128, 128)),
         _const_spec((128, 128)), _const_spec((1, SG_W)), _const_spec((8, SC_W))],
        [jax.ShapeDtypeStruct((t, D), F32), jax.ShapeDtypeStruct((t, EVEN_IN), BF16),
         jax.ShapeDtypeStruct((t, D), BF16), jax.ShapeDtypeStruct((t, D), BF16),
         jax.ShapeDtypeStruct((1, D), F32), jax.ShapeDtypeStruct((SG_HEADS, 128, 128), F32),
         jax.ShapeDtypeStruct((128, 128), F32), jax.ShapeDtypeStruct((1, SG_W), F32),
         jax.ShapeDtypeStruct((8, SC_W), F32)],
        (x, proj, proj, dx1, gamma, wg, wg, lng, ws, bst, cw), [pltpu.VMEM((HALO, SC_W), F32)], comm)


def _last_block_fwd(x, c_out, d_out, w_mix1, w_gu, w_d, gamma, target):
    t = x.shape[0]

    def body(x_ref, c_ref, d_ref, wo_ref, gam_ref, wg_ref, wu_ref, wd_ref, t_ref,
             x3_ref, dy_ref, g_ref, u_ref, loss_ref):
        @pl.when(pl.program_id(0) == 0)
        def _():
            loss_ref[...] = jnp.zeros_like(loss_ref)

        xv = (x_ref[...] + _dot(c_ref[...], wo_ref[0:2].reshape(POOL_W, D), 1, 0)
              + _dot(d_ref[...], wo_ref[2:8].reshape(HEADS * V_DIM, D), 1, 0))
        x3_ref[...] = xv
        h, _ = _rms(xv, gam_ref[...])
        hb = h.astype(BF16)
        g = _dot(hb, wg_ref[...].reshape(D_FF, D), 1, 1)
        u = _dot(hb, wu_ref[...].reshape(D_FF, D), 1, 1)
        g_ref[...] = g.astype(BF16)
        u_ref[...] = u.astype(BF16)
        act = g * jax.nn.sigmoid(g) * u
        err = xv + _dot(act.astype(BF16), wd_ref[...].reshape(D_FF, D), 1, 0) - t_ref[...]
        dy_ref[...] = err * (1.0 / D)
        sq = jnp.sum(jnp.sum(err * err, axis=-1, keepdims=True), axis=0, keepdims=True)
        loss_ref[...] += (0.5 / D) * sq

    def row(w):
        return pl.BlockSpec((TB, w), lambda i: (i, 0))

    res, _ = _call(
        body, "last_block_fwd", (t // TB,),
        [row(D), row(POOL_W), row(HEADS * V_DIM), _wspec(N_SQ, OFF_OOUT), _const_spec((1, D)),
         _wspec(N_FF, OFF_GATE), _wspec(N_FF, OFF_UP), _wspec(N_FF, 0), row(D)],
        [row(D), row(D), row(D_FF), row(D_FF), _const_spec((8, 128))],
        [jax.ShapeDtypeStruct((t, D), F32), jax.ShapeDtypeStruct((t, D), F32), jax.ShapeDtypeStruct((t, D_FF), BF16),
         jax.ShapeDtypeStruct((t, D_FF), BF16), jax.ShapeDtypeStruct((8, 128), F32)],
        (x, c_out, d_out, w_mix1, gamma, w_gu, w_gu, w_d, target))
    return res


def _ffn_up(x, w_gu, gamma, name, comm=None):
    t = x.shape[0]

    def body(x_ref, gam_ref, wg_ref, wu_ref, g_ref, u_ref, act_ref):
        h, _ = _rms(x_ref[...], gam_ref[...])
        hb = h.astype(BF16)
        g = _dot(hb, wg_ref[...].reshape(D_FF, D), 1, 1)
        u = _dot(hb, wu_ref[...].reshape(D_FF, D), 1, 1)
        g_ref[...] = g.astype(BF16)
        u_ref[...] = u.astype(BF16)
        act_ref[...] = (g * jax.nn.sigmoid(g) * u).astype(BF16)

    row = pl.BlockSpec((TB, D), lambda i: (i, 0))
    wide = pl.BlockSpec((TB, D_FF), lambda i: (i, 0))
    return _call(body, name, (t // TB,), [row, _const_spec((1, D)), _wspec(N_FF, OFF_GATE), _wspec(N_FF, OFF_UP)],
                 [wide, wide, wide], [jax.ShapeDtypeStruct((t, D_FF), BF16)] * 3, (x, gamma, w_gu, w_gu), (), comm)


def _ffn_down(x, act, w_d, name, comm=None):
    t = x.shape[0]

    def body(x_ref, a_ref, wd_ref, y_ref):
        y_ref[...] = x_ref[...] + _dot(a_ref[...], wd_ref[...].reshape(D_FF, D), 1, 0)

    row = pl.BlockSpec((TB, D), lambda i: (i, 0))
    wide = pl.BlockSpec((TB, D_FF), lambda i: (i, 0))
    return _call(body, name, (t // TB,), [row, wide, _wspec(N_FF, 0)], [row], [jax.ShapeDtypeStruct((t, D), F32)],
                 (x, act, w_d), (), comm)


def _ffn_bwd(x, g, u, dy, w_gu, w_d, gamma, name, comm=None, w_mix1=None):
    t = x.shape[0]
    with_dmix = w_mix1 is not None

    def body(*refs):
        x_ref, g_ref, u_ref, dy_ref, gam_ref, wg_ref, wu_ref, wd_ref = refs[:8]
        dx_ref, act_ref, dg_ref, du_ref, h_ref, dgam_ref = refs[8 + with_dmix:14 + with_dmix]

        @pl.when(pl.program_id(0) == 0)
        def _():
            dgam_ref[...] = jnp.zeros_like(dgam_ref)

        xv = x_ref[...]
        gam = gam_ref[...]
        h, r = _rms(xv, gam)
        h_ref[...] = h.astype(BF16)
        dyv = dy_ref[...]
        dact = _dot(dyv.astype(BF16), wd_ref[...].reshape(D_FF, D), 1, 1)
        gv = g_ref[...].astype(F32)
        uv = u_ref[...].astype(F32)
        sg = jax.nn.sigmoid(gv)
        silu = gv * sg
        act_ref[...] = (silu * uv).astype(BF16)
        dgb = (dact * uv * (sg * (1.0 + gv * (1.0 - sg)))).astype(BF16)
        dub = (dact * silu).astype(BF16)
        dg_ref[...] = dgb
        du_ref[...] = dub
        dh = _dot(dgb, wg_ref[...].reshape(D_FF, D), 1, 0) + _dot(dub, wu_ref[...].reshape(D_FF, D), 1, 0)
        dxn, dgam = _rms_bwd(xv, r, gam, dh)
        dgam_ref[...] += dgam
        dx = dyv + dxn
        dx_ref[...] = dx
        if with_dmix:
            refs[15][...] = _dot(dx.astype(BF16), refs[8][...].reshape(D, D), 1, 1).astype(BF16)

    row = pl.BlockSpec((TB_FFN_BWD, D), lambda i: (i, 0))
    wide = pl.BlockSpec((TB_FFN_BWD, D_FF), lambda i: (i, 0))
    in_specs = [row, wide, wide, row, _const_spec((1, D)), _wspec(N_FF, OFF_GATE), _wspec(N_FF, OFF_UP),
                _wspec(N_FF, 0)]
    out_specs = [row, wide, wide, wide, row, _const_spec((1, D))]
    out_shape = [jax.ShapeDtypeStruct((t, D), F32), jax.ShapeDtypeStruct((t, D_FF), BF16),
                 jax.ShapeDtypeStruct((t, D_FF), BF16), jax.ShapeDtypeStruct((t, D_FF), BF16),
                 jax.ShapeDtypeStruct((t, D), BF16), jax.ShapeDtypeStruct((1, D), F32)]
    args = (x, g, u, dy, gamma, w_gu, w_gu, w_d)
    if with_dmix:
        in_specs, args = in_specs + [_wspec(N_SQ, OFF_OOUT)], args + (w_mix1,)
        out_specs, out_shape = out_specs + [row], out_shape + [jax.ShapeDtypeStruct((t, D), BF16)]
    return _call(body, name, (t // TB_FFN_BWD,), in_specs, out_specs, out_shape, args, (), comm)


def _odd_pre_fwd(x, wg, gamma, qbt, kvbt, qa_g, kva_g, pw_bd, pscale, seq, comm=None):
    t = x.shape[0]
    nbs = seq // TB

    def body(x_ref, gam_ref, win_ref, qb_ref, kvb_ref, qa_ref, kva_ref, pw_ref, ps_ref,
             proj_ref, q_ref, kv_ref, kr_ref, c_ref, tail_ref):
        i = pl.program_id(0)
        h, _ = _rms(x_ref[...], gam_ref[...])
        proj = _dot(h.astype(BF16), win_ref[...].reshape(D, D), 1, 0)
        proj_ref[...] = proj.astype(BF16)
        zp, ql, kvl = proj[:, :POOL_W], proj[:, 256:640], proj[:, 640:896]
        kr_ref[...] = proj[:, 896:1024]
        qn, _ = _rms(ql, qa_ref[...])
        q_ref[...] = _dot(qn.astype(BF16), qb_ref[...], 1, 1).astype(BF16)
        kvn, _ = _rms(kvl, kva_ref[...])
        kv_ref[...] = _dot(kvn.astype(BF16), kvb_ref[...], 1, 1).astype(BF16)
        tail = jnp.where(i % nbs == 0, 0.0, tail_ref[...])
        pooled, _, _ = _pool_fwd(zp, tail, i % nbs)
        tail_ref[...] = zp[TB - HALO:, :]
        c_ref[...] = (_dot(pooled.astype(BF16), pw_ref[...], 1, 0) * ps_ref[...]).astype(BF16)

    def row(w):
        return pl.BlockSpec((TB, w), lambda i: (i, 0))

    return _call(
        body, "odd_pre_fwd", (t // TB,),
        [row(D), _const_spec((1, D)), _wspec(N_SQ, OFF_OIN), _const_spec((HEADS * HP, Q_LORA)),
         _const_spec((HEADS * HP, KV_LORA)), _const_spec((1, Q_LORA)), _const_spec((1, KV_LORA)),
         _const_spec((POOL_W, POOL_W)), _const_spec((1, POOL_W))],
        [row(D), row(HEADS * HP), row(HEADS * HP), row(128), row(POOL_W)],
        [jax.ShapeDtypeStruct((t, D), BF16), jax.ShapeDtypeStruct((t, HEADS * HP), BF16),
         jax.ShapeDtypeStruct((t, HEADS * HP), BF16), jax.ShapeDtypeStruct((t, 128), F32),
         jax.ShapeDtypeStruct((t, POOL_W), BF16)],
        (x, gamma, wg, qbt, kvbt, qa_g, kva_g, pw_bd, pscale), [pltpu.VMEM((HALO, POOL_W), F32)], comm)


def _odd_pre_bwd(x, proj, dx3, dmix, dq, dkv, dkr, wg, gamma, qbt, kvbt, qa_g, kva_g, pw_bd, pscale, seq):
    t = x.shape[0]
    nb, nbs = t // TB, seq // TB

    def body(x_ref, proj_ref, ptail_ref, dx3_ref, dco_ref, dq_ref, dkv_ref, dkr_ref, gam_ref, win_ref, qb_ref,
             kvb_ref, qa_ref, kva_ref, pw_ref, ps_ref,
             dx2_ref, dproj_ref, h_ref, qn_ref, kvn_ref, dgam_ref, dqa_ref, dkva_ref, dpw_ref, dps_ref, head_ref):
        i = pl.program_id(0)
        blk = nb - 1 - i

        @pl.when(i == 0)
        def _():
            dgam_ref[...] = jnp.zeros_like(dgam_ref)
            dqa_ref[...] = jnp.zeros_like(dqa_ref)
            dkva_ref[...] = jnp.zeros_like(dkva_ref)
            dpw_ref[...] = jnp.zeros_like(dpw_ref)
            dps_ref[...] = jnp.zeros_like(dps_ref)

        xv = x_ref[...]
        gam = gam_ref[...]
        h, r = _rms(xv, gam)
        h_ref[...] = h.astype(BF16)
        proj = proj_ref[...].astype(F32)
        zp, ql, kvl = proj[:, :POOL_W], proj[:, 256:640], proj[:, 640:896]

        qa = qa_ref[...]
        qn, rq = _rms(ql, qa)
        qn_ref[...] = qn.astype(BF16)
        dql, dqa = _rms_bwd(ql, rq, qa, _dot(dq_ref[...], qb_ref[...], 1, 0))
        dqa_ref[...] += dqa
        kva = kva_ref[...]
        kvn, rkv = _rms(kvl, kva)
        kvn_ref[...] = kvn.astype(BF16)
        dkvl, dkva = _rms_bwd(kvl, rkv, kva, _dot(dkv_ref[...], kvb_ref[...], 1, 0))
        dkva_ref[...] += dkva

        pt = ptail_ref[...].astype(F32)
        tail = jnp.where(blk % nbs == 0, 0.0, pt[:, :POOL_W])
        pooled, cnt, grp = _pool_fwd(zp, tail, blk % nbs)
        pb = pooled.astype(BF16)
        pw = pw_ref[...]
        dco = dco_ref[...].astype(F32)
        dps_ref[...] += jnp.sum(dco * _dot(pb, pw, 1, 0), axis=0, keepdims=True)
        dpo = (dco * ps_ref[...]).astype(BF16)
        dpw_ref[...] += _dot(pb, dpo, 0, 0)
        dpooled = _dot(dpo, pw, 1, 1)
        dpm = dpooled / cnt
        head = jnp.where(blk % nbs == nbs - 1, 0.0, head_ref[...])
        dz = _pool_bwd(dpooled, dpm, head, grp)
        head_ref[...] = dpm[:HALO, :]

        dproj_ref[:, :POOL_W] = dz.astype(BF16)
        dproj_ref[:, 256:640] = dql.astype(BF16)
        dproj_ref[:, 640:896] = dkvl.astype(BF16)
        dproj_ref[:, 896:1024] = dkr_ref[...].astype(BF16)
        dh = _dot(dproj_ref[...], win_ref[...].reshape(D, D), 1, 1)
        dxn, dgam = _rms_bwd(xv, r, gam, dh)
        dgam_ref[...] += dgam
        dx2_ref[...] = dx3_ref[...] + dxn

    def rev(w):
        return pl.BlockSpec((TB, w), lambda i: (nb - 1 - i, 0))

    ptail = pl.BlockSpec((HALO, D), lambda i: (jnp.maximum((nb - 1 - i) * (TB // HALO) - 1, 0), 0))
    return pl.pallas_call(
        body, name="odd_pre_bwd",
        out_shape=[jax.ShapeDtypeStruct((t, D), F32), jax.ShapeDtypeStruct((t, D), BF16),
                   jax.ShapeDtypeStruct((t, D), BF16), jax.ShapeDtypeStruct((t, Q_LORA), BF16),
                   jax.ShapeDtypeStruct((t, KV_LORA), BF16), jax.ShapeDtypeStruct((1, D), F32),
                   jax.ShapeDtypeStruct((1, Q_LORA), F32), jax.ShapeDtypeStruct((1, KV_LORA), F32),
                   jax.ShapeDtypeStruct((POOL_W, POOL_W), F32), jax.ShapeDtypeStruct((1, POOL_W), F32)],
        grid=(nb,),
        in_specs=[rev(D), rev(D), ptail, rev(D), rev(POOL_W), rev(HEADS * HP), rev(HEADS * HP), rev(128),
                  _const_spec((1, D)), _wspec(N_SQ, OFF_OIN), _const_spec((HEADS * HP, Q_LORA)),
                  _const_spec((HEADS * HP, KV_LORA)), _const_spec((1, Q_LORA)), _const_spec((1, KV_LORA)),
                  _const_spec((POOL_W, POOL_W)), _const_spec((1, POOL_W))],
        out_specs=[rev(D), rev(D), rev(D), rev(Q_LORA), rev(KV_LORA), _const_spec((1, D)), _const_spec((1, Q_LORA)),
                   _const_spec((1, KV_LORA)), _const_spec((POOL_W, POOL_W)), _const_spec((1, POOL_W))],
        scratch_shapes=[pltpu.VMEM((HALO, POOL_W), F32)],
        compiler_params=_cparams(1),
    )(x, proj, proj, dx3, dmix, dq, dkv, dkr, gamma, wg, qbt, kvbt, qa_g, kva_g, pw_bd, pscale)


def _attn_specs(seq):
    head = pl.BlockSpec((seq, HP), lambda b, h: (b, h))
    shared = pl.BlockSpec((seq, 128), lambda b, h: (b, 0))
    gain = pl.BlockSpec((1, HP), lambda b, h: (0, 0))
    return head, shared, gain


def _causal_bias(n):
    rows = lax.broadcasted_iota(jnp.int32, (n, n), 0)
    cols = lax.broadcasted_iota(jnp.int32, (n, n), 1)
    return jnp.where(cols <= rows, 0.0, NEG_INF)


def _attn_fwd(q, kv, kr, cos, sin, gq, gk, seq, comm=None):
    t = q.shape[0]
    qb = min(512, seq)

    def body(q_ref, kv_ref, kr_ref, c_ref, s_ref, gq_ref, gk_ref, o_ref, lse_ref):
        c, s = c_ref[...], s_ref[...]
        qf, _ = _qk_prep(q_ref[...].astype(F32), gq_ref[...], c, s)
        kin = jnp.concatenate([kv_ref[:, :128].astype(F32), kr_ref[...]], axis=1)
        kf, _ = _qk_prep(kin, gk_ref[...], c, s)
        qf, kf = qf.astype(BF16), kf.astype(BF16)
        v1 = jnp.concatenate([kv_ref[:, 128:], jnp.ones((seq, V_DIM), BF16)], axis=1)
        bias = _causal_bias(qb)
        for q0 in range(0, seq, qb):
            q1 = q0 + qb
            qblk = qf[q0:q1]
            s_dg = _dot(qblk, kf[q0:q1], 1, 1) + bias
            m = jnp.max(s_dg, axis=-1, keepdims=True)
            if q0:
                s_off = _dot(qblk, kf[:q0], 1, 1)
                m = jnp.maximum(m, jnp.max(s_off, axis=-1, keepdims=True))
            acc = _dot(jnp.exp(s_dg - m).astype(BF16), v1[q0:q1], 1, 0)
            if q0:
                acc = acc + _dot(jnp.exp(s_off - m).astype(BF16), v1[:q0], 1, 0)
            l = acc[:, V_DIM:]
            o_ref[q0:q1, :] = (acc[:, :V_DIM] / l).astype(BF16)
            lse_ref[q0:q1, :] = m + jnp.log(l)

    head, shared, gain = _attn_specs(seq)
    per_head = pl.BlockSpec((seq, V_DIM), lambda b, h: (b, h))
    return _call(
        body, "attn_fwd", (t // seq, HEADS),
        [head, head, shared, shared, shared, gain, gain], [per_head, per_head],
        [jax.ShapeDtypeStruct((t, HEADS * V_DIM), BF16), jax.ShapeDtypeStruct((t, HEADS * V_DIM), F32)],
        (q, kv, kr, cos, sin, gq, gk), (), comm)


def _attn_bwd(q, kv, kr, cos, sin, gq, gk, dmix, d_out, lse, seq, comm=None):
    t = q.shape[0]
    qb = min(512, seq)

    def body(q_ref, kv_ref, kr_ref, c_ref, s_ref, gq_ref, gk_ref, do_ref, o_ref, lse_ref,
             dq_ref, dkv_ref, dkr_ref, dgq_ref, dgk_ref, dqf_ref, dkf_ref, dv_ref):
        b, hd = pl.program_id(0), pl.program_id(1)

        @pl.when((b == 0) & (hd == 0))
        def _():
            dgq_ref[...] = jnp.zeros_like(dgq_ref)
            dgk_ref[...] = jnp.zeros_like(dgk_ref)

        c, sn = c_ref[...], s_ref[...]
        gq_v, gk_v = gq_ref[...], gk_ref[...]
        qin = q_ref[...].astype(F32)
        kin = jnp.concatenate([kv_ref[:, :128].astype(F32), kr_ref[...]], axis=1)
        qf32, rq = _qk_prep(qin, gq_v, c, sn)
        kf32, rk = _qk_prep(kin, gk_v, c, sn)
        qf, kf = qf32.astype(BF16), kf32.astype(BF16)
        vb = kv_ref[:, 128:]
        dkf_ref[...] = jnp.zeros_like(dkf_ref)
        dv_ref[...] = jnp.zeros_like(dv_ref)
        bias = _causal_bias(qb)
        for q0 in range(0, seq, qb):
            q1 = q0 + qb
            qblk = qf[q0:q1]
            do = do_ref[q0:q1, :]
            lse_col = lse_ref[q0:q1, 0:1]
            d_col = jnp.sum(do.astype(F32) * o_ref[q0:q1, :].astype(F32), axis=-1, keepdims=True)
            dq_acc = None
            for k0, k1, diag in ((q0, q1, True), (0, q0, False)):
                if k1 == k0:
                    continue
                s = _dot(qblk, kf[k0:k1], 1, 1)
                p = jnp.exp((s + bias if diag else s) - lse_col)
                dv_ref[k0:k1, :] += _dot(p.astype(BF16), do, 0, 0)
                ds = (p * (_dot(do, vb[k0:k1], 1, 1) - d_col)).astype(BF16)
                part = _dot(ds, kf[k0:k1], 1, 0)
                dq_acc = part if dq_acc is None else dq_acc + part
                dkf_ref[k0:k1, :] += _dot(ds, qblk, 0, 0)
            dqf_ref[q0:q1, :] = dq_acc
        dqin, dgq = _qk_prep_bwd(dqf_ref[...], qin, rq, gq_v, c, sn)
        dkin, dgk = _qk_prep_bwd(dkf_ref[...], kin, rk, gk_v, c, sn)
        dgq_ref[...] += dgq
        dgk_ref[...] += dgk
        dq_ref[...] = dqin.astype(BF16)
        dkv_ref[:, :128] = dkin[:, :128].astype(BF16)
        dkv_ref[:, 128:] = dv_ref[...].astype(BF16)

        @pl.when(hd == 0)
        def _():
            dkr_ref[...] = dkin[:, 128:]

        @pl.when(hd != 0)
        def _():
            dkr_ref[...] += dkin[:, 128:]

    head, shared, gain = _attn_specs(seq)
    per_head = pl.BlockSpec((seq, V_DIM), lambda b, h: (b, h))
    return _call(
        body, "attn_bwd", (t // seq, HEADS),
        [head, head, shared, shared, shared, gain, gain,
         pl.BlockSpec((seq, V_DIM), lambda b, h: (b, 2 + h)), per_head, per_head],
        [head, head, shared, gain, gain],
        [jax.ShapeDtypeStruct((t, HEADS * HP), BF16), jax.ShapeDtypeStruct((t, HEADS * HP), BF16),
         jax.ShapeDtypeStruct((t, 128), F32), jax.ShapeDtypeStruct((1, HP), F32),
         jax.ShapeDtypeStruct((1, HP), F32)],
        (q, kv, kr, cos, sin, gq, gk, dmix, d_out, lse),
        [pltpu.VMEM((seq, HP), F32), pltpu.VMEM((seq, HP), F32), pltpu.VMEM((seq, V_DIM), F32)], comm)


def _tn(a_list, b, tm, name, into=None, comm=None, after=None):
    t, n_out = b.shape
    widths = [a.shape[1] for a in a_list]
    tk = min(TK_DW, t)
    m, na, nk = sum(widths), len(a_list), t // tk
    assert na == 1 or tm == m

    def body(*refs):
        a_refs, b_ref, o_ref, acc_ref = refs[:na], refs[na], refs[-2], refs[-1]
        k = pl.program_id(1)

        @pl.when(k == 0)
        def _():
            acc_ref[...] = jnp.zeros_like(acc_ref)

        bb = b_ref[...].astype(BF16)
        m0 = 0
        for a_ref, w in zip(a_refs, widths):
            rows = slice(0, tm) if na == 1 else slice(m0, m0 + w)
            acc_ref[rows, :] += _dot(a_ref[...].astype(BF16), bb, 0, 0)
            m0 += w

        @pl.when(k == nk - 1)
        def _():
            o_ref[...] = acc_ref[...].astype(BF16).reshape(o_ref.shape)

    if na == 1:
        in_specs = [pl.BlockSpec((tk, tm), lambda i, k: (k, i))]
    else:
        in_specs = [pl.BlockSpec((tk, w), lambda i, k: (k, 0)) for w in widths]
    in_specs.append(pl.BlockSpec((tk, n_out), lambda i, k: (k, 0)))
    args = list(a_list) + [b]
    if into is None:
        out_spec = pl.BlockSpec((tm, n_out), lambda i, k: (i, 0))
        out_shape = jax.ShapeDtypeStruct((m, n_out), BF16)
        aliases = {}
    else:
        buf, n, off = into
        assert n_out == D and tm % n == 0 and off % n == 0 and (na == 1 or tm // n == N_DEV)
        idx = off // n
        out_spec = pl.BlockSpec((tm // n, n, D), lambda i, k: (i, idx, 0))
        out_shape = jax.ShapeDtypeStruct(buf.shape, BF16)
        in_specs.append(pl.BlockSpec(memory_space=pl.ANY))
        args.append(buf)
        aliases = {len(args) - 1: 0}
    if after is not None:
        in_specs.append(pl.BlockSpec(memory_space=pl.ANY))
        args.append(after)
    (res,), extra = _call(body, name, (m // tm, nk), in_specs, [out_spec], [out_shape], args,
                          [pltpu.VMEM((tm, n_out), F32)], comm, aliases)
    return (res, extra) if comm is not None else res


def _adamw(ws, gs, ms, vs, name, nblk=1):
    n = len(ws)
    c1 = 1.0 - B1 ** STEP
    c2 = 1.0 - B2 ** STEP

    def body(*refs):
        for a in range(n):
            w, g, m, v = (refs[k * n + a][...] for k in range(4))
            d_ref, m_ref, v_ref = (refs[(4 + k) * n + a] for k in range(3))
            m_new = B1 * m + (1.0 - B1) * g
            v_new = B2 * v + (1.0 - B2) * (g * g)
            d_ref[...] = -LR * ((m_new / c1) / (jnp.sqrt(v_new / c2) + ADAM_EPS) + WD * w)
            m_ref[...] = m_new
            v_ref[...] = v_new

    grid = (nblk,)
    assert all(w.shape[0] % nblk == 0 and (nblk == 1 or (w.shape[0] // nblk) % 8 == 0) for w in ws)
    specs = [pl.BlockSpec((w.shape[0] // nblk, w.shape[1]), lambda i: (i, 0)) for w in ws]
    outs, _ = _call(body, name, grid, specs * 4, specs * 3, [jax.ShapeDtypeStruct(w.shape, F32) for w in ws] * 3,
                    (*ws, *gs, *ms, *vs))
    return outs[:n], outs[n:2 * n], outs[2 * n:]


def _rows1024(a, rows):
    flat = a.reshape(-1, D)
    return jnp.pad(flat, ((0, rows - flat.shape[0]), (0, 0)))


def _pack_shards(even_w_in, even_w_out, odd_w_in, q_b, kv_b, odd_w_out, ffn_w_gate, ffn_w_up, ffn_w_down):
    mix0 = jnp.concatenate([even_w_in[0].T, jnp.zeros((OFF_EOUT - N_EIN, D), F32), even_w_out[0]], axis=0)
    gu = [jnp.concatenate([ffn_w_gate[layer].T, ffn_w_up[layer].T], axis=0) for layer in range(2)]
    mix1 = jnp.concatenate([jnp.pad(odd_w_in[0], ((0, 0), (0, D - ODD_IN))), odd_w_out[0],
                            _rows1024(q_b[0].T, N_QB), _rows1024(kv_b[0].T, N_KVB),
                            jnp.zeros((R_MIX1 - OFF_KVB - N_KVB, D), F32)], axis=0)
    return [c.astype(BF16) for c in (mix0, gu[0], ffn_w_down[0], mix1, gu[1], ffn_w_down[1])]


def _pad_heads(a):
    k = a.shape[1]
    return jnp.pad(a.reshape(HEADS, QK_DIM, k), ((0, 0), (0, HP - QK_DIM), (0, 0))).reshape(HEADS * HP, k)


def _small_pack(parts):
    flat = []
    for p in parts:
        v = p.reshape(-1)
        flat.append(jnp.pad(v, (0, (-v.shape[0]) % 1024)))
    return jnp.concatenate(flat).reshape(-1, 128)


def _small_unpack(buf, shapes):
    flat = buf.reshape(-1)
    out, off = [], 0
    for s in shapes:
        size = int(np.prod(s))
        out.append(flat[off:off + size].reshape(s))
        off += size + (-size) % 1024
    return out


def _step(x3d, positions, target3d, chunks, tile, where, mix_norm, ffn_norm, sg_ln_g, sg_w_s, sg_b_s,
          pool_w, q_norm, k_norm):
    bsz, seq, _ = x3d.shape
    t = bsz * seq
    x0 = x3d.reshape(t, D)
    target = target3d.reshape(t, D)
    my_mix0, my_gu0, my_d0, my_mix1, my_gu1, my_d1 = chunks

    lane = np.arange(128)
    inv_freq = np.where(lane < QK_ROPE, ROPE_THETA ** (-(2.0 * (lane % 32)) / QK_ROPE), 0.0)
    inv_freq = jnp.asarray(inv_freq.reshape(1, 128), F32)
    (cos, sin), (w_mix0, tiles) = _rope_tables(positions.reshape(t, 1), inv_freq, _gather_comm([my_mix0, tile]))

    conv_w = tiles[:, 0:3, 0:64].transpose(1, 0, 2).reshape(3, SC_W)
    pool_scale = tiles[:, 3, 0:32].reshape(1, POOL_W)
    q_a_norm = tiles[:, 4, 0:48].reshape(1, Q_LORA)
    kv_a_norm = tiles[:, 5, 0:32].reshape(1, KV_LORA)
    ws = sg_w_s[0]
    bst = jnp.pad(sg_b_s[0].T, ((0, 0), (0, 128 - SG_HEADS)))
    cw = jnp.pad(conv_w, ((0, 8 - 3), (0, 0)))
    pw_bd = jax.scipy.linalg.block_diag(*[pool_w[0, g] for g in range(4)]).astype(BF16)
    gq = jnp.pad(q_norm * ATT_SCALE, ((0, 0), (0, HP - QK_DIM)))
    gk = jnp.pad(k_norm, ((0, 0), (0, HP - QK_DIM)))

    (x1, proj_e), (w_gu0,) = _even_fwd(x0, w_mix0, mix_norm[0:1], sg_ln_g, ws, bst, cw, seq, _gather_comm([my_gu0]))
    (g0, u0, act0), (w_d0, w_mix1) = _ffn_up(x1, w_gu0, ffn_norm[0:1], "ffn_up0", _gather_comm([my_d0, my_mix1]))
    (x2,), (w_d1,) = _ffn_down(x1, act0, w_d0, "ffn_down0", _gather_comm([my_d1]))
    qbt = _pad_heads(w_mix1[:, OFF_QB:OFF_QB + N_QB_USED, :].reshape(HEADS * QK_DIM, Q_LORA))
    kvbt = w_mix1[:, OFF_KVB:OFF_KVB + N_KVB, :].reshape(HEADS * HP, KV_LORA)
    (proj_o, q, kv, kr, c_out), _ = _odd_pre_fwd(x2, w_mix1, mix_norm[1:2], qbt, kvbt, q_a_norm, kv_a_norm,
                                                pw_bd, pool_scale, seq)
    (d_out, lse), (w_gu1,) = _attn_fwd(q, kv, kr, cos, sin, gq, gk, seq, _gather_comm([my_gu1]))
    x3, dy, g1, u1, loss_tile = _last_block_fwd(x2, c_out, d_out, w_mix1, w_gu1, w_d1, ffn_norm[1:2], target)

    def chunk(rows, padded=False):
        return jnp.zeros((N_DEV, rows, D), BF16) if padded else lax.empty((N_DEV, rows, D), BF16)

    (dx3, act1, dg1, du1, h3, dgam_f1, dmix_o), _ = _ffn_bwd(x3, g1, u1, dy, w_gu1, w_d1, ffn_norm[1:2], "ffn_bwd1",
                                                           None, w_mix1)
    gp_ffn1 = _tn([dg1], h3, 1408, "dw_gate1", (chunk(R_GU + N_FF), N_FF, OFF_GATE))
    gp_ffn1 = _tn([du1], h3, 1408, "dw_up1", (gp_ffn1, N_FF, OFF_UP))
    gp_ffn1 = _tn([act1], dy, 1408, "dw_down1", (gp_ffn1, N_FF, R_GU))

    gp_mix1, (ga_ffn1,) = _tn([c_out, d_out], dx3, D, "dw_oout", (chunk(R_MIX1, True), N_SQ, OFF_OOUT),
                              _pair_exchange_comm(gp_ffn1))
    pb_ffn1 = _rs_pair_sum(gp_ffn1, ga_ffn1, where, "rs_pair_sum_ffn1")
    (dq, dkv, dkr, dgq, dgk), (gb_ffn1,) = _attn_bwd(q, kv, kr, cos, sin, gq, gk, dmix_o, d_out, lse, seq,
                                                    _chip_exchange_comm(pb_ffn1))
    (dx2, dproj_o, h2, qn, kvn, dgam_m1, dqa, dkva, dpw_bd, dps) = _odd_pre_bwd(
        x2, proj_o, dx3, dmix_o, dq, dkv, dkr, w_mix1, mix_norm[1:2], qbt, kvbt, q_a_norm, kv_a_norm, pw_bd,
        pool_scale, seq)
    gp_mix1 = _tn([h2], dproj_o, D, "dw_oin", (gp_mix1, N_SQ, OFF_OIN))
    d_qbt = _tn([dq], qn, HEADS * HP, "dw_qb")
    d_qb_rows = d_qbt.reshape(HEADS, HP, Q_LORA)[:, :QK_DIM].reshape(N_DEV, N_QB_USED, D)
    d_kvb_rows = _tn([dkv], kvn, HEADS * HP, "dw_kvb").reshape(N_DEV, N_KVB, D)
    gp_mix1 = lax.dynamic_update_slice(gp_mix1, d_qb_rows, (0, OFF_QB, 0))
    gp_mix1 = lax.dynamic_update_slice(gp_mix1, d_kvb_rows, (0, OFF_KVB, 0))

    (dx1, act0, dg0, du0, h1, dgam_f0), (ga_mix1,) = _ffn_bwd(x1, g0, u0, dx2, w_gu0, w_d0, ffn_norm[0:1], "ffn_bwd0",
                                                             _pair_exchange_comm(gp_mix1))
    pb_mix1 = _rs_pair_sum(gp_mix1, ga_mix1, where, "rs_pair_sum_mix1")
    *open_mix1, started = _chip_exchange_start(pb_mix1, "mix1")
    gp_ffn0a = _tn([dg0], h1, 1408, "dw_gate0", (chunk(R_GU), N_FF, OFF_GATE), None, started)
    gp_ffn0a = _tn([du0], h1, 1408, "dw_up0", (gp_ffn0a, N_FF, OFF_UP))
    gp_ffn0b, (ga_ffn0a,) = _tn([act0], dx2, 1408, "dw_down0", (chunk(N_FF), N_FF, 0),
                                _pair_exchange_comm(gp_ffn0a))
    pb_ffn0a = _rs_pair_sum(gp_ffn0a, ga_ffn0a, where, "rs_pair_sum_ffn0a")
    *open_ffn0a, started = _chip_exchange_start(pb_ffn0a, "ffn0a")

    (dx0, dproj_e, mix_e, h0, dgam_m0, dws, dbc, dlng, dcw), _ = _even_bwd(
        x0, proj_e, dx1, w_mix0, mix_norm[0:1], sg_ln_g, ws, bst, cw + 0.0 * started[:, :1], seq)

    small = _small_pack([
        jnp.concatenate([dgam_m0, dgam_m1], 0), jnp.concatenate([dgam_f0, dgam_f1], 0), dlng,
        dws[None], dbc[:, :SG_HEADS].T[None], dcw[:3],
        jnp.stack([dpw_bd[g * POOL_GD:(g + 1) * POOL_GD, g * POOL_GD:(g + 1) * POOL_GD] for g in range(4)])[None],
        dps, dqa, dkva, dgq[:, :QK_DIM] * ATT_SCALE, dgk[:, :QK_DIM], loss_tile[0:1, 0:1]])
    gp_mix0, (small_all, ga_ffn0b) = _tn([mix_e], dx1, D, "dw_eout", (chunk(R_MIX0, True), N_SQ, OFF_EOUT),
                                         _both(_gather_comm([small]), _pair_exchange_comm(gp_ffn0b)))
    pb_ffn0b = _rs_pair_sum(gp_ffn0b, ga_ffn0b, where, "rs_pair_sum_ffn0b")
    *open_ffn0b, started = _chip_exchange_start(pb_ffn0b, "ffn0b")
    gp_mix0 = _tn([dproj_e], h0, 1280, "dw_ein", (gp_mix0, N_EIN, OFF_EIN), None, started)
    small_sum = _small_unpack(_sum_gathered(small_all), SMALL_SHAPES)
    return dx0.reshape(bsz, seq, D), (pb_ffn1, gb_ffn1), (open_ffn0a, open_ffn0b, open_mix1), gp_mix0, small_sum


SMALL_SHAPES = [(2, D), (2, D), (1, SG_W), (1, SG_HEADS, 128, 128), (1, SG_HEADS, 128), (3, SC_W),
                (1, 4, POOL_GD, POOL_GD), (1, POOL_W), (1, Q_LORA), (1, KV_LORA), (1, QK_DIM), (1, QK_DIM), (1, 1)]


def kernel(x, positions, mix_norm, ffn_norm, even_w_in, sg_ln_g, sg_w_s, sg_b_s, sc_conv_w, even_w_out, odd_w_in, pool_w, pool_scale, q_a_norm, q_b, kv_a_norm, kv_b, q_norm, k_norm, odd_w_out, ffn_w_gate, ffn_w_up, ffn_w_down, loss_target, m_mix_norm, m_ffn_norm, m_even_w_in, m_sg_ln_g, m_sg_w_s, m_sg_b_s, m_sc_conv_w, m_even_w_out, m_odd_w_in, m_pool_w, m_pool_scale, m_q_a_norm, m_q_b, m_kv_a_norm, m_kv_b, m_q_norm, m_k_norm, m_odd_w_out, m_ffn_w_gate, m_ffn_w_up, m_ffn_w_down, v_mix_norm, v_ffn_norm, v_even_w_in, v_sg_ln_g, v_sg_w_s, v_sg_b_s, v_sc_conv_w, v_even_w_out, v_odd_w_in, v_pool_w, v_pool_scale, v_q_a_norm, v_q_b, v_kv_a_norm, v_kv_b, v_q_norm, v_k_norm, v_odd_w_out, v_ffn_w_gate, v_ffn_w_up, v_ffn_w_down):
    xi, yi, ci = _place()
    me = 4 * xi + 2 * yi + ci

    chunks = _pack_shards(even_w_in, even_w_out, odd_w_in, q_b, kv_b, odd_w_out, ffn_w_gate, ffn_w_up, ffn_w_down)

    def lane_pad(a):
        return jnp.pad(a, ((0, 0), (0, 128 - a.shape[1])))

    tile = jnp.concatenate([lane_pad(sc_conv_w[0]), lane_pad(pool_scale), lane_pad(q_a_norm), lane_pad(kv_a_norm),
                            jnp.zeros((2, 128), F32)], axis=0)
    chip = 2 * xi + yi
    where = jnp.stack([ci, chip, chip ^ 2, chip ^ 1, chip ^ 3]).astype(jnp.int32)
    grad_x, (pb_ffn1, gb_ffn1), in_flight, gp_mix0, tot = _step(
        x, positions, loss_target, chunks, tile, where, mix_norm, ffn_norm, sg_ln_g, sg_w_s, sg_b_s,
        pool_w, q_norm, k_norm)

    (ga_mix0,) = _comm_alone(_pair_exchange_comm(gp_mix0), "rs_pair_exchange_mix0")
    pb_mix0 = _rs_pair_sum(gp_mix0, ga_mix0, where, "rs_pair_sum_mix0")
    mix0_sems, pb_mix0, land_mix0, started = _chip_exchange_start(pb_mix0, "mix0")
    landed = [_chip_exchange_wait(*parts, started, tag) for parts, tag in zip(in_flight, ("ffn0a", "ffn0b", "mix1"))]
    gsh_ffn0a, gsh_ffn0b, gsh_mix1, gsh_ffn1 = _rs_final_sums(
        [pb for pb, _ in landed] + [pb_ffn1], [gb for _, gb in landed] + [gb_ffn1], "rs_final_sums", started)

    (g_mix, g_ffn, g_lng, g_ws, g_bs, g_cw_full, g_pw, g_ps_full, g_qa_full, g_kva_full, g_qn, g_kn, loss) = tot
    g_cw = lax.dynamic_slice_in_dim(g_cw_full, me * 64, 64, axis=1)[None]
    g_ps = lax.dynamic_slice_in_dim(g_ps_full, me * 32, 32, axis=1)
    g_qa = lax.dynamic_slice_in_dim(g_qa_full, me * 48, 48, axis=1)
    g_kva = lax.dynamic_slice_in_dim(g_kva_full, me * 32, 32, axis=1)

    def tr(a):
        return jnp.swapaxes(a, -1, -2)

    g_gate = tr(jnp.stack([gsh_ffn0a[OFF_GATE:OFF_GATE + N_FF], gsh_ffn1[OFF_GATE:OFF_GATE + N_FF]]))
    g_up = tr(jnp.stack([gsh_ffn0a[OFF_UP:OFF_UP + N_FF], gsh_ffn1[OFF_UP:OFF_UP + N_FF]]))
    g_down = jnp.stack([gsh_ffn0b, gsh_ffn1[R_GU:R_GU + N_FF]])
    g_oin = gsh_mix1[OFF_OIN:OFF_OIN + N_SQ, :ODD_IN][None]
    g_oout = gsh_mix1[OFF_OOUT:OFF_OOUT + N_SQ][None]
    g_qb = tr(gsh_mix1[OFF_QB:OFF_QB + N_QB_USED].reshape(1, 144, Q_LORA))
    g_kvb = tr(gsh_mix1[OFF_KVB:OFF_KVB + N_KVB].reshape(1, 192, KV_LORA))
    transposed = ("even_w_in", "odd_w_in", "q_b", "kv_b", "ffn_w_gate", "ffn_w_up")

    names = ("mix_norm", "ffn_norm", "even_w_in", "sg_ln_g", "sg_w_s", "sg_b_s", "sc_conv_w", "even_w_out",
             "odd_w_in", "pool_w", "pool_scale", "q_a_norm", "q_b", "kv_a_norm", "kv_b", "q_norm", "k_norm",
             "odd_w_out", "ffn_w_gate", "ffn_w_up", "ffn_w_down")
    grads = dict(mix_norm=g_mix, ffn_norm=g_ffn, sg_ln_g=g_lng, sg_w_s=g_ws, sg_b_s=g_bs,
                 sc_conv_w=g_cw, odd_w_in=g_oin, pool_w=g_pw, pool_scale=g_ps, q_a_norm=g_qa,
                 q_b=g_qb, kv_a_norm=g_kva, kv_b=g_kvb, q_norm=g_qn, k_norm=g_kn, odd_w_out=g_oout,
                 ffn_w_gate=g_gate, ffn_w_up=g_up, ffn_w_down=g_down)
    weights = dict(mix_norm=mix_norm, ffn_norm=ffn_norm, even_w_in=even_w_in, sg_ln_g=sg_ln_g, sg_w_s=sg_w_s,
                   sg_b_s=sg_b_s, sc_conv_w=sc_conv_w, even_w_out=even_w_out, odd_w_in=odd_w_in, pool_w=pool_w,
                   pool_scale=pool_scale, q_a_norm=q_a_norm, q_b=q_b, kv_a_norm=kv_a_norm, kv_b=kv_b, q_norm=q_norm,
                   k_norm=k_norm, odd_w_out=odd_w_out, ffn_w_gate=ffn_w_gate, ffn_w_up=ffn_w_up,
                   ffn_w_down=ffn_w_down)
    m_in = dict(mix_norm=m_mix_norm, ffn_norm=m_ffn_norm, even_w_in=m_even_w_in, sg_ln_g=m_sg_ln_g, sg_w_s=m_sg_w_s,
                sg_b_s=m_sg_b_s, sc_conv_w=m_sc_conv_w, even_w_out=m_even_w_out, odd_w_in=m_odd_w_in,
                pool_w=m_pool_w, pool_scale=m_pool_scale, q_a_norm=m_q_a_norm, q_b=m_q_b, kv_a_norm=m_kv_a_norm,
                kv_b=m_kv_b, q_norm=m_q_norm, k_norm=m_k_norm, odd_w_out=m_odd_w_out, ffn_w_gate=m_ffn_w_gate,
                ffn_w_up=m_ffn_w_up, ffn_w_down=m_ffn_w_down)
    v_in = dict(mix_norm=v_mix_norm, ffn_norm=v_ffn_norm, even_w_in=v_even_w_in, sg_ln_g=v_sg_ln_g, sg_w_s=v_sg_w_s,
                sg_b_s=v_sg_b_s, sc_conv_w=v_sc_conv_w, even_w_out=v_even_w_out, odd_w_in=v_odd_w_in,
                pool_w=v_pool_w, pool_scale=v_pool_scale, q_a_norm=v_q_a_norm, q_b=v_q_b, kv_a_norm=v_kv_a_norm,
                kv_b=v_kv_b, q_norm=v_q_norm, k_norm=v_k_norm, odd_w_out=v_odd_w_out, ffn_w_gate=v_ffn_w_gate,
                ffn_w_up=v_ffn_w_up, ffn_w_down=v_ffn_w_down)
    delta, new_m, new_v = {}, {}, {}

    def as2d(k, a):
        a = tr(a) if k in transposed else a
        return a.reshape(-1, a.shape[-1])

    def back(k, a):
        shape = weights[k].shape
        return tr(a.reshape(shape[:-2] + (shape[-1], shape[-2]))) if k in transposed else a.reshape(shape)

    def update(group, name, nblk=1):
        outs = _adamw([as2d(k, weights[k]) for k in group], [as2d(k, grads[k]) for k in group],
                      [as2d(k, m_in[k]) for k in group], [as2d(k, v_in[k]) for k in group], name, nblk)
        for i, k in enumerate(group):
            delta[k], new_m[k], new_v[k] = (back(k, o[i]) for o in outs)

    update(["ffn_w_gate", "ffn_w_up", "ffn_w_down"], "adamw_ffn", 4)
    update(["odd_w_in", "odd_w_out"], "adamw_mix1", 2)
    update([k for k in names if k not in delta and k not in ("even_w_in", "even_w_out")], "adamw_small")

    pb_mix0, gb_mix0 = _chip_exchange_wait(mix0_sems, pb_mix0, land_mix0, new_v["k_norm"], "mix0")
    (gsh_mix0,) = _rs_final_sums([pb_mix0], [gb_mix0], "rs_final_sum_mix0")
    grads["even_w_in"] = tr(gsh_mix0[OFF_EIN:OFF_EIN + N_EIN][None])
    grads["even_w_out"] = gsh_mix0[OFF_EOUT:OFF_EOUT + N_SQ][None]
    update(["even_w_in", "even_w_out"], "adamw_mix0", 2)

    return (loss.reshape(()), grad_x, *[grads[k] for k in names], *[delta[k] for k in names],
            *[new_m[k] for k in names], *[new_v[k] for k in names])
```

```python
import functools

import numpy as np
import jax
import jax.numpy as jnp
from jax import lax
from jax.experimental import pallas as pl
from jax.experimental.pallas import tpu as pltpu

F32 = jnp.float32
BF16 = jnp.bfloat16
MESH = pl.DeviceIdType.MESH

D = 1024
EPS = 1e-6
NEG_INF = -1e30
SG_HEADS, SG_HD, SG_W, SG_CHUNK = 4, 128, 512, 128
SC_W = 512
EVEN_IN = 2560
POOL_W = 256
POOL_GD = 64
Q_LORA, KV_LORA, QK_ROPE, QK_NOPE, V_DIM = 384, 256, 64, 128, 128
QK_DIM = QK_NOPE + QK_ROPE
HEADS = 6
HP = 256
ODD_IN = 960
D_FF = 2816
ROPE_THETA = 10000.0
ATT_SCALE = QK_DIM ** -0.5
LR, B1, B2, ADAM_EPS, WD, STEP = 0.001, 0.9, 0.999, 1e-08, 0.01, 10

N_DEV = 8
TB = 512
TB_FFN_BWD = 256
TK_DW = 1024
HALO = 16
VMEM_LIMIT = 56 * 1024 * 1024

N_EIN, N_FF, N_SQ = 320, 352, 128
OFF_EIN, OFF_EOUT = 0, 384
OFF_GATE, OFF_UP, R_GU = 0, 352, 704
OFF_OIN, OFF_OOUT, OFF_QB, OFF_KVB, R_MIX1 = 0, 128, 256, 320, 384
N_QB, N_QB_USED, N_KVB = 64, 54, 48

INV_SQRT2 = 0.7071067811865476
INV_SQRT_2PI = 0.3989422804014327


def _dot(a, b, ca, cb):
    return lax.dot_general(a, b, (((ca,), (cb,)), ((), ())), preferred_element_type=F32)


def _cparams(n_axes=1):
    return pltpu.CompilerParams(dimension_semantics=("arbitrary",) * n_axes, vmem_limit_bytes=VMEM_LIMIT)


def _wspec(n, off):
    assert off % n == 0
    idx = off // n
    return pl.BlockSpec((N_DEV, n, D), lambda i: (0, idx, 0), pipeline_mode=pl.Buffered(1))


def _const_spec(shape):
    zeros = (0,) * len(shape)
    return pl.BlockSpec(shape, lambda *_: zeros)


class _Comm:
    def __init__(self, ins, out_shapes, sems, start, wait, mid=None):
        self.ins, self.out_shapes, self.sems, self.start, self.wait, self.mid = ins, out_shapes, sems, start, wait, mid


def _both(c1, c2):
    def split(f1, f2):
        def run(ins, outs, sems):
            f1(ins[:len(c1.ins)], outs[:len(c1.out_shapes)], sems[:len(c1.sems)])
            f2(ins[len(c1.ins):], outs[len(c1.out_shapes):], sems[len(c1.sems):])
        return run

    def nothing(ins, outs, sems):
        pass

    mid = None if c1.mid is None and c2.mid is None else split(c1.mid or nothing, c2.mid or nothing)
    return _Comm(c1.ins + c2.ins, c1.out_shapes + c2.out_shapes, c1.sems + c2.sems,
                 split(c1.start, c2.start), split(c1.wait, c2.wait), mid)


def _call(body, name, grid, in_specs, out_specs, out_shape, args, scratch_shapes=(), comm=None, aliases=None):
    n_axes = len(grid)
    aliases = aliases or {}
    if comm is None:
        res = pl.pallas_call(
            body, name=name, grid=grid, in_specs=list(in_specs), out_specs=list(out_specs),
            out_shape=list(out_shape), scratch_shapes=list(scratch_shapes), input_output_aliases=aliases,
            compiler_params=_cparams(n_axes))(*args)
        return list(res), []
    ni, no, ns = len(in_specs), len(out_specs), len(scratch_shapes)
    ci, co = len(comm.ins), len(comm.out_shapes)
    n_steps = int(np.prod(grid))

    def carrier(*refs):
        ins, cin = refs[:ni], refs[ni:ni + ci]
        outs, cout = refs[ni + ci:ni + ci + no], refs[ni + ci + no:ni + ci + no + co]
        scr, sems = refs[ni + ci + no + co:ni + ci + no + co + ns], refs[ni + ci + no + co + ns:]
        step = 0
        for a in range(n_axes):
            step = step * grid[a] + pl.program_id(a)

        @pl.when(step == 0)
        def _():
            comm.start(cin, cout, sems)

        body(*ins, *outs, *scr)

        if comm.mid is not None and n_steps >= 4:
            @pl.when(step == n_steps // 2)
            def _():
                comm.mid(cin, cout, sems)

        @pl.when(step == n_steps - 1)
        def _():
            if comm.mid is not None and n_steps < 4:
                comm.mid(cin, cout, sems)
            comm.wait(cin, cout, sems)

    any_spec = pl.BlockSpec(memory_space=pl.ANY)
    res = pl.pallas_call(
        carrier, name=name, grid=grid, in_specs=list(in_specs) + [any_spec] * ci,
        out_specs=list(out_specs) + [any_spec] * co, out_shape=list(out_shape) + list(comm.out_shapes),
        scratch_shapes=list(scratch_shapes) + list(comm.sems), input_output_aliases=aliases,
        compiler_params=_cparams(n_axes))(*args, *comm.ins)
    return list(res[:no]), list(res[no:])


def _comm_alone(comm, name):
    ci, co = len(comm.ins), len(comm.out_shapes)

    def body(*refs):
        cin, cout, sems = refs[:ci], refs[ci:ci + co], refs[ci + co:]
        comm.start(cin, cout, sems)
        if comm.mid is not None:
            comm.mid(cin, cout, sems)
        comm.wait(cin, cout, sems)

    any_spec = pl.BlockSpec(memory_space=pl.ANY)
    res = pl.pallas_call(
        body, name=name, out_shape=list(comm.out_shapes), in_specs=[any_spec] * ci, out_specs=[any_spec] * co,
        scratch_shapes=list(comm.sems))(*comm.ins)
    return list(res)


def _rms(x, g):
    r = lax.rsqrt(jnp.mean(x * x, axis=-1, keepdims=True) + EPS)
    return x * r * g, r


def _rms_bwd(x, r, g, dy):
    xh = x * r
    dxh = dy * g
    dx = r * (dxh - xh * jnp.mean(dxh * xh, axis=-1, keepdims=True))
    dg = jnp.sum(dy * xh, axis=0, keepdims=True)
    return dx, dg


def _gelu(x):
    return 0.5 * x * (1.0 + lax.erf(x * INV_SQRT2))


def _gelu_grad(x):
    return 0.5 * (1.0 + lax.erf(x * INV_SQRT2)) + x * jnp.exp(-0.5 * x * x) * INV_SQRT_2PI


def _shift_down(a, k):
    rows = lax.broadcasted_iota(jnp.int32, a.shape, 0)
    return jnp.where(rows >= k, pltpu.roll(a, k, 0), 0.0)


def _shift_up(a, k):
    n = a.shape[0]
    rows = lax.broadcasted_iota(jnp.int32, a.shape, 0)
    return jnp.where(rows < n - k, pltpu.roll(a, n - k, 0), 0.0)


def _tril_bf16(w):
    r = lax.broadcasted_iota(jnp.int32, w.shape, 0)
    c = lax.broadcasted_iota(jnp.int32, w.shape, 1)
    return jnp.where(r >= c, w, 0.0).astype(BF16)


def _ln_head(vh, g):
    mu = jnp.mean(vh, axis=-1, keepdims=True)
    xc = vh - mu
    rr = lax.rsqrt(jnp.mean(xc * xc, axis=-1, keepdims=True) + EPS)
    xh = xc * rr
    return xh * g, xh, rr


def _conv_fwd(z, tail, cw_ref):
    ext = jnp.concatenate([tail, z], axis=0)
    zs1 = _shift_down(ext, 1)[HALO:]
    zs2 = _shift_down(ext, 2)[HALO:]
    y = cw_ref[2:3, :] * z + cw_ref[1:2, :] * zs1 + cw_ref[0:1, :] * zs2
    return y, zs1, zs2


def _pool_cnt(shape, blk_in_seq):
    rows = lax.broadcasted_iota(jnp.int32, shape, 0)
    grp = lax.broadcasted_iota(jnp.int32, shape, 1) // POOL_GD
    win = jnp.where(grp == 0, 2, jnp.where(grp == 1, 4, jnp.where(grp == 2, 8, 16)))
    tpos = blk_in_seq * shape[0] + rows + 1
    return jnp.minimum(tpos, win).astype(F32), grp


def _pool_select(grp, s2, s4, s8, s16):
    return jnp.where(grp == 0, s2, jnp.where(grp == 1, s4, jnp.where(grp == 2, s8, s16)))


def _pool_fwd(z, tail, blk_in_seq):
    ext = jnp.concatenate([tail, z], axis=0)
    s2 = ext + _shift_down(ext, 1)
    s4 = s2 + _shift_down(s2, 2)
    s8 = s4 + _shift_down(s4, 4)
    s16 = s8 + _shift_down(s8, 8)
    cnt, grp = _pool_cnt(z.shape, blk_in_seq)
    sums = _pool_select(grp, s2[HALO:], s4[HALO:], s8[HALO:], s16[HALO:])
    return sums / cnt - z, cnt, grp


def _pool_bwd(dpooled, dpm, head, grp):
    n = dpm.shape[0]
    ext = jnp.concatenate([dpm, head], axis=0)
    u2 = ext + _shift_up(ext, 1)
    u4 = u2 + _shift_up(u2, 2)
    u8 = u4 + _shift_up(u4, 4)
    u16 = u8 + _shift_up(u8, 8)
    return _pool_select(grp, u2[:n], u4[:n], u8[:n], u16[:n]) - dpooled


def _lane_sums(a):
    return _dot(a.astype(BF16), jnp.ones((a.shape[1], a.shape[1]), BF16), 1, 0)


def _swap_halves(y1):
    src = lax.broadcasted_iota(jnp.int32, (128, 128), 0)
    dst = lax.broadcasted_iota(jnp.int32, (128, 128), 1)
    perm = jnp.where(((dst < 32) & (src == dst + 32)) | ((dst >= 32) & (dst < QK_ROPE) & (src == dst - 32)), 1.0, 0.0)
    return _dot(y1.astype(BF16), perm.astype(BF16), 1, 0)


def _rope(y1, c, s):
    return y1 * c + _swap_halves(y1) * s


def _rope_bwd(d1, c, s):
    return d1 * c + _swap_halves(d1 * s)


def _qk_prep(x, g, c, s):
    r = lax.rsqrt(_lane_sums(x * x) * (1.0 / QK_DIM) + EPS)
    y = x * r * g
    return jnp.concatenate([y[:, :128], _rope(y[:, 128:], c, s)], axis=1), r


def _qk_prep_bwd(dout, x, r, g, c, s):
    dy = jnp.concatenate([dout[:, :128], _rope_bwd(dout[:, 128:], c, s)], axis=1)
    xh = x * r
    dxh = dy * g
    dx = r * (dxh - xh * (_lane_sums(dxh * xh) * (1.0 / QK_DIM)))
    return dx, jnp.sum(dy * xh, axis=0, keepdims=True)


def _place():
    return lax.axis_index("x"), lax.axis_index("y"), lax.axis_index("c")


def _gather_comm(arrs):
    n = len(arrs)

    def halves(a):
        rows = arrs[a].shape[0]
        tile = 16 if arrs[a].dtype == BF16 else 8
        top = rows // 2 if rows % (2 * tile) == 0 else rows
        return (0, top), (top, rows - top)

    def plan(ins, outs, sems):
        send_sems, recv_sems, local_sems = sems
        x, y, c = _place()
        me, sib, xn, yn, dg = (x, y, c), (x, y, 1 - c), (1 - x, y, c), (x, 1 - y, c), (1 - x, 1 - y, c)

        def slot(a, dev, part=None):
            ref = outs[a].at[4 * dev[0] + 2 * dev[1] + dev[2]]
            return ref if part is None else ref.at[pl.ds(part[0], part[1])]

        def copy(a, k, block, to, src=None, part=None):
            return pltpu.make_async_remote_copy(
                src_ref=slot(a, block, part) if src is None else src, dst_ref=slot(a, block, part),
                send_sem=send_sems.at[a, k], recv_sem=recv_sems.at[a, k], device_id=to, device_id_type=MESH)

        local = [pltpu.make_async_copy(ins[a], slot(a, me), local_sems.at[a]) for a in range(n)]
        return me, sib, xn, yn, dg, copy, local

    def start(ins, outs, sems):
        me, sib, xn, yn, _, copy, local = plan(ins, outs, sems)
        for a in range(n):
            local[a].start()
            for k, to in enumerate((sib, xn, yn)):
                copy(a, k, me, to, src=ins[a]).start()

    def mid(ins, outs, sems):
        me, sib, xn, yn, _, copy, _ = plan(ins, outs, sems)
        for a in range(n):
            top, bottom = halves(a)
            copy(a, 1, xn, me).wait_recv()
            copy(a, 3, xn, yn, part=top).start()
            copy(a, 5, xn, sib).start()
            copy(a, 2, yn, me).wait_recv()
            if bottom[1]:
                copy(a, 4, yn, xn, part=bottom).start()
            copy(a, 6, yn, sib).start()

    def wait(ins, outs, sems):
        me, sib, xn, yn, dg, copy, local = plan(ins, outs, sems)
        other = lambda dev: (dev[0], dev[1], 1 - dev[2])
        for a in range(n):
            top, bottom = halves(a)
            copy(a, 3, dg, me, part=top).wait_recv()
            if bottom[1]:
                copy(a, 4, dg, me, part=bottom).wait_recv()
            copy(a, 7, dg, sib).start()
        for a in range(n):
            top, bottom = halves(a)
            for k, block in ((0, sib), (5, other(xn)), (6, other(yn)), (7, other(dg))):
                copy(a, k, block, me).wait_recv()
            for k, block in ((0, me), (1, me), (2, me), (5, xn), (6, yn), (7, dg)):
                copy(a, k, block, me, src=ins[a] if k < 3 else None).wait_send()
            copy(a, 3, xn, me, part=top).wait_send()
            if bottom[1]:
                copy(a, 4, yn, me, part=bottom).wait_send()
            local[a].wait()

    return _Comm(
        list(arrs), [jax.ShapeDtypeStruct((N_DEV,) + a.shape, a.dtype) for a in arrs],
        [pltpu.SemaphoreType.DMA((n, 8)), pltpu.SemaphoreType.DMA((n, 8)), pltpu.SemaphoreType.DMA((n,))],
        start, wait, mid)


def _sum_gathered(g):
    rows = g.shape[1]

    def body(g_ref, sum_ref):
        total = g_ref[0]
        for d in range(1, N_DEV):
            total = total + g_ref[d]
        sum_ref[...] = total

    return pl.pallas_call(
        body, name="sum_gathered_small", out_shape=jax.ShapeDtypeStruct((rows, 128), F32), grid=(1,),
        in_specs=[pl.BlockSpec((N_DEV, rows, 128), lambda i: (0, 0, 0))],
        out_specs=pl.BlockSpec((rows, 128), lambda i: (0, 0)), compiler_params=_cparams(1),
    )(g)


def _sum_rows(rows):
    return rows if rows <= 512 else rows // 2


def _pair_exchange_comm(gp):
    _, rows, cols = gp.shape

    def copies(ins, outs, sems):
        send_sems, recv_sems = sems
        x, y, c = _place()
        return [pltpu.make_async_remote_copy(
            src_ref=ins[0].at[2 * j + (1 - c)], dst_ref=outs[0].at[j], send_sem=send_sems.at[j],
            recv_sem=recv_sems.at[j], device_id=(x, y, 1 - c), device_id_type=MESH) for j in range(4)]

    def start(ins, outs, sems):
        for cp in copies(ins, outs, sems):
            cp.start()

    def wait(ins, outs, sems):
        for cp in copies(ins, outs, sems):
            cp.wait()

    return _Comm([gp], [jax.ShapeDtypeStruct((4, rows, cols), gp.dtype)],
                 [pltpu.SemaphoreType.DMA((4,)), pltpu.SemaphoreType.DMA((4,))], start, wait)


def _rs_pair_sum(gp, got, where, name):
    _, rows, cols = got.shape
    rb = _sum_rows(rows)
    gp4 = gp.reshape(4, 2, rows, cols)

    def body(w_ref, a_ref, b_ref, o_ref):
        o_ref[0] = (a_ref[0, 0].astype(F32) + b_ref[0].astype(F32)).astype(o_ref.dtype)

    return pl.pallas_call(
        body, name=name, out_shape=jax.ShapeDtypeStruct((4, rows, cols), gp.dtype),
        grid_spec=pltpu.PrefetchScalarGridSpec(
            num_scalar_prefetch=1, grid=(4, rows // rb),
            in_specs=[pl.BlockSpec((1, 1, rb, cols), lambda k, r, w: (w[1 + k], w[0], r, 0)),
                      pl.BlockSpec((1, rb, cols), lambda k, r, w: (w[1 + k], r, 0))],
            out_specs=pl.BlockSpec((1, rb, cols), lambda k, r, w: (k, r, 0))),
        compiler_params=_cparams(2),
    )(where, gp4, got)


def _chip_exchange_comm(pb):
    _, rows, cols = pb.shape

    def copies(ins, outs, sems):
        send_sems, recv_sems = sems
        x, y, c = _place()
        chips = [(1 - x, y), (x, 1 - y), (1 - x, 1 - y)]
        return [pltpu.make_async_remote_copy(
            src_ref=ins[0].at[1 + k], dst_ref=outs[0].at[k], send_sem=send_sems.at[k],
            recv_sem=recv_sems.at[k], device_id=(px, py, c), device_id_type=MESH)
            for k, (px, py) in enumerate(chips)]

    def start(ins, outs, sems):
        for cp in copies(ins, outs, sems):
            cp.start()

    def wait(ins, outs, sems):
        for cp in copies(ins, outs, sems):
            cp.wait()

    return _Comm([pb], [jax.ShapeDtypeStruct((3, rows, cols), pb.dtype)],
                 [pltpu.SemaphoreType.DMA((3,)), pltpu.SemaphoreType.DMA((3,))], start, wait)


def _chip_exchange_start(pb, tag):
    _, rows, cols = pb.shape

    def body(pb_ref, land_ref, *rest):
        sems, token = rest[:6], rest[8]
        x, y, c = _place()
        chips = [(1 - x, y), (x, 1 - y), (1 - x, 1 - y)]
        for k, (px, py) in enumerate(chips):
            pltpu.make_async_remote_copy(
                src_ref=pb_ref.at[1 + k], dst_ref=land_ref.at[k], send_sem=sems[k], recv_sem=sems[3 + k],
                device_id=(px, py, c), device_id_type=MESH).start()
        token[...] = jnp.zeros_like(token)

    hbm = pl.BlockSpec(memory_space=pltpu.HBM)
    sem = pl.BlockSpec(memory_space=pltpu.SEMAPHORE)
    land = lax.empty((3, rows, cols), pb.dtype)
    res = pl.pallas_call(
        body, name="rs_chip_exchange_start_" + tag,
        out_shape=(*[pltpu.SemaphoreType.DMA(())] * 6, pltpu.HBM(pb.shape, pb.dtype), pltpu.HBM(land.shape, land.dtype),
                   jax.ShapeDtypeStruct((8, 128), F32)),
        in_specs=(hbm, hbm), out_specs=(*[sem] * 6, hbm, hbm, pl.BlockSpec(memory_space=pltpu.VMEM)),
        input_output_aliases={0: 6, 1: 7},
        compiler_params=pltpu.CompilerParams(has_side_effects=pltpu.SideEffectType.DATAFLOW_SIDE_EFFECTING),
    )(pltpu.with_memory_space_constraint(pb, pltpu.HBM), pltpu.with_memory_space_constraint(land, pltpu.HBM))
    return list(res[:6]), res[6], res[7], res[8]


def _chip_exchange_wait(sems, pb_thru, land_thru, after, tag):
    def body(pb_ref, land_ref, *rest):
        sems_in = rest[:6]
        x, y, c = _place()
        chips = [(1 - x, y), (x, 1 - y), (1 - x, 1 - y)]
        for k, (px, py) in enumerate(chips):
            cp = pltpu.make_async_remote_copy(
                src_ref=pb_ref.at[1 + k], dst_ref=land_ref.at[k], send_sem=sems_in[k], recv_sem=sems_in[3 + k],
                device_id=(px, py, c), device_id_type=MESH)
            cp.wait_send()
            cp.wait_recv()

    hbm = pl.BlockSpec(memory_space=pltpu.HBM)
    sem = pl.BlockSpec(memory_space=pltpu.SEMAPHORE)
    res = pl.pallas_call(
        body, name="rs_chip_exchange_wait_" + tag,
        out_shape=(pltpu.HBM(pb_thru.shape, pb_thru.dtype), pltpu.HBM(land_thru.shape, land_thru.dtype)),
        in_specs=(hbm, hbm, *[sem] * 6, pl.BlockSpec(memory_space=pl.ANY)), out_specs=(hbm, hbm),
        input_output_aliases={0: 0, 1: 1},
        compiler_params=pltpu.CompilerParams(has_side_effects=pltpu.SideEffectType.DATAFLOW_SIDE_EFFECTING),
    )(pb_thru, land_thru, *sems, after)
    return res[0], res[1]


def _rs_final_sums(pbs, gots, name, after=None):
    n = len(pbs)

    def body(*refs):
        outs = refs[len(refs) - n:]
        for a in range(n):
            m_ref, g_ref, o_ref = refs[a], refs[n + a], outs[a]
            o_ref[...] = ((m_ref[0].astype(F32) + g_ref[0].astype(F32)) + g_ref[1].astype(F32)) + g_ref[2].astype(F32)

    half = [pb.shape[1] // 2 for pb in pbs]
    in_specs = ([pl.BlockSpec((1, h, D), lambda i: (0, i, 0)) for h in half]
                + [pl.BlockSpec((3, h, D), lambda i: (0, i, 0)) for h in half])
    args = (*pbs, *gots)
    if after is not None:
        in_specs, args = in_specs + [pl.BlockSpec(memory_space=pl.ANY)], args + (after,)
    res, _ = _call(body, name, (2,), in_specs, [pl.BlockSpec((h, D), lambda i: (i, 0)) for h in half],
                   [jax.ShapeDtypeStruct(pb.shape[1:], F32) for pb in pbs], args)
    return res


def _rope_tables(pos_col, inv_freq, comm=None):
    t = pos_col.shape[0]

    def body(p_ref, f_ref, c_ref, s_ref):
        ang = p_ref[...].astype(F32) * f_ref[...]
        lane = lax.broadcasted_iota(jnp.int32, ang.shape, 1)
        c_ref[...] = jnp.where(lane < QK_ROPE, jnp.cos(ang), 0.0)
        s = jnp.sin(ang)
        s_ref[...] = jnp.where(lane < 32, -s, jnp.where(lane < QK_ROPE, s, 0.0))

    spec = pl.BlockSpec((TB, 128), lambda i: (i, 0))
    return _call(
        body, "rope_tables", (t // TB,), [pl.BlockSpec((TB, 1), lambda i: (i, 0)), _const_spec((1, 128))],
        [spec] * 2, [jax.ShapeDtypeStruct((t, 128), F32)] * 2, (pos_col, inv_freq), (), comm)


def _sgu_conv_fwd(proj, tail, lng_ref, ws_ref, bst_ref, cw_ref):
    gu = _gelu(proj[:, 0:SG_W])
    gv = _gelu(proj[:, SG_W:2 * SG_W])
    bg = proj[:, 1024:1536]
    z = proj[:, 1536:2048] * proj[:, 2048:2560]
    heads = []
    for h in range(SG_HEADS):
        sl = slice(h * SG_HD, (h + 1) * SG_HD)
        vn, _, _ = _ln_head(gv[:, sl], lng_ref[:, sl])
        vnb = vn.astype(BF16)
        wm = _tril_bf16(ws_ref[h])
        bcol = bst_ref[:, h:h + 1]
        mixed = jnp.concatenate(
            [_dot(wm, vnb[k * SG_CHUNK:(k + 1) * SG_CHUNK], 1, 0) + bcol for k in range(TB // SG_CHUNK)], axis=0)
        heads.append(gu[:, sl] * mixed)
    a_out = jnp.concatenate(heads, axis=1)
    y, _, _ = _conv_fwd(z, tail, cw_ref)
    return a_out, bg * y, z


def _even_fwd(x, wg, gamma, lng, ws, bst, cw, seq, comm=None):
    t = x.shape[0]
    nbs = seq // TB

    def body(x_ref, gam_ref, win_ref, wout_ref, lng_ref, ws_ref, bst_ref, cw_ref, x1_ref, proj_ref, tail_ref):
        i = pl.program_id(0)
        xv = x_ref[...]
        h, _ = _rms(xv, gam_ref[...])
        proj = _dot(h.astype(BF16), win_ref[...].reshape(EVEN_IN, D), 1, 1)
        proj_ref[...] = proj.astype(BF16)
        tail = jnp.where(i % nbs == 0, 0.0, tail_ref[...])
        a_out, b_out, z = _sgu_conv_fwd(proj, tail, lng_ref, ws_ref, bst_ref, cw_ref)
        tail_ref[...] = z[TB - HALO:, :]
        x1_ref[...] = (xv + _dot(a_out.astype(BF16), wout_ref[0:4].reshape(512, D), 1, 0)
                       + _dot(b_out.astype(BF16), wout_ref[4:8].reshape(512, D), 1, 0))

    row = pl.BlockSpec((TB, D), lambda i: (i, 0))
    return _call(
        body, "even_fwd", (t // TB,),
        [row, _const_spec((1, D)), _wspec(N_EIN, OFF_EIN), _wspec(N_SQ, OFF_EOUT), _const_spec((1, SG_W)),
         _const_spec((SG_HEADS, 128, 128)), _const_spec((128, 128)), _const_spec((8, SC_W))],
        [row, pl.BlockSpec((TB, EVEN_IN), lambda i: (i, 0))],
        [jax.ShapeDtypeStruct((t, D), F32), jax.ShapeDtypeStruct((t, EVEN_IN), BF16)],
        (x, gamma, wg, wg, lng, ws, bst, cw), [pltpu.VMEM((HALO, SC_W), F32)], comm)


def _even_bwd(x, proj, dx1, wg, gamma, lng, ws, bst, cw, seq, comm=None):
    t = x.shape[0]
    nb, nbs = t // TB, seq // TB

    def body(x_ref, proj_ref, ptail_ref, dx1_ref, gam_ref, win_ref, wout_ref, lng_ref, ws_ref, bst_ref, cw_ref,
             dx0_ref, dproj_ref, mix_ref, h_ref, dgam_ref, dws_ref, dbc_ref, dlng_ref, dcw_ref, head_ref):
        i = pl.program_id(0)
        blk = nb - 1 - i

        @pl.when(i == 0)
        def _():
            dgam_ref[...] = jnp.zeros_like(dgam_ref)
            dws_ref[...] = jnp.zeros_like(dws_ref)
            dbc_ref[...] = jnp.zeros_like(dbc_ref)
            dlng_ref[...] = jnp.zeros_like(dlng_ref)
            dcw_ref[...] = jnp.zeros_like(dcw_ref)

        xv = x_ref[...]
        gam = gam_ref[...]
        h, r = _rms(xv, gam)
        h_ref[...] = h.astype(BF16)
        dx1 = dx1_ref[...]
        dmix = _dot(dx1.astype(BF16), wout_ref[...].reshape(D, D), 1, 1)
        da, db = dmix[:, :SG_W], dmix[:, SG_W:]
        proj = proj_ref[...].astype(F32)
        u, v = proj[:, 0:SG_W], proj[:, SG_W:2 * SG_W]
        bg, cg, hv = proj[:, 1024:1536], proj[:, 1536:2048], proj[:, 2048:2560]
        gu, gv = _gelu(u), _gelu(v)

        a_heads, dgv_heads = [], []
        for hd in range(SG_HEADS):
            sl = slice(hd * SG_HD, (hd + 1) * SG_HD)
            g_h = lng_ref[:, sl]
            vn, xh, rr = _ln_head(gv[:, sl], g_h)
            vnb = vn.astype(BF16)
            wm = _tril_bf16(ws_ref[hd])
            bcol = bst_ref[:, hd:hd + 1]
            mixed_c, dvn_c = [], []
            dw_acc = jnp.zeros((128, 128), F32)
            db_acc = jnp.zeros((128, 1), F32)
            for k in range(TB // SG_CHUNK):
                rs = slice(k * SG_CHUNK, (k + 1) * SG_CHUNK)
                mixed = _dot(wm, vnb[rs], 1, 0) + bcol
                dmixed = da[rs, sl] * gu[rs, sl]
                dmb = dmixed.astype(BF16)
                dvn_c.append(_dot(wm, dmb, 0, 0))
                dw_acc = dw_acc + _dot(dmb, vnb[rs], 1, 1)
                db_acc = db_acc + jnp.sum(dmixed, axis=1, keepdims=True)
                mixed_c.append(mixed)
            mixed_h = jnp.concatenate(mixed_c, axis=0)
            dvn = jnp.concatenate(dvn_c, axis=0)
            r_i = lax.broadcasted_iota(jnp.int32, (128, 128), 0)
            c_i = lax.broadcasted_iota(jnp.int32, (128, 128), 1)
            dws_ref[hd] += jnp.where(r_i >= c_i, dw_acc, 0.0)
            dbc_ref[:, hd:hd + 1] += db_acc
            dlng_ref[:, sl] += jnp.sum(dvn * xh, axis=0, keepdims=True)
            dxh = dvn * g_h
            dgv = rr * (dxh - jnp.mean(dxh, axis=-1, keepdims=True)
                        - xh * jnp.mean(dxh * xh, axis=-1, keepdims=True))
            a_heads.append(gu[:, sl] * mixed_h)
            dproj_ref[:, sl] = (da[:, sl] * mixed_h * _gelu_grad(u[:, sl])).astype(BF16)
            dgv_heads.append(dgv * _gelu_grad(v[:, sl]))
        dproj_ref[:, SG_W:2 * SG_W] = jnp.concatenate(dgv_heads, axis=1).astype(BF16)
        mix_ref[:, :SG_W] = jnp.concatenate(a_heads, axis=1).astype(BF16)

        z = cg * hv
        pt = ptail_ref[...].astype(F32)
        tail = jnp.where(blk % nbs == 0, 0.0, pt[:, 1536:2048] * pt[:, 2048:2560])
        y, zs1, zs2 = _conv_fwd(z, tail, cw_ref)
        mix_ref[:, SG_W:] = (bg * y).astype(BF16)
        dy = db * bg
        head = jnp.where(blk % nbs == nbs - 1, 0.0, head_ref[...])
        ext = jnp.concatenate([dy, head], axis=0)
        dz = (cw_ref[2:3, :] * dy + cw_ref[1:2, :] * _shift_up(ext, 1)[:TB]
              + cw_ref[0:1, :] * _shift_up(ext, 2)[:TB])
        head_ref[...] = dy[:HALO, :]
        dcw_ref[2:3, :] += jnp.sum(dy * z, axis=0, keepdims=True)
        dcw_ref[1:2, :] += jnp.sum(dy * zs1, axis=0, keepdims=True)
        dcw_ref[0:1, :] += jnp.sum(dy * zs2, axis=0, keepdims=True)
        dproj_ref[:, 1024:1536] = (db * y).astype(BF16)
        dproj_ref[:, 1536:2048] = (dz * hv).astype(BF16)
        dproj_ref[:, 2048:2560] = (dz * cg).astype(BF16)

        dh = _dot(dproj_ref[...], win_ref[...].reshape(EVEN_IN, D), 1, 0)
        dxn, dgam = _rms_bwd(xv, r, gam, dh)
        dgam_ref[...] += dgam
        dx0_ref[...] = dx1 + dxn

    def rev(w):
        return pl.BlockSpec((TB, w), lambda i: (nb - 1 - i, 0))

    ptail = pl.BlockSpec((HALO, EVEN_IN), lambda i: (jnp.maximum((nb - 1 - i) * (TB // HALO) - 1, 0), 0))
    return _call(
        body, "even_bwd", (nb,),
        [rev(D), rev(EVEN_IN), ptail, rev(D), _const_spec((1, D)), _wspec(N_EIN, OFF_EIN),
         _wspec(N_SQ, OFF_EOUT), _const_spec((1, SG_W)), _const_spec((SG_HEADS, 128, 128)),
         _const_spec((128, 128)), _const_spec((8, SC_W))],
        [rev(D), rev(EVEN_IN), rev(D), rev(D), _const_spec((1, D)), _const_spec((SG_HEADS, 128, 128)),
         _const_spec((128, 128)), _const_spec((1, SG_W)), _const_spec((8, SC_W))],
        [jax.ShapeDtypeStruct((t, D), F32), jax.ShapeDtypeStruct((t, EVEN_IN), BF16),
         jax.ShapeDtypeStruct((t, D), BF16), jax.ShapeDtypeStruct((t, D), BF16),
         jax.ShapeDtypeStruct((1, D), F32), jax.ShapeDtypeStruct((SG_HEADS, 128, 128), F32),
         jax.ShapeDtypeStruct((128, 128), F32), jax.ShapeDtypeStruct((1, SG_W), F32),
         jax.ShapeDtypeStruct((8, SC_W), F32)],
        (x, proj, proj, dx1, gamma, wg, wg, lng, ws, bst, cw), [pltpu.VMEM((HALO, SC_W), F32)], comm)


def _last_block_fwd(x, c_out, d_out, w_mix1, w_gu, w_d, gamma, target):
    t = x.shape[0]

    def body(x_ref, c_ref, d_ref, wo_ref, gam_ref, wg_ref, wu_ref, wd_ref, t_ref,
             x3_ref, dy_ref, g_ref, u_ref, loss_ref):
        @pl.when(pl.program_id(0) == 0)
        def _():
            loss_ref[...] = jnp.zeros_like(loss_ref)

        xv = (x_ref[...] + _dot(c_ref[...], wo_ref[0:2].reshape(POOL_W, D), 1, 0)
              + _dot(d_ref[...], wo_ref[2:8].reshape(HEADS * V_DIM, D), 1, 0))
        x3_ref[...] = xv
        h, _ = _rms(xv, gam_ref[...])
        hb = h.astype(BF16)
        g = _dot(hb, wg_ref[...].reshape(D_FF, D), 1, 1)
        u = _dot(hb, wu_ref[...].reshape(D_FF, D), 1, 1)
        g_ref[...] = g.astype(BF16)
        u_ref[...] = u.astype(BF16)
        act = g * jax.nn.sigmoid(g) * u
        err = xv + _dot(act.astype(BF16), wd_ref[...].reshape(D_FF, D), 1, 0) - t_ref[...]
        dy_ref[...] = err * (1.0 / D)
        sq = jnp.sum(jnp.sum(err * err, axis=-1, keepdims=True), axis=0, keepdims=True)
        loss_ref[...] += (0.5 / D) * sq

    def row(w):
        return pl.BlockSpec((TB, w), lambda i: (i, 0))

    res, _ = _call(
        body, "last_block_fwd", (t // TB,),
        [row(D), row(POOL_W), row(HEADS * V_DIM), _wspec(N_SQ, OFF_OOUT), _const_spec((1, D)),
         _wspec(N_FF, OFF_GATE), _wspec(N_FF, OFF_UP), _wspec(N_FF, 0), row(D)],
        [row(D), row(D), row(D_FF), row(D_FF), _const_spec((8, 128))],
        [jax.ShapeDtypeStruct((t, D), F32), jax.ShapeDtypeStruct((t, D), F32), jax.ShapeDtypeStruct((t, D_FF), BF16),
         jax.ShapeDtypeStruct((t, D_FF), BF16), jax.ShapeDtypeStruct((8, 128), F32)],
        (x, c_out, d_out, w_mix1, gamma, w_gu, w_gu, w_d, target))
    return res


def _ffn_up(x, w_gu, gamma, name, comm=None):
    t = x.shape[0]

    def body(x_ref, gam_ref, wg_ref, wu_ref, g_ref, u_ref, act_ref):
        h, _ = _rms(x_ref[...], gam_ref[...])
        hb = h.astype(BF16)
        g = _dot(hb, wg_ref[...].reshape(D_FF, D), 1, 1)
        u = _dot(hb, wu_ref[...].reshape(D_FF, D), 1, 1)
        g_ref[...] = g.astype(BF16)
        u_ref[...] = u.astype(BF16)
        act_ref[...] = (g * jax.nn.sigmoid(g) * u).astype(BF16)

    row = pl.BlockSpec((TB, D), lambda i: (i, 0))
    wide = pl.BlockSpec((TB, D_FF), lambda i: (i, 0))
    return _call(body, name, (t // TB,), [row, _const_spec((1, D)), _wspec(N_FF, OFF_GATE), _wspec(N_FF, OFF_UP)],
                 [wide, wide, wide], [jax.ShapeDtypeStruct((t, D_FF), BF16)] * 3, (x, gamma, w_gu, w_gu), (), comm)


def _ffn_down(x, act, w_d, name, comm=None):
    t = x.shape[0]

    def body(x_ref, a_ref, wd_ref, y_ref):
        y_ref[...] = x_ref[...] + _dot(a_ref[...], wd_ref[...].reshape(D_FF, D), 1, 0)

    row = pl.BlockSpec((TB, D), lambda i: (i, 0))
    wide = pl.BlockSpec((TB, D_FF), lambda i: (i, 0))
    return _call(body, name, (t // TB,), [row, wide, _wspec(N_FF, 0)], [row], [jax.ShapeDtypeStruct((t, D), F32)],
                 (x, act, w_d), (), comm)


def _ffn_bwd(x, g, u, dy, w_gu, w_d, gamma, name, comm=None, w_mix1=None):
    t = x.shape[0]
    with_dmix = w_mix1 is not None

    def body(*refs):
        x_ref, g_ref, u_ref, dy_ref, gam_ref, wg_ref, wu_ref, wd_ref = refs[:8]
        dx_ref, act_ref, dg_ref, du_ref, h_ref, dgam_ref = refs[8 + with_dmix:14 + with_dmix]

        @pl.when(pl.program_id(0) == 0)
        def _():
            dgam_ref[...] = jnp.zeros_like(dgam_ref)

        xv = x_ref[...]
        gam = gam_ref[...]
        h, r = _rms(xv, gam)
        h_ref[...] = h.astype(BF16)
        dyv = dy_ref[...]
        dact = _dot(dyv.astype(BF16), wd_ref[...].reshape(D_FF, D), 1, 1)
        gv = g_ref[...].astype(F32)
        uv = u_ref[...].astype(F32)
        sg = jax.nn.sigmoid(gv)
        silu = gv * sg
        act_ref[...] = (silu * uv).astype(BF16)
        dgb = (dact * uv * (sg * (1.0 + gv * (1.0 - sg)))).astype(BF16)
        dub = (dact * silu).astype(BF16)
        dg_ref[...] = dgb
        du_ref[...] = dub
        dh = _dot(dgb, wg_ref[...].reshape(D_FF, D), 1, 0) + _dot(dub, wu_ref[...].reshape(D_FF, D), 1, 0)
        dxn, dgam = _rms_bwd(xv, r, gam, dh)
        dgam_ref[...] += dgam
        dx = dyv + dxn
        dx_ref[...] = dx
        if with_dmix:
            refs[15][...] = _dot(dx.astype(BF16), refs[8][...].reshape(D, D), 1, 1).astype(BF16)

    row = pl.BlockSpec((TB_FFN_BWD, D), lambda i: (i, 0))
    wide = pl.BlockSpec((TB_FFN_BWD, D_FF), lambda i: (i, 0))
    in_specs = [row, wide, wide, row, _const_spec((1, D)), _wspec(N_FF, OFF_GATE), _wspec(N_FF, OFF_UP),
                _wspec(N_FF, 0)]
    out_specs = [row, wide, wide, wide, row, _const_spec((1, D))]
    out_shape = [jax.ShapeDtypeStruct((t, D), F32), jax.ShapeDtypeStruct((t, D_FF), BF16),
                 jax.ShapeDtypeStruct((t, D_FF), BF16), jax.ShapeDtypeStruct((t, D_FF), BF16),
                 jax.ShapeDtypeStruct((t, D), BF16), jax.ShapeDtypeStruct((1, D), F32)]
    args = (x, g, u, dy, gamma, w_gu, w_gu, w_d)
    if with_dmix:
        in_specs, args = in_specs + [_wspec(N_SQ, OFF_OOUT)], args + (w_mix1,)
        out_specs, out_shape = out_specs + [row], out_shape + [jax.ShapeDtypeStruct((t, D), BF16)]
    return _call(body, name, (t // TB_FFN_BWD,), in_specs, out_specs, out_shape, args, (), comm)


def _odd_pre_fwd(x, wg, gamma, qbt, kvbt, qa_g, kva_g, pw_bd, pscale, seq, comm=None):
    t = x.shape[0]
    nbs = seq // TB

    def body(x_ref, gam_ref, win_ref, qb_ref, kvb_ref, qa_ref, kva_ref, pw_ref, ps_ref,
             proj_ref, q_ref, kv_ref, kr_ref, c_ref, tail_ref):
        i = pl.program_id(0)
        h, _ = _rms(x_ref[...], gam_ref[...])
        proj = _dot(h.astype(BF16), win_ref[...].reshape(D, D), 1, 0)
        proj_ref[...] = proj.astype(BF16)
        zp, ql, kvl = proj[:, :POOL_W], proj[:, 256:640], proj[:, 640:896]
        kr_ref[...] = proj[:, 896:1024]
        qn, _ = _rms(ql, qa_ref[...])
        q_ref[...] = _dot(qn.astype(BF16), qb_ref[...], 1, 1).astype(BF16)
        kvn, _ = _rms(kvl, kva_ref[...])
        kv_ref[...] = _dot(kvn.astype(BF16), kvb_ref[...], 1, 1).astype(BF16)
        tail = jnp.where(i % nbs == 0, 0.0, tail_ref[...])
        pooled, _, _ = _pool_fwd(zp, tail, i % nbs)
        tail_ref[...] = zp[TB - HALO:, :]
        c_ref[...] = (_dot(pooled.astype(BF16), pw_ref[...], 1, 0) * ps_ref[...]).astype(BF16)

    def row(w):
        return pl.BlockSpec((TB, w), lambda i: (i, 0))

    return _call(
        body, "odd_pre_fwd", (t // TB,),
        [row(D), _const_spec((1, D)), _wspec(N_SQ, OFF_OIN), _const_spec((HEADS * HP, Q_LORA)),
         _const_spec((HEADS * HP, KV_LORA)), _const_spec((1, Q_LORA)), _const_spec((1, KV_LORA)),
         _const_spec((POOL_W, POOL_W)), _const_spec((1, POOL_W))],
        [row(D), row(HEADS * HP), row(HEADS * HP), row(128), row(POOL_W)],
        [jax.ShapeDtypeStruct((t, D), BF16), jax.ShapeDtypeStruct((t, HEADS * HP), BF16),
         jax.ShapeDtypeStruct((t, HEADS * HP), BF16), jax.ShapeDtypeStruct((t, 128), F32),
         jax.ShapeDtypeStruct((t, POOL_W), BF16)],
        (x, gamma, wg, qbt, kvbt, qa_g, kva_g, pw_bd, pscale), [pltpu.VMEM((HALO, POOL_W), F32)], comm)


def _odd_pre_bwd(x, proj, dx3, dmix, dq, dkv, dkr, wg, gamma, qbt, kvbt, qa_g, kva_g, pw_bd, pscale, seq):
    t = x.shape[0]
    nb, nbs = t // TB, seq // TB

    def body(x_ref, proj_ref, ptail_ref, dx3_ref, dco_ref, dq_ref, dkv_ref, dkr_ref, gam_ref, win_ref, qb_ref,
             kvb_ref, qa_ref, kva_ref, pw_ref, ps_ref,
             dx2_ref, dproj_ref, h_ref, qn_ref, kvn_ref, dgam_ref, dqa_ref, dkva_ref, dpw_ref, dps_ref, head_ref):
        i = pl.program_id(0)
        blk = nb - 1 - i

        @pl.when(i == 0)
        def _():
            dgam_ref[...] = jnp.zeros_like(dgam_ref)
            dqa_ref[...] = jnp.zeros_like(dqa_ref)
            dkva_ref[...] = jnp.zeros_like(dkva_ref)
            dpw_ref[...] = jnp.zeros_like(dpw_ref)
            dps_ref[...] = jnp.zeros_like(dps_ref)

        xv = x_ref[...]
        gam = gam_ref[...]
        h, r = _rms(xv, gam)
        h_ref[...] = h.astype(BF16)
        proj = proj_ref[...].astype(F32)
        zp, ql, kvl = proj[:, :POOL_W], proj[:, 256:640], proj[:, 640:896]

        qa = qa_ref[...]
        qn, rq = _rms(ql, qa)
        qn_ref[...] = qn.astype(BF16)
        dql, dqa = _rms_bwd(ql, rq, qa, _dot(dq_ref[...], qb_ref[...], 1, 0))
        dqa_ref[...] += dqa
        kva = kva_ref[...]
        kvn, rkv = _rms(kvl, kva)
        kvn_ref[...] = kvn.astype(BF16)
        dkvl, dkva = _rms_bwd(kvl, rkv, kva, _dot(dkv_ref[...], kvb_ref[...], 1, 0))
        dkva_ref[...] += dkva

        pt = ptail_ref[...].astype(F32)
        tail = jnp.where(blk % nbs == 0, 0.0, pt[:, :POOL_W])
        pooled, cnt, grp = _pool_fwd(zp, tail, blk % nbs)
        pb = pooled.astype(BF16)
        pw = pw_ref[...]
        dco = dco_ref[...].astype(F32)
        dps_ref[...] += jnp.sum(dco * _dot(pb, pw, 1, 0), axis=0, keepdims=True)
        dpo = (dco * ps_ref[...]).astype(BF16)
        dpw_ref[...] += _dot(pb, dpo, 0, 0)
        dpooled = _dot(dpo, pw, 1, 1)
        dpm = dpooled / cnt
        head = jnp.where(blk % nbs == nbs - 1, 0.0, head_ref[...])
        dz = _pool_bwd(dpooled, dpm, head, grp)
        head_ref[...] = dpm[:HALO, :]

        dproj_ref[:, :POOL_W] = dz.astype(BF16)
        dproj_ref[:, 256:640] = dql.astype(BF16)
        dproj_ref[:, 640:896] = dkvl.astype(BF16)
        dproj_ref[:, 896:1024] = dkr_ref[...].astype(BF16)
        dh = _dot(dproj_ref[...], win_ref[...].reshape(D, D), 1, 1)
        dxn, dgam = _rms_bwd(xv, r, gam, dh)
        dgam_ref[...] += dgam
        dx2_ref[...] = dx3_ref[...] + dxn

    def rev(w):
        return pl.BlockSpec((TB, w), lambda i: (nb - 1 - i, 0))

    ptail = pl.BlockSpec((HALO, D), lambda i: (jnp.maximum((nb - 1 - i) * (TB // HALO) - 1, 0), 0))
    return pl.pallas_call(
        body, name="odd_pre_bwd",
        out_shape=[jax.ShapeDtypeStruct((t, D), F32), jax.ShapeDtypeStruct((t, D), BF16),
                   jax.ShapeDtypeStruct((t, D), BF16), jax.ShapeDtypeStruct((t, Q_LORA), BF16),
                   jax.ShapeDtypeStruct((t, KV_LORA), BF16), jax.ShapeDtypeStruct((1, D), F32),
                   jax.ShapeDtypeStruct((1, Q_LORA), F32), jax.ShapeDtypeStruct((1, KV_LORA), F32),
                   jax.ShapeDtypeStruct((POOL_W, POOL_W), F32), jax.ShapeDtypeStruct((1, POOL_W), F32)],
        grid=(nb,),
        in_specs=[rev(D), rev(D), ptail, rev(D), rev(POOL_W), rev(HEADS * HP), rev(HEADS * HP), rev(128),
                  _const_spec((1, D)), _wspec(N_SQ, OFF_OIN), _const_spec((HEADS * HP, Q_LORA)),
                  _const_spec((HEADS * HP, KV_LORA)), _const_spec((1, Q_LORA)), _const_spec((1, KV_LORA)),
                  _const_spec((POOL_W, POOL_W)), _const_spec((1, POOL_W))],
        out_specs=[rev(D), rev(D), rev(D), rev(Q_LORA), rev(KV_LORA), _const_spec((1, D)), _const_spec((1, Q_LORA)),
                   _const_spec((1, KV_LORA)), _const_spec((POOL_W, POOL_W)), _const_spec((1, POOL_W))],
        scratch_shapes=[pltpu.VMEM((HALO, POOL_W), F32)],
        compiler_params=_cparams(1),
    )(x, proj, proj, dx3, dmix, dq, dkv, dkr, gamma, wg, qbt, kvbt, qa_g, kva_g, pw_bd, pscale)


def _attn_specs(seq):
    head = pl.BlockSpec((seq, HP), lambda b, h: (b, h))
    shared = pl.BlockSpec((seq, 128), lambda b, h: (b, 0))
    gain = pl.BlockSpec((1, HP), lambda b, h: (0, 0))
    return head, shared, gain


def _causal_bias(n):
    rows = lax.broadcasted_iota(jnp.int32, (n, n), 0)
    cols = lax.broadcasted_iota(jnp.int32, (n, n), 1)
    return jnp.where(cols <= rows, 0.0, NEG_INF)


def _attn_fwd(q, kv, kr, cos, sin, gq, gk, seq, comm=None):
    t = q.shape[0]
    qb = min(512, seq)

    def body(q_ref, kv_ref, kr_ref, c_ref, s_ref, gq_ref, gk_ref, o_ref, lse_ref):
        c, s = c_ref[...], s_ref[...]
        qf, _ = _qk_prep(q_ref[...].astype(F32), gq_ref[...], c, s)
        kin = jnp.concatenate([kv_ref[:, :128].astype(F32), kr_ref[...]], axis=1)
        kf, _ = _qk_prep(kin, gk_ref[...], c, s)
        qf, kf = qf.astype(BF16), kf.astype(BF16)
        v1 = jnp.concatenate([kv_ref[:, 128:], jnp.ones((seq, V_DIM), BF16)], axis=1)
        bias = _causal_bias(qb)
        for q0 in range(0, seq, qb):
            q1 = q0 + qb
            qblk = qf[q0:q1]
            s_dg = _dot(qblk, kf[q0:q1], 1, 1) + bias
            m = jnp.max(s_dg, axis=-1, keepdims=True)
            if q0:
                s_off = _dot(qblk, kf[:q0], 1, 1)
                m = jnp.maximum(m, jnp.max(s_off, axis=-1, keepdims=True))
            acc = _dot(jnp.exp(s_dg - m).astype(BF16), v1[q0:q1], 1, 0)
            if q0:
                acc = acc + _dot(jnp.exp(s_off - m).astype(BF16), v1[:q0], 1, 0)
            l = acc[:, V_DIM:]
            o_ref[q0:q1, :] = (acc[:, :V_DIM] / l).astype(BF16)
            lse_ref[q0:q1, :] = m + jnp.log(l)

    head, shared, gain = _attn_specs(seq)
    per_head = pl.BlockSpec((seq, V_DIM), lambda b, h: (b, h))
    return _call(
        body, "attn_fwd", (t // seq, HEADS),
        [head, head, shared, shared, shared, gain, gain], [per_head, per_head],
        [jax.ShapeDtypeStruct((t, HEADS * V_DIM), BF16), jax.ShapeDtypeStruct((t, HEADS * V_DIM), F32)],
        (q, kv, kr, cos, sin, gq, gk), (), comm)


def _attn_bwd(q, kv, kr, cos, sin, gq, gk, dmix, d_out, lse, seq, comm=None):
    t = q.shape[0]
    qb = min(512, seq)

    def body(q_ref, kv_ref, kr_ref, c_ref, s_ref, gq_ref, gk_ref, do_ref, o_ref, lse_ref,
             dq_ref, dkv_ref, dkr_ref, dgq_ref, dgk_ref, dqf_ref, dkf_ref, dv_ref):
        b, hd = pl.program_id(0), pl.program_id(1)

        @pl.when((b == 0) & (hd == 0))
        def _():
            dgq_ref[...] = jnp.zeros_like(dgq_ref)
            dgk_ref[...] = jnp.zeros_like(dgk_ref)

        c, sn = c_ref[...], s_ref[...]
        gq_v, gk_v = gq_ref[...], gk_ref[...]
        qin = q_ref[...].astype(F32)
        kin = jnp.concatenate([kv_ref[:, :128].astype(F32), kr_ref[...]], axis=1)
        qf32, rq = _qk_prep(qin, gq_v, c, sn)
        kf32, rk = _qk_prep(kin, gk_v, c, sn)
        qf, kf = qf32.astype(BF16), kf32.astype(BF16)
        vb = kv_ref[:, 128:]
        dkf_ref[...] = jnp.zeros_like(dkf_ref)
        dv_ref[...] = jnp.zeros_like(dv_ref)
        bias = _causal_bias(qb)
        for q0 in range(0, seq, qb):
            q1 = q0 + qb
            qblk = qf[q0:q1]
            do = do_ref[q0:q1, :]
            lse_col = lse_ref[q0:q1, 0:1]
            d_col = jnp.sum(do.astype(F32) * o_ref[q0:q1, :].astype(F32), axis=-1, keepdims=True)
            dq_acc = None
            for k0, k1, diag in ((q0, q1, True), (0, q0, False)):
                if k1 == k0:
                    continue
                s = _dot(qblk, kf[k0:k1], 1, 1)
                p = jnp.exp((s + bias if diag else s) - lse_col)
                dv_ref[k0:k1, :] += _dot(p.astype(BF16), do, 0, 0)
                ds = (p * (_dot(do, vb[k0:k1], 1, 1) - d_col)).astype(BF16)
                part = _dot(ds, kf[k0:k1], 1, 0)
                dq_acc = part if dq_acc is None else dq_acc + part
                dkf_ref[k0:k1, :] += _dot(ds, qblk, 0, 0)
            dqf_ref[q0:q1, :] = dq_acc
        dqin, dgq = _qk_prep_bwd(dqf_ref[...], qin, rq, gq_v, c, sn)
        dkin, dgk = _qk_prep_bwd(dkf_ref[...], kin, rk, gk_v, c, sn)
        dgq_ref[...] += dgq
        dgk_ref[...] += dgk
        dq_ref[...] = dqin.astype(BF16)
        dkv_ref[:, :128] = dkin[:, :128].astype(BF16)
        dkv_ref[:, 128:] = dv_ref[...].astype(BF16)

        @pl.when(hd == 0)
        def _():
            dkr_ref[...] = dkin[:, 128:]

        @pl.when(hd != 0)
        def _():
            dkr_ref[...] += dkin[:, 128:]

    head, shared, gain = _attn_specs(seq)
    per_head = pl.BlockSpec((seq, V_DIM), lambda b, h: (b, h))
    return _call(
        body, "attn_bwd", (t // seq, HEADS),
        [head, head, shared, shared, shared, gain, gain,
         pl.BlockSpec((seq, V_DIM), lambda b, h: (b, 2 + h)), per_head, per_head],
        [head, head, shared, gain, gain],
        [jax.ShapeDtypeStruct((t, HEADS * HP), BF16), jax.ShapeDtypeStruct((t, HEADS * HP), BF16),
         jax.ShapeDtypeStruct((t, 128), F32), jax.ShapeDtypeStruct((1, HP), F32),
         jax.ShapeDtypeStruct((1, HP), F32)],
        (q, kv, kr, cos, sin, gq, gk, dmix, d_out, lse),
        [pltpu.VMEM((seq, HP), F32), pltpu.VMEM((seq, HP), F32), pltpu.VMEM((seq, V_DIM), F32)], comm)


def _tn(a_list, b, tm, name, into=None, comm=None, after=None):
    t, n_out = b.shape
    widths = [a.shape[1] for a in a_list]
    tk = min(TK_DW, t)
    m, na, nk = sum(widths), len(a_list), t // tk
    assert na == 1 or tm == m

    def body(*refs):
        a_refs, b_ref, o_ref, acc_ref = refs[:na], refs[na], refs[-2], refs[-1]
        k = pl.program_id(1)

        @pl.when(k == 0)
        def _():
            acc_ref[...] = jnp.zeros_like(acc_ref)

        bb = b_ref[...].astype(BF16)
        m0 = 0
        for a_ref, w in zip(a_refs, widths):
            rows = slice(0, tm) if na == 1 else slice(m0, m0 + w)
            acc_ref[rows, :] += _dot(a_ref[...].astype(BF16), bb, 0, 0)
            m0 += w

        @pl.when(k == nk - 1)
        def _():
            o_ref[...] = acc_ref[...].astype(BF16).reshape(o_ref.shape)

    if na == 1:
        in_specs = [pl.BlockSpec((tk, tm), lambda i, k: (k, i))]
    else:
        in_specs = [pl.BlockSpec((tk, w), lambda i, k: (k, 0)) for w in widths]
    in_specs.append(pl.BlockSpec((tk, n_out), lambda i, k: (k, 0)))
    args = list(a_list) + [b]
    if into is None:
        out_spec = pl.BlockSpec((tm, n_out), lambda i, k: (i, 0))
        out_shape = jax.ShapeDtypeStruct((m, n_out), BF16)
        aliases = {}
    else:
        buf, n, off = into
        assert n_out == D and tm % n == 0 and off % n == 0 and (na == 1 or tm // n == N_DEV)
        idx = off // n
        out_spec = pl.BlockSpec((tm // n, n, D), lambda i, k: (i, idx, 0))
        out_shape = jax.ShapeDtypeStruct(buf.shape, BF16)
        in_specs.append(pl.BlockSpec(memory_space=pl.ANY))
        args.append(buf)
        aliases = {len(args) - 1: 0}
    if after is not None:
        in_specs.append(pl.BlockSpec(memory_space=pl.ANY))
        args.append(after)
    (res,), extra = _call(body, name, (m // tm, nk), in_specs, [out_spec], [out_shape], args,
                          [pltpu.VMEM((tm, n_out), F32)], comm, aliases)
    return (res, extra) if comm is not None else res


def _adamw(ws, gs, ms, vs, name, nblk=1):
    n = len(ws)
    c1 = 1.0 - B1 ** STEP
    c2 = 1.0 - B2 ** STEP

    def body(*refs):
        for a in range(n):
            w, g, m, v = (refs[k * n + a][...] for k in range(4))
            d_ref, m_ref, v_ref = (refs[(4 + k) * n + a] for k in range(3))
            m_new = B1 * m + (1.0 - B1) * g
            v_new = B2 * v + (1.0 - B2) * (g * g)
            d_ref[...] = -LR * ((m_new / c1) / (jnp.sqrt(v_new / c2) + ADAM_EPS) + WD * w)
            m_ref[...] = m_new
            v_ref[...] = v_new

    grid = (nblk,)
    assert all(w.shape[0] % nblk == 0 and (nblk == 1 or (w.shape[0] // nblk) % 8 == 0) for w in ws)
    specs = [pl.BlockSpec((w.shape[0] // nblk, w.shape[1]), lambda i: (i, 0)) for w in ws]
    outs, _ = _call(body, name, grid, specs * 4, specs * 3, [jax.ShapeDtypeStruct(w.shape, F32) for w in ws] * 3,
                    (*ws, *gs, *ms, *vs))
    return outs[:n], outs[n:2 * n], outs[2 * n:]


def _rows1024(a, rows):
    flat = a.reshape(-1, D)
    return jnp.pad(flat, ((0, rows - flat.shape[0]), (0, 0)))


def _pack_shards(even_w_in, even_w_out, odd_w_in, q_b, kv_b, odd_w_out, ffn_w_gate, ffn_w_up, ffn_w_down):
    mix0 = jnp.concatenate([even_w_in[0].T, jnp.zeros((OFF_EOUT - N_EIN, D), F32), even_w_out[0]], axis=0)
    gu = [jnp.concatenate([ffn_w_gate[layer].T, ffn_w_up[layer].T], axis=0) for layer in range(2)]
    mix1 = jnp.concatenate([jnp.pad(odd_w_in[0], ((0, 0), (0, D - ODD_IN))), odd_w_out[0],
                            _rows1024(q_b[0].T, N_QB), _rows1024(kv_b[0].T, N_KVB),
                            jnp.zeros((R_MIX1 - OFF_KVB - N_KVB, D), F32)], axis=0)
    return [c.astype(BF16) for c in (mix0, gu[0], ffn_w_down[0], mix1, gu[1], ffn_w_down[1])]


def _pad_heads(a):
    k = a.shape[1]
    return jnp.pad(a.reshape(HEADS, QK_DIM, k), ((0, 0), (0, HP - QK_DIM), (0, 0))).reshape(HEADS * HP, k)


def _small_pack(parts):
    flat = []
    for p in parts:
        v = p.reshape(-1)
        flat.append(jnp.pad(v, (0, (-v.shape[0]) % 1024)))
    return jnp.concatenate(flat).reshape(-1, 128)


def _small_unpack(buf, shapes):
    flat = buf.reshape(-1)
    out, off = [], 0
    for s in shapes:
        size = int(np.prod(s))
        out.append(flat[off:off + size].reshape(s))
        off += size + (-size) % 1024
    return out


def _step(x3d, positions, target3d, chunks, tile, where, mix_norm, ffn_norm, sg_ln_g, sg_w_s, sg_b_s,
          pool_w, q_norm, k_norm):
    bsz, seq, _ = x3d.shape
    t = bsz * seq
    x0 = x3d.reshape(t, D)
    target = target3d.reshape(t, D)
    my_mix0, my_gu0, my_d0, my_mix1, my_gu1, my_d1 = chunks

    lane = np.arange(128)
    inv_freq = np.where(lane < QK_ROPE, ROPE_THETA ** (-(2.0 * (lane % 32)) / QK_ROPE), 0.0)
    inv_freq = jnp.asarray(inv_freq.reshape(1, 128), F32)
    (cos, sin), (w_mix0, tiles) = _rope_tables(positions.reshape(t, 1), inv_freq, _gather_comm([my_mix0, tile]))

    conv_w = tiles[:, 0:3, 0:64].transpose(1, 0, 2).reshape(3, SC_W)
    pool_scale = tiles[:, 3, 0:32].reshape(1, POOL_W)
    q_a_norm = tiles[:, 4, 0:48].reshape(1, Q_LORA)
    kv_a_norm = tiles[:, 5, 0:32].reshape(1, KV_LORA)
    ws = sg_w_s[0]
    bst = jnp.pad(sg_b_s[0].T, ((0, 0), (0, 128 - SG_HEADS)))
    cw = jnp.pad(conv_w, ((0, 8 - 3), (0, 0)))
    pw_bd = jax.scipy.linalg.block_diag(*[pool_w[0, g] for g in range(4)]).astype(BF16)
    gq = jnp.pad(q_norm * ATT_SCALE, ((0, 0), (0, HP - QK_DIM)))
    gk = jnp.pad(k_norm, ((0, 0), (0, HP - QK_DIM)))

    (x1, proj_e), (w_gu0,) = _even_fwd(x0, w_mix0, mix_norm[0:1], sg_ln_g, ws, bst, cw, seq, _gather_comm([my_gu0]))
    (g0, u0, act0), (w_d0, w_mix1) = _ffn_up(x1, w_gu0, ffn_norm[0:1], "ffn_up0", _gather_comm([my_d0, my_mix1]))
    (x2,), (w_d1,) = _ffn_down(x1, act0, w_d0, "ffn_down0", _gather_comm([my_d1]))
    qbt = _pad_heads(w_mix1[:, OFF_QB:OFF_QB + N_QB_USED, :].reshape(HEADS * QK_DIM, Q_LORA))
    kvbt = w_mix1[:, OFF_KVB:OFF_KVB + N_KVB, :].reshape(HEADS * HP, KV_LORA)
    (proj_o, q, kv, kr, c_out), _ = _odd_pre_fwd(x2, w_mix1, mix_norm[1:2], qbt, kvbt, q_a_norm, kv_a_norm,
                                                pw_bd, pool_scale, seq)
    (d_out, lse), (w_gu1,) = _attn_fwd(q, kv, kr, cos, sin, gq, gk, seq, _gather_comm([my_gu1]))
    x3, dy, g1, u1, loss_tile = _last_block_fwd(x2, c_out, d_out, w_mix1, w_gu1, w_d1, ffn_norm[1:2], target)

    def chunk(rows, padded=False):
        return jnp.zeros((N_DEV, rows, D), BF16) if padded else lax.empty((N_DEV, rows, D), BF16)

    (dx3, act1, dg1, du1, h3, dgam_f1, dmix_o), _ = _ffn_bwd(x3, g1, u1, dy, w_gu1, w_d1, ffn_norm[1:2], "ffn_bwd1",
                                                           None, w_mix1)
    gp_ffn1 = _tn([dg1], h3, 1408, "dw_gate1", (chunk(R_GU + N_FF), N_FF, OFF_GATE))
    gp_ffn1 = _tn([du1], h3, 1408, "dw_up1", (gp_ffn1, N_FF, OFF_UP))
    gp_ffn1 = _tn([act1], dy, 1408, "dw_down1", (gp_ffn1, N_FF, R_GU))

    gp_mix1, (ga_ffn1,) = _tn([c_out, d_out], dx3, D, "dw_oout", (chunk(R_MIX1, True), N_SQ, OFF_OOUT),
                              _pair_exchange_comm(gp_ffn1))
    pb_ffn1 = _rs_pair_sum(gp_ffn1, ga_ffn1, where, "rs_pair_sum_ffn1")
    (dq, dkv, dkr, dgq, dgk), (gb_ffn1,) = _attn_bwd(q, kv, kr, cos, sin, gq, gk, dmix_o, d_out, lse, seq,
                                                    _chip_exchange_comm(pb_ffn1))
    (dx2, dproj_o, h2, qn, kvn, dgam_m1, dqa, dkva, dpw_bd, dps) = _odd_pre_bwd(
        x2, proj_o, dx3, dmix_o, dq, dkv, dkr, w_mix1, mix_norm[1:2], qbt, kvbt, q_a_norm, kv_a_norm, pw_bd,
        pool_scale, seq)
    gp_mix1 = _tn([h2], dproj_o, D, "dw_oin", (gp_mix1, N_SQ, OFF_OIN))
    d_qbt = _tn([dq], qn, HEADS * HP, "dw_qb")
    d_qb_rows = d_qbt.reshape(HEADS, HP, Q_LORA)[:, :QK_DIM].reshape(N_DEV, N_QB_USED, D)
    d_kvb_rows = _tn([dkv], kvn, HEADS * HP, "dw_kvb").reshape(N_DEV, N_KVB, D)
    gp_mix1 = lax.dynamic_update_slice(gp_mix1, d_qb_rows, (0, OFF_QB, 0))
    gp_mix1 = lax.dynamic_update_slice(gp_mix1, d_kvb_rows, (0, OFF_KVB, 0))

    (dx1, act0, dg0, du0, h1, dgam_f0), (ga_mix1,) = _ffn_bwd(x1, g0, u0, dx2, w_gu0, w_d0, ffn_norm[0:1], "ffn_bwd0",
                                                             _pair_exchange_comm(gp_mix1))
    pb_mix1 = _rs_pair_sum(gp_mix1, ga_mix1, where, "rs_pair_sum_mix1")
    *open_mix1, started = _chip_exchange_start(pb_mix1, "mix1")
    gp_ffn0a = _tn([dg0], h1, 1408, "dw_gate0", (chunk(R_GU), N_FF, OFF_GATE), None, started)
    gp_ffn0a = _tn([du0], h1, 1408, "dw_up0", (gp_ffn0a, N_FF, OFF_UP))
    gp_ffn0b, (ga_ffn0a,) = _tn([act0], dx2, 1408, "dw_down0", (chunk(N_FF), N_FF, 0),
                                _pair_exchange_comm(gp_ffn0a))
    pb_ffn0a = _rs_pair_sum(gp_ffn0a, ga_ffn0a, where, "rs_pair_sum_ffn0a")
    *open_ffn0a, started = _chip_exchange_start(pb_ffn0a, "ffn0a")

    (dx0, dproj_e, mix_e, h0, dgam_m0, dws, dbc, dlng, dcw), _ = _even_bwd(
        x0, proj_e, dx1, w_mix0, mix_norm[0:1], sg_ln_g, ws, bst, cw + 0.0 * started[:, :1], seq)

    small = _small_pack([
        jnp.concatenate([dgam_m0, dgam_m1], 0), jnp.concatenate([dgam_f0, dgam_f1], 0), dlng,
        dws[None], dbc[:, :SG_HEADS].T[None], dcw[:3],
        jnp.stack([dpw_bd[g * POOL_GD:(g + 1) * POOL_GD, g * POOL_GD:(g + 1) * POOL_GD] for g in range(4)])[None],
        dps, dqa, dkva, dgq[:, :QK_DIM] * ATT_SCALE, dgk[:, :QK_DIM], loss_tile[0:1, 0:1]])
    gp_eout, (small_all, ga_ffn0b) = _tn([mix_e], dx1, D, "dw_eout", (chunk(N_SQ), N_SQ, 0),
                                         _both(_gather_comm([small]), _pair_exchange_comm(gp_ffn0b)))
    pb_ffn0b = _rs_pair_sum(gp_ffn0b, ga_ffn0b, where, "rs_pair_sum_ffn0b")
    *open_ffn0b, started = _chip_exchange_start(pb_ffn0b, "ffn0b")
    gp_ein, (ga_eout,) = _tn([dproj_e], h0, 1280, "dw_ein", (chunk(N_EIN), N_EIN, 0), _pair_exchange_comm(gp_eout),
                             started)
    pb_eout = _rs_pair_sum(gp_eout, ga_eout, where, "rs_pair_sum_eout")
    *open_eout, started = _chip_exchange_start(pb_eout, "eout")
    small_sum = _small_unpack(_sum_gathered(small_all), SMALL_SHAPES)
    in_flight = (open_ffn0a, open_ffn0b, open_mix1, open_eout)
    return dx0.reshape(bsz, seq, D), (pb_ffn1, gb_ffn1), in_flight, (gp_ein, started), small_sum


SMALL_SHAPES = [(2, D), (2, D), (1, SG_W), (1, SG_HEADS, 128, 128), (1, SG_HEADS, 128), (3, SC_W),
                (1, 4, POOL_GD, POOL_GD), (1, POOL_W), (1, Q_LORA), (1, KV_LORA), (1, QK_DIM), (1, QK_DIM), (1, 1)]


def kernel(x, positions, mix_norm, ffn_norm, even_w_in, sg_ln_g, sg_w_s, sg_b_s, sc_conv_w, even_w_out, odd_w_in, pool_w, pool_scale, q_a_norm, q_b, kv_a_norm, kv_b, q_norm, k_norm, odd_w_out, ffn_w_gate, ffn_w_up, ffn_w_down, loss_target, m_mix_norm, m_ffn_norm, m_even_w_in, m_sg_ln_g, m_sg_w_s, m_sg_b_s, m_sc_conv_w, m_even_w_out, m_odd_w_in, m_pool_w, m_pool_scale, m_q_a_norm, m_q_b, m_kv_a_norm, m_kv_b, m_q_norm, m_k_norm, m_odd_w_out, m_ffn_w_gate, m_ffn_w_up, m_ffn_w_down, v_mix_norm, v_ffn_norm, v_even_w_in, v_sg_ln_g, v_sg_w_s, v_sg_b_s, v_sc_conv_w, v_even_w_out, v_odd_w_in, v_pool_w, v_pool_scale, v_q_a_norm, v_q_b, v_kv_a_norm, v_kv_b, v_q_norm, v_k_norm, v_odd_w_out, v_ffn_w_gate, v_ffn_w_up, v_ffn_w_down):
    xi, yi, ci = _place()
    me = 4 * xi + 2 * yi + ci

    chunks = _pack_shards(even_w_in, even_w_out, odd_w_in, q_b, kv_b, odd_w_out, ffn_w_gate, ffn_w_up, ffn_w_down)

    def lane_pad(a):
        return jnp.pad(a, ((0, 0), (0, 128 - a.shape[1])))

    tile = jnp.concatenate([lane_pad(sc_conv_w[0]), lane_pad(pool_scale), lane_pad(q_a_norm), lane_pad(kv_a_norm),
                            jnp.zeros((2, 128), F32)], axis=0)
    chip = 2 * xi + yi
    where = jnp.stack([ci, chip, chip ^ 2, chip ^ 1, chip ^ 3]).astype(jnp.int32)
    grad_x, (pb_ffn1, gb_ffn1), in_flight, (gp_ein, started), tot = _step(
        x, positions, loss_target, chunks, tile, where, mix_norm, ffn_norm, sg_ln_g, sg_w_s, sg_b_s,
        pool_w, q_norm, k_norm)

    (ga_ein,) = _comm_alone(_pair_exchange_comm(gp_ein), "rs_pair_exchange_ein")
    pb_ein = _rs_pair_sum(gp_ein, ga_ein, where + (0.0 * started[0, :1]).astype(jnp.int32), "rs_pair_sum_ein")
    ein_sems, pb_ein, land_ein, started = _chip_exchange_start(pb_ein, "ein")
    landed = [_chip_exchange_wait(*parts, started, tag)
              for parts, tag in zip(in_flight, ("ffn0a", "ffn0b", "mix1", "eout"))]
    gsh_ffn0a, gsh_ffn0b, gsh_mix1, gsh_eout, gsh_ffn1 = _rs_final_sums(
        [pb for pb, _ in landed] + [pb_ffn1], [gb for _, gb in landed] + [gb_ffn1], "rs_final_sums", started)

    (g_mix, g_ffn, g_lng, g_ws, g_bs, g_cw_full, g_pw, g_ps_full, g_qa_full, g_kva_full, g_qn, g_kn, loss) = tot
    g_cw = lax.dynamic_slice_in_dim(g_cw_full, me * 64, 64, axis=1)[None]
    g_ps = lax.dynamic_slice_in_dim(g_ps_full, me * 32, 32, axis=1)
    g_qa = lax.dynamic_slice_in_dim(g_qa_full, me * 48, 48, axis=1)
    g_kva = lax.dynamic_slice_in_dim(g_kva_full, me * 32, 32, axis=1)

    def tr(a):
        return jnp.swapaxes(a, -1, -2)

    g_gate = tr(jnp.stack([gsh_ffn0a[OFF_GATE:OFF_GATE + N_FF], gsh_ffn1[OFF_GATE:OFF_GATE + N_FF]]))
    g_up = tr(jnp.stack([gsh_ffn0a[OFF_UP:OFF_UP + N_FF], gsh_ffn1[OFF_UP:OFF_UP + N_FF]]))
    g_down = jnp.stack([gsh_ffn0b, gsh_ffn1[R_GU:R_GU + N_FF]])
    g_oin = gsh_mix1[OFF_OIN:OFF_OIN + N_SQ, :ODD_IN][None]
    g_oout = gsh_mix1[OFF_OOUT:OFF_OOUT + N_SQ][None]
    g_qb = tr(gsh_mix1[OFF_QB:OFF_QB + N_QB_USED].reshape(1, 144, Q_LORA))
    g_kvb = tr(gsh_mix1[OFF_KVB:OFF_KVB + N_KVB].reshape(1, 192, KV_LORA))
    transposed = ("even_w_in", "odd_w_in", "q_b", "kv_b", "ffn_w_gate", "ffn_w_up")

    names = ("mix_norm", "ffn_norm", "even_w_in", "sg_ln_g", "sg_w_s", "sg_b_s", "sc_conv_w", "even_w_out",
             "odd_w_in", "pool_w", "pool_scale", "q_a_norm", "q_b", "kv_a_norm", "kv_b", "q_norm", "k_norm",
             "odd_w_out", "ffn_w_gate", "ffn_w_up", "ffn_w_down")
    grads = dict(mix_norm=g_mix, ffn_norm=g_ffn, sg_ln_g=g_lng, sg_w_s=g_ws, sg_b_s=g_bs,
                 sc_conv_w=g_cw, odd_w_in=g_oin, pool_w=g_pw, pool_scale=g_ps, q_a_norm=g_qa,
                 q_b=g_qb, kv_a_norm=g_kva, kv_b=g_kvb, q_norm=g_qn, k_norm=g_kn, odd_w_out=g_oout,
                 ffn_w_gate=g_gate, ffn_w_up=g_up, ffn_w_down=g_down)
    weights = dict(mix_norm=mix_norm, ffn_norm=ffn_norm, even_w_in=even_w_in, sg_ln_g=sg_ln_g, sg_w_s=sg_w_s,
                   sg_b_s=sg_b_s, sc_conv_w=sc_conv_w, even_w_out=even_w_out, odd_w_in=odd_w_in, pool_w=pool_w,
                   pool_scale=pool_scale, q_a_norm=q_a_norm, q_b=q_b, kv_a_norm=kv_a_norm, kv_b=kv_b, q_norm=q_norm,
                   k_norm=k_norm, odd_w_out=odd_w_out, ffn_w_gate=ffn_w_gate, ffn_w_up=ffn_w_up,
                   ffn_w_down=ffn_w_down)
    m_in = dict(mix_norm=m_mix_norm, ffn_norm=m_ffn_norm, even_w_in=m_even_w_in, sg_ln_g=m_sg_ln_g, sg_w_s=m_sg_w_s,
                sg_b_s=m_sg_b_s, sc_conv_w=m_sc_conv_w, even_w_out=m_even_w_out, odd_w_in=m_odd_w_in,
                pool_w=m_pool_w, pool_scale=m_pool_scale, q_a_norm=m_q_a_norm, q_b=m_q_b, kv_a_norm=m_kv_a_norm,
                kv_b=m_kv_b, q_norm=m_q_norm, k_norm=m_k_norm, odd_w_out=m_odd_w_out, ffn_w_gate=m_ffn_w_gate,
                ffn_w_up=m_ffn_w_up, ffn_w_down=m_ffn_w_down)
    v_in = dict(mix_norm=v_mix_norm, ffn_norm=v_ffn_norm, even_w_in=v_even_w_in, sg_ln_g=v_sg_ln_g, sg_w_s=v_sg_w_s,
                sg_b_s=v_sg_b_s, sc_conv_w=v_sc_conv_w, even_w_out=v_even_w_out, odd_w_in=v_odd_w_in,
                pool_w=v_pool_w, pool_scale=v_pool_scale, q_a_norm=v_q_a_norm, q_b=v_q_b, kv_a_norm=v_kv_a_norm,
                kv_b=v_kv_b, q_norm=v_q_norm, k_norm=v_k_norm, odd_w_out=v_odd_w_out, ffn_w_gate=v_ffn_w_gate,
                ffn_w_up=v_ffn_w_up, ffn_w_down=v_ffn_w_down)
    delta, new_m, new_v = {}, {}, {}

    def as2d(k, a):
        a = tr(a) if k in transposed else a
        return a.reshape(-1, a.shape[-1])

    def back(k, a):
        shape = weights[k].shape
        return tr(a.reshape(shape[:-2] + (shape[-1], shape[-2]))) if k in transposed else a.reshape(shape)

    def update(group, name, nblk=1):
        outs = _adamw([as2d(k, weights[k]) for k in group], [as2d(k, grads[k]) for k in group],
                      [as2d(k, m_in[k]) for k in group], [as2d(k, v_in[k]) for k in group], name, nblk)
        for i, k in enumerate(group):
            delta[k], new_m[k], new_v[k] = (back(k, o[i]) for o in outs)

    grads["even_w_out"] = gsh_eout[None]
    update(["ffn_w_gate", "ffn_w_up", "ffn_w_down"], "adamw_ffn", 4)
    update(["even_w_out", "odd_w_in", "odd_w_out"], "adamw_mix", 2)
    update([k for k in names if k not in delta and k != "even_w_in"], "adamw_small")

    pb_ein, gb_ein = _chip_exchange_wait(ein_sems, pb_ein, land_ein, new_v["k_norm"], "ein")
    (gsh_ein,) = _rs_final_sums([pb_ein], [gb_ein], "rs_final_sum_ein")
    grads["even_w_in"] = tr(gsh_ein[None])
    update(["even_w_in"], "adamw_even_w_in", 2)

    return (loss.reshape(()), grad_x, *[grads[k] for k in names], *[delta[k] for k in names],
            *[new_m[k] for k in names], *[new_v[k] for k in names])
```

```python
import functools

import numpy as np
import jax
import jax.numpy as jnp
from jax import lax
from jax.experimental import pallas as pl
from jax.experimental.pallas import tpu as pltpu

F32 = jnp.float32
BF16 = jnp.bfloat16
MESH = pl.DeviceIdType.MESH

D = 1024
EPS = 1e-6
NEG_INF = -1e30
SG_HEADS, SG_HD, SG_W, SG_CHUNK = 4, 128, 512, 128
SC_W = 512
EVEN_IN = 2560
POOL_W = 256
POOL_GD = 64
Q_LORA, KV_LORA, QK_ROPE, QK_NOPE, V_DIM = 384, 256, 64, 128, 128
QK_DIM = QK_NOPE + QK_ROPE
HEADS = 6
HP = 256
ODD_IN = 960
D_FF = 2816
ROPE_THETA = 10000.0
ATT_SCALE = QK_DIM ** -0.5
LR, B1, B2, ADAM_EPS, WD, STEP = 0.001, 0.9, 0.999, 1e-08, 0.01, 10

N_DEV = 8
TB = 512
TB_FFN_BWD = 256
TK_DW = 1024
HALO = 16
VMEM_LIMIT = 56 * 1024 * 1024

N_EIN, N_FF, N_SQ = 320, 352, 128
OFF_EIN, OFF_EOUT = 0, 384
OFF_GATE, OFF_UP, R_GU = 0, 352, 704
OFF_OIN, OFF_OOUT, OFF_QB, OFF_KVB, R_MIX1 = 0, 128, 256, 320, 384
N_QB, N_QB_USED, N_KVB = 64, 54, 48

INV_SQRT2 = 0.7071067811865476
INV_SQRT_2PI = 0.3989422804014327


def _dot(a, b, ca, cb):
    return lax.dot_general(a, b, (((ca,), (cb,)), ((), ())), preferred_element_type=F32)


def _cparams(n_axes=1):
    return pltpu.CompilerParams(dimension_semantics=("arbitrary",) * n_axes, vmem_limit_bytes=VMEM_LIMIT)


def _wspec(n, off):
    assert off % n == 0
    idx = off // n
    return pl.BlockSpec((N_DEV, n, D), lambda i: (0, idx, 0), pipeline_mode=pl.Buffered(1))


def _const_spec(shape):
    zeros = (0,) * len(shape)
    return pl.BlockSpec(shape, lambda *_: zeros)


class _Comm:
    def __init__(self, ins, out_shapes, sems, start, wait, mid=None):
        self.ins, self.out_shapes, self.sems, self.start, self.wait, self.mid = ins, out_shapes, sems, start, wait, mid


def _both(c1, c2):
    def split(f1, f2):
        def run(ins, outs, sems):
            f1(ins[:len(c1.ins)], outs[:len(c1.out_shapes)], sems[:len(c1.sems)])
            f2(ins[len(c1.ins):], outs[len(c1.out_shapes):], sems[len(c1.sems):])
        return run

    def nothing(ins, outs, sems):
        pass

    mid = None if c1.mid is None and c2.mid is None else split(c1.mid or nothing, c2.mid or nothing)
    return _Comm(c1.ins + c2.ins, c1.out_shapes + c2.out_shapes, c1.sems + c2.sems,
                 split(c1.start, c2.start), split(c1.wait, c2.wait), mid)


def _call(body, name, grid, in_specs, out_specs, out_shape, args, scratch_shapes=(), comm=None, aliases=None):
    n_axes = len(grid)
    aliases = aliases or {}
    if comm is None:
        res = pl.pallas_call(
            body, name=name, grid=grid, in_specs=list(in_specs), out_specs=list(out_specs),
            out_shape=list(out_shape), scratch_shapes=list(scratch_shapes), input_output_aliases=aliases,
            compiler_params=_cparams(n_axes))(*args)
        return list(res), []
    ni, no, ns = len(in_specs), len(out_specs), len(scratch_shapes)
    ci, co = len(comm.ins), len(comm.out_shapes)
    n_steps = int(np.prod(grid))

    def carrier(*refs):
        ins, cin = refs[:ni], refs[ni:ni + ci]
        outs, cout = refs[ni + ci:ni + ci + no], refs[ni + ci + no:ni + ci + no + co]
        scr, sems = refs[ni + ci + no + co:ni + ci + no + co + ns], refs[ni + ci + no + co + ns:]
        step = 0
        for a in range(n_axes):
            step = step * grid[a] + pl.program_id(a)

        @pl.when(step == 0)
        def _():
            comm.start(cin, cout, sems)

        body(*ins, *outs, *scr)

        if comm.mid is not None and n_steps >= 4:
            @pl.when(step == n_steps // 2)
            def _():
                comm.mid(cin, cout, sems)

        @pl.when(step == n_steps - 1)
        def _():
            if comm.mid is not None and n_steps < 4:
                comm.mid(cin, cout, sems)
            comm.wait(cin, cout, sems)

    any_spec = pl.BlockSpec(memory_space=pl.ANY)
    res = pl.pallas_call(
        carrier, name=name, grid=grid, in_specs=list(in_specs) + [any_spec] * ci,
        out_specs=list(out_specs) + [any_spec] * co, out_shape=list(out_shape) + list(comm.out_shapes),
        scratch_shapes=list(scratch_shapes) + list(comm.sems), input_output_aliases=aliases,
        compiler_params=_cparams(n_axes))(*args, *comm.ins)
    return list(res[:no]), list(res[no:])


def _comm_alone(comm, name):
    ci, co = len(comm.ins), len(comm.out_shapes)

    def body(*refs):
        cin, cout, sems = refs[:ci], refs[ci:ci + co], refs[ci + co:]
        comm.start(cin, cout, sems)
        if comm.mid is not None:
            comm.mid(cin, cout, sems)
        comm.wait(cin, cout, sems)

    any_spec = pl.BlockSpec(memory_space=pl.ANY)
    res = pl.pallas_call(
        body, name=name, out_shape=list(comm.out_shapes), in_specs=[any_spec] * ci, out_specs=[any_spec] * co,
        scratch_shapes=list(comm.sems))(*comm.ins)
    return list(res)


def _rms(x, g):
    r = lax.rsqrt(jnp.mean(x * x, axis=-1, keepdims=True) + EPS)
    return x * r * g, r


def _rms_bwd(x, r, g, dy):
    xh = x * r
    dxh = dy * g
    dx = r * (dxh - xh * jnp.mean(dxh * xh, axis=-1, keepdims=True))
    dg = jnp.sum(dy * xh, axis=0, keepdims=True)
    return dx, dg


def _gelu(x):
    return 0.5 * x * (1.0 + lax.erf(x * INV_SQRT2))


def _gelu_grad(x):
    return 0.5 * (1.0 + lax.erf(x * INV_SQRT2)) + x * jnp.exp(-0.5 * x * x) * INV_SQRT_2PI


def _shift_down(a, k):
    rows = lax.broadcasted_iota(jnp.int32, a.shape, 0)
    return jnp.where(rows >= k, pltpu.roll(a, k, 0), 0.0)


def _shift_up(a, k):
    n = a.shape[0]
    rows = lax.broadcasted_iota(jnp.int32, a.shape, 0)
    return jnp.where(rows < n - k, pltpu.roll(a, n - k, 0), 0.0)


def _tril_bf16(w):
    r = lax.broadcasted_iota(jnp.int32, w.shape, 0)
    c = lax.broadcasted_iota(jnp.int32, w.shape, 1)
    return jnp.where(r >= c, w, 0.0).astype(BF16)


def _ln_head(vh, g):
    mu = jnp.mean(vh, axis=-1, keepdims=True)
    xc = vh - mu
    rr = lax.rsqrt(jnp.mean(xc * xc, axis=-1, keepdims=True) + EPS)
    xh = xc * rr
    return xh * g, xh, rr


def _conv_fwd(z, tail, cw_ref):
    ext = jnp.concatenate([tail, z], axis=0)
    zs1 = _shift_down(ext, 1)[HALO:]
    zs2 = _shift_down(ext, 2)[HALO:]
    y = cw_ref[2:3, :] * z + cw_ref[1:2, :] * zs1 + cw_ref[0:1, :] * zs2
    return y, zs1, zs2


def _pool_cnt(shape, blk_in_seq):
    rows = lax.broadcasted_iota(jnp.int32, shape, 0)
    grp = lax.broadcasted_iota(jnp.int32, shape, 1) // POOL_GD
    win = jnp.where(grp == 0, 2, jnp.where(grp == 1, 4, jnp.where(grp == 2, 8, 16)))
    tpos = blk_in_seq * shape[0] + rows + 1
    return jnp.minimum(tpos, win).astype(F32), grp


def _pool_select(grp, s2, s4, s8, s16):
    return jnp.where(grp == 0, s2, jnp.where(grp == 1, s4, jnp.where(grp == 2, s8, s16)))


def _pool_fwd(z, tail, blk_in_seq):
    ext = jnp.concatenate([tail, z], axis=0)
    s2 = ext + _shift_down(ext, 1)
    s4 = s2 + _shift_down(s2, 2)
    s8 = s4 + _shift_down(s4, 4)
    s16 = s8 + _shift_down(s8, 8)
    cnt, grp = _pool_cnt(z.shape, blk_in_seq)
    sums = _pool_select(grp, s2[HALO:], s4[HALO:], s8[HALO:], s16[HALO:])
    return sums / cnt - z, cnt, grp


def _pool_bwd(dpooled, dpm, head, grp):
    n = dpm.shape[0]
    ext = jnp.concatenate([dpm, head], axis=0)
    u2 = ext + _shift_up(ext, 1)
    u4 = u2 + _shift_up(u2, 2)
    u8 = u4 + _shift_up(u4, 4)
    u16 = u8 + _shift_up(u8, 8)
    return _pool_select(grp, u2[:n], u4[:n], u8[:n], u16[:n]) - dpooled


def _lane_sums(a):
    return _dot(a.astype(BF16), jnp.ones((a.shape[1], a.shape[1]), BF16), 1, 0)


def _swap_halves(y1):
    src = lax.broadcasted_iota(jnp.int32, (128, 128), 0)
    dst = lax.broadcasted_iota(jnp.int32, (128, 128), 1)
    perm = jnp.where(((dst < 32) & (src == dst + 32)) | ((dst >= 32) & (dst < QK_ROPE) & (src == dst - 32)), 1.0, 0.0)
    return _dot(y1.astype(BF16), perm.astype(BF16), 1, 0)


def _rope(y1, c, s):
    return y1 * c + _swap_halves(y1) * s


def _rope_bwd(d1, c, s):
    return d1 * c + _swap_halves(d1 * s)


def _qk_prep(x, g, c, s):
    r = lax.rsqrt(_lane_sums(x * x) * (1.0 / QK_DIM) + EPS)
    y = x * r * g
    return jnp.concatenate([y[:, :128], _rope(y[:, 128:], c, s)], axis=1), r


def _qk_prep_bwd(dout, x, r, g, c, s):
    dy = jnp.concatenate([dout[:, :128], _rope_bwd(dout[:, 128:], c, s)], axis=1)
    xh = x * r
    dxh = dy * g
    dx = r * (dxh - xh * (_lane_sums(dxh * xh) * (1.0 / QK_DIM)))
    return dx, jnp.sum(dy * xh, axis=0, keepdims=True)


def _place():
    return lax.axis_index("x"), lax.axis_index("y"), lax.axis_index("c")


def _gather_comm(arrs):
    n = len(arrs)

    def halves(a):
        rows = arrs[a].shape[0]
        tile = 16 if arrs[a].dtype == BF16 else 8
        top = rows // 2 if rows % (2 * tile) == 0 else rows
        return (0, top), (top, rows - top)

    def plan(ins, outs, sems):
        send_sems, recv_sems, local_sems = sems
        x, y, c = _place()
        me, sib, xn, yn, dg = (x, y, c), (x, y, 1 - c), (1 - x, y, c), (x, 1 - y, c), (1 - x, 1 - y, c)

        def slot(a, dev, part=None):
            ref = outs[a].at[4 * dev[0] + 2 * dev[1] + dev[2]]
            return ref if part is None else ref.at[pl.ds(part[0], part[1])]

        def copy(a, k, block, to, src=None, part=None):
            return pltpu.make_async_remote_copy(
                src_ref=slot(a, block, part) if src is None else src, dst_ref=slot(a, block, part),
                send_sem=send_sems.at[a, k], recv_sem=recv_sems.at[a, k], device_id=to, device_id_type=MESH)

        local = [pltpu.make_async_copy(ins[a], slot(a, me), local_sems.at[a]) for a in range(n)]
        return me, sib, xn, yn, dg, copy, local

    def start(ins, outs, sems):
        me, sib, xn, yn, _, copy, local = plan(ins, outs, sems)
        for a in range(n):
            local[a].start()
            for k, to in enumerate((sib, xn, yn)):
                copy(a, k, me, to, src=ins[a]).start()

    def mid(ins, outs, sems):
        me, sib, xn, yn, _, copy, _ = plan(ins, outs, sems)
        for a in range(n):
            top, bottom = halves(a)
            copy(a, 1, xn, me).wait_recv()
            copy(a, 3, xn, yn, part=top).start()
            copy(a, 5, xn, sib).start()
            copy(a, 2, yn, me).wait_recv()
            if bottom[1]:
                copy(a, 4, yn, xn, part=bottom).start()
            copy(a, 6, yn, sib).start()

    def wait(ins, outs, sems):
        me, sib, xn, yn, dg, copy, local = plan(ins, outs, sems)
        other = lambda dev: (dev[0], dev[1], 1 - dev[2])
        for a in range(n):
            top, bottom = halves(a)
            copy(a, 3, dg, me, part=top).wait_recv()
            if bottom[1]:
                copy(a, 4, dg, me, part=bottom).wait_recv()
            copy(a, 7, dg, sib).start()
        for a in range(n):
            top, bottom = halves(a)
            for k, block in ((0, sib), (5, other(xn)), (6, other(yn)), (7, other(dg))):
                copy(a, k, block, me).wait_recv()
            for k, block in ((0, me), (1, me), (2, me), (5, xn), (6, yn), (7, dg)):
                copy(a, k, block, me, src=ins[a] if k < 3 else None).wait_send()
            copy(a, 3, xn, me, part=top).wait_send()
            if bottom[1]:
                copy(a, 4, yn, me, part=bottom).wait_send()
            local[a].wait()

    return _Comm(
        list(arrs), [jax.ShapeDtypeStruct((N_DEV,) + a.shape, a.dtype) for a in arrs],
        [pltpu.SemaphoreType.DMA((n, 8)), pltpu.SemaphoreType.DMA((n, 8)), pltpu.SemaphoreType.DMA((n,))],
        start, wait, mid)


def _sum_gathered(g):
    rows = g.shape[1]

    def body(g_ref, sum_ref):
        total = g_ref[0]
        for d in range(1, N_DEV):
            total = total + g_ref[d]
        sum_ref[...] = total

    return pl.pallas_call(
        body, name="sum_gathered_small", out_shape=jax.ShapeDtypeStruct((rows, 128), F32), grid=(1,),
        in_specs=[pl.BlockSpec((N_DEV, rows, 128), lambda i: (0, 0, 0))],
        out_specs=pl.BlockSpec((rows, 128), lambda i: (0, 0)), compiler_params=_cparams(1),
    )(g)


def _sum_rows(rows):
    return rows if rows <= 512 else rows // 2


def _pair_exchange_comm(gp):
    _, rows, cols = gp.shape

    def copies(ins, outs, sems):
        send_sems, recv_sems = sems
        x, y, c = _place()
        return [pltpu.make_async_remote_copy(
            src_ref=ins[0].at[2 * j + (1 - c)], dst_ref=outs[0].at[j], send_sem=send_sems.at[j],
            recv_sem=recv_sems.at[j], device_id=(x, y, 1 - c), device_id_type=MESH) for j in range(4)]

    def start(ins, outs, sems):
        for cp in copies(ins, outs, sems):
            cp.start()

    def wait(ins, outs, sems):
        for cp in copies(ins, outs, sems):
            cp.wait()

    return _Comm([gp], [jax.ShapeDtypeStruct((4, rows, cols), gp.dtype)],
                 [pltpu.SemaphoreType.DMA((4,)), pltpu.SemaphoreType.DMA((4,))], start, wait)


def _rs_pair_sum(gp, got, where, name):
    _, rows, cols = got.shape
    rb = _sum_rows(rows)
    gp4 = gp.reshape(4, 2, rows, cols)

    def body(w_ref, a_ref, b_ref, o_ref):
        o_ref[0] = (a_ref[0, 0].astype(F32) + b_ref[0].astype(F32)).astype(o_ref.dtype)

    return pl.pallas_call(
        body, name=name, out_shape=jax.ShapeDtypeStruct((4, rows, cols), gp.dtype),
        grid_spec=pltpu.PrefetchScalarGridSpec(
            num_scalar_prefetch=1, grid=(4, rows // rb),
            in_specs=[pl.BlockSpec((1, 1, rb, cols), lambda k, r, w: (w[1 + k], w[0], r, 0)),
                      pl.BlockSpec((1, rb, cols), lambda k, r, w: (w[1 + k], r, 0))],
            out_specs=pl.BlockSpec((1, rb, cols), lambda k, r, w: (k, r, 0))),
        compiler_params=_cparams(2),
    )(where, gp4, got)


def _chip_exchange_comm(pb):
    _, rows, cols = pb.shape

    def copies(ins, outs, sems):
        send_sems, recv_sems = sems
        x, y, c = _place()
        chips = [(1 - x, y), (x, 1 - y), (1 - x, 1 - y)]
        return [pltpu.make_async_remote_copy(
            src_ref=ins[0].at[1 + k], dst_ref=outs[0].at[k], send_sem=send_sems.at[k],
            recv_sem=recv_sems.at[k], device_id=(px, py, c), device_id_type=MESH)
            for k, (px, py) in enumerate(chips)]

    def start(ins, outs, sems):
        for cp in copies(ins, outs, sems):
            cp.start()

    def wait(ins, outs, sems):
        for cp in copies(ins, outs, sems):
            cp.wait()

    return _Comm([pb], [jax.ShapeDtypeStruct((3, rows, cols), pb.dtype)],
                 [pltpu.SemaphoreType.DMA((3,)), pltpu.SemaphoreType.DMA((3,))], start, wait)


def _chip_exchange_start(pb, tag):
    _, rows, cols = pb.shape

    def body(pb_ref, land_ref, *rest):
        sems, token = rest[:6], rest[8]
        x, y, c = _place()
        chips = [(1 - x, y), (x, 1 - y), (1 - x, 1 - y)]
        for k, (px, py) in enumerate(chips):
            pltpu.make_async_remote_copy(
                src_ref=pb_ref.at[1 + k], dst_ref=land_ref.at[k], send_sem=sems[k], recv_sem=sems[3 + k],
                device_id=(px, py, c), device_id_type=MESH).start()
        token[...] = jnp.zeros_like(token)

    hbm = pl.BlockSpec(memory_space=pltpu.HBM)
    sem = pl.BlockSpec(memory_space=pltpu.SEMAPHORE)
    land = lax.empty((3, rows, cols), pb.dtype)
    res = pl.pallas_call(
        body, name="rs_chip_exchange_start_" + tag,
        out_shape=(*[pltpu.SemaphoreType.DMA(())] * 6, pltpu.HBM(pb.shape, pb.dtype), pltpu.HBM(land.shape, land.dtype),
                   jax.ShapeDtypeStruct((8, 128), F32)),
        in_specs=(hbm, hbm), out_specs=(*[sem] * 6, hbm, hbm, pl.BlockSpec(memory_space=pltpu.VMEM)),
        input_output_aliases={0: 6, 1: 7},
        compiler_params=pltpu.CompilerParams(has_side_effects=pltpu.SideEffectType.DATAFLOW_SIDE_EFFECTING),
    )(pltpu.with_memory_space_constraint(pb, pltpu.HBM), pltpu.with_memory_space_constraint(land, pltpu.HBM))
    return list(res[:6]), res[6], res[7], res[8]


def _chip_exchange_wait(sems, pb_thru, land_thru, after, tag):
    def body(pb_ref, land_ref, *rest):
        sems_in = rest[:6]
        x, y, c = _place()
        chips = [(1 - x, y), (x, 1 - y), (1 - x, 1 - y)]
        for k, (px, py) in enumerate(chips):
            cp = pltpu.make_async_remote_copy(
                src_ref=pb_ref.at[1 + k], dst_ref=land_ref.at[k], send_sem=sems_in[k], recv_sem=sems_in[3 + k],
                device_id=(px, py, c), device_id_type=MESH)
            cp.wait_send()
            cp.wait_recv()

    hbm = pl.BlockSpec(memory_space=pltpu.HBM)
    sem = pl.BlockSpec(memory_space=pltpu.SEMAPHORE)
    res = pl.pallas_call(
        body, name="rs_chip_exchange_wait_" + tag,
        out_shape=(pltpu.HBM(pb_thru.shape, pb_thru.dtype), pltpu.HBM(land_thru.shape, land_thru.dtype)),
        in_specs=(hbm, hbm, *[sem] * 6, pl.BlockSpec(memory_space=pl.ANY)), out_specs=(hbm, hbm),
        input_output_aliases={0: 0, 1: 1},
        compiler_params=pltpu.CompilerParams(has_side_effects=pltpu.SideEffectType.DATAFLOW_SIDE_EFFECTING),
    )(pb_thru, land_thru, *sems, after)
    return res[0], res[1]


def _rs_final_sums(pbs, gots, name, after=None):
    n = len(pbs)

    def body(*refs):
        outs = refs[len(refs) - n:]
        for a in range(n):
            m_ref, g_ref, o_ref = refs[a], refs[n + a], outs[a]
            o_ref[...] = ((m_ref[0].astype(F32) + g_ref[0].astype(F32)) + g_ref[1].astype(F32)) + g_ref[2].astype(F32)

    half = [pb.shape[1] // 2 for pb in pbs]
    in_specs = ([pl.BlockSpec((1, h, D), lambda i: (0, i, 0)) for h in half]
                + [pl.BlockSpec((3, h, D), lambda i: (0, i, 0)) for h in half])
    args = (*pbs, *gots)
    if after is not None:
        in_specs, args = in_specs + [pl.BlockSpec(memory_space=pl.ANY)], args + (after,)
    res, _ = _call(body, name, (2,), in_specs, [pl.BlockSpec((h, D), lambda i: (i, 0)) for h in half],
                   [jax.ShapeDtypeStruct(pb.shape[1:], F32) for pb in pbs], args)
    return res


def _rope_tables(pos_col, inv_freq, comm=None):
    t = pos_col.shape[0]

    def body(p_ref, f_ref, c_ref, s_ref):
        ang = p_ref[...].astype(F32) * f_ref[...]
        lane = lax.broadcasted_iota(jnp.int32, ang.shape, 1)
        c_ref[...] = jnp.where(lane < QK_ROPE, jnp.cos(ang), 0.0)
        s = jnp.sin(ang)
        s_ref[...] = jnp.where(lane < 32, -s, jnp.where(lane < QK_ROPE, s, 0.0))

    spec = pl.BlockSpec((TB, 128), lambda i: (i, 0))
    return _call(
        body, "rope_tables", (t // TB,), [pl.BlockSpec((TB, 1), lambda i: (i, 0)), _const_spec((1, 128))],
        [spec] * 2, [jax.ShapeDtypeStruct((t, 128), F32)] * 2, (pos_col, inv_freq), (), comm)


def _sgu_conv_fwd(proj, tail, lng_ref, ws_ref, bst_ref, cw_ref):
    gu = _gelu(proj[:, 0:SG_W])
    gv = _gelu(proj[:, SG_W:2 * SG_W])
    bg = proj[:, 1024:1536]
    z = proj[:, 1536:2048] * proj[:, 2048:2560]
    heads = []
    for h in range(SG_HEADS):
        sl = slice(h * SG_HD, (h + 1) * SG_HD)
        vn, _, _ = _ln_head(gv[:, sl], lng_ref[:, sl])
        vnb = vn.astype(BF16)
        wm = _tril_bf16(ws_ref[h])
        bcol = bst_ref[:, h:h + 1]
        mixed = jnp.concatenate(
            [_dot(wm, vnb[k * SG_CHUNK:(k + 1) * SG_CHUNK], 1, 0) + bcol for k in range(TB // SG_CHUNK)], axis=0)
        heads.append(gu[:, sl] * mixed)
    a_out = jnp.concatenate(heads, axis=1)
    y, _, _ = _conv_fwd(z, tail, cw_ref)
    return a_out, bg * y, z


def _even_fwd(x, wg, gamma, lng, ws, bst, cw, seq, comm=None):
    t = x.shape[0]
    nbs = seq // TB

    def body(x_ref, gam_ref, win_ref, wout_ref, lng_ref, ws_ref, bst_ref, cw_ref, x1_ref, proj_ref, tail_ref):
        i = pl.program_id(0)
        xv = x_ref[...]
        h, _ = _rms(xv, gam_ref[...])
        proj = _dot(h.astype(BF16), win_ref[...].reshape(EVEN_IN, D), 1, 1)
        proj_ref[...] = proj.astype(BF16)
        tail = jnp.where(i % nbs == 0, 0.0, tail_ref[...])
        a_out, b_out, z = _sgu_conv_fwd(proj, tail, lng_ref, ws_ref, bst_ref, cw_ref)
        tail_ref[...] = z[TB - HALO:, :]
        x1_ref[...] = (xv + _dot(a_out.astype(BF16), wout_ref[0:4].reshape(512, D), 1, 0)
                       + _dot(b_out.astype(BF16), wout_ref[4:8].reshape(512, D), 1, 0))

    row = pl.BlockSpec((TB, D), lambda i: (i, 0))
    return _call(
        body, "even_fwd", (t // TB,),
        [row, _const_spec((1, D)), _wspec(N_EIN, OFF_EIN), _wspec(N_SQ, OFF_EOUT), _const_spec((1, SG_W)),
         _const_spec((SG_HEADS, 128, 128)), _const_spec((128, 128)), _const_spec((8, SC_W))],
        [row, pl.BlockSpec((TB, EVEN_IN), lambda i: (i, 0))],
        [jax.ShapeDtypeStruct((t, D), F32), jax.ShapeDtypeStruct((t, EVEN_IN), BF16)],
        (x, gamma, wg, wg, lng, ws, bst, cw), [pltpu.VMEM((HALO, SC_W), F32)], comm)


def _even_bwd(x, proj, dx1, wg, gamma, lng, ws, bst, cw, seq, comm=None):
    t = x.shape[0]
    nb, nbs = t // TB, seq // TB

    def body(x_ref, proj_ref, ptail_ref, dx1_ref, gam_ref, win_ref, wout_ref, lng_ref, ws_ref, bst_ref, cw_ref,
             dx0_ref, dproj_ref, mix_ref, h_ref, dgam_ref, dws_ref, dbc_ref, dlng_ref, dcw_ref, head_ref):
        i = pl.program_id(0)
        blk = nb - 1 - i

        @pl.when(i == 0)
        def _():
            dgam_ref[...] = jnp.zeros_like(dgam_ref)
            dws_ref[...] = jnp.zeros_like(dws_ref)
            dbc_ref[...] = jnp.zeros_like(dbc_ref)
            dlng_ref[...] = jnp.zeros_like(dlng_ref)
            dcw_ref[...] = jnp.zeros_like(dcw_ref)

        xv = x_ref[...]
        gam = gam_ref[...]
        h, r = _rms(xv, gam)
        h_ref[...] = h.astype(BF16)
        dx1 = dx1_ref[...]
        dmix = _dot(dx1.astype(BF16), wout_ref[...].reshape(D, D), 1, 1)
        da, db = dmix[:, :SG_W], dmix[:, SG_W:]
        proj = proj_ref[...].astype(F32)
        u, v = proj[:, 0:SG_W], proj[:, SG_W:2 * SG_W]
        bg, cg, hv = proj[:, 1024:1536], proj[:, 1536:2048], proj[:, 2048:2560]
        gu, gv = _gelu(u), _gelu(v)

        a_heads, dgv_heads = [], []
        for hd in range(SG_HEADS):
            sl = slice(hd * SG_HD, (hd + 1) * SG_HD)
            g_h = lng_ref[:, sl]
            vn, xh, rr = _ln_head(gv[:, sl], g_h)
            vnb = vn.astype(BF16)
            wm = _tril_bf16(ws_ref[hd])
            bcol = bst_ref[:, hd:hd + 1]
            mixed_c, dvn_c = [], []
            dw_acc = jnp.zeros((128, 128), F32)
            db_acc = jnp.zeros((128, 1), F32)
            for k in range(TB // SG_CHUNK):
                rs = slice(k * SG_CHUNK, (k + 1) * SG_CHUNK)
                mixed = _dot(wm, vnb[rs], 1, 0) + bcol
                dmixed = da[rs, sl] * gu[rs, sl]
                dmb = dmixed.astype(BF16)
                dvn_c.append(_dot(wm, dmb, 0, 0))
                dw_acc = dw_acc + _dot(dmb, vnb[rs], 1, 1)
                db_acc = db_acc + jnp.sum(dmixed, axis=1, keepdims=True)
                mixed_c.append(mixed)
            mixed_h = jnp.concatenate(mixed_c, axis=0)
            dvn = jnp.concatenate(dvn_c, axis=0)
            r_i = lax.broadcasted_iota(jnp.int32, (128, 128), 0)
            c_i = lax.broadcasted_iota(jnp.int32, (128, 128), 1)
            dws_ref[hd] += jnp.where(r_i >= c_i, dw_acc, 0.0)
            dbc_ref[:, hd:hd + 1] += db_acc
            dlng_ref[:, sl] += jnp.sum(dvn * xh, axis=0, keepdims=True)
            dxh = dvn * g_h
            dgv = rr * (dxh - jnp.mean(dxh, axis=-1, keepdims=True)
                        - xh * jnp.mean(dxh * xh, axis=-1, keepdims=True))
            a_heads.append(gu[:, sl] * mixed_h)
            dproj_ref[:, sl] = (da[:, sl] * mixed_h * _gelu_grad(u[:, sl])).astype(BF16)
            dgv_heads.append(dgv * _gelu_grad(v[:, sl]))
        dproj_ref[:, SG_W:2 * SG_W] = jnp.concatenate(dgv_heads, axis=1).astype(BF16)
        mix_ref[:, :SG_W] = jnp.concatenate(a_heads, axis=1).astype(BF16)

        z = cg * hv
        pt = ptail_ref[...].astype(F32)
        tail = jnp.where(blk % nbs == 0, 0.0, pt[:, 1536:2048] * pt[:, 2048:2560])
        y, zs1, zs2 = _conv_fwd(z, tail, cw_ref)
        mix_ref[:, SG_W:] = (bg * y).astype(BF16)
        dy = db * bg
        head = jnp.where(blk % nbs == nbs - 1, 0.0, head_ref[...])
        ext = jnp.concatenate([dy, head], axis=0)
        dz = (cw_ref[2:3, :] * dy + cw_ref[1:2, :] * _shift_up(ext, 1)[:TB]
              + cw_ref[0:1, :] * _shift_up(ext, 2)[:TB])
        head_ref[...] = dy[:HALO, :]
        dcw_ref[2:3, :] += jnp.sum(dy * z, axis=0, keepdims=True)
        dcw_ref[1:2, :] += jnp.sum(dy * zs1, axis=0, keepdims=True)
        dcw_ref[0:1, :] += jnp.sum(dy * zs2, axis=0, keepdims=True)
        dproj_ref[:, 1024:1536] = (db * y).astype(BF16)
        dproj_ref[:, 1536:2048] = (dz * hv).astype(BF16)
        dproj_ref[:, 2048:2560] = (dz * cg).astype(BF16)

        dh = _dot(dproj_ref[...], win_ref[...].reshape(EVEN_IN, D), 1, 0)
        dxn, dgam = _rms_bwd(xv, r, gam, dh)
        dgam_ref[...] += dgam
        dx0_ref[...] = dx1 + dxn

    def rev(w):
        return pl.BlockSpec((TB, w), lambda i: (nb - 1 - i, 0))

    ptail = pl.BlockSpec((HALO, EVEN_IN), lambda i: (jnp.maximum((nb - 1 - i) * (TB // HALO) - 1, 0), 0))
    return _call(
        body, "even_bwd", (nb,),
        [rev(D), rev(EVEN_IN), ptail, rev(D), _const_spec((1, D)), _wspec(N_EIN, OFF_EIN),
         _wspec(N_SQ, OFF_EOUT), _const_spec((1, SG_W)), _const_spec((SG_HEADS, 128, 128)),
         _const_spec((128, 128)), _const_spec((8, SC_W))],
        [rev(D), rev(EVEN_IN), rev(D), rev(D), _const_spec((1, D)), _const_spec((SG_HEADS, 128, 128)),
         _const_spec((128, 128)), _const_spec((1, SG_W)), _const_spec((8, SC_W))],
        [jax.ShapeDtypeStruct((t, D), F32), jax.ShapeDtypeStruct((t, EVEN_IN), BF16),
         jax.ShapeDtypeStruct((t, D), BF16), jax.ShapeDtypeStruct((t, D), BF16),
         jax.ShapeDtypeStruct((1, D), F32), jax.ShapeDtypeStruct((SG_HEADS, 128, 128), F32),
         jax.ShapeDtypeStruct((128, 128), F32), jax.ShapeDtypeStruct((1, SG_W), F32),
         jax.ShapeDtypeStruct((8, SC_W), F32)],
        (x, proj, proj, dx1, gamma, wg, wg, lng, ws, bst, cw), [pltpu.VMEM((HALO, SC_W), F32)], comm)


def _last_block_fwd(x, c_out, d_out, w_mix1, w_gu, w_d, gamma, target):
    t = x.shape[0]

    def body(x_ref, c_ref, d_ref, wo_ref, gam_ref, wg_ref, wu_ref, wd_ref, t_ref,
             x3_ref, dy_ref, g_ref, u_ref, loss_ref):
        @pl.when(pl.program_id(0) == 0)
        def _():
            loss_ref[...] = jnp.zeros_like(loss_ref)

        xv = (x_ref[...] + _dot(c_ref[...], wo_ref[0:2].reshape(POOL_W, D), 1, 0)
              + _dot(d_ref[...], wo_ref[2:8].reshape(HEADS * V_DIM, D), 1, 0))
        x3_ref[...] = xv
        h, _ = _rms(xv, gam_ref[...])
        hb = h.astype(BF16)
        g = _dot(hb, wg_ref[...].reshape(D_FF, D), 1, 1)
        u = _dot(hb, wu_ref[...].reshape(D_FF, D), 1, 1)
        g_ref[...] = g.astype(BF16)
        u_ref[...] = u.astype(BF16)
        act = g * jax.nn.sigmoid(g) * u
        err = xv + _dot(act.astype(BF16), wd_ref[...].reshape(D_FF, D), 1, 0) - t_ref[...]
        dy_ref[...] = err * (1.0 / D)
        sq = jnp.sum(jnp.sum(err * err, axis=-1, keepdims=True), axis=0, keepdims=True)
        loss_ref[...] += (0.5 / D) * sq

    def row(w):
        return pl.BlockSpec((TB, w), lambda i: (i, 0))

    res, _ = _call(
        body, "last_block_fwd", (t // TB,),
        [row(D), row(POOL_W), row(HEADS * V_DIM), _wspec(N_SQ, OFF_OOUT), _const_spec((1, D)),
         _wspec(N_FF, OFF_GATE), _wspec(N_FF, OFF_UP), _wspec(N_FF, 0), row(D)],
        [row(D), row(D), row(D_FF), row(D_FF), _const_spec((8, 128))],
        [jax.ShapeDtypeStruct((t, D), F32), jax.ShapeDtypeStruct((t, D), F32), jax.ShapeDtypeStruct((t, D_FF), BF16),
         jax.ShapeDtypeStruct((t, D_FF), BF16), jax.ShapeDtypeStruct((8, 128), F32)],
        (x, c_out, d_out, w_mix1, gamma, w_gu, w_gu, w_d, target))
    return res


def _ffn_up(x, w_gu, gamma, name, comm=None):
    t = x.shape[0]

    def body(x_ref, gam_ref, wg_ref, wu_ref, g_ref, u_ref, act_ref):
        h, _ = _rms(x_ref[...], gam_ref[...])
        hb = h.astype(BF16)
        g = _dot(hb, wg_ref[...].reshape(D_FF, D), 1, 1)
        u = _dot(hb, wu_ref[...].reshape(D_FF, D), 1, 1)
        g_ref[...] = g.astype(BF16)
        u_ref[...] = u.astype(BF16)
        act_ref[...] = (g * jax.nn.sigmoid(g) * u).astype(BF16)

    row = pl.BlockSpec((TB, D), lambda i: (i, 0))
    wide = pl.BlockSpec((TB, D_FF), lambda i: (i, 0))
    return _call(body, name, (t // TB,), [row, _const_spec((1, D)), _wspec(N_FF, OFF_GATE), _wspec(N_FF, OFF_UP)],
                 [wide, wide, wide], [jax.ShapeDtypeStruct((t, D_FF), BF16)] * 3, (x, gamma, w_gu, w_gu), (), comm)


def _ffn_down(x, act, w_d, name, comm=None):
    t = x.shape[0]

    def body(x_ref, a_ref, wd_ref, y_ref):
        y_ref[...] = x_ref[...] + _dot(a_ref[...], wd_ref[...].reshape(D_FF, D), 1, 0)

    row = pl.BlockSpec((TB, D), lambda i: (i, 0))
    wide = pl.BlockSpec((TB, D_FF), lambda i: (i, 0))
    return _call(body, name, (t // TB,), [row, wide, _wspec(N_FF, 0)], [row], [jax.ShapeDtypeStruct((t, D), F32)],
                 (x, act, w_d), (), comm)


def _ffn_bwd(x, g, u, dy, w_gu, w_d, gamma, name, comm=None, w_mix1=None):
    t = x.shape[0]
    with_dmix = w_mix1 is not None

    def body(*refs):
        x_ref, g_ref, u_ref, dy_ref, gam_ref, wg_ref, wu_ref, wd_ref = refs[:8]
        dx_ref, act_ref, dg_ref, du_ref, h_ref, dgam_ref = refs[8 + with_dmix:14 + with_dmix]

        @pl.when(pl.program_id(0) == 0)
        def _():
            dgam_ref[...] = jnp.zeros_like(dgam_ref)

        xv = x_ref[...]
        gam = gam_ref[...]
        h, r = _rms(xv, gam)
        h_ref[...] = h.astype(BF16)
        dyv = dy_ref[...]
        dact = _dot(dyv.astype(BF16), wd_ref[...].reshape(D_FF, D), 1, 1)
        gv = g_ref[...].astype(F32)
        uv = u_ref[...].astype(F32)
        sg = jax.nn.sigmoid(gv)
        silu = gv * sg
        act_ref[...] = (silu * uv).astype(BF16)
        dgb = (dact * uv * (sg * (1.0 + gv * (1.0 - sg)))).astype(BF16)
        dub = (dact * silu).astype(BF16)
        dg_ref[...] = dgb
        du_ref[...] = dub
        dh = _dot(dgb, wg_ref[...].reshape(D_FF, D), 1, 0) + _dot(dub, wu_ref[...].reshape(D_FF, D), 1, 0)
        dxn, dgam = _rms_bwd(xv, r, gam, dh)
        dgam_ref[...] += dgam
        dx = dyv + dxn
        dx_ref[...] = dx
        if with_dmix:
            refs[15][...] = _dot(dx.astype(BF16), refs[8][...].reshape(D, D), 1, 1).astype(BF16)

    row = pl.BlockSpec((TB_FFN_BWD, D), lambda i: (i, 0))
    wide = pl.BlockSpec((TB_FFN_BWD, D_FF), lambda i: (i, 0))
    in_specs = [row, wide, wide, row, _const_spec((1, D)), _wspec(N_FF, OFF_GATE), _wspec(N_FF, OFF_UP),
                _wspec(N_FF, 0)]
    out_specs = [row, wide, wide, wide, row, _const_spec((1, D))]
    out_shape = [jax.ShapeDtypeStruct((t, D), F32), jax.ShapeDtypeStruct((t, D_FF), BF16),
                 jax.ShapeDtypeStruct((t, D_FF), BF16), jax.ShapeDtypeStruct((t, D_FF), BF16),
                 jax.ShapeDtypeStruct((t, D), BF16), jax.ShapeDtypeStruct((1, D), F32)]
    args = (x, g, u, dy, gamma, w_gu, w_gu, w_d)
    if with_dmix:
        in_specs, args = in_specs + [_wspec(N_SQ, OFF_OOUT)], args + (w_mix1,)
        out_specs, out_shape = out_specs + [row], out_shape + [jax.ShapeDtypeStruct((t, D), BF16)]
    return _call(body, name, (t // TB_FFN_BWD,), in_specs, out_specs, out_shape, args, (), comm)


def _odd_pre_fwd(x, wg, gamma, qbt, kvbt, qa_g, kva_g, pw_bd, pscale, seq, comm=None):
    t = x.shape[0]
    nbs = seq // TB

    def body(x_ref, gam_ref, win_ref, qb_ref, kvb_ref, qa_ref, kva_ref, pw_ref, ps_ref,
             proj_ref, q_ref, kv_ref, kr_ref, c_ref, tail_ref):
        i = pl.program_id(0)
        h, _ = _rms(x_ref[...], gam_ref[...])
        proj = _dot(h.astype(BF16), win_ref[...].reshape(D, D), 1, 0)
        proj_ref[...] = proj.astype(BF16)
        zp, ql, kvl = proj[:, :POOL_W], proj[:, 256:640], proj[:, 640:896]
        kr_ref[...] = proj[:, 896:1024]
        qn, _ = _rms(ql, qa_ref[...])
        q_ref[...] = _dot(qn.astype(BF16), qb_ref[...], 1, 1).astype(BF16)
        kvn, _ = _rms(kvl, kva_ref[...])
        kv_ref[...] = _dot(kvn.astype(BF16), kvb_ref[...], 1, 1).astype(BF16)
        tail = jnp.where(i % nbs == 0, 0.0, tail_ref[...])
        pooled, _, _ = _pool_fwd(zp, tail, i % nbs)
        tail_ref[...] = zp[TB - HALO:, :]
        c_ref[...] = (_dot(pooled.astype(BF16), pw_ref[...], 1, 0) * ps_ref[...]).astype(BF16)

    def row(w):
        return pl.BlockSpec((TB, w), lambda i: (i, 0))

    return _call(
        body, "odd_pre_fwd", (t // TB,),
        [row(D), _const_spec((1, D)), _wspec(N_SQ, OFF_OIN), _const_spec((HEADS * HP, Q_LORA)),
         _const_spec((HEADS * HP, KV_LORA)), _const_spec((1, Q_LORA)), _const_spec((1, KV_LORA)),
         _const_spec((POOL_W, POOL_W)), _const_spec((1, POOL_W))],
        [row(D), row(HEADS * HP), row(HEADS * HP), row(128), row(POOL_W)],
        [jax.ShapeDtypeStruct((t, D), BF16), jax.ShapeDtypeStruct((t, HEADS * HP), BF16),
         jax.ShapeDtypeStruct((t, HEADS * HP), BF16), jax.ShapeDtypeStruct((t, 128), F32),
         jax.ShapeDtypeStruct((t, POOL_W), BF16)],
        (x, gamma, wg, qbt, kvbt, qa_g, kva_g, pw_bd, pscale), [pltpu.VMEM((HALO, POOL_W), F32)], comm)


def _odd_pre_bwd(x, proj, dx3, dmix, dq, dkv, dkr, wg, gamma, qbt, kvbt, qa_g, kva_g, pw_bd, pscale, seq):
    t = x.shape[0]
    nb, nbs = t // TB, seq // TB

    def body(x_ref, proj_ref, ptail_ref, dx3_ref, dco_ref, dq_ref, dkv_ref, dkr_ref, gam_ref, win_ref, qb_ref,
             kvb_ref, qa_ref, kva_ref, pw_ref, ps_ref,
             dx2_ref, dproj_ref, h_ref, qn_ref, kvn_ref, dgam_ref, dqa_ref, dkva_ref, dpw_ref, dps_ref, head_ref):
        i = pl.program_id(0)
        blk = nb - 1 - i

        @pl.when(i == 0)
        def _():
            dgam_ref[...] = jnp.zeros_like(dgam_ref)
            dqa_ref[...] = jnp.zeros_like(dqa_ref)
            dkva_ref[...] = jnp.zeros_like(dkva_ref)
            dpw_ref[...] = jnp.zeros_like(dpw_ref)
            dps_ref[...] = jnp.zeros_like(dps_ref)

        xv = x_ref[...]
        gam = gam_ref[...]
        h, r = _rms(xv, gam)
        h_ref[...] = h.astype(BF16)
        proj = proj_ref[...].astype(F32)
        zp, ql, kvl = proj[:, :POOL_W], proj[:, 256:640], proj[:, 640:896]

        qa = qa_ref[...]
        qn, rq = _rms(ql, qa)
        qn_ref[...] = qn.astype(BF16)
        dql, dqa = _rms_bwd(ql, rq, qa, _dot(dq_ref[...], qb_ref[...], 1, 0))
        dqa_ref[...] += dqa
        kva = kva_ref[...]
        kvn, rkv = _rms(kvl, kva)
        kvn_ref[...] = kvn.astype(BF16)
        dkvl, dkva = _rms_bwd(kvl, rkv, kva, _dot(dkv_ref[...], kvb_ref[...], 1, 0))
        dkva_ref[...] += dkva

        pt = ptail_ref[...].astype(F32)
        tail = jnp.where(blk % nbs == 0, 0.0, pt[:, :POOL_W])
        pooled, cnt, grp = _pool_fwd(zp, tail, blk % nbs)
        pb = pooled.astype(BF16)
        pw = pw_ref[...]
        dco = dco_ref[...].astype(F32)
        dps_ref[...] += jnp.sum(dco * _dot(pb, pw, 1, 0), axis=0, keepdims=True)
        dpo = (dco * ps_ref[...]).astype(BF16)
        dpw_ref[...] += _dot(pb, dpo, 0, 0)
        dpooled = _dot(dpo, pw, 1, 1)
        dpm = dpooled / cnt
        head = jnp.where(blk % nbs == nbs - 1, 0.0, head_ref[...])
        dz = _pool_bwd(dpooled, dpm, head, grp)
        head_ref[...] = dpm[:HALO, :]

        dproj_ref[:, :POOL_W] = dz.astype(BF16)
        dproj_ref[:, 256:640] = dql.astype(BF16)
        dproj_ref[:, 640:896] = dkvl.astype(BF16)
        dproj_ref[:, 896:1024] = dkr_ref[...].astype(BF16)
        dh = _dot(dproj_ref[...], win_ref[...].reshape(D, D), 1, 1)
        dxn, dgam = _rms_bwd(xv, r, gam, dh)
        dgam_ref[...] += dgam
        dx2_ref[...] = dx3_ref[...] + dxn

    def rev(w):
        return pl.BlockSpec((TB, w), lambda i: (nb - 1 - i, 0))

    ptail = pl.BlockSpec((HALO, D), lambda i: (jnp.maximum((nb - 1 - i) * (TB // HALO) - 1, 0), 0))
    return pl.pallas_call(
        body, name="odd_pre_bwd",
        out_shape=[jax.ShapeDtypeStruct((t, D), F32), jax.ShapeDtypeStruct((t, D), BF16),
                   jax.ShapeDtypeStruct((t, D), BF16), jax.ShapeDtypeStruct((t, Q_LORA), BF16),
                   jax.ShapeDtypeStruct((t, KV_LORA), BF16), jax.ShapeDtypeStruct((1, D), F32),
                   jax.ShapeDtypeStruct((1, Q_LORA), F32), jax.ShapeDtypeStruct((1, KV_LORA), F32),
                   jax.ShapeDtypeStruct((POOL_W, POOL_W), F32), jax.ShapeDtypeStruct((1, POOL_W), F32)],
        grid=(nb,),
        in_specs=[rev(D), rev(D), ptail, rev(D), rev(POOL_W), rev(HEADS * HP), rev(HEADS * HP), rev(128),
                  _const_spec((1, D)), _wspec(N_SQ, OFF_OIN), _const_spec((HEADS * HP, Q_LORA)),
                  _const_spec((HEADS * HP, KV_LORA)), _const_spec((1, Q_LORA)), _const_spec((1, KV_LORA)),
                  _const_spec((POOL_W, POOL_W)), _const_spec((1, POOL_W))],
        out_specs=[rev(D), rev(D), rev(D), rev(Q_LORA), rev(KV_LORA), _const_spec((1, D)), _const_spec((1, Q_LORA)),
                   _const_spec((1, KV_LORA)), _const_spec((POOL_W, POOL_W)), _const_spec((1, POOL_W))],
        scratch_shapes=[pltpu.VMEM((HALO, POOL_W), F32)],
        compiler_params=_cparams(1),
    )(x, proj, proj, dx3, dmix, dq, dkv, dkr, gamma, wg, qbt, kvbt, qa_g, kva_g, pw_bd, pscale)


def _attn_specs(seq):
    head = pl.BlockSpec((seq, HP), lambda b, h: (b, h))
    shared = pl.BlockSpec((seq, 128), lambda b, h: (b, 0))
    gain = pl.BlockSpec((1, HP), lambda b, h: (0, 0))
    return head, shared, gain


def _causal_bias(n):
    rows = lax.broadcasted_iota(jnp.int32, (n, n), 0)
    cols = lax.broadcasted_iota(jnp.int32, (n, n), 1)
    return jnp.where(cols <= rows, 0.0, NEG_INF)


def _attn_fwd(q, kv, kr, cos, sin, gq, gk, seq, comm=None):
    t = q.shape[0]
    qb = min(512, seq)

    def body(q_ref, kv_ref, kr_ref, c_ref, s_ref, gq_ref, gk_ref, o_ref, lse_ref):
        c, s = c_ref[...], s_ref[...]
        qf, _ = _qk_prep(q_ref[...].astype(F32), gq_ref[...], c, s)
        kin = jnp.concatenate([kv_ref[:, :128].astype(F32), kr_ref[...]], axis=1)
        kf, _ = _qk_prep(kin, gk_ref[...], c, s)
        qf, kf = qf.astype(BF16), kf.astype(BF16)
        v1 = jnp.concatenate([kv_ref[:, 128:], jnp.ones((seq, V_DIM), BF16)], axis=1)
        bias = _causal_bias(qb)
        for q0 in range(0, seq, qb):
            q1 = q0 + qb
            qblk = qf[q0:q1]
            s_dg = _dot(qblk, kf[q0:q1], 1, 1) + bias
            m = jnp.max(s_dg, axis=-1, keepdims=True)
            if q0:
                s_off = _dot(qblk, kf[:q0], 1, 1)
                m = jnp.maximum(m, jnp.max(s_off, axis=-1, keepdims=True))
            acc = _dot(jnp.exp(s_dg - m).astype(BF16), v1[q0:q1], 1, 0)
            if q0:
                acc = acc + _dot(jnp.exp(s_off - m).astype(BF16), v1[:q0], 1, 0)
            l = acc[:, V_DIM:]
            o_ref[q0:q1, :] = (acc[:, :V_DIM] / l).astype(BF16)
            lse_ref[q0:q1, :] = m + jnp.log(l)

    head, shared, gain = _attn_specs(seq)
    per_head = pl.BlockSpec((seq, V_DIM), lambda b, h: (b, h))
    return _call(
        body, "attn_fwd", (t // seq, HEADS),
        [head, head, shared, shared, shared, gain, gain], [per_head, per_head],
        [jax.ShapeDtypeStruct((t, HEADS * V_DIM), BF16), jax.ShapeDtypeStruct((t, HEADS * V_DIM), F32)],
        (q, kv, kr, cos, sin, gq, gk), (), comm)


def _attn_bwd(q, kv, kr, cos, sin, gq, gk, dmix, d_out, lse, seq, comm=None):
    t = q.shape[0]
    qb = min(512, seq)

    def body(q_ref, kv_ref, kr_ref, c_ref, s_ref, gq_ref, gk_ref, do_ref, o_ref, lse_ref,
             dq_ref, dkv_ref, dkr_ref, dgq_ref, dgk_ref, dqf_ref, dkf_ref, dv_ref):
        b, hd = pl.program_id(0), pl.program_id(1)

        @pl.when((b == 0) & (hd == 0))
        def _():
            dgq_ref[...] = jnp.zeros_like(dgq_ref)
            dgk_ref[...] = jnp.zeros_like(dgk_ref)

        c, sn = c_ref[...], s_ref[...]
        gq_v, gk_v = gq_ref[...], gk_ref[...]
        qin = q_ref[...].astype(F32)
        kin = jnp.concatenate([kv_ref[:, :128].astype(F32), kr_ref[...]], axis=1)
        qf32, rq = _qk_prep(qin, gq_v, c, sn)
        kf32, rk = _qk_prep(kin, gk_v, c, sn)
        qf, kf = qf32.astype(BF16), kf32.astype(BF16)
        vb = kv_ref[:, 128:]
        dkf_ref[...] = jnp.zeros_like(dkf_ref)
        dv_ref[...] = jnp.zeros_like(dv_ref)
        bias = _causal_bias(qb)
        for q0 in range(0, seq, qb):
            q1 = q0 + qb
            qblk = qf[q0:q1]
            do = do_ref[q0:q1, :]
            lse_col = lse_ref[q0:q1, 0:1]
            d_col = jnp.sum(do.astype(F32) * o_ref[q0:q1, :].astype(F32), axis=-1, keepdims=True)
            dq_acc = None
            for k0, k1, diag in ((q0, q1, True), (0, q0, False)):
                if k1 == k0:
                    continue
                s = _dot(qblk, kf[k0:k1], 1, 1)
                p = jnp.exp((s + bias if diag else s) - lse_col)
                dv_ref[k0:k1, :] += _dot(p.astype(BF16), do, 0, 0)
                ds = (p * (_dot(do, vb[k0:k1], 1, 1) - d_col)).astype(BF16)
                part = _dot(ds, kf[k0:k1], 1, 0)
                dq_acc = part if dq_acc is None else dq_acc + part
                dkf_ref[k0:k1, :] += _dot(ds, qblk, 0, 0)
            dqf_ref[q0:q1, :] = dq_acc
        dqin, dgq = _qk_prep_bwd(dqf_ref[...], qin, rq, gq_v, c, sn)
        dkin, dgk = _qk_prep_bwd(dkf_ref[...], kin, rk, gk_v, c, sn)
        dgq_ref[...] += dgq
        dgk_ref[...] += dgk
        dq_ref[...] = dqin.astype(BF16)
        dkv_ref[:, :128] = dkin[:, :128].astype(BF16)
        dkv_ref[:, 128:] = dv_ref[...].astype(BF16)

        @pl.when(hd == 0)
        def _():
            dkr_ref[...] = dkin[:, 128:]

        @pl.when(hd != 0)
        def _():
            dkr_ref[...] += dkin[:, 128:]

    head, shared, gain = _attn_specs(seq)
    per_head = pl.BlockSpec((seq, V_DIM), lambda b, h: (b, h))
    return _call(
        body, "attn_bwd", (t // seq, HEADS),
        [head, head, shared, shared, shared, gain, gain,
         pl.BlockSpec((seq, V_DIM), lambda b, h: (b, 2 + h)), per_head, per_head],
        [head, head, shared, gain, gain],
        [jax.ShapeDtypeStruct((t, HEADS * HP), BF16), jax.ShapeDtypeStruct((t, HEADS * HP), BF16),
         jax.ShapeDtypeStruct((t, 128), F32), jax.ShapeDtypeStruct((1, HP), F32),
         jax.ShapeDtypeStruct((1, HP), F32)],
        (q, kv, kr, cos, sin, gq, gk, dmix, d_out, lse),
        [pltpu.VMEM((seq, HP), F32), pltpu.VMEM((seq, HP), F32), pltpu.VMEM((seq, V_DIM), F32)], comm)


def _tn(a_list, b, tm, name, into=None, comm=None, after=None):
    t, n_out = b.shape
    widths = [a.shape[1] for a in a_list]
    tk = min(TK_DW, t)
    m, na, nk = sum(widths), len(a_list), t // tk
    assert na == 1 or tm == m

    def body(*refs):
        a_refs, b_ref, o_ref, acc_ref = refs[:na], refs[na], refs[-2], refs[-1]
        k = pl.program_id(1)

        @pl.when(k == 0)
        def _():
            acc_ref[...] = jnp.zeros_like(acc_ref)

        bb = b_ref[...].astype(BF16)
        m0 = 0
        for a_ref, w in zip(a_refs, widths):
            rows = slice(0, tm) if na == 1 else slice(m0, m0 + w)
            acc_ref[rows, :] += _dot(a_ref[...].astype(BF16), bb, 0, 0)
            m0 += w

        @pl.when(k == nk - 1)
        def _():
            o_ref[...] = acc_ref[...].astype(BF16).reshape(o_ref.shape)

    if na == 1:
        in_specs = [pl.BlockSpec((tk, tm), lambda i, k: (k, i))]
    else:
        in_specs = [pl.BlockSpec((tk, w), lambda i, k: (k, 0)) for w in widths]
    in_specs.append(pl.BlockSpec((tk, n_out), lambda i, k: (k, 0)))
    args = list(a_list) + [b]
    if into is None:
        out_spec = pl.BlockSpec((tm, n_out), lambda i, k: (i, 0))
        out_shape = jax.ShapeDtypeStruct((m, n_out), BF16)
        aliases = {}
    else:
        buf, n, off = into
        assert n_out == D and tm % n == 0 and off % n == 0 and (na == 1 or tm // n == N_DEV)
        idx = off // n
        out_spec = pl.BlockSpec((tm // n, n, D), lambda i, k: (i, idx, 0))
        out_shape = jax.ShapeDtypeStruct(buf.shape, BF16)
        in_specs.append(pl.BlockSpec(memory_space=pl.ANY))
        args.append(buf)
        aliases = {len(args) - 1: 0}
    if after is not None:
        in_specs.append(pl.BlockSpec(memory_space=pl.ANY))
        args.append(after)
    (res,), extra = _call(body, name, (m // tm, nk), in_specs, [out_spec], [out_shape], args,
                          [pltpu.VMEM((tm, n_out), F32)], comm, aliases)
    return (res, extra) if comm is not None else res


def _adamw(ws, gs, ms, vs, name, nblk=1):
    n = len(ws)
    c1 = 1.0 - B1 ** STEP
    c2 = 1.0 - B2 ** STEP

    def body(*refs):
        for a in range(n):
            w, g, m, v = (refs[k * n + a][...] for k in range(4))
            d_ref, m_ref, v_ref = (refs[(4 + k) * n + a] for k in range(3))
            m_new = B1 * m + (1.0 - B1) * g
            v_new = B2 * v + (1.0 - B2) * (g * g)
            d_ref[...] = -LR * ((m_new / c1) / (jnp.sqrt(v_new / c2) + ADAM_EPS) + WD * w)
            m_ref[...] = m_new
            v_ref[...] = v_new

    grid = (nblk,)
    assert all(w.shape[0] % nblk == 0 and (nblk == 1 or (w.shape[0] // nblk) % 8 == 0) for w in ws)
    specs = [pl.BlockSpec((w.shape[0] // nblk, w.shape[1]), lambda i: (i, 0)) for w in ws]
    outs, _ = _call(body, name, grid, specs * 4, specs * 3, [jax.ShapeDtypeStruct(w.shape, F32) for w in ws] * 3,
                    (*ws, *gs, *ms, *vs))
    return outs[:n], outs[n:2 * n], outs[2 * n:]


def _rows1024(a, rows):
    flat = a.reshape(-1, D)
    return jnp.pad(flat, ((0, rows - flat.shape[0]), (0, 0)))


def _pack_shards(even_w_in, even_w_out, odd_w_in, q_b, kv_b, odd_w_out, ffn_w_gate, ffn_w_up, ffn_w_down):
    mix0 = jnp.concatenate([even_w_in[0].T, jnp.zeros((OFF_EOUT - N_EIN, D), F32), even_w_out[0]], axis=0)
    gu = [jnp.concatenate([ffn_w_gate[layer].T, ffn_w_up[layer].T], axis=0) for layer in range(2)]
    mix1 = jnp.concatenate([jnp.pad(odd_w_in[0], ((0, 0), (0, D - ODD_IN))), odd_w_out[0],
                            _rows1024(q_b[0].T, N_QB), _rows1024(kv_b[0].T, N_KVB),
                            jnp.zeros((R_MIX1 - OFF_KVB - N_KVB, D), F32)], axis=0)
    return [c.astype(BF16) for c in (mix0, gu[0], ffn_w_down[0], mix1, gu[1], ffn_w_down[1])]


def _pad_heads(a):
    k = a.shape[1]
    return jnp.pad(a.reshape(HEADS, QK_DIM, k), ((0, 0), (0, HP - QK_DIM), (0, 0))).reshape(HEADS * HP, k)


def _small_pack(parts):
    flat = []
    for p in parts:
        v = p.reshape(-1)
        flat.append(jnp.pad(v, (0, (-v.shape[0]) % 1024)))
    return jnp.concatenate(flat).reshape(-1, 128)


def _small_unpack(buf, shapes):
    flat = buf.reshape(-1)
    out, off = [], 0
    for s in shapes:
        size = int(np.prod(s))
        out.append(flat[off:off + size].reshape(s))
        off += size + (-size) % 1024
    return out


def _step(x3d, positions, target3d, chunks, tile, where, mix_norm, ffn_norm, sg_ln_g, sg_w_s, sg_b_s,
          pool_w, q_norm, k_norm):
    bsz, seq, _ = x3d.shape
    t = bsz * seq
    x0 = x3d.reshape(t, D)
    target = target3d.reshape(t, D)
    my_mix0, my_gu0, my_d0, my_mix1, my_gu1, my_d1 = chunks

    lane = np.arange(128)
    inv_freq = np.where(lane < QK_ROPE, ROPE_THETA ** (-(2.0 * (lane % 32)) / QK_ROPE), 0.0)
    inv_freq = jnp.asarray(inv_freq.reshape(1, 128), F32)
    (cos, sin), (w_mix0, tiles) = _rope_tables(positions.reshape(t, 1), inv_freq, _gather_comm([my_mix0, tile]))

    conv_w = tiles[:, 0:3, 0:64].transpose(1, 0, 2).reshape(3, SC_W)
    pool_scale = tiles[:, 3, 0:32].reshape(1, POOL_W)
    q_a_norm = tiles[:, 4, 0:48].reshape(1, Q_LORA)
    kv_a_norm = tiles[:, 5, 0:32].reshape(1, KV_LORA)
    ws = sg_w_s[0]
    bst = jnp.pad(sg_b_s[0].T, ((0, 0), (0, 128 - SG_HEADS)))
    cw = jnp.pad(conv_w, ((0, 8 - 3), (0, 0)))
    pw_bd = jax.scipy.linalg.block_diag(*[pool_w[0, g] for g in range(4)]).astype(BF16)
    gq = jnp.pad(q_norm * ATT_SCALE, ((0, 0), (0, HP - QK_DIM)))
    gk = jnp.pad(k_norm, ((0, 0), (0, HP - QK_DIM)))

    (x1, proj_e), (w_gu0,) = _even_fwd(x0, w_mix0, mix_norm[0:1], sg_ln_g, ws, bst, cw, seq, _gather_comm([my_gu0]))
    (g0, u0, act0), (w_d0, w_mix1) = _ffn_up(x1, w_gu0, ffn_norm[0:1], "ffn_up0", _gather_comm([my_d0, my_mix1]))
    (x2,), (w_d1,) = _ffn_down(x1, act0, w_d0, "ffn_down0", _gather_comm([my_d1]))
    qbt = _pad_heads(w_mix1[:, OFF_QB:OFF_QB + N_QB_USED, :].reshape(HEADS * QK_DIM, Q_LORA))
    kvbt = w_mix1[:, OFF_KVB:OFF_KVB + N_KVB, :].reshape(HEADS * HP, KV_LORA)
    (proj_o, q, kv, kr, c_out), _ = _odd_pre_fwd(x2, w_mix1, mix_norm[1:2], qbt, kvbt, q_a_norm, kv_a_norm,
                                                pw_bd, pool_scale, seq)
    (d_out, lse), (w_gu1,) = _attn_fwd(q, kv, kr, cos, sin, gq, gk, seq, _gather_comm([my_gu1]))
    x3, dy, g1, u1, loss_tile = _last_block_fwd(x2, c_out, d_out, w_mix1, w_gu1, w_d1, ffn_norm[1:2], target)

    def chunk(rows, padded=False):
        return jnp.zeros((N_DEV, rows, D), BF16) if padded else lax.empty((N_DEV, rows, D), BF16)

    (dx3, act1, dg1, du1, h3, dgam_f1, dmix_o), _ = _ffn_bwd(x3, g1, u1, dy, w_gu1, w_d1, ffn_norm[1:2], "ffn_bwd1",
                                                           None, w_mix1)
    gp_ffn1 = _tn([dg1], h3, 1408, "dw_gate1", (chunk(R_GU + N_FF), N_FF, OFF_GATE))
    gp_ffn1 = _tn([du1], h3, 1408, "dw_up1", (gp_ffn1, N_FF, OFF_UP))
    gp_ffn1 = _tn([act1], dy, 1408, "dw_down1", (gp_ffn1, N_FF, R_GU))

    gp_mix1, (ga_ffn1,) = _tn([c_out, d_out], dx3, D, "dw_oout", (chunk(R_MIX1, True), N_SQ, OFF_OOUT),
                              _pair_exchange_comm(gp_ffn1))
    pb_ffn1 = _rs_pair_sum(gp_ffn1, ga_ffn1, where, "rs_pair_sum_ffn1")
    (dq, dkv, dkr, dgq, dgk), (gb_ffn1,) = _attn_bwd(q, kv, kr, cos, sin, gq, gk, dmix_o, d_out, lse, seq,
                                                    _chip_exchange_comm(pb_ffn1))
    (dx2, dproj_o, h2, qn, kvn, dgam_m1, dqa, dkva, dpw_bd, dps) = _odd_pre_bwd(
        x2, proj_o, dx3, dmix_o, dq, dkv, dkr, w_mix1, mix_norm[1:2], qbt, kvbt, q_a_norm, kv_a_norm, pw_bd,
        pool_scale, seq)
    gp_mix1 = _tn([h2], dproj_o, D, "dw_oin", (gp_mix1, N_SQ, OFF_OIN))
    d_qbt = _tn([dq], qn, HEADS * HP, "dw_qb")
    d_qb_rows = d_qbt.reshape(HEADS, HP, Q_LORA)[:, :QK_DIM].reshape(N_DEV, N_QB_USED, D)
    d_kvb_rows = _tn([dkv], kvn, HEADS * HP, "dw_kvb").reshape(N_DEV, N_KVB, D)
    gp_mix1 = lax.dynamic_update_slice(gp_mix1, d_qb_rows, (0, OFF_QB, 0))
    gp_mix1 = lax.dynamic_update_slice(gp_mix1, d_kvb_rows, (0, OFF_KVB, 0))

    (dx1, act0, dg0, du0, h1, dgam_f0), (ga_mix1,) = _ffn_bwd(x1, g0, u0, dx2, w_gu0, w_d0, ffn_norm[0:1], "ffn_bwd0",
                                                             _pair_exchange_comm(gp_mix1))
    pb_mix1 = _rs_pair_sum(gp_mix1, ga_mix1, where, "rs_pair_sum_mix1")
    *open_mix1, started = _chip_exchange_start(pb_mix1, "mix1")
    gp_ffn0a = _tn([dg0], h1, 1408, "dw_gate0", (chunk(R_GU), N_FF, OFF_GATE), None, started)
    gp_ffn0a = _tn([du0], h1, 1408, "dw_up0", (gp_ffn0a, N_FF, OFF_UP))
    gp_ffn0b, (ga_ffn0a,) = _tn([act0], dx2, 1408, "dw_down0", (chunk(N_FF), N_FF, 0),
                                _pair_exchange_comm(gp_ffn0a))
    pb_ffn0a = _rs_pair_sum(gp_ffn0a, ga_ffn0a, where, "rs_pair_sum_ffn0a")
    *open_ffn0a, started = _chip_exchange_start(pb_ffn0a, "ffn0a")

    (dx0, dproj_e, mix_e, h0, dgam_m0, dws, dbc, dlng, dcw), _ = _even_bwd(
        x0, proj_e, dx1, w_mix0, mix_norm[0:1], sg_ln_g, ws, bst, cw + 0.0 * started[:, :1], seq)

    small = _small_pack([
        jnp.concatenate([dgam_m0, dgam_m1], 0), jnp.concatenate([dgam_f0, dgam_f1], 0), dlng,
        dws[None], dbc[:, :SG_HEADS].T[None], dcw[:3],
        jnp.stack([dpw_bd[g * POOL_GD:(g + 1) * POOL_GD, g * POOL_GD:(g + 1) * POOL_GD] for g in range(4)])[None],
        dps, dqa, dkva, dgq[:, :QK_DIM] * ATT_SCALE, dgk[:, :QK_DIM], loss_tile[0:1, 0:1]])
    gp_ein, (small_all, ga_ffn0b) = _tn([dproj_e], h0, 1280, "dw_ein", (chunk(N_EIN), N_EIN, 0),
                                        _both(_gather_comm([small]), _pair_exchange_comm(gp_ffn0b)))
    pb_ffn0b = _rs_pair_sum(gp_ffn0b, ga_ffn0b, where, "rs_pair_sum_ffn0b")
    *open_ffn0b, started = _chip_exchange_start(pb_ffn0b, "ffn0b")
    gp_eout, (ga_ein,) = _tn([mix_e], dx1, D, "dw_eout", (chunk(N_SQ), N_SQ, 0), _pair_exchange_comm(gp_ein),
                             started)
    pb_ein = _rs_pair_sum(gp_ein, ga_ein, where, "rs_pair_sum_ein")
    *open_ein, started = _chip_exchange_start(pb_ein, "ein")
    small_sum = _small_unpack(_sum_gathered(small_all), SMALL_SHAPES)
    in_flight = (open_ffn0a, open_ffn0b, open_mix1, open_ein)
    return dx0.reshape(bsz, seq, D), (pb_ffn1, gb_ffn1), in_flight, (gp_eout, started), small_sum


SMALL_SHAPES = [(2, D), (2, D), (1, SG_W), (1, SG_HEADS, 128, 128), (1, SG_HEADS, 128), (3, SC_W),
                (1, 4, POOL_GD, POOL_GD), (1, POOL_W), (1, Q_LORA), (1, KV_LORA), (1, QK_DIM), (1, QK_DIM), (1, 1)]


def kernel(x, positions, mix_norm, ffn_norm, even_w_in, sg_ln_g, sg_w_s, sg_b_s, sc_conv_w, even_w_out, odd_w_in, pool_w, pool_scale, q_a_norm, q_b, kv_a_norm, kv_b, q_norm, k_norm, odd_w_out, ffn_w_gate, ffn_w_up, ffn_w_down, loss_target, m_mix_norm, m_ffn_norm, m_even_w_in, m_sg_ln_g, m_sg_w_s, m_sg_b_s, m_sc_conv_w, m_even_w_out, m_odd_w_in, m_pool_w, m_pool_scale, m_q_a_norm, m_q_b, m_kv_a_norm, m_kv_b, m_q_norm, m_k_norm, m_odd_w_out, m_ffn_w_gate, m_ffn_w_up, m_ffn_w_down, v_mix_norm, v_ffn_norm, v_even_w_in, v_sg_ln_g, v_sg_w_s, v_sg_b_s, v_sc_conv_w, v_even_w_out, v_odd_w_in, v_pool_w, v_pool_scale, v_q_a_norm, v_q_b, v_kv_a_norm, v_kv_b, v_q_norm, v_k_norm, v_odd_w_out, v_ffn_w_gate, v_ffn_w_up, v_ffn_w_down):
    xi, yi, ci = _place()
    me = 4 * xi + 2 * yi + ci

    chunks = _pack_shards(even_w_in, even_w_out, odd_w_in, q_b, kv_b, odd_w_out, ffn_w_gate, ffn_w_up, ffn_w_down)

    def lane_pad(a):
        return jnp.pad(a, ((0, 0), (0, 128 - a.shape[1])))

    tile = jnp.concatenate([lane_pad(sc_conv_w[0]), lane_pad(pool_scale), lane_pad(q_a_norm), lane_pad(kv_a_norm),
                            jnp.zeros((2, 128), F32)], axis=0)
    chip = 2 * xi + yi
    where = jnp.stack([ci, chip, chip ^ 2, chip ^ 1, chip ^ 3]).astype(jnp.int32)
    grad_x, (pb_ffn1, gb_ffn1), in_flight, (gp_eout, started), tot = _step(
        x, positions, loss_target, chunks, tile, where, mix_norm, ffn_norm, sg_ln_g, sg_w_s, sg_b_s,
        pool_w, q_norm, k_norm)

    (ga_eout,) = _comm_alone(_pair_exchange_comm(gp_eout), "rs_pair_exchange_eout")
    pb_eout = _rs_pair_sum(gp_eout, ga_eout, where + (0.0 * started[0, :1]).astype(jnp.int32), "rs_pair_sum_eout")
    eout_sems, pb_eout, land_eout, started = _chip_exchange_start(pb_eout, "eout")
    landed = [_chip_exchange_wait(*parts, started, tag)
              for parts, tag in zip(in_flight, ("ffn0a", "ffn0b", "mix1", "ein"))]
    gsh_ffn0a, gsh_ffn0b, gsh_mix1, gsh_ein, gsh_ffn1 = _rs_final_sums(
        [pb for pb, _ in landed] + [pb_ffn1], [gb for _, gb in landed] + [gb_ffn1], "rs_final_sums", started)

    (g_mix, g_ffn, g_lng, g_ws, g_bs, g_cw_full, g_pw, g_ps_full, g_qa_full, g_kva_full, g_qn, g_kn, loss) = tot
    g_cw = lax.dynamic_slice_in_dim(g_cw_full, me * 64, 64, axis=1)[None]
    g_ps = lax.dynamic_slice_in_dim(g_ps_full, me * 32, 32, axis=1)
    g_qa = lax.dynamic_slice_in_dim(g_qa_full, me * 48, 48, axis=1)
    g_kva = lax.dynamic_slice_in_dim(g_kva_full, me * 32, 32, axis=1)

    def tr(a):
        return jnp.swapaxes(a, -1, -2)

    g_gate = tr(jnp.stack([gsh_ffn0a[OFF_GATE:OFF_GATE + N_FF], gsh_ffn1[OFF_GATE:OFF_GATE + N_FF]]))
    g_up = tr(jnp.stack([gsh_ffn0a[OFF_UP:OFF_UP + N_FF], gsh_ffn1[OFF_UP:OFF_UP + N_FF]]))
    g_down = jnp.stack([gsh_ffn0b, gsh_ffn1[R_GU:R_GU + N_FF]])
    g_oin = gsh_mix1[OFF_OIN:OFF_OIN + N_SQ, :ODD_IN][None]
    g_oout = gsh_mix1[OFF_OOUT:OFF_OOUT + N_SQ][None]
    g_qb = tr(gsh_mix1[OFF_QB:OFF_QB + N_QB_USED].reshape(1, 144, Q_LORA))
    g_kvb = tr(gsh_mix1[OFF_KVB:OFF_KVB + N_KVB].reshape(1, 192, KV_LORA))
    transposed = ("even_w_in", "odd_w_in", "q_b", "kv_b", "ffn_w_gate", "ffn_w_up")

    names = ("mix_norm", "ffn_norm", "even_w_in", "sg_ln_g", "sg_w_s", "sg_b_s", "sc_conv_w", "even_w_out",
             "odd_w_in", "pool_w", "pool_scale", "q_a_norm", "q_b", "kv_a_norm", "kv_b", "q_norm", "k_norm",
             "odd_w_out", "ffn_w_gate", "ffn_w_up", "ffn_w_down")
    grads = dict(mix_norm=g_mix, ffn_norm=g_ffn, sg_ln_g=g_lng, sg_w_s=g_ws, sg_b_s=g_bs,
                 sc_conv_w=g_cw, odd_w_in=g_oin, pool_w=g_pw, pool_scale=g_ps, q_a_norm=g_qa,
                 q_b=g_qb, kv_a_norm=g_kva, kv_b=g_kvb, q_norm=g_qn, k_norm=g_kn, odd_w_out=g_oout,
                 ffn_w_gate=g_gate, ffn_w_up=g_up, ffn_w_down=g_down)
    weights = dict(mix_norm=mix_norm, ffn_norm=ffn_norm, even_w_in=even_w_in, sg_ln_g=sg_ln_g, sg_w_s=sg_w_s,
                   sg_b_s=sg_b_s, sc_conv_w=sc_conv_w, even_w_out=even_w_out, odd_w_in=odd_w_in, pool_w=pool_w,
                   pool_scale=pool_scale, q_a_norm=q_a_norm, q_b=q_b, kv_a_norm=kv_a_norm, kv_b=kv_b, q_norm=q_norm,
                   k_norm=k_norm, odd_w_out=odd_w_out, ffn_w_gate=ffn_w_gate, ffn_w_up=ffn_w_up,
                   ffn_w_down=ffn_w_down)
    m_in = dict(mix_norm=m_mix_norm, ffn_norm=m_ffn_norm, even_w_in=m_even_w_in, sg_ln_g=m_sg_ln_g, sg_w_s=m_sg_w_s,
                sg_b_s=m_sg_b_s, sc_conv_w=m_sc_conv_w, even_w_out=m_even_w_out, odd_w_in=m_odd_w_in,
                pool_w=m_pool_w, pool_scale=m_pool_scale, q_a_norm=m_q_a_norm, q_b=m_q_b, kv_a_norm=m_kv_a_norm,
                kv_b=m_kv_b, q_norm=m_q_norm, k_norm=m_k_norm, odd_w_out=m_odd_w_out, ffn_w_gate=m_ffn_w_gate,
                ffn_w_up=m_ffn_w_up, ffn_w_down=m_ffn_w_down)
    v_in = dict(mix_norm=v_mix_norm, ffn_norm=v_ffn_norm, even_w_in=v_even_w_in, sg_ln_g=v_sg_ln_g, sg_w_s=v_sg_w_s,
                sg_b_s=v_sg_b_s, sc_conv_w=v_sc_conv_w, even_w_out=v_even_w_out, odd_w_in=v_odd_w_in,
                pool_w=v_pool_w, pool_scale=v_pool_scale, q_a_norm=v_q_a_norm, q_b=v_q_b, kv_a_norm=v_kv_a_norm,
                kv_b=v_kv_b, q_norm=v_q_norm, k_norm=v_k_norm, odd_w_out=v_odd_w_out, ffn_w_gate=v_ffn_w_gate,
                ffn_w_up=v_ffn_w_up, ffn_w_down=v_ffn_w_down)
    delta, new_m, new_v = {}, {}, {}

    def as2d(k, a):
        a = tr(a) if k in transposed else a
        return a.reshape(-1, a.shape[-1])

    def back(k, a):
        shape = weights[k].shape
        return tr(a.reshape(shape[:-2] + (shape[-1], shape[-2]))) if k in transposed else a.reshape(shape)

    def update(group, name, nblk=1):
        outs = _adamw([as2d(k, weights[k]) for k in group], [as2d(k, grads[k]) for k in group],
                      [as2d(k, m_in[k]) for k in group], [as2d(k, v_in[k]) for k in group], name, nblk)
        for i, k in enumerate(group):
            delta[k], new_m[k], new_v[k] = (back(k, o[i]) for o in outs)

    grads["even_w_in"] = tr(gsh_ein[None])
    update(["ffn_w_gate", "ffn_w_up", "ffn_w_down"], "adamw_ffn", 4)
    update(["even_w_in", "odd_w_in", "odd_w_out"], "adamw_mix", 2)
    update([k for k in names if k not in delta and k != "even_w_out"], "adamw_small")

    pb_eout, gb_eout = _chip_exchange_wait(eout_sems, pb_eout, land_eout, new_v["k_norm"], "eout")
    (gsh_eout,) = _rs_final_sums([pb_eout], [gb_eout], "rs_final_sum_eout")
    grads["even_w_out"] = gsh_eout[None]
    update(["even_w_out"], "adamw_even_w_out", 2)

    return (loss.reshape(()), grad_x, *[grads[k] for k in names], *[delta[k] for k in names],
            *[new_m[k] for k in names], *[new_v[k] for k in names])
```

```python
import functools

import numpy as np
import jax
import jax.numpy as jnp
from jax import lax
from jax.experimental import pallas as pl
from jax.experimental.pallas import tpu as pltpu

F32 = jnp.float32
BF16 = jnp.bfloat16
MESH = pl.DeviceIdType.MESH

D = 1024
EPS = 1e-6
NEG_INF = -1e30
SG_HEADS, SG_HD, SG_W, SG_CHUNK = 4, 128, 512, 128
SC_W = 512
EVEN_IN = 2560
POOL_W = 256
POOL_GD = 64
Q_LORA, KV_LORA, QK_ROPE, QK_NOPE, V_DIM = 384, 256, 64, 128, 128
QK_DIM = QK_NOPE + QK_ROPE
HEADS = 6
HP = 256
ODD_IN = 960
D_FF = 2816
ROPE_THETA = 10000.0
ATT_SCALE = QK_DIM ** -0.5
LR, B1, B2, ADAM_EPS, WD, STEP = 0.001, 0.9, 0.999, 1e-08, 0.01, 10

N_DEV = 8
TB = 512
TB_FFN_BWD = 256
TK_DW = 1024
HALO = 16
VMEM_LIMIT = 56 * 1024 * 1024

N_EIN, N_FF, N_SQ = 320, 352, 128
OFF_EIN, OFF_EOUT = 0, 384
OFF_GATE, OFF_UP, R_GU = 0, 352, 704
OFF_OIN, OFF_OOUT, OFF_QB, OFF_KVB, R_MIX1 = 0, 128, 256, 320, 384
N_QB, N_QB_USED, N_KVB = 64, 54, 48

INV_SQRT2 = 0.7071067811865476
INV_SQRT_2PI = 0.3989422804014327


def _dot(a, b, ca, cb):
    return lax.dot_general(a, b, (((ca,), (cb,)), ((), ())), preferred_element_type=F32)


def _cparams(n_axes=1):
    return pltpu.CompilerParams(dimension_semantics=("arbitrary",) * n_axes, vmem_limit_bytes=VMEM_LIMIT)


def _wspec(n, off):
    assert off % n == 0
    idx = off // n
    return pl.BlockSpec((N_DEV, n, D), lambda i: (0, idx, 0), pipeline_mode=pl.Buffered(1))


def _const_spec(shape):
    zeros = (0,) * len(shape)
    return pl.BlockSpec(shape, lambda *_: zeros)


class _Comm:
    def __init__(self, ins, out_shapes, sems, start, wait, mid=None):
        self.ins, self.out_shapes, self.sems, self.start, self.wait, self.mid = ins, out_shapes, sems, start, wait, mid


def _both(c1, c2):
    def split(f1, f2):
        def run(ins, outs, sems):
            f1(ins[:len(c1.ins)], outs[:len(c1.out_shapes)], sems[:len(c1.sems)])
            f2(ins[len(c1.ins):], outs[len(c1.out_shapes):], sems[len(c1.sems):])
        return run

    def nothing(ins, outs, sems):
        pass

    mid = None if c1.mid is None and c2.mid is None else split(c1.mid or nothing, c2.mid or nothing)
    return _Comm(c1.ins + c2.ins, c1.out_shapes + c2.out_shapes, c1.sems + c2.sems,
                 split(c1.start, c2.start), split(c1.wait, c2.wait), mid)


def _call(body, name, grid, in_specs, out_specs, out_shape, args, scratch_shapes=(), comm=None, aliases=None):
    n_axes = len(grid)
    aliases = aliases or {}
    if comm is None:
        res = pl.pallas_call(
            body, name=name, grid=grid, in_specs=list(in_specs), out_specs=list(out_specs),
            out_shape=list(out_shape), scratch_shapes=list(scratch_shapes), input_output_aliases=aliases,
            compiler_params=_cparams(n_axes))(*args)
        return list(res), []
    ni, no, ns = len(in_specs), len(out_specs), len(scratch_shapes)
    ci, co = len(comm.ins), len(comm.out_shapes)
    n_steps = int(np.prod(grid))

    def carrier(*refs):
        ins, cin = refs[:ni], refs[ni:ni + ci]
        outs, cout = refs[ni + ci:ni + ci + no], refs[ni + ci + no:ni + ci + no + co]
        scr, sems = refs[ni + ci + no + co:ni + ci + no + co + ns], refs[ni + ci + no + co + ns:]
        step = 0
        for a in range(n_axes):
            step = step * grid[a] + pl.program_id(a)

        @pl.when(step == 0)
        def _():
            comm.start(cin, cout, sems)

        body(*ins, *outs, *scr)

        if comm.mid is not None and n_steps >= 4:
            @pl.when(step == n_steps // 2)
            def _():
                comm.mid(cin, cout, sems)

        @pl.when(step == n_steps - 1)
        def _():
            if comm.mid is not None and n_steps < 4:
                comm.mid(cin, cout, sems)
            comm.wait(cin, cout, sems)

    any_spec = pl.BlockSpec(memory_space=pl.ANY)
    res = pl.pallas_call(
        carrier, name=name, grid=grid, in_specs=list(in_specs) + [any_spec] * ci,
        out_specs=list(out_specs) + [any_spec] * co, out_shape=list(out_shape) + list(comm.out_shapes),
        scratch_shapes=list(scratch_shapes) + list(comm.sems), input_output_aliases=aliases,
        compiler_params=_cparams(n_axes))(*args, *comm.ins)
    return list(res[:no]), list(res[no:])


def _comm_alone(comm, name):
    ci, co = len(comm.ins), len(comm.out_shapes)

    def body(*refs):
        cin, cout, sems = refs[:ci], refs[ci:ci + co], refs[ci + co:]
        comm.start(cin, cout, sems)
        if comm.mid is not None:
            comm.mid(cin, cout, sems)
        comm.wait(cin, cout, sems)

    any_spec = pl.BlockSpec(memory_space=pl.ANY)
    res = pl.pallas_call(
        body, name=name, out_shape=list(comm.out_shapes), in_specs=[any_spec] * ci, out_specs=[any_spec] * co,
        scratch_shapes=list(comm.sems))(*comm.ins)
    return list(res)


def _rms(x, g):
    r = lax.rsqrt(jnp.mean(x * x, axis=-1, keepdims=True) + EPS)
    return x * r * g, r


def _rms_bwd(x, r, g, dy):
    xh = x * r
    dxh = dy * g
    dx = r * (dxh - xh * jnp.mean(dxh * xh, axis=-1, keepdims=True))
    dg = jnp.sum(dy * xh, axis=0, keepdims=True)
    return dx, dg


def _gelu(x):
    return 0.5 * x * (1.0 + lax.erf(x * INV_SQRT2))


def _gelu_grad(x):
    return 0.5 * (1.0 + lax.erf(x * INV_SQRT2)) + x * jnp.exp(-0.5 * x * x) * INV_SQRT_2PI


def _shift_down(a, k):
    rows = lax.broadcasted_iota(jnp.int32, a.shape, 0)
    return jnp.where(rows >= k, pltpu.roll(a, k, 0), 0.0)


def _shift_up(a, k):
    n = a.shape[0]
    rows = lax.broadcasted_iota(jnp.int32, a.shape, 0)
    return jnp.where(rows < n - k, pltpu.roll(a, n - k, 0), 0.0)


def _tril_bf16(w):
    r = lax.broadcasted_iota(jnp.int32, w.shape, 0)
    c = lax.broadcasted_iota(jnp.int32, w.shape, 1)
    return jnp.where(r >= c, w, 0.0).astype(BF16)


def _ln_head(vh, g):
    mu = jnp.mean(vh, axis=-1, keepdims=True)
    xc = vh - mu
    rr = lax.rsqrt(jnp.mean(xc * xc, axis=-1, keepdims=True) + EPS)
    xh = xc * rr
    return xh * g, xh, rr


def _conv_fwd(z, tail, cw_ref):
    ext = jnp.concatenate([tail, z], axis=0)
    zs1 = _shift_down(ext, 1)[HALO:]
    zs2 = _shift_down(ext, 2)[HALO:]
    y = cw_ref[2:3, :] * z + cw_ref[1:2, :] * zs1 + cw_ref[0:1, :] * zs2
    return y, zs1, zs2


def _pool_cnt(shape, blk_in_seq):
    rows = lax.broadcasted_iota(jnp.int32, shape, 0)
    grp = lax.broadcasted_iota(jnp.int32, shape, 1) // POOL_GD
    win = jnp.where(grp == 0, 2, jnp.where(grp == 1, 4, jnp.where(grp == 2, 8, 16)))
    tpos = blk_in_seq * shape[0] + rows + 1
    return jnp.minimum(tpos, win).astype(F32), grp


def _pool_select(grp, s2, s4, s8, s16):
    return jnp.where(grp == 0, s2, jnp.where(grp == 1, s4, jnp.where(grp == 2, s8, s16)))


def _pool_fwd(z, tail, blk_in_seq):
    ext = jnp.concatenate([tail, z], axis=0)
    s2 = ext + _shift_down(ext, 1)
    s4 = s2 + _shift_down(s2, 2)
    s8 = s4 + _shift_down(s4, 4)
    s16 = s8 + _shift_down(s8, 8)
    cnt, grp = _pool_cnt(z.shape, blk_in_seq)
    sums = _pool_select(grp, s2[HALO:], s4[HALO:], s8[HALO:], s16[HALO:])
    return sums / cnt - z, cnt, grp


def _pool_bwd(dpooled, dpm, head, grp):
    n = dpm.shape[0]
    ext = jnp.concatenate([dpm, head], axis=0)
    u2 = ext + _shift_up(ext, 1)
    u4 = u2 + _shift_up(u2, 2)
    u8 = u4 + _shift_up(u4, 4)
    u16 = u8 + _shift_up(u8, 8)
    return _pool_select(grp, u2[:n], u4[:n], u8[:n], u16[:n]) - dpooled


def _lane_sums(a):
    return _dot(a.astype(BF16), jnp.ones((a.shape[1], a.shape[1]), BF16), 1, 0)


def _swap_halves(y1):
    src = lax.broadcasted_iota(jnp.int32, (128, 128), 0)
    dst = lax.broadcasted_iota(jnp.int32, (128, 128), 1)
    perm = jnp.where(((dst < 32) & (src == dst + 32)) | ((dst >= 32) & (dst < QK_ROPE) & (src == dst - 32)), 1.0, 0.0)
    return _dot(y1.astype(BF16), perm.astype(BF16), 1, 0)


def _rope(y1, c, s):
    return y1 * c + _swap_halves(y1) * s


def _rope_bwd(d1, c, s):
    return d1 * c + _swap_halves(d1 * s)


def _qk_prep(x, g, c, s):
    r = lax.rsqrt(_lane_sums(x * x) * (1.0 / QK_DIM) + EPS)
    y = x * r * g
    return jnp.concatenate([y[:, :128], _rope(y[:, 128:], c, s)], axis=1), r


def _qk_prep_bwd(dout, x, r, g, c, s):
    dy = jnp.concatenate([dout[:, :128], _rope_bwd(dout[:, 128:], c, s)], axis=1)
    xh = x * r
    dxh = dy * g
    dx = r * (dxh - xh * (_lane_sums(dxh * xh) * (1.0 / QK_DIM)))
    return dx, jnp.sum(dy * xh, axis=0, keepdims=True)


def _place():
    return lax.axis_index("x"), lax.axis_index("y"), lax.axis_index("c")


def _gather_comm(arrs):
    n = len(arrs)

    def halves(a):
        rows = arrs[a].shape[0]
        tile = 16 if arrs[a].dtype == BF16 else 8
        top = rows // 2 if rows % (2 * tile) == 0 else rows
        return (0, top), (top, rows - top)

    def plan(ins, outs, sems):
        send_sems, recv_sems, local_sems = sems
        x, y, c = _place()
        me, sib, xn, yn, dg = (x, y, c), (x, y, 1 - c), (1 - x, y, c), (x, 1 - y, c), (1 - x, 1 - y, c)

        def slot(a, dev, part=None):
            ref = outs[a].at[4 * dev[0] + 2 * dev[1] + dev[2]]
            return ref if part is None else ref.at[pl.ds(part[0], part[1])]

        def copy(a, k, block, to, src=None, part=None):
            return pltpu.make_async_remote_copy(
                src_ref=slot(a, block, part) if src is None else src, dst_ref=slot(a, block, part),
                send_sem=send_sems.at[a, k], recv_sem=recv_sems.at[a, k], device_id=to, device_id_type=MESH)

        local = [pltpu.make_async_copy(ins[a], slot(a, me), local_sems.at[a]) for a in range(n)]
        return me, sib, xn, yn, dg, copy, local

    def start(ins, outs, sems):
        me, sib, xn, yn, _, copy, local = plan(ins, outs, sems)
        for a in range(n):
            local[a].start()
            for k, to in enumerate((sib, xn, yn)):
                copy(a, k, me, to, src=ins[a]).start()

    def mid(ins, outs, sems):
        me, sib, xn, yn, _, copy, _ = plan(ins, outs, sems)
        for a in range(n):
            top, bottom = halves(a)
            copy(a, 1, xn, me).wait_recv()
            copy(a, 3, xn, yn, part=top).start()
            copy(a, 5, xn, sib).start()
            copy(a, 2, yn, me).wait_recv()
            if bottom[1]:
                copy(a, 4, yn, xn, part=bottom).start()
            copy(a, 6, yn, sib).start()

    def wait(ins, outs, sems):
        me, sib, xn, yn, dg, copy, local = plan(ins, outs, sems)
        other = lambda dev: (dev[0], dev[1], 1 - dev[2])
        for a in range(n):
            top, bottom = halves(a)
            copy(a, 3, dg, me, part=top).wait_recv()
            if bottom[1]:
                copy(a, 4, dg, me, part=bottom).wait_recv()
            copy(a, 7, dg, sib).start()
        for a in range(n):
            top, bottom = halves(a)
            for k, block in ((0, sib), (5, other(xn)), (6, other(yn)), (7, other(dg))):
                copy(a, k, block, me).wait_recv()
            for k, block in ((0, me), (1, me), (2, me), (5, xn), (6, yn), (7, dg)):
                copy(a, k, block, me, src=ins[a] if k < 3 else None).wait_send()
            copy(a, 3, xn, me, part=top).wait_send()
            if bottom[1]:
                copy(a, 4, yn, me, part=bottom).wait_send()
            local[a].wait()

    return _Comm(
        list(arrs), [jax.ShapeDtypeStruct((N_DEV,) + a.shape, a.dtype) for a in arrs],
        [pltpu.SemaphoreType.DMA((n, 8)), pltpu.SemaphoreType.DMA((n, 8)), pltpu.SemaphoreType.DMA((n,))],
        start, wait, mid)


def _sum_gathered(g):
    rows = g.shape[1]

    def body(g_ref, sum_ref):
        total = g_ref[0]
        for d in range(1, N_DEV):
            total = total + g_ref[d]
        sum_ref[...] = total

    return pl.pallas_call(
        body, name="sum_gathered_small", out_shape=jax.ShapeDtypeStruct((rows, 128), F32), grid=(1,),
        in_specs=[pl.BlockSpec((N_DEV, rows, 128), lambda i: (0, 0, 0))],
        out_specs=pl.BlockSpec((rows, 128), lambda i: (0, 0)), compiler_params=_cparams(1),
    )(g)


def _sum_rows(rows):
    return rows if rows <= 512 else rows // 2


def _pair_exchange_comm(gp):
    _, rows, cols = gp.shape

    def copies(ins, outs, sems):
        send_sems, recv_sems = sems
        x, y, c = _place()
        return [pltpu.make_async_remote_copy(
            src_ref=ins[0].at[2 * j + (1 - c)], dst_ref=outs[0].at[j], send_sem=send_sems.at[j],
            recv_sem=recv_sems.at[j], device_id=(x, y, 1 - c), device_id_type=MESH) for j in range(4)]

    def start(ins, outs, sems):
        for cp in copies(ins, outs, sems):
            cp.start()

    def wait(ins, outs, sems):
        for cp in copies(ins, outs, sems):
            cp.wait()

    return _Comm([gp], [jax.ShapeDtypeStruct((4, rows, cols), gp.dtype)],
                 [pltpu.SemaphoreType.DMA((4,)), pltpu.SemaphoreType.DMA((4,))], start, wait)


def _rs_pair_sum(gp, got, where, name):
    _, rows, cols = got.shape
    rb = _sum_rows(rows)
    gp4 = gp.reshape(4, 2, rows, cols)

    def body(w_ref, a_ref, b_ref, o_ref):
        o_ref[0] = (a_ref[0, 0].astype(F32) + b_ref[0].astype(F32)).astype(o_ref.dtype)

    return pl.pallas_call(
        body, name=name, out_shape=jax.ShapeDtypeStruct((4, rows, cols), gp.dtype),
        grid_spec=pltpu.PrefetchScalarGridSpec(
            num_scalar_prefetch=1, grid=(4, rows // rb),
            in_specs=[pl.BlockSpec((1, 1, rb, cols), lambda k, r, w: (w[1 + k], w[0], r, 0)),
                      pl.BlockSpec((1, rb, cols), lambda k, r, w: (w[1 + k], r, 0))],
            out_specs=pl.BlockSpec((1, rb, cols), lambda k, r, w: (k, r, 0))),
        compiler_params=_cparams(2),
    )(where, gp4, got)


def _chip_exchange_comm(pb):
    _, rows, cols = pb.shape

    def copies(ins, outs, sems):
        send_sems, recv_sems = sems
        x, y, c = _place()
        chips = [(1 - x, y), (x, 1 - y), (1 - x, 1 - y)]
        return [pltpu.make_async_remote_copy(
            src_ref=ins[0].at[1 + k], dst_ref=outs[0].at[k], send_sem=send_sems.at[k],
            recv_sem=recv_sems.at[k], device_id=(px, py, c), device_id_type=MESH)
            for k, (px, py) in enumerate(chips)]

    def start(ins, outs, sems):
        for cp in copies(ins, outs, sems):
            cp.start()

    def wait(ins, outs, sems):
        for cp in copies(ins, outs, sems):
            cp.wait()

    return _Comm([pb], [jax.ShapeDtypeStruct((3, rows, cols), pb.dtype)],
                 [pltpu.SemaphoreType.DMA((3,)), pltpu.SemaphoreType.DMA((3,))], start, wait)


def _chip_exchange_start(pb, tag):
    _, rows, cols = pb.shape

    def body(pb_ref, land_ref, *rest):
        sems, token = rest[:6], rest[8]
        x, y, c = _place()
        chips = [(1 - x, y), (x, 1 - y), (1 - x, 1 - y)]
        for k, (px, py) in enumerate(chips):
            pltpu.make_async_remote_copy(
                src_ref=pb_ref.at[1 + k], dst_ref=land_ref.at[k], send_sem=sems[k], recv_sem=sems[3 + k],
                device_id=(px, py, c), device_id_type=MESH).start()
        token[...] = jnp.zeros_like(token)

    hbm = pl.BlockSpec(memory_space=pltpu.HBM)
    sem = pl.BlockSpec(memory_space=pltpu.SEMAPHORE)
    land = lax.empty((3, rows, cols), pb.dtype)
    res = pl.pallas_call(
        body, name="rs_chip_exchange_start_" + tag,
        out_shape=(*[pltpu.SemaphoreType.DMA(())] * 6, pltpu.HBM(pb.shape, pb.dtype), pltpu.HBM(land.shape, land.dtype),
                   jax.ShapeDtypeStruct((8, 128), F32)),
        in_specs=(hbm, hbm), out_specs=(*[sem] * 6, hbm, hbm, pl.BlockSpec(memory_space=pltpu.VMEM)),
        input_output_aliases={0: 6, 1: 7},
        compiler_params=pltpu.CompilerParams(has_side_effects=pltpu.SideEffectType.DATAFLOW_SIDE_EFFECTING),
    )(pltpu.with_memory_space_constraint(pb, pltpu.HBM), pltpu.with_memory_space_constraint(land, pltpu.HBM))
    return list(res[:6]), res[6], res[7], res[8]


def _chip_exchange_wait(sems, pb_thru, land_thru, after, tag):
    def body(pb_ref, land_ref, *rest):
        sems_in = rest[:6]
        x, y, c = _place()
        chips = [(1 - x, y), (x, 1 - y), (1 - x, 1 - y)]
        for k, (px, py) in enumerate(chips):
            cp = pltpu.make_async_remote_copy(
                src_ref=pb_ref.at[1 + k], dst_ref=land_ref.at[k], send_sem=sems_in[k], recv_sem=sems_in[3 + k],
                device_id=(px, py, c), device_id_type=MESH)
            cp.wait_send()
            cp.wait_recv()

    hbm = pl.BlockSpec(memory_space=pltpu.HBM)
    sem = pl.BlockSpec(memory_space=pltpu.SEMAPHORE)
    res = pl.pallas_call(
        body, name="rs_chip_exchange_wait_" + tag,
        out_shape=(pltpu.HBM(pb_thru.shape, pb_thru.dtype), pltpu.HBM(land_thru.shape, land_thru.dtype)),
        in_specs=(hbm, hbm, *[sem] * 6, pl.BlockSpec(memory_space=pl.ANY)), out_specs=(hbm, hbm),
        input_output_aliases={0: 0, 1: 1},
        compiler_params=pltpu.CompilerParams(has_side_effects=pltpu.SideEffectType.DATAFLOW_SIDE_EFFECTING),
    )(pb_thru, land_thru, *sems, after)
    return res[0], res[1]


def _rs_final_sums(pbs, gots, name, after=None):
    n = len(pbs)

    def body(*refs):
        outs = refs[len(refs) - n:]
        for a in range(n):
            m_ref, g_ref, o_ref = refs[a], refs[n + a], outs[a]
            o_ref[...] = ((m_ref[0].astype(F32) + g_ref[0].astype(F32)) + g_ref[1].astype(F32)) + g_ref[2].astype(F32)

    half = [pb.shape[1] // 2 for pb in pbs]
    in_specs = ([pl.BlockSpec((1, h, D), lambda i: (0, i, 0)) for h in half]
                + [pl.BlockSpec((3, h, D), lambda i: (0, i, 0)) for h in half])
    args = (*pbs, *gots)
    if after is not None:
        in_specs, args = in_specs + [pl.BlockSpec(memory_space=pl.ANY)], args + (after,)
    res, _ = _call(body, name, (2,), in_specs, [pl.BlockSpec((h, D), lambda i: (i, 0)) for h in half],
                   [jax.ShapeDtypeStruct(pb.shape[1:], F32) for pb in pbs], args)
    return res


def _rope_tables(pos_col, inv_freq, comm=None):
    t = pos_col.shape[0]

    def body(p_ref, f_ref, c_ref, s_ref):
        ang = p_ref[...].astype(F32) * f_ref[...]
        lane = lax.broadcasted_iota(jnp.int32, ang.shape, 1)
        c_ref[...] = jnp.where(lane < QK_ROPE, jnp.cos(ang), 0.0)
        s = jnp.sin(ang)
        s_ref[...] = jnp.where(lane < 32, -s, jnp.where(lane < QK_ROPE, s, 0.0))

    spec = pl.BlockSpec((TB, 128), lambda i: (i, 0))
    return _call(
        body, "rope_tables", (t // TB,), [pl.BlockSpec((TB, 1), lambda i: (i, 0)), _const_spec((1, 128))],
        [spec] * 2, [jax.ShapeDtypeStruct((t, 128), F32)] * 2, (pos_col, inv_freq), (), comm)


def _sgu_conv_fwd(proj, tail, lng_ref, ws_ref, bst_ref, cw_ref):
    gu = _gelu(proj[:, 0:SG_W])
    gv = _gelu(proj[:, SG_W:2 * SG_W])
    bg = proj[:, 1024:1536]
    z = proj[:, 1536:2048] * proj[:, 2048:2560]
    heads = []
    for h in range(SG_HEADS):
        sl = slice(h * SG_HD, (h + 1) * SG_HD)
        vn, _, _ = _ln_head(gv[:, sl], lng_ref[:, sl])
        vnb = vn.astype(BF16)
        wm = _tril_bf16(ws_ref[h])
        bcol = bst_ref[:, h:h + 1]
        mixed = jnp.concatenate(
            [_dot(wm, vnb[k * SG_CHUNK:(k + 1) * SG_CHUNK], 1, 0) + bcol for k in range(TB // SG_CHUNK)], axis=0)
        heads.append(gu[:, sl] * mixed)
    a_out = jnp.concatenate(heads, axis=1)
    y, _, _ = _conv_fwd(z, tail, cw_ref)
    return a_out, bg * y, z


def _even_fwd(x, wg, gamma, lng, ws, bst, cw, seq, comm=None):
    t = x.shape[0]
    nbs = seq // TB

    def body(x_ref, gam_ref, win_ref, wout_ref, lng_ref, ws_ref, bst_ref, cw_ref, x1_ref, proj_ref, tail_ref):
        i = pl.program_id(0)
        xv = x_ref[...]
        h, _ = _rms(xv, gam_ref[...])
        proj = _dot(h.astype(BF16), win_ref[...].reshape(EVEN_IN, D), 1, 1)
        proj_ref[...] = proj.astype(BF16)
        tail = jnp.where(i % nbs == 0, 0.0, tail_ref[...])
        a_out, b_out, z = _sgu_conv_fwd(proj, tail, lng_ref, ws_ref, bst_ref, cw_ref)
        tail_ref[...] = z[TB - HALO:, :]
        x1_ref[...] = (xv + _dot(a_out.astype(BF16), wout_ref[0:4].reshape(512, D), 1, 0)
                       + _dot(b_out.astype(BF16), wout_ref[4:8].reshape(512, D), 1, 0))

    row = pl.BlockSpec((TB, D), lambda i: (i, 0))
    return _call(
        body, "even_fwd", (t // TB,),
        [row, _const_spec((1, D)), _wspec(N_EIN, OFF_EIN), _wspec(N_SQ, OFF_EOUT), _const_spec((1, SG_W)),
         _const_spec((SG_HEADS, 128, 128)), _const_spec((128, 128)), _const_spec((8, SC_W))],
        [row, pl.BlockSpec((TB, EVEN_IN), lambda i: (i, 0))],
        [jax.ShapeDtypeStruct((t, D), F32), jax.ShapeDtypeStruct((t, EVEN_IN), BF16)],
        (x, gamma, wg, wg, lng, ws, bst, cw), [pltpu.VMEM((HALO, SC_W), F32)], comm)


def _even_bwd(x, proj, dx1, wg, gamma, lng, ws, bst, cw, seq, comm=None):
    t = x.shape[0]
    nb, nbs = t // TB, seq // TB

    def body(x_ref, proj_ref, ptail_ref, dx1_ref, gam_ref, win_ref, wout_ref, lng_ref, ws_ref, bst_ref, cw_ref,
             dx0_ref, dproj_ref, mix_ref, h_ref, dgam_ref, dws_ref, dbc_ref, dlng_ref, dcw_ref, head_ref):
        i = pl.program_id(0)
        blk = nb - 1 - i

        @pl.when(i == 0)
        def _():
            dgam_ref[...] = jnp.zeros_like(dgam_ref)
            dws_ref[...] = jnp.zeros_like(dws_ref)
            dbc_ref[...] = jnp.zeros_like(dbc_ref)
            dlng_ref[...] = jnp.zeros_like(dlng_ref)
            dcw_ref[...] = jnp.zeros_like(dcw_ref)

        xv = x_ref[...]
        gam = gam_ref[...]
        h, r = _rms(xv, gam)
        h_ref[...] = h.astype(BF16)
        dx1 = dx1_ref[...]
        dmix = _dot(dx1.astype(BF16), wout_ref[...].reshape(D, D), 1, 1)
        da, db = dmix[:, :SG_W], dmix[:, SG_W:]
        proj = proj_ref[...].astype(F32)
        u, v = proj[:, 0:SG_W], proj[:, SG_W:2 * SG_W]
        bg, cg, hv = proj[:, 1024:1536], proj[:, 1536:2048], proj[:, 2048:2560]
        gu, gv = _gelu(u), _gelu(v)

        a_heads, dgv_heads = [], []
        for hd in range(SG_HEADS):
            sl = slice(hd * SG_HD, (hd + 1) * SG_HD)
            g_h = lng_ref[:, sl]
            vn, xh, rr = _ln_head(gv[:, sl], g_h)
            vnb = vn.astype(BF16)
            wm = _tril_bf16(ws_ref[hd])
            bcol = bst_ref[:, hd:hd + 1]
            mixed_c, dvn_c = [], []
            dw_acc = jnp.zeros((128, 128), F32)
            db_acc = jnp.zeros((128, 1), F32)
            for k in range(TB // SG_CHUNK):
                rs = slice(k * SG_CHUNK, (k + 1) * SG_CHUNK)
                mixed = _dot(wm, vnb[rs], 1, 0) + bcol
                dmixed = da[rs, sl] * gu[rs, sl]
                dmb = dmixed.astype(BF16)
                dvn_c.append(_dot(wm, dmb, 0, 0))
                dw_acc = dw_acc + _dot(dmb, vnb[rs], 1, 1)
                db_acc = db_acc + jnp.sum(dmixed, axis=1, keepdims=True)
                mixed_c.append(mixed)
            mixed_h = jnp.concatenate(mixed_c, axis=0)
            dvn = jnp.concatenate(dvn_c, axis=0)
            r_i = lax.broadcasted_iota(jnp.int32, (128, 128), 0)
            c_i = lax.broadcasted_iota(jnp.int32, (128, 128), 1)
            dws_ref[hd] += jnp.where(r_i >= c_i, dw_acc, 0.0)
            dbc_ref[:, hd:hd + 1] += db_acc
            dlng_ref[:, sl] += jnp.sum(dvn * xh, axis=0, keepdims=True)
            dxh = dvn * g_h
            dgv = rr * (dxh - jnp.mean(dxh, axis=-1, keepdims=True)
                        - xh * jnp.mean(dxh * xh, axis=-1, keepdims=True))
            a_heads.append(gu[:, sl] * mixed_h)
            dproj_ref[:, sl] = (da[:, sl] * mixed_h * _gelu_grad(u[:, sl])).astype(BF16)
            dgv_heads.append(dgv * _gelu_grad(v[:, sl]))
        dproj_ref[:, SG_W:2 * SG_W] = jnp.concatenate(dgv_heads, axis=1).astype(BF16)
        mix_ref[:, :SG_W] = jnp.concatenate(a_heads, axis=1).astype(BF16)

        z = cg * hv
        pt = ptail_ref[...].astype(F32)
        tail = jnp.where(blk % nbs == 0, 0.0, pt[:, 1536:2048] * pt[:, 2048:2560])
        y, zs1, zs2 = _conv_fwd(z, tail, cw_ref)
        mix_ref[:, SG_W:] = (bg * y).astype(BF16)
        dy = db * bg
        head = jnp.where(blk % nbs == nbs - 1, 0.0, head_ref[...])
        ext = jnp.concatenate([dy, head], axis=0)
        dz = (cw_ref[2:3, :] * dy + cw_ref[1:2, :] * _shift_up(ext, 1)[:TB]
              + cw_ref[0:1, :] * _shift_up(ext, 2)[:TB])
        head_ref[...] = dy[:HALO, :]
        dcw_ref[2:3, :] += jnp.sum(dy * z, axis=0, keepdims=True)
        dcw_ref[1:2, :] += jnp.sum(dy * zs1, axis=0, keepdims=True)
        dcw_ref[0:1, :] += jnp.sum(dy * zs2, axis=0, keepdims=True)
        dproj_ref[:, 1024:1536] = (db * y).astype(BF16)
        dproj_ref[:, 1536:2048] = (dz * hv).astype(BF16)
        dproj_ref[:, 2048:2560] = (dz * cg).astype(BF16)

        dh = _dot(dproj_ref[...], win_ref[...].reshape(EVEN_IN, D), 1, 0)
        dxn, dgam = _rms_bwd(xv, r, gam, dh)
        dgam_ref[...] += dgam
        dx0_ref[...] = dx1 + dxn

    def rev(w):
        return pl.BlockSpec((TB, w), lambda i: (nb - 1 - i, 0))

    ptail = pl.BlockSpec((HALO, EVEN_IN), lambda i: (jnp.maximum((nb - 1 - i) * (TB // HALO) - 1, 0), 0))
    return _call(
        body, "even_bwd", (nb,),
        [rev(D), rev(EVEN_IN), ptail, rev(D), _const_spec((1, D)), _wspec(N_EIN, OFF_EIN),
         _wspec(N_SQ, OFF_EOUT), _const_spec((1, SG_W)), _const_spec((SG_HEADS, 128, 128)),
         _const_spec((128, 128)), _const_spec((8, SC_W))],
        [rev(D), rev(EVEN_IN), rev(D), rev(D), _const_spec((1, D)), _const_spec((SG_HEADS, 128, 128)),
         _const_spec((128, 128)), _const_spec((1, SG_W)), _const_spec((8, SC_W))],
        [jax.ShapeDtypeStruct((t, D), F32), jax.ShapeDtypeStruct((t, EVEN_IN), BF16),
         jax.ShapeDtypeStruct((t, D), BF16), jax.ShapeDtypeStruct((t, D), BF16),
         jax.ShapeDtypeStruct((1, D), F32), jax.ShapeDtypeStruct((SG_HEADS, 128, 128), F32),
         jax.ShapeDtypeStruct((128, 128), F32), jax.ShapeDtypeStruct((1, SG_W), F32),
         jax.ShapeDtypeStruct((8, SC_W), F32)],
        (x, proj, proj, dx1, gamma, wg, wg, lng, ws, bst, cw), [pltpu.VMEM((HALO, SC_W), F32)], comm)


def _last_block_fwd(x, c_out, d_out, w_mix1, w_gu, w_d, gamma, target):
    t = x.shape[0]

    def body(x_ref, c_ref, d_ref, wo_ref, gam_ref, wg_ref, wu_ref, wd_ref, t_ref,
             x3_ref, dy_ref, g_ref, u_ref, loss_ref):
        @pl.when(pl.program_id(0) == 0)
        def _():
            loss_ref[...] = jnp.zeros_like(loss_ref)

        xv = (x_ref[...] + _dot(c_ref[...], wo_ref[0:2].reshape(POOL_W, D), 1, 0)
              + _dot(d_ref[...], wo_ref[2:8].reshape(HEADS * V_DIM, D), 1, 0))
        x3_ref[...] = xv
        h, _ = _rms(xv, gam_ref[...])
        hb = h.astype(BF16)
        g = _dot(hb, wg_ref[...].reshape(D_FF, D), 1, 1)
        u = _dot(hb, wu_ref[...].reshape(D_FF, D), 1, 1)
        g_ref[...] = g.astype(BF16)
        u_ref[...] = u.astype(BF16)
        act = g * jax.nn.sigmoid(g) * u
        err = xv + _dot(act.astype(BF16), wd_ref[...].reshape(D_FF, D), 1, 0) - t_ref[...]
        dy_ref[...] = err * (1.0 / D)
        sq = jnp.sum(jnp.sum(err * err, axis=-1, keepdims=True), axis=0, keepdims=True)
        loss_ref[...] += (0.5 / D) * sq

    def row(w):
        return pl.BlockSpec((TB, w), lambda i: (i, 0))

    res, _ = _call(
        body, "last_block_fwd", (t // TB,),
        [row(D), row(POOL_W), row(HEADS * V_DIM), _wspec(N_SQ, OFF_OOUT), _const_spec((1, D)),
         _wspec(N_FF, OFF_GATE), _wspec(N_FF, OFF_UP), _wspec(N_FF, 0), row(D)],
        [row(D), row(D), row(D_FF), row(D_FF), _const_spec((8, 128))],
        [jax.ShapeDtypeStruct((t, D), F32), jax.ShapeDtypeStruct((t, D), F32), jax.ShapeDtypeStruct((t, D_FF), BF16),
         jax.ShapeDtypeStruct((t, D_FF), BF16), jax.ShapeDtypeStruct((8, 128), F32)],
        (x, c_out, d_out, w_mix1, gamma, w_gu, w_gu, w_d, target))
    return res


def _ffn_up(x, w_gu, gamma, name, comm=None):
    t = x.shape[0]

    def body(x_ref, gam_ref, wg_ref, wu_ref, g_ref, u_ref, act_ref):
        h, _ = _rms(x_ref[...], gam_ref[...])
        hb = h.astype(BF16)
        g = _dot(hb, wg_ref[...].reshape(D_FF, D), 1, 1)
        u = _dot(hb, wu_ref[...].reshape(D_FF, D), 1, 1)
        g_ref[...] = g.astype(BF16)
        u_ref[...] = u.astype(BF16)
        act_ref[...] = (g * jax.nn.sigmoid(g) * u).astype(BF16)

    row = pl.BlockSpec((TB, D), lambda i: (i, 0))
    wide = pl.BlockSpec((TB, D_FF), lambda i: (i, 0))
    return _call(body, name, (t // TB,), [row, _const_spec((1, D)), _wspec(N_FF, OFF_GATE), _wspec(N_FF, OFF_UP)],
                 [wide, wide, wide], [jax.ShapeDtypeStruct((t, D_FF), BF16)] * 3, (x, gamma, w_gu, w_gu), (), comm)


def _ffn_down(x, act, w_d, name, comm=None):
    t = x.shape[0]

    def body(x_ref, a_ref, wd_ref, y_ref):
        y_ref[...] = x_ref[...] + _dot(a_ref[...], wd_ref[...].reshape(D_FF, D), 1, 0)

    row = pl.BlockSpec((TB, D), lambda i: (i, 0))
    wide = pl.BlockSpec((TB, D_FF), lambda i: (i, 0))
    return _call(body, name, (t // TB,), [row, wide, _wspec(N_FF, 0)], [row], [jax.ShapeDtypeStruct((t, D), F32)],
                 (x, act, w_d), (), comm)


def _ffn_bwd(x, g, u, dy, w_gu, w_d, gamma, name, comm=None, w_mix1=None):
    t = x.shape[0]
    with_dmix = w_mix1 is not None

    def body(*refs):
        x_ref, g_ref, u_ref, dy_ref, gam_ref, wg_ref, wu_ref, wd_ref = refs[:8]
        dx_ref, act_ref, dg_ref, du_ref, h_ref, dgam_ref = refs[8 + with_dmix:14 + with_dmix]

        @pl.when(pl.program_id(0) == 0)
        def _():
            dgam_ref[...] = jnp.zeros_like(dgam_ref)

        xv = x_ref[...]
        gam = gam_ref[...]
        h, r = _rms(xv, gam)
        h_ref[...] = h.astype(BF16)
        dyv = dy_ref[...]
        dact = _dot(dyv.astype(BF16), wd_ref[...].reshape(D_FF, D), 1, 1)
        gv = g_ref[...].astype(F32)
        uv = u_ref[...].astype(F32)
        sg = jax.nn.sigmoid(gv)
        silu = gv * sg
        act_ref[...] = (silu * uv).astype(BF16)
        dgb = (dact * uv * (sg * (1.0 + gv * (1.0 - sg)))).astype(BF16)
        dub = (dact * silu).astype(BF16)
        dg_ref[...] = dgb
        du_ref[...] = dub
        dh = _dot(dgb, wg_ref[...].reshape(D_FF, D), 1, 0) + _dot(dub, wu_ref[...].reshape(D_FF, D), 1, 0)
        dxn, dgam = _rms_bwd(xv, r, gam, dh)
        dgam_ref[...] += dgam
        dx = dyv + dxn
        dx_ref[...] = dx
        if with_dmix:
            refs[15][...] = _dot(dx.astype(BF16), refs[8][...].reshape(D, D), 1, 1).astype(BF16)

    row = pl.BlockSpec((TB_FFN_BWD, D), lambda i: (i, 0))
    wide = pl.BlockSpec((TB_FFN_BWD, D_FF), lambda i: (i, 0))
    in_specs = [row, wide, wide, row, _const_spec((1, D)), _wspec(N_FF, OFF_GATE), _wspec(N_FF, OFF_UP),
                _wspec(N_FF, 0)]
    out_specs = [row, wide, wide, wide, row, _const_spec((1, D))]
    out_shape = [jax.ShapeDtypeStruct((t, D), F32), jax.ShapeDtypeStruct((t, D_FF), BF16),
                 jax.ShapeDtypeStruct((t, D_FF), BF16), jax.ShapeDtypeStruct((t, D_FF), BF16),
                 jax.ShapeDtypeStruct((t, D), BF16), jax.ShapeDtypeStruct((1, D), F32)]
    args = (x, g, u, dy, gamma, w_gu, w_gu, w_d)
    if with_dmix:
        in_specs, args = in_specs + [_wspec(N_SQ, OFF_OOUT)], args + (w_mix1,)
        out_specs, out_shape = out_specs + [row], out_shape + [jax.ShapeDtypeStruct((t, D), BF16)]
    return _call(body, name, (t // TB_FFN_BWD,), in_specs, out_specs, out_shape, args, (), comm)


def _odd_pre_fwd(x, wg, gamma, qbt, kvbt, qa_g, kva_g, pw_bd, pscale, seq, comm=None):
    t = x.shape[0]
    nbs = seq // TB

    def body(x_ref, gam_ref, win_ref, qb_ref, kvb_ref, qa_ref, kva_ref, pw_ref, ps_ref,
             proj_ref, q_ref, kv_ref, kr_ref, c_ref, tail_ref):
        i = pl.program_id(0)
        h, _ = _rms(x_ref[...], gam_ref[...])
        proj = _dot(h.astype(BF16), win_ref[...].reshape(D, D), 1, 0)
        proj_ref[...] = proj.astype(BF16)
        zp, ql, kvl = proj[:, :POOL_W], proj[:, 256:640], proj[:, 640:896]
        kr_ref[...] = proj[:, 896:1024]
        qn, _ = _rms(ql, qa_ref[...])
        q_ref[...] = _dot(qn.astype(BF16), qb_ref[...], 1, 1).astype(BF16)
        kvn, _ = _rms(kvl, kva_ref[...])
        kv_ref[...] = _dot(kvn.astype(BF16), kvb_ref[...], 1, 1).astype(BF16)
        tail = jnp.where(i % nbs == 0, 0.0, tail_ref[...])
        pooled, _, _ = _pool_fwd(zp, tail, i % nbs)
        tail_ref[...] = zp[TB - HALO:, :]
        c_ref[...] = (_dot(pooled.astype(BF16), pw_ref[...], 1, 0) * ps_ref[...]).astype(BF16)

    def row(w):
        return pl.BlockSpec((TB, w), lambda i: (i, 0))

    return _call(
        body, "odd_pre_fwd", (t // TB,),
        [row(D), _const_spec((1, D)), _wspec(N_SQ, OFF_OIN), _const_spec((HEADS * HP, Q_LORA)),
         _const_spec((HEADS * HP, KV_LORA)), _const_spec((1, Q_LORA)), _const_spec((1, KV_LORA)),
         _const_spec((POOL_W, POOL_W)), _const_spec((1, POOL_W))],
        [row(D), row(HEADS * HP), row(HEADS * HP), row(128), row(POOL_W)],
        [jax.ShapeDtypeStruct((t, D), BF16), jax.ShapeDtypeStruct((t, HEADS * HP), BF16),
         jax.ShapeDtypeStruct((t, HEADS * HP), BF16), jax.ShapeDtypeStruct((t, 128), F32),
         jax.ShapeDtypeStruct((t, POOL_W), BF16)],
        (x, gamma, wg, qbt, kvbt, qa_g, kva_g, pw_bd, pscale), [pltpu.VMEM((HALO, POOL_W), F32)], comm)


def _odd_pre_bwd(x, proj, dx3, dmix, dq, dkv, dkr, wg, gamma, qbt, kvbt, qa_g, kva_g, pw_bd, pscale, seq):
    t = x.shape[0]
    nb, nbs = t // TB, seq // TB

    def body(x_ref, proj_ref, ptail_ref, dx3_ref, dco_ref, dq_ref, dkv_ref, dkr_ref, gam_ref, win_ref, qb_ref,
             kvb_ref, qa_ref, kva_ref, pw_ref, ps_ref,
             dx2_ref, dproj_ref, h_ref, qn_ref, kvn_ref, dgam_ref, dqa_ref, dkva_ref, dpw_ref, dps_ref, head_ref):
        i = pl.program_id(0)
        blk = nb - 1 - i

        @pl.when(i == 0)
        def _():
            dgam_ref[...] = jnp.zeros_like(dgam_ref)
            dqa_ref[...] = jnp.zeros_like(dqa_ref)
            dkva_ref[...] = jnp.zeros_like(dkva_ref)
            dpw_ref[...] = jnp.zeros_like(dpw_ref)
            dps_ref[...] = jnp.zeros_like(dps_ref)

        xv = x_ref[...]
        gam = gam_ref[...]
        h, r = _rms(xv, gam)
        h_ref[...] = h.astype(BF16)
        proj = proj_ref[...].astype(F32)
        zp, ql, kvl = proj[:, :POOL_W], proj[:, 256:640], proj[:, 640:896]

        qa = qa_ref[...]
        qn, rq = _rms(ql, qa)
        qn_ref[...] = qn.astype(BF16)
        dql, dqa = _rms_bwd(ql, rq, qa, _dot(dq_ref[...], qb_ref[...], 1, 0))
        dqa_ref[...] += dqa
        kva = kva_ref[...]
        kvn, rkv = _rms(kvl, kva)
        kvn_ref[...] = kvn.astype(BF16)
        dkvl, dkva = _rms_bwd(kvl, rkv, kva, _dot(dkv_ref[...], kvb_ref[...], 1, 0))
        dkva_ref[...] += dkva

        pt = ptail_ref[...].astype(F32)
        tail = jnp.where(blk % nbs == 0, 0.0, pt[:, :POOL_W])
        pooled, cnt, grp = _pool_fwd(zp, tail, blk % nbs)
        pb = pooled.astype(BF16)
        pw = pw_ref[...]
        dco = dco_ref[...].astype(F32)
        dps_ref[...] += jnp.sum(dco * _dot(pb, pw, 1, 0), axis=0, keepdims=True)
        dpo = (dco * ps_ref[...]).astype(BF16)
        dpw_ref[...] += _dot(pb, dpo, 0, 0)
        dpooled = _dot(dpo, pw, 1, 1)
        dpm = dpooled / cnt
        head = jnp.where(blk % nbs == nbs - 1, 0.0, head_ref[...])
        dz = _pool_bwd(dpooled, dpm, head, grp)
        head_ref[...] = dpm[:HALO, :]

        dproj_ref[:, :POOL_W] = dz.astype(BF16)
        dproj_ref[:, 256:640] = dql.astype(BF16)
        dproj_ref[:, 640:896] = dkvl.astype(BF16)
        dproj_ref[:, 896:1024] = dkr_ref[...].astype(BF16)
        dh = _dot(dproj_ref[...], win_ref[...].reshape(D, D), 1, 1)
        dxn, dgam = _rms_bwd(xv, r, gam, dh)
        dgam_ref[...] += dgam
        dx2_ref[...] = dx3_ref[...] + dxn

    def rev(w):
        return pl.BlockSpec((TB, w), lambda i: (nb - 1 - i, 0))

    ptail = pl.BlockSpec((HALO, D), lambda i: (jnp.maximum((nb - 1 - i) * (TB // HALO) - 1, 0), 0))
    return pl.pallas_call(
        body, name="odd_pre_bwd",
        out_shape=[jax.ShapeDtypeStruct((t, D), F32), jax.ShapeDtypeStruct((t, D), BF16),
                   jax.ShapeDtypeStruct((t, D), BF16), jax.ShapeDtypeStruct((t, Q_LORA), BF16),
                   jax.ShapeDtypeStruct((t, KV_LORA), BF16), jax.ShapeDtypeStruct((1, D), F32),
                   jax.ShapeDtypeStruct((1, Q_LORA), F32), jax.ShapeDtypeStruct((1, KV_LORA), F32),
                   jax.ShapeDtypeStruct((POOL_W, POOL_W), F32), jax.ShapeDtypeStruct((1, POOL_W), F32)],
        grid=(nb,),
        in_specs=[rev(D), rev(D), ptail, rev(D), rev(POOL_W), rev(HEADS * HP), rev(HEADS * HP), rev(128),
                  _const_spec((1, D)), _wspec(N_SQ, OFF_OIN), _const_spec((HEADS * HP, Q_LORA)),
                  _const_spec((HEADS * HP, KV_LORA)), _const_spec((1, Q_LORA)), _const_spec((1, KV_LORA)),
                  _const_spec((POOL_W, POOL_W)), _const_spec((1, POOL_W))],
        out_specs=[rev(D), rev(D), rev(D), rev(Q_LORA), rev(KV_LORA), _const_spec((1, D)), _const_spec((1, Q_LORA)),
                   _const_spec((1, KV_LORA)), _const_spec((POOL_W, POOL_W)), _const_spec((1, POOL_W))],
        scratch_shapes=[pltpu.VMEM((HALO, POOL_W), F32)],
        compiler_params=_cparams(1),
    )(x, proj, proj, dx3, dmix, dq, dkv, dkr, gamma, wg, qbt, kvbt, qa_g, kva_g, pw_bd, pscale)


def _attn_specs(seq):
    head = pl.BlockSpec((seq, HP), lambda b, h: (b, h))
    shared = pl.BlockSpec((seq, 128), lambda b, h: (b, 0))
    gain = pl.BlockSpec((1, HP), lambda b, h: (0, 0))
    return head, shared, gain


def _causal_bias(n):
    rows = lax.broadcasted_iota(jnp.int32, (n, n), 0)
    cols = lax.broadcasted_iota(jnp.int32, (n, n), 1)
    return jnp.where(cols <= rows, 0.0, NEG_INF)


def _attn_fwd(q, kv, kr, cos, sin, gq, gk, seq, comm=None):
    t = q.shape[0]
    qb = min(512, seq)

    def body(q_ref, kv_ref, kr_ref, c_ref, s_ref, gq_ref, gk_ref, o_ref, lse_ref):
        c, s = c_ref[...], s_ref[...]
        qf, _ = _qk_prep(q_ref[...].astype(F32), gq_ref[...], c, s)
        kin = jnp.concatenate([kv_ref[:, :128].astype(F32), kr_ref[...]], axis=1)
        kf, _ = _qk_prep(kin, gk_ref[...], c, s)
        qf, kf = qf.astype(BF16), kf.astype(BF16)
        v1 = jnp.concatenate([kv_ref[:, 128:], jnp.ones((seq, V_DIM), BF16)], axis=1)
        bias = _causal_bias(qb)
        for q0 in range(0, seq, qb):
            q1 = q0 + qb
            qblk = qf[q0:q1]
            s_dg = _dot(qblk, kf[q0:q1], 1, 1) + bias
            m = jnp.max(s_dg, axis=-1, keepdims=True)
            if q0:
                s_off = _dot(qblk, kf[:q0], 1, 1)
                m = jnp.maximum(m, jnp.max(s_off, axis=-1, keepdims=True))
            acc = _dot(jnp.exp(s_dg - m).astype(BF16), v1[q0:q1], 1, 0)
            if q0:
                acc = acc + _dot(jnp.exp(s_off - m).astype(BF16), v1[:q0], 1, 0)
            l = acc[:, V_DIM:]
            o_ref[q0:q1, :] = (acc[:, :V_DIM] / l).astype(BF16)
            lse_ref[q0:q1, :] = m + jnp.log(l)

    head, shared, gain = _attn_specs(seq)
    per_head = pl.BlockSpec((seq, V_DIM), lambda b, h: (b, h))
    return _call(
        body, "attn_fwd", (t // seq, HEADS),
        [head, head, shared, shared, shared, gain, gain], [per_head, per_head],
        [jax.ShapeDtypeStruct((t, HEADS * V_DIM), BF16), jax.ShapeDtypeStruct((t, HEADS * V_DIM), F32)],
        (q, kv, kr, cos, sin, gq, gk), (), comm)


def _attn_bwd(q, kv, kr, cos, sin, gq, gk, dmix, d_out, lse, seq, comm=None):
    t = q.shape[0]
    qb = min(512, seq)

    def body(q_ref, kv_ref, kr_ref, c_ref, s_ref, gq_ref, gk_ref, do_ref, o_ref, lse_ref,
             dq_ref, dkv_ref, dkr_ref, dgq_ref, dgk_ref, dqf_ref, dkf_ref, dv_ref):
        b, hd = pl.program_id(0), pl.program_id(1)

        @pl.when((b == 0) & (hd == 0))
        def _():
            dgq_ref[...] = jnp.zeros_like(dgq_ref)
            dgk_ref[...] = jnp.zeros_like(dgk_ref)

        c, sn = c_ref[...], s_ref[...]
        gq_v, gk_v = gq_ref[...], gk_ref[...]
        qin = q_ref[...].astype(F32)
        kin = jnp.concatenate([kv_ref[:, :128].astype(F32), kr_ref[...]], axis=1)
        qf32, rq = _qk_prep(qin, gq_v, c, sn)
        kf32, rk = _qk_prep(kin, gk_v, c, sn)
        qf, kf = qf32.astype(BF16), kf32.astype(BF16)
        vb = kv_ref[:, 128:]
        dkf_ref[...] = jnp.zeros_like(dkf_ref)
        dv_ref[...] = jnp.zeros_like(dv_ref)
        bias = _causal_bias(qb)
        for q0 in range(0, seq, qb):
            q1 = q0 + qb
            qblk = qf[q0:q1]
            do = do_ref[q0:q1, :]
            lse_col = lse_ref[q0:q1, 0:1]
            d_col = jnp.sum(do.astype(F32) * o_ref[q0:q1, :].astype(F32), axis=-1, keepdims=True)
            dq_acc = None
            for k0, k1, diag in ((q0, q1, True), (0, q0, False)):
                if k1 == k0:
                    continue
                s = _dot(qblk, kf[k0:k1], 1, 1)
                p = jnp.exp((s + bias if diag else s) - lse_col)
                dv_ref[k0:k1, :] += _dot(p.astype(BF16), do, 0, 0)
                ds = (p * (_dot(do, vb[k0:k1], 1, 1) - d_col)).astype(BF16)
                part = _dot(ds, kf[k0:k1], 1, 0)
                dq_acc = part if dq_acc is None else dq_acc + part
                dkf_ref[k0:k1, :] += _dot(ds, qblk, 0, 0)
            dqf_ref[q0:q1, :] = dq_acc
        dqin, dgq = _qk_prep_bwd(dqf_ref[...], qin, rq, gq_v, c, sn)
        dkin, dgk = _qk_prep_bwd(dkf_ref[...], kin, rk, gk_v, c, sn)
        dgq_ref[...] += dgq
        dgk_ref[...] += dgk
        dq_ref[...] = dqin.astype(BF16)
        dkv_ref[:, :128] = dkin[:, :128].astype(BF16)
        dkv_ref[:, 128:] = dv_ref[...].astype(BF16)

        @pl.when(hd == 0)
        def _():
            dkr_ref[...] = dkin[:, 128:]

        @pl.when(hd != 0)
        def _():
            dkr_ref[...] += dkin[:, 128:]

    head, shared, gain = _attn_specs(seq)
    per_head = pl.BlockSpec((seq, V_DIM), lambda b, h: (b, h))
    return _call(
        body, "attn_bwd", (t // seq, HEADS),
        [head, head, shared, shared, shared, gain, gain,
         pl.BlockSpec((seq, V_DIM), lambda b, h: (b, 2 + h)), per_head, per_head],
        [head, head, shared, gain, gain],
        [jax.ShapeDtypeStruct((t, HEADS * HP), BF16), jax.ShapeDtypeStruct((t, HEADS * HP), BF16),
         jax.ShapeDtypeStruct((t, 128), F32), jax.ShapeDtypeStruct((1, HP), F32),
         jax.ShapeDtypeStruct((1, HP), F32)],
        (q, kv, kr, cos, sin, gq, gk, dmix, d_out, lse),
        [pltpu.VMEM((seq, HP), F32), pltpu.VMEM((seq, HP), F32), pltpu.VMEM((seq, V_DIM), F32)], comm)


def _tn(a_list, b, tm, name, into=None, comm=None, after=None):
    t, n_out = b.shape
    widths = [a.shape[1] for a in a_list]
    tk = min(TK_DW, t)
    m, na, nk = sum(widths), len(a_list), t // tk
    assert na == 1 or tm == m

    def body(*refs):
        a_refs, b_ref, o_ref, acc_ref = refs[:na], refs[na], refs[-2], refs[-1]
        k = pl.program_id(1)

        @pl.when(k == 0)
        def _():
            acc_ref[...] = jnp.zeros_like(acc_ref)

        bb = b_ref[...].astype(BF16)
        m0 = 0
        for a_ref, w in zip(a_refs, widths):
            rows = slice(0, tm) if na == 1 else slice(m0, m0 + w)
            acc_ref[rows, :] += _dot(a_ref[...].astype(BF16), bb, 0, 0)
            m0 += w

        @pl.when(k == nk - 1)
        def _():
            o_ref[...] = acc_ref[...].astype(BF16).reshape(o_ref.shape)

    if na == 1:
        in_specs = [pl.BlockSpec((tk, tm), lambda i, k: (k, i))]
    else:
        in_specs = [pl.BlockSpec((tk, w), lambda i, k: (k, 0)) for w in widths]
    in_specs.append(pl.BlockSpec((tk, n_out), lambda i, k: (k, 0)))
    args = list(a_list) + [b]
    if into is None:
        out_spec = pl.BlockSpec((tm, n_out), lambda i, k: (i, 0))
        out_shape = jax.ShapeDtypeStruct((m, n_out), BF16)
        aliases = {}
    else:
        buf, n, off = into
        assert n_out == D and tm % n == 0 and off % n == 0 and (na == 1 or tm // n == N_DEV)
        idx = off // n
        out_spec = pl.BlockSpec((tm // n, n, D), lambda i, k: (i, idx, 0))
        out_shape = jax.ShapeDtypeStruct(buf.shape, BF16)
        in_specs.append(pl.BlockSpec(memory_space=pl.ANY))
        args.append(buf)
        aliases = {len(args) - 1: 0}
    if after is not None:
        in_specs.append(pl.BlockSpec(memory_space=pl.ANY))
        args.append(after)
    (res,), extra = _call(body, name, (m // tm, nk), in_specs, [out_spec], [out_shape], args,
                          [pltpu.VMEM((tm, n_out), F32)], comm, aliases)
    return (res, extra) if comm is not None else res


def _adamw(ws, gs, ms, vs, name, nblk=1):
    n = len(ws)
    c1 = 1.0 - B1 ** STEP
    c2 = 1.0 - B2 ** STEP

    def body(*refs):
        for a in range(n):
            w, g, m, v = (refs[k * n + a][...] for k in range(4))
            d_ref, m_ref, v_ref = (refs[(4 + k) * n + a] for k in range(3))
            m_new = B1 * m + (1.0 - B1) * g
            v_new = B2 * v + (1.0 - B2) * (g * g)
            d_ref[...] = -LR * ((m_new / c1) / (jnp.sqrt(v_new / c2) + ADAM_EPS) + WD * w)
            m_ref[...] = m_new
            v_ref[...] = v_new

    grid = (nblk,)
    assert all(w.shape[0] % nblk == 0 and (nblk == 1 or (w.shape[0] // nblk) % 8 == 0) for w in ws)
    specs = [pl.BlockSpec((w.shape[0] // nblk, w.shape[1]), lambda i: (i, 0)) for w in ws]
    outs, _ = _call(body, name, grid, specs * 4, specs * 3, [jax.ShapeDtypeStruct(w.shape, F32) for w in ws] * 3,
                    (*ws, *gs, *ms, *vs))
    return outs[:n], outs[n:2 * n], outs[2 * n:]


def _rows1024(a, rows):
    flat = a.reshape(-1, D)
    return jnp.pad(flat, ((0, rows - flat.shape[0]), (0, 0)))


def _pack_shards(even_w_in, even_w_out, odd_w_in, q_b, kv_b, odd_w_out, ffn_w_gate, ffn_w_up, ffn_w_down):
    mix0 = jnp.concatenate([even_w_in[0].T, jnp.zeros((OFF_EOUT - N_EIN, D), F32), even_w_out[0]], axis=0)
    gu = [jnp.concatenate([ffn_w_gate[layer].T, ffn_w_up[layer].T], axis=0) for layer in range(2)]
    mix1 = jnp.concatenate([jnp.pad(odd_w_in[0], ((0, 0), (0, D - ODD_IN))), odd_w_out[0],
                            _rows1024(q_b[0].T, N_QB), _rows1024(kv_b[0].T, N_KVB),
                            jnp.zeros((R_MIX1 - OFF_KVB - N_KVB, D), F32)], axis=0)
    return [c.astype(BF16) for c in (mix0, gu[0], ffn_w_down[0], mix1, gu[1], ffn_w_down[1])]


def _pad_heads(a):
    k = a.shape[1]
    return jnp.pad(a.reshape(HEADS, QK_DIM, k), ((0, 0), (0, HP - QK_DIM), (0, 0))).reshape(HEADS * HP, k)


def _small_pack(parts):
    flat = []
    for p in parts:
        v = p.reshape(-1)
        flat.append(jnp.pad(v, (0, (-v.shape[0]) % 1024)))
    return jnp.concatenate(flat).reshape(-1, 128)


def _small_unpack(buf, shapes):
    flat = buf.reshape(-1)
    out, off = [], 0
    for s in shapes:
        size = int(np.prod(s))
        out.append(flat[off:off + size].reshape(s))
        off += size + (-size) % 1024
    return out


def _step(x3d, positions, target3d, chunks, tile, where, mix_norm, ffn_norm, sg_ln_g, sg_w_s, sg_b_s,
          pool_w, q_norm, k_norm):
    bsz, seq, _ = x3d.shape
    t = bsz * seq
    x0 = x3d.reshape(t, D)
    target = target3d.reshape(t, D)
    my_mix0, my_gu0, my_d0, my_mix1, my_gu1, my_d1 = chunks

    lane = np.arange(128)
    inv_freq = np.where(lane < QK_ROPE, ROPE_THETA ** (-(2.0 * (lane % 32)) / QK_ROPE), 0.0)
    inv_freq = jnp.asarray(inv_freq.reshape(1, 128), F32)
    (cos, sin), (w_mix0, tiles) = _rope_tables(positions.reshape(t, 1), inv_freq, _gather_comm([my_mix0, tile]))

    conv_w = tiles[:, 0:3, 0:64].transpose(1, 0, 2).reshape(3, SC_W)
    pool_scale = tiles[:, 3, 0:32].reshape(1, POOL_W)
    q_a_norm = tiles[:, 4, 0:48].reshape(1, Q_LORA)
    kv_a_norm = tiles[:, 5, 0:32].reshape(1, KV_LORA)
    ws = sg_w_s[0]
    bst = jnp.pad(sg_b_s[0].T, ((0, 0), (0, 128 - SG_HEADS)))
    cw = jnp.pad(conv_w, ((0, 8 - 3), (0, 0)))
    pw_bd = jax.scipy.linalg.block_diag(*[pool_w[0, g] for g in range(4)]).astype(BF16)
    gq = jnp.pad(q_norm * ATT_SCALE, ((0, 0), (0, HP - QK_DIM)))
    gk = jnp.pad(k_norm, ((0, 0), (0, HP - QK_DIM)))

    (x1, proj_e), (w_gu0,) = _even_fwd(x0, w_mix0, mix_norm[0:1], sg_ln_g, ws, bst, cw, seq, _gather_comm([my_gu0]))
    (g0, u0, act0), (w_d0, w_mix1) = _ffn_up(x1, w_gu0, ffn_norm[0:1], "ffn_up0", _gather_comm([my_d0, my_mix1]))
    (x2,), (w_d1,) = _ffn_down(x1, act0, w_d0, "ffn_down0", _gather_comm([my_d1]))
    qbt = _pad_heads(w_mix1[:, OFF_QB:OFF_QB + N_QB_USED, :].reshape(HEADS * QK_DIM, Q_LORA))
    kvbt = w_mix1[:, OFF_KVB:OFF_KVB + N_KVB, :].reshape(HEADS * HP, KV_LORA)
    (proj_o, q, kv, kr, c_out), _ = _odd_pre_fwd(x2, w_mix1, mix_norm[1:2], qbt, kvbt, q_a_norm, kv_a_norm,
                                                pw_bd, pool_scale, seq)
    (d_out, lse), (w_gu1,) = _attn_fwd(q, kv, kr, cos, sin, gq, gk, seq, _gather_comm([my_gu1]))
    x3, dy, g1, u1, loss_tile = _last_block_fwd(x2, c_out, d_out, w_mix1, w_gu1, w_d1, ffn_norm[1:2], target)

    def chunk(rows, padded=False):
        return jnp.zeros((N_DEV, rows, D), BF16) if padded else lax.empty((N_DEV, rows, D), BF16)

    (dx3, act1, dg1, du1, h3, dgam_f1, dmix_o), _ = _ffn_bwd(x3, g1, u1, dy, w_gu1, w_d1, ffn_norm[1:2], "ffn_bwd1",
                                                           None, w_mix1)
    gp_ffn1 = _tn([dg1], h3, 1408, "dw_gate1", (chunk(R_GU + N_FF), N_FF, OFF_GATE))
    gp_ffn1 = _tn([du1], h3, 1408, "dw_up1", (gp_ffn1, N_FF, OFF_UP))
    gp_ffn1 = _tn([act1], dy, 1408, "dw_down1", (gp_ffn1, N_FF, R_GU))

    gp_mix1, (ga_ffn1,) = _tn([c_out, d_out], dx3, D, "dw_oout", (chunk(R_MIX1, True), N_SQ, OFF_OOUT),
                              _pair_exchange_comm(gp_ffn1))
    pb_ffn1 = _rs_pair_sum(gp_ffn1, ga_ffn1, where, "rs_pair_sum_ffn1")
    (dq, dkv, dkr, dgq, dgk), (gb_ffn1,) = _attn_bwd(q, kv, kr, cos, sin, gq, gk, dmix_o, d_out, lse, seq,
                                                    _chip_exchange_comm(pb_ffn1))
    (dx2, dproj_o, h2, qn, kvn, dgam_m1, dqa, dkva, dpw_bd, dps) = _odd_pre_bwd(
        x2, proj_o, dx3, dmix_o, dq, dkv, dkr, w_mix1, mix_norm[1:2], qbt, kvbt, q_a_norm, kv_a_norm, pw_bd,
        pool_scale, seq)
    gp_mix1 = _tn([h2], dproj_o, D, "dw_oin", (gp_mix1, N_SQ, OFF_OIN))
    d_qbt = _tn([dq], qn, HEADS * HP, "dw_qb")
    d_qb_rows = d_qbt.reshape(HEADS, HP, Q_LORA)[:, :QK_DIM].reshape(N_DEV, N_QB_USED, D)
    d_kvb_rows = _tn([dkv], kvn, HEADS * HP, "dw_kvb").reshape(N_DEV, N_KVB, D)
    gp_mix1 = lax.dynamic_update_slice(gp_mix1, d_qb_rows, (0, OFF_QB, 0))
    gp_mix1 = lax.dynamic_update_slice(gp_mix1, d_kvb_rows, (0, OFF_KVB, 0))

    (dx1, act0, dg0, du0, h1, dgam_f0), (ga_mix1,) = _ffn_bwd(x1, g0, u0, dx2, w_gu0, w_d0, ffn_norm[0:1], "ffn_bwd0",
                                                             _pair_exchange_comm(gp_mix1))
    pb_mix1 = _rs_pair_sum(gp_mix1, ga_mix1, where, "rs_pair_sum_mix1")
    *open_mix1, started = _chip_exchange_start(pb_mix1, "mix1")
    gp_ffn0a = _tn([dg0], h1, 1408, "dw_gate0", (chunk(R_GU), N_FF, OFF_GATE), None, started)
    gp_ffn0a = _tn([du0], h1, 1408, "dw_up0", (gp_ffn0a, N_FF, OFF_UP))
    gp_ffn0b, (ga_ffn0a,) = _tn([act0], dx2, 1408, "dw_down0", (chunk(N_FF), N_FF, 0),
                                _pair_exchange_comm(gp_ffn0a))
    pb_ffn0a = _rs_pair_sum(gp_ffn0a, ga_ffn0a, where, "rs_pair_sum_ffn0a")
    *open_ffn0a, started = _chip_exchange_start(pb_ffn0a, "ffn0a")

    (dx0, dproj_e, mix_e, h0, dgam_m0, dws, dbc, dlng, dcw), _ = _even_bwd(
        x0, proj_e, dx1, w_mix0, mix_norm[0:1], sg_ln_g, ws, bst, cw + 0.0 * started[:, :1], seq)

    small = _small_pack([
        jnp.concatenate([dgam_m0, dgam_m1], 0), jnp.concatenate([dgam_f0, dgam_f1], 0), dlng,
        dws[None], dbc[:, :SG_HEADS].T[None], dcw[:3],
        jnp.stack([dpw_bd[g * POOL_GD:(g + 1) * POOL_GD, g * POOL_GD:(g + 1) * POOL_GD] for g in range(4)])[None],
        dps, dqa, dkva, dgq[:, :QK_DIM] * ATT_SCALE, dgk[:, :QK_DIM], loss_tile[0:1, 0:1]])
    gp_ein, (small_all, ga_ffn0b) = _tn([dproj_e], h0, 1280, "dw_ein", (chunk(N_EIN), N_EIN, 0),
                                        _both(_gather_comm([small]), _pair_exchange_comm(gp_ffn0b)))
    pb_ffn0b = _rs_pair_sum(gp_ffn0b, ga_ffn0b, where, "rs_pair_sum_ffn0b")
    *open_ffn0b, started = _chip_exchange_start(pb_ffn0b, "ffn0b")
    gp_eout, (ga_ein,) = _tn([mix_e], dx1, D, "dw_eout", (chunk(N_SQ), N_SQ, 0), _pair_exchange_comm(gp_ein),
                             started)
    pb_ein = _rs_pair_sum(gp_ein, ga_ein, where, "rs_pair_sum_ein")
    *open_ein, started = _chip_exchange_start(pb_ein, "ein")
    small_sum = _small_unpack(_sum_gathered(small_all), SMALL_SHAPES)
    in_flight = (open_ffn0a, open_ffn0b, open_mix1, open_ein)
    return dx0.reshape(bsz, seq, D), (pb_ffn1, gb_ffn1), in_flight, (gp_eout, started), small_sum


SMALL_SHAPES = [(2, D), (2, D), (1, SG_W), (1, SG_HEADS, 128, 128), (1, SG_HEADS, 128), (3, SC_W),
                (1, 4, POOL_GD, POOL_GD), (1, POOL_W), (1, Q_LORA), (1, KV_LORA), (1, QK_DIM), (1, QK_DIM), (1, 1)]


def kernel(x, positions, mix_norm, ffn_norm, even_w_in, sg_ln_g, sg_w_s, sg_b_s, sc_conv_w, even_w_out, odd_w_in, pool_w, pool_scale, q_a_norm, q_b, kv_a_norm, kv_b, q_norm, k_norm, odd_w_out, ffn_w_gate, ffn_w_up, ffn_w_down, loss_target, m_mix_norm, m_ffn_norm, m_even_w_in, m_sg_ln_g, m_sg_w_s, m_sg_b_s, m_sc_conv_w, m_even_w_out, m_odd_w_in, m_pool_w, m_pool_scale, m_q_a_norm, m_q_b, m_kv_a_norm, m_kv_b, m_q_norm, m_k_norm, m_odd_w_out, m_ffn_w_gate, m_ffn_w_up, m_ffn_w_down, v_mix_norm, v_ffn_norm, v_even_w_in, v_sg_ln_g, v_sg_w_s, v_sg_b_s, v_sc_conv_w, v_even_w_out, v_odd_w_in, v_pool_w, v_pool_scale, v_q_a_norm, v_q_b, v_kv_a_norm, v_kv_b, v_q_norm, v_k_norm, v_odd_w_out, v_ffn_w_gate, v_ffn_w_up, v_ffn_w_down):
    xi, yi, ci = _place()
    me = 4 * xi + 2 * yi + ci

    chunks = _pack_shards(even_w_in, even_w_out, odd_w_in, q_b, kv_b, odd_w_out, ffn_w_gate, ffn_w_up, ffn_w_down)

    def lane_pad(a):
        return jnp.pad(a, ((0, 0), (0, 128 - a.shape[1])))

    tile = jnp.concatenate([lane_pad(sc_conv_w[0]), lane_pad(pool_scale), lane_pad(q_a_norm), lane_pad(kv_a_norm),
                            jnp.zeros((2, 128), F32)], axis=0)
    chip = 2 * xi + yi
    where = jnp.stack([ci, chip, chip ^ 2, chip ^ 1, chip ^ 3]).astype(jnp.int32)
    grad_x, (pb_ffn1, gb_ffn1), in_flight, (gp_eout, started), tot = _step(
        x, positions, loss_target, chunks, tile, where, mix_norm, ffn_norm, sg_ln_g, sg_w_s, sg_b_s,
        pool_w, q_norm, k_norm)

    (ga_eout,) = _comm_alone(_pair_exchange_comm(gp_eout), "rs_pair_exchange_eout")
    pb_eout = _rs_pair_sum(gp_eout, ga_eout, where + (0.0 * started[0, :1]).astype(jnp.int32), "rs_pair_sum_eout")
    eout_sems, pb_eout, land_eout, started = _chip_exchange_start(pb_eout, "eout")
    *in_flight, open_ein = in_flight
    landed = [_chip_exchange_wait(*parts, started, tag) for parts, tag in zip(in_flight, ("ffn0a", "ffn0b", "mix1"))]
    gsh_ffn0a, gsh_ffn0b, gsh_mix1, gsh_ffn1 = _rs_final_sums(
        [pb for pb, _ in landed] + [pb_ffn1], [gb for _, gb in landed] + [gb_ffn1], "rs_final_sums", started)

    (g_mix, g_ffn, g_lng, g_ws, g_bs, g_cw_full, g_pw, g_ps_full, g_qa_full, g_kva_full, g_qn, g_kn, loss) = tot
    g_cw = lax.dynamic_slice_in_dim(g_cw_full, me * 64, 64, axis=1)[None]
    g_ps = lax.dynamic_slice_in_dim(g_ps_full, me * 32, 32, axis=1)
    g_qa = lax.dynamic_slice_in_dim(g_qa_full, me * 48, 48, axis=1)
    g_kva = lax.dynamic_slice_in_dim(g_kva_full, me * 32, 32, axis=1)

    def tr(a):
        return jnp.swapaxes(a, -1, -2)

    g_gate = tr(jnp.stack([gsh_ffn0a[OFF_GATE:OFF_GATE + N_FF], gsh_ffn1[OFF_GATE:OFF_GATE + N_FF]]))
    g_up = tr(jnp.stack([gsh_ffn0a[OFF_UP:OFF_UP + N_FF], gsh_ffn1[OFF_UP:OFF_UP + N_FF]]))
    g_down = jnp.stack([gsh_ffn0b, gsh_ffn1[R_GU:R_GU + N_FF]])
    g_oin = gsh_mix1[OFF_OIN:OFF_OIN + N_SQ, :ODD_IN][None]
    g_oout = gsh_mix1[OFF_OOUT:OFF_OOUT + N_SQ][None]
    g_qb = tr(gsh_mix1[OFF_QB:OFF_QB + N_QB_USED].reshape(1, 144, Q_LORA))
    g_kvb = tr(gsh_mix1[OFF_KVB:OFF_KVB + N_KVB].reshape(1, 192, KV_LORA))
    transposed = ("even_w_in", "odd_w_in", "q_b", "kv_b", "ffn_w_gate", "ffn_w_up")

    names = ("mix_norm", "ffn_norm", "even_w_in", "sg_ln_g", "sg_w_s", "sg_b_s", "sc_conv_w", "even_w_out",
             "odd_w_in", "pool_w", "pool_scale", "q_a_norm", "q_b", "kv_a_norm", "kv_b", "q_norm", "k_norm",
             "odd_w_out", "ffn_w_gate", "ffn_w_up", "ffn_w_down")
    grads = dict(mix_norm=g_mix, ffn_norm=g_ffn, sg_ln_g=g_lng, sg_w_s=g_ws, sg_b_s=g_bs,
                 sc_conv_w=g_cw, odd_w_in=g_oin, pool_w=g_pw, pool_scale=g_ps, q_a_norm=g_qa,
                 q_b=g_qb, kv_a_norm=g_kva, kv_b=g_kvb, q_norm=g_qn, k_norm=g_kn, odd_w_out=g_oout,
                 ffn_w_gate=g_gate, ffn_w_up=g_up, ffn_w_down=g_down)
    weights = dict(mix_norm=mix_norm, ffn_norm=ffn_norm, even_w_in=even_w_in, sg_ln_g=sg_ln_g, sg_w_s=sg_w_s,
                   sg_b_s=sg_b_s, sc_conv_w=sc_conv_w, even_w_out=even_w_out, odd_w_in=odd_w_in, pool_w=pool_w,
                   pool_scale=pool_scale, q_a_norm=q_a_norm, q_b=q_b, kv_a_norm=kv_a_norm, kv_b=kv_b, q_norm=q_norm,
                   k_norm=k_norm, odd_w_out=odd_w_out, ffn_w_gate=ffn_w_gate, ffn_w_up=ffn_w_up,
                   ffn_w_down=ffn_w_down)
    m_in = dict(mix_norm=m_mix_norm, ffn_norm=m_ffn_norm, even_w_in=m_even_w_in, sg_ln_g=m_sg_ln_g, sg_w_s=m_sg_w_s,
                sg_b_s=m_sg_b_s, sc_conv_w=m_sc_conv_w, even_w_out=m_even_w_out, odd_w_in=m_odd_w_in,
                pool_w=m_pool_w, pool_scale=m_pool_scale, q_a_norm=m_q_a_norm, q_b=m_q_b, kv_a_norm=m_kv_a_norm,
                kv_b=m_kv_b, q_norm=m_q_norm, k_norm=m_k_norm, odd_w_out=m_odd_w_out, ffn_w_gate=m_ffn_w_gate,
                ffn_w_up=m_ffn_w_up, ffn_w_down=m_ffn_w_down)
    v_in = dict(mix_norm=v_mix_norm, ffn_norm=v_ffn_norm, even_w_in=v_even_w_in, sg_ln_g=v_sg_ln_g, sg_w_s=v_sg_w_s,
                sg_b_s=v_sg_b_s, sc_conv_w=v_sc_conv_w, even_w_out=v_even_w_out, odd_w_in=v_odd_w_in,
                pool_w=v_pool_w, pool_scale=v_pool_scale, q_a_norm=v_q_a_norm, q_b=v_q_b, kv_a_norm=v_kv_a_norm,
                kv_b=v_kv_b, q_norm=v_q_norm, k_norm=v_k_norm, odd_w_out=v_odd_w_out, ffn_w_gate=v_ffn_w_gate,
                ffn_w_up=v_ffn_w_up, ffn_w_down=v_ffn_w_down)
    delta, new_m, new_v = {}, {}, {}

    def as2d(k, a):
        a = tr(a) if k in transposed else a
        return a.reshape(-1, a.shape[-1])

    def back(k, a):
        shape = weights[k].shape
        return tr(a.reshape(shape[:-2] + (shape[-1], shape[-2]))) if k in transposed else a.reshape(shape)

    def update(group, name, nblk=1):
        outs = _adamw([as2d(k, weights[k]) for k in group], [as2d(k, grads[k]) for k in group],
                      [as2d(k, m_in[k]) for k in group], [as2d(k, v_in[k]) for k in group], name, nblk)
        for i, k in enumerate(group):
            delta[k], new_m[k], new_v[k] = (back(k, o[i]) for o in outs)

    update(["ffn_w_gate", "ffn_w_up", "ffn_w_down"], "adamw_ffn", 4)
    update(["odd_w_in", "odd_w_out"], "adamw_mix", 2)
    update([k for k in names if k not in delta and k not in ("even_w_in", "even_w_out")], "adamw_small")

    pb_ein, gb_ein = _chip_exchange_wait(*open_ein, new_v["k_norm"], "ein")
    pb_eout, gb_eout = _chip_exchange_wait(eout_sems, pb_eout, land_eout, new_v["k_norm"], "eout")
    gsh_ein, gsh_eout = _rs_final_sums([pb_ein, pb_eout], [gb_ein, gb_eout], "rs_final_sums_even")
    grads["even_w_in"] = tr(gsh_ein[None])
    grads["even_w_out"] = gsh_eout[None]
    update(["even_w_in", "even_w_out"], "adamw_even", 2)

    return (loss.reshape(()), grad_x, *[grads[k] for k in names], *[delta[k] for k in names],
            *[new_m[k] for k in names], *[new_v[k] for k in names])
```

```python
import functools

import numpy as np
import jax
import jax.numpy as jnp
from jax import lax
from jax.experimental import pallas as pl
from jax.experimental.pallas import tpu as pltpu

F32 = jnp.float32
BF16 = jnp.bfloat16
MESH = pl.DeviceIdType.MESH

D = 1024
EPS = 1e-6
NEG_INF = -1e30
SG_HEADS, SG_HD, SG_W, SG_CHUNK = 4, 128, 512, 128
SC_W = 512
EVEN_IN = 2560
POOL_W = 256
POOL_GD = 64
Q_LORA, KV_LORA, QK_ROPE, QK_NOPE, V_DIM = 384, 256, 64, 128, 128
QK_DIM = QK_NOPE + QK_ROPE
HEADS = 6
HP = 256
ODD_IN = 960
D_FF = 2816
ROPE_THETA = 10000.0
ATT_SCALE = QK_DIM ** -0.5
LR, B1, B2, ADAM_EPS, WD, STEP = 0.001, 0.9, 0.999, 1e-08, 0.01, 10

N_DEV = 8
TB = 512
TB_FFN_BWD = 256
TK_DW = 1024
HALO = 16
VMEM_LIMIT = 56 * 1024 * 1024

N_EIN, N_FF, N_SQ = 320, 352, 128
OFF_GATE, OFF_UP, R_GU = 0, 352, 704
OFF_OIN, OFF_OOUT, OFF_QB, OFF_KVB, R_MIX1 = 0, 128, 256, 320, 384
N_QB, N_QB_USED, N_KVB = 64, 54, 48

INV_SQRT2 = 0.7071067811865476
INV_SQRT_2PI = 0.3989422804014327


def _dot(a, b, ca, cb):
    return lax.dot_general(a, b, (((ca,), (cb,)), ((), ())), preferred_element_type=F32)


def _cparams(n_axes=1):
    return pltpu.CompilerParams(dimension_semantics=("arbitrary",) * n_axes, vmem_limit_bytes=VMEM_LIMIT)


def _wspec(n, off):
    assert off % n == 0
    idx = off // n
    return pl.BlockSpec((N_DEV, n, D), lambda i: (0, idx, 0), pipeline_mode=pl.Buffered(1))


def _const_spec(shape):
    zeros = (0,) * len(shape)
    return pl.BlockSpec(shape, lambda *_: zeros)


class _Comm:
    def __init__(self, ins, out_shapes, sems, start, wait, mid=None):
        self.ins, self.out_shapes, self.sems, self.start, self.wait, self.mid = ins, out_shapes, sems, start, wait, mid


def _both(c1, c2):
    def split(f1, f2):
        def run(ins, outs, sems):
            f1(ins[:len(c1.ins)], outs[:len(c1.out_shapes)], sems[:len(c1.sems)])
            f2(ins[len(c1.ins):], outs[len(c1.out_shapes):], sems[len(c1.sems):])
        return run

    def nothing(ins, outs, sems):
        pass

    mid = None if c1.mid is None and c2.mid is None else split(c1.mid or nothing, c2.mid or nothing)
    return _Comm(c1.ins + c2.ins, c1.out_shapes + c2.out_shapes, c1.sems + c2.sems,
                 split(c1.start, c2.start), split(c1.wait, c2.wait), mid)


def _call(body, name, grid, in_specs, out_specs, out_shape, args, scratch_shapes=(), comm=None, aliases=None):
    n_axes = len(grid)
    aliases = aliases or {}
    if comm is None:
        res = pl.pallas_call(
            body, name=name, grid=grid, in_specs=list(in_specs), out_specs=list(out_specs),
            out_shape=list(out_shape), scratch_shapes=list(scratch_shapes), input_output_aliases=aliases,
            compiler_params=_cparams(n_axes))(*args)
        return list(res), []
    ni, no, ns = len(in_specs), len(out_specs), len(scratch_shapes)
    ci, co = len(comm.ins), len(comm.out_shapes)
    n_steps = int(np.prod(grid))

    def carrier(*refs):
        ins, cin = refs[:ni], refs[ni:ni + ci]
        outs, cout = refs[ni + ci:ni + ci + no], refs[ni + ci + no:ni + ci + no + co]
        scr, sems = refs[ni + ci + no + co:ni + ci + no + co + ns], refs[ni + ci + no + co + ns:]
        step = 0
        for a in range(n_axes):
            step = step * grid[a] + pl.program_id(a)

        @pl.when(step == 0)
        def _():
            comm.start(cin, cout, sems)

        body(*ins, *outs, *scr)

        if comm.mid is not None and n_steps >= 4:
            @pl.when(step == n_steps // 2)
            def _():
                comm.mid(cin, cout, sems)

        @pl.when(step == n_steps - 1)
        def _():
            if comm.mid is not None and n_steps < 4:
                comm.mid(cin, cout, sems)
            comm.wait(cin, cout, sems)

    any_spec = pl.BlockSpec(memory_space=pl.ANY)
    res = pl.pallas_call(
        carrier, name=name, grid=grid, in_specs=list(in_specs) + [any_spec] * ci,
        out_specs=list(out_specs) + [any_spec] * co, out_shape=list(out_shape) + list(comm.out_shapes),
        scratch_shapes=list(scratch_shapes) + list(comm.sems), input_output_aliases=aliases,
        compiler_params=_cparams(n_axes))(*args, *comm.ins)
    return list(res[:no]), list(res[no:])


def _comm_alone(comm, name):
    ci, co = len(comm.ins), len(comm.out_shapes)

    def body(*refs):
        cin, cout, sems = refs[:ci], refs[ci:ci + co], refs[ci + co:]
        comm.start(cin, cout, sems)
        if comm.mid is not None:
            comm.mid(cin, cout, sems)
        comm.wait(cin, cout, sems)

    any_spec = pl.BlockSpec(memory_space=pl.ANY)
    res = pl.pallas_call(
        body, name=name, out_shape=list(comm.out_shapes), in_specs=[any_spec] * ci, out_specs=[any_spec] * co,
        scratch_shapes=list(comm.sems))(*comm.ins)
    return list(res)


def _rms(x, g):
    r = lax.rsqrt(jnp.mean(x * x, axis=-1, keepdims=True) + EPS)
    return x * r * g, r


def _rms_bwd(x, r, g, dy):
    xh = x * r
    dxh = dy * g
    dx = r * (dxh - xh * jnp.mean(dxh * xh, axis=-1, keepdims=True))
    dg = jnp.sum(dy * xh, axis=0, keepdims=True)
    return dx, dg


def _gelu(x):
    return 0.5 * x * (1.0 + lax.erf(x * INV_SQRT2))


def _gelu_grad(x):
    return 0.5 * (1.0 + lax.erf(x * INV_SQRT2)) + x * jnp.exp(-0.5 * x * x) * INV_SQRT_2PI


def _shift_down(a, k):
    rows = lax.broadcasted_iota(jnp.int32, a.shape, 0)
    return jnp.where(rows >= k, pltpu.roll(a, k, 0), 0.0)


def _shift_up(a, k):
    n = a.shape[0]
    rows = lax.broadcasted_iota(jnp.int32, a.shape, 0)
    return jnp.where(rows < n - k, pltpu.roll(a, n - k, 0), 0.0)


def _tril_bf16(w):
    r = lax.broadcasted_iota(jnp.int32, w.shape, 0)
    c = lax.broadcasted_iota(jnp.int32, w.shape, 1)
    return jnp.where(r >= c, w, 0.0).astype(BF16)


def _ln_head(vh, g):
    mu = jnp.mean(vh, axis=-1, keepdims=True)
    xc = vh - mu
    rr = lax.rsqrt(jnp.mean(xc * xc, axis=-1, keepdims=True) + EPS)
    xh = xc * rr
    return xh * g, xh, rr


def _conv_fwd(z, tail, cw_ref):
    ext = jnp.concatenate([tail, z], axis=0)
    zs1 = _shift_down(ext, 1)[HALO:]
    zs2 = _shift_down(ext, 2)[HALO:]
    y = cw_ref[2:3, :] * z + cw_ref[1:2, :] * zs1 + cw_ref[0:1, :] * zs2
    return y, zs1, zs2


def _pool_cnt(shape, blk_in_seq):
    rows = lax.broadcasted_iota(jnp.int32, shape, 0)
    grp = lax.broadcasted_iota(jnp.int32, shape, 1) // POOL_GD
    win = jnp.where(grp == 0, 2, jnp.where(grp == 1, 4, jnp.where(grp == 2, 8, 16)))
    tpos = blk_in_seq * shape[0] + rows + 1
    return jnp.minimum(tpos, win).astype(F32), grp


def _pool_select(grp, s2, s4, s8, s16):
    return jnp.where(grp == 0, s2, jnp.where(grp == 1, s4, jnp.where(grp == 2, s8, s16)))


def _pool_fwd(z, tail, blk_in_seq):
    ext = jnp.concatenate([tail, z], axis=0)
    s2 = ext + _shift_down(ext, 1)
    s4 = s2 + _shift_down(s2, 2)
    s8 = s4 + _shift_down(s4, 4)
    s16 = s8 + _shift_down(s8, 8)
    cnt, grp = _pool_cnt(z.shape, blk_in_seq)
    sums = _pool_select(grp, s2[HALO:], s4[HALO:], s8[HALO:], s16[HALO:])
    return sums / cnt - z, cnt, grp


def _pool_bwd(dpooled, dpm, head, grp):
    n = dpm.shape[0]
    ext = jnp.concatenate([dpm, head], axis=0)
    u2 = ext + _shift_up(ext, 1)
    u4 = u2 + _shift_up(u2, 2)
    u8 = u4 + _shift_up(u4, 4)
    u16 = u8 + _shift_up(u8, 8)
    return _pool_select(grp, u2[:n], u4[:n], u8[:n], u16[:n]) - dpooled


def _lane_sums(a):
    return _dot(a.astype(BF16), jnp.ones((a.shape[1], a.shape[1]), BF16), 1, 0)


def _swap_halves(y1):
    src = lax.broadcasted_iota(jnp.int32, (128, 128), 0)
    dst = lax.broadcasted_iota(jnp.int32, (128, 128), 1)
    perm = jnp.where(((dst < 32) & (src == dst + 32)) | ((dst >= 32) & (dst < QK_ROPE) & (src == dst - 32)), 1.0, 0.0)
    return _dot(y1.astype(BF16), perm.astype(BF16), 1, 0)


def _rope(y1, c, s):
    return y1 * c + _swap_halves(y1) * s


def _rope_bwd(d1, c, s):
    return d1 * c + _swap_halves(d1 * s)


def _qk_prep(x, g, c, s):
    r = lax.rsqrt(_lane_sums(x * x) * (1.0 / QK_DIM) + EPS)
    y = x * r * g
    return jnp.concatenate([y[:, :128], _rope(y[:, 128:], c, s)], axis=1), r


def _qk_prep_bwd(dout, x, r, g, c, s):
    dy = jnp.concatenate([dout[:, :128], _rope_bwd(dout[:, 128:], c, s)], axis=1)
    xh = x * r
    dxh = dy * g
    dx = r * (dxh - xh * (_lane_sums(dxh * xh) * (1.0 / QK_DIM)))
    return dx, jnp.sum(dy * xh, axis=0, keepdims=True)


def _place():
    return lax.axis_index("x"), lax.axis_index("y"), lax.axis_index("c")


def _gather_comm(arrs):
    n = len(arrs)

    def halves(a):
        rows = arrs[a].shape[0]
        tile = 16 if arrs[a].dtype == BF16 else 8
        top = rows // 2 if rows % (2 * tile) == 0 else rows
        return (0, top), (top, rows - top)

    def plan(ins, outs, sems):
        send_sems, recv_sems, local_sems = sems
        x, y, c = _place()
        me, sib, xn, yn, dg = (x, y, c), (x, y, 1 - c), (1 - x, y, c), (x, 1 - y, c), (1 - x, 1 - y, c)

        def slot(a, dev, part=None):
            ref = outs[a].at[4 * dev[0] + 2 * dev[1] + dev[2]]
            return ref if part is None else ref.at[pl.ds(part[0], part[1])]

        def copy(a, k, block, to, src=None, part=None):
            return pltpu.make_async_remote_copy(
                src_ref=slot(a, block, part) if src is None else src, dst_ref=slot(a, block, part),
                send_sem=send_sems.at[a, k], recv_sem=recv_sems.at[a, k], device_id=to, device_id_type=MESH)

        local = [pltpu.make_async_copy(ins[a], slot(a, me), local_sems.at[a]) for a in range(n)]
        return me, sib, xn, yn, dg, copy, local

    def start(ins, outs, sems):
        me, sib, xn, yn, _, copy, local = plan(ins, outs, sems)
        for a in range(n):
            local[a].start()
            for k, to in enumerate((sib, xn, yn)):
                copy(a, k, me, to, src=ins[a]).start()

    def mid(ins, outs, sems):
        me, sib, xn, yn, _, copy, _ = plan(ins, outs, sems)
        for a in range(n):
            top, bottom = halves(a)
            copy(a, 1, xn, me).wait_recv()
            copy(a, 3, xn, yn, part=top).start()
            copy(a, 5, xn, sib).start()
            copy(a, 2, yn, me).wait_recv()
            if bottom[1]:
                copy(a, 4, yn, xn, part=bottom).start()
            copy(a, 6, yn, sib).start()

    def wait(ins, outs, sems):
        me, sib, xn, yn, dg, copy, local = plan(ins, outs, sems)
        other = lambda dev: (dev[0], dev[1], 1 - dev[2])
        for a in range(n):
            top, bottom = halves(a)
            copy(a, 3, dg, me, part=top).wait_recv()
            if bottom[1]:
                copy(a, 4, dg, me, part=bottom).wait_recv()
            copy(a, 7, dg, sib).start()
        for a in range(n):
            top, bottom = halves(a)
            for k, block in ((0, sib), (5, other(xn)), (6, other(yn)), (7, other(dg))):
                copy(a, k, block, me).wait_recv()
            for k, block in ((0, me), (1, me), (2, me), (5, xn), (6, yn), (7, dg)):
                copy(a, k, block, me, src=ins[a] if k < 3 else None).wait_send()
            copy(a, 3, xn, me, part=top).wait_send()
            if bottom[1]:
                copy(a, 4, yn, me, part=bottom).wait_send()
            local[a].wait()

    return _Comm(
        list(arrs), [jax.ShapeDtypeStruct((N_DEV,) + a.shape, a.dtype) for a in arrs],
        [pltpu.SemaphoreType.DMA((n, 8)), pltpu.SemaphoreType.DMA((n, 8)), pltpu.SemaphoreType.DMA((n,))],
        start, wait, mid)


def _sum_gathered(g):
    rows = g.shape[1]

    def body(g_ref, sum_ref):
        total = g_ref[0]
        for d in range(1, N_DEV):
            total = total + g_ref[d]
        sum_ref[...] = total

    return pl.pallas_call(
        body, name="sum_gathered_small", out_shape=jax.ShapeDtypeStruct((rows, 128), F32), grid=(1,),
        in_specs=[pl.BlockSpec((N_DEV, rows, 128), lambda i: (0, 0, 0))],
        out_specs=pl.BlockSpec((rows, 128), lambda i: (0, 0)), compiler_params=_cparams(1),
    )(g)


def _sum_rows(rows):
    return rows if rows <= 512 else rows // 2


def _pair_exchange_comm(gp):
    _, rows, cols = gp.shape

    def copies(ins, outs, sems):
        send_sems, recv_sems = sems
        x, y, c = _place()
        return [pltpu.make_async_remote_copy(
            src_ref=ins[0].at[2 * j + (1 - c)], dst_ref=outs[0].at[j], send_sem=send_sems.at[j],
            recv_sem=recv_sems.at[j], device_id=(x, y, 1 - c), device_id_type=MESH) for j in range(4)]

    def start(ins, outs, sems):
        for cp in copies(ins, outs, sems):
            cp.start()

    def wait(ins, outs, sems):
        for cp in copies(ins, outs, sems):
            cp.wait()

    return _Comm([gp], [jax.ShapeDtypeStruct((4, rows, cols), gp.dtype)],
                 [pltpu.SemaphoreType.DMA((4,)), pltpu.SemaphoreType.DMA((4,))], start, wait)


def _rs_pair_sum(gp, got, where, name):
    _, rows, cols = got.shape
    rb = _sum_rows(rows)
    gp4 = gp.reshape(4, 2, rows, cols)

    def body(w_ref, a_ref, b_ref, o_ref):
        o_ref[0] = (a_ref[0, 0].astype(F32) + b_ref[0].astype(F32)).astype(o_ref.dtype)

    return pl.pallas_call(
        body, name=name, out_shape=jax.ShapeDtypeStruct((4, rows, cols), gp.dtype),
        grid_spec=pltpu.PrefetchScalarGridSpec(
            num_scalar_prefetch=1, grid=(4, rows // rb),
            in_specs=[pl.BlockSpec((1, 1, rb, cols), lambda k, r, w: (w[1 + k], w[0], r, 0)),
                      pl.BlockSpec((1, rb, cols), lambda k, r, w: (w[1 + k], r, 0))],
            out_specs=pl.BlockSpec((1, rb, cols), lambda k, r, w: (k, r, 0))),
        compiler_params=_cparams(2),
    )(where, gp4, got)


def _chip_exchange_comm(pb):
    _, rows, cols = pb.shape

    def copies(ins, outs, sems):
        send_sems, recv_sems = sems
        x, y, c = _place()
        chips = [(1 - x, y), (x, 1 - y), (1 - x, 1 - y)]
        return [pltpu.make_async_remote_copy(
            src_ref=ins[0].at[1 + k], dst_ref=outs[0].at[k], send_sem=send_sems.at[k],
            recv_sem=recv_sems.at[k], device_id=(px, py, c), device_id_type=MESH)
            for k, (px, py) in enumerate(chips)]

    def start(ins, outs, sems):
        for cp in copies(ins, outs, sems):
            cp.start()

    def wait(ins, outs, sems):
        for cp in copies(ins, outs, sems):
            cp.wait()

    return _Comm([pb], [jax.ShapeDtypeStruct((3, rows, cols), pb.dtype)],
                 [pltpu.SemaphoreType.DMA((3,)), pltpu.SemaphoreType.DMA((3,))], start, wait)


def _chip_exchange_start(pb, tag):
    _, rows, cols = pb.shape

    def body(pb_ref, land_ref, *rest):
        sems, token = rest[:6], rest[8]
        x, y, c = _place()
        chips = [(1 - x, y), (x, 1 - y), (1 - x, 1 - y)]
        for k, (px, py) in enumerate(chips):
            pltpu.make_async_remote_copy(
                src_ref=pb_ref.at[1 + k], dst_ref=land_ref.at[k], send_sem=sems[k], recv_sem=sems[3 + k],
                device_id=(px, py, c), device_id_type=MESH).start()
        token[...] = jnp.zeros_like(token)

    hbm = pl.BlockSpec(memory_space=pltpu.HBM)
    sem = pl.BlockSpec(memory_space=pltpu.SEMAPHORE)
    land = lax.empty((3, rows, cols), pb.dtype)
    res = pl.pallas_call(
        body, name="rs_chip_exchange_start_" + tag,
        out_shape=(*[pltpu.SemaphoreType.DMA(())] * 6, pltpu.HBM(pb.shape, pb.dtype), pltpu.HBM(land.shape, land.dtype),
                   jax.ShapeDtypeStruct((8, 128), F32)),
        in_specs=(hbm, hbm), out_specs=(*[sem] * 6, hbm, hbm, pl.BlockSpec(memory_space=pltpu.VMEM)),
        input_output_aliases={0: 6, 1: 7},
        compiler_params=pltpu.CompilerParams(has_side_effects=pltpu.SideEffectType.DATAFLOW_SIDE_EFFECTING),
    )(pltpu.with_memory_space_constraint(pb, pltpu.HBM), pltpu.with_memory_space_constraint(land, pltpu.HBM))
    return list(res[:6]), res[6], res[7], res[8]


def _chip_exchange_wait(sems, pb_thru, land_thru, after, tag):
    def body(pb_ref, land_ref, *rest):
        sems_in = rest[:6]
        x, y, c = _place()
        chips = [(1 - x, y), (x, 1 - y), (1 - x, 1 - y)]
        for k, (px, py) in enumerate(chips):
            cp = pltpu.make_async_remote_copy(
                src_ref=pb_ref.at[1 + k], dst_ref=land_ref.at[k], send_sem=sems_in[k], recv_sem=sems_in[3 + k],
                device_id=(px, py, c), device_id_type=MESH)
            cp.wait_send()
            cp.wait_recv()

    hbm = pl.BlockSpec(memory_space=pltpu.HBM)
    sem = pl.BlockSpec(memory_space=pltpu.SEMAPHORE)
    res = pl.pallas_call(
        body, name="rs_chip_exchange_wait_" + tag,
        out_shape=(pltpu.HBM(pb_thru.shape, pb_thru.dtype), pltpu.HBM(land_thru.shape, land_thru.dtype)),
        in_specs=(hbm, hbm, *[sem] * 6, pl.BlockSpec(memory_space=pl.ANY)), out_specs=(hbm, hbm),
        input_output_aliases={0: 0, 1: 1},
        compiler_params=pltpu.CompilerParams(has_side_effects=pltpu.SideEffectType.DATAFLOW_SIDE_EFFECTING),
    )(pb_thru, land_thru, *sems, after)
    return res[0], res[1]


def _rs_final_sums(pbs, gots, name, after=None):
    n = len(pbs)

    def body(*refs):
        outs = refs[len(refs) - n:]
        for a in range(n):
            m_ref, g_ref, o_ref = refs[a], refs[n + a], outs[a]
            o_ref[...] = ((m_ref[0].astype(F32) + g_ref[0].astype(F32)) + g_ref[1].astype(F32)) + g_ref[2].astype(F32)

    half = [pb.shape[1] // 2 for pb in pbs]
    in_specs = ([pl.BlockSpec((1, h, D), lambda i: (0, i, 0)) for h in half]
                + [pl.BlockSpec((3, h, D), lambda i: (0, i, 0)) for h in half])
    args = (*pbs, *gots)
    if after is not None:
        in_specs, args = in_specs + [pl.BlockSpec(memory_space=pl.ANY)], args + (after,)
    res, _ = _call(body, name, (2,), in_specs, [pl.BlockSpec((h, D), lambda i: (i, 0)) for h in half],
                   [jax.ShapeDtypeStruct(pb.shape[1:], F32) for pb in pbs], args)
    return res


def _rope_tables(pos_col, inv_freq, comm=None):
    t = pos_col.shape[0]

    def body(p_ref, f_ref, c_ref, s_ref):
        ang = p_ref[...].astype(F32) * f_ref[...]
        lane = lax.broadcasted_iota(jnp.int32, ang.shape, 1)
        c_ref[...] = jnp.where(lane < QK_ROPE, jnp.cos(ang), 0.0)
        s = jnp.sin(ang)
        s_ref[...] = jnp.where(lane < 32, -s, jnp.where(lane < QK_ROPE, s, 0.0))

    spec = pl.BlockSpec((TB, 128), lambda i: (i, 0))
    return _call(
        body, "rope_tables", (t // TB,), [pl.BlockSpec((TB, 1), lambda i: (i, 0)), _const_spec((1, 128))],
        [spec] * 2, [jax.ShapeDtypeStruct((t, 128), F32)] * 2, (pos_col, inv_freq), (), comm)


def _sgu_conv_fwd(proj, tail, lng_ref, ws_ref, bst_ref, cw_ref):
    gu = _gelu(proj[:, 0:SG_W])
    gv = _gelu(proj[:, SG_W:2 * SG_W])
    bg = proj[:, 1024:1536]
    z = proj[:, 1536:2048] * proj[:, 2048:2560]
    heads = []
    for h in range(SG_HEADS):
        sl = slice(h * SG_HD, (h + 1) * SG_HD)
        vn, _, _ = _ln_head(gv[:, sl], lng_ref[:, sl])
        vnb = vn.astype(BF16)
        wm = _tril_bf16(ws_ref[h])
        bcol = bst_ref[:, h:h + 1]
        mixed = jnp.concatenate(
            [_dot(wm, vnb[k * SG_CHUNK:(k + 1) * SG_CHUNK], 1, 0) + bcol for k in range(TB // SG_CHUNK)], axis=0)
        heads.append(gu[:, sl] * mixed)
    a_out = jnp.concatenate(heads, axis=1)
    y, _, _ = _conv_fwd(z, tail, cw_ref)
    return a_out, bg * y, z


def _even_fwd(x, wg, gamma, lng, ws, bst, cw, seq, comm=None):
    t = x.shape[0]
    nbs = seq // TB

    def body(x_ref, gam_ref, win_ref, wout_ref, lng_ref, ws_ref, bst_ref, cw_ref, x1_ref, proj_ref, tail_ref):
        i = pl.program_id(0)
        xv = x_ref[...]
        h, _ = _rms(xv, gam_ref[...])
        proj = _dot(h.astype(BF16), win_ref[...].reshape(EVEN_IN, D), 1, 1)
        proj_ref[...] = proj.astype(BF16)
        tail = jnp.where(i % nbs == 0, 0.0, tail_ref[...])
        a_out, b_out, z = _sgu_conv_fwd(proj, tail, lng_ref, ws_ref, bst_ref, cw_ref)
        tail_ref[...] = z[TB - HALO:, :]
        x1_ref[...] = (xv + _dot(a_out.astype(BF16), wout_ref[0:4].reshape(512, D), 1, 0)
                       + _dot(b_out.astype(BF16), wout_ref[4:8].reshape(512, D), 1, 0))

    row = pl.BlockSpec((TB, D), lambda i: (i, 0))
    return _call(
        body, "even_fwd", (t // TB,),
        [row, _const_spec((1, D)), _wspec(N_EIN, 0), _wspec(N_SQ, 0), _const_spec((1, SG_W)),
         _const_spec((SG_HEADS, 128, 128)), _const_spec((128, 128)), _const_spec((8, SC_W))],
        [row, pl.BlockSpec((TB, EVEN_IN), lambda i: (i, 0))],
        [jax.ShapeDtypeStruct((t, D), F32), jax.ShapeDtypeStruct((t, EVEN_IN), BF16)],
        (x, gamma, *wg, lng, ws, bst, cw), [pltpu.VMEM((HALO, SC_W), F32)], comm)


def _even_bwd(x, proj, dx1, wg, gamma, lng, ws, bst, cw, seq, comm=None):
    t = x.shape[0]
    nb, nbs = t // TB, seq // TB

    def body(x_ref, proj_ref, ptail_ref, dx1_ref, gam_ref, win_ref, wout_ref, lng_ref, ws_ref, bst_ref, cw_ref,
             dx0_ref, dproj_ref, mix_ref, h_ref, dgam_ref, dws_ref, dbc_ref, dlng_ref, dcw_ref, head_ref):
        i = pl.program_id(0)
        blk = nb - 1 - i

        @pl.when(i == 0)
        def _():
            dgam_ref[...] = jnp.zeros_like(dgam_ref)
            dws_ref[...] = jnp.zeros_like(dws_ref)
            dbc_ref[...] = jnp.zeros_like(dbc_ref)
            dlng_ref[...] = jnp.zeros_like(dlng_ref)
            dcw_ref[...] = jnp.zeros_like(dcw_ref)

        xv = x_ref[...]
        gam = gam_ref[...]
        h, r = _rms(xv, gam)
        h_ref[...] = h.astype(BF16)
        dx1 = dx1_ref[...]
        dmix = _dot(dx1.astype(BF16), wout_ref[...].reshape(D, D), 1, 1)
        da, db = dmix[:, :SG_W], dmix[:, SG_W:]
        proj = proj_ref[...].astype(F32)
        u, v = proj[:, 0:SG_W], proj[:, SG_W:2 * SG_W]
        bg, cg, hv = proj[:, 1024:1536], proj[:, 1536:2048], proj[:, 2048:2560]
        gu, gv = _gelu(u), _gelu(v)

        a_heads, dgv_heads = [], []
        for hd in range(SG_HEADS):
            sl = slice(hd * SG_HD, (hd + 1) * SG_HD)
            g_h = lng_ref[:, sl]
            vn, xh, rr = _ln_head(gv[:, sl], g_h)
            vnb = vn.astype(BF16)
            wm = _tril_bf16(ws_ref[hd])
            bcol = bst_ref[:, hd:hd + 1]
            mixed_c, dvn_c = [], []
            dw_acc = jnp.zeros((128, 128), F32)
            db_acc = jnp.zeros((128, 1), F32)
            for k in range(TB // SG_CHUNK):
                rs = slice(k * SG_CHUNK, (k + 1) * SG_CHUNK)
                mixed = _dot(wm, vnb[rs], 1, 0) + bcol
                dmixed = da[rs, sl] * gu[rs, sl]
                dmb = dmixed.astype(BF16)
                dvn_c.append(_dot(wm, dmb, 0, 0))
                dw_acc = dw_acc + _dot(dmb, vnb[rs], 1, 1)
                db_acc = db_acc + jnp.sum(dmixed, axis=1, keepdims=True)
                mixed_c.append(mixed)
            mixed_h = jnp.concatenate(mixed_c, axis=0)
            dvn = jnp.concatenate(dvn_c, axis=0)
            r_i = lax.broadcasted_iota(jnp.int32, (128, 128), 0)
            c_i = lax.broadcasted_iota(jnp.int32, (128, 128), 1)
            dws_ref[hd] += jnp.where(r_i >= c_i, dw_acc, 0.0)
            dbc_ref[:, hd:hd + 1] += db_acc
            dlng_ref[:, sl] += jnp.sum(dvn * xh, axis=0, keepdims=True)
            dxh = dvn * g_h
            dgv = rr * (dxh - jnp.mean(dxh, axis=-1, keepdims=True)
                        - xh * jnp.mean(dxh * xh, axis=-1, keepdims=True))
            a_heads.append(gu[:, sl] * mixed_h)
            dproj_ref[:, sl] = (da[:, sl] * mixed_h * _gelu_grad(u[:, sl])).astype(BF16)
            dgv_heads.append(dgv * _gelu_grad(v[:, sl]))
        dproj_ref[:, SG_W:2 * SG_W] = jnp.concatenate(dgv_heads, axis=1).astype(BF16)
        mix_ref[:, :SG_W] = jnp.concatenate(a_heads, axis=1).astype(BF16)

        z = cg * hv
        pt = ptail_ref[...].astype(F32)
        tail = jnp.where(blk % nbs == 0, 0.0, pt[:, 1536:2048] * pt[:, 2048:2560])
        y, zs1, zs2 = _conv_fwd(z, tail, cw_ref)
        mix_ref[:, SG_W:] = (bg * y).astype(BF16)
        dy = db * bg
        head = jnp.where(blk % nbs == nbs - 1, 0.0, head_ref[...])
        ext = jnp.concatenate([dy, head], axis=0)
        dz = (cw_ref[2:3, :] * dy + cw_ref[1:2, :] * _shift_up(ext, 1)[:TB]
              + cw_ref[0:1, :] * _shift_up(ext, 2)[:TB])
        head_ref[...] = dy[:HALO, :]
        dcw_ref[2:3, :] += jnp.sum(dy * z, axis=0, keepdims=True)
        dcw_ref[1:2, :] += jnp.sum(dy * zs1, axis=0, keepdims=True)
        dcw_ref[0:1, :] += jnp.sum(dy * zs2, axis=0, keepdims=True)
        dproj_ref[:, 1024:1536] = (db * y).astype(BF16)
        dproj_ref[:, 1536:2048] = (dz * hv).astype(BF16)
        dproj_ref[:, 2048:2560] = (dz * cg).astype(BF16)

        dh = _dot(dproj_ref[...], win_ref[...].reshape(EVEN_IN, D), 1, 0)
        dxn, dgam = _rms_bwd(xv, r, gam, dh)
        dgam_ref[...] += dgam
        dx0_ref[...] = dx1 + dxn

    def rev(w):
        return pl.BlockSpec((TB, w), lambda i: (nb - 1 - i, 0))

    ptail = pl.BlockSpec((HALO, EVEN_IN), lambda i: (jnp.maximum((nb - 1 - i) * (TB // HALO) - 1, 0), 0))
    return _call(
        body, "even_bwd", (nb,),
        [rev(D), rev(EVEN_IN), ptail, rev(D), _const_spec((1, D)), _wspec(N_EIN, 0),
         _wspec(N_SQ, 0), _const_spec((1, SG_W)), _const_spec((SG_HEADS, 128, 128)),
         _const_spec((128, 128)), _const_spec((8, SC_W))],
        [rev(D), rev(EVEN_IN), rev(D), rev(D), _const_spec((1, D)), _const_spec((SG_HEADS, 128, 128)),
         _const_spec((128, 128)), _const_spec((1, SG_W)), _const_spec((8, SC_W))],
        [jax.ShapeDtypeStruct((t, D), F32), jax.ShapeDtypeStruct((t, EVEN_IN), BF16),
         jax.ShapeDtypeStruct((t, D), BF16), jax.ShapeDtypeStruct((t, D), BF16),
         jax.ShapeDtypeStruct((1, D), F32), jax.ShapeDtypeStruct((SG_HEADS, 128, 128), F32),
         jax.ShapeDtypeStruct((128, 128), F32), jax.ShapeDtypeStruct((1, SG_W), F32),
         jax.ShapeDtypeStruct((8, SC_W), F32)],
        (x, proj, proj, dx1, gamma, *wg, lng, ws, bst, cw), [pltpu.VMEM((HALO, SC_W), F32)], comm)


def _last_block_fwd(x, c_out, d_out, w_mix1, w_gu, w_d, gamma, target):
    t = x.shape[0]

    def body(x_ref, c_ref, d_ref, wo_ref, gam_ref, wg_ref, wu_ref, wd_ref, t_ref,
             x3_ref, dy_ref, g_ref, u_ref, loss_ref):
        @pl.when(pl.program_id(0) == 0)
        def _():
            loss_ref[...] = jnp.zeros_like(loss_ref)

        xv = (x_ref[...] + _dot(c_ref[...], wo_ref[0:2].reshape(POOL_W, D), 1, 0)
              + _dot(d_ref[...], wo_ref[2:8].reshape(HEADS * V_DIM, D), 1, 0))
        x3_ref[...] = xv
        h, _ = _rms(xv, gam_ref[...])
        hb = h.astype(BF16)
        g = _dot(hb, wg_ref[...].reshape(D_FF, D), 1, 1)
        u = _dot(hb, wu_ref[...].reshape(D_FF, D), 1, 1)
        g_ref[...] = g.astype(BF16)
        u_ref[...] = u.astype(BF16)
        act = g * jax.nn.sigmoid(g) * u
        err = xv + _dot(act.astype(BF16), wd_ref[...].reshape(D_FF, D), 1, 0) - t_ref[...]
        dy_ref[...] = err * (1.0 / D)
        sq = jnp.sum(jnp.sum(err * err, axis=-1, keepdims=True), axis=0, keepdims=True)
        loss_ref[...] += (0.5 / D) * sq

    def row(w):
        return pl.BlockSpec((TB, w), lambda i: (i, 0))

    res, _ = _call(
        body, "last_block_fwd", (t // TB,),
        [row(D), row(POOL_W), row(HEADS * V_DIM), _wspec(N_SQ, OFF_OOUT), _const_spec((1, D)),
         _wspec(N_FF, OFF_GATE), _wspec(N_FF, OFF_UP), _wspec(N_FF, 0), row(D)],
        [row(D), row(D), row(D_FF), row(D_FF), _const_spec((8, 128))],
        [jax.ShapeDtypeStruct((t, D), F32), jax.ShapeDtypeStruct((t, D), F32), jax.ShapeDtypeStruct((t, D_FF), BF16),
         jax.ShapeDtypeStruct((t, D_FF), BF16), jax.ShapeDtypeStruct((8, 128), F32)],
        (x, c_out, d_out, w_mix1, gamma, w_gu, w_gu, w_d, target))
    return res


def _ffn_up(x, w_gu, gamma, name, comm=None):
    t = x.shape[0]

    def body(x_ref, gam_ref, wg_ref, wu_ref, g_ref, u_ref, act_ref):
        h, _ = _rms(x_ref[...], gam_ref[...])
        hb = h.astype(BF16)
        g = _dot(hb, wg_ref[...].reshape(D_FF, D), 1, 1)
        u = _dot(hb, wu_ref[...].reshape(D_FF, D), 1, 1)
        g_ref[...] = g.astype(BF16)
        u_ref[...] = u.astype(BF16)
        act_ref[...] = (g * jax.nn.sigmoid(g) * u).astype(BF16)

    row = pl.BlockSpec((TB, D), lambda i: (i, 0))
    wide = pl.BlockSpec((TB, D_FF), lambda i: (i, 0))
    return _call(body, name, (t // TB,), [row, _const_spec((1, D)), _wspec(N_FF, OFF_GATE), _wspec(N_FF, OFF_UP)],
                 [wide, wide, wide], [jax.ShapeDtypeStruct((t, D_FF), BF16)] * 3, (x, gamma, w_gu, w_gu), (), comm)


def _ffn_down(x, act, w_d, name, comm=None):
    t = x.shape[0]

    def body(x_ref, a_ref, wd_ref, y_ref):
        y_ref[...] = x_ref[...] + _dot(a_ref[...], wd_ref[...].reshape(D_FF, D), 1, 0)

    row = pl.BlockSpec((TB, D), lambda i: (i, 0))
    wide = pl.BlockSpec((TB, D_FF), lambda i: (i, 0))
    return _call(body, name, (t // TB,), [row, wide, _wspec(N_FF, 0)], [row], [jax.ShapeDtypeStruct((t, D), F32)],
                 (x, act, w_d), (), comm)


def _ffn_bwd(x, g, u, dy, w_gu, w_d, gamma, name, comm=None, w_mix1=None):
    t = x.shape[0]
    with_dmix = w_mix1 is not None

    def body(*refs):
        x_ref, g_ref, u_ref, dy_ref, gam_ref, wg_ref, wu_ref, wd_ref = refs[:8]
        dx_ref, act_ref, dg_ref, du_ref, h_ref, dgam_ref = refs[8 + with_dmix:14 + with_dmix]

        @pl.when(pl.program_id(0) == 0)
        def _():
            dgam_ref[...] = jnp.zeros_like(dgam_ref)

        xv = x_ref[...]
        gam = gam_ref[...]
        h, r = _rms(xv, gam)
        h_ref[...] = h.astype(BF16)
        dyv = dy_ref[...]
        dact = _dot(dyv.astype(BF16), wd_ref[...].reshape(D_FF, D), 1, 1)
        gv = g_ref[...].astype(F32)
        uv = u_ref[...].astype(F32)
        sg = jax.nn.sigmoid(gv)
        silu = gv * sg
        act_ref[...] = (silu * uv).astype(BF16)
        dgb = (dact * uv * (sg * (1.0 + gv * (1.0 - sg)))).astype(BF16)
        dub = (dact * silu).astype(BF16)
        dg_ref[...] = dgb
        du_ref[...] = dub
        dh = _dot(dgb, wg_ref[...].reshape(D_FF, D), 1, 0) + _dot(dub, wu_ref[...].reshape(D_FF, D), 1, 0)
        dxn, dgam = _rms_bwd(xv, r, gam, dh)
        dgam_ref[...] += dgam
        dx = dyv + dxn
        dx_ref[...] = dx
        if with_dmix:
            refs[15][...] = _dot(dx.astype(BF16), refs[8][...].reshape(D, D), 1, 1).astype(BF16)

    row = pl.BlockSpec((TB_FFN_BWD, D), lambda i: (i, 0))
    wide = pl.BlockSpec((TB_FFN_BWD, D_FF), lambda i: (i, 0))
    in_specs = [row, wide, wide, row, _const_spec((1, D)), _wspec(N_FF, OFF_GATE), _wspec(N_FF, OFF_UP),
                _wspec(N_FF, 0)]
    out_specs = [row, wide, wide, wide, row, _const_spec((1, D))]
    out_shape = [jax.ShapeDtypeStruct((t, D), F32), jax.ShapeDtypeStruct((t, D_FF), BF16),
                 jax.ShapeDtypeStruct((t, D_FF), BF16), jax.ShapeDtypeStruct((t, D_FF), BF16),
                 jax.ShapeDtypeStruct((t, D), BF16), jax.ShapeDtypeStruct((1, D), F32)]
    args = (x, g, u, dy, gamma, w_gu, w_gu, w_d)
    if with_dmix:
        in_specs, args = in_specs + [_wspec(N_SQ, OFF_OOUT)], args + (w_mix1,)
        out_specs, out_shape = out_specs + [row], out_shape + [jax.ShapeDtypeStruct((t, D), BF16)]
    return _call(body, name, (t // TB_FFN_BWD,), in_specs, out_specs, out_shape, args, (), comm)


def _odd_pre_fwd(x, wg, gamma, qbt, kvbt, qa_g, kva_g, pw_bd, pscale, seq, comm=None):
    t = x.shape[0]
    nbs = seq // TB

    def body(x_ref, gam_ref, win_ref, qb_ref, kvb_ref, qa_ref, kva_ref, pw_ref, ps_ref,
             proj_ref, q_ref, kv_ref, kr_ref, c_ref, tail_ref):
        i = pl.program_id(0)
        h, _ = _rms(x_ref[...], gam_ref[...])
        proj = _dot(h.astype(BF16), win_ref[...].reshape(D, D), 1, 0)
        proj_ref[...] = proj.astype(BF16)
        zp, ql, kvl = proj[:, :POOL_W], proj[:, 256:640], proj[:, 640:896]
        kr_ref[...] = proj[:, 896:1024]
        qn, _ = _rms(ql, qa_ref[...])
        q_ref[...] = _dot(qn.astype(BF16), qb_ref[...], 1, 1).astype(BF16)
        kvn, _ = _rms(kvl, kva_ref[...])
        kv_ref[...] = _dot(kvn.astype(BF16), kvb_ref[...], 1, 1).astype(BF16)
        tail = jnp.where(i % nbs == 0, 0.0, tail_ref[...])
        pooled, _, _ = _pool_fwd(zp, tail, i % nbs)
        tail_ref[...] = zp[TB - HALO:, :]
        c_ref[...] = (_dot(pooled.astype(BF16), pw_ref[...], 1, 0) * ps_ref[...]).astype(BF16)

    def row(w):
        return pl.BlockSpec((TB, w), lambda i: (i, 0))

    return _call(
        body, "odd_pre_fwd", (t // TB,),
        [row(D), _const_spec((1, D)), _wspec(N_SQ, OFF_OIN), _const_spec((HEADS * HP, Q_LORA)),
         _const_spec((HEADS * HP, KV_LORA)), _const_spec((1, Q_LORA)), _const_spec((1, KV_LORA)),
         _const_spec((POOL_W, POOL_W)), _const_spec((1, POOL_W))],
        [row(D), row(HEADS * HP), row(HEADS * HP), row(128), row(POOL_W)],
        [jax.ShapeDtypeStruct((t, D), BF16), jax.ShapeDtypeStruct((t, HEADS * HP), BF16),
         jax.ShapeDtypeStruct((t, HEADS * HP), BF16), jax.ShapeDtypeStruct((t, 128), F32),
         jax.ShapeDtypeStruct((t, POOL_W), BF16)],
        (x, gamma, wg, qbt, kvbt, qa_g, kva_g, pw_bd, pscale), [pltpu.VMEM((HALO, POOL_W), F32)], comm)


def _odd_pre_bwd(x, proj, dx3, dmix, dq, dkv, dkr, wg, gamma, qbt, kvbt, qa_g, kva_g, pw_bd, pscale, seq):
    t = x.shape[0]
    nb, nbs = t // TB, seq // TB

    def body(x_ref, proj_ref, ptail_ref, dx3_ref, dco_ref, dq_ref, dkv_ref, dkr_ref, gam_ref, win_ref, qb_ref,
             kvb_ref, qa_ref, kva_ref, pw_ref, ps_ref,
             dx2_ref, dproj_ref, h_ref, qn_ref, kvn_ref, dgam_ref, dqa_ref, dkva_ref, dpw_ref, dps_ref, head_ref):
        i = pl.program_id(0)
        blk = nb - 1 - i

        @pl.when(i == 0)
        def _():
            dgam_ref[...] = jnp.zeros_like(dgam_ref)
            dqa_ref[...] = jnp.zeros_like(dqa_ref)
            dkva_ref[...] = jnp.zeros_like(dkva_ref)
            dpw_ref[...] = jnp.zeros_like(dpw_ref)
            dps_ref[...] = jnp.zeros_like(dps_ref)

        xv = x_ref[...]
        gam = gam_ref[...]
        h, r = _rms(xv, gam)
        h_ref[...] = h.astype(BF16)
        proj = proj_ref[...].astype(F32)
        zp, ql, kvl = proj[:, :POOL_W], proj[:, 256:640], proj[:, 640:896]

        qa = qa_ref[...]
        qn, rq = _rms(ql, qa)
        qn_ref[...] = qn.astype(BF16)
        dql, dqa = _rms_bwd(ql, rq, qa, _dot(dq_ref[...], qb_ref[...], 1, 0))
        dqa_ref[...] += dqa
        kva = kva_ref[...]
        kvn, rkv = _rms(kvl, kva)
        kvn_ref[...] = kvn.astype(BF16)
        dkvl, dkva = _rms_bwd(kvl, rkv, kva, _dot(dkv_ref[...], kvb_ref[...], 1, 0))
        dkva_ref[...] += dkva

        pt = ptail_ref[...].astype(F32)
        tail = jnp.where(blk % nbs == 0, 0.0, pt[:, :POOL_W])
        pooled, cnt, grp = _pool_fwd(zp, tail, blk % nbs)
        pb = pooled.astype(BF16)
        pw = pw_ref[...]
        dco = dco_ref[...].astype(F32)
        dps_ref[...] += jnp.sum(dco * _dot(pb, pw, 1, 0), axis=0, keepdims=True)
        dpo = (dco * ps_ref[...]).astype(BF16)
        dpw_ref[...] += _dot(pb, dpo, 0, 0)
        dpooled = _dot(dpo, pw, 1, 1)
        dpm = dpooled / cnt
        head = jnp.where(blk % nbs == nbs - 1, 0.0, head_ref[...])
        dz = _pool_bwd(dpooled, dpm, head, grp)
        head_ref[...] = dpm[:HALO, :]

        dproj_ref[:, :POOL_W] = dz.astype(BF16)
        dproj_ref[:, 256:640] = dql.astype(BF16)
        dproj_ref[:, 640:896] = dkvl.astype(BF16)
        dproj_ref[:, 896:1024] = dkr_ref[...].astype(BF16)
        dh = _dot(dproj_ref[...], win_ref[...].reshape(D, D), 1, 1)
        dxn, dgam = _rms_bwd(xv, r, gam, dh)
        dgam_ref[...] += dgam
        dx2_ref[...] = dx3_ref[...] + dxn

    def rev(w):
        return pl.BlockSpec((TB, w), lambda i: (nb - 1 - i, 0))

    ptail = pl.BlockSpec((HALO, D), lambda i: (jnp.maximum((nb - 1 - i) * (TB // HALO) - 1, 0), 0))
    return pl.pallas_call(
        body, name="odd_pre_bwd",
        out_shape=[jax.ShapeDtypeStruct((t, D), F32), jax.ShapeDtypeStruct((t, D), BF16),
                   jax.ShapeDtypeStruct((t, D), BF16), jax.ShapeDtypeStruct((t, Q_LORA), BF16),
                   jax.ShapeDtypeStruct((t, KV_LORA), BF16), jax.ShapeDtypeStruct((1, D), F32),
                   jax.ShapeDtypeStruct((1, Q_LORA), F32), jax.ShapeDtypeStruct((1, KV_LORA), F32),
                   jax.ShapeDtypeStruct((POOL_W, POOL_W), F32), jax.ShapeDtypeStruct((1, POOL_W), F32)],
        grid=(nb,),
        in_specs=[rev(D), rev(D), ptail, rev(D), rev(POOL_W), rev(HEADS * HP), rev(HEADS * HP), rev(128),
                  _const_spec((1, D)), _wspec(N_SQ, OFF_OIN), _const_spec((HEADS * HP, Q_LORA)),
                  _const_spec((HEADS * HP, KV_LORA)), _const_spec((1, Q_LORA)), _const_spec((1, KV_LORA)),
                  _const_spec((POOL_W, POOL_W)), _const_spec((1, POOL_W))],
        out_specs=[rev(D), rev(D), rev(D), rev(Q_LORA), rev(KV_LORA), _const_spec((1, D)), _const_spec((1, Q_LORA)),
                   _const_spec((1, KV_LORA)), _const_spec((POOL_W, POOL_W)), _const_spec((1, POOL_W))],
        scratch_shapes=[pltpu.VMEM((HALO, POOL_W), F32)],
        compiler_params=_cparams(1),
    )(x, proj, proj, dx3, dmix, dq, dkv, dkr, gamma, wg, qbt, kvbt, qa_g, kva_g, pw_bd, pscale)


def _attn_specs(seq):
    head = pl.BlockSpec((seq, HP), lambda b, h: (b, h))
    shared = pl.BlockSpec((seq, 128), lambda b, h: (b, 0))
    gain = pl.BlockSpec((1, HP), lambda b, h: (0, 0))
    return head, shared, gain


def _causal_bias(n):
    rows = lax.broadcasted_iota(jnp.int32, (n, n), 0)
    cols = lax.broadcasted_iota(jnp.int32, (n, n), 1)
    return jnp.where(cols <= rows, 0.0, NEG_INF)


def _attn_fwd(q, kv, kr, cos, sin, gq, gk, seq, comm=None):
    t = q.shape[0]
    qb = min(512, seq)

    def body(q_ref, kv_ref, kr_ref, c_ref, s_ref, gq_ref, gk_ref, o_ref, lse_ref):
        c, s = c_ref[...], s_ref[...]
        qf, _ = _qk_prep(q_ref[...].astype(F32), gq_ref[...], c, s)
        kin = jnp.concatenate([kv_ref[:, :128].astype(F32), kr_ref[...]], axis=1)
        kf, _ = _qk_prep(kin, gk_ref[...], c, s)
        qf, kf = qf.astype(BF16), kf.astype(BF16)
        v1 = jnp.concatenate([kv_ref[:, 128:], jnp.ones((seq, V_DIM), BF16)], axis=1)
        bias = _causal_bias(qb)
        for q0 in range(0, seq, qb):
            q1 = q0 + qb
            qblk = qf[q0:q1]
            s_dg = _dot(qblk, kf[q0:q1], 1, 1) + bias
            m = jnp.max(s_dg, axis=-1, keepdims=True)
            if q0:
                s_off = _dot(qblk, kf[:q0], 1, 1)
                m = jnp.maximum(m, jnp.max(s_off, axis=-1, keepdims=True))
            acc = _dot(jnp.exp(s_dg - m).astype(BF16), v1[q0:q1], 1, 0)
            if q0:
                acc = acc + _dot(jnp.exp(s_off - m).astype(BF16), v1[:q0], 1, 0)
            l = acc[:, V_DIM:]
            o_ref[q0:q1, :] = (acc[:, :V_DIM] / l).astype(BF16)
            lse_ref[q0:q1, :] = m + jnp.log(l)

    head, shared, gain = _attn_specs(seq)
    per_head = pl.BlockSpec((seq, V_DIM), lambda b, h: (b, h))
    return _call(
        body, "attn_fwd", (t // seq, HEADS),
        [head, head, shared, shared, shared, gain, gain], [per_head, per_head],
        [jax.ShapeDtypeStruct((t, HEADS * V_DIM), BF16), jax.ShapeDtypeStruct((t, HEADS * V_DIM), F32)],
        (q, kv, kr, cos, sin, gq, gk), (), comm)


def _attn_bwd(q, kv, kr, cos, sin, gq, gk, dmix, d_out, lse, seq, comm=None):
    t = q.shape[0]
    qb = min(512, seq)

    def body(q_ref, kv_ref, kr_ref, c_ref, s_ref, gq_ref, gk_ref, do_ref, o_ref, lse_ref,
             dq_ref, dkv_ref, dkr_ref, dgq_ref, dgk_ref, dqf_ref, dkf_ref, dv_ref):
        b, hd = pl.program_id(0), pl.program_id(1)

        @pl.when((b == 0) & (hd == 0))
        def _():
            dgq_ref[...] = jnp.zeros_like(dgq_ref)
            dgk_ref[...] = jnp.zeros_like(dgk_ref)

        c, sn = c_ref[...], s_ref[...]
        gq_v, gk_v = gq_ref[...], gk_ref[...]
        qin = q_ref[...].astype(F32)
        kin = jnp.concatenate([kv_ref[:, :128].astype(F32), kr_ref[...]], axis=1)
        qf32, rq = _qk_prep(qin, gq_v, c, sn)
        kf32, rk = _qk_prep(kin, gk_v, c, sn)
        qf, kf = qf32.astype(BF16), kf32.astype(BF16)
        vb = kv_ref[:, 128:]
        dkf_ref[...] = jnp.zeros_like(dkf_ref)
        dv_ref[...] = jnp.zeros_like(dv_ref)
        bias = _causal_bias(qb)
        for q0 in range(0, seq, qb):
            q1 = q0 + qb
            qblk = qf[q0:q1]
            do = do_ref[q0:q1, :]
            lse_col = lse_ref[q0:q1, 0:1]
            d_col = jnp.sum(do.astype(F32) * o_ref[q0:q1, :].astype(F32), axis=-1, keepdims=True)
            dq_acc = None
            for k0, k1, diag in ((q0, q1, True), (0, q0, False)):
                if k1 == k0:
                    continue
                s = _dot(qblk, kf[k0:k1], 1, 1)
                p = jnp.exp((s + bias if diag else s) - lse_col)
                dv_ref[k0:k1, :] += _dot(p.astype(BF16), do, 0, 0)
                ds = (p * (_dot(do, vb[k0:k1], 1, 1) - d_col)).astype(BF16)
                part = _dot(ds, kf[k0:k1], 1, 0)
                dq_acc = part if dq_acc is None else dq_acc + part
                dkf_ref[k0:k1, :] += _dot(ds, qblk, 0, 0)
            dqf_ref[q0:q1, :] = dq_acc
        dqin, dgq = _qk_prep_bwd(dqf_ref[...], qin, rq, gq_v, c, sn)
        dkin, dgk = _qk_prep_bwd(dkf_ref[...], kin, rk, gk_v, c, sn)
        dgq_ref[...] += dgq
        dgk_ref[...] += dgk
        dq_ref[...] = dqin.astype(BF16)
        dkv_ref[:, :128] = dkin[:, :128].astype(BF16)
        dkv_ref[:, 128:] = dv_ref[...].astype(BF16)

        @pl.when(hd == 0)
        def _():
            dkr_ref[...] = dkin[:, 128:]

        @pl.when(hd != 0)
        def _():
            dkr_ref[...] += dkin[:, 128:]

    head, shared, gain = _attn_specs(seq)
    per_head = pl.BlockSpec((seq, V_DIM), lambda b, h: (b, h))
    return _call(
        body, "attn_bwd", (t // seq, HEADS),
        [head, head, shared, shared, shared, gain, gain,
         pl.BlockSpec((seq, V_DIM), lambda b, h: (b, 2 + h)), per_head, per_head],
        [head, head, shared, gain, gain],
        [jax.ShapeDtypeStruct((t, HEADS * HP), BF16), jax.ShapeDtypeStruct((t, HEADS * HP), BF16),
         jax.ShapeDtypeStruct((t, 128), F32), jax.ShapeDtypeStruct((1, HP), F32),
         jax.ShapeDtypeStruct((1, HP), F32)],
        (q, kv, kr, cos, sin, gq, gk, dmix, d_out, lse),
        [pltpu.VMEM((seq, HP), F32), pltpu.VMEM((seq, HP), F32), pltpu.VMEM((seq, V_DIM), F32)], comm)


def _tn(a_list, b, tm, name, into=None, comm=None, after=None):
    t, n_out = b.shape
    widths = [a.shape[1] for a in a_list]
    tk = min(TK_DW, t)
    m, na, nk = sum(widths), len(a_list), t // tk
    assert na == 1 or tm == m

    def body(*refs):
        a_refs, b_ref, o_ref, acc_ref = refs[:na], refs[na], refs[-2], refs[-1]
        k = pl.program_id(1)

        @pl.when(k == 0)
        def _():
            acc_ref[...] = jnp.zeros_like(acc_ref)

        bb = b_ref[...].astype(BF16)
        m0 = 0
        for a_ref, w in zip(a_refs, widths):
            rows = slice(0, tm) if na == 1 else slice(m0, m0 + w)
            acc_ref[rows, :] += _dot(a_ref[...].astype(BF16), bb, 0, 0)
            m0 += w

        @pl.when(k == nk - 1)
        def _():
            o_ref[...] = acc_ref[...].astype(BF16).reshape(o_ref.shape)

    if na == 1:
        in_specs = [pl.BlockSpec((tk, tm), lambda i, k: (k, i))]
    else:
        in_specs = [pl.BlockSpec((tk, w), lambda i, k: (k, 0)) for w in widths]
    in_specs.append(pl.BlockSpec((tk, n_out), lambda i, k: (k, 0)))
    args = list(a_list) + [b]
    if into is None:
        out_spec = pl.BlockSpec((tm, n_out), lambda i, k: (i, 0))
        out_shape = jax.ShapeDtypeStruct((m, n_out), BF16)
        aliases = {}
    else:
        buf, n, off = into
        assert n_out == D and tm % n == 0 and off % n == 0 and (na == 1 or tm // n == N_DEV)
        idx = off // n
        out_spec = pl.BlockSpec((tm // n, n, D), lambda i, k: (i, idx, 0))
        out_shape = jax.ShapeDtypeStruct(buf.shape, BF16)
        in_specs.append(pl.BlockSpec(memory_space=pl.ANY))
        args.append(buf)
        aliases = {len(args) - 1: 0}
    if after is not None:
        in_specs.append(pl.BlockSpec(memory_space=pl.ANY))
        args.append(after)
    (res,), extra = _call(body, name, (m // tm, nk), in_specs, [out_spec], [out_shape], args,
                          [pltpu.VMEM((tm, n_out), F32)], comm, aliases)
    return (res, extra) if comm is not None else res


def _adamw(ws, gs, ms, vs, name, nblk=1):
    n = len(ws)
    c1 = 1.0 - B1 ** STEP
    c2 = 1.0 - B2 ** STEP

    def body(*refs):
        for a in range(n):
            w, g, m, v = (refs[k * n + a][...] for k in range(4))
            d_ref, m_ref, v_ref = (refs[(4 + k) * n + a] for k in range(3))
            m_new = B1 * m + (1.0 - B1) * g
            v_new = B2 * v + (1.0 - B2) * (g * g)
            d_ref[...] = -LR * ((m_new / c1) / (jnp.sqrt(v_new / c2) + ADAM_EPS) + WD * w)
            m_ref[...] = m_new
            v_ref[...] = v_new

    grid = (nblk,)
    assert all(w.shape[0] % nblk == 0 and (nblk == 1 or (w.shape[0] // nblk) % 8 == 0) for w in ws)
    specs = [pl.BlockSpec((w.shape[0] // nblk, w.shape[1]), lambda i: (i, 0)) for w in ws]
    outs, _ = _call(body, name, grid, specs * 4, specs * 3, [jax.ShapeDtypeStruct(w.shape, F32) for w in ws] * 3,
                    (*ws, *gs, *ms, *vs))
    return outs[:n], outs[n:2 * n], outs[2 * n:]


def _rows1024(a, rows):
    flat = a.reshape(-1, D)
    return jnp.pad(flat, ((0, rows - flat.shape[0]), (0, 0)))


def _pack_shards(even_w_in, even_w_out, odd_w_in, q_b, kv_b, odd_w_out, ffn_w_gate, ffn_w_up, ffn_w_down):
    gu =[jnp.concatenate([ffn_w_gate[layer].T, ffn_w_up[layer].T], axis=0) for layer in range(2)]
    mix1 = jnp.concatenate([jnp.pad(odd_w_in[0], ((0, 0), (0, D - ODD_IN))), odd_w_out[0],
                            _rows1024(q_b[0].T, N_QB), _rows1024(kv_b[0].T, N_KVB),
                            jnp.zeros((R_MIX1 - OFF_KVB - N_KVB, D), F32)], axis=0)
    return [c.astype(BF16) for c in (even_w_in[0].T, even_w_out[0], gu[0], ffn_w_down[0], mix1, gu[1], ffn_w_down[1])]


def _pad_heads(a):
    k = a.shape[1]
    return jnp.pad(a.reshape(HEADS, QK_DIM, k), ((0, 0), (0, HP - QK_DIM), (0, 0))).reshape(HEADS * HP, k)


def _small_pack(parts):
    flat = []
    for p in parts:
        v = p.reshape(-1)
        flat.append(jnp.pad(v, (0, (-v.shape[0]) % 1024)))
    return jnp.concatenate(flat).reshape(-1, 128)


def _small_unpack(buf, shapes):
    flat = buf.reshape(-1)
    out, off = [], 0
    for s in shapes:
        size = int(np.prod(s))
        out.append(flat[off:off + size].reshape(s))
        off += size + (-size) % 1024
    return out


def _step(x3d, positions, target3d, chunks, tile, where, mix_norm, ffn_norm, sg_ln_g, sg_w_s, sg_b_s,
          pool_w, q_norm, k_norm):
    bsz, seq, _ = x3d.shape
    t = bsz * seq
    x0 = x3d.reshape(t, D)
    target = target3d.reshape(t, D)
    my_ein, my_eout, my_gu0, my_d0, my_mix1, my_gu1, my_d1 = chunks

    lane = np.arange(128)
    inv_freq = np.where(lane < QK_ROPE, ROPE_THETA ** (-(2.0 * (lane % 32)) / QK_ROPE), 0.0)
    inv_freq = jnp.asarray(inv_freq.reshape(1, 128), F32)
    (cos, sin), (*w_mix0, tiles) = _rope_tables(positions.reshape(t, 1), inv_freq,
                                                _gather_comm([my_ein, my_eout, tile]))

    conv_w = tiles[:, 0:3, 0:64].transpose(1, 0, 2).reshape(3, SC_W)
    pool_scale = tiles[:, 3, 0:32].reshape(1, POOL_W)
    q_a_norm = tiles[:, 4, 0:48].reshape(1, Q_LORA)
    kv_a_norm = tiles[:, 5, 0:32].reshape(1, KV_LORA)
    ws = sg_w_s[0]
    bst = jnp.pad(sg_b_s[0].T, ((0, 0), (0, 128 - SG_HEADS)))
    cw = jnp.pad(conv_w, ((0, 8 - 3), (0, 0)))
    pw_bd = jax.scipy.linalg.block_diag(*[pool_w[0, g] for g in range(4)]).astype(BF16)
    gq = jnp.pad(q_norm * ATT_SCALE, ((0, 0), (0, HP - QK_DIM)))
    gk = jnp.pad(k_norm, ((0, 0), (0, HP - QK_DIM)))

    (x1, proj_e), (w_gu0,) = _even_fwd(x0, w_mix0, mix_norm[0:1], sg_ln_g, ws, bst, cw, seq, _gather_comm([my_gu0]))
    (g0, u0, act0), (w_d0, w_mix1) = _ffn_up(x1, w_gu0, ffn_norm[0:1], "ffn_up0", _gather_comm([my_d0, my_mix1]))
    (x2,), (w_d1,) = _ffn_down(x1, act0, w_d0, "ffn_down0", _gather_comm([my_d1]))
    qbt = _pad_heads(w_mix1[:, OFF_QB:OFF_QB + N_QB_USED, :].reshape(HEADS * QK_DIM, Q_LORA))
    kvbt = w_mix1[:, OFF_KVB:OFF_KVB + N_KVB, :].reshape(HEADS * HP, KV_LORA)
    (proj_o, q, kv, kr, c_out), _ = _odd_pre_fwd(x2, w_mix1, mix_norm[1:2], qbt, kvbt, q_a_norm, kv_a_norm,
                                                pw_bd, pool_scale, seq)
    (d_out, lse), (w_gu1,) = _attn_fwd(q, kv, kr, cos, sin, gq, gk, seq, _gather_comm([my_gu1]))
    x3, dy, g1, u1, loss_tile = _last_block_fwd(x2, c_out, d_out, w_mix1, w_gu1, w_d1, ffn_norm[1:2], target)

    def chunk(rows, padded=False):
        return jnp.zeros((N_DEV, rows, D), BF16) if padded else lax.empty((N_DEV, rows, D), BF16)

    (dx3, act1, dg1, du1, h3, dgam_f1, dmix_o), _ = _ffn_bwd(x3, g1, u1, dy, w_gu1, w_d1, ffn_norm[1:2], "ffn_bwd1",
                                                           None, w_mix1)
    gp_ffn1 = _tn([dg1], h3, 1408, "dw_gate1", (chunk(R_GU + N_FF), N_FF, OFF_GATE))
    gp_ffn1 = _tn([du1], h3, 1408, "dw_up1", (gp_ffn1, N_FF, OFF_UP))
    gp_ffn1 = _tn([act1], dy, 1408, "dw_down1", (gp_ffn1, N_FF, R_GU))

    gp_mix1, (ga_ffn1,) = _tn([c_out, d_out], dx3, D, "dw_oout", (chunk(R_MIX1, True), N_SQ, OFF_OOUT),
                              _pair_exchange_comm(gp_ffn1))
    pb_ffn1 = _rs_pair_sum(gp_ffn1, ga_ffn1, where, "rs_pair_sum_ffn1")
    (dq, dkv, dkr, dgq, dgk), (gb_ffn1,) = _attn_bwd(q, kv, kr, cos, sin, gq, gk, dmix_o, d_out, lse, seq,
                                                    _chip_exchange_comm(pb_ffn1))
    (dx2, dproj_o, h2, qn, kvn, dgam_m1, dqa, dkva, dpw_bd, dps) = _odd_pre_bwd(
        x2, proj_o, dx3, dmix_o, dq, dkv, dkr, w_mix1, mix_norm[1:2], qbt, kvbt, q_a_norm, kv_a_norm, pw_bd,
        pool_scale, seq)
    gp_mix1 = _tn([h2], dproj_o, D, "dw_oin", (gp_mix1, N_SQ, OFF_OIN))
    d_qbt = _tn([dq], qn, HEADS * HP, "dw_qb")
    d_qb_rows = d_qbt.reshape(HEADS, HP, Q_LORA)[:, :QK_DIM].reshape(N_DEV, N_QB_USED, D)
    d_kvb_rows = _tn([dkv], kvn, HEADS * HP, "dw_kvb").reshape(N_DEV, N_KVB, D)
    gp_mix1 = lax.dynamic_update_slice(gp_mix1, d_qb_rows, (0, OFF_QB, 0))
    gp_mix1 = lax.dynamic_update_slice(gp_mix1, d_kvb_rows, (0, OFF_KVB, 0))

    (dx1, act0, dg0, du0, h1, dgam_f0), (ga_mix1,) = _ffn_bwd(x1, g0, u0, dx2, w_gu0, w_d0, ffn_norm[0:1], "ffn_bwd0",
                                                             _pair_exchange_comm(gp_mix1))
    pb_mix1 = _rs_pair_sum(gp_mix1, ga_mix1, where, "rs_pair_sum_mix1")
    *open_mix1, started = _chip_exchange_start(pb_mix1, "mix1")
    gp_ffn0a = _tn([dg0], h1, 1408, "dw_gate0", (chunk(R_GU), N_FF, OFF_GATE), None, started)
    gp_ffn0a = _tn([du0], h1, 1408, "dw_up0", (gp_ffn0a, N_FF, OFF_UP))
    gp_ffn0b, (ga_ffn0a,) = _tn([act0], dx2, 1408, "dw_down0", (chunk(N_FF), N_FF, 0),
                                _pair_exchange_comm(gp_ffn0a))
    pb_ffn0a = _rs_pair_sum(gp_ffn0a, ga_ffn0a, where, "rs_pair_sum_ffn0a")
    *open_ffn0a, started = _chip_exchange_start(pb_ffn0a, "ffn0a")

    (dx0, dproj_e, mix_e, h0, dgam_m0, dws, dbc, dlng, dcw), _ = _even_bwd(
        x0, proj_e, dx1, w_mix0, mix_norm[0:1], sg_ln_g, ws, bst, cw + 0.0 * started[:, :1], seq)

    small = _small_pack([
        jnp.concatenate([dgam_m0, dgam_m1], 0), jnp.concatenate([dgam_f0, dgam_f1], 0), dlng,
        dws[None], dbc[:, :SG_HEADS].T[None], dcw[:3],
        jnp.stack([dpw_bd[g * POOL_GD:(g + 1) * POOL_GD, g * POOL_GD:(g + 1) * POOL_GD] for g in range(4)])[None],
        dps, dqa, dkva, dgq[:, :QK_DIM] * ATT_SCALE, dgk[:, :QK_DIM], loss_tile[0:1, 0:1]])
    gp_ein, (small_all, ga_ffn0b) = _tn([dproj_e], h0, 1280, "dw_ein", (chunk(N_EIN), N_EIN, 0),
                                        _both(_gather_comm([small]), _pair_exchange_comm(gp_ffn0b)))
    pb_ffn0b = _rs_pair_sum(gp_ffn0b, ga_ffn0b, where, "rs_pair_sum_ffn0b")
    *open_ffn0b, started = _chip_exchange_start(pb_ffn0b, "ffn0b")
    gp_eout, (ga_ein,) = _tn([mix_e], dx1, D, "dw_eout", (chunk(N_SQ), N_SQ, 0), _pair_exchange_comm(gp_ein),
                             started)
    pb_ein = _rs_pair_sum(gp_ein, ga_ein, where, "rs_pair_sum_ein")
    *open_ein, started = _chip_exchange_start(pb_ein, "ein")
    small_sum = _small_unpack(_sum_gathered(small_all), SMALL_SHAPES)
    in_flight = (open_ffn0a, open_ffn0b, open_mix1, open_ein)
    return dx0.reshape(bsz, seq, D), (pb_ffn1, gb_ffn1), in_flight, (gp_eout, started), small_sum


SMALL_SHAPES = [(2, D), (2, D), (1, SG_W), (1, SG_HEADS, 128, 128), (1, SG_HEADS, 128), (3, SC_W),
                (1, 4, POOL_GD, POOL_GD), (1, POOL_W), (1, Q_LORA), (1, KV_LORA), (1, QK_DIM), (1, QK_DIM), (1, 1)]


def kernel(x, positions, mix_norm, ffn_norm, even_w_in, sg_ln_g, sg_w_s, sg_b_s, sc_conv_w, even_w_out, odd_w_in, pool_w, pool_scale, q_a_norm, q_b, kv_a_norm, kv_b, q_norm, k_norm, odd_w_out, ffn_w_gate, ffn_w_up, ffn_w_down, loss_target, m_mix_norm, m_ffn_norm, m_even_w_in, m_sg_ln_g, m_sg_w_s, m_sg_b_s, m_sc_conv_w, m_even_w_out, m_odd_w_in, m_pool_w, m_pool_scale, m_q_a_norm, m_q_b, m_kv_a_norm, m_kv_b, m_q_norm, m_k_norm, m_odd_w_out, m_ffn_w_gate, m_ffn_w_up, m_ffn_w_down, v_mix_norm, v_ffn_norm, v_even_w_in, v_sg_ln_g, v_sg_w_s, v_sg_b_s, v_sc_conv_w, v_even_w_out, v_odd_w_in, v_pool_w, v_pool_scale, v_q_a_norm, v_q_b, v_kv_a_norm, v_kv_b, v_q_norm, v_k_norm, v_odd_w_out, v_ffn_w_gate, v_ffn_w_up, v_ffn_w_down):
    xi, yi, ci = _place()
    me = 4 * xi + 2 * yi + ci

    chunks = _pack_shards(even_w_in, even_w_out, odd_w_in, q_b, kv_b, odd_w_out, ffn_w_gate, ffn_w_up, ffn_w_down)

    def lane_pad(a):
        return jnp.pad(a, ((0, 0), (0, 128 - a.shape[1])))

    tile = jnp.concatenate([lane_pad(sc_conv_w[0]), lane_pad(pool_scale), lane_pad(q_a_norm), lane_pad(kv_a_norm),
                            jnp.zeros((2, 128), F32)], axis=0)
    chip = 2 * xi + yi
    where = jnp.stack([ci, chip, chip ^ 2, chip ^ 1, chip ^ 3]).astype(jnp.int32)
    grad_x, (pb_ffn1, gb_ffn1), in_flight, (gp_eout, started), tot = _step(
        x, positions, loss_target, chunks, tile, where, mix_norm, ffn_norm, sg_ln_g, sg_w_s, sg_b_s,
        pool_w, q_norm, k_norm)

    (ga_eout,) = _comm_alone(_pair_exchange_comm(gp_eout), "rs_pair_exchange_eout")
    pb_eout = _rs_pair_sum(gp_eout, ga_eout, where + (0.0 * started[0, :1]).astype(jnp.int32), "rs_pair_sum_eout")
    eout_sems, pb_eout, land_eout, started = _chip_exchange_start(pb_eout, "eout")
    *in_flight, open_ein = in_flight
    landed = [_chip_exchange_wait(*parts, started, tag) for parts, tag in zip(in_flight, ("ffn0a", "ffn0b", "mix1"))]
    gsh_ffn0a, gsh_ffn0b, gsh_mix1, gsh_ffn1 = _rs_final_sums(
        [pb for pb, _ in landed] + [pb_ffn1], [gb for _, gb in landed] + [gb_ffn1], "rs_final_sums", started)

    (g_mix, g_ffn, g_lng, g_ws, g_bs, g_cw_full, g_pw, g_ps_full, g_qa_full, g_kva_full, g_qn, g_kn, loss) = tot
    g_cw = lax.dynamic_slice_in_dim(g_cw_full, me * 64, 64, axis=1)[None]
    g_ps = lax.dynamic_slice_in_dim(g_ps_full, me * 32, 32, axis=1)
    g_qa = lax.dynamic_slice_in_dim(g_qa_full, me * 48, 48, axis=1)
    g_kva = lax.dynamic_slice_in_dim(g_kva_full, me * 32, 32, axis=1)

    def tr(a):
        return jnp.swapaxes(a, -1, -2)

    g_gate = tr(jnp.stack([gsh_ffn0a[OFF_GATE:OFF_GATE + N_FF], gsh_ffn1[OFF_GATE:OFF_GATE + N_FF]]))
    g_up = tr(jnp.stack([gsh_ffn0a[OFF_UP:OFF_UP + N_FF], gsh_ffn1[OFF_UP:OFF_UP + N_FF]]))
    g_down = jnp.stack([gsh_ffn0b, gsh_ffn1[R_GU:R_GU + N_FF]])
    g_oin = gsh_mix1[OFF_OIN:OFF_OIN + N_SQ, :ODD_IN][None]
    g_oout = gsh_mix1[OFF_OOUT:OFF_OOUT + N_SQ][None]
    g_qb = tr(gsh_mix1[OFF_QB:OFF_QB + N_QB_USED].reshape(1, 144, Q_LORA))
    g_kvb = tr(gsh_mix1[OFF_KVB:OFF_KVB + N_KVB].reshape(1, 192, KV_LORA))
    transposed = ("even_w_in", "odd_w_in", "q_b", "kv_b", "ffn_w_gate", "ffn_w_up")

    names = ("mix_norm", "ffn_norm", "even_w_in", "sg_ln_g", "sg_w_s", "sg_b_s", "sc_conv_w", "even_w_out",
             "odd_w_in", "pool_w", "pool_scale", "q_a_norm", "q_b", "kv_a_norm", "kv_b", "q_norm", "k_norm",
             "odd_w_out", "ffn_w_gate", "ffn_w_up", "ffn_w_down")
    grads = dict(mix_norm=g_mix, ffn_norm=g_ffn, sg_ln_g=g_lng, sg_w_s=g_ws, sg_b_s=g_bs,
                 sc_conv_w=g_cw, odd_w_in=g_oin, pool_w=g_pw, pool_scale=g_ps, q_a_norm=g_qa,
                 q_b=g_qb, kv_a_norm=g_kva, kv_b=g_kvb, q_norm=g_qn, k_norm=g_kn, odd_w_out=g_oout,
                 ffn_w_gate=g_gate, ffn_w_up=g_up, ffn_w_down=g_down)
    weights = dict(mix_norm=mix_norm, ffn_norm=ffn_norm, even_w_in=even_w_in, sg_ln_g=sg_ln_g, sg_w_s=sg_w_s,
                   sg_b_s=sg_b_s, sc_conv_w=sc_conv_w, even_w_out=even_w_out, odd_w_in=odd_w_in, pool_w=pool_w,
                   pool_scale=pool_scale, q_a_norm=q_a_norm, q_b=q_b, kv_a_norm=kv_a_norm, kv_b=kv_b, q_norm=q_norm,
                   k_norm=k_norm, odd_w_out=odd_w_out, ffn_w_gate=ffn_w_gate, ffn_w_up=ffn_w_up,
                   ffn_w_down=ffn_w_down)
    m_in = dict(mix_norm=m_mix_norm, ffn_norm=m_ffn_norm, even_w_in=m_even_w_in, sg_ln_g=m_sg_ln_g, sg_w_s=m_sg_w_s,
                sg_b_s=m_sg_b_s, sc_conv_w=m_sc_conv_w, even_w_out=m_even_w_out, odd_w_in=m_odd_w_in,
                pool_w=m_pool_w, pool_scale=m_pool_scale, q_a_norm=m_q_a_norm, q_b=m_q_b, kv_a_norm=m_kv_a_norm,
                kv_b=m_kv_b, q_norm=m_q_norm, k_norm=m_k_norm, odd_w_out=m_odd_w_out, ffn_w_gate=m_ffn_w_gate,
                ffn_w_up=m_ffn_w_up, ffn_w_down=m_ffn_w_down)
    v_in = dict(mix_norm=v_mix_norm, ffn_norm=v_ffn_norm, even_w_in=v_even_w_in, sg_ln_g=v_sg_ln_g, sg_w_s=v_sg_w_s,
                sg_b_s=v_sg_b_s, sc_conv_w=v_sc_conv_w, even_w_out=v_even_w_out, odd_w_in=v_odd_w_in,
                pool_w=v_pool_w, pool_scale=v_pool_scale, q_a_norm=v_q_a_norm, q_b=v_q_b, kv_a_norm=v_kv_a_norm,
                kv_b=v_kv_b, q_norm=v_q_norm, k_norm=v_k_norm, odd_w_out=v_odd_w_out, ffn_w_gate=v_ffn_w_gate,
                ffn_w_up=v_ffn_w_up, ffn_w_down=v_ffn_w_down)
    delta, new_m, new_v = {}, {}, {}

    def as2d(k, a):
        a = tr(a) if k in transposed else a
        return a.reshape(-1, a.shape[-1])

    def back(k, a):
        shape = weights[k].shape
        return tr(a.reshape(shape[:-2] + (shape[-1], shape[-2]))) if k in transposed else a.reshape(shape)

    def update(group, name, nblk=1):
        outs = _adamw([as2d(k, weights[k]) for k in group], [as2d(k, grads[k]) for k in group],
                      [as2d(k, m_in[k]) for k in group], [as2d(k, v_in[k]) for k in group], name, nblk)
        for i, k in enumerate(group):
            delta[k], new_m[k], new_v[k] = (back(k, o[i]) for o in outs)

    update(["ffn_w_gate", "ffn_w_up", "ffn_w_down"], "adamw_ffn", 4)
    update(["odd_w_in", "odd_w_out"], "adamw_mix", 2)
    update([k for k in names if k not in delta and k not in ("even_w_in", "even_w_out")], "adamw_small")

    pb_ein, gb_ein = _chip_exchange_wait(*open_ein, new_v["k_norm"], "ein")
    pb_eout, gb_eout = _chip_exchange_wait(eout_sems, pb_eout, land_eout, new_v["k_norm"], "eout")
    gsh_ein, gsh_eout = _rs_final_sums([pb_ein, pb_eout], [gb_ein, gb_eout], "rs_final_sums_even")
    grads["even_w_in"] = tr(gsh_ein[None])
    grads["even_w_out"] = gsh_eout[None]
    update(["even_w_in", "even_w_out"], "adamw_even", 2)

    return (loss.reshape(()), grad_x, *[grads[k] for k in names], *[delta[k] for k in names],
            *[new_m[k] for k in names], *[new_v[k] for k in names])
```

```python
import functools

import numpy as np
import jax
import jax.numpy as jnp
from jax import lax
from jax.experimental import pallas as pl
from jax.experimental.pallas import tpu as pltpu

F32 = jnp.float32
BF16 = jnp.bfloat16
MESH = pl.DeviceIdType.MESH

D = 1024
EPS = 1e-6
NEG_INF = -1e30
SG_HEADS, SG_HD, SG_W, SG_CHUNK = 4, 128, 512, 128
SC_W = 512
EVEN_IN = 2560
POOL_W = 256
POOL_GD = 64
Q_LORA, KV_LORA, QK_ROPE, QK_NOPE, V_DIM = 384, 256, 64, 128, 128
QK_DIM = QK_NOPE + QK_ROPE
HEADS = 6
HP = 256
ODD_IN = 960
D_FF = 2816
ROPE_THETA = 10000.0
ATT_SCALE = QK_DIM ** -0.5
LR, B1, B2, ADAM_EPS, WD, STEP = 0.001, 0.9, 0.999, 1e-08, 0.01, 10

N_DEV = 8
TB = 512
TB_FFN_BWD = 256
TK_DW = 1024
HALO = 16
VMEM_LIMIT = 56 * 1024 * 1024

N_EIN, N_FF, N_SQ = 320, 352, 128
OFF_GATE, OFF_UP, R_GU = 0, 352, 704
OFF_OIN, OFF_OOUT, OFF_QB, OFF_KVB, R_MIX1 = 0, 128, 256, 320, 384
N_QB, N_QB_USED, N_KVB = 64, 54, 48

INV_SQRT2 = 0.7071067811865476
INV_SQRT_2PI = 0.3989422804014327


def _dot(a, b, ca, cb):
    return lax.dot_general(a, b, (((ca,), (cb,)), ((), ())), preferred_element_type=F32)


def _cparams(n_axes=1):
    return pltpu.CompilerParams(dimension_semantics=("arbitrary",) * n_axes, vmem_limit_bytes=VMEM_LIMIT)


def _wspec(n, off):
    assert off % n == 0
    idx = off // n
    return pl.BlockSpec((N_DEV, n, D), lambda i: (0, idx, 0), pipeline_mode=pl.Buffered(1))


def _const_spec(shape):
    zeros = (0,) * len(shape)
    return pl.BlockSpec(shape, lambda *_: zeros)


class _Comm:
    def __init__(self, ins, out_shapes, sems, start, wait, mid=None):
        self.ins, self.out_shapes, self.sems, self.start, self.wait, self.mid = ins, out_shapes, sems, start, wait, mid


def _both(c1, c2):
    def split(f1, f2):
        def run(ins, outs, sems):
            f1(ins[:len(c1.ins)], outs[:len(c1.out_shapes)], sems[:len(c1.sems)])
            f2(ins[len(c1.ins):], outs[len(c1.out_shapes):], sems[len(c1.sems):])
        return run

    def nothing(ins, outs, sems):
        pass

    mid = None if c1.mid is None and c2.mid is None else split(c1.mid or nothing, c2.mid or nothing)
    return _Comm(c1.ins + c2.ins, c1.out_shapes + c2.out_shapes, c1.sems + c2.sems,
                 split(c1.start, c2.start), split(c1.wait, c2.wait), mid)


def _call(body, name, grid, in_specs, out_specs, out_shape, args, scratch_shapes=(), comm=None, aliases=None):
    n_axes = len(grid)
    aliases = aliases or {}
    if comm is None:
        res = pl.pallas_call(
            body, name=name, grid=grid, in_specs=list(in_specs), out_specs=list(out_specs),
            out_shape=list(out_shape), scratch_shapes=list(scratch_shapes), input_output_aliases=aliases,
            compiler_params=_cparams(n_axes))(*args)
        return list(res), []
    ni, no, ns = len(in_specs), len(out_specs), len(scratch_shapes)
    ci, co = len(comm.ins), len(comm.out_shapes)
    n_steps = int(np.prod(grid))

    def carrier(*refs):
        ins, cin = refs[:ni], refs[ni:ni + ci]
        outs, cout = refs[ni + ci:ni + ci + no], refs[ni + ci + no:ni + ci + no + co]
        scr, sems = refs[ni + ci + no + co:ni + ci + no + co + ns], refs[ni + ci + no + co + ns:]
        step = 0
        for a in range(n_axes):
            step = step * grid[a] + pl.program_id(a)

        @pl.when(step == 0)
        def _():
            comm.start(cin, cout, sems)

        body(*ins, *outs, *scr)

        if comm.mid is not None and n_steps >= 4:
            @pl.when(step == n_steps // 2)
            def _():
                comm.mid(cin, cout, sems)

        @pl.when(step == n_steps - 1)
        def _():
            if comm.mid is not None and n_steps < 4:
                comm.mid(cin, cout, sems)
            comm.wait(cin, cout, sems)

    any_spec = pl.BlockSpec(memory_space=pl.ANY)
    res = pl.pallas_call(
        carrier, name=name, grid=grid, in_specs=list(in_specs) + [any_spec] * ci,
        out_specs=list(out_specs) + [any_spec] * co, out_shape=list(out_shape) + list(comm.out_shapes),
        scratch_shapes=list(scratch_shapes) + list(comm.sems), input_output_aliases=aliases,
        compiler_params=_cparams(n_axes))(*args, *comm.ins)
    return list(res[:no]), list(res[no:])


def _comm_alone(comm, name):
    ci, co = len(comm.ins), len(comm.out_shapes)

    def body(*refs):
        cin, cout, sems = refs[:ci], refs[ci:ci + co], refs[ci + co:]
        comm.start(cin, cout, sems)
        if comm.mid is not None:
            comm.mid(cin, cout, sems)
        comm.wait(cin, cout, sems)

    any_spec = pl.BlockSpec(memory_space=pl.ANY)
    res = pl.pallas_call(
        body, name=name, out_shape=list(comm.out_shapes), in_specs=[any_spec] * ci, out_specs=[any_spec] * co,
        scratch_shapes=list(comm.sems))(*comm.ins)
    return list(res)


def _rms(x, g):
    r = lax.rsqrt(jnp.mean(x * x, axis=-1, keepdims=True) + EPS)
    return x * r * g, r


def _rms_bwd(x, r, g, dy):
    xh = x * r
    dxh = dy * g
    dx = r * (dxh - xh * jnp.mean(dxh * xh, axis=-1, keepdims=True))
    dg = jnp.sum(dy * xh, axis=0, keepdims=True)
    return dx, dg


def _gelu(x):
    return 0.5 * x * (1.0 + lax.erf(x * INV_SQRT2))


def _gelu_grad(x):
    return 0.5 * (1.0 + lax.erf(x * INV_SQRT2)) + x * jnp.exp(-0.5 * x * x) * INV_SQRT_2PI


def _shift_down(a, k):
    rows = lax.broadcasted_iota(jnp.int32, a.shape, 0)
    return jnp.where(rows >= k, pltpu.roll(a, k, 0), 0.0)


def _shift_up(a, k):
    n = a.shape[0]
    rows = lax.broadcasted_iota(jnp.int32, a.shape, 0)
    return jnp.where(rows < n - k, pltpu.roll(a, n - k, 0), 0.0)


def _tril_bf16(w):
    r = lax.broadcasted_iota(jnp.int32, w.shape, 0)
    c = lax.broadcasted_iota(jnp.int32, w.shape, 1)
    return jnp.where(r >= c, w, 0.0).astype(BF16)


def _ln_head(vh, g):
    mu = jnp.mean(vh, axis=-1, keepdims=True)
    xc = vh - mu
    rr = lax.rsqrt(jnp.mean(xc * xc, axis=-1, keepdims=True) + EPS)
    xh = xc * rr
    return xh * g, xh, rr


def _conv_fwd(z, tail, cw_ref):
    ext = jnp.concatenate([tail, z], axis=0)
    zs1 = _shift_down(ext, 1)[HALO:]
    zs2 = _shift_down(ext, 2)[HALO:]
    y = cw_ref[2:3, :] * z + cw_ref[1:2, :] * zs1 + cw_ref[0:1, :] * zs2
    return y, zs1, zs2


def _pool_cnt(shape, blk_in_seq):
    rows = lax.broadcasted_iota(jnp.int32, shape, 0)
    grp = lax.broadcasted_iota(jnp.int32, shape, 1) // POOL_GD
    win = jnp.where(grp == 0, 2, jnp.where(grp == 1, 4, jnp.where(grp == 2, 8, 16)))
    tpos = blk_in_seq * shape[0] + rows + 1
    return jnp.minimum(tpos, win).astype(F32), grp


def _pool_select(grp, s2, s4, s8, s16):
    return jnp.where(grp == 0, s2, jnp.where(grp == 1, s4, jnp.where(grp == 2, s8, s16)))


def _pool_fwd(z, tail, blk_in_seq):
    ext = jnp.concatenate([tail, z], axis=0)
    s2 = ext + _shift_down(ext, 1)
    s4 = s2 + _shift_down(s2, 2)
    s8 = s4 + _shift_down(s4, 4)
    s16 = s8 + _shift_down(s8, 8)
    cnt, grp = _pool_cnt(z.shape, blk_in_seq)
    sums = _pool_select(grp, s2[HALO:], s4[HALO:], s8[HALO:], s16[HALO:])
    return sums / cnt - z, cnt, grp


def _pool_bwd(dpooled, dpm, head, grp):
    n = dpm.shape[0]
    ext = jnp.concatenate([dpm, head], axis=0)
    u2 = ext + _shift_up(ext, 1)
    u4 = u2 + _shift_up(u2, 2)
    u8 = u4 + _shift_up(u4, 4)
    u16 = u8 + _shift_up(u8, 8)
    return _pool_select(grp, u2[:n], u4[:n], u8[:n], u16[:n]) - dpooled


def _lane_sums(a):
    return _dot(a.astype(BF16), jnp.ones((a.shape[1], a.shape[1]), BF16), 1, 0)


def _swap_halves(y1):
    src = lax.broadcasted_iota(jnp.int32, (128, 128), 0)
    dst = lax.broadcasted_iota(jnp.int32, (128, 128), 1)
    perm = jnp.where(((dst < 32) & (src == dst + 32)) | ((dst >= 32) & (dst < QK_ROPE) & (src == dst - 32)), 1.0, 0.0)
    return _dot(y1.astype(BF16), perm.astype(BF16), 1, 0)


def _rope(y1, c, s):
    return y1 * c + _swap_halves(y1) * s


def _rope_bwd(d1, c, s):
    return d1 * c + _swap_halves(d1 * s)


def _qk_prep(x, g, c, s):
    r = lax.rsqrt(_lane_sums(x * x) * (1.0 / QK_DIM) + EPS)
    y = x * r * g
    return jnp.concatenate([y[:, :128], _rope(y[:, 128:], c, s)], axis=1), r


def _qk_prep_bwd(dout, x, r, g, c, s):
    dy = jnp.concatenate([dout[:, :128], _rope_bwd(dout[:, 128:], c, s)], axis=1)
    xh = x * r
    dxh = dy * g
    dx = r * (dxh - xh * (_lane_sums(dxh * xh) * (1.0 / QK_DIM)))
    return dx, jnp.sum(dy * xh, axis=0, keepdims=True)


def _place():
    return lax.axis_index("x"), lax.axis_index("y"), lax.axis_index("c")


def _gather_comm(arrs):
    n = len(arrs)

    def halves(a):
        rows = arrs[a].shape[0]
        tile = 16 if arrs[a].dtype == BF16 else 8
        top = rows // 2 if rows % (2 * tile) == 0 else rows
        return (0, top), (top, rows - top)

    def plan(ins, outs, sems):
        send_sems, recv_sems, local_sems = sems
        x, y, c = _place()
        me, sib, xn, yn, dg = (x, y, c), (x, y, 1 - c), (1 - x, y, c), (x, 1 - y, c), (1 - x, 1 - y, c)

        def slot(a, dev, part=None):
            ref = outs[a].at[4 * dev[0] + 2 * dev[1] + dev[2]]
            return ref if part is None else ref.at[pl.ds(part[0], part[1])]

        def copy(a, k, block, to, src=None, part=None):
            return pltpu.make_async_remote_copy(
                src_ref=slot(a, block, part) if src is None else src, dst_ref=slot(a, block, part),
                send_sem=send_sems.at[a, k], recv_sem=recv_sems.at[a, k], device_id=to, device_id_type=MESH)

        local = [pltpu.make_async_copy(ins[a], slot(a, me), local_sems.at[a]) for a in range(n)]
        return me, sib, xn, yn, dg, copy, local

    def start(ins, outs, sems):
        me, sib, xn, yn, _, copy, local = plan(ins, outs, sems)
        for a in range(n):
            local[a].start()
            for k, to in enumerate((sib, xn, yn)):
                copy(a, k, me, to, src=ins[a]).start()

    def mid(ins, outs, sems):
        me, sib, xn, yn, _, copy, _ = plan(ins, outs, sems)
        for a in range(n):
            top, bottom = halves(a)
            copy(a, 1, xn, me).wait_recv()
            copy(a, 3, xn, yn, part=top).start()
            copy(a, 5, xn, sib).start()
            copy(a, 2, yn, me).wait_recv()
            if bottom[1]:
                copy(a, 4, yn, xn, part=bottom).start()
            copy(a, 6, yn, sib).start()

    def wait(ins, outs, sems):
        me, sib, xn, yn, dg, copy, local = plan(ins, outs, sems)
        other = lambda dev: (dev[0], dev[1], 1 - dev[2])
        for a in range(n):
            top, bottom = halves(a)
            copy(a, 3, dg, me, part=top).wait_recv()
            if bottom[1]:
                copy(a, 4, dg, me, part=bottom).wait_recv()
            copy(a, 7, dg, sib).start()
        for a in range(n):
            top, bottom = halves(a)
            for k, block in ((0, sib), (5, other(xn)), (6, other(yn)), (7, other(dg))):
                copy(a, k, block, me).wait_recv()
            for k, block in ((0, me), (1, me), (2, me), (5, xn), (6, yn), (7, dg)):
                copy(a, k, block, me, src=ins[a] if k < 3 else None).wait_send()
            copy(a, 3, xn, me, part=top).wait_send()
            if bottom[1]:
                copy(a, 4, yn, me, part=bottom).wait_send()
            local[a].wait()

    return _Comm(
        list(arrs), [jax.ShapeDtypeStruct((N_DEV,) + a.shape, a.dtype) for a in arrs],
        [pltpu.SemaphoreType.DMA((n, 8)), pltpu.SemaphoreType.DMA((n, 8)), pltpu.SemaphoreType.DMA((n,))],
        start, wait, mid)


def _sum_gathered(g):
    rows = g.shape[1]

    def body(g_ref, sum_ref):
        total = g_ref[0]
        for d in range(1, N_DEV):
            total = total + g_ref[d]
        sum_ref[...] = total

    return pl.pallas_call(
        body, name="sum_gathered_small", out_shape=jax.ShapeDtypeStruct((rows, 128), F32), grid=(1,),
        in_specs=[pl.BlockSpec((N_DEV, rows, 128), lambda i: (0, 0, 0))],
        out_specs=pl.BlockSpec((rows, 128), lambda i: (0, 0)), compiler_params=_cparams(1),
    )(g)


def _sum_rows(rows):
    return rows if rows <= 512 else rows // 2


def _pair_exchange_comm(gp):
    _, rows, cols = gp.shape

    def copies(ins, outs, sems):
        send_sems, recv_sems = sems
        x, y, c = _place()
        return [pltpu.make_async_remote_copy(
            src_ref=ins[0].at[2 * j + (1 - c)], dst_ref=outs[0].at[j], send_sem=send_sems.at[j],
            recv_sem=recv_sems.at[j], device_id=(x, y, 1 - c), device_id_type=MESH) for j in range(4)]

    def start(ins, outs, sems):
        for cp in copies(ins, outs, sems):
            cp.start()

    def wait(ins, outs, sems):
        for cp in copies(ins, outs, sems):
            cp.wait()

    return _Comm([gp], [jax.ShapeDtypeStruct((4, rows, cols), gp.dtype)],
                 [pltpu.SemaphoreType.DMA((4,)), pltpu.SemaphoreType.DMA((4,))], start, wait)


def _rs_pair_sum(gp, got, where, name):
    _, rows, cols = got.shape
    rb = _sum_rows(rows)
    gp4 = gp.reshape(4, 2, rows, cols)

    def body(w_ref, a_ref, b_ref, o_ref):
        o_ref[0] = (a_ref[0, 0].astype(F32) + b_ref[0].astype(F32)).astype(o_ref.dtype)

    return pl.pallas_call(
        body, name=name, out_shape=jax.ShapeDtypeStruct((4, rows, cols), gp.dtype),
        grid_spec=pltpu.PrefetchScalarGridSpec(
            num_scalar_prefetch=1, grid=(4, rows // rb),
            in_specs=[pl.BlockSpec((1, 1, rb, cols), lambda k, r, w: (w[1 + k], w[0], r, 0)),
                      pl.BlockSpec((1, rb, cols), lambda k, r, w: (w[1 + k], r, 0))],
            out_specs=pl.BlockSpec((1, rb, cols), lambda k, r, w: (k, r, 0))),
        compiler_params=_cparams(2),
    )(where, gp4, got)


def _chip_exchange_comm(pb):
    _, rows, cols = pb.shape

    def copies(ins, outs, sems):
        send_sems, recv_sems = sems
        x, y, c = _place()
        chips = [(1 - x, y), (x, 1 - y), (1 - x, 1 - y)]
        return [pltpu.make_async_remote_copy(
            src_ref=ins[0].at[1 + k], dst_ref=outs[0].at[k], send_sem=send_sems.at[k],
            recv_sem=recv_sems.at[k], device_id=(px, py, c), device_id_type=MESH)
            for k, (px, py) in enumerate(chips)]

    def start(ins, outs, sems):
        for cp in copies(ins, outs, sems):
            cp.start()

    def wait(ins, outs, sems):
        for cp in copies(ins, outs, sems):
            cp.wait()

    return _Comm([pb], [jax.ShapeDtypeStruct((3, rows, cols), pb.dtype)],
                 [pltpu.SemaphoreType.DMA((3,)), pltpu.SemaphoreType.DMA((3,))], start, wait)


def _chip_exchange_start(pb, tag):
    _, rows, cols = pb.shape

    def body(pb_ref, land_ref, *rest):
        sems, token = rest[:6], rest[8]
        x, y, c = _place()
        chips = [(1 - x, y), (x, 1 - y), (1 - x, 1 - y)]
        for k, (px, py) in enumerate(chips):
            pltpu.make_async_remote_copy(
                src_ref=pb_ref.at[1 + k], dst_ref=land_ref.at[k], send_sem=sems[k], recv_sem=sems[3 + k],
                device_id=(px, py, c), device_id_type=MESH).start()
        token[...] = jnp.zeros_like(token)

    hbm = pl.BlockSpec(memory_space=pltpu.HBM)
    sem = pl.BlockSpec(memory_space=pltpu.SEMAPHORE)
    land = lax.empty((3, rows, cols), pb.dtype)
    res = pl.pallas_call(
        body, name="rs_chip_exchange_start_" + tag,
        out_shape=(*[pltpu.SemaphoreType.DMA(())] * 6, pltpu.HBM(pb.shape, pb.dtype), pltpu.HBM(land.shape, land.dtype),
                   jax.ShapeDtypeStruct((8, 128), F32)),
        in_specs=(hbm, hbm), out_specs=(*[sem] * 6, hbm, hbm, pl.BlockSpec(memory_space=pltpu.VMEM)),
        input_output_aliases={0: 6, 1: 7},
        compiler_params=pltpu.CompilerParams(has_side_effects=pltpu.SideEffectType.DATAFLOW_SIDE_EFFECTING),
    )(pltpu.with_memory_space_constraint(pb, pltpu.HBM), pltpu.with_memory_space_constraint(land, pltpu.HBM))
    return list(res[:6]), res[6], res[7], res[8]


def _chip_exchange_wait(sems, pb_thru, land_thru, after, tag):
    def body(pb_ref, land_ref, *rest):
        sems_in = rest[:6]
        x, y, c = _place()
        chips = [(1 - x, y), (x, 1 - y), (1 - x, 1 - y)]
        for k, (px, py) in enumerate(chips):
            cp = pltpu.make_async_remote_copy(
                src_ref=pb_ref.at[1 + k], dst_ref=land_ref.at[k], send_sem=sems_in[k], recv_sem=sems_in[3 + k],
                device_id=(px, py, c), device_id_type=MESH)
            cp.wait_send()
            cp.wait_recv()

    hbm = pl.BlockSpec(memory_space=pltpu.HBM)
    sem = pl.BlockSpec(memory_space=pltpu.SEMAPHORE)
    res = pl.pallas_call(
        body, name="rs_chip_exchange_wait_" + tag,
        out_shape=(pltpu.HBM(pb_thru.shape, pb_thru.dtype), pltpu.HBM(land_thru.shape, land_thru.dtype)),
        in_specs=(hbm, hbm, *[sem] * 6, pl.BlockSpec(memory_space=pl.ANY)), out_specs=(hbm, hbm),
        input_output_aliases={0: 0, 1: 1},
        compiler_params=pltpu.CompilerParams(has_side_effects=pltpu.SideEffectType.DATAFLOW_SIDE_EFFECTING),
    )(pb_thru, land_thru, *sems, after)
    return res[0], res[1]


def _rs_final_sums(pbs, gots, name, after=None):
    n = len(pbs)

    def body(*refs):
        outs = refs[len(refs) - n:]
        for a in range(n):
            m_ref, g_ref, o_ref = refs[a], refs[n + a], outs[a]
            o_ref[...] = ((m_ref[0].astype(F32) + g_ref[0].astype(F32)) + g_ref[1].astype(F32)) + g_ref[2].astype(F32)

    half = [pb.shape[1] // 2 for pb in pbs]
    in_specs = ([pl.BlockSpec((1, h, D), lambda i: (0, i, 0)) for h in half]
                + [pl.BlockSpec((3, h, D), lambda i: (0, i, 0)) for h in half])
    args = (*pbs, *gots)
    if after is not None:
        in_specs, args = in_specs + [pl.BlockSpec(memory_space=pl.ANY)], args + (after,)
    res, _ = _call(body, name, (2,), in_specs, [pl.BlockSpec((h, D), lambda i: (i, 0)) for h in half],
                   [jax.ShapeDtypeStruct(pb.shape[1:], F32) for pb in pbs], args)
    return res


def _rope_tables(pos_col, inv_freq, comm=None):
    t = pos_col.shape[0]

    def body(p_ref, f_ref, c_ref, s_ref):
        ang = p_ref[...].astype(F32) * f_ref[...]
        lane = lax.broadcasted_iota(jnp.int32, ang.shape, 1)
        c_ref[...] = jnp.where(lane < QK_ROPE, jnp.cos(ang), 0.0)
        s = jnp.sin(ang)
        s_ref[...] = jnp.where(lane < 32, -s, jnp.where(lane < QK_ROPE, s, 0.0))

    spec = pl.BlockSpec((TB, 128), lambda i: (i, 0))
    return _call(
        body, "rope_tables", (t // TB,), [pl.BlockSpec((TB, 1), lambda i: (i, 0)), _const_spec((1, 128))],
        [spec] * 2, [jax.ShapeDtypeStruct((t, 128), F32)] * 2, (pos_col, inv_freq), (), comm)


def _sgu_conv_fwd(proj, tail, lng_ref, ws_ref, bst_ref, cw_ref):
    gu = _gelu(proj[:, 0:SG_W])
    gv = _gelu(proj[:, SG_W:2 * SG_W])
    bg = proj[:, 1024:1536]
    z = proj[:, 1536:2048] * proj[:, 2048:2560]
    heads = []
    for h in range(SG_HEADS):
        sl = slice(h * SG_HD, (h + 1) * SG_HD)
        vn, _, _ = _ln_head(gv[:, sl], lng_ref[:, sl])
        vnb = vn.astype(BF16)
        wm = _tril_bf16(ws_ref[h])
        bcol = bst_ref[:, h:h + 1]
        mixed = jnp.concatenate(
            [_dot(wm, vnb[k * SG_CHUNK:(k + 1) * SG_CHUNK], 1, 0) + bcol for k in range(TB // SG_CHUNK)], axis=0)
        heads.append(gu[:, sl] * mixed)
    a_out = jnp.concatenate(heads, axis=1)
    y, _, _ = _conv_fwd(z, tail, cw_ref)
    return a_out, bg * y, z


def _even_fwd(x, wg, gamma, lng, ws, bst, cw, seq, comm=None):
    t = x.shape[0]
    nbs = seq // TB

    def body(x_ref, gam_ref, win_ref, wout_ref, lng_ref, ws_ref, bst_ref, cw_ref, x1_ref, proj_ref, tail_ref):
        i = pl.program_id(0)
        xv = x_ref[...]
        h, _ = _rms(xv, gam_ref[...])
        proj = _dot(h.astype(BF16), win_ref[...].reshape(EVEN_IN, D), 1, 1)
        proj_ref[...] = proj.astype(BF16)
        tail = jnp.where(i % nbs == 0, 0.0, tail_ref[...])
        a_out, b_out, z = _sgu_conv_fwd(proj, tail, lng_ref, ws_ref, bst_ref, cw_ref)
        tail_ref[...] = z[TB - HALO:, :]
        x1_ref[...] = (xv + _dot(a_out.astype(BF16), wout_ref[0:4].reshape(512, D), 1, 0)
                       + _dot(b_out.astype(BF16), wout_ref[4:8].reshape(512, D), 1, 0))

    row = pl.BlockSpec((TB, D), lambda i: (i, 0))
    return _call(
        body, "even_fwd", (t // TB,),
        [row, _const_spec((1, D)), _wspec(N_EIN, 0), _wspec(N_SQ, 0), _const_spec((1, SG_W)),
         _const_spec((SG_HEADS, 128, 128)), _const_spec((128, 128)), _const_spec((8, SC_W))],
        [row, pl.BlockSpec((TB, EVEN_IN), lambda i: (i, 0))],
        [jax.ShapeDtypeStruct((t, D), F32), jax.ShapeDtypeStruct((t, EVEN_IN), BF16)],
        (x, gamma, *wg, lng, ws, bst, cw), [pltpu.VMEM((HALO, SC_W), F32)], comm)


def _even_bwd(x, proj, dx1, wg, gamma, lng, ws, bst, cw, seq, comm=None):
    t = x.shape[0]
    nb, nbs = t // TB, seq // TB

    def body(x_ref, proj_ref, ptail_ref, dx1_ref, gam_ref, win_ref, wout_ref, lng_ref, ws_ref, bst_ref, cw_ref,
             dx0_ref, dproj_ref, mix_ref, h_ref, dgam_ref, dws_ref, dbc_ref, dlng_ref, dcw_ref, head_ref):
        i = pl.program_id(0)
        blk = nb - 1 - i

        @pl.when(i == 0)
        def _():
            dgam_ref[...] = jnp.zeros_like(dgam_ref)
            dws_ref[...] = jnp.zeros_like(dws_ref)
            dbc_ref[...] = jnp.zeros_like(dbc_ref)
            dlng_ref[...] = jnp.zeros_like(dlng_ref)
            dcw_ref[...] = jnp.zeros_like(dcw_ref)

        xv = x_ref[...]
        gam = gam_ref[...]
        h, r = _rms(xv, gam)
        h_ref[...] = h.astype(BF16)
        dx1 = dx1_ref[...]
        dmix = _dot(dx1.astype(BF16), wout_ref[...].reshape(D, D), 1, 1)
        da, db = dmix[:, :SG_W], dmix[:, SG_W:]
        proj = proj_ref[...].astype(F32)
        u, v = proj[:, 0:SG_W], proj[:, SG_W:2 * SG_W]
        bg, cg, hv = proj[:, 1024:1536], proj[:, 1536:2048], proj[:, 2048:2560]
        gu, gv = _gelu(u), _gelu(v)

        a_heads, dgv_heads = [], []
        for hd in range(SG_HEADS):
            sl = slice(hd * SG_HD, (hd + 1) * SG_HD)
            g_h = lng_ref[:, sl]
            vn, xh, rr = _ln_head(gv[:, sl], g_h)
            vnb = vn.astype(BF16)
            wm = _tril_bf16(ws_ref[hd])
            bcol = bst_ref[:, hd:hd + 1]
            mixed_c, dvn_c = [], []
            dw_acc = jnp.zeros((128, 128), F32)
            db_acc = jnp.zeros((128, 1), F32)
            for k in range(TB // SG_CHUNK):
                rs = slice(k * SG_CHUNK, (k + 1) * SG_CHUNK)
                mixed = _dot(wm, vnb[rs], 1, 0) + bcol
                dmixed = da[rs, sl] * gu[rs, sl]
                dmb = dmixed.astype(BF16)
                dvn_c.append(_dot(wm, dmb, 0, 0))
                dw_acc = dw_acc + _dot(dmb, vnb[rs], 1, 1)
                db_acc = db_acc + jnp.sum(dmixed, axis=1, keepdims=True)
                mixed_c.append(mixed)
            mixed_h = jnp.concatenate(mixed_c, axis=0)
            dvn = jnp.concatenate(dvn_c, axis=0)
            r_i = lax.broadcasted_iota(jnp.int32, (128, 128), 0)
            c_i = lax.broadcasted_iota(jnp.int32, (128, 128), 1)
            dws_ref[hd] += jnp.where(r_i >= c_i, dw_acc, 0.0)
            dbc_ref[:, hd:hd + 1] += db_acc
            dlng_ref[:, sl] += jnp.sum(dvn * xh, axis=0, keepdims=True)
            dxh = dvn * g_h
            dgv = rr * (dxh - jnp.mean(dxh, axis=-1, keepdims=True)
                        - xh * jnp.mean(dxh * xh, axis=-1, keepdims=True))
            a_heads.append(gu[:, sl] * mixed_h)
            dproj_ref[:, sl] = (da[:, sl] * mixed_h * _gelu_grad(u[:, sl])).astype(BF16)
            dgv_heads.append(dgv * _gelu_grad(v[:, sl]))
        dproj_ref[:, SG_W:2 * SG_W] = jnp.concatenate(dgv_heads, axis=1).astype(BF16)
        mix_ref[:, :SG_W] = jnp.concatenate(a_heads, axis=1).astype(BF16)

        z = cg * hv
        pt = ptail_ref[...].astype(F32)
        tail = jnp.where(blk % nbs == 0, 0.0, pt[:, 1536:2048] * pt[:, 2048:2560])
        y, zs1, zs2 = _conv_fwd(z, tail, cw_ref)
        mix_ref[:, SG_W:] = (bg * y).astype(BF16)
        dy = db * bg
        head = jnp.where(blk % nbs == nbs - 1, 0.0, head_ref[...])
        ext = jnp.concatenate([dy, head], axis=0)
        dz = (cw_ref[2:3, :] * dy + cw_ref[1:2, :] * _shift_up(ext, 1)[:TB]
              + cw_ref[0:1, :] * _shift_up(ext, 2)[:TB])
        head_ref[...] = dy[:HALO, :]
        dcw_ref[2:3, :] += jnp.sum(dy * z, axis=0, keepdims=True)
        dcw_ref[1:2, :] += jnp.sum(dy * zs1, axis=0, keepdims=True)
        dcw_ref[0:1, :] += jnp.sum(dy * zs2, axis=0, keepdims=True)
        dproj_ref[:, 1024:1536] = (db * y).astype(BF16)
        dproj_ref[:, 1536:2048] = (dz * hv).astype(BF16)
        dproj_ref[:, 2048:2560] = (dz * cg).astype(BF16)

        dh = _dot(dproj_ref[...], win_ref[...].reshape(EVEN_IN, D), 1, 0)
        dxn, dgam = _rms_bwd(xv, r, gam, dh)
        dgam_ref[...] += dgam
        dx0_ref[...] = dx1 + dxn

    def rev(w):
        return pl.BlockSpec((TB, w), lambda i: (nb - 1 - i, 0))

    ptail = pl.BlockSpec((HALO, EVEN_IN), lambda i: (jnp.maximum((nb - 1 - i) * (TB // HALO) - 1, 0), 0))
    return _call(
        body, "even_bwd", (nb,),
        [rev(D), rev(EVEN_IN), ptail, rev(D), _const_spec((1, D)), _wspec(N_EIN, 0),
         _wspec(N_SQ, 0), _const_spec((1, SG_W)), _const_spec((SG_HEADS, 128, 128)),
         _const_spec((128, 128)), _const_spec((8, SC_W))],
        [rev(D), rev(EVEN_IN), rev(D), rev(D), _const_spec((1, D)), _const_spec((SG_HEADS, 128, 128)),
         _const_spec((128, 128)), _const_spec((1, SG_W)), _const_spec((8, SC_W))],
        [jax.ShapeDtypeStruct((t, D), F32), jax.ShapeDtypeStruct((t, EVEN_IN), BF16),
         jax.ShapeDtypeStruct((t, D), BF16), jax.ShapeDtypeStruct((t, D), BF16),
         jax.ShapeDtypeStruct((1, D), F32), jax.ShapeDtypeStruct((SG_HEADS, 128, 128), F32),
         jax.ShapeDtypeStruct((128, 128), F32), jax.ShapeDtypeStruct((1, SG_W), F32),
         jax.ShapeDtypeStruct((8, SC_W), F32)],
        (x, proj, proj, dx1, gamma, *wg, lng, ws, bst, cw), [pltpu.VMEM((HALO, SC_W), F32)], comm)


def _last_block_fwd(x, c_out, d_out, w_mix1, w_gu, w_d, gamma, target):
    t = x.shape[0]

    def body(x_ref, c_ref, d_ref, wo_ref, gam_ref, wg_ref, wu_ref, wd_ref, t_ref,
             x3_ref, dy_ref, g_ref, u_ref, loss_ref):
        @pl.when(pl.program_id(0) == 0)
        def _():
            loss_ref[...] = jnp.zeros_like(loss_ref)

        xv = (x_ref[...] + _dot(c_ref[...], wo_ref[0:2].reshape(POOL_W, D), 1, 0)
              + _dot(d_ref[...], wo_ref[2:8].reshape(HEADS * V_DIM, D), 1, 0))
        x3_ref[...] = xv
        h, _ = _rms(xv, gam_ref[...])
        hb = h.astype(BF16)
        g = _dot(hb, wg_ref[...].reshape(D_FF, D), 1, 1)
        u = _dot(hb, wu_ref[...].reshape(D_FF, D), 1, 1)
        g_ref[...] = g.astype(BF16)
        u_ref[...] = u.astype(BF16)
        act = g * jax.nn.sigmoid(g) * u
        err = xv + _dot(act.astype(BF16), wd_ref[...].reshape(D_FF, D), 1, 0) - t_ref[...]
        dy_ref[...] = err * (1.0 / D)
        sq = jnp.sum(jnp.sum(err * err, axis=-1, keepdims=True), axis=0, keepdims=True)
        loss_ref[...] += (0.5 / D) * sq

    def row(w):
        return pl.BlockSpec((TB, w), lambda i: (i, 0))

    res, _ = _call(
        body, "last_block_fwd", (t // TB,),
        [row(D), row(POOL_W), row(HEADS * V_DIM), _wspec(N_SQ, OFF_OOUT), _const_spec((1, D)),
         _wspec(N_FF, OFF_GATE), _wspec(N_FF, OFF_UP), _wspec(N_FF, 0), row(D)],
        [row(D), row(D), row(D_FF), row(D_FF), _const_spec((8, 128))],
        [jax.ShapeDtypeStruct((t, D), F32), jax.ShapeDtypeStruct((t, D), F32), jax.ShapeDtypeStruct((t, D_FF), BF16),
         jax.ShapeDtypeStruct((t, D_FF), BF16), jax.ShapeDtypeStruct((8, 128), F32)],
        (x, c_out, d_out, w_mix1, gamma, w_gu, w_gu, w_d, target))
    return res


def _ffn_up(x, w_gu, gamma, name, comm=None):
    t = x.shape[0]

    def body(x_ref, gam_ref, wg_ref, wu_ref, g_ref, u_ref, act_ref):
        h, _ = _rms(x_ref[...], gam_ref[...])
        hb = h.astype(BF16)
        g = _dot(hb, wg_ref[...].reshape(D_FF, D), 1, 1)
        u = _dot(hb, wu_ref[...].reshape(D_FF, D), 1, 1)
        g_ref[...] = g.astype(BF16)
        u_ref[...] = u.astype(BF16)
        act_ref[...] = (g * jax.nn.sigmoid(g) * u).astype(BF16)

    row = pl.BlockSpec((TB, D), lambda i: (i, 0))
    wide = pl.BlockSpec((TB, D_FF), lambda i: (i, 0))
    return _call(body, name, (t // TB,), [row, _const_spec((1, D)), _wspec(N_FF, OFF_GATE), _wspec(N_FF, OFF_UP)],
                 [wide, wide, wide], [jax.ShapeDtypeStruct((t, D_FF), BF16)] * 3, (x, gamma, w_gu, w_gu), (), comm)


def _ffn_down(x, act, w_d, name, comm=None):
    t = x.shape[0]

    def body(x_ref, a_ref, wd_ref, y_ref):
        y_ref[...] = x_ref[...] + _dot(a_ref[...], wd_ref[...].reshape(D_FF, D), 1, 0)

    row = pl.BlockSpec((TB, D), lambda i: (i, 0))
    wide = pl.BlockSpec((TB, D_FF), lambda i: (i, 0))
    return _call(body, name, (t // TB,), [row, wide, _wspec(N_FF, 0)], [row], [jax.ShapeDtypeStruct((t, D), F32)],
                 (x, act, w_d), (), comm)


def _ffn_bwd(x, g, u, dy, w_gu, w_d, gamma, name, comm=None, w_mix1=None):
    t = x.shape[0]
    with_dmix = w_mix1 is not None

    def body(*refs):
        x_ref, g_ref, u_ref, dy_ref, gam_ref, wg_ref, wu_ref, wd_ref = refs[:8]
        dx_ref, act_ref, dg_ref, du_ref, h_ref, dgam_ref = refs[8 + with_dmix:14 + with_dmix]

        @pl.when(pl.program_id(0) == 0)
        def _():
            dgam_ref[...] = jnp.zeros_like(dgam_ref)

        xv = x_ref[...]
        gam = gam_ref[...]
        h, r = _rms(xv, gam)
        h_ref[...] = h.astype(BF16)
        dyv = dy_ref[...]
        dact = _dot(dyv.astype(BF16), wd_ref[...].reshape(D_FF, D), 1, 1)
        gv = g_ref[...].astype(F32)
        uv = u_ref[...].astype(F32)
        sg = jax.nn.sigmoid(gv)
        silu = gv * sg
        act_ref[...] = (silu * uv).astype(BF16)
        dgb = (dact * uv * (sg * (1.0 + gv * (1.0 - sg)))).astype(BF16)
        dub = (dact * silu).astype(BF16)
        dg_ref[...] = dgb
        du_ref[...] = dub
        dh = _dot(dgb, wg_ref[...].reshape(D_FF, D), 1, 0) + _dot(dub, wu_ref[...].reshape(D_FF, D), 1, 0)
        dxn, dgam = _rms_bwd(xv, r, gam, dh)
        dgam_ref[...] += dgam
        dx = dyv + dxn
        dx_ref[...] = dx
        if with_dmix:
            refs[15][...] = _dot(dx.astype(BF16), refs[8][...].reshape(D, D), 1, 1).astype(BF16)

    row = pl.BlockSpec((TB_FFN_BWD, D), lambda i: (i, 0))
    wide = pl.BlockSpec((TB_FFN_BWD, D_FF), lambda i: (i, 0))
    in_specs = [row, wide, wide, row, _const_spec((1, D)), _wspec(N_FF, OFF_GATE), _wspec(N_FF, OFF_UP),
                _wspec(N_FF, 0)]
    out_specs = [row, wide, wide, wide, row, _const_spec((1, D))]
    out_shape = [jax.ShapeDtypeStruct((t, D), F32), jax.ShapeDtypeStruct((t, D_FF), BF16),
                 jax.ShapeDtypeStruct((t, D_FF), BF16), jax.ShapeDtypeStruct((t, D_FF), BF16),
                 jax.ShapeDtypeStruct((t, D), BF16), jax.ShapeDtypeStruct((1, D), F32)]
    args = (x, g, u, dy, gamma, w_gu, w_gu, w_d)
    if with_dmix:
        in_specs, args = in_specs + [_wspec(N_SQ, OFF_OOUT)], args + (w_mix1,)
        out_specs, out_shape = out_specs + [row], out_shape + [jax.ShapeDtypeStruct((t, D), BF16)]
    return _call(body, name, (t // TB_FFN_BWD,), in_specs, out_specs, out_shape, args, (), comm)


def _odd_pre_fwd(x, wg, gamma, qbt, kvbt, qa_g, kva_g, pw_bd, pscale, seq, comm=None):
    t = x.shape[0]
    nbs = seq // TB

    def body(x_ref, gam_ref, win_ref, qb_ref, kvb_ref, qa_ref, kva_ref, pw_ref, ps_ref,
             proj_ref, q_ref, kv_ref, kr_ref, c_ref, tail_ref):
        i = pl.program_id(0)
        h, _ = _rms(x_ref[...], gam_ref[...])
        proj = _dot(h.astype(BF16), win_ref[...].reshape(D, D), 1, 0)
        proj_ref[...] = proj.astype(BF16)
        zp, ql, kvl = proj[:, :POOL_W], proj[:, 256:640], proj[:, 640:896]
        kr_ref[...] = proj[:, 896:1024]
        qn, _ = _rms(ql, qa_ref[...])
        q_ref[...] = _dot(qn.astype(BF16), qb_ref[...], 1, 1).astype(BF16)
        kvn, _ = _rms(kvl, kva_ref[...])
        kv_ref[...] = _dot(kvn.astype(BF16), kvb_ref[...], 1, 1).astype(BF16)
        tail = jnp.where(i % nbs == 0, 0.0, tail_ref[...])
        pooled, _, _ = _pool_fwd(zp, tail, i % nbs)
        tail_ref[...] = zp[TB - HALO:, :]
        c_ref[...] = (_dot(pooled.astype(BF16), pw_ref[...], 1, 0) * ps_ref[...]).astype(BF16)

    def row(w):
        return pl.BlockSpec((TB, w), lambda i: (i, 0))

    return _call(
        body, "odd_pre_fwd", (t // TB,),
        [row(D), _const_spec((1, D)), _wspec(N_SQ, OFF_OIN), _const_spec((HEADS * HP, Q_LORA)),
         _const_spec((HEADS * HP, KV_LORA)), _const_spec((1, Q_LORA)), _const_spec((1, KV_LORA)),
         _const_spec((POOL_W, POOL_W)), _const_spec((1, POOL_W))],
        [row(D), row(HEADS * HP), row(HEADS * HP), row(128), row(POOL_W)],
        [jax.ShapeDtypeStruct((t, D), BF16), jax.ShapeDtypeStruct((t, HEADS * HP), BF16),
         jax.ShapeDtypeStruct((t, HEADS * HP), BF16), jax.ShapeDtypeStruct((t, 128), F32),
         jax.ShapeDtypeStruct((t, POOL_W), BF16)],
        (x, gamma, wg, qbt, kvbt, qa_g, kva_g, pw_bd, pscale), [pltpu.VMEM((HALO, POOL_W), F32)], comm)


def _odd_pre_bwd(x, proj, dx3, dmix, dq, dkv, dkr, wg, gamma, qbt, kvbt, qa_g, kva_g, pw_bd, pscale, seq):
    t = x.shape[0]
    nb, nbs = t // TB, seq // TB

    def body(x_ref, proj_ref, ptail_ref, dx3_ref, dco_ref, dq_ref, dkv_ref, dkr_ref, gam_ref, win_ref, qb_ref,
             kvb_ref, qa_ref, kva_ref, pw_ref, ps_ref,
             dx2_ref, dproj_ref, h_ref, qn_ref, kvn_ref, dgam_ref, dqa_ref, dkva_ref, dpw_ref, dps_ref, head_ref):
        i = pl.program_id(0)
        blk = nb - 1 - i

        @pl.when(i == 0)
        def _():
            dgam_ref[...] = jnp.zeros_like(dgam_ref)
            dqa_ref[...] = jnp.zeros_like(dqa_ref)
            dkva_ref[...] = jnp.zeros_like(dkva_ref)
            dpw_ref[...] = jnp.zeros_like(dpw_ref)
            dps_ref[...] = jnp.zeros_like(dps_ref)

        xv = x_ref[...]
        gam = gam_ref[...]
        h, r = _rms(xv, gam)
        h_ref[...] = h.astype(BF16)
        proj = proj_ref[...].astype(F32)
        zp, ql, kvl = proj[:, :POOL_W], proj[:, 256:640], proj[:, 640:896]

        qa = qa_ref[...]
        qn, rq = _rms(ql, qa)
        qn_ref[...] = qn.astype(BF16)
        dql, dqa = _rms_bwd(ql, rq, qa, _dot(dq_ref[...], qb_ref[...], 1, 0))
        dqa_ref[...] += dqa
        kva = kva_ref[...]
        kvn, rkv = _rms(kvl, kva)
        kvn_ref[...] = kvn.astype(BF16)
        dkvl, dkva = _rms_bwd(kvl, rkv, kva, _dot(dkv_ref[...], kvb_ref[...], 1, 0))
        dkva_ref[...] += dkva

        pt = ptail_ref[...].astype(F32)
        tail = jnp.where(blk % nbs == 0, 0.0, pt[:, :POOL_W])
        pooled, cnt, grp = _pool_fwd(zp, tail, blk % nbs)
        pb = pooled.astype(BF16)
        pw = pw_ref[...]
        dco = dco_ref[...].astype(F32)
        dps_ref[...] += jnp.sum(dco * _dot(pb, pw, 1, 0), axis=0, keepdims=True)
        dpo = (dco * ps_ref[...]).astype(BF16)
        dpw_ref[...] += _dot(pb, dpo, 0, 0)
        dpooled = _dot(dpo, pw, 1, 1)
        dpm = dpooled / cnt
        head = jnp.where(blk % nbs == nbs - 1, 0.0, head_ref[...])
        dz = _pool_bwd(dpooled, dpm, head, grp)
        head_ref[...] = dpm[:HALO, :]

        dproj_ref[:, :POOL_W] = dz.astype(BF16)
        dproj_ref[:, 256:640] = dql.astype(BF16)
        dproj_ref[:, 640:896] = dkvl.astype(BF16)
        dproj_ref[:, 896:1024] = dkr_ref[...].astype(BF16)
        dh = _dot(dproj_ref[...], win_ref[...].reshape(D, D), 1, 1)
        dxn, dgam = _rms_bwd(xv, r, gam, dh)
        dgam_ref[...] += dgam
        dx2_ref[...] = dx3_ref[...] + dxn

    def rev(w):
        return pl.BlockSpec((TB, w), lambda i: (nb - 1 - i, 0))

    ptail = pl.BlockSpec((HALO, D), lambda i: (jnp.maximum((nb - 1 - i) * (TB // HALO) - 1, 0), 0))
    return pl.pallas_call(
        body, name="odd_pre_bwd",
        out_shape=[jax.ShapeDtypeStruct((t, D), F32), jax.ShapeDtypeStruct((t, D), BF16),
                   jax.ShapeDtypeStruct((t, D), BF16), jax.ShapeDtypeStruct((t, Q_LORA), BF16),
                   jax.ShapeDtypeStruct((t, KV_LORA), BF16), jax.ShapeDtypeStruct((1, D), F32),
                   jax.ShapeDtypeStruct((1, Q_LORA), F32), jax.ShapeDtypeStruct((1, KV_LORA), F32),
                   jax.ShapeDtypeStruct((POOL_W, POOL_W), F32), jax.ShapeDtypeStruct((1, POOL_W), F32)],
        grid=(nb,),
        in_specs=[rev(D), rev(D), ptail, rev(D), rev(POOL_W), rev(HEADS * HP), rev(HEADS * HP), rev(128),
                  _const_spec((1, D)), _wspec(N_SQ, OFF_OIN), _const_spec((HEADS * HP, Q_LORA)),
                  _const_spec((HEADS * HP, KV_LORA)), _const_spec((1, Q_LORA)), _const_spec((1, KV_LORA)),
                  _const_spec((POOL_W, POOL_W)), _const_spec((1, POOL_W))],
        out_specs=[rev(D), rev(D), rev(D), rev(Q_LORA), rev(KV_LORA), _const_spec((1, D)), _const_spec((1, Q_LORA)),
                   _const_spec((1, KV_LORA)), _const_spec((POOL_W, POOL_W)), _const_spec((1, POOL_W))],
        scratch_shapes=[pltpu.VMEM((HALO, POOL_W), F32)],
        compiler_params=_cparams(1),
    )(x, proj, proj, dx3, dmix, dq, dkv, dkr, gamma, wg, qbt, kvbt, qa_g, kva_g, pw_bd, pscale)


def _attn_specs(seq):
    head = pl.BlockSpec((seq, HP), lambda b, h: (b, h))
    shared = pl.BlockSpec((seq, 128), lambda b, h: (b, 0))
    gain = pl.BlockSpec((1, HP), lambda b, h: (0, 0))
    return head, shared, gain


def _causal_bias(n):
    rows = lax.broadcasted_iota(jnp.int32, (n, n), 0)
    cols = lax.broadcasted_iota(jnp.int32, (n, n), 1)
    return jnp.where(cols <= rows, 0.0, NEG_INF)


def _attn_fwd(q, kv, kr, cos, sin, gq, gk, seq, comm=None):
    t = q.shape[0]
    qb = min(512, seq)

    def body(q_ref, kv_ref, kr_ref, c_ref, s_ref, gq_ref, gk_ref, o_ref, lse_ref):
        c, s = c_ref[...], s_ref[...]
        qf, _ = _qk_prep(q_ref[...].astype(F32), gq_ref[...], c, s)
        kin = jnp.concatenate([kv_ref[:, :128].astype(F32), kr_ref[...]], axis=1)
        kf, _ = _qk_prep(kin, gk_ref[...], c, s)
        qf, kf = qf.astype(BF16), kf.astype(BF16)
        v1 = jnp.concatenate([kv_ref[:, 128:], jnp.ones((seq, V_DIM), BF16)], axis=1)
        bias = _causal_bias(qb)
        for q0 in range(0, seq, qb):
            q1 = q0 + qb
            qblk = qf[q0:q1]
            s_dg = _dot(qblk, kf[q0:q1], 1, 1) + bias
            m = jnp.max(s_dg, axis=-1, keepdims=True)
            if q0:
                s_off = _dot(qblk, kf[:q0], 1, 1)
                m = jnp.maximum(m, jnp.max(s_off, axis=-1, keepdims=True))
            acc = _dot(jnp.exp(s_dg - m).astype(BF16), v1[q0:q1], 1, 0)
            if q0:
                acc = acc + _dot(jnp.exp(s_off - m).astype(BF16), v1[:q0], 1, 0)
            l = acc[:, V_DIM:]
            o_ref[q0:q1, :] = (acc[:, :V_DIM] / l).astype(BF16)
            lse_ref[q0:q1, :] = m + jnp.log(l)

    head, shared, gain = _attn_specs(seq)
    per_head = pl.BlockSpec((seq, V_DIM), lambda b, h: (b, h))
    return _call(
        body, "attn_fwd", (t // seq, HEADS),
        [head, head, shared, shared, shared, gain, gain], [per_head, per_head],
        [jax.ShapeDtypeStruct((t, HEADS * V_DIM), BF16), jax.ShapeDtypeStruct((t, HEADS * V_DIM), F32)],
        (q, kv, kr, cos, sin, gq, gk), (), comm)


def _attn_bwd(q, kv, kr, cos, sin, gq, gk, dmix, d_out, lse, seq, comm=None):
    t = q.shape[0]
    qb = min(512, seq)

    def body(q_ref, kv_ref, kr_ref, c_ref, s_ref, gq_ref, gk_ref, do_ref, o_ref, lse_ref,
             dq_ref, dkv_ref, dkr_ref, dgq_ref, dgk_ref, dqf_ref, dkf_ref, dv_ref):
        b, hd = pl.program_id(0), pl.program_id(1)

        @pl.when((b == 0) & (hd == 0))
        def _():
            dgq_ref[...] = jnp.zeros_like(dgq_ref)
            dgk_ref[...] = jnp.zeros_like(dgk_ref)

        c, sn = c_ref[...], s_ref[...]
        gq_v, gk_v = gq_ref[...], gk_ref[...]
        qin = q_ref[...].astype(F32)
        kin = jnp.concatenate([kv_ref[:, :128].astype(F32), kr_ref[...]], axis=1)
        qf32, rq = _qk_prep(qin, gq_v, c, sn)
        kf32, rk = _qk_prep(kin, gk_v, c, sn)
        qf, kf = qf32.astype(BF16), kf32.astype(BF16)
        vb = kv_ref[:, 128:]
        dkf_ref[...] = jnp.zeros_like(dkf_ref)
        dv_ref[...] = jnp.zeros_like(dv_ref)
        bias = _causal_bias(qb)
        for q0 in range(0, seq, qb):
            q1 = q0 + qb
            qblk = qf[q0:q1]
            do = do_ref[q0:q1, :]
            lse_col = lse_ref[q0:q1, 0:1]
            d_col = jnp.sum(do.astype(F32) * o_ref[q0:q1, :].astype(F32), axis=-1, keepdims=True)
            dq_acc = None
            for k0, k1, diag in ((q0, q1, True), (0, q0, False)):
                if k1 == k0:
                    continue
                s = _dot(qblk, kf[k0:k1], 1, 1)
                p = jnp.exp((s + bias if diag else s) - lse_col)
                dv_ref[k0:k1, :] += _dot(p.astype(BF16), do, 0, 0)
                ds = (p * (_dot(do, vb[k0:k1], 1, 1) - d_col)).astype(BF16)
                part = _dot(ds, kf[k0:k1], 1, 0)
                dq_acc = part if dq_acc is None else dq_acc + part
                dkf_ref[k0:k1, :] += _dot(ds, qblk, 0, 0)
            dqf_ref[q0:q1, :] = dq_acc
        dqin, dgq = _qk_prep_bwd(dqf_ref[...], qin, rq, gq_v, c, sn)
        dkin, dgk = _qk_prep_bwd(dkf_ref[...], kin, rk, gk_v, c, sn)
        dgq_ref[...] += dgq
        dgk_ref[...] += dgk
        dq_ref[...] = dqin.astype(BF16)
        dkv_ref[:, :128] = dkin[:, :128].astype(BF16)
        dkv_ref[:, 128:] = dv_ref[...].astype(BF16)

        @pl.when(hd == 0)
        def _():
            dkr_ref[...] = dkin[:, 128:]

        @pl.when(hd != 0)
        def _():
            dkr_ref[...] += dkin[:, 128:]

    head, shared, gain = _attn_specs(seq)
    per_head = pl.BlockSpec((seq, V_DIM), lambda b, h: (b, h))
    return _call(
        body, "attn_bwd", (t // seq, HEADS),
        [head, head, shared, shared, shared, gain, gain,
         pl.BlockSpec((seq, V_DIM), lambda b, h: (b, 2 + h)), per_head, per_head],
        [head, head, shared, gain, gain],
        [jax.ShapeDtypeStruct((t, HEADS * HP), BF16), jax.ShapeDtypeStruct((t, HEADS * HP), BF16),
         jax.ShapeDtypeStruct((t, 128), F32), jax.ShapeDtypeStruct((1, HP), F32),
         jax.ShapeDtypeStruct((1, HP), F32)],
        (q, kv, kr, cos, sin, gq, gk, dmix, d_out, lse),
        [pltpu.VMEM((seq, HP), F32), pltpu.VMEM((seq, HP), F32), pltpu.VMEM((seq, V_DIM), F32)], comm)


def _tn(a_list, b, tm, name, into=None, comm=None, after=None):
    t, n_out = b.shape
    widths = [a.shape[1] for a in a_list]
    tk = min(TK_DW, t)
    m, na, nk = sum(widths), len(a_list), t // tk
    assert na == 1 or tm == m

    def body(*refs):
        a_refs, b_ref, o_ref, acc_ref = refs[:na], refs[na], refs[-2], refs[-1]
        k = pl.program_id(1)

        @pl.when(k == 0)
        def _():
            acc_ref[...] = jnp.zeros_like(acc_ref)

        bb = b_ref[...].astype(BF16)
        m0 = 0
        for a_ref, w in zip(a_refs, widths):
            rows = slice(0, tm) if na == 1 else slice(m0, m0 + w)
            acc_ref[rows, :] += _dot(a_ref[...].astype(BF16), bb, 0, 0)
            m0 += w

        @pl.when(k == nk - 1)
        def _():
            o_ref[...] = acc_ref[...].astype(BF16).reshape(o_ref.shape)

    if na == 1:
        in_specs = [pl.BlockSpec((tk, tm), lambda i, k: (k, i))]
    else:
        in_specs = [pl.BlockSpec((tk, w), lambda i, k: (k, 0)) for w in widths]
    in_specs.append(pl.BlockSpec((tk, n_out), lambda i, k: (k, 0)))
    args = list(a_list) + [b]
    if into is None:
        out_spec = pl.BlockSpec((tm, n_out), lambda i, k: (i, 0))
        out_shape = jax.ShapeDtypeStruct((m, n_out), BF16)
        aliases = {}
    else:
        buf, n, off = into
        assert n_out == D and tm % n == 0 and off % n == 0 and (na == 1 or tm // n == N_DEV)
        idx = off // n
        out_spec = pl.BlockSpec((tm // n, n, D), lambda i, k: (i, idx, 0))
        out_shape = jax.ShapeDtypeStruct(buf.shape, BF16)
        in_specs.append(pl.BlockSpec(memory_space=pl.ANY))
        args.append(buf)
        aliases = {len(args) - 1: 0}
    if after is not None:
        in_specs.append(pl.BlockSpec(memory_space=pl.ANY))
        args.append(after)
    (res,), extra = _call(body, name, (m // tm, nk), in_specs, [out_spec], [out_shape], args,
                          [pltpu.VMEM((tm, n_out), F32)], comm, aliases)
    return (res, extra) if comm is not None else res


def _adamw(ws, gs, ms, vs, name, nblk=1):
    n = len(ws)
    c1 = 1.0 - B1 ** STEP
    c2 = 1.0 - B2 ** STEP

    def body(*refs):
        for a in range(n):
            w, g, m, v = (refs[k * n + a][...] for k in range(4))
            d_ref, m_ref, v_ref = (refs[(4 + k) * n + a] for k in range(3))
            m_new = B1 * m + (1.0 - B1) * g
            v_new = B2 * v + (1.0 - B2) * (g * g)
            d_ref[...] = -LR * ((m_new / c1) / (jnp.sqrt(v_new / c2) + ADAM_EPS) + WD * w)
            m_ref[...] = m_new
            v_ref[...] = v_new

    grid = (nblk,)
    assert all(w.shape[0] % nblk == 0 and (nblk == 1 or (w.shape[0] // nblk) % 8 == 0) for w in ws)
    specs = [pl.BlockSpec((w.shape[0] // nblk, w.shape[1]), lambda i: (i, 0)) for w in ws]
    outs, _ = _call(body, name, grid, specs * 4, specs * 3, [jax.ShapeDtypeStruct(w.shape, F32) for w in ws] * 3,
                    (*ws, *gs, *ms, *vs))
    return outs[:n], outs[n:2 * n], outs[2 * n:]


def _adamw_summing(ws, pbs, gbs, ms, vs, name, nblk):
    n = len(ws)
    c1 = 1.0 - B1 ** STEP
    c2 = 1.0 - B2 ** STEP

    def body(*refs):
        for a in range(n):
            w, m, v = refs[a][...], refs[3 * n + a][...], refs[4 * n + a][...]
            pb_ref, gb_ref = refs[n + a], refs[2 * n + a]
            g_ref, d_ref, m_ref, v_ref = (refs[(5 + k) * n + a] for k in range(4))
            g = ((pb_ref[0].astype(F32) + gb_ref[0].astype(F32)) + gb_ref[1].astype(F32)) + gb_ref[2].astype(F32)
            m_new = B1 * m + (1.0 - B1) * g
            v_new = B2 * v + (1.0 - B2) * (g * g)
            g_ref[...] = g
            d_ref[...] = -LR * ((m_new / c1) / (jnp.sqrt(v_new / c2) + ADAM_EPS) + WD * w)
            m_ref[...] = m_new
            v_ref[...] = v_new

    assert all(w.shape[1] == D and w.shape[0] % (16 * nblk) == 0 for w in ws)
    rows = [w.shape[0] // nblk for w in ws]
    flat = [pl.BlockSpec((r, D), lambda i: (i, 0)) for r in rows]
    mine = [pl.BlockSpec((1, r, D), lambda i: (0, i, 0)) for r in rows]
    theirs = [pl.BlockSpec((3, r, D), lambda i: (0, i, 0)) for r in rows]
    outs, _ = _call(body, name, (nblk,), flat + mine + theirs + flat * 2, flat * 4,
                    [jax.ShapeDtypeStruct(w.shape, F32) for w in ws] * 4, (*ws, *pbs, *gbs, *ms, *vs))
    return outs[:n], outs[n:2 * n], outs[2 * n:3 * n], outs[3 * n:]


def _rows1024(a, rows):
    flat = a.reshape(-1, D)
    return jnp.pad(flat, ((0, rows - flat.shape[0]), (0, 0)))


def _pack_shards(even_w_in, even_w_out, odd_w_in, q_b, kv_b, odd_w_out, ffn_w_gate, ffn_w_up, ffn_w_down):
    gu =[jnp.concatenate([ffn_w_gate[layer].T, ffn_w_up[layer].T], axis=0) for layer in range(2)]
    mix1 = jnp.concatenate([jnp.pad(odd_w_in[0], ((0, 0), (0, D - ODD_IN))), odd_w_out[0],
                            _rows1024(q_b[0].T, N_QB), _rows1024(kv_b[0].T, N_KVB),
                            jnp.zeros((R_MIX1 - OFF_KVB - N_KVB, D), F32)], axis=0)
    return [c.astype(BF16) for c in (even_w_in[0].T, even_w_out[0], gu[0], ffn_w_down[0], mix1, gu[1], ffn_w_down[1])]


def _pad_heads(a):
    k = a.shape[1]
    return jnp.pad(a.reshape(HEADS, QK_DIM, k), ((0, 0), (0, HP - QK_DIM), (0, 0))).reshape(HEADS * HP, k)


def _small_pack(parts):
    flat = []
    for p in parts:
        v = p.reshape(-1)
        flat.append(jnp.pad(v, (0, (-v.shape[0]) % 1024)))
    return jnp.concatenate(flat).reshape(-1, 128)


def _small_unpack(buf, shapes):
    flat = buf.reshape(-1)
    out, off = [], 0
    for s in shapes:
        size = int(np.prod(s))
        out.append(flat[off:off + size].reshape(s))
        off += size + (-size) % 1024
    return out


def _step(x3d, positions, target3d, chunks, tile, where, mix_norm, ffn_norm, sg_ln_g, sg_w_s, sg_b_s,
          pool_w, q_norm, k_norm):
    bsz, seq, _ = x3d.shape
    t = bsz * seq
    x0 = x3d.reshape(t, D)
    target = target3d.reshape(t, D)
    my_ein, my_eout, my_gu0, my_d0, my_mix1, my_gu1, my_d1 = chunks

    lane = np.arange(128)
    inv_freq = np.where(lane < QK_ROPE, ROPE_THETA ** (-(2.0 * (lane % 32)) / QK_ROPE), 0.0)
    inv_freq = jnp.asarray(inv_freq.reshape(1, 128), F32)
    (cos, sin), (*w_mix0, tiles) = _rope_tables(positions.reshape(t, 1), inv_freq,
                                                _gather_comm([my_ein, my_eout, tile]))

    conv_w = tiles[:, 0:3, 0:64].transpose(1, 0, 2).reshape(3, SC_W)
    pool_scale = tiles[:, 3, 0:32].reshape(1, POOL_W)
    q_a_norm = tiles[:, 4, 0:48].reshape(1, Q_LORA)
    kv_a_norm = tiles[:, 5, 0:32].reshape(1, KV_LORA)
    ws = sg_w_s[0]
    bst = jnp.pad(sg_b_s[0].T, ((0, 0), (0, 128 - SG_HEADS)))
    cw = jnp.pad(conv_w, ((0, 8 - 3), (0, 0)))
    pw_bd = jax.scipy.linalg.block_diag(*[pool_w[0, g] for g in range(4)]).astype(BF16)
    gq = jnp.pad(q_norm * ATT_SCALE, ((0, 0), (0, HP - QK_DIM)))
    gk = jnp.pad(k_norm, ((0, 0), (0, HP - QK_DIM)))

    (x1, proj_e), (w_gu0,) = _even_fwd(x0, w_mix0, mix_norm[0:1], sg_ln_g, ws, bst, cw, seq, _gather_comm([my_gu0]))
    (g0, u0, act0), (w_d0, w_mix1) = _ffn_up(x1, w_gu0, ffn_norm[0:1], "ffn_up0", _gather_comm([my_d0, my_mix1]))
    (x2,), (w_d1,) = _ffn_down(x1, act0, w_d0, "ffn_down0", _gather_comm([my_d1]))
    qbt = _pad_heads(w_mix1[:, OFF_QB:OFF_QB + N_QB_USED, :].reshape(HEADS * QK_DIM, Q_LORA))
    kvbt = w_mix1[:, OFF_KVB:OFF_KVB + N_KVB, :].reshape(HEADS * HP, KV_LORA)
    (proj_o, q, kv, kr, c_out), _ = _odd_pre_fwd(x2, w_mix1, mix_norm[1:2], qbt, kvbt, q_a_norm, kv_a_norm,
                                                pw_bd, pool_scale, seq)
    (d_out, lse), (w_gu1,) = _attn_fwd(q, kv, kr, cos, sin, gq, gk, seq, _gather_comm([my_gu1]))
    x3, dy, g1, u1, loss_tile = _last_block_fwd(x2, c_out, d_out, w_mix1, w_gu1, w_d1, ffn_norm[1:2], target)

    def chunk(rows, padded=False):
        return jnp.zeros((N_DEV, rows, D), BF16) if padded else lax.empty((N_DEV, rows, D), BF16)

    (dx3, act1, dg1, du1, h3, dgam_f1, dmix_o), _ = _ffn_bwd(x3, g1, u1, dy, w_gu1, w_d1, ffn_norm[1:2], "ffn_bwd1",
                                                           None, w_mix1)
    gp_ffn1 = _tn([dg1], h3, 1408, "dw_gate1", (chunk(R_GU + N_FF), N_FF, OFF_GATE))
    gp_ffn1 = _tn([du1], h3, 1408, "dw_up1", (gp_ffn1, N_FF, OFF_UP))
    gp_ffn1 = _tn([act1], dy, 1408, "dw_down1", (gp_ffn1, N_FF, R_GU))

    gp_mix1, (ga_ffn1,) = _tn([c_out, d_out], dx3, D, "dw_oout", (chunk(R_MIX1, True), N_SQ, OFF_OOUT),
                              _pair_exchange_comm(gp_ffn1))
    pb_ffn1 = _rs_pair_sum(gp_ffn1, ga_ffn1, where, "rs_pair_sum_ffn1")
    (dq, dkv, dkr, dgq, dgk), (gb_ffn1,) = _attn_bwd(q, kv, kr, cos, sin, gq, gk, dmix_o, d_out, lse, seq,
                                                    _chip_exchange_comm(pb_ffn1))
    (dx2, dproj_o, h2, qn, kvn, dgam_m1, dqa, dkva, dpw_bd, dps) = _odd_pre_bwd(
        x2, proj_o, dx3, dmix_o, dq, dkv, dkr, w_mix1, mix_norm[1:2], qbt, kvbt, q_a_norm, kv_a_norm, pw_bd,
        pool_scale, seq)
    gp_mix1 = _tn([h2], dproj_o, D, "dw_oin", (gp_mix1, N_SQ, OFF_OIN))
    d_qbt = _tn([dq], qn, HEADS * HP, "dw_qb")
    d_qb_rows = d_qbt.reshape(HEADS, HP, Q_LORA)[:, :QK_DIM].reshape(N_DEV, N_QB_USED, D)
    d_kvb_rows = _tn([dkv], kvn, HEADS * HP, "dw_kvb").reshape(N_DEV, N_KVB, D)
    gp_mix1 = lax.dynamic_update_slice(gp_mix1, d_qb_rows, (0, OFF_QB, 0))
    gp_mix1 = lax.dynamic_update_slice(gp_mix1, d_kvb_rows, (0, OFF_KVB, 0))

    (dx1, act0, dg0, du0, h1, dgam_f0), (ga_mix1,) = _ffn_bwd(x1, g0, u0, dx2, w_gu0, w_d0, ffn_norm[0:1], "ffn_bwd0",
                                                             _pair_exchange_comm(gp_mix1))
    pb_mix1 = _rs_pair_sum(gp_mix1, ga_mix1, where, "rs_pair_sum_mix1")
    *open_mix1, started = _chip_exchange_start(pb_mix1, "mix1")
    gp_ffn0a = _tn([dg0], h1, 1408, "dw_gate0", (chunk(R_GU), N_FF, OFF_GATE), None, started)
    gp_ffn0a = _tn([du0], h1, 1408, "dw_up0", (gp_ffn0a, N_FF, OFF_UP))
    gp_ffn0b, (ga_ffn0a,) = _tn([act0], dx2, 1408, "dw_down0", (chunk(N_FF), N_FF, 0),
                                _pair_exchange_comm(gp_ffn0a))
    pb_ffn0a = _rs_pair_sum(gp_ffn0a, ga_ffn0a, where, "rs_pair_sum_ffn0a")
    *open_ffn0a, started = _chip_exchange_start(pb_ffn0a, "ffn0a")

    (dx0, dproj_e, mix_e, h0, dgam_m0, dws, dbc, dlng, dcw), _ = _even_bwd(
        x0, proj_e, dx1, w_mix0, mix_norm[0:1], sg_ln_g, ws, bst, cw + 0.0 * started[:, :1], seq)

    small = _small_pack([
        jnp.concatenate([dgam_m0, dgam_m1], 0), jnp.concatenate([dgam_f0, dgam_f1], 0), dlng,
        dws[None], dbc[:, :SG_HEADS].T[None], dcw[:3],
        jnp.stack([dpw_bd[g * POOL_GD:(g + 1) * POOL_GD, g * POOL_GD:(g + 1) * POOL_GD] for g in range(4)])[None],
        dps, dqa, dkva, dgq[:, :QK_DIM] * ATT_SCALE, dgk[:, :QK_DIM], loss_tile[0:1, 0:1]])
    gp_ein, (small_all, ga_ffn0b) = _tn([dproj_e], h0, 1280, "dw_ein", (chunk(N_EIN), N_EIN, 0),
                                        _both(_gather_comm([small]), _pair_exchange_comm(gp_ffn0b)))
    pb_ffn0b = _rs_pair_sum(gp_ffn0b, ga_ffn0b, where, "rs_pair_sum_ffn0b")
    *open_ffn0b, started = _chip_exchange_start(pb_ffn0b, "ffn0b")
    gp_eout, (ga_ein,) = _tn([mix_e], dx1, D, "dw_eout", (chunk(N_SQ), N_SQ, 0), _pair_exchange_comm(gp_ein),
                             started)
    pb_ein = _rs_pair_sum(gp_ein, ga_ein, where, "rs_pair_sum_ein")
    *open_ein, started = _chip_exchange_start(pb_ein, "ein")
    small_sum = _small_unpack(_sum_gathered(small_all), SMALL_SHAPES)
    in_flight = (open_ffn0a, open_ffn0b, open_mix1, open_ein)
    return dx0.reshape(bsz, seq, D), (pb_ffn1, gb_ffn1), in_flight, (gp_eout, started), small_sum


SMALL_SHAPES = [(2, D), (2, D), (1, SG_W), (1, SG_HEADS, 128, 128), (1, SG_HEADS, 128), (3, SC_W),
                (1, 4, POOL_GD, POOL_GD), (1, POOL_W), (1, Q_LORA), (1, KV_LORA), (1, QK_DIM), (1, QK_DIM), (1, 1)]


def kernel(x, positions, mix_norm, ffn_norm, even_w_in, sg_ln_g, sg_w_s, sg_b_s, sc_conv_w, even_w_out, odd_w_in, pool_w, pool_scale, q_a_norm, q_b, kv_a_norm, kv_b, q_norm, k_norm, odd_w_out, ffn_w_gate, ffn_w_up, ffn_w_down, loss_target, m_mix_norm, m_ffn_norm, m_even_w_in, m_sg_ln_g, m_sg_w_s, m_sg_b_s, m_sc_conv_w, m_even_w_out, m_odd_w_in, m_pool_w, m_pool_scale, m_q_a_norm, m_q_b, m_kv_a_norm, m_kv_b, m_q_norm, m_k_norm, m_odd_w_out, m_ffn_w_gate, m_ffn_w_up, m_ffn_w_down, v_mix_norm, v_ffn_norm, v_even_w_in, v_sg_ln_g, v_sg_w_s, v_sg_b_s, v_sc_conv_w, v_even_w_out, v_odd_w_in, v_pool_w, v_pool_scale, v_q_a_norm, v_q_b, v_kv_a_norm, v_kv_b, v_q_norm, v_k_norm, v_odd_w_out, v_ffn_w_gate, v_ffn_w_up, v_ffn_w_down):
    xi, yi, ci = _place()
    me = 4 * xi + 2 * yi + ci

    chunks = _pack_shards(even_w_in, even_w_out, odd_w_in, q_b, kv_b, odd_w_out, ffn_w_gate, ffn_w_up, ffn_w_down)

    def lane_pad(a):
        return jnp.pad(a, ((0, 0), (0, 128 - a.shape[1])))

    tile = jnp.concatenate([lane_pad(sc_conv_w[0]), lane_pad(pool_scale), lane_pad(q_a_norm), lane_pad(kv_a_norm),
                            jnp.zeros((2, 128), F32)], axis=0)
    chip = 2 * xi + yi
    where = jnp.stack([ci, chip, chip ^ 2, chip ^ 1, chip ^ 3]).astype(jnp.int32)
    grad_x, (pb_ffn1, gb_ffn1), in_flight, (gp_eout, started), tot = _step(
        x, positions, loss_target, chunks, tile, where, mix_norm, ffn_norm, sg_ln_g, sg_w_s, sg_b_s,
        pool_w, q_norm, k_norm)

    (ga_eout,) = _comm_alone(_pair_exchange_comm(gp_eout), "rs_pair_exchange_eout")
    pb_eout = _rs_pair_sum(gp_eout, ga_eout, where + (0.0 * started[0, :1]).astype(jnp.int32), "rs_pair_sum_eout")
    eout_sems, pb_eout, land_eout, started = _chip_exchange_start(pb_eout, "eout")
    *in_flight, open_ein = in_flight
    landed = [_chip_exchange_wait(*parts, started, tag) for parts, tag in zip(in_flight, ("ffn0a", "ffn0b", "mix1"))]
    gsh_ffn0a, gsh_ffn0b, gsh_mix1, gsh_ffn1 = _rs_final_sums(
        [pb for pb, _ in landed] + [pb_ffn1], [gb for _, gb in landed] + [gb_ffn1], "rs_final_sums", started)

    (g_mix, g_ffn, g_lng, g_ws, g_bs, g_cw_full, g_pw, g_ps_full, g_qa_full, g_kva_full, g_qn, g_kn, loss) = tot
    g_cw = lax.dynamic_slice_in_dim(g_cw_full, me * 64, 64, axis=1)[None]
    g_ps = lax.dynamic_slice_in_dim(g_ps_full, me * 32, 32, axis=1)
    g_qa = lax.dynamic_slice_in_dim(g_qa_full, me * 48, 48, axis=1)
    g_kva = lax.dynamic_slice_in_dim(g_kva_full, me * 32, 32, axis=1)

    def tr(a):
        return jnp.swapaxes(a, -1, -2)

    g_gate = tr(jnp.stack([gsh_ffn0a[OFF_GATE:OFF_GATE + N_FF], gsh_ffn1[OFF_GATE:OFF_GATE + N_FF]]))
    g_up = tr(jnp.stack([gsh_ffn0a[OFF_UP:OFF_UP + N_FF], gsh_ffn1[OFF_UP:OFF_UP + N_FF]]))
    g_down = jnp.stack([gsh_ffn0b, gsh_ffn1[R_GU:R_GU + N_FF]])
    g_oin = gsh_mix1[OFF_OIN:OFF_OIN + N_SQ, :ODD_IN][None]
    g_oout = gsh_mix1[OFF_OOUT:OFF_OOUT + N_SQ][None]
    g_qb = tr(gsh_mix1[OFF_QB:OFF_QB + N_QB_USED].reshape(1, 144, Q_LORA))
    g_kvb = tr(gsh_mix1[OFF_KVB:OFF_KVB + N_KVB].reshape(1, 192, KV_LORA))
    transposed = ("even_w_in", "odd_w_in", "q_b", "kv_b", "ffn_w_gate", "ffn_w_up")

    names = ("mix_norm", "ffn_norm", "even_w_in", "sg_ln_g", "sg_w_s", "sg_b_s", "sc_conv_w", "even_w_out",
             "odd_w_in", "pool_w", "pool_scale", "q_a_norm", "q_b", "kv_a_norm", "kv_b", "q_norm", "k_norm",
             "odd_w_out", "ffn_w_gate", "ffn_w_up", "ffn_w_down")
    grads = dict(mix_norm=g_mix, ffn_norm=g_ffn, sg_ln_g=g_lng, sg_w_s=g_ws, sg_b_s=g_bs,
                 sc_conv_w=g_cw, odd_w_in=g_oin, pool_w=g_pw, pool_scale=g_ps, q_a_norm=g_qa,
                 q_b=g_qb, kv_a_norm=g_kva, kv_b=g_kvb, q_norm=g_qn, k_norm=g_kn, odd_w_out=g_oout,
                 ffn_w_gate=g_gate, ffn_w_up=g_up, ffn_w_down=g_down)
    weights = dict(mix_norm=mix_norm, ffn_norm=ffn_norm, even_w_in=even_w_in, sg_ln_g=sg_ln_g, sg_w_s=sg_w_s,
                   sg_b_s=sg_b_s, sc_conv_w=sc_conv_w, even_w_out=even_w_out, odd_w_in=odd_w_in, pool_w=pool_w,
                   pool_scale=pool_scale, q_a_norm=q_a_norm, q_b=q_b, kv_a_norm=kv_a_norm, kv_b=kv_b, q_norm=q_norm,
                   k_norm=k_norm, odd_w_out=odd_w_out, ffn_w_gate=ffn_w_gate, ffn_w_up=ffn_w_up,
                   ffn_w_down=ffn_w_down)
    m_in = dict(mix_norm=m_mix_norm, ffn_norm=m_ffn_norm, even_w_in=m_even_w_in, sg_ln_g=m_sg_ln_g, sg_w_s=m_sg_w_s,
                sg_b_s=m_sg_b_s, sc_conv_w=m_sc_conv_w, even_w_out=m_even_w_out, odd_w_in=m_odd_w_in,
                pool_w=m_pool_w, pool_scale=m_pool_scale, q_a_norm=m_q_a_norm, q_b=m_q_b, kv_a_norm=m_kv_a_norm,
                kv_b=m_kv_b, q_norm=m_q_norm, k_norm=m_k_norm, odd_w_out=m_odd_w_out, ffn_w_gate=m_ffn_w_gate,
                ffn_w_up=m_ffn_w_up, ffn_w_down=m_ffn_w_down)
    v_in = dict(mix_norm=v_mix_norm, ffn_norm=v_ffn_norm, even_w_in=v_even_w_in, sg_ln_g=v_sg_ln_g, sg_w_s=v_sg_w_s,
                sg_b_s=v_sg_b_s, sc_conv_w=v_sc_conv_w, even_w_out=v_even_w_out, odd_w_in=v_odd_w_in,
                pool_w=v_pool_w, pool_scale=v_pool_scale, q_a_norm=v_q_a_norm, q_b=v_q_b, kv_a_norm=v_kv_a_norm,
                kv_b=v_kv_b, q_norm=v_q_norm, k_norm=v_k_norm, odd_w_out=v_odd_w_out, ffn_w_gate=v_ffn_w_gate,
                ffn_w_up=v_ffn_w_up, ffn_w_down=v_ffn_w_down)
    delta, new_m, new_v = {}, {}, {}

    def as2d(k, a):
        a = tr(a) if k in transposed else a
        return a.reshape(-1, a.shape[-1])

    def back(k, a):
        shape = weights[k].shape
        return tr(a.reshape(shape[:-2] + (shape[-1], shape[-2]))) if k in transposed else a.reshape(shape)

    def update(group, name, nblk=1):
        outs = _adamw([as2d(k, weights[k]) for k in group], [as2d(k, grads[k]) for k in group],
                      [as2d(k, m_in[k]) for k in group], [as2d(k, v_in[k]) for k in group], name, nblk)
        for i, k in enumerate(group):
            delta[k], new_m[k], new_v[k] = (back(k, o[i]) for o in outs)

    update(["ffn_w_gate", "ffn_w_up", "ffn_w_down"], "adamw_ffn", 4)
    update(["odd_w_in", "odd_w_out"], "adamw_mix", 2)
    update([k for k in names if k not in delta and k not in ("even_w_in", "even_w_out")], "adamw_small")

    pb_ein, gb_ein = _chip_exchange_wait(*open_ein, new_v["k_norm"], "ein")
    pb_eout, gb_eout = _chip_exchange_wait(eout_sems, pb_eout, land_eout, new_v["k_norm"], "eout")
    last = ["even_w_in", "even_w_out"]
    (gsh_ein, gsh_eout), *outs = _adamw_summing(
        [as2d(k, weights[k]) for k in last], [pb_ein, pb_eout], [gb_ein, gb_eout],
        [as2d(k, m_in[k]) for k in last], [as2d(k, v_in[k]) for k in last], "adamw_even", 2)
    grads["even_w_in"] = tr(gsh_ein[None])
    grads["even_w_out"] = gsh_eout[None]
    for i, k in enumerate(last):
        delta[k], new_m[k], new_v[k] = (back(k, o[i]) for o in outs)

    return (loss.reshape(()), grad_x, *[grads[k] for k in names], *[delta[k] for k in names],
            *[new_m[k] for k in names], *[new_v[k] for k in names])
```
